```python
import jax, jax.numpy as jnp
from jax import lax
import numpy as np

D_MODEL = 1024
BATCH = 8
SEQ = 8192
DEPTH = 2

HEAD_DIM = 64
POOL_WINDOWS = (2, 4, 8, 16)
POOL_GROUPS = 4
POOL_GROUP_DIM = D_MODEL // 16
POOL_WIDTH = POOL_GROUPS * POOL_GROUP_DIM
N_Q_HEADS = D_MODEL // 128
N_KV_HEADS = 2
Q_PER_KV = N_Q_HEADS // N_KV_HEADS
WINDOW = 128
ATTN_BLOCK = 128
ATTN_WIDTH = N_Q_HEADS * HEAD_DIM
KV_WIDTH = N_KV_HEADS * HEAD_DIM
CHUNK = 128
SGU_GROUPS = 4
SGU_GROUP_DIM = D_MODEL // 16
SGU_WIDTH = SGU_GROUPS * SGU_GROUP_DIM
N_BRANCHES = 3
IN_COLS = POOL_WIDTH + ATTN_WIDTH + 2 * KV_WIDTH + 2 * SGU_WIDTH + N_BRANCHES * D_MODEL
D_FF = 2816
CONV_WIDTH = 3
ROPE_THETA = 10000.0
EPS = 1e-6

kernel_name = "hybrid_pool_swa_sgu_convffn"


def rms_norm(x, g):
    xf = x.astype(jnp.float32)
    y = xf * lax.rsqrt(jnp.mean(xf * xf, axis=-1, keepdims=True) + EPS)
    return (y * g.astype(jnp.float32)).astype(x.dtype)


def rope_tables(positions):
    inv_freq = ROPE_THETA ** (-jnp.arange(0, HEAD_DIM, 2, dtype=jnp.float32) / HEAD_DIM)
    ang = positions.astype(jnp.float32)[..., None] * inv_freq
    return jnp.cos(ang)[:, :, None, :], jnp.sin(ang)[:, :, None, :]


def apply_rope(t, cos, sin):
    tf = t.astype(jnp.float32)
    t1, t2 = jnp.split(tf, 2, axis=-1)
    return jnp.concatenate([t1 * cos - t2 * sin, t2 * cos + t1 * sin], axis=-1).astype(t.dtype)


def pool_mixer(xa, w_pool, pool_scale):
    B, S, _ = xa.shape
    xf = xa.astype(jnp.float32)
    cs = jnp.concatenate([jnp.zeros((B, 1, POOL_WIDTH), jnp.float32), jnp.cumsum(xf, axis=1)], axis=1)
    t = jnp.arange(S)
    pooled = []
    for g, w in enumerate(POOL_WINDOWS):
        c = cs[..., g * POOL_GROUP_DIM:(g + 1) * POOL_GROUP_DIM]
        upper = c[:, 1:]
        lower = jnp.concatenate([jnp.zeros((B, w - 1, POOL_GROUP_DIM), jnp.float32), c[:, :S - w + 1]], axis=1)
        count = jnp.minimum(t + 1, w).astype(jnp.float32)[None, :, None]
        pooled.append((upper - lower) / count)
    pooled = jnp.stack(pooled, axis=2)
    diff = (pooled - xf.reshape(B, S, POOL_GROUPS, POOL_GROUP_DIM)).astype(xa.dtype)
    mixed = jnp.einsum('bsgc,gcd->bsgd', diff, w_pool).reshape(B, S, POOL_WIDTH)
    return mixed * pool_scale


def swa_attention(q, k, v, sinks):
    B, S = q.shape[:2]
    nb = S // ATTN_BLOCK
    qb = q.reshape(B, nb, ATTN_BLOCK, N_KV_HEADS, Q_PER_KV, HEAD_DIM)

    def band(t):
        tb = t.reshape(B, nb, ATTN_BLOCK, N_KV_HEADS, HEAD_DIM)
        prev = jnp.concatenate([jnp.zeros_like(tb[:, :1]), tb[:, :-1]], axis=1)
        return jnp.concatenate([prev, tb], axis=2)

    kb, vb = band(k), band(v)
    scores = jnp.einsum('bnqhgd,bnkhd->bnhgqk', qb, kb).astype(jnp.float32) * (HEAD_DIM ** -0.5)
    qi = jnp.arange(ATTN_BLOCK)[:, None]
    kj = jnp.arange(2 * ATTN_BLOCK)[None, :]
    dist = qi + ATTN_BLOCK - kj
    in_window = (dist >= 0) & (dist < WINDOW)
    key_pos = (jnp.arange(nb)[:, None, None] - 1) * ATTN_BLOCK + kj[None]
    mask = in_window[None] & (key_pos >= 0)
    scores = jnp.where(mask[None, :, None, None], scores, -jnp.inf)
    sink = sinks.astype(jnp.float32).reshape(N_KV_HEADS, Q_PER_KV)[None, None, :, :, None, None]
    sink = jnp.broadcast_to(sink, scores.shape[:-1] + (1,))
    probs = jax.nn.softmax(jnp.concatenate([scores, sink], axis=-1), axis=-1)[..., :-1]
    out = jnp.einsum('bnhgqk,bnkhd->bnqhgd', probs.astype(v.dtype), vb)
    return out.reshape(B, S, ATTN_WIDTH)


def spatial_gating(u, v, w_s, b_s, v_norm):
    B, S, _ = u.shape
    nc = S // CHUNK
    u = jax.nn.gelu(u)
    vg = rms_norm(jax.nn.gelu(v).reshape(B, S, SGU_GROUPS, SGU_GROUP_DIM), v_norm)
    vc = vg.reshape(B, nc, CHUNK, SGU_GROUPS, SGU_GROUP_DIM)
    w_causal = jnp.tril(w_s)
    s = jnp.einsum('gts,bnsgc->bntgc', w_causal, vc) + b_s.T[None, None, :, :, None]
    return u * s.reshape(B, S, SGU_WIDTH)


def causal_dwconv(x, w, b):
    C = x.shape[-1]
    y = lax.conv_general_dilated(
        x, w[:, None, :].astype(x.dtype), window_strides=(1,),
        padding=((CONV_WIDTH - 1, 0),), dimension_numbers=('NWC', 'WIO', 'NWC'),
        feature_group_count=C)
    return y + b


def _fwd_setup_inputs(seed: int = 0) -> dict:
    key = jax.random.key(seed)
    ks = jax.random.split(key, 24)
    f32 = jnp.float32
    nrm = lambda k, shape, s: jax.random.normal(k, shape, f32) * s
    return {
        "x": nrm(ks[0], (BATCH, SEQ, D_MODEL), 1.0),
        "positions": (jnp.arange(SEQ, dtype=jnp.int32)[None, :]
                      + jax.random.randint(ks[1], (BATCH, 1), 0, SEQ, dtype=jnp.int32)),
        "norm1": 1.0 + nrm(ks[2], (DEPTH, D_MODEL), 0.02),
        "w_in": nrm(ks[3], (DEPTH, D_MODEL, IN_COLS), D_MODEL ** -0.5),
        "q_norm": 1.0 + nrm(ks[4], (DEPTH, HEAD_DIM), 0.02),
        "k_norm": 1.0 + nrm(ks[5], (DEPTH, HEAD_DIM), 0.02),
        "sinks": nrm(ks[6], (DEPTH, N_Q_HEADS), 0.5),
        "w_pool": nrm(ks[7], (DEPTH, POOL_GROUPS, POOL_GROUP_DIM, POOL_GROUP_DIM), POOL_GROUP_DIM ** -0.5),
        "pool_scale": 1.0 + nrm(ks[8], (DEPTH, POOL_WIDTH), 0.02),
        "sgu_v_norm": 1.0 + nrm(ks[9], (DEPTH, SGU_GROUP_DIM), 0.02),
        "w_s": nrm(ks[10], (DEPTH, SGU_GROUPS, CHUNK, CHUNK), CHUNK ** -0.5),
        "b_s": 1.0 + nrm(ks[11], (DEPTH, SGU_GROUPS, CHUNK), 0.02),
        "w_proj_a": nrm(ks[12], (DEPTH, POOL_WIDTH, D_MODEL), POOL_WIDTH ** -0.5),
        "w_proj_b": nrm(ks[13], (DEPTH, ATTN_WIDTH, D_MODEL), ATTN_WIDTH ** -0.5),
        "w_proj_c": nrm(ks[14], (DEPTH, SGU_WIDTH, D_MODEL), SGU_WIDTH ** -0.5),
        "w_out": nrm(ks[15], (DEPTH, D_MODEL, D_MODEL), D_MODEL ** -0.5),
        "norm2": 1.0 + nrm(ks[16], (DEPTH, D_MODEL), 0.02),
        "w_up": nrm(ks[17], (DEPTH, D_MODEL, 2 * D_FF), D_MODEL ** -0.5),
        "conv_w": nrm(ks[18], (DEPTH, CONV_WIDTH, 2 * D_FF), CONV_WIDTH ** -0.5),
        "conv_b": nrm(ks[19], (DEPTH, 2 * D_FF), 0.02),
        "w_down": nrm(ks[20], (DEPTH, D_FF, D_MODEL), D_FF ** -0.5),
    }


def _fwd_reference(x, positions, norm1, w_in, q_norm, k_norm, sinks, w_pool, pool_scale,
              sgu_v_norm, w_s, b_s, w_proj_a, w_proj_b, w_proj_c, w_out, norm2,
              w_up, conv_w, conv_b, w_down):
    B, S, _ = x.shape
    cos, sin = rope_tables(positions)
    splits = np.cumsum([POOL_WIDTH, ATTN_WIDTH, KV_WIDTH, KV_WIDTH, SGU_WIDTH, SGU_WIDTH]).tolist()
    for l in range(DEPTH):
        h = rms_norm(x, norm1[l])
        z = h @ w_in[l]
        x_pool, q, k, v, u_s, v_s, gates = jnp.split(z, splits, axis=-1)

        y_a = pool_mixer(x_pool, w_pool[l], pool_scale[l]) @ w_proj_a[l]

        q = apply_rope(rms_norm(q.reshape(B, S, N_Q_HEADS, HEAD_DIM), q_norm[l]), cos, sin)
        k = apply_rope(rms_norm(k.reshape(B, S, N_KV_HEADS, HEAD_DIM), k_norm[l]), cos, sin)
        v = v.reshape(B, S, N_KV_HEADS, HEAD_DIM)
        y_b = swa_attention(q, k, v, sinks[l]) @ w_proj_b[l]

        y_c = spatial_gating(u_s, v_s, w_s[l], b_s[l], sgu_v_norm[l]) @ w_proj_c[l]

        g = jax.nn.sigmoid(gates.astype(jnp.float32)).astype(x.dtype).reshape(B, S, N_BRANCHES, D_MODEL)
        merged = g[:, :, 0] * y_a + g[:, :, 1] * y_b + g[:, :, 2] * y_c
        x = x + merged @ w_out[l]

        h = rms_norm(x, norm2[l])
        up = causal_dwconv(h @ w_up[l], conv_w[l], conv_b[l])
        gate, val = jnp.split(up, 2, axis=-1)
        x = x + (jax.nn.silu(gate) * val) @ w_down[l]
    return x


import jax as _jax
import jax.numpy as _jnp

TWIN_FORMAT = 'train_step'
FWD_PARAMS = ['x', 'positions', 'norm1', 'w_in', 'q_norm', 'k_norm', 'sinks', 'w_pool', 'pool_scale', 'sgu_v_norm', 'w_s', 'b_s', 'w_proj_a', 'w_proj_b', 'w_proj_c', 'w_out', 'norm2', 'w_up', 'conv_w', 'conv_b', 'w_down']
TWIN_WEIGHTS = ['norm1', 'w_in', 'q_norm', 'k_norm', 'sinks', 'w_pool', 'pool_scale', 'sgu_v_norm', 'w_s', 'b_s', 'w_proj_a', 'w_proj_b', 'w_proj_c', 'w_out', 'norm2', 'w_up', 'conv_w', 'conv_b', 'w_down']
TWIN_DIFF_INPUT = 'x'
TWIN_INPUTS = ['x', 'positions', 'norm1', 'w_in', 'q_norm', 'k_norm', 'sinks', 'w_pool', 'pool_scale', 'sgu_v_norm', 'w_s', 'b_s', 'w_proj_a', 'w_proj_b', 'w_proj_c', 'w_out', 'norm2', 'w_up', 'conv_w', 'conv_b', 'w_down', 'loss_target', 'm_norm1', 'm_w_in', 'm_q_norm', 'm_k_norm', 'm_sinks', 'm_w_pool', 'm_pool_scale', 'm_sgu_v_norm', 'm_w_s', 'm_b_s', 'm_w_proj_a', 'm_w_proj_b', 'm_w_proj_c', 'm_w_out', 'm_norm2', 'm_w_up', 'm_conv_w', 'm_conv_b', 'm_w_down', 'v_norm1', 'v_w_in', 'v_q_norm', 'v_k_norm', 'v_sinks', 'v_w_pool', 'v_pool_scale', 'v_sgu_v_norm', 'v_w_s', 'v_b_s', 'v_w_proj_a', 'v_w_proj_b', 'v_w_proj_c', 'v_w_out', 'v_norm2', 'v_w_up', 'v_conv_w', 'v_conv_b', 'v_w_down']
TWIN_OUTPUTS = ['loss', 'grad_x', 'grad_norm1', 'grad_w_in', 'grad_q_norm', 'grad_k_norm', 'grad_sinks', 'grad_w_pool', 'grad_pool_scale', 'grad_sgu_v_norm', 'grad_w_s', 'grad_b_s', 'grad_w_proj_a', 'grad_w_proj_b', 'grad_w_proj_c', 'grad_w_out', 'grad_norm2', 'grad_w_up', 'grad_conv_w', 'grad_conv_b', 'grad_w_down', 'delta_norm1', 'delta_w_in', 'delta_q_norm', 'delta_k_norm', 'delta_sinks', 'delta_w_pool', 'delta_pool_scale', 'delta_sgu_v_norm', 'delta_w_s', 'delta_b_s', 'delta_w_proj_a', 'delta_w_proj_b', 'delta_w_proj_c', 'delta_w_out', 'delta_norm2', 'delta_w_up', 'delta_conv_w', 'delta_conv_b', 'delta_w_down', 'new_m_norm1', 'new_m_w_in', 'new_m_q_norm', 'new_m_k_norm', 'new_m_sinks', 'new_m_w_pool', 'new_m_pool_scale', 'new_m_sgu_v_norm', 'new_m_w_s', 'new_m_b_s', 'new_m_w_proj_a', 'new_m_w_proj_b', 'new_m_w_proj_c', 'new_m_w_out', 'new_m_norm2', 'new_m_w_up', 'new_m_conv_w', 'new_m_conv_b', 'new_m_w_down', 'new_v_norm1', 'new_v_w_in', 'new_v_q_norm', 'new_v_k_norm', 'new_v_sinks', 'new_v_w_pool', 'new_v_pool_scale', 'new_v_sgu_v_norm', 'new_v_w_s', 'new_v_b_s', 'new_v_w_proj_a', 'new_v_w_proj_b', 'new_v_w_proj_c', 'new_v_w_out', 'new_v_norm2', 'new_v_w_up', 'new_v_conv_w', 'new_v_conv_b', 'new_v_w_down']
TWIN_LEAF_KINDS = {'loss': 'loss', 'grad_x': 'grad_x', 'grad_norm1': 'grad_w', 'grad_w_in': 'grad_w', 'grad_q_norm': 'grad_w', 'grad_k_norm': 'grad_w', 'grad_sinks': 'grad_w', 'grad_w_pool': 'grad_w', 'grad_pool_scale': 'grad_w', 'grad_sgu_v_norm': 'grad_w', 'grad_w_s': 'grad_w', 'grad_b_s': 'grad_w', 'grad_w_proj_a': 'grad_w', 'grad_w_proj_b': 'grad_w', 'grad_w_proj_c': 'grad_w', 'grad_w_out': 'grad_w', 'grad_norm2': 'grad_w', 'grad_w_up': 'grad_w', 'grad_conv_w': 'grad_w', 'grad_conv_b': 'grad_w', 'grad_w_down': 'grad_w', 'delta_norm1': 'delta_w', 'delta_w_in': 'delta_w', 'delta_q_norm': 'delta_w', 'delta_k_norm': 'delta_w', 'delta_sinks': 'delta_w', 'delta_w_pool': 'delta_w', 'delta_pool_scale': 'delta_w', 'delta_sgu_v_norm': 'delta_w', 'delta_w_s': 'delta_w', 'delta_b_s': 'delta_w', 'delta_w_proj_a': 'delta_w', 'delta_w_proj_b': 'delta_w', 'delta_w_proj_c': 'delta_w', 'delta_w_out': 'delta_w', 'delta_norm2': 'delta_w', 'delta_w_up': 'delta_w', 'delta_conv_w': 'delta_w', 'delta_conv_b': 'delta_w', 'delta_w_down': 'delta_w', 'new_m_norm1': 'new_m', 'new_m_w_in': 'new_m', 'new_m_q_norm': 'new_m', 'new_m_k_norm': 'new_m', 'new_m_sinks': 'new_m', 'new_m_w_pool': 'new_m', 'new_m_pool_scale': 'new_m', 'new_m_sgu_v_norm': 'new_m', 'new_m_w_s': 'new_m', 'new_m_b_s': 'new_m', 'new_m_w_proj_a': 'new_m', 'new_m_w_proj_b': 'new_m', 'new_m_w_proj_c': 'new_m', 'new_m_w_out': 'new_m', 'new_m_norm2': 'new_m', 'new_m_w_up': 'new_m', 'new_m_conv_w': 'new_m', 'new_m_conv_b': 'new_m', 'new_m_w_down': 'new_m', 'new_v_norm1': 'new_v', 'new_v_w_in': 'new_v', 'new_v_q_norm': 'new_v', 'new_v_k_norm': 'new_v', 'new_v_sinks': 'new_v', 'new_v_w_pool': 'new_v', 'new_v_pool_scale': 'new_v', 'new_v_sgu_v_norm': 'new_v', 'new_v_w_s': 'new_v', 'new_v_b_s': 'new_v', 'new_v_w_proj_a': 'new_v', 'new_v_w_proj_b': 'new_v', 'new_v_w_proj_c': 'new_v', 'new_v_w_out': 'new_v', 'new_v_norm2': 'new_v', 'new_v_w_up': 'new_v', 'new_v_conv_w': 'new_v', 'new_v_conv_b': 'new_v', 'new_v_w_down': 'new_v'}


def _forward(args):
    return _fwd_reference(*[args[k] for k in FWD_PARAMS])


def _output_shape():
    def fwd():
        inp = _fwd_setup_inputs(0)
        return _fwd_reference(*[inp[k] for k in FWD_PARAMS])
    out = _jax.eval_shape(fwd)
    return out.shape, out.dtype

N_MICROBATCH = 1
ADAM_LR = 0.001
ADAM_B1 = 0.9
ADAM_B2 = 0.999
ADAM_EPS = 1e-08
ADAM_WD = 0.01
ADAM_STEP = 10
PER_EXAMPLE_BATCH_AXIS = {'x': 0, 'positions': 0, 'loss_target': 0}
SHARED_INPUTS = []
_WEIGHT_DTYPES = {'norm1': _jnp.float32, 'w_in': _jnp.float32, 'q_norm': _jnp.float32, 'k_norm': _jnp.float32, 'sinks': _jnp.float32, 'w_pool': _jnp.float32, 'pool_scale': _jnp.float32, 'sgu_v_norm': _jnp.float32, 'w_s': _jnp.float32, 'b_s': _jnp.float32, 'w_proj_a': _jnp.float32, 'w_proj_b': _jnp.float32, 'w_proj_c': _jnp.float32, 'w_out': _jnp.float32, 'norm2': _jnp.float32, 'w_up': _jnp.float32, 'conv_w': _jnp.float32, 'conv_b': _jnp.float32, 'w_down': _jnp.float32}
MOMENT_SCALE = {'norm1': 3.139933e+01, 'w_in': 1.109033e+00, 'q_norm': 3.279977e+00, 'k_norm': 3.264647e+00, 'sinks': 9.897021e-01, 'w_pool': 7.798593e+00, 'pool_scale': 5.933041e+01, 'sgu_v_norm': 6.467475e+01, 'w_s': 5.061837e+00, 'b_s': 1.683169e+01, 'w_proj_a': 1.887485e+00, 'w_proj_b': 7.161573e-01, 'w_proj_c': 4.102887e+00, 'w_out': 4.134248e+00, 'norm2': 4.964770e+01, 'w_up': 1.094297e+00, 'conv_w': 7.270311e+00, 'conv_b': 6.842562e+00, 'w_down': 9.206152e-01}


def _to_microbatches(a, axis):
    t = _jnp.moveaxis(a, axis, 0)
    t = t.reshape((N_MICROBATCH, t.shape[0] // N_MICROBATCH) + t.shape[1:])
    return _jnp.moveaxis(t, 1, axis + 1)


def setup_inputs(seed: int = 0) -> dict:
    inp = _fwd_setup_inputs(seed)
    key = _jax.random.fold_in(_jax.random.key(seed), 7919)
    shape, _ = _output_shape()
    out = dict(inp)
    out["loss_target"] = _jax.random.normal(_jax.random.fold_in(key, 0), shape, _jnp.float32)
    for i, name in enumerate(TWIN_WEIGHTS):
        w = inp[name].astype(_jnp.float32)
        if MOMENT_SCALE is None:
            s = _jnp.sqrt(_jnp.mean(_jnp.square(w)) + 1e-30)
        else:
            s = MOMENT_SCALE[name]
        km, kv = _jax.random.split(_jax.random.fold_in(key, i + 1))
        out[name] = w
        out["m_" + name] = s * _jax.random.normal(km, w.shape, _jnp.float32)
        out["v_" + name] = (s * s) * _jax.random.uniform(kv, w.shape, _jnp.float32, 0.5, 1.5)
    if N_MICROBATCH > 1:
        for name, axis in PER_EXAMPLE_BATCH_AXIS.items():
            out[name] = _to_microbatches(out[name], axis)
    return {'x': out['x'], 'positions': out['positions'], 'norm1': out['norm1'], 'w_in': out['w_in'], 'q_norm': out['q_norm'], 'k_norm': out['k_norm'], 'sinks': out['sinks'], 'w_pool': out['w_pool'], 'pool_scale': out['pool_scale'], 'sgu_v_norm': out['sgu_v_norm'], 'w_s': out['w_s'], 'b_s': out['b_s'], 'w_proj_a': out['w_proj_a'], 'w_proj_b': out['w_proj_b'], 'w_proj_c': out['w_proj_c'], 'w_out': out['w_out'], 'norm2': out['norm2'], 'w_up': out['w_up'], 'conv_w': out['conv_w'], 'conv_b': out['conv_b'], 'w_down': out['w_down'], 'loss_target': out['loss_target'], 'm_norm1': out['m_norm1'], 'm_w_in': out['m_w_in'], 'm_q_norm': out['m_q_norm'], 'm_k_norm': out['m_k_norm'], 'm_sinks': out['m_sinks'], 'm_w_pool': out['m_w_pool'], 'm_pool_scale': out['m_pool_scale'], 'm_sgu_v_norm': out['m_sgu_v_norm'], 'm_w_s': out['m_w_s'], 'm_b_s': out['m_b_s'], 'm_w_proj_a': out['m_w_proj_a'], 'm_w_proj_b': out['m_w_proj_b'], 'm_w_proj_c': out['m_w_proj_c'], 'm_w_out': out['m_w_out'], 'm_norm2': out['m_norm2'], 'm_w_up': out['m_w_up'], 'm_conv_w': out['m_conv_w'], 'm_conv_b': out['m_conv_b'], 'm_w_down': out['m_w_down'], 'v_norm1': out['v_norm1'], 'v_w_in': out['v_w_in'], 'v_q_norm': out['v_q_norm'], 'v_k_norm': out['v_k_norm'], 'v_sinks': out['v_sinks'], 'v_w_pool': out['v_w_pool'], 'v_pool_scale': out['v_pool_scale'], 'v_sgu_v_norm': out['v_sgu_v_norm'], 'v_w_s': out['v_w_s'], 'v_b_s': out['v_b_s'], 'v_w_proj_a': out['v_w_proj_a'], 'v_w_proj_b': out['v_w_proj_b'], 'v_w_proj_c': out['v_w_proj_c'], 'v_w_out': out['v_w_out'], 'v_norm2': out['v_norm2'], 'v_w_up': out['v_w_up'], 'v_conv_w': out['v_conv_w'], 'v_conv_b': out['v_conv_b'], 'v_w_down': out['v_w_down']}


def _loss(weights, diff, rest, loss_target):
    with _jax.named_scope("forward"):
        args = {**rest, TWIN_DIFF_INPUT: diff, **{k: w.astype(_WEIGHT_DTYPES[k]) for k, w in weights.items()}}
        y = _forward(args)
    with _jax.named_scope("loss_head"):
        err = _jnp.square(y.astype(_jnp.float32) - loss_target)
        return 0.5 * _jnp.sum(_jnp.mean(err, axis=-1)) if err.ndim else 0.5 * err


def _adamw(w, g, m, v):
    m = ADAM_B1 * m + (1.0 - ADAM_B1) * g
    v = ADAM_B2 * v + (1.0 - ADAM_B2) * _jnp.square(g)
    m_hat = m / (1.0 - ADAM_B1 ** ADAM_STEP)
    v_hat = v / (1.0 - ADAM_B2 ** ADAM_STEP)
    delta = -ADAM_LR * (m_hat / (_jnp.sqrt(v_hat) + ADAM_EPS) + ADAM_WD * w)
    return delta, m, v


def reference(x, positions, norm1, w_in, q_norm, k_norm, sinks, w_pool, pool_scale, sgu_v_norm, w_s, b_s, w_proj_a, w_proj_b, w_proj_c, w_out, norm2, w_up, conv_w, conv_b, w_down, loss_target, m_norm1, m_w_in, m_q_norm, m_k_norm, m_sinks, m_w_pool, m_pool_scale, m_sgu_v_norm, m_w_s, m_b_s, m_w_proj_a, m_w_proj_b, m_w_proj_c, m_w_out, m_norm2, m_w_up, m_conv_w, m_conv_b, m_w_down, v_norm1, v_w_in, v_q_norm, v_k_norm, v_sinks, v_w_pool, v_pool_scale, v_sgu_v_norm, v_w_s, v_b_s, v_w_proj_a, v_w_proj_b, v_w_proj_c, v_w_out, v_norm2, v_w_up, v_conv_w, v_conv_b, v_w_down):
    given = dict(x=x, positions=positions, norm1=norm1, w_in=w_in, q_norm=q_norm, k_norm=k_norm, sinks=sinks, w_pool=w_pool, pool_scale=pool_scale, sgu_v_norm=sgu_v_norm, w_s=w_s, b_s=b_s, w_proj_a=w_proj_a, w_proj_b=w_proj_b, w_proj_c=w_proj_c, w_out=w_out, norm2=norm2, w_up=w_up, conv_w=conv_w, conv_b=conv_b, w_down=w_down, loss_target=loss_target, m_norm1=m_norm1, m_w_in=m_w_in, m_q_norm=m_q_norm, m_k_norm=m_k_norm, m_sinks=m_sinks, m_w_pool=m_w_pool, m_pool_scale=m_pool_scale, m_sgu_v_norm=m_sgu_v_norm, m_w_s=m_w_s, m_b_s=m_b_s, m_w_proj_a=m_w_proj_a, m_w_proj_b=m_w_proj_b, m_w_proj_c=m_w_proj_c, m_w_out=m_w_out, m_norm2=m_norm2, m_w_up=m_w_up, m_conv_w=m_conv_w, m_conv_b=m_conv_b, m_w_down=m_w_down, v_norm1=v_norm1, v_w_in=v_w_in, v_q_norm=v_q_norm, v_k_norm=v_k_norm, v_sinks=v_sinks, v_w_pool=v_w_pool, v_pool_scale=v_pool_scale, v_sgu_v_norm=v_sgu_v_norm, v_w_s=v_w_s, v_b_s=v_b_s, v_w_proj_a=v_w_proj_a, v_w_proj_b=v_w_proj_b, v_w_proj_c=v_w_proj_c, v_w_out=v_w_out, v_norm2=v_norm2, v_w_up=v_w_up, v_conv_w=v_conv_w, v_conv_b=v_conv_b, v_w_down=v_w_down)
    weights = {n: given[n] for n in TWIN_WEIGHTS}
    shared = {n: given[n] for n in SHARED_INPUTS}
    per_example = {n: given[n] for n in ['x', 'positions']}
    grad_fn = _jax.value_and_grad(_loss, argnums=(0, 1))

    def one_microbatch(ex, loss_target):
        ex = dict(ex)
        diff = ex.pop(TWIN_DIFF_INPUT)
        return grad_fn(weights, diff, {**shared, **ex}, loss_target)

    if N_MICROBATCH == 1:
        loss, (grad_w, grad_x) = one_microbatch(per_example, given["loss_target"])
    else:
        def body(carry, xs):
            loss_sum, grad_sum = carry
            l_k, (gw_k, gx_k) = one_microbatch(xs[0], xs[1])
            with _jax.named_scope("update"):
                return (loss_sum + l_k, _jax.tree.map(_jnp.add, grad_sum, gw_k)), gx_k

        init = (_jnp.zeros((), _jnp.float32), _jax.tree.map(_jnp.zeros_like, weights))
        (loss, grad_w), grad_x = _jax.lax.scan(body, init, (per_example, given["loss_target"]))
    with _jax.named_scope("update"):
        delta_w, new_m, new_v = {}, {}, {}
        for n in TWIN_WEIGHTS:
            delta_w[n], new_m[n], new_v[n] = _adamw(weights[n], grad_w[n], given["m_" + n], given["v_" + n])
    return (loss, grad_x, *[grad_w[n] for n in TWIN_WEIGHTS], *[delta_w[n] for n in TWIN_WEIGHTS],
            *[new_m[n] for n in TWIN_WEIGHTS], *[new_v[n] for n in TWIN_WEIGHTS])
```

```python
import functools
import math

import numpy as np
import jax
import jax.numpy as jnp
from jax import lax
from jax.experimental import pallas as pl
from jax.experimental.pallas import tpu as pltpu

F32 = jnp.float32
MXU_DTYPE = jnp.bfloat16

D_MODEL = 1024
DEPTH = 2
HEAD_DIM = 64
POOL_WINDOWS = (2, 4, 8, 16)
POOL_WIDTH = 256
N_Q_HEADS = 8
ATTN_BLOCK = 128
ATTN_WIDTH = 512
KV_WIDTH = 128
CHUNK = 128
SGU_WIDTH = 256
IN_COLS = 4608
GATE_COL0 = 1536
D_FF = 2816
ROPE_THETA = 10000.0
EPS = 1e-6
ADAM_LR, ADAM_B1, ADAM_B2, ADAM_EPS, ADAM_WD, ADAM_STEP = 0.001, 0.9, 0.999, 1e-08, 0.01, 10

N_CHIPS = 4
N_DEV = 8
VMEM_LIMIT_BYTES = 56 * 1024 * 1024
NEG_BIG = -1e30
MESH = pl.DeviceIdType.MESH
ANY = pl.BlockSpec(memory_space=pl.ANY)

SDS = jax.ShapeDtypeStruct


def _cp(*sem):
    return pltpu.CompilerParams(dimension_semantics=sem, vmem_limit_bytes=VMEM_LIMIT_BYTES)


def _dot(a, b, dims=((1,), (0,))):
    return lax.dot_general(a.astype(MXU_DTYPE), b.astype(MXU_DTYPE), (dims, ((), ())),
                           preferred_element_type=F32)


NT = ((1,), (1,))
TN = ((0,), (0,))


def _split_dot(x, m):
    hi = x.astype(MXU_DTYPE)
    lo = (x - hi.astype(F32)).astype(MXU_DTYPE)
    return _dot(hi, m) + _dot(lo, m)


def _seg_matrix(width, seg):
    idx = np.arange(width) // seg
    return jnp.asarray((idx[:, None] == idx[None, :]).astype(np.float32), dtype=MXU_DTYPE)


def _lane(shape):
    return lax.broadcasted_iota(jnp.int32, shape, len(shape) - 1)


def _row(shape):
    return lax.broadcasted_iota(jnp.int32, shape, 0)


def _full(shape):
    nd = len(shape)
    return pl.BlockSpec(shape, lambda *_: (0,) * nd)


def _gelu(x):
    k = math.sqrt(2.0 / math.pi)
    th = jnp.tanh(k * (x + 0.044715 * (x * x * x)))
    return 0.5 * x * (1.0 + th)


def _gelu_and_grad(x):
    k = math.sqrt(2.0 / math.pi)
    x2 = x * x
    th = jnp.tanh(k * (x + 0.044715 * (x2 * x)))
    g = 0.5 * x * (1.0 + th)
    dg = 0.5 * (1.0 + th) + 0.5 * x * (1.0 - th * th) * (k * (1.0 + 3.0 * 0.044715 * x2))
    return g, dg


def _sigmoid(x):
    return 1.0 / (1.0 + jnp.exp(-x))


def _swap_halves(x):
    w = x.shape[-1]
    first = (_lane(x.shape) % HEAD_DIM) < (HEAD_DIM // 2)
    return jnp.where(first, pltpu.roll(x, w - HEAD_DIM // 2, 1), pltpu.roll(x, HEAD_DIM // 2, 1))


def _tile_lanes(x, reps):
    return x if reps == 1 else jnp.concatenate([x] * reps, axis=1)


def _fold_lanes(x, period):
    w = x.shape[-1]
    while w > period:
        w //= 2
        x = x + pltpu.roll(x, w, 1)
    return x


def _mm(a, b, *, mode, tm, tn, tk, out_dtype=F32, add=None, name,
        a_lead=None, b_lead=None, out_lead=None, out_into=None, n_lead=DEPTH,
        b_koff=0, out_joff=0, out_n=None):
    ash = a.shape[1:] if a_lead is not None else a.shape
    bsh = b.shape[1:] if b_lead is not None else b.shape
    if mode == 'nn':
        (M, K), (K2, N) = ash, bsh
    elif mode == 'nt':
        (M, K), (N, K2) = ash, bsh
    else:
        (K, M), (K2, N) = ash, bsh
    assert K == K2 or (mode == 'nt' and K2 > K), (ash, bsh, mode)
    assert M % tm == 0 and N % tn == 0 and K % tk == 0, (M, N, K, tm, tn, tk)
    nk = K // tk
    dims = {'nn': ((1,), (0,)), 'nt': NT, 'tn': TN}[mode]

    def lead(spec_shape, imap, lead_idx):
        if lead_idx is None:
            return pl.BlockSpec(spec_shape, imap)
        return pl.BlockSpec((None,) + spec_shape, lambda i, j, k: (lead_idx,) + imap(i, j, k))

    if mode == 'tn':
        a_spec = lead((tk, tm), lambda i, j, k: (k, i), a_lead)
    else:
        a_spec = lead((tm, tk), lambda i, j, k: (i, k), a_lead)
    if mode == 'nt':
        b_spec = lead((tn, tk), lambda i, j, k: (j, k + b_koff), b_lead)
    else:
        b_spec = lead((tk, tn), lambda i, j, k: (k, j), b_lead)
    o_spec = lead((tm, tn), lambda i, j, k: (i, j + out_joff), out_lead)
    n_out = N if out_n is None else out_n
    in_specs = [a_spec, b_spec]
    operands = [a, b]
    if add is not None:
        in_specs.append(pl.BlockSpec((tm, tn), lambda i, j, k: (i, j)))
        operands.append(add)
    aliases = {}
    if out_into is not None:
        in_specs.append(ANY)
        operands.append(out_into)
        aliases = {len(operands) - 1: 0}
    has_add = add is not None
    acc_in_out = nk > 1 and out_dtype == F32

    def body(*refs):
        a_ref, b_ref = refs[0], refs[1]
        pos = 2
        add_ref = None
        if has_add:
            add_ref = refs[pos]
            pos += 1
        if out_into is not None:
            pos += 1
        o_ref = refs[pos]
        acc_ref = refs[pos + 1] if (nk > 1 and not acc_in_out) else None
        p = _dot(a_ref[...], b_ref[...], dims)
        if nk == 1:
            if has_add:
                p = p + add_ref[...]
            o_ref[...] = p.astype(o_ref.dtype)
            return
        k = pl.program_id(2)
        tgt = o_ref if acc_in_out else acc_ref

        @pl.when(k == 0)
        def _():
            tgt[...] = p + add_ref[...] if has_add else p

        @pl.when(k > 0)
        def _():
            tgt[...] += p

        if not acc_in_out:
            @pl.when(k == nk - 1)
            def _():
                o_ref[...] = acc_ref[...].astype(o_ref.dtype)

    out_shape = SDS(((n_lead, M, n_out) if out_lead is not None else (M, n_out)), out_dtype)
    scratch = [pltpu.VMEM((tm, tn), F32)] if (nk > 1 and not acc_in_out) else []
    return pl.pallas_call(
        body, grid=(M // tm, N // tn, nk), in_specs=in_specs, out_specs=o_spec, out_shape=out_shape,
        scratch_shapes=scratch, input_output_aliases=aliases, name=name,
        compiler_params=_cp("parallel", "parallel", "arbitrary"))(*operands)


def _rms_fwd(x, g, *, tr, name):
    T, D = x.shape

    def body(x_ref, g_ref, o_ref):
        xv = x_ref[...]
        r = lax.rsqrt(jnp.mean(xv * xv, axis=-1, keepdims=True) + EPS)
        o_ref[...] = (xv * r * g_ref[...]).astype(o_ref.dtype)

    return pl.pallas_call(
        body, grid=(T // tr,),
        in_specs=[pl.BlockSpec((tr, D), lambda i: (i, 0)), _full((1, D))],
        out_specs=pl.BlockSpec((tr, D), lambda i: (i, 0)),
        out_shape=SDS((T, D), MXU_DTYPE), name=name, compiler_params=_cp("parallel"))(x, g)


def _rms_bwd(x, g, dh, dres, *, tr, name):
    T, D = x.shape

    def body(x_ref, g_ref, dh_ref, dres_ref, dx_ref, dxb_ref, dg_ref):
        i = pl.program_id(0)
        xv = x_ref[...]
        r = lax.rsqrt(jnp.mean(xv * xv, axis=-1, keepdims=True) + EPS)
        xh = xv * r
        dh = dh_ref[...]
        gy = dh * g_ref[...]
        dx = r * (gy - xh * jnp.mean(xh * gy, axis=-1, keepdims=True)) + dres_ref[...]
        dx_ref[...] = dx
        dxb_ref[...] = dx.astype(dxb_ref.dtype)

        @pl.when(i == 0)
        def _():
            dg_ref[...] = jnp.zeros_like(dg_ref)
        dg_ref[...] += jnp.sum(dh * xh, axis=0, keepdims=True)

    rows = pl.BlockSpec((tr, D), lambda i: (i, 0))
    return pl.pallas_call(
        body, grid=(T // tr,), in_specs=[rows, _full((1, D)), rows, rows],
        out_specs=[rows, rows, _full((1, D))],
        out_shape=[SDS((T, D), F32), SDS((T, D), MXU_DTYPE), SDS((1, D), F32)],
        name=name, compiler_params=_cp("arbitrary"))(x, g, dh, dres)


def _loss_head(y, target, *, tr, name):
    T, D = y.shape

    def body(y_ref, t_ref, loss_ref, dy_ref, dyb_ref):
        i = pl.program_id(0)
        d = y_ref[...] - t_ref[...]
        dy = d * (1.0 / D)
        dy_ref[...] = dy
        dyb_ref[...] = dy.astype(dyb_ref.dtype)
        part = jnp.sum(jnp.sum(d * d, axis=1, keepdims=True), axis=0, keepdims=True) * (0.5 / D)

        @pl.when(i == 0)
        def _():
            loss_ref[...] = jnp.zeros_like(loss_ref)
        loss_ref[...] += jnp.broadcast_to(part, loss_ref.shape)

    rows = pl.BlockSpec((tr, D), lambda i: (i, 0))
    return pl.pallas_call(
        body, grid=(T // tr,), in_specs=[rows, rows],
        out_specs=[_full((1, 128)), rows, rows],
        out_shape=[SDS((1, 128), F32), SDS((T, D), F32), SDS((T, D), MXU_DTYPE)],
        name=name, compiler_params=_cp("arbitrary"))(y, target)


def _pool_lane_consts(shape):
    lane = _lane(shape)
    grp = lane // (POOL_WIDTH // 4)
    win = jnp.where(grp == 0, 2, jnp.where(grp == 1, 4, jnp.where(grp == 2, 8, 16)))
    return grp, win


def _pool_select(grp, s2, s4, s8, s16):
    return jnp.where(grp == 0, s2, jnp.where(grp == 1, s4, jnp.where(grp == 2, s8, s16)))


def _pool_diff(xe, row0, tr):
    s2 = xe + pltpu.roll(xe, 1, 0)
    s4 = s2 + pltpu.roll(s2, 2, 0)
    s8 = s4 + pltpu.roll(s4, 4, 0)
    s16 = s8 + pltpu.roll(s8, 8, 0)
    shape = (tr, POOL_WIDTH)
    grp, win = _pool_lane_consts(shape)
    sums = _pool_select(grp, s2[16:], s4[16:], s8[16:], s16[16:])
    t = row0 + _row(shape)
    cnt = jnp.minimum(t + 1, win).astype(F32)
    return sums / cnt - xe[16:]


def _pool_fwd(z, wbd, scale, *, tr, name):
    T = z.shape[0]
    hb = tr // 16

    def body(x_ref, xp_ref, w_ref, s_ref, o_ref):
        i = pl.program_id(0)
        halo = jnp.where(i == 0, 0.0, xp_ref[...])
        diff = _pool_diff(jnp.concatenate([halo, x_ref[...]], axis=0), i * tr, tr)
        o_ref[...] = (_dot(diff, w_ref[...]) * s_ref[...]).astype(o_ref.dtype)

    return pl.pallas_call(
        body, grid=(T // tr,),
        in_specs=[pl.BlockSpec((tr, POOL_WIDTH), lambda i: (i, 0)),
                  pl.BlockSpec((16, POOL_WIDTH), lambda i: (jnp.maximum(i * hb - 1, 0), 0)),
                  _full((POOL_WIDTH, POOL_WIDTH)), _full((1, POOL_WIDTH))],
        out_specs=pl.BlockSpec((tr, POOL_WIDTH), lambda i: (i, 0)),
        out_shape=SDS((T, POOL_WIDTH), MXU_DTYPE), name=name, compiler_params=_cp("parallel"))(z, z, wbd, scale)


def _pool_bwd_tile(i, n, tr, x, xprev, dpa, dpa_next, wbd, scale):
    halo = jnp.where(i == 0, 0.0, xprev)
    diff = _pool_diff(jnp.concatenate([halo, x], axis=0), i * tr, tr)
    mixed = _dot(diff, wbd)
    dscale = jnp.sum(dpa * mixed, axis=0, keepdims=True)
    dnext = jnp.where(i == n - 1, 0.0, dpa_next)
    dmix_e = jnp.concatenate([dpa, dnext], axis=0) * scale
    ddiff_e = _dot(dmix_e, wbd, NT)
    dwbd = _dot(diff, dmix_e[:tr], TN)
    shape = (tr + 16, POOL_WIDTH)
    grp, win = _pool_lane_consts(shape)
    t = i * tr + _row(shape)
    e = ddiff_e / jnp.minimum(t + 1, win).astype(F32)
    nrow = tr + 16
    a2 = e + pltpu.roll(e, nrow - 1, 0)
    a4 = a2 + pltpu.roll(a2, nrow - 2, 0)
    a8 = a4 + pltpu.roll(a4, nrow - 4, 0)
    a16 = a8 + pltpu.roll(a8, nrow - 8, 0)
    dx = _pool_select(grp, a2, a4, a8, a16)[:tr] - ddiff_e[:tr]
    return dx, dwbd, dscale


def _norm_rope(x, g, cos, sin_signed, seg):
    reps = x.shape[1] // 128
    ms = _split_dot(x * x, seg) * (1.0 / HEAD_DIM)
    r = lax.rsqrt(ms + EPS)
    xn = x * r * g
    c, s = _tile_lanes(cos, reps), _tile_lanes(sin_signed, reps)
    return xn * c + _swap_halves(xn) * s


def _norm_rope_bwd(x, g, cos, sin_signed, seg, dout):
    reps = x.shape[1] // 128
    c, s = _tile_lanes(cos, reps), _tile_lanes(sin_signed, reps)
    dxn = dout * c + _swap_halves(dout * s)
    ms = _split_dot(x * x, seg) * (1.0 / HEAD_DIM)
    r = lax.rsqrt(ms + EPS)
    xh = x * r
    gy = dxn * g
    dx = r * (gy - xh * (_split_dot(xh * gy, seg) * (1.0 / HEAD_DIM)))
    dg = jnp.sum(dxn * xh, axis=0, keepdims=True)
    return dx, dg


def _dup_heads(k):
    first = _lane(k.shape) < HEAD_DIM
    kr = pltpu.roll(k, HEAD_DIM, 1)
    return jnp.concatenate([jnp.where(first, k, kr), jnp.where(first, kr, k)], axis=1)


def _qkv_prep(z, cos, sin_signed, gq, gk, seg, *, tr, name):
    T = z.shape[0]

    def body(qa_ref, qb_ref, kv_ref, c_ref, s_ref, gq_ref, gk_ref, seg_ref, q_ref, k_ref, v_ref):
        c, s, seg_m = c_ref[...], s_ref[...], seg_ref[...]
        scale = HEAD_DIM ** -0.5
        qa = _norm_rope(qa_ref[...], gq_ref[...], c, s, seg_m) * scale
        qb = _norm_rope(qb_ref[...], gq_ref[...], c, s, seg_m) * scale
        q_ref[...] = jnp.concatenate([qa, qb], axis=1).astype(q_ref.dtype)
        kv = kv_ref[...]
        k = _norm_rope(kv[:, :KV_WIDTH], gk_ref[...], c, s, seg_m[:128, :128])
        k_ref[...] = _dup_heads(k).astype(k_ref.dtype)
        v_ref[...] = _dup_heads(kv[:, KV_WIDTH:]).astype(v_ref.dtype)

    col = lambda j: pl.BlockSpec((tr, 256), lambda i: (i, j))
    tab = pl.BlockSpec((tr, 128), lambda i: (i, 0))
    return pl.pallas_call(
        body, grid=(T // tr,),
        in_specs=[col(1), col(2), col(3), tab, tab, _full((1, 256)), _full((1, 128)), _full((256, 256))],
        out_specs=[pl.BlockSpec((tr, 512), lambda i: (i, 0)), col(0), col(0)],
        out_shape=[SDS((T, 512), MXU_DTYPE), SDS((T, 256), MXU_DTYPE), SDS((T, 256), MXU_DTYPE)],
        name=name, compiler_params=_cp("parallel"))(z, z, z, cos, sin_signed, gq, gk, seg)


def _attn_mask(n):
    qi = _row((ATTN_BLOCK, 2 * ATTN_BLOCK))
    kj = _lane((ATTN_BLOCK, 2 * ATTN_BLOCK))
    return (kj > qi) & (kj <= qi + ATTN_BLOCK) & ((kj >= ATTN_BLOCK) | (n > 0))


def _attn_probs(q128, head_odd, k2, mask, sink):
    in_head = (_lane(q128.shape) >= HEAD_DIM) == head_odd
    qm = jnp.where(in_head, q128, jnp.zeros_like(q128))
    s = _dot(qm, k2, NT)
    s = jnp.where(mask, s, NEG_BIG)
    m = jnp.maximum(jnp.max(s, axis=1, keepdims=True), sink)
    p = jnp.exp(s - m)
    ps = jnp.exp(sink - m)
    inv = 1.0 / (jnp.sum(p, axis=1, keepdims=True) + ps)
    return qm, in_head, p * inv, ps * inv


def _attn_fwd(q, k, v, sinks_b, *, name):
    T = q.shape[0]
    nb = T // ATTN_BLOCK

    def body(q_ref, kc_ref, kp_ref, vc_ref, vp_ref, sk_ref, o_ref):
        n = pl.program_id(0)
        mask = _attn_mask(n)
        k2 = jnp.concatenate([kp_ref[...], kc_ref[...]], axis=0)
        v2 = jnp.concatenate([vp_ref[...], vc_ref[...]], axis=0)
        qv = q_ref[...]
        outs = []
        for pair in range(N_Q_HEADS // 2):
            g = pair // 2
            kg = k2[:, 128 * g:128 * (g + 1)]
            vg = v2[:, 128 * g:128 * (g + 1)]
            q128 = qv[:, 128 * pair:128 * (pair + 1)]
            o_pair = None
            for odd in (False, True):
                h = 2 * pair + int(odd)
                _, in_head, pn, _ = _attn_probs(q128, odd, kg, mask, sk_ref[h:h + 1, 0:1])
                o2 = _dot(pn, vg)
                o_pair = o2 if o_pair is None else jnp.where(in_head, o2, o_pair)
            outs.append(o_pair)
        o_ref[...] = jnp.concatenate(outs, axis=1).astype(o_ref.dtype)

    cur = lambda w: pl.BlockSpec((ATTN_BLOCK, w), lambda n: (n, 0))
    prev = lambda w: pl.BlockSpec((ATTN_BLOCK, w), lambda n: (jnp.maximum(n - 1, 0), 0))
    return pl.pallas_call(
        body, grid=(nb,),
        in_specs=[cur(512), cur(256), prev(256), cur(256), prev(256), _full((8, 128))],
        out_specs=cur(512), out_shape=SDS((T, 512), MXU_DTYPE), name=name,
        compiler_params=_cp("parallel"))(q, k, k, v, v, sinks_b)


def _attn_bwd(q, k, v, sinks_b, do, *, name):
    T = q.shape[0]
    nb = T // ATTN_BLOCK

    def body(q_ref, kc_ref, kp_ref, vc_ref, vp_ref, sk_ref, do_ref,
             dq_ref, dkc_ref, dkp_ref, dvc_ref, dvp_ref, dsk_ref):
        n = pl.program_id(0)
        mask = _attn_mask(n)
        k2 = jnp.concatenate([kp_ref[...], kc_ref[...]], axis=0)
        v2 = jnp.concatenate([vp_ref[...], vc_ref[...]], axis=0)
        qv = q_ref[...]
        dov = do_ref[...]

        @pl.when(n == 0)
        def _():
            dsk_ref[...] = jnp.zeros_like(dsk_ref)

        dqs, dks, dvs = [], [], []
        for g in range(2):
            kg = k2[:, 128 * g:128 * (g + 1)]
            vg = v2[:, 128 * g:128 * (g + 1)]
            dk_g = jnp.zeros((2 * ATTN_BLOCK, 128), F32)
            dv_g = jnp.zeros((2 * ATTN_BLOCK, 128), F32)
            for pair in (2 * g, 2 * g + 1):
                q128 = qv[:, 128 * pair:128 * (pair + 1)]
                do128 = dov[:, 128 * pair:128 * (pair + 1)]
                dq_pair = None
                for odd in (False, True):
                    h = 2 * pair + int(odd)
                    qm, in_head, pn, psn = _attn_probs(q128, odd, kg, mask, sk_ref[h:h + 1, 0:1])
                    dom = jnp.where(in_head, do128, 0.0)
                    o2 = _dot(pn, vg)
                    delta = jnp.sum(dom * o2, axis=1, keepdims=True)
                    dp = _dot(dom, vg, NT)
                    ds = pn * (dp - delta)
                    dq2 = _dot(ds, kg)
                    dq_pair = dq2 if dq_pair is None else jnp.where(in_head, dq2, dq_pair)
                    dk_g = dk_g + _dot(ds, qm, TN)
                    dv_g = dv_g + _dot(pn, dom, TN)
                    dsink = -jnp.sum(psn * delta, axis=0, keepdims=True)
                    dsk_ref[h:h + 1, :] += jnp.broadcast_to(dsink, (1, 128))
                dqs.append(dq_pair)
            dks.append(dk_g)
            dvs.append(dv_g)
        dq_ref[...] = jnp.concatenate(dqs, axis=1)
        dk = jnp.concatenate(dks, axis=1)
        dv = jnp.concatenate(dvs, axis=1)
        dkp_ref[...] = dk[:ATTN_BLOCK]
        dkc_ref[...] = dk[ATTN_BLOCK:]
        dvp_ref[...] = dv[:ATTN_BLOCK]
        dvc_ref[...] = dv[ATTN_BLOCK:]

    cur = lambda w: pl.BlockSpec((ATTN_BLOCK, w), lambda n: (n, 0))
    prev = lambda w: pl.BlockSpec((ATTN_BLOCK, w), lambda n: (jnp.maximum(n - 1, 0), 0))
    f = lambda w: SDS((T, w), F32)
    return pl.pallas_call(
        body, grid=(nb,),
        in_specs=[cur(512), cur(256), prev(256), cur(256), prev(256), _full((8, 128)), cur(512)],
        out_specs=[cur(512), cur(256), cur(256), cur(256), cur(256), _full((8, 128))],
        out_shape=[f(512), f(256), f(256), f(256), f(256), SDS((8, 128), F32)],
        name=name, compiler_params=_cp("arbitrary"))(q, k, k, v, v, sinks_b, do)


def _mixer_ab_bwd(z, cos, sin_signed, gq, gk, seg, dq, dkc, dkp, dvc, dvp, dpa, wbd, scale, dz, *, tr, name):
    T = z.shape[0]
    n = T // tr
    hb = tr // 16
    ab = tr // ATTN_BLOCK

    def unfold(cur, nxt_tile, nxt_halo, i):
        nxt = jnp.concatenate([nxt_tile[ATTN_BLOCK:], jnp.where(i == n - 1, 0.0, nxt_halo)], axis=0)
        tot = cur + nxt
        first = _lane((tr, 128)) < HEAD_DIM
        a = tot[:, :128]
        b = tot[:, 128:]
        a = a + pltpu.roll(a, HEAD_DIM, 1)
        b = b + pltpu.roll(b, HEAD_DIM, 1)
        return jnp.where(first, a, b)

    def body(xp_ref, xpp_ref, qa_ref, qb_ref, kv_ref, c_ref, s_ref, gq_ref, gk_ref, seg_ref,
             dq_ref, dkc_ref, dkp_ref, dkh_ref, dvc_ref, dvp_ref, dvh_ref, dpa_ref, dpan_ref, w_ref, sc_ref, _dz_in,
             dz_ref, dgq_ref, dgk_ref, dw_ref, dsc_ref):
        i = pl.program_id(0)
        c, s, seg_m = c_ref[...], s_ref[...], seg_ref[...]
        scale_q = HEAD_DIM ** -0.5
        dqv = dq_ref[...] * scale_q
        dxa, dga = _norm_rope_bwd(qa_ref[...], gq_ref[...], c, s, seg_m, dqv[:, :256])
        dxb, dgb = _norm_rope_bwd(qb_ref[...], gq_ref[...], c, s, seg_m, dqv[:, 256:])
        dk = unfold(dkc_ref[...], dkp_ref[...], dkh_ref[...], i)
        dv = unfold(dvc_ref[...], dvp_ref[...], dvh_ref[...], i)
        kv = kv_ref[...]
        dxk, dgk = _norm_rope_bwd(kv[:, :KV_WIDTH], gk_ref[...], c, s, seg_m[:128, :128], dk)
        dxp, dwbd, dscale = _pool_bwd_tile(i, n, tr, xp_ref[...], xpp_ref[...], dpa_ref[...], dpan_ref[...],
                                           w_ref[...], sc_ref[...])
        dz_ref[...] = jnp.concatenate([dxp, dxa, dxb, dxk, dv], axis=1).astype(dz_ref.dtype)

        @pl.when(i == 0)
        def _():
            dgq_ref[...] = jnp.zeros_like(dgq_ref)
            dgk_ref[...] = jnp.zeros_like(dgk_ref)
            dw_ref[...] = jnp.zeros_like(dw_ref)
            dsc_ref[...] = jnp.zeros_like(dsc_ref)
        dgq_ref[...] += _fold_lanes(dga + dgb, HEAD_DIM)
        dgk_ref[...] += _fold_lanes(dgk, HEAD_DIM)
        dw_ref[...] += dwbd
        dsc_ref[...] += dscale

    col = lambda j: pl.BlockSpec((tr, 256), lambda i: (i, j))
    rows = lambda w: pl.BlockSpec((tr, w), lambda i: (i, 0))
    nxt_blk = pl.BlockSpec((ATTN_BLOCK, 256), lambda i: (jnp.minimum((i + 1) * ab, T // ATTN_BLOCK - 1), 0))
    prev16 = pl.BlockSpec((16, 256), lambda i: (jnp.maximum(i * hb - 1, 0), 0))
    next16 = pl.BlockSpec((16, 256), lambda i: (jnp.minimum((i + 1) * hb, T // 16 - 1), 0))
    return pl.pallas_call(
        body, grid=(n,),
        in_specs=[col(0), prev16, col(1), col(2), col(3), rows(128), rows(128),
                  _full((1, 256)), _full((1, 128)), _full((256, 256)),
                  rows(512), rows(256), rows(256), nxt_blk, rows(256), rows(256), nxt_blk,
                  rows(256), next16, _full((256, 256)), _full((1, 256)), ANY],
        out_specs=[rows(1024), _full((1, 256)), _full((1, 128)), _full((256, 256)), _full((1, 256))],
        out_shape=[SDS((T, IN_COLS), MXU_DTYPE), SDS((1, 256), F32), SDS((1, 128), F32),
                   SDS((256, 256), F32), SDS((1, 256), F32)],
        input_output_aliases={21: 0}, name=name, compiler_params=_cp("arbitrary"))(
            z, z, z, z, z, cos, sin_signed, gq, gk, seg, dq, dkc, dkp, dkp, dvc, dvp, dvp, dpa, dpa, wbd, scale, dz)


def _sgu_common(zu, zv, vn, seg):
    u, du = _gelu_and_grad(zu)
    gv, dgv = _gelu_and_grad(zv)
    ms = _split_dot(gv * gv, seg) * (1.0 / HEAD_DIM)
    r = lax.rsqrt(ms + EPS)
    xh = gv * r
    return u, du, dgv, r, xh, xh * vn


def _sgu_fwd(z, wtril, bexp, vn, seg, *, tr, name):
    T = z.shape[0]
    nch = tr // CHUNK

    def body(u_ref, v_ref, w_ref, b_ref, vn_ref, seg_ref, o_ref):
        u, _, _, _, _, vg = _sgu_common(u_ref[...], v_ref[...], vn_ref[...], seg_ref[...])
        grp = _lane((CHUNK, SGU_WIDTH)) // HEAD_DIM
        outs = []
        for ch in range(nch):
            vc = vg[ch * CHUNK:(ch + 1) * CHUNK]
            s = b_ref[...]
            for g in range(4):
                s = s + jnp.where(grp == g, _dot(w_ref[g], vc), 0.0)
            outs.append(u[ch * CHUNK:(ch + 1) * CHUNK] * s)
        o_ref[...] = jnp.concatenate(outs, axis=0).astype(o_ref.dtype)

    col = lambda j: pl.BlockSpec((tr, 256), lambda i: (i, j))
    return pl.pallas_call(
        body, grid=(T // tr,),
        in_specs=[col(4), col(5), _full((4, CHUNK, CHUNK)), _full((CHUNK, 256)), _full((1, 256)), _full((256, 256))],
        out_specs=col(0), out_shape=SDS((T, SGU_WIDTH), MXU_DTYPE), name=name,
        compiler_params=_cp("parallel"))(z, z, wtril, bexp, vn, seg)


def _sgu_bwd(z, wtril, bexp, vn, seg, dsg, dz, *, tr, name):
    T = z.shape[0]
    nch = tr // CHUNK

    def body(u_ref, v_ref, w_ref, b_ref, vn_ref, seg_ref, d_ref, _dz_in, dz_ref, dw_ref, db_ref, dvn_ref):
        i = pl.program_id(0)
        seg_m = seg_ref[...]
        vn_v = vn_ref[...]
        u, du, dgv, r, xh, vg = _sgu_common(u_ref[...], v_ref[...], vn_v, seg_m)
        d = d_ref[...]
        grp = _lane((CHUNK, SGU_WIDTH)) // HEAD_DIM
        tril = _row((CHUNK, CHUNK)) >= _lane((CHUNK, CHUNK))

        @pl.when(i == 0)
        def _():
            dw_ref[...] = jnp.zeros_like(dw_ref)
            db_ref[...] = jnp.zeros_like(db_ref)
            dvn_ref[...] = jnp.zeros_like(dvn_ref)

        dus, dvgs = [], []
        for ch in range(nch):
            sl = slice(ch * CHUNK, (ch + 1) * CHUNK)
            vc = vg[sl]
            s = b_ref[...]
            for g in range(4):
                s = s + jnp.where(grp == g, _dot(w_ref[g], vc), 0.0)
            dus.append(d[sl] * s)
            ds = d[sl] * u[sl]
            db_ref[...] += _split_dot(ds, seg_m)
            dvg = jnp.zeros((CHUNK, SGU_WIDTH), F32)
            for g in range(4):
                dsm = jnp.where(grp == g, ds, 0.0)
                dvg = dvg + jnp.where(grp == g, _dot(w_ref[g], ds, TN), 0.0)
                dw_ref[g] += jnp.where(tril, _dot(dsm, vc, NT), 0.0)
            dvgs.append(dvg)
        dup = jnp.concatenate(dus, axis=0)
        dvg = jnp.concatenate(dvgs, axis=0)
        dvn_ref[...] += _fold_lanes(jnp.sum(dvg * xh, axis=0, keepdims=True), HEAD_DIM)
        gy = dvg * vn_v
        dgvv = r * (gy - xh * (_split_dot(xh * gy, seg_m) * (1.0 / HEAD_DIM)))
        dz_ref[...] = jnp.concatenate([dup * du, dgvv * dgv], axis=1).astype(dz_ref.dtype)

    col = lambda j: pl.BlockSpec((tr, 256), lambda i: (i, j))
    return pl.pallas_call(
        body, grid=(T // tr,),
        in_specs=[col(4), col(5), _full((4, CHUNK, CHUNK)), _full((CHUNK, 256)), _full((1, 256)), _full((256, 256)),
                  col(0), ANY],
        out_specs=[pl.BlockSpec((tr, 512), lambda i: (i, 2)), _full((4, CHUNK, CHUNK)), _full((CHUNK, 256)),
                   _full((1, 256))],
        out_shape=[SDS((T, IN_COLS), MXU_DTYPE), SDS((4, CHUNK, CHUNK), F32), SDS((CHUNK, 256), F32),
                   SDS((1, 256), F32)],
        input_output_aliases={7: 0}, name=name, compiler_params=_cp("arbitrary"))(
            z, z, wtril, bexp, vn, seg, dsg, dz)


def _merge_fwd(pa, at, sg, wa, wb, wc, z, layer, *, tm, tn, name):
    T = pa.shape[0]
    gb = GATE_COL0 // tn
    nb = D_MODEL // tn

    def body(pa_ref, at_ref, sg_ref, wa_ref, wb_ref, wc_ref, g0_ref, g1_ref, g2_ref, m_ref, y_ref):
        acc = None
        for idx, (op_ref, w_ref, g_ref) in enumerate(((pa_ref, wa_ref, g0_ref), (at_ref, wb_ref, g1_ref),
                                                      (sg_ref, wc_ref, g2_ref))):
            y = _dot(op_ref[...], w_ref[...])
            y_ref[idx] = y
            t = _sigmoid(g_ref[...]) * y
            acc = t if acc is None else acc + t
        m_ref[...] = acc.astype(m_ref.dtype)

    op = lambda w: pl.BlockSpec((tm, w), lambda i, j: (i, 0))
    wt = lambda k: pl.BlockSpec((None, k, tn), lambda i, j: (layer, 0, j))
    gate = lambda b: pl.BlockSpec((tm, tn), lambda i, j: (i, gb + b * nb + j))
    return pl.pallas_call(
        body, grid=(T // tm, nb),
        in_specs=[op(256), op(512), op(256), wt(256), wt(512), wt(256), gate(0), gate(1), gate(2)],
        out_specs=[pl.BlockSpec((tm, tn), lambda i, j: (i, j)), pl.BlockSpec((3, tm, tn), lambda i, j: (0, i, j))],
        out_shape=[SDS((T, D_MODEL), MXU_DTYPE), SDS((3, T, D_MODEL), F32)],
        name=name, compiler_params=_cp("parallel", "parallel"))(pa, at, sg, wa, wb, wc, z, z, z)


def _merge_bwd(dm, y, z, *, tr, tn, name):
    T = dm.shape[0]
    gb = GATE_COL0 // tn
    nb = D_MODEL // tn

    def body(dm_ref, y_ref, g_ref, dy_ref, dz_ref):
        g = _sigmoid(g_ref[...])
        d = dm_ref[...]
        dy_ref[...] = (d * g).astype(dy_ref.dtype)
        dz_ref[...] = (d * y_ref[...] * g * (1.0 - g)).astype(dz_ref.dtype)

    return pl.pallas_call(
        body, grid=(T // tr, 3, nb),
        in_specs=[pl.BlockSpec((tr, tn), lambda i, b, j: (i, j)),
                  pl.BlockSpec((None, tr, tn), lambda i, b, j: (b, i, j)),
                  pl.BlockSpec((tr, tn), lambda i, b, j: (i, gb + b * nb + j))],
        out_specs=[pl.BlockSpec((None, tr, tn), lambda i, b, j: (b, i, j)),
                   pl.BlockSpec((tr, tn), lambda i, b, j: (i, gb + b * nb + j))],
        out_shape=[SDS((3, T, D_MODEL), MXU_DTYPE), SDS((T, IN_COLS), MXU_DTYPE)],
        name=name, compiler_params=_cp("parallel", "parallel", "parallel"))(dm, y, z)


def _conv3(xe, w, b):
    return (w[0:1] * pltpu.roll(xe, 2, 0) + w[1:2] * pltpu.roll(xe, 1, 0) + w[2:3] * xe)[8:] + b


def _conv_act_fwd(up, cw, cb, *, tr, tc, name):
    T = up.shape[0]
    nc = D_FF // tc
    hb = tr // 8

    def body(ug_ref, ugp_ref, uv_ref, uvp_ref, wg_ref, wv_ref, bg_ref, bv_ref, o_ref):
        i = pl.program_id(1)
        first = i == 0
        cg = _conv3(jnp.concatenate([jnp.where(first, 0.0, ugp_ref[...]), ug_ref[...]], axis=0), wg_ref[...], bg_ref[...])
        cv = _conv3(jnp.concatenate([jnp.where(first, 0.0, uvp_ref[...]), uv_ref[...]], axis=0), wv_ref[...], bv_ref[...])
        o_ref[...] = (cg * _sigmoid(cg) * cv).astype(o_ref.dtype)

    tile = lambda off: pl.BlockSpec((tr, tc), lambda j, i: (i, off + j))
    prev = lambda off: pl.BlockSpec((8, tc), lambda j, i: (jnp.maximum(i * hb - 1, 0), off + j))
    par = lambda rows, off: pl.BlockSpec((rows, tc), lambda j, i: (0, off + j))
    return pl.pallas_call(
        body, grid=(nc, T // tr),
        in_specs=[tile(0), prev(0), tile(nc), prev(nc), par(3, 0), par(3, nc), par(1, 0), par(1, nc)],
        out_specs=pl.BlockSpec((tr, tc), lambda j, i: (i, j)),
        out_shape=SDS((T, D_FF), MXU_DTYPE), name=name,
        compiler_params=_cp("parallel", "parallel"))(up, up, up, up, cw, cw, cb, cb)


def _conv_act_bwd(up, cw, cb, dact, *, tr, tc, name):
    T = up.shape[0]
    nc = D_FF // tc
    hb = tr // 8
    nr = T // tr

    def body(ug_ref, ugp_ref, ugn_ref, uv_ref, uvp_ref, uvn_ref, da_ref, dan_ref, wg_ref, wv_ref, bg_ref, bv_ref,
             du_ref, dwg_ref, dwv_ref, dbg_ref, dbv_ref):
        i = pl.program_id(1)
        first, last = i == 0, i == nr - 1
        da = jnp.concatenate([da_ref[...], jnp.where(last, 0.0, dan_ref[...])], axis=0)
        uge = jnp.concatenate([jnp.where(first, 0.0, ugp_ref[...]), ug_ref[...], ugn_ref[...]], axis=0)
        uve = jnp.concatenate([jnp.where(first, 0.0, uvp_ref[...]), uv_ref[...], uvn_ref[...]], axis=0)
        wg, wv = wg_ref[...], wv_ref[...]
        cg = _conv3(uge, wg, bg_ref[...])
        cv = _conv3(uve, wv, bv_ref[...])
        sg = _sigmoid(cg)
        dcg = da * cv * (sg * (1.0 + cg * (1.0 - sg)))
        dcv = da * (cg * sg)
        nrow = tr + 8

        def back(dc, w):
            return (w[2:3] * dc + w[1:2] * pltpu.roll(dc, nrow - 1, 0) + w[0:1] * pltpu.roll(dc, nrow - 2, 0))[:tr]

        du_ref[0] = back(dcg, wg).astype(du_ref.dtype)
        du_ref[1] = back(dcv, wv).astype(du_ref.dtype)

        def wgrad(dc, ue):
            d = dc[:tr]
            rows = [jnp.sum(d * pltpu.roll(ue, 2, 0)[8:8 + tr], axis=0, keepdims=True),
                    jnp.sum(d * pltpu.roll(ue, 1, 0)[8:8 + tr], axis=0, keepdims=True),
                    jnp.sum(d * ue[8:8 + tr], axis=0, keepdims=True)]
            return jnp.concatenate(rows, axis=0), jnp.sum(d, axis=0, keepdims=True)

        dwg, dbg = wgrad(dcg, uge)
        dwv, dbv = wgrad(dcv, uve)

        @pl.when(first)
        def _():
            dwg_ref[...] = jnp.zeros_like(dwg_ref)
            dwv_ref[...] = jnp.zeros_like(dwv_ref)
            dbg_ref[...] = jnp.zeros_like(dbg_ref)
            dbv_ref[...] = jnp.zeros_like(dbv_ref)
        dwg_ref[...] += dwg
        dwv_ref[...] += dwv
        dbg_ref[...] += dbg
        dbv_ref[...] += dbv

    tile = lambda off: pl.BlockSpec((tr, tc), lambda j, i: (i, off + j))
    prev = lambda off: pl.BlockSpec((8, tc), lambda j, i: (jnp.maximum(i * hb - 1, 0), off + j))
    nxt = lambda off: pl.BlockSpec((8, tc), lambda j, i: (jnp.minimum((i + 1) * hb, T // 8 - 1), off + j))
    par = lambda rows, off: pl.BlockSpec((rows, tc), lambda j, i: (0, off + j))
    acc = lambda rows: pl.BlockSpec((rows, tc), lambda j, i: (0, j))
    return pl.pallas_call(
        body, grid=(nc, nr),
        in_specs=[tile(0), prev(0), nxt(0), tile(nc), prev(nc), nxt(nc), tile(0), nxt(0),
                  par(3, 0), par(3, nc), par(1, 0), par(1, nc)],
        out_specs=[pl.BlockSpec((2, tr, tc), lambda j, i: (0, i, j)), acc(3), acc(3), acc(1), acc(1)],
        out_shape=[SDS((2, T, D_FF), MXU_DTYPE), SDS((3, D_FF), F32), SDS((3, D_FF), F32),
                   SDS((1, D_FF), F32), SDS((1, D_FF), F32)],
        name=name, compiler_params=_cp("parallel", "arbitrary"))(
            up, up, up, up, up, up, dact, dact, cw, cw, cb, cb)


def _row_tile(rows, cap):
    t = min(cap, rows)
    t -= t % 8
    while rows % t:
        t -= 8
    return t


def _adamw(w, g, m, v, *, tr, name):
    R, C = w.shape
    assert R % tr == 0, (R, tr)

    def body(w_ref, g_ref, m_ref, v_ref, d_ref, nm_ref, nv_ref):
        gv = g_ref[...]
        mn = ADAM_B1 * m_ref[...] + (1.0 - ADAM_B1) * gv
        vn = ADAM_B2 * v_ref[...] + (1.0 - ADAM_B2) * (gv * gv)
        m_hat = mn / (1.0 - ADAM_B1 ** ADAM_STEP)
        v_hat = vn / (1.0 - ADAM_B2 ** ADAM_STEP)
        d_ref[...] = -ADAM_LR * (m_hat / (jnp.sqrt(v_hat) + ADAM_EPS) + ADAM_WD * w_ref[...])
        nm_ref[...] = mn
        nv_ref[...] = vn

    rows = pl.BlockSpec((tr, C), lambda i: (i, 0))
    return pl.pallas_call(
        body, grid=(R // tr,), in_specs=[rows] * 4, out_specs=[rows] * 3,
        out_shape=[SDS((R, C), F32)] * 3, name=name, compiler_params=_cp("parallel"))(w, g, m, v)


def _sum_slots(r, *, tr, name):
    S, R, C = r.shape
    assert R % tr == 0, (R, tr)

    def body(r_ref, o_ref):
        acc = r_ref[0]
        for s in range(1, S):
            acc = acc + r_ref[s]
        o_ref[...] = acc

    return pl.pallas_call(
        body, grid=(R // tr,), in_specs=[pl.BlockSpec((S, tr, C), lambda i: (0, i, 0))],
        out_specs=pl.BlockSpec((tr, C), lambda i: (i, 0)), out_shape=SDS((R, C), F32),
        name=name, compiler_params=_cp("parallel"))(r)


def _add_layer(g, r, c_idx, *, tr, name):
    _, R, C = g.shape
    assert R % tr == 0, (R, tr)

    def body(c_ref, g_ref, r_ref, o_ref):
        o_ref[...] = g_ref[...] + r_ref[...]

    grid_spec = pltpu.PrefetchScalarGridSpec(
        num_scalar_prefetch=1, grid=(R // tr,),
        in_specs=[pl.BlockSpec((None, tr, C), lambda i, c: (c[0], i, 0)), pl.BlockSpec((tr, C), lambda i, c: (i, 0))],
        out_specs=pl.BlockSpec((tr, C), lambda i, c: (i, 0)))
    return pl.pallas_call(body, grid_spec=grid_spec, out_shape=SDS((R, C), F32), name=name,
                          compiler_params=_cp("parallel"))(c_idx, g, r)


def _mesh_pos():
    return lax.axis_index("x"), lax.axis_index("y"), lax.axis_index("c")


def _shard_view(ref, axis, s, size):
    idx = [slice(None)] * len(ref.shape)
    idx[axis] = pl.ds(s * size, size)
    return ref.at[tuple(idx)]


def _allgather_chips(shards, axes, *, name):
    n = len(shards)
    out_shapes = []
    for a, ax in zip(shards, axes):
        shp = list(a.shape)
        shp[ax] *= N_CHIPS
        out_shapes.append(SDS(tuple(shp), a.dtype))

    def body(*refs):
        ins, outs = refs[:n], refs[n:2 * n]
        send_sems, recv_sems, local_sems = refs[2 * n:]
        x, y, c = _mesh_pos()
        me = 2 * x + y
        peers = [(1 - x, y, c), (x, 1 - y, c), (1 - x, 1 - y, c)]

        def remote(w, p, s):
            return pltpu.make_async_remote_copy(
                src_ref=ins[w], dst_ref=_shard_view(outs[w], axes[w], s, ins[w].shape[axes[w]]),
                send_sem=send_sems.at[3 * w + p], recv_sem=recv_sems.at[3 * w + p],
                device_id=peers[p], device_id_type=MESH)

        def local(w, s):
            return pltpu.make_async_copy(ins[w], _shard_view(outs[w], axes[w], s, ins[w].shape[axes[w]]),
                                         local_sems.at[w])

        for s in range(N_CHIPS):
            @pl.when(me == s)
            def _():
                for w in range(n):
                    local(w, s).start()
                    for p in range(3):
                        remote(w, p, s).start()
        for w in range(n):
            local(w, 0).wait()
            for p in range(3):
                cp = remote(w, p, 0)
                cp.wait_send()
                cp.wait_recv()

    return pl.pallas_call(
        body, in_specs=[ANY] * n, out_specs=[ANY] * n, out_shape=out_shapes,
        scratch_shapes=[pltpu.SemaphoreType.DMA((3 * n,)), pltpu.SemaphoreType.DMA((3 * n,)),
                        pltpu.SemaphoreType.DMA((n,))],
        name=name, compiler_params=pltpu.CompilerParams(has_side_effects=True))(*shards)


def _swap_layers(grads, *, name):
    n = len(grads)

    def body(*refs):
        ins, outs = refs[:n], refs[n:2 * n]
        send_sems, recv_sems = refs[2 * n:]
        x, y, c = _mesh_pos()
        copies = [pltpu.make_async_remote_copy(
            src_ref=ins[w].at[1 - c], dst_ref=outs[w], send_sem=send_sems.at[w], recv_sem=recv_sems.at[w],
            device_id=(x, y, 1 - c), device_id_type=MESH) for w in range(n)]
        for cp in copies:
            cp.start()
        for cp in copies:
            cp.wait_send()
            cp.wait_recv()

    return pl.pallas_call(
        body, in_specs=[ANY] * n, out_specs=[ANY] * n,
        out_shape=[SDS(g.shape[1:], g.dtype) for g in grads],
        scratch_shapes=[pltpu.SemaphoreType.DMA((n,)), pltpu.SemaphoreType.DMA((n,))],
        name=name, compiler_params=pltpu.CompilerParams(has_side_effects=True))(*grads)


def _scatter_chips(parts, axes, *, name):
    n = len(parts)
    sizes = [p.shape[ax] // N_CHIPS for p, ax in zip(parts, axes)]
    out_shapes = []
    for p, ax, sz in zip(parts, axes, sizes):
        shp = list(p.shape)
        shp[ax] = sz
        out_shapes.append(SDS((N_CHIPS,) + tuple(shp), p.dtype))
    flips = (2, 1, 3)

    def body(*refs):
        ins, outs = refs[:n], refs[n:2 * n]
        send_sems, recv_sems, local_sems = refs[2 * n:]
        x, y, c = _mesh_pos()
        me = 2 * x + y
        peers = [(1 - x, y, c), (x, 1 - y, c), (1 - x, 1 - y, c)]

        def remote(w, p, s):
            return pltpu.make_async_remote_copy(
                src_ref=_shard_view(ins[w], axes[w], s ^ flips[p], sizes[w]), dst_ref=outs[w].at[s],
                send_sem=send_sems.at[3 * w + p], recv_sem=recv_sems.at[3 * w + p],
                device_id=peers[p], device_id_type=MESH)

        def local(w, s):
            return pltpu.make_async_copy(_shard_view(ins[w], axes[w], s, sizes[w]), outs[w].at[s], local_sems.at[w])

        for s in range(N_CHIPS):
            @pl.when(me == s)
            def _():
                for w in range(n):
                    local(w, s).start()
                    for p in range(3):
                        remote(w, p, s).start()
        for w in range(n):
            local(w, 0).wait()
            for p in range(3):
                cp = remote(w, p, 0)
                cp.wait_send()
                cp.wait_recv()

    return pl.pallas_call(
        body, in_specs=[ANY] * n, out_specs=[ANY] * n, out_shape=out_shapes,
        scratch_shapes=[pltpu.SemaphoreType.DMA((3 * n,)), pltpu.SemaphoreType.DMA((3 * n,)),
                        pltpu.SemaphoreType.DMA((n,))],
        name=name, compiler_params=pltpu.CompilerParams(has_side_effects=True))(*parts)


def _pair_layers(halves, *, name):
    n = len(halves)

    def body(*refs):
        ins, outs = refs[:n], refs[n:2 * n]
        send_sems, recv_sems, local_sems = refs[2 * n:]
        x, y, c = _mesh_pos()
        remotes = [pltpu.make_async_remote_copy(
            src_ref=ins[w], dst_ref=outs[w].at[c], send_sem=send_sems.at[w], recv_sem=recv_sems.at[w],
            device_id=(x, y, 1 - c), device_id_type=MESH) for w in range(n)]
        locals_ = [pltpu.make_async_copy(ins[w], outs[w].at[c], local_sems.at[w]) for w in range(n)]
        for cp in locals_ + remotes:
            cp.start()
        for cp in locals_:
            cp.wait()
        for cp in remotes:
            cp.wait_send()
            cp.wait_recv()

    return pl.pallas_call(
        body, in_specs=[ANY] * n, out_specs=[ANY] * n,
        out_shape=[SDS((DEPTH,) + h.shape, h.dtype) for h in halves],
        scratch_shapes=[pltpu.SemaphoreType.DMA((n,)), pltpu.SemaphoreType.DMA((n,)), pltpu.SemaphoreType.DMA((n,))],
        name=name, compiler_params=pltpu.CompilerParams(has_side_effects=True))(*halves)


def _allgather_devices(pack, *, name):
    def body(in_ref, out_ref, send_sems, recv_sems, local_sem):
        x, y, c = _mesh_pos()
        me = 4 * x + 2 * y + c
        remotes = []
        for f in range(1, N_DEV):
            fx, fy, fc = (f >> 2) & 1, (f >> 1) & 1, f & 1
            peer = (x ^ fx, y ^ fy, c ^ fc)
            remotes.append(pltpu.make_async_remote_copy(
                src_ref=in_ref, dst_ref=out_ref.at[me], send_sem=send_sems.at[f - 1], recv_sem=recv_sems.at[f - 1],
                device_id=peer, device_id_type=MESH))
        mine = pltpu.make_async_copy(in_ref, out_ref.at[me], local_sem)
        mine.start()
        for cp in remotes:
            cp.start()
        mine.wait()
        for cp in remotes:
            cp.wait_send()
            cp.wait_recv()

    return pl.pallas_call(
        body, in_specs=[ANY], out_specs=ANY, out_shape=SDS((N_DEV,) + pack.shape, pack.dtype),
        scratch_shapes=[pltpu.SemaphoreType.DMA((N_DEV - 1,)), pltpu.SemaphoreType.DMA((N_DEV - 1,)),
                        pltpu.SemaphoreType.DMA],
        name=name, compiler_params=pltpu.CompilerParams(has_side_effects=True))(pack)


BIG = ('w_in', 'w_proj_a', 'w_proj_b', 'w_proj_c', 'w_out', 'w_up', 'w_down')
BIG_SHARD_AXIS = {'w_in': 2, 'w_proj_a': 2, 'w_proj_b': 2, 'w_proj_c': 2, 'w_out': 1, 'w_up': 2, 'w_down': 1}
SMALL = ('norm1', 'q_norm', 'k_norm', 'sinks', 'w_pool', 'pool_scale', 'sgu_v_norm', 'w_s', 'b_s', 'norm2',
         'conv_b', 'conv_w')
WEIGHTS = ('norm1', 'w_in', 'q_norm', 'k_norm', 'sinks', 'w_pool', 'pool_scale', 'sgu_v_norm', 'w_s', 'b_s',
           'w_proj_a', 'w_proj_b', 'w_proj_c', 'w_out', 'norm2', 'w_up', 'conv_w', 'conv_b', 'w_down')


def _rope_tables(positions):
    inv_freq = ROPE_THETA ** (-jnp.arange(0, HEAD_DIM, 2, dtype=F32) / HEAD_DIM)
    ang = positions.astype(F32)[:, None] * inv_freq
    cos, sin = jnp.cos(ang), jnp.sin(ang)
    c = jnp.concatenate([cos, cos], axis=1)
    s = jnp.concatenate([-sin, sin], axis=1)
    return jnp.concatenate([c, c], axis=1), jnp.concatenate([s, s], axis=1)


def _block_diag4(w):
    out = jnp.zeros((POOL_WIDTH, POOL_WIDTH), w.dtype)
    for g in range(4):
        out = lax.dynamic_update_slice(out, w[g], (g * HEAD_DIM, g * HEAD_DIM))
    return out


def _local_step(x, target, cos, sin, wf, sp):
    T = x.shape[0]
    tm1 = min(1024, T)
    tm = min(512, T)
    tr = min(256, T)
    tkt = min(512, T)
    seg = _seg_matrix(256, HEAD_DIM)
    saved = []
    xl = x
    for l in range(DEPTH):
        p = f"l{l}_"
        c = dict(
            g1=sp['norm1'][l][None], g2=sp['norm2'][l][None],
            wbd=_block_diag4(sp['w_pool'][l]).astype(MXU_DTYPE), scale=sp['pool_scale'][l][None],
            gq=jnp.tile(sp['q_norm'][l], 4)[None], gk=jnp.tile(sp['k_norm'][l], 2)[None],
            sinks=jnp.broadcast_to(sp['sinks'][l][:, None], (N_Q_HEADS, 128)),
            wtril=jnp.tril(sp['w_s'][l]).astype(MXU_DTYPE),
            bexp=jnp.repeat(sp['b_s'][l].T, HEAD_DIM, axis=1), vn=jnp.tile(sp['sgu_v_norm'][l], 4)[None],
            cw=wf['conv_w'][l], cb=sp['conv_b'][l][None])
        h1 = _rms_fwd(xl, c['g1'], tr=tr, name=p + "rms1")
        z = _mm(h1, wf['w_in'], mode='nn', b_lead=l, tm=tm1, tn=768, tk=D_MODEL, name=p + "in_proj")
        pa = _pool_fwd(z, c['wbd'], c['scale'], tr=tr, name=p + "pool")
        q, k, v = _qkv_prep(z, cos, sin, c['gq'], c['gk'], seg, tr=tr, name=p + "qkv_prep")
        at = _attn_fwd(q, k, v, c['sinks'], name=p + "attn")
        sg = _sgu_fwd(z, c['wtril'], c['bexp'], c['vn'], seg, tr=tr, name=p + "sgu")
        merged, y3 = _merge_fwd(pa, at, sg, wf['w_proj_a'], wf['w_proj_b'], wf['w_proj_c'], z, l,
                                tm=tm, tn=512, name=p + "merge")
        x1 = _mm(merged, wf['w_out'], mode='nn', b_lead=l, add=xl, tm=tm, tn=D_MODEL, tk=D_MODEL, name=p + "out_proj")
        h2 = _rms_fwd(x1, c['g2'], tr=tr, name=p + "rms2")
        up = _mm(h2, wf['w_up'], mode='nn', b_lead=l, tm=tm1, tn=512, tk=D_MODEL, name=p + "up_proj")
        act = _conv_act_fwd(up, c['cw'], c['cb'], tr=tr, tc=1408, name=p + "conv_act")
        x2 = _mm(act, wf['w_down'], mode='nn', b_lead=l, add=x1, tm=tm, tn=D_MODEL, tk=1408, name=p + "down_proj")
        saved.append(dict(c, x=xl, h1=h1, z=z, pa=pa, q=q, k=k, v=v, at=at, sg=sg, merged=merged, y3=y3,
                          x1=x1, h2=h2, up=up, act=act))
        xl = x2

    loss_row, dx, dxb = _loss_head(xl, target, tr=tr, name="loss_head")

    gb = {n: None for n in BIG}
    gs = {n: [None] * DEPTH for n in SMALL}
    for l in reversed(range(DEPTH)):
        p = f"l{l}_b_"
        s = saved[l]
        dact = _mm(dxb, wf['w_down'], mode='nt', b_lead=l, tm=tm, tn=1408, tk=D_MODEL, name=p + "down_dx")
        gb['w_down'] = _mm(s['act'], dxb, mode='tn', tm=1408, tn=D_MODEL, tk=tkt, out_lead=l,
                           out_into=gb['w_down'], name=p + "down_dw")
        dup, dwg, dwv, dbg, dbv = _conv_act_bwd(s['up'], s['cw'], s['cb'], dact, tr=min(512, T), tc=256,
                                                name=p + "conv_act")
        gs['conv_w'][l] = jnp.concatenate([dwg, dwv], axis=1)
        gs['conv_b'][l] = jnp.concatenate([dbg, dbv], axis=1)[0]
        dh2 = None
        for half in range(2):
            dh2 = _mm(dup, wf['w_up'], mode='nt', a_lead=half, b_lead=l, tm=tm, tn=D_MODEL, tk=1408,
                      b_koff=2 * half, add=dh2, name=p + f"up_dx{half}")
            gb['w_up'] = _mm(s['h2'], dup, mode='tn', b_lead=half, tm=D_MODEL, tn=1408, tk=tkt, out_lead=l,
                             out_into=gb['w_up'], out_joff=2 * half, out_n=2 * D_FF, name=p + f"up_dw{half}")
        dx1, dx1b, dg2 = _rms_bwd(s['x1'], s['g2'], dh2, dx, tr=tr, name=p + "rms2")
        gs['norm2'][l] = dg2[0]
        dmerged = _mm(dx1b, wf['w_out'], mode='nt', b_lead=l, tm=tm, tn=D_MODEL, tk=D_MODEL, name=p + "out_dx")
        gb['w_out'] = _mm(s['merged'], dx1b, mode='tn', tm=D_MODEL, tn=D_MODEL, tk=tkt, out_lead=l,
                          out_into=gb['w_out'], name=p + "out_dw")
        dy3, dz = _merge_bwd(dmerged, s['y3'], s['z'], tr=tr, tn=512, name=p + "merge")
        dbr = []
        for idx, (wn, opn, width) in enumerate((('w_proj_a', 'pa', POOL_WIDTH), ('w_proj_b', 'at', ATTN_WIDTH),
                                                ('w_proj_c', 'sg', SGU_WIDTH))):
            dbr.append(_mm(dy3, wf[wn], mode='nt', a_lead=idx, b_lead=l, tm=tm, tn=width, tk=D_MODEL,
                           name=p + f"proj{idx}_dx"))
            gb[wn] = _mm(s[opn], dy3, mode='tn', b_lead=idx, tm=width, tn=D_MODEL, tk=tkt, out_lead=l,
                         out_into=gb[wn], name=p + f"proj{idx}_dw")
        dpa, dat, dsg = dbr
        dq, dkc, dkp, dvc, dvp, dsk = _attn_bwd(s['q'], s['k'], s['v'], s['sinks'], dat, name=p + "attn")
        gs['sinks'][l] = dsk[:, 0]
        dz, dgq, dgk, dwbd, dsc = _mixer_ab_bwd(s['z'], cos, sin, s['gq'], s['gk'], seg, dq, dkc, dkp, dvc, dvp,
                                                dpa, s['wbd'], s['scale'], dz, tr=tr, name=p + "qkv_pool")
        gs['q_norm'][l] = dgq[0, :HEAD_DIM]
        gs['k_norm'][l] = dgk[0, :HEAD_DIM]
        gs['w_pool'][l] = jnp.stack([dwbd[g * HEAD_DIM:(g + 1) * HEAD_DIM, g * HEAD_DIM:(g + 1) * HEAD_DIM]
                                     for g in range(4)])
        gs['pool_scale'][l] = dsc[0]
        dz, dws, dbrows, dvn = _sgu_bwd(s['z'], s['wtril'], s['bexp'], s['vn'], seg, dsg, dz, tr=tr, name=p + "sgu")
        gs['w_s'][l] = dws
        gs['b_s'][l] = dbrows[:, ::HEAD_DIM].T
        gs['sgu_v_norm'][l] = dvn[0, :HEAD_DIM]
        dh1 = _mm(dz, wf['w_in'], mode='nt', b_lead=l, tm=tm, tn=D_MODEL, tk=1536, name=p + "in_dx")
        gb['w_in'] = _mm(s['h1'], dz, mode='tn', tm=D_MODEL, tn=1152, tk=tkt, out_lead=l,
                         out_into=gb['w_in'], name=p + "in_dw")
        dx, dxb, dg1 = _rms_bwd(s['x'], s['g1'], dh1, dx1, tr=tr, name=p + "rms1")
        gs['norm1'][l] = dg1[0]
    gs = {n: jnp.stack(v) for n, v in gs.items()}
    return loss_row, dx, gb, gs


def _pack(arrays):
    flat = []
    for a in arrays:
        f = a.reshape(-1).astype(F32)
        flat.append(jnp.pad(f, (0, (-f.shape[0]) % 128)))
    v = jnp.concatenate(flat)
    v = jnp.pad(v, (0, (-v.shape[0]) % 1024))
    return v.reshape(-1, 128)


def _unpack(pack, shapes):
    v = pack.reshape(-1)
    out, off = [], 0
    for shp in shapes:
        nel = int(np.prod(shp))
        out.append(v[off:off + nel].reshape(shp))
        off += nel + (-nel) % 128
    return out


def kernel(x, positions, norm1, w_in, q_norm, k_norm, sinks, w_pool, pool_scale, sgu_v_norm, w_s, b_s, w_proj_a, w_proj_b, w_proj_c, w_out, norm2, w_up, conv_w, conv_b, w_down, loss_target, m_norm1, m_w_in, m_q_norm, m_k_norm, m_sinks, m_w_pool, m_pool_scale, m_sgu_v_norm, m_w_s, m_b_s, m_w_proj_a, m_w_proj_b, m_w_proj_c, m_w_out, m_norm2, m_w_up, m_conv_w, m_conv_b, m_w_down, v_norm1, v_w_in, v_q_norm, v_k_norm, v_sinks, v_w_pool, v_pool_scale, v_sgu_v_norm, v_w_s, v_b_s, v_w_proj_a, v_w_proj_b, v_w_proj_c, v_w_out, v_norm2, v_w_up, v_conv_w, v_conv_b, v_w_down):
    w = dict(norm1=norm1, w_in=w_in, q_norm=q_norm, k_norm=k_norm, sinks=sinks, w_pool=w_pool, pool_scale=pool_scale,
             sgu_v_norm=sgu_v_norm, w_s=w_s, b_s=b_s, w_proj_a=w_proj_a, w_proj_b=w_proj_b, w_proj_c=w_proj_c,
             w_out=w_out, norm2=norm2, w_up=w_up, conv_w=conv_w, conv_b=conv_b, w_down=w_down)
    m = dict(norm1=m_norm1, w_in=m_w_in, q_norm=m_q_norm, k_norm=m_k_norm, sinks=m_sinks, w_pool=m_w_pool,
             pool_scale=m_pool_scale, sgu_v_norm=m_sgu_v_norm, w_s=m_w_s, b_s=m_b_s, w_proj_a=m_w_proj_a,
             w_proj_b=m_w_proj_b, w_proj_c=m_w_proj_c, w_out=m_w_out, norm2=m_norm2, w_up=m_w_up, conv_w=m_conv_w,
             conv_b=m_conv_b, w_down=m_w_down)
    v = dict(norm1=v_norm1, w_in=v_w_in, q_norm=v_q_norm, k_norm=v_k_norm, sinks=v_sinks, w_pool=v_w_pool,
             pool_scale=v_pool_scale, sgu_v_norm=v_sgu_v_norm, w_s=v_w_s, b_s=v_b_s, w_proj_a=v_w_proj_a,
             w_proj_b=v_w_proj_b, w_proj_c=v_w_proj_c, w_out=v_w_out, norm2=v_norm2, w_up=v_w_up, conv_w=v_conv_w,
             conv_b=v_conv_b, w_down=v_w_down)
    chip = 2 * lax.axis_index("x") + lax.axis_index("y")
    core = lax.axis_index("c")

    gathered = _allgather_chips([w[n].astype(MXU_DTYPE) for n in BIG] + [conv_w],
                                [BIG_SHARD_AXIS[n] for n in BIG] + [2], name="gather_weights")
    wf = dict(zip(BIG + ('conv_w',), gathered))

    cos, sin = _rope_tables(positions[0])
    sp = {n: w[n] for n in SMALL if n != 'conv_w'}
    loss_row, dx, gb, gs = _local_step(x[0], loss_target[0], cos, sin, wf, sp)
    loss = lax.psum(loss_row[0, 0], ("x", "y", "c"))

    big = [gb[n] for n in BIG]
    from_sibling = _swap_layers(big, name="grads_swap_layers")
    core_idx = jnp.reshape(core, (1,)).astype(jnp.int32)
    parts = [_add_layer(g, r, core_idx, tr=_row_tile(g.shape[1], 128), name=f"grads_pair_sum_{n}")
             for n, g, r in zip(BIG, big, from_sibling)]
    slots = _scatter_chips(parts, [BIG_SHARD_AXIS[n] - 1 for n in BIG], name="grads_scatter")
    halves = [_sum_slots(sl, tr=_row_tile(sl.shape[1], 128), name=f"grads_chip_sum_{n}") for n, sl in zip(BIG, slots)]
    g_big = dict(zip(BIG, _pair_layers(halves, name="grads_pair_layers")))

    small_shapes = [gs[n].shape for n in SMALL]
    small_pack = _pack([gs[n] for n in SMALL])
    red = _sum_slots(_allgather_devices(small_pack, name="small_gather"), tr=small_pack.shape[0], name="small_sum")
    g_small = dict(zip(SMALL, _unpack(red, small_shapes)))
    grads = dict(g_big)
    grads.update(g_small)
    shard_cols = conv_w.shape[2]
    grads['conv_w'] = lax.dynamic_slice_in_dim(g_small['conv_w'], chip * shard_cols, shard_cols, axis=2)

    delta, new_m, new_v = {}, {}, {}
    for n in BIG:
        shp = w[n].shape
        two_d = lambda a: a.reshape(shp[0] * shp[1], shp[2])
        d, nm, nv = _adamw(two_d(w[n]), two_d(grads[n]), two_d(m[n]), two_d(v[n]),
                           tr=_row_tile(shp[0] * shp[1], 256), name=f"adamw_{n}")
        delta[n], new_m[n], new_v[n] = d.reshape(shp), nm.reshape(shp), nv.reshape(shp)
    shapes = [w[n].shape for n in SMALL]
    packs = [_pack([src[n] for n in SMALL]) for src in (w, grads, m, v)]
    d, nm, nv = _adamw(*packs, tr=packs[0].shape[0], name="adamw_small")
    for dst, src in ((delta, d), (new_m, nm), (new_v, nv)):
        dst.update(zip(SMALL, _unpack(src, shapes)))

    return (loss, dx[None], *[grads[n] for n in WEIGHTS], *[delta[n] for n in WEIGHTS],
            *[new_m[n] for n in WEIGHTS], *[new_v[n] for n in WEIGHTS])
```

```python
import functools
import math

import numpy as np
import jax
import jax.numpy as jnp
from jax import lax
from jax.experimental import pallas as pl
from jax.experimental.pallas import tpu as pltpu

F32 = jnp.float32
MXU_DTYPE = jnp.bfloat16

D_MODEL = 1024
DEPTH = 2
HEAD_DIM = 64
POOL_WINDOWS = (2, 4, 8, 16)
POOL_WIDTH = 256
N_Q_HEADS = 8
ATTN_BLOCK = 128
ATTN_WIDTH = 512
KV_WIDTH = 128
CHUNK = 128
SGU_WIDTH = 256
IN_COLS = 4608
GATE_COL0 = 1536
D_FF = 2816
ROPE_THETA = 10000.0
EPS = 1e-6
ADAM_LR, ADAM_B1, ADAM_B2, ADAM_EPS, ADAM_WD, ADAM_STEP = 0.001, 0.9, 0.999, 1e-08, 0.01, 10

N_CHIPS = 4
N_DEV = 8
VMEM_LIMIT_BYTES = 56 * 1024 * 1024
NEG_BIG = -1e30
MESH = pl.DeviceIdType.MESH
ANY = pl.BlockSpec(memory_space=pl.ANY)

SDS = jax.ShapeDtypeStruct


def _cp(*sem):
    return pltpu.CompilerParams(dimension_semantics=sem, vmem_limit_bytes=VMEM_LIMIT_BYTES)


def _dot(a, b, dims=((1,), (0,))):
    return lax.dot_general(a.astype(MXU_DTYPE), b.astype(MXU_DTYPE), (dims, ((), ())),
                           preferred_element_type=F32)


NT = ((1,), (1,))
TN = ((0,), (0,))


def _split_dot(x, m):
    hi = x.astype(MXU_DTYPE)
    lo = (x - hi.astype(F32)).astype(MXU_DTYPE)
    return _dot(hi, m) + _dot(lo, m)


def _seg_matrix(width, seg):
    idx = np.arange(width) // seg
    return jnp.asarray((idx[:, None] == idx[None, :]).astype(np.float32), dtype=MXU_DTYPE)


def _lane(shape):
    return lax.broadcasted_iota(jnp.int32, shape, len(shape) - 1)


def _row(shape):
    return lax.broadcasted_iota(jnp.int32, shape, 0)


def _full(shape):
    nd = len(shape)
    return pl.BlockSpec(shape, lambda *_: (0,) * nd)


def _gelu(x):
    k = math.sqrt(2.0 / math.pi)
    th = jnp.tanh(k * (x + 0.044715 * (x * x * x)))
    return 0.5 * x * (1.0 + th)


def _gelu_and_grad(x):
    k = math.sqrt(2.0 / math.pi)
    x2 = x * x
    th = jnp.tanh(k * (x + 0.044715 * (x2 * x)))
    g = 0.5 * x * (1.0 + th)
    dg = 0.5 * (1.0 + th) + 0.5 * x * (1.0 - th * th) * (k * (1.0 + 3.0 * 0.044715 * x2))
    return g, dg


def _sigmoid(x):
    return 0.5 * jnp.tanh(0.5 * x) + 0.5


def _swap_halves(x):
    w = x.shape[-1]
    first = (_lane(x.shape) % HEAD_DIM) < (HEAD_DIM // 2)
    return jnp.where(first, pltpu.roll(x, w - HEAD_DIM // 2, 1), pltpu.roll(x, HEAD_DIM // 2, 1))


def _tile_lanes(x, reps):
    return x if reps == 1 else jnp.concatenate([x] * reps, axis=1)


def _fold_lanes(x, period):
    w = x.shape[-1]
    while w > period:
        w //= 2
        x = x + pltpu.roll(x, w, 1)
    return x


def _mm(a, b, *, mode, tm, tn, tk, out_dtype=F32, add=None, name,
        a_lead=None, b_lead=None, out_lead=None, out_into=None, n_lead=DEPTH,
        b_koff=0, out_joff=0, out_n=None):
    ash = a.shape[1:] if a_lead is not None else a.shape
    bsh = b.shape[1:] if b_lead is not None else b.shape
    if mode == 'nn':
        (M, K), (K2, N) = ash, bsh
    elif mode == 'nt':
        (M, K), (N, K2) = ash, bsh
    else:
        (K, M), (K2, N) = ash, bsh
    assert K == K2 or (mode == 'nt' and K2 > K), (ash, bsh, mode)
    assert M % tm == 0 and N % tn == 0 and K % tk == 0, (M, N, K, tm, tn, tk)
    nk = K // tk
    dims = {'nn': ((1,), (0,)), 'nt': NT, 'tn': TN}[mode]

    def lead(spec_shape, imap, lead_idx):
        if lead_idx is None:
            return pl.BlockSpec(spec_shape, imap)
        return pl.BlockSpec((None,) + spec_shape, lambda i, j, k: (lead_idx,) + imap(i, j, k))

    if mode == 'tn':
        a_spec = lead((tk, tm), lambda i, j, k: (k, i), a_lead)
    else:
        a_spec = lead((tm, tk), lambda i, j, k: (i, k), a_lead)
    if mode == 'nt':
        b_spec = lead((tn, tk), lambda i, j, k: (j, k + b_koff), b_lead)
    else:
        b_spec = lead((tk, tn), lambda i, j, k: (k, j), b_lead)
    o_spec = lead((tm, tn), lambda i, j, k: (i, j + out_joff), out_lead)
    n_out = N if out_n is None else out_n
    in_specs = [a_spec, b_spec]
    operands = [a, b]
    if add is not None:
        in_specs.append(pl.BlockSpec((tm, tn), lambda i, j, k: (i, j)))
        operands.append(add)
    aliases = {}
    if out_into is not None:
        in_specs.append(ANY)
        operands.append(out_into)
        aliases = {len(operands) - 1: 0}
    has_add = add is not None
    acc_in_out = nk > 1 and out_dtype == F32

    def body(*refs):
        a_ref, b_ref = refs[0], refs[1]
        pos = 2
        add_ref = None
        if has_add:
            add_ref = refs[pos]
            pos += 1
        if out_into is not None:
            pos += 1
        o_ref = refs[pos]
        acc_ref = refs[pos + 1] if (nk > 1 and not acc_in_out) else None
        p = _dot(a_ref[...], b_ref[...], dims)
        if nk == 1:
            if has_add:
                p = p + add_ref[...]
            o_ref[...] = p.astype(o_ref.dtype)
            return
        k = pl.program_id(2)
        tgt = o_ref if acc_in_out else acc_ref

        @pl.when(k == 0)
        def _():
            tgt[...] = p + add_ref[...] if has_add else p

        @pl.when(k > 0)
        def _():
            tgt[...] += p

        if not acc_in_out:
            @pl.when(k == nk - 1)
            def _():
                o_ref[...] = acc_ref[...].astype(o_ref.dtype)

    out_shape = SDS(((n_lead, M, n_out) if out_lead is not None else (M, n_out)), out_dtype)
    scratch = [pltpu.VMEM((tm, tn), F32)] if (nk > 1 and not acc_in_out) else []
    return pl.pallas_call(
        body, grid=(M // tm, N // tn, nk), in_specs=in_specs, out_specs=o_spec, out_shape=out_shape,
        scratch_shapes=scratch, input_output_aliases=aliases, name=name,
        compiler_params=_cp("parallel", "parallel", "arbitrary"))(*operands)


def _rms_fwd(x, g, *, tr, name):
    T, D = x.shape

    def body(x_ref, g_ref, o_ref):
        xv = x_ref[...]
        r = lax.rsqrt(jnp.mean(xv * xv, axis=-1, keepdims=True) + EPS)
        o_ref[...] = (xv * r * g_ref[...]).astype(o_ref.dtype)

    return pl.pallas_call(
        body, grid=(T // tr,),
        in_specs=[pl.BlockSpec((tr, D), lambda i: (i, 0)), _full((1, D))],
        out_specs=pl.BlockSpec((tr, D), lambda i: (i, 0)),
        out_shape=SDS((T, D), MXU_DTYPE), name=name, compiler_params=_cp("parallel"))(x, g)


def _rms_bwd(x, g, dh, dres, *, tr, name):
    T, D = x.shape

    def body(x_ref, g_ref, dh_ref, dres_ref, dx_ref, dxb_ref, dg_ref):
        i = pl.program_id(0)
        xv = x_ref[...]
        r = lax.rsqrt(jnp.mean(xv * xv, axis=-1, keepdims=True) + EPS)
        xh = xv * r
        dh = dh_ref[...]
        gy = dh * g_ref[...]
        dx = r * (gy - xh * jnp.mean(xh * gy, axis=-1, keepdims=True)) + dres_ref[...]
        dx_ref[...] = dx
        dxb_ref[...] = dx.astype(dxb_ref.dtype)

        @pl.when(i == 0)
        def _():
            dg_ref[...] = jnp.zeros_like(dg_ref)
        dg_ref[...] += jnp.sum(dh * xh, axis=0, keepdims=True)

    rows = pl.BlockSpec((tr, D), lambda i: (i, 0))
    return pl.pallas_call(
        body, grid=(T // tr,), in_specs=[rows, _full((1, D)), rows, rows],
        out_specs=[rows, rows, _full((1, D))],
        out_shape=[SDS((T, D), F32), SDS((T, D), MXU_DTYPE), SDS((1, D), F32)],
        name=name, compiler_params=_cp("arbitrary"))(x, g, dh, dres)


def _loss_head(y, target, *, tr, name):
    T, D = y.shape

    def body(y_ref, t_ref, loss_ref, dy_ref, dyb_ref):
        i = pl.program_id(0)
        d = y_ref[...] - t_ref[...]
        dy = d * (1.0 / D)
        dy_ref[...] = dy
        dyb_ref[...] = dy.astype(dyb_ref.dtype)
        part = jnp.sum(jnp.sum(d * d, axis=1, keepdims=True), axis=0, keepdims=True) * (0.5 / D)

        @pl.when(i == 0)
        def _():
            loss_ref[...] = jnp.zeros_like(loss_ref)
        loss_ref[...] += jnp.broadcast_to(part, loss_ref.shape)

    rows = pl.BlockSpec((tr, D), lambda i: (i, 0))
    return pl.pallas_call(
        body, grid=(T // tr,), in_specs=[rows, rows],
        out_specs=[_full((1, 128)), rows, rows],
        out_shape=[SDS((1, 128), F32), SDS((T, D), F32), SDS((T, D), MXU_DTYPE)],
        name=name, compiler_params=_cp("arbitrary"))(y, target)


def _pool_lane_consts(shape):
    lane = _lane(shape)
    grp = lane // (POOL_WIDTH // 4)
    win = jnp.where(grp == 0, 2, jnp.where(grp == 1, 4, jnp.where(grp == 2, 8, 16)))
    return grp, win


def _pool_select(grp, s2, s4, s8, s16):
    return jnp.where(grp == 0, s2, jnp.where(grp == 1, s4, jnp.where(grp == 2, s8, s16)))


def _pool_diff(xe, row0, tr):
    s2 = xe + pltpu.roll(xe, 1, 0)
    s4 = s2 + pltpu.roll(s2, 2, 0)
    s8 = s4 + pltpu.roll(s4, 4, 0)
    s16 = s8 + pltpu.roll(s8, 8, 0)
    shape = (tr, POOL_WIDTH)
    grp, win = _pool_lane_consts(shape)
    sums = _pool_select(grp, s2[16:], s4[16:], s8[16:], s16[16:])
    t = row0 + _row(shape)
    cnt = jnp.minimum(t + 1, win).astype(F32)
    return sums / cnt - xe[16:]


def _pool_fwd(z, wbd, scale, *, tr, name):
    T = z.shape[0]
    hb = tr // 16

    def body(x_ref, xp_ref, w_ref, s_ref, o_ref):
        i = pl.program_id(0)
        halo = jnp.where(i == 0, 0.0, xp_ref[...])
        diff = _pool_diff(jnp.concatenate([halo, x_ref[...]], axis=0), i * tr, tr)
        o_ref[...] = (_dot(diff, w_ref[...]) * s_ref[...]).astype(o_ref.dtype)

    return pl.pallas_call(
        body, grid=(T // tr,),
        in_specs=[pl.BlockSpec((tr, POOL_WIDTH), lambda i: (i, 0)),
                  pl.BlockSpec((16, POOL_WIDTH), lambda i: (jnp.maximum(i * hb - 1, 0), 0)),
                  _full((POOL_WIDTH, POOL_WIDTH)), _full((1, POOL_WIDTH))],
        out_specs=pl.BlockSpec((tr, POOL_WIDTH), lambda i: (i, 0)),
        out_shape=SDS((T, POOL_WIDTH), MXU_DTYPE), name=name, compiler_params=_cp("parallel"))(z, z, wbd, scale)


def _pool_bwd_tile(i, n, tr, x, xprev, dpa, dpa_next, wbd, scale):
    halo = jnp.where(i == 0, 0.0, xprev)
    diff = _pool_diff(jnp.concatenate([halo, x], axis=0), i * tr, tr)
    mixed = _dot(diff, wbd)
    dscale = jnp.sum(dpa * mixed, axis=0, keepdims=True)
    dnext = jnp.where(i == n - 1, 0.0, dpa_next)
    dmix_e = jnp.concatenate([dpa, dnext], axis=0) * scale
    ddiff_e = _dot(dmix_e, wbd, NT)
    dwbd = _dot(diff, dmix_e[:tr], TN)
    shape = (tr + 16, POOL_WIDTH)
    grp, win = _pool_lane_consts(shape)
    t = i * tr + _row(shape)
    e = ddiff_e / jnp.minimum(t + 1, win).astype(F32)
    nrow = tr + 16
    a2 = e + pltpu.roll(e, nrow - 1, 0)
    a4 = a2 + pltpu.roll(a2, nrow - 2, 0)
    a8 = a4 + pltpu.roll(a4, nrow - 4, 0)
    a16 = a8 + pltpu.roll(a8, nrow - 8, 0)
    dx = _pool_select(grp, a2, a4, a8, a16)[:tr] - ddiff_e[:tr]
    return dx, dwbd, dscale


def _norm_rope(x, g, cos, sin_signed, seg):
    reps = x.shape[1] // 128
    ms = _split_dot(x * x, seg) * (1.0 / HEAD_DIM)
    r = lax.rsqrt(ms + EPS)
    xn = x * r * g
    c, s = _tile_lanes(cos, reps), _tile_lanes(sin_signed, reps)
    return xn * c + _swap_halves(xn) * s


def _norm_rope_bwd(x, g, cos, sin_signed, seg, dout):
    reps = x.shape[1] // 128
    c, s = _tile_lanes(cos, reps), _tile_lanes(sin_signed, reps)
    dxn = dout * c + _swap_halves(dout * s)
    ms = _split_dot(x * x, seg) * (1.0 / HEAD_DIM)
    r = lax.rsqrt(ms + EPS)
    xh = x * r
    gy = dxn * g
    dx = r * (gy - xh * (_split_dot(xh * gy, seg) * (1.0 / HEAD_DIM)))
    dg = jnp.sum(dxn * xh, axis=0, keepdims=True)
    return dx, dg


def _dup_heads(k):
    first = _lane(k.shape) < HEAD_DIM
    kr = pltpu.roll(k, HEAD_DIM, 1)
    return jnp.concatenate([jnp.where(first, k, kr), jnp.where(first, kr, k)], axis=1)


def _qkv_prep(z, cos, sin_signed, gq, gk, seg, *, tr, name):
    T = z.shape[0]

    def body(qa_ref, qb_ref, kv_ref, c_ref, s_ref, gq_ref, gk_ref, seg_ref, q_ref, k_ref, v_ref):
        c, s, seg_m = c_ref[...], s_ref[...], seg_ref[...]
        scale = HEAD_DIM ** -0.5
        qa = _norm_rope(qa_ref[...], gq_ref[...], c, s, seg_m) * scale
        qb = _norm_rope(qb_ref[...], gq_ref[...], c, s, seg_m) * scale
        q_ref[...] = jnp.concatenate([qa, qb], axis=1).astype(q_ref.dtype)
        kv = kv_ref[...]
        k = _norm_rope(kv[:, :KV_WIDTH], gk_ref[...], c, s, seg_m[:128, :128])
        k_ref[...] = _dup_heads(k).astype(k_ref.dtype)
        v_ref[...] = _dup_heads(kv[:, KV_WIDTH:]).astype(v_ref.dtype)

    col = lambda j: pl.BlockSpec((tr, 256), lambda i: (i, j))
    tab = pl.BlockSpec((tr, 128), lambda i: (i, 0))
    return pl.pallas_call(
        body, grid=(T // tr,),
        in_specs=[col(1), col(2), col(3), tab, tab, _full((1, 256)), _full((1, 128)), _full((256, 256))],
        out_specs=[pl.BlockSpec((tr, 512), lambda i: (i, 0)), col(0), col(0)],
        out_shape=[SDS((T, 512), MXU_DTYPE), SDS((T, 256), MXU_DTYPE), SDS((T, 256), MXU_DTYPE)],
        name=name, compiler_params=_cp("parallel"))(z, z, z, cos, sin_signed, gq, gk, seg)


def _attn_mask(n):
    qi = _row((ATTN_BLOCK, 2 * ATTN_BLOCK))
    kj = _lane((ATTN_BLOCK, 2 * ATTN_BLOCK))
    return (kj > qi) & (kj <= qi + ATTN_BLOCK) & ((kj >= ATTN_BLOCK) | (n > 0))


def _attn_probs(q128, head_odd, k2, mask, sink):
    in_head = (_lane(q128.shape) >= HEAD_DIM) == head_odd
    qm = jnp.where(in_head, q128, jnp.zeros_like(q128))
    s = _dot(qm, k2, NT)
    s = jnp.where(mask, s, NEG_BIG)
    m = jnp.maximum(jnp.max(s, axis=1, keepdims=True), sink)
    p = jnp.exp(s - m)
    ps = jnp.exp(sink - m)
    inv = 1.0 / (jnp.sum(p, axis=1, keepdims=True) + ps)
    return qm, in_head, p * inv, ps * inv


def _attn_fwd(q, k, v, sinks_b, *, name):
    T = q.shape[0]
    nb = T // ATTN_BLOCK

    def body(q_ref, kc_ref, kp_ref, vc_ref, vp_ref, sk_ref, o_ref):
        n = pl.program_id(0)
        mask = _attn_mask(n)
        k2 = jnp.concatenate([kp_ref[...], kc_ref[...]], axis=0)
        v2 = jnp.concatenate([vp_ref[...], vc_ref[...]], axis=0)
        qv = q_ref[...]
        outs = []
        for pair in range(N_Q_HEADS // 2):
            g = pair // 2
            kg = k2[:, 128 * g:128 * (g + 1)]
            vg = v2[:, 128 * g:128 * (g + 1)]
            q128 = qv[:, 128 * pair:128 * (pair + 1)]
            o_pair = None
            for odd in (False, True):
                h = 2 * pair + int(odd)
                _, in_head, pn, _ = _attn_probs(q128, odd, kg, mask, sk_ref[h:h + 1, 0:1])
                o2 = _dot(pn, vg)
                o_pair = o2 if o_pair is None else jnp.where(in_head, o2, o_pair)
            outs.append(o_pair)
        o_ref[...] = jnp.concatenate(outs, axis=1).astype(o_ref.dtype)

    cur = lambda w: pl.BlockSpec((ATTN_BLOCK, w), lambda n: (n, 0))
    prev = lambda w: pl.BlockSpec((ATTN_BLOCK, w), lambda n: (jnp.maximum(n - 1, 0), 0))
    return pl.pallas_call(
        body, grid=(nb,),
        in_specs=[cur(512), cur(256), prev(256), cur(256), prev(256), _full((8, 128))],
        out_specs=cur(512), out_shape=SDS((T, 512), MXU_DTYPE), name=name,
        compiler_params=_cp("parallel"))(q, k, k, v, v, sinks_b)


def _attn_bwd(q, k, v, sinks_b, do, *, name):
    T = q.shape[0]
    nb = T // ATTN_BLOCK

    def body(q_ref, kc_ref, kp_ref, vc_ref, vp_ref, sk_ref, do_ref,
             dq_ref, dkc_ref, dkp_ref, dvc_ref, dvp_ref, dsk_ref):
        n = pl.program_id(0)
        mask = _attn_mask(n)
        k2 = jnp.concatenate([kp_ref[...], kc_ref[...]], axis=0)
        v2 = jnp.concatenate([vp_ref[...], vc_ref[...]], axis=0)
        qv = q_ref[...]
        dov = do_ref[...]

        @pl.when(n == 0)
        def _():
            dsk_ref[...] = jnp.zeros_like(dsk_ref)

        dqs, dks, dvs = [], [], []
        for g in range(2):
            kg = k2[:, 128 * g:128 * (g + 1)]
            vg = v2[:, 128 * g:128 * (g + 1)]
            dk_g = jnp.zeros((2 * ATTN_BLOCK, 128), F32)
            dv_g = jnp.zeros((2 * ATTN_BLOCK, 128), F32)
            for pair in (2 * g, 2 * g + 1):
                q128 = qv[:, 128 * pair:128 * (pair + 1)]
                do128 = dov[:, 128 * pair:128 * (pair + 1)]
                dq_pair = None
                for odd in (False, True):
                    h = 2 * pair + int(odd)
                    qm, in_head, pn, psn = _attn_probs(q128, odd, kg, mask, sk_ref[h:h + 1, 0:1])
                    dom = jnp.where(in_head, do128, 0.0)
                    o2 = _dot(pn, vg)
                    delta = jnp.sum(dom * o2, axis=1, keepdims=True)
                    dp = _dot(dom, vg, NT)
                    ds = pn * (dp - delta)
                    dq2 = _dot(ds, kg)
                    dq_pair = dq2 if dq_pair is None else jnp.where(in_head, dq2, dq_pair)
                    dk_g = dk_g + _dot(ds, qm, TN)
                    dv_g = dv_g + _dot(pn, dom, TN)
                    dsink = -jnp.sum(psn * delta, axis=0, keepdims=True)
                    dsk_ref[h:h + 1, :] += jnp.broadcast_to(dsink, (1, 128))
                dqs.append(dq_pair)
            dks.append(dk_g)
            dvs.append(dv_g)
        dq_ref[...] = jnp.concatenate(dqs, axis=1)
        dk = jnp.concatenate(dks, axis=1)
        dv = jnp.concatenate(dvs, axis=1)
        dkp_ref[...] = dk[:ATTN_BLOCK]
        dkc_ref[...] = dk[ATTN_BLOCK:]
        dvp_ref[...] = dv[:ATTN_BLOCK]
        dvc_ref[...] = dv[ATTN_BLOCK:]

    cur = lambda w: pl.BlockSpec((ATTN_BLOCK, w), lambda n: (n, 0))
    prev = lambda w: pl.BlockSpec((ATTN_BLOCK, w), lambda n: (jnp.maximum(n - 1, 0), 0))
    f = lambda w: SDS((T, w), F32)
    return pl.pallas_call(
        body, grid=(nb,),
        in_specs=[cur(512), cur(256), prev(256), cur(256), prev(256), _full((8, 128)), cur(512)],
        out_specs=[cur(512), cur(256), cur(256), cur(256), cur(256), _full((8, 128))],
        out_shape=[f(512), f(256), f(256), f(256), f(256), SDS((8, 128), F32)],
        name=name, compiler_params=_cp("arbitrary"))(q, k, k, v, v, sinks_b, do)


def _mixer_ab_bwd(z, cos, sin_signed, gq, gk, seg, dq, dkc, dkp, dvc, dvp, dpa, wbd, scale, dz, *, tr, name):
    T = z.shape[0]
    n = T // tr
    hb = tr // 16
    ab = tr // ATTN_BLOCK

    def unfold(cur, nxt_tile, nxt_halo, i):
        nxt = jnp.concatenate([nxt_tile[ATTN_BLOCK:], jnp.where(i == n - 1, 0.0, nxt_halo)], axis=0)
        tot = cur + nxt
        first = _lane((tr, 128)) < HEAD_DIM
        a = tot[:, :128]
        b = tot[:, 128:]
        a = a + pltpu.roll(a, HEAD_DIM, 1)
        b = b + pltpu.roll(b, HEAD_DIM, 1)
        return jnp.where(first, a, b)

    def body(xp_ref, xpp_ref, qa_ref, qb_ref, kv_ref, c_ref, s_ref, gq_ref, gk_ref, seg_ref,
             dq_ref, dkc_ref, dkp_ref, dkh_ref, dvc_ref, dvp_ref, dvh_ref, dpa_ref, dpan_ref, w_ref, sc_ref, _dz_in,
             dz_ref, dgq_ref, dgk_ref, dw_ref, dsc_ref):
        i = pl.program_id(0)
        c, s, seg_m = c_ref[...], s_ref[...], seg_ref[...]
        scale_q = HEAD_DIM ** -0.5
        dqv = dq_ref[...] * scale_q
        dxa, dga = _norm_rope_bwd(qa_ref[...], gq_ref[...], c, s, seg_m, dqv[:, :256])
        dxb, dgb = _norm_rope_bwd(qb_ref[...], gq_ref[...], c, s, seg_m, dqv[:, 256:])
        dk = unfold(dkc_ref[...], dkp_ref[...], dkh_ref[...], i)
        dv = unfold(dvc_ref[...], dvp_ref[...], dvh_ref[...], i)
        kv = kv_ref[...]
        dxk, dgk = _norm_rope_bwd(kv[:, :KV_WIDTH], gk_ref[...], c, s, seg_m[:128, :128], dk)
        dxp, dwbd, dscale = _pool_bwd_tile(i, n, tr, xp_ref[...], xpp_ref[...], dpa_ref[...], dpan_ref[...],
                                           w_ref[...], sc_ref[...])
        dz_ref[...] = jnp.concatenate([dxp, dxa, dxb, dxk, dv], axis=1).astype(dz_ref.dtype)

        @pl.when(i == 0)
        def _():
            dgq_ref[...] = jnp.zeros_like(dgq_ref)
            dgk_ref[...] = jnp.zeros_like(dgk_ref)
            dw_ref[...] = jnp.zeros_like(dw_ref)
            dsc_ref[...] = jnp.zeros_like(dsc_ref)
        dgq_ref[...] += _fold_lanes(dga + dgb, HEAD_DIM)
        dgk_ref[...] += _fold_lanes(dgk, HEAD_DIM)
        dw_ref[...] += dwbd
        dsc_ref[...] += dscale

    col = lambda j: pl.BlockSpec((tr, 256), lambda i: (i, j))
    rows = lambda w: pl.BlockSpec((tr, w), lambda i: (i, 0))
    nxt_blk = pl.BlockSpec((ATTN_BLOCK, 256), lambda i: (jnp.minimum((i + 1) * ab, T // ATTN_BLOCK - 1), 0))
    prev16 = pl.BlockSpec((16, 256), lambda i: (jnp.maximum(i * hb - 1, 0), 0))
    next16 = pl.BlockSpec((16, 256), lambda i: (jnp.minimum((i + 1) * hb, T // 16 - 1), 0))
    return pl.pallas_call(
        body, grid=(n,),
        in_specs=[col(0), prev16, col(1), col(2), col(3), rows(128), rows(128),
                  _full((1, 256)), _full((1, 128)), _full((256, 256)),
                  rows(512), rows(256), rows(256), nxt_blk, rows(256), rows(256), nxt_blk,
                  rows(256), next16, _full((256, 256)), _full((1, 256)), ANY],
        out_specs=[rows(1024), _full((1, 256)), _full((1, 128)), _full((256, 256)), _full((1, 256))],
        out_shape=[SDS((T, IN_COLS), MXU_DTYPE), SDS((1, 256), F32), SDS((1, 128), F32),
                   SDS((256, 256), F32), SDS((1, 256), F32)],
        input_output_aliases={21: 0}, name=name, compiler_params=_cp("arbitrary"))(
            z, z, z, z, z, cos, sin_signed, gq, gk, seg, dq, dkc, dkp, dkp, dvc, dvp, dvp, dpa, dpa, wbd, scale, dz)


def _sgu_common(zu, zv, vn, seg):
    u, du = _gelu_and_grad(zu)
    gv, dgv = _gelu_and_grad(zv)
    ms = _split_dot(gv * gv, seg) * (1.0 / HEAD_DIM)
    r = lax.rsqrt(ms + EPS)
    xh = gv * r
    return u, du, dgv, r, xh, xh * vn


def _sgu_fwd(z, wtril, bexp, vn, seg, *, tr, name):
    T = z.shape[0]
    nch = tr // CHUNK

    def body(u_ref, v_ref, w_ref, b_ref, vn_ref, seg_ref, o_ref):
        u, _, _, _, _, vg = _sgu_common(u_ref[...], v_ref[...], vn_ref[...], seg_ref[...])
        grp = _lane((CHUNK, SGU_WIDTH)) // HEAD_DIM
        outs = []
        for ch in range(nch):
            vc = vg[ch * CHUNK:(ch + 1) * CHUNK]
            s = b_ref[...]
            for g in range(4):
                s = s + jnp.where(grp == g, _dot(w_ref[g], vc), 0.0)
            outs.append(u[ch * CHUNK:(ch + 1) * CHUNK] * s)
        o_ref[...] = jnp.concatenate(outs, axis=0).astype(o_ref.dtype)

    col = lambda j: pl.BlockSpec((tr, 256), lambda i: (i, j))
    return pl.pallas_call(
        body, grid=(T // tr,),
        in_specs=[col(4), col(5), _full((4, CHUNK, CHUNK)), _full((CHUNK, 256)), _full((1, 256)), _full((256, 256))],
        out_specs=col(0), out_shape=SDS((T, SGU_WIDTH), MXU_DTYPE), name=name,
        compiler_params=_cp("parallel"))(z, z, wtril, bexp, vn, seg)


def _sgu_bwd(z, wtril, bexp, vn, seg, dsg, dz, *, tr, name):
    T = z.shape[0]
    nch = tr // CHUNK

    def body(u_ref, v_ref, w_ref, b_ref, vn_ref, seg_ref, d_ref, _dz_in, dz_ref, dw_ref, db_ref, dvn_ref):
        i = pl.program_id(0)
        seg_m = seg_ref[...]
        vn_v = vn_ref[...]
        u, du, dgv, r, xh, vg = _sgu_common(u_ref[...], v_ref[...], vn_v, seg_m)
        d = d_ref[...]
        grp = _lane((CHUNK, SGU_WIDTH)) // HEAD_DIM
        tril = _row((CHUNK, CHUNK)) >= _lane((CHUNK, CHUNK))

        @pl.when(i == 0)
        def _():
            dw_ref[...] = jnp.zeros_like(dw_ref)
            db_ref[...] = jnp.zeros_like(db_ref)
            dvn_ref[...] = jnp.zeros_like(dvn_ref)

        dus, dvgs = [], []
        for ch in range(nch):
            sl = slice(ch * CHUNK, (ch + 1) * CHUNK)
            vc = vg[sl]
            s = b_ref[...]
            for g in range(4):
                s = s + jnp.where(grp == g, _dot(w_ref[g], vc), 0.0)
            dus.append(d[sl] * s)
            ds = d[sl] * u[sl]
            db_ref[...] += _split_dot(ds, seg_m)
            dvg = jnp.zeros((CHUNK, SGU_WIDTH), F32)
            for g in range(4):
                dsm = jnp.where(grp == g, ds, 0.0)
                dvg = dvg + jnp.where(grp == g, _dot(w_ref[g], ds, TN), 0.0)
                dw_ref[g] += jnp.where(tril, _dot(dsm, vc, NT), 0.0)
            dvgs.append(dvg)
        dup = jnp.concatenate(dus, axis=0)
        dvg = jnp.concatenate(dvgs, axis=0)
        dvn_ref[...] += _fold_lanes(jnp.sum(dvg * xh, axis=0, keepdims=True), HEAD_DIM)
        gy = dvg * vn_v
        dgvv = r * (gy - xh * (_split_dot(xh * gy, seg_m) * (1.0 / HEAD_DIM)))
        dz_ref[...] = jnp.concatenate([dup * du, dgvv * dgv], axis=1).astype(dz_ref.dtype)

    col = lambda j: pl.BlockSpec((tr, 256), lambda i: (i, j))
    return pl.pallas_call(
        body, grid=(T // tr,),
        in_specs=[col(4), col(5), _full((4, CHUNK, CHUNK)), _full((CHUNK, 256)), _full((1, 256)), _full((256, 256)),
                  col(0), ANY],
        out_specs=[pl.BlockSpec((tr, 512), lambda i: (i, 2)), _full((4, CHUNK, CHUNK)), _full((CHUNK, 256)),
                   _full((1, 256))],
        out_shape=[SDS((T, IN_COLS), MXU_DTYPE), SDS((4, CHUNK, CHUNK), F32), SDS((CHUNK, 256), F32),
                   SDS((1, 256), F32)],
        input_output_aliases={7: 0}, name=name, compiler_params=_cp("arbitrary"))(
            z, z, wtril, bexp, vn, seg, dsg, dz)


def _merge_fwd(pa, at, sg, wa, wb, wc, z, *, tm, tn, name):
    T = pa.shape[0]
    gb = GATE_COL0 // tn
    nb = D_MODEL // tn

    def body(pa_ref, at_ref, sg_ref, wa_ref, wb_ref, wc_ref, g0_ref, g1_ref, g2_ref, m_ref, y_ref):
        acc = None
        for idx, (op_ref, w_ref, g_ref) in enumerate(((pa_ref, wa_ref, g0_ref), (at_ref, wb_ref, g1_ref),
                                                      (sg_ref, wc_ref, g2_ref))):
            y = _dot(op_ref[...], w_ref[...])
            y_ref[idx] = y
            t = _sigmoid(g_ref[...]) * y
            acc = t if acc is None else acc + t
        m_ref[...] = acc.astype(m_ref.dtype)

    op = lambda w: pl.BlockSpec((tm, w), lambda i, j: (i, 0))
    wt = lambda k: pl.BlockSpec((k, tn), lambda i, j: (0, j))
    gate = lambda b: pl.BlockSpec((tm, tn), lambda i, j: (i, gb + b * nb + j))
    return pl.pallas_call(
        body, grid=(T // tm, nb),
        in_specs=[op(256), op(512), op(256), wt(256), wt(512), wt(256), gate(0), gate(1), gate(2)],
        out_specs=[pl.BlockSpec((tm, tn), lambda i, j: (i, j)), pl.BlockSpec((3, tm, tn), lambda i, j: (0, i, j))],
        out_shape=[SDS((T, D_MODEL), MXU_DTYPE), SDS((3, T, D_MODEL), F32)],
        name=name, compiler_params=_cp("parallel", "parallel"))(pa, at, sg, wa, wb, wc, z, z, z)


def _merge_bwd(dm, y, z, *, tr, tn, name):
    T = dm.shape[0]
    gb = GATE_COL0 // tn
    nb = D_MODEL // tn

    def body(dm_ref, y_ref, g_ref, dy_ref, dz_ref):
        g = _sigmoid(g_ref[...])
        d = dm_ref[...]
        dy_ref[...] = (d * g).astype(dy_ref.dtype)
        dz_ref[...] = (d * y_ref[...] * g * (1.0 - g)).astype(dz_ref.dtype)

    return pl.pallas_call(
        body, grid=(T // tr, 3, nb),
        in_specs=[pl.BlockSpec((tr, tn), lambda i, b, j: (i, j)),
                  pl.BlockSpec((None, tr, tn), lambda i, b, j: (b, i, j)),
                  pl.BlockSpec((tr, tn), lambda i, b, j: (i, gb + b * nb + j))],
        out_specs=[pl.BlockSpec((None, tr, tn), lambda i, b, j: (b, i, j)),
                   pl.BlockSpec((tr, tn), lambda i, b, j: (i, gb + b * nb + j))],
        out_shape=[SDS((3, T, D_MODEL), MXU_DTYPE), SDS((T, IN_COLS), MXU_DTYPE)],
        name=name, compiler_params=_cp("parallel", "parallel", "parallel"))(dm, y, z)


def _conv3(xe, w, b):
    return (w[0:1] * pltpu.roll(xe, 2, 0) + w[1:2] * pltpu.roll(xe, 1, 0) + w[2:3] * xe)[8:] + b


def _conv_act_fwd(up, cw, cb, *, tr, tc, name):
    T = up.shape[0]
    nc = D_FF // tc
    hb = tr // 8

    def body(ug_ref, ugp_ref, uv_ref, uvp_ref, wg_ref, wv_ref, bg_ref, bv_ref, o_ref):
        i = pl.program_id(1)
        first = i == 0
        cg = _conv3(jnp.concatenate([jnp.where(first, 0.0, ugp_ref[...]), ug_ref[...]], axis=0), wg_ref[...], bg_ref[...])
        cv = _conv3(jnp.concatenate([jnp.where(first, 0.0, uvp_ref[...]), uv_ref[...]], axis=0), wv_ref[...], bv_ref[...])
        o_ref[...] = (cg * _sigmoid(cg) * cv).astype(o_ref.dtype)

    tile = lambda off: pl.BlockSpec((tr, tc), lambda j, i: (i, off + j))
    prev = lambda off: pl.BlockSpec((8, tc), lambda j, i: (jnp.maximum(i * hb - 1, 0), off + j))
    par = lambda rows, off: pl.BlockSpec((rows, tc), lambda j, i: (0, off + j))
    return pl.pallas_call(
        body, grid=(nc, T // tr),
        in_specs=[tile(0), prev(0), tile(nc), prev(nc), par(3, 0), par(3, nc), par(1, 0), par(1, nc)],
        out_specs=pl.BlockSpec((tr, tc), lambda j, i: (i, j)),
        out_shape=SDS((T, D_FF), MXU_DTYPE), name=name,
        compiler_params=_cp("parallel", "parallel"))(up, up, up, up, cw, cw, cb, cb)


def _conv_act_bwd(up, cw, cb, dact, *, tr, tc, name):
    T = up.shape[0]
    nc = D_FF // tc
    hb = tr // 8
    nr = T // tr

    def body(ug_ref, ugp_ref, ugn_ref, uv_ref, uvp_ref, uvn_ref, da_ref, dan_ref, wg_ref, wv_ref, bg_ref, bv_ref,
             du_ref, dwg_ref, dwv_ref, dbg_ref, dbv_ref):
        i = pl.program_id(1)
        first, last = i == 0, i == nr - 1
        da = jnp.concatenate([da_ref[...], jnp.where(last, 0.0, dan_ref[...])], axis=0)
        uge = jnp.concatenate([jnp.where(first, 0.0, ugp_ref[...]), ug_ref[...], ugn_ref[...]], axis=0)
        uve = jnp.concatenate([jnp.where(first, 0.0, uvp_ref[...]), uv_ref[...], uvn_ref[...]], axis=0)
        wg, wv = wg_ref[...], wv_ref[...]
        cg = _conv3(uge, wg, bg_ref[...])
        cv = _conv3(uve, wv, bv_ref[...])
        sg = _sigmoid(cg)
        dcg = da * cv * (sg * (1.0 + cg * (1.0 - sg)))
        dcv = da * (cg * sg)
        nrow = tr + 8

        def back(dc, w):
            return (w[2:3] * dc + w[1:2] * pltpu.roll(dc, nrow - 1, 0) + w[0:1] * pltpu.roll(dc, nrow - 2, 0))[:tr]

        du_ref[0] = back(dcg, wg).astype(du_ref.dtype)
        du_ref[1] = back(dcv, wv).astype(du_ref.dtype)

        def wgrad(dc, ue):
            d = dc[:tr]
            rows = [jnp.sum(d * pltpu.roll(ue, 2, 0)[8:8 + tr], axis=0, keepdims=True),
                    jnp.sum(d * pltpu.roll(ue, 1, 0)[8:8 + tr], axis=0, keepdims=True),
                    jnp.sum(d * ue[8:8 + tr], axis=0, keepdims=True)]
            return jnp.concatenate(rows, axis=0), jnp.sum(d, axis=0, keepdims=True)

        dwg, dbg = wgrad(dcg, uge)
        dwv, dbv = wgrad(dcv, uve)

        @pl.when(first)
        def _():
            dwg_ref[...] = jnp.zeros_like(dwg_ref)
            dwv_ref[...] = jnp.zeros_like(dwv_ref)
            dbg_ref[...] = jnp.zeros_like(dbg_ref)
            dbv_ref[...] = jnp.zeros_like(dbv_ref)
        dwg_ref[...] += dwg
        dwv_ref[...] += dwv
        dbg_ref[...] += dbg
        dbv_ref[...] += dbv

    tile = lambda off: pl.BlockSpec((tr, tc), lambda j, i: (i, off + j))
    prev = lambda off: pl.BlockSpec((8, tc), lambda j, i: (jnp.maximum(i * hb - 1, 0), off + j))
    nxt = lambda off: pl.BlockSpec((8, tc), lambda j, i: (jnp.minimum((i + 1) * hb, T // 8 - 1), off + j))
    par = lambda rows, off: pl.BlockSpec((rows, tc), lambda j, i: (0, off + j))
    acc = lambda rows: pl.BlockSpec((rows, tc), lambda j, i: (0, j))
    return pl.pallas_call(
        body, grid=(nc, nr),
        in_specs=[tile(0), prev(0), nxt(0), tile(nc), prev(nc), nxt(nc), tile(0), nxt(0),
                  par(3, 0), par(3, nc), par(1, 0), par(1, nc)],
        out_specs=[pl.BlockSpec((2, tr, tc), lambda j, i: (0, i, j)), acc(3), acc(3), acc(1), acc(1)],
        out_shape=[SDS((2, T, D_FF), MXU_DTYPE), SDS((3, D_FF), F32), SDS((3, D_FF), F32),
                   SDS((1, D_FF), F32), SDS((1, D_FF), F32)],
        name=name, compiler_params=_cp("parallel", "arbitrary"))(
            up, up, up, up, up, up, dact, dact, cw, cw, cb, cb)


def _row_tile(rows, cap):
    t = min(cap, rows)
    t -= t % 8
    while rows % t:
        t -= 8
    return t


def _adamw(w, g, m, v, *, tr, name):
    R, C = w.shape
    assert R % tr == 0, (R, tr)

    def body(w_ref, g_ref, m_ref, v_ref, d_ref, nm_ref, nv_ref):
        gv = g_ref[...]
        mn = ADAM_B1 * m_ref[...] + (1.0 - ADAM_B1) * gv
        vn = ADAM_B2 * v_ref[...] + (1.0 - ADAM_B2) * (gv * gv)
        m_hat = mn / (1.0 - ADAM_B1 ** ADAM_STEP)
        v_hat = vn / (1.0 - ADAM_B2 ** ADAM_STEP)
        d_ref[...] = -ADAM_LR * (m_hat / (jnp.sqrt(v_hat) + ADAM_EPS) + ADAM_WD * w_ref[...])
        nm_ref[...] = mn
        nv_ref[...] = vn

    rows = pl.BlockSpec((tr, C), lambda i: (i, 0))
    return pl.pallas_call(
        body, grid=(R // tr,), in_specs=[rows] * 4, out_specs=[rows] * 3,
        out_shape=[SDS((R, C), F32)] * 3, name=name, compiler_params=_cp("parallel"))(w, g, m, v)


def _sum_slots(r, *, tr, name):
    S, R, C = r.shape
    assert R % tr == 0, (R, tr)

    def body(r_ref, o_ref):
        acc = r_ref[0]
        for s in range(1, S):
            acc = acc + r_ref[s]
        o_ref[...] = acc

    return pl.pallas_call(
        body, grid=(R // tr,), in_specs=[pl.BlockSpec((S, tr, C), lambda i: (0, i, 0))],
        out_specs=pl.BlockSpec((tr, C), lambda i: (i, 0)), out_shape=SDS((R, C), F32),
        name=name, compiler_params=_cp("parallel"))(r)


def _pair_add(g4, h, pos, *, name):
    A, _, r, C = g4.shape
    tr = _row_tile(r, 128)

    def body(pos_ref, g_ref, h_ref, o_ref):
        o_ref[...] = g_ref[...] + h_ref[...]

    grid_spec = pltpu.PrefetchScalarGridSpec(
        num_scalar_prefetch=1, grid=(A, r // tr),
        in_specs=[pl.BlockSpec((None, None, tr, C), lambda a, i, pos: (a, pos[1], i, 0)),
                  pl.BlockSpec((None, tr, C), lambda a, i, pos: (a, i, 0))],
        out_specs=pl.BlockSpec((None, tr, C), lambda a, i, pos: (a, i, 0)))
    return pl.pallas_call(body, grid_spec=grid_spec, out_shape=SDS((A, r, C), F32), name=name,
                          compiler_params=_cp("parallel", "parallel"))(pos, g4, h)


def _chip_sum(p, r2, f_into, pos, layer, *, name):
    A, r, C = p.shape
    cs = r2.shape[2]
    tr = _row_tile(r, 128)
    if A == 1:
        own_spec = pl.BlockSpec((None, tr, cs), lambda i, pos: (0, i, pos[0]))
    else:
        own_spec = pl.BlockSpec((None, tr, cs), lambda i, pos: (pos[0], i, 0))

    def body(pos_ref, own_ref, r_ref, *rest):
        o_ref = rest[-1]
        o_ref[...] = ((own_ref[...] + r_ref[0]) + r_ref[1]) + r_ref[2]

    in_specs = [own_spec, pl.BlockSpec((3, tr, cs), lambda i, pos: (0, i, 0))]
    operands = [pos, p, r2]
    aliases = {}
    if f_into is not None:
        in_specs.append(ANY)
        operands.append(f_into)
        aliases = {3: 0}
    grid_spec = pltpu.PrefetchScalarGridSpec(
        num_scalar_prefetch=1, grid=(r // tr,), in_specs=in_specs,
        out_specs=pl.BlockSpec((None, None, tr, cs), lambda i, pos: (layer, pos[1], i, 0)))
    return pl.pallas_call(body, grid_spec=grid_spec, out_shape=SDS((DEPTH, 2, r, cs), F32), name=name,
                          input_output_aliases=aliases, compiler_params=_cp("parallel"))(*operands)


def _mesh_pos():
    return lax.axis_index("x"), lax.axis_index("y"), lax.axis_index("c")


def _shard_view(ref, axis, s, size):
    idx = [slice(None)] * len(ref.shape)
    idx[axis] = pl.ds(s * size, size)
    return ref.at[tuple(idx)]


HBM = pl.BlockSpec(memory_space=pltpu.HBM)
SEM = pl.BlockSpec(memory_space=pltpu.SEMAPHORE)
DATAFLOW = pltpu.SideEffectType.DATAFLOW_SIDE_EFFECTING
CHIP_FLIPS = (2, 1, 3)


def _chip_peers():
    x, y, c = _mesh_pos()
    return 2 * x + y, [(1 - x, y, c), (x, 1 - y, c), (1 - x, 1 - y, c)], (x, y, 1 - c), c


def _tie(x, *deps):
    return lax.optimization_barrier((x,) + tuple(deps))[0] if deps else x


def _split_start(arrays, n_copies, issue, *, name):
    k = len(arrays)

    def body(*refs):
        issue(refs[:k], refs[k], refs[k + 1])
        refs[2 * k + 2][...] = jnp.zeros((8, 128), F32)

    out = pl.pallas_call(
        body, name=name,
        out_shape=(pltpu.SemaphoreType.DMA((n_copies,)), pltpu.SemaphoreType.DMA((n_copies,)),
                   *[pltpu.HBM(a.shape, a.dtype) for a in arrays], SDS((8, 128), F32)),
        in_specs=[HBM] * k, out_specs=(SEM, SEM, *[HBM] * k, pl.BlockSpec(memory_space=pltpu.VMEM)),
        input_output_aliases={i: 2 + i for i in range(k)},
        compiler_params=pltpu.CompilerParams(has_side_effects=DATAFLOW))(
            *[pltpu.with_memory_space_constraint(a, pltpu.HBM) for a in arrays])
    return (out[0], out[1]), list(out[2:2 + k]), out[2 + k]


def _split_wait(sems, arrays, after, waits, *, name):
    k = len(arrays)

    def body(*refs):
        waits(refs[:k], refs[k], refs[k + 1])

    out = pl.pallas_call(
        body, name=name, out_shape=tuple(pltpu.HBM(a.shape, a.dtype) for a in arrays),
        in_specs=[HBM] * k + [SEM, SEM, ANY], out_specs=tuple([HBM] * k),
        input_output_aliases={i: i for i in range(k)},
        compiler_params=pltpu.CompilerParams(has_side_effects=DATAFLOW))(*arrays, sems[0], sems[1], after)
    return list(out)


def _wait_both(cp):
    cp.wait_send()
    cp.wait_recv()


def _place_own(shards, axes, *, name):
    n = len(shards)
    out_shapes = []
    for a, ax in zip(shards, axes):
        shp = list(a.shape)
        shp[ax] *= N_CHIPS
        out_shapes.append(SDS(tuple(shp), a.dtype))

    def body(*refs):
        ins, outs, sems = refs[:n], refs[n:2 * n], refs[2 * n]
        x, y, _ = _mesh_pos()
        me = 2 * x + y

        def local(w, s):
            return pltpu.make_async_copy(ins[w], _shard_view(outs[w], axes[w], s, ins[w].shape[axes[w]]), sems.at[w])

        for s in range(N_CHIPS):
            @pl.when(me == s)
            def _():
                for w in range(n):
                    local(w, s).start()
        for w in range(n):
            local(w, 0).wait()

    return pl.pallas_call(
        body, in_specs=[ANY] * n, out_specs=[ANY] * n, out_shape=out_shapes,
        scratch_shapes=[pltpu.SemaphoreType.DMA((n,))], name=name,
        compiler_params=pltpu.CompilerParams(has_side_effects=True))(*shards)


class _Gather:
    def __init__(self, shards, lands, axes, name):
        n = len(shards)
        self.n, self.axes, self.name = n, axes, name
        sizes = [a.shape[ax] for a, ax in zip(shards, axes)]

        def copy(refs, ss, rs, w, p, s, peers):
            return pltpu.make_async_remote_copy(
                src_ref=refs[w], dst_ref=_shard_view(refs[n + w], axes[w], s, sizes[w]),
                send_sem=ss.at[3 * w + p], recv_sem=rs.at[3 * w + p], device_id=peers[p], device_id_type=MESH)

        def issue(refs, ss, rs):
            me, peers, _, _ = _chip_peers()
            for s in range(N_CHIPS):
                @pl.when(me == s)
                def _():
                    for w in range(n):
                        for p in range(3):
                            copy(refs, ss, rs, w, p, s, peers).start()

        def waits(refs, ss, rs):
            _, peers, _, _ = _chip_peers()
            for w in range(n):
                for p in range(3):
                    _wait_both(copy(refs, ss, rs, w, p, 0, peers))

        self._waits = waits
        self.sems, self.arrays, self.token = _split_start(list(shards) + list(lands), 3 * n, issue, name=name + "_start")

    def wait(self, after):
        out = _split_wait(self.sems, self.arrays, after, self._waits, name=self.name + "_wait")
        return out[self.n:]


def _swap_halves_start(g4s, *, name):
    n = len(g4s)
    lands = [lax.empty((g.shape[0],) + g.shape[2:], g.dtype) for g in g4s]

    def copies(refs, ss, rs):
        _, _, sibling, c = _chip_peers()
        return [pltpu.make_async_remote_copy(
            src_ref=refs[w].at[:, 1 - c], dst_ref=refs[n + w], send_sem=ss.at[w], recv_sem=rs.at[w],
            device_id=sibling, device_id_type=MESH) for w in range(n)]

    def issue(refs, ss, rs):
        for cp in copies(refs, ss, rs):
            cp.start()

    def waits(refs, ss, rs):
        for cp in copies(refs, ss, rs):
            _wait_both(cp)

    sems, arrays, token = _split_start(list(g4s) + lands, n, issue, name=name + "_start")
    return sems, arrays, token, waits


def _scatter_start(parts, col_counts, *, name):
    n = len(parts)
    lands = [lax.empty((3, p.shape[1], cs), p.dtype) for p, cs in zip(parts, col_counts)]

    def piece(ref, w, t):
        if parts[w].shape[0] == 1:
            return ref.at[0, :, pl.ds(t * col_counts[w], col_counts[w])]
        return ref.at[t]

    def copy(refs, ss, rs, w, p, s, peers):
        return pltpu.make_async_remote_copy(
            src_ref=piece(refs[w], w, s ^ CHIP_FLIPS[p]), dst_ref=refs[n + w].at[p],
            send_sem=ss.at[3 * w + p], recv_sem=rs.at[3 * w + p], device_id=peers[p], device_id_type=MESH)

    def issue(refs, ss, rs):
        me, peers, _, _ = _chip_peers()
        for s in range(N_CHIPS):
            @pl.when(me == s)
            def _():
                for w in range(n):
                    for p in range(3):
                        copy(refs, ss, rs, w, p, s, peers).start()

    def waits(refs, ss, rs):
        _, peers, _, _ = _chip_peers()
        for w in range(n):
            for p in range(3):
                _wait_both(copy(refs, ss, rs, w, p, 0, peers))

    sems, arrays, token = _split_start(list(parts) + lands, 3 * n, issue, name=name + "_start")
    return sems, arrays, token, waits


def _pair_share_start(fs, layer, *, name):
    n = len(fs)

    def copies(refs, ss, rs):
        _, _, sibling, c = _chip_peers()
        return [pltpu.make_async_remote_copy(
            src_ref=refs[w].at[layer, c], dst_ref=refs[w].at[layer, c], send_sem=ss.at[w], recv_sem=rs.at[w],
            device_id=sibling, device_id_type=MESH) for w in range(n)]

    def issue(refs, ss, rs):
        for cp in copies(refs, ss, rs):
            cp.start()

    def waits(refs, ss, rs):
        for cp in copies(refs, ss, rs):
            _wait_both(cp)

    sems, arrays, token = _split_start(list(fs), n, issue, name=name + "_start")
    return sems, arrays, token, waits


def _allgather_devices(pack, *, name):
    def body(in_ref, out_ref, send_sems, recv_sems, local_sem):
        x, y, c = _mesh_pos()
        me = 4 * x + 2 * y + c
        remotes = []
        for f in range(1, N_DEV):
            fx, fy, fc = (f >> 2) & 1, (f >> 1) & 1, f & 1
            peer = (x ^ fx, y ^ fy, c ^ fc)
            remotes.append(pltpu.make_async_remote_copy(
                src_ref=in_ref, dst_ref=out_ref.at[me], send_sem=send_sems.at[f - 1], recv_sem=recv_sems.at[f - 1],
                device_id=peer, device_id_type=MESH))
        mine = pltpu.make_async_copy(in_ref, out_ref.at[me], local_sem)
        mine.start()
        for cp in remotes:
            cp.start()
        mine.wait()
        for cp in remotes:
            cp.wait_send()
            cp.wait_recv()

    return pl.pallas_call(
        body, in_specs=[ANY], out_specs=ANY, out_shape=SDS((N_DEV,) + pack.shape, pack.dtype),
        scratch_shapes=[pltpu.SemaphoreType.DMA((N_DEV - 1,)), pltpu.SemaphoreType.DMA((N_DEV - 1,)),
                        pltpu.SemaphoreType.DMA],
        name=name, compiler_params=pltpu.CompilerParams(has_side_effects=True))(pack)


BIG = ('w_in', 'w_proj_a', 'w_proj_b', 'w_proj_c', 'w_out', 'w_up', 'w_down')
BIG_SHARD_AXIS = {'w_in': 2, 'w_proj_a': 2, 'w_proj_b': 2, 'w_proj_c': 2, 'w_out': 1, 'w_up': 2, 'w_down': 1}
SMALL = ('norm1', 'q_norm', 'k_norm', 'sinks', 'w_pool', 'pool_scale', 'sgu_v_norm', 'w_s', 'b_s', 'norm2',
         'conv_b', 'conv_w')
WEIGHTS = ('norm1', 'w_in', 'q_norm', 'k_norm', 'sinks', 'w_pool', 'pool_scale', 'sgu_v_norm', 'w_s', 'b_s',
           'w_proj_a', 'w_proj_b', 'w_proj_c', 'w_out', 'norm2', 'w_up', 'conv_w', 'conv_b', 'w_down')


def _rope_tables(positions):
    inv_freq = ROPE_THETA ** (-jnp.arange(0, HEAD_DIM, 2, dtype=F32) / HEAD_DIM)
    ang = positions.astype(F32)[:, None] * inv_freq
    cos, sin = jnp.cos(ang), jnp.sin(ang)
    c = jnp.concatenate([cos, cos], axis=1)
    s = jnp.concatenate([-sin, sin], axis=1)
    return jnp.concatenate([c, c], axis=1), jnp.concatenate([s, s], axis=1)


def _block_diag4(w):
    out = jnp.zeros((POOL_WIDTH, POOL_WIDTH), w.dtype)
    for g in range(4):
        out = lax.dynamic_update_slice(out, w[g], (g * HEAD_DIM, g * HEAD_DIM))
    return out


def _local_step(x, target, cos, sin, sp, sched):
    T = x.shape[0]
    tm1 = min(1024, T)
    tm = min(512, T)
    tr = min(256, T)
    tkt = min(512, T)
    seg = _seg_matrix(256, HEAD_DIM)
    saved = []
    xl = x
    for l in range(DEPTH):
        p = f"l{l}_"
        c = dict(
            g1=sp['norm1'][l][None], g2=sp['norm2'][l][None],
            wbd=_block_diag4(sp['w_pool'][l]).astype(MXU_DTYPE), scale=sp['pool_scale'][l][None],
            gq=jnp.tile(sp['q_norm'][l], 4)[None], gk=jnp.tile(sp['k_norm'][l], 2)[None],
            sinks=jnp.broadcast_to(sp['sinks'][l][:, None], (N_Q_HEADS, 128)),
            wtril=jnp.tril(sp['w_s'][l]).astype(MXU_DTYPE),
            bexp=jnp.repeat(sp['b_s'][l].T, HEAD_DIM, axis=1), vn=jnp.tile(sp['sgu_v_norm'][l], 4)[None],
            cb=sp['conv_b'][l][None])
        h1 = _rms_fwd(xl, c['g1'], tr=tr, name=p + "rms1")
        c['w_in'] = sched.weight('w_in', l, h1)
        z = _mm(h1, c['w_in'], mode='nn', tm=tm1, tn=768, tk=D_MODEL, name=p + "in_proj")
        pa = _pool_fwd(z, c['wbd'], c['scale'], tr=tr, name=p + "pool")
        q, k, v = _qkv_prep(z, cos, sin, c['gq'], c['gk'], seg, tr=tr, name=p + "qkv_prep")
        at = _attn_fwd(q, k, v, c['sinks'], name=p + "attn")
        sg = _sgu_fwd(z, c['wtril'], c['bexp'], c['vn'], seg, tr=tr, name=p + "sgu")
        for n in ('w_proj_a', 'w_proj_b', 'w_proj_c', 'w_out', 'w_up', 'conv_w', 'w_down'):
            c[n] = sched.weight(n, l, sg)
        merged, y3 = _merge_fwd(pa, at, sg, c['w_proj_a'], c['w_proj_b'], c['w_proj_c'], z,
                                tm=tm, tn=512, name=p + "merge")
        x1 = _mm(merged, c['w_out'], mode='nn', add=xl, tm=tm, tn=D_MODEL, tk=D_MODEL, name=p + "out_proj")
        h2 = _rms_fwd(x1, c['g2'], tr=tr, name=p + "rms2")
        up = _mm(h2, c['w_up'], mode='nn', tm=tm1, tn=512, tk=D_MODEL, name=p + "up_proj")
        act = _conv_act_fwd(up, c['conv_w'], c['cb'], tr=tr, tc=1408, name=p + "conv_act")
        x2 = _mm(act, c['w_down'], mode='nn', add=x1, tm=tm, tn=D_MODEL, tk=1408, name=p + "down_proj")
        saved.append(dict(c, x=xl, h1=h1, z=z, pa=pa, q=q, k=k, v=v, at=at, sg=sg, merged=merged, y3=y3,
                          x1=x1, h2=h2, up=up, act=act))
        xl = x2

    loss_row, dx, dxb = _loss_head(xl, target, tr=tr, name="loss_head")

    gs = {n: [None] * DEPTH for n in SMALL}
    for l in reversed(range(DEPTH)):
        p = f"l{l}_b_"
        s = saved[l]
        gb = {}
        dact = _mm(dxb, s['w_down'], mode='nt', tm=tm, tn=1408, tk=D_MODEL, name=p + "down_dx")
        gb['w_down'] = _mm(s['act'], dxb, mode='tn', tm=1408, tn=D_MODEL, tk=tkt, name=p + "down_dw")
        dact = _tie(dact, *sched.slot(l, 'down', gb['w_down']))
        dup, dwg, dwv, dbg, dbv = _conv_act_bwd(s['up'], s['conv_w'], s['cb'], dact, tr=min(512, T), tc=256,
                                                name=p + "conv_act")
        gs['conv_w'][l] = jnp.concatenate([dwg, dwv], axis=1)
        gs['conv_b'][l] = jnp.concatenate([dbg, dbv], axis=1)[0]
        dup = _tie(dup, *sched.slot(l, 'conv', dup))
        dh2 = None
        for half in range(2):
            dh2 = _mm(dup, s['w_up'], mode='nt', a_lead=half, tm=tm, tn=D_MODEL, tk=1408,
                      b_koff=2 * half, add=dh2, name=p + f"up_dx{half}")
            gb['w_up'] = _mm(s['h2'], dup, mode='tn', b_lead=half, tm=D_MODEL, tn=1408, tk=tkt,
                             out_into=gb.get('w_up'), out_joff=2 * half, out_n=2 * D_FF, name=p + f"up_dw{half}")
        dh2 = _tie(dh2, *sched.slot(l, 'ffn', gb['w_up'], gb))
        dx1, dx1b, dg2 = _rms_bwd(s['x1'], s['g2'], dh2, dx, tr=tr, name=p + "rms2")
        gs['norm2'][l] = dg2[0]
        dmerged = _mm(dx1b, s['w_out'], mode='nt', tm=tm, tn=D_MODEL, tk=D_MODEL, name=p + "out_dx")
        gb['w_out'] = _mm(s['merged'], dx1b, mode='tn', tm=D_MODEL, tn=D_MODEL, tk=tkt, name=p + "out_dw")
        dy3, dz = _merge_bwd(dmerged, s['y3'], s['z'], tr=min(1024, T), tn=512, name=p + "merge")
        dy3 = _tie(dy3, *sched.slot(l, 'mid', dz))
        dbr = []
        for idx, (wn, opn, width) in enumerate((('w_proj_a', 'pa', POOL_WIDTH), ('w_proj_b', 'at', ATTN_WIDTH),
                                                ('w_proj_c', 'sg', SGU_WIDTH))):
            dbr.append(_mm(dy3, s[wn], mode='nt', a_lead=idx, tm=tm, tn=width, tk=D_MODEL,
                           name=p + f"proj{idx}_dx"))
            gb[wn] = _mm(s[opn], dy3, mode='tn', b_lead=idx, tm=width, tn=D_MODEL, tk=tkt,
                         name=p + f"proj{idx}_dw")
        dpa, dat, dsg = dbr
        dq, dkc, dkp, dvc, dvp, dsk = _attn_bwd(s['q'], s['k'], s['v'], s['sinks'], dat, name=p + "attn")
        gs['sinks'][l] = dsk[:, 0]
        dq = _tie(dq, *sched.slot(l, 'attn', dq))
        dz, dgq, dgk, dwbd, dsc = _mixer_ab_bwd(s['z'], cos, sin, s['gq'], s['gk'], seg, dq, dkc, dkp, dvc, dvp,
                                                dpa, s['wbd'], s['scale'], dz, tr=tr, name=p + "qkv_pool")
        gs['q_norm'][l] = dgq[0, :HEAD_DIM]
        gs['k_norm'][l] = dgk[0, :HEAD_DIM]
        gs['w_pool'][l] = jnp.stack([dwbd[g * HEAD_DIM:(g + 1) * HEAD_DIM, g * HEAD_DIM:(g + 1) * HEAD_DIM]
                                     for g in range(4)])
        gs['pool_scale'][l] = dsc[0]
        dz, dws, dbrows, dvn = _sgu_bwd(s['z'], s['wtril'], s['bexp'], s['vn'], seg, dsg, dz, tr=tr, name=p + "sgu")
        gs['w_s'][l] = dws
        gs['b_s'][l] = dbrows[:, ::HEAD_DIM].T
        gs['sgu_v_norm'][l] = dvn[0, :HEAD_DIM]
        dh1 = _mm(dz, s['w_in'], mode='nt', tm=tm, tn=D_MODEL, tk=1536, name=p + "in_dx")
        gb['w_in'] = _mm(s['h1'], dz, mode='tn', tm=D_MODEL, tn=1152, tk=tkt, name=p + "in_dw")
        dh1 = _tie(dh1, *sched.slot(l, 'mix', gb['w_in'], gb))
        dx, dxb, dg1 = _rms_bwd(s['x'], s['g1'], dh1, dx1, tr=tr, name=p + "rms1")
        gs['norm1'][l] = dg1[0]
    gs = {n: jnp.stack(v) for n, v in gs.items()}
    return loss_row, dx, gs


GROUP_F = ('w_down', 'w_up')
GROUP_M = ('w_out', 'w_proj_a', 'w_proj_b', 'w_proj_c', 'w_in')
ROW_SHARDED = ('w_out', 'w_down')

REDUCE_PLAN = {
    (1, 'ffn'): (('S1', 'F', 1),),
    (1, 'mid'): (('W1', 'F', 1),),
    (1, 'mix'): (('S1', 'M', 1),),
    (0, 'down'): (('W1', 'M', 1),),
    (0, 'conv'): (('W2', 'F', 1),),
    (0, 'ffn'): (('S1', 'F', 0), ('W3', 'F', 1)),
    (0, 'mid'): (('W1', 'F', 0),),
    (0, 'attn'): (('W2', 'M', 1),),
    (0, 'mix'): (('S1', 'M', 0), ('W3', 'M', 1)),
}
REDUCE_TAIL_A = (('W1', 'M', 0), ('W2', 'F', 0))
REDUCE_TAIL_B = (('W2', 'M', 0), ('W3', 'F', 0), ('W3', 'M', 0))


class _Comm:
    def __init__(self, w, pos):
        self.pos = pos
        groups = {'a': [('w_in', 0)],
                  'b': [(n, 0) for n in BIG[1:]] + [('conv_w', 0)],
                  'c': [(n, 1) for n in BIG] + [('conv_w', 1)]}
        keys = groups['a'] + groups['b'] + groups['c']
        shards = [w[n][l] if n == 'conv_w' else w[n][l].astype(MXU_DTYPE) for n, l in keys]
        axes = [1 if n == 'conv_w' else BIG_SHARD_AXIS[n] - 1 for n, l in keys]
        lands = _place_own(shards, axes, name="gw_place")
        self.gathers, self.group_of, self.weights = {}, {}, {}
        self.tokens = []
        at = 0
        for g, ks in groups.items():
            sl = slice(at, at + len(ks))
            at += len(ks)
            srcs = list(shards[sl])
            srcs[0] = _tie(srcs[0], *self.tokens[-1:])
            self.gathers[g] = (_Gather(srcs, lands[sl], axes[sl], "gw_" + g), ks)
            self.tokens.append(self.gathers[g][0].token)
            self.group_of.update({k: g for k in ks})
        self.red = {}
        self.final = {}

    def weight(self, name, layer, after):
        if (name, layer) not in self.weights:
            gather, ks = self.gathers[self.group_of[(name, layer)]]
            self.weights.update(zip(ks, gather.wait(after)))
        return self.weights[(name, layer)]

    def slot(self, layer, slot, after, grads=None):
        tokens = []
        for step, grp, lyr in REDUCE_PLAN.get((layer, slot), ()):
            tok = self._step(step, grp, lyr, after, grads)
            if tok is not None:
                tokens.append(tok)
        return tokens

    def tail(self, steps, after):
        return [t for t in (self._step(step, grp, lyr, after, None) for step, grp, lyr in steps) if t is not None]

    def shards(self):
        return {n: f.reshape(DEPTH, 2 * f.shape[2], f.shape[3]) for n, f in self.final.items()}

    def _step(self, step, grp, layer, after, grads):
        names = GROUP_F if grp == 'F' else GROUP_M
        tag = f"{grp.lower()}{layer}"
        st = self.red.setdefault((grp, layer), {})
        n = len(names)
        if step == 'S1':
            g4s = []
            for nm in names:
                g = grads[nm]
                R, C = g.shape
                g4s.append(g.reshape(N_CHIPS, 2, R // (2 * N_CHIPS), C) if nm in ROW_SHARDED
                           else g.reshape(1, 2, R // 2, C))
            st['s1'] = _swap_halves_start(g4s, name="rs1_" + tag)
            return st['s1'][2]
        if step == 'W1':
            sems, arrays, _, waits = st.pop('s1')
            arrays = _split_wait(sems, arrays, after, waits, name=f"rs1_{tag}_wait")
            parts = [_pair_add(arrays[i], arrays[n + i], self.pos, name=f"pair_add_{tag}_{names[i]}")
                     for i in range(n)]
            cols = [p.shape[2] if nm in ROW_SHARDED else p.shape[2] // N_CHIPS for p, nm in zip(parts, names)]
            st['s2'] = _scatter_start(parts, cols, name="rs2_" + tag)
            return st['s2'][2]
        if step == 'W2':
            sems, arrays, _, waits = st.pop('s2')
            arrays = _split_wait(sems, arrays, after, waits, name=f"rs2_{tag}_wait")
            fs = [_chip_sum(arrays[i], arrays[n + i], self.final.get(names[i]), self.pos, layer,
                            name=f"chip_sum_{tag}_{names[i]}") for i in range(n)]
            st['s3'] = _pair_share_start(fs, layer, name="rs3_" + tag)
            return st['s3'][2]
        sems, arrays, _, waits = st.pop('s3')
        self.final.update(zip(names, _split_wait(sems, arrays, after, waits, name=f"rs3_{tag}_wait")))
        return None


def _pack(arrays):
    flat = []
    for a in arrays:
        f = a.reshape(-1).astype(F32)
        flat.append(jnp.pad(f, (0, (-f.shape[0]) % 128)))
    v = jnp.concatenate(flat)
    v = jnp.pad(v, (0, (-v.shape[0]) % 1024))
    return v.reshape(-1, 128)


def _unpack(pack, shapes):
    v = pack.reshape(-1)
    out, off = [], 0
    for shp in shapes:
        nel = int(np.prod(shp))
        out.append(v[off:off + nel].reshape(shp))
        off += nel + (-nel) % 128
    return out


def kernel(x, positions, norm1, w_in, q_norm, k_norm, sinks, w_pool, pool_scale, sgu_v_norm, w_s, b_s, w_proj_a, w_proj_b, w_proj_c, w_out, norm2, w_up, conv_w, conv_b, w_down, loss_target, m_norm1, m_w_in, m_q_norm, m_k_norm, m_sinks, m_w_pool, m_pool_scale, m_sgu_v_norm, m_w_s, m_b_s, m_w_proj_a, m_w_proj_b, m_w_proj_c, m_w_out, m_norm2, m_w_up, m_conv_w, m_conv_b, m_w_down, v_norm1, v_w_in, v_q_norm, v_k_norm, v_sinks, v_w_pool, v_pool_scale, v_sgu_v_norm, v_w_s, v_b_s, v_w_proj_a, v_w_proj_b, v_w_proj_c, v_w_out, v_norm2, v_w_up, v_conv_w, v_conv_b, v_w_down):
    w = dict(norm1=norm1, w_in=w_in, q_norm=q_norm, k_norm=k_norm, sinks=sinks, w_pool=w_pool, pool_scale=pool_scale,
             sgu_v_norm=sgu_v_norm, w_s=w_s, b_s=b_s, w_proj_a=w_proj_a, w_proj_b=w_proj_b, w_proj_c=w_proj_c,
             w_out=w_out, norm2=norm2, w_up=w_up, conv_w=conv_w, conv_b=conv_b, w_down=w_down)
    m = dict(norm1=m_norm1, w_in=m_w_in, q_norm=m_q_norm, k_norm=m_k_norm, sinks=m_sinks, w_pool=m_w_pool,
             pool_scale=m_pool_scale, sgu_v_norm=m_sgu_v_norm, w_s=m_w_s, b_s=m_b_s, w_proj_a=m_w_proj_a,
             w_proj_b=m_w_proj_b, w_proj_c=m_w_proj_c, w_out=m_w_out, norm2=m_norm2, w_up=m_w_up, conv_w=m_conv_w,
             conv_b=m_conv_b, w_down=m_w_down)
    v = dict(norm1=v_norm1, w_in=v_w_in, q_norm=v_q_norm, k_norm=v_k_norm, sinks=v_sinks, w_pool=v_w_pool,
             pool_scale=v_pool_scale, sgu_v_norm=v_sgu_v_norm, w_s=v_w_s, b_s=v_b_s, w_proj_a=v_w_proj_a,
             w_proj_b=v_w_proj_b, w_proj_c=v_w_proj_c, w_out=v_w_out, norm2=v_norm2, w_up=v_w_up, conv_w=v_conv_w,
             conv_b=v_conv_b, w_down=v_w_down)
    chip = 2 * lax.axis_index("x") + lax.axis_index("y")
    core = lax.axis_index("c")

    pos = jnp.stack([chip, core]).astype(jnp.int32)
    comm = _Comm(w, pos)

    cos, sin = _rope_tables(positions[0])
    sp = {n: w[n] for n in SMALL if n != 'conv_w'}
    loss_row, dx, gs = _local_step(_tie(x[0], *comm.tokens), loss_target[0], cos, sin, sp, comm)
    loss = lax.psum(loss_row[0, 0], ("x", "y", "c"))

    small_shapes = [gs[n].shape for n in SMALL]
    small_pack = _tie(_pack([gs[n] for n in SMALL]), *comm.tail(REDUCE_TAIL_A, dx))
    red = _sum_slots(_allgather_devices(small_pack, name="small_gather"), tr=small_pack.shape[0], name="small_sum")
    g_small = dict(zip(SMALL, _unpack(red, small_shapes)))
    comm.tail(REDUCE_TAIL_B, red)
    grads = comm.shards()
    grads.update(g_small)
    shard_cols = conv_w.shape[2]
    grads['conv_w'] = lax.dynamic_slice_in_dim(g_small['conv_w'], chip * shard_cols, shard_cols, axis=2)

    delta, new_m, new_v = {}, {}, {}
    for n in BIG:
        shp = w[n].shape
        two_d = lambda a: a.reshape(shp[0] * shp[1], shp[2])
        d, nm, nv = _adamw(two_d(w[n]), two_d(grads[n]), two_d(m[n]), two_d(v[n]),
                           tr=_row_tile(shp[0] * shp[1], 256), name=f"adamw_{n}")
        delta[n], new_m[n], new_v[n] = d.reshape(shp), nm.reshape(shp), nv.reshape(shp)
    shapes = [w[n].shape for n in SMALL]
    packs = [_pack([src[n] for n in SMALL]) for src in (w, grads, m, v)]
    d, nm, nv = _adamw(*packs, tr=packs[0].shape[0], name="adamw_small")
    for dst, src in ((delta, d), (new_m, nm), (new_v, nv)):
        dst.update(zip(SMALL, _unpack(src, shapes)))

    return (loss, dx[None], *[grads[n] for n in WEIGHTS], *[delta[n] for n in WEIGHTS],
            *[new_m[n] for n in WEIGHTS], *[new_v[n] for n in WEIGHTS])
```

```python
import functools
import math

import numpy as np
import jax
import jax.numpy as jnp
from jax import lax
from jax.experimental import pallas as pl
from jax.experimental.pallas import tpu as pltpu

F32 = jnp.float32
MXU_DTYPE = jnp.bfloat16
COMM_DTYPE = jnp.bfloat16

D_MODEL = 1024
DEPTH = 2
HEAD_DIM = 64
POOL_WINDOWS = (2, 4, 8, 16)
POOL_WIDTH = 256
N_Q_HEADS = 8
ATTN_BLOCK = 128
ATTN_WIDTH = 512
KV_WIDTH = 128
CHUNK = 128
SGU_WIDTH = 256
IN_COLS = 4608
GATE_COL0 = 1536
D_FF = 2816
ROPE_THETA = 10000.0
EPS = 1e-6
ADAM_LR, ADAM_B1, ADAM_B2, ADAM_EPS, ADAM_WD, ADAM_STEP = 0.001, 0.9, 0.999, 1e-08, 0.01, 10

N_CHIPS = 4
N_DEV = 8
VMEM_LIMIT_BYTES = 56 * 1024 * 1024
NEG_BIG = -1e30
MESH = pl.DeviceIdType.MESH
ANY = pl.BlockSpec(memory_space=pl.ANY)

SDS = jax.ShapeDtypeStruct


def _cp(*sem):
    return pltpu.CompilerParams(dimension_semantics=sem, vmem_limit_bytes=VMEM_LIMIT_BYTES)


def _dot(a, b, dims=((1,), (0,))):
    return lax.dot_general(a.astype(MXU_DTYPE), b.astype(MXU_DTYPE), (dims, ((), ())),
                           preferred_element_type=F32)


NT = ((1,), (1,))
TN = ((0,), (0,))


def _split_dot(x, m):
    hi = x.astype(MXU_DTYPE)
    lo = (x - hi.astype(F32)).astype(MXU_DTYPE)
    return _dot(hi, m) + _dot(lo, m)


def _seg_matrix(width, seg):
    idx = np.arange(width) // seg
    return jnp.asarray((idx[:, None] == idx[None, :]).astype(np.float32), dtype=MXU_DTYPE)


def _lane(shape):
    return lax.broadcasted_iota(jnp.int32, shape, len(shape) - 1)


def _row(shape):
    return lax.broadcasted_iota(jnp.int32, shape, 0)


def _full(shape):
    nd = len(shape)
    return pl.BlockSpec(shape, lambda *_: (0,) * nd)


def _gelu(x):
    k = math.sqrt(2.0 / math.pi)
    th = jnp.tanh(k * (x + 0.044715 * (x * x * x)))
    return 0.5 * x * (1.0 + th)


def _gelu_and_grad(x):
    k = math.sqrt(2.0 / math.pi)
    x2 = x * x
    th = jnp.tanh(k * (x + 0.044715 * (x2 * x)))
    g = 0.5 * x * (1.0 + th)
    dg = 0.5 * (1.0 + th) + 0.5 * x * (1.0 - th * th) * (k * (1.0 + 3.0 * 0.044715 * x2))
    return g, dg


def _sigmoid(x):
    return 0.5 * jnp.tanh(0.5 * x) + 0.5


def _swap_halves(x):
    w = x.shape[-1]
    first = (_lane(x.shape) % HEAD_DIM) < (HEAD_DIM // 2)
    return jnp.where(first, pltpu.roll(x, w - HEAD_DIM // 2, 1), pltpu.roll(x, HEAD_DIM // 2, 1))


def _tile_lanes(x, reps):
    return x if reps == 1 else jnp.concatenate([x] * reps, axis=1)


def _fold_lanes(x, period):
    w = x.shape[-1]
    while w > period:
        w //= 2
        x = x + pltpu.roll(x, w, 1)
    return x


def _mm(a, b, *, mode, tm, tn, tk, out_dtype=F32, add=None, name,
        a_lead=None, b_lead=None, b_sharded=False, out_into=None,
        b_koff=0, out_joff=0, out_n=None, deps=()):
    ash = a.shape[1:] if a_lead is not None else a.shape
    bsh = b.shape[1:] if b_lead is not None else b.shape
    if b_sharded:
        bsh = (b.shape[1], N_CHIPS * b.shape[2])
    if mode == 'nn':
        (M, K), (K2, N) = ash, bsh
    elif mode == 'nt':
        (M, K), (N, K2) = ash, bsh
    else:
        (K, M), (K2, N) = ash, bsh
    assert K == K2 or (mode == 'nt' and K2 > K), (ash, bsh, mode)
    assert M % tm == 0 and N % tn == 0 and K % tk == 0, (M, N, K, tm, tn, tk)
    nk = K // tk
    dims = {'nn': ((1,), (0,)), 'nt': NT, 'tn': TN}[mode]

    def lead(spec_shape, imap, lead_idx):
        if lead_idx is None:
            return pl.BlockSpec(spec_shape, imap)
        return pl.BlockSpec((None,) + spec_shape, lambda i, j, k: (lead_idx,) + imap(i, j, k))

    if mode == 'tn':
        a_spec = lead((tk, tm), lambda i, j, k: (k, i), a_lead)
    else:
        a_spec = lead((tm, tk), lambda i, j, k: (i, k), a_lead)
    if b_sharded:
        per = b.shape[2] // (tk if mode == 'nt' else tn)
        assert per * (tk if mode == 'nt' else tn) == b.shape[2] and mode != 'tn'
        if mode == 'nt':
            b_spec = pl.BlockSpec((None, tn, tk), lambda i, j, k: ((k + b_koff) // per, j, (k + b_koff) % per))
        else:
            b_spec = pl.BlockSpec((None, tk, tn), lambda i, j, k: (j // per, k, j % per))
    elif mode == 'nt':
        b_spec = lead((tn, tk), lambda i, j, k: (j, k + b_koff), b_lead)
    else:
        b_spec = lead((tk, tn), lambda i, j, k: (k, j), b_lead)
    o_spec = pl.BlockSpec((tm, tn), lambda i, j, k: (i, j + out_joff))
    n_out = N if out_n is None else out_n
    in_specs = [a_spec, b_spec]
    operands = [a, b]
    if add is not None:
        in_specs.append(pl.BlockSpec((tm, tn), lambda i, j, k: (i, j)))
        operands.append(add)
    aliases = {}
    if out_into is not None:
        in_specs.append(ANY)
        operands.append(out_into)
        aliases = {len(operands) - 1: 0}
    in_specs += [ANY] * len(deps)
    operands += list(deps)
    has_add = add is not None
    acc_in_out = nk > 1 and out_dtype == F32

    def body(*refs):
        a_ref, b_ref = refs[0], refs[1]
        pos = 2
        add_ref = None
        if has_add:
            add_ref = refs[pos]
            pos += 1
        if out_into is not None:
            pos += 1
        pos += len(deps)
        o_ref = refs[pos]
        acc_ref = refs[pos + 1] if (nk > 1 and not acc_in_out) else None
        p = _dot(a_ref[...], b_ref[...], dims)
        if nk == 1:
            if has_add:
                p = p + add_ref[...]
            o_ref[...] = p.astype(o_ref.dtype)
            return
        k = pl.program_id(2)
        tgt = o_ref if acc_in_out else acc_ref

        @pl.when(k == 0)
        def _():
            tgt[...] = p + add_ref[...] if has_add else p

        @pl.when(k > 0)
        def _():
            tgt[...] += p

        if not acc_in_out:
            @pl.when(k == nk - 1)
            def _():
                o_ref[...] = acc_ref[...].astype(o_ref.dtype)

    out_shape = SDS((M, n_out), out_dtype)
    scratch = [pltpu.VMEM((tm, tn), F32)] if (nk > 1 and not acc_in_out) else []
    return pl.pallas_call(
        body, grid=(M // tm, N // tn, nk), in_specs=in_specs, out_specs=o_spec, out_shape=out_shape,
        scratch_shapes=scratch, input_output_aliases=aliases, name=name,
        compiler_params=_cp("parallel", "parallel", "arbitrary"))(*operands)


def _after(body, n_in, deps):
    nd = len(deps)
    if nd == 0:
        return body
    return lambda *refs: body(*refs[:n_in], *refs[n_in + nd:])


def _rms_fwd(x, g, *, tr, name, deps=()):
    T, D = x.shape

    def body(x_ref, g_ref, o_ref):
        xv = x_ref[...]
        r = lax.rsqrt(jnp.mean(xv * xv, axis=-1, keepdims=True) + EPS)
        o_ref[...] = (xv * r * g_ref[...]).astype(o_ref.dtype)

    return pl.pallas_call(
        _after(body, 2, deps), grid=(T // tr,),
        in_specs=[pl.BlockSpec((tr, D), lambda i: (i, 0)), _full((1, D))] + [ANY] * len(deps),
        out_specs=pl.BlockSpec((tr, D), lambda i: (i, 0)),
        out_shape=SDS((T, D), MXU_DTYPE), name=name, compiler_params=_cp("parallel"))(x, g, *deps)


def _rms_bwd(x, g, dh, dres, *, tr, name, deps=()):
    T, D = x.shape

    def body(x_ref, g_ref, dh_ref, dres_ref, dx_ref, dxb_ref, dg_ref):
        i = pl.program_id(0)
        xv = x_ref[...]
        r = lax.rsqrt(jnp.mean(xv * xv, axis=-1, keepdims=True) + EPS)
        xh = xv * r
        dh = dh_ref[...]
        gy = dh * g_ref[...]
        dx = r * (gy - xh * jnp.mean(xh * gy, axis=-1, keepdims=True)) + dres_ref[...]
        dx_ref[...] = dx
        dxb_ref[...] = dx.astype(dxb_ref.dtype)

        @pl.when(i == 0)
        def _():
            dg_ref[...] = jnp.zeros_like(dg_ref)
        dg_ref[...] += jnp.sum(dh * xh, axis=0, keepdims=True)

    rows = pl.BlockSpec((tr, D), lambda i: (i, 0))
    return pl.pallas_call(
        _after(body, 4, deps), grid=(T // tr,), in_specs=[rows, _full((1, D)), rows, rows] + [ANY] * len(deps),
        out_specs=[rows, rows, _full((1, D))],
        out_shape=[SDS((T, D), F32), SDS((T, D), MXU_DTYPE), SDS((1, D), F32)],
        name=name, compiler_params=_cp("arbitrary"))(x, g, dh, dres, *deps)


def _loss_head(y, target, *, tr, name):
    T, D = y.shape

    def body(y_ref, t_ref, loss_ref, dy_ref, dyb_ref):
        i = pl.program_id(0)
        d = y_ref[...] - t_ref[...]
        dy = d * (1.0 / D)
        dy_ref[...] = dy
        dyb_ref[...] = dy.astype(dyb_ref.dtype)
        part = jnp.sum(jnp.sum(d * d, axis=1, keepdims=True), axis=0, keepdims=True) * (0.5 / D)

        @pl.when(i == 0)
        def _():
            loss_ref[...] = jnp.zeros_like(loss_ref)
        loss_ref[...] += jnp.broadcast_to(part, loss_ref.shape)

    rows = pl.BlockSpec((tr, D), lambda i: (i, 0))
    return pl.pallas_call(
        body, grid=(T // tr,), in_specs=[rows, rows],
        out_specs=[_full((1, 128)), rows, rows],
        out_shape=[SDS((1, 128), F32), SDS((T, D), F32), SDS((T, D), MXU_DTYPE)],
        name=name, compiler_params=_cp("arbitrary"))(y, target)


def _pool_lane_consts(shape):
    lane = _lane(shape)
    grp = lane // (POOL_WIDTH // 4)
    win = jnp.where(grp == 0, 2, jnp.where(grp == 1, 4, jnp.where(grp == 2, 8, 16)))
    return grp, win


def _pool_select(grp, s2, s4, s8, s16):
    return jnp.where(grp == 0, s2, jnp.where(grp == 1, s4, jnp.where(grp == 2, s8, s16)))


def _pool_diff(xe, row0, tr):
    s2 = xe + pltpu.roll(xe, 1, 0)
    s4 = s2 + pltpu.roll(s2, 2, 0)
    s8 = s4 + pltpu.roll(s4, 4, 0)
    s16 = s8 + pltpu.roll(s8, 8, 0)
    shape = (tr, POOL_WIDTH)
    grp, win = _pool_lane_consts(shape)
    sums = _pool_select(grp, s2[16:], s4[16:], s8[16:], s16[16:])
    t = row0 + _row(shape)
    cnt = jnp.minimum(t + 1, win).astype(F32)
    return sums / cnt - xe[16:]


def _pool_fwd(z, wbd, scale, *, tr, name):
    T = z.shape[0]
    hb = tr // 16

    def body(x_ref, xp_ref, w_ref, s_ref, o_ref):
        i = pl.program_id(0)
        halo = jnp.where(i == 0, 0.0, xp_ref[...])
        diff = _pool_diff(jnp.concatenate([halo, x_ref[...]], axis=0), i * tr, tr)
        o_ref[...] = (_dot(diff, w_ref[...]) * s_ref[...]).astype(o_ref.dtype)

    return pl.pallas_call(
        body, grid=(T // tr,),
        in_specs=[pl.BlockSpec((tr, POOL_WIDTH), lambda i: (i, 0)),
                  pl.BlockSpec((16, POOL_WIDTH), lambda i: (jnp.maximum(i * hb - 1, 0), 0)),
                  _full((POOL_WIDTH, POOL_WIDTH)), _full((1, POOL_WIDTH))],
        out_specs=pl.BlockSpec((tr, POOL_WIDTH), lambda i: (i, 0)),
        out_shape=SDS((T, POOL_WIDTH), MXU_DTYPE), name=name, compiler_params=_cp("parallel"))(z, z, wbd, scale)


def _pool_bwd_tile(i, n, tr, x, xprev, dpa, dpa_next, wbd, scale):
    halo = jnp.where(i == 0, 0.0, xprev)
    diff = _pool_diff(jnp.concatenate([halo, x], axis=0), i * tr, tr)
    mixed = _dot(diff, wbd)
    dscale = jnp.sum(dpa * mixed, axis=0, keepdims=True)
    dnext = jnp.where(i == n - 1, 0.0, dpa_next)
    dmix_e = jnp.concatenate([dpa, dnext], axis=0) * scale
    ddiff_e = _dot(dmix_e, wbd, NT)
    dwbd = _dot(diff, dmix_e[:tr], TN)
    shape = (tr + 16, POOL_WIDTH)
    grp, win = _pool_lane_consts(shape)
    t = i * tr + _row(shape)
    e = ddiff_e / jnp.minimum(t + 1, win).astype(F32)
    nrow = tr + 16
    a2 = e + pltpu.roll(e, nrow - 1, 0)
    a4 = a2 + pltpu.roll(a2, nrow - 2, 0)
    a8 = a4 + pltpu.roll(a4, nrow - 4, 0)
    a16 = a8 + pltpu.roll(a8, nrow - 8, 0)
    dx = _pool_select(grp, a2, a4, a8, a16)[:tr] - ddiff_e[:tr]
    return dx, dwbd, dscale


def _norm_rope(x, g, cos, sin_signed, seg):
    reps = x.shape[1] // 128
    ms = _split_dot(x * x, seg) * (1.0 / HEAD_DIM)
    r = lax.rsqrt(ms + EPS)
    xn = x * r * g
    c, s = _tile_lanes(cos, reps), _tile_lanes(sin_signed, reps)
    return xn * c + _swap_halves(xn) * s


def _norm_rope_bwd(x, g, cos, sin_signed, seg, dout):
    reps = x.shape[1] // 128
    c, s = _tile_lanes(cos, reps), _tile_lanes(sin_signed, reps)
    dxn = dout * c + _swap_halves(dout * s)
    ms = _split_dot(x * x, seg) * (1.0 / HEAD_DIM)
    r = lax.rsqrt(ms + EPS)
    xh = x * r
    gy = dxn * g
    dx = r * (gy - xh * (_split_dot(xh * gy, seg) * (1.0 / HEAD_DIM)))
    dg = jnp.sum(dxn * xh, axis=0, keepdims=True)
    return dx, dg


def _dup_heads(k):
    first = _lane(k.shape) < HEAD_DIM
    kr = pltpu.roll(k, HEAD_DIM, 1)
    return jnp.concatenate([jnp.where(first, k, kr), jnp.where(first, kr, k)], axis=1)


def _qkv_prep(z, cos, sin_signed, gq, gk, seg, *, tr, name):
    T = z.shape[0]

    def body(qa_ref, qb_ref, kv_ref, c_ref, s_ref, gq_ref, gk_ref, seg_ref, q_ref, k_ref, v_ref):
        c, s, seg_m = c_ref[...], s_ref[...], seg_ref[...]
        scale = HEAD_DIM ** -0.5
        qa = _norm_rope(qa_ref[...], gq_ref[...], c, s, seg_m) * scale
        qb = _norm_rope(qb_ref[...], gq_ref[...], c, s, seg_m) * scale
        q_ref[...] = jnp.concatenate([qa, qb], axis=1).astype(q_ref.dtype)
        kv = kv_ref[...]
        k = _norm_rope(kv[:, :KV_WIDTH], gk_ref[...], c, s, seg_m[:128, :128])
        k_ref[...] = _dup_heads(k).astype(k_ref.dtype)
        v_ref[...] = _dup_heads(kv[:, KV_WIDTH:]).astype(v_ref.dtype)

    col = lambda j: pl.BlockSpec((tr, 256), lambda i: (i, j))
    tab = pl.BlockSpec((tr, 128), lambda i: (i, 0))
    return pl.pallas_call(
        body, grid=(T // tr,),
        in_specs=[col(1), col(2), col(3), tab, tab, _full((1, 256)), _full((1, 128)), _full((256, 256))],
        out_specs=[pl.BlockSpec((tr, 512), lambda i: (i, 0)), col(0), col(0)],
        out_shape=[SDS((T, 512), MXU_DTYPE), SDS((T, 256), MXU_DTYPE), SDS((T, 256), MXU_DTYPE)],
        name=name, compiler_params=_cp("parallel"))(z, z, z, cos, sin_signed, gq, gk, seg)


def _attn_mask(n):
    qi = _row((ATTN_BLOCK, 2 * ATTN_BLOCK))
    kj = _lane((ATTN_BLOCK, 2 * ATTN_BLOCK))
    return (kj > qi) & (kj <= qi + ATTN_BLOCK) & ((kj >= ATTN_BLOCK) | (n > 0))


def _attn_probs(q128, head_odd, k2, mask, sink):
    in_head = (_lane(q128.shape) >= HEAD_DIM) == head_odd
    qm = jnp.where(in_head, q128, jnp.zeros_like(q128))
    s = _dot(qm, k2, NT)
    s = jnp.where(mask, s, NEG_BIG)
    m = jnp.maximum(jnp.max(s, axis=1, keepdims=True), sink)
    p = jnp.exp(s - m)
    ps = jnp.exp(sink - m)
    inv = 1.0 / (jnp.sum(p, axis=1, keepdims=True) + ps)
    return qm, in_head, p * inv, ps * inv


def _attn_fwd(q, k, v, sinks_b, *, name):
    T = q.shape[0]
    nb = T // ATTN_BLOCK

    def body(q_ref, kc_ref, kp_ref, vc_ref, vp_ref, sk_ref, o_ref):
        n = pl.program_id(0)
        mask = _attn_mask(n)
        k2 = jnp.concatenate([kp_ref[...], kc_ref[...]], axis=0)
        v2 = jnp.concatenate([vp_ref[...], vc_ref[...]], axis=0)
        qv = q_ref[...]
        outs = []
        for pair in range(N_Q_HEADS // 2):
            g = pair // 2
            kg = k2[:, 128 * g:128 * (g + 1)]
            vg = v2[:, 128 * g:128 * (g + 1)]
            q128 = qv[:, 128 * pair:128 * (pair + 1)]
            o_pair = None
            for odd in (False, True):
                h = 2 * pair + int(odd)
                _, in_head, pn, _ = _attn_probs(q128, odd, kg, mask, sk_ref[h:h + 1, 0:1])
                o2 = _dot(pn, vg)
                o_pair = o2 if o_pair is None else jnp.where(in_head, o2, o_pair)
            outs.append(o_pair)
        o_ref[...] = jnp.concatenate(outs, axis=1).astype(o_ref.dtype)

    cur = lambda w: pl.BlockSpec((ATTN_BLOCK, w), lambda n: (n, 0))
    prev = lambda w: pl.BlockSpec((ATTN_BLOCK, w), lambda n: (jnp.maximum(n - 1, 0), 0))
    return pl.pallas_call(
        body, grid=(nb,),
        in_specs=[cur(512), cur(256), prev(256), cur(256), prev(256), _full((8, 128))],
        out_specs=cur(512), out_shape=SDS((T, 512), MXU_DTYPE), name=name,
        compiler_params=_cp("parallel"))(q, k, k, v, v, sinks_b)


def _attn_bwd(q, k, v, sinks_b, do, *, name):
    T = q.shape[0]
    nb = T // ATTN_BLOCK

    def body(q_ref, kc_ref, kp_ref, vc_ref, vp_ref, sk_ref, do_ref,
             dq_ref, dkc_ref, dkp_ref, dvc_ref, dvp_ref, dsk_ref):
        n = pl.program_id(0)
        mask = _attn_mask(n)
        k2 = jnp.concatenate([kp_ref[...], kc_ref[...]], axis=0)
        v2 = jnp.concatenate([vp_ref[...], vc_ref[...]], axis=0)
        qv = q_ref[...]
        dov = do_ref[...]

        @pl.when(n == 0)
        def _():
            dsk_ref[...] = jnp.zeros_like(dsk_ref)

        dqs, dks, dvs = [], [], []
        for g in range(2):
            kg = k2[:, 128 * g:128 * (g + 1)]
            vg = v2[:, 128 * g:128 * (g + 1)]
            dk_g = jnp.zeros((2 * ATTN_BLOCK, 128), F32)
            dv_g = jnp.zeros((2 * ATTN_BLOCK, 128), F32)
            for pair in (2 * g, 2 * g + 1):
                q128 = qv[:, 128 * pair:128 * (pair + 1)]
                do128 = dov[:, 128 * pair:128 * (pair + 1)]
                dq_pair = None
                for odd in (False, True):
                    h = 2 * pair + int(odd)
                    qm, in_head, pn, psn = _attn_probs(q128, odd, kg, mask, sk_ref[h:h + 1, 0:1])
                    dom = jnp.where(in_head, do128, 0.0)
                    o2 = _dot(pn, vg)
                    delta = jnp.sum(dom * o2, axis=1, keepdims=True)
                    dp = _dot(dom, vg, NT)
                    ds = pn * (dp - delta)
                    dq2 = _dot(ds, kg)
                    dq_pair = dq2 if dq_pair is None else jnp.where(in_head, dq2, dq_pair)
                    dk_g = dk_g + _dot(ds, qm, TN)
                    dv_g = dv_g + _dot(pn, dom, TN)
                    dsink = -jnp.sum(psn * delta, axis=0, keepdims=True)
                    dsk_ref[h:h + 1, :] += jnp.broadcast_to(dsink, (1, 128))
                dqs.append(dq_pair)
            dks.append(dk_g)
            dvs.append(dv_g)
        dq_ref[...] = jnp.concatenate(dqs, axis=1)
        dk = jnp.concatenate(dks, axis=1)
        dv = jnp.concatenate(dvs, axis=1)
        dkp_ref[...] = dk[:ATTN_BLOCK]
        dkc_ref[...] = dk[ATTN_BLOCK:]
        dvp_ref[...] = dv[:ATTN_BLOCK]
        dvc_ref[...] = dv[ATTN_BLOCK:]

    cur = lambda w: pl.BlockSpec((ATTN_BLOCK, w), lambda n: (n, 0))
    prev = lambda w: pl.BlockSpec((ATTN_BLOCK, w), lambda n: (jnp.maximum(n - 1, 0), 0))
    f = lambda w: SDS((T, w), F32)
    return pl.pallas_call(
        body, grid=(nb,),
        in_specs=[cur(512), cur(256), prev(256), cur(256), prev(256), _full((8, 128)), cur(512)],
        out_specs=[cur(512), cur(256), cur(256), cur(256), cur(256), _full((8, 128))],
        out_shape=[f(512), f(256), f(256), f(256), f(256), SDS((8, 128), F32)],
        name=name, compiler_params=_cp("arbitrary"))(q, k, k, v, v, sinks_b, do)


def _mixer_ab_bwd(z, cos, sin_signed, gq, gk, seg, dq, dkc, dkp, dvc, dvp, dpa, wbd, scale, dz, *, tr, name, deps=()):
    T = z.shape[0]
    n = T // tr
    hb = tr // 16
    ab = tr // ATTN_BLOCK

    def unfold(cur, nxt_tile, nxt_halo, i):
        nxt = jnp.concatenate([nxt_tile[ATTN_BLOCK:], jnp.where(i == n - 1, 0.0, nxt_halo)], axis=0)
        tot = cur + nxt
        first = _lane((tr, 128)) < HEAD_DIM
        a = tot[:, :128]
        b = tot[:, 128:]
        a = a + pltpu.roll(a, HEAD_DIM, 1)
        b = b + pltpu.roll(b, HEAD_DIM, 1)
        return jnp.where(first, a, b)

    def body(xp_ref, xpp_ref, qa_ref, qb_ref, kv_ref, c_ref, s_ref, gq_ref, gk_ref, seg_ref,
             dq_ref, dkc_ref, dkp_ref, dkh_ref, dvc_ref, dvp_ref, dvh_ref, dpa_ref, dpan_ref, w_ref, sc_ref, _dz_in,
             dz_ref, dgq_ref, dgk_ref, dw_ref, dsc_ref):
        i = pl.program_id(0)
        c, s, seg_m = c_ref[...], s_ref[...], seg_ref[...]
        scale_q = HEAD_DIM ** -0.5
        dqv = dq_ref[...] * scale_q
        dxa, dga = _norm_rope_bwd(qa_ref[...], gq_ref[...], c, s, seg_m, dqv[:, :256])
        dxb, dgb = _norm_rope_bwd(qb_ref[...], gq_ref[...], c, s, seg_m, dqv[:, 256:])
        dk = unfold(dkc_ref[...], dkp_ref[...], dkh_ref[...], i)
        dv = unfold(dvc_ref[...], dvp_ref[...], dvh_ref[...], i)
        kv = kv_ref[...]
        dxk, dgk = _norm_rope_bwd(kv[:, :KV_WIDTH], gk_ref[...], c, s, seg_m[:128, :128], dk)
        dxp, dwbd, dscale = _pool_bwd_tile(i, n, tr, xp_ref[...], xpp_ref[...], dpa_ref[...], dpan_ref[...],
                                           w_ref[...], sc_ref[...])
        dz_ref[...] = jnp.concatenate([dxp, dxa, dxb, dxk, dv], axis=1).astype(dz_ref.dtype)

        @pl.when(i == 0)
        def _():
            dgq_ref[...] = jnp.zeros_like(dgq_ref)
            dgk_ref[...] = jnp.zeros_like(dgk_ref)
            dw_ref[...] = jnp.zeros_like(dw_ref)
            dsc_ref[...] = jnp.zeros_like(dsc_ref)
        dgq_ref[...] += _fold_lanes(dga + dgb, HEAD_DIM)
        dgk_ref[...] += _fold_lanes(dgk, HEAD_DIM)
        dw_ref[...] += dwbd
        dsc_ref[...] += dscale

    col = lambda j: pl.BlockSpec((tr, 256), lambda i: (i, j))
    rows = lambda w: pl.BlockSpec((tr, w), lambda i: (i, 0))
    nxt_blk = pl.BlockSpec((ATTN_BLOCK, 256), lambda i: (jnp.minimum((i + 1) * ab, T // ATTN_BLOCK - 1), 0))
    prev16 = pl.BlockSpec((16, 256), lambda i: (jnp.maximum(i * hb - 1, 0), 0))
    next16 = pl.BlockSpec((16, 256), lambda i: (jnp.minimum((i + 1) * hb, T // 16 - 1), 0))
    return pl.pallas_call(
        _after(body, 22, deps), grid=(n,),
        in_specs=[col(0), prev16, col(1), col(2), col(3), rows(128), rows(128),
                  _full((1, 256)), _full((1, 128)), _full((256, 256)),
                  rows(512), rows(256), rows(256), nxt_blk, rows(256), rows(256), nxt_blk,
                  rows(256), next16, _full((256, 256)), _full((1, 256)), ANY] + [ANY] * len(deps),
        out_specs=[rows(1024), _full((1, 256)), _full((1, 128)), _full((256, 256)), _full((1, 256))],
        out_shape=[SDS((T, IN_COLS), MXU_DTYPE), SDS((1, 256), F32), SDS((1, 128), F32),
                   SDS((256, 256), F32), SDS((1, 256), F32)],
        input_output_aliases={21: 0}, name=name, compiler_params=_cp("arbitrary"))(
            z, z, z, z, z, cos, sin_signed, gq, gk, seg, dq, dkc, dkp, dkp, dvc, dvp, dvp, dpa, dpa, wbd, scale, dz,
            *deps)


def _sgu_common(zu, zv, vn, seg):
    u, du = _gelu_and_grad(zu)
    gv, dgv = _gelu_and_grad(zv)
    ms = _split_dot(gv * gv, seg) * (1.0 / HEAD_DIM)
    r = lax.rsqrt(ms + EPS)
    xh = gv * r
    return u, du, dgv, r, xh, xh * vn


def _sgu_fwd(z, wtril, bexp, vn, seg, *, tr, name):
    T = z.shape[0]
    nch = tr // CHUNK

    def body(u_ref, v_ref, w_ref, b_ref, vn_ref, seg_ref, o_ref):
        u, _, _, _, _, vg = _sgu_common(u_ref[...], v_ref[...], vn_ref[...], seg_ref[...])
        grp = _lane((CHUNK, SGU_WIDTH)) // HEAD_DIM
        outs = []
        for ch in range(nch):
            vc = vg[ch * CHUNK:(ch + 1) * CHUNK]
            s = b_ref[...]
            for g in range(4):
                s = s + jnp.where(grp == g, _dot(w_ref[g], vc), 0.0)
            outs.append(u[ch * CHUNK:(ch + 1) * CHUNK] * s)
        o_ref[...] = jnp.concatenate(outs, axis=0).astype(o_ref.dtype)

    col = lambda j: pl.BlockSpec((tr, 256), lambda i: (i, j))
    return pl.pallas_call(
        body, grid=(T // tr,),
        in_specs=[col(4), col(5), _full((4, CHUNK, CHUNK)), _full((CHUNK, 256)), _full((1, 256)), _full((256, 256))],
        out_specs=col(0), out_shape=SDS((T, SGU_WIDTH), MXU_DTYPE), name=name,
        compiler_params=_cp("parallel"))(z, z, wtril, bexp, vn, seg)


def _sgu_bwd(z, wtril, bexp, vn, seg, dsg, dz, *, tr, name):
    T = z.shape[0]
    nch = tr // CHUNK

    def body(u_ref, v_ref, w_ref, b_ref, vn_ref, seg_ref, d_ref, _dz_in, dz_ref, dw_ref, db_ref, dvn_ref):
        i = pl.program_id(0)
        seg_m = seg_ref[...]
        vn_v = vn_ref[...]
        u, du, dgv, r, xh, vg = _sgu_common(u_ref[...], v_ref[...], vn_v, seg_m)
        d = d_ref[...]
        grp = _lane((CHUNK, SGU_WIDTH)) // HEAD_DIM
        tril = _row((CHUNK, CHUNK)) >= _lane((CHUNK, CHUNK))

        @pl.when(i == 0)
        def _():
            dw_ref[...] = jnp.zeros_like(dw_ref)
            db_ref[...] = jnp.zeros_like(db_ref)
            dvn_ref[...] = jnp.zeros_like(dvn_ref)

        dus, dvgs = [], []
        for ch in range(nch):
            sl = slice(ch * CHUNK, (ch + 1) * CHUNK)
            vc = vg[sl]
            s = b_ref[...]
            for g in range(4):
                s = s + jnp.where(grp == g, _dot(w_ref[g], vc), 0.0)
            dus.append(d[sl] * s)
            ds = d[sl] * u[sl]
            db_ref[...] += _split_dot(ds, seg_m)
            dvg = jnp.zeros((CHUNK, SGU_WIDTH), F32)
            for g in range(4):
                dsm = jnp.where(grp == g, ds, 0.0)
                dvg = dvg + jnp.where(grp == g, _dot(w_ref[g], ds, TN), 0.0)
                dw_ref[g] += jnp.where(tril, _dot(dsm, vc, NT), 0.0)
            dvgs.append(dvg)
        dup = jnp.concatenate(dus, axis=0)
        dvg = jnp.concatenate(dvgs, axis=0)
        dvn_ref[...] += _fold_lanes(jnp.sum(dvg * xh, axis=0, keepdims=True), HEAD_DIM)
        gy = dvg * vn_v
        dgvv = r * (gy - xh * (_split_dot(xh * gy, seg_m) * (1.0 / HEAD_DIM)))
        dz_ref[...] = jnp.concatenate([dup * du, dgvv * dgv], axis=1).astype(dz_ref.dtype)

    col = lambda j: pl.BlockSpec((tr, 256), lambda i: (i, j))
    return pl.pallas_call(
        body, grid=(T // tr,),
        in_specs=[col(4), col(5), _full((4, CHUNK, CHUNK)), _full((CHUNK, 256)), _full((1, 256)), _full((256, 256)),
                  col(0), ANY],
        out_specs=[pl.BlockSpec((tr, 512), lambda i: (i, 2)), _full((4, CHUNK, CHUNK)), _full((CHUNK, 256)),
                   _full((1, 256))],
        out_shape=[SDS((T, IN_COLS), MXU_DTYPE), SDS((4, CHUNK, CHUNK), F32), SDS((CHUNK, 256), F32),
                   SDS((1, 256), F32)],
        input_output_aliases={7: 0}, name=name, compiler_params=_cp("arbitrary"))(
            z, z, wtril, bexp, vn, seg, dsg, dz)


def _merge_fwd(pa, at, sg, wa, wb, wc, z, *, tm, tn, name):
    T = pa.shape[0]
    gb = GATE_COL0 // tn
    nb = D_MODEL // tn

    def body(pa_ref, at_ref, sg_ref, wa_ref, wb_ref, wc_ref, g0_ref, g1_ref, g2_ref, m_ref, y_ref):
        acc = None
        for idx, (op_ref, w_ref, g_ref) in enumerate(((pa_ref, wa_ref, g0_ref), (at_ref, wb_ref, g1_ref),
                                                      (sg_ref, wc_ref, g2_ref))):
            y = _dot(op_ref[...], w_ref[...])
            y_ref[idx] = y
            t = _sigmoid(g_ref[...]) * y
            acc = t if acc is None else acc + t
        m_ref[...] = acc.astype(m_ref.dtype)

    op = lambda w: pl.BlockSpec((tm, w), lambda i, j: (i, 0))
    assert tn == wa.shape[2]
    wt = lambda k: pl.BlockSpec((None, k, tn), lambda i, j: (j, 0, 0))
    gate = lambda b: pl.BlockSpec((tm, tn), lambda i, j: (i, gb + b * nb + j))
    return pl.pallas_call(
        body, grid=(T // tm, nb),
        in_specs=[op(256), op(512), op(256), wt(256), wt(512), wt(256), gate(0), gate(1), gate(2)],
        out_specs=[pl.BlockSpec((tm, tn), lambda i, j: (i, j)), pl.BlockSpec((3, tm, tn), lambda i, j: (0, i, j))],
        out_shape=[SDS((T, D_MODEL), MXU_DTYPE), SDS((3, T, D_MODEL), F32)],
        name=name, compiler_params=_cp("parallel", "parallel"))(pa, at, sg, wa, wb, wc, z, z, z)


def _merge_bwd(dm, y, z, *, tr, tn, name):
    T = dm.shape[0]
    gb = GATE_COL0 // tn
    nb = D_MODEL // tn

    def body(dm_ref, y_ref, g_ref, dy_ref, dz_ref):
        g = _sigmoid(g_ref[...])
        d = dm_ref[...]
        dy_ref[...] = (d * g).astype(dy_ref.dtype)
        dz_ref[...] = (d * y_ref[...] * g * (1.0 - g)).astype(dz_ref.dtype)

    return pl.pallas_call(
        body, grid=(T // tr, 3, nb),
        in_specs=[pl.BlockSpec((tr, tn), lambda i, b, j: (i, j)),
                  pl.BlockSpec((None, tr, tn), lambda i, b, j: (b, i, j)),
                  pl.BlockSpec((tr, tn), lambda i, b, j: (i, gb + b * nb + j))],
        out_specs=[pl.BlockSpec((None, tr, tn), lambda i, b, j: (b, i, j)),
                   pl.BlockSpec((tr, tn), lambda i, b, j: (i, gb + b * nb + j))],
        out_shape=[SDS((3, T, D_MODEL), MXU_DTYPE), SDS((T, IN_COLS), MXU_DTYPE)],
        name=name, compiler_params=_cp("parallel", "parallel", "parallel"))(dm, y, z)


def _conv3(xe, w, b):
    return (w[0:1] * pltpu.roll(xe, 2, 0) + w[1:2] * pltpu.roll(xe, 1, 0) + w[2:3] * xe)[8:] + b


def _conv_act_fwd(up, cw, cb, *, tr, tc, name):
    T = up.shape[0]
    nc = D_FF // tc
    hb = tr // 8

    def body(ug_ref, ugp_ref, uv_ref, uvp_ref, wg_ref, wv_ref, bg_ref, bv_ref, o_ref):
        i = pl.program_id(1)
        first = i == 0
        cg = _conv3(jnp.concatenate([jnp.where(first, 0.0, ugp_ref[...]), ug_ref[...]], axis=0), wg_ref[...], bg_ref[...])
        cv = _conv3(jnp.concatenate([jnp.where(first, 0.0, uvp_ref[...]), uv_ref[...]], axis=0), wv_ref[...], bv_ref[...])
        o_ref[...] = (cg * _sigmoid(cg) * cv).astype(o_ref.dtype)

    tile = lambda off: pl.BlockSpec((tr, tc), lambda j, i: (i, off + j))
    prev = lambda off: pl.BlockSpec((8, tc), lambda j, i: (jnp.maximum(i * hb - 1, 0), off + j))
    par = lambda rows, off: pl.BlockSpec((rows, tc), lambda j, i: (0, off + j))
    return pl.pallas_call(
        body, grid=(nc, T // tr),
        in_specs=[tile(0), prev(0), tile(nc), prev(nc), par(3, 0), par(3, nc), par(1, 0), par(1, nc)],
        out_specs=pl.BlockSpec((tr, tc), lambda j, i: (i, j)),
        out_shape=SDS((T, D_FF), MXU_DTYPE), name=name,
        compiler_params=_cp("parallel", "parallel"))(up, up, up, up, cw, cw, cb, cb)


def _conv_act_bwd(up, cw, cb, dact, *, tr, tc, name, deps=()):
    T = up.shape[0]
    nc = D_FF // tc
    hb = tr // 8
    nr = T // tr

    def body(ug_ref, ugp_ref, ugn_ref, uv_ref, uvp_ref, uvn_ref, da_ref, dan_ref, wg_ref, wv_ref, bg_ref, bv_ref,
             du_ref, dwg_ref, dwv_ref, dbg_ref, dbv_ref):
        i = pl.program_id(1)
        first, last = i == 0, i == nr - 1
        da = jnp.concatenate([da_ref[...], jnp.where(last, 0.0, dan_ref[...])], axis=0)
        uge = jnp.concatenate([jnp.where(first, 0.0, ugp_ref[...]), ug_ref[...], ugn_ref[...]], axis=0)
        uve = jnp.concatenate([jnp.where(first, 0.0, uvp_ref[...]), uv_ref[...], uvn_ref[...]], axis=0)
        wg, wv = wg_ref[...], wv_ref[...]
        cg = _conv3(uge, wg, bg_ref[...])
        cv = _conv3(uve, wv, bv_ref[...])
        sg = _sigmoid(cg)
        dcg = da * cv * (sg * (1.0 + cg * (1.0 - sg)))
        dcv = da * (cg * sg)
        nrow = tr + 8

        def back(dc, w):
            return (w[2:3] * dc + w[1:2] * pltpu.roll(dc, nrow - 1, 0) + w[0:1] * pltpu.roll(dc, nrow - 2, 0))[:tr]

        du_ref[0] = back(dcg, wg).astype(du_ref.dtype)
        du_ref[1] = back(dcv, wv).astype(du_ref.dtype)

        def wgrad(dc, ue):
            d = dc[:tr]
            rows = [jnp.sum(d * pltpu.roll(ue, 2, 0)[8:8 + tr], axis=0, keepdims=True),
                    jnp.sum(d * pltpu.roll(ue, 1, 0)[8:8 + tr], axis=0, keepdims=True),
                    jnp.sum(d * ue[8:8 + tr], axis=0, keepdims=True)]
            return jnp.concatenate(rows, axis=0), jnp.sum(d, axis=0, keepdims=True)

        dwg, dbg = wgrad(dcg, uge)
        dwv, dbv = wgrad(dcv, uve)

        @pl.when(first)
        def _():
            dwg_ref[...] = jnp.zeros_like(dwg_ref)
            dwv_ref[...] = jnp.zeros_like(dwv_ref)
            dbg_ref[...] = jnp.zeros_like(dbg_ref)
            dbv_ref[...] = jnp.zeros_like(dbv_ref)
        dwg_ref[...] += dwg
        dwv_ref[...] += dwv
        dbg_ref[...] += dbg
        dbv_ref[...] += dbv

    tile = lambda off: pl.BlockSpec((tr, tc), lambda j, i: (i, off + j))
    prev = lambda off: pl.BlockSpec((8, tc), lambda j, i: (jnp.maximum(i * hb - 1, 0), off + j))
    nxt = lambda off: pl.BlockSpec((8, tc), lambda j, i: (jnp.minimum((i + 1) * hb, T // 8 - 1), off + j))
    par = lambda rows, off: pl.BlockSpec((rows, tc), lambda j, i: (0, off + j))
    acc = lambda rows: pl.BlockSpec((rows, tc), lambda j, i: (0, j))
    return pl.pallas_call(
        _after(body, 12, deps), grid=(nc, nr),
        in_specs=[tile(0), prev(0), nxt(0), tile(nc), prev(nc), nxt(nc), tile(0), nxt(0),
                  par(3, 0), par(3, nc), par(1, 0), par(1, nc)] + [ANY] * len(deps),
        out_specs=[pl.BlockSpec((2, tr, tc), lambda j, i: (0, i, j)), acc(3), acc(3), acc(1), acc(1)],
        out_shape=[SDS((2, T, D_FF), MXU_DTYPE), SDS((3, D_FF), F32), SDS((3, D_FF), F32),
                   SDS((1, D_FF), F32), SDS((1, D_FF), F32)],
        name=name, compiler_params=_cp("parallel", "arbitrary"))(
            up, up, up, up, up, up, dact, dact, cw, cw, cb, cb, *deps)


def _row_tile(rows, cap):
    t = min(cap, rows)
    t -= t % 8
    while rows % t:
        t -= 8
    return t


def _adamw(w, g, m, v, *, tr, name):
    R, C = w.shape
    assert R % tr == 0, (R, tr)

    def body(w_ref, g_ref, m_ref, v_ref, d_ref, nm_ref, nv_ref):
        gv = g_ref[...]
        mn = ADAM_B1 * m_ref[...] + (1.0 - ADAM_B1) * gv
        vn = ADAM_B2 * v_ref[...] + (1.0 - ADAM_B2) * (gv * gv)
        m_hat = mn / (1.0 - ADAM_B1 ** ADAM_STEP)
        v_hat = vn / (1.0 - ADAM_B2 ** ADAM_STEP)
        d_ref[...] = -ADAM_LR * (m_hat / (jnp.sqrt(v_hat) + ADAM_EPS) + ADAM_WD * w_ref[...])
        nm_ref[...] = mn
        nv_ref[...] = vn

    rows = pl.BlockSpec((tr, C), lambda i: (i, 0))
    return pl.pallas_call(
        body, grid=(R // tr,), in_specs=[rows] * 4, out_specs=[rows] * 3,
        out_shape=[SDS((R, C), F32)] * 3, name=name, compiler_params=_cp("parallel"))(w, g, m, v)


def _sum_slots(r, *, tr, name):
    S, R, C = r.shape
    assert R % tr == 0, (R, tr)

    def body(r_ref, o_ref):
        acc = r_ref[0]
        for s in range(1, S):
            acc = acc + r_ref[s]
        o_ref[...] = acc

    return pl.pallas_call(
        body, grid=(R // tr,), in_specs=[pl.BlockSpec((S, tr, C), lambda i: (0, i, 0))],
        out_specs=pl.BlockSpec((tr, C), lambda i: (i, 0)), out_shape=SDS((R, C), F32),
        name=name, compiler_params=_cp("parallel"))(r)


def _pair_add(g4, h, pos, *, name):
    A, _, r, C = g4.shape
    cs = C if A == N_CHIPS else C // N_CHIPS
    tr = _row_tile(r, 256)
    if A == N_CHIPS:
        g_map, h_map = (lambda t, i, pos: (t, pos[1], i, 0)), (lambda t, i, pos: (t, i, 0))
    else:
        g_map, h_map = (lambda t, i, pos: (0, pos[1], i, t)), (lambda t, i, pos: (0, i, t))

    def body(pos_ref, g_ref, h_ref, o_ref):
        o_ref[...] = (g_ref[...] + h_ref[...]).astype(o_ref.dtype)

    grid_spec = pltpu.PrefetchScalarGridSpec(
        num_scalar_prefetch=1, grid=(N_CHIPS, r // tr),
        in_specs=[pl.BlockSpec((None, None, tr, cs), g_map), pl.BlockSpec((None, tr, cs), h_map)],
        out_specs=pl.BlockSpec((None, tr, cs), lambda t, i, pos: (t, i, 0)))
    return pl.pallas_call(body, grid_spec=grid_spec, out_shape=SDS((N_CHIPS, r, cs), COMM_DTYPE), name=name,
                          compiler_params=_cp("parallel", "parallel"))(pos, g4, h)


def _chip_sum(p, r2, f_into, pos, layer, *, name):
    _, r, cs = p.shape
    tr = _row_tile(r, 256)

    def body(pos_ref, own_ref, r_ref, *rest):
        o_ref = rest[-1]
        o_ref[...] = ((own_ref[...].astype(F32) + r_ref[0].astype(F32)) + r_ref[1].astype(F32)) + r_ref[2].astype(F32)

    in_specs = [pl.BlockSpec((None, tr, cs), lambda i, pos: (pos[0], i, 0)),
                pl.BlockSpec((3, tr, cs), lambda i, pos: (0, i, 0))]
    operands = [pos, p, r2]
    aliases = {}
    if f_into is not None:
        in_specs.append(ANY)
        operands.append(f_into)
        aliases = {3: 0}
    grid_spec = pltpu.PrefetchScalarGridSpec(
        num_scalar_prefetch=1, grid=(r // tr,), in_specs=in_specs,
        out_specs=pl.BlockSpec((None, None, tr, cs), lambda i, pos: (layer, pos[1], i, 0)))
    return pl.pallas_call(body, grid_spec=grid_spec, out_shape=SDS((DEPTH, 2, r, cs), F32), name=name,
                          input_output_aliases=aliases, compiler_params=_cp("parallel"))(*operands)


def _mesh_pos():
    return lax.axis_index("x"), lax.axis_index("y"), lax.axis_index("c")


HBM = pl.BlockSpec(memory_space=pltpu.HBM)
SEM = pl.BlockSpec(memory_space=pltpu.SEMAPHORE)
DATAFLOW = pltpu.SideEffectType.DATAFLOW_SIDE_EFFECTING
CHIP_FLIPS = (2, 1, 3)


def _chip_peers():
    x, y, c = _mesh_pos()
    return 2 * x + y, [(1 - x, y, c), (x, 1 - y, c), (1 - x, 1 - y, c)], (x, y, 1 - c), c


def _split_start(arrays, n_copies, issue, *, name, deps=()):
    k = len(arrays)
    nd = len(deps)

    def body(*refs):
        issue(refs[:k], refs[k + nd], refs[k + nd + 1])
        refs[2 * k + nd + 2][...] = jnp.zeros((8, 128), F32)

    out = pl.pallas_call(
        body, name=name,
        out_shape=(pltpu.SemaphoreType.DMA((n_copies,)), pltpu.SemaphoreType.DMA((n_copies,)),
                   *[pltpu.HBM(a.shape, a.dtype) for a in arrays], SDS((8, 128), F32)),
        in_specs=[HBM] * k + [ANY] * nd, out_specs=(SEM, SEM, *[HBM] * k, pl.BlockSpec(memory_space=pltpu.VMEM)),
        input_output_aliases={i: 2 + i for i in range(k)},
        compiler_params=pltpu.CompilerParams(has_side_effects=DATAFLOW))(
            *[pltpu.with_memory_space_constraint(a, pltpu.HBM) for a in arrays], *deps)
    return (out[0], out[1]), list(out[2:2 + k]), out[2 + k]


def _split_wait(sems, arrays, after, waits, *, name):
    k = len(arrays)

    def body(*refs):
        waits(refs[:k], refs[k], refs[k + 1])

    out = pl.pallas_call(
        body, name=name, out_shape=tuple(pltpu.HBM(a.shape, a.dtype) for a in arrays),
        in_specs=[HBM] * k + [SEM, SEM, ANY], out_specs=tuple([HBM] * k),
        input_output_aliases={i: i for i in range(k)},
        compiler_params=pltpu.CompilerParams(has_side_effects=DATAFLOW))(*arrays, sems[0], sems[1], after)
    return list(out)


def _wait_both(cp):
    cp.wait_send()
    cp.wait_recv()


def _cast_place(shard, pos, dtype, *, name):
    R, C = shard.shape
    tr = R if R % 8 else _row_tile(R, 256)

    def body(pos_ref, x_ref, o_ref):
        o_ref[...] = x_ref[...].astype(o_ref.dtype)

    grid_spec = pltpu.PrefetchScalarGridSpec(
        num_scalar_prefetch=1, grid=(R // tr,),
        in_specs=[pl.BlockSpec((tr, C), lambda i, pos: (i, 0))],
        out_specs=pl.BlockSpec((None, tr, C), lambda i, pos: (pos[0], i, 0)))
    return pl.pallas_call(body, grid_spec=grid_spec, out_shape=SDS((N_CHIPS, R, C), dtype), name=name,
                          compiler_params=_cp("parallel"))(pos, shard)


class _Gather:
    def __init__(self, lands, name, deps=()):
        n = len(lands)
        self.name = name

        def copies(refs, ss, rs):
            me, peers, _, _ = _chip_peers()
            return [pltpu.make_async_remote_copy(
                src_ref=refs[w].at[me], dst_ref=refs[w].at[me], send_sem=ss.at[3 * w + p], recv_sem=rs.at[3 * w + p],
                device_id=peers[p], device_id_type=MESH) for w in range(n) for p in range(3)]

        def issue(refs, ss, rs):
            for cp in copies(refs, ss, rs):
                cp.start()

        def waits(refs, ss, rs):
            for cp in copies(refs, ss, rs):
                _wait_both(cp)

        self._waits = waits
        self.sems, self.arrays, self.token = _split_start(list(lands), 3 * n, issue, name=name + "_start", deps=deps)

    def wait(self, after):
        return _split_wait(self.sems, self.arrays, after, self._waits, name=self.name + "_wait")


def _swap_halves_start(g4s, *, name):
    n = len(g4s)
    lands = [lax.empty((g.shape[0],) + g.shape[2:], g.dtype) for g in g4s]

    def copies(refs, ss, rs):
        _, _, sibling, c = _chip_peers()
        return [pltpu.make_async_remote_copy(
            src_ref=refs[w].at[:, 1 - c], dst_ref=refs[n + w], send_sem=ss.at[w], recv_sem=rs.at[w],
            device_id=sibling, device_id_type=MESH) for w in range(n)]

    def issue(refs, ss, rs):
        for cp in copies(refs, ss, rs):
            cp.start()

    def waits(refs, ss, rs):
        for cp in copies(refs, ss, rs):
            _wait_both(cp)

    sems, arrays, token = _split_start(list(g4s) + lands, n, issue, name=name + "_start")
    return sems, arrays, token, waits


def _scatter_start(parts, *, name):
    n = len(parts)
    lands = [lax.empty((3,) + p.shape[1:], p.dtype) for p in parts]

    def copies(refs, ss, rs):
        me, peers, _, _ = _chip_peers()
        return [pltpu.make_async_remote_copy(
            src_ref=refs[w].at[me ^ CHIP_FLIPS[p]], dst_ref=refs[n + w].at[p],
            send_sem=ss.at[3 * w + p], recv_sem=rs.at[3 * w + p], device_id=peers[p], device_id_type=MESH)
            for w in range(n) for p in range(3)]

    def issue(refs, ss, rs):
        for cp in copies(refs, ss, rs):
            cp.start()

    def waits(refs, ss, rs):
        for cp in copies(refs, ss, rs):
            _wait_both(cp)

    sems, arrays, token = _split_start(list(parts) + lands, 3 * n, issue, name=name + "_start")
    return sems, arrays, token, waits


def _pair_share_start(fs, layer, *, name):
    n = len(fs)

    def copies(refs, ss, rs):
        _, _, sibling, c = _chip_peers()
        return [pltpu.make_async_remote_copy(
            src_ref=refs[w].at[layer, c], dst_ref=refs[w].at[layer, c], send_sem=ss.at[w], recv_sem=rs.at[w],
            device_id=sibling, device_id_type=MESH) for w in range(n)]

    def issue(refs, ss, rs):
        for cp in copies(refs, ss, rs):
            cp.start()

    def waits(refs, ss, rs):
        for cp in copies(refs, ss, rs):
            _wait_both(cp)

    sems, arrays, token = _split_start(list(fs), n, issue, name=name + "_start")
    return sems, arrays, token, waits


def _allgather_devices(pack, *, name, deps=()):
    def body_all(in_ref, out_ref, send_sems, recv_sems, local_sem):
        x, y, c = _mesh_pos()
        me = 4 * x + 2 * y + c
        remotes = []
        for f in range(1, N_DEV):
            fx, fy, fc = (f >> 2) & 1, (f >> 1) & 1, f & 1
            peer = (x ^ fx, y ^ fy, c ^ fc)
            remotes.append(pltpu.make_async_remote_copy(
                src_ref=in_ref, dst_ref=out_ref.at[me], send_sem=send_sems.at[f - 1], recv_sem=recv_sems.at[f - 1],
                device_id=peer, device_id_type=MESH))
        mine = pltpu.make_async_copy(in_ref, out_ref.at[me], local_sem)
        mine.start()
        for cp in remotes:
            cp.start()
        mine.wait()
        for cp in remotes:
            cp.wait_send()
            cp.wait_recv()

    return pl.pallas_call(
        _after(body_all, 1, deps), in_specs=[ANY] * (1 + len(deps)), out_specs=ANY,
        out_shape=SDS((N_DEV,) + pack.shape, pack.dtype),
        scratch_shapes=[pltpu.SemaphoreType.DMA((N_DEV - 1,)), pltpu.SemaphoreType.DMA((N_DEV - 1,)),
                        pltpu.SemaphoreType.DMA],
        name=name, compiler_params=pltpu.CompilerParams(has_side_effects=True))(pack, *deps)


BIG = ('w_in', 'w_proj_a', 'w_proj_b', 'w_proj_c', 'w_out', 'w_up', 'w_down')
BIG_SHARD_AXIS = {'w_in': 2, 'w_proj_a': 2, 'w_proj_b': 2, 'w_proj_c': 2, 'w_out': 1, 'w_up': 2, 'w_down': 1}
SMALL = ('norm1', 'q_norm', 'k_norm', 'sinks', 'w_pool', 'pool_scale', 'sgu_v_norm', 'w_s', 'b_s', 'norm2',
         'conv_b', 'conv_w')
WEIGHTS = ('norm1', 'w_in', 'q_norm', 'k_norm', 'sinks', 'w_pool', 'pool_scale', 'sgu_v_norm', 'w_s', 'b_s',
           'w_proj_a', 'w_proj_b', 'w_proj_c', 'w_out', 'norm2', 'w_up', 'conv_w', 'conv_b', 'w_down')


def _rope_tables(positions):
    inv_freq = ROPE_THETA ** (-jnp.arange(0, HEAD_DIM, 2, dtype=F32) / HEAD_DIM)
    ang = positions.astype(F32)[:, None] * inv_freq
    cos, sin = jnp.cos(ang), jnp.sin(ang)
    c = jnp.concatenate([cos, cos], axis=1)
    s = jnp.concatenate([-sin, sin], axis=1)
    return jnp.concatenate([c, c], axis=1), jnp.concatenate([s, s], axis=1)


def _block_diag4(w):
    out = jnp.zeros((POOL_WIDTH, POOL_WIDTH), w.dtype)
    for g in range(4):
        out = lax.dynamic_update_slice(out, w[g], (g * HEAD_DIM, g * HEAD_DIM))
    return out


def _local_step(x, target, cos, sin, sp, sched):
    T = x.shape[0]
    tm1 = min(1024, T)
    tm = min(512, T)
    tr = min(256, T)
    tkt = min(512, T)
    seg = _seg_matrix(256, HEAD_DIM)
    saved = []
    xl = x
    for l in range(DEPTH):
        p = f"l{l}_"
        c = dict(
            g1=sp['norm1'][l][None], g2=sp['norm2'][l][None],
            wbd=_block_diag4(sp['w_pool'][l]).astype(MXU_DTYPE), scale=sp['pool_scale'][l][None],
            gq=jnp.tile(sp['q_norm'][l], 4)[None], gk=jnp.tile(sp['k_norm'][l], 2)[None],
            sinks=jnp.broadcast_to(sp['sinks'][l][:, None], (N_Q_HEADS, 128)),
            wtril=jnp.tril(sp['w_s'][l]).astype(MXU_DTYPE),
            bexp=jnp.repeat(sp['b_s'][l].T, HEAD_DIM, axis=1), vn=jnp.tile(sp['sgu_v_norm'][l], 4)[None],
            cb=sp['conv_b'][l][None])
        h1 = _rms_fwd(xl, c['g1'], tr=tr, name=p + "rms1", deps=sched.start_tokens() if l == 0 else ())
        c['w_in'] = sched.weight('w_in', l, h1)
        z = _mm(h1, c['w_in'], mode='nn', b_sharded=True, tm=tm1, tn=1152, tk=D_MODEL, name=p + "in_proj")
        pa = _pool_fwd(z, c['wbd'], c['scale'], tr=tr, name=p + "pool")
        q, k, v = _qkv_prep(z, cos, sin, c['gq'], c['gk'], seg, tr=tr, name=p + "qkv_prep")
        at = _attn_fwd(q, k, v, c['sinks'], name=p + "attn")
        sg = _sgu_fwd(z, c['wtril'], c['bexp'], c['vn'], seg, tr=tr, name=p + "sgu")
        for n in ('w_proj_a', 'w_proj_b', 'w_proj_c', 'w_out', 'w_up', 'conv_w', 'w_down'):
            c[n] = sched.weight(n, l, sg)
        merged, y3 = _merge_fwd(pa, at, sg, c['w_proj_a'], c['w_proj_b'], c['w_proj_c'], z,
                                tm=tm, tn=256, name=p + "merge")
        x1 = _mm(merged, c['w_out'], mode='nn', add=xl, tm=tm, tn=D_MODEL, tk=D_MODEL, name=p + "out_proj")
        h2 = _rms_fwd(x1, c['g2'], tr=tr, name=p + "rms2")
        up = _mm(h2, c['w_up'], mode='nn', b_sharded=True, tm=tm1, tn=1408, tk=D_MODEL, name=p + "up_proj")
        act = _conv_act_fwd(up, c['conv_w'], c['cb'], tr=tr, tc=1408, name=p + "conv_act")
        x2 = _mm(act, c['w_down'], mode='nn', add=x1, tm=tm, tn=D_MODEL, tk=1408, name=p + "down_proj")
        saved.append(dict(c, x=xl, h1=h1, z=z, pa=pa, q=q, k=k, v=v, at=at, sg=sg, merged=merged, y3=y3,
                          x1=x1, h2=h2, up=up, act=act))
        xl = x2

    loss_row, dx, dxb = _loss_head(xl, target, tr=tr, name="loss_head")

    gs = {n: [None] * DEPTH for n in SMALL}
    for l in reversed(range(DEPTH)):
        p = f"l{l}_b_"
        s = saved[l]
        gb = {}
        dact = _mm(dxb, s['w_down'], mode='nt', tm=tm, tn=1408, tk=D_MODEL, name=p + "down_dx")
        gb['w_down'] = _mm(s['act'], dxb, mode='tn', tm=1408, tn=D_MODEL, tk=tkt, name=p + "down_dw")
        toks = sched.slot(l, 'down', gb['w_down'])
        dup, dwg, dwv, dbg, dbv = _conv_act_bwd(s['up'], s['conv_w'], s['cb'], dact, tr=min(512, T), tc=256,
                                                name=p + "conv_act", deps=toks)
        gs['conv_w'][l] = jnp.concatenate([dwg, dwv], axis=1)
        gs['conv_b'][l] = jnp.concatenate([dbg, dbv], axis=1)[0]
        toks = sched.slot(l, 'conv', dup)
        dh2 = None
        for half in range(2):
            dh2 = _mm(dup, s['w_up'], mode='nt', a_lead=half, b_sharded=True, tm=tm, tn=D_MODEL, tk=1408,
                      b_koff=2 * half, add=dh2, name=p + f"up_dx{half}", deps=toks if half == 0 else ())
            gb['w_up'] = _mm(s['h2'], dup, mode='tn', b_lead=half, tm=D_MODEL, tn=1408, tk=tkt,
                             out_into=gb.get('w_up'), out_joff=2 * half, out_n=2 * D_FF, name=p + f"up_dw{half}")
        toks = sched.slot(l, 'ffn', gb['w_up'], gb)
        dx1, dx1b, dg2 = _rms_bwd(s['x1'], s['g2'], dh2, dx, tr=tr, name=p + "rms2", deps=toks)
        gs['norm2'][l] = dg2[0]
        dmerged = _mm(dx1b, s['w_out'], mode='nt', tm=tm, tn=D_MODEL, tk=D_MODEL, name=p + "out_dx")
        gb['w_out'] = _mm(s['merged'], dx1b, mode='tn', tm=D_MODEL, tn=D_MODEL, tk=tkt, name=p + "out_dw")
        dy3, dz = _merge_bwd(dmerged, s['y3'], s['z'], tr=min(1024, T), tn=512, name=p + "merge")
        toks = sched.slot(l, 'mid', dz)
        dbr = []
        for idx, (wn, opn, width) in enumerate((('w_proj_a', 'pa', POOL_WIDTH), ('w_proj_b', 'at', ATTN_WIDTH),
                                                ('w_proj_c', 'sg', SGU_WIDTH))):
            dbr.append(_mm(dy3, s[wn], mode='nt', a_lead=idx, b_sharded=True, tm=tm, tn=width, tk=256,
                           name=p + f"proj{idx}_dx", deps=toks if idx == 0 else ()))
            gb[wn] = _mm(s[opn], dy3, mode='tn', b_lead=idx, tm=width, tn=D_MODEL, tk=tkt,
                         name=p + f"proj{idx}_dw")
        dpa, dat, dsg = dbr
        dq, dkc, dkp, dvc, dvp, dsk = _attn_bwd(s['q'], s['k'], s['v'], s['sinks'], dat, name=p + "attn")
        gs['sinks'][l] = dsk[:, 0]
        toks = sched.slot(l, 'attn', dq)
        dz, dgq, dgk, dwbd, dsc = _mixer_ab_bwd(s['z'], cos, sin, s['gq'], s['gk'], seg, dq, dkc, dkp, dvc, dvp,
                                                dpa, s['wbd'], s['scale'], dz, tr=tr, name=p + "qkv_pool", deps=toks)
        gs['q_norm'][l] = dgq[0, :HEAD_DIM]
        gs['k_norm'][l] = dgk[0, :HEAD_DIM]
        gs['w_pool'][l] = jnp.stack([dwbd[g * HEAD_DIM:(g + 1) * HEAD_DIM, g * HEAD_DIM:(g + 1) * HEAD_DIM]
                                     for g in range(4)])
        gs['pool_scale'][l] = dsc[0]
        dz, dws, dbrows, dvn = _sgu_bwd(s['z'], s['wtril'], s['bexp'], s['vn'], seg, dsg, dz, tr=tr, name=p + "sgu")
        gs['w_s'][l] = dws
        gs['b_s'][l] = dbrows[:, ::HEAD_DIM].T
        gs['sgu_v_norm'][l] = dvn[0, :HEAD_DIM]
        dh1 = _mm(dz, s['w_in'], mode='nt', b_sharded=True, tm=tm, tn=D_MODEL, tk=1152, name=p + "in_dx")
        gb['w_in'] = _mm(s['h1'], dz, mode='tn', tm=D_MODEL, tn=1152, tk=tkt, name=p + "in_dw")
        toks = sched.slot(l, 'mix', gb['w_in'], gb)
        dx, dxb, dg1 = _rms_bwd(s['x'], s['g1'], dh1, dx1, tr=tr, name=p + "rms1", deps=toks)
        gs['norm1'][l] = dg1[0]
    gs = {n: jnp.stack(v) for n, v in gs.items()}
    return loss_row, dx, gs


GROUP_F = ('w_down', 'w_up')
GROUP_M = ('w_out', 'w_proj_a', 'w_proj_b', 'w_proj_c', 'w_in')
ROW_SHARDED = ('w_out', 'w_down')

REDUCE_PLAN = {
    (1, 'ffn'): (('S1', 'F', 1),),
    (1, 'mid'): (('W1', 'F', 1),),
    (1, 'mix'): (('S1', 'M', 1),),
    (0, 'down'): (('W1', 'M', 1),),
    (0, 'conv'): (('W2', 'F', 1),),
    (0, 'ffn'): (('S1', 'F', 0), ('W3', 'F', 1)),
    (0, 'mid'): (('W1', 'F', 0),),
    (0, 'attn'): (('W2', 'M', 1),),
    (0, 'mix'): (('S1', 'M', 0), ('W3', 'M', 1)),
}
REDUCE_TAIL_A = (('W1', 'M', 0), ('W2', 'F', 0))
REDUCE_TAIL_B = (('W2', 'M', 0), ('W3', 'F', 0), ('W3', 'M', 0))


class _Comm:
    def __init__(self, w, pos):
        self.pos = pos
        groups = {'a': [('w_in', 0)],
                  'b': [(n, 0) for n in BIG[1:]] + [('conv_w', 0)],
                  'c': [(n, 1) for n in BIG] + [('conv_w', 1)]}
        self.gathers, self.group_of, self.weights = {}, {}, {}
        self.tokens = []
        for g, ks in groups.items():
            lands = [_cast_place(w[n][l], pos, F32 if n == 'conv_w' else MXU_DTYPE, name=f"gw_place_{n}{l}")
                     for n, l in ks]
            self.gathers[g] = (_Gather(lands, "gw_" + g, deps=self.tokens[-1:]), ks)
            self.tokens.append(self.gathers[g][0].token)
            self.group_of.update({k: g for k in ks})
        self.red = {}
        self.final = {}

    def start_tokens(self):
        return self.tokens[-1:]

    def weight(self, name, layer, after):
        if (name, layer) not in self.weights:
            gather, ks = self.gathers[self.group_of[(name, layer)]]
            for (n, l), full in zip(ks, gather.wait(after)):
                if n == 'conv_w':
                    full = full.transpose(1, 0, 2).reshape(full.shape[1], -1)
                elif n in ROW_SHARDED:
                    full = full.reshape(-1, full.shape[2])
                self.weights[(n, l)] = full
        return self.weights[(name, layer)]

    def slot(self, layer, slot, after, grads=None):
        tokens = []
        for step, grp, lyr in REDUCE_PLAN.get((layer, slot), ()):
            tok = self._step(step, grp, lyr, after, grads)
            if tok is not None:
                tokens.append(tok)
        return tokens

    def tail(self, steps, after):
        return [t for t in (self._step(step, grp, lyr, after, None) for step, grp, lyr in steps) if t is not None]

    def shards(self):
        return {n: f.reshape(DEPTH, 2 * f.shape[2], f.shape[3]) for n, f in self.final.items()}

    def _step(self, step, grp, layer, after, grads):
        names = GROUP_F if grp == 'F' else GROUP_M
        tag = f"{grp.lower()}{layer}"
        st = self.red.setdefault((grp, layer), {})
        n = len(names)
        if step == 'S1':
            g4s = []
            for nm in names:
                g = grads[nm]
                R, C = g.shape
                g4s.append(g.reshape(N_CHIPS, 2, R // (2 * N_CHIPS), C) if nm in ROW_SHARDED
                           else g.reshape(1, 2, R // 2, C))
            st['s1'] = _swap_halves_start(g4s, name="rs1_" + tag)
            return st['s1'][2]
        if step == 'W1':
            sems, arrays, _, waits = st.pop('s1')
            arrays = _split_wait(sems, arrays, after, waits, name=f"rs1_{tag}_wait")
            parts = [_pair_add(arrays[i], arrays[n + i], self.pos, name=f"pair_add_{tag}_{names[i]}")
                     for i in range(n)]
            st['s2'] = _scatter_start(parts, name="rs2_" + tag)
            return st['s2'][2]
        if step == 'W2':
            sems, arrays, _, waits = st.pop('s2')
            arrays = _split_wait(sems, arrays, after, waits, name=f"rs2_{tag}_wait")
            fs = [_chip_sum(arrays[i], arrays[n + i], self.final.get(names[i]), self.pos, layer,
                            name=f"chip_sum_{tag}_{names[i]}") for i in range(n)]
            st['s3'] = _pair_share_start(fs, layer, name="rs3_" + tag)
            return st['s3'][2]
        sems, arrays, _, waits = st.pop('s3')
        self.final.update(zip(names, _split_wait(sems, arrays, after, waits, name=f"rs3_{tag}_wait")))
        return None


def _pack(arrays):
    flat = []
    for a in arrays:
        f = a.reshape(-1).astype(F32)
        flat.append(jnp.pad(f, (0, (-f.shape[0]) % 128)))
    v = jnp.concatenate(flat)
    v = jnp.pad(v, (0, (-v.shape[0]) % 1024))
    return v.reshape(-1, 128)


def _unpack(pack, shapes):
    v = pack.reshape(-1)
    out, off = [], 0
    for shp in shapes:
        nel = int(np.prod(shp))
        out.append(v[off:off + nel].reshape(shp))
        off += nel + (-nel) % 128
    return out


def kernel(x, positions, norm1, w_in, q_norm, k_norm, sinks, w_pool, pool_scale, sgu_v_norm, w_s, b_s, w_proj_a, w_proj_b, w_proj_c, w_out, norm2, w_up, conv_w, conv_b, w_down, loss_target, m_norm1, m_w_in, m_q_norm, m_k_norm, m_sinks, m_w_pool, m_pool_scale, m_sgu_v_norm, m_w_s, m_b_s, m_w_proj_a, m_w_proj_b, m_w_proj_c, m_w_out, m_norm2, m_w_up, m_conv_w, m_conv_b, m_w_down, v_norm1, v_w_in, v_q_norm, v_k_norm, v_sinks, v_w_pool, v_pool_scale, v_sgu_v_norm, v_w_s, v_b_s, v_w_proj_a, v_w_proj_b, v_w_proj_c, v_w_out, v_norm2, v_w_up, v_conv_w, v_conv_b, v_w_down):
    w = dict(norm1=norm1, w_in=w_in, q_norm=q_norm, k_norm=k_norm, sinks=sinks, w_pool=w_pool, pool_scale=pool_scale,
             sgu_v_norm=sgu_v_norm, w_s=w_s, b_s=b_s, w_proj_a=w_proj_a, w_proj_b=w_proj_b, w_proj_c=w_proj_c,
             w_out=w_out, norm2=norm2, w_up=w_up, conv_w=conv_w, conv_b=conv_b, w_down=w_down)
    m = dict(norm1=m_norm1, w_in=m_w_in, q_norm=m_q_norm, k_norm=m_k_norm, sinks=m_sinks, w_pool=m_w_pool,
             pool_scale=m_pool_scale, sgu_v_norm=m_sgu_v_norm, w_s=m_w_s, b_s=m_b_s, w_proj_a=m_w_proj_a,
             w_proj_b=m_w_proj_b, w_proj_c=m_w_proj_c, w_out=m_w_out, norm2=m_norm2, w_up=m_w_up, conv_w=m_conv_w,
             conv_b=m_conv_b, w_down=m_w_down)
    v = dict(norm1=v_norm1, w_in=v_w_in, q_norm=v_q_norm, k_norm=v_k_norm, sinks=v_sinks, w_pool=v_w_pool,
             pool_scale=v_pool_scale, sgu_v_norm=v_sgu_v_norm, w_s=v_w_s, b_s=v_b_s, w_proj_a=v_w_proj_a,
             w_proj_b=v_w_proj_b, w_proj_c=v_w_proj_c, w_out=v_w_out, norm2=v_norm2, w_up=v_w_up, conv_w=v_conv_w,
             conv_b=v_conv_b, w_down=v_w_down)
    chip = 2 * lax.axis_index("x") + lax.axis_index("y")
    core = lax.axis_index("c")

    pos = jnp.stack([chip, core]).astype(jnp.int32)
    comm = _Comm(w, pos)

    cos, sin = _rope_tables(positions[0])
    sp = {n: w[n] for n in SMALL if n != 'conv_w'}
    loss_row, dx, gs = _local_step(x[0], loss_target[0], cos, sin, sp, comm)
    loss = lax.psum(loss_row[0, 0], ("x", "y", "c"))

    small_shapes = [gs[n].shape for n in SMALL]
    small_pack = _pack([gs[n] for n in SMALL])
    toks = comm.tail(REDUCE_TAIL_A, dx)
    red = _sum_slots(_allgather_devices(small_pack, name="small_gather", deps=toks), tr=small_pack.shape[0],
                     name="small_sum")
    g_small = dict(zip(SMALL, _unpack(red, small_shapes)))
    comm.tail(REDUCE_TAIL_B, red)
    grads = comm.shards()
    grads.update(g_small)
    shard_cols = conv_w.shape[2]
    grads['conv_w'] = lax.dynamic_slice_in_dim(g_small['conv_w'], chip * shard_cols, shard_cols, axis=2)

    delta, new_m, new_v = {}, {}, {}
    for n in BIG:
        shp = w[n].shape
        two_d = lambda a: a.reshape(shp[0] * shp[1], shp[2])
        d, nm, nv = _adamw(two_d(w[n]), two_d(grads[n]), two_d(m[n]), two_d(v[n]),
                           tr=_row_tile(shp[0] * shp[1], 256), name=f"adamw_{n}")
        delta[n], new_m[n], new_v[n] = d.reshape(shp), nm.reshape(shp), nv.reshape(shp)
    shapes = [w[n].shape for n in SMALL]
    packs = [_pack([src[n] for n in SMALL]) for src in (w, grads, m, v)]
    d, nm, nv = _adamw(*packs, tr=packs[0].shape[0], name="adamw_small")
    for dst, src in ((delta, d), (new_m, nm), (new_v, nv)):
        dst.update(zip(SMALL, _unpack(src, shapes)))

    return (loss, dx[None], *[grads[n] for n in WEIGHTS], *[delta[n] for n in WEIGHTS],
            *[new_m[n] for n in WEIGHTS], *[new_v[n] for n in WEIGHTS])
```

```python
import functools
import math

import numpy as np
import jax
import jax.numpy as jnp
from jax import lax
from jax.experimental import pallas as pl
from jax.experimental.pallas import tpu as pltpu

F32 = jnp.float32
MXU_DTYPE = jnp.bfloat16
COMM_DTYPE = jnp.bfloat16

D_MODEL = 1024
DEPTH = 2
HEAD_DIM = 64
POOL_WINDOWS = (2, 4, 8, 16)
POOL_WIDTH = 256
N_Q_HEADS = 8
ATTN_BLOCK = 128
ATTN_WIDTH = 512
KV_WIDTH = 128
CHUNK = 128
SGU_WIDTH = 256
IN_COLS = 4608
GATE_COL0 = 1536
D_FF = 2816
ROPE_THETA = 10000.0
EPS = 1e-6
ADAM_LR, ADAM_B1, ADAM_B2, ADAM_EPS, ADAM_WD, ADAM_STEP = 0.001, 0.9, 0.999, 1e-08, 0.01, 10

N_CHIPS = 4
N_DEV = 8
VMEM_LIMIT_BYTES = 56 * 1024 * 1024
NEG_BIG = -1e30
MESH = pl.DeviceIdType.MESH
ANY = pl.BlockSpec(memory_space=pl.ANY)

SDS = jax.ShapeDtypeStruct


def _cp(*sem):
    return pltpu.CompilerParams(dimension_semantics=sem, vmem_limit_bytes=VMEM_LIMIT_BYTES)


def _dot(a, b, dims=((1,), (0,))):
    return lax.dot_general(a.astype(MXU_DTYPE), b.astype(MXU_DTYPE), (dims, ((), ())),
                           preferred_element_type=F32)


NT = ((1,), (1,))
TN = ((0,), (0,))


def _split_dot(x, m):
    hi = x.astype(MXU_DTYPE)
    lo = (x - hi.astype(F32)).astype(MXU_DTYPE)
    return _dot(hi, m) + _dot(lo, m)


def _seg_matrix(width, seg):
    idx = np.arange(width) // seg
    return jnp.asarray((idx[:, None] == idx[None, :]).astype(np.float32), dtype=MXU_DTYPE)


def _lane(shape):
    return lax.broadcasted_iota(jnp.int32, shape, len(shape) - 1)


def _row(shape):
    return lax.broadcasted_iota(jnp.int32, shape, 0)


def _full(shape):
    nd = len(shape)
    return pl.BlockSpec(shape, lambda *_: (0,) * nd)


def _gelu(x):
    k = math.sqrt(2.0 / math.pi)
    th = jnp.tanh(k * (x + 0.044715 * (x * x * x)))
    return 0.5 * x * (1.0 + th)


def _gelu_and_grad(x):
    k = math.sqrt(2.0 / math.pi)
    x2 = x * x
    th = jnp.tanh(k * (x + 0.044715 * (x2 * x)))
    g = 0.5 * x * (1.0 + th)
    dg = 0.5 * (1.0 + th) + 0.5 * x * (1.0 - th * th) * (k * (1.0 + 3.0 * 0.044715 * x2))
    return g, dg


def _sigmoid(x):
    return 0.5 * jnp.tanh(0.5 * x) + 0.5


def _swap_halves(x):
    w = x.shape[-1]
    first = (_lane(x.shape) % HEAD_DIM) < (HEAD_DIM // 2)
    return jnp.where(first, pltpu.roll(x, w - HEAD_DIM // 2, 1), pltpu.roll(x, HEAD_DIM // 2, 1))


def _tile_lanes(x, reps):
    return x if reps == 1 else jnp.concatenate([x] * reps, axis=1)


def _fold_lanes(x, period):
    w = x.shape[-1]
    while w > period:
        w //= 2
        x = x + pltpu.roll(x, w, 1)
    return x


def _mm(a, b, *, mode, tm, tn, tk, out_dtype=F32, add=None, name,
        a_lead=None, b_lead=None, b_sharded=False, out_into=None,
        b_koff=0, out_joff=0, out_n=None, deps=()):
    ash = a.shape[1:] if a_lead is not None else a.shape
    bsh = b.shape[1:] if b_lead is not None else b.shape
    if b_sharded:
        bsh = (b.shape[1], N_CHIPS * b.shape[2])
    if mode == 'nn':
        (M, K), (K2, N) = ash, bsh
    elif mode == 'nt':
        (M, K), (N, K2) = ash, bsh
    else:
        (K, M), (K2, N) = ash, bsh
    assert K == K2 or (mode == 'nt' and K2 > K), (ash, bsh, mode)
    assert M % tm == 0 and N % tn == 0 and K % tk == 0, (M, N, K, tm, tn, tk)
    nk = K // tk
    dims = {'nn': ((1,), (0,)), 'nt': NT, 'tn': TN}[mode]

    def lead(spec_shape, imap, lead_idx):
        if lead_idx is None:
            return pl.BlockSpec(spec_shape, imap)
        return pl.BlockSpec((None,) + spec_shape, lambda i, j, k: (lead_idx,) + imap(i, j, k))

    if mode == 'tn':
        a_spec = lead((tk, tm), lambda i, j, k: (k, i), a_lead)
    else:
        a_spec = lead((tm, tk), lambda i, j, k: (i, k), a_lead)
    if b_sharded:
        per = b.shape[2] // (tk if mode == 'nt' else tn)
        assert per * (tk if mode == 'nt' else tn) == b.shape[2] and mode != 'tn'
        if mode == 'nt':
            b_spec = pl.BlockSpec((None, tn, tk), lambda i, j, k: ((k + b_koff) // per, j, (k + b_koff) % per))
        else:
            b_spec = pl.BlockSpec((None, tk, tn), lambda i, j, k: (j // per, k, j % per))
    elif mode == 'nt':
        b_spec = lead((tn, tk), lambda i, j, k: (j, k + b_koff), b_lead)
    else:
        b_spec = lead((tk, tn), lambda i, j, k: (k, j), b_lead)
    o_spec = pl.BlockSpec((tm, tn), lambda i, j, k: (i, j + out_joff))
    n_out = N if out_n is None else out_n
    in_specs = [a_spec, b_spec]
    operands = [a, b]
    if add is not None:
        in_specs.append(pl.BlockSpec((tm, tn), lambda i, j, k: (i, j)))
        operands.append(add)
    aliases = {}
    if out_into is not None:
        in_specs.append(ANY)
        operands.append(out_into)
        aliases = {len(operands) - 1: 0}
    in_specs += [ANY] * len(deps)
    operands += list(deps)
    has_add = add is not None
    acc_in_out = nk > 1 and out_dtype == F32

    def body(*refs):
        a_ref, b_ref = refs[0], refs[1]
        pos = 2
        add_ref = None
        if has_add:
            add_ref = refs[pos]
            pos += 1
        if out_into is not None:
            pos += 1
        pos += len(deps)
        o_ref = refs[pos]
        acc_ref = refs[pos + 1] if (nk > 1 and not acc_in_out) else None
        p = _dot(a_ref[...], b_ref[...], dims)
        if nk == 1:
            if has_add:
                p = p + add_ref[...]
            o_ref[...] = p.astype(o_ref.dtype)
            return
        k = pl.program_id(2)
        tgt = o_ref if acc_in_out else acc_ref

        @pl.when(k == 0)
        def _():
            tgt[...] = p + add_ref[...] if has_add else p

        @pl.when(k > 0)
        def _():
            tgt[...] += p

        if not acc_in_out:
            @pl.when(k == nk - 1)
            def _():
                o_ref[...] = acc_ref[...].astype(o_ref.dtype)

    out_shape = SDS((M, n_out), out_dtype)
    scratch = [pltpu.VMEM((tm, tn), F32)] if (nk > 1 and not acc_in_out) else []
    return pl.pallas_call(
        body, grid=(M // tm, N // tn, nk), in_specs=in_specs, out_specs=o_spec, out_shape=out_shape,
        scratch_shapes=scratch, input_output_aliases=aliases, name=name,
        compiler_params=_cp("parallel", "parallel", "arbitrary"))(*operands)


def _mm_nt_sharded(a, b, *, tm, name, deps=()):
    a3 = a if a.ndim == 3 else a[None]
    A, M, ka = a3.shape
    S, N, ns = b.shape
    per = S // A
    assert ka == per * ns and M % tm == 0, (a3.shape, b.shape)

    def body(a_ref, b_ref, o_ref):
        acc = None
        for s in range(S):
            lo = (s % per) * ns
            p = _dot(a_ref[s // per, :, lo:lo + ns], b_ref[s], NT)
            acc = p if acc is None else acc + p
        o_ref[...] = acc

    return pl.pallas_call(
        _after(body, 2, deps), grid=(M // tm,),
        in_specs=[pl.BlockSpec((A, tm, ka), lambda i: (0, i, 0)), pl.BlockSpec((S, N, ns), lambda i: (0, 0, 0))]
        + [ANY] * len(deps),
        out_specs=pl.BlockSpec((tm, N), lambda i: (i, 0)), out_shape=SDS((M, N), F32), name=name,
        compiler_params=_cp("parallel"))(a3, b, *deps)


def _after(body, n_in, deps):
    nd = len(deps)
    if nd == 0:
        return body
    return lambda *refs: body(*refs[:n_in], *refs[n_in + nd:])


def _rms_fwd(x, g, *, tr, name, deps=()):
    T, D = x.shape

    def body(x_ref, g_ref, o_ref):
        xv = x_ref[...]
        r = lax.rsqrt(jnp.mean(xv * xv, axis=-1, keepdims=True) + EPS)
        o_ref[...] = (xv * r * g_ref[...]).astype(o_ref.dtype)

    return pl.pallas_call(
        _after(body, 2, deps), grid=(T // tr,),
        in_specs=[pl.BlockSpec((tr, D), lambda i: (i, 0)), _full((1, D))] + [ANY] * len(deps),
        out_specs=pl.BlockSpec((tr, D), lambda i: (i, 0)),
        out_shape=SDS((T, D), MXU_DTYPE), name=name, compiler_params=_cp("parallel"))(x, g, *deps)


def _rms_bwd(x, g, dh, dres, *, tr, name, deps=()):
    T, D = x.shape

    def body(x_ref, g_ref, dh_ref, dres_ref, dx_ref, dxb_ref, dg_ref):
        i = pl.program_id(0)
        xv = x_ref[...]
        r = lax.rsqrt(jnp.mean(xv * xv, axis=-1, keepdims=True) + EPS)
        xh = xv * r
        dh = dh_ref[...]
        gy = dh * g_ref[...]
        dx = r * (gy - xh * jnp.mean(xh * gy, axis=-1, keepdims=True)) + dres_ref[...]
        dx_ref[...] = dx
        dxb_ref[...] = dx.astype(dxb_ref.dtype)

        @pl.when(i == 0)
        def _():
            dg_ref[...] = jnp.zeros_like(dg_ref)
        dg_ref[...] += jnp.sum(dh * xh, axis=0, keepdims=True)

    rows = pl.BlockSpec((tr, D), lambda i: (i, 0))
    return pl.pallas_call(
        _after(body, 4, deps), grid=(T // tr,), in_specs=[rows, _full((1, D)), rows, rows] + [ANY] * len(deps),
        out_specs=[rows, rows, _full((1, D))],
        out_shape=[SDS((T, D), F32), SDS((T, D), MXU_DTYPE), SDS((1, D), F32)],
        name=name, compiler_params=_cp("arbitrary"))(x, g, dh, dres, *deps)


def _loss_head(y, target, *, tr, name):
    T, D = y.shape

    def body(y_ref, t_ref, loss_ref, dy_ref, dyb_ref):
        i = pl.program_id(0)
        d = y_ref[...] - t_ref[...]
        dy = d * (1.0 / D)
        dy_ref[...] = dy
        dyb_ref[...] = dy.astype(dyb_ref.dtype)
        part = jnp.sum(jnp.sum(d * d, axis=1, keepdims=True), axis=0, keepdims=True) * (0.5 / D)

        @pl.when(i == 0)
        def _():
            loss_ref[...] = jnp.zeros_like(loss_ref)
        loss_ref[...] += jnp.broadcast_to(part, loss_ref.shape)

    rows = pl.BlockSpec((tr, D), lambda i: (i, 0))
    return pl.pallas_call(
        body, grid=(T // tr,), in_specs=[rows, rows],
        out_specs=[_full((1, 128)), rows, rows],
        out_shape=[SDS((1, 128), F32), SDS((T, D), F32), SDS((T, D), MXU_DTYPE)],
        name=name, compiler_params=_cp("arbitrary"))(y, target)


def _pool_lane_consts(shape):
    lane = _lane(shape)
    grp = lane // (POOL_WIDTH // 4)
    win = jnp.where(grp == 0, 2, jnp.where(grp == 1, 4, jnp.where(grp == 2, 8, 16)))
    return grp, win


def _pool_select(grp, s2, s4, s8, s16):
    return jnp.where(grp == 0, s2, jnp.where(grp == 1, s4, jnp.where(grp == 2, s8, s16)))


def _pool_diff(xe, row0, tr):
    s2 = xe + pltpu.roll(xe, 1, 0)
    s4 = s2 + pltpu.roll(s2, 2, 0)
    s8 = s4 + pltpu.roll(s4, 4, 0)
    s16 = s8 + pltpu.roll(s8, 8, 0)
    shape = (tr, POOL_WIDTH)
    grp, win = _pool_lane_consts(shape)
    sums = _pool_select(grp, s2[16:], s4[16:], s8[16:], s16[16:])
    t = row0 + _row(shape)
    cnt = jnp.minimum(t + 1, win).astype(F32)
    return sums / cnt - xe[16:]


def _pool_fwd(z, wbd, scale, *, tr, name):
    T = z.shape[0]
    hb = tr // 16

    def body(x_ref, xp_ref, w_ref, s_ref, o_ref):
        i = pl.program_id(0)
        halo = jnp.where(i == 0, 0.0, xp_ref[...])
        diff = _pool_diff(jnp.concatenate([halo, x_ref[...]], axis=0), i * tr, tr)
        o_ref[...] = (_dot(diff, w_ref[...]) * s_ref[...]).astype(o_ref.dtype)

    return pl.pallas_call(
        body, grid=(T // tr,),
        in_specs=[pl.BlockSpec((tr, POOL_WIDTH), lambda i: (i, 0)),
                  pl.BlockSpec((16, POOL_WIDTH), lambda i: (jnp.maximum(i * hb - 1, 0), 0)),
                  _full((POOL_WIDTH, POOL_WIDTH)), _full((1, POOL_WIDTH))],
        out_specs=pl.BlockSpec((tr, POOL_WIDTH), lambda i: (i, 0)),
        out_shape=SDS((T, POOL_WIDTH), MXU_DTYPE), name=name, compiler_params=_cp("parallel"))(z, z, wbd, scale)


def _pool_bwd_tile(i, n, tr, x, xprev, dpa, dpa_next, wbd, scale):
    halo = jnp.where(i == 0, 0.0, xprev)
    diff = _pool_diff(jnp.concatenate([halo, x], axis=0), i * tr, tr)
    mixed = _dot(diff, wbd)
    dscale = jnp.sum(dpa * mixed, axis=0, keepdims=True)
    dnext = jnp.where(i == n - 1, 0.0, dpa_next)
    dmix_e = jnp.concatenate([dpa, dnext], axis=0) * scale
    ddiff_e = _dot(dmix_e, wbd, NT)
    dwbd = _dot(diff, dmix_e[:tr], TN)
    shape = (tr + 16, POOL_WIDTH)
    grp, win = _pool_lane_consts(shape)
    t = i * tr + _row(shape)
    e = ddiff_e / jnp.minimum(t + 1, win).astype(F32)
    nrow = tr + 16
    a2 = e + pltpu.roll(e, nrow - 1, 0)
    a4 = a2 + pltpu.roll(a2, nrow - 2, 0)
    a8 = a4 + pltpu.roll(a4, nrow - 4, 0)
    a16 = a8 + pltpu.roll(a8, nrow - 8, 0)
    dx = _pool_select(grp, a2, a4, a8, a16)[:tr] - ddiff_e[:tr]
    return dx, dwbd, dscale


def _norm_rope(x, g, cos, sin_signed, seg):
    reps = x.shape[1] // 128
    ms = _split_dot(x * x, seg) * (1.0 / HEAD_DIM)
    r = lax.rsqrt(ms + EPS)
    xn = x * r * g
    c, s = _tile_lanes(cos, reps), _tile_lanes(sin_signed, reps)
    return xn * c + _swap_halves(xn) * s


def _norm_rope_bwd(x, g, cos, sin_signed, seg, dout):
    reps = x.shape[1] // 128
    c, s = _tile_lanes(cos, reps), _tile_lanes(sin_signed, reps)
    dxn = dout * c + _swap_halves(dout * s)
    ms = _split_dot(x * x, seg) * (1.0 / HEAD_DIM)
    r = lax.rsqrt(ms + EPS)
    xh = x * r
    gy = dxn * g
    dx = r * (gy - xh * (_split_dot(xh * gy, seg) * (1.0 / HEAD_DIM)))
    dg = jnp.sum(dxn * xh, axis=0, keepdims=True)
    return dx, dg


def _dup_heads(k):
    first = _lane(k.shape) < HEAD_DIM
    kr = pltpu.roll(k, HEAD_DIM, 1)
    return jnp.concatenate([jnp.where(first, k, kr), jnp.where(first, kr, k)], axis=1)


def _qkv_prep(z, cos, sin_signed, gq, gk, seg, *, tr, name):
    T = z.shape[0]

    def body(qa_ref, qb_ref, kv_ref, c_ref, s_ref, gq_ref, gk_ref, seg_ref, q_ref, k_ref, v_ref):
        c, s, seg_m = c_ref[...], s_ref[...], seg_ref[...]
        scale = HEAD_DIM ** -0.5
        qa = _norm_rope(qa_ref[...], gq_ref[...], c, s, seg_m) * scale
        qb = _norm_rope(qb_ref[...], gq_ref[...], c, s, seg_m) * scale
        q_ref[...] = jnp.concatenate([qa, qb], axis=1).astype(q_ref.dtype)
        kv = kv_ref[...]
        k = _norm_rope(kv[:, :KV_WIDTH], gk_ref[...], c, s, seg_m[:128, :128])
        k_ref[...] = _dup_heads(k).astype(k_ref.dtype)
        v_ref[...] = _dup_heads(kv[:, KV_WIDTH:]).astype(v_ref.dtype)

    col = lambda j: pl.BlockSpec((tr, 256), lambda i: (i, j))
    tab = pl.BlockSpec((tr, 128), lambda i: (i, 0))
    return pl.pallas_call(
        body, grid=(T // tr,),
        in_specs=[col(1), col(2), col(3), tab, tab, _full((1, 256)), _full((1, 128)), _full((256, 256))],
        out_specs=[pl.BlockSpec((tr, 512), lambda i: (i, 0)), col(0), col(0)],
        out_shape=[SDS((T, 512), MXU_DTYPE), SDS((T, 256), MXU_DTYPE), SDS((T, 256), MXU_DTYPE)],
        name=name, compiler_params=_cp("parallel"))(z, z, z, cos, sin_signed, gq, gk, seg)


GROUP_HEADS = 4
GROUP_ROWS = GROUP_HEADS * ATTN_BLOCK


def _attn_mask(n):
    qi = _row((GROUP_ROWS, 2 * ATTN_BLOCK)) % ATTN_BLOCK
    kj = _lane((GROUP_ROWS, 2 * ATTN_BLOCK))
    return (kj > qi) & (kj <= qi + ATTN_BLOCK) & ((kj >= ATTN_BLOCK) | (n > 0))


def _stack_heads(x, g):
    first = _lane((ATTN_BLOCK, 128)) < HEAD_DIM
    parts = []
    for pair in (2 * g, 2 * g + 1):
        x128 = x[:, 128 * pair:128 * (pair + 1)]
        zero = jnp.zeros_like(x128)
        parts += [jnp.where(first, x128, zero), jnp.where(first, zero, x128)]
    return jnp.concatenate(parts, axis=0)


def _unstack_heads(y):
    first = _lane((ATTN_BLOCK, 128)) < HEAD_DIM
    b = ATTN_BLOCK
    return jnp.concatenate([jnp.where(first, y[0:b], y[b:2 * b]), jnp.where(first, y[2 * b:3 * b], y[3 * b:4 * b])],
                           axis=1)


def _sink_col(sk_ref, g):
    return jnp.concatenate([jnp.broadcast_to(sk_ref[h:h + 1, 0:1], (ATTN_BLOCK, 1))
                            for h in range(GROUP_HEADS * g, GROUP_HEADS * (g + 1))], axis=0)


def _group_exp(q4, kg, mask, sink):
    s = _dot(q4, kg, NT)
    s = jnp.where(mask, s, NEG_BIG)
    m = jnp.maximum(jnp.max(s, axis=1, keepdims=True), sink)
    p = jnp.exp(s - m)
    ps = jnp.exp(sink - m)
    return p, ps, 1.0 / (jnp.sum(p, axis=1, keepdims=True) + ps)


def _group_probs(q4, kg, mask, sink):
    p, ps, inv = _group_exp(q4, kg, mask, sink)
    return p * inv, ps * inv


def _attn_fwd(q, k, v, sinks_b, *, name):
    T = q.shape[0]
    nb = T // ATTN_BLOCK

    def body(q_ref, kc_ref, kp_ref, vc_ref, vp_ref, sk_ref, o_ref):
        n = pl.program_id(0)
        mask = _attn_mask(n)
        k2 = jnp.concatenate([kp_ref[...], kc_ref[...]], axis=0)
        v2 = jnp.concatenate([vp_ref[...], vc_ref[...]], axis=0)
        qv = q_ref[...]
        outs = []
        for g in range(2):
            kg = k2[:, 128 * g:128 * (g + 1)]
            vg = v2[:, 128 * g:128 * (g + 1)]
            p, _, inv = _group_exp(_stack_heads(qv, g), kg, mask, _sink_col(sk_ref, g))
            outs.append(_unstack_heads(_dot(p, vg) * inv))
        o_ref[...] = jnp.concatenate(outs, axis=1).astype(o_ref.dtype)

    cur = lambda w: pl.BlockSpec((ATTN_BLOCK, w), lambda n: (n, 0))
    prev = lambda w: pl.BlockSpec((ATTN_BLOCK, w), lambda n: (jnp.maximum(n - 1, 0), 0))
    return pl.pallas_call(
        body, grid=(nb,),
        in_specs=[cur(512), cur(256), prev(256), cur(256), prev(256), _full((8, 128))],
        out_specs=cur(512), out_shape=SDS((T, 512), MXU_DTYPE), name=name,
        compiler_params=_cp("parallel"))(q, k, k, v, v, sinks_b)


def _attn_bwd(q, k, v, sinks_b, do, *, name):
    T = q.shape[0]
    nb = T // ATTN_BLOCK

    def body(q_ref, kc_ref, kp_ref, vc_ref, vp_ref, sk_ref, do_ref,
             dq_ref, dkc_ref, dkp_ref, dvc_ref, dvp_ref, dsk_ref):
        n = pl.program_id(0)
        mask = _attn_mask(n)
        k2 = jnp.concatenate([kp_ref[...], kc_ref[...]], axis=0)
        v2 = jnp.concatenate([vp_ref[...], vc_ref[...]], axis=0)
        qv = q_ref[...]
        dov = do_ref[...]

        @pl.when(n == 0)
        def _():
            dsk_ref[...] = jnp.zeros_like(dsk_ref)

        dqs, dks, dvs = [], [], []
        for g in range(2):
            kg = k2[:, 128 * g:128 * (g + 1)]
            vg = v2[:, 128 * g:128 * (g + 1)]
            q4 = _stack_heads(qv, g)
            do4 = _stack_heads(dov, g)
            pn, psn = _group_probs(q4, kg, mask, _sink_col(sk_ref, g))
            o2 = _dot(pn, vg)
            delta = jnp.sum(do4 * o2, axis=1, keepdims=True)
            ds = pn * (_dot(do4, vg, NT) - delta)
            dqs.append(_unstack_heads(_dot(ds, kg)))
            dks.append(_dot(ds, q4, TN))
            dvs.append(_dot(pn, do4, TN))
            wsink = psn * delta
            for j in range(GROUP_HEADS):
                h = GROUP_HEADS * g + j
                dsink = -jnp.sum(wsink[ATTN_BLOCK * j:ATTN_BLOCK * (j + 1)], axis=0, keepdims=True)
                dsk_ref[h:h + 1, :] += jnp.broadcast_to(dsink, (1, 128))
        dq_ref[...] = jnp.concatenate(dqs, axis=1)
        dk = jnp.concatenate(dks, axis=1)
        dv = jnp.concatenate(dvs, axis=1)
        dkp_ref[...] = dk[:ATTN_BLOCK]
        dkc_ref[...] = dk[ATTN_BLOCK:]
        dvp_ref[...] = dv[:ATTN_BLOCK]
        dvc_ref[...] = dv[ATTN_BLOCK:]

    cur = lambda w: pl.BlockSpec((ATTN_BLOCK, w), lambda n: (n, 0))
    prev = lambda w: pl.BlockSpec((ATTN_BLOCK, w), lambda n: (jnp.maximum(n - 1, 0), 0))
    f = lambda w: SDS((T, w), F32)
    return pl.pallas_call(
        body, grid=(nb,),
        in_specs=[cur(512), cur(256), prev(256), cur(256), prev(256), _full((8, 128)), cur(512)],
        out_specs=[cur(512), cur(256), cur(256), cur(256), cur(256), _full((8, 128))],
        out_shape=[f(512), f(256), f(256), f(256), f(256), SDS((8, 128), F32)],
        name=name, compiler_params=_cp("arbitrary"))(q, k, k, v, v, sinks_b, do)


def _mixer_ab_bwd(z, cos, sin_signed, gq, gk, seg, dq, dkc, dkp, dvc, dvp, dpa, wbd, scale, dz, *, tr, name, deps=()):
    T = z.shape[0]
    n = T // tr
    hb = tr // 16
    ab = tr // ATTN_BLOCK

    def unfold(cur, nxt_tile, nxt_halo, i):
        nxt = jnp.concatenate([nxt_tile[ATTN_BLOCK:], jnp.where(i == n - 1, 0.0, nxt_halo)], axis=0)
        tot = cur + nxt
        first = _lane((tr, 128)) < HEAD_DIM
        a = tot[:, :128]
        b = tot[:, 128:]
        a = a + pltpu.roll(a, HEAD_DIM, 1)
        b = b + pltpu.roll(b, HEAD_DIM, 1)
        return jnp.where(first, a, b)

    def body(xp_ref, xpp_ref, qa_ref, qb_ref, kv_ref, c_ref, s_ref, gq_ref, gk_ref, seg_ref,
             dq_ref, dkc_ref, dkp_ref, dkh_ref, dvc_ref, dvp_ref, dvh_ref, dpa_ref, dpan_ref, w_ref, sc_ref, _dz_in,
             dz_ref, dgq_ref, dgk_ref, dw_ref, dsc_ref):
        i = pl.program_id(0)
        c, s, seg_m = c_ref[...], s_ref[...], seg_ref[...]
        scale_q = HEAD_DIM ** -0.5
        dqv = dq_ref[...] * scale_q
        dxa, dga = _norm_rope_bwd(qa_ref[...], gq_ref[...], c, s, seg_m, dqv[:, :256])
        dxb, dgb = _norm_rope_bwd(qb_ref[...], gq_ref[...], c, s, seg_m, dqv[:, 256:])
        dk = unfold(dkc_ref[...], dkp_ref[...], dkh_ref[...], i)
        dv = unfold(dvc_ref[...], dvp_ref[...], dvh_ref[...], i)
        kv = kv_ref[...]
        dxk, dgk = _norm_rope_bwd(kv[:, :KV_WIDTH], gk_ref[...], c, s, seg_m[:128, :128], dk)
        dxp, dwbd, dscale = _pool_bwd_tile(i, n, tr, xp_ref[...], xpp_ref[...], dpa_ref[...], dpan_ref[...],
                                           w_ref[...], sc_ref[...])
        dz_ref[...] = jnp.concatenate([dxp, dxa, dxb, dxk, dv], axis=1).astype(dz_ref.dtype)

        @pl.when(i == 0)
        def _():
            dgq_ref[...] = jnp.zeros_like(dgq_ref)
            dgk_ref[...] = jnp.zeros_like(dgk_ref)
            dw_ref[...] = jnp.zeros_like(dw_ref)
            dsc_ref[...] = jnp.zeros_like(dsc_ref)
        dgq_ref[...] += _fold_lanes(dga + dgb, HEAD_DIM)
        dgk_ref[...] += _fold_lanes(dgk, HEAD_DIM)
        dw_ref[...] += dwbd
        dsc_ref[...] += dscale

    col = lambda j: pl.BlockSpec((tr, 256), lambda i: (i, j))
    rows = lambda w: pl.BlockSpec((tr, w), lambda i: (i, 0))
    nxt_blk = pl.BlockSpec((ATTN_BLOCK, 256), lambda i: (jnp.minimum((i + 1) * ab, T // ATTN_BLOCK - 1), 0))
    prev16 = pl.BlockSpec((16, 256), lambda i: (jnp.maximum(i * hb - 1, 0), 0))
    next16 = pl.BlockSpec((16, 256), lambda i: (jnp.minimum((i + 1) * hb, T // 16 - 1), 0))
    return pl.pallas_call(
        _after(body, 22, deps), grid=(n,),
        in_specs=[col(0), prev16, col(1), col(2), col(3), rows(128), rows(128),
                  _full((1, 256)), _full((1, 128)), _full((256, 256)),
                  rows(512), rows(256), rows(256), nxt_blk, rows(256), rows(256), nxt_blk,
                  rows(256), next16, _full((256, 256)), _full((1, 256)), ANY] + [ANY] * len(deps),
        out_specs=[rows(1024), _full((1, 256)), _full((1, 128)), _full((256, 256)), _full((1, 256))],
        out_shape=[SDS((T, IN_COLS), MXU_DTYPE), SDS((1, 256), F32), SDS((1, 128), F32),
                   SDS((256, 256), F32), SDS((1, 256), F32)],
        input_output_aliases={21: 0}, name=name, compiler_params=_cp("arbitrary"))(
            z, z, z, z, z, cos, sin_signed, gq, gk, seg, dq, dkc, dkp, dkp, dvc, dvp, dvp, dpa, dpa, wbd, scale, dz,
            *deps)


def _sgu_common(zu, zv, vn, seg):
    u, du = _gelu_and_grad(zu)
    gv, dgv = _gelu_and_grad(zv)
    ms = _split_dot(gv * gv, seg) * (1.0 / HEAD_DIM)
    r = lax.rsqrt(ms + EPS)
    xh = gv * r
    return u, du, dgv, r, xh, xh * vn


def _sgu_fwd(z, wtril, bexp, vn, seg, *, tr, name):
    T = z.shape[0]
    nch = tr // CHUNK

    def body(u_ref, v_ref, w_ref, b_ref, vn_ref, seg_ref, o_ref):
        u, _, _, _, _, vg = _sgu_common(u_ref[...], v_ref[...], vn_ref[...], seg_ref[...])
        grp = _lane((CHUNK, SGU_WIDTH)) // HEAD_DIM
        outs = []
        for ch in range(nch):
            vc = vg[ch * CHUNK:(ch + 1) * CHUNK]
            s = b_ref[...]
            for g in range(4):
                s = s + jnp.where(grp == g, _dot(w_ref[g], vc), 0.0)
            outs.append(u[ch * CHUNK:(ch + 1) * CHUNK] * s)
        o_ref[...] = jnp.concatenate(outs, axis=0).astype(o_ref.dtype)

    col = lambda j: pl.BlockSpec((tr, 256), lambda i: (i, j))
    return pl.pallas_call(
        body, grid=(T // tr,),
        in_specs=[col(4), col(5), _full((4, CHUNK, CHUNK)), _full((CHUNK, 256)), _full((1, 256)), _full((256, 256))],
        out_specs=col(0), out_shape=SDS((T, SGU_WIDTH), MXU_DTYPE), name=name,
        compiler_params=_cp("parallel"))(z, z, wtril, bexp, vn, seg)


def _sgu_bwd(z, wtril, bexp, vn, seg, dsg, dz, *, tr, name):
    T = z.shape[0]
    nch = tr // CHUNK

    def body(u_ref, v_ref, w_ref, b_ref, vn_ref, seg_ref, d_ref, _dz_in, dz_ref, dw_ref, db_ref, dvn_ref):
        i = pl.program_id(0)
        seg_m = seg_ref[...]
        vn_v = vn_ref[...]
        u, du, dgv, r, xh, vg = _sgu_common(u_ref[...], v_ref[...], vn_v, seg_m)
        d = d_ref[...]
        grp = _lane((CHUNK, SGU_WIDTH)) // HEAD_DIM
        tril = _row((CHUNK, CHUNK)) >= _lane((CHUNK, CHUNK))

        @pl.when(i == 0)
        def _():
            dw_ref[...] = jnp.zeros_like(dw_ref)
            db_ref[...] = jnp.zeros_like(db_ref)
            dvn_ref[...] = jnp.zeros_like(dvn_ref)

        dus, dvgs = [], []
        for ch in range(nch):
            sl = slice(ch * CHUNK, (ch + 1) * CHUNK)
            vc = vg[sl]
            s = b_ref[...]
            for g in range(4):
                s = s + jnp.where(grp == g, _dot(w_ref[g], vc), 0.0)
            dus.append(d[sl] * s)
            ds = d[sl] * u[sl]
            db_ref[...] += _split_dot(ds, seg_m)
            dvg = jnp.zeros((CHUNK, SGU_WIDTH), F32)
            for g in range(4):
                dsm = jnp.where(grp == g, ds, 0.0)
                dvg = dvg + jnp.where(grp == g, _dot(w_ref[g], ds, TN), 0.0)
                dw_ref[g] += jnp.where(tril, _dot(dsm, vc, NT), 0.0)
            dvgs.append(dvg)
        dup = jnp.concatenate(dus, axis=0)
        dvg = jnp.concatenate(dvgs, axis=0)
        dvn_ref[...] += _fold_lanes(jnp.sum(dvg * xh, axis=0, keepdims=True), HEAD_DIM)
        gy = dvg * vn_v
        dgvv = r * (gy - xh * (_split_dot(xh * gy, seg_m) * (1.0 / HEAD_DIM)))
        dz_ref[...] = jnp.concatenate([dup * du, dgvv * dgv], axis=1).astype(dz_ref.dtype)

    col = lambda j: pl.BlockSpec((tr, 256), lambda i: (i, j))
    return pl.pallas_call(
        body, grid=(T // tr,),
        in_specs=[col(4), col(5), _full((4, CHUNK, CHUNK)), _full((CHUNK, 256)), _full((1, 256)), _full((256, 256)),
                  col(0), ANY],
        out_specs=[pl.BlockSpec((tr, 512), lambda i: (i, 2)), _full((4, CHUNK, CHUNK)), _full((CHUNK, 256)),
                   _full((1, 256))],
        out_shape=[SDS((T, IN_COLS), MXU_DTYPE), SDS((4, CHUNK, CHUNK), F32), SDS((CHUNK, 256), F32),
                   SDS((1, 256), F32)],
        input_output_aliases={7: 0}, name=name, compiler_params=_cp("arbitrary"))(
            z, z, wtril, bexp, vn, seg, dsg, dz)


def _merge_fwd(pa, at, sg, wa, wb, wc, z, *, tm, tn, name):
    T = pa.shape[0]
    gb = GATE_COL0 // tn
    nb = D_MODEL // tn

    def body(pa_ref, at_ref, sg_ref, wa_ref, wb_ref, wc_ref, g0_ref, g1_ref, g2_ref, m_ref, y_ref):
        acc = None
        for idx, (op_ref, w_ref, g_ref) in enumerate(((pa_ref, wa_ref, g0_ref), (at_ref, wb_ref, g1_ref),
                                                      (sg_ref, wc_ref, g2_ref))):
            y = _dot(op_ref[...], w_ref[...])
            y_ref[idx] = y
            t = _sigmoid(g_ref[...]) * y
            acc = t if acc is None else acc + t
        m_ref[...] = acc.astype(m_ref.dtype)

    op = lambda w: pl.BlockSpec((tm, w), lambda i, j: (i, 0))
    wt = lambda k: pl.BlockSpec((k, tn), lambda i, j: (0, j))
    gate = lambda b: pl.BlockSpec((tm, tn), lambda i, j: (i, gb + b * nb + j))
    return pl.pallas_call(
        body, grid=(T // tm, nb),
        in_specs=[op(256), op(512), op(256), wt(256), wt(512), wt(256), gate(0), gate(1), gate(2)],
        out_specs=[pl.BlockSpec((tm, tn), lambda i, j: (i, j)), pl.BlockSpec((3, tm, tn), lambda i, j: (0, i, j))],
        out_shape=[SDS((T, D_MODEL), MXU_DTYPE), SDS((3, T, D_MODEL), F32)],
        name=name, compiler_params=_cp("parallel", "parallel"))(pa, at, sg, wa, wb, wc, z, z, z)


def _merge_bwd(dm, y, z, *, tr, tn, name):
    T = dm.shape[0]
    gb = GATE_COL0 // tn
    nb = D_MODEL // tn

    def body(dm_ref, y_ref, g_ref, dy_ref, dz_ref):
        g = _sigmoid(g_ref[...])
        d = dm_ref[...]
        dy_ref[...] = (d * g).astype(dy_ref.dtype)
        dz_ref[...] = (d * y_ref[...] * g * (1.0 - g)).astype(dz_ref.dtype)

    return pl.pallas_call(
        body, grid=(T // tr, 3, nb),
        in_specs=[pl.BlockSpec((tr, tn), lambda i, b, j: (i, j)),
                  pl.BlockSpec((None, tr, tn), lambda i, b, j: (b, i, j)),
                  pl.BlockSpec((tr, tn), lambda i, b, j: (i, gb + b * nb + j))],
        out_specs=[pl.BlockSpec((None, tr, tn), lambda i, b, j: (b, i, j)),
                   pl.BlockSpec((tr, tn), lambda i, b, j: (i, gb + b * nb + j))],
        out_shape=[SDS((3, T, D_MODEL), MXU_DTYPE), SDS((T, IN_COLS), MXU_DTYPE)],
        name=name, compiler_params=_cp("parallel", "parallel", "parallel"))(dm, y, z)


def _conv3(xe, w, b):
    return (w[0:1] * pltpu.roll(xe, 2, 0) + w[1:2] * pltpu.roll(xe, 1, 0) + w[2:3] * xe)[8:] + b


def _conv_act_fwd(up, cw, cb, *, tr, tc, name):
    T = up.shape[0]
    nc = D_FF // tc
    hb = tr // 8

    def body(ug_ref, ugp_ref, uv_ref, uvp_ref, wg_ref, wv_ref, bg_ref, bv_ref, o_ref):
        i = pl.program_id(1)
        first = i == 0
        cg = _conv3(jnp.concatenate([jnp.where(first, 0.0, ugp_ref[...]), ug_ref[...]], axis=0), wg_ref[...], bg_ref[...])
        cv = _conv3(jnp.concatenate([jnp.where(first, 0.0, uvp_ref[...]), uv_ref[...]], axis=0), wv_ref[...], bv_ref[...])
        o_ref[...] = (cg * _sigmoid(cg) * cv).astype(o_ref.dtype)

    tile = lambda off: pl.BlockSpec((tr, tc), lambda j, i: (i, off + j))
    prev = lambda off: pl.BlockSpec((8, tc), lambda j, i: (jnp.maximum(i * hb - 1, 0), off + j))
    par = lambda rows, off: pl.BlockSpec((rows, tc), lambda j, i: (0, off + j))
    return pl.pallas_call(
        body, grid=(nc, T // tr),
        in_specs=[tile(0), prev(0), tile(nc), prev(nc), par(3, 0), par(3, nc), par(1, 0), par(1, nc)],
        out_specs=pl.BlockSpec((tr, tc), lambda j, i: (i, j)),
        out_shape=SDS((T, D_FF), MXU_DTYPE), name=name,
        compiler_params=_cp("parallel", "parallel"))(up, up, up, up, cw, cw, cb, cb)


def _conv_act_bwd(up, cw, cb, dact, *, tr, tc, name, deps=()):
    T = up.shape[0]
    nc = D_FF // tc
    hb = tr // 8
    nr = T // tr

    def body(ug_ref, ugp_ref, ugn_ref, uv_ref, uvp_ref, uvn_ref, da_ref, dan_ref, wg_ref, wv_ref, bg_ref, bv_ref,
             du_ref, dwg_ref, dwv_ref, dbg_ref, dbv_ref):
        i = pl.program_id(1)
        first, last = i == 0, i == nr - 1
        da = jnp.concatenate([da_ref[...], jnp.where(last, 0.0, dan_ref[...])], axis=0)
        uge = jnp.concatenate([jnp.where(first, 0.0, ugp_ref[...]), ug_ref[...], ugn_ref[...]], axis=0)
        uve = jnp.concatenate([jnp.where(first, 0.0, uvp_ref[...]), uv_ref[...], uvn_ref[...]], axis=0)
        wg, wv = wg_ref[...], wv_ref[...]
        cg = _conv3(uge, wg, bg_ref[...])
        cv = _conv3(uve, wv, bv_ref[...])
        sg = _sigmoid(cg)
        dcg = da * cv * (sg * (1.0 + cg * (1.0 - sg)))
        dcv = da * (cg * sg)
        nrow = tr + 8

        def back(dc, w):
            return (w[2:3] * dc + w[1:2] * pltpu.roll(dc, nrow - 1, 0) + w[0:1] * pltpu.roll(dc, nrow - 2, 0))[:tr]

        du_ref[0] = back(dcg, wg).astype(du_ref.dtype)
        du_ref[1] = back(dcv, wv).astype(du_ref.dtype)

        def wgrad(dc, ue):
            d = dc[:tr]
            rows = [jnp.sum(d * pltpu.roll(ue, 2, 0)[8:8 + tr], axis=0, keepdims=True),
                    jnp.sum(d * pltpu.roll(ue, 1, 0)[8:8 + tr], axis=0, keepdims=True),
                    jnp.sum(d * ue[8:8 + tr], axis=0, keepdims=True)]
            return jnp.concatenate(rows, axis=0), jnp.sum(d, axis=0, keepdims=True)

        dwg, dbg = wgrad(dcg, uge)
        dwv, dbv = wgrad(dcv, uve)

        @pl.when(first)
        def _():
            dwg_ref[...] = jnp.zeros_like(dwg_ref)
            dwv_ref[...] = jnp.zeros_like(dwv_ref)
            dbg_ref[...] = jnp.zeros_like(dbg_ref)
            dbv_ref[...] = jnp.zeros_like(dbv_ref)
        dwg_ref[...] += dwg
        dwv_ref[...] += dwv
        dbg_ref[...] += dbg
        dbv_ref[...] += dbv

    tile = lambda off: pl.BlockSpec((tr, tc), lambda j, i: (i, off + j))
    prev = lambda off: pl.BlockSpec((8, tc), lambda j, i: (jnp.maximum(i * hb - 1, 0), off + j))
    nxt = lambda off: pl.BlockSpec((8, tc), lambda j, i: (jnp.minimum((i + 1) * hb, T // 8 - 1), off + j))
    par = lambda rows, off: pl.BlockSpec((rows, tc), lambda j, i: (0, off + j))
    acc = lambda rows: pl.BlockSpec((rows, tc), lambda j, i: (0, j))
    return pl.pallas_call(
        _after(body, 12, deps), grid=(nc, nr),
        in_specs=[tile(0), prev(0), nxt(0), tile(nc), prev(nc), nxt(nc), tile(0), nxt(0),
                  par(3, 0), par(3, nc), par(1, 0), par(1, nc)] + [ANY] * len(deps),
        out_specs=[pl.BlockSpec((2, tr, tc), lambda j, i: (0, i, j)), acc(3), acc(3), acc(1), acc(1)],
        out_shape=[SDS((2, T, D_FF), MXU_DTYPE), SDS((3, D_FF), F32), SDS((3, D_FF), F32),
                   SDS((1, D_FF), F32), SDS((1, D_FF), F32)],
        name=name, compiler_params=_cp("parallel", "arbitrary"))(
            up, up, up, up, up, up, dact, dact, cw, cw, cb, cb, *deps)


def _row_tile(rows, cap):
    t = min(cap, rows)
    t -= t % 8
    while rows % t:
        t -= 8
    return t


def _adamw(w, g, m, v, *, tr, name):
    R, C = w.shape
    assert R % tr == 0, (R, tr)

    def body(w_ref, g_ref, m_ref, v_ref, d_ref, nm_ref, nv_ref):
        gv = g_ref[...]
        mn = ADAM_B1 * m_ref[...] + (1.0 - ADAM_B1) * gv
        vn = ADAM_B2 * v_ref[...] + (1.0 - ADAM_B2) * (gv * gv)
        m_hat = mn / (1.0 - ADAM_B1 ** ADAM_STEP)
        v_hat = vn / (1.0 - ADAM_B2 ** ADAM_STEP)
        d_ref[...] = -ADAM_LR * (m_hat / (jnp.sqrt(v_hat) + ADAM_EPS) + ADAM_WD * w_ref[...])
        nm_ref[...] = mn
        nv_ref[...] = vn

    rows = pl.BlockSpec((tr, C), lambda i: (i, 0))
    return pl.pallas_call(
        body, grid=(R // tr,), in_specs=[rows] * 4, out_specs=[rows] * 3,
        out_shape=[SDS((R, C), F32)] * 3, name=name, compiler_params=_cp("parallel"))(w, g, m, v)


def _sum_slots(r, *, tr, name):
    S, R, C = r.shape
    assert R % tr == 0, (R, tr)

    def body(r_ref, o_ref):
        acc = r_ref[0]
        for s in range(1, S):
            acc = acc + r_ref[s]
        o_ref[...] = acc

    return pl.pallas_call(
        body, grid=(R // tr,), in_specs=[pl.BlockSpec((S, tr, C), lambda i: (0, i, 0))],
        out_specs=pl.BlockSpec((tr, C), lambda i: (i, 0)), out_shape=SDS((R, C), F32),
        name=name, compiler_params=_cp("parallel"))(r)


def _pair_add(g4, h, pos, *, name):
    A, _, r, C = g4.shape
    cs = C if A == N_CHIPS else C // N_CHIPS
    tr = _row_tile(r, 256)
    if A == N_CHIPS:
        g_map, h_map = (lambda t, i, pos: (t, pos[1], i, 0)), (lambda t, i, pos: (t, i, 0))
    else:
        g_map, h_map = (lambda t, i, pos: (0, pos[1], i, t)), (lambda t, i, pos: (0, i, t))

    def body(pos_ref, g_ref, h_ref, o_ref):
        o_ref[...] = (g_ref[...] + h_ref[...]).astype(o_ref.dtype)

    grid_spec = pltpu.PrefetchScalarGridSpec(
        num_scalar_prefetch=1, grid=(N_CHIPS, r // tr),
        in_specs=[pl.BlockSpec((None, None, tr, cs), g_map), pl.BlockSpec((None, tr, cs), h_map)],
        out_specs=pl.BlockSpec((None, tr, cs), lambda t, i, pos: (t, i, 0)))
    return pl.pallas_call(body, grid_spec=grid_spec, out_shape=SDS((N_CHIPS, r, cs), COMM_DTYPE), name=name,
                          compiler_params=_cp("parallel", "parallel"))(pos, g4, h)


def _chip_sum(p, r2, f_into, pos, layer, *, name):
    _, r, cs = p.shape
    tr = _row_tile(r, 256)

    def body(pos_ref, own_ref, r_ref, *rest):
        o_ref = rest[-1]
        o_ref[...] = ((own_ref[...].astype(F32) + r_ref[0].astype(F32)) + r_ref[1].astype(F32)) + r_ref[2].astype(F32)

    in_specs = [pl.BlockSpec((None, tr, cs), lambda i, pos: (pos[0], i, 0)),
                pl.BlockSpec((3, tr, cs), lambda i, pos: (0, i, 0))]
    operands = [pos, p, r2]
    aliases = {}
    if f_into is not None:
        in_specs.append(ANY)
        operands.append(f_into)
        aliases = {3: 0}
    grid_spec = pltpu.PrefetchScalarGridSpec(
        num_scalar_prefetch=1, grid=(r // tr,), in_specs=in_specs,
        out_specs=pl.BlockSpec((None, None, tr, cs), lambda i, pos: (layer, pos[1], i, 0)))
    return pl.pallas_call(body, grid_spec=grid_spec, out_shape=SDS((DEPTH, 2, r, cs), F32), name=name,
                          input_output_aliases=aliases, compiler_params=_cp("parallel"))(*operands)


def _mesh_pos():
    return lax.axis_index("x"), lax.axis_index("y"), lax.axis_index("c")


HBM = pl.BlockSpec(memory_space=pltpu.HBM)
SEM = pl.BlockSpec(memory_space=pltpu.SEMAPHORE)
DATAFLOW = pltpu.SideEffectType.DATAFLOW_SIDE_EFFECTING
CHIP_FLIPS = (2, 1, 3)


def _chip_peers():
    x, y, c = _mesh_pos()
    return 2 * x + y, [(1 - x, y, c), (x, 1 - y, c), (1 - x, 1 - y, c)], (x, y, 1 - c), c


def _split_start(arrays, n_copies, issue, *, name, deps=()):
    k = len(arrays)
    nd = len(deps)

    def body(*refs):
        issue(refs[:k], refs[k + nd], refs[k + nd + 1])
        refs[2 * k + nd + 2][...] = jnp.zeros((8, 128), F32)

    out = pl.pallas_call(
        body, name=name,
        out_shape=(pltpu.SemaphoreType.DMA((n_copies,)), pltpu.SemaphoreType.DMA((n_copies,)),
                   *[pltpu.HBM(a.shape, a.dtype) for a in arrays], SDS((8, 128), F32)),
        in_specs=[HBM] * k + [ANY] * nd, out_specs=(SEM, SEM, *[HBM] * k, pl.BlockSpec(memory_space=pltpu.VMEM)),
        input_output_aliases={i: 2 + i for i in range(k)},
        compiler_params=pltpu.CompilerParams(has_side_effects=DATAFLOW))(
            *[pltpu.with_memory_space_constraint(a, pltpu.HBM) for a in arrays], *deps)
    return (out[0], out[1]), list(out[2:2 + k]), out[2 + k]


def _split_wait(sems, arrays, after, waits, *, name):
    k = len(arrays)

    def body(*refs):
        waits(refs[:k], refs[k], refs[k + 1])

    out = pl.pallas_call(
        body, name=name, out_shape=tuple(pltpu.HBM(a.shape, a.dtype) for a in arrays),
        in_specs=[HBM] * k + [SEM, SEM, ANY], out_specs=tuple([HBM] * k),
        input_output_aliases={i: i for i in range(k)},
        compiler_params=pltpu.CompilerParams(has_side_effects=DATAFLOW))(*arrays, sems[0], sems[1], after)
    return list(out)


def _wait_both(cp):
    cp.wait_send()
    cp.wait_recv()


def _cast_place(shard, pos, dtype, *, name):
    R, C = shard.shape
    tr = R if R % 8 else _row_tile(R, 256)

    def body(pos_ref, x_ref, o_ref):
        o_ref[...] = x_ref[...].astype(o_ref.dtype)

    grid_spec = pltpu.PrefetchScalarGridSpec(
        num_scalar_prefetch=1, grid=(R // tr,),
        in_specs=[pl.BlockSpec((tr, C), lambda i, pos: (i, 0))],
        out_specs=pl.BlockSpec((None, tr, C), lambda i, pos: (pos[0], i, 0)))
    return pl.pallas_call(body, grid_spec=grid_spec, out_shape=SDS((N_CHIPS, R, C), dtype), name=name,
                          compiler_params=_cp("parallel"))(pos, shard)


class _Gather:
    def __init__(self, lands, name, deps=()):
        n = len(lands)
        self.name = name

        def copies(refs, ss, rs):
            me, peers, _, _ = _chip_peers()
            return [pltpu.make_async_remote_copy(
                src_ref=refs[w].at[me], dst_ref=refs[w].at[me], send_sem=ss.at[3 * w + p], recv_sem=rs.at[3 * w + p],
                device_id=peers[p], device_id_type=MESH) for w in range(n) for p in range(3)]

        def issue(refs, ss, rs):
            for cp in copies(refs, ss, rs):
                cp.start()

        def waits(refs, ss, rs):
            for cp in copies(refs, ss, rs):
                _wait_both(cp)

        self._waits = waits
        self.sems, self.arrays, self.token = _split_start(list(lands), 3 * n, issue, name=name + "_start", deps=deps)

    def wait(self, after):
        return _split_wait(self.sems, self.arrays, after, self._waits, name=self.name + "_wait")


def _swap_halves_start(g4s, *, name):
    n = len(g4s)
    lands = [lax.empty((g.shape[0],) + g.shape[2:], g.dtype) for g in g4s]

    def copies(refs, ss, rs):
        _, _, sibling, c = _chip_peers()
        return [pltpu.make_async_remote_copy(
            src_ref=refs[w].at[:, 1 - c], dst_ref=refs[n + w], send_sem=ss.at[w], recv_sem=rs.at[w],
            device_id=sibling, device_id_type=MESH) for w in range(n)]

    def issue(refs, ss, rs):
        for cp in copies(refs, ss, rs):
            cp.start()

    def waits(refs, ss, rs):
        for cp in copies(refs, ss, rs):
            _wait_both(cp)

    sems, arrays, token = _split_start(list(g4s) + lands, n, issue, name=name + "_start")
    return sems, arrays, token, waits


def _scatter_start(parts, *, name):
    n = len(parts)
    lands = [lax.empty((3,) + p.shape[1:], p.dtype) for p in parts]

    def copies(refs, ss, rs):
        me, peers, _, _ = _chip_peers()
        return [pltpu.make_async_remote_copy(
            src_ref=refs[w].at[me ^ CHIP_FLIPS[p]], dst_ref=refs[n + w].at[p],
            send_sem=ss.at[3 * w + p], recv_sem=rs.at[3 * w + p], device_id=peers[p], device_id_type=MESH)
            for w in range(n) for p in range(3)]

    def issue(refs, ss, rs):
        for cp in copies(refs, ss, rs):
            cp.start()

    def waits(refs, ss, rs):
        for cp in copies(refs, ss, rs):
            _wait_both(cp)

    sems, arrays, token = _split_start(list(parts) + lands, 3 * n, issue, name=name + "_start")
    return sems, arrays, token, waits


def _pair_share_start(fs, layer, *, name):
    n = len(fs)

    def copies(refs, ss, rs):
        _, _, sibling, c = _chip_peers()
        return [pltpu.make_async_remote_copy(
            src_ref=refs[w].at[layer, c], dst_ref=refs[w].at[layer, c], send_sem=ss.at[w], recv_sem=rs.at[w],
            device_id=sibling, device_id_type=MESH) for w in range(n)]

    def issue(refs, ss, rs):
        for cp in copies(refs, ss, rs):
            cp.start()

    def waits(refs, ss, rs):
        for cp in copies(refs, ss, rs):
            _wait_both(cp)

    sems, arrays, token = _split_start(list(fs), n, issue, name=name + "_start")
    return sems, arrays, token, waits


def _allgather_devices(pack, *, name, deps=()):
    def body_all(in_ref, out_ref, send_sems, recv_sems, local_sem):
        x, y, c = _mesh_pos()
        me = 4 * x + 2 * y + c
        remotes = []
        for f in range(1, N_DEV):
            fx, fy, fc = (f >> 2) & 1, (f >> 1) & 1, f & 1
            peer = (x ^ fx, y ^ fy, c ^ fc)
            remotes.append(pltpu.make_async_remote_copy(
                src_ref=in_ref, dst_ref=out_ref.at[me], send_sem=send_sems.at[f - 1], recv_sem=recv_sems.at[f - 1],
                device_id=peer, device_id_type=MESH))
        mine = pltpu.make_async_copy(in_ref, out_ref.at[me], local_sem)
        mine.start()
        for cp in remotes:
            cp.start()
        mine.wait()
        for cp in remotes:
            cp.wait_send()
            cp.wait_recv()

    return pl.pallas_call(
        _after(body_all, 1, deps), in_specs=[ANY] * (1 + len(deps)), out_specs=ANY,
        out_shape=SDS((N_DEV,) + pack.shape, pack.dtype),
        scratch_shapes=[pltpu.SemaphoreType.DMA((N_DEV - 1,)), pltpu.SemaphoreType.DMA((N_DEV - 1,)),
                        pltpu.SemaphoreType.DMA],
        name=name, compiler_params=pltpu.CompilerParams(has_side_effects=True))(pack, *deps)


BIG = ('w_in', 'w_proj_a', 'w_proj_b', 'w_proj_c', 'w_out', 'w_up', 'w_down')
BIG_SHARD_AXIS = {'w_in': 2, 'w_proj_a': 2, 'w_proj_b': 2, 'w_proj_c': 2, 'w_out': 1, 'w_up': 2, 'w_down': 1}
SMALL = ('norm1', 'q_norm', 'k_norm', 'sinks', 'w_pool', 'pool_scale', 'sgu_v_norm', 'w_s', 'b_s', 'norm2',
         'conv_b', 'conv_w')
WEIGHTS = ('norm1', 'w_in', 'q_norm', 'k_norm', 'sinks', 'w_pool', 'pool_scale', 'sgu_v_norm', 'w_s', 'b_s',
           'w_proj_a', 'w_proj_b', 'w_proj_c', 'w_out', 'norm2', 'w_up', 'conv_w', 'conv_b', 'w_down')


def _rope_tables(positions):
    inv_freq = ROPE_THETA ** (-jnp.arange(0, HEAD_DIM, 2, dtype=F32) / HEAD_DIM)
    ang = positions.astype(F32)[:, None] * inv_freq
    cos, sin = jnp.cos(ang), jnp.sin(ang)
    c = jnp.concatenate([cos, cos], axis=1)
    s = jnp.concatenate([-sin, sin], axis=1)
    return jnp.concatenate([c, c], axis=1), jnp.concatenate([s, s], axis=1)


def _block_diag4(w):
    out = jnp.zeros((POOL_WIDTH, POOL_WIDTH), w.dtype)
    for g in range(4):
        out = lax.dynamic_update_slice(out, w[g], (g * HEAD_DIM, g * HEAD_DIM))
    return out


def _local_step(x, target, cos, sin, sp, sched):
    T = x.shape[0]
    tm1 = min(1024, T)
    tm = min(512, T)
    tr = min(256, T)
    tkt = min(1024, T)
    seg = _seg_matrix(256, HEAD_DIM)
    saved = []
    xl = x
    for l in range(DEPTH):
        p = f"l{l}_"
        c = dict(
            g1=sp['norm1'][l][None], g2=sp['norm2'][l][None],
            wbd=_block_diag4(sp['w_pool'][l]).astype(MXU_DTYPE), scale=sp['pool_scale'][l][None],
            gq=jnp.tile(sp['q_norm'][l], 4)[None], gk=jnp.tile(sp['k_norm'][l], 2)[None],
            sinks=jnp.broadcast_to(sp['sinks'][l][:, None], (N_Q_HEADS, 128)),
            wtril=jnp.tril(sp['w_s'][l]).astype(MXU_DTYPE),
            bexp=jnp.repeat(sp['b_s'][l].T, HEAD_DIM, axis=1), vn=jnp.tile(sp['sgu_v_norm'][l], 4)[None],
            cb=sp['conv_b'][l][None])
        h1 = _rms_fwd(xl, c['g1'], tr=tr, name=p + "rms1", deps=sched.start_tokens() if l == 0 else ())
        c['w_in'] = sched.weight('w_in', l, h1)
        z = _mm(h1, c['w_in'], mode='nn', b_sharded=True, tm=tm1, tn=1152, tk=D_MODEL, name=p + "in_proj")
        pa = _pool_fwd(z, c['wbd'], c['scale'], tr=tr, name=p + "pool")
        q, k, v = _qkv_prep(z, cos, sin, c['gq'], c['gk'], seg, tr=tr, name=p + "qkv_prep")
        at = _attn_fwd(q, k, v, c['sinks'], name=p + "attn")
        sg = _sgu_fwd(z, c['wtril'], c['bexp'], c['vn'], seg, tr=tr, name=p + "sgu")
        for n in ('w_proj_a', 'w_proj_b', 'w_proj_c', 'w_out', 'w_up', 'conv_w', 'w_down'):
            c[n] = sched.weight(n, l, sg)
        merged, y3 = _merge_fwd(pa, at, sg, c['w_proj_a'], c['w_proj_b'], c['w_proj_c'], z,
                                tm=tm, tn=512, name=p + "merge")
        x1 = _mm(merged, c['w_out'], mode='nn', add=xl, tm=tm, tn=D_MODEL, tk=D_MODEL, name=p + "out_proj")
        h2 = _rms_fwd(x1, c['g2'], tr=tr, name=p + "rms2")
        up = _mm(h2, c['w_up'], mode='nn', b_sharded=True, tm=tm1, tn=1408, tk=D_MODEL, name=p + "up_proj")
        act = _conv_act_fwd(up, c['conv_w'], c['cb'], tr=tr, tc=1408, name=p + "conv_act")
        x2 = _mm(act, c['w_down'], mode='nn', add=x1, tm=tm, tn=D_MODEL, tk=1408, name=p + "down_proj")
        saved.append(dict(c, x=xl, h1=h1, z=z, pa=pa, q=q, k=k, v=v, at=at, sg=sg, merged=merged, y3=y3,
                          x1=x1, h2=h2, up=up, act=act))
        xl = x2

    loss_row, dx, dxb = _loss_head(xl, target, tr=tr, name="loss_head")

    gs = {n: [None] * DEPTH for n in SMALL}
    for l in reversed(range(DEPTH)):
        p = f"l{l}_b_"
        s = saved[l]
        gb = {}
        dact = _mm(dxb, s['w_down'], mode='nt', tm=tm1, tn=1408, tk=D_MODEL, name=p + "down_dx")
        gb['w_down'] = _mm(s['act'], dxb, mode='tn', tm=1408, tn=D_MODEL, tk=tkt, name=p + "down_dw")
        toks = sched.slot(l, 'down', gb['w_down'])
        dup, dwg, dwv, dbg, dbv = _conv_act_bwd(s['up'], s['conv_w'], s['cb'], dact, tr=min(512, T), tc=256,
                                                name=p + "conv_act", deps=toks)
        gs['conv_w'][l] = jnp.concatenate([dwg, dwv], axis=1)
        gs['conv_b'][l] = jnp.concatenate([dbg, dbv], axis=1)[0]
        toks = sched.slot(l, 'conv', dup)
        dh2 = _mm_nt_sharded(dup, s['w_up'], tm=tm, name=p + "up_dx", deps=toks)
        for half in range(2):
            gb['w_up'] = _mm(s['h2'], dup, mode='tn', b_lead=half, tm=D_MODEL, tn=1408, tk=tkt,
                             out_into=gb.get('w_up'), out_joff=2 * half, out_n=2 * D_FF, name=p + f"up_dw{half}")
        toks = sched.slot(l, 'ffn', gb['w_up'], gb)
        dx1, dx1b, dg2 = _rms_bwd(s['x1'], s['g2'], dh2, dx, tr=tr, name=p + "rms2", deps=toks)
        gs['norm2'][l] = dg2[0]
        dmerged = _mm(dx1b, s['w_out'], mode='nt', tm=tm, tn=D_MODEL, tk=D_MODEL, name=p + "out_dx")
        gb['w_out'] = _mm(s['merged'], dx1b, mode='tn', tm=D_MODEL, tn=D_MODEL, tk=tkt, name=p + "out_dw")
        dy3, dz = _merge_bwd(dmerged, s['y3'], s['z'], tr=min(1024, T), tn=512, name=p + "merge")
        toks = sched.slot(l, 'mid', dz)
        dbr = []
        for idx, (wn, opn, width) in enumerate((('w_proj_a', 'pa', POOL_WIDTH), ('w_proj_b', 'at', ATTN_WIDTH),
                                                ('w_proj_c', 'sg', SGU_WIDTH))):
            dbr.append(_mm(dy3, s[wn], mode='nt', a_lead=idx, tm=tm, tn=width, tk=D_MODEL,
                           name=p + f"proj{idx}_dx", deps=toks if idx == 0 else ()))
            gb[wn] = _mm(s[opn], dy3, mode='tn', b_lead=idx, tm=width, tn=D_MODEL, tk=tkt,
                         name=p + f"proj{idx}_dw")
        dpa, dat, dsg = dbr
        dq, dkc, dkp, dvc, dvp, dsk = _attn_bwd(s['q'], s['k'], s['v'], s['sinks'], dat, name=p + "attn")
        gs['sinks'][l] = dsk[:, 0]
        toks = sched.slot(l, 'attn', dq)
        dz, dgq, dgk, dwbd, dsc = _mixer_ab_bwd(s['z'], cos, sin, s['gq'], s['gk'], seg, dq, dkc, dkp, dvc, dvp,
                                                dpa, s['wbd'], s['scale'], dz, tr=tr, name=p + "qkv_pool", deps=toks)
        gs['q_norm'][l] = dgq[0, :HEAD_DIM]
        gs['k_norm'][l] = dgk[0, :HEAD_DIM]
        gs['w_pool'][l] = jnp.stack([dwbd[g * HEAD_DIM:(g + 1) * HEAD_DIM, g * HEAD_DIM:(g + 1) * HEAD_DIM]
                                     for g in range(4)])
        gs['pool_scale'][l] = dsc[0]
        dz, dws, dbrows, dvn = _sgu_bwd(s['z'], s['wtril'], s['bexp'], s['vn'], seg, dsg, dz, tr=tr, name=p + "sgu")
        gs['w_s'][l] = dws
        gs['b_s'][l] = dbrows[:, ::HEAD_DIM].T
        gs['sgu_v_norm'][l] = dvn[0, :HEAD_DIM]
        dh1 = _mm_nt_sharded(dz, s['w_in'], tm=tm, name=p + "in_dx")
        gb['w_in'] = _mm(s['h1'], dz, mode='tn', tm=D_MODEL, tn=1152, tk=tkt, name=p + "in_dw")
        toks = sched.slot(l, 'mix', gb['w_in'], gb)
        dx, dxb, dg1 = _rms_bwd(s['x'], s['g1'], dh1, dx1, tr=tr, name=p + "rms1", deps=toks)
        gs['norm1'][l] = dg1[0]
    gs = {n: jnp.stack(v) for n, v in gs.items()}
    return loss_row, dx, gs


GROUP_F = ('w_down', 'w_up')
GROUP_M = ('w_out', 'w_proj_a', 'w_proj_b', 'w_proj_c', 'w_in')
ROW_SHARDED = ('w_out', 'w_down')

REDUCE_PLAN = {
    (1, 'ffn'): (('S1', 'F', 1),),
    (1, 'mid'): (('W1', 'F', 1),),
    (1, 'mix'): (('S1', 'M', 1),),
    (0, 'down'): (('W1', 'M', 1),),
    (0, 'conv'): (('W2', 'F', 1),),
    (0, 'ffn'): (('S1', 'F', 0), ('W3', 'F', 1)),
    (0, 'mid'): (('W1', 'F', 0),),
    (0, 'attn'): (('W2', 'M', 1),),
    (0, 'mix'): (('S1', 'M', 0), ('W3', 'M', 1)),
}
REDUCE_TAIL_A = (('W1', 'M', 0), ('W2', 'F', 0))
REDUCE_TAIL_B = (('W2', 'M', 0), ('W3', 'F', 0), ('W3', 'M', 0))


class _Comm:
    def __init__(self, w, pos):
        self.pos = pos
        groups = {'a': [('w_in', 0)],
                  'b': [(n, 0) for n in BIG[1:]] + [('conv_w', 0)],
                  'c': [(n, 1) for n in BIG] + [('conv_w', 1)]}
        self.gathers, self.group_of, self.weights = {}, {}, {}
        self.tokens = []
        for g, ks in groups.items():
            lands = [_cast_place(w[n][l], pos, F32 if n == 'conv_w' else MXU_DTYPE, name=f"gw_place_{n}{l}")
                     for n, l in ks]
            self.gathers[g] = (_Gather(lands, "gw_" + g, deps=self.tokens[-1:]), ks)
            self.tokens.append(self.gathers[g][0].token)
            self.group_of.update({k: g for k in ks})
        self.red = {}
        self.final = {}

    def start_tokens(self):
        return self.tokens[-1:]

    def weight(self, name, layer, after):
        if (name, layer) not in self.weights:
            gather, ks = self.gathers[self.group_of[(name, layer)]]
            for (n, l), full in zip(ks, gather.wait(after)):
                if n == 'conv_w' or n.startswith('w_proj'):
                    full = full.transpose(1, 0, 2).reshape(full.shape[1], -1)
                elif n in ROW_SHARDED:
                    full = full.reshape(-1, full.shape[2])
                self.weights[(n, l)] = full
        return self.weights[(name, layer)]

    def slot(self, layer, slot, after, grads=None):
        tokens = []
        for step, grp, lyr in REDUCE_PLAN.get((layer, slot), ()):
            tok = self._step(step, grp, lyr, after, grads)
            if tok is not None:
                tokens.append(tok)
        return tokens

    def tail(self, steps, after):
        return [t for t in (self._step(step, grp, lyr, after, None) for step, grp, lyr in steps) if t is not None]

    def shards(self):
        return {n: f.reshape(DEPTH, 2 * f.shape[2], f.shape[3]) for n, f in self.final.items()}

    def _step(self, step, grp, layer, after, grads):
        names = GROUP_F if grp == 'F' else GROUP_M
        tag = f"{grp.lower()}{layer}"
        st = self.red.setdefault((grp, layer), {})
        n = len(names)
        if step == 'S1':
            g4s = []
            for nm in names:
                g = grads[nm]
                R, C = g.shape
                g4s.append(g.reshape(N_CHIPS, 2, R // (2 * N_CHIPS), C) if nm in ROW_SHARDED
                           else g.reshape(1, 2, R // 2, C))
            st['s1'] = _swap_halves_start(g4s, name="rs1_" + tag)
            return st['s1'][2]
        if step == 'W1':
            sems, arrays, _, waits = st.pop('s1')
            arrays = _split_wait(sems, arrays, after, waits, name=f"rs1_{tag}_wait")
            parts = [_pair_add(arrays[i], arrays[n + i], self.pos, name=f"pair_add_{tag}_{names[i]}")
                     for i in range(n)]
            st['s2'] = _scatter_start(parts, name="rs2_" + tag)
            return st['s2'][2]
        if step == 'W2':
            sems, arrays, _, waits = st.pop('s2')
            arrays = _split_wait(sems, arrays, after, waits, name=f"rs2_{tag}_wait")
            fs = [_chip_sum(arrays[i], arrays[n + i], self.final.get(names[i]), self.pos, layer,
                            name=f"chip_sum_{tag}_{names[i]}") for i in range(n)]
            st['s3'] = _pair_share_start(fs, layer, name="rs3_" + tag)
            return st['s3'][2]
        sems, arrays, _, waits = st.pop('s3')
        self.final.update(zip(names, _split_wait(sems, arrays, after, waits, name=f"rs3_{tag}_wait")))
        return None


def _pack(arrays):
    flat = []
    for a in arrays:
        f = a.reshape(-1).astype(F32)
        flat.append(jnp.pad(f, (0, (-f.shape[0]) % 128)))
    v = jnp.concatenate(flat)
    v = jnp.pad(v, (0, (-v.shape[0]) % 1024))
    return v.reshape(-1, 128)


def _unpack(pack, shapes):
    v = pack.reshape(-1)
    out, off = [], 0
    for shp in shapes:
        nel = int(np.prod(shp))
        out.append(v[off:off + nel].reshape(shp))
        off += nel + (-nel) % 128
    return out


def kernel(x, positions, norm1, w_in, q_norm, k_norm, sinks, w_pool, pool_scale, sgu_v_norm, w_s, b_s, w_proj_a, w_proj_b, w_proj_c, w_out, norm2, w_up, conv_w, conv_b, w_down, loss_target, m_norm1, m_w_in, m_q_norm, m_k_norm, m_sinks, m_w_pool, m_pool_scale, m_sgu_v_norm, m_w_s, m_b_s, m_w_proj_a, m_w_proj_b, m_w_proj_c, m_w_out, m_norm2, m_w_up, m_conv_w, m_conv_b, m_w_down, v_norm1, v_w_in, v_q_norm, v_k_norm, v_sinks, v_w_pool, v_pool_scale, v_sgu_v_norm, v_w_s, v_b_s, v_w_proj_a, v_w_proj_b, v_w_proj_c, v_w_out, v_norm2, v_w_up, v_conv_w, v_conv_b, v_w_down):
    w = dict(norm1=norm1, w_in=w_in, q_norm=q_norm, k_norm=k_norm, sinks=sinks, w_pool=w_pool, pool_scale=pool_scale,
             sgu_v_norm=sgu_v_norm, w_s=w_s, b_s=b_s, w_proj_a=w_proj_a, w_proj_b=w_proj_b, w_proj_c=w_proj_c,
             w_out=w_out, norm2=norm2, w_up=w_up, conv_w=conv_w, conv_b=conv_b, w_down=w_down)
    m = dict(norm1=m_norm1, w_in=m_w_in, q_norm=m_q_norm, k_norm=m_k_norm, sinks=m_sinks, w_pool=m_w_pool,
             pool_scale=m_pool_scale, sgu_v_norm=m_sgu_v_norm, w_s=m_w_s, b_s=m_b_s, w_proj_a=m_w_proj_a,
             w_proj_b=m_w_proj_b, w_proj_c=m_w_proj_c, w_out=m_w_out, norm2=m_norm2, w_up=m_w_up, conv_w=m_conv_w,
             conv_b=m_conv_b, w_down=m_w_down)
    v = dict(norm1=v_norm1, w_in=v_w_in, q_norm=v_q_norm, k_norm=v_k_norm, sinks=v_sinks, w_pool=v_w_pool,
             pool_scale=v_pool_scale, sgu_v_norm=v_sgu_v_norm, w_s=v_w_s, b_s=v_b_s, w_proj_a=v_w_proj_a,
             w_proj_b=v_w_proj_b, w_proj_c=v_w_proj_c, w_out=v_w_out, norm2=v_norm2, w_up=v_w_up, conv_w=v_conv_w,
             conv_b=v_conv_b, w_down=v_w_down)
    chip = 2 * lax.axis_index("x") + lax.axis_index("y")
    core = lax.axis_index("c")

    pos = jnp.stack([chip, core]).astype(jnp.int32)
    comm = _Comm(w, pos)

    cos, sin = _rope_tables(positions[0])
    sp = {n: w[n] for n in SMALL if n != 'conv_w'}
    loss_row, dx, gs = _local_step(x[0], loss_target[0], cos, sin, sp, comm)
    loss = lax.psum(loss_row[0, 0], ("x", "y", "c"))

    small_shapes = [gs[n].shape for n in SMALL]
    small_pack = _pack([gs[n] for n in SMALL])
    toks = comm.tail(REDUCE_TAIL_A, dx)
    red = _sum_slots(_allgather_devices(small_pack, name="small_gather", deps=toks), tr=small_pack.shape[0],
                     name="small_sum")
    g_small = dict(zip(SMALL, _unpack(red, small_shapes)))
    comm.tail(REDUCE_TAIL_B, red)
    grads = comm.shards()
    grads.update(g_small)
    shard_cols = conv_w.shape[2]
    grads['conv_w'] = lax.dynamic_slice_in_dim(g_small['conv_w'], chip * shard_cols, shard_cols, axis=2)

    delta, new_m, new_v = {}, {}, {}
    for n in BIG:
        shp = w[n].shape
        two_d = lambda a: a.reshape(shp[0] * shp[1], shp[2])
        d, nm, nv = _adamw(two_d(w[n]), two_d(grads[n]), two_d(m[n]), two_d(v[n]),
                           tr=_row_tile(shp[0] * shp[1], 256), name=f"adamw_{n}")
        delta[n], new_m[n], new_v[n] = d.reshape(shp), nm.reshape(shp), nv.reshape(shp)
    shapes = [w[n].shape for n in SMALL]
    packs = [_pack([src[n] for n in SMALL]) for src in (w, grads, m, v)]
    d, nm, nv = _adamw(*packs, tr=packs[0].shape[0], name="adamw_small")
    for dst, src in ((delta, d), (new_m, nm), (new_v, nv)):
        dst.update(zip(SMALL, _unpack(src, shapes)))

    return (loss, dx[None], *[grads[n] for n in WEIGHTS], *[delta[n] for n in WEIGHTS],
            *[new_m[n] for n in WEIGHTS], *[new_v[n] for n in WEIGHTS])
```

```python
import functools
import math

import numpy as np
import jax
import jax.numpy as jnp
from jax import lax
from jax.experimental import pallas as pl
from jax.experimental.pallas import tpu as pltpu

F32 = jnp.float32
MXU_DTYPE = jnp.bfloat16
COMM_DTYPE = jnp.bfloat16

D_MODEL = 1024
DEPTH = 2
HEAD_DIM = 64
POOL_WINDOWS = (2, 4, 8, 16)
POOL_WIDTH = 256
N_Q_HEADS = 8
ATTN_BLOCK = 128
ATTN_WIDTH = 512
KV_WIDTH = 128
CHUNK = 128
SGU_WIDTH = 256
IN_COLS = 4608
GATE_COL0 = 1536
D_FF = 2816
ROPE_THETA = 10000.0
EPS = 1e-6
ADAM_LR, ADAM_B1, ADAM_B2, ADAM_EPS, ADAM_WD, ADAM_STEP = 0.001, 0.9, 0.999, 1e-08, 0.01, 10

N_CHIPS = 4
N_DEV = 8
VMEM_LIMIT_BYTES = 56 * 1024 * 1024
NEG_BIG = -1e30
MESH = pl.DeviceIdType.MESH
ANY = pl.BlockSpec(memory_space=pl.ANY)

SDS = jax.ShapeDtypeStruct


def _cp(*sem):
    return pltpu.CompilerParams(dimension_semantics=sem, vmem_limit_bytes=VMEM_LIMIT_BYTES)


def _dot(a, b, dims=((1,), (0,))):
    return lax.dot_general(a.astype(MXU_DTYPE), b.astype(MXU_DTYPE), (dims, ((), ())),
                           preferred_element_type=F32)


NT = ((1,), (1,))
TN = ((0,), (0,))


def _split_dot(x, m):
    hi = x.astype(MXU_DTYPE)
    lo = (x - hi.astype(F32)).astype(MXU_DTYPE)
    return _dot(hi, m) + _dot(lo, m)


def _seg_matrix(width, seg):
    idx = np.arange(width) // seg
    return jnp.asarray((idx[:, None] == idx[None, :]).astype(np.float32), dtype=MXU_DTYPE)


def _lane(shape):
    return lax.broadcasted_iota(jnp.int32, shape, len(shape) - 1)


def _row(shape):
    return lax.broadcasted_iota(jnp.int32, shape, 0)


def _full(shape):
    nd = len(shape)
    return pl.BlockSpec(shape, lambda *_: (0,) * nd)


def _gelu(x):
    k = math.sqrt(2.0 / math.pi)
    th = jnp.tanh(k * (x + 0.044715 * (x * x * x)))
    return 0.5 * x * (1.0 + th)


def _gelu_and_grad(x):
    k = math.sqrt(2.0 / math.pi)
    x2 = x * x
    th = jnp.tanh(k * (x + 0.044715 * (x2 * x)))
    g = 0.5 * x * (1.0 + th)
    dg = 0.5 * (1.0 + th) + 0.5 * x * (1.0 - th * th) * (k * (1.0 + 3.0 * 0.044715 * x2))
    return g, dg


def _sigmoid(x):
    return 0.5 * jnp.tanh(0.5 * x) + 0.5


def _swap_halves(x):
    w = x.shape[-1]
    first = (_lane(x.shape) % HEAD_DIM) < (HEAD_DIM // 2)
    return jnp.where(first, pltpu.roll(x, w - HEAD_DIM // 2, 1), pltpu.roll(x, HEAD_DIM // 2, 1))


def _tile_lanes(x, reps):
    return x if reps == 1 else jnp.concatenate([x] * reps, axis=1)


def _fold_lanes(x, period):
    w = x.shape[-1]
    while w > period:
        w //= 2
        x = x + pltpu.roll(x, w, 1)
    return x


def _mm(a, b, *, mode, tm, tn, tk, out_dtype=F32, add=None, name,
        a_lead=None, b_lead=None, b_sharded=False, out_into=None,
        b_koff=0, out_joff=0, out_n=None, deps=()):
    ash = a.shape[1:] if a_lead is not None else a.shape
    bsh = b.shape[1:] if b_lead is not None else b.shape
    if b_sharded:
        bsh = (b.shape[1], N_CHIPS * b.shape[2])
    if mode == 'nn':
        (M, K), (K2, N) = ash, bsh
    elif mode == 'nt':
        (M, K), (N, K2) = ash, bsh
    else:
        (K, M), (K2, N) = ash, bsh
    assert K == K2 or (mode == 'nt' and K2 > K), (ash, bsh, mode)
    assert M % tm == 0 and N % tn == 0 and K % tk == 0, (M, N, K, tm, tn, tk)
    nk = K // tk
    dims = {'nn': ((1,), (0,)), 'nt': NT, 'tn': TN}[mode]

    def lead(spec_shape, imap, lead_idx):
        if lead_idx is None:
            return pl.BlockSpec(spec_shape, imap)
        return pl.BlockSpec((None,) + spec_shape, lambda i, j, k: (lead_idx,) + imap(i, j, k))

    if mode == 'tn':
        a_spec = lead((tk, tm), lambda i, j, k: (k, i), a_lead)
    else:
        a_spec = lead((tm, tk), lambda i, j, k: (i, k), a_lead)
    if b_sharded:
        per = b.shape[2] // (tk if mode == 'nt' else tn)
        assert per * (tk if mode == 'nt' else tn) == b.shape[2] and mode != 'tn'
        if mode == 'nt':
            b_spec = pl.BlockSpec((None, tn, tk), lambda i, j, k: ((k + b_koff) // per, j, (k + b_koff) % per))
        else:
            b_spec = pl.BlockSpec((None, tk, tn), lambda i, j, k: (j // per, k, j % per))
    elif mode == 'nt':
        b_spec = lead((tn, tk), lambda i, j, k: (j, k + b_koff), b_lead)
    else:
        b_spec = lead((tk, tn), lambda i, j, k: (k, j), b_lead)
    o_spec = pl.BlockSpec((tm, tn), lambda i, j, k: (i, j + out_joff))
    n_out = N if out_n is None else out_n
    in_specs = [a_spec, b_spec]
    operands = [a, b]
    if add is not None:
        in_specs.append(pl.BlockSpec((tm, tn), lambda i, j, k: (i, j)))
        operands.append(add)
    aliases = {}
    if out_into is not None:
        in_specs.append(ANY)
        operands.append(out_into)
        aliases = {len(operands) - 1: 0}
    in_specs += [ANY] * len(deps)
    operands += list(deps)
    has_add = add is not None
    acc_in_out = nk > 1 and out_dtype == F32

    def body(*refs):
        a_ref, b_ref = refs[0], refs[1]
        pos = 2
        add_ref = None
        if has_add:
            add_ref = refs[pos]
            pos += 1
        if out_into is not None:
            pos += 1
        pos += len(deps)
        o_ref = refs[pos]
        acc_ref = refs[pos + 1] if (nk > 1 and not acc_in_out) else None
        p = _dot(a_ref[...], b_ref[...], dims)
        if nk == 1:
            if has_add:
                p = p + add_ref[...]
            o_ref[...] = p.astype(o_ref.dtype)
            return
        k = pl.program_id(2)
        tgt = o_ref if acc_in_out else acc_ref

        @pl.when(k == 0)
        def _():
            tgt[...] = p + add_ref[...] if has_add else p

        @pl.when(k > 0)
        def _():
            tgt[...] += p

        if not acc_in_out:
            @pl.when(k == nk - 1)
            def _():
                o_ref[...] = acc_ref[...].astype(o_ref.dtype)

    out_shape = SDS((M, n_out), out_dtype)
    scratch = [pltpu.VMEM((tm, tn), F32)] if (nk > 1 and not acc_in_out) else []
    return pl.pallas_call(
        body, grid=(M // tm, N // tn, nk), in_specs=in_specs, out_specs=o_spec, out_shape=out_shape,
        scratch_shapes=scratch, input_output_aliases=aliases, name=name,
        compiler_params=_cp("parallel", "parallel", "arbitrary"))(*operands)


def _mm_nt_sharded(a, b, *, tm, name, deps=()):
    a3 = a if a.ndim == 3 else a[None]
    A, M, ka = a3.shape
    S, N, ns = b.shape
    per = S // A
    assert ka == per * ns and M % tm == 0, (a3.shape, b.shape)

    def body(a_ref, b_ref, o_ref):
        acc = None
        for s in range(S):
            lo = (s % per) * ns
            p = _dot(a_ref[s // per, :, lo:lo + ns], b_ref[s], NT)
            acc = p if acc is None else acc + p
        o_ref[...] = acc

    return pl.pallas_call(
        _after(body, 2, deps), grid=(M // tm,),
        in_specs=[pl.BlockSpec((A, tm, ka), lambda i: (0, i, 0)), pl.BlockSpec((S, N, ns), lambda i: (0, 0, 0))]
        + [ANY] * len(deps),
        out_specs=pl.BlockSpec((tm, N), lambda i: (i, 0)), out_shape=SDS((M, N), F32), name=name,
        compiler_params=_cp("parallel"))(a3, b, *deps)


def _after(body, n_in, deps):
    nd = len(deps)
    if nd == 0:
        return body
    return lambda *refs: body(*refs[:n_in], *refs[n_in + nd:])


def _rms_fwd(x, g, *, tr, name, deps=()):
    T, D = x.shape

    def body(x_ref, g_ref, o_ref):
        xv = x_ref[...]
        r = lax.rsqrt(jnp.mean(xv * xv, axis=-1, keepdims=True) + EPS)
        o_ref[...] = (xv * r * g_ref[...]).astype(o_ref.dtype)

    return pl.pallas_call(
        _after(body, 2, deps), grid=(T // tr,),
        in_specs=[pl.BlockSpec((tr, D), lambda i: (i, 0)), _full((1, D))] + [ANY] * len(deps),
        out_specs=pl.BlockSpec((tr, D), lambda i: (i, 0)),
        out_shape=SDS((T, D), MXU_DTYPE), name=name, compiler_params=_cp("parallel"))(x, g, *deps)


def _rms_bwd(x, g, dh, dres, *, tr, name, deps=()):
    T, D = x.shape

    def body(x_ref, g_ref, dh_ref, dres_ref, dx_ref, dxb_ref, dg_ref):
        i = pl.program_id(0)
        xv = x_ref[...]
        r = lax.rsqrt(jnp.mean(xv * xv, axis=-1, keepdims=True) + EPS)
        xh = xv * r
        dh = dh_ref[...]
        gy = dh * g_ref[...]
        dx = r * (gy - xh * jnp.mean(xh * gy, axis=-1, keepdims=True)) + dres_ref[...]
        dx_ref[...] = dx
        dxb_ref[...] = dx.astype(dxb_ref.dtype)

        @pl.when(i == 0)
        def _():
            dg_ref[...] = jnp.zeros_like(dg_ref)
        dg_ref[...] += jnp.sum(dh * xh, axis=0, keepdims=True)

    rows = pl.BlockSpec((tr, D), lambda i: (i, 0))
    return pl.pallas_call(
        _after(body, 4, deps), grid=(T // tr,), in_specs=[rows, _full((1, D)), rows, rows] + [ANY] * len(deps),
        out_specs=[rows, rows, _full((1, D))],
        out_shape=[SDS((T, D), F32), SDS((T, D), MXU_DTYPE), SDS((1, D), F32)],
        name=name, compiler_params=_cp("arbitrary"))(x, g, dh, dres, *deps)


def _loss_head(y, target, *, tr, name):
    T, D = y.shape

    def body(y_ref, t_ref, loss_ref, dy_ref, dyb_ref):
        i = pl.program_id(0)
        d = y_ref[...] - t_ref[...]
        dy = d * (1.0 / D)
        dy_ref[...] = dy
        dyb_ref[...] = dy.astype(dyb_ref.dtype)
        part = jnp.sum(jnp.sum(d * d, axis=1, keepdims=True), axis=0, keepdims=True) * (0.5 / D)

        @pl.when(i == 0)
        def _():
            loss_ref[...] = jnp.zeros_like(loss_ref)
        loss_ref[...] += jnp.broadcast_to(part, loss_ref.shape)

    rows = pl.BlockSpec((tr, D), lambda i: (i, 0))
    return pl.pallas_call(
        body, grid=(T // tr,), in_specs=[rows, rows],
        out_specs=[_full((1, 128)), rows, rows],
        out_shape=[SDS((1, 128), F32), SDS((T, D), F32), SDS((T, D), MXU_DTYPE)],
        name=name, compiler_params=_cp("arbitrary"))(y, target)


def _pool_lane_consts(shape):
    lane = _lane(shape)
    grp = lane // (POOL_WIDTH // 4)
    win = jnp.where(grp == 0, 2, jnp.where(grp == 1, 4, jnp.where(grp == 2, 8, 16)))
    return grp, win


def _pool_select(grp, s2, s4, s8, s16):
    return jnp.where(grp == 0, s2, jnp.where(grp == 1, s4, jnp.where(grp == 2, s8, s16)))


def _pool_diff(xe, row0, tr):
    s2 = xe + pltpu.roll(xe, 1, 0)
    s4 = s2 + pltpu.roll(s2, 2, 0)
    s8 = s4 + pltpu.roll(s4, 4, 0)
    s16 = s8 + pltpu.roll(s8, 8, 0)
    shape = (tr, POOL_WIDTH)
    grp, win = _pool_lane_consts(shape)
    sums = _pool_select(grp, s2[16:], s4[16:], s8[16:], s16[16:])
    t = row0 + _row(shape)
    cnt = jnp.minimum(t + 1, win).astype(F32)
    return sums / cnt - xe[16:]


def _pool_fwd(z, wbd, scale, *, tr, name):
    T = z.shape[0]
    hb = tr // 16

    def body(x_ref, xp_ref, w_ref, s_ref, o_ref):
        i = pl.program_id(0)
        halo = jnp.where(i == 0, 0.0, xp_ref[...])
        diff = _pool_diff(jnp.concatenate([halo, x_ref[...]], axis=0), i * tr, tr)
        o_ref[...] = (_dot(diff, w_ref[...]) * s_ref[...]).astype(o_ref.dtype)

    return pl.pallas_call(
        body, grid=(T // tr,),
        in_specs=[pl.BlockSpec((tr, POOL_WIDTH), lambda i: (i, 0)),
                  pl.BlockSpec((16, POOL_WIDTH), lambda i: (jnp.maximum(i * hb - 1, 0), 0)),
                  _full((POOL_WIDTH, POOL_WIDTH)), _full((1, POOL_WIDTH))],
        out_specs=pl.BlockSpec((tr, POOL_WIDTH), lambda i: (i, 0)),
        out_shape=SDS((T, POOL_WIDTH), MXU_DTYPE), name=name, compiler_params=_cp("parallel"))(z, z, wbd, scale)


def _pool_bwd_tile(i, n, tr, x, xprev, dpa, dpa_next, wbd, scale):
    halo = jnp.where(i == 0, 0.0, xprev)
    diff = _pool_diff(jnp.concatenate([halo, x], axis=0), i * tr, tr)
    mixed = _dot(diff, wbd)
    dscale = jnp.sum(dpa * mixed, axis=0, keepdims=True)
    dnext = jnp.where(i == n - 1, 0.0, dpa_next)
    dmix_e = jnp.concatenate([dpa, dnext], axis=0) * scale
    ddiff_e = _dot(dmix_e, wbd, NT)
    dwbd = _dot(diff, dmix_e[:tr], TN)
    shape = (tr + 16, POOL_WIDTH)
    grp, win = _pool_lane_consts(shape)
    t = i * tr + _row(shape)
    e = ddiff_e / jnp.minimum(t + 1, win).astype(F32)
    nrow = tr + 16
    a2 = e + pltpu.roll(e, nrow - 1, 0)
    a4 = a2 + pltpu.roll(a2, nrow - 2, 0)
    a8 = a4 + pltpu.roll(a4, nrow - 4, 0)
    a16 = a8 + pltpu.roll(a8, nrow - 8, 0)
    dx = _pool_select(grp, a2, a4, a8, a16)[:tr] - ddiff_e[:tr]
    return dx, dwbd, dscale


def _norm_rope(x, g, cos, sin_signed, seg):
    reps = x.shape[1] // 128
    ms = _split_dot(x * x, seg) * (1.0 / HEAD_DIM)
    r = lax.rsqrt(ms + EPS)
    xn = x * r * g
    c, s = _tile_lanes(cos, reps), _tile_lanes(sin_signed, reps)
    return xn * c + _swap_halves(xn) * s


def _norm_rope_bwd(x, g, cos, sin_signed, seg, dout):
    reps = x.shape[1] // 128
    c, s = _tile_lanes(cos, reps), _tile_lanes(sin_signed, reps)
    dxn = dout * c + _swap_halves(dout * s)
    ms = _split_dot(x * x, seg) * (1.0 / HEAD_DIM)
    r = lax.rsqrt(ms + EPS)
    xh = x * r
    gy = dxn * g
    dx = r * (gy - xh * (_split_dot(xh * gy, seg) * (1.0 / HEAD_DIM)))
    dg = jnp.sum(dxn * xh, axis=0, keepdims=True)
    return dx, dg


def _dup_heads(k):
    first = _lane(k.shape) < HEAD_DIM
    kr = pltpu.roll(k, HEAD_DIM, 1)
    return jnp.concatenate([jnp.where(first, k, kr), jnp.where(first, kr, k)], axis=1)


def _qkv_prep(z, cos, sin_signed, gq, gk, seg, *, tr, name):
    T = z.shape[0]

    def body(qa_ref, qb_ref, kv_ref, c_ref, s_ref, gq_ref, gk_ref, seg_ref, q_ref, k_ref, v_ref):
        c, s, seg_m = c_ref[...], s_ref[...], seg_ref[...]
        scale = HEAD_DIM ** -0.5
        qa = _norm_rope(qa_ref[...], gq_ref[...], c, s, seg_m) * scale
        qb = _norm_rope(qb_ref[...], gq_ref[...], c, s, seg_m) * scale
        q_ref[...] = jnp.concatenate([qa, qb], axis=1).astype(q_ref.dtype)
        kv = kv_ref[...]
        k = _norm_rope(kv[:, :KV_WIDTH], gk_ref[...], c, s, seg_m[:128, :128])
        k_ref[...] = _dup_heads(k).astype(k_ref.dtype)
        v_ref[...] = _dup_heads(kv[:, KV_WIDTH:]).astype(v_ref.dtype)

    col = lambda j: pl.BlockSpec((tr, 256), lambda i: (i, j))
    tab = pl.BlockSpec((tr, 128), lambda i: (i, 0))
    return pl.pallas_call(
        body, grid=(T // tr,),
        in_specs=[col(1), col(2), col(3), tab, tab, _full((1, 256)), _full((1, 128)), _full((256, 256))],
        out_specs=[pl.BlockSpec((tr, 512), lambda i: (i, 0)), col(0), col(0)],
        out_shape=[SDS((T, 512), MXU_DTYPE), SDS((T, 256), MXU_DTYPE), SDS((T, 256), MXU_DTYPE)],
        name=name, compiler_params=_cp("parallel"))(z, z, z, cos, sin_signed, gq, gk, seg)


GROUP_HEADS = 4
GROUP_ROWS = GROUP_HEADS * ATTN_BLOCK


def _attn_mask(n):
    qi = _row((GROUP_ROWS, 2 * ATTN_BLOCK)) % ATTN_BLOCK
    kj = _lane((GROUP_ROWS, 2 * ATTN_BLOCK))
    return (kj > qi) & (kj <= qi + ATTN_BLOCK) & ((kj >= ATTN_BLOCK) | (n > 0))


def _stack_heads(x, g):
    first = _lane((ATTN_BLOCK, 128)) < HEAD_DIM
    parts = []
    for pair in (2 * g, 2 * g + 1):
        x128 = x[:, 128 * pair:128 * (pair + 1)]
        zero = jnp.zeros_like(x128)
        parts += [jnp.where(first, x128, zero), jnp.where(first, zero, x128)]
    return jnp.concatenate(parts, axis=0)


def _unstack_heads(y):
    first = _lane((ATTN_BLOCK, 128)) < HEAD_DIM
    b = ATTN_BLOCK
    return jnp.concatenate([jnp.where(first, y[0:b], y[b:2 * b]), jnp.where(first, y[2 * b:3 * b], y[3 * b:4 * b])],
                           axis=1)


def _sink_col(sk_ref, g):
    return jnp.concatenate([jnp.broadcast_to(sk_ref[h:h + 1, 0:1], (ATTN_BLOCK, 1))
                            for h in range(GROUP_HEADS * g, GROUP_HEADS * (g + 1))], axis=0)


def _group_exp(q4, kg, mask, sink):
    s = _dot(q4, kg, NT)
    s = jnp.where(mask, s, NEG_BIG)
    m = jnp.maximum(jnp.max(s, axis=1, keepdims=True), sink)
    p = jnp.exp(s - m)
    ps = jnp.exp(sink - m)
    return p, ps, 1.0 / (jnp.sum(p, axis=1, keepdims=True) + ps)


def _group_probs(q4, kg, mask, sink):
    p, ps, inv = _group_exp(q4, kg, mask, sink)
    return p * inv, ps * inv


def _attn_fwd(q, k, v, sinks_b, *, name):
    T = q.shape[0]
    nb = T // ATTN_BLOCK

    def body(q_ref, kc_ref, kp_ref, vc_ref, vp_ref, sk_ref, o_ref):
        n = pl.program_id(0)
        mask = _attn_mask(n)
        k2 = jnp.concatenate([kp_ref[...], kc_ref[...]], axis=0)
        v2 = jnp.concatenate([vp_ref[...], vc_ref[...]], axis=0)
        qv = q_ref[...]
        outs = []
        for g in range(2):
            kg = k2[:, 128 * g:128 * (g + 1)]
            vg = v2[:, 128 * g:128 * (g + 1)]
            p, _, inv = _group_exp(_stack_heads(qv, g), kg, mask, _sink_col(sk_ref, g))
            outs.append(_unstack_heads(_dot(p, vg) * inv))
        o_ref[...] = jnp.concatenate(outs, axis=1).astype(o_ref.dtype)

    cur = lambda w: pl.BlockSpec((ATTN_BLOCK, w), lambda n: (n, 0))
    prev = lambda w: pl.BlockSpec((ATTN_BLOCK, w), lambda n: (jnp.maximum(n - 1, 0), 0))
    return pl.pallas_call(
        body, grid=(nb,),
        in_specs=[cur(512), cur(256), prev(256), cur(256), prev(256), _full((8, 128))],
        out_specs=cur(512), out_shape=SDS((T, 512), MXU_DTYPE), name=name,
        compiler_params=_cp("parallel"))(q, k, k, v, v, sinks_b)


def _attn_bwd(q, k, v, sinks_b, do, *, name):
    T = q.shape[0]
    nb = T // ATTN_BLOCK

    def body(q_ref, kc_ref, kp_ref, vc_ref, vp_ref, sk_ref, do_ref,
             dq_ref, dkc_ref, dkp_ref, dvc_ref, dvp_ref, dsk_ref):
        n = pl.program_id(0)
        mask = _attn_mask(n)
        k2 = jnp.concatenate([kp_ref[...], kc_ref[...]], axis=0)
        v2 = jnp.concatenate([vp_ref[...], vc_ref[...]], axis=0)
        qv = q_ref[...]
        dov = do_ref[...]

        @pl.when(n == 0)
        def _():
            dsk_ref[...] = jnp.zeros_like(dsk_ref)

        dqs, dks, dvs = [], [], []
        for g in range(2):
            kg = k2[:, 128 * g:128 * (g + 1)]
            vg = v2[:, 128 * g:128 * (g + 1)]
            q4 = _stack_heads(qv, g)
            do4 = _stack_heads(dov, g)
            pn, psn = _group_probs(q4, kg, mask, _sink_col(sk_ref, g))
            o2 = _dot(pn, vg)
            delta = jnp.sum(do4 * o2, axis=1, keepdims=True)
            ds = pn * (_dot(do4, vg, NT) - delta)
            dqs.append(_unstack_heads(_dot(ds, kg)))
            dks.append(_dot(ds, q4, TN))
            dvs.append(_dot(pn, do4, TN))
            wsink = psn * delta
            for j in range(GROUP_HEADS):
                h = GROUP_HEADS * g + j
                dsink = -jnp.sum(wsink[ATTN_BLOCK * j:ATTN_BLOCK * (j + 1)], axis=0, keepdims=True)
                dsk_ref[h:h + 1, :] += jnp.broadcast_to(dsink, (1, 128))
        dq_ref[...] = jnp.concatenate(dqs, axis=1)
        dk = jnp.concatenate(dks, axis=1)
        dv = jnp.concatenate(dvs, axis=1)
        dkp_ref[...] = dk[:ATTN_BLOCK]
        dkc_ref[...] = dk[ATTN_BLOCK:]
        dvp_ref[...] = dv[:ATTN_BLOCK]
        dvc_ref[...] = dv[ATTN_BLOCK:]

    cur = lambda w: pl.BlockSpec((ATTN_BLOCK, w), lambda n: (n, 0))
    prev = lambda w: pl.BlockSpec((ATTN_BLOCK, w), lambda n: (jnp.maximum(n - 1, 0), 0))
    f = lambda w: SDS((T, w), F32)
    return pl.pallas_call(
        body, grid=(nb,),
        in_specs=[cur(512), cur(256), prev(256), cur(256), prev(256), _full((8, 128)), cur(512)],
        out_specs=[cur(512), cur(256), cur(256), cur(256), cur(256), _full((8, 128))],
        out_shape=[f(512), f(256), f(256), f(256), f(256), SDS((8, 128), F32)],
        name=name, compiler_params=_cp("arbitrary"))(q, k, k, v, v, sinks_b, do)


def _mixer_ab_bwd(z, cos, sin_signed, gq, gk, seg, dq, dkc, dkp, dvc, dvp, dpa, wbd, scale, dz, *, tr, name, deps=()):
    T = z.shape[0]
    n = T // tr
    hb = tr // 16
    ab = tr // ATTN_BLOCK

    def unfold(cur, nxt_tile, nxt_halo, i):
        nxt = jnp.concatenate([nxt_tile[ATTN_BLOCK:], jnp.where(i == n - 1, 0.0, nxt_halo)], axis=0)
        tot = cur + nxt
        first = _lane((tr, 128)) < HEAD_DIM
        a = tot[:, :128]
        b = tot[:, 128:]
        a = a + pltpu.roll(a, HEAD_DIM, 1)
        b = b + pltpu.roll(b, HEAD_DIM, 1)
        return jnp.where(first, a, b)

    def body(xp_ref, xpp_ref, qa_ref, qb_ref, kv_ref, c_ref, s_ref, gq_ref, gk_ref, seg_ref,
             dq_ref, dkc_ref, dkp_ref, dkh_ref, dvc_ref, dvp_ref, dvh_ref, dpa_ref, dpan_ref, w_ref, sc_ref, _dz_in,
             dz_ref, dgq_ref, dgk_ref, dw_ref, dsc_ref):
        i = pl.program_id(0)
        c, s, seg_m = c_ref[...], s_ref[...], seg_ref[...]
        scale_q = HEAD_DIM ** -0.5
        dqv = dq_ref[...] * scale_q
        dxa, dga = _norm_rope_bwd(qa_ref[...], gq_ref[...], c, s, seg_m, dqv[:, :256])
        dxb, dgb = _norm_rope_bwd(qb_ref[...], gq_ref[...], c, s, seg_m, dqv[:, 256:])
        dk = unfold(dkc_ref[...], dkp_ref[...], dkh_ref[...], i)
        dv = unfold(dvc_ref[...], dvp_ref[...], dvh_ref[...], i)
        kv = kv_ref[...]
        dxk, dgk = _norm_rope_bwd(kv[:, :KV_WIDTH], gk_ref[...], c, s, seg_m[:128, :128], dk)
        dxp, dwbd, dscale = _pool_bwd_tile(i, n, tr, xp_ref[...], xpp_ref[...], dpa_ref[...], dpan_ref[...],
                                           w_ref[...], sc_ref[...])
        dz_ref[...] = jnp.concatenate([dxp, dxa, dxb, dxk, dv], axis=1).astype(dz_ref.dtype)

        @pl.when(i == 0)
        def _():
            dgq_ref[...] = jnp.zeros_like(dgq_ref)
            dgk_ref[...] = jnp.zeros_like(dgk_ref)
            dw_ref[...] = jnp.zeros_like(dw_ref)
            dsc_ref[...] = jnp.zeros_like(dsc_ref)
        dgq_ref[...] += _fold_lanes(dga + dgb, HEAD_DIM)
        dgk_ref[...] += _fold_lanes(dgk, HEAD_DIM)
        dw_ref[...] += dwbd
        dsc_ref[...] += dscale

    col = lambda j: pl.BlockSpec((tr, 256), lambda i: (i, j))
    rows = lambda w: pl.BlockSpec((tr, w), lambda i: (i, 0))
    nxt_blk = pl.BlockSpec((ATTN_BLOCK, 256), lambda i: (jnp.minimum((i + 1) * ab, T // ATTN_BLOCK - 1), 0))
    prev16 = pl.BlockSpec((16, 256), lambda i: (jnp.maximum(i * hb - 1, 0), 0))
    next16 = pl.BlockSpec((16, 256), lambda i: (jnp.minimum((i + 1) * hb, T // 16 - 1), 0))
    return pl.pallas_call(
        _after(body, 22, deps), grid=(n,),
        in_specs=[col(0), prev16, col(1), col(2), col(3), rows(128), rows(128),
                  _full((1, 256)), _full((1, 128)), _full((256, 256)),
                  rows(512), rows(256), rows(256), nxt_blk, rows(256), rows(256), nxt_blk,
                  rows(256), next16, _full((256, 256)), _full((1, 256)), ANY] + [ANY] * len(deps),
        out_specs=[rows(1024), _full((1, 256)), _full((1, 128)), _full((256, 256)), _full((1, 256))],
        out_shape=[SDS((T, IN_COLS), MXU_DTYPE), SDS((1, 256), F32), SDS((1, 128), F32),
                   SDS((256, 256), F32), SDS((1, 256), F32)],
        input_output_aliases={21: 0}, name=name, compiler_params=_cp("arbitrary"))(
            z, z, z, z, z, cos, sin_signed, gq, gk, seg, dq, dkc, dkp, dkp, dvc, dvp, dvp, dpa, dpa, wbd, scale, dz,
            *deps)


def _sgu_common(zu, zv, vn, seg):
    u, du = _gelu_and_grad(zu)
    gv, dgv = _gelu_and_grad(zv)
    ms = _split_dot(gv * gv, seg) * (1.0 / HEAD_DIM)
    r = lax.rsqrt(ms + EPS)
    xh = gv * r
    return u, du, dgv, r, xh, xh * vn


def _sgu_fwd(z, wtril, bexp, vn, seg, *, tr, name):
    T = z.shape[0]
    nch = tr // CHUNK

    def body(u_ref, v_ref, w_ref, b_ref, vn_ref, seg_ref, o_ref):
        u, _, _, _, _, vg = _sgu_common(u_ref[...], v_ref[...], vn_ref[...], seg_ref[...])
        grp = _lane((CHUNK, SGU_WIDTH)) // HEAD_DIM
        outs = []
        for ch in range(nch):
            vc = vg[ch * CHUNK:(ch + 1) * CHUNK]
            s = b_ref[...]
            for g in range(4):
                s = s + jnp.where(grp == g, _dot(w_ref[g], vc), 0.0)
            outs.append(u[ch * CHUNK:(ch + 1) * CHUNK] * s)
        o_ref[...] = jnp.concatenate(outs, axis=0).astype(o_ref.dtype)

    col = lambda j: pl.BlockSpec((tr, 256), lambda i: (i, j))
    return pl.pallas_call(
        body, grid=(T // tr,),
        in_specs=[col(4), col(5), _full((4, CHUNK, CHUNK)), _full((CHUNK, 256)), _full((1, 256)), _full((256, 256))],
        out_specs=col(0), out_shape=SDS((T, SGU_WIDTH), MXU_DTYPE), name=name,
        compiler_params=_cp("parallel"))(z, z, wtril, bexp, vn, seg)


def _sgu_bwd(z, wtril, bexp, vn, seg, dsg, dz, *, tr, name):
    T = z.shape[0]
    nch = tr // CHUNK

    def body(u_ref, v_ref, w_ref, b_ref, vn_ref, seg_ref, d_ref, _dz_in, dz_ref, dw_ref, db_ref, dvn_ref):
        i = pl.program_id(0)
        seg_m = seg_ref[...]
        vn_v = vn_ref[...]
        u, du, dgv, r, xh, vg = _sgu_common(u_ref[...], v_ref[...], vn_v, seg_m)
        d = d_ref[...]
        grp = _lane((CHUNK, SGU_WIDTH)) // HEAD_DIM
        tril = _row((CHUNK, CHUNK)) >= _lane((CHUNK, CHUNK))

        @pl.when(i == 0)
        def _():
            dw_ref[...] = jnp.zeros_like(dw_ref)
            db_ref[...] = jnp.zeros_like(db_ref)
            dvn_ref[...] = jnp.zeros_like(dvn_ref)

        dus, dvgs = [], []
        for ch in range(nch):
            sl = slice(ch * CHUNK, (ch + 1) * CHUNK)
            vc = vg[sl]
            s = b_ref[...]
            for g in range(4):
                s = s + jnp.where(grp == g, _dot(w_ref[g], vc), 0.0)
            dus.append(d[sl] * s)
            ds = d[sl] * u[sl]
            db_ref[...] += _split_dot(ds, seg_m)
            dvg = jnp.zeros((CHUNK, SGU_WIDTH), F32)
            for g in range(4):
                dsm = jnp.where(grp == g, ds, 0.0)
                dvg = dvg + jnp.where(grp == g, _dot(w_ref[g], ds, TN), 0.0)
                dw_ref[g] += jnp.where(tril, _dot(dsm, vc, NT), 0.0)
            dvgs.append(dvg)
        dup = jnp.concatenate(dus, axis=0)
        dvg = jnp.concatenate(dvgs, axis=0)
        dvn_ref[...] += _fold_lanes(jnp.sum(dvg * xh, axis=0, keepdims=True), HEAD_DIM)
        gy = dvg * vn_v
        dgvv = r * (gy - xh * (_split_dot(xh * gy, seg_m) * (1.0 / HEAD_DIM)))
        dz_ref[...] = jnp.concatenate([dup * du, dgvv * dgv], axis=1).astype(dz_ref.dtype)

    col = lambda j: pl.BlockSpec((tr, 256), lambda i: (i, j))
    return pl.pallas_call(
        body, grid=(T // tr,),
        in_specs=[col(4), col(5), _full((4, CHUNK, CHUNK)), _full((CHUNK, 256)), _full((1, 256)), _full((256, 256)),
                  col(0), ANY],
        out_specs=[pl.BlockSpec((tr, 512), lambda i: (i, 2)), _full((4, CHUNK, CHUNK)), _full((CHUNK, 256)),
                   _full((1, 256))],
        out_shape=[SDS((T, IN_COLS), MXU_DTYPE), SDS((4, CHUNK, CHUNK), F32), SDS((CHUNK, 256), F32),
                   SDS((1, 256), F32)],
        input_output_aliases={7: 0}, name=name, compiler_params=_cp("arbitrary"))(
            z, z, wtril, bexp, vn, seg, dsg, dz)


def _merge_fwd(pa, at, sg, wa, wb, wc, z, *, tm, tn, name):
    T = pa.shape[0]
    gb = GATE_COL0 // tn
    nb = D_MODEL // tn

    def body(pa_ref, at_ref, sg_ref, wa_ref, wb_ref, wc_ref, g0_ref, g1_ref, g2_ref, m_ref, y_ref):
        acc = None
        for idx, (op_ref, w_ref, g_ref) in enumerate(((pa_ref, wa_ref, g0_ref), (at_ref, wb_ref, g1_ref),
                                                      (sg_ref, wc_ref, g2_ref))):
            y = _dot(op_ref[...], w_ref[...])
            y_ref[idx] = y.astype(y_ref.dtype)
            t = _sigmoid(g_ref[...]) * y
            acc = t if acc is None else acc + t
        m_ref[...] = acc.astype(m_ref.dtype)

    op = lambda w: pl.BlockSpec((tm, w), lambda i, j: (i, 0))
    wt = lambda k: pl.BlockSpec((k, tn), lambda i, j: (0, j))
    gate = lambda b: pl.BlockSpec((tm, tn), lambda i, j: (i, gb + b * nb + j))
    return pl.pallas_call(
        body, grid=(T // tm, nb),
        in_specs=[op(256), op(512), op(256), wt(256), wt(512), wt(256), gate(0), gate(1), gate(2)],
        out_specs=[pl.BlockSpec((tm, tn), lambda i, j: (i, j)), pl.BlockSpec((3, tm, tn), lambda i, j: (0, i, j))],
        out_shape=[SDS((T, D_MODEL), MXU_DTYPE), SDS((3, T, D_MODEL), MXU_DTYPE)],
        name=name, compiler_params=_cp("parallel", "parallel"))(pa, at, sg, wa, wb, wc, z, z, z)


def _merge_bwd(dm, y, z, *, tr, tn, name):
    T = dm.shape[0]
    gb = GATE_COL0 // tn
    nb = D_MODEL // tn

    def body(dm_ref, y_ref, g_ref, dy_ref, dz_ref):
        g = _sigmoid(g_ref[...])
        d = dm_ref[...]
        dy_ref[...] = (d * g).astype(dy_ref.dtype)
        dz_ref[...] = (d * y_ref[...].astype(F32) * g * (1.0 - g)).astype(dz_ref.dtype)

    return pl.pallas_call(
        body, grid=(T // tr, nb, 3),
        in_specs=[pl.BlockSpec((tr, tn), lambda i, j, b: (i, j)),
                  pl.BlockSpec((None, tr, tn), lambda i, j, b: (b, i, j)),
                  pl.BlockSpec((tr, tn), lambda i, j, b: (i, gb + b * nb + j))],
        out_specs=[pl.BlockSpec((None, tr, tn), lambda i, j, b: (b, i, j)),
                   pl.BlockSpec((tr, tn), lambda i, j, b: (i, gb + b * nb + j))],
        out_shape=[SDS((3, T, D_MODEL), MXU_DTYPE), SDS((T, IN_COLS), MXU_DTYPE)],
        name=name, compiler_params=_cp("parallel", "parallel", "parallel"))(dm, y, z)


def _conv3(xe, w, b):
    return (w[0:1] * pltpu.roll(xe, 2, 0) + w[1:2] * pltpu.roll(xe, 1, 0) + w[2:3] * xe)[8:] + b


def _conv_act_fwd(up, cw, cb, *, tr, tc, name):
    T = up.shape[0]
    nc = D_FF // tc
    hb = tr // 8

    def body(ug_ref, ugp_ref, uv_ref, uvp_ref, wg_ref, wv_ref, bg_ref, bv_ref, o_ref):
        i = pl.program_id(1)
        first = i == 0
        cg = _conv3(jnp.concatenate([jnp.where(first, 0.0, ugp_ref[...]), ug_ref[...]], axis=0), wg_ref[...], bg_ref[...])
        cv = _conv3(jnp.concatenate([jnp.where(first, 0.0, uvp_ref[...]), uv_ref[...]], axis=0), wv_ref[...], bv_ref[...])
        o_ref[...] = (cg * _sigmoid(cg) * cv).astype(o_ref.dtype)

    tile = lambda off: pl.BlockSpec((tr, tc), lambda j, i: (i, off + j))
    prev = lambda off: pl.BlockSpec((8, tc), lambda j, i: (jnp.maximum(i * hb - 1, 0), off + j))
    par = lambda rows, off: pl.BlockSpec((rows, tc), lambda j, i: (0, off + j))
    return pl.pallas_call(
        body, grid=(nc, T // tr),
        in_specs=[tile(0), prev(0), tile(nc), prev(nc), par(3, 0), par(3, nc), par(1, 0), par(1, nc)],
        out_specs=pl.BlockSpec((tr, tc), lambda j, i: (i, j)),
        out_shape=SDS((T, D_FF), MXU_DTYPE), name=name,
        compiler_params=_cp("parallel", "parallel"))(up, up, up, up, cw, cw, cb, cb)


def _conv_act_bwd(up, cw, cb, dact, *, tr, tc, name, deps=()):
    T = up.shape[0]
    nc = D_FF // tc
    hb = tr // 8
    nr = T // tr

    def body(ug_ref, ugp_ref, ugn_ref, uv_ref, uvp_ref, uvn_ref, da_ref, dan_ref, wg_ref, wv_ref, bg_ref, bv_ref,
             du_ref, dwg_ref, dwv_ref, dbg_ref, dbv_ref):
        i = pl.program_id(1)
        first, last = i == 0, i == nr - 1
        da = jnp.concatenate([da_ref[...], jnp.where(last, 0.0, dan_ref[...])], axis=0)
        uge = jnp.concatenate([jnp.where(first, 0.0, ugp_ref[...]), ug_ref[...], ugn_ref[...]], axis=0)
        uve = jnp.concatenate([jnp.where(first, 0.0, uvp_ref[...]), uv_ref[...], uvn_ref[...]], axis=0)
        wg, wv = wg_ref[...], wv_ref[...]
        cg = _conv3(uge, wg, bg_ref[...])
        cv = _conv3(uve, wv, bv_ref[...])
        sg = _sigmoid(cg)
        dcg = da * cv * (sg * (1.0 + cg * (1.0 - sg)))
        dcv = da * (cg * sg)
        nrow = tr + 8

        def back(dc, w):
            return (w[2:3] * dc + w[1:2] * pltpu.roll(dc, nrow - 1, 0) + w[0:1] * pltpu.roll(dc, nrow - 2, 0))[:tr]

        du_ref[0] = back(dcg, wg).astype(du_ref.dtype)
        du_ref[1] = back(dcv, wv).astype(du_ref.dtype)

        def wgrad(dc, ue):
            d = dc[:tr]
            rows = [jnp.sum(d * pltpu.roll(ue, 2, 0)[8:8 + tr], axis=0, keepdims=True),
                    jnp.sum(d * pltpu.roll(ue, 1, 0)[8:8 + tr], axis=0, keepdims=True),
                    jnp.sum(d * ue[8:8 + tr], axis=0, keepdims=True)]
            return jnp.concatenate(rows, axis=0), jnp.sum(d, axis=0, keepdims=True)

        dwg, dbg = wgrad(dcg, uge)
        dwv, dbv = wgrad(dcv, uve)

        @pl.when(first)
        def _():
            dwg_ref[...] = jnp.zeros_like(dwg_ref)
            dwv_ref[...] = jnp.zeros_like(dwv_ref)
            dbg_ref[...] = jnp.zeros_like(dbg_ref)
            dbv_ref[...] = jnp.zeros_like(dbv_ref)
        dwg_ref[...] += dwg
        dwv_ref[...] += dwv
        dbg_ref[...] += dbg
        dbv_ref[...] += dbv

    tile = lambda off: pl.BlockSpec((tr, tc), lambda j, i: (i, off + j))
    prev = lambda off: pl.BlockSpec((8, tc), lambda j, i: (jnp.maximum(i * hb - 1, 0), off + j))
    nxt = lambda off: pl.BlockSpec((8, tc), lambda j, i: (jnp.minimum((i + 1) * hb, T // 8 - 1), off + j))
    par = lambda rows, off: pl.BlockSpec((rows, tc), lambda j, i: (0, off + j))
    acc = lambda rows: pl.BlockSpec((rows, tc), lambda j, i: (0, j))
    return pl.pallas_call(
        _after(body, 12, deps), grid=(nc, nr),
        in_specs=[tile(0), prev(0), nxt(0), tile(nc), prev(nc), nxt(nc), tile(0), nxt(0),
                  par(3, 0), par(3, nc), par(1, 0), par(1, nc)] + [ANY] * len(deps),
        out_specs=[pl.BlockSpec((2, tr, tc), lambda j, i: (0, i, j)), acc(3), acc(3), acc(1), acc(1)],
        out_shape=[SDS((2, T, D_FF), MXU_DTYPE), SDS((3, D_FF), F32), SDS((3, D_FF), F32),
                   SDS((1, D_FF), F32), SDS((1, D_FF), F32)],
        name=name, compiler_params=_cp("parallel", "arbitrary"))(
            up, up, up, up, up, up, dact, dact, cw, cw, cb, cb, *deps)


def _row_tile(rows, cap):
    t = min(cap, rows)
    t -= t % 8
    while rows % t:
        t -= 8
    return t


def _adamw(w, g, m, v, *, tr, name):
    R, C = w.shape
    assert R % tr == 0, (R, tr)

    def body(w_ref, g_ref, m_ref, v_ref, d_ref, nm_ref, nv_ref):
        gv = g_ref[...]
        mn = ADAM_B1 * m_ref[...] + (1.0 - ADAM_B1) * gv
        vn = ADAM_B2 * v_ref[...] + (1.0 - ADAM_B2) * (gv * gv)
        m_hat = mn / (1.0 - ADAM_B1 ** ADAM_STEP)
        v_hat = vn / (1.0 - ADAM_B2 ** ADAM_STEP)
        d_ref[...] = -ADAM_LR * (m_hat / (jnp.sqrt(v_hat) + ADAM_EPS) + ADAM_WD * w_ref[...])
        nm_ref[...] = mn
        nv_ref[...] = vn

    rows = pl.BlockSpec((tr, C), lambda i: (i, 0))
    return pl.pallas_call(
        body, grid=(R // tr,), in_specs=[rows] * 4, out_specs=[rows] * 3,
        out_shape=[SDS((R, C), F32)] * 3, name=name, compiler_params=_cp("parallel"))(w, g, m, v)


def _sum_slots(r, *, tr, name):
    S, R, C = r.shape
    assert R % tr == 0, (R, tr)

    def body(r_ref, o_ref):
        acc = r_ref[0]
        for s in range(1, S):
            acc = acc + r_ref[s]
        o_ref[...] = acc

    return pl.pallas_call(
        body, grid=(R // tr,), in_specs=[pl.BlockSpec((S, tr, C), lambda i: (0, i, 0))],
        out_specs=pl.BlockSpec((tr, C), lambda i: (i, 0)), out_shape=SDS((R, C), F32),
        name=name, compiler_params=_cp("parallel"))(r)


def _pair_add(g4, h, pos, *, name):
    A, _, r, C = g4.shape
    cs = C if A == N_CHIPS else C // N_CHIPS
    tr = _row_tile(r, 256)
    if A == N_CHIPS:
        g_map, h_map = (lambda t, i, pos: (t, pos[1], i, 0)), (lambda t, i, pos: (t, i, 0))
    else:
        g_map, h_map = (lambda t, i, pos: (0, pos[1], i, t)), (lambda t, i, pos: (0, i, t))

    def body(pos_ref, g_ref, h_ref, o_ref):
        o_ref[...] = (g_ref[...] + h_ref[...]).astype(o_ref.dtype)

    grid_spec = pltpu.PrefetchScalarGridSpec(
        num_scalar_prefetch=1, grid=(N_CHIPS, r // tr),
        in_specs=[pl.BlockSpec((None, None, tr, cs), g_map), pl.BlockSpec((None, tr, cs), h_map)],
        out_specs=pl.BlockSpec((None, tr, cs), lambda t, i, pos: (t, i, 0)))
    return pl.pallas_call(body, grid_spec=grid_spec, out_shape=SDS((N_CHIPS, r, cs), COMM_DTYPE), name=name,
                          compiler_params=_cp("parallel", "parallel"))(pos, g4, h)


def _chip_sum(p, r2, f_into, pos, layer, *, name):
    _, r, cs = p.shape
    tr = _row_tile(r, 256)

    def body(pos_ref, own_ref, r_ref, *rest):
        o_ref = rest[-1]
        o_ref[...] = ((own_ref[...].astype(F32) + r_ref[0].astype(F32)) + r_ref[1].astype(F32)) + r_ref[2].astype(F32)

    in_specs = [pl.BlockSpec((None, tr, cs), lambda i, pos: (pos[0], i, 0)),
                pl.BlockSpec((3, tr, cs), lambda i, pos: (0, i, 0))]
    operands = [pos, p, r2]
    aliases = {}
    if f_into is not None:
        in_specs.append(ANY)
        operands.append(f_into)
        aliases = {3: 0}
    grid_spec = pltpu.PrefetchScalarGridSpec(
        num_scalar_prefetch=1, grid=(r // tr,), in_specs=in_specs,
        out_specs=pl.BlockSpec((None, None, tr, cs), lambda i, pos: (layer, pos[1], i, 0)))
    return pl.pallas_call(body, grid_spec=grid_spec, out_shape=SDS((DEPTH, 2, r, cs), F32), name=name,
                          input_output_aliases=aliases, compiler_params=_cp("parallel"))(*operands)


def _mesh_pos():
    return lax.axis_index("x"), lax.axis_index("y"), lax.axis_index("c")


HBM = pl.BlockSpec(memory_space=pltpu.HBM)
SEM = pl.BlockSpec(memory_space=pltpu.SEMAPHORE)
DATAFLOW = pltpu.SideEffectType.DATAFLOW_SIDE_EFFECTING
CHIP_FLIPS = (2, 1, 3)


def _chip_peers():
    x, y, c = _mesh_pos()
    return 2 * x + y, [(1 - x, y, c), (x, 1 - y, c), (1 - x, 1 - y, c)], (x, y, 1 - c), c


def _split_start(arrays, n_copies, issue, *, name, deps=()):
    k = len(arrays)
    nd = len(deps)

    def body(*refs):
        issue(refs[:k], refs[k + nd], refs[k + nd + 1])
        refs[2 * k + nd + 2][...] = jnp.zeros((8, 128), F32)

    out = pl.pallas_call(
        body, name=name,
        out_shape=(pltpu.SemaphoreType.DMA((n_copies,)), pltpu.SemaphoreType.DMA((n_copies,)),
                   *[pltpu.HBM(a.shape, a.dtype) for a in arrays], SDS((8, 128), F32)),
        in_specs=[HBM] * k + [ANY] * nd, out_specs=(SEM, SEM, *[HBM] * k, pl.BlockSpec(memory_space=pltpu.VMEM)),
        input_output_aliases={i: 2 + i for i in range(k)},
        compiler_params=pltpu.CompilerParams(has_side_effects=DATAFLOW))(
            *[pltpu.with_memory_space_constraint(a, pltpu.HBM) for a in arrays], *deps)
    return (out[0], out[1]), list(out[2:2 + k]), out[2 + k]


def _split_wait(sems, arrays, after, waits, *, name):
    k = len(arrays)
    afters = tuple(after) if isinstance(after, (tuple, list)) else (after,)

    def body(*refs):
        waits(refs[:k], refs[k], refs[k + 1])

    out = pl.pallas_call(
        body, name=name, out_shape=tuple(pltpu.HBM(a.shape, a.dtype) for a in arrays),
        in_specs=[HBM] * k + [SEM, SEM] + [ANY] * len(afters), out_specs=tuple([HBM] * k),
        input_output_aliases={i: i for i in range(k)},
        compiler_params=pltpu.CompilerParams(has_side_effects=DATAFLOW))(*arrays, sems[0], sems[1], *afters)
    return list(out)


def _wait_both(cp):
    cp.wait_send()
    cp.wait_recv()


def _cast_place(shard, pos, dtype, *, name, layer=None, slots=N_CHIPS, which=0):
    R, C = shard.shape[-2:]
    tr = R if R % 8 else _row_tile(R, 256)
    if layer is None:
        in_spec = pl.BlockSpec((tr, C), lambda i, pos: (i, 0))
    else:
        in_spec = pl.BlockSpec((None, tr, C), lambda i, pos: (layer, i, 0))

    def body(pos_ref, x_ref, o_ref):
        o_ref[...] = x_ref[...].astype(o_ref.dtype)

    grid_spec = pltpu.PrefetchScalarGridSpec(
        num_scalar_prefetch=1, grid=(R // tr,), in_specs=[in_spec],
        out_specs=pl.BlockSpec((None, tr, C), lambda i, pos: (pos[which], i, 0)))
    return pl.pallas_call(body, grid_spec=grid_spec, out_shape=SDS((slots, R, C), dtype), name=name,
                          compiler_params=_cp("parallel"))(pos, shard)


def _device_peers():
    x, y, c = _mesh_pos()
    peers = [(x ^ ((f >> 2) & 1), y ^ ((f >> 1) & 1), c ^ (f & 1)) for f in range(1, N_DEV)]
    return 4 * x + 2 * y + c, peers


class _Gather:
    def __init__(self, lands, name, deps=(), all_devices=False):
        n = len(lands)
        self.name = name
        npeer = N_DEV - 1 if all_devices else N_CHIPS - 1

        def copies(refs, ss, rs):
            me, peers = _device_peers() if all_devices else _chip_peers()[:2]
            return [pltpu.make_async_remote_copy(
                src_ref=refs[w].at[me], dst_ref=refs[w].at[me], send_sem=ss.at[npeer * w + p],
                recv_sem=rs.at[npeer * w + p], device_id=peers[p], device_id_type=MESH)
                for w in range(n) for p in range(npeer)]

        def issue(refs, ss, rs):
            for cp in copies(refs, ss, rs):
                cp.start()

        def waits(refs, ss, rs):
            for cp in copies(refs, ss, rs):
                _wait_both(cp)

        self._waits = waits
        self.sems, self.arrays, self.token = _split_start(list(lands), npeer * n, issue, name=name + "_start",
                                                          deps=deps)

    def wait(self, after):
        return _split_wait(self.sems, self.arrays, after, self._waits, name=self.name + "_wait")


def _swap_halves_start(g4s, *, name):
    n = len(g4s)
    lands = [lax.empty((g.shape[0],) + g.shape[2:], g.dtype) for g in g4s]

    def copies(refs, ss, rs):
        _, _, sibling, c = _chip_peers()
        return [pltpu.make_async_remote_copy(
            src_ref=refs[w].at[:, 1 - c], dst_ref=refs[n + w], send_sem=ss.at[w], recv_sem=rs.at[w],
            device_id=sibling, device_id_type=MESH) for w in range(n)]

    def issue(refs, ss, rs):
        for cp in copies(refs, ss, rs):
            cp.start()

    def waits(refs, ss, rs):
        for cp in copies(refs, ss, rs):
            _wait_both(cp)

    sems, arrays, token = _split_start(list(g4s) + lands, n, issue, name=name + "_start")
    return sems, arrays, token, waits


def _scatter_start(parts, *, name, deps=()):
    n = len(parts)
    lands = [lax.empty((3,) + p.shape[1:], p.dtype) for p in parts]

    def copies(refs, ss, rs):
        me, peers, _, _ = _chip_peers()
        return [pltpu.make_async_remote_copy(
            src_ref=refs[w].at[me ^ CHIP_FLIPS[p]], dst_ref=refs[n + w].at[p],
            send_sem=ss.at[3 * w + p], recv_sem=rs.at[3 * w + p], device_id=peers[p], device_id_type=MESH)
            for w in range(n) for p in range(3)]

    def issue(refs, ss, rs):
        for cp in copies(refs, ss, rs):
            cp.start()

    def waits(refs, ss, rs):
        for cp in copies(refs, ss, rs):
            _wait_both(cp)

    sems, arrays, token = _split_start(list(parts) + lands, 3 * n, issue, name=name + "_start", deps=deps)
    return sems, arrays, token, waits


def _pair_share_start(fs, layer, *, name):
    n = len(fs)

    def copies(refs, ss, rs):
        _, _, sibling, c = _chip_peers()
        return [pltpu.make_async_remote_copy(
            src_ref=refs[w].at[layer, c], dst_ref=refs[w].at[layer, c], send_sem=ss.at[w], recv_sem=rs.at[w],
            device_id=sibling, device_id_type=MESH) for w in range(n)]

    def issue(refs, ss, rs):
        for cp in copies(refs, ss, rs):
            cp.start()

    def waits(refs, ss, rs):
        for cp in copies(refs, ss, rs):
            _wait_both(cp)

    sems, arrays, token = _split_start(list(fs), n, issue, name=name + "_start")
    return sems, arrays, token, waits


BIG = ('w_in', 'w_proj_a', 'w_proj_b', 'w_proj_c', 'w_out', 'w_up', 'w_down')
BIG_SHARD_AXIS = {'w_in': 2, 'w_proj_a': 2, 'w_proj_b': 2, 'w_proj_c': 2, 'w_out': 1, 'w_up': 2, 'w_down': 1}
SMALL = ('norm1', 'q_norm', 'k_norm', 'sinks', 'w_pool', 'pool_scale', 'sgu_v_norm', 'w_s', 'b_s', 'norm2',
         'conv_b', 'conv_w')
WEIGHTS = ('norm1', 'w_in', 'q_norm', 'k_norm', 'sinks', 'w_pool', 'pool_scale', 'sgu_v_norm', 'w_s', 'b_s',
           'w_proj_a', 'w_proj_b', 'w_proj_c', 'w_out', 'norm2', 'w_up', 'conv_w', 'conv_b', 'w_down')


def _rope_tables(positions):
    inv_freq = ROPE_THETA ** (-jnp.arange(0, HEAD_DIM, 2, dtype=F32) / HEAD_DIM)
    ang = positions.astype(F32)[:, None] * inv_freq
    cos, sin = jnp.cos(ang), jnp.sin(ang)
    c = jnp.concatenate([cos, cos], axis=1)
    s = jnp.concatenate([-sin, sin], axis=1)
    return jnp.concatenate([c, c], axis=1), jnp.concatenate([s, s], axis=1)


def _block_diag4(w):
    out = jnp.zeros((POOL_WIDTH, POOL_WIDTH), w.dtype)
    for g in range(4):
        out = lax.dynamic_update_slice(out, w[g], (g * HEAD_DIM, g * HEAD_DIM))
    return out


def _local_step(x, target, cos, sin, sp, sched):
    T = x.shape[0]
    tm1 = min(1024, T)
    tm = min(512, T)
    tr = min(256, T)
    tkt = min(1024, T)
    seg = _seg_matrix(256, HEAD_DIM)
    saved = []
    xl = x
    for l in range(DEPTH):
        p = f"l{l}_"
        c = dict(
            g1=sp['norm1'][l][None], g2=sp['norm2'][l][None],
            wbd=_block_diag4(sp['w_pool'][l]).astype(MXU_DTYPE), scale=sp['pool_scale'][l][None],
            gq=jnp.tile(sp['q_norm'][l], 4)[None], gk=jnp.tile(sp['k_norm'][l], 2)[None],
            sinks=jnp.broadcast_to(sp['sinks'][l][:, None], (N_Q_HEADS, 128)),
            wtril=jnp.tril(sp['w_s'][l]).astype(MXU_DTYPE),
            bexp=jnp.repeat(sp['b_s'][l].T, HEAD_DIM, axis=1), vn=jnp.tile(sp['sgu_v_norm'][l], 4)[None],
            cb=sp['conv_b'][l][None])
        h1 = _rms_fwd(xl, c['g1'], tr=tr, name=p + "rms1", deps=sched.start_tokens() if l == 0 else ())
        c['w_in'] = sched.weight('w_in', l, h1)
        z = _mm(h1, c['w_in'], mode='nn', b_sharded=True, tm=tm1, tn=1152, tk=D_MODEL, name=p + "in_proj")
        pa = _pool_fwd(z, c['wbd'], c['scale'], tr=tr, name=p + "pool")
        q, k, v = _qkv_prep(z, cos, sin, c['gq'], c['gk'], seg, tr=tr, name=p + "qkv_prep")
        at = _attn_fwd(q, k, v, c['sinks'], name=p + "attn")
        sg = _sgu_fwd(z, c['wtril'], c['bexp'], c['vn'], seg, tr=tr, name=p + "sgu")
        for n in ('w_proj_a', 'w_proj_b', 'w_proj_c', 'w_out'):
            c[n] = sched.weight(n, l, (pa, at, sg))
        merged, y3 = _merge_fwd(pa, at, sg, c['w_proj_a'], c['w_proj_b'], c['w_proj_c'], z,
                                tm=tm, tn=512, name=p + "merge")
        x1 = _mm(merged, c['w_out'], mode='nn', add=xl, tm=tm, tn=D_MODEL, tk=D_MODEL, name=p + "out_proj")
        h2 = _rms_fwd(x1, c['g2'], tr=tr, name=p + "rms2")
        for n in ('w_up', 'conv_w', 'w_down'):
            c[n] = sched.weight(n, l, h2)
        up = _mm(h2, c['w_up'], mode='nn', b_sharded=True, tm=tm1, tn=1408, tk=D_MODEL, name=p + "up_proj")
        act = _conv_act_fwd(up, c['conv_w'], c['cb'], tr=tr, tc=1408, name=p + "conv_act")
        x2 = _mm(act, c['w_down'], mode='nn', add=x1, tm=tm, tn=D_MODEL, tk=1408, name=p + "down_proj")
        saved.append(dict(c, x=xl, h1=h1, z=z, pa=pa, q=q, k=k, v=v, at=at, sg=sg, merged=merged, y3=y3,
                          x1=x1, h2=h2, up=up, act=act))
        xl = x2

    loss_row, dx, dxb = _loss_head(xl, target, tr=tr, name="loss_head")

    gs = {n: [None] * DEPTH for n in SMALL}
    for l in reversed(range(DEPTH)):
        p = f"l{l}_b_"
        s = saved[l]
        gb = {}
        dact = _mm(dxb, s['w_down'], mode='nt', tm=tm1, tn=1408, tk=D_MODEL, name=p + "down_dx")
        gb['w_down'] = _mm(s['act'], dxb, mode='tn', tm=1408, tn=D_MODEL, tk=tkt, name=p + "down_dw")
        toks = sched.slot(l, 'down', gb['w_down'])
        dup, dwg, dwv, dbg, dbv = _conv_act_bwd(s['up'], s['conv_w'], s['cb'], dact, tr=min(512, T), tc=256,
                                                name=p + "conv_act", deps=toks)
        gs['conv_w'][l] = jnp.concatenate([dwg, dwv], axis=1)
        gs['conv_b'][l] = jnp.concatenate([dbg, dbv], axis=1)[0]
        toks = sched.slot(l, 'conv', dup)
        dh2 = _mm_nt_sharded(dup, s['w_up'], tm=tm, name=p + "up_dx", deps=toks)
        for half in range(2):
            gb['w_up'] = _mm(s['h2'], dup, mode='tn', b_lead=half, tm=D_MODEL, tn=1408, tk=tkt,
                             out_into=gb.get('w_up'), out_joff=2 * half, out_n=2 * D_FF, name=p + f"up_dw{half}")
        toks = sched.slot(l, 'ffn', gb['w_up'], gb)
        dx1, dx1b, dg2 = _rms_bwd(s['x1'], s['g2'], dh2, dx, tr=tr, name=p + "rms2", deps=toks)
        gs['norm2'][l] = dg2[0]
        dmerged = _mm(dx1b, s['w_out'], mode='nt', tm=tm, tn=D_MODEL, tk=D_MODEL, name=p + "out_dx")
        gb['w_out'] = _mm(s['merged'], dx1b, mode='tn', tm=D_MODEL, tn=D_MODEL, tk=tkt, name=p + "out_dw")
        dy3, dz = _merge_bwd(dmerged, s['y3'], s['z'], tr=min(1024, T), tn=512, name=p + "merge")
        toks = sched.slot(l, 'mid', dz)
        dbr = []
        for idx, (wn, opn, width) in enumerate((('w_proj_a', 'pa', POOL_WIDTH), ('w_proj_b', 'at', ATTN_WIDTH),
                                                ('w_proj_c', 'sg', SGU_WIDTH))):
            dbr.append(_mm(dy3, s[wn], mode='nt', a_lead=idx, tm=tm, tn=width, tk=D_MODEL,
                           name=p + f"proj{idx}_dx", deps=toks if idx == 0 else ()))
            gb[wn] = _mm(s[opn], dy3, mode='tn', b_lead=idx, tm=width, tn=D_MODEL, tk=tkt,
                         name=p + f"proj{idx}_dw")
        dpa, dat, dsg = dbr
        dq, dkc, dkp, dvc, dvp, dsk = _attn_bwd(s['q'], s['k'], s['v'], s['sinks'], dat, name=p + "attn")
        gs['sinks'][l] = dsk[:, 0]
        toks = sched.slot(l, 'attn', dq)
        dz, dgq, dgk, dwbd, dsc = _mixer_ab_bwd(s['z'], cos, sin, s['gq'], s['gk'], seg, dq, dkc, dkp, dvc, dvp,
                                                dpa, s['wbd'], s['scale'], dz, tr=tr, name=p + "qkv_pool", deps=toks)
        gs['q_norm'][l] = dgq[0, :HEAD_DIM]
        gs['k_norm'][l] = dgk[0, :HEAD_DIM]
        gs['w_pool'][l] = jnp.stack([dwbd[g * HEAD_DIM:(g + 1) * HEAD_DIM, g * HEAD_DIM:(g + 1) * HEAD_DIM]
                                     for g in range(4)])
        gs['pool_scale'][l] = dsc[0]
        dz, dws, dbrows, dvn = _sgu_bwd(s['z'], s['wtril'], s['bexp'], s['vn'], seg, dsg, dz, tr=tr, name=p + "sgu")
        gs['w_s'][l] = dws
        gs['b_s'][l] = dbrows[:, ::HEAD_DIM].T
        gs['sgu_v_norm'][l] = dvn[0, :HEAD_DIM]
        dh1 = _mm_nt_sharded(dz, s['w_in'], tm=tm, name=p + "in_dx")
        gb['w_in'] = _mm(s['h1'], dz, mode='tn', tm=D_MODEL, tn=1152, tk=tkt, name=p + "in_dw")
        toks = sched.slot(l, 'mix', gb['w_in'], gb)
        dx, dxb, dg1 = _rms_bwd(s['x'], s['g1'], dh1, dx1, tr=tr, name=p + "rms1", deps=toks)
        gs['norm1'][l] = dg1[0]
    gs = {n: jnp.stack(v) for n, v in gs.items()}
    return loss_row, dx, gs


GROUP_F = ('w_down', 'w_up')
GROUP_M = ('w_out', 'w_proj_a', 'w_proj_b', 'w_proj_c', 'w_in')
ROW_SHARDED = ('w_out', 'w_down')

REDUCE_PLAN = {
    (1, 'ffn'): (('S1', 'F', 1),),
    (1, 'mid'): (('W1', 'F', 1),),
    (1, 'mix'): (('S1', 'M', 1),),
    (0, 'down'): (('W1', 'M', 1),),
    (0, 'conv'): (('W2', 'F', 1),),
    (0, 'ffn'): (('S1', 'F', 0), ('W3', 'F', 1)),
    (0, 'mid'): (('W1', 'F', 0),),
    (0, 'attn'): (('W2', 'M', 1),),
    (0, 'mix'): (('S1', 'M', 0), ('W3', 'M', 1)),
}
REDUCE_TAIL_A = (('W1', 'M', 0), ('W2', 'F', 0))
REDUCE_TAIL_B = (('W3', 'F', 0),)
REDUCE_TAIL_C = (('W2', 'M', 0), ('W3', 'M', 0))


class _Comm:
    def __init__(self, w, pos):
        self.pos = pos
        groups = {'a': [('w_in', 0)],
                  'b': [(n, 0) for n in ('w_proj_a', 'w_proj_b', 'w_proj_c', 'w_out')],
                  'c': [(n, 0) for n in ('w_up', 'conv_w', 'w_down')],
                  'd': [(n, 1) for n in BIG] + [('conv_w', 1)]}
        self.gathers, self.group_of, self.weights = {}, {}, {}
        self.tokens = []
        for g, ks in groups.items():
            lands = [_cast_place(w[n], pos, F32 if n == 'conv_w' else MXU_DTYPE, layer=l, name=f"gw_place_{n}{l}")
                     for n, l in ks]
            self.gathers[g] = (_Gather(lands, "gw_" + g, deps=self.tokens[-1:]), ks)
            self.tokens.append(self.gathers[g][0].token)
            self.group_of.update({k: g for k in ks})
        self.red = {}
        self.final = {}

    def start_tokens(self):
        return self.tokens[-1:]

    def weight(self, name, layer, after):
        if (name, layer) not in self.weights:
            gather, ks = self.gathers[self.group_of[(name, layer)]]
            for (n, l), full in zip(ks, gather.wait(after)):
                if n == 'conv_w' or n.startswith('w_proj'):
                    full = full.transpose(1, 0, 2).reshape(full.shape[1], -1)
                elif n in ROW_SHARDED:
                    full = full.reshape(-1, full.shape[2])
                self.weights[(n, l)] = full
        return self.weights[(name, layer)]

    def slot(self, layer, slot, after, grads=None):
        tokens = []
        for step, grp, lyr in REDUCE_PLAN.get((layer, slot), ()):
            tok = self._step(step, grp, lyr, after, grads)
            if tok is not None:
                tokens.append(tok)
        return tokens

    def tail(self, steps, after, deps=()):
        toks = (self._step(step, grp, lyr, after, None, deps) for step, grp, lyr in steps)
        return [t for t in toks if t is not None]

    def shards(self):
        return {n: f.reshape(DEPTH, 2 * f.shape[2], f.shape[3]) for n, f in self.final.items()}

    def _step(self, step, grp, layer, after, grads, deps=()):
        names = GROUP_F if grp == 'F' else GROUP_M
        tag = f"{grp.lower()}{layer}"
        st = self.red.setdefault((grp, layer), {})
        n = len(names)
        if step == 'S1':
            g4s = []
            for nm in names:
                g = grads[nm]
                R, C = g.shape
                g4s.append(g.reshape(N_CHIPS, 2, R // (2 * N_CHIPS), C) if nm in ROW_SHARDED
                           else g.reshape(1, 2, R // 2, C))
            st['s1'] = _swap_halves_start(g4s, name="rs1_" + tag)
            return st['s1'][2]
        if step == 'W1':
            sems, arrays, _, waits = st.pop('s1')
            arrays = _split_wait(sems, arrays, after, waits, name=f"rs1_{tag}_wait")
            parts = [_pair_add(arrays[i], arrays[n + i], self.pos, name=f"pair_add_{tag}_{names[i]}")
                     for i in range(n)]
            st['s2'] = _scatter_start(parts, name="rs2_" + tag, deps=deps)
            return st['s2'][2]
        if step == 'W2':
            sems, arrays, _, waits = st.pop('s2')
            arrays = _split_wait(sems, arrays, after, waits, name=f"rs2_{tag}_wait")
            fs = [_chip_sum(arrays[i], arrays[n + i], self.final.get(names[i]), self.pos, layer,
                            name=f"chip_sum_{tag}_{names[i]}") for i in range(n)]
            st['s3'] = _pair_share_start(fs, layer, name="rs3_" + tag)
            return st['s3'][2]
        sems, arrays, _, waits = st.pop('s3')
        self.final.update(zip(names, _split_wait(sems, arrays, after, waits, name=f"rs3_{tag}_wait")))
        return None


def _pack(arrays):
    flat = []
    for a in arrays:
        f = a.reshape(-1).astype(F32)
        flat.append(jnp.pad(f, (0, (-f.shape[0]) % 128)))
    v = jnp.concatenate(flat)
    v = jnp.pad(v, (0, (-v.shape[0]) % 1024))
    return v.reshape(-1, 128)


def _unpack(pack, shapes):
    v = pack.reshape(-1)
    out, off = [], 0
    for shp in shapes:
        nel = int(np.prod(shp))
        out.append(v[off:off + nel].reshape(shp))
        off += nel + (-nel) % 128
    return out


def kernel(x, positions, norm1, w_in, q_norm, k_norm, sinks, w_pool, pool_scale, sgu_v_norm, w_s, b_s, w_proj_a, w_proj_b, w_proj_c, w_out, norm2, w_up, conv_w, conv_b, w_down, loss_target, m_norm1, m_w_in, m_q_norm, m_k_norm, m_sinks, m_w_pool, m_pool_scale, m_sgu_v_norm, m_w_s, m_b_s, m_w_proj_a, m_w_proj_b, m_w_proj_c, m_w_out, m_norm2, m_w_up, m_conv_w, m_conv_b, m_w_down, v_norm1, v_w_in, v_q_norm, v_k_norm, v_sinks, v_w_pool, v_pool_scale, v_sgu_v_norm, v_w_s, v_b_s, v_w_proj_a, v_w_proj_b, v_w_proj_c, v_w_out, v_norm2, v_w_up, v_conv_w, v_conv_b, v_w_down):
    w = dict(norm1=norm1, w_in=w_in, q_norm=q_norm, k_norm=k_norm, sinks=sinks, w_pool=w_pool, pool_scale=pool_scale,
             sgu_v_norm=sgu_v_norm, w_s=w_s, b_s=b_s, w_proj_a=w_proj_a, w_proj_b=w_proj_b, w_proj_c=w_proj_c,
             w_out=w_out, norm2=norm2, w_up=w_up, conv_w=conv_w, conv_b=conv_b, w_down=w_down)
    m = dict(norm1=m_norm1, w_in=m_w_in, q_norm=m_q_norm, k_norm=m_k_norm, sinks=m_sinks, w_pool=m_w_pool,
             pool_scale=m_pool_scale, sgu_v_norm=m_sgu_v_norm, w_s=m_w_s, b_s=m_b_s, w_proj_a=m_w_proj_a,
             w_proj_b=m_w_proj_b, w_proj_c=m_w_proj_c, w_out=m_w_out, norm2=m_norm2, w_up=m_w_up, conv_w=m_conv_w,
             conv_b=m_conv_b, w_down=m_w_down)
    v = dict(norm1=v_norm1, w_in=v_w_in, q_norm=v_q_norm, k_norm=v_k_norm, sinks=v_sinks, w_pool=v_w_pool,
             pool_scale=v_pool_scale, sgu_v_norm=v_sgu_v_norm, w_s=v_w_s, b_s=v_b_s, w_proj_a=v_w_proj_a,
             w_proj_b=v_w_proj_b, w_proj_c=v_w_proj_c, w_out=v_w_out, norm2=v_norm2, w_up=v_w_up, conv_w=v_conv_w,
             conv_b=v_conv_b, w_down=v_w_down)
    chip = 2 * lax.axis_index("x") + lax.axis_index("y")
    core = lax.axis_index("c")

    pos = jnp.stack([chip, core, 2 * chip + core]).astype(jnp.int32)
    comm = _Comm(w, pos)

    cos, sin = _rope_tables(positions[0])
    sp = {n: w[n] for n in SMALL if n != 'conv_w'}
    loss_row, dx, gs = _local_step(x[0], loss_target[0], cos, sin, sp, comm)
    loss = lax.psum(loss_row[0, 0], ("x", "y", "c"))

    delta, new_m, new_v = {}, {}, {}

    def adamw_big(names, grads):
        for n in names:
            shp = w[n].shape
            two_d = lambda a: a.reshape(shp[0] * shp[1], shp[2])
            d, nm, nv = _adamw(two_d(w[n]), two_d(grads[n]), two_d(m[n]), two_d(v[n]),
                               tr=_row_tile(shp[0] * shp[1], 256), name=f"adamw_{n}")
            delta[n], new_m[n], new_v[n] = d.reshape(shp), nm.reshape(shp), nv.reshape(shp)

    small_shapes = [gs[n].shape for n in SMALL]
    small_pack = _pack([gs[n] for n in SMALL])
    small = _Gather([_cast_place(small_pack, pos, F32, slots=N_DEV, which=2, name="small_place")], "small_gather",
                    all_devices=True)
    comm.tail(REDUCE_TAIL_A, dx, deps=[small.token])
    comm.tail(REDUCE_TAIL_B, dx)
    adamw_big(GROUP_F, comm.shards())
    red = _sum_slots(small.wait(new_v[GROUP_F[-1]])[0], tr=small_pack.shape[0], name="small_sum")
    g_small = dict(zip(SMALL, _unpack(red, small_shapes)))
    comm.tail(REDUCE_TAIL_C, red)
    grads = comm.shards()
    grads.update(g_small)
    shard_cols = conv_w.shape[2]
    grads['conv_w'] = lax.dynamic_slice_in_dim(g_small['conv_w'], chip * shard_cols, shard_cols, axis=2)

    adamw_big(GROUP_M, grads)
    shapes = [w[n].shape for n in SMALL]
    packs = [_pack([src[n] for n in SMALL]) for src in (w, grads, m, v)]
    d, nm, nv = _adamw(*packs, tr=packs[0].shape[0], name="adamw_small")
    for dst, src in ((delta, d), (new_m, nm), (new_v, nv)):
        dst.update(zip(SMALL, _unpack(src, shapes)))

    return (loss, dx[None], *[grads[n] for n in WEIGHTS], *[delta[n] for n in WEIGHTS],
            *[new_m[n] for n in WEIGHTS], *[new_v[n] for n in WEIGHTS])
```

```python
import functools
import math

import numpy as np
import jax
import jax.numpy as jnp
from jax import lax
from jax.experimental import pallas as pl
from jax.experimental.pallas import tpu as pltpu

F32 = jnp.float32
MXU_DTYPE = jnp.bfloat16
COMM_DTYPE = jnp.bfloat16

D_MODEL = 1024
DEPTH = 2
HEAD_DIM = 64
POOL_WINDOWS = (2, 4, 8, 16)
POOL_WIDTH = 256
N_Q_HEADS = 8
ATTN_BLOCK = 128
ATTN_WIDTH = 512
KV_WIDTH = 128
CHUNK = 128
SGU_WIDTH = 256
IN_COLS = 4608
GATE_COL0 = 1536
D_FF = 2816
ROPE_THETA = 10000.0
EPS = 1e-6
ADAM_LR, ADAM_B1, ADAM_B2, ADAM_EPS, ADAM_WD, ADAM_STEP = 0.001, 0.9, 0.999, 1e-08, 0.01, 10

N_CHIPS = 4
N_DEV = 8
VMEM_LIMIT_BYTES = 56 * 1024 * 1024
NEG_BIG = -1e30
MESH = pl.DeviceIdType.MESH
ANY = pl.BlockSpec(memory_space=pl.ANY)

SDS = jax.ShapeDtypeStruct


def _cp(*sem):
    return pltpu.CompilerParams(dimension_semantics=sem, vmem_limit_bytes=VMEM_LIMIT_BYTES)


def _dot(a, b, dims=((1,), (0,))):
    return lax.dot_general(a.astype(MXU_DTYPE), b.astype(MXU_DTYPE), (dims, ((), ())),
                           preferred_element_type=F32)


NT = ((1,), (1,))
TN = ((0,), (0,))


def _split_dot(x, m):
    hi = x.astype(MXU_DTYPE)
    lo = (x - hi.astype(F32)).astype(MXU_DTYPE)
    return _dot(hi, m) + _dot(lo, m)


def _seg_matrix(width, seg):
    idx = np.arange(width) // seg
    return jnp.asarray((idx[:, None] == idx[None, :]).astype(np.float32), dtype=MXU_DTYPE)


def _lane(shape):
    return lax.broadcasted_iota(jnp.int32, shape, len(shape) - 1)


def _row(shape):
    return lax.broadcasted_iota(jnp.int32, shape, 0)


def _full(shape):
    nd = len(shape)
    return pl.BlockSpec(shape, lambda *_: (0,) * nd)


def _gelu(x):
    k = math.sqrt(2.0 / math.pi)
    th = jnp.tanh(k * (x + 0.044715 * (x * x * x)))
    return 0.5 * x * (1.0 + th)


def _gelu_and_grad(x):
    k = math.sqrt(2.0 / math.pi)
    x2 = x * x
    th = jnp.tanh(k * (x + 0.044715 * (x2 * x)))
    g = 0.5 * x * (1.0 + th)
    dg = 0.5 * (1.0 + th) + 0.5 * x * (1.0 - th * th) * (k * (1.0 + 3.0 * 0.044715 * x2))
    return g, dg


def _sigmoid(x):
    return 0.5 * jnp.tanh(0.5 * x) + 0.5


def _swap_halves(x):
    w = x.shape[-1]
    first = (_lane(x.shape) % HEAD_DIM) < (HEAD_DIM // 2)
    return jnp.where(first, pltpu.roll(x, w - HEAD_DIM // 2, 1), pltpu.roll(x, HEAD_DIM // 2, 1))


def _tile_lanes(x, reps):
    return x if reps == 1 else jnp.concatenate([x] * reps, axis=1)


def _fold_lanes(x, period):
    w = x.shape[-1]
    while w > period:
        w //= 2
        x = x + pltpu.roll(x, w, 1)
    return x


def _mm(a, b, *, mode, tm, tn, tk, out_dtype=F32, add=None, name,
        a_lead=None, b_lead=None, b_sharded=False, out_into=None,
        b_koff=0, out_joff=0, out_n=None, deps=()):
    ash = a.shape[1:] if a_lead is not None else a.shape
    bsh = b.shape[1:] if b_lead is not None else b.shape
    if b_sharded:
        bsh = (b.shape[1], N_CHIPS * b.shape[2])
    if mode == 'nn':
        (M, K), (K2, N) = ash, bsh
    elif mode == 'nt':
        (M, K), (N, K2) = ash, bsh
    else:
        (K, M), (K2, N) = ash, bsh
    assert K == K2 or (mode == 'nt' and K2 > K), (ash, bsh, mode)
    assert M % tm == 0 and N % tn == 0 and K % tk == 0, (M, N, K, tm, tn, tk)
    nk = K // tk
    dims = {'nn': ((1,), (0,)), 'nt': NT, 'tn': TN}[mode]

    def lead(spec_shape, imap, lead_idx):
        if lead_idx is None:
            return pl.BlockSpec(spec_shape, imap)
        return pl.BlockSpec((None,) + spec_shape, lambda i, j, k: (lead_idx,) + imap(i, j, k))

    if mode == 'tn':
        a_spec = lead((tk, tm), lambda i, j, k: (k, i), a_lead)
    else:
        a_spec = lead((tm, tk), lambda i, j, k: (i, k), a_lead)
    if b_sharded:
        per = b.shape[2] // (tk if mode == 'nt' else tn)
        assert per * (tk if mode == 'nt' else tn) == b.shape[2] and mode != 'tn'
        if mode == 'nt':
            b_spec = pl.BlockSpec((None, tn, tk), lambda i, j, k: ((k + b_koff) // per, j, (k + b_koff) % per))
        else:
            b_spec = pl.BlockSpec((None, tk, tn), lambda i, j, k: (j // per, k, j % per))
    elif mode == 'nt':
        b_spec = lead((tn, tk), lambda i, j, k: (j, k + b_koff), b_lead)
    else:
        b_spec = lead((tk, tn), lambda i, j, k: (k, j), b_lead)
    o_spec = pl.BlockSpec((tm, tn), lambda i, j, k: (i, j + out_joff))
    n_out = N if out_n is None else out_n
    in_specs = [a_spec, b_spec]
    operands = [a, b]
    if add is not None:
        in_specs.append(pl.BlockSpec((tm, tn), lambda i, j, k: (i, j)))
        operands.append(add)
    aliases = {}
    if out_into is not None:
        in_specs.append(ANY)
        operands.append(out_into)
        aliases = {len(operands) - 1: 0}
    in_specs += [ANY] * len(deps)
    operands += list(deps)
    has_add = add is not None
    acc_in_out = nk > 1 and out_dtype == F32

    def body(*refs):
        a_ref, b_ref = refs[0], refs[1]
        pos = 2
        add_ref = None
        if has_add:
            add_ref = refs[pos]
            pos += 1
        if out_into is not None:
            pos += 1
        pos += len(deps)
        o_ref = refs[pos]
        acc_ref = refs[pos + 1] if (nk > 1 and not acc_in_out) else None
        p = _dot(a_ref[...], b_ref[...], dims)
        if nk == 1:
            if has_add:
                p = p + add_ref[...]
            o_ref[...] = p.astype(o_ref.dtype)
            return
        k = pl.program_id(2)
        tgt = o_ref if acc_in_out else acc_ref

        @pl.when(k == 0)
        def _():
            tgt[...] = p + add_ref[...] if has_add else p

        @pl.when(k > 0)
        def _():
            tgt[...] += p

        if not acc_in_out:
            @pl.when(k == nk - 1)
            def _():
                o_ref[...] = acc_ref[...].astype(o_ref.dtype)

    out_shape = SDS((M, n_out), out_dtype)
    scratch = [pltpu.VMEM((tm, tn), F32)] if (nk > 1 and not acc_in_out) else []
    return pl.pallas_call(
        body, grid=(M // tm, N // tn, nk), in_specs=in_specs, out_specs=o_spec, out_shape=out_shape,
        scratch_shapes=scratch, input_output_aliases=aliases, name=name,
        compiler_params=_cp("parallel", "parallel", "arbitrary"))(*operands)


def _rms_bwd_rows(xv, g, dh, dres):
    r = lax.rsqrt(jnp.mean(xv * xv, axis=-1, keepdims=True) + EPS)
    xh = xv * r
    gy = dh * g
    dx = r * (gy - xh * jnp.mean(xh * gy, axis=-1, keepdims=True)) + dres
    return dx, jnp.sum(dh * xh, axis=0, keepdims=True)


def _mm_nt_sharded_rms(a, b, x, g, dres, *, tm, name, deps=()):
    a3 = a if a.ndim == 3 else a[None]
    A, M, ka = a3.shape
    S, N, ns = b.shape
    per = S // A
    assert ka == per * ns and M % tm == 0 and N == x.shape[1], (a3.shape, b.shape, x.shape)

    def body(a_ref, b_ref, x_ref, g_ref, dres_ref, dx_ref, dxb_ref, dg_ref):
        acc = None
        for s in range(S):
            lo = (s % per) * ns
            p = _dot(a_ref[s // per, :, lo:lo + ns], b_ref[s], NT)
            acc = p if acc is None else acc + p
        dx, dg = _rms_bwd_rows(x_ref[...], g_ref[...], acc, dres_ref[...])
        dx_ref[...] = dx
        dxb_ref[...] = dx.astype(dxb_ref.dtype)

        @pl.when(pl.program_id(0) == 0)
        def _():
            dg_ref[...] = jnp.zeros_like(dg_ref)
        dg_ref[...] += dg

    rows = pl.BlockSpec((tm, N), lambda i: (i, 0))
    return pl.pallas_call(
        _after(body, 5, deps), grid=(M // tm,),
        in_specs=[pl.BlockSpec((A, tm, ka), lambda i: (0, i, 0)), pl.BlockSpec((S, N, ns), lambda i: (0, 0, 0)),
                  rows, _full((1, N)), rows] + [ANY] * len(deps),
        out_specs=[rows, rows, _full((1, N))],
        out_shape=[SDS((M, N), F32), SDS((M, N), MXU_DTYPE), SDS((1, N), F32)], name=name,
        compiler_params=_cp("arbitrary"))(a3, b, x, g, dres, *deps)


def _norm_mm(x, g, b, *, tm, tn, name, deps=()):
    M, K = x.shape
    S, K2, ns = b.shape
    per = ns // tn
    assert K == K2 and per * tn == ns and M % tm == 0, (x.shape, b.shape)

    def body(x_ref, g_ref, b_ref, o_ref, h_ref):
        @pl.when(pl.program_id(1) == 0)
        def _():
            xv = x_ref[...]
            r = lax.rsqrt(jnp.mean(xv * xv, axis=-1, keepdims=True) + EPS)
            h_ref[...] = (xv * r * g_ref[...]).astype(h_ref.dtype)
        o_ref[...] = _dot(h_ref[...], b_ref[...])

    return pl.pallas_call(
        _after(body, 3, deps), grid=(M // tm, S * per),
        in_specs=[pl.BlockSpec((tm, K), lambda i, j: (i, 0)), _full((1, K)),
                  pl.BlockSpec((None, K, tn), lambda i, j: (j // per, 0, j % per))] + [ANY] * len(deps),
        out_specs=[pl.BlockSpec((tm, tn), lambda i, j: (i, j)), pl.BlockSpec((tm, K), lambda i, j: (i, 0))],
        out_shape=[SDS((M, S * ns), F32), SDS((M, K), MXU_DTYPE)], name=name,
        compiler_params=_cp("parallel", "arbitrary"))(x, g, b, *deps)


def _after(body, n_in, deps):
    nd = len(deps)
    if nd == 0:
        return body
    return lambda *refs: body(*refs[:n_in], *refs[n_in + nd:])


def _loss_head(y, target, *, tr, name):
    T, D = y.shape

    def body(y_ref, t_ref, loss_ref, dy_ref, dyb_ref):
        i = pl.program_id(0)
        d = y_ref[...] - t_ref[...]
        dy = d * (1.0 / D)
        dy_ref[...] = dy
        dyb_ref[...] = dy.astype(dyb_ref.dtype)
        part = jnp.sum(jnp.sum(d * d, axis=1, keepdims=True), axis=0, keepdims=True) * (0.5 / D)

        @pl.when(i == 0)
        def _():
            loss_ref[...] = jnp.zeros_like(loss_ref)
        loss_ref[...] += jnp.broadcast_to(part, loss_ref.shape)

    rows = pl.BlockSpec((tr, D), lambda i: (i, 0))
    return pl.pallas_call(
        body, grid=(T // tr,), in_specs=[rows, rows],
        out_specs=[_full((1, 128)), rows, rows],
        out_shape=[SDS((1, 128), F32), SDS((T, D), F32), SDS((T, D), MXU_DTYPE)],
        name=name, compiler_params=_cp("arbitrary"))(y, target)


def _pool_lane_consts(shape):
    lane = _lane(shape)
    grp = lane // (POOL_WIDTH // 4)
    win = jnp.where(grp == 0, 2, jnp.where(grp == 1, 4, jnp.where(grp == 2, 8, 16)))
    return grp, win


def _pool_select(grp, s2, s4, s8, s16):
    return jnp.where(grp == 0, s2, jnp.where(grp == 1, s4, jnp.where(grp == 2, s8, s16)))


def _pool_diff(xe, row0, tr):
    s2 = xe + pltpu.roll(xe, 1, 0)
    s4 = s2 + pltpu.roll(s2, 2, 0)
    s8 = s4 + pltpu.roll(s4, 4, 0)
    s16 = s8 + pltpu.roll(s8, 8, 0)
    shape = (tr, POOL_WIDTH)
    grp, win = _pool_lane_consts(shape)
    sums = _pool_select(grp, s2[16:], s4[16:], s8[16:], s16[16:])
    t = row0 + _row(shape)
    cnt = jnp.minimum(t + 1, win).astype(F32)
    return sums / cnt - xe[16:]


def _pool_fwd(z, wbd, scale, *, tr, name):
    T = z.shape[0]
    hb = tr // 16

    def body(x_ref, xp_ref, w_ref, s_ref, o_ref):
        i = pl.program_id(0)
        halo = jnp.where(i == 0, 0.0, xp_ref[...])
        diff = _pool_diff(jnp.concatenate([halo, x_ref[...]], axis=0), i * tr, tr)
        o_ref[...] = (_dot(diff, w_ref[...]) * s_ref[...]).astype(o_ref.dtype)

    return pl.pallas_call(
        body, grid=(T // tr,),
        in_specs=[pl.BlockSpec((tr, POOL_WIDTH), lambda i: (i, 0)),
                  pl.BlockSpec((16, POOL_WIDTH), lambda i: (jnp.maximum(i * hb - 1, 0), 0)),
                  _full((POOL_WIDTH, POOL_WIDTH)), _full((1, POOL_WIDTH))],
        out_specs=pl.BlockSpec((tr, POOL_WIDTH), lambda i: (i, 0)),
        out_shape=SDS((T, POOL_WIDTH), MXU_DTYPE), name=name, compiler_params=_cp("parallel"))(z, z, wbd, scale)


def _pool_bwd_tile(i, n, tr, x, xprev, dpa, dpa_next, wbd, scale):
    halo = jnp.where(i == 0, 0.0, xprev)
    diff = _pool_diff(jnp.concatenate([halo, x], axis=0), i * tr, tr)
    mixed = _dot(diff, wbd)
    dscale = jnp.sum(dpa * mixed, axis=0, keepdims=True)
    dnext = jnp.where(i == n - 1, 0.0, dpa_next)
    dmix_e = jnp.concatenate([dpa, dnext], axis=0) * scale
    ddiff_e = _dot(dmix_e, wbd, NT)
    dwbd = _dot(diff, dmix_e[:tr], TN)
    shape = (tr + 16, POOL_WIDTH)
    grp, win = _pool_lane_consts(shape)
    t = i * tr + _row(shape)
    e = ddiff_e / jnp.minimum(t + 1, win).astype(F32)
    nrow = tr + 16
    a2 = e + pltpu.roll(e, nrow - 1, 0)
    a4 = a2 + pltpu.roll(a2, nrow - 2, 0)
    a8 = a4 + pltpu.roll(a4, nrow - 4, 0)
    a16 = a8 + pltpu.roll(a8, nrow - 8, 0)
    dx = _pool_select(grp, a2, a4, a8, a16)[:tr] - ddiff_e[:tr]
    return dx, dwbd, dscale


def _norm_rope(x, g, cos, sin_signed, seg):
    reps = x.shape[1] // 128
    ms = _split_dot(x * x, seg) * (1.0 / HEAD_DIM)
    r = lax.rsqrt(ms + EPS)
    xn = x * r * g
    c, s = _tile_lanes(cos, reps), _tile_lanes(sin_signed, reps)
    return xn * c + _swap_halves(xn) * s


def _norm_rope_bwd(x, g, cos, sin_signed, seg, dout):
    reps = x.shape[1] // 128
    c, s = _tile_lanes(cos, reps), _tile_lanes(sin_signed, reps)
    dxn = dout * c + _swap_halves(dout * s)
    ms = _split_dot(x * x, seg) * (1.0 / HEAD_DIM)
    r = lax.rsqrt(ms + EPS)
    xh = x * r
    gy = dxn * g
    dx = r * (gy - xh * (_split_dot(xh * gy, seg) * (1.0 / HEAD_DIM)))
    dg = jnp.sum(dxn * xh, axis=0, keepdims=True)
    return dx, dg


def _dup_heads(k):
    first = _lane(k.shape) < HEAD_DIM
    kr = pltpu.roll(k, HEAD_DIM, 1)
    return jnp.concatenate([jnp.where(first, k, kr), jnp.where(first, kr, k)], axis=1)


def _qkv_prep(z, cos, sin_signed, gq, gk, seg, *, tr, name):
    T = z.shape[0]

    def body(qa_ref, qb_ref, kv_ref, c_ref, s_ref, gq_ref, gk_ref, seg_ref, q_ref, k_ref, v_ref):
        c, s, seg_m = c_ref[...], s_ref[...], seg_ref[...]
        scale = HEAD_DIM ** -0.5
        qa = _norm_rope(qa_ref[...], gq_ref[...], c, s, seg_m) * scale
        qb = _norm_rope(qb_ref[...], gq_ref[...], c, s, seg_m) * scale
        q_ref[...] = jnp.concatenate([qa, qb], axis=1).astype(q_ref.dtype)
        kv = kv_ref[...]
        k = _norm_rope(kv[:, :KV_WIDTH], gk_ref[...], c, s, seg_m[:128, :128])
        k_ref[...] = _dup_heads(k).astype(k_ref.dtype)
        v_ref[...] = _dup_heads(kv[:, KV_WIDTH:]).astype(v_ref.dtype)

    col = lambda j: pl.BlockSpec((tr, 256), lambda i: (i, j))
    tab = pl.BlockSpec((tr, 128), lambda i: (i, 0))
    return pl.pallas_call(
        body, grid=(T // tr,),
        in_specs=[col(1), col(2), col(3), tab, tab, _full((1, 256)), _full((1, 128)), _full((256, 256))],
        out_specs=[pl.BlockSpec((tr, 512), lambda i: (i, 0)), col(0), col(0)],
        out_shape=[SDS((T, 512), MXU_DTYPE), SDS((T, 256), MXU_DTYPE), SDS((T, 256), MXU_DTYPE)],
        name=name, compiler_params=_cp("parallel"))(z, z, z, cos, sin_signed, gq, gk, seg)


GROUP_HEADS = 4
GROUP_ROWS = GROUP_HEADS * ATTN_BLOCK


def _attn_mask(n):
    qi = _row((GROUP_ROWS, 2 * ATTN_BLOCK)) % ATTN_BLOCK
    kj = _lane((GROUP_ROWS, 2 * ATTN_BLOCK))
    return (kj > qi) & (kj <= qi + ATTN_BLOCK) & ((kj >= ATTN_BLOCK) | (n > 0))


def _stack_heads(x, g):
    first = _lane((ATTN_BLOCK, 128)) < HEAD_DIM
    parts = []
    for pair in (2 * g, 2 * g + 1):
        x128 = x[:, 128 * pair:128 * (pair + 1)]
        zero = jnp.zeros_like(x128)
        parts += [jnp.where(first, x128, zero), jnp.where(first, zero, x128)]
    return jnp.concatenate(parts, axis=0)


def _unstack_heads(y):
    first = _lane((ATTN_BLOCK, 128)) < HEAD_DIM
    b = ATTN_BLOCK
    return jnp.concatenate([jnp.where(first, y[0:b], y[b:2 * b]), jnp.where(first, y[2 * b:3 * b], y[3 * b:4 * b])],
                           axis=1)


def _sink_col(sk_ref, g):
    return jnp.concatenate([jnp.broadcast_to(sk_ref[h:h + 1, 0:1], (ATTN_BLOCK, 1))
                            for h in range(GROUP_HEADS * g, GROUP_HEADS * (g + 1))], axis=0)


def _group_exp(q4, kg, mask, sink):
    s = _dot(q4, kg, NT)
    s = jnp.where(mask, s, NEG_BIG)
    m = jnp.maximum(jnp.max(s, axis=1, keepdims=True), sink)
    p = jnp.exp(s - m)
    ps = jnp.exp(sink - m)
    return p, ps, 1.0 / (jnp.sum(p, axis=1, keepdims=True) + ps)


def _group_probs(q4, kg, mask, sink):
    p, ps, inv = _group_exp(q4, kg, mask, sink)
    return p * inv, ps * inv


def _attn_fwd(q, k, v, sinks_b, *, name):
    T = q.shape[0]
    nb = T // ATTN_BLOCK

    def body(q_ref, kc_ref, kp_ref, vc_ref, vp_ref, sk_ref, o_ref):
        n = pl.program_id(0)
        mask = _attn_mask(n)
        k2 = jnp.concatenate([kp_ref[...], kc_ref[...]], axis=0)
        v2 = jnp.concatenate([vp_ref[...], vc_ref[...]], axis=0)
        qv = q_ref[...]
        outs = []
        for g in range(2):
            kg = k2[:, 128 * g:128 * (g + 1)]
            vg = v2[:, 128 * g:128 * (g + 1)]
            p, _, inv = _group_exp(_stack_heads(qv, g), kg, mask, _sink_col(sk_ref, g))
            outs.append(_unstack_heads(_dot(p, vg) * inv))
        o_ref[...] = jnp.concatenate(outs, axis=1).astype(o_ref.dtype)

    cur = lambda w: pl.BlockSpec((ATTN_BLOCK, w), lambda n: (n, 0))
    prev = lambda w: pl.BlockSpec((ATTN_BLOCK, w), lambda n: (jnp.maximum(n - 1, 0), 0))
    return pl.pallas_call(
        body, grid=(nb,),
        in_specs=[cur(512), cur(256), prev(256), cur(256), prev(256), _full((8, 128))],
        out_specs=cur(512), out_shape=SDS((T, 512), MXU_DTYPE), name=name,
        compiler_params=_cp("parallel"))(q, k, k, v, v, sinks_b)


def _attn_bwd(q, k, v, sinks_b, do, *, name):
    T = q.shape[0]
    nb = T // ATTN_BLOCK

    def body(q_ref, kc_ref, kp_ref, vc_ref, vp_ref, sk_ref, do_ref,
             dq_ref, dkc_ref, dkp_ref, dvc_ref, dvp_ref, dsk_ref):
        n = pl.program_id(0)
        mask = _attn_mask(n)
        k2 = jnp.concatenate([kp_ref[...], kc_ref[...]], axis=0)
        v2 = jnp.concatenate([vp_ref[...], vc_ref[...]], axis=0)
        qv = q_ref[...]
        dov = do_ref[...]

        @pl.when(n == 0)
        def _():
            dsk_ref[...] = jnp.zeros_like(dsk_ref)

        dqs, dks, dvs = [], [], []
        for g in range(2):
            kg = k2[:, 128 * g:128 * (g + 1)]
            vg = v2[:, 128 * g:128 * (g + 1)]
            q4 = _stack_heads(qv, g)
            do4 = _stack_heads(dov, g)
            pn, psn = _group_probs(q4, kg, mask, _sink_col(sk_ref, g))
            o2 = _dot(pn, vg)
            delta = jnp.sum(do4 * o2, axis=1, keepdims=True)
            ds = pn * (_dot(do4, vg, NT) - delta)
            dqs.append(_unstack_heads(_dot(ds, kg)))
            dks.append(_dot(ds, q4, TN))
            dvs.append(_dot(pn, do4, TN))
            wsink = psn * delta
            for j in range(GROUP_HEADS):
                h = GROUP_HEADS * g + j
                dsink = -jnp.sum(wsink[ATTN_BLOCK * j:ATTN_BLOCK * (j + 1)], axis=0, keepdims=True)
                dsk_ref[h:h + 1, :] += jnp.broadcast_to(dsink, (1, 128))
        dq_ref[...] = jnp.concatenate(dqs, axis=1)
        dk = jnp.concatenate(dks, axis=1)
        dv = jnp.concatenate(dvs, axis=1)
        dkp_ref[...] = dk[:ATTN_BLOCK]
        dkc_ref[...] = dk[ATTN_BLOCK:]
        dvp_ref[...] = dv[:ATTN_BLOCK]
        dvc_ref[...] = dv[ATTN_BLOCK:]

    cur = lambda w: pl.BlockSpec((ATTN_BLOCK, w), lambda n: (n, 0))
    prev = lambda w: pl.BlockSpec((ATTN_BLOCK, w), lambda n: (jnp.maximum(n - 1, 0), 0))
    f = lambda w: SDS((T, w), F32)
    return pl.pallas_call(
        body, grid=(nb,),
        in_specs=[cur(512), cur(256), prev(256), cur(256), prev(256), _full((8, 128)), cur(512)],
        out_specs=[cur(512), cur(256), cur(256), cur(256), cur(256), _full((8, 128))],
        out_shape=[f(512), f(256), f(256), f(256), f(256), SDS((8, 128), F32)],
        name=name, compiler_params=_cp("arbitrary"))(q, k, k, v, v, sinks_b, do)


def _mixer_ab_bwd(z, cos, sin_signed, gq, gk, seg, dq, dkc, dkp, dvc, dvp, dpa, wbd, scale, dz, *, tr, name, deps=()):
    T = z.shape[0]
    n = T // tr
    hb = tr // 16
    ab = tr // ATTN_BLOCK

    def unfold(cur, nxt_tile, nxt_halo, i):
        nxt = jnp.concatenate([nxt_tile[ATTN_BLOCK:], jnp.where(i == n - 1, 0.0, nxt_halo)], axis=0)
        tot = cur + nxt
        first = _lane((tr, 128)) < HEAD_DIM
        a = tot[:, :128]
        b = tot[:, 128:]
        a = a + pltpu.roll(a, HEAD_DIM, 1)
        b = b + pltpu.roll(b, HEAD_DIM, 1)
        return jnp.where(first, a, b)

    def body(xp_ref, xpp_ref, qa_ref, qb_ref, kv_ref, c_ref, s_ref, gq_ref, gk_ref, seg_ref,
             dq_ref, dkc_ref, dkp_ref, dkh_ref, dvc_ref, dvp_ref, dvh_ref, dpa_ref, dpan_ref, w_ref, sc_ref, _dz_in,
             dz_ref, dgq_ref, dgk_ref, dw_ref, dsc_ref):
        i = pl.program_id(0)
        c, s, seg_m = c_ref[...], s_ref[...], seg_ref[...]
        scale_q = HEAD_DIM ** -0.5
        dqv = dq_ref[...] * scale_q
        dxa, dga = _norm_rope_bwd(qa_ref[...], gq_ref[...], c, s, seg_m, dqv[:, :256])
        dxb, dgb = _norm_rope_bwd(qb_ref[...], gq_ref[...], c, s, seg_m, dqv[:, 256:])
        dk = unfold(dkc_ref[...], dkp_ref[...], dkh_ref[...], i)
        dv = unfold(dvc_ref[...], dvp_ref[...], dvh_ref[...], i)
        kv = kv_ref[...]
        dxk, dgk = _norm_rope_bwd(kv[:, :KV_WIDTH], gk_ref[...], c, s, seg_m[:128, :128], dk)
        dxp, dwbd, dscale = _pool_bwd_tile(i, n, tr, xp_ref[...], xpp_ref[...], dpa_ref[...], dpan_ref[...],
                                           w_ref[...], sc_ref[...])
        dz_ref[...] = jnp.concatenate([dxp, dxa, dxb, dxk, dv], axis=1).astype(dz_ref.dtype)

        @pl.when(i == 0)
        def _():
            dgq_ref[...] = jnp.zeros_like(dgq_ref)
            dgk_ref[...] = jnp.zeros_like(dgk_ref)
            dw_ref[...] = jnp.zeros_like(dw_ref)
            dsc_ref[...] = jnp.zeros_like(dsc_ref)
        dgq_ref[...] += _fold_lanes(dga + dgb, HEAD_DIM)
        dgk_ref[...] += _fold_lanes(dgk, HEAD_DIM)
        dw_ref[...] += dwbd
        dsc_ref[...] += dscale

    col = lambda j: pl.BlockSpec((tr, 256), lambda i: (i, j))
    rows = lambda w: pl.BlockSpec((tr, w), lambda i: (i, 0))
    nxt_blk = pl.BlockSpec((ATTN_BLOCK, 256), lambda i: (jnp.minimum((i + 1) * ab, T // ATTN_BLOCK - 1), 0))
    prev16 = pl.BlockSpec((16, 256), lambda i: (jnp.maximum(i * hb - 1, 0), 0))
    next16 = pl.BlockSpec((16, 256), lambda i: (jnp.minimum((i + 1) * hb, T // 16 - 1), 0))
    return pl.pallas_call(
        _after(body, 22, deps), grid=(n,),
        in_specs=[col(0), prev16, col(1), col(2), col(3), rows(128), rows(128),
                  _full((1, 256)), _full((1, 128)), _full((256, 256)),
                  rows(512), rows(256), rows(256), nxt_blk, rows(256), rows(256), nxt_blk,
                  rows(256), next16, _full((256, 256)), _full((1, 256)), ANY] + [ANY] * len(deps),
        out_specs=[rows(1024), _full((1, 256)), _full((1, 128)), _full((256, 256)), _full((1, 256))],
        out_shape=[SDS((T, IN_COLS), MXU_DTYPE), SDS((1, 256), F32), SDS((1, 128), F32),
                   SDS((256, 256), F32), SDS((1, 256), F32)],
        input_output_aliases={21: 0}, name=name, compiler_params=_cp("arbitrary"))(
            z, z, z, z, z, cos, sin_signed, gq, gk, seg, dq, dkc, dkp, dkp, dvc, dvp, dvp, dpa, dpa, wbd, scale, dz,
            *deps)


def _sgu_common(zu, zv, vn, seg):
    u, du = _gelu_and_grad(zu)
    gv, dgv = _gelu_and_grad(zv)
    ms = _split_dot(gv * gv, seg) * (1.0 / HEAD_DIM)
    r = lax.rsqrt(ms + EPS)
    xh = gv * r
    return u, du, dgv, r, xh, xh * vn


def _sgu_fwd(z, wtril, bexp, vn, seg, *, tr, name):
    T = z.shape[0]
    nch = tr // CHUNK

    def body(u_ref, v_ref, w_ref, b_ref, vn_ref, seg_ref, o_ref):
        u, _, _, _, _, vg = _sgu_common(u_ref[...], v_ref[...], vn_ref[...], seg_ref[...])
        grp = _lane((CHUNK, SGU_WIDTH)) // HEAD_DIM
        outs = []
        for ch in range(nch):
            vc = vg[ch * CHUNK:(ch + 1) * CHUNK]
            s = b_ref[...]
            for g in range(4):
                s = s + jnp.where(grp == g, _dot(w_ref[g], vc), 0.0)
            outs.append(u[ch * CHUNK:(ch + 1) * CHUNK] * s)
        o_ref[...] = jnp.concatenate(outs, axis=0).astype(o_ref.dtype)

    col = lambda j: pl.BlockSpec((tr, 256), lambda i: (i, j))
    return pl.pallas_call(
        body, grid=(T // tr,),
        in_specs=[col(4), col(5), _full((4, CHUNK, CHUNK)), _full((CHUNK, 256)), _full((1, 256)), _full((256, 256))],
        out_specs=col(0), out_shape=SDS((T, SGU_WIDTH), MXU_DTYPE), name=name,
        compiler_params=_cp("parallel"))(z, z, wtril, bexp, vn, seg)


def _sgu_bwd(z, wtril, bexp, vn, seg, dsg, dz, *, tr, name):
    T = z.shape[0]
    nch = tr // CHUNK

    def body(u_ref, v_ref, w_ref, b_ref, vn_ref, seg_ref, d_ref, _dz_in, dz_ref, dw_ref, db_ref, dvn_ref):
        i = pl.program_id(0)
        seg_m = seg_ref[...]
        vn_v = vn_ref[...]
        u, du, dgv, r, xh, vg = _sgu_common(u_ref[...], v_ref[...], vn_v, seg_m)
        d = d_ref[...]
        grp = _lane((CHUNK, SGU_WIDTH)) // HEAD_DIM
        tril = _row((CHUNK, CHUNK)) >= _lane((CHUNK, CHUNK))

        @pl.when(i == 0)
        def _():
            dw_ref[...] = jnp.zeros_like(dw_ref)
            db_ref[...] = jnp.zeros_like(db_ref)
            dvn_ref[...] = jnp.zeros_like(dvn_ref)

        dus, dvgs = [], []
        for ch in range(nch):
            sl = slice(ch * CHUNK, (ch + 1) * CHUNK)
            vc = vg[sl]
            s = b_ref[...]
            for g in range(4):
                s = s + jnp.where(grp == g, _dot(w_ref[g], vc), 0.0)
            dus.append(d[sl] * s)
            ds = d[sl] * u[sl]
            db_ref[...] += _split_dot(ds, seg_m)
            dvg = jnp.zeros((CHUNK, SGU_WIDTH), F32)
            for g in range(4):
                dsm = jnp.where(grp == g, ds, 0.0)
                dvg = dvg + jnp.where(grp == g, _dot(w_ref[g], ds, TN), 0.0)
                dw_ref[g] += jnp.where(tril, _dot(dsm, vc, NT), 0.0)
            dvgs.append(dvg)
        dup = jnp.concatenate(dus, axis=0)
        dvg = jnp.concatenate(dvgs, axis=0)
        dvn_ref[...] += _fold_lanes(jnp.sum(dvg * xh, axis=0, keepdims=True), HEAD_DIM)
        gy = dvg * vn_v
        dgvv = r * (gy - xh * (_split_dot(xh * gy, seg_m) * (1.0 / HEAD_DIM)))
        dz_ref[...] = jnp.concatenate([dup * du, dgvv * dgv], axis=1).astype(dz_ref.dtype)

    col = lambda j: pl.BlockSpec((tr, 256), lambda i: (i, j))
    return pl.pallas_call(
        body, grid=(T // tr,),
        in_specs=[col(4), col(5), _full((4, CHUNK, CHUNK)), _full((CHUNK, 256)), _full((1, 256)), _full((256, 256)),
                  col(0), ANY],
        out_specs=[pl.BlockSpec((tr, 512), lambda i: (i, 2)), _full((4, CHUNK, CHUNK)), _full((CHUNK, 256)),
                   _full((1, 256))],
        out_shape=[SDS((T, IN_COLS), MXU_DTYPE), SDS((4, CHUNK, CHUNK), F32), SDS((CHUNK, 256), F32),
                   SDS((1, 256), F32)],
        input_output_aliases={7: 0}, name=name, compiler_params=_cp("arbitrary"))(
            z, z, wtril, bexp, vn, seg, dsg, dz)


def _merge_fwd(pa, at, sg, wa, wb, wc, z, *, tm, tn, name):
    T = pa.shape[0]
    gb = GATE_COL0 // tn
    nb = D_MODEL // tn

    def body(pa_ref, at_ref, sg_ref, wa_ref, wb_ref, wc_ref, g0_ref, g1_ref, g2_ref, m_ref, y_ref):
        acc = None
        for idx, (op_ref, w_ref, g_ref) in enumerate(((pa_ref, wa_ref, g0_ref), (at_ref, wb_ref, g1_ref),
                                                      (sg_ref, wc_ref, g2_ref))):
            y = _dot(op_ref[...], w_ref[...])
            y_ref[idx] = y.astype(y_ref.dtype)
            t = _sigmoid(g_ref[...]) * y
            acc = t if acc is None else acc + t
        m_ref[...] = acc.astype(m_ref.dtype)

    op = lambda w: pl.BlockSpec((tm, w), lambda i, j: (i, 0))
    wt = lambda k: pl.BlockSpec((k, tn), lambda i, j: (0, j))
    gate = lambda b: pl.BlockSpec((tm, tn), lambda i, j: (i, gb + b * nb + j))
    return pl.pallas_call(
        body, grid=(T // tm, nb),
        in_specs=[op(256), op(512), op(256), wt(256), wt(512), wt(256), gate(0), gate(1), gate(2)],
        out_specs=[pl.BlockSpec((tm, tn), lambda i, j: (i, j)), pl.BlockSpec((3, tm, tn), lambda i, j: (0, i, j))],
        out_shape=[SDS((T, D_MODEL), MXU_DTYPE), SDS((3, T, D_MODEL), MXU_DTYPE)],
        name=name, compiler_params=_cp("parallel", "parallel"))(pa, at, sg, wa, wb, wc, z, z, z)


def _merge_bwd(dm, y, z, *, tr, tn, name):
    T = dm.shape[0]
    gb = GATE_COL0 // tn
    nb = D_MODEL // tn

    def body(dm_ref, y_ref, g_ref, dy_ref, dz_ref):
        g = _sigmoid(g_ref[...])
        d = dm_ref[...]
        dy_ref[...] = (d * g).astype(dy_ref.dtype)
        dz_ref[...] = (d * y_ref[...].astype(F32) * g * (1.0 - g)).astype(dz_ref.dtype)

    return pl.pallas_call(
        body, grid=(T // tr, nb, 3),
        in_specs=[pl.BlockSpec((tr, tn), lambda i, j, b: (i, j)),
                  pl.BlockSpec((None, tr, tn), lambda i, j, b: (b, i, j)),
                  pl.BlockSpec((tr, tn), lambda i, j, b: (i, gb + b * nb + j))],
        out_specs=[pl.BlockSpec((None, tr, tn), lambda i, j, b: (b, i, j)),
                   pl.BlockSpec((tr, tn), lambda i, j, b: (i, gb + b * nb + j))],
        out_shape=[SDS((3, T, D_MODEL), MXU_DTYPE), SDS((T, IN_COLS), MXU_DTYPE)],
        name=name, compiler_params=_cp("parallel", "parallel", "parallel"))(dm, y, z)


def _conv3(xe, w, b):
    return (w[0:1] * pltpu.roll(xe, 2, 0) + w[1:2] * pltpu.roll(xe, 1, 0) + w[2:3] * xe)[8:] + b


def _conv_act_fwd(up, cw, cb, *, tr, tc, name):
    T = up.shape[0]
    nc = D_FF // tc
    hb = tr // 8

    def body(ug_ref, ugp_ref, uv_ref, uvp_ref, wg_ref, wv_ref, bg_ref, bv_ref, o_ref):
        i = pl.program_id(1)
        first = i == 0
        cg = _conv3(jnp.concatenate([jnp.where(first, 0.0, ugp_ref[...]), ug_ref[...]], axis=0), wg_ref[...], bg_ref[...])
        cv = _conv3(jnp.concatenate([jnp.where(first, 0.0, uvp_ref[...]), uv_ref[...]], axis=0), wv_ref[...], bv_ref[...])
        o_ref[...] = (cg * _sigmoid(cg) * cv).astype(o_ref.dtype)

    tile = lambda off: pl.BlockSpec((tr, tc), lambda j, i: (i, off + j))
    prev = lambda off: pl.BlockSpec((8, tc), lambda j, i: (jnp.maximum(i * hb - 1, 0), off + j))
    par = lambda rows, off: pl.BlockSpec((rows, tc), lambda j, i: (0, off + j))
    return pl.pallas_call(
        body, grid=(nc, T // tr),
        in_specs=[tile(0), prev(0), tile(nc), prev(nc), par(3, 0), par(3, nc), par(1, 0), par(1, nc)],
        out_specs=pl.BlockSpec((tr, tc), lambda j, i: (i, j)),
        out_shape=SDS((T, D_FF), MXU_DTYPE), name=name,
        compiler_params=_cp("parallel", "parallel"))(up, up, up, up, cw, cw, cb, cb)


def _conv_act_bwd(up, cw, cb, dact, *, tr, tc, name, deps=()):
    T = up.shape[0]
    nc = D_FF // tc
    hb = tr // 8
    nr = T // tr

    def body(ug_ref, ugp_ref, ugn_ref, uv_ref, uvp_ref, uvn_ref, da_ref, dan_ref, wg_ref, wv_ref, bg_ref, bv_ref,
             du_ref, dwg_ref, dwv_ref, dbg_ref, dbv_ref):
        i = pl.program_id(1)
        first, last = i == 0, i == nr - 1
        da = jnp.concatenate([da_ref[...], jnp.where(last, 0.0, dan_ref[...])], axis=0)
        uge = jnp.concatenate([jnp.where(first, 0.0, ugp_ref[...]), ug_ref[...], ugn_ref[...]], axis=0)
        uve = jnp.concatenate([jnp.where(first, 0.0, uvp_ref[...]), uv_ref[...], uvn_ref[...]], axis=0)
        wg, wv = wg_ref[...], wv_ref[...]
        ug1, ug2 = pltpu.roll(uge, 1, 0)[8:], pltpu.roll(uge, 2, 0)[8:]
        uv1, uv2 = pltpu.roll(uve, 1, 0)[8:], pltpu.roll(uve, 2, 0)[8:]
        cg = wg[0:1] * ug2 + wg[1:2] * ug1 + wg[2:3] * uge[8:] + bg_ref[...]
        cv = wv[0:1] * uv2 + wv[1:2] * uv1 + wv[2:3] * uve[8:] + bv_ref[...]
        sg = _sigmoid(cg)
        dcg = da * cv * (sg * (1.0 + cg * (1.0 - sg)))
        dcv = da * (cg * sg)
        nrow = tr + 8

        def back(dc, w):
            return (w[2:3] * dc + w[1:2] * pltpu.roll(dc, nrow - 1, 0) + w[0:1] * pltpu.roll(dc, nrow - 2, 0))[:tr]

        du_ref[0] = back(dcg, wg).astype(du_ref.dtype)
        du_ref[1] = back(dcv, wv).astype(du_ref.dtype)

        def wgrad(dc, u0, u1, u2):
            d = dc[:tr]
            rows = [jnp.sum(d * u2[:tr], axis=0, keepdims=True), jnp.sum(d * u1[:tr], axis=0, keepdims=True),
                    jnp.sum(d * u0[8:8 + tr], axis=0, keepdims=True)]
            return jnp.concatenate(rows, axis=0), jnp.sum(d, axis=0, keepdims=True)

        dwg, dbg = wgrad(dcg, uge, ug1, ug2)
        dwv, dbv = wgrad(dcv, uve, uv1, uv2)

        @pl.when(first)
        def _():
            dwg_ref[...] = jnp.zeros_like(dwg_ref)
            dwv_ref[...] = jnp.zeros_like(dwv_ref)
            dbg_ref[...] = jnp.zeros_like(dbg_ref)
            dbv_ref[...] = jnp.zeros_like(dbv_ref)
        dwg_ref[...] += dwg
        dwv_ref[...] += dwv
        dbg_ref[...] += dbg
        dbv_ref[...] += dbv

    tile = lambda off: pl.BlockSpec((tr, tc), lambda j, i: (i, off + j))
    prev = lambda off: pl.BlockSpec((8, tc), lambda j, i: (jnp.maximum(i * hb - 1, 0), off + j))
    nxt = lambda off: pl.BlockSpec((8, tc), lambda j, i: (jnp.minimum((i + 1) * hb, T // 8 - 1), off + j))
    par = lambda rows, off: pl.BlockSpec((rows, tc), lambda j, i: (0, off + j))
    acc = lambda rows: pl.BlockSpec((rows, tc), lambda j, i: (0, j))
    return pl.pallas_call(
        _after(body, 12, deps), grid=(nc, nr),
        in_specs=[tile(0), prev(0), nxt(0), tile(nc), prev(nc), nxt(nc), tile(0), nxt(0),
                  par(3, 0), par(3, nc), par(1, 0), par(1, nc)] + [ANY] * len(deps),
        out_specs=[pl.BlockSpec((2, tr, tc), lambda j, i: (0, i, j)), acc(3), acc(3), acc(1), acc(1)],
        out_shape=[SDS((2, T, D_FF), MXU_DTYPE), SDS((3, D_FF), F32), SDS((3, D_FF), F32),
                   SDS((1, D_FF), F32), SDS((1, D_FF), F32)],
        name=name, compiler_params=_cp("parallel", "arbitrary"))(
            up, up, up, up, up, up, dact, dact, cw, cw, cb, cb, *deps)


def _row_tile(rows, cap):
    t = min(cap, rows)
    t -= t % 8
    while rows % t:
        t -= 8
    return t


def _adamw(w, g, m, v, *, tr, name):
    R, C = w.shape
    assert R % tr == 0, (R, tr)

    def body(w_ref, g_ref, m_ref, v_ref, d_ref, nm_ref, nv_ref):
        gv = g_ref[...]
        mn = ADAM_B1 * m_ref[...] + (1.0 - ADAM_B1) * gv
        vn = ADAM_B2 * v_ref[...] + (1.0 - ADAM_B2) * (gv * gv)
        m_hat = mn / (1.0 - ADAM_B1 ** ADAM_STEP)
        v_hat = vn / (1.0 - ADAM_B2 ** ADAM_STEP)
        d_ref[...] = -ADAM_LR * (m_hat / (jnp.sqrt(v_hat) + ADAM_EPS) + ADAM_WD * w_ref[...])
        nm_ref[...] = mn
        nv_ref[...] = vn

    rows = pl.BlockSpec((tr, C), lambda i: (i, 0))
    return pl.pallas_call(
        body, grid=(R // tr,), in_specs=[rows] * 4, out_specs=[rows] * 3,
        out_shape=[SDS((R, C), F32)] * 3, name=name, compiler_params=_cp("parallel"))(w, g, m, v)


def _sum_slots(r, *, tr, name):
    S, R, C = r.shape
    assert R % tr == 0, (R, tr)

    def body(r_ref, o_ref):
        acc = r_ref[0]
        for s in range(1, S):
            acc = acc + r_ref[s]
        o_ref[...] = acc

    return pl.pallas_call(
        body, grid=(R // tr,), in_specs=[pl.BlockSpec((S, tr, C), lambda i: (0, i, 0))],
        out_specs=pl.BlockSpec((tr, C), lambda i: (i, 0)), out_shape=SDS((R, C), F32),
        name=name, compiler_params=_cp("parallel"))(r)


def _pair_add(g4, h, pos, *, name):
    A, _, r, C = g4.shape
    cs = C if A == N_CHIPS else C // N_CHIPS
    tr = _row_tile(r, 256)
    if A == N_CHIPS:
        g_map, h_map = (lambda t, i, pos: (t, pos[1], i, 0)), (lambda t, i, pos: (t, i, 0))
    else:
        g_map, h_map = (lambda t, i, pos: (0, pos[1], i, t)), (lambda t, i, pos: (0, i, t))

    def body(pos_ref, g_ref, h_ref, o_ref):
        o_ref[...] = (g_ref[...] + h_ref[...]).astype(o_ref.dtype)

    grid_spec = pltpu.PrefetchScalarGridSpec(
        num_scalar_prefetch=1, grid=(N_CHIPS, r // tr),
        in_specs=[pl.BlockSpec((None, None, tr, cs), g_map), pl.BlockSpec((None, tr, cs), h_map)],
        out_specs=pl.BlockSpec((None, tr, cs), lambda t, i, pos: (t, i, 0)))
    return pl.pallas_call(body, grid_spec=grid_spec, out_shape=SDS((N_CHIPS, r, cs), COMM_DTYPE), name=name,
                          compiler_params=_cp("parallel", "parallel"))(pos, g4, h)


def _chip_sum(p, r2, f_into, pos, layer, *, name):
    _, r, cs = p.shape
    tr = _row_tile(r, 256)

    def body(pos_ref, own_ref, r_ref, *rest):
        o_ref = rest[-1]
        o_ref[...] = ((own_ref[...].astype(F32) + r_ref[0].astype(F32)) + r_ref[1].astype(F32)) + r_ref[2].astype(F32)

    in_specs = [pl.BlockSpec((None, tr, cs), lambda i, pos: (pos[0], i, 0)),
                pl.BlockSpec((3, tr, cs), lambda i, pos: (0, i, 0))]
    operands = [pos, p, r2]
    aliases = {}
    if f_into is not None:
        in_specs.append(ANY)
        operands.append(f_into)
        aliases = {3: 0}
    grid_spec = pltpu.PrefetchScalarGridSpec(
        num_scalar_prefetch=1, grid=(r // tr,), in_specs=in_specs,
        out_specs=pl.BlockSpec((None, None, tr, cs), lambda i, pos: (layer, pos[1], i, 0)))
    return pl.pallas_call(body, grid_spec=grid_spec, out_shape=SDS((DEPTH, 2, r, cs), F32), name=name,
                          input_output_aliases=aliases, compiler_params=_cp("parallel"))(*operands)


def _mesh_pos():
    return lax.axis_index("x"), lax.axis_index("y"), lax.axis_index("c")


HBM = pl.BlockSpec(memory_space=pltpu.HBM)
SEM = pl.BlockSpec(memory_space=pltpu.SEMAPHORE)
DATAFLOW = pltpu.SideEffectType.DATAFLOW_SIDE_EFFECTING
CHIP_FLIPS = (2, 1, 3)


def _chip_peers():
    x, y, c = _mesh_pos()
    return 2 * x + y, [(1 - x, y, c), (x, 1 - y, c), (1 - x, 1 - y, c)], (x, y, 1 - c), c


def _split_start(arrays, n_copies, issue, *, name, deps=()):
    k = len(arrays)
    nd = len(deps)

    def body(*refs):
        issue(refs[:k], refs[k + nd], refs[k + nd + 1])
        refs[2 * k + nd + 2][...] = jnp.zeros((8, 128), F32)

    out = pl.pallas_call(
        body, name=name,
        out_shape=(pltpu.SemaphoreType.DMA((n_copies,)), pltpu.SemaphoreType.DMA((n_copies,)),
                   *[pltpu.HBM(a.shape, a.dtype) for a in arrays], SDS((8, 128), F32)),
        in_specs=[HBM] * k + [ANY] * nd, out_specs=(SEM, SEM, *[HBM] * k, pl.BlockSpec(memory_space=pltpu.VMEM)),
        input_output_aliases={i: 2 + i for i in range(k)},
        compiler_params=pltpu.CompilerParams(has_side_effects=DATAFLOW))(
            *[pltpu.with_memory_space_constraint(a, pltpu.HBM) for a in arrays], *deps)
    return (out[0], out[1]), list(out[2:2 + k]), out[2 + k]


def _split_wait(sems, arrays, after, waits, *, name):
    k = len(arrays)
    afters = tuple(after) if isinstance(after, (tuple, list)) else (after,)

    def body(*refs):
        waits(refs[:k], refs[k], refs[k + 1])

    out = pl.pallas_call(
        body, name=name, out_shape=tuple(pltpu.HBM(a.shape, a.dtype) for a in arrays),
        in_specs=[HBM] * k + [SEM, SEM] + [ANY] * len(afters), out_specs=tuple([HBM] * k),
        input_output_aliases={i: i for i in range(k)},
        compiler_params=pltpu.CompilerParams(has_side_effects=DATAFLOW))(*arrays, sems[0], sems[1], *afters)
    return list(out)


def _wait_both(cp):
    cp.wait_send()
    cp.wait_recv()


def _cast_place(shard, pos, dtype, *, name, layer=None, slots=N_CHIPS, which=0):
    R, C = shard.shape[-2:]
    tr = R if R % 8 else _row_tile(R, 256)
    if layer is None:
        in_spec = pl.BlockSpec((tr, C), lambda i, pos: (i, 0))
    else:
        in_spec = pl.BlockSpec((None, tr, C), lambda i, pos: (layer, i, 0))

    def body(pos_ref, x_ref, o_ref):
        o_ref[...] = x_ref[...].astype(o_ref.dtype)

    grid_spec = pltpu.PrefetchScalarGridSpec(
        num_scalar_prefetch=1, grid=(R // tr,), in_specs=[in_spec],
        out_specs=pl.BlockSpec((None, tr, C), lambda i, pos: (pos[which], i, 0)))
    return pl.pallas_call(body, grid_spec=grid_spec, out_shape=SDS((slots, R, C), dtype), name=name,
                          compiler_params=_cp("parallel"))(pos, shard)


def _device_peers():
    x, y, c = _mesh_pos()
    peers = [(x ^ ((f >> 2) & 1), y ^ ((f >> 1) & 1), c ^ (f & 1)) for f in range(1, N_DEV)]
    return 4 * x + 2 * y + c, peers


class _Gather:
    def __init__(self, lands, name, deps=(), all_devices=False):
        n = len(lands)
        self.name = name
        npeer = N_DEV - 1 if all_devices else N_CHIPS - 1

        def copies(refs, ss, rs):
            me, peers = _device_peers() if all_devices else _chip_peers()[:2]
            return [pltpu.make_async_remote_copy(
                src_ref=refs[w].at[me], dst_ref=refs[w].at[me], send_sem=ss.at[npeer * w + p],
                recv_sem=rs.at[npeer * w + p], device_id=peers[p], device_id_type=MESH)
                for w in range(n) for p in range(npeer)]

        def issue(refs, ss, rs):
            for cp in copies(refs, ss, rs):
                cp.start()

        def waits(refs, ss, rs):
            for cp in copies(refs, ss, rs):
                _wait_both(cp)

        self._waits = waits
        self.sems, self.arrays, self.token = _split_start(list(lands), npeer * n, issue, name=name + "_start",
                                                          deps=deps)

    def wait(self, after):
        return _split_wait(self.sems, self.arrays, after, self._waits, name=self.name + "_wait")


def _swap_halves_start(g4s, *, name):
    n = len(g4s)
    lands = [lax.empty((g.shape[0],) + g.shape[2:], g.dtype) for g in g4s]

    def copies(refs, ss, rs):
        _, _, sibling, c = _chip_peers()
        return [pltpu.make_async_remote_copy(
            src_ref=refs[w].at[:, 1 - c], dst_ref=refs[n + w], send_sem=ss.at[w], recv_sem=rs.at[w],
            device_id=sibling, device_id_type=MESH) for w in range(n)]

    def issue(refs, ss, rs):
        for cp in copies(refs, ss, rs):
            cp.start()

    def waits(refs, ss, rs):
        for cp in copies(refs, ss, rs):
            _wait_both(cp)

    sems, arrays, token = _split_start(list(g4s) + lands, n, issue, name=name + "_start")
    return sems, arrays, token, waits


def _scatter_start(parts, *, name, deps=()):
    n = len(parts)
    lands = [lax.empty((3,) + p.shape[1:], p.dtype) for p in parts]

    def copies(refs, ss, rs):
        me, peers, _, _ = _chip_peers()
        return [pltpu.make_async_remote_copy(
            src_ref=refs[w].at[me ^ CHIP_FLIPS[p]], dst_ref=refs[n + w].at[p],
            send_sem=ss.at[3 * w + p], recv_sem=rs.at[3 * w + p], device_id=peers[p], device_id_type=MESH)
            for w in range(n) for p in range(3)]

    def issue(refs, ss, rs):
        for cp in copies(refs, ss, rs):
            cp.start()

    def waits(refs, ss, rs):
        for cp in copies(refs, ss, rs):
            _wait_both(cp)

    sems, arrays, token = _split_start(list(parts) + lands, 3 * n, issue, name=name + "_start", deps=deps)
    return sems, arrays, token, waits


def _pair_share_start(fs, layer, *, name):
    n = len(fs)

    def copies(refs, ss, rs):
        _, _, sibling, c = _chip_peers()
        return [pltpu.make_async_remote_copy(
            src_ref=refs[w].at[layer, c], dst_ref=refs[w].at[layer, c], send_sem=ss.at[w], recv_sem=rs.at[w],
            device_id=sibling, device_id_type=MESH) for w in range(n)]

    def issue(refs, ss, rs):
        for cp in copies(refs, ss, rs):
            cp.start()

    def waits(refs, ss, rs):
        for cp in copies(refs, ss, rs):
            _wait_both(cp)

    sems, arrays, token = _split_start(list(fs), n, issue, name=name + "_start")
    return sems, arrays, token, waits


BIG = ('w_in', 'w_proj_a', 'w_proj_b', 'w_proj_c', 'w_out', 'w_up', 'w_down')
BIG_SHARD_AXIS = {'w_in': 2, 'w_proj_a': 2, 'w_proj_b': 2, 'w_proj_c': 2, 'w_out': 1, 'w_up': 2, 'w_down': 1}
SMALL = ('norm1', 'q_norm', 'k_norm', 'sinks', 'w_pool', 'pool_scale', 'sgu_v_norm', 'w_s', 'b_s', 'norm2',
         'conv_b', 'conv_w')
WEIGHTS = ('norm1', 'w_in', 'q_norm', 'k_norm', 'sinks', 'w_pool', 'pool_scale', 'sgu_v_norm', 'w_s', 'b_s',
           'w_proj_a', 'w_proj_b', 'w_proj_c', 'w_out', 'norm2', 'w_up', 'conv_w', 'conv_b', 'w_down')


def _rope_tables(positions):
    inv_freq = ROPE_THETA ** (-jnp.arange(0, HEAD_DIM, 2, dtype=F32) / HEAD_DIM)
    ang = positions.astype(F32)[:, None] * inv_freq
    cos, sin = jnp.cos(ang), jnp.sin(ang)
    c = jnp.concatenate([cos, cos], axis=1)
    s = jnp.concatenate([-sin, sin], axis=1)
    return jnp.concatenate([c, c], axis=1), jnp.concatenate([s, s], axis=1)


def _block_diag4(w):
    out = jnp.zeros((POOL_WIDTH, POOL_WIDTH), w.dtype)
    for g in range(4):
        out = lax.dynamic_update_slice(out, w[g], (g * HEAD_DIM, g * HEAD_DIM))
    return out


def _local_step(x, target, cos, sin, sp, sched):
    T = x.shape[0]
    tm1 = min(1024, T)
    tm = min(512, T)
    tr = min(256, T)
    tkt = min(1024, T)
    seg = _seg_matrix(256, HEAD_DIM)
    saved = []
    xl = x
    for l in range(DEPTH):
        p = f"l{l}_"
        c = dict(
            g1=sp['norm1'][l][None], g2=sp['norm2'][l][None],
            wbd=_block_diag4(sp['w_pool'][l]).astype(MXU_DTYPE), scale=sp['pool_scale'][l][None],
            gq=jnp.tile(sp['q_norm'][l], 4)[None], gk=jnp.tile(sp['k_norm'][l], 2)[None],
            sinks=jnp.broadcast_to(sp['sinks'][l][:, None], (N_Q_HEADS, 128)),
            wtril=jnp.tril(sp['w_s'][l]).astype(MXU_DTYPE),
            bexp=jnp.repeat(sp['b_s'][l].T, HEAD_DIM, axis=1), vn=jnp.tile(sp['sgu_v_norm'][l], 4)[None],
            cb=sp['conv_b'][l][None])
        c['w_in'] = sched.weight('w_in', l, xl)
        z, h1 = _norm_mm(xl, c['g1'], c['w_in'], tm=tm1, tn=1152, name=p + "in_proj",
                         deps=sched.start_tokens() if l == 0 else ())
        pa = _pool_fwd(z, c['wbd'], c['scale'], tr=tr, name=p + "pool")
        q, k, v = _qkv_prep(z, cos, sin, c['gq'], c['gk'], seg, tr=tr, name=p + "qkv_prep")
        at = _attn_fwd(q, k, v, c['sinks'], name=p + "attn")
        sg = _sgu_fwd(z, c['wtril'], c['bexp'], c['vn'], seg, tr=tr, name=p + "sgu")
        for n in ('w_proj_a', 'w_proj_b', 'w_proj_c', 'w_out'):
            c[n] = sched.weight(n, l, (pa, at, sg))
        merged, y3 = _merge_fwd(pa, at, sg, c['w_proj_a'], c['w_proj_b'], c['w_proj_c'], z,
                                tm=tm, tn=512, name=p + "merge")
        x1 = _mm(merged, c['w_out'], mode='nn', add=xl, tm=tm, tn=D_MODEL, tk=D_MODEL, name=p + "out_proj")
        for n in ('w_up', 'conv_w', 'w_down'):
            c[n] = sched.weight(n, l, x1)
        up, h2 = _norm_mm(x1, c['g2'], c['w_up'], tm=tm1, tn=1408, name=p + "up_proj")
        act = _conv_act_fwd(up, c['conv_w'], c['cb'], tr=tr, tc=1408, name=p + "conv_act")
        x2 = _mm(act, c['w_down'], mode='nn', add=x1, tm=tm, tn=D_MODEL, tk=1408, name=p + "down_proj")
        saved.append(dict(c, x=xl, h1=h1, z=z, pa=pa, q=q, k=k, v=v, at=at, sg=sg, merged=merged, y3=y3,
                          x1=x1, h2=h2, up=up, act=act))
        xl = x2

    loss_row, dx, dxb = _loss_head(xl, target, tr=tr, name="loss_head")

    gs = {n: [None] * DEPTH for n in SMALL}
    for l in reversed(range(DEPTH)):
        p = f"l{l}_b_"
        s = saved[l]
        gb = {}
        dact = _mm(dxb, s['w_down'], mode='nt', tm=tm1, tn=1408, tk=D_MODEL, name=p + "down_dx")
        gb['w_down'] = _mm(s['act'], dxb, mode='tn', tm=1408, tn=D_MODEL, tk=tkt, name=p + "down_dw")
        toks = sched.slot(l, 'down', gb['w_down'])
        dup, dwg, dwv, dbg, dbv = _conv_act_bwd(s['up'], s['conv_w'], s['cb'], dact, tr=min(512, T), tc=256,
                                                name=p + "conv_act", deps=toks)
        gs['conv_w'][l] = jnp.concatenate([dwg, dwv], axis=1)
        gs['conv_b'][l] = jnp.concatenate([dbg, dbv], axis=1)[0]
        toks = sched.slot(l, 'conv', dup)
        for half in range(2):
            gb['w_up'] = _mm(s['h2'], dup, mode='tn', b_lead=half, tm=D_MODEL, tn=1408, tk=tkt,
                             out_into=gb.get('w_up'), out_joff=2 * half, out_n=2 * D_FF, name=p + f"up_dw{half}",
                             deps=toks if half == 0 else ())
        toks = sched.slot(l, 'ffn', gb['w_up'], gb)
        dx1, dx1b, dg2 = _mm_nt_sharded_rms(dup, s['w_up'], s['x1'], s['g2'], dx, tm=min(256, T),
                                            name=p + "up_dx_rms2", deps=toks)
        gs['norm2'][l] = dg2[0]
        dmerged = _mm(dx1b, s['w_out'], mode='nt', tm=tm, tn=D_MODEL, tk=D_MODEL, name=p + "out_dx")
        gb['w_out'] = _mm(s['merged'], dx1b, mode='tn', tm=D_MODEL, tn=D_MODEL, tk=tkt, name=p + "out_dw")
        dy3, dz = _merge_bwd(dmerged, s['y3'], s['z'], tr=min(1024, T), tn=512, name=p + "merge")
        toks = sched.slot(l, 'mid', dz)
        dbr = []
        for idx, (wn, opn, width) in enumerate((('w_proj_a', 'pa', POOL_WIDTH), ('w_proj_b', 'at', ATTN_WIDTH),
                                                ('w_proj_c', 'sg', SGU_WIDTH))):
            dbr.append(_mm(dy3, s[wn], mode='nt', a_lead=idx, tm=tm, tn=width, tk=D_MODEL,
                           name=p + f"proj{idx}_dx", deps=toks if idx == 0 else ()))
            gb[wn] = _mm(s[opn], dy3, mode='tn', b_lead=idx, tm=width, tn=D_MODEL, tk=tkt,
                         name=p + f"proj{idx}_dw")
        dpa, dat, dsg = dbr
        dq, dkc, dkp, dvc, dvp, dsk = _attn_bwd(s['q'], s['k'], s['v'], s['sinks'], dat, name=p + "attn")
        gs['sinks'][l] = dsk[:, 0]
        toks = sched.slot(l, 'attn', dq)
        dz, dgq, dgk, dwbd, dsc = _mixer_ab_bwd(s['z'], cos, sin, s['gq'], s['gk'], seg, dq, dkc, dkp, dvc, dvp,
                                                dpa, s['wbd'], s['scale'], dz, tr=tr, name=p + "qkv_pool", deps=toks)
        gs['q_norm'][l] = dgq[0, :HEAD_DIM]
        gs['k_norm'][l] = dgk[0, :HEAD_DIM]
        gs['w_pool'][l] = jnp.stack([dwbd[g * HEAD_DIM:(g + 1) * HEAD_DIM, g * HEAD_DIM:(g + 1) * HEAD_DIM]
                                     for g in range(4)])
        gs['pool_scale'][l] = dsc[0]
        dz, dws, dbrows, dvn = _sgu_bwd(s['z'], s['wtril'], s['bexp'], s['vn'], seg, dsg, dz, tr=tr, name=p + "sgu")
        gs['w_s'][l] = dws
        gs['b_s'][l] = dbrows[:, ::HEAD_DIM].T
        gs['sgu_v_norm'][l] = dvn[0, :HEAD_DIM]
        gb['w_in'] = _mm(s['h1'], dz, mode='tn', tm=D_MODEL, tn=1152, tk=tkt, name=p + "in_dw")
        toks = sched.slot(l, 'mix', gb['w_in'], gb)
        dx, dxb, dg1 = _mm_nt_sharded_rms(dz, s['w_in'], s['x'], s['g1'], dx1, tm=min(256, T),
                                          name=p + "in_dx_rms1", deps=toks)
        gs['norm1'][l] = dg1[0]
    gs = {n: jnp.stack(v) for n, v in gs.items()}
    return loss_row, dx, gs


GROUP_F = ('w_down', 'w_up')
GROUP_M = ('w_out', 'w_proj_a', 'w_proj_b', 'w_proj_c', 'w_in')
ROW_SHARDED = ('w_out', 'w_down')

REDUCE_PLAN = {
    (1, 'ffn'): (('S1', 'F', 1),),
    (1, 'mid'): (('W1', 'F', 1),),
    (1, 'mix'): (('S1', 'M', 1),),
    (0, 'down'): (('W1', 'M', 1),),
    (0, 'conv'): (('W2', 'F', 1),),
    (0, 'ffn'): (('S1', 'F', 0), ('W3', 'F', 1)),
    (0, 'mid'): (('W1', 'F', 0),),
    (0, 'attn'): (('W2', 'M', 1),),
    (0, 'mix'): (('S1', 'M', 0), ('W3', 'M', 1)),
}
REDUCE_TAIL_A = (('W1', 'M', 0), ('W2', 'F', 0))
REDUCE_TAIL_B = (('W3', 'F', 0),)
REDUCE_TAIL_C = (('W2', 'M', 0), ('W3', 'M', 0))


class _Comm:
    def __init__(self, w, pos):
        self.pos = pos
        groups = {'a': [('w_in', 0)],
                  'b': [(n, 0) for n in ('w_proj_a', 'w_proj_b', 'w_proj_c', 'w_out')],
                  'c': [(n, 0) for n in ('w_up', 'conv_w', 'w_down')],
                  'd': [(n, 1) for n in BIG] + [('conv_w', 1)]}
        self.gathers, self.group_of, self.weights = {}, {}, {}
        self.tokens = []
        for g, ks in groups.items():
            lands = [_cast_place(w[n], pos, F32 if n == 'conv_w' else MXU_DTYPE, layer=l, name=f"gw_place_{n}{l}")
                     for n, l in ks]
            self.gathers[g] = (_Gather(lands, "gw_" + g, deps=self.tokens[-1:]), ks)
            self.tokens.append(self.gathers[g][0].token)
            self.group_of.update({k: g for k in ks})
        self.red = {}
        self.final = {}

    def start_tokens(self):
        return self.tokens[-1:]

    def weight(self, name, layer, after):
        if (name, layer) not in self.weights:
            gather, ks = self.gathers[self.group_of[(name, layer)]]
            for (n, l), full in zip(ks, gather.wait(after)):
                if n == 'conv_w' or n.startswith('w_proj'):
                    full = full.transpose(1, 0, 2).reshape(full.shape[1], -1)
                elif n in ROW_SHARDED:
                    full = full.reshape(-1, full.shape[2])
                self.weights[(n, l)] = full
        return self.weights[(name, layer)]

    def slot(self, layer, slot, after, grads=None):
        tokens = []
        for step, grp, lyr in REDUCE_PLAN.get((layer, slot), ()):
            tok = self._step(step, grp, lyr, after, grads)
            if tok is not None:
                tokens.append(tok)
        return tokens

    def tail(self, steps, after, deps=()):
        toks = (self._step(step, grp, lyr, after, None, deps) for step, grp, lyr in steps)
        return [t for t in toks if t is not None]

    def shards(self):
        return {n: f.reshape(DEPTH, 2 * f.shape[2], f.shape[3]) for n, f in self.final.items()}

    def _step(self, step, grp, layer, after, grads, deps=()):
        names = GROUP_F if grp == 'F' else GROUP_M
        tag = f"{grp.lower()}{layer}"
        st = self.red.setdefault((grp, layer), {})
        n = len(names)
        if step == 'S1':
            g4s = []
            for nm in names:
                g = grads[nm]
                R, C = g.shape
                g4s.append(g.reshape(N_CHIPS, 2, R // (2 * N_CHIPS), C) if nm in ROW_SHARDED
                           else g.reshape(1, 2, R // 2, C))
            st['s1'] = _swap_halves_start(g4s, name="rs1_" + tag)
            return st['s1'][2]
        if step == 'W1':
            sems, arrays, _, waits = st.pop('s1')
            arrays = _split_wait(sems, arrays, after, waits, name=f"rs1_{tag}_wait")
            parts = [_pair_add(arrays[i], arrays[n + i], self.pos, name=f"pair_add_{tag}_{names[i]}")
                     for i in range(n)]
            st['s2'] = _scatter_start(parts, name="rs2_" + tag, deps=deps)
            return st['s2'][2]
        if step == 'W2':
            sems, arrays, _, waits = st.pop('s2')
            arrays = _split_wait(sems, arrays, after, waits, name=f"rs2_{tag}_wait")
            fs = [_chip_sum(arrays[i], arrays[n + i], self.final.get(names[i]), self.pos, layer,
                            name=f"chip_sum_{tag}_{names[i]}") for i in range(n)]
            st['s3'] = _pair_share_start(fs, layer, name="rs3_" + tag)
            return st['s3'][2]
        sems, arrays, _, waits = st.pop('s3')
        self.final.update(zip(names, _split_wait(sems, arrays, after, waits, name=f"rs3_{tag}_wait")))
        return None


def _pack(arrays):
    flat = []
    for a in arrays:
        f = a.reshape(-1).astype(F32)
        flat.append(jnp.pad(f, (0, (-f.shape[0]) % 128)))
    v = jnp.concatenate(flat)
    v = jnp.pad(v, (0, (-v.shape[0]) % 1024))
    return v.reshape(-1, 128)


def _unpack(pack, shapes):
    v = pack.reshape(-1)
    out, off = [], 0
    for shp in shapes:
        nel = int(np.prod(shp))
        out.append(v[off:off + nel].reshape(shp))
        off += nel + (-nel) % 128
    return out


def kernel(x, positions, norm1, w_in, q_norm, k_norm, sinks, w_pool, pool_scale, sgu_v_norm, w_s, b_s, w_proj_a, w_proj_b, w_proj_c, w_out, norm2, w_up, conv_w, conv_b, w_down, loss_target, m_norm1, m_w_in, m_q_norm, m_k_norm, m_sinks, m_w_pool, m_pool_scale, m_sgu_v_norm, m_w_s, m_b_s, m_w_proj_a, m_w_proj_b, m_w_proj_c, m_w_out, m_norm2, m_w_up, m_conv_w, m_conv_b, m_w_down, v_norm1, v_w_in, v_q_norm, v_k_norm, v_sinks, v_w_pool, v_pool_scale, v_sgu_v_norm, v_w_s, v_b_s, v_w_proj_a, v_w_proj_b, v_w_proj_c, v_w_out, v_norm2, v_w_up, v_conv_w, v_conv_b, v_w_down):
    w = dict(norm1=norm1, w_in=w_in, q_norm=q_norm, k_norm=k_norm, sinks=sinks, w_pool=w_pool, pool_scale=pool_scale,
             sgu_v_norm=sgu_v_norm, w_s=w_s, b_s=b_s, w_proj_a=w_proj_a, w_proj_b=w_proj_b, w_proj_c=w_proj_c,
             w_out=w_out, norm2=norm2, w_up=w_up, conv_w=conv_w, conv_b=conv_b, w_down=w_down)
    m = dict(norm1=m_norm1, w_in=m_w_in, q_norm=m_q_norm, k_norm=m_k_norm, sinks=m_sinks, w_pool=m_w_pool,
             pool_scale=m_pool_scale, sgu_v_norm=m_sgu_v_norm, w_s=m_w_s, b_s=m_b_s, w_proj_a=m_w_proj_a,
             w_proj_b=m_w_proj_b, w_proj_c=m_w_proj_c, w_out=m_w_out, norm2=m_norm2, w_up=m_w_up, conv_w=m_conv_w,
             conv_b=m_conv_b, w_down=m_w_down)
    v = dict(norm1=v_norm1, w_in=v_w_in, q_norm=v_q_norm, k_norm=v_k_norm, sinks=v_sinks, w_pool=v_w_pool,
             pool_scale=v_pool_scale, sgu_v_norm=v_sgu_v_norm, w_s=v_w_s, b_s=v_b_s, w_proj_a=v_w_proj_a,
             w_proj_b=v_w_proj_b, w_proj_c=v_w_proj_c, w_out=v_w_out, norm2=v_norm2, w_up=v_w_up, conv_w=v_conv_w,
             conv_b=v_conv_b, w_down=v_w_down)
    chip = 2 * lax.axis_index("x") + lax.axis_index("y")
    core = lax.axis_index("c")

    pos = jnp.stack([chip, core, 2 * chip + core]).astype(jnp.int32)
    comm = _Comm(w, pos)

    cos, sin = _rope_tables(positions[0])
    sp = {n: w[n] for n in SMALL if n != 'conv_w'}
    loss_row, dx, gs = _local_step(x[0], loss_target[0], cos, sin, sp, comm)
    loss = lax.psum(loss_row[0, 0], ("x", "y", "c"))

    delta, new_m, new_v = {}, {}, {}

    def adamw_big(names, grads):
        for n in names:
            shp = w[n].shape
            two_d = lambda a: a.reshape(shp[0] * shp[1], shp[2])
            d, nm, nv = _adamw(two_d(w[n]), two_d(grads[n]), two_d(m[n]), two_d(v[n]),
                               tr=_row_tile(shp[0] * shp[1], 256), name=f"adamw_{n}")
            delta[n], new_m[n], new_v[n] = d.reshape(shp), nm.reshape(shp), nv.reshape(shp)

    small_shapes = [gs[n].shape for n in SMALL]
    small_pack = _pack([gs[n] for n in SMALL])
    small = _Gather([_cast_place(small_pack, pos, F32, slots=N_DEV, which=2, name="small_place")], "small_gather",
                    all_devices=True)
    toks = comm.tail(REDUCE_TAIL_A[:1], (dx, small.token))
    comm.tail(REDUCE_TAIL_A[1:], (dx, *toks))
    comm.tail(REDUCE_TAIL_B, dx)
    adamw_big(GROUP_F, comm.shards())
    red = _sum_slots(small.wait(new_v[GROUP_F[-1]])[0], tr=small_pack.shape[0], name="small_sum")
    g_small = dict(zip(SMALL, _unpack(red, small_shapes)))
    comm.tail(REDUCE_TAIL_C, red)
    grads = comm.shards()
    grads.update(g_small)
    shard_cols = conv_w.shape[2]
    grads['conv_w'] = lax.dynamic_slice_in_dim(g_small['conv_w'], chip * shard_cols, shard_cols, axis=2)

    adamw_big(GROUP_M, grads)
    shapes = [w[n].shape for n in SMALL]
    packs = [_pack([src[n] for n in SMALL]) for src in (w, grads, m, v)]
    d, nm, nv = _adamw(*packs, tr=packs[0].shape[0], name="adamw_small")
    for dst, src in ((delta, d), (new_m, nm), (new_v, nv)):
        dst.update(zip(SMALL, _unpack(src, shapes)))

    return (loss, dx[None], *[grads[n] for n in WEIGHTS], *[delta[n] for n in WEIGHTS],
            *[new_m[n] for n in WEIGHTS], *[new_v[n] for n in WEIGHTS])
```

```python
import functools
import math

import numpy as np
import jax
import jax.numpy as jnp
from jax import lax
from jax.experimental import pallas as pl
from jax.experimental.pallas import tpu as pltpu

F32 = jnp.float32
MXU_DTYPE = jnp.bfloat16
COMM_DTYPE = jnp.bfloat16

D_MODEL = 1024
DEPTH = 2
HEAD_DIM = 64
POOL_WINDOWS = (2, 4, 8, 16)
POOL_WIDTH = 256
N_Q_HEADS = 8
ATTN_BLOCK = 128
ATTN_WIDTH = 512
KV_WIDTH = 128
CHUNK = 128
SGU_WIDTH = 256
IN_COLS = 4608
GATE_COL0 = 1536
D_FF = 2816
ROPE_THETA = 10000.0
EPS = 1e-6
ADAM_LR, ADAM_B1, ADAM_B2, ADAM_EPS, ADAM_WD, ADAM_STEP = 0.001, 0.9, 0.999, 1e-08, 0.01, 10

N_CHIPS = 4
N_DEV = 8
VMEM_LIMIT_BYTES = 56 * 1024 * 1024
NEG_BIG = -1e30
MESH = pl.DeviceIdType.MESH
ANY = pl.BlockSpec(memory_space=pl.ANY)

SDS = jax.ShapeDtypeStruct


def _cp(*sem):
    return pltpu.CompilerParams(dimension_semantics=sem, vmem_limit_bytes=VMEM_LIMIT_BYTES)


def _dot(a, b, dims=((1,), (0,))):
    return lax.dot_general(a.astype(MXU_DTYPE), b.astype(MXU_DTYPE), (dims, ((), ())),
                           preferred_element_type=F32)


NT = ((1,), (1,))
TN = ((0,), (0,))


def _split_dot(x, m):
    hi = x.astype(MXU_DTYPE)
    lo = (x - hi.astype(F32)).astype(MXU_DTYPE)
    return _dot(hi, m) + _dot(lo, m)


def _seg_matrix(width, seg):
    idx = np.arange(width) // seg
    return jnp.asarray((idx[:, None] == idx[None, :]).astype(np.float32), dtype=MXU_DTYPE)


def _lane(shape):
    return lax.broadcasted_iota(jnp.int32, shape, len(shape) - 1)


def _row(shape):
    return lax.broadcasted_iota(jnp.int32, shape, 0)


def _full(shape):
    nd = len(shape)
    return pl.BlockSpec(shape, lambda *_: (0,) * nd)


def _gelu(x):
    k = math.sqrt(2.0 / math.pi)
    th = jnp.tanh(k * (x + 0.044715 * (x * x * x)))
    return 0.5 * x * (1.0 + th)


def _gelu_and_grad(x):
    k = math.sqrt(2.0 / math.pi)
    x2 = x * x
    th = jnp.tanh(k * (x + 0.044715 * (x2 * x)))
    g = 0.5 * x * (1.0 + th)
    dg = 0.5 * (1.0 + th) + 0.5 * x * (1.0 - th * th) * (k * (1.0 + 3.0 * 0.044715 * x2))
    return g, dg


def _sigmoid(x):
    return 0.5 * jnp.tanh(0.5 * x) + 0.5


def _swap_halves(x):
    w = x.shape[-1]
    first = (_lane(x.shape) % HEAD_DIM) < (HEAD_DIM // 2)
    return jnp.where(first, pltpu.roll(x, w - HEAD_DIM // 2, 1), pltpu.roll(x, HEAD_DIM // 2, 1))


def _tile_lanes(x, reps):
    return x if reps == 1 else jnp.concatenate([x] * reps, axis=1)


def _fold_lanes(x, period):
    w = x.shape[-1]
    while w > period:
        w //= 2
        x = x + pltpu.roll(x, w, 1)
    return x


def _mm(a, b, *, mode, tm, tn, tk, out_dtype=F32, add=None, name,
        a_lead=None, b_lead=None, b_sharded=False, out_into=None,
        b_koff=0, out_joff=0, out_n=None, deps=()):
    ash = a.shape[1:] if a_lead is not None else a.shape
    bsh = b.shape[1:] if b_lead is not None else b.shape
    if b_sharded:
        bsh = (b.shape[1], N_CHIPS * b.shape[2])
    if mode == 'nn':
        (M, K), (K2, N) = ash, bsh
    elif mode == 'nt':
        (M, K), (N, K2) = ash, bsh
    else:
        (K, M), (K2, N) = ash, bsh
    assert K == K2 or (mode == 'nt' and K2 > K), (ash, bsh, mode)
    assert M % tm == 0 and N % tn == 0 and K % tk == 0, (M, N, K, tm, tn, tk)
    nk = K // tk
    dims = {'nn': ((1,), (0,)), 'nt': NT, 'tn': TN}[mode]

    def lead(spec_shape, imap, lead_idx):
        if lead_idx is None:
            return pl.BlockSpec(spec_shape, imap)
        return pl.BlockSpec((None,) + spec_shape, lambda i, j, k: (lead_idx,) + imap(i, j, k))

    if mode == 'tn':
        a_spec = lead((tk, tm), lambda i, j, k: (k, i), a_lead)
    else:
        a_spec = lead((tm, tk), lambda i, j, k: (i, k), a_lead)
    if b_sharded:
        per = b.shape[2] // (tk if mode == 'nt' else tn)
        assert per * (tk if mode == 'nt' else tn) == b.shape[2] and mode != 'tn'
        if mode == 'nt':
            b_spec = pl.BlockSpec((None, tn, tk), lambda i, j, k: ((k + b_koff) // per, j, (k + b_koff) % per))
        else:
            b_spec = pl.BlockSpec((None, tk, tn), lambda i, j, k: (j // per, k, j % per))
    elif mode == 'nt':
        b_spec = lead((tn, tk), lambda i, j, k: (j, k + b_koff), b_lead)
    else:
        b_spec = lead((tk, tn), lambda i, j, k: (k, j), b_lead)
    o_spec = pl.BlockSpec((tm, tn), lambda i, j, k: (i, j + out_joff))
    n_out = N if out_n is None else out_n
    in_specs = [a_spec, b_spec]
    operands = [a, b]
    if add is not None:
        in_specs.append(pl.BlockSpec((tm, tn), lambda i, j, k: (i, j)))
        operands.append(add)
    aliases = {}
    if out_into is not None:
        in_specs.append(ANY)
        operands.append(out_into)
        aliases = {len(operands) - 1: 0}
    in_specs += [ANY] * len(deps)
    operands += list(deps)
    has_add = add is not None
    acc_in_out = nk > 1 and out_dtype == F32

    def body(*refs):
        a_ref, b_ref = refs[0], refs[1]
        pos = 2
        add_ref = None
        if has_add:
            add_ref = refs[pos]
            pos += 1
        if out_into is not None:
            pos += 1
        pos += len(deps)
        o_ref = refs[pos]
        acc_ref = refs[pos + 1] if (nk > 1 and not acc_in_out) else None
        p = _dot(a_ref[...], b_ref[...], dims)
        if nk == 1:
            if has_add:
                p = p + add_ref[...]
            o_ref[...] = p.astype(o_ref.dtype)
            return
        k = pl.program_id(2)
        tgt = o_ref if acc_in_out else acc_ref

        @pl.when(k == 0)
        def _():
            tgt[...] = p + add_ref[...] if has_add else p

        @pl.when(k > 0)
        def _():
            tgt[...] += p

        if not acc_in_out:
            @pl.when(k == nk - 1)
            def _():
                o_ref[...] = acc_ref[...].astype(o_ref.dtype)

    out_shape = SDS((M, n_out), out_dtype)
    scratch = [pltpu.VMEM((tm, tn), F32)] if (nk > 1 and not acc_in_out) else []
    return pl.pallas_call(
        body, grid=(M // tm, N // tn, nk), in_specs=in_specs, out_specs=o_spec, out_shape=out_shape,
        scratch_shapes=scratch, input_output_aliases=aliases, name=name,
        compiler_params=_cp("parallel", "parallel", "arbitrary"))(*operands)


def _rms_bwd_rows(xv, g, dh, dres):
    r = lax.rsqrt(jnp.mean(xv * xv, axis=-1, keepdims=True) + EPS)
    xh = xv * r
    gy = dh * g
    dx = r * (gy - xh * jnp.mean(xh * gy, axis=-1, keepdims=True)) + dres
    return dx, jnp.sum(dh * xh, axis=0, keepdims=True)


def _mm_nt_sharded_rms(a, b, x, g, dres, *, tm, name, deps=()):
    a3 = a if a.ndim == 3 else a[None]
    A, M, ka = a3.shape
    S, N, ns = b.shape
    per = S // A
    assert ka == per * ns and M % tm == 0 and N == x.shape[1], (a3.shape, b.shape, x.shape)

    def body(a_ref, b_ref, x_ref, g_ref, dres_ref, dx_ref, dxb_ref, dg_ref):
        acc = None
        for s in range(S):
            lo = (s % per) * ns
            p = _dot(a_ref[s // per, :, lo:lo + ns], b_ref[s], NT)
            acc = p if acc is None else acc + p
        dx, dg = _rms_bwd_rows(x_ref[...], g_ref[...], acc, dres_ref[...])
        dx_ref[...] = dx
        dxb_ref[...] = dx.astype(dxb_ref.dtype)

        @pl.when(pl.program_id(0) == 0)
        def _():
            dg_ref[...] = jnp.zeros_like(dg_ref)
        dg_ref[...] += dg

    rows = pl.BlockSpec((tm, N), lambda i: (i, 0))
    return pl.pallas_call(
        _after(body, 5, deps), grid=(M // tm,),
        in_specs=[pl.BlockSpec((A, tm, ka), lambda i: (0, i, 0)), pl.BlockSpec((S, N, ns), lambda i: (0, 0, 0)),
                  rows, _full((1, N)), rows] + [ANY] * len(deps),
        out_specs=[rows, rows, _full((1, N))],
        out_shape=[SDS((M, N), F32), SDS((M, N), MXU_DTYPE), SDS((1, N), F32)], name=name,
        compiler_params=_cp("arbitrary"))(a3, b, x, g, dres, *deps)


def _norm_mm(x, g, b, *, tm, tn, name, deps=()):
    M, K = x.shape
    S, K2, ns = b.shape
    per = ns // tn
    assert K == K2 and per * tn == ns and M % tm == 0, (x.shape, b.shape)

    def body(x_ref, g_ref, b_ref, o_ref, h_ref):
        @pl.when(pl.program_id(1) == 0)
        def _():
            xv = x_ref[...]
            r = lax.rsqrt(jnp.mean(xv * xv, axis=-1, keepdims=True) + EPS)
            h_ref[...] = (xv * r * g_ref[...]).astype(h_ref.dtype)
        o_ref[...] = _dot(h_ref[...], b_ref[...])

    return pl.pallas_call(
        _after(body, 3, deps), grid=(M // tm, S * per),
        in_specs=[pl.BlockSpec((tm, K), lambda i, j: (i, 0)), _full((1, K)),
                  pl.BlockSpec((None, K, tn), lambda i, j: (j // per, 0, j % per))] + [ANY] * len(deps),
        out_specs=[pl.BlockSpec((tm, tn), lambda i, j: (i, j)), pl.BlockSpec((tm, K), lambda i, j: (i, 0))],
        out_shape=[SDS((M, S * ns), F32), SDS((M, K), MXU_DTYPE)], name=name,
        compiler_params=_cp("parallel", "arbitrary"))(x, g, b, *deps)


def _after(body, n_in, deps):
    nd = len(deps)
    if nd == 0:
        return body
    return lambda *refs: body(*refs[:n_in], *refs[n_in + nd:])


def _down_proj_loss(act, w, x1, target, *, tm, name):
    T, K = act.shape
    D = w.shape[1]

    def body(a_ref, w_ref, x_ref, t_ref, loss_ref, dy_ref, dyb_ref):
        i = pl.program_id(0)
        d = (x_ref[...] + _dot(a_ref[...], w_ref[...])) - t_ref[...]
        dy = d * (1.0 / D)
        dy_ref[...] = dy
        dyb_ref[...] = dy.astype(dyb_ref.dtype)
        part = jnp.sum(jnp.sum(d * d, axis=1, keepdims=True), axis=0, keepdims=True) * (0.5 / D)

        @pl.when(i == 0)
        def _():
            loss_ref[...] = jnp.zeros_like(loss_ref)
        loss_ref[...] += jnp.broadcast_to(part, loss_ref.shape)

    rows = pl.BlockSpec((tm, D), lambda i: (i, 0))
    return pl.pallas_call(
        body, grid=(T // tm,), in_specs=[pl.BlockSpec((tm, K), lambda i: (i, 0)), _full((K, D)), rows, rows],
        out_specs=[_full((1, 128)), rows, rows],
        out_shape=[SDS((1, 128), F32), SDS((T, D), F32), SDS((T, D), MXU_DTYPE)],
        name=name, compiler_params=_cp("arbitrary"))(act, w, x1, target)


def _pool_lane_consts(shape):
    lane = _lane(shape)
    grp = lane // (POOL_WIDTH // 4)
    win = jnp.where(grp == 0, 2, jnp.where(grp == 1, 4, jnp.where(grp == 2, 8, 16)))
    return grp, win


def _pool_select(grp, s2, s4, s8, s16):
    return jnp.where(grp == 0, s2, jnp.where(grp == 1, s4, jnp.where(grp == 2, s8, s16)))


def _pool_diff(xe, row0, tr):
    s2 = xe + pltpu.roll(xe, 1, 0)
    s4 = s2 + pltpu.roll(s2, 2, 0)
    s8 = s4 + pltpu.roll(s4, 4, 0)
    s16 = s8 + pltpu.roll(s8, 8, 0)
    shape = (tr, POOL_WIDTH)
    grp, win = _pool_lane_consts(shape)
    sums = _pool_select(grp, s2[16:], s4[16:], s8[16:], s16[16:])
    t = row0 + _row(shape)
    cnt = jnp.minimum(t + 1, win).astype(F32)
    return sums / cnt - xe[16:]


def _pool_fwd(z, wbd, scale, *, tr, name):
    T = z.shape[0]
    hb = tr // 16

    def body(x_ref, xp_ref, w_ref, s_ref, o_ref):
        i = pl.program_id(0)
        halo = jnp.where(i == 0, 0.0, xp_ref[...])
        diff = _pool_diff(jnp.concatenate([halo, x_ref[...]], axis=0), i * tr, tr)
        o_ref[...] = (_dot(diff, w_ref[...]) * s_ref[...]).astype(o_ref.dtype)

    return pl.pallas_call(
        body, grid=(T // tr,),
        in_specs=[pl.BlockSpec((tr, POOL_WIDTH), lambda i: (i, 0)),
                  pl.BlockSpec((16, POOL_WIDTH), lambda i: (jnp.maximum(i * hb - 1, 0), 0)),
                  _full((POOL_WIDTH, POOL_WIDTH)), _full((1, POOL_WIDTH))],
        out_specs=pl.BlockSpec((tr, POOL_WIDTH), lambda i: (i, 0)),
        out_shape=SDS((T, POOL_WIDTH), MXU_DTYPE), name=name, compiler_params=_cp("parallel"))(z, z, wbd, scale)


def _pool_bwd_tile(i, n, tr, x, xprev, dpa, dpa_next, wbd, scale):
    halo = jnp.where(i == 0, 0.0, xprev)
    diff = _pool_diff(jnp.concatenate([halo, x], axis=0), i * tr, tr)
    mixed = _dot(diff, wbd)
    dscale = jnp.sum(dpa * mixed, axis=0, keepdims=True)
    dnext = jnp.where(i == n - 1, 0.0, dpa_next)
    dmix_e = jnp.concatenate([dpa, dnext], axis=0) * scale
    ddiff_e = _dot(dmix_e, wbd, NT)
    dwbd = _dot(diff, dmix_e[:tr], TN)
    shape = (tr + 16, POOL_WIDTH)
    grp, win = _pool_lane_consts(shape)
    t = i * tr + _row(shape)
    e = ddiff_e / jnp.minimum(t + 1, win).astype(F32)
    nrow = tr + 16
    a2 = e + pltpu.roll(e, nrow - 1, 0)
    a4 = a2 + pltpu.roll(a2, nrow - 2, 0)
    a8 = a4 + pltpu.roll(a4, nrow - 4, 0)
    a16 = a8 + pltpu.roll(a8, nrow - 8, 0)
    dx = _pool_select(grp, a2, a4, a8, a16)[:tr] - ddiff_e[:tr]
    return dx, dwbd, dscale


def _norm_rope(x, g, cos, sin_signed, seg):
    reps = x.shape[1] // 128
    ms = _split_dot(x * x, seg) * (1.0 / HEAD_DIM)
    r = lax.rsqrt(ms + EPS)
    xn = x * r * g
    c, s = _tile_lanes(cos, reps), _tile_lanes(sin_signed, reps)
    return xn * c + _swap_halves(xn) * s


def _norm_rope_bwd(x, g, cos, sin_signed, seg, dout):
    reps = x.shape[1] // 128
    c, s = _tile_lanes(cos, reps), _tile_lanes(sin_signed, reps)
    dxn = dout * c + _swap_halves(dout * s)
    ms = _split_dot(x * x, seg) * (1.0 / HEAD_DIM)
    r = lax.rsqrt(ms + EPS)
    xh = x * r
    gy = dxn * g
    dx = r * (gy - xh * (_split_dot(xh * gy, seg) * (1.0 / HEAD_DIM)))
    dg = jnp.sum(dxn * xh, axis=0, keepdims=True)
    return dx, dg


def _dup_heads(k):
    first = _lane(k.shape) < HEAD_DIM
    kr = pltpu.roll(k, HEAD_DIM, 1)
    return jnp.concatenate([jnp.where(first, k, kr), jnp.where(first, kr, k)], axis=1)


def _qkv_prep(z, cos, sin_signed, gq, gk, seg, *, tr, name):
    T = z.shape[0]

    def body(qa_ref, qb_ref, kv_ref, c_ref, s_ref, gq_ref, gk_ref, seg_ref, q_ref, k_ref, v_ref):
        c, s, seg_m = c_ref[...], s_ref[...], seg_ref[...]
        scale = HEAD_DIM ** -0.5
        qa = _norm_rope(qa_ref[...], gq_ref[...], c, s, seg_m) * scale
        qb = _norm_rope(qb_ref[...], gq_ref[...], c, s, seg_m) * scale
        q_ref[...] = jnp.concatenate([qa, qb], axis=1).astype(q_ref.dtype)
        kv = kv_ref[...]
        k = _norm_rope(kv[:, :KV_WIDTH], gk_ref[...], c, s, seg_m[:128, :128])
        k_ref[...] = _dup_heads(k).astype(k_ref.dtype)
        v_ref[...] = _dup_heads(kv[:, KV_WIDTH:]).astype(v_ref.dtype)

    col = lambda j: pl.BlockSpec((tr, 256), lambda i: (i, j))
    tab = pl.BlockSpec((tr, 128), lambda i: (i, 0))
    return pl.pallas_call(
        body, grid=(T // tr,),
        in_specs=[col(1), col(2), col(3), tab, tab, _full((1, 256)), _full((1, 128)), _full((256, 256))],
        out_specs=[pl.BlockSpec((tr, 512), lambda i: (i, 0)), col(0), col(0)],
        out_shape=[SDS((T, 512), MXU_DTYPE), SDS((T, 256), MXU_DTYPE), SDS((T, 256), MXU_DTYPE)],
        name=name, compiler_params=_cp("parallel"))(z, z, z, cos, sin_signed, gq, gk, seg)


GROUP_HEADS = 4
GROUP_ROWS = GROUP_HEADS * ATTN_BLOCK
ALL_ROWS = N_Q_HEADS * ATTN_BLOCK


def _attn_mask(n):
    qi = _row((ALL_ROWS, 2 * ATTN_BLOCK)) % ATTN_BLOCK
    kj = _lane((ALL_ROWS, 2 * ATTN_BLOCK))
    return (kj > qi) & (kj <= qi + ATTN_BLOCK) & ((kj >= ATTN_BLOCK) | (n > 0))


def _stack_heads(x, g):
    first = _lane((ATTN_BLOCK, 128)) < HEAD_DIM
    parts = []
    for pair in (2 * g, 2 * g + 1):
        x128 = x[:, 128 * pair:128 * (pair + 1)]
        zero = jnp.zeros_like(x128)
        parts += [jnp.where(first, x128, zero), jnp.where(first, zero, x128)]
    return jnp.concatenate(parts, axis=0)


def _unstack_heads(y):
    first = _lane((ATTN_BLOCK, 128)) < HEAD_DIM
    b = ATTN_BLOCK
    return jnp.concatenate([jnp.where(first, y[0:b], y[b:2 * b]), jnp.where(first, y[2 * b:3 * b], y[3 * b:4 * b])],
                           axis=1)


def _sink_col(sk_ref):
    return jnp.concatenate([jnp.broadcast_to(sk_ref[h:h + 1, 0:1], (ATTN_BLOCK, 1)) for h in range(N_Q_HEADS)],
                           axis=0)


def _by_group(a8, b2, dims=((1,), (0,))):
    return jnp.concatenate([_dot(a8[:GROUP_ROWS], b2[:, :128], dims), _dot(a8[GROUP_ROWS:], b2[:, 128:], dims)],
                           axis=0)


def _softmax_exp(q8, k2, mask, sink):
    s = jnp.where(mask, _by_group(q8, k2, NT), NEG_BIG)
    m = jnp.maximum(jnp.max(s, axis=1, keepdims=True), sink)
    p = jnp.exp(s - m)
    ps = jnp.exp(sink - m)
    return p, ps, 1.0 / (jnp.sum(p, axis=1, keepdims=True) + ps)


def _attn_fwd(q, k, v, sinks_b, *, name):
    T = q.shape[0]
    nb = T // ATTN_BLOCK

    def body(q_ref, kc_ref, kp_ref, vc_ref, vp_ref, sk_ref, o_ref):
        n = pl.program_id(0)
        mask = _attn_mask(n)
        k2 = jnp.concatenate([kp_ref[...], kc_ref[...]], axis=0)
        v2 = jnp.concatenate([vp_ref[...], vc_ref[...]], axis=0)
        qv = q_ref[...]
        q8 = jnp.concatenate([_stack_heads(qv, 0), _stack_heads(qv, 1)], axis=0)
        p, _, inv = _softmax_exp(q8, k2, mask, _sink_col(sk_ref))
        o8 = _by_group(p, v2) * inv
        o_ref[...] = jnp.concatenate([_unstack_heads(o8[:GROUP_ROWS]), _unstack_heads(o8[GROUP_ROWS:])],
                                     axis=1).astype(o_ref.dtype)

    cur = lambda w: pl.BlockSpec((ATTN_BLOCK, w), lambda n: (n, 0))
    prev = lambda w: pl.BlockSpec((ATTN_BLOCK, w), lambda n: (jnp.maximum(n - 1, 0), 0))
    return pl.pallas_call(
        body, grid=(nb,),
        in_specs=[cur(512), cur(256), prev(256), cur(256), prev(256), _full((8, 128))],
        out_specs=cur(512), out_shape=SDS((T, 512), MXU_DTYPE), name=name,
        compiler_params=_cp("parallel"))(q, k, k, v, v, sinks_b)


def _attn_bwd(q, k, v, sinks_b, do, *, name):
    T = q.shape[0]
    nb = T // ATTN_BLOCK

    def body(q_ref, kc_ref, kp_ref, vc_ref, vp_ref, sk_ref, do_ref,
             dq_ref, dkc_ref, dkp_ref, dvc_ref, dvp_ref, dsk_ref):
        n = pl.program_id(0)
        mask = _attn_mask(n)
        k2 = jnp.concatenate([kp_ref[...], kc_ref[...]], axis=0)
        v2 = jnp.concatenate([vp_ref[...], vc_ref[...]], axis=0)
        qv = q_ref[...]
        dov = do_ref[...]

        @pl.when(n == 0)
        def _():
            dsk_ref[...] = jnp.zeros_like(dsk_ref)

        q8 = jnp.concatenate([_stack_heads(qv, 0), _stack_heads(qv, 1)], axis=0)
        do8 = jnp.concatenate([_stack_heads(dov, 0), _stack_heads(dov, 1)], axis=0)
        p, ps, inv = _softmax_exp(q8, k2, mask, _sink_col(sk_ref))
        pn = p * inv
        delta = jnp.sum(do8 * _by_group(pn, v2), axis=1, keepdims=True)
        ds = pn * (_by_group(do8, v2, NT) - delta)
        dq8 = _by_group(ds, k2)
        dq_ref[...] = jnp.concatenate([_unstack_heads(dq8[:GROUP_ROWS]), _unstack_heads(dq8[GROUP_ROWS:])], axis=1)
        dk = jnp.concatenate([_dot(ds[:GROUP_ROWS], q8[:GROUP_ROWS], TN), _dot(ds[GROUP_ROWS:], q8[GROUP_ROWS:], TN)],
                             axis=1)
        dv = jnp.concatenate([_dot(pn[:GROUP_ROWS], do8[:GROUP_ROWS], TN),
                              _dot(pn[GROUP_ROWS:], do8[GROUP_ROWS:], TN)], axis=1)
        wsink = (ps * inv) * delta
        for h in range(N_Q_HEADS):
            dsink = -jnp.sum(wsink[ATTN_BLOCK * h:ATTN_BLOCK * (h + 1)], axis=0, keepdims=True)
            dsk_ref[h:h + 1, :] += jnp.broadcast_to(dsink, (1, 128))
        dkp_ref[...] = dk[:ATTN_BLOCK]
        dkc_ref[...] = dk[ATTN_BLOCK:]
        dvp_ref[...] = dv[:ATTN_BLOCK]
        dvc_ref[...] = dv[ATTN_BLOCK:]

    cur = lambda w: pl.BlockSpec((ATTN_BLOCK, w), lambda n: (n, 0))
    prev = lambda w: pl.BlockSpec((ATTN_BLOCK, w), lambda n: (jnp.maximum(n - 1, 0), 0))
    f = lambda w: SDS((T, w), F32)
    return pl.pallas_call(
        body, grid=(nb,),
        in_specs=[cur(512), cur(256), prev(256), cur(256), prev(256), _full((8, 128)), cur(512)],
        out_specs=[cur(512), cur(256), cur(256), cur(256), cur(256), _full((8, 128))],
        out_shape=[f(512), f(256), f(256), f(256), f(256), SDS((8, 128), F32)],
        name=name, compiler_params=_cp("arbitrary"))(q, k, k, v, v, sinks_b, do)


def _mixer_ab_bwd(z, cos, sin_signed, gq, gk, seg, dq, dkc, dkp, dvc, dvp, dpa, wbd, scale, dz, *, tr, name, deps=()):
    T = z.shape[0]
    n = T // tr
    hb = tr // 16
    ab = tr // ATTN_BLOCK

    def unfold(cur, nxt_tile, nxt_halo, i):
        nxt = jnp.concatenate([nxt_tile[ATTN_BLOCK:], jnp.where(i == n - 1, 0.0, nxt_halo)], axis=0)
        tot = cur + nxt
        first = _lane((tr, 128)) < HEAD_DIM
        a = tot[:, :128]
        b = tot[:, 128:]
        a = a + pltpu.roll(a, HEAD_DIM, 1)
        b = b + pltpu.roll(b, HEAD_DIM, 1)
        return jnp.where(first, a, b)

    def body(xp_ref, xpp_ref, qa_ref, qb_ref, kv_ref, c_ref, s_ref, gq_ref, gk_ref, seg_ref,
             dq_ref, dkc_ref, dkp_ref, dkh_ref, dvc_ref, dvp_ref, dvh_ref, dpa_ref, dpan_ref, w_ref, sc_ref, _dz_in,
             dz_ref, dgq_ref, dgk_ref, dw_ref, dsc_ref):
        i = pl.program_id(0)
        c, s, seg_m = c_ref[...], s_ref[...], seg_ref[...]
        scale_q = HEAD_DIM ** -0.5
        dqv = dq_ref[...] * scale_q
        dxa, dga = _norm_rope_bwd(qa_ref[...], gq_ref[...], c, s, seg_m, dqv[:, :256])
        dxb, dgb = _norm_rope_bwd(qb_ref[...], gq_ref[...], c, s, seg_m, dqv[:, 256:])
        dk = unfold(dkc_ref[...], dkp_ref[...], dkh_ref[...], i)
        dv = unfold(dvc_ref[...], dvp_ref[...], dvh_ref[...], i)
        kv = kv_ref[...]
        dxk, dgk = _norm_rope_bwd(kv[:, :KV_WIDTH], gk_ref[...], c, s, seg_m[:128, :128], dk)
        dxp, dwbd, dscale = _pool_bwd_tile(i, n, tr, xp_ref[...], xpp_ref[...], dpa_ref[...], dpan_ref[...],
                                           w_ref[...], sc_ref[...])
        dz_ref[...] = jnp.concatenate([dxp, dxa, dxb, dxk, dv], axis=1).astype(dz_ref.dtype)

        @pl.when(i == 0)
        def _():
            dgq_ref[...] = jnp.zeros_like(dgq_ref)
            dgk_ref[...] = jnp.zeros_like(dgk_ref)
            dw_ref[...] = jnp.zeros_like(dw_ref)
            dsc_ref[...] = jnp.zeros_like(dsc_ref)
        dgq_ref[...] += _fold_lanes(dga + dgb, HEAD_DIM)
        dgk_ref[...] += _fold_lanes(dgk, HEAD_DIM)
        dw_ref[...] += dwbd
        dsc_ref[...] += dscale

    col = lambda j: pl.BlockSpec((tr, 256), lambda i: (i, j))
    rows = lambda w: pl.BlockSpec((tr, w), lambda i: (i, 0))
    nxt_blk = pl.BlockSpec((ATTN_BLOCK, 256), lambda i: (jnp.minimum((i + 1) * ab, T // ATTN_BLOCK - 1), 0))
    prev16 = pl.BlockSpec((16, 256), lambda i: (jnp.maximum(i * hb - 1, 0), 0))
    next16 = pl.BlockSpec((16, 256), lambda i: (jnp.minimum((i + 1) * hb, T // 16 - 1), 0))
    return pl.pallas_call(
        _after(body, 22, deps), grid=(n,),
        in_specs=[col(0), prev16, col(1), col(2), col(3), rows(128), rows(128),
                  _full((1, 256)), _full((1, 128)), _full((256, 256)),
                  rows(512), rows(256), rows(256), nxt_blk, rows(256), rows(256), nxt_blk,
                  rows(256), next16, _full((256, 256)), _full((1, 256)), ANY] + [ANY] * len(deps),
        out_specs=[rows(1024), _full((1, 256)), _full((1, 128)), _full((256, 256)), _full((1, 256))],
        out_shape=[SDS((T, IN_COLS), MXU_DTYPE), SDS((1, 256), F32), SDS((1, 128), F32),
                   SDS((256, 256), F32), SDS((1, 256), F32)],
        input_output_aliases={21: 0}, name=name, compiler_params=_cp("arbitrary"))(
            z, z, z, z, z, cos, sin_signed, gq, gk, seg, dq, dkc, dkp, dkp, dvc, dvp, dvp, dpa, dpa, wbd, scale, dz,
            *deps)


def _sgu_common(zu, zv, vn, seg):
    u, du = _gelu_and_grad(zu)
    gv, dgv = _gelu_and_grad(zv)
    ms = _split_dot(gv * gv, seg) * (1.0 / HEAD_DIM)
    r = lax.rsqrt(ms + EPS)
    xh = gv * r
    return u, du, dgv, r, xh, xh * vn


def _sgu_fwd(z, wtril, bexp, vn, seg, *, tr, name):
    T = z.shape[0]
    nch = tr // CHUNK

    def body(u_ref, v_ref, w_ref, b_ref, vn_ref, seg_ref, o_ref):
        u, _, _, _, _, vg = _sgu_common(u_ref[...], v_ref[...], vn_ref[...], seg_ref[...])
        grp = _lane((CHUNK, SGU_WIDTH)) // HEAD_DIM
        outs = []
        for ch in range(nch):
            vc = vg[ch * CHUNK:(ch + 1) * CHUNK]
            s = b_ref[...]
            for g in range(4):
                s = s + jnp.where(grp == g, _dot(w_ref[g], vc), 0.0)
            outs.append(u[ch * CHUNK:(ch + 1) * CHUNK] * s)
        o_ref[...] = jnp.concatenate(outs, axis=0).astype(o_ref.dtype)

    col = lambda j: pl.BlockSpec((tr, 256), lambda i: (i, j))
    return pl.pallas_call(
        body, grid=(T // tr,),
        in_specs=[col(4), col(5), _full((4, CHUNK, CHUNK)), _full((CHUNK, 256)), _full((1, 256)), _full((256, 256))],
        out_specs=col(0), out_shape=SDS((T, SGU_WIDTH), MXU_DTYPE), name=name,
        compiler_params=_cp("parallel"))(z, z, wtril, bexp, vn, seg)


def _sgu_bwd(z, wtril, bexp, vn, seg, dsg, dz, *, tr, name):
    T = z.shape[0]
    nch = tr // CHUNK

    def body(u_ref, v_ref, w_ref, b_ref, vn_ref, seg_ref, d_ref, _dz_in, dz_ref, dw_ref, db_ref, dvn_ref):
        i = pl.program_id(0)
        seg_m = seg_ref[...]
        vn_v = vn_ref[...]
        u, du, dgv, r, xh, vg = _sgu_common(u_ref[...], v_ref[...], vn_v, seg_m)
        d = d_ref[...]
        grp = _lane((CHUNK, SGU_WIDTH)) // HEAD_DIM
        tril = _row((CHUNK, CHUNK)) >= _lane((CHUNK, CHUNK))

        @pl.when(i == 0)
        def _():
            dw_ref[...] = jnp.zeros_like(dw_ref)
            db_ref[...] = jnp.zeros_like(db_ref)
            dvn_ref[...] = jnp.zeros_like(dvn_ref)

        dus, dvgs = [], []
        for ch in range(nch):
            sl = slice(ch * CHUNK, (ch + 1) * CHUNK)
            vc = vg[sl]
            s = b_ref[...]
            for g in range(4):
                s = s + jnp.where(grp == g, _dot(w_ref[g], vc), 0.0)
            dus.append(d[sl] * s)
            ds = d[sl] * u[sl]
            db_ref[...] += _split_dot(ds, seg_m)
            dvg = jnp.zeros((CHUNK, SGU_WIDTH), F32)
            for g in range(4):
                dsm = jnp.where(grp == g, ds, 0.0)
                dvg = dvg + jnp.where(grp == g, _dot(w_ref[g], ds, TN), 0.0)
                dw_ref[g] += jnp.where(tril, _dot(dsm, vc, NT), 0.0)
            dvgs.append(dvg)
        dup = jnp.concatenate(dus, axis=0)
        dvg = jnp.concatenate(dvgs, axis=0)
        dvn_ref[...] += _fold_lanes(jnp.sum(dvg * xh, axis=0, keepdims=True), HEAD_DIM)
        gy = dvg * vn_v
        dgvv = r * (gy - xh * (_split_dot(xh * gy, seg_m) * (1.0 / HEAD_DIM)))
        dz_ref[...] = jnp.concatenate([dup * du, dgvv * dgv], axis=1).astype(dz_ref.dtype)

    col = lambda j: pl.BlockSpec((tr, 256), lambda i: (i, j))
    return pl.pallas_call(
        body, grid=(T // tr,),
        in_specs=[col(4), col(5), _full((4, CHUNK, CHUNK)), _full((CHUNK, 256)), _full((1, 256)), _full((256, 256)),
                  col(0), ANY],
        out_specs=[pl.BlockSpec((tr, 512), lambda i: (i, 2)), _full((4, CHUNK, CHUNK)), _full((CHUNK, 256)),
                   _full((1, 256))],
        out_shape=[SDS((T, IN_COLS), MXU_DTYPE), SDS((4, CHUNK, CHUNK), F32), SDS((CHUNK, 256), F32),
                   SDS((1, 256), F32)],
        input_output_aliases={7: 0}, name=name, compiler_params=_cp("arbitrary"))(
            z, z, wtril, bexp, vn, seg, dsg, dz)


def _merge_fwd(pa, at, sg, wa, wb, wc, z, *, tm, tn, name):
    T = pa.shape[0]
    gb = GATE_COL0 // tn
    nb = D_MODEL // tn

    def body(pa_ref, at_ref, sg_ref, wa_ref, wb_ref, wc_ref, g0_ref, g1_ref, g2_ref, m_ref, y_ref):
        acc = None
        for idx, (op_ref, w_ref, g_ref) in enumerate(((pa_ref, wa_ref, g0_ref), (at_ref, wb_ref, g1_ref),
                                                      (sg_ref, wc_ref, g2_ref))):
            y = _dot(op_ref[...], w_ref[...])
            y_ref[idx] = y.astype(y_ref.dtype)
            t = _sigmoid(g_ref[...]) * y
            acc = t if acc is None else acc + t
        m_ref[...] = acc.astype(m_ref.dtype)

    op = lambda w: pl.BlockSpec((tm, w), lambda i, j: (i, 0))
    wt = lambda k: pl.BlockSpec((k, tn), lambda i, j: (0, j))
    gate = lambda b: pl.BlockSpec((tm, tn), lambda i, j: (i, gb + b * nb + j))
    return pl.pallas_call(
        body, grid=(T // tm, nb),
        in_specs=[op(256), op(512), op(256), wt(256), wt(512), wt(256), gate(0), gate(1), gate(2)],
        out_specs=[pl.BlockSpec((tm, tn), lambda i, j: (i, j)), pl.BlockSpec((3, tm, tn), lambda i, j: (0, i, j))],
        out_shape=[SDS((T, D_MODEL), MXU_DTYPE), SDS((3, T, D_MODEL), MXU_DTYPE)],
        name=name, compiler_params=_cp("parallel", "parallel"))(pa, at, sg, wa, wb, wc, z, z, z)


def _out_dx_merge_bwd(dxb, w_out, y, z, *, tm, tn, name):
    T = dxb.shape[0]
    gb = GATE_COL0 // tn
    nb = D_MODEL // tn

    def body(dx_ref, w_ref, y_ref, g_ref, dy_ref, dz_ref, dm_ref):
        b, j = pl.program_id(1), pl.program_id(2)

        @pl.when((b == 0) & (j == 0))
        def _():
            dm = _dot(dx_ref[...], w_ref[...], NT)
            for jj in range(nb):
                dm_ref[jj] = dm[:, jj * tn:(jj + 1) * tn]

        d = dm_ref[j]
        g = _sigmoid(g_ref[...])
        dy_ref[...] = (d * g).astype(dy_ref.dtype)
        dz_ref[...] = (d * y_ref[...].astype(F32) * g * (1.0 - g)).astype(dz_ref.dtype)

    return pl.pallas_call(
        body, grid=(T // tm, 3, nb),
        in_specs=[pl.BlockSpec((tm, D_MODEL), lambda i, b, j: (i, 0)), _full((D_MODEL, D_MODEL)),
                  pl.BlockSpec((None, tm, tn), lambda i, b, j: (b, i, j)),
                  pl.BlockSpec((tm, tn), lambda i, b, j: (i, gb + b * nb + j))],
        out_specs=[pl.BlockSpec((None, tm, tn), lambda i, b, j: (b, i, j)),
                   pl.BlockSpec((tm, tn), lambda i, b, j: (i, gb + b * nb + j))],
        out_shape=[SDS((3, T, D_MODEL), MXU_DTYPE), SDS((T, IN_COLS), MXU_DTYPE)],
        scratch_shapes=[pltpu.VMEM((nb, tm, tn), F32)],
        name=name, compiler_params=_cp("parallel", "arbitrary", "arbitrary"))(dxb, w_out, y, z)


def _conv3(xe, w, b):
    return (w[0:1] * pltpu.roll(xe, 2, 0) + w[1:2] * pltpu.roll(xe, 1, 0) + w[2:3] * xe)[8:] + b


def _conv_act_fwd(up, cw, cb, *, tr, tc, name):
    T = up.shape[0]
    nc = D_FF // tc
    hb = tr // 8

    def body(ug_ref, ugp_ref, uv_ref, uvp_ref, wg_ref, wv_ref, bg_ref, bv_ref, o_ref):
        i = pl.program_id(1)
        first = i == 0
        cg = _conv3(jnp.concatenate([jnp.where(first, 0.0, ugp_ref[...]), ug_ref[...]], axis=0), wg_ref[...], bg_ref[...])
        cv = _conv3(jnp.concatenate([jnp.where(first, 0.0, uvp_ref[...]), uv_ref[...]], axis=0), wv_ref[...], bv_ref[...])
        o_ref[...] = (cg * _sigmoid(cg) * cv).astype(o_ref.dtype)

    tile = lambda off: pl.BlockSpec((tr, tc), lambda j, i: (i, off + j))
    prev = lambda off: pl.BlockSpec((8, tc), lambda j, i: (jnp.maximum(i * hb - 1, 0), off + j))
    par = lambda rows, off: pl.BlockSpec((rows, tc), lambda j, i: (0, off + j))
    return pl.pallas_call(
        body, grid=(nc, T // tr),
        in_specs=[tile(0), prev(0), tile(nc), prev(nc), par(3, 0), par(3, nc), par(1, 0), par(1, nc)],
        out_specs=pl.BlockSpec((tr, tc), lambda j, i: (i, j)),
        out_shape=SDS((T, D_FF), MXU_DTYPE), name=name,
        compiler_params=_cp("parallel", "parallel"))(up, up, up, up, cw, cw, cb, cb)


def _conv_act_bwd(up, cw, cb, dact, *, tr, tc, name, deps=()):
    T = up.shape[0]
    nc = D_FF // tc
    hb = tr // 8
    nr = T // tr

    def body(ug_ref, ugp_ref, ugn_ref, uv_ref, uvp_ref, uvn_ref, da_ref, dan_ref, wg_ref, wv_ref, bg_ref, bv_ref,
             du_ref, dwg_ref, dwv_ref, dbg_ref, dbv_ref):
        i = pl.program_id(1)
        first, last = i == 0, i == nr - 1
        da = jnp.concatenate([da_ref[...], jnp.where(last, 0.0, dan_ref[...])], axis=0)
        uge = jnp.concatenate([jnp.where(first, 0.0, ugp_ref[...]), ug_ref[...], ugn_ref[...]], axis=0)
        uve = jnp.concatenate([jnp.where(first, 0.0, uvp_ref[...]), uv_ref[...], uvn_ref[...]], axis=0)
        wg, wv = wg_ref[...], wv_ref[...]
        ug1, ug2 = pltpu.roll(uge, 1, 0)[8:], pltpu.roll(uge, 2, 0)[8:]
        uv1, uv2 = pltpu.roll(uve, 1, 0)[8:], pltpu.roll(uve, 2, 0)[8:]
        cg = wg[0:1] * ug2 + wg[1:2] * ug1 + wg[2:3] * uge[8:] + bg_ref[...]
        cv = wv[0:1] * uv2 + wv[1:2] * uv1 + wv[2:3] * uve[8:] + bv_ref[...]
        sg = _sigmoid(cg)
        dcg = da * cv * (sg * (1.0 + cg * (1.0 - sg)))
        dcv = da * (cg * sg)
        nrow = tr + 8

        def back(dc, w):
            return (w[2:3] * dc + w[1:2] * pltpu.roll(dc, nrow - 1, 0) + w[0:1] * pltpu.roll(dc, nrow - 2, 0))[:tr]

        du_ref[0] = back(dcg, wg).astype(du_ref.dtype)
        du_ref[1] = back(dcv, wv).astype(du_ref.dtype)

        def wgrad(dc, u0, u1, u2):
            d = dc[:tr]
            rows = [jnp.sum(d * u2[:tr], axis=0, keepdims=True), jnp.sum(d * u1[:tr], axis=0, keepdims=True),
                    jnp.sum(d * u0[8:8 + tr], axis=0, keepdims=True)]
            return jnp.concatenate(rows, axis=0), jnp.sum(d, axis=0, keepdims=True)

        dwg, dbg = wgrad(dcg, uge, ug1, ug2)
        dwv, dbv = wgrad(dcv, uve, uv1, uv2)

        @pl.when(first)
        def _():
            dwg_ref[...] = jnp.zeros_like(dwg_ref)
            dwv_ref[...] = jnp.zeros_like(dwv_ref)
            dbg_ref[...] = jnp.zeros_like(dbg_ref)
            dbv_ref[...] = jnp.zeros_like(dbv_ref)
        dwg_ref[...] += dwg
        dwv_ref[...] += dwv
        dbg_ref[...] += dbg
        dbv_ref[...] += dbv

    tile = lambda off: pl.BlockSpec((tr, tc), lambda j, i: (i, off + j))
    prev = lambda off: pl.BlockSpec((8, tc), lambda j, i: (jnp.maximum(i * hb - 1, 0), off + j))
    nxt = lambda off: pl.BlockSpec((8, tc), lambda j, i: (jnp.minimum((i + 1) * hb, T // 8 - 1), off + j))
    par = lambda rows, off: pl.BlockSpec((rows, tc), lambda j, i: (0, off + j))
    acc = lambda rows: pl.BlockSpec((rows, tc), lambda j, i: (0, j))
    return pl.pallas_call(
        _after(body, 12, deps), grid=(nc, nr),
        in_specs=[tile(0), prev(0), nxt(0), tile(nc), prev(nc), nxt(nc), tile(0), nxt(0),
                  par(3, 0), par(3, nc), par(1, 0), par(1, nc)] + [ANY] * len(deps),
        out_specs=[pl.BlockSpec((2, tr, tc), lambda j, i: (0, i, j)), acc(3), acc(3), acc(1), acc(1)],
        out_shape=[SDS((2, T, D_FF), MXU_DTYPE), SDS((3, D_FF), F32), SDS((3, D_FF), F32),
                   SDS((1, D_FF), F32), SDS((1, D_FF), F32)],
        name=name, compiler_params=_cp("parallel", "arbitrary"))(
            up, up, up, up, up, up, dact, dact, cw, cw, cb, cb, *deps)


def _row_tile(rows, cap):
    t = min(cap, rows)
    t -= t % 8
    while rows % t:
        t -= 8
    return t


def _adamw(w, g, m, v, *, tr, name):
    R, C = w.shape
    assert R % tr == 0, (R, tr)

    def body(w_ref, g_ref, m_ref, v_ref, d_ref, nm_ref, nv_ref):
        gv = g_ref[...]
        mn = ADAM_B1 * m_ref[...] + (1.0 - ADAM_B1) * gv
        vn = ADAM_B2 * v_ref[...] + (1.0 - ADAM_B2) * (gv * gv)
        m_hat = mn / (1.0 - ADAM_B1 ** ADAM_STEP)
        v_hat = vn / (1.0 - ADAM_B2 ** ADAM_STEP)
        d_ref[...] = -ADAM_LR * (m_hat / (jnp.sqrt(v_hat) + ADAM_EPS) + ADAM_WD * w_ref[...])
        nm_ref[...] = mn
        nv_ref[...] = vn

    rows = pl.BlockSpec((tr, C), lambda i: (i, 0))
    return pl.pallas_call(
        body, grid=(R // tr,), in_specs=[rows] * 4, out_specs=[rows] * 3,
        out_shape=[SDS((R, C), F32)] * 3, name=name, compiler_params=_cp("parallel"))(w, g, m, v)


def _sum_slots(r, *, tr, name):
    S, R, C = r.shape
    assert R % tr == 0, (R, tr)

    def body(r_ref, o_ref):
        acc = r_ref[0]
        for s in range(1, S):
            acc = acc + r_ref[s]
        o_ref[...] = acc

    return pl.pallas_call(
        body, grid=(R // tr,), in_specs=[pl.BlockSpec((S, tr, C), lambda i: (0, i, 0))],
        out_specs=pl.BlockSpec((tr, C), lambda i: (i, 0)), out_shape=SDS((R, C), F32),
        name=name, compiler_params=_cp("parallel"))(r)


def _pair_add(g4, h, pos, *, name):
    A, _, r, C = g4.shape
    cs = C if A == N_CHIPS else C // N_CHIPS
    tr = _row_tile(r, 256)
    if A == N_CHIPS:
        g_map, h_map = (lambda t, i, pos: (t, pos[1], i, 0)), (lambda t, i, pos: (t, i, 0))
    else:
        g_map, h_map = (lambda t, i, pos: (0, pos[1], i, t)), (lambda t, i, pos: (0, i, t))

    def body(pos_ref, g_ref, h_ref, o_ref):
        o_ref[...] = (g_ref[...] + h_ref[...]).astype(o_ref.dtype)

    grid_spec = pltpu.PrefetchScalarGridSpec(
        num_scalar_prefetch=1, grid=(N_CHIPS, r // tr),
        in_specs=[pl.BlockSpec((None, None, tr, cs), g_map), pl.BlockSpec((None, tr, cs), h_map)],
        out_specs=pl.BlockSpec((None, tr, cs), lambda t, i, pos: (t, i, 0)))
    return pl.pallas_call(body, grid_spec=grid_spec, out_shape=SDS((N_CHIPS, r, cs), COMM_DTYPE), name=name,
                          compiler_params=_cp("parallel", "parallel"))(pos, g4, h)


def _chip_sum(p, r2, f_into, pos, layer, *, name):
    _, r, cs = p.shape
    tr = _row_tile(r, 256)

    def body(pos_ref, own_ref, r_ref, *rest):
        o_ref = rest[-1]
        o_ref[...] = ((own_ref[...].astype(F32) + r_ref[0].astype(F32)) + r_ref[1].astype(F32)) + r_ref[2].astype(F32)

    in_specs = [pl.BlockSpec((None, tr, cs), lambda i, pos: (pos[0], i, 0)),
                pl.BlockSpec((3, tr, cs), lambda i, pos: (0, i, 0))]
    operands = [pos, p, r2]
    aliases = {}
    if f_into is not None:
        in_specs.append(ANY)
        operands.append(f_into)
        aliases = {3: 0}
    grid_spec = pltpu.PrefetchScalarGridSpec(
        num_scalar_prefetch=1, grid=(r // tr,), in_specs=in_specs,
        out_specs=pl.BlockSpec((None, None, tr, cs), lambda i, pos: (layer, pos[1], i, 0)))
    return pl.pallas_call(body, grid_spec=grid_spec, out_shape=SDS((DEPTH, 2, r, cs), F32), name=name,
                          input_output_aliases=aliases, compiler_params=_cp("parallel"))(*operands)


def _mesh_pos():
    return lax.axis_index("x"), lax.axis_index("y"), lax.axis_index("c")


HBM = pl.BlockSpec(memory_space=pltpu.HBM)
SEM = pl.BlockSpec(memory_space=pltpu.SEMAPHORE)
DATAFLOW = pltpu.SideEffectType.DATAFLOW_SIDE_EFFECTING
CHIP_FLIPS = (2, 1, 3)


def _chip_peers():
    x, y, c = _mesh_pos()
    return 2 * x + y, [(1 - x, y, c), (x, 1 - y, c), (1 - x, 1 - y, c)], (x, y, 1 - c), c


def _split_start(arrays, n_copies, issue, *, name, deps=()):
    k = len(arrays)
    nd = len(deps)

    def body(*refs):
        issue(refs[:k], refs[k + nd], refs[k + nd + 1])
        refs[2 * k + nd + 2][...] = jnp.zeros((8, 128), F32)

    out = pl.pallas_call(
        body, name=name,
        out_shape=(pltpu.SemaphoreType.DMA((n_copies,)), pltpu.SemaphoreType.DMA((n_copies,)),
                   *[pltpu.HBM(a.shape, a.dtype) for a in arrays], SDS((8, 128), F32)),
        in_specs=[HBM] * k + [ANY] * nd, out_specs=(SEM, SEM, *[HBM] * k, pl.BlockSpec(memory_space=pltpu.VMEM)),
        input_output_aliases={i: 2 + i for i in range(k)},
        compiler_params=pltpu.CompilerParams(has_side_effects=DATAFLOW))(
            *[pltpu.with_memory_space_constraint(a, pltpu.HBM) for a in arrays], *deps)
    return (out[0], out[1]), list(out[2:2 + k]), out[2 + k]


def _split_wait(sems, arrays, after, waits, *, name):
    k = len(arrays)
    afters = tuple(after) if isinstance(after, (tuple, list)) else (after,)

    def body(*refs):
        waits(refs[:k], refs[k], refs[k + 1])

    out = pl.pallas_call(
        body, name=name, out_shape=tuple(pltpu.HBM(a.shape, a.dtype) for a in arrays),
        in_specs=[HBM] * k + [SEM, SEM] + [ANY] * len(afters), out_specs=tuple([HBM] * k),
        input_output_aliases={i: i for i in range(k)},
        compiler_params=pltpu.CompilerParams(has_side_effects=DATAFLOW))(*arrays, sems[0], sems[1], *afters)
    return list(out)


def _wait_both(cp):
    cp.wait_send()
    cp.wait_recv()


def _cast_place(shard, pos, dtype, *, name, layer=None, slots=N_CHIPS, which=0):
    R, C = shard.shape[-2:]
    tr = R if R % 8 else _row_tile(R, 256)
    if layer is None:
        in_spec = pl.BlockSpec((tr, C), lambda i, pos: (i, 0))
    else:
        in_spec = pl.BlockSpec((None, tr, C), lambda i, pos: (layer, i, 0))

    def body(pos_ref, x_ref, o_ref):
        o_ref[...] = x_ref[...].astype(o_ref.dtype)

    grid_spec = pltpu.PrefetchScalarGridSpec(
        num_scalar_prefetch=1, grid=(R // tr,), in_specs=[in_spec],
        out_specs=pl.BlockSpec((None, tr, C), lambda i, pos: (pos[which], i, 0)))
    return pl.pallas_call(body, grid_spec=grid_spec, out_shape=SDS((slots, R, C), dtype), name=name,
                          compiler_params=_cp("parallel"))(pos, shard)


def _device_peers():
    x, y, c = _mesh_pos()
    peers = [(x ^ ((f >> 2) & 1), y ^ ((f >> 1) & 1), c ^ (f & 1)) for f in range(1, N_DEV)]
    return 4 * x + 2 * y + c, peers


class _Gather:
    def __init__(self, lands, name, deps=(), all_devices=False):
        n = len(lands)
        self.name = name
        npeer = N_DEV - 1 if all_devices else N_CHIPS - 1

        def copies(refs, ss, rs):
            me, peers = _device_peers() if all_devices else _chip_peers()[:2]
            return [pltpu.make_async_remote_copy(
                src_ref=refs[w].at[me], dst_ref=refs[w].at[me], send_sem=ss.at[npeer * w + p],
                recv_sem=rs.at[npeer * w + p], device_id=peers[p], device_id_type=MESH)
                for w in range(n) for p in range(npeer)]

        def issue(refs, ss, rs):
            for cp in copies(refs, ss, rs):
                cp.start()

        def waits(refs, ss, rs):
            for cp in copies(refs, ss, rs):
                _wait_both(cp)

        self._waits = waits
        self.sems, self.arrays, self.token = _split_start(list(lands), npeer * n, issue, name=name + "_start",
                                                          deps=deps)

    def wait(self, after):
        return _split_wait(self.sems, self.arrays, after, self._waits, name=self.name + "_wait")


def _swap_halves_start(g4s, *, name):
    n = len(g4s)
    lands = [lax.empty((g.shape[0],) + g.shape[2:], g.dtype) for g in g4s]

    def copies(refs, ss, rs):
        _, _, sibling, c = _chip_peers()
        return [pltpu.make_async_remote_copy(
            src_ref=refs[w].at[:, 1 - c], dst_ref=refs[n + w], send_sem=ss.at[w], recv_sem=rs.at[w],
            device_id=sibling, device_id_type=MESH) for w in range(n)]

    def issue(refs, ss, rs):
        for cp in copies(refs, ss, rs):
            cp.start()

    def waits(refs, ss, rs):
        for cp in copies(refs, ss, rs):
            _wait_both(cp)

    sems, arrays, token = _split_start(list(g4s) + lands, n, issue, name=name + "_start")
    return sems, arrays, token, waits


def _scatter_start(parts, *, name, deps=()):
    n = len(parts)
    lands = [lax.empty((3,) + p.shape[1:], p.dtype) for p in parts]

    def copies(refs, ss, rs):
        me, peers, _, _ = _chip_peers()
        return [pltpu.make_async_remote_copy(
            src_ref=refs[w].at[me ^ CHIP_FLIPS[p]], dst_ref=refs[n + w].at[p],
            send_sem=ss.at[3 * w + p], recv_sem=rs.at[3 * w + p], device_id=peers[p], device_id_type=MESH)
            for w in range(n) for p in range(3)]

    def issue(refs, ss, rs):
        for cp in copies(refs, ss, rs):
            cp.start()

    def waits(refs, ss, rs):
        for cp in copies(refs, ss, rs):
            _wait_both(cp)

    sems, arrays, token = _split_start(list(parts) + lands, 3 * n, issue, name=name + "_start", deps=deps)
    return sems, arrays, token, waits


def _pair_share_start(fs, layer, *, name):
    n = len(fs)

    def copies(refs, ss, rs):
        _, _, sibling, c = _chip_peers()
        return [pltpu.make_async_remote_copy(
            src_ref=refs[w].at[layer, c], dst_ref=refs[w].at[layer, c], send_sem=ss.at[w], recv_sem=rs.at[w],
            device_id=sibling, device_id_type=MESH) for w in range(n)]

    def issue(refs, ss, rs):
        for cp in copies(refs, ss, rs):
            cp.start()

    def waits(refs, ss, rs):
        for cp in copies(refs, ss, rs):
            _wait_both(cp)

    sems, arrays, token = _split_start(list(fs), n, issue, name=name + "_start")
    return sems, arrays, token, waits


BIG = ('w_in', 'w_proj_a', 'w_proj_b', 'w_proj_c', 'w_out', 'w_up', 'w_down')
BIG_SHARD_AXIS = {'w_in': 2, 'w_proj_a': 2, 'w_proj_b': 2, 'w_proj_c': 2, 'w_out': 1, 'w_up': 2, 'w_down': 1}
SMALL = ('norm1', 'q_norm', 'k_norm', 'sinks', 'w_pool', 'pool_scale', 'sgu_v_norm', 'w_s', 'b_s', 'norm2',
         'conv_b', 'conv_w')
WEIGHTS = ('norm1', 'w_in', 'q_norm', 'k_norm', 'sinks', 'w_pool', 'pool_scale', 'sgu_v_norm', 'w_s', 'b_s',
           'w_proj_a', 'w_proj_b', 'w_proj_c', 'w_out', 'norm2', 'w_up', 'conv_w', 'conv_b', 'w_down')


def _rope_tables(positions):
    inv_freq = ROPE_THETA ** (-jnp.arange(0, HEAD_DIM, 2, dtype=F32) / HEAD_DIM)
    ang = positions.astype(F32)[:, None] * inv_freq
    cos, sin = jnp.cos(ang), jnp.sin(ang)
    c = jnp.concatenate([cos, cos], axis=1)
    s = jnp.concatenate([-sin, sin], axis=1)
    return jnp.concatenate([c, c], axis=1), jnp.concatenate([s, s], axis=1)


def _block_diag4(w):
    out = jnp.zeros((POOL_WIDTH, POOL_WIDTH), w.dtype)
    for g in range(4):
        out = lax.dynamic_update_slice(out, w[g], (g * HEAD_DIM, g * HEAD_DIM))
    return out


def _local_step(x, target, cos, sin, sp, sched):
    T = x.shape[0]
    tm1 = min(1024, T)
    tm = min(512, T)
    tr = min(256, T)
    tkt = min(1024, T)
    seg = _seg_matrix(256, HEAD_DIM)
    saved = []
    xl = x
    for l in range(DEPTH):
        p = f"l{l}_"
        c = dict(
            g1=sp['norm1'][l][None], g2=sp['norm2'][l][None],
            wbd=_block_diag4(sp['w_pool'][l]).astype(MXU_DTYPE), scale=sp['pool_scale'][l][None],
            gq=jnp.tile(sp['q_norm'][l], 4)[None], gk=jnp.tile(sp['k_norm'][l], 2)[None],
            sinks=jnp.broadcast_to(sp['sinks'][l][:, None], (N_Q_HEADS, 128)),
            wtril=jnp.tril(sp['w_s'][l]).astype(MXU_DTYPE),
            bexp=jnp.repeat(sp['b_s'][l].T, HEAD_DIM, axis=1), vn=jnp.tile(sp['sgu_v_norm'][l], 4)[None],
            cb=sp['conv_b'][l][None])
        c['w_in'] = sched.weight('w_in', l, xl)
        z, h1 = _norm_mm(xl, c['g1'], c['w_in'], tm=tm1, tn=1152, name=p + "in_proj",
                         deps=sched.start_tokens() if l == 0 else ())
        pa = _pool_fwd(z, c['wbd'], c['scale'], tr=tr, name=p + "pool")
        q, k, v = _qkv_prep(z, cos, sin, c['gq'], c['gk'], seg, tr=tr, name=p + "qkv_prep")
        at = _attn_fwd(q, k, v, c['sinks'], name=p + "attn")
        sg = _sgu_fwd(z, c['wtril'], c['bexp'], c['vn'], seg, tr=tr, name=p + "sgu")
        for n in ('w_proj_a', 'w_proj_b', 'w_proj_c', 'w_out'):
            c[n] = sched.weight(n, l, (pa, at, sg))
        merged, y3 = _merge_fwd(pa, at, sg, c['w_proj_a'], c['w_proj_b'], c['w_proj_c'], z,
                                tm=tm, tn=512, name=p + "merge")
        x1 = _mm(merged, c['w_out'], mode='nn', add=xl, tm=tm, tn=D_MODEL, tk=D_MODEL, name=p + "out_proj")
        for n in ('w_up', 'conv_w', 'w_down'):
            c[n] = sched.weight(n, l, x1)
        up, h2 = _norm_mm(x1, c['g2'], c['w_up'], tm=tm1, tn=1408, name=p + "up_proj")
        act = _conv_act_fwd(up, c['conv_w'], c['cb'], tr=tr, tc=1408, name=p + "conv_act")
        saved.append(dict(c, x=xl, h1=h1, z=z, pa=pa, q=q, k=k, v=v, at=at, sg=sg, merged=merged, y3=y3,
                          x1=x1, h2=h2, up=up, act=act))
        if l < DEPTH - 1:
            xl = _mm(act, c['w_down'], mode='nn', add=x1, tm=tm, tn=D_MODEL, tk=D_FF, name=p + "down_proj")
        else:
            loss_row, dx, dxb = _down_proj_loss(act, c['w_down'], x1, target, tm=tm, name=p + "down_proj_loss")

    gs = {n: [None] * DEPTH for n in SMALL}
    for l in reversed(range(DEPTH)):
        p = f"l{l}_b_"
        s = saved[l]
        gb = {}
        dact = _mm(dxb, s['w_down'], mode='nt', tm=tm1, tn=1408, tk=D_MODEL, name=p + "down_dx")
        gb['w_down'] = _mm(s['act'], dxb, mode='tn', tm=1408, tn=D_MODEL, tk=tkt, name=p + "down_dw")
        toks = sched.slot(l, 'down', gb['w_down'])
        dup, dwg, dwv, dbg, dbv = _conv_act_bwd(s['up'], s['conv_w'], s['cb'], dact, tr=min(512, T), tc=256,
                                                name=p + "conv_act", deps=toks)
        gs['conv_w'][l] = jnp.concatenate([dwg, dwv], axis=1)
        gs['conv_b'][l] = jnp.concatenate([dbg, dbv], axis=1)[0]
        toks = sched.slot(l, 'conv', dup)
        for half in range(2):
            gb['w_up'] = _mm(s['h2'], dup, mode='tn', b_lead=half, tm=D_MODEL, tn=1408, tk=tkt,
                             out_into=gb.get('w_up'), out_joff=2 * half, out_n=2 * D_FF, name=p + f"up_dw{half}",
                             deps=toks if half == 0 else ())
        toks = sched.slot(l, 'ffn', gb['w_up'], gb)
        dx1, dx1b, dg2 = _mm_nt_sharded_rms(dup, s['w_up'], s['x1'], s['g2'], dx, tm=min(256, T),
                                            name=p + "up_dx_rms2", deps=toks)
        gs['norm2'][l] = dg2[0]
        gb['w_out'] = _mm(s['merged'], dx1b, mode='tn', tm=D_MODEL, tn=D_MODEL, tk=tkt, name=p + "out_dw")
        dy3, dz = _out_dx_merge_bwd(dx1b, s['w_out'], s['y3'], s['z'], tm=min(1024, T), tn=512,
                                    name=p + "out_dx_merge")
        toks = sched.slot(l, 'mid', dz)
        dbr = []
        for idx, (wn, opn, width) in enumerate((('w_proj_a', 'pa', POOL_WIDTH), ('w_proj_b', 'at', ATTN_WIDTH),
                                                ('w_proj_c', 'sg', SGU_WIDTH))):
            dbr.append(_mm(dy3, s[wn], mode='nt', a_lead=idx, tm=tm, tn=width, tk=D_MODEL,
                           name=p + f"proj{idx}_dx", deps=toks if idx == 0 else ()))
            gb[wn] = _mm(s[opn], dy3, mode='tn', b_lead=idx, tm=width, tn=D_MODEL, tk=tkt,
                         name=p + f"proj{idx}_dw")
        dpa, dat, dsg = dbr
        dq, dkc, dkp, dvc, dvp, dsk = _attn_bwd(s['q'], s['k'], s['v'], s['sinks'], dat, name=p + "attn")
        gs['sinks'][l] = dsk[:, 0]
        toks = sched.slot(l, 'attn', dq)
        dz, dgq, dgk, dwbd, dsc = _mixer_ab_bwd(s['z'], cos, sin, s['gq'], s['gk'], seg, dq, dkc, dkp, dvc, dvp,
                                                dpa, s['wbd'], s['scale'], dz, tr=tr, name=p + "qkv_pool", deps=toks)
        gs['q_norm'][l] = dgq[0, :HEAD_DIM]
        gs['k_norm'][l] = dgk[0, :HEAD_DIM]
        gs['w_pool'][l] = jnp.stack([dwbd[g * HEAD_DIM:(g + 1) * HEAD_DIM, g * HEAD_DIM:(g + 1) * HEAD_DIM]
                                     for g in range(4)])
        gs['pool_scale'][l] = dsc[0]
        dz, dws, dbrows, dvn = _sgu_bwd(s['z'], s['wtril'], s['bexp'], s['vn'], seg, dsg, dz, tr=tr, name=p + "sgu")
        gs['w_s'][l] = dws
        gs['b_s'][l] = dbrows[:, ::HEAD_DIM].T
        gs['sgu_v_norm'][l] = dvn[0, :HEAD_DIM]
        gb['w_in'] = _mm(s['h1'], dz, mode='tn', tm=D_MODEL, tn=1152, tk=tkt, name=p + "in_dw")
        toks = sched.slot(l, 'mix', gb['w_in'], gb)
        dx, dxb, dg1 = _mm_nt_sharded_rms(dz, s['w_in'], s['x'], s['g1'], dx1, tm=min(256, T),
                                          name=p + "in_dx_rms1", deps=toks)
        gs['norm1'][l] = dg1[0]
    gs = {n: jnp.stack(v) for n, v in gs.items()}
    return loss_row, dx, gs


GROUP_F = ('w_down', 'w_up')
GROUP_M = ('w_out', 'w_proj_a', 'w_proj_b', 'w_proj_c', 'w_in')
ROW_SHARDED = ('w_out', 'w_down')

REDUCE_PLAN = {
    (1, 'ffn'): (('S1', 'F', 1),),
    (1, 'mid'): (('W1', 'F', 1),),
    (1, 'mix'): (('S1', 'M', 1),),
    (0, 'down'): (('W1', 'M', 1),),
    (0, 'conv'): (('W2', 'F', 1),),
    (0, 'ffn'): (('S1', 'F', 0), ('W3', 'F', 1)),
    (0, 'mid'): (('W1', 'F', 0),),
    (0, 'attn'): (('W2', 'M', 1),),
    (0, 'mix'): (('S1', 'M', 0), ('W3', 'M', 1)),
}
REDUCE_TAIL_A = (('W1', 'M', 0), ('W2', 'F', 0))
REDUCE_TAIL_B = (('W3', 'F', 0),)
REDUCE_TAIL_C = (('W2', 'M', 0), ('W3', 'M', 0))


class _Comm:
    def __init__(self, w, pos):
        self.pos = pos
        groups = {'a': [('w_in', 0)],
                  'b': [(n, 0) for n in ('w_proj_a', 'w_proj_b', 'w_proj_c', 'w_out')],
                  'c': [(n, 0) for n in ('w_up', 'conv_w', 'w_down')],
                  'd': [(n, 1) for n in BIG] + [('conv_w', 1)]}
        self.gathers, self.group_of, self.weights = {}, {}, {}
        self.tokens = []
        for g, ks in groups.items():
            lands = [_cast_place(w[n], pos, F32 if n == 'conv_w' else MXU_DTYPE, layer=l, name=f"gw_place_{n}{l}")
                     for n, l in ks]
            self.gathers[g] = (_Gather(lands, "gw_" + g, deps=self.tokens[-1:]), ks)
            self.tokens.append(self.gathers[g][0].token)
            self.group_of.update({k: g for k in ks})
        self.red = {}
        self.final = {}

    def start_tokens(self):
        return self.tokens[-1:]

    def weight(self, name, layer, after):
        if (name, layer) not in self.weights:
            gather, ks = self.gathers[self.group_of[(name, layer)]]
            for (n, l), full in zip(ks, gather.wait(after)):
                if n == 'conv_w' or n.startswith('w_proj'):
                    full = full.transpose(1, 0, 2).reshape(full.shape[1], -1)
                elif n in ROW_SHARDED:
                    full = full.reshape(-1, full.shape[2])
                self.weights[(n, l)] = full
        return self.weights[(name, layer)]

    def slot(self, layer, slot, after, grads=None):
        tokens = []
        for step, grp, lyr in REDUCE_PLAN.get((layer, slot), ()):
            tok = self._step(step, grp, lyr, after, grads)
            if tok is not None:
                tokens.append(tok)
        return tokens

    def tail(self, steps, after, deps=()):
        toks = (self._step(step, grp, lyr, after, None, deps) for step, grp, lyr in steps)
        return [t for t in toks if t is not None]

    def shards(self):
        return {n: f.reshape(DEPTH, 2 * f.shape[2], f.shape[3]) for n, f in self.final.items()}

    def _step(self, step, grp, layer, after, grads, deps=()):
        names = GROUP_F if grp == 'F' else GROUP_M
        tag = f"{grp.lower()}{layer}"
        st = self.red.setdefault((grp, layer), {})
        n = len(names)
        if step == 'S1':
            g4s = []
            for nm in names:
                g = grads[nm]
                R, C = g.shape
                g4s.append(g.reshape(N_CHIPS, 2, R // (2 * N_CHIPS), C) if nm in ROW_SHARDED
                           else g.reshape(1, 2, R // 2, C))
            st['s1'] = _swap_halves_start(g4s, name="rs1_" + tag)
            return st['s1'][2]
        if step == 'W1':
            sems, arrays, _, waits = st.pop('s1')
            arrays = _split_wait(sems, arrays, after, waits, name=f"rs1_{tag}_wait")
            parts = [_pair_add(arrays[i], arrays[n + i], self.pos, name=f"pair_add_{tag}_{names[i]}")
                     for i in range(n)]
            st['s2'] = _scatter_start(parts, name="rs2_" + tag, deps=deps)
            return st['s2'][2]
        if step == 'W2':
            sems, arrays, _, waits = st.pop('s2')
            arrays = _split_wait(sems, arrays, after, waits, name=f"rs2_{tag}_wait")
            fs = [_chip_sum(arrays[i], arrays[n + i], self.final.get(names[i]), self.pos, layer,
                            name=f"chip_sum_{tag}_{names[i]}") for i in range(n)]
            st['s3'] = _pair_share_start(fs, layer, name="rs3_" + tag)
            return st['s3'][2]
        sems, arrays, _, waits = st.pop('s3')
        self.final.update(zip(names, _split_wait(sems, arrays, after, waits, name=f"rs3_{tag}_wait")))
        return None


def _pack(arrays):
    flat = []
    for a in arrays:
        f = a.reshape(-1).astype(F32)
        flat.append(jnp.pad(f, (0, (-f.shape[0]) % 128)))
    v = jnp.concatenate(flat)
    v = jnp.pad(v, (0, (-v.shape[0]) % 1024))
    return v.reshape(-1, 128)


def _unpack(pack, shapes):
    v = pack.reshape(-1)
    out, off = [], 0
    for shp in shapes:
        nel = int(np.prod(shp))
        out.append(v[off:off + nel].reshape(shp))
        off += nel + (-nel) % 128
    return out


def kernel(x, positions, norm1, w_in, q_norm, k_norm, sinks, w_pool, pool_scale, sgu_v_norm, w_s, b_s, w_proj_a, w_proj_b, w_proj_c, w_out, norm2, w_up, conv_w, conv_b, w_down, loss_target, m_norm1, m_w_in, m_q_norm, m_k_norm, m_sinks, m_w_pool, m_pool_scale, m_sgu_v_norm, m_w_s, m_b_s, m_w_proj_a, m_w_proj_b, m_w_proj_c, m_w_out, m_norm2, m_w_up, m_conv_w, m_conv_b, m_w_down, v_norm1, v_w_in, v_q_norm, v_k_norm, v_sinks, v_w_pool, v_pool_scale, v_sgu_v_norm, v_w_s, v_b_s, v_w_proj_a, v_w_proj_b, v_w_proj_c, v_w_out, v_norm2, v_w_up, v_conv_w, v_conv_b, v_w_down):
    w = dict(norm1=norm1, w_in=w_in, q_norm=q_norm, k_norm=k_norm, sinks=sinks, w_pool=w_pool, pool_scale=pool_scale,
             sgu_v_norm=sgu_v_norm, w_s=w_s, b_s=b_s, w_proj_a=w_proj_a, w_proj_b=w_proj_b, w_proj_c=w_proj_c,
             w_out=w_out, norm2=norm2, w_up=w_up, conv_w=conv_w, conv_b=conv_b, w_down=w_down)
    m = dict(norm1=m_norm1, w_in=m_w_in, q_norm=m_q_norm, k_norm=m_k_norm, sinks=m_sinks, w_pool=m_w_pool,
             pool_scale=m_pool_scale, sgu_v_norm=m_sgu_v_norm, w_s=m_w_s, b_s=m_b_s, w_proj_a=m_w_proj_a,
             w_proj_b=m_w_proj_b, w_proj_c=m_w_proj_c, w_out=m_w_out, norm2=m_norm2, w_up=m_w_up, conv_w=m_conv_w,
             conv_b=m_conv_b, w_down=m_w_down)
    v = dict(norm1=v_norm1, w_in=v_w_in, q_norm=v_q_norm, k_norm=v_k_norm, sinks=v_sinks, w_pool=v_w_pool,
             pool_scale=v_pool_scale, sgu_v_norm=v_sgu_v_norm, w_s=v_w_s, b_s=v_b_s, w_proj_a=v_w_proj_a,
             w_proj_b=v_w_proj_b, w_proj_c=v_w_proj_c, w_out=v_w_out, norm2=v_norm2, w_up=v_w_up, conv_w=v_conv_w,
             conv_b=v_conv_b, w_down=v_w_down)
    chip = 2 * lax.axis_index("x") + lax.axis_index("y")
    core = lax.axis_index("c")

    pos = jnp.stack([chip, core, 2 * chip + core]).astype(jnp.int32)
    comm = _Comm(w, pos)

    cos, sin = _rope_tables(positions[0])
    sp = {n: w[n] for n in SMALL if n != 'conv_w'}
    loss_row, dx, gs = _local_step(x[0], loss_target[0], cos, sin, sp, comm)
    loss = lax.psum(loss_row[0, 0], ("x", "y", "c"))

    delta, new_m, new_v = {}, {}, {}

    def adamw_big(names, grads):
        for n in names:
            shp = w[n].shape
            two_d = lambda a: a.reshape(shp[0] * shp[1], shp[2])
            d, nm, nv = _adamw(two_d(w[n]), two_d(grads[n]), two_d(m[n]), two_d(v[n]),
                               tr=_row_tile(shp[0] * shp[1], 256), name=f"adamw_{n}")
            delta[n], new_m[n], new_v[n] = d.reshape(shp), nm.reshape(shp), nv.reshape(shp)

    small_shapes = [gs[n].shape for n in SMALL]
    small_pack = _pack([gs[n] for n in SMALL])
    small = _Gather([_cast_place(small_pack, pos, F32, slots=N_DEV, which=2, name="small_place")], "small_gather",
                    all_devices=True)
    toks = comm.tail(REDUCE_TAIL_A[:1], (dx, small.token))
    comm.tail(REDUCE_TAIL_A[1:], (dx, *toks))
    comm.tail(REDUCE_TAIL_B, dx)
    adamw_big(GROUP_F, comm.shards())
    red = _sum_slots(small.wait(new_v[GROUP_F[-1]])[0], tr=small_pack.shape[0], name="small_sum")
    g_small = dict(zip(SMALL, _unpack(red, small_shapes)))
    comm.tail(REDUCE_TAIL_C, red)
    grads = comm.shards()
    grads.update(g_small)
    shard_cols = conv_w.shape[2]
    grads['conv_w'] = lax.dynamic_slice_in_dim(g_small['conv_w'], chip * shard_cols, shard_cols, axis=2)

    adamw_big(GROUP_M, grads)
    shapes = [w[n].shape for n in SMALL]
    packs = [_pack([src[n] for n in SMALL]) for src in (w, grads, m, v)]
    d, nm, nv = _adamw(*packs, tr=packs[0].shape[0], name="adamw_small")
    for dst, src in ((delta, d), (new_m, nm), (new_v, nv)):
        dst.update(zip(SMALL, _unpack(src, shapes)))

    return (loss, dx[None], *[grads[n] for n in WEIGHTS], *[delta[n] for n in WEIGHTS],
            *[new_m[n] for n in WEIGHTS], *[new_v[n] for n in WEIGHTS])
```

```python
import functools
import math

import numpy as np
import jax
import jax.numpy as jnp
from jax import lax
from jax.experimental import pallas as pl
from jax.experimental.pallas import tpu as pltpu

F32 = jnp.float32
MXU_DTYPE = jnp.bfloat16
COMM_DTYPE = jnp.bfloat16
CONV_SAVE_DTYPE = jnp.bfloat16

D_MODEL = 1024
DEPTH = 2
HEAD_DIM = 64
POOL_WINDOWS = (2, 4, 8, 16)
POOL_WIDTH = 256
N_Q_HEADS = 8
ATTN_BLOCK = 128
ATTN_WIDTH = 512
KV_WIDTH = 128
CHUNK = 128
SGU_WIDTH = 256
IN_COLS = 4608
GATE_COL0 = 1536
D_FF = 2816
ROPE_THETA = 10000.0
EPS = 1e-6
ADAM_LR, ADAM_B1, ADAM_B2, ADAM_EPS, ADAM_WD, ADAM_STEP = 0.001, 0.9, 0.999, 1e-08, 0.01, 10

N_CHIPS = 4
N_DEV = 8
VMEM_LIMIT_BYTES = 56 * 1024 * 1024
NEG_BIG = -1e30
MESH = pl.DeviceIdType.MESH
ANY = pl.BlockSpec(memory_space=pl.ANY)

SDS = jax.ShapeDtypeStruct


def _cp(*sem):
    return pltpu.CompilerParams(dimension_semantics=sem, vmem_limit_bytes=VMEM_LIMIT_BYTES)


def _dot(a, b, dims=((1,), (0,))):
    return lax.dot_general(a.astype(MXU_DTYPE), b.astype(MXU_DTYPE), (dims, ((), ())),
                           preferred_element_type=F32)


NT = ((1,), (1,))
TN = ((0,), (0,))


def _split_dot(x, m):
    hi = x.astype(MXU_DTYPE)
    lo = (x - hi.astype(F32)).astype(MXU_DTYPE)
    return _dot(hi, m) + _dot(lo, m)


def _seg_matrix(width, seg):
    idx = np.arange(width) // seg
    return jnp.asarray((idx[:, None] == idx[None, :]).astype(np.float32), dtype=MXU_DTYPE)


def _lane(shape):
    return lax.broadcasted_iota(jnp.int32, shape, len(shape) - 1)


def _row(shape):
    return lax.broadcasted_iota(jnp.int32, shape, 0)


def _full(shape):
    nd = len(shape)
    return pl.BlockSpec(shape, lambda *_: (0,) * nd)


def _gelu(x):
    k = math.sqrt(2.0 / math.pi)
    th = jnp.tanh(k * (x + 0.044715 * (x * x * x)))
    return 0.5 * x * (1.0 + th)


def _gelu_and_grad(x):
    k = math.sqrt(2.0 / math.pi)
    x2 = x * x
    th = jnp.tanh(k * (x + 0.044715 * (x2 * x)))
    g = 0.5 * x * (1.0 + th)
    dg = 0.5 * (1.0 + th) + 0.5 * x * (1.0 - th * th) * (k * (1.0 + 3.0 * 0.044715 * x2))
    return g, dg


def _sigmoid(x):
    return 0.5 * jnp.tanh(0.5 * x) + 0.5


def _swap_halves(x):
    w = x.shape[-1]
    first = (_lane(x.shape) % HEAD_DIM) < (HEAD_DIM // 2)
    return jnp.where(first, pltpu.roll(x, w - HEAD_DIM // 2, 1), pltpu.roll(x, HEAD_DIM // 2, 1))


def _tile_lanes(x, reps):
    return x if reps == 1 else jnp.concatenate([x] * reps, axis=1)


def _fold_lanes(x, period):
    w = x.shape[-1]
    while w > period:
        w //= 2
        x = x + pltpu.roll(x, w, 1)
    return x


def _mm(a, b, *, mode, tm, tn, tk, out_dtype=F32, add=None, name,
        a_lead=None, b_lead=None, b_sharded=False, out_into=None,
        b_koff=0, out_joff=0, out_n=None, deps=()):
    ash = a.shape[1:] if a_lead is not None else a.shape
    bsh = b.shape[1:] if b_lead is not None else b.shape
    if b_sharded:
        bsh = (b.shape[1], N_CHIPS * b.shape[2])
    if mode == 'nn':
        (M, K), (K2, N) = ash, bsh
    elif mode == 'nt':
        (M, K), (N, K2) = ash, bsh
    else:
        (K, M), (K2, N) = ash, bsh
    assert K == K2 or (mode == 'nt' and K2 > K), (ash, bsh, mode)
    assert M % tm == 0 and N % tn == 0 and K % tk == 0, (M, N, K, tm, tn, tk)
    nk = K // tk
    dims = {'nn': ((1,), (0,)), 'nt': NT, 'tn': TN}[mode]

    def lead(spec_shape, imap, lead_idx):
        if lead_idx is None:
            return pl.BlockSpec(spec_shape, imap)
        return pl.BlockSpec((None,) + spec_shape, lambda i, j, k: (lead_idx,) + imap(i, j, k))

    if mode == 'tn':
        a_spec = lead((tk, tm), lambda i, j, k: (k, i), a_lead)
    else:
        a_spec = lead((tm, tk), lambda i, j, k: (i, k), a_lead)
    if b_sharded:
        per = b.shape[2] // (tk if mode == 'nt' else tn)
        assert per * (tk if mode == 'nt' else tn) == b.shape[2] and mode != 'tn'
        if mode == 'nt':
            b_spec = pl.BlockSpec((None, tn, tk), lambda i, j, k: ((k + b_koff) // per, j, (k + b_koff) % per))
        else:
            b_spec = pl.BlockSpec((None, tk, tn), lambda i, j, k: (j // per, k, j % per))
    elif mode == 'nt':
        b_spec = lead((tn, tk), lambda i, j, k: (j, k + b_koff), b_lead)
    else:
        b_spec = lead((tk, tn), lambda i, j, k: (k, j), b_lead)
    o_spec = pl.BlockSpec((tm, tn), lambda i, j, k: (i, j + out_joff))
    n_out = N if out_n is None else out_n
    in_specs = [a_spec, b_spec]
    operands = [a, b]
    if add is not None:
        in_specs.append(pl.BlockSpec((tm, tn), lambda i, j, k: (i, j)))
        operands.append(add)
    aliases = {}
    if out_into is not None:
        in_specs.append(ANY)
        operands.append(out_into)
        aliases = {len(operands) - 1: 0}
    in_specs += [ANY] * len(deps)
    operands += list(deps)
    has_add = add is not None
    acc_in_out = nk > 1 and out_dtype == F32

    def body(*refs):
        a_ref, b_ref = refs[0], refs[1]
        pos = 2
        add_ref = None
        if has_add:
            add_ref = refs[pos]
            pos += 1
        if out_into is not None:
            pos += 1
        pos += len(deps)
        o_ref = refs[pos]
        acc_ref = refs[pos + 1] if (nk > 1 and not acc_in_out) else None
        p = _dot(a_ref[...], b_ref[...], dims)
        if nk == 1:
            if has_add:
                p = p + add_ref[...]
            o_ref[...] = p.astype(o_ref.dtype)
            return
        k = pl.program_id(2)
        tgt = o_ref if acc_in_out else acc_ref

        @pl.when(k == 0)
        def _():
            tgt[...] = p + add_ref[...] if has_add else p

        @pl.when(k > 0)
        def _():
            tgt[...] += p

        if not acc_in_out:
            @pl.when(k == nk - 1)
            def _():
                o_ref[...] = acc_ref[...].astype(o_ref.dtype)

    out_shape = SDS((M, n_out), out_dtype)
    scratch = [pltpu.VMEM((tm, tn), F32)] if (nk > 1 and not acc_in_out) else []
    return pl.pallas_call(
        body, grid=(M // tm, N // tn, nk), in_specs=in_specs, out_specs=o_spec, out_shape=out_shape,
        scratch_shapes=scratch, input_output_aliases=aliases, name=name,
        compiler_params=_cp("parallel", "parallel", "arbitrary"))(*operands)


def _rms_bwd_rows(xv, g, dh, dres):
    r = lax.rsqrt(jnp.mean(xv * xv, axis=-1, keepdims=True) + EPS)
    xh = xv * r
    gy = dh * g
    dx = r * (gy - xh * jnp.mean(xh * gy, axis=-1, keepdims=True)) + dres
    return dx, jnp.sum(dh * xh, axis=0, keepdims=True)


def _mm_nt_sharded_rms(a, b, x, g, dres, *, tm, name, deps=()):
    a3 = a if a.ndim == 3 else a[None]
    A, M, ka = a3.shape
    S, N, ns = b.shape
    per = S // A
    assert ka == per * ns and M % tm == 0 and N == x.shape[1], (a3.shape, b.shape, x.shape)

    def body(a_ref, b_ref, x_ref, g_ref, dres_ref, dx_ref, dxb_ref, dg_ref):
        acc = None
        for s in range(S):
            lo = (s % per) * ns
            p = _dot(a_ref[s // per, :, lo:lo + ns], b_ref[s], NT)
            acc = p if acc is None else acc + p
        dx, dg = _rms_bwd_rows(x_ref[...], g_ref[...], acc, dres_ref[...])
        dx_ref[...] = dx
        dxb_ref[...] = dx.astype(dxb_ref.dtype)

        @pl.when(pl.program_id(0) == 0)
        def _():
            dg_ref[...] = jnp.zeros_like(dg_ref)
        dg_ref[...] += dg

    rows = pl.BlockSpec((tm, N), lambda i: (i, 0))
    return pl.pallas_call(
        _after(body, 5, deps), grid=(M // tm,),
        in_specs=[pl.BlockSpec((A, tm, ka), lambda i: (0, i, 0)), pl.BlockSpec((S, N, ns), lambda i: (0, 0, 0)),
                  rows, _full((1, N)), rows] + [ANY] * len(deps),
        out_specs=[rows, rows, _full((1, N))],
        out_shape=[SDS((M, N), F32), SDS((M, N), MXU_DTYPE), SDS((1, N), F32)], name=name,
        compiler_params=_cp("arbitrary"))(a3, b, x, g, dres, *deps)


def _norm_mm(x, g, b, *, tm, tn, name, deps=()):
    M, K = x.shape
    S, K2, ns = b.shape
    per = ns // tn
    assert K == K2 and per * tn == ns and M % tm == 0, (x.shape, b.shape)

    def body(x_ref, g_ref, b_ref, o_ref, h_ref):
        @pl.when(pl.program_id(1) == 0)
        def _():
            xv = x_ref[...]
            r = lax.rsqrt(jnp.mean(xv * xv, axis=-1, keepdims=True) + EPS)
            h_ref[...] = (xv * r * g_ref[...]).astype(h_ref.dtype)
        o_ref[...] = _dot(h_ref[...], b_ref[...])

    return pl.pallas_call(
        _after(body, 3, deps), grid=(M // tm, S * per),
        in_specs=[pl.BlockSpec((tm, K), lambda i, j: (i, 0)), _full((1, K)),
                  pl.BlockSpec((None, K, tn), lambda i, j: (j // per, 0, j % per))] + [ANY] * len(deps),
        out_specs=[pl.BlockSpec((tm, tn), lambda i, j: (i, j)), pl.BlockSpec((tm, K), lambda i, j: (i, 0))],
        out_shape=[SDS((M, S * ns), F32), SDS((M, K), MXU_DTYPE)], name=name,
        compiler_params=_cp("parallel", "arbitrary"))(x, g, b, *deps)


def _after(body, n_in, deps):
    nd = len(deps)
    if nd == 0:
        return body
    return lambda *refs: body(*refs[:n_in], *refs[n_in + nd:])


def _down_proj_loss(act, w, x1, target, *, tm, name):
    T, K = act.shape
    D = w.shape[1]

    def body(a_ref, w_ref, x_ref, t_ref, loss_ref, dy_ref, dyb_ref):
        i = pl.program_id(0)
        d = (x_ref[...] + _dot(a_ref[...], w_ref[...])) - t_ref[...]
        dy = d * (1.0 / D)
        dy_ref[...] = dy
        dyb_ref[...] = dy.astype(dyb_ref.dtype)
        part = jnp.sum(jnp.sum(d * d, axis=1, keepdims=True), axis=0, keepdims=True) * (0.5 / D)

        @pl.when(i == 0)
        def _():
            loss_ref[...] = jnp.zeros_like(loss_ref)
        loss_ref[...] += jnp.broadcast_to(part, loss_ref.shape)

    rows = pl.BlockSpec((tm, D), lambda i: (i, 0))
    return pl.pallas_call(
        body, grid=(T // tm,), in_specs=[pl.BlockSpec((tm, K), lambda i: (i, 0)), _full((K, D)), rows, rows],
        out_specs=[_full((1, 128)), rows, rows],
        out_shape=[SDS((1, 128), F32), SDS((T, D), F32), SDS((T, D), MXU_DTYPE)],
        name=name, compiler_params=_cp("arbitrary"))(act, w, x1, target)


def _pool_lane_consts(shape):
    lane = _lane(shape)
    grp = lane // (POOL_WIDTH // 4)
    win = jnp.where(grp == 0, 2, jnp.where(grp == 1, 4, jnp.where(grp == 2, 8, 16)))
    return grp, win


def _pool_select(grp, s2, s4, s8, s16):
    return jnp.where(grp == 0, s2, jnp.where(grp == 1, s4, jnp.where(grp == 2, s8, s16)))


def _pool_diff(xe, row0, tr):
    s2 = xe + pltpu.roll(xe, 1, 0)
    s4 = s2 + pltpu.roll(s2, 2, 0)
    s8 = s4 + pltpu.roll(s4, 4, 0)
    s16 = s8 + pltpu.roll(s8, 8, 0)
    shape = (tr, POOL_WIDTH)
    grp, win = _pool_lane_consts(shape)
    sums = _pool_select(grp, s2[16:], s4[16:], s8[16:], s16[16:])
    t = row0 + _row(shape)
    cnt = jnp.minimum(t + 1, win).astype(F32)
    return sums / cnt - xe[16:]


def _pool_fwd(z, wbd, scale, *, tr, name):
    T = z.shape[0]
    hb = tr // 16

    def body(x_ref, xp_ref, w_ref, s_ref, o_ref):
        i = pl.program_id(0)
        halo = jnp.where(i == 0, 0.0, xp_ref[...])
        diff = _pool_diff(jnp.concatenate([halo, x_ref[...]], axis=0), i * tr, tr)
        o_ref[...] = (_dot(diff, w_ref[...]) * s_ref[...]).astype(o_ref.dtype)

    return pl.pallas_call(
        body, grid=(T // tr,),
        in_specs=[pl.BlockSpec((tr, POOL_WIDTH), lambda i: (i, 0)),
                  pl.BlockSpec((16, POOL_WIDTH), lambda i: (jnp.maximum(i * hb - 1, 0), 0)),
                  _full((POOL_WIDTH, POOL_WIDTH)), _full((1, POOL_WIDTH))],
        out_specs=pl.BlockSpec((tr, POOL_WIDTH), lambda i: (i, 0)),
        out_shape=SDS((T, POOL_WIDTH), MXU_DTYPE), name=name, compiler_params=_cp("parallel"))(z, z, wbd, scale)


def _pool_bwd_tile(i, n, tr, x, xprev, dpa, dpa_next, wbd, scale):
    halo = jnp.where(i == 0, 0.0, xprev)
    diff = _pool_diff(jnp.concatenate([halo, x], axis=0), i * tr, tr)
    mixed = _dot(diff, wbd)
    dscale = jnp.sum(dpa * mixed, axis=0, keepdims=True)
    dnext = jnp.where(i == n - 1, 0.0, dpa_next)
    dmix_e = jnp.concatenate([dpa, dnext], axis=0) * scale
    ddiff_e = _dot(dmix_e, wbd, NT)
    dwbd = _dot(diff, dmix_e[:tr], TN)
    shape = (tr + 16, POOL_WIDTH)
    grp, win = _pool_lane_consts(shape)
    t = i * tr + _row(shape)
    e = ddiff_e / jnp.minimum(t + 1, win).astype(F32)
    nrow = tr + 16
    a2 = e + pltpu.roll(e, nrow - 1, 0)
    a4 = a2 + pltpu.roll(a2, nrow - 2, 0)
    a8 = a4 + pltpu.roll(a4, nrow - 4, 0)
    a16 = a8 + pltpu.roll(a8, nrow - 8, 0)
    dx = _pool_select(grp, a2, a4, a8, a16)[:tr] - ddiff_e[:tr]
    return dx, dwbd, dscale


def _norm_rope(x, g, cos, sin_signed, seg):
    reps = x.shape[1] // 128
    ms = _split_dot(x * x, seg) * (1.0 / HEAD_DIM)
    r = lax.rsqrt(ms + EPS)
    xn = x * r * g
    c, s = _tile_lanes(cos, reps), _tile_lanes(sin_signed, reps)
    return xn * c + _swap_halves(xn) * s


def _norm_rope_bwd(x, g, cos, sin_signed, seg, dout):
    reps = x.shape[1] // 128
    c, s = _tile_lanes(cos, reps), _tile_lanes(sin_signed, reps)
    dxn = dout * c + _swap_halves(dout * s)
    ms = _split_dot(x * x, seg) * (1.0 / HEAD_DIM)
    r = lax.rsqrt(ms + EPS)
    xh = x * r
    gy = dxn * g
    dx = r * (gy - xh * (_split_dot(xh * gy, seg) * (1.0 / HEAD_DIM)))
    dg = jnp.sum(dxn * xh, axis=0, keepdims=True)
    return dx, dg


def _dup_heads(k):
    first = _lane(k.shape) < HEAD_DIM
    kr = pltpu.roll(k, HEAD_DIM, 1)
    return jnp.concatenate([jnp.where(first, k, kr), jnp.where(first, kr, k)], axis=1)


def _qkv_prep(z, cos, sin_signed, gq, gk, seg, *, tr, name):
    T = z.shape[0]

    def body(qa_ref, qb_ref, kv_ref, c_ref, s_ref, gq_ref, gk_ref, seg_ref, q_ref, k_ref, v_ref):
        c, s, seg_m = c_ref[...], s_ref[...], seg_ref[...]
        scale = HEAD_DIM ** -0.5
        qa = _norm_rope(qa_ref[...], gq_ref[...], c, s, seg_m) * scale
        qb = _norm_rope(qb_ref[...], gq_ref[...], c, s, seg_m) * scale
        q_ref[...] = jnp.concatenate([qa, qb], axis=1).astype(q_ref.dtype)
        kv = kv_ref[...]
        k = _norm_rope(kv[:, :KV_WIDTH], gk_ref[...], c, s, seg_m[:128, :128])
        k_ref[...] = _dup_heads(k).astype(k_ref.dtype)
        v_ref[...] = _dup_heads(kv[:, KV_WIDTH:]).astype(v_ref.dtype)

    col = lambda j: pl.BlockSpec((tr, 256), lambda i: (i, j))
    tab = pl.BlockSpec((tr, 128), lambda i: (i, 0))
    return pl.pallas_call(
        body, grid=(T // tr,),
        in_specs=[col(1), col(2), col(3), tab, tab, _full((1, 256)), _full((1, 128)), _full((256, 256))],
        out_specs=[pl.BlockSpec((tr, 512), lambda i: (i, 0)), col(0), col(0)],
        out_shape=[SDS((T, 512), MXU_DTYPE), SDS((T, 256), MXU_DTYPE), SDS((T, 256), MXU_DTYPE)],
        name=name, compiler_params=_cp("parallel"))(z, z, z, cos, sin_signed, gq, gk, seg)


GROUP_HEADS = 4
GROUP_ROWS = GROUP_HEADS * ATTN_BLOCK
ALL_ROWS = N_Q_HEADS * ATTN_BLOCK


def _attn_mask(n):
    qi = _row((ALL_ROWS, 2 * ATTN_BLOCK)) % ATTN_BLOCK
    kj = _lane((ALL_ROWS, 2 * ATTN_BLOCK))
    return (kj > qi) & (kj <= qi + ATTN_BLOCK) & ((kj >= ATTN_BLOCK) | (n > 0))


def _stack_heads(x, g):
    first = _lane((ATTN_BLOCK, 128)) < HEAD_DIM
    parts = []
    for pair in (2 * g, 2 * g + 1):
        x128 = x[:, 128 * pair:128 * (pair + 1)]
        zero = jnp.zeros_like(x128)
        parts += [jnp.where(first, x128, zero), jnp.where(first, zero, x128)]
    return jnp.concatenate(parts, axis=0)


def _unstack_heads(y):
    first = _lane((ATTN_BLOCK, 128)) < HEAD_DIM
    b = ATTN_BLOCK
    return jnp.concatenate([jnp.where(first, y[0:b], y[b:2 * b]), jnp.where(first, y[2 * b:3 * b], y[3 * b:4 * b])],
                           axis=1)


def _sink_col(sk_ref):
    return jnp.concatenate([jnp.broadcast_to(sk_ref[h:h + 1, 0:1], (ATTN_BLOCK, 1)) for h in range(N_Q_HEADS)],
                           axis=0)


def _by_group(a8, b2, dims=((1,), (0,))):
    return jnp.concatenate([_dot(a8[:GROUP_ROWS], b2[:, :128], dims), _dot(a8[GROUP_ROWS:], b2[:, 128:], dims)],
                           axis=0)


def _softmax_exp(q8, k2, mask, sink):
    s = jnp.where(mask, _by_group(q8, k2, NT), NEG_BIG)
    m = jnp.maximum(jnp.max(s, axis=1, keepdims=True), sink)
    p = jnp.exp(s - m)
    ps = jnp.exp(sink - m)
    return p, ps, 1.0 / (jnp.sum(p, axis=1, keepdims=True) + ps)


def _attn_fwd(q, k, v, sinks_b, *, name):
    T = q.shape[0]
    nb = T // ATTN_BLOCK

    def body(q_ref, kc_ref, kp_ref, vc_ref, vp_ref, sk_ref, o_ref):
        n = pl.program_id(0)
        mask = _attn_mask(n)
        k2 = jnp.concatenate([kp_ref[...], kc_ref[...]], axis=0)
        v2 = jnp.concatenate([vp_ref[...], vc_ref[...]], axis=0)
        qv = q_ref[...]
        q8 = jnp.concatenate([_stack_heads(qv, 0), _stack_heads(qv, 1)], axis=0)
        p, _, inv = _softmax_exp(q8, k2, mask, _sink_col(sk_ref))
        o8 = _by_group(p, v2) * inv
        o_ref[...] = jnp.concatenate([_unstack_heads(o8[:GROUP_ROWS]), _unstack_heads(o8[GROUP_ROWS:])],
                                     axis=1).astype(o_ref.dtype)

    cur = lambda w: pl.BlockSpec((ATTN_BLOCK, w), lambda n: (n, 0))
    prev = lambda w: pl.BlockSpec((ATTN_BLOCK, w), lambda n: (jnp.maximum(n - 1, 0), 0))
    return pl.pallas_call(
        body, grid=(nb,),
        in_specs=[cur(512), cur(256), prev(256), cur(256), prev(256), _full((8, 128))],
        out_specs=cur(512), out_shape=SDS((T, 512), MXU_DTYPE), name=name,
        compiler_params=_cp("parallel"))(q, k, k, v, v, sinks_b)


def _attn_bwd(q, k, v, sinks_b, do, *, name, deps=()):
    T = q.shape[0]
    nb = T // ATTN_BLOCK

    def body(q_ref, kc_ref, kp_ref, vc_ref, vp_ref, sk_ref, do_ref,
             dq_ref, dkc_ref, dkp_ref, dvc_ref, dvp_ref, dsk_ref):
        n = pl.program_id(0)
        mask = _attn_mask(n)
        k2 = jnp.concatenate([kp_ref[...], kc_ref[...]], axis=0)
        v2 = jnp.concatenate([vp_ref[...], vc_ref[...]], axis=0)
        qv = q_ref[...]
        dov = do_ref[...]

        @pl.when(n == 0)
        def _():
            dsk_ref[...] = jnp.zeros_like(dsk_ref)

        q8 = jnp.concatenate([_stack_heads(qv, 0), _stack_heads(qv, 1)], axis=0)
        do8 = jnp.concatenate([_stack_heads(dov, 0), _stack_heads(dov, 1)], axis=0)
        p, ps, inv = _softmax_exp(q8, k2, mask, _sink_col(sk_ref))
        pn = p * inv
        delta = jnp.sum(do8 * _by_group(pn, v2), axis=1, keepdims=True)
        ds = pn * (_by_group(do8, v2, NT) - delta)
        dq8 = _by_group(ds, k2)
        dq_ref[...] = jnp.concatenate([_unstack_heads(dq8[:GROUP_ROWS]), _unstack_heads(dq8[GROUP_ROWS:])], axis=1)
        dk = jnp.concatenate([_dot(ds[:GROUP_ROWS], q8[:GROUP_ROWS], TN), _dot(ds[GROUP_ROWS:], q8[GROUP_ROWS:], TN)],
                             axis=1)
        dv = jnp.concatenate([_dot(pn[:GROUP_ROWS], do8[:GROUP_ROWS], TN),
                              _dot(pn[GROUP_ROWS:], do8[GROUP_ROWS:], TN)], axis=1)
        wsink = (ps * inv) * delta
        for h in range(N_Q_HEADS):
            dsink = -jnp.sum(wsink[ATTN_BLOCK * h:ATTN_BLOCK * (h + 1)], axis=0, keepdims=True)
            dsk_ref[h:h + 1, :] += jnp.broadcast_to(dsink, (1, 128))
        dkp_ref[...] = dk[:ATTN_BLOCK]
        dkc_ref[...] = dk[ATTN_BLOCK:]
        dvp_ref[...] = dv[:ATTN_BLOCK]
        dvc_ref[...] = dv[ATTN_BLOCK:]

    cur = lambda w: pl.BlockSpec((ATTN_BLOCK, w), lambda n: (n, 0))
    prev = lambda w: pl.BlockSpec((ATTN_BLOCK, w), lambda n: (jnp.maximum(n - 1, 0), 0))
    f = lambda w: SDS((T, w), F32)
    return pl.pallas_call(
        _after(body, 7, deps), grid=(nb,),
        in_specs=[cur(512), cur(256), prev(256), cur(256), prev(256), _full((8, 128)), cur(512)] + [ANY] * len(deps),
        out_specs=[cur(512), cur(256), cur(256), cur(256), cur(256), _full((8, 128))],
        out_shape=[f(512), f(256), f(256), f(256), f(256), SDS((8, 128), F32)],
        name=name, compiler_params=_cp("arbitrary"))(q, k, k, v, v, sinks_b, do, *deps)


def _mixer_ab_bwd(z, cos, sin_signed, gq, gk, seg, dq, dkc, dkp, dvc, dvp, dpa, wbd, scale, dz, *, tr, name, deps=()):
    T = z.shape[0]
    n = T // tr
    hb = tr // 16
    ab = tr // ATTN_BLOCK

    def unfold(cur, nxt_tile, nxt_halo, i):
        nxt = jnp.concatenate([nxt_tile[ATTN_BLOCK:], jnp.where(i == n - 1, 0.0, nxt_halo)], axis=0)
        tot = cur + nxt
        first = _lane((tr, 128)) < HEAD_DIM
        a = tot[:, :128]
        b = tot[:, 128:]
        a = a + pltpu.roll(a, HEAD_DIM, 1)
        b = b + pltpu.roll(b, HEAD_DIM, 1)
        return jnp.where(first, a, b)

    def body(xp_ref, xpp_ref, qa_ref, qb_ref, kv_ref, c_ref, s_ref, gq_ref, gk_ref, seg_ref,
             dq_ref, dkc_ref, dkp_ref, dkh_ref, dvc_ref, dvp_ref, dvh_ref, dpa_ref, dpan_ref, w_ref, sc_ref, _dz_in,
             dz_ref, dgq_ref, dgk_ref, dw_ref, dsc_ref):
        i = pl.program_id(0)
        c, s, seg_m = c_ref[...], s_ref[...], seg_ref[...]
        scale_q = HEAD_DIM ** -0.5
        dqv = dq_ref[...] * scale_q
        dxa, dga = _norm_rope_bwd(qa_ref[...], gq_ref[...], c, s, seg_m, dqv[:, :256])
        dxb, dgb = _norm_rope_bwd(qb_ref[...], gq_ref[...], c, s, seg_m, dqv[:, 256:])
        dk = unfold(dkc_ref[...], dkp_ref[...], dkh_ref[...], i)
        dv = unfold(dvc_ref[...], dvp_ref[...], dvh_ref[...], i)
        kv = kv_ref[...]
        dxk, dgk = _norm_rope_bwd(kv[:, :KV_WIDTH], gk_ref[...], c, s, seg_m[:128, :128], dk)
        dxp, dwbd, dscale = _pool_bwd_tile(i, n, tr, xp_ref[...], xpp_ref[...], dpa_ref[...], dpan_ref[...],
                                           w_ref[...], sc_ref[...])
        dz_ref[...] = jnp.concatenate([dxp, dxa, dxb, dxk, dv], axis=1).astype(dz_ref.dtype)

        @pl.when(i == 0)
        def _():
            dgq_ref[...] = jnp.zeros_like(dgq_ref)
            dgk_ref[...] = jnp.zeros_like(dgk_ref)
            dw_ref[...] = jnp.zeros_like(dw_ref)
            dsc_ref[...] = jnp.zeros_like(dsc_ref)
        dgq_ref[...] += _fold_lanes(dga + dgb, HEAD_DIM)
        dgk_ref[...] += _fold_lanes(dgk, HEAD_DIM)
        dw_ref[...] += dwbd
        dsc_ref[...] += dscale

    col = lambda j: pl.BlockSpec((tr, 256), lambda i: (i, j))
    rows = lambda w: pl.BlockSpec((tr, w), lambda i: (i, 0))
    nxt_blk = pl.BlockSpec((ATTN_BLOCK, 256), lambda i: (jnp.minimum((i + 1) * ab, T // ATTN_BLOCK - 1), 0))
    prev16 = pl.BlockSpec((16, 256), lambda i: (jnp.maximum(i * hb - 1, 0), 0))
    next16 = pl.BlockSpec((16, 256), lambda i: (jnp.minimum((i + 1) * hb, T // 16 - 1), 0))
    return pl.pallas_call(
        _after(body, 22, deps), grid=(n,),
        in_specs=[col(0), prev16, col(1), col(2), col(3), rows(128), rows(128),
                  _full((1, 256)), _full((1, 128)), _full((256, 256)),
                  rows(512), rows(256), rows(256), nxt_blk, rows(256), rows(256), nxt_blk,
                  rows(256), next16, _full((256, 256)), _full((1, 256)), ANY] + [ANY] * len(deps),
        out_specs=[rows(1024), _full((1, 256)), _full((1, 128)), _full((256, 256)), _full((1, 256))],
        out_shape=[SDS((T, IN_COLS), MXU_DTYPE), SDS((1, 256), F32), SDS((1, 128), F32),
                   SDS((256, 256), F32), SDS((1, 256), F32)],
        input_output_aliases={21: 0}, name=name, compiler_params=_cp("arbitrary"))(
            z, z, z, z, z, cos, sin_signed, gq, gk, seg, dq, dkc, dkp, dkp, dvc, dvp, dvp, dpa, dpa, wbd, scale, dz,
            *deps)


def _sgu_common(zu, zv, vn, seg):
    u, du = _gelu_and_grad(zu)
    gv, dgv = _gelu_and_grad(zv)
    ms = _split_dot(gv * gv, seg) * (1.0 / HEAD_DIM)
    r = lax.rsqrt(ms + EPS)
    xh = gv * r
    return u, du, dgv, r, xh, xh * vn


def _sgu_fwd(z, wtril, bexp, vn, seg, *, tr, name):
    T = z.shape[0]
    nch = tr // CHUNK

    def body(u_ref, v_ref, w_ref, b_ref, vn_ref, seg_ref, o_ref):
        u, _, _, _, _, vg = _sgu_common(u_ref[...], v_ref[...], vn_ref[...], seg_ref[...])
        grp = _lane((CHUNK, SGU_WIDTH)) // HEAD_DIM
        outs = []
        for ch in range(nch):
            vc = vg[ch * CHUNK:(ch + 1) * CHUNK]
            s = b_ref[...]
            for g in range(4):
                s = s + jnp.where(grp == g, _dot(w_ref[g], vc), 0.0)
            outs.append(u[ch * CHUNK:(ch + 1) * CHUNK] * s)
        o_ref[...] = jnp.concatenate(outs, axis=0).astype(o_ref.dtype)

    col = lambda j: pl.BlockSpec((tr, 256), lambda i: (i, j))
    return pl.pallas_call(
        body, grid=(T // tr,),
        in_specs=[col(4), col(5), _full((4, CHUNK, CHUNK)), _full((CHUNK, 256)), _full((1, 256)), _full((256, 256))],
        out_specs=col(0), out_shape=SDS((T, SGU_WIDTH), MXU_DTYPE), name=name,
        compiler_params=_cp("parallel"))(z, z, wtril, bexp, vn, seg)


def _sgu_bwd(z, wtril, bexp, vn, seg, dsg, dz, *, tr, name):
    T = z.shape[0]
    nch = tr // CHUNK

    def body(u_ref, v_ref, w_ref, b_ref, vn_ref, seg_ref, d_ref, _dz_in, dz_ref, dw_ref, db_ref, dvn_ref):
        i = pl.program_id(0)
        seg_m = seg_ref[...]
        vn_v = vn_ref[...]
        u, du, dgv, r, xh, vg = _sgu_common(u_ref[...], v_ref[...], vn_v, seg_m)
        d = d_ref[...]
        grp = _lane((CHUNK, SGU_WIDTH)) // HEAD_DIM
        tril = _row((CHUNK, CHUNK)) >= _lane((CHUNK, CHUNK))

        @pl.when(i == 0)
        def _():
            dw_ref[...] = jnp.zeros_like(dw_ref)
            db_ref[...] = jnp.zeros_like(db_ref)
            dvn_ref[...] = jnp.zeros_like(dvn_ref)

        dus, dvgs = [], []
        for ch in range(nch):
            sl = slice(ch * CHUNK, (ch + 1) * CHUNK)
            vc = vg[sl]
            s = b_ref[...]
            for g in range(4):
                s = s + jnp.where(grp == g, _dot(w_ref[g], vc), 0.0)
            dus.append(d[sl] * s)
            ds = d[sl] * u[sl]
            db_ref[...] += _split_dot(ds, seg_m)
            dvg = jnp.zeros((CHUNK, SGU_WIDTH), F32)
            for g in range(4):
                dsm = jnp.where(grp == g, ds, 0.0)
                dvg = dvg + jnp.where(grp == g, _dot(w_ref[g], ds, TN), 0.0)
                dw_ref[g] += jnp.where(tril, _dot(dsm, vc, NT), 0.0)
            dvgs.append(dvg)
        dup = jnp.concatenate(dus, axis=0)
        dvg = jnp.concatenate(dvgs, axis=0)
        dvn_ref[...] += _fold_lanes(jnp.sum(dvg * xh, axis=0, keepdims=True), HEAD_DIM)
        gy = dvg * vn_v
        dgvv = r * (gy - xh * (_split_dot(xh * gy, seg_m) * (1.0 / HEAD_DIM)))
        dz_ref[...] = jnp.concatenate([dup * du, dgvv * dgv], axis=1).astype(dz_ref.dtype)

    col = lambda j: pl.BlockSpec((tr, 256), lambda i: (i, j))
    return pl.pallas_call(
        body, grid=(T // tr,),
        in_specs=[col(4), col(5), _full((4, CHUNK, CHUNK)), _full((CHUNK, 256)), _full((1, 256)), _full((256, 256)),
                  col(0), ANY],
        out_specs=[pl.BlockSpec((tr, 512), lambda i: (i, 2)), _full((4, CHUNK, CHUNK)), _full((CHUNK, 256)),
                   _full((1, 256))],
        out_shape=[SDS((T, IN_COLS), MXU_DTYPE), SDS((4, CHUNK, CHUNK), F32), SDS((CHUNK, 256), F32),
                   SDS((1, 256), F32)],
        input_output_aliases={7: 0}, name=name, compiler_params=_cp("arbitrary"))(
            z, z, wtril, bexp, vn, seg, dsg, dz)


def _merge_fwd(pa, at, sg, wa, wb, wc, z, *, tm, tn, name):
    T = pa.shape[0]
    gb = GATE_COL0 // tn
    nb = D_MODEL // tn

    def body(pa_ref, at_ref, sg_ref, wa_ref, wb_ref, wc_ref, g0_ref, g1_ref, g2_ref, m_ref, y_ref):
        acc = None
        for idx, (op_ref, w_ref, g_ref) in enumerate(((pa_ref, wa_ref, g0_ref), (at_ref, wb_ref, g1_ref),
                                                      (sg_ref, wc_ref, g2_ref))):
            y = _dot(op_ref[...], w_ref[...])
            y_ref[idx] = y.astype(y_ref.dtype)
            t = _sigmoid(g_ref[...]) * y
            acc = t if acc is None else acc + t
        m_ref[...] = acc.astype(m_ref.dtype)

    op = lambda w: pl.BlockSpec((tm, w), lambda i, j: (i, 0))
    wt = lambda k: pl.BlockSpec((k, tn), lambda i, j: (0, j))
    gate = lambda b: pl.BlockSpec((tm, tn), lambda i, j: (i, gb + b * nb + j))
    return pl.pallas_call(
        body, grid=(T // tm, nb),
        in_specs=[op(256), op(512), op(256), wt(256), wt(512), wt(256), gate(0), gate(1), gate(2)],
        out_specs=[pl.BlockSpec((tm, tn), lambda i, j: (i, j)), pl.BlockSpec((3, tm, tn), lambda i, j: (0, i, j))],
        out_shape=[SDS((T, D_MODEL), MXU_DTYPE), SDS((3, T, D_MODEL), MXU_DTYPE)],
        name=name, compiler_params=_cp("parallel", "parallel"))(pa, at, sg, wa, wb, wc, z, z, z)


def _out_dx_merge_bwd(dxb, w_out, y, z, ws, xs, *, tm, tn, name):
    T = dxb.shape[0]
    gb = GATE_COL0 // tn
    nb = D_MODEL // tn
    nr = T // tm
    widths = [w.shape[0] for w in ws]

    def body(dx_ref, w_ref, y_ref, g_ref, *refs):
        w_refs, x_refs = refs[0:3], refs[3:6]
        dz_ref, dx_refs, dw_refs = refs[6], refs[7:10], refs[10:13]
        dm_ref, acc_refs = refs[13], refs[14:17]
        i, b, j = pl.program_id(0), pl.program_id(1), pl.program_id(2)

        @pl.when((b == 0) & (j == 0))
        def _():
            dm = _dot(dx_ref[...], w_ref[...], NT)
            for jj in range(nb):
                dm_ref[jj] = dm[:, jj * tn:(jj + 1) * tn]

        d = dm_ref[j]
        g = _sigmoid(g_ref[...])
        dy = (d * g).astype(MXU_DTYPE)
        dz_ref[...] = (d * y_ref[...].astype(F32) * g * (1.0 - g)).astype(dz_ref.dtype)
        for branch in range(3):
            @pl.when(b == branch)
            def _():
                p = _dot(dy, w_refs[branch][...], NT)
                q = _dot(x_refs[branch][...], dy, TN)

                @pl.when(j == 0)
                def _():
                    dx_refs[branch][...] = p

                @pl.when(j > 0)
                def _():
                    dx_refs[branch][...] += p

                @pl.when(i == 0)
                def _():
                    acc_refs[branch][j] = q

                @pl.when(i > 0)
                def _():
                    acc_refs[branch][j] += q

        @pl.when((i == nr - 1) & (b == 2) & (j == nb - 1))
        def _():
            for branch in range(3):
                for jj in range(nb):
                    dw_refs[branch][:, jj * tn:(jj + 1) * tn] = acc_refs[branch][jj]

    wspec = lambda k: pl.BlockSpec((k, tn), lambda i, b, j: (0, j))
    rows = lambda k: pl.BlockSpec((tm, k), lambda i, b, j: (i, 0))
    return pl.pallas_call(
        body, grid=(nr, 3, nb),
        in_specs=[rows(D_MODEL), _full((D_MODEL, D_MODEL)),
                  pl.BlockSpec((None, tm, tn), lambda i, b, j: (b, i, j)),
                  pl.BlockSpec((tm, tn), lambda i, b, j: (i, gb + b * nb + j))]
        + [wspec(k) for k in widths] + [rows(k) for k in widths],
        out_specs=[pl.BlockSpec((tm, tn), lambda i, b, j: (i, gb + b * nb + j))]
        + [rows(k) for k in widths] + [_full((k, D_MODEL)) for k in widths],
        out_shape=[SDS((T, IN_COLS), MXU_DTYPE)] + [SDS((T, k), F32) for k in widths]
        + [SDS((k, D_MODEL), F32) for k in widths],
        scratch_shapes=[pltpu.VMEM((nb, tm, tn), F32)] + [pltpu.VMEM((nb, k, tn), F32) for k in widths],
        name=name, compiler_params=_cp("arbitrary", "arbitrary", "arbitrary"))(dxb, w_out, y, z, *ws, *xs)


def _conv3(xe, w, b):
    return (w[0:1] * pltpu.roll(xe, 2, 0) + w[1:2] * pltpu.roll(xe, 1, 0) + w[2:3] * xe)[8:] + b


def _conv_act_fwd(up, cw, cb, *, tr, tc, name):
    T = up.shape[0]
    nc = D_FF // tc
    hb = tr // 8

    def body(ug_ref, ugp_ref, uv_ref, uvp_ref, wg_ref, wv_ref, bg_ref, bv_ref, o_ref, c_ref):
        i = pl.program_id(1)
        first = i == 0
        cg = _conv3(jnp.concatenate([jnp.where(first, 0.0, ugp_ref[...]), ug_ref[...]], axis=0), wg_ref[...], bg_ref[...])
        cv = _conv3(jnp.concatenate([jnp.where(first, 0.0, uvp_ref[...]), uv_ref[...]], axis=0), wv_ref[...], bv_ref[...])
        o_ref[...] = (cg * _sigmoid(cg) * cv).astype(o_ref.dtype)
        c_ref[0] = cg.astype(c_ref.dtype)
        c_ref[1] = cv.astype(c_ref.dtype)

    tile = lambda off: pl.BlockSpec((tr, tc), lambda j, i: (i, off + j))
    prev = lambda off: pl.BlockSpec((8, tc), lambda j, i: (jnp.maximum(i * hb - 1, 0), off + j))
    par = lambda rows, off: pl.BlockSpec((rows, tc), lambda j, i: (0, off + j))
    return pl.pallas_call(
        body, grid=(nc, T // tr),
        in_specs=[tile(0), prev(0), tile(nc), prev(nc), par(3, 0), par(3, nc), par(1, 0), par(1, nc)],
        out_specs=[pl.BlockSpec((tr, tc), lambda j, i: (i, j)), pl.BlockSpec((2, tr, tc), lambda j, i: (0, i, j))],
        out_shape=[SDS((T, D_FF), MXU_DTYPE), SDS((2, T, D_FF), CONV_SAVE_DTYPE)], name=name,
        compiler_params=_cp("parallel", "parallel"))(up, up, up, up, cw, cw, cb, cb)


def _conv_act_bwd(up, c, cw, dact, *, tr, tc, name, deps=()):
    T = up.shape[0]
    nc = D_FF // tc
    nr = T // tr

    def body(ug_ref, uv_ref, cg_ref, cgn_ref, cv_ref, cvn_ref, da_ref, dan_ref, wg_ref, wv_ref,
             du_ref, dwg_ref, dwv_ref, dbg_ref, dbv_ref):
        i = pl.program_id(1)
        first, last = i == 0, i == nr - 1
        da = jnp.concatenate([da_ref[...], jnp.where(last, 0.0, dan_ref[...])], axis=0)
        cg = jnp.concatenate([cg_ref[...].astype(F32), cgn_ref[...].astype(F32)[:8]], axis=0)
        cv = jnp.concatenate([cv_ref[...].astype(F32), cvn_ref[...].astype(F32)[:8]], axis=0)
        sg = _sigmoid(cg)
        dcg = da * cv * (sg * (1.0 + cg * (1.0 - sg)))
        dcv = da * (cg * sg)
        nrow = tr + 8

        def grads(dc, w, u):
            d0 = dc[:tr]
            d1 = pltpu.roll(dc, nrow - 1, 0)[:tr]
            d2 = pltpu.roll(dc, nrow - 2, 0)[:tr]
            du = w[2:3] * d0 + w[1:2] * d1 + w[0:1] * d2
            dw = jnp.concatenate([jnp.sum(u * d2, axis=0, keepdims=True), jnp.sum(u * d1, axis=0, keepdims=True),
                                  jnp.sum(u * d0, axis=0, keepdims=True)], axis=0)
            return du, dw, jnp.sum(d0, axis=0, keepdims=True)

        dug, dwg, dbg = grads(dcg, wg_ref[...], ug_ref[...])
        duv, dwv, dbv = grads(dcv, wv_ref[...], uv_ref[...])
        du_ref[0] = dug.astype(du_ref.dtype)
        du_ref[1] = duv.astype(du_ref.dtype)

        @pl.when(first)
        def _():
            dwg_ref[...] = jnp.zeros_like(dwg_ref)
            dwv_ref[...] = jnp.zeros_like(dwv_ref)
            dbg_ref[...] = jnp.zeros_like(dbg_ref)
            dbv_ref[...] = jnp.zeros_like(dbv_ref)
        dwg_ref[...] += dwg
        dwv_ref[...] += dwv
        dbg_ref[...] += dbg
        dbv_ref[...] += dbv

    tile = lambda off: pl.BlockSpec((tr, tc), lambda j, i: (i, off + j))
    nxt = pl.BlockSpec((8, tc), lambda j, i: (jnp.minimum((i + 1) * (tr // 8), T // 8 - 1), j))
    ctile = lambda h: pl.BlockSpec((None, tr, tc), lambda j, i: (h, i, j))
    cnext = lambda h: pl.BlockSpec((None, 16, tc), lambda j, i: (h, jnp.minimum((i + 1) * (tr // 16), T // 16 - 1), j))
    par = lambda rows, off: pl.BlockSpec((rows, tc), lambda j, i: (0, off + j))
    acc = lambda rows: pl.BlockSpec((rows, tc), lambda j, i: (0, j))
    return pl.pallas_call(
        _after(body, 10, deps), grid=(nc, nr),
        in_specs=[tile(0), tile(nc), ctile(0), cnext(0), ctile(1), cnext(1), tile(0), nxt, par(3, 0), par(3, nc)]
        + [ANY] * len(deps),
        out_specs=[pl.BlockSpec((2, tr, tc), lambda j, i: (0, i, j)), acc(3), acc(3), acc(1), acc(1)],
        out_shape=[SDS((2, T, D_FF), MXU_DTYPE), SDS((3, D_FF), F32), SDS((3, D_FF), F32),
                   SDS((1, D_FF), F32), SDS((1, D_FF), F32)],
        name=name, compiler_params=_cp("parallel", "arbitrary"))(
            up, up, c, c, c, c, dact, dact, cw, cw, *deps)


def _row_tile(rows, cap):
    t = min(cap, rows)
    t -= t % 8
    while rows % t:
        t -= 8
    return t


def _adamw(w, g, m, v, *, tr, name):
    R, C = w.shape
    assert R % tr == 0, (R, tr)

    def body(w_ref, g_ref, m_ref, v_ref, d_ref, nm_ref, nv_ref):
        gv = g_ref[...]
        mn = ADAM_B1 * m_ref[...] + (1.0 - ADAM_B1) * gv
        vn = ADAM_B2 * v_ref[...] + (1.0 - ADAM_B2) * (gv * gv)
        m_hat = mn / (1.0 - ADAM_B1 ** ADAM_STEP)
        v_hat = vn / (1.0 - ADAM_B2 ** ADAM_STEP)
        d_ref[...] = -ADAM_LR * (m_hat / (jnp.sqrt(v_hat) + ADAM_EPS) + ADAM_WD * w_ref[...])
        nm_ref[...] = mn
        nv_ref[...] = vn

    rows = pl.BlockSpec((tr, C), lambda i: (i, 0))
    return pl.pallas_call(
        body, grid=(R // tr,), in_specs=[rows] * 4, out_specs=[rows] * 3,
        out_shape=[SDS((R, C), F32)] * 3, name=name, compiler_params=_cp("parallel"))(w, g, m, v)


def _sum_slots(r, *, tr, name):
    S, R, C = r.shape
    assert R % tr == 0, (R, tr)

    def body(r_ref, o_ref):
        acc = r_ref[0]
        for s in range(1, S):
            acc = acc + r_ref[s]
        o_ref[...] = acc

    return pl.pallas_call(
        body, grid=(R // tr,), in_specs=[pl.BlockSpec((S, tr, C), lambda i: (0, i, 0))],
        out_specs=pl.BlockSpec((tr, C), lambda i: (i, 0)), out_shape=SDS((R, C), F32),
        name=name, compiler_params=_cp("parallel"))(r)


def _pair_add(g4, h, pos, *, name):
    A, _, r, C = g4.shape
    cs = C if A == N_CHIPS else C // N_CHIPS
    tr = _row_tile(r, 256)
    if A == N_CHIPS:
        g_map, h_map = (lambda t, i, pos: (t, pos[1], i, 0)), (lambda t, i, pos: (t, i, 0))
    else:
        g_map, h_map = (lambda t, i, pos: (0, pos[1], i, t)), (lambda t, i, pos: (0, i, t))

    def body(pos_ref, g_ref, h_ref, o_ref):
        o_ref[...] = (g_ref[...] + h_ref[...]).astype(o_ref.dtype)

    grid_spec = pltpu.PrefetchScalarGridSpec(
        num_scalar_prefetch=1, grid=(N_CHIPS, r // tr),
        in_specs=[pl.BlockSpec((None, None, tr, cs), g_map), pl.BlockSpec((None, tr, cs), h_map)],
        out_specs=pl.BlockSpec((None, tr, cs), lambda t, i, pos: (t, i, 0)))
    return pl.pallas_call(body, grid_spec=grid_spec, out_shape=SDS((N_CHIPS, r, cs), COMM_DTYPE), name=name,
                          compiler_params=_cp("parallel", "parallel"))(pos, g4, h)


def _chip_sum(p, r2, f_into, pos, layer, *, name):
    _, r, cs = p.shape
    tr = _row_tile(r, 256)

    def body(pos_ref, own_ref, r_ref, *rest):
        o_ref = rest[-1]
        o_ref[...] = ((own_ref[...].astype(F32) + r_ref[0].astype(F32)) + r_ref[1].astype(F32)) + r_ref[2].astype(F32)

    in_specs = [pl.BlockSpec((None, tr, cs), lambda i, pos: (pos[0], i, 0)),
                pl.BlockSpec((3, tr, cs), lambda i, pos: (0, i, 0))]
    operands = [pos, p, r2]
    aliases = {}
    if f_into is not None:
        in_specs.append(ANY)
        operands.append(f_into)
        aliases = {3: 0}
    grid_spec = pltpu.PrefetchScalarGridSpec(
        num_scalar_prefetch=1, grid=(r // tr,), in_specs=in_specs,
        out_specs=pl.BlockSpec((None, None, tr, cs), lambda i, pos: (layer, pos[1], i, 0)))
    return pl.pallas_call(body, grid_spec=grid_spec, out_shape=SDS((DEPTH, 2, r, cs), F32), name=name,
                          input_output_aliases=aliases, compiler_params=_cp("parallel"))(*operands)


def _mesh_pos():
    return lax.axis_index("x"), lax.axis_index("y"), lax.axis_index("c")


HBM = pl.BlockSpec(memory_space=pltpu.HBM)
SEM = pl.BlockSpec(memory_space=pltpu.SEMAPHORE)
DATAFLOW = pltpu.SideEffectType.DATAFLOW_SIDE_EFFECTING
CHIP_FLIPS = (2, 1, 3)


def _chip_peers():
    x, y, c = _mesh_pos()
    return 2 * x + y, [(1 - x, y, c), (x, 1 - y, c), (1 - x, 1 - y, c)], (x, y, 1 - c), c


def _split_start(arrays, n_copies, issue, *, name, deps=()):
    k = len(arrays)
    nd = len(deps)

    def body(*refs):
        issue(refs[:k], refs[k + nd], refs[k + nd + 1])
        refs[2 * k + nd + 2][...] = jnp.zeros((8, 128), F32)

    out = pl.pallas_call(
        body, name=name,
        out_shape=(pltpu.SemaphoreType.DMA((n_copies,)), pltpu.SemaphoreType.DMA((n_copies,)),
                   *[pltpu.HBM(a.shape, a.dtype) for a in arrays], SDS((8, 128), F32)),
        in_specs=[HBM] * k + [ANY] * nd, out_specs=(SEM, SEM, *[HBM] * k, pl.BlockSpec(memory_space=pltpu.VMEM)),
        input_output_aliases={i: 2 + i for i in range(k)},
        compiler_params=pltpu.CompilerParams(has_side_effects=DATAFLOW))(
            *[pltpu.with_memory_space_constraint(a, pltpu.HBM) for a in arrays], *deps)
    return (out[0], out[1]), list(out[2:2 + k]), out[2 + k]


def _split_wait(sems, arrays, after, waits, *, name):
    k = len(arrays)
    afters = tuple(after) if isinstance(after, (tuple, list)) else (after,)

    def body(*refs):
        waits(refs[:k], refs[k], refs[k + 1])

    out = pl.pallas_call(
        body, name=name, out_shape=tuple(pltpu.HBM(a.shape, a.dtype) for a in arrays),
        in_specs=[HBM] * k + [SEM, SEM] + [ANY] * len(afters), out_specs=tuple([HBM] * k),
        input_output_aliases={i: i for i in range(k)},
        compiler_params=pltpu.CompilerParams(has_side_effects=DATAFLOW))(*arrays, sems[0], sems[1], *afters)
    return list(out)


def _wait_both(cp):
    cp.wait_send()
    cp.wait_recv()


def _cast_place(shard, pos, dtype, *, name, layer=None, slots=N_CHIPS, which=0):
    R, C = shard.shape[-2:]
    tr = R if R % 8 else _row_tile(R, 256)
    if layer is None:
        in_spec = pl.BlockSpec((tr, C), lambda i, pos: (i, 0))
    else:
        in_spec = pl.BlockSpec((None, tr, C), lambda i, pos: (layer, i, 0))

    def body(pos_ref, x_ref, o_ref):
        o_ref[...] = x_ref[...].astype(o_ref.dtype)

    grid_spec = pltpu.PrefetchScalarGridSpec(
        num_scalar_prefetch=1, grid=(R // tr,), in_specs=[in_spec],
        out_specs=pl.BlockSpec((None, tr, C), lambda i, pos: (pos[which], i, 0)))
    return pl.pallas_call(body, grid_spec=grid_spec, out_shape=SDS((slots, R, C), dtype), name=name,
                          compiler_params=_cp("parallel"))(pos, shard)


def _device_peers():
    x, y, c = _mesh_pos()
    peers = [(x ^ ((f >> 2) & 1), y ^ ((f >> 1) & 1), c ^ (f & 1)) for f in range(1, N_DEV)]
    return 4 * x + 2 * y + c, peers


class _Gather:
    def __init__(self, lands, name, deps=(), all_devices=False):
        n = len(lands)
        self.name = name
        npeer = N_DEV - 1 if all_devices else N_CHIPS - 1

        def copies(refs, ss, rs):
            me, peers = _device_peers() if all_devices else _chip_peers()[:2]
            return [pltpu.make_async_remote_copy(
                src_ref=refs[w].at[me], dst_ref=refs[w].at[me], send_sem=ss.at[npeer * w + p],
                recv_sem=rs.at[npeer * w + p], device_id=peers[p], device_id_type=MESH)
                for w in range(n) for p in range(npeer)]

        def issue(refs, ss, rs):
            for cp in copies(refs, ss, rs):
                cp.start()

        def waits(refs, ss, rs):
            for cp in copies(refs, ss, rs):
                _wait_both(cp)

        self._waits = waits
        self.sems, self.arrays, self.token = _split_start(list(lands), npeer * n, issue, name=name + "_start",
                                                          deps=deps)

    def wait(self, after):
        return _split_wait(self.sems, self.arrays, after, self._waits, name=self.name + "_wait")


def _swap_halves_start(g4s, *, name):
    n = len(g4s)
    lands = [lax.empty((g.shape[0],) + g.shape[2:], g.dtype) for g in g4s]

    def copies(refs, ss, rs):
        _, _, sibling, c = _chip_peers()
        return [pltpu.make_async_remote_copy(
            src_ref=refs[w].at[:, 1 - c], dst_ref=refs[n + w], send_sem=ss.at[w], recv_sem=rs.at[w],
            device_id=sibling, device_id_type=MESH) for w in range(n)]

    def issue(refs, ss, rs):
        for cp in copies(refs, ss, rs):
            cp.start()

    def waits(refs, ss, rs):
        for cp in copies(refs, ss, rs):
            _wait_both(cp)

    sems, arrays, token = _split_start(list(g4s) + lands, n, issue, name=name + "_start")
    return sems, arrays, token, waits


def _scatter_start(parts, *, name, deps=()):
    n = len(parts)
    lands = [lax.empty((3,) + p.shape[1:], p.dtype) for p in parts]

    def copies(refs, ss, rs):
        me, peers, _, _ = _chip_peers()
        return [pltpu.make_async_remote_copy(
            src_ref=refs[w].at[me ^ CHIP_FLIPS[p]], dst_ref=refs[n + w].at[p],
            send_sem=ss.at[3 * w + p], recv_sem=rs.at[3 * w + p], device_id=peers[p], device_id_type=MESH)
            for w in range(n) for p in range(3)]

    def issue(refs, ss, rs):
        for cp in copies(refs, ss, rs):
            cp.start()

    def waits(refs, ss, rs):
        for cp in copies(refs, ss, rs):
            _wait_both(cp)

    sems, arrays, token = _split_start(list(parts) + lands, 3 * n, issue, name=name + "_start", deps=deps)
    return sems, arrays, token, waits


def _pair_share_start(fs, layer, *, name):
    n = len(fs)

    def copies(refs, ss, rs):
        _, _, sibling, c = _chip_peers()
        return [pltpu.make_async_remote_copy(
            src_ref=refs[w].at[layer, c], dst_ref=refs[w].at[layer, c], send_sem=ss.at[w], recv_sem=rs.at[w],
            device_id=sibling, device_id_type=MESH) for w in range(n)]

    def issue(refs, ss, rs):
        for cp in copies(refs, ss, rs):
            cp.start()

    def waits(refs, ss, rs):
        for cp in copies(refs, ss, rs):
            _wait_both(cp)

    sems, arrays, token = _split_start(list(fs), n, issue, name=name + "_start")
    return sems, arrays, token, waits


BIG = ('w_in', 'w_proj_a', 'w_proj_b', 'w_proj_c', 'w_out', 'w_up', 'w_down')
BIG_SHARD_AXIS = {'w_in': 2, 'w_proj_a': 2, 'w_proj_b': 2, 'w_proj_c': 2, 'w_out': 1, 'w_up': 2, 'w_down': 1}
SMALL = ('norm1', 'q_norm', 'k_norm', 'sinks', 'w_pool', 'pool_scale', 'sgu_v_norm', 'w_s', 'b_s', 'norm2',
         'conv_b', 'conv_w')
WEIGHTS = ('norm1', 'w_in', 'q_norm', 'k_norm', 'sinks', 'w_pool', 'pool_scale', 'sgu_v_norm', 'w_s', 'b_s',
           'w_proj_a', 'w_proj_b', 'w_proj_c', 'w_out', 'norm2', 'w_up', 'conv_w', 'conv_b', 'w_down')


def _rope_tables(positions):
    inv_freq = ROPE_THETA ** (-jnp.arange(0, HEAD_DIM, 2, dtype=F32) / HEAD_DIM)
    ang = positions.astype(F32)[:, None] * inv_freq
    cos, sin = jnp.cos(ang), jnp.sin(ang)
    c = jnp.concatenate([cos, cos], axis=1)
    s = jnp.concatenate([-sin, sin], axis=1)
    return jnp.concatenate([c, c], axis=1), jnp.concatenate([s, s], axis=1)


def _block_diag4(w):
    out = jnp.zeros((POOL_WIDTH, POOL_WIDTH), w.dtype)
    for g in range(4):
        out = lax.dynamic_update_slice(out, w[g], (g * HEAD_DIM, g * HEAD_DIM))
    return out


def _local_step(x, target, cos, sin, sp, sched):
    T = x.shape[0]
    tm1 = min(1024, T)
    tm = min(512, T)
    tr = min(256, T)
    tkt = min(1024, T)
    seg = _seg_matrix(256, HEAD_DIM)
    saved = []
    xl = x
    for l in range(DEPTH):
        p = f"l{l}_"
        c = dict(
            g1=sp['norm1'][l][None], g2=sp['norm2'][l][None],
            wbd=_block_diag4(sp['w_pool'][l]).astype(MXU_DTYPE), scale=sp['pool_scale'][l][None],
            gq=jnp.tile(sp['q_norm'][l], 4)[None], gk=jnp.tile(sp['k_norm'][l], 2)[None],
            sinks=jnp.broadcast_to(sp['sinks'][l][:, None], (N_Q_HEADS, 128)),
            wtril=jnp.tril(sp['w_s'][l]).astype(MXU_DTYPE),
            bexp=jnp.repeat(sp['b_s'][l].T, HEAD_DIM, axis=1), vn=jnp.tile(sp['sgu_v_norm'][l], 4)[None],
            cb=sp['conv_b'][l][None])
        c['w_in'] = sched.weight('w_in', l, xl)
        z, h1 = _norm_mm(xl, c['g1'], c['w_in'], tm=tm1, tn=1152, name=p + "in_proj",
                         deps=sched.start_tokens() if l == 0 else ())
        pa = _pool_fwd(z, c['wbd'], c['scale'], tr=tr, name=p + "pool")
        q, k, v = _qkv_prep(z, cos, sin, c['gq'], c['gk'], seg, tr=tr, name=p + "qkv_prep")
        at = _attn_fwd(q, k, v, c['sinks'], name=p + "attn")
        sg = _sgu_fwd(z, c['wtril'], c['bexp'], c['vn'], seg, tr=tr, name=p + "sgu")
        for n in ('w_proj_a', 'w_proj_b', 'w_proj_c', 'w_out'):
            c[n] = sched.weight(n, l, (pa, at, sg))
        merged, y3 = _merge_fwd(pa, at, sg, c['w_proj_a'], c['w_proj_b'], c['w_proj_c'], z,
                                tm=tm, tn=512, name=p + "merge")
        x1 = _mm(merged, c['w_out'], mode='nn', add=xl, tm=tm, tn=D_MODEL, tk=D_MODEL, name=p + "out_proj")
        for n in ('w_up', 'conv_w', 'w_down'):
            c[n] = sched.weight(n, l, x1)
        up, h2 = _norm_mm(x1, c['g2'], c['w_up'], tm=tm1, tn=1408, name=p + "up_proj")
        act, conv_out = _conv_act_fwd(up, c['conv_w'], c['cb'], tr=tr, tc=1408, name=p + "conv_act")
        saved.append(dict(c, x=xl, h1=h1, z=z, pa=pa, q=q, k=k, v=v, at=at, sg=sg, merged=merged, y3=y3,
                          x1=x1, h2=h2, up=up, act=act, conv_out=conv_out))
        if l < DEPTH - 1:
            xl = _mm(act, c['w_down'], mode='nn', add=x1, tm=tm, tn=D_MODEL, tk=D_FF, name=p + "down_proj")
        else:
            loss_row, dx, dxb = _down_proj_loss(act, c['w_down'], x1, target, tm=tm, name=p + "down_proj_loss")

    gs = {n: [None] * DEPTH for n in SMALL}
    for l in reversed(range(DEPTH)):
        p = f"l{l}_b_"
        s = saved[l]
        gb = {}
        dact = _mm(dxb, s['w_down'], mode='nt', tm=tm1, tn=1408, tk=D_MODEL, name=p + "down_dx")
        gb['w_down'] = _mm(s['act'], dxb, mode='tn', tm=1408, tn=D_MODEL, tk=tkt, name=p + "down_dw")
        toks = sched.slot(l, 'down', gb['w_down'])
        dup, dwg, dwv, dbg, dbv = _conv_act_bwd(s['up'], s['conv_out'], s['conv_w'], dact, tr=min(512, T), tc=256,
                                                name=p + "conv_act", deps=toks)
        gs['conv_w'][l] = jnp.concatenate([dwg, dwv], axis=1)
        gs['conv_b'][l] = jnp.concatenate([dbg, dbv], axis=1)[0]
        toks = sched.slot(l, 'conv', dup)
        for half in range(2):
            gb['w_up'] = _mm(s['h2'], dup, mode='tn', b_lead=half, tm=D_MODEL, tn=1408, tk=tkt,
                             out_into=gb.get('w_up'), out_joff=2 * half, out_n=2 * D_FF, name=p + f"up_dw{half}",
                             deps=toks if half == 0 else ())
        toks = sched.slot(l, 'ffn', gb['w_up'], gb)
        dx1, dx1b, dg2 = _mm_nt_sharded_rms(dup, s['w_up'], s['x1'], s['g2'], dx, tm=min(256, T),
                                            name=p + "up_dx_rms2", deps=toks)
        gs['norm2'][l] = dg2[0]
        gb['w_out'] = _mm(s['merged'], dx1b, mode='tn', tm=D_MODEL, tn=D_MODEL, tk=tkt, name=p + "out_dw")
        (dz, dpa, dat, dsg, gb['w_proj_a'], gb['w_proj_b'], gb['w_proj_c']) = _out_dx_merge_bwd(
            dx1b, s['w_out'], s['y3'], s['z'], [s['w_proj_a'], s['w_proj_b'], s['w_proj_c']],
            [s['pa'], s['at'], s['sg']], tm=tm, tn=512, name=p + "out_dx_merge")
        toks = sched.slot(l, 'mid', dz)
        dq, dkc, dkp, dvc, dvp, dsk = _attn_bwd(s['q'], s['k'], s['v'], s['sinks'], dat, name=p + "attn", deps=toks)
        gs['sinks'][l] = dsk[:, 0]
        toks = sched.slot(l, 'attn', dq)
        dz, dgq, dgk, dwbd, dsc = _mixer_ab_bwd(s['z'], cos, sin, s['gq'], s['gk'], seg, dq, dkc, dkp, dvc, dvp,
                                                dpa, s['wbd'], s['scale'], dz, tr=tr, name=p + "qkv_pool", deps=toks)
        gs['q_norm'][l] = dgq[0, :HEAD_DIM]
        gs['k_norm'][l] = dgk[0, :HEAD_DIM]
        gs['w_pool'][l] = jnp.stack([dwbd[g * HEAD_DIM:(g + 1) * HEAD_DIM, g * HEAD_DIM:(g + 1) * HEAD_DIM]
                                     for g in range(4)])
        gs['pool_scale'][l] = dsc[0]
        dz, dws, dbrows, dvn = _sgu_bwd(s['z'], s['wtril'], s['bexp'], s['vn'], seg, dsg, dz, tr=tr, name=p + "sgu")
        gs['w_s'][l] = dws
        gs['b_s'][l] = dbrows[:, ::HEAD_DIM].T
        gs['sgu_v_norm'][l] = dvn[0, :HEAD_DIM]
        gb['w_in'] = _mm(s['h1'], dz, mode='tn', tm=D_MODEL, tn=1152, tk=tkt, name=p + "in_dw")
        toks = sched.slot(l, 'mix', gb['w_in'], gb)
        dx, dxb, dg1 = _mm_nt_sharded_rms(dz, s['w_in'], s['x'], s['g1'], dx1, tm=min(256, T),
                                          name=p + "in_dx_rms1", deps=toks)
        gs['norm1'][l] = dg1[0]
    gs = {n: jnp.stack(v) for n, v in gs.items()}
    return loss_row, dx, gs


GROUP_F = ('w_down', 'w_up')
GROUP_M = ('w_out', 'w_proj_a', 'w_proj_b', 'w_proj_c', 'w_in')
ROW_SHARDED = ('w_out', 'w_down')

REDUCE_PLAN = {
    (1, 'ffn'): (('S1', 'F', 1),),
    (1, 'mid'): (('W1', 'F', 1),),
    (1, 'mix'): (('S1', 'M', 1),),
    (0, 'down'): (('W1', 'M', 1),),
    (0, 'conv'): (('W2', 'F', 1),),
    (0, 'ffn'): (('S1', 'F', 0), ('W3', 'F', 1)),
    (0, 'mid'): (('W1', 'F', 0),),
    (0, 'attn'): (('W2', 'M', 1),),
    (0, 'mix'): (('S1', 'M', 0), ('W3', 'M', 1)),
}
REDUCE_TAIL_A = (('W1', 'M', 0), ('W2', 'F', 0))
REDUCE_TAIL_B = (('W3', 'F', 0),)
REDUCE_TAIL_C = (('W2', 'M', 0), ('W3', 'M', 0))


class _Comm:
    def __init__(self, w, pos):
        self.pos = pos
        groups = {'a': [('w_in', 0)],
                  'b': [(n, 0) for n in ('w_proj_a', 'w_proj_b', 'w_proj_c', 'w_out')],
                  'c': [(n, 0) for n in ('w_up', 'conv_w', 'w_down')],
                  'd': [(n, 1) for n in BIG] + [('conv_w', 1)]}
        self.gathers, self.group_of, self.weights = {}, {}, {}
        self.tokens = []
        for g, ks in groups.items():
            lands = [_cast_place(w[n], pos, F32 if n == 'conv_w' else MXU_DTYPE, layer=l, name=f"gw_place_{n}{l}")
                     for n, l in ks]
            self.gathers[g] = (_Gather(lands, "gw_" + g, deps=self.tokens[-1:]), ks)
            self.tokens.append(self.gathers[g][0].token)
            self.group_of.update({k: g for k in ks})
        self.red = {}
        self.final = {}

    def start_tokens(self):
        return self.tokens[-1:]

    def weight(self, name, layer, after):
        if (name, layer) not in self.weights:
            gather, ks = self.gathers[self.group_of[(name, layer)]]
            for (n, l), full in zip(ks, gather.wait(after)):
                if n == 'conv_w' or n.startswith('w_proj'):
                    full = full.transpose(1, 0, 2).reshape(full.shape[1], -1)
                elif n in ROW_SHARDED:
                    full = full.reshape(-1, full.shape[2])
                self.weights[(n, l)] = full
        return self.weights[(name, layer)]

    def slot(self, layer, slot, after, grads=None):
        tokens = []
        for step, grp, lyr in REDUCE_PLAN.get((layer, slot), ()):
            tok = self._step(step, grp, lyr, after, grads)
            if tok is not None:
                tokens.append(tok)
        return tokens

    def tail(self, steps, after, deps=()):
        toks = (self._step(step, grp, lyr, after, None, deps) for step, grp, lyr in steps)
        return [t for t in toks if t is not None]

    def shards(self):
        return {n: f.reshape(DEPTH, 2 * f.shape[2], f.shape[3]) for n, f in self.final.items()}

    def _step(self, step, grp, layer, after, grads, deps=()):
        names = GROUP_F if grp == 'F' else GROUP_M
        tag = f"{grp.lower()}{layer}"
        st = self.red.setdefault((grp, layer), {})
        n = len(names)
        if step == 'S1':
            g4s = []
            for nm in names:
                g = grads[nm]
                R, C = g.shape
                g4s.append(g.reshape(N_CHIPS, 2, R // (2 * N_CHIPS), C) if nm in ROW_SHARDED
                           else g.reshape(1, 2, R // 2, C))
            st['s1'] = _swap_halves_start(g4s, name="rs1_" + tag)
            return st['s1'][2]
        if step == 'W1':
            sems, arrays, _, waits = st.pop('s1')
            arrays = _split_wait(sems, arrays, after, waits, name=f"rs1_{tag}_wait")
            parts = [_pair_add(arrays[i], arrays[n + i], self.pos, name=f"pair_add_{tag}_{names[i]}")
                     for i in range(n)]
            st['s2'] = _scatter_start(parts, name="rs2_" + tag, deps=deps)
            return st['s2'][2]
        if step == 'W2':
            sems, arrays, _, waits = st.pop('s2')
            arrays = _split_wait(sems, arrays, after, waits, name=f"rs2_{tag}_wait")
            fs = [_chip_sum(arrays[i], arrays[n + i], self.final.get(names[i]), self.pos, layer,
                            name=f"chip_sum_{tag}_{names[i]}") for i in range(n)]
            st['s3'] = _pair_share_start(fs, layer, name="rs3_" + tag)
            return st['s3'][2]
        sems, arrays, _, waits = st.pop('s3')
        self.final.update(zip(names, _split_wait(sems, arrays, after, waits, name=f"rs3_{tag}_wait")))
        return None


def _pack(arrays):
    flat = []
    for a in arrays:
        f = a.reshape(-1).astype(F32)
        flat.append(jnp.pad(f, (0, (-f.shape[0]) % 128)))
    v = jnp.concatenate(flat)
    v = jnp.pad(v, (0, (-v.shape[0]) % 1024))
    return v.reshape(-1, 128)


def _unpack(pack, shapes):
    v = pack.reshape(-1)
    out, off = [], 0
    for shp in shapes:
        nel = int(np.prod(shp))
        out.append(v[off:off + nel].reshape(shp))
        off += nel + (-nel) % 128
    return out


def kernel(x, positions, norm1, w_in, q_norm, k_norm, sinks, w_pool, pool_scale, sgu_v_norm, w_s, b_s, w_proj_a, w_proj_b, w_proj_c, w_out, norm2, w_up, conv_w, conv_b, w_down, loss_target, m_norm1, m_w_in, m_q_norm, m_k_norm, m_sinks, m_w_pool, m_pool_scale, m_sgu_v_norm, m_w_s, m_b_s, m_w_proj_a, m_w_proj_b, m_w_proj_c, m_w_out, m_norm2, m_w_up, m_conv_w, m_conv_b, m_w_down, v_norm1, v_w_in, v_q_norm, v_k_norm, v_sinks, v_w_pool, v_pool_scale, v_sgu_v_norm, v_w_s, v_b_s, v_w_proj_a, v_w_proj_b, v_w_proj_c, v_w_out, v_norm2, v_w_up, v_conv_w, v_conv_b, v_w_down):
    w = dict(norm1=norm1, w_in=w_in, q_norm=q_norm, k_norm=k_norm, sinks=sinks, w_pool=w_pool, pool_scale=pool_scale,
             sgu_v_norm=sgu_v_norm, w_s=w_s, b_s=b_s, w_proj_a=w_proj_a, w_proj_b=w_proj_b, w_proj_c=w_proj_c,
             w_out=w_out, norm2=norm2, w_up=w_up, conv_w=conv_w, conv_b=conv_b, w_down=w_down)
    m = dict(norm1=m_norm1, w_in=m_w_in, q_norm=m_q_norm, k_norm=m_k_norm, sinks=m_sinks, w_pool=m_w_pool,
             pool_scale=m_pool_scale, sgu_v_norm=m_sgu_v_norm, w_s=m_w_s, b_s=m_b_s, w_proj_a=m_w_proj_a,
             w_proj_b=m_w_proj_b, w_proj_c=m_w_proj_c, w_out=m_w_out, norm2=m_norm2, w_up=m_w_up, conv_w=m_conv_w,
             conv_b=m_conv_b, w_down=m_w_down)
    v = dict(norm1=v_norm1, w_in=v_w_in, q_norm=v_q_norm, k_norm=v_k_norm, sinks=v_sinks, w_pool=v_w_pool,
             pool_scale=v_pool_scale, sgu_v_norm=v_sgu_v_norm, w_s=v_w_s, b_s=v_b_s, w_proj_a=v_w_proj_a,
             w_proj_b=v_w_proj_b, w_proj_c=v_w_proj_c, w_out=v_w_out, norm2=v_norm2, w_up=v_w_up, conv_w=v_conv_w,
             conv_b=v_conv_b, w_down=v_w_down)
    chip = 2 * lax.axis_index("x") + lax.axis_index("y")
    core = lax.axis_index("c")

    pos = jnp.stack([chip, core, 2 * chip + core]).astype(jnp.int32)
    comm = _Comm(w, pos)

    cos, sin = _rope_tables(positions[0])
    sp = {n: w[n] for n in SMALL if n != 'conv_w'}
    loss_row, dx, gs = _local_step(x[0], loss_target[0], cos, sin, sp, comm)
    loss = lax.psum(loss_row[0, 0], ("x", "y", "c"))

    delta, new_m, new_v = {}, {}, {}

    def adamw_big(names, grads):
        for n in names:
            shp = w[n].shape
            two_d = lambda a: a.reshape(shp[0] * shp[1], shp[2])
            d, nm, nv = _adamw(two_d(w[n]), two_d(grads[n]), two_d(m[n]), two_d(v[n]),
                               tr=_row_tile(shp[0] * shp[1], 256), name=f"adamw_{n}")
            delta[n], new_m[n], new_v[n] = d.reshape(shp), nm.reshape(shp), nv.reshape(shp)

    small_shapes = [gs[n].shape for n in SMALL]
    small_pack = _pack([gs[n] for n in SMALL])
    small = _Gather([_cast_place(small_pack, pos, F32, slots=N_DEV, which=2, name="small_place")], "small_gather",
                    all_devices=True)
    toks = comm.tail(REDUCE_TAIL_A[:1], (dx, small.token))
    comm.tail(REDUCE_TAIL_A[1:], (dx, *toks))
    comm.tail(REDUCE_TAIL_B, dx)
    adamw_big(GROUP_F, comm.shards())
    red = _sum_slots(small.wait(new_v[GROUP_F[-1]])[0], tr=small_pack.shape[0], name="small_sum")
    g_small = dict(zip(SMALL, _unpack(red, small_shapes)))
    comm.tail(REDUCE_TAIL_C, red)
    grads = comm.shards()
    grads.update(g_small)
    shard_cols = conv_w.shape[2]
    grads['conv_w'] = lax.dynamic_slice_in_dim(g_small['conv_w'], chip * shard_cols, shard_cols, axis=2)

    adamw_big(GROUP_M, grads)
    shapes = [w[n].shape for n in SMALL]
    packs = [_pack([src[n] for n in SMALL]) for src in (w, grads, m, v)]
    d, nm, nv = _adamw(*packs, tr=packs[0].shape[0], name="adamw_small")
    for dst, src in ((delta, d), (new_m, nm), (new_v, nv)):
        dst.update(zip(SMALL, _unpack(src, shapes)))

    return (loss, dx[None], *[grads[n] for n in WEIGHTS], *[delta[n] for n in WEIGHTS],
            *[new_m[n] for n in WEIGHTS], *[new_v[n] for n in WEIGHTS])
```

```python
import functools
import math

import numpy as np
import jax
import jax.numpy as jnp
from jax import lax
from jax.experimental import pallas as pl
from jax.experimental.pallas import tpu as pltpu

F32 = jnp.float32
MXU_DTYPE = jnp.bfloat16
COMM_DTYPE = jnp.bfloat16

D_MODEL = 1024
DEPTH = 2
HEAD_DIM = 64
POOL_WINDOWS = (2, 4, 8, 16)
POOL_WIDTH = 256
N_Q_HEADS = 8
ATTN_BLOCK = 128
ATTN_WIDTH = 512
KV_WIDTH = 128
CHUNK = 128
SGU_WIDTH = 256
IN_COLS = 4608
GATE_COL0 = 1536
D_FF = 2816
ROPE_THETA = 10000.0
EPS = 1e-6
ADAM_LR, ADAM_B1, ADAM_B2, ADAM_EPS, ADAM_WD, ADAM_STEP = 0.001, 0.9, 0.999, 1e-08, 0.01, 10

N_CHIPS = 4
N_DEV = 8
VMEM_LIMIT_BYTES = 56 * 1024 * 1024
NEG_BIG = -1e30
MESH = pl.DeviceIdType.MESH
ANY = pl.BlockSpec(memory_space=pl.ANY)

SDS = jax.ShapeDtypeStruct


def _cp(*sem):
    return pltpu.CompilerParams(dimension_semantics=sem, vmem_limit_bytes=VMEM_LIMIT_BYTES)


def _dot(a, b, dims=((1,), (0,))):
    return lax.dot_general(a.astype(MXU_DTYPE), b.astype(MXU_DTYPE), (dims, ((), ())),
                           preferred_element_type=F32)


NT = ((1,), (1,))
TN = ((0,), (0,))


def _split_dot(x, m):
    hi = x.astype(MXU_DTYPE)
    lo = (x - hi.astype(F32)).astype(MXU_DTYPE)
    return _dot(hi, m) + _dot(lo, m)


def _seg_matrix(width, seg):
    idx = np.arange(width) // seg
    return jnp.asarray((idx[:, None] == idx[None, :]).astype(np.float32), dtype=MXU_DTYPE)


def _lane(shape):
    return lax.broadcasted_iota(jnp.int32, shape, len(shape) - 1)


def _row(shape):
    return lax.broadcasted_iota(jnp.int32, shape, 0)


def _full(shape):
    nd = len(shape)
    return pl.BlockSpec(shape, lambda *_: (0,) * nd)


def _gelu(x):
    k = math.sqrt(2.0 / math.pi)
    th = jnp.tanh(k * (x + 0.044715 * (x * x * x)))
    return 0.5 * x * (1.0 + th)


def _gelu_and_grad(x):
    k = math.sqrt(2.0 / math.pi)
    x2 = x * x
    th = jnp.tanh(k * (x + 0.044715 * (x2 * x)))
    g = 0.5 * x * (1.0 + th)
    dg = 0.5 * (1.0 + th) + 0.5 * x * (1.0 - th * th) * (k * (1.0 + 3.0 * 0.044715 * x2))
    return g, dg


def _sigmoid(x):
    return 0.5 * jnp.tanh(0.5 * x) + 0.5


def _swap_halves(x):
    w = x.shape[-1]
    first = (_lane(x.shape) % HEAD_DIM) < (HEAD_DIM // 2)
    return jnp.where(first, pltpu.roll(x, w - HEAD_DIM // 2, 1), pltpu.roll(x, HEAD_DIM // 2, 1))


def _tile_lanes(x, reps):
    return x if reps == 1 else jnp.concatenate([x] * reps, axis=1)


def _fold_lanes(x, period):
    w = x.shape[-1]
    while w > period:
        w //= 2
        x = x + pltpu.roll(x, w, 1)
    return x


def _mm(a, b, *, mode, tm, tn, tk, out_dtype=F32, add=None, name,
        a_lead=None, b_lead=None, b_sharded=False, out_into=None,
        b_koff=0, out_joff=0, out_n=None, deps=()):
    ash = a.shape[1:] if a_lead is not None else a.shape
    bsh = b.shape[1:] if b_lead is not None else b.shape
    if b_sharded:
        bsh = (b.shape[1], N_CHIPS * b.shape[2])
    if mode == 'nn':
        (M, K), (K2, N) = ash, bsh
    elif mode == 'nt':
        (M, K), (N, K2) = ash, bsh
    else:
        (K, M), (K2, N) = ash, bsh
    assert K == K2 or (mode == 'nt' and K2 > K), (ash, bsh, mode)
    assert M % tm == 0 and N % tn == 0 and K % tk == 0, (M, N, K, tm, tn, tk)
    nk = K // tk
    dims = {'nn': ((1,), (0,)), 'nt': NT, 'tn': TN}[mode]

    def lead(spec_shape, imap, lead_idx):
        if lead_idx is None:
            return pl.BlockSpec(spec_shape, imap)
        return pl.BlockSpec((None,) + spec_shape, lambda i, j, k: (lead_idx,) + imap(i, j, k))

    if mode == 'tn':
        a_spec = lead((tk, tm), lambda i, j, k: (k, i), a_lead)
    else:
        a_spec = lead((tm, tk), lambda i, j, k: (i, k), a_lead)
    if b_sharded:
        per = b.shape[2] // (tk if mode == 'nt' else tn)
        assert per * (tk if mode == 'nt' else tn) == b.shape[2] and mode != 'tn'
        if mode == 'nt':
            b_spec = pl.BlockSpec((None, tn, tk), lambda i, j, k: ((k + b_koff) // per, j, (k + b_koff) % per))
        else:
            b_spec = pl.BlockSpec((None, tk, tn), lambda i, j, k: (j // per, k, j % per))
    elif mode == 'nt':
        b_spec = lead((tn, tk), lambda i, j, k: (j, k + b_koff), b_lead)
    else:
        b_spec = lead((tk, tn), lambda i, j, k: (k, j), b_lead)
    o_spec = pl.BlockSpec((tm, tn), lambda i, j, k: (i, j + out_joff))
    n_out = N if out_n is None else out_n
    in_specs = [a_spec, b_spec]
    operands = [a, b]
    if add is not None:
        in_specs.append(pl.BlockSpec((tm, tn), lambda i, j, k: (i, j)))
        operands.append(add)
    aliases = {}
    if out_into is not None:
        in_specs.append(ANY)
        operands.append(out_into)
        aliases = {len(operands) - 1: 0}
    in_specs += [ANY] * len(deps)
    operands += list(deps)
    has_add = add is not None
    acc_in_out = nk > 1 and out_dtype == F32

    def body(*refs):
        a_ref, b_ref = refs[0], refs[1]
        pos = 2
        add_ref = None
        if has_add:
            add_ref = refs[pos]
            pos += 1
        if out_into is not None:
            pos += 1
        pos += len(deps)
        o_ref = refs[pos]
        acc_ref = refs[pos + 1] if (nk > 1 and not acc_in_out) else None
        p = _dot(a_ref[...], b_ref[...], dims)
        if nk == 1:
            if has_add:
                p = p + add_ref[...]
            o_ref[...] = p.astype(o_ref.dtype)
            return
        k = pl.program_id(2)
        tgt = o_ref if acc_in_out else acc_ref

        @pl.when(k == 0)
        def _():
            tgt[...] = p + add_ref[...] if has_add else p

        @pl.when(k > 0)
        def _():
            tgt[...] += p

        if not acc_in_out:
            @pl.when(k == nk - 1)
            def _():
                o_ref[...] = acc_ref[...].astype(o_ref.dtype)

    out_shape = SDS((M, n_out), out_dtype)
    scratch = [pltpu.VMEM((tm, tn), F32)] if (nk > 1 and not acc_in_out) else []
    return pl.pallas_call(
        body, grid=(M // tm, N // tn, nk), in_specs=in_specs, out_specs=o_spec, out_shape=out_shape,
        scratch_shapes=scratch, input_output_aliases=aliases, name=name,
        compiler_params=_cp("parallel", "parallel", "arbitrary"))(*operands)


def _rms_bwd_rows(xv, g, dh, dres):
    r = lax.rsqrt(jnp.mean(xv * xv, axis=-1, keepdims=True) + EPS)
    xh = xv * r
    gy = dh * g
    dx = r * (gy - xh * jnp.mean(xh * gy, axis=-1, keepdims=True)) + dres
    return dx, jnp.sum(dh * xh, axis=0, keepdims=True)


def _mm_nt_sharded_rms(a, b, x, g, dres, *, tm, name, deps=()):
    a3 = a if a.ndim == 3 else a[None]
    A, M, ka = a3.shape
    S, N, ns = b.shape
    per = S // A
    assert ka == per * ns and M % tm == 0 and N == x.shape[1], (a3.shape, b.shape, x.shape)

    def body(a_ref, b_ref, x_ref, g_ref, dres_ref, dx_ref, dxb_ref, dg_ref):
        acc = None
        for s in range(S):
            lo = (s % per) * ns
            p = _dot(a_ref[s // per, :, lo:lo + ns], b_ref[s], NT)
            acc = p if acc is None else acc + p
        dx, dg = _rms_bwd_rows(x_ref[...], g_ref[...], acc, dres_ref[...])
        dx_ref[...] = dx
        dxb_ref[...] = dx.astype(dxb_ref.dtype)

        @pl.when(pl.program_id(0) == 0)
        def _():
            dg_ref[...] = jnp.zeros_like(dg_ref)
        dg_ref[...] += dg

    rows = pl.BlockSpec((tm, N), lambda i: (i, 0))
    return pl.pallas_call(
        _after(body, 5, deps), grid=(M // tm,),
        in_specs=[pl.BlockSpec((A, tm, ka), lambda i: (0, i, 0)), pl.BlockSpec((S, N, ns), lambda i: (0, 0, 0)),
                  rows, _full((1, N)), rows] + [ANY] * len(deps),
        out_specs=[rows, rows, _full((1, N))],
        out_shape=[SDS((M, N), F32), SDS((M, N), MXU_DTYPE), SDS((1, N), F32)], name=name,
        compiler_params=_cp("arbitrary"))(a3, b, x, g, dres, *deps)


def _norm_mm(x, g, b, *, tm, tn, name, deps=()):
    M, K = x.shape
    S, K2, ns = b.shape
    per = ns // tn
    assert K == K2 and per * tn == ns and M % tm == 0, (x.shape, b.shape)

    def body(x_ref, g_ref, b_ref, o_ref, h_ref):
        @pl.when(pl.program_id(1) == 0)
        def _():
            xv = x_ref[...]
            r = lax.rsqrt(jnp.mean(xv * xv, axis=-1, keepdims=True) + EPS)
            h_ref[...] = (xv * r * g_ref[...]).astype(h_ref.dtype)
        o_ref[...] = _dot(h_ref[...], b_ref[...])

    return pl.pallas_call(
        _after(body, 3, deps), grid=(M // tm, S * per),
        in_specs=[pl.BlockSpec((tm, K), lambda i, j: (i, 0)), _full((1, K)),
                  pl.BlockSpec((None, K, tn), lambda i, j: (j // per, 0, j % per))] + [ANY] * len(deps),
        out_specs=[pl.BlockSpec((tm, tn), lambda i, j: (i, j)), pl.BlockSpec((tm, K), lambda i, j: (i, 0))],
        out_shape=[SDS((M, S * ns), F32), SDS((M, K), MXU_DTYPE)], name=name,
        compiler_params=_cp("parallel", "arbitrary"))(x, g, b, *deps)


def _after(body, n_in, deps):
    nd = len(deps)
    if nd == 0:
        return body
    return lambda *refs: body(*refs[:n_in], *refs[n_in + nd:])


def _down_proj_loss(act, w, x1, target, *, tm, name):
    T, K = act.shape
    D = w.shape[1]

    def body(a_ref, w_ref, x_ref, t_ref, loss_ref, dy_ref, dyb_ref):
        i = pl.program_id(0)
        d = (x_ref[...] + _dot(a_ref[...], w_ref[...])) - t_ref[...]
        dy = d * (1.0 / D)
        dy_ref[...] = dy
        dyb_ref[...] = dy.astype(dyb_ref.dtype)
        part = jnp.sum(jnp.sum(d * d, axis=1, keepdims=True), axis=0, keepdims=True) * (0.5 / D)

        @pl.when(i == 0)
        def _():
            loss_ref[...] = jnp.zeros_like(loss_ref)
        loss_ref[...] += jnp.broadcast_to(part, loss_ref.shape)

    rows = pl.BlockSpec((tm, D), lambda i: (i, 0))
    return pl.pallas_call(
        body, grid=(T // tm,), in_specs=[pl.BlockSpec((tm, K), lambda i: (i, 0)), _full((K, D)), rows, rows],
        out_specs=[_full((1, 128)), rows, rows],
        out_shape=[SDS((1, 128), F32), SDS((T, D), F32), SDS((T, D), MXU_DTYPE)],
        name=name, compiler_params=_cp("arbitrary"))(act, w, x1, target)


def _pool_lane_consts(shape):
    lane = _lane(shape)
    grp = lane // (POOL_WIDTH // 4)
    win = jnp.where(grp == 0, 2, jnp.where(grp == 1, 4, jnp.where(grp == 2, 8, 16)))
    return grp, win


def _pool_select(grp, s2, s4, s8, s16):
    return jnp.where(grp == 0, s2, jnp.where(grp == 1, s4, jnp.where(grp == 2, s8, s16)))


def _pool_diff(xe, row0, tr):
    s2 = xe + pltpu.roll(xe, 1, 0)
    s4 = s2 + pltpu.roll(s2, 2, 0)
    s8 = s4 + pltpu.roll(s4, 4, 0)
    s16 = s8 + pltpu.roll(s8, 8, 0)
    shape = (tr, POOL_WIDTH)
    grp, win = _pool_lane_consts(shape)
    sums = _pool_select(grp, s2[16:], s4[16:], s8[16:], s16[16:])
    t = row0 + _row(shape)
    cnt = jnp.minimum(t + 1, win).astype(F32)
    return sums / cnt - xe[16:]


def _pool_fwd(z, wbd, scale, *, tr, name):
    T = z.shape[0]
    hb = tr // 16

    def body(x_ref, xp_ref, w_ref, s_ref, o_ref):
        i = pl.program_id(0)
        halo = jnp.where(i == 0, 0.0, xp_ref[...])
        diff = _pool_diff(jnp.concatenate([halo, x_ref[...]], axis=0), i * tr, tr)
        o_ref[...] = (_dot(diff, w_ref[...]) * s_ref[...]).astype(o_ref.dtype)

    return pl.pallas_call(
        body, grid=(T // tr,),
        in_specs=[pl.BlockSpec((tr, POOL_WIDTH), lambda i: (i, 0)),
                  pl.BlockSpec((16, POOL_WIDTH), lambda i: (jnp.maximum(i * hb - 1, 0), 0)),
                  _full((POOL_WIDTH, POOL_WIDTH)), _full((1, POOL_WIDTH))],
        out_specs=pl.BlockSpec((tr, POOL_WIDTH), lambda i: (i, 0)),
        out_shape=SDS((T, POOL_WIDTH), MXU_DTYPE), name=name, compiler_params=_cp("parallel"))(z, z, wbd, scale)


def _pool_bwd_tile(i, n, tr, x, xprev, dpa, dpa_next, wbd, scale):
    halo = jnp.where(i == 0, 0.0, xprev)
    diff = _pool_diff(jnp.concatenate([halo, x], axis=0), i * tr, tr)
    mixed = _dot(diff, wbd)
    dscale = jnp.sum(dpa * mixed, axis=0, keepdims=True)
    dnext = jnp.where(i == n - 1, 0.0, dpa_next)
    dmix_e = jnp.concatenate([dpa, dnext], axis=0) * scale
    ddiff_e = _dot(dmix_e, wbd, NT)
    dwbd = _dot(diff, dmix_e[:tr], TN)
    shape = (tr + 16, POOL_WIDTH)
    grp, win = _pool_lane_consts(shape)
    t = i * tr + _row(shape)
    e = ddiff_e / jnp.minimum(t + 1, win).astype(F32)
    nrow = tr + 16
    a2 = e + pltpu.roll(e, nrow - 1, 0)
    a4 = a2 + pltpu.roll(a2, nrow - 2, 0)
    a8 = a4 + pltpu.roll(a4, nrow - 4, 0)
    a16 = a8 + pltpu.roll(a8, nrow - 8, 0)
    dx = _pool_select(grp, a2, a4, a8, a16)[:tr] - ddiff_e[:tr]
    return dx, dwbd, dscale


def _norm_rope(x, g, cos, sin_signed, seg):
    reps = x.shape[1] // 128
    ms = _split_dot(x * x, seg) * (1.0 / HEAD_DIM)
    r = lax.rsqrt(ms + EPS)
    xn = x * r * g
    c, s = _tile_lanes(cos, reps), _tile_lanes(sin_signed, reps)
    return xn * c + _swap_halves(xn) * s


def _norm_rope_bwd(x, g, cos, sin_signed, seg, dout):
    reps = x.shape[1] // 128
    c, s = _tile_lanes(cos, reps), _tile_lanes(sin_signed, reps)
    dxn = dout * c + _swap_halves(dout * s)
    ms = _split_dot(x * x, seg) * (1.0 / HEAD_DIM)
    r = lax.rsqrt(ms + EPS)
    xh = x * r
    gy = dxn * g
    dx = r * (gy - xh * (_split_dot(xh * gy, seg) * (1.0 / HEAD_DIM)))
    dg = jnp.sum(dxn * xh, axis=0, keepdims=True)
    return dx, dg


def _dup_heads(k):
    first = _lane(k.shape) < HEAD_DIM
    kr = pltpu.roll(k, HEAD_DIM, 1)
    return jnp.concatenate([jnp.where(first, k, kr), jnp.where(first, kr, k)], axis=1)


def _qkv_prep(z, cos, sin_signed, gq, gk, seg, *, tr, name):
    T = z.shape[0]

    def body(qa_ref, qb_ref, kv_ref, c_ref, s_ref, gq_ref, gk_ref, seg_ref, q_ref, k_ref, v_ref):
        c, s, seg_m = c_ref[...], s_ref[...], seg_ref[...]
        scale = HEAD_DIM ** -0.5
        qa = _norm_rope(qa_ref[...], gq_ref[...], c, s, seg_m) * scale
        qb = _norm_rope(qb_ref[...], gq_ref[...], c, s, seg_m) * scale
        q_ref[...] = jnp.concatenate([qa, qb], axis=1).astype(q_ref.dtype)
        kv = kv_ref[...]
        k = _norm_rope(kv[:, :KV_WIDTH], gk_ref[...], c, s, seg_m[:128, :128])
        k_ref[...] = _dup_heads(k).astype(k_ref.dtype)
        v_ref[...] = _dup_heads(kv[:, KV_WIDTH:]).astype(v_ref.dtype)

    col = lambda j: pl.BlockSpec((tr, 256), lambda i: (i, j))
    tab = pl.BlockSpec((tr, 128), lambda i: (i, 0))
    return pl.pallas_call(
        body, grid=(T // tr,),
        in_specs=[col(1), col(2), col(3), tab, tab, _full((1, 256)), _full((1, 128)), _full((256, 256))],
        out_specs=[pl.BlockSpec((tr, 512), lambda i: (i, 0)), col(0), col(0)],
        out_shape=[SDS((T, 512), MXU_DTYPE), SDS((T, 256), MXU_DTYPE), SDS((T, 256), MXU_DTYPE)],
        name=name, compiler_params=_cp("parallel"))(z, z, z, cos, sin_signed, gq, gk, seg)


GROUP_HEADS = 4
GROUP_ROWS = GROUP_HEADS * ATTN_BLOCK
ALL_ROWS = N_Q_HEADS * ATTN_BLOCK


def _attn_mask(n):
    qi = _row((ALL_ROWS, 2 * ATTN_BLOCK)) % ATTN_BLOCK
    kj = _lane((ALL_ROWS, 2 * ATTN_BLOCK))
    return (kj > qi) & (kj <= qi + ATTN_BLOCK) & ((kj >= ATTN_BLOCK) | (n > 0))


def _stack_heads(x, g):
    first = _lane((ATTN_BLOCK, 128)) < HEAD_DIM
    parts = []
    for pair in (2 * g, 2 * g + 1):
        x128 = x[:, 128 * pair:128 * (pair + 1)]
        zero = jnp.zeros_like(x128)
        parts += [jnp.where(first, x128, zero), jnp.where(first, zero, x128)]
    return jnp.concatenate(parts, axis=0)


def _unstack_heads(y):
    first = _lane((ATTN_BLOCK, 128)) < HEAD_DIM
    b = ATTN_BLOCK
    return jnp.concatenate([jnp.where(first, y[0:b], y[b:2 * b]), jnp.where(first, y[2 * b:3 * b], y[3 * b:4 * b])],
                           axis=1)


def _sink_col(sk_ref):
    return jnp.concatenate([jnp.broadcast_to(sk_ref[h:h + 1, 0:1], (ATTN_BLOCK, 1)) for h in range(N_Q_HEADS)],
                           axis=0)


def _by_group(a8, b2, dims=((1,), (0,))):
    return jnp.concatenate([_dot(a8[:GROUP_ROWS], b2[:, :128], dims), _dot(a8[GROUP_ROWS:], b2[:, 128:], dims)],
                           axis=0)


def _softmax_exp(q8, k2, mask, sink):
    s = jnp.where(mask, _by_group(q8, k2, NT), NEG_BIG)
    m = jnp.maximum(jnp.max(s, axis=1, keepdims=True), sink)
    p = jnp.exp(s - m)
    ps = jnp.exp(sink - m)
    return p, ps, 1.0 / (jnp.sum(p, axis=1, keepdims=True) + ps)


def _attn_fwd(q, k, v, sinks_b, *, name):
    T = q.shape[0]
    nb = T // ATTN_BLOCK

    def body(q_ref, kc_ref, kp_ref, vc_ref, vp_ref, sk_ref, o_ref):
        n = pl.program_id(0)
        mask = _attn_mask(n)
        k2 = jnp.concatenate([kp_ref[...], kc_ref[...]], axis=0)
        v2 = jnp.concatenate([vp_ref[...], vc_ref[...]], axis=0)
        qv = q_ref[...]
        q8 = jnp.concatenate([_stack_heads(qv, 0), _stack_heads(qv, 1)], axis=0)
        p, _, inv = _softmax_exp(q8, k2, mask, _sink_col(sk_ref))
        o8 = _by_group(p, v2) * inv
        o_ref[...] = jnp.concatenate([_unstack_heads(o8[:GROUP_ROWS]), _unstack_heads(o8[GROUP_ROWS:])],
                                     axis=1).astype(o_ref.dtype)

    cur = lambda w: pl.BlockSpec((ATTN_BLOCK, w), lambda n: (n, 0))
    prev = lambda w: pl.BlockSpec((ATTN_BLOCK, w), lambda n: (jnp.maximum(n - 1, 0), 0))
    return pl.pallas_call(
        body, grid=(nb,),
        in_specs=[cur(512), cur(256), prev(256), cur(256), prev(256), _full((8, 128))],
        out_specs=cur(512), out_shape=SDS((T, 512), MXU_DTYPE), name=name,
        compiler_params=_cp("parallel"))(q, k, k, v, v, sinks_b)


def _attn_bwd(q, k, v, sinks_b, do, *, name, deps=()):
    T = q.shape[0]
    nb = T // ATTN_BLOCK

    def body(q_ref, kc_ref, kp_ref, vc_ref, vp_ref, sk_ref, do_ref,
             dq_ref, dkc_ref, dkp_ref, dvc_ref, dvp_ref, dsk_ref):
        n = pl.program_id(0)
        mask = _attn_mask(n)
        k2 = jnp.concatenate([kp_ref[...], kc_ref[...]], axis=0)
        v2 = jnp.concatenate([vp_ref[...], vc_ref[...]], axis=0)
        qv = q_ref[...]
        dov = do_ref[...]

        @pl.when(n == 0)
        def _():
            dsk_ref[...] = jnp.zeros_like(dsk_ref)

        q8 = jnp.concatenate([_stack_heads(qv, 0), _stack_heads(qv, 1)], axis=0)
        do8 = jnp.concatenate([_stack_heads(dov, 0), _stack_heads(dov, 1)], axis=0)
        p, ps, inv = _softmax_exp(q8, k2, mask, _sink_col(sk_ref))
        pn = p * inv
        delta = jnp.sum(do8 * _by_group(pn, v2), axis=1, keepdims=True)
        ds = pn * (_by_group(do8, v2, NT) - delta)
        dq8 = _by_group(ds, k2)
        dq_ref[...] = jnp.concatenate([_unstack_heads(dq8[:GROUP_ROWS]), _unstack_heads(dq8[GROUP_ROWS:])], axis=1)
        dk = jnp.concatenate([_dot(ds[:GROUP_ROWS], q8[:GROUP_ROWS], TN), _dot(ds[GROUP_ROWS:], q8[GROUP_ROWS:], TN)],
                             axis=1)
        dv = jnp.concatenate([_dot(pn[:GROUP_ROWS], do8[:GROUP_ROWS], TN),
                              _dot(pn[GROUP_ROWS:], do8[GROUP_ROWS:], TN)], axis=1)
        wsink = (ps * inv) * delta
        for h in range(N_Q_HEADS):
            dsink = -jnp.sum(wsink[ATTN_BLOCK * h:ATTN_BLOCK * (h + 1)], axis=0, keepdims=True)
            dsk_ref[h:h + 1, :] += jnp.broadcast_to(dsink, (1, 128))
        dkp_ref[...] = dk[:ATTN_BLOCK]
        dkc_ref[...] = dk[ATTN_BLOCK:]
        dvp_ref[...] = dv[:ATTN_BLOCK]
        dvc_ref[...] = dv[ATTN_BLOCK:]

    cur = lambda w: pl.BlockSpec((ATTN_BLOCK, w), lambda n: (n, 0))
    prev = lambda w: pl.BlockSpec((ATTN_BLOCK, w), lambda n: (jnp.maximum(n - 1, 0), 0))
    f = lambda w: SDS((T, w), F32)
    return pl.pallas_call(
        _after(body, 7, deps), grid=(nb,),
        in_specs=[cur(512), cur(256), prev(256), cur(256), prev(256), _full((8, 128)), cur(512)] + [ANY] * len(deps),
        out_specs=[cur(512), cur(256), cur(256), cur(256), cur(256), _full((8, 128))],
        out_shape=[f(512), f(256), f(256), f(256), f(256), SDS((8, 128), F32)],
        name=name, compiler_params=_cp("arbitrary"))(q, k, k, v, v, sinks_b, do, *deps)


def _mixer_ab_bwd(z, cos, sin_signed, gq, gk, seg, dq, dkc, dkp, dvc, dvp, dpa, wbd, scale, dz, *, tr, name, deps=()):
    T = z.shape[0]
    n = T // tr
    hb = tr // 16
    ab = tr // ATTN_BLOCK

    def unfold(cur, nxt_tile, nxt_halo, i):
        nxt = jnp.concatenate([nxt_tile[ATTN_BLOCK:], jnp.where(i == n - 1, 0.0, nxt_halo)], axis=0)
        tot = cur + nxt
        first = _lane((tr, 128)) < HEAD_DIM
        a = tot[:, :128]
        b = tot[:, 128:]
        a = a + pltpu.roll(a, HEAD_DIM, 1)
        b = b + pltpu.roll(b, HEAD_DIM, 1)
        return jnp.where(first, a, b)

    def body(xp_ref, xpp_ref, qa_ref, qb_ref, kv_ref, c_ref, s_ref, gq_ref, gk_ref, seg_ref,
             dq_ref, dkc_ref, dkp_ref, dkh_ref, dvc_ref, dvp_ref, dvh_ref, dpa_ref, dpan_ref, w_ref, sc_ref, _dz_in,
             dz_ref, dgq_ref, dgk_ref, dw_ref, dsc_ref):
        i = pl.program_id(0)
        c, s, seg_m = c_ref[...], s_ref[...], seg_ref[...]
        scale_q = HEAD_DIM ** -0.5
        dqv = dq_ref[...] * scale_q
        dxa, dga = _norm_rope_bwd(qa_ref[...], gq_ref[...], c, s, seg_m, dqv[:, :256])
        dxb, dgb = _norm_rope_bwd(qb_ref[...], gq_ref[...], c, s, seg_m, dqv[:, 256:])
        dk = unfold(dkc_ref[...], dkp_ref[...], dkh_ref[...], i)
        dv = unfold(dvc_ref[...], dvp_ref[...], dvh_ref[...], i)
        kv = kv_ref[...]
        dxk, dgk = _norm_rope_bwd(kv[:, :KV_WIDTH], gk_ref[...], c, s, seg_m[:128, :128], dk)
        dxp, dwbd, dscale = _pool_bwd_tile(i, n, tr, xp_ref[...], xpp_ref[...], dpa_ref[...], dpan_ref[...],
                                           w_ref[...], sc_ref[...])
        dz_ref[...] = jnp.concatenate([dxp, dxa, dxb, dxk, dv], axis=1).astype(dz_ref.dtype)

        @pl.when(i == 0)
        def _():
            dgq_ref[...] = jnp.zeros_like(dgq_ref)
            dgk_ref[...] = jnp.zeros_like(dgk_ref)
            dw_ref[...] = jnp.zeros_like(dw_ref)
            dsc_ref[...] = jnp.zeros_like(dsc_ref)
        dgq_ref[...] += _fold_lanes(dga + dgb, HEAD_DIM)
        dgk_ref[...] += _fold_lanes(dgk, HEAD_DIM)
        dw_ref[...] += dwbd
        dsc_ref[...] += dscale

    col = lambda j: pl.BlockSpec((tr, 256), lambda i: (i, j))
    rows = lambda w: pl.BlockSpec((tr, w), lambda i: (i, 0))
    nxt_blk = pl.BlockSpec((ATTN_BLOCK, 256), lambda i: (jnp.minimum((i + 1) * ab, T // ATTN_BLOCK - 1), 0))
    prev16 = pl.BlockSpec((16, 256), lambda i: (jnp.maximum(i * hb - 1, 0), 0))
    next16 = pl.BlockSpec((16, 256), lambda i: (jnp.minimum((i + 1) * hb, T // 16 - 1), 0))
    return pl.pallas_call(
        _after(body, 22, deps), grid=(n,),
        in_specs=[col(0), prev16, col(1), col(2), col(3), rows(128), rows(128),
                  _full((1, 256)), _full((1, 128)), _full((256, 256)),
                  rows(512), rows(256), rows(256), nxt_blk, rows(256), rows(256), nxt_blk,
                  rows(256), next16, _full((256, 256)), _full((1, 256)), ANY] + [ANY] * len(deps),
        out_specs=[rows(1024), _full((1, 256)), _full((1, 128)), _full((256, 256)), _full((1, 256))],
        out_shape=[SDS((T, IN_COLS), MXU_DTYPE), SDS((1, 256), F32), SDS((1, 128), F32),
                   SDS((256, 256), F32), SDS((1, 256), F32)],
        input_output_aliases={21: 0}, name=name, compiler_params=_cp("arbitrary"))(
            z, z, z, z, z, cos, sin_signed, gq, gk, seg, dq, dkc, dkp, dkp, dvc, dvp, dvp, dpa, dpa, wbd, scale, dz,
            *deps)


def _sgu_common(zu, zv, vn, seg):
    u, du = _gelu_and_grad(zu)
    gv, dgv = _gelu_and_grad(zv)
    ms = _split_dot(gv * gv, seg) * (1.0 / HEAD_DIM)
    r = lax.rsqrt(ms + EPS)
    xh = gv * r
    return u, du, dgv, r, xh, xh * vn


def _sgu_fwd(z, wtril, bexp, vn, seg, *, tr, name):
    T = z.shape[0]
    nch = tr // CHUNK

    def body(u_ref, v_ref, w_ref, b_ref, vn_ref, seg_ref, o_ref):
        u, _, _, _, _, vg = _sgu_common(u_ref[...], v_ref[...], vn_ref[...], seg_ref[...])
        grp = _lane((CHUNK, SGU_WIDTH)) // HEAD_DIM
        outs = []
        for ch in range(nch):
            vc = vg[ch * CHUNK:(ch + 1) * CHUNK]
            s = b_ref[...]
            for g in range(4):
                s = s + jnp.where(grp == g, _dot(w_ref[g], vc), 0.0)
            outs.append(u[ch * CHUNK:(ch + 1) * CHUNK] * s)
        o_ref[...] = jnp.concatenate(outs, axis=0).astype(o_ref.dtype)

    col = lambda j: pl.BlockSpec((tr, 256), lambda i: (i, j))
    return pl.pallas_call(
        body, grid=(T // tr,),
        in_specs=[col(4), col(5), _full((4, CHUNK, CHUNK)), _full((CHUNK, 256)), _full((1, 256)), _full((256, 256))],
        out_specs=col(0), out_shape=SDS((T, SGU_WIDTH), MXU_DTYPE), name=name,
        compiler_params=_cp("parallel"))(z, z, wtril, bexp, vn, seg)


def _sgu_bwd(z, wtril, bexp, vn, seg, dsg, dz, *, tr, name):
    T = z.shape[0]
    nch = tr // CHUNK

    def body(u_ref, v_ref, w_ref, b_ref, vn_ref, seg_ref, d_ref, _dz_in, dz_ref, dw_ref, db_ref, dvn_ref):
        i = pl.program_id(0)
        seg_m = seg_ref[...]
        vn_v = vn_ref[...]
        u, du, dgv, r, xh, vg = _sgu_common(u_ref[...], v_ref[...], vn_v, seg_m)
        d = d_ref[...]
        grp = _lane((CHUNK, SGU_WIDTH)) // HEAD_DIM
        tril = _row((CHUNK, CHUNK)) >= _lane((CHUNK, CHUNK))

        @pl.when(i == 0)
        def _():
            dw_ref[...] = jnp.zeros_like(dw_ref)
            db_ref[...] = jnp.zeros_like(db_ref)
            dvn_ref[...] = jnp.zeros_like(dvn_ref)

        dus, dvgs = [], []
        for ch in range(nch):
            sl = slice(ch * CHUNK, (ch + 1) * CHUNK)
            vc = vg[sl]
            s = b_ref[...]
            for g in range(4):
                s = s + jnp.where(grp == g, _dot(w_ref[g], vc), 0.0)
            dus.append(d[sl] * s)
            ds = d[sl] * u[sl]
            db_ref[...] += _split_dot(ds, seg_m)
            dvg = jnp.zeros((CHUNK, SGU_WIDTH), F32)
            for g in range(4):
                dsm = jnp.where(grp == g, ds, 0.0)
                dvg = dvg + jnp.where(grp == g, _dot(w_ref[g], ds, TN), 0.0)
                dw_ref[g] += jnp.where(tril, _dot(dsm, vc, NT), 0.0)
            dvgs.append(dvg)
        dup = jnp.concatenate(dus, axis=0)
        dvg = jnp.concatenate(dvgs, axis=0)
        dvn_ref[...] += _fold_lanes(jnp.sum(dvg * xh, axis=0, keepdims=True), HEAD_DIM)
        gy = dvg * vn_v
        dgvv = r * (gy - xh * (_split_dot(xh * gy, seg_m) * (1.0 / HEAD_DIM)))
        dz_ref[...] = jnp.concatenate([dup * du, dgvv * dgv], axis=1).astype(dz_ref.dtype)

    col = lambda j: pl.BlockSpec((tr, 256), lambda i: (i, j))
    return pl.pallas_call(
        body, grid=(T // tr,),
        in_specs=[col(4), col(5), _full((4, CHUNK, CHUNK)), _full((CHUNK, 256)), _full((1, 256)), _full((256, 256)),
                  col(0), ANY],
        out_specs=[pl.BlockSpec((tr, 512), lambda i: (i, 2)), _full((4, CHUNK, CHUNK)), _full((CHUNK, 256)),
                   _full((1, 256))],
        out_shape=[SDS((T, IN_COLS), MXU_DTYPE), SDS((4, CHUNK, CHUNK), F32), SDS((CHUNK, 256), F32),
                   SDS((1, 256), F32)],
        input_output_aliases={7: 0}, name=name, compiler_params=_cp("arbitrary"))(
            z, z, wtril, bexp, vn, seg, dsg, dz)


def _merge_fwd(pa, at, sg, wa, wb, wc, z, x, w_out, *, tm, tn, name):
    T = pa.shape[0]
    gb = GATE_COL0 // tn
    nb = D_MODEL // tn

    def body(pa_ref, at_ref, sg_ref, wa_ref, wb_ref, wc_ref, g0_ref, g1_ref, g2_ref, x_ref, wo_ref,
             m_ref, y_ref, x1_ref):
        j = pl.program_id(1)
        acc = None
        for idx, (op_ref, w_ref, g_ref) in enumerate(((pa_ref, wa_ref, g0_ref), (at_ref, wb_ref, g1_ref),
                                                      (sg_ref, wc_ref, g2_ref))):
            y = _dot(op_ref[...], w_ref[...])
            y_ref[idx] = y.astype(y_ref.dtype)
            t = _sigmoid(g_ref[...]) * y
            acc = t if acc is None else acc + t
        merged = acc.astype(m_ref.dtype)
        m_ref[...] = merged
        p = _dot(merged, wo_ref[...])

        @pl.when(j == 0)
        def _():
            x1_ref[...] = x_ref[...] + p

        @pl.when(j > 0)
        def _():
            x1_ref[...] += p

    op = lambda w: pl.BlockSpec((tm, w), lambda i, j: (i, 0))
    wt = lambda k: pl.BlockSpec((k, tn), lambda i, j: (0, j))
    gate = lambda b: pl.BlockSpec((tm, tn), lambda i, j: (i, gb + b * nb + j))
    return pl.pallas_call(
        body, grid=(T // tm, nb),
        in_specs=[op(256), op(512), op(256), wt(256), wt(512), wt(256), gate(0), gate(1), gate(2),
                  op(D_MODEL), pl.BlockSpec((tn, D_MODEL), lambda i, j: (j, 0))],
        out_specs=[pl.BlockSpec((tm, tn), lambda i, j: (i, j)), pl.BlockSpec((3, tm, tn), lambda i, j: (0, i, j)),
                   op(D_MODEL)],
        out_shape=[SDS((T, D_MODEL), MXU_DTYPE), SDS((3, T, D_MODEL), MXU_DTYPE), SDS((T, D_MODEL), F32)],
        name=name, compiler_params=_cp("parallel", "arbitrary"))(pa, at, sg, wa, wb, wc, z, z, z, x, w_out)


def _out_dx_merge_bwd(dxb, w_out, y, z, ws, xs, *, tm, tn, name):
    T = dxb.shape[0]
    gb = GATE_COL0 // tn
    nb = D_MODEL // tn
    nr = T // tm
    widths = [w.shape[0] for w in ws]

    def body(dx_ref, w_ref, y_ref, g_ref, *refs):
        w_refs, x_refs = refs[0:3], refs[3:6]
        dz_ref, dx_refs, dw_refs = refs[6], refs[7:10], refs[10:13]
        dm_ref, acc_refs = refs[13], refs[14:17]
        i, b, j = pl.program_id(0), pl.program_id(1), pl.program_id(2)

        @pl.when((b == 0) & (j == 0))
        def _():
            dm = _dot(dx_ref[...], w_ref[...], NT)
            for jj in range(nb):
                dm_ref[jj] = dm[:, jj * tn:(jj + 1) * tn]

        d = dm_ref[j]
        g = _sigmoid(g_ref[...])
        dy = (d * g).astype(MXU_DTYPE)
        dz_ref[...] = (d * y_ref[...].astype(F32) * g * (1.0 - g)).astype(dz_ref.dtype)
        for branch in range(3):
            @pl.when(b == branch)
            def _():
                p = _dot(dy, w_refs[branch][...], NT)
                q = _dot(x_refs[branch][...], dy, TN)

                @pl.when(j == 0)
                def _():
                    dx_refs[branch][...] = p

                @pl.when(j > 0)
                def _():
                    dx_refs[branch][...] += p

                @pl.when(i == 0)
                def _():
                    acc_refs[branch][j] = q

                @pl.when(i > 0)
                def _():
                    acc_refs[branch][j] += q

        @pl.when((i == nr - 1) & (b == 2) & (j == nb - 1))
        def _():
            for branch in range(3):
                for jj in range(nb):
                    dw_refs[branch][:, jj * tn:(jj + 1) * tn] = acc_refs[branch][jj]

    wspec = lambda k: pl.BlockSpec((k, tn), lambda i, b, j: (0, j))
    rows = lambda k: pl.BlockSpec((tm, k), lambda i, b, j: (i, 0))
    return pl.pallas_call(
        body, grid=(nr, 3, nb),
        in_specs=[rows(D_MODEL), _full((D_MODEL, D_MODEL)),
                  pl.BlockSpec((None, tm, tn), lambda i, b, j: (b, i, j)),
                  pl.BlockSpec((tm, tn), lambda i, b, j: (i, gb + b * nb + j))]
        + [wspec(k) for k in widths] + [rows(k) for k in widths],
        out_specs=[pl.BlockSpec((tm, tn), lambda i, b, j: (i, gb + b * nb + j))]
        + [rows(k) for k in widths] + [_full((k, D_MODEL)) for k in widths],
        out_shape=[SDS((T, IN_COLS), MXU_DTYPE)] + [SDS((T, k), F32) for k in widths]
        + [SDS((k, D_MODEL), F32) for k in widths],
        scratch_shapes=[pltpu.VMEM((nb, tm, tn), F32)] + [pltpu.VMEM((nb, k, tn), F32) for k in widths],
        name=name, compiler_params=_cp("arbitrary", "arbitrary", "arbitrary"))(dxb, w_out, y, z, *ws, *xs)


def _conv3(xe, w, b):
    return (w[0:1] * pltpu.roll(xe, 2, 0) + w[1:2] * pltpu.roll(xe, 1, 0) + w[2:3] * xe)[8:] + b


def _conv_act_fwd(up, cw, cb, *, tr, tc, name):
    T = up.shape[0]
    nc = D_FF // tc
    hb = tr // 8

    def body(ug_ref, ugp_ref, uv_ref, uvp_ref, wg_ref, wv_ref, bg_ref, bv_ref, o_ref):
        i = pl.program_id(1)
        first = i == 0
        cg = _conv3(jnp.concatenate([jnp.where(first, 0.0, ugp_ref[...]), ug_ref[...]], axis=0), wg_ref[...], bg_ref[...])
        cv = _conv3(jnp.concatenate([jnp.where(first, 0.0, uvp_ref[...]), uv_ref[...]], axis=0), wv_ref[...], bv_ref[...])
        o_ref[...] = (cg * _sigmoid(cg) * cv).astype(o_ref.dtype)

    tile = lambda off: pl.BlockSpec((tr, tc), lambda j, i: (i, off + j))
    prev = lambda off: pl.BlockSpec((8, tc), lambda j, i: (jnp.maximum(i * hb - 1, 0), off + j))
    par = lambda rows, off: pl.BlockSpec((rows, tc), lambda j, i: (0, off + j))
    return pl.pallas_call(
        body, grid=(nc, T // tr),
        in_specs=[tile(0), prev(0), tile(nc), prev(nc), par(3, 0), par(3, nc), par(1, 0), par(1, nc)],
        out_specs=pl.BlockSpec((tr, tc), lambda j, i: (i, j)),
        out_shape=SDS((T, D_FF), MXU_DTYPE), name=name,
        compiler_params=_cp("parallel", "parallel"))(up, up, up, up, cw, cw, cb, cb)


def _conv_act_bwd(up, cw, cb, dact, *, tr, tc, name, deps=()):
    T = up.shape[0]
    nc = D_FF // tc
    hb = tr // 8
    nr = T // tr

    def body(ug_ref, ugp_ref, ugn_ref, uv_ref, uvp_ref, uvn_ref, da_ref, dan_ref, wg_ref, wv_ref, bg_ref, bv_ref,
             du_ref, dwg_ref, dwv_ref, dbg_ref, dbv_ref):
        i = pl.program_id(1)
        first, last = i == 0, i == nr - 1
        da = jnp.concatenate([da_ref[...], jnp.where(last, 0.0, dan_ref[...])], axis=0)
        uge = jnp.concatenate([jnp.where(first, 0.0, ugp_ref[...]), ug_ref[...], ugn_ref[...]], axis=0)
        uve = jnp.concatenate([jnp.where(first, 0.0, uvp_ref[...]), uv_ref[...], uvn_ref[...]], axis=0)
        wg, wv = wg_ref[...], wv_ref[...]
        ug1, ug2 = pltpu.roll(uge, 1, 0)[8:], pltpu.roll(uge, 2, 0)[8:]
        uv1, uv2 = pltpu.roll(uve, 1, 0)[8:], pltpu.roll(uve, 2, 0)[8:]
        cg = wg[0:1] * ug2 + wg[1:2] * ug1 + wg[2:3] * uge[8:] + bg_ref[...]
        cv = wv[0:1] * uv2 + wv[1:2] * uv1 + wv[2:3] * uve[8:] + bv_ref[...]
        sg = _sigmoid(cg)
        dcg = da * cv * (sg * (1.0 + cg * (1.0 - sg)))
        dcv = da * (cg * sg)
        nrow = tr + 8

        def back(dc, w):
            return (w[2:3] * dc + w[1:2] * pltpu.roll(dc, nrow - 1, 0) + w[0:1] * pltpu.roll(dc, nrow - 2, 0))[:tr]

        du_ref[0] = back(dcg, wg).astype(du_ref.dtype)
        du_ref[1] = back(dcv, wv).astype(du_ref.dtype)

        def wgrad(dc, u0, u1, u2):
            d = dc[:tr]
            rows = [jnp.sum(d * u2[:tr], axis=0, keepdims=True), jnp.sum(d * u1[:tr], axis=0, keepdims=True),
                    jnp.sum(d * u0[8:8 + tr], axis=0, keepdims=True)]
            return jnp.concatenate(rows, axis=0), jnp.sum(d, axis=0, keepdims=True)

        dwg, dbg = wgrad(dcg, uge, ug1, ug2)
        dwv, dbv = wgrad(dcv, uve, uv1, uv2)

        @pl.when(first)
        def _():
            dwg_ref[...] = jnp.zeros_like(dwg_ref)
            dwv_ref[...] = jnp.zeros_like(dwv_ref)
            dbg_ref[...] = jnp.zeros_like(dbg_ref)
            dbv_ref[...] = jnp.zeros_like(dbv_ref)
        dwg_ref[...] += dwg
        dwv_ref[...] += dwv
        dbg_ref[...] += dbg
        dbv_ref[...] += dbv

    tile = lambda off: pl.BlockSpec((tr, tc), lambda j, i: (i, off + j))
    prev = lambda off: pl.BlockSpec((8, tc), lambda j, i: (jnp.maximum(i * hb - 1, 0), off + j))
    nxt = lambda off: pl.BlockSpec((8, tc), lambda j, i: (jnp.minimum((i + 1) * hb, T // 8 - 1), off + j))
    par = lambda rows, off: pl.BlockSpec((rows, tc), lambda j, i: (0, off + j))
    acc = lambda rows: pl.BlockSpec((rows, tc), lambda j, i: (0, j))
    return pl.pallas_call(
        _after(body, 12, deps), grid=(nc, nr),
        in_specs=[tile(0), prev(0), nxt(0), tile(nc), prev(nc), nxt(nc), tile(0), nxt(0),
                  par(3, 0), par(3, nc), par(1, 0), par(1, nc)] + [ANY] * len(deps),
        out_specs=[pl.BlockSpec((2, tr, tc), lambda j, i: (0, i, j)), acc(3), acc(3), acc(1), acc(1)],
        out_shape=[SDS((2, T, D_FF), MXU_DTYPE), SDS((3, D_FF), F32), SDS((3, D_FF), F32),
                   SDS((1, D_FF), F32), SDS((1, D_FF), F32)],
        name=name, compiler_params=_cp("parallel", "arbitrary"))(
            up, up, up, up, up, up, dact, dact, cw, cw, cb, cb, *deps)


def _row_tile(rows, cap):
    t = min(cap, rows)
    t -= t % 8
    while rows % t:
        t -= 8
    return t


def _adamw(w, g, m, v, *, tr, name):
    R, C = w.shape
    assert R % tr == 0, (R, tr)

    def body(w_ref, g_ref, m_ref, v_ref, d_ref, nm_ref, nv_ref):
        gv = g_ref[...]
        mn = ADAM_B1 * m_ref[...] + (1.0 - ADAM_B1) * gv
        vn = ADAM_B2 * v_ref[...] + (1.0 - ADAM_B2) * (gv * gv)
        m_hat = mn / (1.0 - ADAM_B1 ** ADAM_STEP)
        v_hat = vn / (1.0 - ADAM_B2 ** ADAM_STEP)
        d_ref[...] = -ADAM_LR * (m_hat / (jnp.sqrt(v_hat) + ADAM_EPS) + ADAM_WD * w_ref[...])
        nm_ref[...] = mn
        nv_ref[...] = vn

    rows = pl.BlockSpec((tr, C), lambda i: (i, 0))
    return pl.pallas_call(
        body, grid=(R // tr,), in_specs=[rows] * 4, out_specs=[rows] * 3,
        out_shape=[SDS((R, C), F32)] * 3, name=name, compiler_params=_cp("parallel"))(w, g, m, v)


def _sum_slots(r, *, tr, name):
    S, R, C = r.shape
    assert R % tr == 0, (R, tr)

    def body(r_ref, o_ref):
        acc = r_ref[0]
        for s in range(1, S):
            acc = acc + r_ref[s]
        o_ref[...] = acc

    return pl.pallas_call(
        body, grid=(R // tr,), in_specs=[pl.BlockSpec((S, tr, C), lambda i: (0, i, 0))],
        out_specs=pl.BlockSpec((tr, C), lambda i: (i, 0)), out_shape=SDS((R, C), F32),
        name=name, compiler_params=_cp("parallel"))(r)


def _pair_add(g4, h, pos, *, name):
    A, _, r, C = g4.shape
    cs = C if A == N_CHIPS else C // N_CHIPS
    tr = _row_tile(r, 256)
    if A == N_CHIPS:
        g_map, h_map = (lambda t, i, pos: (t, pos[1], i, 0)), (lambda t, i, pos: (t, i, 0))
    else:
        g_map, h_map = (lambda t, i, pos: (0, pos[1], i, t)), (lambda t, i, pos: (0, i, t))

    def body(pos_ref, g_ref, h_ref, o_ref):
        o_ref[...] = (g_ref[...] + h_ref[...]).astype(o_ref.dtype)

    grid_spec = pltpu.PrefetchScalarGridSpec(
        num_scalar_prefetch=1, grid=(N_CHIPS, r // tr),
        in_specs=[pl.BlockSpec((None, None, tr, cs), g_map), pl.BlockSpec((None, tr, cs), h_map)],
        out_specs=pl.BlockSpec((None, tr, cs), lambda t, i, pos: (t, i, 0)))
    return pl.pallas_call(body, grid_spec=grid_spec, out_shape=SDS((N_CHIPS, r, cs), COMM_DTYPE), name=name,
                          compiler_params=_cp("parallel", "parallel"))(pos, g4, h)


def _chip_sum(p, r2, f_into, pos, layer, *, name):
    _, r, cs = p.shape
    tr = _row_tile(r, 256)

    def body(pos_ref, own_ref, r_ref, *rest):
        o_ref = rest[-1]
        o_ref[...] = ((own_ref[...].astype(F32) + r_ref[0].astype(F32)) + r_ref[1].astype(F32)) + r_ref[2].astype(F32)

    in_specs = [pl.BlockSpec((None, tr, cs), lambda i, pos: (pos[0], i, 0)),
                pl.BlockSpec((3, tr, cs), lambda i, pos: (0, i, 0))]
    operands = [pos, p, r2]
    aliases = {}
    if f_into is not None:
        in_specs.append(ANY)
        operands.append(f_into)
        aliases = {3: 0}
    grid_spec = pltpu.PrefetchScalarGridSpec(
        num_scalar_prefetch=1, grid=(r // tr,), in_specs=in_specs,
        out_specs=pl.BlockSpec((None, None, tr, cs), lambda i, pos: (layer, pos[1], i, 0)))
    return pl.pallas_call(body, grid_spec=grid_spec, out_shape=SDS((DEPTH, 2, r, cs), F32), name=name,
                          input_output_aliases=aliases, compiler_params=_cp("parallel"))(*operands)


def _mesh_pos():
    return lax.axis_index("x"), lax.axis_index("y"), lax.axis_index("c")


HBM = pl.BlockSpec(memory_space=pltpu.HBM)
SEM = pl.BlockSpec(memory_space=pltpu.SEMAPHORE)
DATAFLOW = pltpu.SideEffectType.DATAFLOW_SIDE_EFFECTING
CHIP_FLIPS = (2, 1, 3)


def _chip_peers():
    x, y, c = _mesh_pos()
    return 2 * x + y, [(1 - x, y, c), (x, 1 - y, c), (1 - x, 1 - y, c)], (x, y, 1 - c), c


def _split_start(arrays, n_copies, issue, *, name, deps=()):
    k = len(arrays)
    nd = len(deps)

    def body(*refs):
        issue(refs[:k], refs[k + nd], refs[k + nd + 1])
        refs[2 * k + nd + 2][...] = jnp.zeros((8, 128), F32)

    out = pl.pallas_call(
        body, name=name,
        out_shape=(pltpu.SemaphoreType.DMA((n_copies,)), pltpu.SemaphoreType.DMA((n_copies,)),
                   *[pltpu.HBM(a.shape, a.dtype) for a in arrays], SDS((8, 128), F32)),
        in_specs=[HBM] * k + [ANY] * nd, out_specs=(SEM, SEM, *[HBM] * k, pl.BlockSpec(memory_space=pltpu.VMEM)),
        input_output_aliases={i: 2 + i for i in range(k)},
        compiler_params=pltpu.CompilerParams(has_side_effects=DATAFLOW))(
            *[pltpu.with_memory_space_constraint(a, pltpu.HBM) for a in arrays], *deps)
    return (out[0], out[1]), list(out[2:2 + k]), out[2 + k]


def _split_wait(sems, arrays, after, waits, *, name):
    k = len(arrays)
    afters = tuple(after) if isinstance(after, (tuple, list)) else (after,)

    def body(*refs):
        waits(refs[:k], refs[k], refs[k + 1])

    out = pl.pallas_call(
        body, name=name, out_shape=tuple(pltpu.HBM(a.shape, a.dtype) for a in arrays),
        in_specs=[HBM] * k + [SEM, SEM] + [ANY] * len(afters), out_specs=tuple([HBM] * k),
        input_output_aliases={i: i for i in range(k)},
        compiler_params=pltpu.CompilerParams(has_side_effects=DATAFLOW))(*arrays, sems[0], sems[1], *afters)
    return list(out)


def _wait_both(cp):
    cp.wait_send()
    cp.wait_recv()


def _cast_place(shard, pos, dtype, *, name, layer=None, slots=N_CHIPS, which=0):
    R, C = shard.shape[-2:]
    tr = R if R % 8 else _row_tile(R, 256)
    if layer is None:
        in_spec = pl.BlockSpec((tr, C), lambda i, pos: (i, 0))
    else:
        in_spec = pl.BlockSpec((None, tr, C), lambda i, pos: (layer, i, 0))

    def body(pos_ref, x_ref, o_ref):
        o_ref[...] = x_ref[...].astype(o_ref.dtype)

    grid_spec = pltpu.PrefetchScalarGridSpec(
        num_scalar_prefetch=1, grid=(R // tr,), in_specs=[in_spec],
        out_specs=pl.BlockSpec((None, tr, C), lambda i, pos: (pos[which], i, 0)))
    return pl.pallas_call(body, grid_spec=grid_spec, out_shape=SDS((slots, R, C), dtype), name=name,
                          compiler_params=_cp("parallel"))(pos, shard)


def _device_peers():
    x, y, c = _mesh_pos()
    peers = [(x ^ ((f >> 2) & 1), y ^ ((f >> 1) & 1), c ^ (f & 1)) for f in range(1, N_DEV)]
    return 4 * x + 2 * y + c, peers


class _Gather:
    def __init__(self, lands, name, deps=(), all_devices=False):
        n = len(lands)
        self.name = name
        npeer = N_DEV - 1 if all_devices else N_CHIPS - 1

        def copies(refs, ss, rs):
            me, peers = _device_peers() if all_devices else _chip_peers()[:2]
            return [pltpu.make_async_remote_copy(
                src_ref=refs[w].at[me], dst_ref=refs[w].at[me], send_sem=ss.at[npeer * w + p],
                recv_sem=rs.at[npeer * w + p], device_id=peers[p], device_id_type=MESH)
                for w in range(n) for p in range(npeer)]

        def issue(refs, ss, rs):
            for cp in copies(refs, ss, rs):
                cp.start()

        def waits(refs, ss, rs):
            for cp in copies(refs, ss, rs):
                _wait_both(cp)

        self._waits = waits
        self.sems, self.arrays, self.token = _split_start(list(lands), npeer * n, issue, name=name + "_start",
                                                          deps=deps)

    def wait(self, after):
        return _split_wait(self.sems, self.arrays, after, self._waits, name=self.name + "_wait")


def _swap_halves_start(g4s, *, name):
    n = len(g4s)
    lands = [lax.empty((g.shape[0],) + g.shape[2:], g.dtype) for g in g4s]

    def copies(refs, ss, rs):
        _, _, sibling, c = _chip_peers()
        return [pltpu.make_async_remote_copy(
            src_ref=refs[w].at[:, 1 - c], dst_ref=refs[n + w], send_sem=ss.at[w], recv_sem=rs.at[w],
            device_id=sibling, device_id_type=MESH) for w in range(n)]

    def issue(refs, ss, rs):
        for cp in copies(refs, ss, rs):
            cp.start()

    def waits(refs, ss, rs):
        for cp in copies(refs, ss, rs):
            _wait_both(cp)

    sems, arrays, token = _split_start(list(g4s) + lands, n, issue, name=name + "_start")
    return sems, arrays, token, waits


def _scatter_start(parts, *, name, deps=()):
    n = len(parts)
    lands = [lax.empty((3,) + p.shape[1:], p.dtype) for p in parts]

    def copies(refs, ss, rs):
        me, peers, _, _ = _chip_peers()
        return [pltpu.make_async_remote_copy(
            src_ref=refs[w].at[me ^ CHIP_FLIPS[p]], dst_ref=refs[n + w].at[p],
            send_sem=ss.at[3 * w + p], recv_sem=rs.at[3 * w + p], device_id=peers[p], device_id_type=MESH)
            for w in range(n) for p in range(3)]

    def issue(refs, ss, rs):
        for cp in copies(refs, ss, rs):
            cp.start()

    def waits(refs, ss, rs):
        for cp in copies(refs, ss, rs):
            _wait_both(cp)

    sems, arrays, token = _split_start(list(parts) + lands, 3 * n, issue, name=name + "_start", deps=deps)
    return sems, arrays, token, waits


def _pair_share_start(fs, layer, *, name):
    n = len(fs)

    def copies(refs, ss, rs):
        _, _, sibling, c = _chip_peers()
        return [pltpu.make_async_remote_copy(
            src_ref=refs[w].at[layer, c], dst_ref=refs[w].at[layer, c], send_sem=ss.at[w], recv_sem=rs.at[w],
            device_id=sibling, device_id_type=MESH) for w in range(n)]

    def issue(refs, ss, rs):
        for cp in copies(refs, ss, rs):
            cp.start()

    def waits(refs, ss, rs):
        for cp in copies(refs, ss, rs):
            _wait_both(cp)

    sems, arrays, token = _split_start(list(fs), n, issue, name=name + "_start")
    return sems, arrays, token, waits


BIG = ('w_in', 'w_proj_a', 'w_proj_b', 'w_proj_c', 'w_out', 'w_up', 'w_down')
BIG_SHARD_AXIS = {'w_in': 2, 'w_proj_a': 2, 'w_proj_b': 2, 'w_proj_c': 2, 'w_out': 1, 'w_up': 2, 'w_down': 1}
SMALL = ('norm1', 'q_norm', 'k_norm', 'sinks', 'w_pool', 'pool_scale', 'sgu_v_norm', 'w_s', 'b_s', 'norm2',
         'conv_b', 'conv_w')
WEIGHTS = ('norm1', 'w_in', 'q_norm', 'k_norm', 'sinks', 'w_pool', 'pool_scale', 'sgu_v_norm', 'w_s', 'b_s',
           'w_proj_a', 'w_proj_b', 'w_proj_c', 'w_out', 'norm2', 'w_up', 'conv_w', 'conv_b', 'w_down')


def _rope_tables(positions):
    inv_freq = ROPE_THETA ** (-jnp.arange(0, HEAD_DIM, 2, dtype=F32) / HEAD_DIM)
    ang = positions.astype(F32)[:, None] * inv_freq
    cos, sin = jnp.cos(ang), jnp.sin(ang)
    c = jnp.concatenate([cos, cos], axis=1)
    s = jnp.concatenate([-sin, sin], axis=1)
    return jnp.concatenate([c, c], axis=1), jnp.concatenate([s, s], axis=1)


def _block_diag4(w):
    out = jnp.zeros((POOL_WIDTH, POOL_WIDTH), w.dtype)
    for g in range(4):
        out = lax.dynamic_update_slice(out, w[g], (g * HEAD_DIM, g * HEAD_DIM))
    return out


def _local_step(x, target, cos, sin, sp, sched):
    T = x.shape[0]
    tm1 = min(1024, T)
    tm = min(512, T)
    tr = min(256, T)
    tkt = min(1024, T)
    seg = _seg_matrix(256, HEAD_DIM)
    saved = []
    xl = x
    for l in range(DEPTH):
        p = f"l{l}_"
        c = dict(
            g1=sp['norm1'][l][None], g2=sp['norm2'][l][None],
            wbd=_block_diag4(sp['w_pool'][l]).astype(MXU_DTYPE), scale=sp['pool_scale'][l][None],
            gq=jnp.tile(sp['q_norm'][l], 4)[None], gk=jnp.tile(sp['k_norm'][l], 2)[None],
            sinks=jnp.broadcast_to(sp['sinks'][l][:, None], (N_Q_HEADS, 128)),
            wtril=jnp.tril(sp['w_s'][l]).astype(MXU_DTYPE),
            bexp=jnp.repeat(sp['b_s'][l].T, HEAD_DIM, axis=1), vn=jnp.tile(sp['sgu_v_norm'][l], 4)[None],
            cb=sp['conv_b'][l][None])
        c['w_in'] = sched.weight('w_in', l, xl)
        z, h1 = _norm_mm(xl, c['g1'], c['w_in'], tm=tm1, tn=1152, name=p + "in_proj",
                         deps=sched.start_tokens() if l == 0 else ())
        pa = _pool_fwd(z, c['wbd'], c['scale'], tr=tr, name=p + "pool")
        q, k, v = _qkv_prep(z, cos, sin, c['gq'], c['gk'], seg, tr=tr, name=p + "qkv_prep")
        at = _attn_fwd(q, k, v, c['sinks'], name=p + "attn")
        sg = _sgu_fwd(z, c['wtril'], c['bexp'], c['vn'], seg, tr=tr, name=p + "sgu")
        for n in ('w_proj_a', 'w_proj_b', 'w_proj_c', 'w_out'):
            c[n] = sched.weight(n, l, (pa, at, sg))
        merged, y3, x1 = _merge_fwd(pa, at, sg, c['w_proj_a'], c['w_proj_b'], c['w_proj_c'], z, xl, c['w_out'],
                                    tm=tm, tn=512, name=p + "merge_out_proj")
        for n in ('w_up', 'conv_w', 'w_down'):
            c[n] = sched.weight(n, l, x1)
        up, h2 = _norm_mm(x1, c['g2'], c['w_up'], tm=tm1, tn=1408, name=p + "up_proj")
        act = _conv_act_fwd(up, c['conv_w'], c['cb'], tr=tr, tc=1408, name=p + "conv_act")
        saved.append(dict(c, x=xl, h1=h1, z=z, pa=pa, q=q, k=k, v=v, at=at, sg=sg, merged=merged, y3=y3,
                          x1=x1, h2=h2, up=up, act=act))
        if l < DEPTH - 1:
            xl = _mm(act, c['w_down'], mode='nn', add=x1, tm=tm, tn=D_MODEL, tk=D_FF, name=p + "down_proj")
        else:
            loss_row, dx, dxb = _down_proj_loss(act, c['w_down'], x1, target, tm=tm, name=p + "down_proj_loss")

    gs = {n: [None] * DEPTH for n in SMALL}
    for l in reversed(range(DEPTH)):
        p = f"l{l}_b_"
        s = saved[l]
        gb = {}
        dact = _mm(dxb, s['w_down'], mode='nt', tm=tm1, tn=1408, tk=D_MODEL, name=p + "down_dx")
        gb['w_down'] = _mm(s['act'], dxb, mode='tn', tm=1408, tn=D_MODEL, tk=tkt, name=p + "down_dw")
        toks = sched.slot(l, 'down', gb['w_down'])
        dup, dwg, dwv, dbg, dbv = _conv_act_bwd(s['up'], s['conv_w'], s['cb'], dact, tr=min(512, T), tc=256,
                                                name=p + "conv_act", deps=toks)
        gs['conv_w'][l] = jnp.concatenate([dwg, dwv], axis=1)
        gs['conv_b'][l] = jnp.concatenate([dbg, dbv], axis=1)[0]
        toks = sched.slot(l, 'conv', dup)
        for half in range(2):
            gb['w_up'] = _mm(s['h2'], dup, mode='tn', b_lead=half, tm=D_MODEL, tn=1408, tk=tkt,
                             out_into=gb.get('w_up'), out_joff=2 * half, out_n=2 * D_FF, name=p + f"up_dw{half}",
                             deps=toks if half == 0 else ())
        toks = sched.slot(l, 'ffn', gb['w_up'], gb)
        dx1, dx1b, dg2 = _mm_nt_sharded_rms(dup, s['w_up'], s['x1'], s['g2'], dx, tm=min(256, T),
                                            name=p + "up_dx_rms2", deps=toks)
        gs['norm2'][l] = dg2[0]
        gb['w_out'] = _mm(s['merged'], dx1b, mode='tn', tm=D_MODEL, tn=D_MODEL, tk=tkt, name=p + "out_dw")
        (dz, dpa, dat, dsg, gb['w_proj_a'], gb['w_proj_b'], gb['w_proj_c']) = _out_dx_merge_bwd(
            dx1b, s['w_out'], s['y3'], s['z'], [s['w_proj_a'], s['w_proj_b'], s['w_proj_c']],
            [s['pa'], s['at'], s['sg']], tm=tm, tn=512, name=p + "out_dx_merge")
        toks = sched.slot(l, 'mid', dz)
        dq, dkc, dkp, dvc, dvp, dsk = _attn_bwd(s['q'], s['k'], s['v'], s['sinks'], dat, name=p + "attn", deps=toks)
        gs['sinks'][l] = dsk[:, 0]
        toks = sched.slot(l, 'attn', dq)
        dz, dgq, dgk, dwbd, dsc = _mixer_ab_bwd(s['z'], cos, sin, s['gq'], s['gk'], seg, dq, dkc, dkp, dvc, dvp,
                                                dpa, s['wbd'], s['scale'], dz, tr=tr, name=p + "qkv_pool", deps=toks)
        gs['q_norm'][l] = dgq[0, :HEAD_DIM]
        gs['k_norm'][l] = dgk[0, :HEAD_DIM]
        gs['w_pool'][l] = jnp.stack([dwbd[g * HEAD_DIM:(g + 1) * HEAD_DIM, g * HEAD_DIM:(g + 1) * HEAD_DIM]
                                     for g in range(4)])
        gs['pool_scale'][l] = dsc[0]
        dz, dws, dbrows, dvn = _sgu_bwd(s['z'], s['wtril'], s['bexp'], s['vn'], seg, dsg, dz, tr=tr, name=p + "sgu")
        gs['w_s'][l] = dws
        gs['b_s'][l] = dbrows[:, ::HEAD_DIM].T
        gs['sgu_v_norm'][l] = dvn[0, :HEAD_DIM]
        gb['w_in'] = _mm(s['h1'], dz, mode='tn', tm=D_MODEL, tn=1152, tk=tkt, name=p + "in_dw")
        toks = sched.slot(l, 'mix', gb['w_in'], gb)
        dx, dxb, dg1 = _mm_nt_sharded_rms(dz, s['w_in'], s['x'], s['g1'], dx1, tm=min(256, T),
                                          name=p + "in_dx_rms1", deps=toks)
        gs['norm1'][l] = dg1[0]
    gs = {n: jnp.stack(v) for n, v in gs.items()}
    return loss_row, dx, gs


GROUP_F = ('w_down', 'w_up')
GROUP_M = ('w_out', 'w_proj_a', 'w_proj_b', 'w_proj_c', 'w_in')
ROW_SHARDED = ('w_out', 'w_down')

REDUCE_PLAN = {
    (1, 'ffn'): (('S1', 'F', 1),),
    (1, 'mid'): (('W1', 'F', 1),),
    (1, 'mix'): (('S1', 'M', 1),),
    (0, 'down'): (('W1', 'M', 1),),
    (0, 'conv'): (('W2', 'F', 1),),
    (0, 'ffn'): (('S1', 'F', 0), ('W3', 'F', 1)),
    (0, 'mid'): (('W1', 'F', 0),),
    (0, 'attn'): (('W2', 'M', 1),),
    (0, 'mix'): (('S1', 'M', 0), ('W3', 'M', 1)),
}
REDUCE_TAIL_A = (('W1', 'M', 0), ('W2', 'F', 0))
REDUCE_TAIL_B = (('W3', 'F', 0),)
REDUCE_TAIL_C = (('W2', 'M', 0), ('W3', 'M', 0))


class _Comm:
    def __init__(self, w, pos):
        self.pos = pos
        groups = {'a': [('w_in', 0)],
                  'b': [(n, 0) for n in ('w_proj_a', 'w_proj_b', 'w_proj_c', 'w_out')],
                  'c': [(n, 0) for n in ('w_up', 'conv_w', 'w_down')],
                  'd': [(n, 1) for n in BIG] + [('conv_w', 1)]}
        self.gathers, self.group_of, self.weights = {}, {}, {}
        self.tokens = []
        for g, ks in groups.items():
            lands = [_cast_place(w[n], pos, F32 if n == 'conv_w' else MXU_DTYPE, layer=l, name=f"gw_place_{n}{l}")
                     for n, l in ks]
            self.gathers[g] = (_Gather(lands, "gw_" + g, deps=self.tokens[-1:]), ks)
            self.tokens.append(self.gathers[g][0].token)
            self.group_of.update({k: g for k in ks})
        self.red = {}
        self.final = {}

    def start_tokens(self):
        return self.tokens[-1:]

    def weight(self, name, layer, after):
        if (name, layer) not in self.weights:
            gather, ks = self.gathers[self.group_of[(name, layer)]]
            for (n, l), full in zip(ks, gather.wait(after)):
                if n == 'conv_w' or n.startswith('w_proj'):
                    full = full.transpose(1, 0, 2).reshape(full.shape[1], -1)
                elif n in ROW_SHARDED:
                    full = full.reshape(-1, full.shape[2])
                self.weights[(n, l)] = full
        return self.weights[(name, layer)]

    def slot(self, layer, slot, after, grads=None):
        tokens = []
        for step, grp, lyr in REDUCE_PLAN.get((layer, slot), ()):
            tok = self._step(step, grp, lyr, after, grads)
            if tok is not None:
                tokens.append(tok)
        return tokens

    def tail(self, steps, after, deps=()):
        toks = (self._step(step, grp, lyr, after, None, deps) for step, grp, lyr in steps)
        return [t for t in toks if t is not None]

    def shards(self):
        return {n: f.reshape(DEPTH, 2 * f.shape[2], f.shape[3]) for n, f in self.final.items()}

    def _step(self, step, grp, layer, after, grads, deps=()):
        names = GROUP_F if grp == 'F' else GROUP_M
        tag = f"{grp.lower()}{layer}"
        st = self.red.setdefault((grp, layer), {})
        n = len(names)
        if step == 'S1':
            g4s = []
            for nm in names:
                g = grads[nm]
                R, C = g.shape
                g4s.append(g.reshape(N_CHIPS, 2, R // (2 * N_CHIPS), C) if nm in ROW_SHARDED
                           else g.reshape(1, 2, R // 2, C))
            st['s1'] = _swap_halves_start(g4s, name="rs1_" + tag)
            return st['s1'][2]
        if step == 'W1':
            sems, arrays, _, waits = st.pop('s1')
            arrays = _split_wait(sems, arrays, after, waits, name=f"rs1_{tag}_wait")
            parts = [_pair_add(arrays[i], arrays[n + i], self.pos, name=f"pair_add_{tag}_{names[i]}")
                     for i in range(n)]
            st['s2'] = _scatter_start(parts, name="rs2_" + tag, deps=deps)
            return st['s2'][2]
        if step == 'W2':
            sems, arrays, _, waits = st.pop('s2')
            arrays = _split_wait(sems, arrays, after, waits, name=f"rs2_{tag}_wait")
            fs = [_chip_sum(arrays[i], arrays[n + i], self.final.get(names[i]), self.pos, layer,
                            name=f"chip_sum_{tag}_{names[i]}") for i in range(n)]
            st['s3'] = _pair_share_start(fs, layer, name="rs3_" + tag)
            return st['s3'][2]
        sems, arrays, _, waits = st.pop('s3')
        self.final.update(zip(names, _split_wait(sems, arrays, after, waits, name=f"rs3_{tag}_wait")))
        return None


def _pack(arrays):
    flat = []
    for a in arrays:
        f = a.reshape(-1).astype(F32)
        flat.append(jnp.pad(f, (0, (-f.shape[0]) % 128)))
    v = jnp.concatenate(flat)
    v = jnp.pad(v, (0, (-v.shape[0]) % 1024))
    return v.reshape(-1, 128)


def _unpack(pack, shapes):
    v = pack.reshape(-1)
    out, off = [], 0
    for shp in shapes:
        nel = int(np.prod(shp))
        out.append(v[off:off + nel].reshape(shp))
        off += nel + (-nel) % 128
    return out


def kernel(x, positions, norm1, w_in, q_norm, k_norm, sinks, w_pool, pool_scale, sgu_v_norm, w_s, b_s, w_proj_a, w_proj_b, w_proj_c, w_out, norm2, w_up, conv_w, conv_b, w_down, loss_target, m_norm1, m_w_in, m_q_norm, m_k_norm, m_sinks, m_w_pool, m_pool_scale, m_sgu_v_norm, m_w_s, m_b_s, m_w_proj_a, m_w_proj_b, m_w_proj_c, m_w_out, m_norm2, m_w_up, m_conv_w, m_conv_b, m_w_down, v_norm1, v_w_in, v_q_norm, v_k_norm, v_sinks, v_w_pool, v_pool_scale, v_sgu_v_norm, v_w_s, v_b_s, v_w_proj_a, v_w_proj_b, v_w_proj_c, v_w_out, v_norm2, v_w_up, v_conv_w, v_conv_b, v_w_down):
    w = dict(norm1=norm1, w_in=w_in, q_norm=q_norm, k_norm=k_norm, sinks=sinks, w_pool=w_pool, pool_scale=pool_scale,
             sgu_v_norm=sgu_v_norm, w_s=w_s, b_s=b_s, w_proj_a=w_proj_a, w_proj_b=w_proj_b, w_proj_c=w_proj_c,
             w_out=w_out, norm2=norm2, w_up=w_up, conv_w=conv_w, conv_b=conv_b, w_down=w_down)
    m = dict(norm1=m_norm1, w_in=m_w_in, q_norm=m_q_norm, k_norm=m_k_norm, sinks=m_sinks, w_pool=m_w_pool,
             pool_scale=m_pool_scale, sgu_v_norm=m_sgu_v_norm, w_s=m_w_s, b_s=m_b_s, w_proj_a=m_w_proj_a,
             w_proj_b=m_w_proj_b, w_proj_c=m_w_proj_c, w_out=m_w_out, norm2=m_norm2, w_up=m_w_up, conv_w=m_conv_w,
             conv_b=m_conv_b, w_down=m_w_down)
    v = dict(norm1=v_norm1, w_in=v_w_in, q_norm=v_q_norm, k_norm=v_k_norm, sinks=v_sinks, w_pool=v_w_pool,
             pool_scale=v_pool_scale, sgu_v_norm=v_sgu_v_norm, w_s=v_w_s, b_s=v_b_s, w_proj_a=v_w_proj_a,
             w_proj_b=v_w_proj_b, w_proj_c=v_w_proj_c, w_out=v_w_out, norm2=v_norm2, w_up=v_w_up, conv_w=v_conv_w,
             conv_b=v_conv_b, w_down=v_w_down)
    chip = 2 * lax.axis_index("x") + lax.axis_index("y")
    core = lax.axis_index("c")

    pos = jnp.stack([chip, core, 2 * chip + core]).astype(jnp.int32)
    comm = _Comm(w, pos)

    cos, sin = _rope_tables(positions[0])
    sp = {n: w[n] for n in SMALL if n != 'conv_w'}
    loss_row, dx, gs = _local_step(x[0], loss_target[0], cos, sin, sp, comm)
    loss = lax.psum(loss_row[0, 0], ("x", "y", "c"))

    delta, new_m, new_v = {}, {}, {}

    def adamw_big(names, grads):
        for n in names:
            shp = w[n].shape
            two_d = lambda a: a.reshape(shp[0] * shp[1], shp[2])
            d, nm, nv = _adamw(two_d(w[n]), two_d(grads[n]), two_d(m[n]), two_d(v[n]),
                               tr=_row_tile(shp[0] * shp[1], 256), name=f"adamw_{n}")
            delta[n], new_m[n], new_v[n] = d.reshape(shp), nm.reshape(shp), nv.reshape(shp)

    small_shapes = [gs[n].shape for n in SMALL]
    small_pack = _pack([gs[n] for n in SMALL])
    small = _Gather([_cast_place(small_pack, pos, F32, slots=N_DEV, which=2, name="small_place")], "small_gather",
                    all_devices=True)
    toks = comm.tail(REDUCE_TAIL_A[:1], (dx, small.token))
    comm.tail(REDUCE_TAIL_A[1:], (dx, *toks))
    comm.tail(REDUCE_TAIL_B, dx)
    adamw_big(GROUP_F, comm.shards())
    red = _sum_slots(small.wait(new_v[GROUP_F[-1]])[0], tr=small_pack.shape[0], name="small_sum")
    g_small = dict(zip(SMALL, _unpack(red, small_shapes)))
    comm.tail(REDUCE_TAIL_C, red)
    grads = comm.shards()
    grads.update(g_small)
    shard_cols = conv_w.shape[2]
    grads['conv_w'] = lax.dynamic_slice_in_dim(g_small['conv_w'], chip * shard_cols, shard_cols, axis=2)

    adamw_big(GROUP_M, grads)
    shapes = [w[n].shape for n in SMALL]
    packs = [_pack([src[n] for n in SMALL]) for src in (w, grads, m, v)]
    d, nm, nv = _adamw(*packs, tr=packs[0].shape[0], name="adamw_small")
    for dst, src in ((delta, d), (new_m, nm), (new_v, nv)):
        dst.update(zip(SMALL, _unpack(src, shapes)))

    return (loss, dx[None], *[grads[n] for n in WEIGHTS], *[delta[n] for n in WEIGHTS],
            *[new_m[n] for n in WEIGHTS], *[new_v[n] for n in WEIGHTS])
```

```python
import functools
import math

import numpy as np
import jax
import jax.numpy as jnp
from jax import lax
from jax.experimental import pallas as pl
from jax.experimental.pallas import tpu as pltpu

F32 = jnp.float32
MXU_DTYPE = jnp.bfloat16
COMM_DTYPE = jnp.bfloat16

D_MODEL = 1024
DEPTH = 2
HEAD_DIM = 64
POOL_WINDOWS = (2, 4, 8, 16)
POOL_WIDTH = 256
N_Q_HEADS = 8
ATTN_BLOCK = 128
ATTN_WIDTH = 512
KV_WIDTH = 128
CHUNK = 128
SGU_WIDTH = 256
IN_COLS = 4608
GATE_COL0 = 1536
D_FF = 2816
ROPE_THETA = 10000.0
EPS = 1e-6
ADAM_LR, ADAM_B1, ADAM_B2, ADAM_EPS, ADAM_WD, ADAM_STEP = 0.001, 0.9, 0.999, 1e-08, 0.01, 10

N_CHIPS = 4
N_DEV = 8
VMEM_LIMIT_BYTES = 56 * 1024 * 1024
NEG_BIG = -1e30
MESH = pl.DeviceIdType.MESH
ANY = pl.BlockSpec(memory_space=pl.ANY)

SDS = jax.ShapeDtypeStruct


def _cp(*sem):
    return pltpu.CompilerParams(dimension_semantics=sem, vmem_limit_bytes=VMEM_LIMIT_BYTES)


def _dot(a, b, dims=((1,), (0,))):
    return lax.dot_general(a.astype(MXU_DTYPE), b.astype(MXU_DTYPE), (dims, ((), ())),
                           preferred_element_type=F32)


NT = ((1,), (1,))
TN = ((0,), (0,))


def _split_dot(x, m):
    hi = x.astype(MXU_DTYPE)
    lo = (x - hi.astype(F32)).astype(MXU_DTYPE)
    return _dot(hi, m) + _dot(lo, m)


def _seg_matrix(width, seg):
    idx = np.arange(width) // seg
    return jnp.asarray((idx[:, None] == idx[None, :]).astype(np.float32), dtype=MXU_DTYPE)


def _lane(shape):
    return lax.broadcasted_iota(jnp.int32, shape, len(shape) - 1)


def _row(shape):
    return lax.broadcasted_iota(jnp.int32, shape, 0)


def _full(shape):
    nd = len(shape)
    return pl.BlockSpec(shape, lambda *_: (0,) * nd)


def _gelu(x):
    k = math.sqrt(2.0 / math.pi)
    th = jnp.tanh(k * (x + 0.044715 * (x * x * x)))
    return 0.5 * x * (1.0 + th)


def _gelu_and_grad(x):
    k = math.sqrt(2.0 / math.pi)
    x2 = x * x
    th = jnp.tanh(k * (x + 0.044715 * (x2 * x)))
    g = 0.5 * x * (1.0 + th)
    dg = 0.5 * (1.0 + th) + 0.5 * x * (1.0 - th * th) * (k * (1.0 + 3.0 * 0.044715 * x2))
    return g, dg


def _sigmoid(x):
    return 0.5 * jnp.tanh(0.5 * x) + 0.5


def _swap_halves(x):
    w = x.shape[-1]
    first = (_lane(x.shape) % HEAD_DIM) < (HEAD_DIM // 2)
    return jnp.where(first, pltpu.roll(x, w - HEAD_DIM // 2, 1), pltpu.roll(x, HEAD_DIM // 2, 1))


def _tile_lanes(x, reps):
    return x if reps == 1 else jnp.concatenate([x] * reps, axis=1)


def _fold_lanes(x, period):
    w = x.shape[-1]
    while w > period:
        w //= 2
        x = x + pltpu.roll(x, w, 1)
    return x


def _mm(a, b, *, mode, tm, tn, tk, out_dtype=F32, add=None, name,
        a_lead=None, b_lead=None, b_sharded=False, out_into=None,
        b_koff=0, out_joff=0, out_n=None, deps=()):
    ash = a.shape[1:] if a_lead is not None else a.shape
    bsh = b.shape[1:] if b_lead is not None else b.shape
    if b_sharded:
        bsh = (b.shape[1], N_CHIPS * b.shape[2])
    if mode == 'nn':
        (M, K), (K2, N) = ash, bsh
    elif mode == 'nt':
        (M, K), (N, K2) = ash, bsh
    else:
        (K, M), (K2, N) = ash, bsh
    assert K == K2 or (mode == 'nt' and K2 > K), (ash, bsh, mode)
    assert M % tm == 0 and N % tn == 0 and K % tk == 0, (M, N, K, tm, tn, tk)
    nk = K // tk
    dims = {'nn': ((1,), (0,)), 'nt': NT, 'tn': TN}[mode]

    def lead(spec_shape, imap, lead_idx):
        if lead_idx is None:
            return pl.BlockSpec(spec_shape, imap)
        return pl.BlockSpec((None,) + spec_shape, lambda i, j, k: (lead_idx,) + imap(i, j, k))

    if mode == 'tn':
        a_spec = lead((tk, tm), lambda i, j, k: (k, i), a_lead)
    else:
        a_spec = lead((tm, tk), lambda i, j, k: (i, k), a_lead)
    if b_sharded:
        per = b.shape[2] // (tk if mode == 'nt' else tn)
        assert per * (tk if mode == 'nt' else tn) == b.shape[2] and mode != 'tn'
        if mode == 'nt':
            b_spec = pl.BlockSpec((None, tn, tk), lambda i, j, k: ((k + b_koff) // per, j, (k + b_koff) % per))
        else:
            b_spec = pl.BlockSpec((None, tk, tn), lambda i, j, k: (j // per, k, j % per))
    elif mode == 'nt':
        b_spec = lead((tn, tk), lambda i, j, k: (j, k + b_koff), b_lead)
    else:
        b_spec = lead((tk, tn), lambda i, j, k: (k, j), b_lead)
    o_spec = pl.BlockSpec((tm, tn), lambda i, j, k: (i, j + out_joff))
    n_out = N if out_n is None else out_n
    in_specs = [a_spec, b_spec]
    operands = [a, b]
    if add is not None:
        in_specs.append(pl.BlockSpec((tm, tn), lambda i, j, k: (i, j)))
        operands.append(add)
    aliases = {}
    if out_into is not None:
        in_specs.append(ANY)
        operands.append(out_into)
        aliases = {len(operands) - 1: 0}
    in_specs += [ANY] * len(deps)
    operands += list(deps)
    has_add = add is not None
    acc_in_out = nk > 1 and out_dtype == F32

    def body(*refs):
        a_ref, b_ref = refs[0], refs[1]
        pos = 2
        add_ref = None
        if has_add:
            add_ref = refs[pos]
            pos += 1
        if out_into is not None:
            pos += 1
        pos += len(deps)
        o_ref = refs[pos]
        acc_ref = refs[pos + 1] if (nk > 1 and not acc_in_out) else None
        p = _dot(a_ref[...], b_ref[...], dims)
        if nk == 1:
            if has_add:
                p = p + add_ref[...]
            o_ref[...] = p.astype(o_ref.dtype)
            return
        k = pl.program_id(2)
        tgt = o_ref if acc_in_out else acc_ref

        @pl.when(k == 0)
        def _():
            tgt[...] = p + add_ref[...] if has_add else p

        @pl.when(k > 0)
        def _():
            tgt[...] += p

        if not acc_in_out:
            @pl.when(k == nk - 1)
            def _():
                o_ref[...] = acc_ref[...].astype(o_ref.dtype)

    out_shape = SDS((M, n_out), out_dtype)
    scratch = [pltpu.VMEM((tm, tn), F32)] if (nk > 1 and not acc_in_out) else []
    return pl.pallas_call(
        body, grid=(M // tm, N // tn, nk), in_specs=in_specs, out_specs=o_spec, out_shape=out_shape,
        scratch_shapes=scratch, input_output_aliases=aliases, name=name,
        compiler_params=_cp("parallel", "parallel", "arbitrary"))(*operands)


def _rms_bwd_rows(xv, g, dh, dres):
    r = lax.rsqrt(jnp.mean(xv * xv, axis=-1, keepdims=True) + EPS)
    xh = xv * r
    gy = dh * g
    dx = r * (gy - xh * jnp.mean(xh * gy, axis=-1, keepdims=True)) + dres
    return dx, jnp.sum(dh * xh, axis=0, keepdims=True)


def _mm_nt_sharded_rms(a, b, x, g, dres, *, tm, name, deps=()):
    a3 = a if a.ndim == 3 else a[None]
    A, M, ka = a3.shape
    S, N, ns = b.shape
    per = S // A
    assert ka == per * ns and M % tm == 0 and N == x.shape[1], (a3.shape, b.shape, x.shape)

    def body(a_ref, b_ref, x_ref, g_ref, dres_ref, dx_ref, dxb_ref, dg_ref):
        acc = None
        for s in range(S):
            lo = (s % per) * ns
            p = _dot(a_ref[s // per, :, lo:lo + ns], b_ref[s], NT)
            acc = p if acc is None else acc + p
        dx, dg = _rms_bwd_rows(x_ref[...], g_ref[...], acc, dres_ref[...])
        dx_ref[...] = dx
        dxb_ref[...] = dx.astype(dxb_ref.dtype)

        @pl.when(pl.program_id(0) == 0)
        def _():
            dg_ref[...] = jnp.zeros_like(dg_ref)
        dg_ref[...] += dg

    rows = pl.BlockSpec((tm, N), lambda i: (i, 0))
    return pl.pallas_call(
        _after(body, 5, deps), grid=(M // tm,),
        in_specs=[pl.BlockSpec((A, tm, ka), lambda i: (0, i, 0)), pl.BlockSpec((S, N, ns), lambda i: (0, 0, 0)),
                  rows, _full((1, N)), rows] + [ANY] * len(deps),
        out_specs=[rows, rows, _full((1, N))],
        out_shape=[SDS((M, N), F32), SDS((M, N), MXU_DTYPE), SDS((1, N), F32)], name=name,
        compiler_params=_cp("arbitrary"))(a3, b, x, g, dres, *deps)


def _norm_mm(x, g, b, *, tm, tn, name, deps=()):
    M, K = x.shape
    S, K2, ns = b.shape
    per = ns // tn
    assert K == K2 and per * tn == ns and M % tm == 0, (x.shape, b.shape)

    def body(x_ref, g_ref, b_ref, o_ref, h_ref):
        @pl.when(pl.program_id(1) == 0)
        def _():
            xv = x_ref[...]
            r = lax.rsqrt(jnp.mean(xv * xv, axis=-1, keepdims=True) + EPS)
            h_ref[...] = (xv * r * g_ref[...]).astype(h_ref.dtype)
        o_ref[...] = _dot(h_ref[...], b_ref[...])

    return pl.pallas_call(
        _after(body, 3, deps), grid=(M // tm, S * per),
        in_specs=[pl.BlockSpec((tm, K), lambda i, j: (i, 0)), _full((1, K)),
                  pl.BlockSpec((None, K, tn), lambda i, j: (j // per, 0, j % per))] + [ANY] * len(deps),
        out_specs=[pl.BlockSpec((tm, tn), lambda i, j: (i, j)), pl.BlockSpec((tm, K), lambda i, j: (i, 0))],
        out_shape=[SDS((M, S * ns), F32), SDS((M, K), MXU_DTYPE)], name=name,
        compiler_params=_cp("parallel", "arbitrary"))(x, g, b, *deps)


def _after(body, n_in, deps):
    nd = len(deps)
    if nd == 0:
        return body
    return lambda *refs: body(*refs[:n_in], *refs[n_in + nd:])


def _down_proj_loss(act, w, x1, target, *, tm, name):
    T, K = act.shape
    D = w.shape[1]

    def body(a_ref, w_ref, x_ref, t_ref, loss_ref, dy_ref, dyb_ref):
        i = pl.program_id(0)
        d = (x_ref[...] + _dot(a_ref[...], w_ref[...])) - t_ref[...]
        dy = d * (1.0 / D)
        dy_ref[...] = dy
        dyb_ref[...] = dy.astype(dyb_ref.dtype)
        part = jnp.sum(jnp.sum(d * d, axis=1, keepdims=True), axis=0, keepdims=True) * (0.5 / D)

        @pl.when(i == 0)
        def _():
            loss_ref[...] = jnp.zeros_like(loss_ref)
        loss_ref[...] += jnp.broadcast_to(part, loss_ref.shape)

    rows = pl.BlockSpec((tm, D), lambda i: (i, 0))
    return pl.pallas_call(
        body, grid=(T // tm,), in_specs=[pl.BlockSpec((tm, K), lambda i: (i, 0)), _full((K, D)), rows, rows],
        out_specs=[_full((1, 128)), rows, rows],
        out_shape=[SDS((1, 128), F32), SDS((T, D), F32), SDS((T, D), MXU_DTYPE)],
        name=name, compiler_params=_cp("arbitrary"))(act, w, x1, target)


def _pool_lane_consts(shape):
    lane = _lane(shape)
    grp = lane // (POOL_WIDTH // 4)
    win = jnp.where(grp == 0, 2, jnp.where(grp == 1, 4, jnp.where(grp == 2, 8, 16)))
    return grp, win


def _pool_select(grp, s2, s4, s8, s16):
    return jnp.where(grp == 0, s2, jnp.where(grp == 1, s4, jnp.where(grp == 2, s8, s16)))


def _pool_diff(xe, row0, tr):
    s2 = xe + pltpu.roll(xe, 1, 0)
    s4 = s2 + pltpu.roll(s2, 2, 0)
    s8 = s4 + pltpu.roll(s4, 4, 0)
    s16 = s8 + pltpu.roll(s8, 8, 0)
    shape = (tr, POOL_WIDTH)
    grp, win = _pool_lane_consts(shape)
    sums = _pool_select(grp, s2[16:], s4[16:], s8[16:], s16[16:])
    t = row0 + _row(shape)
    cnt = jnp.minimum(t + 1, win).astype(F32)
    return sums / cnt - xe[16:]


def _pool_fwd(z, wbd, scale, *, tr, name):
    T = z.shape[0]
    hb = tr // 16

    def body(x_ref, xp_ref, w_ref, s_ref, o_ref):
        i = pl.program_id(0)
        halo = jnp.where(i == 0, 0.0, xp_ref[...])
        diff = _pool_diff(jnp.concatenate([halo, x_ref[...]], axis=0), i * tr, tr)
        o_ref[...] = (_dot(diff, w_ref[...]) * s_ref[...]).astype(o_ref.dtype)

    return pl.pallas_call(
        body, grid=(T // tr,),
        in_specs=[pl.BlockSpec((tr, POOL_WIDTH), lambda i: (i, 0)),
                  pl.BlockSpec((16, POOL_WIDTH), lambda i: (jnp.maximum(i * hb - 1, 0), 0)),
                  _full((POOL_WIDTH, POOL_WIDTH)), _full((1, POOL_WIDTH))],
        out_specs=pl.BlockSpec((tr, POOL_WIDTH), lambda i: (i, 0)),
        out_shape=SDS((T, POOL_WIDTH), MXU_DTYPE), name=name, compiler_params=_cp("parallel"))(z, z, wbd, scale)


def _pool_bwd_tile(i, n, tr, x, xprev, dpa, dpa_next, wbd, scale):
    halo = jnp.where(i == 0, 0.0, xprev)
    diff = _pool_diff(jnp.concatenate([halo, x], axis=0), i * tr, tr)
    mixed = _dot(diff, wbd)
    dscale = jnp.sum(dpa * mixed, axis=0, keepdims=True)
    dnext = jnp.where(i == n - 1, 0.0, dpa_next)
    dmix_e = jnp.concatenate([dpa, dnext], axis=0) * scale
    ddiff_e = _dot(dmix_e, wbd, NT)
    dwbd = _dot(diff, dmix_e[:tr], TN)
    shape = (tr + 16, POOL_WIDTH)
    grp, win = _pool_lane_consts(shape)
    t = i * tr + _row(shape)
    e = ddiff_e / jnp.minimum(t + 1, win).astype(F32)
    nrow = tr + 16
    a2 = e + pltpu.roll(e, nrow - 1, 0)
    a4 = a2 + pltpu.roll(a2, nrow - 2, 0)
    a8 = a4 + pltpu.roll(a4, nrow - 4, 0)
    a16 = a8 + pltpu.roll(a8, nrow - 8, 0)
    dx = _pool_select(grp, a2, a4, a8, a16)[:tr] - ddiff_e[:tr]
    return dx, dwbd, dscale


def _norm_rope(x, g, cos, sin_signed, seg):
    reps = x.shape[1] // 128
    ms = _split_dot(x * x, seg) * (1.0 / HEAD_DIM)
    r = lax.rsqrt(ms + EPS)
    xn = x * r * g
    c, s = _tile_lanes(cos, reps), _tile_lanes(sin_signed, reps)
    return xn * c + _swap_halves(xn) * s


def _norm_rope_bwd(x, g, cos, sin_signed, seg, dout):
    reps = x.shape[1] // 128
    c, s = _tile_lanes(cos, reps), _tile_lanes(sin_signed, reps)
    dxn = dout * c + _swap_halves(dout * s)
    ms = _split_dot(x * x, seg) * (1.0 / HEAD_DIM)
    r = lax.rsqrt(ms + EPS)
    xh = x * r
    gy = dxn * g
    dx = r * (gy - xh * (_split_dot(xh * gy, seg) * (1.0 / HEAD_DIM)))
    dg = jnp.sum(dxn * xh, axis=0, keepdims=True)
    return dx, dg


def _dup_heads(k):
    first = _lane(k.shape) < HEAD_DIM
    kr = pltpu.roll(k, HEAD_DIM, 1)
    return jnp.concatenate([jnp.where(first, k, kr), jnp.where(first, kr, k)], axis=1)


def _qkv_prep(z, cos, sin_signed, gq, gk, seg, *, tr, name):
    T = z.shape[0]

    def body(qa_ref, qb_ref, kv_ref, c_ref, s_ref, gq_ref, gk_ref, seg_ref, q_ref, k_ref, v_ref):
        c, s, seg_m = c_ref[...], s_ref[...], seg_ref[...]
        scale = HEAD_DIM ** -0.5
        qa = _norm_rope(qa_ref[...], gq_ref[...], c, s, seg_m) * scale
        qb = _norm_rope(qb_ref[...], gq_ref[...], c, s, seg_m) * scale
        q_ref[...] = jnp.concatenate([qa, qb], axis=1).astype(q_ref.dtype)
        kv = kv_ref[...]
        k = _norm_rope(kv[:, :KV_WIDTH], gk_ref[...], c, s, seg_m[:128, :128])
        k_ref[...] = _dup_heads(k).astype(k_ref.dtype)
        v_ref[...] = _dup_heads(kv[:, KV_WIDTH:]).astype(v_ref.dtype)

    col = lambda j: pl.BlockSpec((tr, 256), lambda i: (i, j))
    tab = pl.BlockSpec((tr, 128), lambda i: (i, 0))
    return pl.pallas_call(
        body, grid=(T // tr,),
        in_specs=[col(1), col(2), col(3), tab, tab, _full((1, 256)), _full((1, 128)), _full((256, 256))],
        out_specs=[pl.BlockSpec((tr, 512), lambda i: (i, 0)), col(0), col(0)],
        out_shape=[SDS((T, 512), MXU_DTYPE), SDS((T, 256), MXU_DTYPE), SDS((T, 256), MXU_DTYPE)],
        name=name, compiler_params=_cp("parallel"))(z, z, z, cos, sin_signed, gq, gk, seg)


GROUP_HEADS = 4
GROUP_ROWS = GROUP_HEADS * ATTN_BLOCK
ALL_ROWS = N_Q_HEADS * ATTN_BLOCK


def _attn_mask(n):
    qi = _row((ALL_ROWS, 2 * ATTN_BLOCK)) % ATTN_BLOCK
    kj = _lane((ALL_ROWS, 2 * ATTN_BLOCK))
    return (kj > qi) & (kj <= qi + ATTN_BLOCK) & ((kj >= ATTN_BLOCK) | (n > 0))


def _stack_heads(x, g):
    first = _lane((ATTN_BLOCK, 128)) < HEAD_DIM
    parts = []
    for pair in (2 * g, 2 * g + 1):
        x128 = x[:, 128 * pair:128 * (pair + 1)]
        zero = jnp.zeros_like(x128)
        parts += [jnp.where(first, x128, zero), jnp.where(first, zero, x128)]
    return jnp.concatenate(parts, axis=0)


def _unstack_heads(y):
    first = _lane((ATTN_BLOCK, 128)) < HEAD_DIM
    b = ATTN_BLOCK
    return jnp.concatenate([jnp.where(first, y[0:b], y[b:2 * b]), jnp.where(first, y[2 * b:3 * b], y[3 * b:4 * b])],
                           axis=1)


def _sink_col(sk_ref):
    return jnp.concatenate([jnp.broadcast_to(sk_ref[h:h + 1, 0:1], (ATTN_BLOCK, 1)) for h in range(N_Q_HEADS)],
                           axis=0)


def _by_group(a8, b2, dims=((1,), (0,))):
    return jnp.concatenate([_dot(a8[:GROUP_ROWS], b2[:, :128], dims), _dot(a8[GROUP_ROWS:], b2[:, 128:], dims)],
                           axis=0)


def _softmax_exp(q8, k2, mask, sink):
    s = jnp.where(mask, _by_group(q8, k2, NT), NEG_BIG)
    m = jnp.maximum(jnp.max(s, axis=1, keepdims=True), sink)
    p = jnp.exp(s - m)
    ps = jnp.exp(sink - m)
    return p, ps, 1.0 / (jnp.sum(p, axis=1, keepdims=True) + ps)


def _attn_fwd(q, k, v, sinks_b, *, name):
    T = q.shape[0]
    nb = T // ATTN_BLOCK

    def body(q_ref, kc_ref, kp_ref, vc_ref, vp_ref, sk_ref, o_ref):
        n = pl.program_id(0)
        mask = _attn_mask(n)
        k2 = jnp.concatenate([kp_ref[...], kc_ref[...]], axis=0)
        v2 = jnp.concatenate([vp_ref[...], vc_ref[...]], axis=0)
        qv = q_ref[...]
        q8 = jnp.concatenate([_stack_heads(qv, 0), _stack_heads(qv, 1)], axis=0)
        p, _, inv = _softmax_exp(q8, k2, mask, _sink_col(sk_ref))
        o8 = _by_group(p, v2) * inv
        o_ref[...] = jnp.concatenate([_unstack_heads(o8[:GROUP_ROWS]), _unstack_heads(o8[GROUP_ROWS:])],
                                     axis=1).astype(o_ref.dtype)

    cur = lambda w: pl.BlockSpec((ATTN_BLOCK, w), lambda n: (n, 0))
    prev = lambda w: pl.BlockSpec((ATTN_BLOCK, w), lambda n: (jnp.maximum(n - 1, 0), 0))
    return pl.pallas_call(
        body, grid=(nb,),
        in_specs=[cur(512), cur(256), prev(256), cur(256), prev(256), _full((8, 128))],
        out_specs=cur(512), out_shape=SDS((T, 512), MXU_DTYPE), name=name,
        compiler_params=_cp("parallel"))(q, k, k, v, v, sinks_b)


def _attn_bwd(q, k, v, sinks_b, do, *, name, deps=()):
    T = q.shape[0]
    nb = T // ATTN_BLOCK

    def body(q_ref, kc_ref, kp_ref, vc_ref, vp_ref, sk_ref, do_ref,
             dq_ref, dkc_ref, dkp_ref, dvc_ref, dvp_ref, dsk_ref):
        n = pl.program_id(0)
        mask = _attn_mask(n)
        k2 = jnp.concatenate([kp_ref[...], kc_ref[...]], axis=0)
        v2 = jnp.concatenate([vp_ref[...], vc_ref[...]], axis=0)
        qv = q_ref[...]
        dov = do_ref[...]

        @pl.when(n == 0)
        def _():
            dsk_ref[...] = jnp.zeros_like(dsk_ref)

        q8 = jnp.concatenate([_stack_heads(qv, 0), _stack_heads(qv, 1)], axis=0)
        do8 = jnp.concatenate([_stack_heads(dov, 0), _stack_heads(dov, 1)], axis=0)
        p, ps, inv = _softmax_exp(q8, k2, mask, _sink_col(sk_ref))
        pn = p * inv
        delta = jnp.sum(do8 * _by_group(pn, v2), axis=1, keepdims=True)
        ds = pn * (_by_group(do8, v2, NT) - delta)
        dq8 = _by_group(ds, k2)
        dq_ref[...] = jnp.concatenate([_unstack_heads(dq8[:GROUP_ROWS]), _unstack_heads(dq8[GROUP_ROWS:])], axis=1)
        dk = jnp.concatenate([_dot(ds[:GROUP_ROWS], q8[:GROUP_ROWS], TN), _dot(ds[GROUP_ROWS:], q8[GROUP_ROWS:], TN)],
                             axis=1)
        dv = jnp.concatenate([_dot(pn[:GROUP_ROWS], do8[:GROUP_ROWS], TN),
                              _dot(pn[GROUP_ROWS:], do8[GROUP_ROWS:], TN)], axis=1)
        wsink = (ps * inv) * delta
        for h in range(N_Q_HEADS):
            dsink = -jnp.sum(wsink[ATTN_BLOCK * h:ATTN_BLOCK * (h + 1)], axis=0, keepdims=True)
            dsk_ref[h:h + 1, :] += jnp.broadcast_to(dsink, (1, 128))
        dkp_ref[...] = dk[:ATTN_BLOCK]
        dkc_ref[...] = dk[ATTN_BLOCK:]
        dvp_ref[...] = dv[:ATTN_BLOCK]
        dvc_ref[...] = dv[ATTN_BLOCK:]

    cur = lambda w: pl.BlockSpec((ATTN_BLOCK, w), lambda n: (n, 0))
    prev = lambda w: pl.BlockSpec((ATTN_BLOCK, w), lambda n: (jnp.maximum(n - 1, 0), 0))
    f = lambda w: SDS((T, w), F32)
    return pl.pallas_call(
        _after(body, 7, deps), grid=(nb,),
        in_specs=[cur(512), cur(256), prev(256), cur(256), prev(256), _full((8, 128)), cur(512)] + [ANY] * len(deps),
        out_specs=[cur(512), cur(256), cur(256), cur(256), cur(256), _full((8, 128))],
        out_shape=[f(512), f(256), f(256), f(256), f(256), SDS((8, 128), F32)],
        name=name, compiler_params=_cp("arbitrary"))(q, k, k, v, v, sinks_b, do, *deps)


def _mixer_ab_bwd(z, cos, sin_signed, gq, gk, seg, dq, dkc, dkp, dvc, dvp, dpa, wbd, scale, dz, *, tr, name, deps=()):
    T = z.shape[0]
    n = T // tr
    hb = tr // 16
    ab = tr // ATTN_BLOCK

    def unfold(cur, nxt_tile, nxt_halo, i):
        nxt = jnp.concatenate([nxt_tile[ATTN_BLOCK:], jnp.where(i == n - 1, 0.0, nxt_halo)], axis=0)
        tot = cur + nxt
        first = _lane((tr, 128)) < HEAD_DIM
        a = tot[:, :128]
        b = tot[:, 128:]
        a = a + pltpu.roll(a, HEAD_DIM, 1)
        b = b + pltpu.roll(b, HEAD_DIM, 1)
        return jnp.where(first, a, b)

    def body(xp_ref, xpp_ref, qa_ref, qb_ref, kv_ref, c_ref, s_ref, gq_ref, gk_ref, seg_ref,
             dq_ref, dkc_ref, dkp_ref, dkh_ref, dvc_ref, dvp_ref, dvh_ref, dpa_ref, dpan_ref, w_ref, sc_ref, _dz_in,
             dz_ref, dgq_ref, dgk_ref, dw_ref, dsc_ref):
        i = pl.program_id(0)
        c, s, seg_m = c_ref[...], s_ref[...], seg_ref[...]
        scale_q = HEAD_DIM ** -0.5
        dqv = dq_ref[...] * scale_q
        dxa, dga = _norm_rope_bwd(qa_ref[...], gq_ref[...], c, s, seg_m, dqv[:, :256])
        dxb, dgb = _norm_rope_bwd(qb_ref[...], gq_ref[...], c, s, seg_m, dqv[:, 256:])
        dk = unfold(dkc_ref[...], dkp_ref[...], dkh_ref[...], i)
        dv = unfold(dvc_ref[...], dvp_ref[...], dvh_ref[...], i)
        kv = kv_ref[...]
        dxk, dgk = _norm_rope_bwd(kv[:, :KV_WIDTH], gk_ref[...], c, s, seg_m[:128, :128], dk)
        dxp, dwbd, dscale = _pool_bwd_tile(i, n, tr, xp_ref[...], xpp_ref[...], dpa_ref[...], dpan_ref[...],
                                           w_ref[...], sc_ref[...])
        dz_ref[...] = jnp.concatenate([dxp, dxa, dxb, dxk, dv], axis=1).astype(dz_ref.dtype)

        @pl.when(i == 0)
        def _():
            dgq_ref[...] = jnp.zeros_like(dgq_ref)
            dgk_ref[...] = jnp.zeros_like(dgk_ref)
            dw_ref[...] = jnp.zeros_like(dw_ref)
            dsc_ref[...] = jnp.zeros_like(dsc_ref)
        dgq_ref[...] += _fold_lanes(dga + dgb, HEAD_DIM)
        dgk_ref[...] += _fold_lanes(dgk, HEAD_DIM)
        dw_ref[...] += dwbd
        dsc_ref[...] += dscale

    col = lambda j: pl.BlockSpec((tr, 256), lambda i: (i, j))
    rows = lambda w: pl.BlockSpec((tr, w), lambda i: (i, 0))
    nxt_blk = pl.BlockSpec((ATTN_BLOCK, 256), lambda i: (jnp.minimum((i + 1) * ab, T // ATTN_BLOCK - 1), 0))
    prev16 = pl.BlockSpec((16, 256), lambda i: (jnp.maximum(i * hb - 1, 0), 0))
    next16 = pl.BlockSpec((16, 256), lambda i: (jnp.minimum((i + 1) * hb, T // 16 - 1), 0))
    return pl.pallas_call(
        _after(body, 22, deps), grid=(n,),
        in_specs=[col(0), prev16, col(1), col(2), col(3), rows(128), rows(128),
                  _full((1, 256)), _full((1, 128)), _full((256, 256)),
                  rows(512), rows(256), rows(256), nxt_blk, rows(256), rows(256), nxt_blk,
                  rows(256), next16, _full((256, 256)), _full((1, 256)), ANY] + [ANY] * len(deps),
        out_specs=[rows(1024), _full((1, 256)), _full((1, 128)), _full((256, 256)), _full((1, 256))],
        out_shape=[SDS((T, IN_COLS), MXU_DTYPE), SDS((1, 256), F32), SDS((1, 128), F32),
                   SDS((256, 256), F32), SDS((1, 256), F32)],
        input_output_aliases={21: 0}, name=name, compiler_params=_cp("arbitrary"))(
            z, z, z, z, z, cos, sin_signed, gq, gk, seg, dq, dkc, dkp, dkp, dvc, dvp, dvp, dpa, dpa, wbd, scale, dz,
            *deps)


def _sgu_common(zu, zv, vn, seg):
    u, du = _gelu_and_grad(zu)
    gv, dgv = _gelu_and_grad(zv)
    ms = _split_dot(gv * gv, seg) * (1.0 / HEAD_DIM)
    r = lax.rsqrt(ms + EPS)
    xh = gv * r
    return u, du, dgv, r, xh, xh * vn


def _sgu_fwd(z, wtril, bexp, vn, seg, *, tr, name):
    T = z.shape[0]
    nch = tr // CHUNK

    def body(u_ref, v_ref, w_ref, b_ref, vn_ref, seg_ref, o_ref):
        u, _, _, _, _, vg = _sgu_common(u_ref[...], v_ref[...], vn_ref[...], seg_ref[...])
        grp = _lane((CHUNK, SGU_WIDTH)) // HEAD_DIM
        outs = []
        for ch in range(nch):
            vc = vg[ch * CHUNK:(ch + 1) * CHUNK]
            s = b_ref[...]
            for g in range(4):
                s = s + jnp.where(grp == g, _dot(w_ref[g], vc), 0.0)
            outs.append(u[ch * CHUNK:(ch + 1) * CHUNK] * s)
        o_ref[...] = jnp.concatenate(outs, axis=0).astype(o_ref.dtype)

    col = lambda j: pl.BlockSpec((tr, 256), lambda i: (i, j))
    return pl.pallas_call(
        body, grid=(T // tr,),
        in_specs=[col(4), col(5), _full((4, CHUNK, CHUNK)), _full((CHUNK, 256)), _full((1, 256)), _full((256, 256))],
        out_specs=col(0), out_shape=SDS((T, SGU_WIDTH), MXU_DTYPE), name=name,
        compiler_params=_cp("parallel"))(z, z, wtril, bexp, vn, seg)


def _sgu_bwd(z, wtril, bexp, vn, seg, dsg, dz, *, tr, name):
    T = z.shape[0]
    nch = tr // CHUNK

    def body(u_ref, v_ref, w_ref, b_ref, vn_ref, seg_ref, d_ref, _dz_in, dz_ref, dw_ref, db_ref, dvn_ref):
        i = pl.program_id(0)
        seg_m = seg_ref[...]
        vn_v = vn_ref[...]
        u, du, dgv, r, xh, vg = _sgu_common(u_ref[...], v_ref[...], vn_v, seg_m)
        d = d_ref[...]
        grp = _lane((CHUNK, SGU_WIDTH)) // HEAD_DIM
        tril = _row((CHUNK, CHUNK)) >= _lane((CHUNK, CHUNK))

        @pl.when(i == 0)
        def _():
            dw_ref[...] = jnp.zeros_like(dw_ref)
            db_ref[...] = jnp.zeros_like(db_ref)
            dvn_ref[...] = jnp.zeros_like(dvn_ref)

        dus, dvgs = [], []
        for ch in range(nch):
            sl = slice(ch * CHUNK, (ch + 1) * CHUNK)
            vc = vg[sl]
            s = b_ref[...]
            for g in range(4):
                s = s + jnp.where(grp == g, _dot(w_ref[g], vc), 0.0)
            dus.append(d[sl] * s)
            ds = d[sl] * u[sl]
            db_ref[...] += _split_dot(ds, seg_m)
            dvg = jnp.zeros((CHUNK, SGU_WIDTH), F32)
            for g in range(4):
                dsm = jnp.where(grp == g, ds, 0.0)
                dvg = dvg + jnp.where(grp == g, _dot(w_ref[g], ds, TN), 0.0)
                dw_ref[g] += jnp.where(tril, _dot(dsm, vc, NT), 0.0)
            dvgs.append(dvg)
        dup = jnp.concatenate(dus, axis=0)
        dvg = jnp.concatenate(dvgs, axis=0)
        dvn_ref[...] += _fold_lanes(jnp.sum(dvg * xh, axis=0, keepdims=True), HEAD_DIM)
        gy = dvg * vn_v
        dgvv = r * (gy - xh * (_split_dot(xh * gy, seg_m) * (1.0 / HEAD_DIM)))
        dz_ref[...] = jnp.concatenate([dup * du, dgvv * dgv], axis=1).astype(dz_ref.dtype)

    col = lambda j: pl.BlockSpec((tr, 256), lambda i: (i, j))
    return pl.pallas_call(
        body, grid=(T // tr,),
        in_specs=[col(4), col(5), _full((4, CHUNK, CHUNK)), _full((CHUNK, 256)), _full((1, 256)), _full((256, 256)),
                  col(0), ANY],
        out_specs=[pl.BlockSpec((tr, 512), lambda i: (i, 2)), _full((4, CHUNK, CHUNK)), _full((CHUNK, 256)),
                   _full((1, 256))],
        out_shape=[SDS((T, IN_COLS), MXU_DTYPE), SDS((4, CHUNK, CHUNK), F32), SDS((CHUNK, 256), F32),
                   SDS((1, 256), F32)],
        input_output_aliases={7: 0}, name=name, compiler_params=_cp("arbitrary"))(
            z, z, wtril, bexp, vn, seg, dsg, dz)


def _merge_fwd(pa, at, sg, wa, wb, wc, z, x, w_out, *, tm, tn, name):
    T = pa.shape[0]
    gb = GATE_COL0 // tn
    nb = D_MODEL // tn

    def body(pa_ref, at_ref, sg_ref, wa_ref, wb_ref, wc_ref, g0_ref, g1_ref, g2_ref, x_ref, wo_ref,
             m_ref, y_ref, x1_ref):
        j = pl.program_id(1)
        acc = None
        for idx, (op_ref, w_ref, g_ref) in enumerate(((pa_ref, wa_ref, g0_ref), (at_ref, wb_ref, g1_ref),
                                                      (sg_ref, wc_ref, g2_ref))):
            y = _dot(op_ref[...], w_ref[...])
            y_ref[idx] = y.astype(y_ref.dtype)
            t = _sigmoid(g_ref[...]) * y
            acc = t if acc is None else acc + t
        merged = acc.astype(m_ref.dtype)
        m_ref[...] = merged
        p = _dot(merged, wo_ref[...])

        @pl.when(j == 0)
        def _():
            x1_ref[...] = x_ref[...] + p

        @pl.when(j > 0)
        def _():
            x1_ref[...] += p

    op = lambda w: pl.BlockSpec((tm, w), lambda i, j: (i, 0))
    wt = lambda k: pl.BlockSpec((k, tn), lambda i, j: (0, j))
    gate = lambda b: pl.BlockSpec((tm, tn), lambda i, j: (i, gb + b * nb + j))
    return pl.pallas_call(
        body, grid=(T // tm, nb),
        in_specs=[op(256), op(512), op(256), wt(256), wt(512), wt(256), gate(0), gate(1), gate(2),
                  op(D_MODEL), pl.BlockSpec((tn, D_MODEL), lambda i, j: (j, 0))],
        out_specs=[pl.BlockSpec((tm, tn), lambda i, j: (i, j)), pl.BlockSpec((3, tm, tn), lambda i, j: (0, i, j)),
                   op(D_MODEL)],
        out_shape=[SDS((T, D_MODEL), MXU_DTYPE), SDS((3, T, D_MODEL), MXU_DTYPE), SDS((T, D_MODEL), F32)],
        name=name, compiler_params=_cp("parallel", "arbitrary"))(pa, at, sg, wa, wb, wc, z, z, z, x, w_out)


def _out_dx_merge_bwd(dxb, w_out, y, z, ws, xs, *, tm, tn, name):
    T = dxb.shape[0]
    gb = GATE_COL0 // tn
    nb = D_MODEL // tn
    nr = T // tm
    widths = [w.shape[0] for w in ws]

    def body(dx_ref, w_ref, y_ref, g_ref, *refs):
        w_refs, x_refs = refs[0:3], refs[3:6]
        dz_ref, dx_refs, dw_refs = refs[6], refs[7:10], refs[10:13]
        dm_ref, acc_refs = refs[13], refs[14:17]
        i, b, j = pl.program_id(0), pl.program_id(1), pl.program_id(2)

        @pl.when((b == 0) & (j == 0))
        def _():
            dm = _dot(dx_ref[...], w_ref[...], NT)
            for jj in range(nb):
                dm_ref[jj] = dm[:, jj * tn:(jj + 1) * tn]

        d = dm_ref[j]
        g = _sigmoid(g_ref[...])
        dy = (d * g).astype(MXU_DTYPE)
        dz_ref[...] = (d * y_ref[...].astype(F32) * g * (1.0 - g)).astype(dz_ref.dtype)
        for branch in range(3):
            @pl.when(b == branch)
            def _():
                p = _dot(dy, w_refs[branch][...], NT)
                q = _dot(x_refs[branch][...], dy, TN)

                @pl.when(j == 0)
                def _():
                    dx_refs[branch][...] = p

                @pl.when(j > 0)
                def _():
                    dx_refs[branch][...] += p

                @pl.when(i == 0)
                def _():
                    acc_refs[branch][j] = q

                @pl.when(i > 0)
                def _():
                    acc_refs[branch][j] += q

        @pl.when((i == nr - 1) & (b == 2) & (j == nb - 1))
        def _():
            for branch in range(3):
                for jj in range(nb):
                    dw_refs[branch][:, jj * tn:(jj + 1) * tn] = acc_refs[branch][jj]

    wspec = lambda k: pl.BlockSpec((k, tn), lambda i, b, j: (0, j))
    rows = lambda k: pl.BlockSpec((tm, k), lambda i, b, j: (i, 0))
    return pl.pallas_call(
        body, grid=(nr, 3, nb),
        in_specs=[rows(D_MODEL), _full((D_MODEL, D_MODEL)),
                  pl.BlockSpec((None, tm, tn), lambda i, b, j: (b, i, j)),
                  pl.BlockSpec((tm, tn), lambda i, b, j: (i, gb + b * nb + j))]
        + [wspec(k) for k in widths] + [rows(k) for k in widths],
        out_specs=[pl.BlockSpec((tm, tn), lambda i, b, j: (i, gb + b * nb + j))]
        + [rows(k) for k in widths] + [_full((k, D_MODEL)) for k in widths],
        out_shape=[SDS((T, IN_COLS), MXU_DTYPE)] + [SDS((T, k), F32) for k in widths]
        + [SDS((k, D_MODEL), F32) for k in widths],
        scratch_shapes=[pltpu.VMEM((nb, tm, tn), F32)] + [pltpu.VMEM((nb, k, tn), F32) for k in widths],
        name=name, compiler_params=_cp("arbitrary", "arbitrary", "arbitrary"))(dxb, w_out, y, z, *ws, *xs)


def _conv3(xe, w, b):
    return (w[0:1] * pltpu.roll(xe, 2, 0) + w[1:2] * pltpu.roll(xe, 1, 0) + w[2:3] * xe)[8:] + b


def _conv_act_fwd(up, cw, cb, *, tr, tc, name):
    T = up.shape[0]
    nc = D_FF // tc
    hb = tr // 8

    def body(ug_ref, ugp_ref, uv_ref, uvp_ref, wg_ref, wv_ref, bg_ref, bv_ref, o_ref):
        i = pl.program_id(1)
        first = i == 0
        cg = _conv3(jnp.concatenate([jnp.where(first, 0.0, ugp_ref[...]), ug_ref[...]], axis=0), wg_ref[...], bg_ref[...])
        cv = _conv3(jnp.concatenate([jnp.where(first, 0.0, uvp_ref[...]), uv_ref[...]], axis=0), wv_ref[...], bv_ref[...])
        o_ref[...] = (cg * _sigmoid(cg) * cv).astype(o_ref.dtype)

    tile = lambda off: pl.BlockSpec((tr, tc), lambda j, i: (i, off + j))
    prev = lambda off: pl.BlockSpec((8, tc), lambda j, i: (jnp.maximum(i * hb - 1, 0), off + j))
    par = lambda rows, off: pl.BlockSpec((rows, tc), lambda j, i: (0, off + j))
    return pl.pallas_call(
        body, grid=(nc, T // tr),
        in_specs=[tile(0), prev(0), tile(nc), prev(nc), par(3, 0), par(3, nc), par(1, 0), par(1, nc)],
        out_specs=pl.BlockSpec((tr, tc), lambda j, i: (i, j)),
        out_shape=SDS((T, D_FF), MXU_DTYPE), name=name,
        compiler_params=_cp("parallel", "parallel"))(up, up, up, up, cw, cw, cb, cb)


def _conv_act_bwd(up, cw, cb, dact, *, tr, tc, name, deps=()):
    T = up.shape[0]
    nc = D_FF // tc
    hb = tr // 8
    nr = T // tr

    def body(ug_ref, ugp_ref, ugn_ref, uv_ref, uvp_ref, uvn_ref, da_ref, dan_ref, wg_ref, wv_ref, bg_ref, bv_ref,
             du_ref, dwg_ref, dwv_ref, dbg_ref, dbv_ref):
        i = pl.program_id(1)
        first, last = i == 0, i == nr - 1
        da = jnp.concatenate([da_ref[...], jnp.where(last, 0.0, dan_ref[...])], axis=0)
        uge = jnp.concatenate([jnp.where(first, 0.0, ugp_ref[...]), ug_ref[...], ugn_ref[...]], axis=0)
        uve = jnp.concatenate([jnp.where(first, 0.0, uvp_ref[...]), uv_ref[...], uvn_ref[...]], axis=0)
        wg, wv = wg_ref[...], wv_ref[...]
        ug1, ug2 = pltpu.roll(uge, 1, 0)[8:], pltpu.roll(uge, 2, 0)[8:]
        uv1, uv2 = pltpu.roll(uve, 1, 0)[8:], pltpu.roll(uve, 2, 0)[8:]
        cg = wg[0:1] * ug2 + wg[1:2] * ug1 + wg[2:3] * uge[8:] + bg_ref[...]
        cv = wv[0:1] * uv2 + wv[1:2] * uv1 + wv[2:3] * uve[8:] + bv_ref[...]
        sg = _sigmoid(cg)
        dcg = da * cv * (sg * (1.0 + cg * (1.0 - sg)))
        dcv = da * (cg * sg)
        nrow = tr + 8

        def back(dc, w):
            return (w[2:3] * dc + w[1:2] * pltpu.roll(dc, nrow - 1, 0) + w[0:1] * pltpu.roll(dc, nrow - 2, 0))[:tr]

        du_ref[0] = back(dcg, wg).astype(du_ref.dtype)
        du_ref[1] = back(dcv, wv).astype(du_ref.dtype)

        def wgrad(dc, u0, u1, u2):
            d = dc[:tr]
            rows = [jnp.sum(d * u2[:tr], axis=0, keepdims=True), jnp.sum(d * u1[:tr], axis=0, keepdims=True),
                    jnp.sum(d * u0[8:8 + tr], axis=0, keepdims=True)]
            return jnp.concatenate(rows, axis=0), jnp.sum(d, axis=0, keepdims=True)

        dwg, dbg = wgrad(dcg, uge, ug1, ug2)
        dwv, dbv = wgrad(dcv, uve, uv1, uv2)

        @pl.when(first)
        def _():
            dwg_ref[...] = jnp.zeros_like(dwg_ref)
            dwv_ref[...] = jnp.zeros_like(dwv_ref)
            dbg_ref[...] = jnp.zeros_like(dbg_ref)
            dbv_ref[...] = jnp.zeros_like(dbv_ref)
        dwg_ref[...] += dwg
        dwv_ref[...] += dwv
        dbg_ref[...] += dbg
        dbv_ref[...] += dbv

    tile = lambda off: pl.BlockSpec((tr, tc), lambda j, i: (i, off + j))
    prev = lambda off: pl.BlockSpec((8, tc), lambda j, i: (jnp.maximum(i * hb - 1, 0), off + j))
    nxt = lambda off: pl.BlockSpec((8, tc), lambda j, i: (jnp.minimum((i + 1) * hb, T // 8 - 1), off + j))
    par = lambda rows, off: pl.BlockSpec((rows, tc), lambda j, i: (0, off + j))
    acc = lambda rows: pl.BlockSpec((rows, tc), lambda j, i: (0, j))
    return pl.pallas_call(
        _after(body, 12, deps), grid=(nc, nr),
        in_specs=[tile(0), prev(0), nxt(0), tile(nc), prev(nc), nxt(nc), tile(0), nxt(0),
                  par(3, 0), par(3, nc), par(1, 0), par(1, nc)] + [ANY] * len(deps),
        out_specs=[pl.BlockSpec((2, tr, tc), lambda j, i: (0, i, j)), acc(3), acc(3), acc(1), acc(1)],
        out_shape=[SDS((2, T, D_FF), MXU_DTYPE), SDS((3, D_FF), F32), SDS((3, D_FF), F32),
                   SDS((1, D_FF), F32), SDS((1, D_FF), F32)],
        name=name, compiler_params=_cp("parallel", "arbitrary"))(
            up, up, up, up, up, up, dact, dact, cw, cw, cb, cb, *deps)


def _row_tile(rows, cap):
    t = min(cap, rows)
    t -= t % 8
    while rows % t:
        t -= 8
    return t


def _adamw(w, g, m, v, *, tr, name, copy_g=False):
    R, C = w.shape
    assert R % tr == 0, (R, tr)

    def body(w_ref, g_ref, m_ref, v_ref, d_ref, nm_ref, nv_ref, *rest):
        gv = g_ref[...]
        mn = ADAM_B1 * m_ref[...] + (1.0 - ADAM_B1) * gv
        vn = ADAM_B2 * v_ref[...] + (1.0 - ADAM_B2) * (gv * gv)
        m_hat = mn / (1.0 - ADAM_B1 ** ADAM_STEP)
        v_hat = vn / (1.0 - ADAM_B2 ** ADAM_STEP)
        d_ref[...] = -ADAM_LR * (m_hat / (jnp.sqrt(v_hat) + ADAM_EPS) + ADAM_WD * w_ref[...])
        nm_ref[...] = mn
        nv_ref[...] = vn
        if copy_g:
            rest[0][...] = gv

    rows = pl.BlockSpec((tr, C), lambda i: (i, 0))
    n_out = 4 if copy_g else 3
    return pl.pallas_call(
        body, grid=(R // tr,), in_specs=[rows] * 4, out_specs=[rows] * n_out,
        out_shape=[SDS((R, C), F32)] * n_out, name=name, compiler_params=_cp("parallel"))(w, g, m, v)


def _sum_slots(r, *, tr, name):
    S, R, C = r.shape
    assert R % tr == 0, (R, tr)

    def body(r_ref, o_ref):
        acc = r_ref[0]
        for s in range(1, S):
            acc = acc + r_ref[s]
        o_ref[...] = acc

    return pl.pallas_call(
        body, grid=(R // tr,), in_specs=[pl.BlockSpec((S, tr, C), lambda i: (0, i, 0))],
        out_specs=pl.BlockSpec((tr, C), lambda i: (i, 0)), out_shape=SDS((R, C), F32),
        name=name, compiler_params=_cp("parallel"))(r)


def _pair_add(g4, h, pos, *, name):
    A, _, r, C = g4.shape
    cs = C if A == N_CHIPS else C // N_CHIPS
    tr = _row_tile(r, 256)
    if A == N_CHIPS:
        g_map, h_map = (lambda t, i, pos: (t, pos[1], i, 0)), (lambda t, i, pos: (t, i, 0))
    else:
        g_map, h_map = (lambda t, i, pos: (0, pos[1], i, t)), (lambda t, i, pos: (0, i, t))

    def body(pos_ref, g_ref, h_ref, o_ref):
        o_ref[...] = (g_ref[...] + h_ref[...]).astype(o_ref.dtype)

    grid_spec = pltpu.PrefetchScalarGridSpec(
        num_scalar_prefetch=1, grid=(N_CHIPS, r // tr),
        in_specs=[pl.BlockSpec((None, None, tr, cs), g_map), pl.BlockSpec((None, tr, cs), h_map)],
        out_specs=pl.BlockSpec((None, tr, cs), lambda t, i, pos: (t, i, 0)))
    return pl.pallas_call(body, grid_spec=grid_spec, out_shape=SDS((N_CHIPS, r, cs), COMM_DTYPE), name=name,
                          compiler_params=_cp("parallel", "parallel"))(pos, g4, h)


def _chip_sum(p, r2, f_into, pos, layer, *, name):
    _, r, cs = p.shape
    tr = _row_tile(r, 256)

    def body(pos_ref, own_ref, r_ref, *rest):
        o_ref = rest[-1]
        o_ref[...] = ((own_ref[...].astype(F32) + r_ref[0].astype(F32)) + r_ref[1].astype(F32)) + r_ref[2].astype(F32)

    in_specs = [pl.BlockSpec((None, tr, cs), lambda i, pos: (pos[0], i, 0)),
                pl.BlockSpec((3, tr, cs), lambda i, pos: (0, i, 0))]
    operands = [pos, p, r2]
    aliases = {}
    if f_into is not None:
        in_specs.append(ANY)
        operands.append(f_into)
        aliases = {3: 0}
    grid_spec = pltpu.PrefetchScalarGridSpec(
        num_scalar_prefetch=1, grid=(r // tr,), in_specs=in_specs,
        out_specs=pl.BlockSpec((None, None, tr, cs), lambda i, pos: (layer, pos[1], i, 0)))
    return pl.pallas_call(body, grid_spec=grid_spec, out_shape=SDS((DEPTH, 2, r, cs), F32), name=name,
                          input_output_aliases=aliases, compiler_params=_cp("parallel"))(*operands)


def _mesh_pos():
    return lax.axis_index("x"), lax.axis_index("y"), lax.axis_index("c")


HBM = pl.BlockSpec(memory_space=pltpu.HBM)
SEM = pl.BlockSpec(memory_space=pltpu.SEMAPHORE)
DATAFLOW = pltpu.SideEffectType.DATAFLOW_SIDE_EFFECTING
CHIP_FLIPS = (2, 1, 3)


def _chip_peers():
    x, y, c = _mesh_pos()
    return 2 * x + y, [(1 - x, y, c), (x, 1 - y, c), (1 - x, 1 - y, c)], (x, y, 1 - c), c


def _split_start(arrays, n_copies, issue, *, name, deps=()):
    k = len(arrays)
    nd = len(deps)

    def body(*refs):
        issue(refs[:k], refs[k + nd], refs[k + nd + 1])
        refs[2 * k + nd + 2][...] = jnp.zeros((8, 128), F32)

    out = pl.pallas_call(
        body, name=name,
        out_shape=(pltpu.SemaphoreType.DMA((n_copies,)), pltpu.SemaphoreType.DMA((n_copies,)),
                   *[pltpu.HBM(a.shape, a.dtype) for a in arrays], SDS((8, 128), F32)),
        in_specs=[HBM] * k + [ANY] * nd, out_specs=(SEM, SEM, *[HBM] * k, pl.BlockSpec(memory_space=pltpu.VMEM)),
        input_output_aliases={i: 2 + i for i in range(k)},
        compiler_params=pltpu.CompilerParams(has_side_effects=DATAFLOW))(
            *[pltpu.with_memory_space_constraint(a, pltpu.HBM) for a in arrays], *deps)
    return (out[0], out[1]), list(out[2:2 + k]), out[2 + k]


def _split_wait(sems, arrays, after, waits, *, name):
    k = len(arrays)
    afters = tuple(after) if isinstance(after, (tuple, list)) else (after,)

    def body(*refs):
        waits(refs[:k], refs[k], refs[k + 1])

    out = pl.pallas_call(
        body, name=name, out_shape=tuple(pltpu.HBM(a.shape, a.dtype) for a in arrays),
        in_specs=[HBM] * k + [SEM, SEM] + [ANY] * len(afters), out_specs=tuple([HBM] * k),
        input_output_aliases={i: i for i in range(k)},
        compiler_params=pltpu.CompilerParams(has_side_effects=DATAFLOW))(*arrays, sems[0], sems[1], *afters)
    return list(out)


def _wait_both(cp):
    cp.wait_send()
    cp.wait_recv()


def _cast_place(shard, pos, dtype, *, name, layer=None, slots=N_CHIPS, which=0):
    R, C = shard.shape[-2:]
    tr = R if R % 8 else _row_tile(R, 256)
    if layer is None:
        in_spec = pl.BlockSpec((tr, C), lambda i, pos: (i, 0))
    else:
        in_spec = pl.BlockSpec((None, tr, C), lambda i, pos: (layer, i, 0))

    def body(pos_ref, x_ref, o_ref):
        o_ref[...] = x_ref[...].astype(o_ref.dtype)

    grid_spec = pltpu.PrefetchScalarGridSpec(
        num_scalar_prefetch=1, grid=(R // tr,), in_specs=[in_spec],
        out_specs=pl.BlockSpec((None, tr, C), lambda i, pos: (pos[which], i, 0)))
    return pl.pallas_call(body, grid_spec=grid_spec, out_shape=SDS((slots, R, C), dtype), name=name,
                          compiler_params=_cp("parallel"))(pos, shard)


def _device_peers():
    x, y, c = _mesh_pos()
    peers = [(x ^ ((f >> 2) & 1), y ^ ((f >> 1) & 1), c ^ (f & 1)) for f in range(1, N_DEV)]
    return 4 * x + 2 * y + c, peers


class _Gather:
    def __init__(self, lands, name, deps=(), all_devices=False):
        n = len(lands)
        self.name = name
        npeer = N_DEV - 1 if all_devices else N_CHIPS - 1

        def copies(refs, ss, rs):
            me, peers = _device_peers() if all_devices else _chip_peers()[:2]
            return [pltpu.make_async_remote_copy(
                src_ref=refs[w].at[me], dst_ref=refs[w].at[me], send_sem=ss.at[npeer * w + p],
                recv_sem=rs.at[npeer * w + p], device_id=peers[p], device_id_type=MESH)
                for w in range(n) for p in range(npeer)]

        def issue(refs, ss, rs):
            for cp in copies(refs, ss, rs):
                cp.start()

        def waits(refs, ss, rs):
            for cp in copies(refs, ss, rs):
                _wait_both(cp)

        self._waits = waits
        self.sems, self.arrays, self.token = _split_start(list(lands), npeer * n, issue, name=name + "_start",
                                                          deps=deps)

    def wait(self, after):
        return _split_wait(self.sems, self.arrays, after, self._waits, name=self.name + "_wait")


def _swap_halves_start(g4s, *, name):
    n = len(g4s)
    lands = [lax.empty((g.shape[0],) + g.shape[2:], g.dtype) for g in g4s]

    def copies(refs, ss, rs):
        _, _, sibling, c = _chip_peers()
        return [pltpu.make_async_remote_copy(
            src_ref=refs[w].at[:, 1 - c], dst_ref=refs[n + w], send_sem=ss.at[w], recv_sem=rs.at[w],
            device_id=sibling, device_id_type=MESH) for w in range(n)]

    def issue(refs, ss, rs):
        for cp in copies(refs, ss, rs):
            cp.start()

    def waits(refs, ss, rs):
        for cp in copies(refs, ss, rs):
            _wait_both(cp)

    sems, arrays, token = _split_start(list(g4s) + lands, n, issue, name=name + "_start")
    return sems, arrays, token, waits


def _scatter_start(parts, *, name, deps=()):
    n = len(parts)
    lands = [lax.empty((3,) + p.shape[1:], p.dtype) for p in parts]

    def copies(refs, ss, rs):
        me, peers, _, _ = _chip_peers()
        return [pltpu.make_async_remote_copy(
            src_ref=refs[w].at[me ^ CHIP_FLIPS[p]], dst_ref=refs[n + w].at[p],
            send_sem=ss.at[3 * w + p], recv_sem=rs.at[3 * w + p], device_id=peers[p], device_id_type=MESH)
            for w in range(n) for p in range(3)]

    def issue(refs, ss, rs):
        for cp in copies(refs, ss, rs):
            cp.start()

    def waits(refs, ss, rs):
        for cp in copies(refs, ss, rs):
            _wait_both(cp)

    sems, arrays, token = _split_start(list(parts) + lands, 3 * n, issue, name=name + "_start", deps=deps)
    return sems, arrays, token, waits


def _pair_share_start(fs, layer, *, name):
    n = len(fs)

    def copies(refs, ss, rs):
        _, _, sibling, c = _chip_peers()
        return [pltpu.make_async_remote_copy(
            src_ref=refs[w].at[layer, c], dst_ref=refs[w].at[layer, c], send_sem=ss.at[w], recv_sem=rs.at[w],
            device_id=sibling, device_id_type=MESH) for w in range(n)]

    def issue(refs, ss, rs):
        for cp in copies(refs, ss, rs):
            cp.start()

    def waits(refs, ss, rs):
        for cp in copies(refs, ss, rs):
            _wait_both(cp)

    sems, arrays, token = _split_start(list(fs), n, issue, name=name + "_start")
    return sems, arrays, token, waits


BIG = ('w_in', 'w_proj_a', 'w_proj_b', 'w_proj_c', 'w_out', 'w_up', 'w_down')
BIG_SHARD_AXIS = {'w_in': 2, 'w_proj_a': 2, 'w_proj_b': 2, 'w_proj_c': 2, 'w_out': 1, 'w_up': 2, 'w_down': 1}
SMALL = ('norm1', 'q_norm', 'k_norm', 'sinks', 'w_pool', 'pool_scale', 'sgu_v_norm', 'w_s', 'b_s', 'norm2',
         'conv_b', 'conv_w')
WEIGHTS = ('norm1', 'w_in', 'q_norm', 'k_norm', 'sinks', 'w_pool', 'pool_scale', 'sgu_v_norm', 'w_s', 'b_s',
           'w_proj_a', 'w_proj_b', 'w_proj_c', 'w_out', 'norm2', 'w_up', 'conv_w', 'conv_b', 'w_down')


def _rope_tables(positions):
    inv_freq = ROPE_THETA ** (-jnp.arange(0, HEAD_DIM, 2, dtype=F32) / HEAD_DIM)
    ang = positions.astype(F32)[:, None] * inv_freq
    cos, sin = jnp.cos(ang), jnp.sin(ang)
    c = jnp.concatenate([cos, cos], axis=1)
    s = jnp.concatenate([-sin, sin], axis=1)
    return jnp.concatenate([c, c], axis=1), jnp.concatenate([s, s], axis=1)


def _block_diag4(w):
    out = jnp.zeros((POOL_WIDTH, POOL_WIDTH), w.dtype)
    for g in range(4):
        out = lax.dynamic_update_slice(out, w[g], (g * HEAD_DIM, g * HEAD_DIM))
    return out


def _local_step(x, target, cos, sin, sp, sched):
    T = x.shape[0]
    tm1 = min(1024, T)
    tm = min(512, T)
    tr = min(256, T)
    tkt = min(1024, T)
    seg = _seg_matrix(256, HEAD_DIM)
    saved = []
    xl = x
    for l in range(DEPTH):
        p = f"l{l}_"
        c = dict(
            g1=sp['norm1'][l][None], g2=sp['norm2'][l][None],
            wbd=_block_diag4(sp['w_pool'][l]).astype(MXU_DTYPE), scale=sp['pool_scale'][l][None],
            gq=jnp.tile(sp['q_norm'][l], 4)[None], gk=jnp.tile(sp['k_norm'][l], 2)[None],
            sinks=jnp.broadcast_to(sp['sinks'][l][:, None], (N_Q_HEADS, 128)),
            wtril=jnp.tril(sp['w_s'][l]).astype(MXU_DTYPE),
            bexp=jnp.repeat(sp['b_s'][l].T, HEAD_DIM, axis=1), vn=jnp.tile(sp['sgu_v_norm'][l], 4)[None],
            cb=sp['conv_b'][l][None])
        c['w_in'] = sched.weight('w_in', l, xl)
        z, h1 = _norm_mm(xl, c['g1'], c['w_in'], tm=tm1, tn=1152, name=p + "in_proj",
                         deps=sched.start_tokens() if l == 0 else ())
        pa = _pool_fwd(z, c['wbd'], c['scale'], tr=tr, name=p + "pool")
        q, k, v = _qkv_prep(z, cos, sin, c['gq'], c['gk'], seg, tr=tr, name=p + "qkv_prep")
        at = _attn_fwd(q, k, v, c['sinks'], name=p + "attn")
        sg = _sgu_fwd(z, c['wtril'], c['bexp'], c['vn'], seg, tr=tr, name=p + "sgu")
        for n in ('w_proj_a', 'w_proj_b', 'w_proj_c', 'w_out'):
            c[n] = sched.weight(n, l, (pa, at, sg))
        merged, y3, x1 = _merge_fwd(pa, at, sg, c['w_proj_a'], c['w_proj_b'], c['w_proj_c'], z, xl, c['w_out'],
                                    tm=tm, tn=512, name=p + "merge_out_proj")
        for n in ('w_up', 'conv_w', 'w_down'):
            c[n] = sched.weight(n, l, x1)
        up, h2 = _norm_mm(x1, c['g2'], c['w_up'], tm=tm1, tn=1408, name=p + "up_proj")
        act = _conv_act_fwd(up, c['conv_w'], c['cb'], tr=tr, tc=1408, name=p + "conv_act")
        saved.append(dict(c, x=xl, h1=h1, z=z, pa=pa, q=q, k=k, v=v, at=at, sg=sg, merged=merged, y3=y3,
                          x1=x1, h2=h2, up=up, act=act))
        if l < DEPTH - 1:
            xl = _mm(act, c['w_down'], mode='nn', add=x1, tm=tm, tn=D_MODEL, tk=D_FF, name=p + "down_proj")
        else:
            loss_row, dx, dxb = _down_proj_loss(act, c['w_down'], x1, target, tm=tm, name=p + "down_proj_loss")

    gs = {n: [None] * DEPTH for n in SMALL}
    for l in reversed(range(DEPTH)):
        p = f"l{l}_b_"
        s = saved[l]
        gb = {}
        dact = _mm(dxb, s['w_down'], mode='nt', tm=tm1, tn=1408, tk=D_MODEL, name=p + "down_dx")
        gb['w_down'] = _mm(s['act'], dxb, mode='tn', tm=1408, tn=D_MODEL, tk=tkt, name=p + "down_dw")
        toks = sched.slot(l, 'down', gb['w_down'])
        dup, dwg, dwv, dbg, dbv = _conv_act_bwd(s['up'], s['conv_w'], s['cb'], dact, tr=min(512, T), tc=256,
                                                name=p + "conv_act", deps=toks)
        gs['conv_w'][l] = jnp.concatenate([dwg, dwv], axis=1)
        gs['conv_b'][l] = jnp.concatenate([dbg, dbv], axis=1)[0]
        toks = sched.slot(l, 'conv', dup)
        for half in range(2):
            gb['w_up'] = _mm(s['h2'], dup, mode='tn', b_lead=half, tm=D_MODEL, tn=1408, tk=tkt,
                             out_into=gb.get('w_up'), out_joff=2 * half, out_n=2 * D_FF, name=p + f"up_dw{half}",
                             deps=toks if half == 0 else ())
        toks = sched.slot(l, 'ffn', gb['w_up'], gb)
        dx1, dx1b, dg2 = _mm_nt_sharded_rms(dup, s['w_up'], s['x1'], s['g2'], dx, tm=min(256, T),
                                            name=p + "up_dx_rms2", deps=toks)
        gs['norm2'][l] = dg2[0]
        gb['w_out'] = _mm(s['merged'], dx1b, mode='tn', tm=D_MODEL, tn=D_MODEL, tk=tkt, name=p + "out_dw")
        (dz, dpa, dat, dsg, gb['w_proj_a'], gb['w_proj_b'], gb['w_proj_c']) = _out_dx_merge_bwd(
            dx1b, s['w_out'], s['y3'], s['z'], [s['w_proj_a'], s['w_proj_b'], s['w_proj_c']],
            [s['pa'], s['at'], s['sg']], tm=tm, tn=512, name=p + "out_dx_merge")
        toks = sched.slot(l, 'mid', dz)
        dq, dkc, dkp, dvc, dvp, dsk = _attn_bwd(s['q'], s['k'], s['v'], s['sinks'], dat, name=p + "attn", deps=toks)
        gs['sinks'][l] = dsk[:, 0]
        toks = sched.slot(l, 'attn', dq)
        dz, dgq, dgk, dwbd, dsc = _mixer_ab_bwd(s['z'], cos, sin, s['gq'], s['gk'], seg, dq, dkc, dkp, dvc, dvp,
                                                dpa, s['wbd'], s['scale'], dz, tr=tr, name=p + "qkv_pool", deps=toks)
        gs['q_norm'][l] = dgq[0, :HEAD_DIM]
        gs['k_norm'][l] = dgk[0, :HEAD_DIM]
        gs['w_pool'][l] = jnp.stack([dwbd[g * HEAD_DIM:(g + 1) * HEAD_DIM, g * HEAD_DIM:(g + 1) * HEAD_DIM]
                                     for g in range(4)])
        gs['pool_scale'][l] = dsc[0]
        dz, dws, dbrows, dvn = _sgu_bwd(s['z'], s['wtril'], s['bexp'], s['vn'], seg, dsg, dz, tr=tr, name=p + "sgu")
        gs['w_s'][l] = dws
        gs['b_s'][l] = dbrows[:, ::HEAD_DIM].T
        gs['sgu_v_norm'][l] = dvn[0, :HEAD_DIM]
        gb['w_in'] = _mm(s['h1'], dz, mode='tn', tm=D_MODEL, tn=1152, tk=tkt, name=p + "in_dw")
        toks = sched.slot(l, 'mix', gb['w_in'], gb)
        dx, dxb, dg1 = _mm_nt_sharded_rms(dz, s['w_in'], s['x'], s['g1'], dx1, tm=min(256, T),
                                          name=p + "in_dx_rms1", deps=toks)
        gs['norm1'][l] = dg1[0]
    gs = {n: jnp.stack(v) for n, v in gs.items()}
    return loss_row, dx, gs


GROUP_F = ('w_down', 'w_up')
GROUP_M = ('w_out', 'w_proj_a', 'w_proj_b', 'w_proj_c', 'w_in')
ROW_SHARDED = ('w_out', 'w_down')

REDUCE_PLAN = {
    (1, 'ffn'): (('S1', 'F', 1),),
    (1, 'mid'): (('W1', 'F', 1),),
    (1, 'mix'): (('S1', 'M', 1),),
    (0, 'down'): (('W1', 'M', 1),),
    (0, 'conv'): (('W2', 'F', 1),),
    (0, 'ffn'): (('S1', 'F', 0), ('W3', 'F', 1)),
    (0, 'mid'): (('W1', 'F', 0),),
    (0, 'attn'): (('W2', 'M', 1),),
    (0, 'mix'): (('S1', 'M', 0), ('W3', 'M', 1)),
}
REDUCE_TAIL_A = (('W1', 'M', 0), ('W2', 'F', 0))
REDUCE_TAIL_B = (('W3', 'F', 0),)
REDUCE_TAIL_C = (('W2', 'M', 0), ('W3', 'M', 0))


class _Comm:
    def __init__(self, w, pos):
        self.pos = pos
        groups = {'a': [('w_in', 0)],
                  'b': [(n, 0) for n in ('w_proj_a', 'w_proj_b', 'w_proj_c', 'w_out')],
                  'c': [(n, 0) for n in ('w_up', 'conv_w', 'w_down')],
                  'd': [(n, 1) for n in BIG] + [('conv_w', 1)]}
        self.gathers, self.group_of, self.weights = {}, {}, {}
        self.tokens = []
        for g, ks in groups.items():
            lands = [_cast_place(w[n], pos, F32 if n == 'conv_w' else MXU_DTYPE, layer=l, name=f"gw_place_{n}{l}")
                     for n, l in ks]
            self.gathers[g] = (_Gather(lands, "gw_" + g, deps=self.tokens[-1:]), ks)
            self.tokens.append(self.gathers[g][0].token)
            self.group_of.update({k: g for k in ks})
        self.red = {}
        self.final = {}

    def start_tokens(self):
        return self.tokens[-1:]

    def weight(self, name, layer, after):
        if (name, layer) not in self.weights:
            gather, ks = self.gathers[self.group_of[(name, layer)]]
            for (n, l), full in zip(ks, gather.wait(after)):
                if n == 'conv_w' or n.startswith('w_proj'):
                    full = full.transpose(1, 0, 2).reshape(full.shape[1], -1)
                elif n in ROW_SHARDED:
                    full = full.reshape(-1, full.shape[2])
                self.weights[(n, l)] = full
        return self.weights[(name, layer)]

    def slot(self, layer, slot, after, grads=None):
        tokens = []
        for step, grp, lyr in REDUCE_PLAN.get((layer, slot), ()):
            tok = self._step(step, grp, lyr, after, grads)
            if tok is not None:
                tokens.append(tok)
        return tokens

    def tail(self, steps, after, deps=()):
        toks = (self._step(step, grp, lyr, after, None, deps) for step, grp, lyr in steps)
        return [t for t in toks if t is not None]

    def shards(self):
        return {n: f.reshape(DEPTH, 2 * f.shape[2], f.shape[3]) for n, f in self.final.items()}

    def _step(self, step, grp, layer, after, grads, deps=()):
        names = GROUP_F if grp == 'F' else GROUP_M
        tag = f"{grp.lower()}{layer}"
        st = self.red.setdefault((grp, layer), {})
        n = len(names)
        if step == 'S1':
            g4s = []
            for nm in names:
                g = grads[nm]
                R, C = g.shape
                g4s.append(g.reshape(N_CHIPS, 2, R // (2 * N_CHIPS), C) if nm in ROW_SHARDED
                           else g.reshape(1, 2, R // 2, C))
            st['s1'] = _swap_halves_start(g4s, name="rs1_" + tag)
            return st['s1'][2]
        if step == 'W1':
            sems, arrays, _, waits = st.pop('s1')
            arrays = _split_wait(sems, arrays, after, waits, name=f"rs1_{tag}_wait")
            parts = [_pair_add(arrays[i], arrays[n + i], self.pos, name=f"pair_add_{tag}_{names[i]}")
                     for i in range(n)]
            st['s2'] = _scatter_start(parts, name="rs2_" + tag, deps=deps)
            return st['s2'][2]
        if step == 'W2':
            sems, arrays, _, waits = st.pop('s2')
            arrays = _split_wait(sems, arrays, after, waits, name=f"rs2_{tag}_wait")
            fs = [_chip_sum(arrays[i], arrays[n + i], self.final.get(names[i]), self.pos, layer,
                            name=f"chip_sum_{tag}_{names[i]}") for i in range(n)]
            st['s3'] = _pair_share_start(fs, layer, name="rs3_" + tag)
            return st['s3'][2]
        sems, arrays, _, waits = st.pop('s3')
        self.final.update(zip(names, _split_wait(sems, arrays, after, waits, name=f"rs3_{tag}_wait")))
        return None


def _pack(arrays):
    rows = []
    for a in arrays:
        nel = int(np.prod(a.shape))
        if nel % 1024 == 0:
            rows.append(a.astype(F32).reshape(nel // 128, 128))
        else:
            f = a.reshape(-1).astype(F32)
            rows.append(jnp.pad(f, (0, (-nel) % 1024)).reshape(-1, 128))
    return jnp.concatenate(rows, axis=0)


def _unpack(pack, shapes):
    out, row = [], 0
    for shp in shapes:
        nel = int(np.prod(shp))
        nrow = 8 * -(-nel // 1024)
        part = pack[row:row + nrow]
        out.append(part.reshape(shp) if nel % 1024 == 0 else part.reshape(-1)[:nel].reshape(shp))
        row += nrow
    return out


def kernel(x, positions, norm1, w_in, q_norm, k_norm, sinks, w_pool, pool_scale, sgu_v_norm, w_s, b_s, w_proj_a, w_proj_b, w_proj_c, w_out, norm2, w_up, conv_w, conv_b, w_down, loss_target, m_norm1, m_w_in, m_q_norm, m_k_norm, m_sinks, m_w_pool, m_pool_scale, m_sgu_v_norm, m_w_s, m_b_s, m_w_proj_a, m_w_proj_b, m_w_proj_c, m_w_out, m_norm2, m_w_up, m_conv_w, m_conv_b, m_w_down, v_norm1, v_w_in, v_q_norm, v_k_norm, v_sinks, v_w_pool, v_pool_scale, v_sgu_v_norm, v_w_s, v_b_s, v_w_proj_a, v_w_proj_b, v_w_proj_c, v_w_out, v_norm2, v_w_up, v_conv_w, v_conv_b, v_w_down):
    w = dict(norm1=norm1, w_in=w_in, q_norm=q_norm, k_norm=k_norm, sinks=sinks, w_pool=w_pool, pool_scale=pool_scale,
             sgu_v_norm=sgu_v_norm, w_s=w_s, b_s=b_s, w_proj_a=w_proj_a, w_proj_b=w_proj_b, w_proj_c=w_proj_c,
             w_out=w_out, norm2=norm2, w_up=w_up, conv_w=conv_w, conv_b=conv_b, w_down=w_down)
    m = dict(norm1=m_norm1, w_in=m_w_in, q_norm=m_q_norm, k_norm=m_k_norm, sinks=m_sinks, w_pool=m_w_pool,
             pool_scale=m_pool_scale, sgu_v_norm=m_sgu_v_norm, w_s=m_w_s, b_s=m_b_s, w_proj_a=m_w_proj_a,
             w_proj_b=m_w_proj_b, w_proj_c=m_w_proj_c, w_out=m_w_out, norm2=m_norm2, w_up=m_w_up, conv_w=m_conv_w,
             conv_b=m_conv_b, w_down=m_w_down)
    v = dict(norm1=v_norm1, w_in=v_w_in, q_norm=v_q_norm, k_norm=v_k_norm, sinks=v_sinks, w_pool=v_w_pool,
             pool_scale=v_pool_scale, sgu_v_norm=v_sgu_v_norm, w_s=v_w_s, b_s=v_b_s, w_proj_a=v_w_proj_a,
             w_proj_b=v_w_proj_b, w_proj_c=v_w_proj_c, w_out=v_w_out, norm2=v_norm2, w_up=v_w_up, conv_w=v_conv_w,
             conv_b=v_conv_b, w_down=v_w_down)
    chip = 2 * lax.axis_index("x") + lax.axis_index("y")
    core = lax.axis_index("c")

    pos = jnp.stack([chip, core, 2 * chip + core]).astype(jnp.int32)
    comm = _Comm(w, pos)

    cos, sin = _rope_tables(positions[0])
    sp = {n: w[n] for n in SMALL if n != 'conv_w'}
    loss_row, dx, gs = _local_step(x[0], loss_target[0], cos, sin, sp, comm)

    delta, new_m, new_v, grad_out = {}, {}, {}, {}

    def adamw_big(names, grads):
        for n in names:
            shp = w[n].shape
            two_d = lambda a: a.reshape(shp[0] * shp[1], shp[2])
            d, nm, nv, g = _adamw(two_d(w[n]), two_d(grads[n]), two_d(m[n]), two_d(v[n]),
                                  tr=_row_tile(shp[0] * shp[1], 256), name=f"adamw_{n}", copy_g=True)
            delta[n], new_m[n], new_v[n], grad_out[n] = d.reshape(shp), nm.reshape(shp), nv.reshape(shp), g.reshape(shp)

    small_shapes = [gs[n].shape for n in SMALL] + [(1,)]
    small_pack = _pack([gs[n] for n in SMALL] + [loss_row[0, :1]])
    small = _Gather([_cast_place(small_pack, pos, F32, slots=N_DEV, which=2, name="small_place")], "small_gather",
                    all_devices=True)
    toks = comm.tail(REDUCE_TAIL_A[:1], (dx, small.token))
    comm.tail(REDUCE_TAIL_A[1:], (dx, *toks))
    comm.tail(REDUCE_TAIL_B, dx)
    adamw_big(GROUP_F, comm.shards())
    red = _sum_slots(small.wait(new_v[GROUP_F[-1]])[0], tr=small_pack.shape[0], name="small_sum")
    *small_grads, loss = _unpack(red, small_shapes)
    g_small = dict(zip(SMALL, small_grads))
    comm.tail(REDUCE_TAIL_C, red)
    grads = comm.shards()
    grads.update(g_small)
    shard_cols = conv_w.shape[2]
    grads['conv_w'] = lax.dynamic_slice_in_dim(g_small['conv_w'], chip * shard_cols, shard_cols, axis=2)

    adamw_big(GROUP_M, grads)
    shapes = [w[n].shape for n in SMALL]
    packs = [_pack([src[n] for n in SMALL]) for src in (w, grads, m, v)]
    d, nm, nv = _adamw(*packs, tr=packs[0].shape[0], name="adamw_small")
    for dst, src in ((delta, d), (new_m, nm), (new_v, nv)):
        dst.update(zip(SMALL, _unpack(src, shapes)))

    grads.update(grad_out)
    return (loss[0], dx[None], *[grads[n] for n in WEIGHTS], *[delta[n] for n in WEIGHTS],
            *[new_m[n] for n in WEIGHTS], *[new_v[n] for n in WEIGHTS])
```

```python
import functools
import math

import numpy as np
import jax
import jax.numpy as jnp
from jax import lax
from jax.experimental import pallas as pl
from jax.experimental.pallas import tpu as pltpu

F32 = jnp.float32
MXU_DTYPE = jnp.bfloat16
COMM_DTYPE = jnp.bfloat16

D_MODEL = 1024
DEPTH = 2
HEAD_DIM = 64
POOL_WINDOWS = (2, 4, 8, 16)
POOL_WIDTH = 256
N_Q_HEADS = 8
ATTN_BLOCK = 128
ATTN_WIDTH = 512
KV_WIDTH = 128
CHUNK = 128
SGU_WIDTH = 256
IN_COLS = 4608
GATE_COL0 = 1536
D_FF = 2816
ROPE_THETA = 10000.0
EPS = 1e-6
ADAM_LR, ADAM_B1, ADAM_B2, ADAM_EPS, ADAM_WD, ADAM_STEP = 0.001, 0.9, 0.999, 1e-08, 0.01, 10

N_CHIPS = 4
N_DEV = 8
VMEM_LIMIT_BYTES = 56 * 1024 * 1024
NEG_BIG = -1e30
MESH = pl.DeviceIdType.MESH
ANY = pl.BlockSpec(memory_space=pl.ANY)

SDS = jax.ShapeDtypeStruct


def _cp(*sem):
    return pltpu.CompilerParams(dimension_semantics=sem, vmem_limit_bytes=VMEM_LIMIT_BYTES)


def _dot(a, b, dims=((1,), (0,))):
    return lax.dot_general(a.astype(MXU_DTYPE), b.astype(MXU_DTYPE), (dims, ((), ())),
                           preferred_element_type=F32)


NT = ((1,), (1,))
TN = ((0,), (0,))


def _split_dot(x, m):
    hi = x.astype(MXU_DTYPE)
    lo = (x - hi.astype(F32)).astype(MXU_DTYPE)
    return _dot(hi, m) + _dot(lo, m)


def _seg_matrix(width, seg):
    idx = np.arange(width) // seg
    return jnp.asarray((idx[:, None] == idx[None, :]).astype(np.float32), dtype=MXU_DTYPE)


def _lane(shape):
    return lax.broadcasted_iota(jnp.int32, shape, len(shape) - 1)


def _row(shape):
    return lax.broadcasted_iota(jnp.int32, shape, 0)


def _full(shape):
    nd = len(shape)
    return pl.BlockSpec(shape, lambda *_: (0,) * nd)


def _gelu(x):
    k = math.sqrt(2.0 / math.pi)
    th = jnp.tanh(k * (x + 0.044715 * (x * x * x)))
    return 0.5 * x * (1.0 + th)


def _gelu_and_grad(x):
    k = math.sqrt(2.0 / math.pi)
    x2 = x * x
    th = jnp.tanh(k * (x + 0.044715 * (x2 * x)))
    g = 0.5 * x * (1.0 + th)
    dg = 0.5 * (1.0 + th) + 0.5 * x * (1.0 - th * th) * (k * (1.0 + 3.0 * 0.044715 * x2))
    return g, dg


def _sigmoid(x):
    return 0.5 * jnp.tanh(0.5 * x) + 0.5


def _swap_halves(x):
    w = x.shape[-1]
    first = (_lane(x.shape) % HEAD_DIM) < (HEAD_DIM // 2)
    return jnp.where(first, pltpu.roll(x, w - HEAD_DIM // 2, 1), pltpu.roll(x, HEAD_DIM // 2, 1))


def _tile_lanes(x, reps):
    return x if reps == 1 else jnp.concatenate([x] * reps, axis=1)


def _fold_lanes(x, period):
    w = x.shape[-1]
    while w > period:
        w //= 2
        x = x + pltpu.roll(x, w, 1)
    return x


def _mm(a, b, *, mode, tm, tn, tk, out_dtype=F32, add=None, name,
        a_lead=None, b_lead=None, b_sharded=False, out_into=None,
        b_koff=0, out_joff=0, out_n=None, deps=()):
    ash = a.shape[1:] if a_lead is not None else a.shape
    bsh = b.shape[1:] if b_lead is not None else b.shape
    if b_sharded:
        bsh = (b.shape[1], N_CHIPS * b.shape[2])
    if mode == 'nn':
        (M, K), (K2, N) = ash, bsh
    elif mode == 'nt':
        (M, K), (N, K2) = ash, bsh
    else:
        (K, M), (K2, N) = ash, bsh
    assert K == K2 or (mode == 'nt' and K2 > K), (ash, bsh, mode)
    assert M % tm == 0 and N % tn == 0 and K % tk == 0, (M, N, K, tm, tn, tk)
    nk = K // tk
    dims = {'nn': ((1,), (0,)), 'nt': NT, 'tn': TN}[mode]

    def lead(spec_shape, imap, lead_idx):
        if lead_idx is None:
            return pl.BlockSpec(spec_shape, imap)
        return pl.BlockSpec((None,) + spec_shape, lambda i, j, k: (lead_idx,) + imap(i, j, k))

    if mode == 'tn':
        a_spec = lead((tk, tm), lambda i, j, k: (k, i), a_lead)
    else:
        a_spec = lead((tm, tk), lambda i, j, k: (i, k), a_lead)
    if b_sharded:
        per = b.shape[2] // (tk if mode == 'nt' else tn)
        assert per * (tk if mode == 'nt' else tn) == b.shape[2] and mode != 'tn'
        if mode == 'nt':
            b_spec = pl.BlockSpec((None, tn, tk), lambda i, j, k: ((k + b_koff) // per, j, (k + b_koff) % per))
        else:
            b_spec = pl.BlockSpec((None, tk, tn), lambda i, j, k: (j // per, k, j % per))
    elif mode == 'nt':
        b_spec = lead((tn, tk), lambda i, j, k: (j, k + b_koff), b_lead)
    else:
        b_spec = lead((tk, tn), lambda i, j, k: (k, j), b_lead)
    o_spec = pl.BlockSpec((tm, tn), lambda i, j, k: (i, j + out_joff))
    n_out = N if out_n is None else out_n
    in_specs = [a_spec, b_spec]
    operands = [a, b]
    if add is not None:
        in_specs.append(pl.BlockSpec((tm, tn), lambda i, j, k: (i, j)))
        operands.append(add)
    aliases = {}
    if out_into is not None:
        in_specs.append(ANY)
        operands.append(out_into)
        aliases = {len(operands) - 1: 0}
    in_specs += [ANY] * len(deps)
    operands += list(deps)
    has_add = add is not None
    acc_in_out = nk > 1 and out_dtype == F32

    def body(*refs):
        a_ref, b_ref = refs[0], refs[1]
        pos = 2
        add_ref = None
        if has_add:
            add_ref = refs[pos]
            pos += 1
        if out_into is not None:
            pos += 1
        pos += len(deps)
        o_ref = refs[pos]
        acc_ref = refs[pos + 1] if (nk > 1 and not acc_in_out) else None
        p = _dot(a_ref[...], b_ref[...], dims)
        if nk == 1:
            if has_add:
                p = p + add_ref[...]
            o_ref[...] = p.astype(o_ref.dtype)
            return
        k = pl.program_id(2)
        tgt = o_ref if acc_in_out else acc_ref

        @pl.when(k == 0)
        def _():
            tgt[...] = p + add_ref[...] if has_add else p

        @pl.when(k > 0)
        def _():
            tgt[...] += p

        if not acc_in_out:
            @pl.when(k == nk - 1)
            def _():
                o_ref[...] = acc_ref[...].astype(o_ref.dtype)

    out_shape = SDS((M, n_out), out_dtype)
    scratch = [pltpu.VMEM((tm, tn), F32)] if (nk > 1 and not acc_in_out) else []
    return pl.pallas_call(
        body, grid=(M // tm, N // tn, nk), in_specs=in_specs, out_specs=o_spec, out_shape=out_shape,
        scratch_shapes=scratch, input_output_aliases=aliases, name=name,
        compiler_params=_cp("parallel", "parallel", "arbitrary"))(*operands)


def _rms_bwd_rows(xv, g, dh, dres):
    r = lax.rsqrt(jnp.mean(xv * xv, axis=-1, keepdims=True) + EPS)
    xh = xv * r
    gy = dh * g
    dx = r * (gy - xh * jnp.mean(xh * gy, axis=-1, keepdims=True)) + dres
    return dx, jnp.sum(dh * xh, axis=0, keepdims=True)


def _mm_nt_sharded_rms(a, b, x, g, dres, *, tm, name, deps=()):
    a3 = a if a.ndim == 3 else a[None]
    A, M, ka = a3.shape
    S, N, ns = b.shape
    per = S // A
    assert ka == per * ns and M % tm == 0 and N == x.shape[1], (a3.shape, b.shape, x.shape)

    def body(a_ref, b_ref, x_ref, g_ref, dres_ref, dx_ref, dxb_ref, dg_ref):
        acc = None
        for s in range(S):
            lo = (s % per) * ns
            p = _dot(a_ref[s // per, :, lo:lo + ns], b_ref[s], NT)
            acc = p if acc is None else acc + p
        dx, dg = _rms_bwd_rows(x_ref[...], g_ref[...], acc, dres_ref[...])
        dx_ref[...] = dx
        dxb_ref[...] = dx.astype(dxb_ref.dtype)

        @pl.when(pl.program_id(0) == 0)
        def _():
            dg_ref[...] = jnp.zeros_like(dg_ref)
        dg_ref[...] += dg

    rows = pl.BlockSpec((tm, N), lambda i: (i, 0))
    return pl.pallas_call(
        _after(body, 5, deps), grid=(M // tm,),
        in_specs=[pl.BlockSpec((A, tm, ka), lambda i: (0, i, 0)), pl.BlockSpec((S, N, ns), lambda i: (0, 0, 0)),
                  rows, _full((1, N)), rows] + [ANY] * len(deps),
        out_specs=[rows, rows, _full((1, N))],
        out_shape=[SDS((M, N), F32), SDS((M, N), MXU_DTYPE), SDS((1, N), F32)], name=name,
        compiler_params=_cp("arbitrary"))(a3, b, x, g, dres, *deps)


def _norm_mm(x, g, b, *, tm, tn, name, deps=()):
    M, K = x.shape
    S, K2, ns = b.shape
    per = ns // tn
    assert K == K2 and per * tn == ns and M % tm == 0, (x.shape, b.shape)

    def body(x_ref, g_ref, b_ref, o_ref, h_ref):
        @pl.when(pl.program_id(1) == 0)
        def _():
            xv = x_ref[...]
            r = lax.rsqrt(jnp.mean(xv * xv, axis=-1, keepdims=True) + EPS)
            h_ref[...] = (xv * r * g_ref[...]).astype(h_ref.dtype)
        o_ref[...] = _dot(h_ref[...], b_ref[...])

    return pl.pallas_call(
        _after(body, 3, deps), grid=(M // tm, S * per),
        in_specs=[pl.BlockSpec((tm, K), lambda i, j: (i, 0)), _full((1, K)),
                  pl.BlockSpec((None, K, tn), lambda i, j: (j // per, 0, j % per))] + [ANY] * len(deps),
        out_specs=[pl.BlockSpec((tm, tn), lambda i, j: (i, j)), pl.BlockSpec((tm, K), lambda i, j: (i, 0))],
        out_shape=[SDS((M, S * ns), F32), SDS((M, K), MXU_DTYPE)], name=name,
        compiler_params=_cp("parallel", "arbitrary"))(x, g, b, *deps)


def _after(body, n_in, deps):
    nd = len(deps)
    if nd == 0:
        return body
    return lambda *refs: body(*refs[:n_in], *refs[n_in + nd:])


def _down_proj_loss(act, w, x1, target, *, tm, name):
    T, K = act.shape
    D = w.shape[1]

    def body(a_ref, w_ref, x_ref, t_ref, loss_ref, dy_ref, dyb_ref):
        i = pl.program_id(0)
        d = (x_ref[...] + _dot(a_ref[...], w_ref[...])) - t_ref[...]
        dy = d * (1.0 / D)
        dy_ref[...] = dy
        dyb_ref[...] = dy.astype(dyb_ref.dtype)
        part = jnp.sum(jnp.sum(d * d, axis=1, keepdims=True), axis=0, keepdims=True) * (0.5 / D)

        @pl.when(i == 0)
        def _():
            loss_ref[...] = jnp.zeros_like(loss_ref)
        loss_ref[...] += jnp.broadcast_to(part, loss_ref.shape)

    rows = pl.BlockSpec((tm, D), lambda i: (i, 0))
    return pl.pallas_call(
        body, grid=(T // tm,), in_specs=[pl.BlockSpec((tm, K), lambda i: (i, 0)), _full((K, D)), rows, rows],
        out_specs=[_full((1, 128)), rows, rows],
        out_shape=[SDS((1, 128), F32), SDS((T, D), F32), SDS((T, D), MXU_DTYPE)],
        name=name, compiler_params=_cp("arbitrary"))(act, w, x1, target)


def _pool_lane_consts(shape):
    lane = _lane(shape)
    grp = lane // (POOL_WIDTH // 4)
    win = jnp.where(grp == 0, 2, jnp.where(grp == 1, 4, jnp.where(grp == 2, 8, 16)))
    return grp, win


def _pool_select(grp, s2, s4, s8, s16):
    return jnp.where(grp == 0, s2, jnp.where(grp == 1, s4, jnp.where(grp == 2, s8, s16)))


def _pool_diff(xe, row0, tr):
    s2 = xe + pltpu.roll(xe, 1, 0)
    s4 = s2 + pltpu.roll(s2, 2, 0)
    s8 = s4 + pltpu.roll(s4, 4, 0)
    s16 = s8 + pltpu.roll(s8, 8, 0)
    shape = (tr, POOL_WIDTH)
    grp, win = _pool_lane_consts(shape)
    sums = _pool_select(grp, s2[16:], s4[16:], s8[16:], s16[16:])
    t = row0 + _row(shape)
    cnt = jnp.minimum(t + 1, win).astype(F32)
    return sums / cnt - xe[16:]


def _pool_fwd(z, wbd, scale, *, tr, name):
    T = z.shape[0]
    hb = tr // 16

    def body(x_ref, xp_ref, w_ref, s_ref, o_ref):
        i = pl.program_id(0)
        halo = jnp.where(i == 0, 0.0, xp_ref[...])
        diff = _pool_diff(jnp.concatenate([halo, x_ref[...]], axis=0), i * tr, tr)
        o_ref[...] = (_dot(diff, w_ref[...]) * s_ref[...]).astype(o_ref.dtype)

    return pl.pallas_call(
        body, grid=(T // tr,),
        in_specs=[pl.BlockSpec((tr, POOL_WIDTH), lambda i: (i, 0)),
                  pl.BlockSpec((16, POOL_WIDTH), lambda i: (jnp.maximum(i * hb - 1, 0), 0)),
                  _full((POOL_WIDTH, POOL_WIDTH)), _full((1, POOL_WIDTH))],
        out_specs=pl.BlockSpec((tr, POOL_WIDTH), lambda i: (i, 0)),
        out_shape=SDS((T, POOL_WIDTH), MXU_DTYPE), name=name, compiler_params=_cp("parallel"))(z, z, wbd, scale)


def _pool_bwd_tile(i, n, tr, x, xprev, dpa, dpa_next, wbd, scale):
    halo = jnp.where(i == 0, 0.0, xprev)
    diff = _pool_diff(jnp.concatenate([halo, x], axis=0), i * tr, tr)
    mixed = _dot(diff, wbd)
    dscale = jnp.sum(dpa * mixed, axis=0, keepdims=True)
    dnext = jnp.where(i == n - 1, 0.0, dpa_next)
    dmix_e = jnp.concatenate([dpa, dnext], axis=0) * scale
    ddiff_e = _dot(dmix_e, wbd, NT)
    dwbd = _dot(diff, dmix_e[:tr], TN)
    shape = (tr + 16, POOL_WIDTH)
    grp, win = _pool_lane_consts(shape)
    t = i * tr + _row(shape)
    e = ddiff_e / jnp.minimum(t + 1, win).astype(F32)
    nrow = tr + 16
    a2 = e + pltpu.roll(e, nrow - 1, 0)
    a4 = a2 + pltpu.roll(a2, nrow - 2, 0)
    a8 = a4 + pltpu.roll(a4, nrow - 4, 0)
    a16 = a8 + pltpu.roll(a8, nrow - 8, 0)
    dx = _pool_select(grp, a2, a4, a8, a16)[:tr] - ddiff_e[:tr]
    return dx, dwbd, dscale


def _norm_rope(x, g, cos, sin_signed, seg):
    reps = x.shape[1] // 128
    ms = _split_dot(x * x, seg) * (1.0 / HEAD_DIM)
    r = lax.rsqrt(ms + EPS)
    xn = x * r * g
    c, s = _tile_lanes(cos, reps), _tile_lanes(sin_signed, reps)
    return xn * c + _swap_halves(xn) * s


def _norm_rope_bwd(x, g, cos, sin_signed, seg, dout):
    reps = x.shape[1] // 128
    c, s = _tile_lanes(cos, reps), _tile_lanes(sin_signed, reps)
    dxn = dout * c + _swap_halves(dout * s)
    ms = _split_dot(x * x, seg) * (1.0 / HEAD_DIM)
    r = lax.rsqrt(ms + EPS)
    xh = x * r
    gy = dxn * g
    dx = r * (gy - xh * (_split_dot(xh * gy, seg) * (1.0 / HEAD_DIM)))
    dg = jnp.sum(dxn * xh, axis=0, keepdims=True)
    return dx, dg


def _dup_heads(k):
    first = _lane(k.shape) < HEAD_DIM
    kr = pltpu.roll(k, HEAD_DIM, 1)
    return jnp.concatenate([jnp.where(first, k, kr), jnp.where(first, kr, k)], axis=1)


def _qkv_prep(z, cos, sin_signed, gq, gk, seg, *, tr, name):
    T = z.shape[0]

    def body(qa_ref, qb_ref, kv_ref, c_ref, s_ref, gq_ref, gk_ref, seg_ref, q_ref, k_ref, v_ref):
        c, s, seg_m = c_ref[...], s_ref[...], seg_ref[...]
        scale = HEAD_DIM ** -0.5
        qa = _norm_rope(qa_ref[...], gq_ref[...], c, s, seg_m) * scale
        qb = _norm_rope(qb_ref[...], gq_ref[...], c, s, seg_m) * scale
        q_ref[...] = jnp.concatenate([qa, qb], axis=1).astype(q_ref.dtype)
        kv = kv_ref[...]
        k = _norm_rope(kv[:, :KV_WIDTH], gk_ref[...], c, s, seg_m[:128, :128])
        k_ref[...] = _dup_heads(k).astype(k_ref.dtype)
        v_ref[...] = _dup_heads(kv[:, KV_WIDTH:]).astype(v_ref.dtype)

    col = lambda j: pl.BlockSpec((tr, 256), lambda i: (i, j))
    tab = pl.BlockSpec((tr, 128), lambda i: (i, 0))
    return pl.pallas_call(
        body, grid=(T // tr,),
        in_specs=[col(1), col(2), col(3), tab, tab, _full((1, 256)), _full((1, 128)), _full((256, 256))],
        out_specs=[pl.BlockSpec((tr, 512), lambda i: (i, 0)), col(0), col(0)],
        out_shape=[SDS((T, 512), MXU_DTYPE), SDS((T, 256), MXU_DTYPE), SDS((T, 256), MXU_DTYPE)],
        name=name, compiler_params=_cp("parallel"))(z, z, z, cos, sin_signed, gq, gk, seg)


GROUP_HEADS = 4
GROUP_ROWS = GROUP_HEADS * ATTN_BLOCK
ALL_ROWS = N_Q_HEADS * ATTN_BLOCK


def _attn_mask(has_prev):
    qi = _row((ALL_ROWS, 2 * ATTN_BLOCK)) % ATTN_BLOCK
    kj = _lane((ALL_ROWS, 2 * ATTN_BLOCK))
    return (kj > qi) & (kj <= qi + ATTN_BLOCK) & ((kj >= ATTN_BLOCK) | has_prev)


STEP_BLOCKS = 2
STEP_ROWS = STEP_BLOCKS * ATTN_BLOCK


def _band(prev, cur, blk):
    lo = cur[(blk - 1) * ATTN_BLOCK:blk * ATTN_BLOCK] if blk else prev
    return jnp.concatenate([lo, cur[blk * ATTN_BLOCK:(blk + 1) * ATTN_BLOCK]], axis=0)


def _stack_heads(x, g):
    first = _lane((ATTN_BLOCK, 128)) < HEAD_DIM
    parts = []
    for pair in (2 * g, 2 * g + 1):
        x128 = x[:, 128 * pair:128 * (pair + 1)]
        zero = jnp.zeros_like(x128)
        parts += [jnp.where(first, x128, zero), jnp.where(first, zero, x128)]
    return jnp.concatenate(parts, axis=0)


def _unstack_heads(y):
    first = _lane((ATTN_BLOCK, 128)) < HEAD_DIM
    b = ATTN_BLOCK
    return jnp.concatenate([jnp.where(first, y[0:b], y[b:2 * b]), jnp.where(first, y[2 * b:3 * b], y[3 * b:4 * b])],
                           axis=1)


def _sink_col(sk_ref):
    return jnp.concatenate([jnp.broadcast_to(sk_ref[h:h + 1, 0:1], (ATTN_BLOCK, 1)) for h in range(N_Q_HEADS)],
                           axis=0)


def _by_group(a8, b2, dims=((1,), (0,))):
    return jnp.concatenate([_dot(a8[:GROUP_ROWS], b2[:, :128], dims), _dot(a8[GROUP_ROWS:], b2[:, 128:], dims)],
                           axis=0)


def _softmax_exp(q8, k2, mask, sink):
    s = jnp.where(mask, _by_group(q8, k2, NT), NEG_BIG)
    m = jnp.maximum(jnp.max(s, axis=1, keepdims=True), sink)
    p = jnp.exp(s - m)
    ps = jnp.exp(sink - m)
    return p, ps, 1.0 / (jnp.sum(p, axis=1, keepdims=True) + ps)


def _attn_fwd(q, k, v, sinks_b, *, name):
    T = q.shape[0]
    nb = T // ATTN_BLOCK

    def body(q_ref, kc_ref, kp_ref, vc_ref, vp_ref, sk_ref, o_ref):
        n = pl.program_id(0)
        kc, kp, vc, vp = kc_ref[...], kp_ref[...], vc_ref[...], vp_ref[...]
        sink = _sink_col(sk_ref)
        for blk in range(STEP_BLOCKS):
            rows = slice(blk * ATTN_BLOCK, (blk + 1) * ATTN_BLOCK)
            mask = _attn_mask((n > 0) if blk == 0 else True)
            k2, v2 = _band(kp, kc, blk), _band(vp, vc, blk)
            qv = q_ref[rows, :]
            q8 = jnp.concatenate([_stack_heads(qv, 0), _stack_heads(qv, 1)], axis=0)
            p, _, inv = _softmax_exp(q8, k2, mask, sink)
            o8 = _by_group(p, v2) * inv
            o_ref[rows, :] = jnp.concatenate([_unstack_heads(o8[:GROUP_ROWS]), _unstack_heads(o8[GROUP_ROWS:])],
                                             axis=1).astype(o_ref.dtype)

    cur = lambda w: pl.BlockSpec((STEP_ROWS, w), lambda n: (n, 0))
    prev = lambda w: pl.BlockSpec((ATTN_BLOCK, w), lambda n: (jnp.maximum(STEP_BLOCKS * n - 1, 0), 0))
    return pl.pallas_call(
        body, grid=(nb // STEP_BLOCKS,),
        in_specs=[cur(512), cur(256), prev(256), cur(256), prev(256), _full((8, 128))],
        out_specs=cur(512), out_shape=SDS((T, 512), MXU_DTYPE), name=name,
        compiler_params=_cp("parallel"))(q, k, k, v, v, sinks_b)


def _attn_bwd(q, k, v, sinks_b, do, *, name, deps=()):
    T = q.shape[0]
    nb = T // ATTN_BLOCK

    def body(q_ref, kc_ref, kp_ref, vc_ref, vp_ref, sk_ref, do_ref,
             dq_ref, dkc_ref, dkp_ref, dvc_ref, dvp_ref, dsk_ref):
        n = pl.program_id(0)
        kc, kp, vc, vp = kc_ref[...], kp_ref[...], vc_ref[...], vp_ref[...]
        sink = _sink_col(sk_ref)

        @pl.when(n == 0)
        def _():
            dsk_ref[...] = jnp.zeros_like(dsk_ref)

        for blk in range(STEP_BLOCKS):
            rows = slice(blk * ATTN_BLOCK, (blk + 1) * ATTN_BLOCK)
            mask = _attn_mask((n > 0) if blk == 0 else True)
            k2, v2 = _band(kp, kc, blk), _band(vp, vc, blk)
            qv, dov = q_ref[rows, :], do_ref[rows, :]
            q8 = jnp.concatenate([_stack_heads(qv, 0), _stack_heads(qv, 1)], axis=0)
            do8 = jnp.concatenate([_stack_heads(dov, 0), _stack_heads(dov, 1)], axis=0)
            p, ps, inv = _softmax_exp(q8, k2, mask, sink)
            pn = p * inv
            delta = jnp.sum(do8 * _by_group(pn, v2), axis=1, keepdims=True)
            ds = pn * (_by_group(do8, v2, NT) - delta)
            dq8 = _by_group(ds, k2)
            dq_ref[rows, :] = jnp.concatenate([_unstack_heads(dq8[:GROUP_ROWS]), _unstack_heads(dq8[GROUP_ROWS:])],
                                              axis=1)
            dk = jnp.concatenate([_dot(ds[:GROUP_ROWS], q8[:GROUP_ROWS], TN),
                                  _dot(ds[GROUP_ROWS:], q8[GROUP_ROWS:], TN)], axis=1)
            dv = jnp.concatenate([_dot(pn[:GROUP_ROWS], do8[:GROUP_ROWS], TN),
                                  _dot(pn[GROUP_ROWS:], do8[GROUP_ROWS:], TN)], axis=1)
            wsink = (ps * inv) * delta
            for h in range(N_Q_HEADS):
                dsink = -jnp.sum(wsink[ATTN_BLOCK * h:ATTN_BLOCK * (h + 1)], axis=0, keepdims=True)
                dsk_ref[h:h + 1, :] += jnp.broadcast_to(dsink, (1, 128))
            dkp_ref[rows, :] = dk[:ATTN_BLOCK]
            dkc_ref[rows, :] = dk[ATTN_BLOCK:]
            dvp_ref[rows, :] = dv[:ATTN_BLOCK]
            dvc_ref[rows, :] = dv[ATTN_BLOCK:]

    cur = lambda w: pl.BlockSpec((STEP_ROWS, w), lambda n: (n, 0))
    prev = lambda w: pl.BlockSpec((ATTN_BLOCK, w), lambda n: (jnp.maximum(STEP_BLOCKS * n - 1, 0), 0))
    f = lambda w: SDS((T, w), F32)
    return pl.pallas_call(
        _after(body, 7, deps), grid=(nb // STEP_BLOCKS,),
        in_specs=[cur(512), cur(256), prev(256), cur(256), prev(256), _full((8, 128)), cur(512)] + [ANY] * len(deps),
        out_specs=[cur(512), cur(256), cur(256), cur(256), cur(256), _full((8, 128))],
        out_shape=[f(512), f(256), f(256), f(256), f(256), SDS((8, 128), F32)],
        name=name, compiler_params=_cp("arbitrary"))(q, k, k, v, v, sinks_b, do, *deps)


def _mixer_ab_bwd(z, cos, sin_signed, gq, gk, seg, dq, dkc, dkp, dvc, dvp, dpa, wbd, scale, dz, *, tr, name, deps=()):
    T = z.shape[0]
    n = T // tr
    hb = tr // 16
    ab = tr // ATTN_BLOCK

    def unfold(cur, nxt_tile, nxt_halo, i):
        nxt = jnp.concatenate([nxt_tile[ATTN_BLOCK:], jnp.where(i == n - 1, 0.0, nxt_halo)], axis=0)
        tot = cur + nxt
        first = _lane((tr, 128)) < HEAD_DIM
        a = tot[:, :128]
        b = tot[:, 128:]
        a = a + pltpu.roll(a, HEAD_DIM, 1)
        b = b + pltpu.roll(b, HEAD_DIM, 1)
        return jnp.where(first, a, b)

    def body(xp_ref, xpp_ref, qa_ref, qb_ref, kv_ref, c_ref, s_ref, gq_ref, gk_ref, seg_ref,
             dq_ref, dkc_ref, dkp_ref, dkh_ref, dvc_ref, dvp_ref, dvh_ref, dpa_ref, dpan_ref, w_ref, sc_ref, _dz_in,
             dz_ref, dgq_ref, dgk_ref, dw_ref, dsc_ref):
        i = pl.program_id(0)
        c, s, seg_m = c_ref[...], s_ref[...], seg_ref[...]
        scale_q = HEAD_DIM ** -0.5
        dqv = dq_ref[...] * scale_q
        dxa, dga = _norm_rope_bwd(qa_ref[...], gq_ref[...], c, s, seg_m, dqv[:, :256])
        dxb, dgb = _norm_rope_bwd(qb_ref[...], gq_ref[...], c, s, seg_m, dqv[:, 256:])
        dk = unfold(dkc_ref[...], dkp_ref[...], dkh_ref[...], i)
        dv = unfold(dvc_ref[...], dvp_ref[...], dvh_ref[...], i)
        kv = kv_ref[...]
        dxk, dgk = _norm_rope_bwd(kv[:, :KV_WIDTH], gk_ref[...], c, s, seg_m[:128, :128], dk)
        dxp, dwbd, dscale = _pool_bwd_tile(i, n, tr, xp_ref[...], xpp_ref[...], dpa_ref[...], dpan_ref[...],
                                           w_ref[...], sc_ref[...])
        dz_ref[...] = jnp.concatenate([dxp, dxa, dxb, dxk, dv], axis=1).astype(dz_ref.dtype)

        @pl.when(i == 0)
        def _():
            dgq_ref[...] = jnp.zeros_like(dgq_ref)
            dgk_ref[...] = jnp.zeros_like(dgk_ref)
            dw_ref[...] = jnp.zeros_like(dw_ref)
            dsc_ref[...] = jnp.zeros_like(dsc_ref)
        dgq_ref[...] += _fold_lanes(dga + dgb, HEAD_DIM)
        dgk_ref[...] += _fold_lanes(dgk, HEAD_DIM)
        dw_ref[...] += dwbd
        dsc_ref[...] += dscale

    col = lambda j: pl.BlockSpec((tr, 256), lambda i: (i, j))
    rows = lambda w: pl.BlockSpec((tr, w), lambda i: (i, 0))
    nxt_blk = pl.BlockSpec((ATTN_BLOCK, 256), lambda i: (jnp.minimum((i + 1) * ab, T // ATTN_BLOCK - 1), 0))
    prev16 = pl.BlockSpec((16, 256), lambda i: (jnp.maximum(i * hb - 1, 0), 0))
    next16 = pl.BlockSpec((16, 256), lambda i: (jnp.minimum((i + 1) * hb, T // 16 - 1), 0))
    return pl.pallas_call(
        _after(body, 22, deps), grid=(n,),
        in_specs=[col(0), prev16, col(1), col(2), col(3), rows(128), rows(128),
                  _full((1, 256)), _full((1, 128)), _full((256, 256)),
                  rows(512), rows(256), rows(256), nxt_blk, rows(256), rows(256), nxt_blk,
                  rows(256), next16, _full((256, 256)), _full((1, 256)), ANY] + [ANY] * len(deps),
        out_specs=[rows(1024), _full((1, 256)), _full((1, 128)), _full((256, 256)), _full((1, 256))],
        out_shape=[SDS((T, IN_COLS), MXU_DTYPE), SDS((1, 256), F32), SDS((1, 128), F32),
                   SDS((256, 256), F32), SDS((1, 256), F32)],
        input_output_aliases={21: 0}, name=name, compiler_params=_cp("arbitrary"))(
            z, z, z, z, z, cos, sin_signed, gq, gk, seg, dq, dkc, dkp, dkp, dvc, dvp, dvp, dpa, dpa, wbd, scale, dz,
            *deps)


def _sgu_common(zu, zv, vn, seg):
    u, du = _gelu_and_grad(zu)
    gv, dgv = _gelu_and_grad(zv)
    ms = _split_dot(gv * gv, seg) * (1.0 / HEAD_DIM)
    r = lax.rsqrt(ms + EPS)
    xh = gv * r
    return u, du, dgv, r, xh, xh * vn


def _sgu_fwd(z, wtril, bexp, vn, seg, *, tr, name):
    T = z.shape[0]
    nch = tr // CHUNK

    def body(u_ref, v_ref, w_ref, b_ref, vn_ref, seg_ref, o_ref):
        u, _, _, _, _, vg = _sgu_common(u_ref[...], v_ref[...], vn_ref[...], seg_ref[...])
        grp = _lane((CHUNK, SGU_WIDTH)) // HEAD_DIM
        outs = []
        for ch in range(nch):
            vc = vg[ch * CHUNK:(ch + 1) * CHUNK]
            s = b_ref[...]
            for g in range(4):
                s = s + jnp.where(grp == g, _dot(w_ref[g], vc), 0.0)
            outs.append(u[ch * CHUNK:(ch + 1) * CHUNK] * s)
        o_ref[...] = jnp.concatenate(outs, axis=0).astype(o_ref.dtype)

    col = lambda j: pl.BlockSpec((tr, 256), lambda i: (i, j))
    return pl.pallas_call(
        body, grid=(T // tr,),
        in_specs=[col(4), col(5), _full((4, CHUNK, CHUNK)), _full((CHUNK, 256)), _full((1, 256)), _full((256, 256))],
        out_specs=col(0), out_shape=SDS((T, SGU_WIDTH), MXU_DTYPE), name=name,
        compiler_params=_cp("parallel"))(z, z, wtril, bexp, vn, seg)


def _sgu_bwd(z, wtril, bexp, vn, seg, dsg, dz, *, tr, name):
    T = z.shape[0]
    nch = tr // CHUNK

    def body(u_ref, v_ref, w_ref, b_ref, vn_ref, seg_ref, d_ref, _dz_in, dz_ref, dw_ref, db_ref, dvn_ref):
        i = pl.program_id(0)
        seg_m = seg_ref[...]
        vn_v = vn_ref[...]
        u, du, dgv, r, xh, vg = _sgu_common(u_ref[...], v_ref[...], vn_v, seg_m)
        d = d_ref[...]
        grp = _lane((CHUNK, SGU_WIDTH)) // HEAD_DIM
        tril = _row((CHUNK, CHUNK)) >= _lane((CHUNK, CHUNK))

        @pl.when(i == 0)
        def _():
            dw_ref[...] = jnp.zeros_like(dw_ref)
            db_ref[...] = jnp.zeros_like(db_ref)
            dvn_ref[...] = jnp.zeros_like(dvn_ref)

        dus, dvgs = [], []
        for ch in range(nch):
            sl = slice(ch * CHUNK, (ch + 1) * CHUNK)
            vc = vg[sl]
            s = b_ref[...]
            for g in range(4):
                s = s + jnp.where(grp == g, _dot(w_ref[g], vc), 0.0)
            dus.append(d[sl] * s)
            ds = d[sl] * u[sl]
            db_ref[...] += _split_dot(ds, seg_m)
            dvg = jnp.zeros((CHUNK, SGU_WIDTH), F32)
            for g in range(4):
                dsm = jnp.where(grp == g, ds, 0.0)
                dvg = dvg + jnp.where(grp == g, _dot(w_ref[g], ds, TN), 0.0)
                dw_ref[g] += jnp.where(tril, _dot(dsm, vc, NT), 0.0)
            dvgs.append(dvg)
        dup = jnp.concatenate(dus, axis=0)
        dvg = jnp.concatenate(dvgs, axis=0)
        dvn_ref[...] += _fold_lanes(jnp.sum(dvg * xh, axis=0, keepdims=True), HEAD_DIM)
        gy = dvg * vn_v
        dgvv = r * (gy - xh * (_split_dot(xh * gy, seg_m) * (1.0 / HEAD_DIM)))
        dz_ref[...] = jnp.concatenate([dup * du, dgvv * dgv], axis=1).astype(dz_ref.dtype)

    col = lambda j: pl.BlockSpec((tr, 256), lambda i: (i, j))
    return pl.pallas_call(
        body, grid=(T // tr,),
        in_specs=[col(4), col(5), _full((4, CHUNK, CHUNK)), _full((CHUNK, 256)), _full((1, 256)), _full((256, 256)),
                  col(0), ANY],
        out_specs=[pl.BlockSpec((tr, 512), lambda i: (i, 2)), _full((4, CHUNK, CHUNK)), _full((CHUNK, 256)),
                   _full((1, 256))],
        out_shape=[SDS((T, IN_COLS), MXU_DTYPE), SDS((4, CHUNK, CHUNK), F32), SDS((CHUNK, 256), F32),
                   SDS((1, 256), F32)],
        input_output_aliases={7: 0}, name=name, compiler_params=_cp("arbitrary"))(
            z, z, wtril, bexp, vn, seg, dsg, dz)


def _merge_fwd(pa, at, sg, wa, wb, wc, z, x, w_out, *, tm, tn, name):
    T = pa.shape[0]
    gb = GATE_COL0 // tn
    nb = D_MODEL // tn

    def body(pa_ref, at_ref, sg_ref, wa_ref, wb_ref, wc_ref, g0_ref, g1_ref, g2_ref, x_ref, wo_ref,
             m_ref, y_ref, x1_ref):
        j = pl.program_id(1)
        acc = None
        for idx, (op_ref, w_ref, g_ref) in enumerate(((pa_ref, wa_ref, g0_ref), (at_ref, wb_ref, g1_ref),
                                                      (sg_ref, wc_ref, g2_ref))):
            y = _dot(op_ref[...], w_ref[...])
            y_ref[idx] = y.astype(y_ref.dtype)
            t = _sigmoid(g_ref[...]) * y
            acc = t if acc is None else acc + t
        merged = acc.astype(m_ref.dtype)
        m_ref[...] = merged
        p = _dot(merged, wo_ref[...])

        @pl.when(j == 0)
        def _():
            x1_ref[...] = x_ref[...] + p

        @pl.when(j > 0)
        def _():
            x1_ref[...] += p

    op = lambda w: pl.BlockSpec((tm, w), lambda i, j: (i, 0))
    wt = lambda k: pl.BlockSpec((k, tn), lambda i, j: (0, j))
    gate = lambda b: pl.BlockSpec((tm, tn), lambda i, j: (i, gb + b * nb + j))
    return pl.pallas_call(
        body, grid=(T // tm, nb),
        in_specs=[op(256), op(512), op(256), wt(256), wt(512), wt(256), gate(0), gate(1), gate(2),
                  op(D_MODEL), pl.BlockSpec((tn, D_MODEL), lambda i, j: (j, 0))],
        out_specs=[pl.BlockSpec((tm, tn), lambda i, j: (i, j)), pl.BlockSpec((3, tm, tn), lambda i, j: (0, i, j)),
                   op(D_MODEL)],
        out_shape=[SDS((T, D_MODEL), MXU_DTYPE), SDS((3, T, D_MODEL), MXU_DTYPE), SDS((T, D_MODEL), F32)],
        name=name, compiler_params=_cp("parallel", "arbitrary"))(pa, at, sg, wa, wb, wc, z, z, z, x, w_out)


def _out_dx_merge_bwd(dxb, w_out, y, z, ws, xs, *, tm, tn, name):
    T = dxb.shape[0]
    gb = GATE_COL0 // tn
    nb = D_MODEL // tn
    nr = T // tm
    widths = [w.shape[0] for w in ws]

    def body(dx_ref, w_ref, y_ref, g_ref, *refs):
        w_refs, x_refs = refs[0:3], refs[3:6]
        dz_ref, dx_refs, dw_refs = refs[6], refs[7:10], refs[10:13]
        dm_ref, acc_refs = refs[13], refs[14:17]
        i, b, j = pl.program_id(0), pl.program_id(1), pl.program_id(2)

        @pl.when((b == 0) & (j == 0))
        def _():
            dm = _dot(dx_ref[...], w_ref[...], NT)
            for jj in range(nb):
                dm_ref[jj] = dm[:, jj * tn:(jj + 1) * tn]

        d = dm_ref[j]
        g = _sigmoid(g_ref[...])
        dy = (d * g).astype(MXU_DTYPE)
        dz_ref[...] = (d * y_ref[...].astype(F32) * g * (1.0 - g)).astype(dz_ref.dtype)
        for branch in range(3):
            @pl.when(b == branch)
            def _():
                p = _dot(dy, w_refs[branch][...], NT)
                q = _dot(x_refs[branch][...], dy, TN)

                @pl.when(j == 0)
                def _():
                    dx_refs[branch][...] = p

                @pl.when(j > 0)
                def _():
                    dx_refs[branch][...] += p

                @pl.when(i == 0)
                def _():
                    acc_refs[branch][j] = q

                @pl.when(i > 0)
                def _():
                    acc_refs[branch][j] += q

        @pl.when((i == nr - 1) & (b == 2) & (j == nb - 1))
        def _():
            for branch in range(3):
                for jj in range(nb):
                    dw_refs[branch][:, jj * tn:(jj + 1) * tn] = acc_refs[branch][jj]

    wspec = lambda k: pl.BlockSpec((k, tn), lambda i, b, j: (0, j))
    rows = lambda k: pl.BlockSpec((tm, k), lambda i, b, j: (i, 0))
    return pl.pallas_call(
        body, grid=(nr, 3, nb),
        in_specs=[rows(D_MODEL), _full((D_MODEL, D_MODEL)),
                  pl.BlockSpec((None, tm, tn), lambda i, b, j: (b, i, j)),
                  pl.BlockSpec((tm, tn), lambda i, b, j: (i, gb + b * nb + j))]
        + [wspec(k) for k in widths] + [rows(k) for k in widths],
        out_specs=[pl.BlockSpec((tm, tn), lambda i, b, j: (i, gb + b * nb + j))]
        + [rows(k) for k in widths] + [_full((k, D_MODEL)) for k in widths],
        out_shape=[SDS((T, IN_COLS), MXU_DTYPE)] + [SDS((T, k), F32) for k in widths]
        + [SDS((k, D_MODEL), F32) for k in widths],
        scratch_shapes=[pltpu.VMEM((nb, tm, tn), F32)] + [pltpu.VMEM((nb, k, tn), F32) for k in widths],
        name=name, compiler_params=_cp("arbitrary", "arbitrary", "arbitrary"))(dxb, w_out, y, z, *ws, *xs)


def _conv3(xe, w, b):
    return (w[0:1] * pltpu.roll(xe, 2, 0) + w[1:2] * pltpu.roll(xe, 1, 0) + w[2:3] * xe)[8:] + b


def _conv_act_fwd(up, cw, cb, *, tr, tc, name):
    T = up.shape[0]
    nc = D_FF // tc
    hb = tr // 8

    def body(ug_ref, ugp_ref, uv_ref, uvp_ref, wg_ref, wv_ref, bg_ref, bv_ref, o_ref):
        i = pl.program_id(1)
        first = i == 0
        cg = _conv3(jnp.concatenate([jnp.where(first, 0.0, ugp_ref[...]), ug_ref[...]], axis=0), wg_ref[...], bg_ref[...])
        cv = _conv3(jnp.concatenate([jnp.where(first, 0.0, uvp_ref[...]), uv_ref[...]], axis=0), wv_ref[...], bv_ref[...])
        o_ref[...] = (cg * _sigmoid(cg) * cv).astype(o_ref.dtype)

    tile = lambda off: pl.BlockSpec((tr, tc), lambda j, i: (i, off + j))
    prev = lambda off: pl.BlockSpec((8, tc), lambda j, i: (jnp.maximum(i * hb - 1, 0), off + j))
    par = lambda rows, off: pl.BlockSpec((rows, tc), lambda j, i: (0, off + j))
    return pl.pallas_call(
        body, grid=(nc, T // tr),
        in_specs=[tile(0), prev(0), tile(nc), prev(nc), par(3, 0), par(3, nc), par(1, 0), par(1, nc)],
        out_specs=pl.BlockSpec((tr, tc), lambda j, i: (i, j)),
        out_shape=SDS((T, D_FF), MXU_DTYPE), name=name,
        compiler_params=_cp("parallel", "parallel"))(up, up, up, up, cw, cw, cb, cb)


def _conv_act_bwd(up, cw, cb, dact, *, tr, tc, name, deps=()):
    T = up.shape[0]
    nc = D_FF // tc
    hb = tr // 8
    nr = T // tr

    def body(ug_ref, ugp_ref, ugn_ref, uv_ref, uvp_ref, uvn_ref, da_ref, dan_ref, wg_ref, wv_ref, bg_ref, bv_ref,
             du_ref, dwg_ref, dwv_ref, dbg_ref, dbv_ref):
        i = pl.program_id(1)
        first, last = i == 0, i == nr - 1
        da = jnp.concatenate([da_ref[...], jnp.where(last, 0.0, dan_ref[...])], axis=0)
        uge = jnp.concatenate([jnp.where(first, 0.0, ugp_ref[...]), ug_ref[...], ugn_ref[...]], axis=0)
        uve = jnp.concatenate([jnp.where(first, 0.0, uvp_ref[...]), uv_ref[...], uvn_ref[...]], axis=0)
        wg, wv = wg_ref[...], wv_ref[...]
        ug1, ug2 = pltpu.roll(uge, 1, 0)[8:], pltpu.roll(uge, 2, 0)[8:]
        uv1, uv2 = pltpu.roll(uve, 1, 0)[8:], pltpu.roll(uve, 2, 0)[8:]
        cg = wg[0:1] * ug2 + wg[1:2] * ug1 + wg[2:3] * uge[8:] + bg_ref[...]
        cv = wv[0:1] * uv2 + wv[1:2] * uv1 + wv[2:3] * uve[8:] + bv_ref[...]
        sg = _sigmoid(cg)
        dcg = da * cv * (sg * (1.0 + cg * (1.0 - sg)))
        dcv = da * (cg * sg)
        nrow = tr + 8

        def back(dc, w):
            return (w[2:3] * dc + w[1:2] * pltpu.roll(dc, nrow - 1, 0) + w[0:1] * pltpu.roll(dc, nrow - 2, 0))[:tr]

        du_ref[0] = back(dcg, wg).astype(du_ref.dtype)
        du_ref[1] = back(dcv, wv).astype(du_ref.dtype)

        def wgrad(dc, u0, u1, u2):
            d = dc[:tr]
            rows = [jnp.sum(d * u2[:tr], axis=0, keepdims=True), jnp.sum(d * u1[:tr], axis=0, keepdims=True),
                    jnp.sum(d * u0[8:8 + tr], axis=0, keepdims=True)]
            return jnp.concatenate(rows, axis=0), jnp.sum(d, axis=0, keepdims=True)

        dwg, dbg = wgrad(dcg, uge, ug1, ug2)
        dwv, dbv = wgrad(dcv, uve, uv1, uv2)

        @pl.when(first)
        def _():
            dwg_ref[...] = jnp.zeros_like(dwg_ref)
            dwv_ref[...] = jnp.zeros_like(dwv_ref)
            dbg_ref[...] = jnp.zeros_like(dbg_ref)
            dbv_ref[...] = jnp.zeros_like(dbv_ref)
        dwg_ref[...] += dwg
        dwv_ref[...] += dwv
        dbg_ref[...] += dbg
        dbv_ref[...] += dbv

    tile = lambda off: pl.BlockSpec((tr, tc), lambda j, i: (i, off + j))
    prev = lambda off: pl.BlockSpec((8, tc), lambda j, i: (jnp.maximum(i * hb - 1, 0), off + j))
    nxt = lambda off: pl.BlockSpec((8, tc), lambda j, i: (jnp.minimum((i + 1) * hb, T // 8 - 1), off + j))
    par = lambda rows, off: pl.BlockSpec((rows, tc), lambda j, i: (0, off + j))
    acc = lambda rows: pl.BlockSpec((rows, tc), lambda j, i: (0, j))
    return pl.pallas_call(
        _after(body, 12, deps), grid=(nc, nr),
        in_specs=[tile(0), prev(0), nxt(0), tile(nc), prev(nc), nxt(nc), tile(0), nxt(0),
                  par(3, 0), par(3, nc), par(1, 0), par(1, nc)] + [ANY] * len(deps),
        out_specs=[pl.BlockSpec((2, tr, tc), lambda j, i: (0, i, j)), acc(3), acc(3), acc(1), acc(1)],
        out_shape=[SDS((2, T, D_FF), MXU_DTYPE), SDS((3, D_FF), F32), SDS((3, D_FF), F32),
                   SDS((1, D_FF), F32), SDS((1, D_FF), F32)],
        name=name, compiler_params=_cp("parallel", "arbitrary"))(
            up, up, up, up, up, up, dact, dact, cw, cw, cb, cb, *deps)


def _row_tile(rows, cap):
    t = min(cap, rows)
    t -= t % 8
    while rows % t:
        t -= 8
    return t


def _adamw(w, g, m, v, *, tr, name, copy_g=False):
    R, C = w.shape
    assert R % tr == 0, (R, tr)

    def body(w_ref, g_ref, m_ref, v_ref, d_ref, nm_ref, nv_ref, *rest):
        gv = g_ref[...]
        mn = ADAM_B1 * m_ref[...] + (1.0 - ADAM_B1) * gv
        vn = ADAM_B2 * v_ref[...] + (1.0 - ADAM_B2) * (gv * gv)
        m_hat = mn / (1.0 - ADAM_B1 ** ADAM_STEP)
        v_hat = vn / (1.0 - ADAM_B2 ** ADAM_STEP)
        d_ref[...] = -ADAM_LR * (m_hat / (jnp.sqrt(v_hat) + ADAM_EPS) + ADAM_WD * w_ref[...])
        nm_ref[...] = mn
        nv_ref[...] = vn
        if copy_g:
            rest[0][...] = gv

    rows = pl.BlockSpec((tr, C), lambda i: (i, 0))
    n_out = 4 if copy_g else 3
    return pl.pallas_call(
        body, grid=(R // tr,), in_specs=[rows] * 4, out_specs=[rows] * n_out,
        out_shape=[SDS((R, C), F32)] * n_out, name=name, compiler_params=_cp("parallel"))(w, g, m, v)


def _sum_slots(r, *, tr, name):
    S, R, C = r.shape
    assert R % tr == 0, (R, tr)

    def body(r_ref, o_ref):
        acc = r_ref[0]
        for s in range(1, S):
            acc = acc + r_ref[s]
        o_ref[...] = acc

    return pl.pallas_call(
        body, grid=(R // tr,), in_specs=[pl.BlockSpec((S, tr, C), lambda i: (0, i, 0))],
        out_specs=pl.BlockSpec((tr, C), lambda i: (i, 0)), out_shape=SDS((R, C), F32),
        name=name, compiler_params=_cp("parallel"))(r)


def _pair_add(g4, h, pos, *, name):
    A, _, r, C = g4.shape
    cs = C if A == N_CHIPS else C // N_CHIPS
    tr = _row_tile(r, 256)
    if A == N_CHIPS:
        g_map, h_map = (lambda t, i, pos: (t, pos[1], i, 0)), (lambda t, i, pos: (t, i, 0))
    else:
        g_map, h_map = (lambda t, i, pos: (0, pos[1], i, t)), (lambda t, i, pos: (0, i, t))

    def body(pos_ref, g_ref, h_ref, o_ref):
        o_ref[...] = (g_ref[...] + h_ref[...]).astype(o_ref.dtype)

    grid_spec = pltpu.PrefetchScalarGridSpec(
        num_scalar_prefetch=1, grid=(N_CHIPS, r // tr),
        in_specs=[pl.BlockSpec((None, None, tr, cs), g_map), pl.BlockSpec((None, tr, cs), h_map)],
        out_specs=pl.BlockSpec((None, tr, cs), lambda t, i, pos: (t, i, 0)))
    return pl.pallas_call(body, grid_spec=grid_spec, out_shape=SDS((N_CHIPS, r, cs), COMM_DTYPE), name=name,
                          compiler_params=_cp("parallel", "parallel"))(pos, g4, h)


def _chip_sum(p, r2, f_into, pos, layer, *, name):
    _, r, cs = p.shape
    tr = _row_tile(r, 256)

    def body(pos_ref, own_ref, r_ref, *rest):
        o_ref = rest[-1]
        o_ref[...] = ((own_ref[...].astype(F32) + r_ref[0].astype(F32)) + r_ref[1].astype(F32)) + r_ref[2].astype(F32)

    in_specs = [pl.BlockSpec((None, tr, cs), lambda i, pos: (pos[0], i, 0)),
                pl.BlockSpec((3, tr, cs), lambda i, pos: (0, i, 0))]
    operands = [pos, p, r2]
    aliases = {}
    if f_into is not None:
        in_specs.append(ANY)
        operands.append(f_into)
        aliases = {3: 0}
    grid_spec = pltpu.PrefetchScalarGridSpec(
        num_scalar_prefetch=1, grid=(r // tr,), in_specs=in_specs,
        out_specs=pl.BlockSpec((None, None, tr, cs), lambda i, pos: (layer, pos[1], i, 0)))
    return pl.pallas_call(body, grid_spec=grid_spec, out_shape=SDS((DEPTH, 2, r, cs), F32), name=name,
                          input_output_aliases=aliases, compiler_params=_cp("parallel"))(*operands)


def _mesh_pos():
    return lax.axis_index("x"), lax.axis_index("y"), lax.axis_index("c")


HBM = pl.BlockSpec(memory_space=pltpu.HBM)
SEM = pl.BlockSpec(memory_space=pltpu.SEMAPHORE)
DATAFLOW = pltpu.SideEffectType.DATAFLOW_SIDE_EFFECTING
CHIP_FLIPS = (2, 1, 3)


def _chip_peers():
    x, y, c = _mesh_pos()
    return 2 * x + y, [(1 - x, y, c), (x, 1 - y, c), (1 - x, 1 - y, c)], (x, y, 1 - c), c


def _split_start(arrays, n_copies, issue, *, name, deps=()):
    k = len(arrays)
    nd = len(deps)

    def body(*refs):
        issue(refs[:k], refs[k + nd], refs[k + nd + 1])
        refs[2 * k + nd + 2][...] = jnp.zeros((8, 128), F32)

    out = pl.pallas_call(
        body, name=name,
        out_shape=(pltpu.SemaphoreType.DMA((n_copies,)), pltpu.SemaphoreType.DMA((n_copies,)),
                   *[pltpu.HBM(a.shape, a.dtype) for a in arrays], SDS((8, 128), F32)),
        in_specs=[HBM] * k + [ANY] * nd, out_specs=(SEM, SEM, *[HBM] * k, pl.BlockSpec(memory_space=pltpu.VMEM)),
        input_output_aliases={i: 2 + i for i in range(k)},
        compiler_params=pltpu.CompilerParams(has_side_effects=DATAFLOW))(
            *[pltpu.with_memory_space_constraint(a, pltpu.HBM) for a in arrays], *deps)
    return (out[0], out[1]), list(out[2:2 + k]), out[2 + k]


def _split_wait(sems, arrays, after, waits, *, name):
    k = len(arrays)
    afters = tuple(after) if isinstance(after, (tuple, list)) else (after,)

    def body(*refs):
        waits(refs[:k], refs[k], refs[k + 1])

    out = pl.pallas_call(
        body, name=name, out_shape=tuple(pltpu.HBM(a.shape, a.dtype) for a in arrays),
        in_specs=[HBM] * k + [SEM, SEM] + [ANY] * len(afters), out_specs=tuple([HBM] * k),
        input_output_aliases={i: i for i in range(k)},
        compiler_params=pltpu.CompilerParams(has_side_effects=DATAFLOW))(*arrays, sems[0], sems[1], *afters)
    return list(out)


def _wait_both(cp):
    cp.wait_send()
    cp.wait_recv()


def _cast_place(shard, pos, dtype, *, name, layer=None, slots=N_CHIPS, which=0):
    R, C = shard.shape[-2:]
    tr = R if R % 8 else _row_tile(R, 256)
    if layer is None:
        in_spec = pl.BlockSpec((tr, C), lambda i, pos: (i, 0))
    else:
        in_spec = pl.BlockSpec((None, tr, C), lambda i, pos: (layer, i, 0))

    def body(pos_ref, x_ref, o_ref):
        o_ref[...] = x_ref[...].astype(o_ref.dtype)

    grid_spec = pltpu.PrefetchScalarGridSpec(
        num_scalar_prefetch=1, grid=(R // tr,), in_specs=[in_spec],
        out_specs=pl.BlockSpec((None, tr, C), lambda i, pos: (pos[which], i, 0)))
    return pl.pallas_call(body, grid_spec=grid_spec, out_shape=SDS((slots, R, C), dtype), name=name,
                          compiler_params=_cp("parallel"))(pos, shard)


def _device_peers():
    x, y, c = _mesh_pos()
    peers = [(x ^ ((f >> 2) & 1), y ^ ((f >> 1) & 1), c ^ (f & 1)) for f in range(1, N_DEV)]
    return 4 * x + 2 * y + c, peers


class _Gather:
    def __init__(self, lands, name, deps=(), all_devices=False):
        n = len(lands)
        self.name = name
        npeer = N_DEV - 1 if all_devices else N_CHIPS - 1

        def copies(refs, ss, rs):
            me, peers = _device_peers() if all_devices else _chip_peers()[:2]
            return [pltpu.make_async_remote_copy(
                src_ref=refs[w].at[me], dst_ref=refs[w].at[me], send_sem=ss.at[npeer * w + p],
                recv_sem=rs.at[npeer * w + p], device_id=peers[p], device_id_type=MESH)
                for w in range(n) for p in range(npeer)]

        def issue(refs, ss, rs):
            for cp in copies(refs, ss, rs):
                cp.start()

        def waits(refs, ss, rs):
            for cp in copies(refs, ss, rs):
                _wait_both(cp)

        self._waits = waits
        self.sems, self.arrays, self.token = _split_start(list(lands), npeer * n, issue, name=name + "_start",
                                                          deps=deps)

    def wait(self, after):
        return _split_wait(self.sems, self.arrays, after, self._waits, name=self.name + "_wait")


def _swap_halves_start(g4s, *, name):
    n = len(g4s)
    lands = [lax.empty((g.shape[0],) + g.shape[2:], g.dtype) for g in g4s]

    def copies(refs, ss, rs):
        _, _, sibling, c = _chip_peers()
        return [pltpu.make_async_remote_copy(
            src_ref=refs[w].at[:, 1 - c], dst_ref=refs[n + w], send_sem=ss.at[w], recv_sem=rs.at[w],
            device_id=sibling, device_id_type=MESH) for w in range(n)]

    def issue(refs, ss, rs):
        for cp in copies(refs, ss, rs):
            cp.start()

    def waits(refs, ss, rs):
        for cp in copies(refs, ss, rs):
            _wait_both(cp)

    sems, arrays, token = _split_start(list(g4s) + lands, n, issue, name=name + "_start")
    return sems, arrays, token, waits


def _scatter_start(parts, *, name, deps=()):
    n = len(parts)
    lands = [lax.empty((3,) + p.shape[1:], p.dtype) for p in parts]

    def copies(refs, ss, rs):
        me, peers, _, _ = _chip_peers()
        return [pltpu.make_async_remote_copy(
            src_ref=refs[w].at[me ^ CHIP_FLIPS[p]], dst_ref=refs[n + w].at[p],
            send_sem=ss.at[3 * w + p], recv_sem=rs.at[3 * w + p], device_id=peers[p], device_id_type=MESH)
            for w in range(n) for p in range(3)]

    def issue(refs, ss, rs):
        for cp in copies(refs, ss, rs):
            cp.start()

    def waits(refs, ss, rs):
        for cp in copies(refs, ss, rs):
            _wait_both(cp)

    sems, arrays, token = _split_start(list(parts) + lands, 3 * n, issue, name=name + "_start", deps=deps)
    return sems, arrays, token, waits


def _pair_share_start(fs, layer, *, name):
    n = len(fs)

    def copies(refs, ss, rs):
        _, _, sibling, c = _chip_peers()
        return [pltpu.make_async_remote_copy(
            src_ref=refs[w].at[layer, c], dst_ref=refs[w].at[layer, c], send_sem=ss.at[w], recv_sem=rs.at[w],
            device_id=sibling, device_id_type=MESH) for w in range(n)]

    def issue(refs, ss, rs):
        for cp in copies(refs, ss, rs):
            cp.start()

    def waits(refs, ss, rs):
        for cp in copies(refs, ss, rs):
            _wait_both(cp)

    sems, arrays, token = _split_start(list(fs), n, issue, name=name + "_start")
    return sems, arrays, token, waits


BIG = ('w_in', 'w_proj_a', 'w_proj_b', 'w_proj_c', 'w_out', 'w_up', 'w_down')
BIG_SHARD_AXIS = {'w_in': 2, 'w_proj_a': 2, 'w_proj_b': 2, 'w_proj_c': 2, 'w_out': 1, 'w_up': 2, 'w_down': 1}
SMALL = ('norm1', 'q_norm', 'k_norm', 'sinks', 'w_pool', 'pool_scale', 'sgu_v_norm', 'w_s', 'b_s', 'norm2',
         'conv_b', 'conv_w')
WEIGHTS = ('norm1', 'w_in', 'q_norm', 'k_norm', 'sinks', 'w_pool', 'pool_scale', 'sgu_v_norm', 'w_s', 'b_s',
           'w_proj_a', 'w_proj_b', 'w_proj_c', 'w_out', 'norm2', 'w_up', 'conv_w', 'conv_b', 'w_down')


def _rope_tables(positions):
    inv_freq = ROPE_THETA ** (-jnp.arange(0, HEAD_DIM, 2, dtype=F32) / HEAD_DIM)
    ang = positions.astype(F32)[:, None] * inv_freq
    cos, sin = jnp.cos(ang), jnp.sin(ang)
    c = jnp.concatenate([cos, cos], axis=1)
    s = jnp.concatenate([-sin, sin], axis=1)
    return jnp.concatenate([c, c], axis=1), jnp.concatenate([s, s], axis=1)


def _block_diag4(w):
    out = jnp.zeros((POOL_WIDTH, POOL_WIDTH), w.dtype)
    for g in range(4):
        out = lax.dynamic_update_slice(out, w[g], (g * HEAD_DIM, g * HEAD_DIM))
    return out


def _local_step(x, target, cos, sin, sp, sched):
    T = x.shape[0]
    tm1 = min(1024, T)
    tm = min(512, T)
    tr = min(512, T)
    trc = min(256, T)
    tkt = min(1024, T)
    seg = _seg_matrix(256, HEAD_DIM)
    saved = []
    xl = x
    for l in range(DEPTH):
        p = f"l{l}_"
        c = dict(
            g1=sp['norm1'][l][None], g2=sp['norm2'][l][None],
            wbd=_block_diag4(sp['w_pool'][l]).astype(MXU_DTYPE), scale=sp['pool_scale'][l][None],
            gq=jnp.tile(sp['q_norm'][l], 4)[None], gk=jnp.tile(sp['k_norm'][l], 2)[None],
            sinks=jnp.broadcast_to(sp['sinks'][l][:, None], (N_Q_HEADS, 128)),
            wtril=jnp.tril(sp['w_s'][l]).astype(MXU_DTYPE),
            bexp=jnp.repeat(sp['b_s'][l].T, HEAD_DIM, axis=1), vn=jnp.tile(sp['sgu_v_norm'][l], 4)[None],
            cb=sp['conv_b'][l][None])
        c['w_in'] = sched.weight('w_in', l, xl)
        z, h1 = _norm_mm(xl, c['g1'], c['w_in'], tm=tm1, tn=1152, name=p + "in_proj",
                         deps=sched.start_tokens() if l == 0 else ())
        pa = _pool_fwd(z, c['wbd'], c['scale'], tr=tr, name=p + "pool")
        q, k, v = _qkv_prep(z, cos, sin, c['gq'], c['gk'], seg, tr=tr, name=p + "qkv_prep")
        at = _attn_fwd(q, k, v, c['sinks'], name=p + "attn")
        sg = _sgu_fwd(z, c['wtril'], c['bexp'], c['vn'], seg, tr=tr, name=p + "sgu")
        for n in ('w_proj_a', 'w_proj_b', 'w_proj_c', 'w_out'):
            c[n] = sched.weight(n, l, (pa, at, sg))
        merged, y3, x1 = _merge_fwd(pa, at, sg, c['w_proj_a'], c['w_proj_b'], c['w_proj_c'], z, xl, c['w_out'],
                                    tm=tm, tn=512, name=p + "merge_out_proj")
        for n in ('w_up', 'conv_w', 'w_down'):
            c[n] = sched.weight(n, l, x1)
        up, h2 = _norm_mm(x1, c['g2'], c['w_up'], tm=tm1, tn=1408, name=p + "up_proj")
        act = _conv_act_fwd(up, c['conv_w'], c['cb'], tr=trc, tc=1408, name=p + "conv_act")
        saved.append(dict(c, x=xl, h1=h1, z=z, pa=pa, q=q, k=k, v=v, at=at, sg=sg, merged=merged, y3=y3,
                          x1=x1, h2=h2, up=up, act=act))
        if l < DEPTH - 1:
            xl = _mm(act, c['w_down'], mode='nn', add=x1, tm=tm, tn=D_MODEL, tk=D_FF, name=p + "down_proj")
        else:
            loss_row, dx, dxb = _down_proj_loss(act, c['w_down'], x1, target, tm=tm, name=p + "down_proj_loss")

    gs = {n: [None] * DEPTH for n in SMALL}
    for l in reversed(range(DEPTH)):
        p = f"l{l}_b_"
        s = saved[l]
        gb = {}
        dact = _mm(dxb, s['w_down'], mode='nt', tm=tm1, tn=1408, tk=D_MODEL, name=p + "down_dx")
        gb['w_down'] = _mm(s['act'], dxb, mode='tn', tm=1408, tn=D_MODEL, tk=tkt, name=p + "down_dw")
        toks = sched.slot(l, 'down', gb['w_down'])
        dup, dwg, dwv, dbg, dbv = _conv_act_bwd(s['up'], s['conv_w'], s['cb'], dact, tr=min(1024, T), tc=256,
                                                name=p + "conv_act", deps=toks)
        gs['conv_w'][l] = jnp.concatenate([dwg, dwv], axis=1)
        gs['conv_b'][l] = jnp.concatenate([dbg, dbv], axis=1)[0]
        toks = sched.slot(l, 'conv', dup)
        for half in range(2):
            gb['w_up'] = _mm(s['h2'], dup, mode='tn', b_lead=half, tm=D_MODEL, tn=1408, tk=tkt,
                             out_into=gb.get('w_up'), out_joff=2 * half, out_n=2 * D_FF, name=p + f"up_dw{half}",
                             deps=toks if half == 0 else ())
        toks = sched.slot(l, 'ffn', gb['w_up'], gb)
        dx1, dx1b, dg2 = _mm_nt_sharded_rms(dup, s['w_up'], s['x1'], s['g2'], dx, tm=min(256, T),
                                            name=p + "up_dx_rms2", deps=toks)
        gs['norm2'][l] = dg2[0]
        gb['w_out'] = _mm(s['merged'], dx1b, mode='tn', tm=D_MODEL, tn=D_MODEL, tk=tkt, name=p + "out_dw")
        (dz, dpa, dat, dsg, gb['w_proj_a'], gb['w_proj_b'], gb['w_proj_c']) = _out_dx_merge_bwd(
            dx1b, s['w_out'], s['y3'], s['z'], [s['w_proj_a'], s['w_proj_b'], s['w_proj_c']],
            [s['pa'], s['at'], s['sg']], tm=tm, tn=512, name=p + "out_dx_merge")
        toks = sched.slot(l, 'mid', dz)
        dq, dkc, dkp, dvc, dvp, dsk = _attn_bwd(s['q'], s['k'], s['v'], s['sinks'], dat, name=p + "attn", deps=toks)
        gs['sinks'][l] = dsk[:, 0]
        toks = sched.slot(l, 'attn', dq)
        dz, dgq, dgk, dwbd, dsc = _mixer_ab_bwd(s['z'], cos, sin, s['gq'], s['gk'], seg, dq, dkc, dkp, dvc, dvp,
                                                dpa, s['wbd'], s['scale'], dz, tr=tr, name=p + "qkv_pool", deps=toks)
        gs['q_norm'][l] = dgq[0, :HEAD_DIM]
        gs['k_norm'][l] = dgk[0, :HEAD_DIM]
        gs['w_pool'][l] = jnp.stack([dwbd[g * HEAD_DIM:(g + 1) * HEAD_DIM, g * HEAD_DIM:(g + 1) * HEAD_DIM]
                                     for g in range(4)])
        gs['pool_scale'][l] = dsc[0]
        dz, dws, dbrows, dvn = _sgu_bwd(s['z'], s['wtril'], s['bexp'], s['vn'], seg, dsg, dz, tr=tr, name=p + "sgu")
        gs['w_s'][l] = dws
        gs['b_s'][l] = dbrows[:, ::HEAD_DIM].T
        gs['sgu_v_norm'][l] = dvn[0, :HEAD_DIM]
        gb['w_in'] = _mm(s['h1'], dz, mode='tn', tm=D_MODEL, tn=1152, tk=tkt, name=p + "in_dw")
        toks = sched.slot(l, 'mix', gb['w_in'], gb)
        dx, dxb, dg1 = _mm_nt_sharded_rms(dz, s['w_in'], s['x'], s['g1'], dx1, tm=min(256, T),
                                          name=p + "in_dx_rms1", deps=toks)
        gs['norm1'][l] = dg1[0]
    gs = {n: jnp.stack(v) for n, v in gs.items()}
    return loss_row, dx, gs


GROUP_F = ('w_down', 'w_up')
GROUP_M = ('w_out', 'w_proj_a', 'w_proj_b', 'w_proj_c', 'w_in')
ROW_SHARDED = ('w_out', 'w_down')

REDUCE_PLAN = {
    (1, 'ffn'): (('S1', 'F', 1),),
    (1, 'mid'): (('W1', 'F', 1),),
    (1, 'mix'): (('S1', 'M', 1),),
    (0, 'down'): (('W1', 'M', 1),),
    (0, 'conv'): (('W2', 'F', 1),),
    (0, 'ffn'): (('S1', 'F', 0), ('W3', 'F', 1)),
    (0, 'mid'): (('W1', 'F', 0),),
    (0, 'attn'): (('W2', 'M', 1),),
    (0, 'mix'): (('S1', 'M', 0), ('W3', 'M', 1)),
}
REDUCE_TAIL_A = (('W1', 'M', 0), ('W2', 'F', 0))
REDUCE_TAIL_B = (('W3', 'F', 0),)
REDUCE_TAIL_C = (('W2', 'M', 0), ('W3', 'M', 0))


class _Comm:
    def __init__(self, w, pos):
        self.pos = pos
        groups = {'a': [('w_in', 0)],
                  'b': [(n, 0) for n in ('w_proj_a', 'w_proj_b', 'w_proj_c', 'w_out')],
                  'c': [(n, 0) for n in ('w_up', 'conv_w', 'w_down')],
                  'd': [(n, 1) for n in BIG] + [('conv_w', 1)]}
        self.gathers, self.group_of, self.weights = {}, {}, {}
        self.tokens = []
        for g, ks in groups.items():
            lands = [_cast_place(w[n], pos, F32 if n == 'conv_w' else MXU_DTYPE, layer=l, name=f"gw_place_{n}{l}")
                     for n, l in ks]
            self.gathers[g] = (_Gather(lands, "gw_" + g, deps=self.tokens[-1:]), ks)
            self.tokens.append(self.gathers[g][0].token)
            self.group_of.update({k: g for k in ks})
        self.red = {}
        self.final = {}

    def start_tokens(self):
        return self.tokens[-1:]

    def weight(self, name, layer, after):
        if (name, layer) not in self.weights:
            gather, ks = self.gathers[self.group_of[(name, layer)]]
            for (n, l), full in zip(ks, gather.wait(after)):
                if n == 'conv_w' or n.startswith('w_proj'):
                    full = full.transpose(1, 0, 2).reshape(full.shape[1], -1)
                elif n in ROW_SHARDED:
                    full = full.reshape(-1, full.shape[2])
                self.weights[(n, l)] = full
        return self.weights[(name, layer)]

    def slot(self, layer, slot, after, grads=None):
        tokens = []
        for step, grp, lyr in REDUCE_PLAN.get((layer, slot), ()):
            tok = self._step(step, grp, lyr, after, grads)
            if tok is not None:
                tokens.append(tok)
        return tokens

    def tail(self, steps, after, deps=()):
        toks = (self._step(step, grp, lyr, after, None, deps) for step, grp, lyr in steps)
        return [t for t in toks if t is not None]

    def shards(self):
        return {n: f.reshape(DEPTH, 2 * f.shape[2], f.shape[3]) for n, f in self.final.items()}

    def _step(self, step, grp, layer, after, grads, deps=()):
        names = GROUP_F if grp == 'F' else GROUP_M
        tag = f"{grp.lower()}{layer}"
        st = self.red.setdefault((grp, layer), {})
        n = len(names)
        if step == 'S1':
            g4s = []
            for nm in names:
                g = grads[nm]
                R, C = g.shape
                g4s.append(g.reshape(N_CHIPS, 2, R // (2 * N_CHIPS), C) if nm in ROW_SHARDED
                           else g.reshape(1, 2, R // 2, C))
            st['s1'] = _swap_halves_start(g4s, name="rs1_" + tag)
            return st['s1'][2]
        if step == 'W1':
            sems, arrays, _, waits = st.pop('s1')
            arrays = _split_wait(sems, arrays, after, waits, name=f"rs1_{tag}_wait")
            parts = [_pair_add(arrays[i], arrays[n + i], self.pos, name=f"pair_add_{tag}_{names[i]}")
                     for i in range(n)]
            st['s2'] = _scatter_start(parts, name="rs2_" + tag, deps=deps)
            return st['s2'][2]
        if step == 'W2':
            sems, arrays, _, waits = st.pop('s2')
            arrays = _split_wait(sems, arrays, after, waits, name=f"rs2_{tag}_wait")
            fs = [_chip_sum(arrays[i], arrays[n + i], self.final.get(names[i]), self.pos, layer,
                            name=f"chip_sum_{tag}_{names[i]}") for i in range(n)]
            st['s3'] = _pair_share_start(fs, layer, name="rs3_" + tag)
            return st['s3'][2]
        sems, arrays, _, waits = st.pop('s3')
        self.final.update(zip(names, _split_wait(sems, arrays, after, waits, name=f"rs3_{tag}_wait")))
        return None


def _pack(arrays):
    rows = []
    for a in arrays:
        nel = int(np.prod(a.shape))
        if nel % 1024 == 0:
            rows.append(a.astype(F32).reshape(nel // 128, 128))
        else:
            f = a.reshape(-1).astype(F32)
            rows.append(jnp.pad(f, (0, (-nel) % 1024)).reshape(-1, 128))
    return jnp.concatenate(rows, axis=0)


def _unpack(pack, shapes):
    out, row = [], 0
    for shp in shapes:
        nel = int(np.prod(shp))
        nrow = 8 * -(-nel // 1024)
        part = pack[row:row + nrow]
        out.append(part.reshape(shp) if nel % 1024 == 0 else part.reshape(-1)[:nel].reshape(shp))
        row += nrow
    return out


def kernel(x, positions, norm1, w_in, q_norm, k_norm, sinks, w_pool, pool_scale, sgu_v_norm, w_s, b_s, w_proj_a, w_proj_b, w_proj_c, w_out, norm2, w_up, conv_w, conv_b, w_down, loss_target, m_norm1, m_w_in, m_q_norm, m_k_norm, m_sinks, m_w_pool, m_pool_scale, m_sgu_v_norm, m_w_s, m_b_s, m_w_proj_a, m_w_proj_b, m_w_proj_c, m_w_out, m_norm2, m_w_up, m_conv_w, m_conv_b, m_w_down, v_norm1, v_w_in, v_q_norm, v_k_norm, v_sinks, v_w_pool, v_pool_scale, v_sgu_v_norm, v_w_s, v_b_s, v_w_proj_a, v_w_proj_b, v_w_proj_c, v_w_out, v_norm2, v_w_up, v_conv_w, v_conv_b, v_w_down):
    w = dict(norm1=norm1, w_in=w_in, q_norm=q_norm, k_norm=k_norm, sinks=sinks, w_pool=w_pool, pool_scale=pool_scale,
             sgu_v_norm=sgu_v_norm, w_s=w_s, b_s=b_s, w_proj_a=w_proj_a, w_proj_b=w_proj_b, w_proj_c=w_proj_c,
             w_out=w_out, norm2=norm2, w_up=w_up, conv_w=conv_w, conv_b=conv_b, w_down=w_down)
    m = dict(norm1=m_norm1, w_in=m_w_in, q_norm=m_q_norm, k_norm=m_k_norm, sinks=m_sinks, w_pool=m_w_pool,
             pool_scale=m_pool_scale, sgu_v_norm=m_sgu_v_norm, w_s=m_w_s, b_s=m_b_s, w_proj_a=m_w_proj_a,
             w_proj_b=m_w_proj_b, w_proj_c=m_w_proj_c, w_out=m_w_out, norm2=m_norm2, w_up=m_w_up, conv_w=m_conv_w,
             conv_b=m_conv_b, w_down=m_w_down)
    v = dict(norm1=v_norm1, w_in=v_w_in, q_norm=v_q_norm, k_norm=v_k_norm, sinks=v_sinks, w_pool=v_w_pool,
             pool_scale=v_pool_scale, sgu_v_norm=v_sgu_v_norm, w_s=v_w_s, b_s=v_b_s, w_proj_a=v_w_proj_a,
             w_proj_b=v_w_proj_b, w_proj_c=v_w_proj_c, w_out=v_w_out, norm2=v_norm2, w_up=v_w_up, conv_w=v_conv_w,
             conv_b=v_conv_b, w_down=v_w_down)
    chip = 2 * lax.axis_index("x") + lax.axis_index("y")
    core = lax.axis_index("c")

    pos = jnp.stack([chip, core, 2 * chip + core]).astype(jnp.int32)
    comm = _Comm(w, pos)

    cos, sin = _rope_tables(positions[0])
    sp = {n: w[n] for n in SMALL if n != 'conv_w'}
    loss_row, dx, gs = _local_step(x[0], loss_target[0], cos, sin, sp, comm)

    delta, new_m, new_v, grad_out = {}, {}, {}, {}

    def adamw_big(names, grads):
        for n in names:
            shp = w[n].shape
            two_d = lambda a: a.reshape(shp[0] * shp[1], shp[2])
            d, nm, nv, g = _adamw(two_d(w[n]), two_d(grads[n]), two_d(m[n]), two_d(v[n]),
                                  tr=_row_tile(shp[0] * shp[1], 256), name=f"adamw_{n}", copy_g=True)
            delta[n], new_m[n], new_v[n], grad_out[n] = d.reshape(shp), nm.reshape(shp), nv.reshape(shp), g.reshape(shp)

    small_shapes = [gs[n].shape for n in SMALL] + [(1,)]
    small_pack = _pack([gs[n] for n in SMALL] + [loss_row[0, :1]])
    small = _Gather([_cast_place(small_pack, pos, F32, slots=N_DEV, which=2, name="small_place")], "small_gather",
                    all_devices=True)
    toks = comm.tail(REDUCE_TAIL_A[:1], (dx, small.token))
    comm.tail(REDUCE_TAIL_A[1:], (dx, *toks))
    comm.tail(REDUCE_TAIL_B, dx)
    adamw_big(GROUP_F, comm.shards())
    red = _sum_slots(small.wait(new_v[GROUP_F[-1]])[0], tr=small_pack.shape[0], name="small_sum")
    *small_grads, loss = _unpack(red, small_shapes)
    g_small = dict(zip(SMALL, small_grads))
    comm.tail(REDUCE_TAIL_C, red)
    grads = comm.shards()
    grads.update(g_small)
    shard_cols = conv_w.shape[2]
    grads['conv_w'] = lax.dynamic_slice_in_dim(g_small['conv_w'], chip * shard_cols, shard_cols, axis=2)

    adamw_big(GROUP_M, grads)
    shapes = [w[n].shape for n in SMALL]
    packs = [_pack([src[n] for n in SMALL]) for src in (w, grads, m, v)]
    d, nm, nv = _adamw(*packs, tr=packs[0].shape[0], name="adamw_small")
    for dst, src in ((delta, d), (new_m, nm), (new_v, nv)):
        dst.update(zip(SMALL, _unpack(src, shapes)))

    grads.update(grad_out)
    return (loss[0], dx[None], *[grads[n] for n in WEIGHTS], *[delta[n] for n in WEIGHTS],
            *[new_m[n] for n in WEIGHTS], *[new_v[n] for n in WEIGHTS])
```

```python
import functools
import math

import numpy as np
import jax
import jax.numpy as jnp
from jax import lax
from jax.experimental import pallas as pl
from jax.experimental.pallas import tpu as pltpu

F32 = jnp.float32
MXU_DTYPE = jnp.bfloat16
COMM_DTYPE = jnp.bfloat16

D_MODEL = 1024
DEPTH = 2
HEAD_DIM = 64
POOL_WINDOWS = (2, 4, 8, 16)
POOL_WIDTH = 256
N_Q_HEADS = 8
ATTN_BLOCK = 128
ATTN_WIDTH = 512
KV_WIDTH = 128
CHUNK = 128
SGU_WIDTH = 256
IN_COLS = 4608
GATE_COL0 = 1536
D_FF = 2816
ROPE_THETA = 10000.0
EPS = 1e-6
ADAM_LR, ADAM_B1, ADAM_B2, ADAM_EPS, ADAM_WD, ADAM_STEP = 0.001, 0.9, 0.999, 1e-08, 0.01, 10

N_CHIPS = 4
N_DEV = 8
VMEM_LIMIT_BYTES = 56 * 1024 * 1024
NEG_BIG = -1e30
MESH = pl.DeviceIdType.MESH
ANY = pl.BlockSpec(memory_space=pl.ANY)

SDS = jax.ShapeDtypeStruct


def _cp(*sem):
    return pltpu.CompilerParams(dimension_semantics=sem, vmem_limit_bytes=VMEM_LIMIT_BYTES)


def _dot(a, b, dims=((1,), (0,))):
    return lax.dot_general(a.astype(MXU_DTYPE), b.astype(MXU_DTYPE), (dims, ((), ())),
                           preferred_element_type=F32)


NT = ((1,), (1,))
TN = ((0,), (0,))


def _split_dot(x, m):
    hi = x.astype(MXU_DTYPE)
    lo = (x - hi.astype(F32)).astype(MXU_DTYPE)
    return _dot(hi, m) + _dot(lo, m)


def _seg_matrix(width, seg):
    idx = np.arange(width) // seg
    return jnp.asarray((idx[:, None] == idx[None, :]).astype(np.float32), dtype=MXU_DTYPE)


def _lane(shape):
    return lax.broadcasted_iota(jnp.int32, shape, len(shape) - 1)


def _row(shape):
    return lax.broadcasted_iota(jnp.int32, shape, 0)


def _full(shape):
    nd = len(shape)
    return pl.BlockSpec(shape, lambda *_: (0,) * nd)


def _gelu(x):
    k = math.sqrt(2.0 / math.pi)
    th = jnp.tanh(k * (x + 0.044715 * (x * x * x)))
    return 0.5 * x * (1.0 + th)


def _gelu_and_grad(x):
    k = math.sqrt(2.0 / math.pi)
    x2 = x * x
    th = jnp.tanh(k * (x + 0.044715 * (x2 * x)))
    g = 0.5 * x * (1.0 + th)
    dg = 0.5 * (1.0 + th) + 0.5 * x * (1.0 - th * th) * (k * (1.0 + 3.0 * 0.044715 * x2))
    return g, dg


def _sigmoid(x):
    return 0.5 * jnp.tanh(0.5 * x) + 0.5


def _swap_halves(x):
    w = x.shape[-1]
    first = (_lane(x.shape) % HEAD_DIM) < (HEAD_DIM // 2)
    return jnp.where(first, pltpu.roll(x, w - HEAD_DIM // 2, 1), pltpu.roll(x, HEAD_DIM // 2, 1))


def _tile_lanes(x, reps):
    return x if reps == 1 else jnp.concatenate([x] * reps, axis=1)


def _fold_lanes(x, period):
    w = x.shape[-1]
    while w > period:
        w //= 2
        x = x + pltpu.roll(x, w, 1)
    return x


def _mm(a, b, *, mode, tm, tn, tk, out_dtype=F32, add=None, name,
        a_lead=None, b_lead=None, b_sharded=False, out_into=None,
        b_koff=0, out_joff=0, out_n=None, deps=()):
    ash = a.shape[1:] if a_lead is not None else a.shape
    bsh = b.shape[1:] if b_lead is not None else b.shape
    if b_sharded:
        bsh = (b.shape[1], N_CHIPS * b.shape[2])
    if mode == 'nn':
        (M, K), (K2, N) = ash, bsh
    elif mode == 'nt':
        (M, K), (N, K2) = ash, bsh
    else:
        (K, M), (K2, N) = ash, bsh
    assert K == K2 or (mode == 'nt' and K2 > K), (ash, bsh, mode)
    assert M % tm == 0 and N % tn == 0 and K % tk == 0, (M, N, K, tm, tn, tk)
    nk = K // tk
    dims = {'nn': ((1,), (0,)), 'nt': NT, 'tn': TN}[mode]

    def lead(spec_shape, imap, lead_idx):
        if lead_idx is None:
            return pl.BlockSpec(spec_shape, imap)
        return pl.BlockSpec((None,) + spec_shape, lambda i, j, k: (lead_idx,) + imap(i, j, k))

    if mode == 'tn':
        a_spec = lead((tk, tm), lambda i, j, k: (k, i), a_lead)
    else:
        a_spec = lead((tm, tk), lambda i, j, k: (i, k), a_lead)
    if b_sharded:
        per = b.shape[2] // (tk if mode == 'nt' else tn)
        assert per * (tk if mode == 'nt' else tn) == b.shape[2] and mode != 'tn'
        if mode == 'nt':
            b_spec = pl.BlockSpec((None, tn, tk), lambda i, j, k: ((k + b_koff) // per, j, (k + b_koff) % per))
        else:
            b_spec = pl.BlockSpec((None, tk, tn), lambda i, j, k: (j // per, k, j % per))
    elif mode == 'nt':
        b_spec = lead((tn, tk), lambda i, j, k: (j, k + b_koff), b_lead)
    else:
        b_spec = lead((tk, tn), lambda i, j, k: (k, j), b_lead)
    o_spec = pl.BlockSpec((tm, tn), lambda i, j, k: (i, j + out_joff))
    n_out = N if out_n is None else out_n
    in_specs = [a_spec, b_spec]
    operands = [a, b]
    if add is not None:
        in_specs.append(pl.BlockSpec((tm, tn), lambda i, j, k: (i, j)))
        operands.append(add)
    aliases = {}
    if out_into is not None:
        in_specs.append(ANY)
        operands.append(out_into)
        aliases = {len(operands) - 1: 0}
    in_specs += [ANY] * len(deps)
    operands += list(deps)
    has_add = add is not None
    acc_in_out = nk > 1 and out_dtype == F32

    def body(*refs):
        a_ref, b_ref = refs[0], refs[1]
        pos = 2
        add_ref = None
        if has_add:
            add_ref = refs[pos]
            pos += 1
        if out_into is not None:
            pos += 1
        pos += len(deps)
        o_ref = refs[pos]
        acc_ref = refs[pos + 1] if (nk > 1 and not acc_in_out) else None
        p = _dot(a_ref[...], b_ref[...], dims)
        if nk == 1:
            if has_add:
                p = p + add_ref[...]
            o_ref[...] = p.astype(o_ref.dtype)
            return
        k = pl.program_id(2)
        tgt = o_ref if acc_in_out else acc_ref

        @pl.when(k == 0)
        def _():
            tgt[...] = p + add_ref[...] if has_add else p

        @pl.when(k > 0)
        def _():
            tgt[...] += p

        if not acc_in_out:
            @pl.when(k == nk - 1)
            def _():
                o_ref[...] = acc_ref[...].astype(o_ref.dtype)

    out_shape = SDS((M, n_out), out_dtype)
    scratch = [pltpu.VMEM((tm, tn), F32)] if (nk > 1 and not acc_in_out) else []
    return pl.pallas_call(
        body, grid=(M // tm, N // tn, nk), in_specs=in_specs, out_specs=o_spec, out_shape=out_shape,
        scratch_shapes=scratch, input_output_aliases=aliases, name=name,
        compiler_params=_cp("parallel", "parallel", "arbitrary"))(*operands)


def _rms_bwd_rows(xv, g, dh, dres):
    r = lax.rsqrt(jnp.mean(xv * xv, axis=-1, keepdims=True) + EPS)
    xh = xv * r
    gy = dh * g
    dx = r * (gy - xh * jnp.mean(xh * gy, axis=-1, keepdims=True)) + dres
    return dx, jnp.sum(dh * xh, axis=0, keepdims=True)


def _mm_nt_sharded_rms(a, b, x, g, dres, *, tm, name, deps=()):
    a3 = a if a.ndim == 3 else a[None]
    A, M, ka = a3.shape
    S, N, ns = b.shape
    per = S // A
    assert ka == per * ns and M % tm == 0 and N == x.shape[1], (a3.shape, b.shape, x.shape)

    def body(a_ref, b_ref, x_ref, g_ref, dres_ref, dx_ref, dxb_ref, dg_ref):
        acc = None
        for s in range(S):
            lo = (s % per) * ns
            p = _dot(a_ref[s // per, :, lo:lo + ns], b_ref[s], NT)
            acc = p if acc is None else acc + p
        dx, dg = _rms_bwd_rows(x_ref[...], g_ref[...], acc, dres_ref[...])
        dx_ref[...] = dx
        dxb_ref[...] = dx.astype(dxb_ref.dtype)

        @pl.when(pl.program_id(0) == 0)
        def _():
            dg_ref[...] = jnp.zeros_like(dg_ref)
        dg_ref[...] += dg

    rows = pl.BlockSpec((tm, N), lambda i: (i, 0))
    return pl.pallas_call(
        _after(body, 5, deps), grid=(M // tm,),
        in_specs=[pl.BlockSpec((A, tm, ka), lambda i: (0, i, 0)),
                  pl.BlockSpec((S, N, ns), lambda i: (0, 0, 0), pipeline_mode=pl.Buffered(1)),
                  rows, _full((1, N)), rows] + [ANY] * len(deps),
        out_specs=[rows, rows, _full((1, N))],
        out_shape=[SDS((M, N), F32), SDS((M, N), MXU_DTYPE), SDS((1, N), F32)], name=name,
        compiler_params=_cp("arbitrary"))(a3, b, x, g, dres, *deps)


def _norm_mm(x, g, b, *, tm, tn, name, deps=()):
    M, K = x.shape
    S, K2, ns = b.shape
    per = ns // tn
    assert K == K2 and per * tn == ns and M % tm == 0, (x.shape, b.shape)

    def body(x_ref, g_ref, b_ref, o_ref, h_ref):
        @pl.when(pl.program_id(1) == 0)
        def _():
            xv = x_ref[...]
            r = lax.rsqrt(jnp.mean(xv * xv, axis=-1, keepdims=True) + EPS)
            h_ref[...] = (xv * r * g_ref[...]).astype(h_ref.dtype)
        o_ref[...] = _dot(h_ref[...], b_ref[...])

    return pl.pallas_call(
        _after(body, 3, deps), grid=(M // tm, S * per),
        in_specs=[pl.BlockSpec((tm, K), lambda i, j: (i, 0)), _full((1, K)),
                  pl.BlockSpec((None, K, tn), lambda i, j: (j // per, 0, j % per))] + [ANY] * len(deps),
        out_specs=[pl.BlockSpec((tm, tn), lambda i, j: (i, j)), pl.BlockSpec((tm, K), lambda i, j: (i, 0))],
        out_shape=[SDS((M, S * ns), F32), SDS((M, K), MXU_DTYPE)], name=name,
        compiler_params=_cp("parallel", "arbitrary"))(x, g, b, *deps)


def _after(body, n_in, deps):
    nd = len(deps)
    if nd == 0:
        return body
    return lambda *refs: body(*refs[:n_in], *refs[n_in + nd:])


def _down_proj_loss(act, w, x1, target, *, tm, name):
    T, K = act.shape
    D = w.shape[1]

    def body(a_ref, w_ref, x_ref, t_ref, loss_ref, dy_ref, dyb_ref):
        i = pl.program_id(0)
        d = (x_ref[...] + _dot(a_ref[...], w_ref[...])) - t_ref[...]
        dy = d * (1.0 / D)
        dy_ref[...] = dy
        dyb_ref[...] = dy.astype(dyb_ref.dtype)
        part = jnp.sum(jnp.sum(d * d, axis=1, keepdims=True), axis=0, keepdims=True) * (0.5 / D)

        @pl.when(i == 0)
        def _():
            loss_ref[...] = jnp.zeros_like(loss_ref)
        loss_ref[...] += jnp.broadcast_to(part, loss_ref.shape)

    rows = pl.BlockSpec((tm, D), lambda i: (i, 0))
    return pl.pallas_call(
        body, grid=(T // tm,), in_specs=[pl.BlockSpec((tm, K), lambda i: (i, 0)), _full((K, D)), rows, rows],
        out_specs=[_full((1, 128)), rows, rows],
        out_shape=[SDS((1, 128), F32), SDS((T, D), F32), SDS((T, D), MXU_DTYPE)],
        name=name, compiler_params=_cp("arbitrary"))(act, w, x1, target)


def _pool_lane_consts(shape):
    lane = _lane(shape)
    grp = lane // (POOL_WIDTH // 4)
    win = jnp.where(grp == 0, 2, jnp.where(grp == 1, 4, jnp.where(grp == 2, 8, 16)))
    return grp, win


def _pool_select(grp, s2, s4, s8, s16):
    return jnp.where(grp == 0, s2, jnp.where(grp == 1, s4, jnp.where(grp == 2, s8, s16)))


def _pool_diff(xe, row0, tr):
    s2 = xe + pltpu.roll(xe, 1, 0)
    s4 = s2 + pltpu.roll(s2, 2, 0)
    s8 = s4 + pltpu.roll(s4, 4, 0)
    s16 = s8 + pltpu.roll(s8, 8, 0)
    shape = (tr, POOL_WIDTH)
    grp, win = _pool_lane_consts(shape)
    sums = _pool_select(grp, s2[16:], s4[16:], s8[16:], s16[16:])
    t = row0 + _row(shape)
    cnt = jnp.minimum(t + 1, win).astype(F32)
    return sums / cnt - xe[16:]


def _pool_fwd(z, wbd, scale, *, tr, name):
    T = z.shape[0]
    hb = tr // 16

    def body(x_ref, xp_ref, w_ref, s_ref, o_ref):
        i = pl.program_id(0)
        halo = jnp.where(i == 0, 0.0, xp_ref[...])
        diff = _pool_diff(jnp.concatenate([halo, x_ref[...]], axis=0), i * tr, tr)
        o_ref[...] = (_dot(diff, w_ref[...]) * s_ref[...]).astype(o_ref.dtype)

    return pl.pallas_call(
        body, grid=(T // tr,),
        in_specs=[pl.BlockSpec((tr, POOL_WIDTH), lambda i: (i, 0)),
                  pl.BlockSpec((16, POOL_WIDTH), lambda i: (jnp.maximum(i * hb - 1, 0), 0)),
                  _full((POOL_WIDTH, POOL_WIDTH)), _full((1, POOL_WIDTH))],
        out_specs=pl.BlockSpec((tr, POOL_WIDTH), lambda i: (i, 0)),
        out_shape=SDS((T, POOL_WIDTH), MXU_DTYPE), name=name, compiler_params=_cp("parallel"))(z, z, wbd, scale)


def _pool_bwd_tile(i, n, tr, x, xprev, dpa, dpa_next, wbd, scale):
    halo = jnp.where(i == 0, 0.0, xprev)
    diff = _pool_diff(jnp.concatenate([halo, x], axis=0), i * tr, tr)
    mixed = _dot(diff, wbd)
    dscale = jnp.sum(dpa * mixed, axis=0, keepdims=True)
    dnext = jnp.where(i == n - 1, 0.0, dpa_next)
    dmix_e = jnp.concatenate([dpa, dnext], axis=0) * scale
    ddiff_e = _dot(dmix_e, wbd, NT)
    dwbd = _dot(diff, dmix_e[:tr], TN)
    shape = (tr + 16, POOL_WIDTH)
    grp, win = _pool_lane_consts(shape)
    t = i * tr + _row(shape)
    e = ddiff_e / jnp.minimum(t + 1, win).astype(F32)
    nrow = tr + 16
    a2 = e + pltpu.roll(e, nrow - 1, 0)
    a4 = a2 + pltpu.roll(a2, nrow - 2, 0)
    a8 = a4 + pltpu.roll(a4, nrow - 4, 0)
    a16 = a8 + pltpu.roll(a8, nrow - 8, 0)
    dx = _pool_select(grp, a2, a4, a8, a16)[:tr] - ddiff_e[:tr]
    return dx, dwbd, dscale


def _norm_rope(x, g, cos, sin_signed, seg):
    reps = x.shape[1] // 128
    ms = _split_dot(x * x, seg) * (1.0 / HEAD_DIM)
    r = lax.rsqrt(ms + EPS)
    xn = x * r * g
    c, s = _tile_lanes(cos, reps), _tile_lanes(sin_signed, reps)
    return xn * c + _swap_halves(xn) * s


def _norm_rope_bwd(x, g, cos, sin_signed, seg, dout):
    reps = x.shape[1] // 128
    c, s = _tile_lanes(cos, reps), _tile_lanes(sin_signed, reps)
    dxn = dout * c + _swap_halves(dout * s)
    ms = _split_dot(x * x, seg) * (1.0 / HEAD_DIM)
    r = lax.rsqrt(ms + EPS)
    xh = x * r
    gy = dxn * g
    dx = r * (gy - xh * (_split_dot(xh * gy, seg) * (1.0 / HEAD_DIM)))
    dg = jnp.sum(dxn * xh, axis=0, keepdims=True)
    return dx, dg


def _dup_heads(k):
    first = _lane(k.shape) < HEAD_DIM
    kr = pltpu.roll(k, HEAD_DIM, 1)
    return jnp.concatenate([jnp.where(first, k, kr), jnp.where(first, kr, k)], axis=1)


def _qkv_prep(z, cos, sin_signed, gq, gk, seg, *, tr, name):
    T = z.shape[0]

    def body(qa_ref, qb_ref, kv_ref, c_ref, s_ref, gq_ref, gk_ref, seg_ref, q_ref, k_ref, v_ref):
        c, s, seg_m = c_ref[...], s_ref[...], seg_ref[...]
        scale = HEAD_DIM ** -0.5
        qa = _norm_rope(qa_ref[...], gq_ref[...], c, s, seg_m) * scale
        qb = _norm_rope(qb_ref[...], gq_ref[...], c, s, seg_m) * scale
        q_ref[...] = jnp.concatenate([qa, qb], axis=1).astype(q_ref.dtype)
        kv = kv_ref[...]
        k = _norm_rope(kv[:, :KV_WIDTH], gk_ref[...], c, s, seg_m[:128, :128])
        k_ref[...] = _dup_heads(k).astype(k_ref.dtype)
        v_ref[...] = _dup_heads(kv[:, KV_WIDTH:]).astype(v_ref.dtype)

    col = lambda j: pl.BlockSpec((tr, 256), lambda i: (i, j))
    tab = pl.BlockSpec((tr, 128), lambda i: (i, 0))
    return pl.pallas_call(
        body, grid=(T // tr,),
        in_specs=[col(1), col(2), col(3), tab, tab, _full((1, 256)), _full((1, 128)), _full((256, 256))],
        out_specs=[pl.BlockSpec((tr, 512), lambda i: (i, 0)), col(0), col(0)],
        out_shape=[SDS((T, 512), MXU_DTYPE), SDS((T, 256), MXU_DTYPE), SDS((T, 256), MXU_DTYPE)],
        name=name, compiler_params=_cp("parallel"))(z, z, z, cos, sin_signed, gq, gk, seg)


GROUP_HEADS = 4
GROUP_ROWS = GROUP_HEADS * ATTN_BLOCK
ALL_ROWS = N_Q_HEADS * ATTN_BLOCK


def _attn_mask(has_prev):
    qi = _row((ALL_ROWS, 2 * ATTN_BLOCK)) % ATTN_BLOCK
    kj = _lane((ALL_ROWS, 2 * ATTN_BLOCK))
    return (kj > qi) & (kj <= qi + ATTN_BLOCK) & ((kj >= ATTN_BLOCK) | has_prev)


STEP_BLOCKS = 4
STEP_ROWS = STEP_BLOCKS * ATTN_BLOCK


def _band(prev, cur, blk):
    lo = cur[(blk - 1) * ATTN_BLOCK:blk * ATTN_BLOCK] if blk else prev
    return jnp.concatenate([lo, cur[blk * ATTN_BLOCK:(blk + 1) * ATTN_BLOCK]], axis=0)


def _stack_heads(x, g):
    first = _lane((ATTN_BLOCK, 128)) < HEAD_DIM
    parts = []
    for pair in (2 * g, 2 * g + 1):
        x128 = x[:, 128 * pair:128 * (pair + 1)]
        zero = jnp.zeros_like(x128)
        parts += [jnp.where(first, x128, zero), jnp.where(first, zero, x128)]
    return jnp.concatenate(parts, axis=0)


def _unstack_heads(y):
    first = _lane((ATTN_BLOCK, 128)) < HEAD_DIM
    b = ATTN_BLOCK
    return jnp.concatenate([jnp.where(first, y[0:b], y[b:2 * b]), jnp.where(first, y[2 * b:3 * b], y[3 * b:4 * b])],
                           axis=1)


def _sink_col(sk_ref):
    return jnp.concatenate([jnp.broadcast_to(sk_ref[h:h + 1, 0:1], (ATTN_BLOCK, 1)) for h in range(N_Q_HEADS)],
                           axis=0)


def _by_group(a8, b2, dims=((1,), (0,))):
    return jnp.concatenate([_dot(a8[:GROUP_ROWS], b2[:, :128], dims), _dot(a8[GROUP_ROWS:], b2[:, 128:], dims)],
                           axis=0)


def _softmax_exp(q8, k2, mask, sink):
    s = jnp.where(mask, _by_group(q8, k2, NT), NEG_BIG)
    m = jnp.maximum(jnp.max(s, axis=1, keepdims=True), sink)
    p = jnp.exp(s - m)
    ps = jnp.exp(sink - m)
    return p, ps, 1.0 / (jnp.sum(p, axis=1, keepdims=True) + ps)


def _attn_fwd(q, k, v, sinks_b, *, name):
    T = q.shape[0]
    nb = T // ATTN_BLOCK

    def body(q_ref, kc_ref, kp_ref, vc_ref, vp_ref, sk_ref, o_ref):
        n = pl.program_id(0)
        kc, kp, vc, vp = kc_ref[...], kp_ref[...], vc_ref[...], vp_ref[...]
        sink = _sink_col(sk_ref)
        for blk in range(STEP_BLOCKS):
            rows = slice(blk * ATTN_BLOCK, (blk + 1) * ATTN_BLOCK)
            mask = _attn_mask((n > 0) if blk == 0 else True)
            k2, v2 = _band(kp, kc, blk), _band(vp, vc, blk)
            qv = q_ref[rows, :]
            q8 = jnp.concatenate([_stack_heads(qv, 0), _stack_heads(qv, 1)], axis=0)
            p, _, inv = _softmax_exp(q8, k2, mask, sink)
            o8 = _by_group(p, v2) * inv
            o_ref[rows, :] = jnp.concatenate([_unstack_heads(o8[:GROUP_ROWS]), _unstack_heads(o8[GROUP_ROWS:])],
                                             axis=1).astype(o_ref.dtype)

    cur = lambda w: pl.BlockSpec((STEP_ROWS, w), lambda n: (n, 0))
    prev = lambda w: pl.BlockSpec((ATTN_BLOCK, w), lambda n: (jnp.maximum(STEP_BLOCKS * n - 1, 0), 0))
    return pl.pallas_call(
        body, grid=(nb // STEP_BLOCKS,),
        in_specs=[cur(512), cur(256), prev(256), cur(256), prev(256), _full((8, 128))],
        out_specs=cur(512), out_shape=SDS((T, 512), MXU_DTYPE), name=name,
        compiler_params=_cp("parallel"))(q, k, k, v, v, sinks_b)


def _attn_bwd(q, k, v, sinks_b, do, *, name, deps=()):
    T = q.shape[0]
    nb = T // ATTN_BLOCK

    def body(q_ref, kc_ref, kp_ref, vc_ref, vp_ref, sk_ref, do_ref,
             dq_ref, dkc_ref, dkp_ref, dvc_ref, dvp_ref, dsk_ref):
        n = pl.program_id(0)
        kc, kp, vc, vp = kc_ref[...], kp_ref[...], vc_ref[...], vp_ref[...]
        sink = _sink_col(sk_ref)

        @pl.when(n == 0)
        def _():
            dsk_ref[...] = jnp.zeros_like(dsk_ref)

        for blk in range(STEP_BLOCKS):
            rows = slice(blk * ATTN_BLOCK, (blk + 1) * ATTN_BLOCK)
            mask = _attn_mask((n > 0) if blk == 0 else True)
            k2, v2 = _band(kp, kc, blk), _band(vp, vc, blk)
            qv, dov = q_ref[rows, :], do_ref[rows, :]
            q8 = jnp.concatenate([_stack_heads(qv, 0), _stack_heads(qv, 1)], axis=0)
            do8 = jnp.concatenate([_stack_heads(dov, 0), _stack_heads(dov, 1)], axis=0)
            p, ps, inv = _softmax_exp(q8, k2, mask, sink)
            pn = p * inv
            delta = jnp.sum(do8 * _by_group(pn, v2), axis=1, keepdims=True)
            ds = pn * (_by_group(do8, v2, NT) - delta)
            dq8 = _by_group(ds, k2)
            dq_ref[rows, :] = jnp.concatenate([_unstack_heads(dq8[:GROUP_ROWS]), _unstack_heads(dq8[GROUP_ROWS:])],
                                              axis=1)
            dk = jnp.concatenate([_dot(ds[:GROUP_ROWS], q8[:GROUP_ROWS], TN),
                                  _dot(ds[GROUP_ROWS:], q8[GROUP_ROWS:], TN)], axis=1)
            dv = jnp.concatenate([_dot(pn[:GROUP_ROWS], do8[:GROUP_ROWS], TN),
                                  _dot(pn[GROUP_ROWS:], do8[GROUP_ROWS:], TN)], axis=1)
            wsink = (ps * inv) * delta
            for h in range(N_Q_HEADS):
                dsink = -jnp.sum(wsink[ATTN_BLOCK * h:ATTN_BLOCK * (h + 1)], axis=0, keepdims=True)
                dsk_ref[h:h + 1, :] += jnp.broadcast_to(dsink, (1, 128))
            dkp_ref[rows, :] = dk[:ATTN_BLOCK]
            dkc_ref[rows, :] = dk[ATTN_BLOCK:]
            dvp_ref[rows, :] = dv[:ATTN_BLOCK]
            dvc_ref[rows, :] = dv[ATTN_BLOCK:]

    cur = lambda w: pl.BlockSpec((STEP_ROWS, w), lambda n: (n, 0))
    prev = lambda w: pl.BlockSpec((ATTN_BLOCK, w), lambda n: (jnp.maximum(STEP_BLOCKS * n - 1, 0), 0))
    f = lambda w: SDS((T, w), F32)
    return pl.pallas_call(
        _after(body, 7, deps), grid=(nb // STEP_BLOCKS,),
        in_specs=[cur(512), cur(256), prev(256), cur(256), prev(256), _full((8, 128)), cur(512)] + [ANY] * len(deps),
        out_specs=[cur(512), cur(256), cur(256), cur(256), cur(256), _full((8, 128))],
        out_shape=[f(512), f(256), f(256), f(256), f(256), SDS((8, 128), F32)],
        name=name, compiler_params=_cp("arbitrary"))(q, k, k, v, v, sinks_b, do, *deps)


def _mixer_ab_bwd(z, cos, sin_signed, gq, gk, seg, dq, dkc, dkp, dvc, dvp, dpa, wbd, scale, dz, *, tr, name, deps=()):
    T = z.shape[0]
    n = T // tr
    hb = tr // 16
    ab = tr // ATTN_BLOCK

    def unfold(cur, nxt_tile, nxt_halo, i):
        nxt = jnp.concatenate([nxt_tile[ATTN_BLOCK:], jnp.where(i == n - 1, 0.0, nxt_halo)], axis=0)
        tot = cur + nxt
        first = _lane((tr, 128)) < HEAD_DIM
        a = tot[:, :128]
        b = tot[:, 128:]
        a = a + pltpu.roll(a, HEAD_DIM, 1)
        b = b + pltpu.roll(b, HEAD_DIM, 1)
        return jnp.where(first, a, b)

    def body(xp_ref, xpp_ref, qa_ref, qb_ref, kv_ref, c_ref, s_ref, gq_ref, gk_ref, seg_ref,
             dq_ref, dkc_ref, dkp_ref, dkh_ref, dvc_ref, dvp_ref, dvh_ref, dpa_ref, dpan_ref, w_ref, sc_ref, _dz_in,
             dz_ref, dgq_ref, dgk_ref, dw_ref, dsc_ref):
        i = pl.program_id(0)
        c, s, seg_m = c_ref[...], s_ref[...], seg_ref[...]
        scale_q = HEAD_DIM ** -0.5
        dqv = dq_ref[...] * scale_q
        dxa, dga = _norm_rope_bwd(qa_ref[...], gq_ref[...], c, s, seg_m, dqv[:, :256])
        dxb, dgb = _norm_rope_bwd(qb_ref[...], gq_ref[...], c, s, seg_m, dqv[:, 256:])
        dk = unfold(dkc_ref[...], dkp_ref[...], dkh_ref[...], i)
        dv = unfold(dvc_ref[...], dvp_ref[...], dvh_ref[...], i)
        kv = kv_ref[...]
        dxk, dgk = _norm_rope_bwd(kv[:, :KV_WIDTH], gk_ref[...], c, s, seg_m[:128, :128], dk)
        dxp, dwbd, dscale = _pool_bwd_tile(i, n, tr, xp_ref[...], xpp_ref[...], dpa_ref[...], dpan_ref[...],
                                           w_ref[...], sc_ref[...])
        dz_ref[...] = jnp.concatenate([dxp, dxa, dxb, dxk, dv], axis=1).astype(dz_ref.dtype)

        @pl.when(i == 0)
        def _():
            dgq_ref[...] = jnp.zeros_like(dgq_ref)
            dgk_ref[...] = jnp.zeros_like(dgk_ref)
            dw_ref[...] = jnp.zeros_like(dw_ref)
            dsc_ref[...] = jnp.zeros_like(dsc_ref)
        dgq_ref[...] += _fold_lanes(dga + dgb, HEAD_DIM)
        dgk_ref[...] += _fold_lanes(dgk, HEAD_DIM)
        dw_ref[...] += dwbd
        dsc_ref[...] += dscale

    col = lambda j: pl.BlockSpec((tr, 256), lambda i: (i, j))
    rows = lambda w: pl.BlockSpec((tr, w), lambda i: (i, 0))
    nxt_blk = pl.BlockSpec((ATTN_BLOCK, 256), lambda i: (jnp.minimum((i + 1) * ab, T // ATTN_BLOCK - 1), 0))
    prev16 = pl.BlockSpec((16, 256), lambda i: (jnp.maximum(i * hb - 1, 0), 0))
    next16 = pl.BlockSpec((16, 256), lambda i: (jnp.minimum((i + 1) * hb, T // 16 - 1), 0))
    return pl.pallas_call(
        _after(body, 22, deps), grid=(n,),
        in_specs=[col(0), prev16, col(1), col(2), col(3), rows(128), rows(128),
                  _full((1, 256)), _full((1, 128)), _full((256, 256)),
                  rows(512), rows(256), rows(256), nxt_blk, rows(256), rows(256), nxt_blk,
                  rows(256), next16, _full((256, 256)), _full((1, 256)), ANY] + [ANY] * len(deps),
        out_specs=[rows(1024), _full((1, 256)), _full((1, 128)), _full((256, 256)), _full((1, 256))],
        out_shape=[SDS((T, IN_COLS), MXU_DTYPE), SDS((1, 256), F32), SDS((1, 128), F32),
                   SDS((256, 256), F32), SDS((1, 256), F32)],
        input_output_aliases={21: 0}, name=name, compiler_params=_cp("arbitrary"))(
            z, z, z, z, z, cos, sin_signed, gq, gk, seg, dq, dkc, dkp, dkp, dvc, dvp, dvp, dpa, dpa, wbd, scale, dz,
            *deps)


def _sgu_common(zu, zv, vn, seg):
    u, du = _gelu_and_grad(zu)
    gv, dgv = _gelu_and_grad(zv)
    ms = _split_dot(gv * gv, seg) * (1.0 / HEAD_DIM)
    r = lax.rsqrt(ms + EPS)
    xh = gv * r
    return u, du, dgv, r, xh, xh * vn


def _sgu_fwd(z, wtril, bexp, vn, seg, *, tr, name):
    T = z.shape[0]
    nch = tr // CHUNK

    def body(u_ref, v_ref, w_ref, b_ref, vn_ref, seg_ref, o_ref):
        u, _, _, _, _, vg = _sgu_common(u_ref[...], v_ref[...], vn_ref[...], seg_ref[...])
        grp = _lane((CHUNK, SGU_WIDTH)) // HEAD_DIM
        outs = []
        for ch in range(nch):
            vc = vg[ch * CHUNK:(ch + 1) * CHUNK]
            s = b_ref[...]
            for g in range(4):
                s = s + jnp.where(grp == g, _dot(w_ref[g], vc), 0.0)
            outs.append(u[ch * CHUNK:(ch + 1) * CHUNK] * s)
        o_ref[...] = jnp.concatenate(outs, axis=0).astype(o_ref.dtype)

    col = lambda j: pl.BlockSpec((tr, 256), lambda i: (i, j))
    return pl.pallas_call(
        body, grid=(T // tr,),
        in_specs=[col(4), col(5), _full((4, CHUNK, CHUNK)), _full((CHUNK, 256)), _full((1, 256)), _full((256, 256))],
        out_specs=col(0), out_shape=SDS((T, SGU_WIDTH), MXU_DTYPE), name=name,
        compiler_params=_cp("parallel"))(z, z, wtril, bexp, vn, seg)


def _sgu_bwd(z, wtril, bexp, vn, seg, dsg, dz, *, tr, name):
    T = z.shape[0]
    nch = tr // CHUNK

    def body(u_ref, v_ref, w_ref, b_ref, vn_ref, seg_ref, d_ref, _dz_in, dz_ref, dw_ref, db_ref, dvn_ref):
        i = pl.program_id(0)
        seg_m = seg_ref[...]
        vn_v = vn_ref[...]
        u, du, dgv, r, xh, vg = _sgu_common(u_ref[...], v_ref[...], vn_v, seg_m)
        d = d_ref[...]
        grp = _lane((CHUNK, SGU_WIDTH)) // HEAD_DIM
        tril = _row((CHUNK, CHUNK)) >= _lane((CHUNK, CHUNK))

        @pl.when(i == 0)
        def _():
            dw_ref[...] = jnp.zeros_like(dw_ref)
            db_ref[...] = jnp.zeros_like(db_ref)
            dvn_ref[...] = jnp.zeros_like(dvn_ref)

        dus, dvgs = [], []
        for ch in range(nch):
            sl = slice(ch * CHUNK, (ch + 1) * CHUNK)
            vc = vg[sl]
            s = b_ref[...]
            for g in range(4):
                s = s + jnp.where(grp == g, _dot(w_ref[g], vc), 0.0)
            dus.append(d[sl] * s)
            ds = d[sl] * u[sl]
            db_ref[...] += _split_dot(ds, seg_m)
            dvg = jnp.zeros((CHUNK, SGU_WIDTH), F32)
            for g in range(4):
                dsm = jnp.where(grp == g, ds, 0.0)
                dvg = dvg + jnp.where(grp == g, _dot(w_ref[g], ds, TN), 0.0)
                dw_ref[g] += jnp.where(tril, _dot(dsm, vc, NT), 0.0)
            dvgs.append(dvg)
        dup = jnp.concatenate(dus, axis=0)
        dvg = jnp.concatenate(dvgs, axis=0)
        dvn_ref[...] += _fold_lanes(jnp.sum(dvg * xh, axis=0, keepdims=True), HEAD_DIM)
        gy = dvg * vn_v
        dgvv = r * (gy - xh * (_split_dot(xh * gy, seg_m) * (1.0 / HEAD_DIM)))
        dz_ref[...] = jnp.concatenate([dup * du, dgvv * dgv], axis=1).astype(dz_ref.dtype)

    col = lambda j: pl.BlockSpec((tr, 256), lambda i: (i, j))
    return pl.pallas_call(
        body, grid=(T // tr,),
        in_specs=[col(4), col(5), _full((4, CHUNK, CHUNK)), _full((CHUNK, 256)), _full((1, 256)), _full((256, 256)),
                  col(0), ANY],
        out_specs=[pl.BlockSpec((tr, 512), lambda i: (i, 2)), _full((4, CHUNK, CHUNK)), _full((CHUNK, 256)),
                   _full((1, 256))],
        out_shape=[SDS((T, IN_COLS), MXU_DTYPE), SDS((4, CHUNK, CHUNK), F32), SDS((CHUNK, 256), F32),
                   SDS((1, 256), F32)],
        input_output_aliases={7: 0}, name=name, compiler_params=_cp("arbitrary"))(
            z, z, wtril, bexp, vn, seg, dsg, dz)


def _merge_fwd(pa, at, sg, wa, wb, wc, z, x, w_out, *, tm, tn, name):
    T = pa.shape[0]
    gb = GATE_COL0 // tn
    nb = D_MODEL // tn

    def body(pa_ref, at_ref, sg_ref, wa_ref, wb_ref, wc_ref, g0_ref, g1_ref, g2_ref, x_ref, wo_ref,
             m_ref, y_ref, x1_ref):
        j = pl.program_id(1)
        acc = None
        for idx, (op_ref, w_ref, g_ref) in enumerate(((pa_ref, wa_ref, g0_ref), (at_ref, wb_ref, g1_ref),
                                                      (sg_ref, wc_ref, g2_ref))):
            y = _dot(op_ref[...], w_ref[...])
            y_ref[idx] = y.astype(y_ref.dtype)
            t = _sigmoid(g_ref[...]) * y
            acc = t if acc is None else acc + t
        merged = acc.astype(m_ref.dtype)
        m_ref[...] = merged
        p = _dot(merged, wo_ref[...])

        @pl.when(j == 0)
        def _():
            x1_ref[...] = x_ref[...] + p

        @pl.when(j > 0)
        def _():
            x1_ref[...] += p

    op = lambda w: pl.BlockSpec((tm, w), lambda i, j: (i, 0))
    wt = lambda k: pl.BlockSpec((k, tn), lambda i, j: (0, j))
    gate = lambda b: pl.BlockSpec((tm, tn), lambda i, j: (i, gb + b * nb + j))
    return pl.pallas_call(
        body, grid=(T // tm, nb),
        in_specs=[op(256), op(512), op(256), wt(256), wt(512), wt(256), gate(0), gate(1), gate(2),
                  op(D_MODEL), pl.BlockSpec((tn, D_MODEL), lambda i, j: (j, 0))],
        out_specs=[pl.BlockSpec((tm, tn), lambda i, j: (i, j)), pl.BlockSpec((3, tm, tn), lambda i, j: (0, i, j)),
                   op(D_MODEL)],
        out_shape=[SDS((T, D_MODEL), MXU_DTYPE), SDS((3, T, D_MODEL), MXU_DTYPE), SDS((T, D_MODEL), F32)],
        name=name, compiler_params=_cp("parallel", "arbitrary"))(pa, at, sg, wa, wb, wc, z, z, z, x, w_out)


def _out_dx_merge_bwd(dxb, w_out, y, z, ws, xs, *, tm, tn, name):
    T = dxb.shape[0]
    gb = GATE_COL0 // tn
    nb = D_MODEL // tn
    nr = T // tm
    widths = [w.shape[0] for w in ws]

    def body(dx_ref, w_ref, y_ref, g_ref, *refs):
        w_refs, x_refs = refs[0:3], refs[3:6]
        dz_ref, dx_refs, dw_refs = refs[6], refs[7:10], refs[10:13]
        dm_ref, acc_refs = refs[13], refs[14:17]
        i, b, j = pl.program_id(0), pl.program_id(1), pl.program_id(2)

        @pl.when((b == 0) & (j == 0))
        def _():
            dm = _dot(dx_ref[...], w_ref[...], NT)
            for jj in range(nb):
                dm_ref[jj] = dm[:, jj * tn:(jj + 1) * tn]

        d = dm_ref[j]
        g = _sigmoid(g_ref[...])
        dy = (d * g).astype(MXU_DTYPE)
        dz_ref[...] = (d * y_ref[...].astype(F32) * g * (1.0 - g)).astype(dz_ref.dtype)
        for branch in range(3):
            @pl.when(b == branch)
            def _():
                p = _dot(dy, w_refs[branch][...], NT)
                q = _dot(x_refs[branch][...], dy, TN)

                @pl.when(j == 0)
                def _():
                    dx_refs[branch][...] = p

                @pl.when(j > 0)
                def _():
                    dx_refs[branch][...] += p

                @pl.when(i == 0)
                def _():
                    acc_refs[branch][j] = q

                @pl.when(i > 0)
                def _():
                    acc_refs[branch][j] += q

        @pl.when((i == nr - 1) & (b == 2) & (j == nb - 1))
        def _():
            for branch in range(3):
                for jj in range(nb):
                    dw_refs[branch][:, jj * tn:(jj + 1) * tn] = acc_refs[branch][jj]

    wspec = lambda k: pl.BlockSpec((k, tn), lambda i, b, j: (0, j))
    rows = lambda k: pl.BlockSpec((tm, k), lambda i, b, j: (i, 0))
    return pl.pallas_call(
        body, grid=(nr, 3, nb),
        in_specs=[rows(D_MODEL), _full((D_MODEL, D_MODEL)),
                  pl.BlockSpec((None, tm, tn), lambda i, b, j: (b, i, j)),
                  pl.BlockSpec((tm, tn), lambda i, b, j: (i, gb + b * nb + j))]
        + [wspec(k) for k in widths] + [rows(k) for k in widths],
        out_specs=[pl.BlockSpec((tm, tn), lambda i, b, j: (i, gb + b * nb + j))]
        + [rows(k) for k in widths] + [_full((k, D_MODEL)) for k in widths],
        out_shape=[SDS((T, IN_COLS), MXU_DTYPE)] + [SDS((T, k), F32) for k in widths]
        + [SDS((k, D_MODEL), F32) for k in widths],
        scratch_shapes=[pltpu.VMEM((nb, tm, tn), F32)] + [pltpu.VMEM((nb, k, tn), F32) for k in widths],
        name=name, compiler_params=_cp("arbitrary", "arbitrary", "arbitrary"))(dxb, w_out, y, z, *ws, *xs)


def _conv3(xe, w, b):
    return (w[0:1] * pltpu.roll(xe, 2, 0) + w[1:2] * pltpu.roll(xe, 1, 0) + w[2:3] * xe)[8:] + b


def _conv_act_fwd(up, cw, cb, *, tr, tc, name):
    T = up.shape[0]
    nc = D_FF // tc
    hb = tr // 8

    def body(ug_ref, ugp_ref, uv_ref, uvp_ref, wg_ref, wv_ref, bg_ref, bv_ref, o_ref):
        i = pl.program_id(1)
        first = i == 0
        cg = _conv3(jnp.concatenate([jnp.where(first, 0.0, ugp_ref[...]), ug_ref[...]], axis=0), wg_ref[...], bg_ref[...])
        cv = _conv3(jnp.concatenate([jnp.where(first, 0.0, uvp_ref[...]), uv_ref[...]], axis=0), wv_ref[...], bv_ref[...])
        o_ref[...] = (cg * _sigmoid(cg) * cv).astype(o_ref.dtype)

    tile = lambda off: pl.BlockSpec((tr, tc), lambda j, i: (i, off + j))
    prev = lambda off: pl.BlockSpec((8, tc), lambda j, i: (jnp.maximum(i * hb - 1, 0), off + j))
    par = lambda rows, off: pl.BlockSpec((rows, tc), lambda j, i: (0, off + j))
    return pl.pallas_call(
        body, grid=(nc, T // tr),
        in_specs=[tile(0), prev(0), tile(nc), prev(nc), par(3, 0), par(3, nc), par(1, 0), par(1, nc)],
        out_specs=pl.BlockSpec((tr, tc), lambda j, i: (i, j)),
        out_shape=SDS((T, D_FF), MXU_DTYPE), name=name,
        compiler_params=_cp("parallel", "parallel"))(up, up, up, up, cw, cw, cb, cb)


def _conv_act_bwd(up, cw, cb, dact, *, tr, tc, name, deps=()):
    T = up.shape[0]
    nc = D_FF // tc
    hb = tr // 8
    nr = T // tr

    def body(ug_ref, ugp_ref, ugn_ref, uv_ref, uvp_ref, uvn_ref, da_ref, dan_ref, wg_ref, wv_ref, bg_ref, bv_ref,
             du_ref, dwg_ref, dwv_ref, dbg_ref, dbv_ref):
        i = pl.program_id(1)
        first, last = i == 0, i == nr - 1
        da = jnp.concatenate([da_ref[...], jnp.where(last, 0.0, dan_ref[...])], axis=0)
        uge = jnp.concatenate([jnp.where(first, 0.0, ugp_ref[...]), ug_ref[...], ugn_ref[...]], axis=0)
        uve = jnp.concatenate([jnp.where(first, 0.0, uvp_ref[...]), uv_ref[...], uvn_ref[...]], axis=0)
        wg, wv = wg_ref[...], wv_ref[...]
        ug1, ug2 = pltpu.roll(uge, 1, 0)[8:], pltpu.roll(uge, 2, 0)[8:]
        uv1, uv2 = pltpu.roll(uve, 1, 0)[8:], pltpu.roll(uve, 2, 0)[8:]
        cg = wg[0:1] * ug2 + wg[1:2] * ug1 + wg[2:3] * uge[8:] + bg_ref[...]
        cv = wv[0:1] * uv2 + wv[1:2] * uv1 + wv[2:3] * uve[8:] + bv_ref[...]
        sg = _sigmoid(cg)
        dcg = da * cv * (sg * (1.0 + cg * (1.0 - sg)))
        dcv = da * (cg * sg)
        nrow = tr + 8

        def back(dc, w):
            return (w[2:3] * dc + w[1:2] * pltpu.roll(dc, nrow - 1, 0) + w[0:1] * pltpu.roll(dc, nrow - 2, 0))[:tr]

        du_ref[0] = back(dcg, wg).astype(du_ref.dtype)
        du_ref[1] = back(dcv, wv).astype(du_ref.dtype)

        def wgrad(dc, u0, u1, u2):
            d = dc[:tr]
            rows = [jnp.sum(d * u2[:tr], axis=0, keepdims=True), jnp.sum(d * u1[:tr], axis=0, keepdims=True),
                    jnp.sum(d * u0[8:8 + tr], axis=0, keepdims=True)]
            return jnp.concatenate(rows, axis=0), jnp.sum(d, axis=0, keepdims=True)

        dwg, dbg = wgrad(dcg, uge, ug1, ug2)
        dwv, dbv = wgrad(dcv, uve, uv1, uv2)

        @pl.when(first)
        def _():
            dwg_ref[...] = jnp.zeros_like(dwg_ref)
            dwv_ref[...] = jnp.zeros_like(dwv_ref)
            dbg_ref[...] = jnp.zeros_like(dbg_ref)
            dbv_ref[...] = jnp.zeros_like(dbv_ref)
        dwg_ref[...] += dwg
        dwv_ref[...] += dwv
        dbg_ref[...] += dbg
        dbv_ref[...] += dbv

    tile = lambda off: pl.BlockSpec((tr, tc), lambda j, i: (i, off + j))
    prev = lambda off: pl.BlockSpec((8, tc), lambda j, i: (jnp.maximum(i * hb - 1, 0), off + j))
    nxt = lambda off: pl.BlockSpec((8, tc), lambda j, i: (jnp.minimum((i + 1) * hb, T // 8 - 1), off + j))
    par = lambda rows, off: pl.BlockSpec((rows, tc), lambda j, i: (0, off + j))
    acc = lambda rows: pl.BlockSpec((rows, tc), lambda j, i: (0, j))
    return pl.pallas_call(
        _after(body, 12, deps), grid=(nc, nr),
        in_specs=[tile(0), prev(0), nxt(0), tile(nc), prev(nc), nxt(nc), tile(0), nxt(0),
                  par(3, 0), par(3, nc), par(1, 0), par(1, nc)] + [ANY] * len(deps),
        out_specs=[pl.BlockSpec((2, tr, tc), lambda j, i: (0, i, j)), acc(3), acc(3), acc(1), acc(1)],
        out_shape=[SDS((2, T, D_FF), MXU_DTYPE), SDS((3, D_FF), F32), SDS((3, D_FF), F32),
                   SDS((1, D_FF), F32), SDS((1, D_FF), F32)],
        name=name, compiler_params=_cp("parallel", "arbitrary"))(
            up, up, up, up, up, up, dact, dact, cw, cw, cb, cb, *deps)


def _row_tile(rows, cap):
    t = min(cap, rows)
    t -= t % 8
    while rows % t:
        t -= 8
    return t


def _adamw(w, g, m, v, *, tr, name, copy_g=False):
    R, C = w.shape
    assert R % tr == 0, (R, tr)

    def body(w_ref, g_ref, m_ref, v_ref, d_ref, nm_ref, nv_ref, *rest):
        gv = g_ref[...]
        mn = ADAM_B1 * m_ref[...] + (1.0 - ADAM_B1) * gv
        vn = ADAM_B2 * v_ref[...] + (1.0 - ADAM_B2) * (gv * gv)
        m_hat = mn / (1.0 - ADAM_B1 ** ADAM_STEP)
        v_hat = vn / (1.0 - ADAM_B2 ** ADAM_STEP)
        d_ref[...] = -ADAM_LR * (m_hat / (jnp.sqrt(v_hat) + ADAM_EPS) + ADAM_WD * w_ref[...])
        nm_ref[...] = mn
        nv_ref[...] = vn
        if copy_g:
            rest[0][...] = gv

    rows = pl.BlockSpec((tr, C), lambda i: (i, 0))
    n_out = 4 if copy_g else 3
    return pl.pallas_call(
        body, grid=(R // tr,), in_specs=[rows] * 4, out_specs=[rows] * n_out,
        out_shape=[SDS((R, C), F32)] * n_out, name=name, compiler_params=_cp("parallel"))(w, g, m, v)


def _sum_slots(r, *, tr, name):
    S, R, C = r.shape
    assert R % tr == 0, (R, tr)

    def body(r_ref, o_ref):
        acc = r_ref[0]
        for s in range(1, S):
            acc = acc + r_ref[s]
        o_ref[...] = acc

    return pl.pallas_call(
        body, grid=(R // tr,), in_specs=[pl.BlockSpec((S, tr, C), lambda i: (0, i, 0))],
        out_specs=pl.BlockSpec((tr, C), lambda i: (i, 0)), out_shape=SDS((R, C), F32),
        name=name, compiler_params=_cp("parallel"))(r)


def _pair_add(g4, h, pos, *, name):
    A, _, r, C = g4.shape
    cs = C if A == N_CHIPS else C // N_CHIPS
    tr = _row_tile(r, 256)
    if A == N_CHIPS:
        g_map, h_map = (lambda t, i, pos: (t, pos[1], i, 0)), (lambda t, i, pos: (t, i, 0))
    else:
        g_map, h_map = (lambda t, i, pos: (0, pos[1], i, t)), (lambda t, i, pos: (0, i, t))

    def body(pos_ref, g_ref, h_ref, o_ref):
        o_ref[...] = (g_ref[...] + h_ref[...]).astype(o_ref.dtype)

    grid_spec = pltpu.PrefetchScalarGridSpec(
        num_scalar_prefetch=1, grid=(N_CHIPS, r // tr),
        in_specs=[pl.BlockSpec((None, None, tr, cs), g_map), pl.BlockSpec((None, tr, cs), h_map)],
        out_specs=pl.BlockSpec((None, tr, cs), lambda t, i, pos: (t, i, 0)))
    return pl.pallas_call(body, grid_spec=grid_spec, out_shape=SDS((N_CHIPS, r, cs), COMM_DTYPE), name=name,
                          compiler_params=_cp("parallel", "parallel"))(pos, g4, h)


def _chip_sum(p, r2, f_into, pos, layer, *, name):
    _, r, cs = p.shape
    tr = _row_tile(r, 256)

    def body(pos_ref, own_ref, r_ref, *rest):
        o_ref = rest[-1]
        o_ref[...] = ((own_ref[...].astype(F32) + r_ref[0].astype(F32)) + r_ref[1].astype(F32)) + r_ref[2].astype(F32)

    in_specs = [pl.BlockSpec((None, tr, cs), lambda i, pos: (pos[0], i, 0)),
                pl.BlockSpec((3, tr, cs), lambda i, pos: (0, i, 0))]
    operands = [pos, p, r2]
    aliases = {}
    if f_into is not None:
        in_specs.append(ANY)
        operands.append(f_into)
        aliases = {3: 0}
    grid_spec = pltpu.PrefetchScalarGridSpec(
        num_scalar_prefetch=1, grid=(r // tr,), in_specs=in_specs,
        out_specs=pl.BlockSpec((None, None, tr, cs), lambda i, pos: (layer, pos[1], i, 0)))
    return pl.pallas_call(body, grid_spec=grid_spec, out_shape=SDS((DEPTH, 2, r, cs), F32), name=name,
                          input_output_aliases=aliases, compiler_params=_cp("parallel"))(*operands)


def _mesh_pos():
    return lax.axis_index("x"), lax.axis_index("y"), lax.axis_index("c")


HBM = pl.BlockSpec(memory_space=pltpu.HBM)
SEM = pl.BlockSpec(memory_space=pltpu.SEMAPHORE)
DATAFLOW = pltpu.SideEffectType.DATAFLOW_SIDE_EFFECTING
CHIP_FLIPS = (2, 1, 3)


def _chip_peers():
    x, y, c = _mesh_pos()
    return 2 * x + y, [(1 - x, y, c), (x, 1 - y, c), (1 - x, 1 - y, c)], (x, y, 1 - c), c


def _split_start(arrays, n_copies, issue, *, name, deps=()):
    k = len(arrays)
    nd = len(deps)

    def body(*refs):
        issue(refs[:k], refs[k + nd], refs[k + nd + 1])
        refs[2 * k + nd + 2][...] = jnp.zeros((8, 128), F32)

    out = pl.pallas_call(
        body, name=name,
        out_shape=(pltpu.SemaphoreType.DMA((n_copies,)), pltpu.SemaphoreType.DMA((n_copies,)),
                   *[pltpu.HBM(a.shape, a.dtype) for a in arrays], SDS((8, 128), F32)),
        in_specs=[HBM] * k + [ANY] * nd, out_specs=(SEM, SEM, *[HBM] * k, pl.BlockSpec(memory_space=pltpu.VMEM)),
        input_output_aliases={i: 2 + i for i in range(k)},
        compiler_params=pltpu.CompilerParams(has_side_effects=DATAFLOW))(
            *[pltpu.with_memory_space_constraint(a, pltpu.HBM) for a in arrays], *deps)
    return (out[0], out[1]), list(out[2:2 + k]), out[2 + k]


def _split_wait(sems, arrays, after, waits, *, name):
    k = len(arrays)
    afters = tuple(after) if isinstance(after, (tuple, list)) else (after,)

    def body(*refs):
        waits(refs[:k], refs[k], refs[k + 1])

    out = pl.pallas_call(
        body, name=name, out_shape=tuple(pltpu.HBM(a.shape, a.dtype) for a in arrays),
        in_specs=[HBM] * k + [SEM, SEM] + [ANY] * len(afters), out_specs=tuple([HBM] * k),
        input_output_aliases={i: i for i in range(k)},
        compiler_params=pltpu.CompilerParams(has_side_effects=DATAFLOW))(*arrays, sems[0], sems[1], *afters)
    return list(out)


def _wait_both(cp):
    cp.wait_send()
    cp.wait_recv()


def _cast_place(shard, pos, dtype, *, name, layer=None, slots=N_CHIPS, which=0):
    R, C = shard.shape[-2:]
    tr = R if R % 8 else _row_tile(R, 256)
    if layer is None:
        in_spec = pl.BlockSpec((tr, C), lambda i, pos: (i, 0))
    else:
        in_spec = pl.BlockSpec((None, tr, C), lambda i, pos: (layer, i, 0))

    def body(pos_ref, x_ref, o_ref):
        o_ref[...] = x_ref[...].astype(o_ref.dtype)

    grid_spec = pltpu.PrefetchScalarGridSpec(
        num_scalar_prefetch=1, grid=(R // tr,), in_specs=[in_spec],
        out_specs=pl.BlockSpec((None, tr, C), lambda i, pos: (pos[which], i, 0)))
    return pl.pallas_call(body, grid_spec=grid_spec, out_shape=SDS((slots, R, C), dtype), name=name,
                          compiler_params=_cp("parallel"))(pos, shard)


def _device_peers():
    x, y, c = _mesh_pos()
    peers = [(x ^ ((f >> 2) & 1), y ^ ((f >> 1) & 1), c ^ (f & 1)) for f in range(1, N_DEV)]
    return 4 * x + 2 * y + c, peers


class _Gather:
    def __init__(self, lands, name, deps=(), all_devices=False):
        n = len(lands)
        self.name = name
        npeer = N_DEV - 1 if all_devices else N_CHIPS - 1

        def copies(refs, ss, rs):
            me, peers = _device_peers() if all_devices else _chip_peers()[:2]
            return [pltpu.make_async_remote_copy(
                src_ref=refs[w].at[me], dst_ref=refs[w].at[me], send_sem=ss.at[npeer * w + p],
                recv_sem=rs.at[npeer * w + p], device_id=peers[p], device_id_type=MESH)
                for w in range(n) for p in range(npeer)]

        def issue(refs, ss, rs):
            for cp in copies(refs, ss, rs):
                cp.start()

        def waits(refs, ss, rs):
            for cp in copies(refs, ss, rs):
                _wait_both(cp)

        self._waits = waits
        self.sems, self.arrays, self.token = _split_start(list(lands), npeer * n, issue, name=name + "_start",
                                                          deps=deps)

    def wait(self, after):
        return _split_wait(self.sems, self.arrays, after, self._waits, name=self.name + "_wait")


def _swap_halves_start(g4s, *, name):
    n = len(g4s)
    lands = [lax.empty((g.shape[0],) + g.shape[2:], g.dtype) for g in g4s]

    def copies(refs, ss, rs):
        _, _, sibling, c = _chip_peers()
        return [pltpu.make_async_remote_copy(
            src_ref=refs[w].at[:, 1 - c], dst_ref=refs[n + w], send_sem=ss.at[w], recv_sem=rs.at[w],
            device_id=sibling, device_id_type=MESH) for w in range(n)]

    def issue(refs, ss, rs):
        for cp in copies(refs, ss, rs):
            cp.start()

    def waits(refs, ss, rs):
        for cp in copies(refs, ss, rs):
            _wait_both(cp)

    sems, arrays, token = _split_start(list(g4s) + lands, n, issue, name=name + "_start")
    return sems, arrays, token, waits


def _scatter_start(parts, *, name, deps=()):
    n = len(parts)
    lands = [lax.empty((3,) + p.shape[1:], p.dtype) for p in parts]

    def copies(refs, ss, rs):
        me, peers, _, _ = _chip_peers()
        return [pltpu.make_async_remote_copy(
            src_ref=refs[w].at[me ^ CHIP_FLIPS[p]], dst_ref=refs[n + w].at[p],
            send_sem=ss.at[3 * w + p], recv_sem=rs.at[3 * w + p], device_id=peers[p], device_id_type=MESH)
            for w in range(n) for p in range(3)]

    def issue(refs, ss, rs):
        for cp in copies(refs, ss, rs):
            cp.start()

    def waits(refs, ss, rs):
        for cp in copies(refs, ss, rs):
            _wait_both(cp)

    sems, arrays, token = _split_start(list(parts) + lands, 3 * n, issue, name=name + "_start", deps=deps)
    return sems, arrays, token, waits


def _pair_share_start(fs, layer, *, name):
    n = len(fs)

    def copies(refs, ss, rs):
        _, _, sibling, c = _chip_peers()
        return [pltpu.make_async_remote_copy(
            src_ref=refs[w].at[layer, c], dst_ref=refs[w].at[layer, c], send_sem=ss.at[w], recv_sem=rs.at[w],
            device_id=sibling, device_id_type=MESH) for w in range(n)]

    def issue(refs, ss, rs):
        for cp in copies(refs, ss, rs):
            cp.start()

    def waits(refs, ss, rs):
        for cp in copies(refs, ss, rs):
            _wait_both(cp)

    sems, arrays, token = _split_start(list(fs), n, issue, name=name + "_start")
    return sems, arrays, token, waits


BIG = ('w_in', 'w_proj_a', 'w_proj_b', 'w_proj_c', 'w_out', 'w_up', 'w_down')
BIG_SHARD_AXIS = {'w_in': 2, 'w_proj_a': 2, 'w_proj_b': 2, 'w_proj_c': 2, 'w_out': 1, 'w_up': 2, 'w_down': 1}
SMALL = ('norm1', 'q_norm', 'k_norm', 'sinks', 'w_pool', 'pool_scale', 'sgu_v_norm', 'w_s', 'b_s', 'norm2',
         'conv_b', 'conv_w')
WEIGHTS = ('norm1', 'w_in', 'q_norm', 'k_norm', 'sinks', 'w_pool', 'pool_scale', 'sgu_v_norm', 'w_s', 'b_s',
           'w_proj_a', 'w_proj_b', 'w_proj_c', 'w_out', 'norm2', 'w_up', 'conv_w', 'conv_b', 'w_down')


def _rope_tables(positions):
    inv_freq = ROPE_THETA ** (-jnp.arange(0, HEAD_DIM, 2, dtype=F32) / HEAD_DIM)
    ang = positions.astype(F32)[:, None] * inv_freq
    cos, sin = jnp.cos(ang), jnp.sin(ang)
    c = jnp.concatenate([cos, cos], axis=1)
    s = jnp.concatenate([-sin, sin], axis=1)
    return jnp.concatenate([c, c], axis=1), jnp.concatenate([s, s], axis=1)


def _block_diag4(w):
    out = jnp.zeros((POOL_WIDTH, POOL_WIDTH), w.dtype)
    for g in range(4):
        out = lax.dynamic_update_slice(out, w[g], (g * HEAD_DIM, g * HEAD_DIM))
    return out


def _local_step(x, target, cos, sin, sp, sched):
    T = x.shape[0]
    tm1 = min(1024, T)
    tm = min(512, T)
    tr = min(1024, T)
    trc = min(512, T)
    tkt = min(1024, T)
    seg = _seg_matrix(256, HEAD_DIM)
    saved = []
    xl = x
    for l in range(DEPTH):
        p = f"l{l}_"
        c = dict(
            g1=sp['norm1'][l][None], g2=sp['norm2'][l][None],
            wbd=_block_diag4(sp['w_pool'][l]).astype(MXU_DTYPE), scale=sp['pool_scale'][l][None],
            gq=jnp.tile(sp['q_norm'][l], 4)[None], gk=jnp.tile(sp['k_norm'][l], 2)[None],
            sinks=jnp.broadcast_to(sp['sinks'][l][:, None], (N_Q_HEADS, 128)),
            wtril=jnp.tril(sp['w_s'][l]).astype(MXU_DTYPE),
            bexp=jnp.repeat(sp['b_s'][l].T, HEAD_DIM, axis=1), vn=jnp.tile(sp['sgu_v_norm'][l], 4)[None],
            cb=sp['conv_b'][l][None])
        c['w_in'] = sched.weight('w_in', l, xl)
        z, h1 = _norm_mm(xl, c['g1'], c['w_in'], tm=tm1, tn=1152, name=p + "in_proj",
                         deps=sched.start_tokens() if l == 0 else ())
        pa = _pool_fwd(z, c['wbd'], c['scale'], tr=tr, name=p + "pool")
        q, k, v = _qkv_prep(z, cos, sin, c['gq'], c['gk'], seg, tr=tr, name=p + "qkv_prep")
        at = _attn_fwd(q, k, v, c['sinks'], name=p + "attn")
        sg = _sgu_fwd(z, c['wtril'], c['bexp'], c['vn'], seg, tr=tr, name=p + "sgu")
        for n in ('w_proj_a', 'w_proj_b', 'w_proj_c', 'w_out'):
            c[n] = sched.weight(n, l, (pa, at, sg))
        merged, y3, x1 = _merge_fwd(pa, at, sg, c['w_proj_a'], c['w_proj_b'], c['w_proj_c'], z, xl, c['w_out'],
                                    tm=tm, tn=512, name=p + "merge_out_proj")
        for n in ('w_up', 'conv_w', 'w_down'):
            c[n] = sched.weight(n, l, x1)
        up, h2 = _norm_mm(x1, c['g2'], c['w_up'], tm=tm1, tn=1408, name=p + "up_proj")
        act = _conv_act_fwd(up, c['conv_w'], c['cb'], tr=trc, tc=1408, name=p + "conv_act")
        saved.append(dict(c, x=xl, h1=h1, z=z, pa=pa, q=q, k=k, v=v, at=at, sg=sg, merged=merged, y3=y3,
                          x1=x1, h2=h2, up=up, act=act))
        if l < DEPTH - 1:
            xl = _mm(act, c['w_down'], mode='nn', add=x1, tm=tm, tn=D_MODEL, tk=D_FF, name=p + "down_proj")
        else:
            loss_row, dx, dxb = _down_proj_loss(act, c['w_down'], x1, target, tm=tm, name=p + "down_proj_loss")

    gs = {n: [None] * DEPTH for n in SMALL}
    for l in reversed(range(DEPTH)):
        p = f"l{l}_b_"
        s = saved[l]
        gb = {}
        dact = _mm(dxb, s['w_down'], mode='nt', tm=tm1, tn=1408, tk=D_MODEL, name=p + "down_dx")
        gb['w_down'] = _mm(s['act'], dxb, mode='tn', tm=1408, tn=D_MODEL, tk=tkt, name=p + "down_dw")
        toks = sched.slot(l, 'down', gb['w_down'])
        dup, dwg, dwv, dbg, dbv = _conv_act_bwd(s['up'], s['conv_w'], s['cb'], dact, tr=min(1024, T), tc=256,
                                                name=p + "conv_act", deps=toks)
        gs['conv_w'][l] = jnp.concatenate([dwg, dwv], axis=1)
        gs['conv_b'][l] = jnp.concatenate([dbg, dbv], axis=1)[0]
        toks = sched.slot(l, 'conv', dup)
        for half in range(2):
            gb['w_up'] = _mm(s['h2'], dup, mode='tn', b_lead=half, tm=D_MODEL, tn=1408, tk=tkt,
                             out_into=gb.get('w_up'), out_joff=2 * half, out_n=2 * D_FF, name=p + f"up_dw{half}",
                             deps=toks if half == 0 else ())
        toks = sched.slot(l, 'ffn', gb['w_up'], gb)
        dx1, dx1b, dg2 = _mm_nt_sharded_rms(dup, s['w_up'], s['x1'], s['g2'], dx, tm=tm,
                                            name=p + "up_dx_rms2", deps=toks)
        gs['norm2'][l] = dg2[0]
        gb['w_out'] = _mm(s['merged'], dx1b, mode='tn', tm=D_MODEL, tn=D_MODEL, tk=tkt, name=p + "out_dw")
        (dz, dpa, dat, dsg, gb['w_proj_a'], gb['w_proj_b'], gb['w_proj_c']) = _out_dx_merge_bwd(
            dx1b, s['w_out'], s['y3'], s['z'], [s['w_proj_a'], s['w_proj_b'], s['w_proj_c']],
            [s['pa'], s['at'], s['sg']], tm=tm, tn=512, name=p + "out_dx_merge")
        toks = sched.slot(l, 'mid', dz)
        dq, dkc, dkp, dvc, dvp, dsk = _attn_bwd(s['q'], s['k'], s['v'], s['sinks'], dat, name=p + "attn", deps=toks)
        gs['sinks'][l] = dsk[:, 0]
        toks = sched.slot(l, 'attn', dq)
        dz, dgq, dgk, dwbd, dsc = _mixer_ab_bwd(s['z'], cos, sin, s['gq'], s['gk'], seg, dq, dkc, dkp, dvc, dvp,
                                                dpa, s['wbd'], s['scale'], dz, tr=tr, name=p + "qkv_pool", deps=toks)
        gs['q_norm'][l] = dgq[0, :HEAD_DIM]
        gs['k_norm'][l] = dgk[0, :HEAD_DIM]
        gs['w_pool'][l] = jnp.stack([dwbd[g * HEAD_DIM:(g + 1) * HEAD_DIM, g * HEAD_DIM:(g + 1) * HEAD_DIM]
                                     for g in range(4)])
        gs['pool_scale'][l] = dsc[0]
        dz, dws, dbrows, dvn = _sgu_bwd(s['z'], s['wtril'], s['bexp'], s['vn'], seg, dsg, dz, tr=tr, name=p + "sgu")
        gs['w_s'][l] = dws
        gs['b_s'][l] = dbrows[:, ::HEAD_DIM].T
        gs['sgu_v_norm'][l] = dvn[0, :HEAD_DIM]
        gb['w_in'] = _mm(s['h1'], dz, mode='tn', tm=D_MODEL, tn=1152, tk=tkt, name=p + "in_dw")
        toks = sched.slot(l, 'mix', gb['w_in'], gb)
        dx, dxb, dg1 = _mm_nt_sharded_rms(dz, s['w_in'], s['x'], s['g1'], dx1, tm=tm,
                                          name=p + "in_dx_rms1", deps=toks)
        gs['norm1'][l] = dg1[0]
    gs = {n: jnp.stack(v) for n, v in gs.items()}
    return loss_row, dx, gs


GROUP_F = ('w_down', 'w_up')
GROUP_M = ('w_out', 'w_proj_a', 'w_proj_b', 'w_proj_c', 'w_in')
ROW_SHARDED = ('w_out', 'w_down')

REDUCE_PLAN = {
    (1, 'ffn'): (('S1', 'F', 1),),
    (1, 'mid'): (('W1', 'F', 1),),
    (1, 'mix'): (('S1', 'M', 1),),
    (0, 'down'): (('W1', 'M', 1),),
    (0, 'conv'): (('W2', 'F', 1),),
    (0, 'ffn'): (('S1', 'F', 0), ('W3', 'F', 1)),
    (0, 'mid'): (('W1', 'F', 0),),
    (0, 'attn'): (('W2', 'M', 1),),
    (0, 'mix'): (('S1', 'M', 0), ('W3', 'M', 1)),
}
REDUCE_TAIL_A = (('W1', 'M', 0), ('W2', 'F', 0))
REDUCE_TAIL_B = (('W3', 'F', 0),)
REDUCE_TAIL_C = (('W2', 'M', 0), ('W3', 'M', 0))


class _Comm:
    def __init__(self, w, pos):
        self.pos = pos
        groups = {'a': [('w_in', 0)],
                  'b': [(n, 0) for n in ('w_proj_a', 'w_proj_b', 'w_proj_c', 'w_out')],
                  'c': [(n, 0) for n in ('w_up', 'conv_w', 'w_down')],
                  'd': [(n, 1) for n in BIG] + [('conv_w', 1)]}
        self.gathers, self.group_of, self.weights = {}, {}, {}
        self.tokens = []
        for g, ks in groups.items():
            lands = [_cast_place(w[n], pos, F32 if n == 'conv_w' else MXU_DTYPE, layer=l, name=f"gw_place_{n}{l}")
                     for n, l in ks]
            self.gathers[g] = (_Gather(lands, "gw_" + g, deps=self.tokens[-1:]), ks)
            self.tokens.append(self.gathers[g][0].token)
            self.group_of.update({k: g for k in ks})
        self.red = {}
        self.final = {}

    def start_tokens(self):
        return self.tokens[-1:]

    def weight(self, name, layer, after):
        if (name, layer) not in self.weights:
            gather, ks = self.gathers[self.group_of[(name, layer)]]
            for (n, l), full in zip(ks, gather.wait(after)):
                if n == 'conv_w' or n.startswith('w_proj'):
                    full = full.transpose(1, 0, 2).reshape(full.shape[1], -1)
                elif n in ROW_SHARDED:
                    full = full.reshape(-1, full.shape[2])
                self.weights[(n, l)] = full
        return self.weights[(name, layer)]

    def slot(self, layer, slot, after, grads=None):
        tokens = []
        for step, grp, lyr in REDUCE_PLAN.get((layer, slot), ()):
            tok = self._step(step, grp, lyr, after, grads)
            if tok is not None:
                tokens.append(tok)
        return tokens

    def tail(self, steps, after, deps=()):
        toks = (self._step(step, grp, lyr, after, None, deps) for step, grp, lyr in steps)
        return [t for t in toks if t is not None]

    def shards(self):
        return {n: f.reshape(DEPTH, 2 * f.shape[2], f.shape[3]) for n, f in self.final.items()}

    def _step(self, step, grp, layer, after, grads, deps=()):
        names = GROUP_F if grp == 'F' else GROUP_M
        tag = f"{grp.lower()}{layer}"
        st = self.red.setdefault((grp, layer), {})
        n = len(names)
        if step == 'S1':
            g4s = []
            for nm in names:
                g = grads[nm]
                R, C = g.shape
                g4s.append(g.reshape(N_CHIPS, 2, R // (2 * N_CHIPS), C) if nm in ROW_SHARDED
                           else g.reshape(1, 2, R // 2, C))
            st['s1'] = _swap_halves_start(g4s, name="rs1_" + tag)
            return st['s1'][2]
        if step == 'W1':
            sems, arrays, _, waits = st.pop('s1')
            arrays = _split_wait(sems, arrays, after, waits, name=f"rs1_{tag}_wait")
            parts = [_pair_add(arrays[i], arrays[n + i], self.pos, name=f"pair_add_{tag}_{names[i]}")
                     for i in range(n)]
            st['s2'] = _scatter_start(parts, name="rs2_" + tag, deps=deps)
            return st['s2'][2]
        if step == 'W2':
            sems, arrays, _, waits = st.pop('s2')
            arrays = _split_wait(sems, arrays, after, waits, name=f"rs2_{tag}_wait")
            fs = [_chip_sum(arrays[i], arrays[n + i], self.final.get(names[i]), self.pos, layer,
                            name=f"chip_sum_{tag}_{names[i]}") for i in range(n)]
            st['s3'] = _pair_share_start(fs, layer, name="rs3_" + tag)
            return st['s3'][2]
        sems, arrays, _, waits = st.pop('s3')
        self.final.update(zip(names, _split_wait(sems, arrays, after, waits, name=f"rs3_{tag}_wait")))
        return None


def _pack(arrays):
    rows = []
    for a in arrays:
        nel = int(np.prod(a.shape))
        if nel % 1024 == 0:
            rows.append(a.astype(F32).reshape(nel // 128, 128))
        else:
            f = a.reshape(-1).astype(F32)
            rows.append(jnp.pad(f, (0, (-nel) % 1024)).reshape(-1, 128))
    return jnp.concatenate(rows, axis=0)


def _unpack(pack, shapes):
    out, row = [], 0
    for shp in shapes:
        nel = int(np.prod(shp))
        nrow = 8 * -(-nel // 1024)
        part = pack[row:row + nrow]
        out.append(part.reshape(shp) if nel % 1024 == 0 else part.reshape(-1)[:nel].reshape(shp))
        row += nrow
    return out


def kernel(x, positions, norm1, w_in, q_norm, k_norm, sinks, w_pool, pool_scale, sgu_v_norm, w_s, b_s, w_proj_a, w_proj_b, w_proj_c, w_out, norm2, w_up, conv_w, conv_b, w_down, loss_target, m_norm1, m_w_in, m_q_norm, m_k_norm, m_sinks, m_w_pool, m_pool_scale, m_sgu_v_norm, m_w_s, m_b_s, m_w_proj_a, m_w_proj_b, m_w_proj_c, m_w_out, m_norm2, m_w_up, m_conv_w, m_conv_b, m_w_down, v_norm1, v_w_in, v_q_norm, v_k_norm, v_sinks, v_w_pool, v_pool_scale, v_sgu_v_norm, v_w_s, v_b_s, v_w_proj_a, v_w_proj_b, v_w_proj_c, v_w_out, v_norm2, v_w_up, v_conv_w, v_conv_b, v_w_down):
    w = dict(norm1=norm1, w_in=w_in, q_norm=q_norm, k_norm=k_norm, sinks=sinks, w_pool=w_pool, pool_scale=pool_scale,
             sgu_v_norm=sgu_v_norm, w_s=w_s, b_s=b_s, w_proj_a=w_proj_a, w_proj_b=w_proj_b, w_proj_c=w_proj_c,
             w_out=w_out, norm2=norm2, w_up=w_up, conv_w=conv_w, conv_b=conv_b, w_down=w_down)
    m = dict(norm1=m_norm1, w_in=m_w_in, q_norm=m_q_norm, k_norm=m_k_norm, sinks=m_sinks, w_pool=m_w_pool,
             pool_scale=m_pool_scale, sgu_v_norm=m_sgu_v_norm, w_s=m_w_s, b_s=m_b_s, w_proj_a=m_w_proj_a,
             w_proj_b=m_w_proj_b, w_proj_c=m_w_proj_c, w_out=m_w_out, norm2=m_norm2, w_up=m_w_up, conv_w=m_conv_w,
             conv_b=m_conv_b, w_down=m_w_down)
    v = dict(norm1=v_norm1, w_in=v_w_in, q_norm=v_q_norm, k_norm=v_k_norm, sinks=v_sinks, w_pool=v_w_pool,
             pool_scale=v_pool_scale, sgu_v_norm=v_sgu_v_norm, w_s=v_w_s, b_s=v_b_s, w_proj_a=v_w_proj_a,
             w_proj_b=v_w_proj_b, w_proj_c=v_w_proj_c, w_out=v_w_out, norm2=v_norm2, w_up=v_w_up, conv_w=v_conv_w,
             conv_b=v_conv_b, w_down=v_w_down)
    chip = 2 * lax.axis_index("x") + lax.axis_index("y")
    core = lax.axis_index("c")

    pos = jnp.stack([chip, core, 2 * chip + core]).astype(jnp.int32)
    comm = _Comm(w, pos)

    cos, sin = _rope_tables(positions[0])
    sp = {n: w[n] for n in SMALL if n != 'conv_w'}
    loss_row, dx, gs = _local_step(x[0], loss_target[0], cos, sin, sp, comm)

    delta, new_m, new_v, grad_out = {}, {}, {}, {}

    def adamw_big(names, grads):
        for n in names:
            shp = w[n].shape
            two_d = lambda a: a.reshape(shp[0] * shp[1], shp[2])
            d, nm, nv, g = _adamw(two_d(w[n]), two_d(grads[n]), two_d(m[n]), two_d(v[n]),
                                  tr=_row_tile(shp[0] * shp[1], 256), name=f"adamw_{n}", copy_g=True)
            delta[n], new_m[n], new_v[n], grad_out[n] = d.reshape(shp), nm.reshape(shp), nv.reshape(shp), g.reshape(shp)

    small_shapes = [gs[n].shape for n in SMALL] + [(1,)]
    small_pack = _pack([gs[n] for n in SMALL] + [loss_row[0, :1]])
    small = _Gather([_cast_place(small_pack, pos, F32, slots=N_DEV, which=2, name="small_place")], "small_gather",
                    all_devices=True)
    toks = comm.tail(REDUCE_TAIL_A[:1], (dx, small.token))
    comm.tail(REDUCE_TAIL_A[1:], (dx, *toks))
    comm.tail(REDUCE_TAIL_B, dx)
    adamw_big(GROUP_F, comm.shards())
    red = _sum_slots(small.wait(new_v[GROUP_F[-1]])[0], tr=small_pack.shape[0], name="small_sum")
    *small_grads, loss = _unpack(red, small_shapes)
    g_small = dict(zip(SMALL, small_grads))
    comm.tail(REDUCE_TAIL_C, red)
    grads = comm.shards()
    grads.update(g_small)
    shard_cols = conv_w.shape[2]
    grads['conv_w'] = lax.dynamic_slice_in_dim(g_small['conv_w'], chip * shard_cols, shard_cols, axis=2)

    adamw_big(GROUP_M, grads)
    shapes = [w[n].shape for n in SMALL]
    packs = [_pack([src[n] for n in SMALL]) for src in (w, grads, m, v)]
    d, nm, nv = _adamw(*packs, tr=packs[0].shape[0], name="adamw_small")
    for dst, src in ((delta, d), (new_m, nm), (new_v, nv)):
        dst.update(zip(SMALL, _unpack(src, shapes)))

    grads.update(grad_out)
    return (loss[0], dx[None], *[grads[n] for n in WEIGHTS], *[delta[n] for n in WEIGHTS],
            *[new_m[n] for n in WEIGHTS], *[new_v[n] for n in WEIGHTS])
```

```python
import functools
import math

import numpy as np
import jax
import jax.numpy as jnp
from jax import lax
from jax.experimental import pallas as pl
from jax.experimental.pallas import tpu as pltpu

F32 = jnp.float32
MXU_DTYPE = jnp.bfloat16
COMM_DTYPE = jnp.bfloat16

D_MODEL = 1024
DEPTH = 2
HEAD_DIM = 64
POOL_WINDOWS = (2, 4, 8, 16)
POOL_WIDTH = 256
N_Q_HEADS = 8
ATTN_BLOCK = 128
ATTN_WIDTH = 512
KV_WIDTH = 128
CHUNK = 128
SGU_WIDTH = 256
IN_COLS = 4608
GATE_COL0 = 1536
D_FF = 2816
ROPE_THETA = 10000.0
EPS = 1e-6
ADAM_LR, ADAM_B1, ADAM_B2, ADAM_EPS, ADAM_WD, ADAM_STEP = 0.001, 0.9, 0.999, 1e-08, 0.01, 10

N_CHIPS = 4
N_DEV = 8
VMEM_LIMIT_BYTES = 56 * 1024 * 1024
NEG_BIG = -1e30
MESH = pl.DeviceIdType.MESH
ANY = pl.BlockSpec(memory_space=pl.ANY)

SDS = jax.ShapeDtypeStruct


def _cp(*sem):
    return pltpu.CompilerParams(dimension_semantics=sem, vmem_limit_bytes=VMEM_LIMIT_BYTES)


def _dot(a, b, dims=((1,), (0,))):
    return lax.dot_general(a.astype(MXU_DTYPE), b.astype(MXU_DTYPE), (dims, ((), ())),
                           preferred_element_type=F32)


NT = ((1,), (1,))
TN = ((0,), (0,))


def _split_dot(x, m):
    hi = x.astype(MXU_DTYPE)
    lo = (x - hi.astype(F32)).astype(MXU_DTYPE)
    return _dot(hi, m) + _dot(lo, m)


def _seg_matrix(width, seg):
    idx = np.arange(width) // seg
    return jnp.asarray((idx[:, None] == idx[None, :]).astype(np.float32), dtype=MXU_DTYPE)


def _lane(shape):
    return lax.broadcasted_iota(jnp.int32, shape, len(shape) - 1)


def _row(shape):
    return lax.broadcasted_iota(jnp.int32, shape, 0)


def _full(shape):
    nd = len(shape)
    return pl.BlockSpec(shape, lambda *_: (0,) * nd)


def _gelu(x):
    k = math.sqrt(2.0 / math.pi)
    th = jnp.tanh(k * (x + 0.044715 * (x * x * x)))
    return 0.5 * x * (1.0 + th)


def _gelu_and_grad(x):
    k = math.sqrt(2.0 / math.pi)
    x2 = x * x
    th = jnp.tanh(k * (x + 0.044715 * (x2 * x)))
    g = 0.5 * x * (1.0 + th)
    dg = 0.5 * (1.0 + th) + 0.5 * x * (1.0 - th * th) * (k * (1.0 + 3.0 * 0.044715 * x2))
    return g, dg


def _sigmoid(x):
    return 0.5 * jnp.tanh(0.5 * x) + 0.5


def _swap_halves(x):
    w = x.shape[-1]
    first = (_lane(x.shape) % HEAD_DIM) < (HEAD_DIM // 2)
    return jnp.where(first, pltpu.roll(x, w - HEAD_DIM // 2, 1), pltpu.roll(x, HEAD_DIM // 2, 1))


def _tile_lanes(x, reps):
    return x if reps == 1 else jnp.concatenate([x] * reps, axis=1)


def _fold_lanes(x, period):
    w = x.shape[-1]
    while w > period:
        w //= 2
        x = x + pltpu.roll(x, w, 1)
    return x


def _mm(a, b, *, mode, tm, tn, tk, out_dtype=F32, add=None, name,
        a_lead=None, b_lead=None, b_sharded=False, out_into=None,
        b_koff=0, out_joff=0, out_n=None, deps=()):
    ash = a.shape[1:] if a_lead is not None else a.shape
    bsh = b.shape[1:] if b_lead is not None else b.shape
    if b_sharded:
        bsh = (b.shape[1], N_CHIPS * b.shape[2])
    if mode == 'nn':
        (M, K), (K2, N) = ash, bsh
    elif mode == 'nt':
        (M, K), (N, K2) = ash, bsh
    else:
        (K, M), (K2, N) = ash, bsh
    assert K == K2 or (mode == 'nt' and K2 > K), (ash, bsh, mode)
    assert M % tm == 0 and N % tn == 0 and K % tk == 0, (M, N, K, tm, tn, tk)
    nk = K // tk
    dims = {'nn': ((1,), (0,)), 'nt': NT, 'tn': TN}[mode]

    def lead(spec_shape, imap, lead_idx):
        if lead_idx is None:
            return pl.BlockSpec(spec_shape, imap)
        return pl.BlockSpec((None,) + spec_shape, lambda i, j, k: (lead_idx,) + imap(i, j, k))

    if mode == 'tn':
        a_spec = lead((tk, tm), lambda i, j, k: (k, i), a_lead)
    else:
        a_spec = lead((tm, tk), lambda i, j, k: (i, k), a_lead)
    if b_sharded:
        per = b.shape[2] // (tk if mode == 'nt' else tn)
        assert per * (tk if mode == 'nt' else tn) == b.shape[2] and mode != 'tn'
        if mode == 'nt':
            b_spec = pl.BlockSpec((None, tn, tk), lambda i, j, k: ((k + b_koff) // per, j, (k + b_koff) % per))
        else:
            b_spec = pl.BlockSpec((None, tk, tn), lambda i, j, k: (j // per, k, j % per))
    elif mode == 'nt':
        b_spec = lead((tn, tk), lambda i, j, k: (j, k + b_koff), b_lead)
    else:
        b_spec = lead((tk, tn), lambda i, j, k: (k, j), b_lead)
    o_spec = pl.BlockSpec((tm, tn), lambda i, j, k: (i, j + out_joff))
    n_out = N if out_n is None else out_n
    in_specs = [a_spec, b_spec]
    operands = [a, b]
    if add is not None:
        in_specs.append(pl.BlockSpec((tm, tn), lambda i, j, k: (i, j)))
        operands.append(add)
    aliases = {}
    if out_into is not None:
        in_specs.append(ANY)
        operands.append(out_into)
        aliases = {len(operands) - 1: 0}
    in_specs += [ANY] * len(deps)
    operands += list(deps)
    has_add = add is not None
    acc_in_out = nk > 1 and out_dtype == F32

    def body(*refs):
        a_ref, b_ref = refs[0], refs[1]
        pos = 2
        add_ref = None
        if has_add:
            add_ref = refs[pos]
            pos += 1
        if out_into is not None:
            pos += 1
        pos += len(deps)
        o_ref = refs[pos]
        acc_ref = refs[pos + 1] if (nk > 1 and not acc_in_out) else None
        p = _dot(a_ref[...], b_ref[...], dims)
        if nk == 1:
            if has_add:
                p = p + add_ref[...]
            o_ref[...] = p.astype(o_ref.dtype)
            return
        k = pl.program_id(2)
        tgt = o_ref if acc_in_out else acc_ref

        @pl.when(k == 0)
        def _():
            tgt[...] = p + add_ref[...] if has_add else p

        @pl.when(k > 0)
        def _():
            tgt[...] += p

        if not acc_in_out:
            @pl.when(k == nk - 1)
            def _():
                o_ref[...] = acc_ref[...].astype(o_ref.dtype)

    out_shape = SDS((M, n_out), out_dtype)
    scratch = [pltpu.VMEM((tm, tn), F32)] if (nk > 1 and not acc_in_out) else []
    return pl.pallas_call(
        body, grid=(M // tm, N // tn, nk), in_specs=in_specs, out_specs=o_spec, out_shape=out_shape,
        scratch_shapes=scratch, input_output_aliases=aliases, name=name,
        compiler_params=_cp("parallel", "parallel", "arbitrary"))(*operands)


def _rms_bwd_rows(xv, g, dh, dres):
    r = lax.rsqrt(jnp.mean(xv * xv, axis=-1, keepdims=True) + EPS)
    xh = xv * r
    gy = dh * g
    dx = r * (gy - xh * jnp.mean(xh * gy, axis=-1, keepdims=True)) + dres
    return dx, jnp.sum(dh * xh, axis=0, keepdims=True)


def _mm_nt_sharded_rms(a, b, x, g, dres, *, tm, name, deps=()):
    a3 = a if a.ndim == 3 else a[None]
    A, M, ka = a3.shape
    S, N, ns = b.shape
    per = S // A
    assert ka == per * ns and M % tm == 0 and N == x.shape[1], (a3.shape, b.shape, x.shape)

    def body(a_ref, b_ref, x_ref, g_ref, dres_ref, dx_ref, dxb_ref, dg_ref):
        acc = None
        for s in range(S):
            lo = (s % per) * ns
            p = _dot(a_ref[s // per, :, lo:lo + ns], b_ref[s], NT)
            acc = p if acc is None else acc + p
        dx, dg = _rms_bwd_rows(x_ref[...], g_ref[...], acc, dres_ref[...])
        dx_ref[...] = dx
        dxb_ref[...] = dx.astype(dxb_ref.dtype)

        @pl.when(pl.program_id(0) == 0)
        def _():
            dg_ref[...] = jnp.zeros_like(dg_ref)
        dg_ref[...] += dg

    rows = pl.BlockSpec((tm, N), lambda i: (i, 0))
    return pl.pallas_call(
        _after(body, 5, deps), grid=(M // tm,),
        in_specs=[pl.BlockSpec((A, tm, ka), lambda i: (0, i, 0)),
                  pl.BlockSpec((S, N, ns), lambda i: (0, 0, 0), pipeline_mode=pl.Buffered(1)),
                  rows, _full((1, N)), rows] + [ANY] * len(deps),
        out_specs=[rows, rows, _full((1, N))],
        out_shape=[SDS((M, N), F32), SDS((M, N), MXU_DTYPE), SDS((1, N), F32)], name=name,
        compiler_params=_cp("arbitrary"))(a3, b, x, g, dres, *deps)


def _norm_mm(x, g, b, *, tm, tn, name, deps=()):
    M, K = x.shape
    S, K2, ns = b.shape
    per = ns // tn
    assert K == K2 and per * tn == ns and M % tm == 0, (x.shape, b.shape)

    def body(x_ref, g_ref, b_ref, o_ref, h_ref):
        @pl.when(pl.program_id(1) == 0)
        def _():
            xv = x_ref[...]
            r = lax.rsqrt(jnp.mean(xv * xv, axis=-1, keepdims=True) + EPS)
            h_ref[...] = (xv * r * g_ref[...]).astype(h_ref.dtype)
        o_ref[...] = _dot(h_ref[...], b_ref[...])

    return pl.pallas_call(
        _after(body, 3, deps), grid=(M // tm, S * per),
        in_specs=[pl.BlockSpec((tm, K), lambda i, j: (i, 0)), _full((1, K)),
                  pl.BlockSpec((None, K, tn), lambda i, j: (j // per, 0, j % per))] + [ANY] * len(deps),
        out_specs=[pl.BlockSpec((tm, tn), lambda i, j: (i, j)), pl.BlockSpec((tm, K), lambda i, j: (i, 0))],
        out_shape=[SDS((M, S * ns), F32), SDS((M, K), MXU_DTYPE)], name=name,
        compiler_params=_cp("parallel", "arbitrary"))(x, g, b, *deps)


def _after(body, n_in, deps):
    nd = len(deps)
    if nd == 0:
        return body
    return lambda *refs: body(*refs[:n_in], *refs[n_in + nd:])


def _down_proj_loss(act, w, x1, target, *, tm, name):
    T, K = act.shape
    D = w.shape[1]

    def body(a_ref, w_ref, x_ref, t_ref, loss_ref, dy_ref, dyb_ref):
        i = pl.program_id(0)
        d = (x_ref[...] + _dot(a_ref[...], w_ref[...])) - t_ref[...]
        dy = d * (1.0 / D)
        dy_ref[...] = dy
        dyb_ref[...] = dy.astype(dyb_ref.dtype)
        part = jnp.sum(jnp.sum(d * d, axis=1, keepdims=True), axis=0, keepdims=True) * (0.5 / D)

        @pl.when(i == 0)
        def _():
            loss_ref[...] = jnp.zeros_like(loss_ref)
        loss_ref[...] += jnp.broadcast_to(part, loss_ref.shape)

    rows = pl.BlockSpec((tm, D), lambda i: (i, 0))
    return pl.pallas_call(
        body, grid=(T // tm,), in_specs=[pl.BlockSpec((tm, K), lambda i: (i, 0)), _full((K, D)), rows, rows],
        out_specs=[_full((1, 128)), rows, rows],
        out_shape=[SDS((1, 128), F32), SDS((T, D), F32), SDS((T, D), MXU_DTYPE)],
        name=name, compiler_params=_cp("arbitrary"))(act, w, x1, target)


def _pool_lane_consts(shape):
    lane = _lane(shape)
    grp = lane // (POOL_WIDTH // 4)
    win = jnp.where(grp == 0, 2, jnp.where(grp == 1, 4, jnp.where(grp == 2, 8, 16)))
    return grp, win


def _pool_select(grp, s2, s4, s8, s16):
    return jnp.where(grp == 0, s2, jnp.where(grp == 1, s4, jnp.where(grp == 2, s8, s16)))


def _pool_diff(xe, row0, tr):
    s2 = xe + pltpu.roll(xe, 1, 0)
    s4 = s2 + pltpu.roll(s2, 2, 0)
    s8 = s4 + pltpu.roll(s4, 4, 0)
    s16 = s8 + pltpu.roll(s8, 8, 0)
    shape = (tr, POOL_WIDTH)
    grp, win = _pool_lane_consts(shape)
    sums = _pool_select(grp, s2[16:], s4[16:], s8[16:], s16[16:])
    t = row0 + _row(shape)
    cnt = jnp.minimum(t + 1, win).astype(F32)
    return sums / cnt - xe[16:]


def _pool_fwd(z, wbd, scale, *, tr, name):
    T = z.shape[0]
    hb = tr // 16

    def body(x_ref, xp_ref, w_ref, s_ref, o_ref):
        i = pl.program_id(0)
        halo = jnp.where(i == 0, 0.0, xp_ref[...])
        diff = _pool_diff(jnp.concatenate([halo, x_ref[...]], axis=0), i * tr, tr)
        o_ref[...] = (_dot(diff, w_ref[...]) * s_ref[...]).astype(o_ref.dtype)

    return pl.pallas_call(
        body, grid=(T // tr,),
        in_specs=[pl.BlockSpec((tr, POOL_WIDTH), lambda i: (i, 0)),
                  pl.BlockSpec((16, POOL_WIDTH), lambda i: (jnp.maximum(i * hb - 1, 0), 0)),
                  _full((POOL_WIDTH, POOL_WIDTH)), _full((1, POOL_WIDTH))],
        out_specs=pl.BlockSpec((tr, POOL_WIDTH), lambda i: (i, 0)),
        out_shape=SDS((T, POOL_WIDTH), MXU_DTYPE), name=name, compiler_params=_cp("parallel"))(z, z, wbd, scale)


def _pool_bwd_tile(i, n, tr, x, xprev, dpa, dpa_next, wbd, scale):
    halo = jnp.where(i == 0, 0.0, xprev)
    diff = _pool_diff(jnp.concatenate([halo, x], axis=0), i * tr, tr)
    mixed = _dot(diff, wbd)
    dscale = jnp.sum(dpa * mixed, axis=0, keepdims=True)
    dnext = jnp.where(i == n - 1, 0.0, dpa_next)
    dmix_e = jnp.concatenate([dpa, dnext], axis=0) * scale
    ddiff_e = _dot(dmix_e, wbd, NT)
    dwbd = _dot(diff, dmix_e[:tr], TN)
    shape = (tr + 16, POOL_WIDTH)
    grp, win = _pool_lane_consts(shape)
    t = i * tr + _row(shape)
    e = ddiff_e / jnp.minimum(t + 1, win).astype(F32)
    nrow = tr + 16
    a2 = e + pltpu.roll(e, nrow - 1, 0)
    a4 = a2 + pltpu.roll(a2, nrow - 2, 0)
    a8 = a4 + pltpu.roll(a4, nrow - 4, 0)
    a16 = a8 + pltpu.roll(a8, nrow - 8, 0)
    dx = _pool_select(grp, a2, a4, a8, a16)[:tr] - ddiff_e[:tr]
    return dx, dwbd, dscale


def _norm_rope(x, g, cos, sin_signed, seg):
    reps = x.shape[1] // 128
    ms = _split_dot(x * x, seg) * (1.0 / HEAD_DIM)
    r = lax.rsqrt(ms + EPS)
    xn = x * r * g
    c, s = _tile_lanes(cos, reps), _tile_lanes(sin_signed, reps)
    return xn * c + _swap_halves(xn) * s


def _norm_rope_bwd(x, g, cos, sin_signed, seg, dout):
    reps = x.shape[1] // 128
    c, s = _tile_lanes(cos, reps), _tile_lanes(sin_signed, reps)
    dxn = dout * c + _swap_halves(dout * s)
    ms = _split_dot(x * x, seg) * (1.0 / HEAD_DIM)
    r = lax.rsqrt(ms + EPS)
    xh = x * r
    gy = dxn * g
    dx = r * (gy - xh * (_split_dot(xh * gy, seg) * (1.0 / HEAD_DIM)))
    dg = jnp.sum(dxn * xh, axis=0, keepdims=True)
    return dx, dg


def _dup_heads(k):
    first = _lane(k.shape) < HEAD_DIM
    kr = pltpu.roll(k, HEAD_DIM, 1)
    return jnp.concatenate([jnp.where(first, k, kr), jnp.where(first, kr, k)], axis=1)


def _qkv_prep(z, cos, sin_signed, gq, gk, seg, *, tr, name):
    T = z.shape[0]

    def body(qa_ref, qb_ref, kv_ref, c_ref, s_ref, gq_ref, gk_ref, seg_ref, q_ref, k_ref, v_ref):
        c, s, seg_m = c_ref[...], s_ref[...], seg_ref[...]
        scale = HEAD_DIM ** -0.5
        qa = _norm_rope(qa_ref[...], gq_ref[...], c, s, seg_m) * scale
        qb = _norm_rope(qb_ref[...], gq_ref[...], c, s, seg_m) * scale
        q_ref[...] = jnp.concatenate([qa, qb], axis=1).astype(q_ref.dtype)
        kv = kv_ref[...]
        k = _norm_rope(kv[:, :KV_WIDTH], gk_ref[...], c, s, seg_m[:128, :128])
        k_ref[...] = _dup_heads(k).astype(k_ref.dtype)
        v_ref[...] = _dup_heads(kv[:, KV_WIDTH:]).astype(v_ref.dtype)

    col = lambda j: pl.BlockSpec((tr, 256), lambda i: (i, j))
    tab = pl.BlockSpec((tr, 128), lambda i: (i, 0))
    return pl.pallas_call(
        body, grid=(T // tr,),
        in_specs=[col(1), col(2), col(3), tab, tab, _full((1, 256)), _full((1, 128)), _full((256, 256))],
        out_specs=[pl.BlockSpec((tr, 512), lambda i: (i, 0)), col(0), col(0)],
        out_shape=[SDS((T, 512), MXU_DTYPE), SDS((T, 256), MXU_DTYPE), SDS((T, 256), MXU_DTYPE)],
        name=name, compiler_params=_cp("parallel"))(z, z, z, cos, sin_signed, gq, gk, seg)


GROUP_HEADS = 4
GROUP_ROWS = GROUP_HEADS * ATTN_BLOCK
ALL_ROWS = N_Q_HEADS * ATTN_BLOCK


def _attn_mask(has_prev):
    qi = _row((ALL_ROWS, 2 * ATTN_BLOCK)) % ATTN_BLOCK
    kj = _lane((ALL_ROWS, 2 * ATTN_BLOCK))
    return (kj > qi) & (kj <= qi + ATTN_BLOCK) & ((kj >= ATTN_BLOCK) | has_prev)


FWD_STEP_BLOCKS = 8
BWD_STEP_BLOCKS = 2


def _band(prev, cur, blk):
    lo = cur[(blk - 1) * ATTN_BLOCK:blk * ATTN_BLOCK] if blk else prev
    return jnp.concatenate([lo, cur[blk * ATTN_BLOCK:(blk + 1) * ATTN_BLOCK]], axis=0)


def _stack_heads(x, g):
    first = _lane((ATTN_BLOCK, 128)) < HEAD_DIM
    parts = []
    for pair in (2 * g, 2 * g + 1):
        x128 = x[:, 128 * pair:128 * (pair + 1)]
        zero = jnp.zeros_like(x128)
        parts += [jnp.where(first, x128, zero), jnp.where(first, zero, x128)]
    return jnp.concatenate(parts, axis=0)


def _unstack_heads(y):
    first = _lane((ATTN_BLOCK, 128)) < HEAD_DIM
    b = ATTN_BLOCK
    return jnp.concatenate([jnp.where(first, y[0:b], y[b:2 * b]), jnp.where(first, y[2 * b:3 * b], y[3 * b:4 * b])],
                           axis=1)


def _sink_col(sk_ref):
    return jnp.concatenate([jnp.broadcast_to(sk_ref[h:h + 1, 0:1], (ATTN_BLOCK, 1)) for h in range(N_Q_HEADS)],
                           axis=0)


def _by_group(a8, b2, dims=((1,), (0,))):
    return jnp.concatenate([_dot(a8[:GROUP_ROWS], b2[:, :128], dims), _dot(a8[GROUP_ROWS:], b2[:, 128:], dims)],
                           axis=0)


def _softmax_exp(q8, k2, mask, sink):
    s = jnp.where(mask, _by_group(q8, k2, NT), NEG_BIG)
    m = jnp.maximum(jnp.max(s, axis=1, keepdims=True), sink)
    p = jnp.exp(s - m)
    ps = jnp.exp(sink - m)
    return p, ps, 1.0 / (jnp.sum(p, axis=1, keepdims=True) + ps)


def _attn_fwd(q, k, v, sinks_b, *, name):
    T = q.shape[0]
    nb = T // ATTN_BLOCK
    STEP_BLOCKS = min(FWD_STEP_BLOCKS, nb)
    STEP_ROWS = STEP_BLOCKS * ATTN_BLOCK

    def body(q_ref, kc_ref, kp_ref, vc_ref, vp_ref, sk_ref, o_ref):
        n = pl.program_id(0)
        kc, kp, vc, vp = kc_ref[...], kp_ref[...], vc_ref[...], vp_ref[...]
        sink = _sink_col(sk_ref)
        for blk in range(STEP_BLOCKS):
            rows = slice(blk * ATTN_BLOCK, (blk + 1) * ATTN_BLOCK)
            mask = _attn_mask((n > 0) if blk == 0 else True)
            k2, v2 = _band(kp, kc, blk), _band(vp, vc, blk)
            qv = q_ref[rows, :]
            q8 = jnp.concatenate([_stack_heads(qv, 0), _stack_heads(qv, 1)], axis=0)
            p, _, inv = _softmax_exp(q8, k2, mask, sink)
            o8 = _by_group(p, v2) * inv
            o_ref[rows, :] = jnp.concatenate([_unstack_heads(o8[:GROUP_ROWS]), _unstack_heads(o8[GROUP_ROWS:])],
                                             axis=1).astype(o_ref.dtype)

    cur = lambda w: pl.BlockSpec((STEP_ROWS, w), lambda n: (n, 0))
    prev = lambda w: pl.BlockSpec((ATTN_BLOCK, w), lambda n: (jnp.maximum(STEP_BLOCKS * n - 1, 0), 0))
    return pl.pallas_call(
        body, grid=(nb // STEP_BLOCKS,),
        in_specs=[cur(512), cur(256), prev(256), cur(256), prev(256), _full((8, 128))],
        out_specs=cur(512), out_shape=SDS((T, 512), MXU_DTYPE), name=name,
        compiler_params=_cp("parallel"))(q, k, k, v, v, sinks_b)


def _attn_bwd(q, k, v, sinks_b, do, *, name, deps=()):
    T = q.shape[0]
    nb = T // ATTN_BLOCK
    STEP_BLOCKS = min(BWD_STEP_BLOCKS, nb)
    STEP_ROWS = STEP_BLOCKS * ATTN_BLOCK

    def body(q_ref, kc_ref, kp_ref, vc_ref, vp_ref, sk_ref, do_ref,
             dq_ref, dkc_ref, dkp_ref, dvc_ref, dvp_ref, dsk_ref):
        n = pl.program_id(0)
        kc, kp, vc, vp = kc_ref[...], kp_ref[...], vc_ref[...], vp_ref[...]
        sink = _sink_col(sk_ref)

        @pl.when(n == 0)
        def _():
            dsk_ref[...] = jnp.zeros_like(dsk_ref)

        for blk in range(STEP_BLOCKS):
            rows = slice(blk * ATTN_BLOCK, (blk + 1) * ATTN_BLOCK)
            mask = _attn_mask((n > 0) if blk == 0 else True)
            k2, v2 = _band(kp, kc, blk), _band(vp, vc, blk)
            qv, dov = q_ref[rows, :], do_ref[rows, :]
            q8 = jnp.concatenate([_stack_heads(qv, 0), _stack_heads(qv, 1)], axis=0)
            do8 = jnp.concatenate([_stack_heads(dov, 0), _stack_heads(dov, 1)], axis=0)
            p, ps, inv = _softmax_exp(q8, k2, mask, sink)
            pn = p * inv
            delta = jnp.sum(do8 * _by_group(pn, v2), axis=1, keepdims=True)
            ds = pn * (_by_group(do8, v2, NT) - delta)
            dq8 = _by_group(ds, k2)
            dq_ref[rows, :] = jnp.concatenate([_unstack_heads(dq8[:GROUP_ROWS]), _unstack_heads(dq8[GROUP_ROWS:])],
                                              axis=1)
            dk = jnp.concatenate([_dot(ds[:GROUP_ROWS], q8[:GROUP_ROWS], TN),
                                  _dot(ds[GROUP_ROWS:], q8[GROUP_ROWS:], TN)], axis=1)
            dv = jnp.concatenate([_dot(pn[:GROUP_ROWS], do8[:GROUP_ROWS], TN),
                                  _dot(pn[GROUP_ROWS:], do8[GROUP_ROWS:], TN)], axis=1)
            wsink = (ps * inv) * delta
            for h in range(N_Q_HEADS):
                dsink = -jnp.sum(wsink[ATTN_BLOCK * h:ATTN_BLOCK * (h + 1)], axis=0, keepdims=True)
                dsk_ref[h:h + 1, :] += jnp.broadcast_to(dsink, (1, 128))
            dkp_ref[rows, :] = dk[:ATTN_BLOCK]
            dkc_ref[rows, :] = dk[ATTN_BLOCK:]
            dvp_ref[rows, :] = dv[:ATTN_BLOCK]
            dvc_ref[rows, :] = dv[ATTN_BLOCK:]

    cur = lambda w: pl.BlockSpec((STEP_ROWS, w), lambda n: (n, 0))
    prev = lambda w: pl.BlockSpec((ATTN_BLOCK, w), lambda n: (jnp.maximum(STEP_BLOCKS * n - 1, 0), 0))
    f = lambda w: SDS((T, w), F32)
    return pl.pallas_call(
        _after(body, 7, deps), grid=(nb // STEP_BLOCKS,),
        in_specs=[cur(512), cur(256), prev(256), cur(256), prev(256), _full((8, 128)), cur(512)] + [ANY] * len(deps),
        out_specs=[cur(512), cur(256), cur(256), cur(256), cur(256), _full((8, 128))],
        out_shape=[f(512), f(256), f(256), f(256), f(256), SDS((8, 128), F32)],
        name=name, compiler_params=_cp("arbitrary"))(q, k, k, v, v, sinks_b, do, *deps)


def _mixer_ab_bwd(z, cos, sin_signed, gq, gk, seg, dq, dkc, dkp, dvc, dvp, dpa, wbd, scale, dz, *, tr, name, deps=()):
    T = z.shape[0]
    n = T // tr
    hb = tr // 16
    ab = tr // ATTN_BLOCK

    def unfold(cur, nxt_tile, nxt_halo, i):
        nxt = jnp.concatenate([nxt_tile[ATTN_BLOCK:], jnp.where(i == n - 1, 0.0, nxt_halo)], axis=0)
        tot = cur + nxt
        first = _lane((tr, 128)) < HEAD_DIM
        a = tot[:, :128]
        b = tot[:, 128:]
        a = a + pltpu.roll(a, HEAD_DIM, 1)
        b = b + pltpu.roll(b, HEAD_DIM, 1)
        return jnp.where(first, a, b)

    def body(xp_ref, xpp_ref, qa_ref, qb_ref, kv_ref, c_ref, s_ref, gq_ref, gk_ref, seg_ref,
             dq_ref, dkc_ref, dkp_ref, dkh_ref, dvc_ref, dvp_ref, dvh_ref, dpa_ref, dpan_ref, w_ref, sc_ref, _dz_in,
             dz_ref, dgq_ref, dgk_ref, dw_ref, dsc_ref):
        i = pl.program_id(0)
        c, s, seg_m = c_ref[...], s_ref[...], seg_ref[...]
        scale_q = HEAD_DIM ** -0.5
        dqv = dq_ref[...] * scale_q
        dxa, dga = _norm_rope_bwd(qa_ref[...], gq_ref[...], c, s, seg_m, dqv[:, :256])
        dxb, dgb = _norm_rope_bwd(qb_ref[...], gq_ref[...], c, s, seg_m, dqv[:, 256:])
        dk = unfold(dkc_ref[...], dkp_ref[...], dkh_ref[...], i)
        dv = unfold(dvc_ref[...], dvp_ref[...], dvh_ref[...], i)
        kv = kv_ref[...]
        dxk, dgk = _norm_rope_bwd(kv[:, :KV_WIDTH], gk_ref[...], c, s, seg_m[:128, :128], dk)
        dxp, dwbd, dscale = _pool_bwd_tile(i, n, tr, xp_ref[...], xpp_ref[...], dpa_ref[...], dpan_ref[...],
                                           w_ref[...], sc_ref[...])
        dz_ref[...] = jnp.concatenate([dxp, dxa, dxb, dxk, dv], axis=1).astype(dz_ref.dtype)

        @pl.when(i == 0)
        def _():
            dgq_ref[...] = jnp.zeros_like(dgq_ref)
            dgk_ref[...] = jnp.zeros_like(dgk_ref)
            dw_ref[...] = jnp.zeros_like(dw_ref)
            dsc_ref[...] = jnp.zeros_like(dsc_ref)
        dgq_ref[...] += _fold_lanes(dga + dgb, HEAD_DIM)
        dgk_ref[...] += _fold_lanes(dgk, HEAD_DIM)
        dw_ref[...] += dwbd
        dsc_ref[...] += dscale

    col = lambda j: pl.BlockSpec((tr, 256), lambda i: (i, j))
    rows = lambda w: pl.BlockSpec((tr, w), lambda i: (i, 0))
    nxt_blk = pl.BlockSpec((ATTN_BLOCK, 256), lambda i: (jnp.minimum((i + 1) * ab, T // ATTN_BLOCK - 1), 0))
    prev16 = pl.BlockSpec((16, 256), lambda i: (jnp.maximum(i * hb - 1, 0), 0))
    next16 = pl.BlockSpec((16, 256), lambda i: (jnp.minimum((i + 1) * hb, T // 16 - 1), 0))
    return pl.pallas_call(
        _after(body, 22, deps), grid=(n,),
        in_specs=[col(0), prev16, col(1), col(2), col(3), rows(128), rows(128),
                  _full((1, 256)), _full((1, 128)), _full((256, 256)),
                  rows(512), rows(256), rows(256), nxt_blk, rows(256), rows(256), nxt_blk,
                  rows(256), next16, _full((256, 256)), _full((1, 256)), ANY] + [ANY] * len(deps),
        out_specs=[rows(1024), _full((1, 256)), _full((1, 128)), _full((256, 256)), _full((1, 256))],
        out_shape=[SDS((T, IN_COLS), MXU_DTYPE), SDS((1, 256), F32), SDS((1, 128), F32),
                   SDS((256, 256), F32), SDS((1, 256), F32)],
        input_output_aliases={21: 0}, name=name, compiler_params=_cp("arbitrary"))(
            z, z, z, z, z, cos, sin_signed, gq, gk, seg, dq, dkc, dkp, dkp, dvc, dvp, dvp, dpa, dpa, wbd, scale, dz,
            *deps)


def _sgu_common(zu, zv, vn, seg):
    u, du = _gelu_and_grad(zu)
    gv, dgv = _gelu_and_grad(zv)
    ms = _split_dot(gv * gv, seg) * (1.0 / HEAD_DIM)
    r = lax.rsqrt(ms + EPS)
    xh = gv * r
    return u, du, dgv, r, xh, xh * vn


def _sgu_fwd(z, wtril, bexp, vn, seg, *, tr, name):
    T = z.shape[0]
    nch = tr // CHUNK

    def body(u_ref, v_ref, w_ref, b_ref, vn_ref, seg_ref, o_ref):
        u, _, _, _, _, vg = _sgu_common(u_ref[...], v_ref[...], vn_ref[...], seg_ref[...])
        grp = _lane((CHUNK, SGU_WIDTH)) // HEAD_DIM
        outs = []
        for ch in range(nch):
            vc = vg[ch * CHUNK:(ch + 1) * CHUNK]
            s = b_ref[...]
            for g in range(4):
                s = s + jnp.where(grp == g, _dot(w_ref[g], vc), 0.0)
            outs.append(u[ch * CHUNK:(ch + 1) * CHUNK] * s)
        o_ref[...] = jnp.concatenate(outs, axis=0).astype(o_ref.dtype)

    col = lambda j: pl.BlockSpec((tr, 256), lambda i: (i, j))
    return pl.pallas_call(
        body, grid=(T // tr,),
        in_specs=[col(4), col(5), _full((4, CHUNK, CHUNK)), _full((CHUNK, 256)), _full((1, 256)), _full((256, 256))],
        out_specs=col(0), out_shape=SDS((T, SGU_WIDTH), MXU_DTYPE), name=name,
        compiler_params=_cp("parallel"))(z, z, wtril, bexp, vn, seg)


def _sgu_bwd(z, wtril, bexp, vn, seg, dsg, dz, *, tr, name):
    T = z.shape[0]
    nch = tr // CHUNK

    def body(u_ref, v_ref, w_ref, b_ref, vn_ref, seg_ref, d_ref, _dz_in, dz_ref, dw_ref, db_ref, dvn_ref):
        i = pl.program_id(0)
        seg_m = seg_ref[...]
        vn_v = vn_ref[...]
        u, du, dgv, r, xh, vg = _sgu_common(u_ref[...], v_ref[...], vn_v, seg_m)
        d = d_ref[...]
        grp = _lane((CHUNK, SGU_WIDTH)) // HEAD_DIM
        tril = _row((CHUNK, CHUNK)) >= _lane((CHUNK, CHUNK))

        @pl.when(i == 0)
        def _():
            dw_ref[...] = jnp.zeros_like(dw_ref)
            db_ref[...] = jnp.zeros_like(db_ref)
            dvn_ref[...] = jnp.zeros_like(dvn_ref)

        dus, dvgs = [], []
        for ch in range(nch):
            sl = slice(ch * CHUNK, (ch + 1) * CHUNK)
            vc = vg[sl]
            s = b_ref[...]
            for g in range(4):
                s = s + jnp.where(grp == g, _dot(w_ref[g], vc), 0.0)
            dus.append(d[sl] * s)
            ds = d[sl] * u[sl]
            db_ref[...] += _split_dot(ds, seg_m)
            dvg = jnp.zeros((CHUNK, SGU_WIDTH), F32)
            for g in range(4):
                dsm = jnp.where(grp == g, ds, 0.0)
                dvg = dvg + jnp.where(grp == g, _dot(w_ref[g], ds, TN), 0.0)
                dw_ref[g] += jnp.where(tril, _dot(dsm, vc, NT), 0.0)
            dvgs.append(dvg)
        dup = jnp.concatenate(dus, axis=0)
        dvg = jnp.concatenate(dvgs, axis=0)
        dvn_ref[...] += _fold_lanes(jnp.sum(dvg * xh, axis=0, keepdims=True), HEAD_DIM)
        gy = dvg * vn_v
        dgvv = r * (gy - xh * (_split_dot(xh * gy, seg_m) * (1.0 / HEAD_DIM)))
        dz_ref[...] = jnp.concatenate([dup * du, dgvv * dgv], axis=1).astype(dz_ref.dtype)

    col = lambda j: pl.BlockSpec((tr, 256), lambda i: (i, j))
    return pl.pallas_call(
        body, grid=(T // tr,),
        in_specs=[col(4), col(5), _full((4, CHUNK, CHUNK)), _full((CHUNK, 256)), _full((1, 256)), _full((256, 256)),
                  col(0), ANY],
        out_specs=[pl.BlockSpec((tr, 512), lambda i: (i, 2)), _full((4, CHUNK, CHUNK)), _full((CHUNK, 256)),
                   _full((1, 256))],
        out_shape=[SDS((T, IN_COLS), MXU_DTYPE), SDS((4, CHUNK, CHUNK), F32), SDS((CHUNK, 256), F32),
                   SDS((1, 256), F32)],
        input_output_aliases={7: 0}, name=name, compiler_params=_cp("arbitrary"))(
            z, z, wtril, bexp, vn, seg, dsg, dz)


def _merge_fwd(pa, at, sg, wa, wb, wc, z, x, w_out, *, tm, tn, name):
    T = pa.shape[0]
    gb = GATE_COL0 // tn
    nb = D_MODEL // tn

    def body(pa_ref, at_ref, sg_ref, wa_ref, wb_ref, wc_ref, g0_ref, g1_ref, g2_ref, x_ref, wo_ref,
             m_ref, y_ref, x1_ref):
        j = pl.program_id(1)
        acc = None
        for idx, (op_ref, w_ref, g_ref) in enumerate(((pa_ref, wa_ref, g0_ref), (at_ref, wb_ref, g1_ref),
                                                      (sg_ref, wc_ref, g2_ref))):
            y = _dot(op_ref[...], w_ref[...])
            y_ref[idx] = y.astype(y_ref.dtype)
            t = _sigmoid(g_ref[...]) * y
            acc = t if acc is None else acc + t
        merged = acc.astype(m_ref.dtype)
        m_ref[...] = merged
        p = _dot(merged, wo_ref[...])

        @pl.when(j == 0)
        def _():
            x1_ref[...] = x_ref[...] + p

        @pl.when(j > 0)
        def _():
            x1_ref[...] += p

    op = lambda w: pl.BlockSpec((tm, w), lambda i, j: (i, 0))
    wt = lambda k: pl.BlockSpec((k, tn), lambda i, j: (0, j))
    gate = lambda b: pl.BlockSpec((tm, tn), lambda i, j: (i, gb + b * nb + j))
    return pl.pallas_call(
        body, grid=(T // tm, nb),
        in_specs=[op(256), op(512), op(256), wt(256), wt(512), wt(256), gate(0), gate(1), gate(2),
                  op(D_MODEL), pl.BlockSpec((tn, D_MODEL), lambda i, j: (j, 0))],
        out_specs=[pl.BlockSpec((tm, tn), lambda i, j: (i, j)), pl.BlockSpec((3, tm, tn), lambda i, j: (0, i, j)),
                   op(D_MODEL)],
        out_shape=[SDS((T, D_MODEL), MXU_DTYPE), SDS((3, T, D_MODEL), MXU_DTYPE), SDS((T, D_MODEL), F32)],
        name=name, compiler_params=_cp("parallel", "arbitrary"))(pa, at, sg, wa, wb, wc, z, z, z, x, w_out)


def _out_dx_merge_bwd(dxb, w_out, y, z, ws, xs, *, tm, tn, name):
    T = dxb.shape[0]
    gb = GATE_COL0 // tn
    nb = D_MODEL // tn
    nr = T // tm
    widths = [w.shape[0] for w in ws]

    def body(dx_ref, w_ref, y_ref, g_ref, *refs):
        w_refs, x_refs = refs[0:3], refs[3:6]
        dz_ref, dx_refs, dw_refs = refs[6], refs[7:10], refs[10:13]
        dm_ref, acc_refs = refs[13], refs[14:17]
        i, b, j = pl.program_id(0), pl.program_id(1), pl.program_id(2)

        @pl.when((b == 0) & (j == 0))
        def _():
            dm = _dot(dx_ref[...], w_ref[...], NT)
            for jj in range(nb):
                dm_ref[jj] = dm[:, jj * tn:(jj + 1) * tn]

        d = dm_ref[j]
        g = _sigmoid(g_ref[...])
        dy = (d * g).astype(MXU_DTYPE)
        dz_ref[...] = (d * y_ref[...].astype(F32) * g * (1.0 - g)).astype(dz_ref.dtype)
        for branch in range(3):
            @pl.when(b == branch)
            def _():
                p = _dot(dy, w_refs[branch][...], NT)
                q = _dot(x_refs[branch][...], dy, TN)

                @pl.when(j == 0)
                def _():
                    dx_refs[branch][...] = p

                @pl.when(j > 0)
                def _():
                    dx_refs[branch][...] += p

                @pl.when(i == 0)
                def _():
                    acc_refs[branch][j] = q

                @pl.when(i > 0)
                def _():
                    acc_refs[branch][j] += q

        @pl.when((i == nr - 1) & (b == 2) & (j == nb - 1))
        def _():
            for branch in range(3):
                for jj in range(nb):
                    dw_refs[branch][:, jj * tn:(jj + 1) * tn] = acc_refs[branch][jj]

    wspec = lambda k: pl.BlockSpec((k, tn), lambda i, b, j: (0, j))
    rows = lambda k: pl.BlockSpec((tm, k), lambda i, b, j: (i, 0))
    return pl.pallas_call(
        body, grid=(nr, 3, nb),
        in_specs=[rows(D_MODEL),
                  pl.BlockSpec((D_MODEL, D_MODEL), lambda i, b, j: (0, 0), pipeline_mode=pl.Buffered(1)),
                  pl.BlockSpec((None, tm, tn), lambda i, b, j: (b, i, j)),
                  pl.BlockSpec((tm, tn), lambda i, b, j: (i, gb + b * nb + j))]
        + [wspec(k) for k in widths] + [rows(k) for k in widths],
        out_specs=[pl.BlockSpec((tm, tn), lambda i, b, j: (i, gb + b * nb + j))]
        + [rows(k) for k in widths] + [_full((k, D_MODEL)) for k in widths],
        out_shape=[SDS((T, IN_COLS), MXU_DTYPE)] + [SDS((T, k), F32) for k in widths]
        + [SDS((k, D_MODEL), F32) for k in widths],
        scratch_shapes=[pltpu.VMEM((nb, tm, tn), F32)] + [pltpu.VMEM((nb, k, tn), F32) for k in widths],
        name=name, compiler_params=_cp("arbitrary", "arbitrary", "arbitrary"))(dxb, w_out, y, z, *ws, *xs)


def _conv3(xe, w, b):
    return (w[0:1] * pltpu.roll(xe, 2, 0) + w[1:2] * pltpu.roll(xe, 1, 0) + w[2:3] * xe)[8:] + b


def _conv_act_fwd(up, cw, cb, *, tr, tc, name):
    T = up.shape[0]
    nc = D_FF // tc
    hb = tr // 8

    def body(ug_ref, ugp_ref, uv_ref, uvp_ref, wg_ref, wv_ref, bg_ref, bv_ref, o_ref):
        i = pl.program_id(1)
        first = i == 0
        cg = _conv3(jnp.concatenate([jnp.where(first, 0.0, ugp_ref[...]), ug_ref[...]], axis=0), wg_ref[...], bg_ref[...])
        cv = _conv3(jnp.concatenate([jnp.where(first, 0.0, uvp_ref[...]), uv_ref[...]], axis=0), wv_ref[...], bv_ref[...])
        o_ref[...] = (cg * _sigmoid(cg) * cv).astype(o_ref.dtype)

    tile = lambda off: pl.BlockSpec((tr, tc), lambda j, i: (i, off + j))
    prev = lambda off: pl.BlockSpec((8, tc), lambda j, i: (jnp.maximum(i * hb - 1, 0), off + j))
    par = lambda rows, off: pl.BlockSpec((rows, tc), lambda j, i: (0, off + j))
    return pl.pallas_call(
        body, grid=(nc, T // tr),
        in_specs=[tile(0), prev(0), tile(nc), prev(nc), par(3, 0), par(3, nc), par(1, 0), par(1, nc)],
        out_specs=pl.BlockSpec((tr, tc), lambda j, i: (i, j)),
        out_shape=SDS((T, D_FF), MXU_DTYPE), name=name,
        compiler_params=_cp("parallel", "parallel"))(up, up, up, up, cw, cw, cb, cb)


def _conv_act_bwd(up, cw, cb, dact, *, tr, tc, name, deps=()):
    T = up.shape[0]
    nc = D_FF // tc
    hb = tr // 8
    nr = T // tr

    def body(ug_ref, ugp_ref, ugn_ref, uv_ref, uvp_ref, uvn_ref, da_ref, dan_ref, wg_ref, wv_ref, bg_ref, bv_ref,
             du_ref, dwg_ref, dwv_ref, dbg_ref, dbv_ref):
        i = pl.program_id(1)
        first, last = i == 0, i == nr - 1
        da = jnp.concatenate([da_ref[...], jnp.where(last, 0.0, dan_ref[...])], axis=0)
        uge = jnp.concatenate([jnp.where(first, 0.0, ugp_ref[...]), ug_ref[...], ugn_ref[...]], axis=0)
        uve = jnp.concatenate([jnp.where(first, 0.0, uvp_ref[...]), uv_ref[...], uvn_ref[...]], axis=0)
        wg, wv = wg_ref[...], wv_ref[...]
        ug1, ug2 = pltpu.roll(uge, 1, 0)[8:], pltpu.roll(uge, 2, 0)[8:]
        uv1, uv2 = pltpu.roll(uve, 1, 0)[8:], pltpu.roll(uve, 2, 0)[8:]
        cg = wg[0:1] * ug2 + wg[1:2] * ug1 + wg[2:3] * uge[8:] + bg_ref[...]
        cv = wv[0:1] * uv2 + wv[1:2] * uv1 + wv[2:3] * uve[8:] + bv_ref[...]
        sg = _sigmoid(cg)
        dcg = da * cv * (sg * (1.0 + cg * (1.0 - sg)))
        dcv = da * (cg * sg)
        nrow = tr + 8

        def back(dc, w):
            return (w[2:3] * dc + w[1:2] * pltpu.roll(dc, nrow - 1, 0) + w[0:1] * pltpu.roll(dc, nrow - 2, 0))[:tr]

        du_ref[0] = back(dcg, wg).astype(du_ref.dtype)
        du_ref[1] = back(dcv, wv).astype(du_ref.dtype)

        def wgrad(dc, u0, u1, u2):
            d = dc[:tr]
            rows = [jnp.sum(d * u2[:tr], axis=0, keepdims=True), jnp.sum(d * u1[:tr], axis=0, keepdims=True),
                    jnp.sum(d * u0[8:8 + tr], axis=0, keepdims=True)]
            return jnp.concatenate(rows, axis=0), jnp.sum(d, axis=0, keepdims=True)

        dwg, dbg = wgrad(dcg, uge, ug1, ug2)
        dwv, dbv = wgrad(dcv, uve, uv1, uv2)

        @pl.when(first)
        def _():
            dwg_ref[...] = jnp.zeros_like(dwg_ref)
            dwv_ref[...] = jnp.zeros_like(dwv_ref)
            dbg_ref[...] = jnp.zeros_like(dbg_ref)
            dbv_ref[...] = jnp.zeros_like(dbv_ref)
        dwg_ref[...] += dwg
        dwv_ref[...] += dwv
        dbg_ref[...] += dbg
        dbv_ref[...] += dbv

    tile = lambda off: pl.BlockSpec((tr, tc), lambda j, i: (i, off + j))
    prev = lambda off: pl.BlockSpec((8, tc), lambda j, i: (jnp.maximum(i * hb - 1, 0), off + j))
    nxt = lambda off: pl.BlockSpec((8, tc), lambda j, i: (jnp.minimum((i + 1) * hb, T // 8 - 1), off + j))
    par = lambda rows, off: pl.BlockSpec((rows, tc), lambda j, i: (0, off + j))
    acc = lambda rows: pl.BlockSpec((rows, tc), lambda j, i: (0, j))
    return pl.pallas_call(
        _after(body, 12, deps), grid=(nc, nr),
        in_specs=[tile(0), prev(0), nxt(0), tile(nc), prev(nc), nxt(nc), tile(0), nxt(0),
                  par(3, 0), par(3, nc), par(1, 0), par(1, nc)] + [ANY] * len(deps),
        out_specs=[pl.BlockSpec((2, tr, tc), lambda j, i: (0, i, j)), acc(3), acc(3), acc(1), acc(1)],
        out_shape=[SDS((2, T, D_FF), MXU_DTYPE), SDS((3, D_FF), F32), SDS((3, D_FF), F32),
                   SDS((1, D_FF), F32), SDS((1, D_FF), F32)],
        name=name, compiler_params=_cp("parallel", "arbitrary"))(
            up, up, up, up, up, up, dact, dact, cw, cw, cb, cb, *deps)


def _row_tile(rows, cap):
    t = min(cap, rows)
    t -= t % 8
    while rows % t:
        t -= 8
    return t


def _adamw(w, g, m, v, *, tr, name, copy_g=False):
    R, C = w.shape
    assert R % tr == 0, (R, tr)

    def body(w_ref, g_ref, m_ref, v_ref, d_ref, nm_ref, nv_ref, *rest):
        gv = g_ref[...]
        mn = ADAM_B1 * m_ref[...] + (1.0 - ADAM_B1) * gv
        vn = ADAM_B2 * v_ref[...] + (1.0 - ADAM_B2) * (gv * gv)
        m_hat = mn / (1.0 - ADAM_B1 ** ADAM_STEP)
        v_hat = vn / (1.0 - ADAM_B2 ** ADAM_STEP)
        d_ref[...] = -ADAM_LR * (m_hat / (jnp.sqrt(v_hat) + ADAM_EPS) + ADAM_WD * w_ref[...])
        nm_ref[...] = mn
        nv_ref[...] = vn
        if copy_g:
            rest[0][...] = gv

    rows = pl.BlockSpec((tr, C), lambda i: (i, 0))
    n_out = 4 if copy_g else 3
    return pl.pallas_call(
        body, grid=(R // tr,), in_specs=[rows] * 4, out_specs=[rows] * n_out,
        out_shape=[SDS((R, C), F32)] * n_out, name=name, compiler_params=_cp("parallel"))(w, g, m, v)


def _sum_slots(r, *, tr, name):
    S, R, C = r.shape
    assert R % tr == 0, (R, tr)

    def body(r_ref, o_ref):
        acc = r_ref[0]
        for s in range(1, S):
            acc = acc + r_ref[s]
        o_ref[...] = acc

    return pl.pallas_call(
        body, grid=(R // tr,), in_specs=[pl.BlockSpec((S, tr, C), lambda i: (0, i, 0))],
        out_specs=pl.BlockSpec((tr, C), lambda i: (i, 0)), out_shape=SDS((R, C), F32),
        name=name, compiler_params=_cp("parallel"))(r)


def _pair_add(g4, h, pos, *, name):
    A, _, r, C = g4.shape
    cs = C if A == N_CHIPS else C // N_CHIPS
    tr = _row_tile(r, 256)
    if A == N_CHIPS:
        g_map, h_map = (lambda t, i, pos: (t, pos[1], i, 0)), (lambda t, i, pos: (t, i, 0))
    else:
        g_map, h_map = (lambda t, i, pos: (0, pos[1], i, t)), (lambda t, i, pos: (0, i, t))

    def body(pos_ref, g_ref, h_ref, o_ref):
        o_ref[...] = (g_ref[...] + h_ref[...]).astype(o_ref.dtype)

    grid_spec = pltpu.PrefetchScalarGridSpec(
        num_scalar_prefetch=1, grid=(N_CHIPS, r // tr),
        in_specs=[pl.BlockSpec((None, None, tr, cs), g_map), pl.BlockSpec((None, tr, cs), h_map)],
        out_specs=pl.BlockSpec((None, tr, cs), lambda t, i, pos: (t, i, 0)))
    return pl.pallas_call(body, grid_spec=grid_spec, out_shape=SDS((N_CHIPS, r, cs), COMM_DTYPE), name=name,
                          compiler_params=_cp("parallel", "parallel"))(pos, g4, h)


def _chip_sum(p, r2, f_into, pos, layer, *, name):
    _, r, cs = p.shape
    tr = _row_tile(r, 256)

    def body(pos_ref, own_ref, r_ref, *rest):
        o_ref = rest[-1]
        o_ref[...] = ((own_ref[...].astype(F32) + r_ref[0].astype(F32)) + r_ref[1].astype(F32)) + r_ref[2].astype(F32)

    in_specs = [pl.BlockSpec((None, tr, cs), lambda i, pos: (pos[0], i, 0)),
                pl.BlockSpec((3, tr, cs), lambda i, pos: (0, i, 0))]
    operands = [pos, p, r2]
    aliases = {}
    if f_into is not None:
        in_specs.append(ANY)
        operands.append(f_into)
        aliases = {3: 0}
    grid_spec = pltpu.PrefetchScalarGridSpec(
        num_scalar_prefetch=1, grid=(r // tr,), in_specs=in_specs,
        out_specs=pl.BlockSpec((None, None, tr, cs), lambda i, pos: (layer, pos[1], i, 0)))
    return pl.pallas_call(body, grid_spec=grid_spec, out_shape=SDS((DEPTH, 2, r, cs), F32), name=name,
                          input_output_aliases=aliases, compiler_params=_cp("parallel"))(*operands)


def _mesh_pos():
    return lax.axis_index("x"), lax.axis_index("y"), lax.axis_index("c")


HBM = pl.BlockSpec(memory_space=pltpu.HBM)
SEM = pl.BlockSpec(memory_space=pltpu.SEMAPHORE)
DATAFLOW = pltpu.SideEffectType.DATAFLOW_SIDE_EFFECTING
CHIP_FLIPS = (2, 1, 3)


def _chip_peers():
    x, y, c = _mesh_pos()
    return 2 * x + y, [(1 - x, y, c), (x, 1 - y, c), (1 - x, 1 - y, c)], (x, y, 1 - c), c


def _split_start(arrays, n_copies, issue, *, name, deps=()):
    k = len(arrays)
    nd = len(deps)

    def body(*refs):
        issue(refs[:k], refs[k + nd], refs[k + nd + 1])
        refs[2 * k + nd + 2][...] = jnp.zeros((8, 128), F32)

    out = pl.pallas_call(
        body, name=name,
        out_shape=(pltpu.SemaphoreType.DMA((n_copies,)), pltpu.SemaphoreType.DMA((n_copies,)),
                   *[pltpu.HBM(a.shape, a.dtype) for a in arrays], SDS((8, 128), F32)),
        in_specs=[HBM] * k + [ANY] * nd, out_specs=(SEM, SEM, *[HBM] * k, pl.BlockSpec(memory_space=pltpu.VMEM)),
        input_output_aliases={i: 2 + i for i in range(k)},
        compiler_params=pltpu.CompilerParams(has_side_effects=DATAFLOW))(
            *[pltpu.with_memory_space_constraint(a, pltpu.HBM) for a in arrays], *deps)
    return (out[0], out[1]), list(out[2:2 + k]), out[2 + k]


def _split_wait(sems, arrays, after, waits, *, name):
    k = len(arrays)
    afters = tuple(after) if isinstance(after, (tuple, list)) else (after,)

    def body(*refs):
        waits(refs[:k], refs[k], refs[k + 1])

    out = pl.pallas_call(
        body, name=name, out_shape=tuple(pltpu.HBM(a.shape, a.dtype) for a in arrays),
        in_specs=[HBM] * k + [SEM, SEM] + [ANY] * len(afters), out_specs=tuple([HBM] * k),
        input_output_aliases={i: i for i in range(k)},
        compiler_params=pltpu.CompilerParams(has_side_effects=DATAFLOW))(*arrays, sems[0], sems[1], *afters)
    return list(out)


def _wait_both(cp):
    cp.wait_send()
    cp.wait_recv()


def _cast_place(shard, pos, dtype, *, name, layer=None, slots=N_CHIPS, which=0):
    R, C = shard.shape[-2:]
    tr = R if R % 8 else _row_tile(R, 256)
    if layer is None:
        in_spec = pl.BlockSpec((tr, C), lambda i, pos: (i, 0))
    else:
        in_spec = pl.BlockSpec((None, tr, C), lambda i, pos: (layer, i, 0))

    def body(pos_ref, x_ref, o_ref):
        o_ref[...] = x_ref[...].astype(o_ref.dtype)

    grid_spec = pltpu.PrefetchScalarGridSpec(
        num_scalar_prefetch=1, grid=(R // tr,), in_specs=[in_spec],
        out_specs=pl.BlockSpec((None, tr, C), lambda i, pos: (pos[which], i, 0)))
    return pl.pallas_call(body, grid_spec=grid_spec, out_shape=SDS((slots, R, C), dtype), name=name,
                          compiler_params=_cp("parallel"))(pos, shard)


def _device_peers():
    x, y, c = _mesh_pos()
    peers = [(x ^ ((f >> 2) & 1), y ^ ((f >> 1) & 1), c ^ (f & 1)) for f in range(1, N_DEV)]
    return 4 * x + 2 * y + c, peers


class _Gather:
    def __init__(self, lands, name, deps=(), all_devices=False):
        n = len(lands)
        self.name = name
        npeer = N_DEV - 1 if all_devices else N_CHIPS - 1

        def copies(refs, ss, rs):
            me, peers = _device_peers() if all_devices else _chip_peers()[:2]
            return [pltpu.make_async_remote_copy(
                src_ref=refs[w].at[me], dst_ref=refs[w].at[me], send_sem=ss.at[npeer * w + p],
                recv_sem=rs.at[npeer * w + p], device_id=peers[p], device_id_type=MESH)
                for w in range(n) for p in range(npeer)]

        def issue(refs, ss, rs):
            for cp in copies(refs, ss, rs):
                cp.start()

        def waits(refs, ss, rs):
            for cp in copies(refs, ss, rs):
                _wait_both(cp)

        self._waits = waits
        self.sems, self.arrays, self.token = _split_start(list(lands), npeer * n, issue, name=name + "_start",
                                                          deps=deps)

    def wait(self, after):
        return _split_wait(self.sems, self.arrays, after, self._waits, name=self.name + "_wait")


def _swap_halves_start(g4s, *, name):
    n = len(g4s)
    lands = [lax.empty((g.shape[0],) + g.shape[2:], g.dtype) for g in g4s]

    def copies(refs, ss, rs):
        _, _, sibling, c = _chip_peers()
        return [pltpu.make_async_remote_copy(
            src_ref=refs[w].at[:, 1 - c], dst_ref=refs[n + w], send_sem=ss.at[w], recv_sem=rs.at[w],
            device_id=sibling, device_id_type=MESH) for w in range(n)]

    def issue(refs, ss, rs):
        for cp in copies(refs, ss, rs):
            cp.start()

    def waits(refs, ss, rs):
        for cp in copies(refs, ss, rs):
            _wait_both(cp)

    sems, arrays, token = _split_start(list(g4s) + lands, n, issue, name=name + "_start")
    return sems, arrays, token, waits


def _scatter_start(parts, *, name, deps=()):
    n = len(parts)
    lands = [lax.empty((3,) + p.shape[1:], p.dtype) for p in parts]

    def copies(refs, ss, rs):
        me, peers, _, _ = _chip_peers()
        return [pltpu.make_async_remote_copy(
            src_ref=refs[w].at[me ^ CHIP_FLIPS[p]], dst_ref=refs[n + w].at[p],
            send_sem=ss.at[3 * w + p], recv_sem=rs.at[3 * w + p], device_id=peers[p], device_id_type=MESH)
            for w in range(n) for p in range(3)]

    def issue(refs, ss, rs):
        for cp in copies(refs, ss, rs):
            cp.start()

    def waits(refs, ss, rs):
        for cp in copies(refs, ss, rs):
            _wait_both(cp)

    sems, arrays, token = _split_start(list(parts) + lands, 3 * n, issue, name=name + "_start", deps=deps)
    return sems, arrays, token, waits


def _pair_share_start(fs, layer, *, name):
    n = len(fs)

    def copies(refs, ss, rs):
        _, _, sibling, c = _chip_peers()
        return [pltpu.make_async_remote_copy(
            src_ref=refs[w].at[layer, c], dst_ref=refs[w].at[layer, c], send_sem=ss.at[w], recv_sem=rs.at[w],
            device_id=sibling, device_id_type=MESH) for w in range(n)]

    def issue(refs, ss, rs):
        for cp in copies(refs, ss, rs):
            cp.start()

    def waits(refs, ss, rs):
        for cp in copies(refs, ss, rs):
            _wait_both(cp)

    sems, arrays, token = _split_start(list(fs), n, issue, name=name + "_start")
    return sems, arrays, token, waits


BIG = ('w_in', 'w_proj_a', 'w_proj_b', 'w_proj_c', 'w_out', 'w_up', 'w_down')
BIG_SHARD_AXIS = {'w_in': 2, 'w_proj_a': 2, 'w_proj_b': 2, 'w_proj_c': 2, 'w_out': 1, 'w_up': 2, 'w_down': 1}
SMALL = ('norm1', 'q_norm', 'k_norm', 'sinks', 'w_pool', 'pool_scale', 'sgu_v_norm', 'w_s', 'b_s', 'norm2',
         'conv_b', 'conv_w')
WEIGHTS = ('norm1', 'w_in', 'q_norm', 'k_norm', 'sinks', 'w_pool', 'pool_scale', 'sgu_v_norm', 'w_s', 'b_s',
           'w_proj_a', 'w_proj_b', 'w_proj_c', 'w_out', 'norm2', 'w_up', 'conv_w', 'conv_b', 'w_down')


def _rope_tables(positions):
    inv_freq = ROPE_THETA ** (-jnp.arange(0, HEAD_DIM, 2, dtype=F32) / HEAD_DIM)
    ang = positions.astype(F32)[:, None] * inv_freq
    cos, sin = jnp.cos(ang), jnp.sin(ang)
    c = jnp.concatenate([cos, cos], axis=1)
    s = jnp.concatenate([-sin, sin], axis=1)
    return jnp.concatenate([c, c], axis=1), jnp.concatenate([s, s], axis=1)


def _block_diag4(w):
    out = jnp.zeros((POOL_WIDTH, POOL_WIDTH), w.dtype)
    for g in range(4):
        out = lax.dynamic_update_slice(out, w[g], (g * HEAD_DIM, g * HEAD_DIM))
    return out


def _local_step(x, target, cos, sin, sp, sched):
    T = x.shape[0]
    tm1 = min(1024, T)
    tm = min(512, T)
    tr = min(1024, T)
    trc = min(512, T)
    tkt = min(1024, T)
    seg = _seg_matrix(256, HEAD_DIM)
    saved = []
    xl = x
    for l in range(DEPTH):
        p = f"l{l}_"
        c = dict(
            g1=sp['norm1'][l][None], g2=sp['norm2'][l][None],
            wbd=_block_diag4(sp['w_pool'][l]).astype(MXU_DTYPE), scale=sp['pool_scale'][l][None],
            gq=jnp.tile(sp['q_norm'][l], 4)[None], gk=jnp.tile(sp['k_norm'][l], 2)[None],
            sinks=jnp.broadcast_to(sp['sinks'][l][:, None], (N_Q_HEADS, 128)),
            wtril=jnp.tril(sp['w_s'][l]).astype(MXU_DTYPE),
            bexp=jnp.repeat(sp['b_s'][l].T, HEAD_DIM, axis=1), vn=jnp.tile(sp['sgu_v_norm'][l], 4)[None],
            cb=sp['conv_b'][l][None])
        c['w_in'] = sched.weight('w_in', l, xl)
        z, h1 = _norm_mm(xl, c['g1'], c['w_in'], tm=tm1, tn=1152, name=p + "in_proj",
                         deps=sched.start_tokens() if l == 0 else ())
        pa = _pool_fwd(z, c['wbd'], c['scale'], tr=tr, name=p + "pool")
        q, k, v = _qkv_prep(z, cos, sin, c['gq'], c['gk'], seg, tr=tr, name=p + "qkv_prep")
        at = _attn_fwd(q, k, v, c['sinks'], name=p + "attn")
        sg = _sgu_fwd(z, c['wtril'], c['bexp'], c['vn'], seg, tr=tr, name=p + "sgu")
        for n in ('w_proj_a', 'w_proj_b', 'w_proj_c', 'w_out'):
            c[n] = sched.weight(n, l, (pa, at, sg))
        merged, y3, x1 = _merge_fwd(pa, at, sg, c['w_proj_a'], c['w_proj_b'], c['w_proj_c'], z, xl, c['w_out'],
                                    tm=tm, tn=512, name=p + "merge_out_proj")
        for n in ('w_up', 'conv_w', 'w_down'):
            c[n] = sched.weight(n, l, x1)
        up, h2 = _norm_mm(x1, c['g2'], c['w_up'], tm=tm1, tn=1408, name=p + "up_proj")
        act = _conv_act_fwd(up, c['conv_w'], c['cb'], tr=trc, tc=1408, name=p + "conv_act")
        saved.append(dict(c, x=xl, h1=h1, z=z, pa=pa, q=q, k=k, v=v, at=at, sg=sg, merged=merged, y3=y3,
                          x1=x1, h2=h2, up=up, act=act))
        if l < DEPTH - 1:
            xl = _mm(act, c['w_down'], mode='nn', add=x1, tm=tm, tn=D_MODEL, tk=D_FF, name=p + "down_proj")
        else:
            loss_row, dx, dxb = _down_proj_loss(act, c['w_down'], x1, target, tm=tm, name=p + "down_proj_loss")

    gs = {n: [None] * DEPTH for n in SMALL}
    for l in reversed(range(DEPTH)):
        p = f"l{l}_b_"
        s = saved[l]
        gb = {}
        dact = _mm(dxb, s['w_down'], mode='nt', tm=tm1, tn=1408, tk=D_MODEL, name=p + "down_dx")
        gb['w_down'] = _mm(s['act'], dxb, mode='tn', tm=1408, tn=D_MODEL, tk=tkt, name=p + "down_dw")
        toks = sched.slot(l, 'down', gb['w_down'])
        dup, dwg, dwv, dbg, dbv = _conv_act_bwd(s['up'], s['conv_w'], s['cb'], dact, tr=min(1024, T), tc=256,
                                                name=p + "conv_act", deps=toks)
        gs['conv_w'][l] = jnp.concatenate([dwg, dwv], axis=1)
        gs['conv_b'][l] = jnp.concatenate([dbg, dbv], axis=1)[0]
        toks = sched.slot(l, 'conv', dup)
        for half in range(2):
            gb['w_up'] = _mm(s['h2'], dup, mode='tn', b_lead=half, tm=D_MODEL, tn=1408, tk=tkt,
                             out_into=gb.get('w_up'), out_joff=2 * half, out_n=2 * D_FF, name=p + f"up_dw{half}",
                             deps=toks if half == 0 else ())
        toks = sched.slot(l, 'ffn', gb['w_up'], gb)
        dx1, dx1b, dg2 = _mm_nt_sharded_rms(dup, s['w_up'], s['x1'], s['g2'], dx, tm=tm,
                                            name=p + "up_dx_rms2", deps=toks)
        gs['norm2'][l] = dg2[0]
        gb['w_out'] = _mm(s['merged'], dx1b, mode='tn', tm=D_MODEL, tn=D_MODEL, tk=tkt, name=p + "out_dw")
        (dz, dpa, dat, dsg, gb['w_proj_a'], gb['w_proj_b'], gb['w_proj_c']) = _out_dx_merge_bwd(
            dx1b, s['w_out'], s['y3'], s['z'], [s['w_proj_a'], s['w_proj_b'], s['w_proj_c']],
            [s['pa'], s['at'], s['sg']], tm=tm1, tn=512, name=p + "out_dx_merge")
        toks = sched.slot(l, 'mid', dz)
        dq, dkc, dkp, dvc, dvp, dsk = _attn_bwd(s['q'], s['k'], s['v'], s['sinks'], dat, name=p + "attn", deps=toks)
        gs['sinks'][l] = dsk[:, 0]
        toks = sched.slot(l, 'attn', dq)
        dz, dgq, dgk, dwbd, dsc = _mixer_ab_bwd(s['z'], cos, sin, s['gq'], s['gk'], seg, dq, dkc, dkp, dvc, dvp,
                                                dpa, s['wbd'], s['scale'], dz, tr=tr, name=p + "qkv_pool", deps=toks)
        gs['q_norm'][l] = dgq[0, :HEAD_DIM]
        gs['k_norm'][l] = dgk[0, :HEAD_DIM]
        gs['w_pool'][l] = jnp.stack([dwbd[g * HEAD_DIM:(g + 1) * HEAD_DIM, g * HEAD_DIM:(g + 1) * HEAD_DIM]
                                     for g in range(4)])
        gs['pool_scale'][l] = dsc[0]
        dz, dws, dbrows, dvn = _sgu_bwd(s['z'], s['wtril'], s['bexp'], s['vn'], seg, dsg, dz, tr=tr, name=p + "sgu")
        gs['w_s'][l] = dws
        gs['b_s'][l] = dbrows[:, ::HEAD_DIM].T
        gs['sgu_v_norm'][l] = dvn[0, :HEAD_DIM]
        gb['w_in'] = _mm(s['h1'], dz, mode='tn', tm=D_MODEL, tn=1152, tk=tkt, name=p + "in_dw")
        toks = sched.slot(l, 'mix', gb['w_in'], gb)
        dx, dxb, dg1 = _mm_nt_sharded_rms(dz, s['w_in'], s['x'], s['g1'], dx1, tm=tm,
                                          name=p + "in_dx_rms1", deps=toks)
        gs['norm1'][l] = dg1[0]
    gs = {n: jnp.stack(v) for n, v in gs.items()}
    return loss_row, dx, gs


GROUP_F = ('w_down', 'w_up')
GROUP_M = ('w_out', 'w_proj_a', 'w_proj_b', 'w_proj_c', 'w_in')
ROW_SHARDED = ('w_out', 'w_down')

REDUCE_PLAN = {
    (1, 'ffn'): (('S1', 'F', 1),),
    (1, 'mid'): (('W1', 'F', 1),),
    (1, 'mix'): (('S1', 'M', 1),),
    (0, 'down'): (('W1', 'M', 1),),
    (0, 'conv'): (('W2', 'F', 1),),
    (0, 'ffn'): (('S1', 'F', 0), ('W3', 'F', 1)),
    (0, 'mid'): (('W1', 'F', 0),),
    (0, 'attn'): (('W2', 'M', 1),),
    (0, 'mix'): (('S1', 'M', 0), ('W3', 'M', 1)),
}
REDUCE_TAIL_A = (('W1', 'M', 0), ('W2', 'F', 0))
REDUCE_TAIL_B = (('W3', 'F', 0),)
REDUCE_TAIL_C = (('W2', 'M', 0), ('W3', 'M', 0))


class _Comm:
    def __init__(self, w, pos):
        self.pos = pos
        groups = {'a': [('w_in', 0)],
                  'b': [(n, 0) for n in ('w_proj_a', 'w_proj_b', 'w_proj_c', 'w_out')],
                  'c': [(n, 0) for n in ('w_up', 'conv_w', 'w_down')],
                  'd': [(n, 1) for n in BIG] + [('conv_w', 1)]}
        self.gathers, self.group_of, self.weights = {}, {}, {}
        self.tokens = []
        for g, ks in groups.items():
            lands = [_cast_place(w[n], pos, F32 if n == 'conv_w' else MXU_DTYPE, layer=l, name=f"gw_place_{n}{l}")
                     for n, l in ks]
            self.gathers[g] = (_Gather(lands, "gw_" + g, deps=self.tokens[-1:]), ks)
            self.tokens.append(self.gathers[g][0].token)
            self.group_of.update({k: g for k in ks})
        self.red = {}
        self.final = {}

    def start_tokens(self):
        return self.tokens[-1:]

    def weight(self, name, layer, after):
        if (name, layer) not in self.weights:
            gather, ks = self.gathers[self.group_of[(name, layer)]]
            for (n, l), full in zip(ks, gather.wait(after)):
                if n == 'conv_w' or n.startswith('w_proj'):
                    full = full.transpose(1, 0, 2).reshape(full.shape[1], -1)
                elif n in ROW_SHARDED:
                    full = full.reshape(-1, full.shape[2])
                self.weights[(n, l)] = full
        return self.weights[(name, layer)]

    def slot(self, layer, slot, after, grads=None):
        tokens = []
        for step, grp, lyr in REDUCE_PLAN.get((layer, slot), ()):
            tok = self._step(step, grp, lyr, after, grads)
            if tok is not None:
                tokens.append(tok)
        return tokens

    def tail(self, steps, after, deps=()):
        toks = (self._step(step, grp, lyr, after, None, deps) for step, grp, lyr in steps)
        return [t for t in toks if t is not None]

    def shards(self):
        return {n: f.reshape(DEPTH, 2 * f.shape[2], f.shape[3]) for n, f in self.final.items()}

    def _step(self, step, grp, layer, after, grads, deps=()):
        names = GROUP_F if grp == 'F' else GROUP_M
        tag = f"{grp.lower()}{layer}"
        st = self.red.setdefault((grp, layer), {})
        n = len(names)
        if step == 'S1':
            g4s = []
            for nm in names:
                g = grads[nm]
                R, C = g.shape
                g4s.append(g.reshape(N_CHIPS, 2, R // (2 * N_CHIPS), C) if nm in ROW_SHARDED
                           else g.reshape(1, 2, R // 2, C))
            st['s1'] = _swap_halves_start(g4s, name="rs1_" + tag)
            return st['s1'][2]
        if step == 'W1':
            sems, arrays, _, waits = st.pop('s1')
            arrays = _split_wait(sems, arrays, after, waits, name=f"rs1_{tag}_wait")
            parts = [_pair_add(arrays[i], arrays[n + i], self.pos, name=f"pair_add_{tag}_{names[i]}")
                     for i in range(n)]
            st['s2'] = _scatter_start(parts, name="rs2_" + tag, deps=deps)
            return st['s2'][2]
        if step == 'W2':
            sems, arrays, _, waits = st.pop('s2')
            arrays = _split_wait(sems, arrays, after, waits, name=f"rs2_{tag}_wait")
            fs = [_chip_sum(arrays[i], arrays[n + i], self.final.get(names[i]), self.pos, layer,
                            name=f"chip_sum_{tag}_{names[i]}") for i in range(n)]
            st['s3'] = _pair_share_start(fs, layer, name="rs3_" + tag)
            return st['s3'][2]
        sems, arrays, _, waits = st.pop('s3')
        self.final.update(zip(names, _split_wait(sems, arrays, after, waits, name=f"rs3_{tag}_wait")))
        return None


def _pack(arrays):
    rows = []
    for a in arrays:
        nel = int(np.prod(a.shape))
        if nel % 1024 == 0:
            rows.append(a.astype(F32).reshape(nel // 128, 128))
        else:
            f = a.reshape(-1).astype(F32)
            rows.append(jnp.pad(f, (0, (-nel) % 1024)).reshape(-1, 128))
    return jnp.concatenate(rows, axis=0)


def _unpack(pack, shapes):
    out, row = [], 0
    for shp in shapes:
        nel = int(np.prod(shp))
        nrow = 8 * -(-nel // 1024)
        part = pack[row:row + nrow]
        out.append(part.reshape(shp) if nel % 1024 == 0 else part.reshape(-1)[:nel].reshape(shp))
        row += nrow
    return out


def kernel(x, positions, norm1, w_in, q_norm, k_norm, sinks, w_pool, pool_scale, sgu_v_norm, w_s, b_s, w_proj_a, w_proj_b, w_proj_c, w_out, norm2, w_up, conv_w, conv_b, w_down, loss_target, m_norm1, m_w_in, m_q_norm, m_k_norm, m_sinks, m_w_pool, m_pool_scale, m_sgu_v_norm, m_w_s, m_b_s, m_w_proj_a, m_w_proj_b, m_w_proj_c, m_w_out, m_norm2, m_w_up, m_conv_w, m_conv_b, m_w_down, v_norm1, v_w_in, v_q_norm, v_k_norm, v_sinks, v_w_pool, v_pool_scale, v_sgu_v_norm, v_w_s, v_b_s, v_w_proj_a, v_w_proj_b, v_w_proj_c, v_w_out, v_norm2, v_w_up, v_conv_w, v_conv_b, v_w_down):
    w = dict(norm1=norm1, w_in=w_in, q_norm=q_norm, k_norm=k_norm, sinks=sinks, w_pool=w_pool, pool_scale=pool_scale,
             sgu_v_norm=sgu_v_norm, w_s=w_s, b_s=b_s, w_proj_a=w_proj_a, w_proj_b=w_proj_b, w_proj_c=w_proj_c,
             w_out=w_out, norm2=norm2, w_up=w_up, conv_w=conv_w, conv_b=conv_b, w_down=w_down)
    m = dict(norm1=m_norm1, w_in=m_w_in, q_norm=m_q_norm, k_norm=m_k_norm, sinks=m_sinks, w_pool=m_w_pool,
             pool_scale=m_pool_scale, sgu_v_norm=m_sgu_v_norm, w_s=m_w_s, b_s=m_b_s, w_proj_a=m_w_proj_a,
             w_proj_b=m_w_proj_b, w_proj_c=m_w_proj_c, w_out=m_w_out, norm2=m_norm2, w_up=m_w_up, conv_w=m_conv_w,
             conv_b=m_conv_b, w_down=m_w_down)
    v = dict(norm1=v_norm1, w_in=v_w_in, q_norm=v_q_norm, k_norm=v_k_norm, sinks=v_sinks, w_pool=v_w_pool,
             pool_scale=v_pool_scale, sgu_v_norm=v_sgu_v_norm, w_s=v_w_s, b_s=v_b_s, w_proj_a=v_w_proj_a,
             w_proj_b=v_w_proj_b, w_proj_c=v_w_proj_c, w_out=v_w_out, norm2=v_norm2, w_up=v_w_up, conv_w=v_conv_w,
             conv_b=v_conv_b, w_down=v_w_down)
    chip = 2 * lax.axis_index("x") + lax.axis_index("y")
    core = lax.axis_index("c")

    pos = jnp.stack([chip, core, 2 * chip + core]).astype(jnp.int32)
    comm = _Comm(w, pos)

    cos, sin = _rope_tables(positions[0])
    sp = {n: w[n] for n in SMALL if n != 'conv_w'}
    loss_row, dx, gs = _local_step(x[0], loss_target[0], cos, sin, sp, comm)

    delta, new_m, new_v, grad_out = {}, {}, {}, {}

    def adamw_big(names, grads):
        for n in names:
            shp = w[n].shape
            two_d = lambda a: a.reshape(shp[0] * shp[1], shp[2])
            d, nm, nv, g = _adamw(two_d(w[n]), two_d(grads[n]), two_d(m[n]), two_d(v[n]),
                                  tr=_row_tile(shp[0] * shp[1], 256), name=f"adamw_{n}", copy_g=True)
            delta[n], new_m[n], new_v[n], grad_out[n] = d.reshape(shp), nm.reshape(shp), nv.reshape(shp), g.reshape(shp)

    small_shapes = [gs[n].shape for n in SMALL] + [(1,)]
    small_pack = _pack([gs[n] for n in SMALL] + [loss_row[0, :1]])
    small = _Gather([_cast_place(small_pack, pos, F32, slots=N_DEV, which=2, name="small_place")], "small_gather",
                    all_devices=True)
    toks = comm.tail(REDUCE_TAIL_A[:1], (dx, small.token))
    comm.tail(REDUCE_TAIL_A[1:], (dx, *toks))
    comm.tail(REDUCE_TAIL_B, dx)
    adamw_big(GROUP_F, comm.shards())
    red = _sum_slots(small.wait(new_v[GROUP_F[-1]])[0], tr=small_pack.shape[0], name="small_sum")
    *small_grads, loss = _unpack(red, small_shapes)
    g_small = dict(zip(SMALL, small_grads))
    comm.tail(REDUCE_TAIL_C, red)
    grads = comm.shards()
    grads.update(g_small)
    shard_cols = conv_w.shape[2]
    grads['conv_w'] = lax.dynamic_slice_in_dim(g_small['conv_w'], chip * shard_cols, shard_cols, axis=2)

    adamw_big(GROUP_M, grads)
    shapes = [w[n].shape for n in SMALL]
    packs = [_pack([src[n] for n in SMALL]) for src in (w, grads, m, v)]
    d, nm, nv = _adamw(*packs, tr=packs[0].shape[0], name="adamw_small")
    for dst, src in ((delta, d), (new_m, nm), (new_v, nv)):
        dst.update(zip(SMALL, _unpack(src, shapes)))

    grads.update(grad_out)
    return (loss[0], dx[None], *[grads[n] for n in WEIGHTS], *[delta[n] for n in WEIGHTS],
            *[new_m[n] for n in WEIGHTS], *[new_v[n] for n in WEIGHTS])
```

```python
import functools
import math

import numpy as np
import jax
import jax.numpy as jnp
from jax import lax
from jax.experimental import pallas as pl
from jax.experimental.pallas import tpu as pltpu

F32 = jnp.float32
MXU_DTYPE = jnp.bfloat16
COMM_DTYPE = jnp.bfloat16

D_MODEL = 1024
DEPTH = 2
HEAD_DIM = 64
POOL_WINDOWS = (2, 4, 8, 16)
POOL_WIDTH = 256
N_Q_HEADS = 8
ATTN_BLOCK = 128
ATTN_WIDTH = 512
KV_WIDTH = 128
CHUNK = 128
SGU_WIDTH = 256
IN_COLS = 4608
GATE_COL0 = 1536
D_FF = 2816
ROPE_THETA = 10000.0
EPS = 1e-6
ADAM_LR, ADAM_B1, ADAM_B2, ADAM_EPS, ADAM_WD, ADAM_STEP = 0.001, 0.9, 0.999, 1e-08, 0.01, 10

N_CHIPS = 4
N_DEV = 8
VMEM_LIMIT_BYTES = 56 * 1024 * 1024
NEG_BIG = -1e30
MESH = pl.DeviceIdType.MESH
ANY = pl.BlockSpec(memory_space=pl.ANY)

SDS = jax.ShapeDtypeStruct


def _cp(*sem):
    return pltpu.CompilerParams(dimension_semantics=sem, vmem_limit_bytes=VMEM_LIMIT_BYTES)


def _dot(a, b, dims=((1,), (0,))):
    return lax.dot_general(a.astype(MXU_DTYPE), b.astype(MXU_DTYPE), (dims, ((), ())),
                           preferred_element_type=F32)


NT = ((1,), (1,))
TN = ((0,), (0,))


def _split_dot(x, m):
    hi = x.astype(MXU_DTYPE)
    lo = (x - hi.astype(F32)).astype(MXU_DTYPE)
    return _dot(hi, m) + _dot(lo, m)


def _seg_matrix(width, seg):
    idx = np.arange(width) // seg
    return jnp.asarray((idx[:, None] == idx[None, :]).astype(np.float32), dtype=MXU_DTYPE)


def _lane(shape):
    return lax.broadcasted_iota(jnp.int32, shape, len(shape) - 1)


def _row(shape):
    return lax.broadcasted_iota(jnp.int32, shape, 0)


def _full(shape):
    nd = len(shape)
    return pl.BlockSpec(shape, lambda *_: (0,) * nd)


def _gelu(x):
    k = math.sqrt(2.0 / math.pi)
    th = jnp.tanh(k * (x + 0.044715 * (x * x * x)))
    return 0.5 * x * (1.0 + th)


def _gelu_and_grad(x):
    k = math.sqrt(2.0 / math.pi)
    x2 = x * x
    th = jnp.tanh(k * (x + 0.044715 * (x2 * x)))
    g = 0.5 * x * (1.0 + th)
    dg = 0.5 * (1.0 + th) + 0.5 * x * (1.0 - th * th) * (k * (1.0 + 3.0 * 0.044715 * x2))
    return g, dg


def _sigmoid(x):
    return 0.5 * jnp.tanh(0.5 * x) + 0.5


def _swap_halves(x):
    w = x.shape[-1]
    first = (_lane(x.shape) % HEAD_DIM) < (HEAD_DIM // 2)
    return jnp.where(first, pltpu.roll(x, w - HEAD_DIM // 2, 1), pltpu.roll(x, HEAD_DIM // 2, 1))


def _tile_lanes(x, reps):
    return x if reps == 1 else jnp.concatenate([x] * reps, axis=1)


def _fold_lanes(x, period):
    w = x.shape[-1]
    while w > period:
        w //= 2
        x = x + pltpu.roll(x, w, 1)
    return x


def _mm(a, b, *, mode, tm, tn, tk, out_dtype=F32, add=None, name,
        a_lead=None, b_lead=None, b_sharded=False, out_into=None,
        b_koff=0, out_joff=0, out_n=None, deps=()):
    ash = a.shape[1:] if a_lead is not None else a.shape
    bsh = b.shape[1:] if b_lead is not None else b.shape
    if b_sharded:
        bsh = (b.shape[1], N_CHIPS * b.shape[2])
    if mode == 'nn':
        (M, K), (K2, N) = ash, bsh
    elif mode == 'nt':
        (M, K), (N, K2) = ash, bsh
    else:
        (K, M), (K2, N) = ash, bsh
    assert K == K2 or (mode == 'nt' and K2 > K), (ash, bsh, mode)
    assert M % tm == 0 and N % tn == 0 and K % tk == 0, (M, N, K, tm, tn, tk)
    nk = K // tk
    dims = {'nn': ((1,), (0,)), 'nt': NT, 'tn': TN}[mode]

    def lead(spec_shape, imap, lead_idx):
        if lead_idx is None:
            return pl.BlockSpec(spec_shape, imap)
        return pl.BlockSpec((None,) + spec_shape, lambda i, j, k: (lead_idx,) + imap(i, j, k))

    if mode == 'tn':
        a_spec = lead((tk, tm), lambda i, j, k: (k, i), a_lead)
    else:
        a_spec = lead((tm, tk), lambda i, j, k: (i, k), a_lead)
    if b_sharded:
        per = b.shape[2] // (tk if mode == 'nt' else tn)
        assert per * (tk if mode == 'nt' else tn) == b.shape[2] and mode != 'tn'
        if mode == 'nt':
            b_spec = pl.BlockSpec((None, tn, tk), lambda i, j, k: ((k + b_koff) // per, j, (k + b_koff) % per))
        else:
            b_spec = pl.BlockSpec((None, tk, tn), lambda i, j, k: (j // per, k, j % per))
    elif mode == 'nt':
        b_spec = lead((tn, tk), lambda i, j, k: (j, k + b_koff), b_lead)
    else:
        b_spec = lead((tk, tn), lambda i, j, k: (k, j), b_lead)
    o_spec = pl.BlockSpec((tm, tn), lambda i, j, k: (i, j + out_joff))
    n_out = N if out_n is None else out_n
    in_specs = [a_spec, b_spec]
    operands = [a, b]
    if add is not None:
        in_specs.append(pl.BlockSpec((tm, tn), lambda i, j, k: (i, j)))
        operands.append(add)
    aliases = {}
    if out_into is not None:
        in_specs.append(ANY)
        operands.append(out_into)
        aliases = {len(operands) - 1: 0}
    in_specs += [ANY] * len(deps)
    operands += list(deps)
    has_add = add is not None
    acc_in_out = nk > 1 and out_dtype == F32

    def body(*refs):
        a_ref, b_ref = refs[0], refs[1]
        pos = 2
        add_ref = None
        if has_add:
            add_ref = refs[pos]
            pos += 1
        if out_into is not None:
            pos += 1
        pos += len(deps)
        o_ref = refs[pos]
        acc_ref = refs[pos + 1] if (nk > 1 and not acc_in_out) else None
        p = _dot(a_ref[...], b_ref[...], dims)
        if nk == 1:
            if has_add:
                p = p + add_ref[...]
            o_ref[...] = p.astype(o_ref.dtype)
            return
        k = pl.program_id(2)
        tgt = o_ref if acc_in_out else acc_ref

        @pl.when(k == 0)
        def _():
            tgt[...] = p + add_ref[...] if has_add else p

        @pl.when(k > 0)
        def _():
            tgt[...] += p

        if not acc_in_out:
            @pl.when(k == nk - 1)
            def _():
                o_ref[...] = acc_ref[...].astype(o_ref.dtype)

    out_shape = SDS((M, n_out), out_dtype)
    scratch = [pltpu.VMEM((tm, tn), F32)] if (nk > 1 and not acc_in_out) else []
    return pl.pallas_call(
        body, grid=(M // tm, N // tn, nk), in_specs=in_specs, out_specs=o_spec, out_shape=out_shape,
        scratch_shapes=scratch, input_output_aliases=aliases, name=name,
        compiler_params=_cp("parallel", "parallel", "arbitrary"))(*operands)


def _rms_bwd_rows(xv, g, dh, dres):
    r = lax.rsqrt(jnp.mean(xv * xv, axis=-1, keepdims=True) + EPS)
    xh = xv * r
    gy = dh * g
    dx = r * (gy - xh * jnp.mean(xh * gy, axis=-1, keepdims=True)) + dres
    return dx, jnp.sum(dh * xh, axis=0, keepdims=True)


def _mm_nt_sharded_rms(a, b, x, g, dres, *, tm, name, deps=()):
    a3 = a if a.ndim == 3 else a[None]
    A, M, ka = a3.shape
    S, N, ns = b.shape
    per = S // A
    assert ka == per * ns and M % tm == 0 and N == x.shape[1], (a3.shape, b.shape, x.shape)

    def body(a_ref, b_ref, x_ref, g_ref, dres_ref, dx_ref, dxb_ref, dg_ref):
        acc = None
        for s in range(S):
            lo = (s % per) * ns
            p = _dot(a_ref[s // per, :, lo:lo + ns], b_ref[s], NT)
            acc = p if acc is None else acc + p
        dx, dg = _rms_bwd_rows(x_ref[...], g_ref[...], acc, dres_ref[...])
        dx_ref[...] = dx
        dxb_ref[...] = dx.astype(dxb_ref.dtype)

        @pl.when(pl.program_id(0) == 0)
        def _():
            dg_ref[...] = jnp.zeros_like(dg_ref)
        dg_ref[...] += dg

    rows = pl.BlockSpec((tm, N), lambda i: (i, 0))
    return pl.pallas_call(
        _after(body, 5, deps), grid=(M // tm,),
        in_specs=[pl.BlockSpec((A, tm, ka), lambda i: (0, i, 0)),
                  pl.BlockSpec((S, N, ns), lambda i: (0, 0, 0), pipeline_mode=pl.Buffered(1)),
                  rows, _full((1, N)), rows] + [ANY] * len(deps),
        out_specs=[rows, rows, _full((1, N))],
        out_shape=[SDS((M, N), F32), SDS((M, N), MXU_DTYPE), SDS((1, N), F32)], name=name,
        compiler_params=_cp("arbitrary"))(a3, b, x, g, dres, *deps)


def _norm_mm(x, g, b, *, tm, tn, name, deps=()):
    M, K = x.shape
    S, K2, ns = b.shape
    per = ns // tn
    assert K == K2 and per * tn == ns and M % tm == 0, (x.shape, b.shape)

    def body(x_ref, g_ref, b_ref, o_ref, h_ref):
        @pl.when(pl.program_id(1) == 0)
        def _():
            xv = x_ref[...]
            r = lax.rsqrt(jnp.mean(xv * xv, axis=-1, keepdims=True) + EPS)
            h_ref[...] = (xv * r * g_ref[...]).astype(h_ref.dtype)
        o_ref[...] = _dot(h_ref[...], b_ref[...])

    return pl.pallas_call(
        _after(body, 3, deps), grid=(M // tm, S * per),
        in_specs=[pl.BlockSpec((tm, K), lambda i, j: (i, 0)), _full((1, K)),
                  pl.BlockSpec((None, K, tn), lambda i, j: (j // per, 0, j % per))] + [ANY] * len(deps),
        out_specs=[pl.BlockSpec((tm, tn), lambda i, j: (i, j)), pl.BlockSpec((tm, K), lambda i, j: (i, 0))],
        out_shape=[SDS((M, S * ns), F32), SDS((M, K), MXU_DTYPE)], name=name,
        compiler_params=_cp("parallel", "arbitrary"))(x, g, b, *deps)


def _after(body, n_in, deps):
    nd = len(deps)
    if nd == 0:
        return body
    return lambda *refs: body(*refs[:n_in], *refs[n_in + nd:])


def _down_proj_loss(act, w, x1, target, *, tm, name):
    T, K = act.shape
    D = w.shape[1]

    def body(a_ref, w_ref, x_ref, t_ref, loss_ref, dy_ref, dyb_ref):
        i = pl.program_id(0)
        d = (x_ref[...] + _dot(a_ref[...], w_ref[...])) - t_ref[...]
        dy = d * (1.0 / D)
        dy_ref[...] = dy
        dyb_ref[...] = dy.astype(dyb_ref.dtype)
        part = jnp.sum(jnp.sum(d * d, axis=1, keepdims=True), axis=0, keepdims=True) * (0.5 / D)

        @pl.when(i == 0)
        def _():
            loss_ref[...] = jnp.zeros_like(loss_ref)
        loss_ref[...] += jnp.broadcast_to(part, loss_ref.shape)

    rows = pl.BlockSpec((tm, D), lambda i: (i, 0))
    return pl.pallas_call(
        body, grid=(T // tm,), in_specs=[pl.BlockSpec((tm, K), lambda i: (i, 0)), _full((K, D)), rows, rows],
        out_specs=[_full((1, 128)), rows, rows],
        out_shape=[SDS((1, 128), F32), SDS((T, D), F32), SDS((T, D), MXU_DTYPE)],
        name=name, compiler_params=_cp("arbitrary"))(act, w, x1, target)


def _pool_lane_consts(shape):
    lane = _lane(shape)
    grp = lane // (POOL_WIDTH // 4)
    win = jnp.where(grp == 0, 2, jnp.where(grp == 1, 4, jnp.where(grp == 2, 8, 16)))
    return grp, win


def _pool_select(grp, s2, s4, s8, s16):
    return jnp.where(grp == 0, s2, jnp.where(grp == 1, s4, jnp.where(grp == 2, s8, s16)))


def _pool_diff(xe, row0, tr):
    s2 = xe + pltpu.roll(xe, 1, 0)
    s4 = s2 + pltpu.roll(s2, 2, 0)
    s8 = s4 + pltpu.roll(s4, 4, 0)
    s16 = s8 + pltpu.roll(s8, 8, 0)
    shape = (tr, POOL_WIDTH)
    grp, win = _pool_lane_consts(shape)
    sums = _pool_select(grp, s2[16:], s4[16:], s8[16:], s16[16:])
    t = row0 + _row(shape)
    cnt = jnp.minimum(t + 1, win).astype(F32)
    return sums / cnt - xe[16:]


def _pool_fwd(z, wbd, scale, *, tr, name):
    T = z.shape[0]
    hb = tr // 16

    def body(x_ref, xp_ref, w_ref, s_ref, o_ref):
        i = pl.program_id(0)
        halo = jnp.where(i == 0, 0.0, xp_ref[...])
        diff = _pool_diff(jnp.concatenate([halo, x_ref[...]], axis=0), i * tr, tr)
        o_ref[...] = (_dot(diff, w_ref[...]) * s_ref[...]).astype(o_ref.dtype)

    return pl.pallas_call(
        body, grid=(T // tr,),
        in_specs=[pl.BlockSpec((tr, POOL_WIDTH), lambda i: (i, 0)),
                  pl.BlockSpec((16, POOL_WIDTH), lambda i: (jnp.maximum(i * hb - 1, 0), 0)),
                  _full((POOL_WIDTH, POOL_WIDTH)), _full((1, POOL_WIDTH))],
        out_specs=pl.BlockSpec((tr, POOL_WIDTH), lambda i: (i, 0)),
        out_shape=SDS((T, POOL_WIDTH), MXU_DTYPE), name=name, compiler_params=_cp("parallel"))(z, z, wbd, scale)


def _pool_bwd_tile(i, n, tr, x, xprev, dpa, dpa_next, wbd, scale):
    halo = jnp.where(i == 0, 0.0, xprev)
    diff = _pool_diff(jnp.concatenate([halo, x], axis=0), i * tr, tr)
    mixed = _dot(diff, wbd)
    dscale = jnp.sum(dpa * mixed, axis=0, keepdims=True)
    dnext = jnp.where(i == n - 1, 0.0, dpa_next)
    dmix_e = jnp.concatenate([dpa, dnext], axis=0) * scale
    ddiff_e = _dot(dmix_e, wbd, NT)
    dwbd = _dot(diff, dmix_e[:tr], TN)
    shape = (tr + 16, POOL_WIDTH)
    grp, win = _pool_lane_consts(shape)
    t = i * tr + _row(shape)
    e = ddiff_e / jnp.minimum(t + 1, win).astype(F32)
    nrow = tr + 16
    a2 = e + pltpu.roll(e, nrow - 1, 0)
    a4 = a2 + pltpu.roll(a2, nrow - 2, 0)
    a8 = a4 + pltpu.roll(a4, nrow - 4, 0)
    a16 = a8 + pltpu.roll(a8, nrow - 8, 0)
    dx = _pool_select(grp, a2, a4, a8, a16)[:tr] - ddiff_e[:tr]
    return dx, dwbd, dscale


def _norm_rope(x, g, cos, sin_signed, seg):
    reps = x.shape[1] // 128
    ms = _split_dot(x * x, seg) * (1.0 / HEAD_DIM)
    r = lax.rsqrt(ms + EPS)
    xn = x * r * g
    c, s = _tile_lanes(cos, reps), _tile_lanes(sin_signed, reps)
    return xn * c + _swap_halves(xn) * s


def _norm_rope_bwd(x, g, cos, sin_signed, seg, dout):
    reps = x.shape[1] // 128
    c, s = _tile_lanes(cos, reps), _tile_lanes(sin_signed, reps)
    dxn = dout * c + _swap_halves(dout * s)
    ms = _split_dot(x * x, seg) * (1.0 / HEAD_DIM)
    r = lax.rsqrt(ms + EPS)
    xh = x * r
    gy = dxn * g
    dx = r * (gy - xh * (_split_dot(xh * gy, seg) * (1.0 / HEAD_DIM)))
    dg = jnp.sum(dxn * xh, axis=0, keepdims=True)
    return dx, dg


def _dup_heads(k):
    first = _lane(k.shape) < HEAD_DIM
    kr = pltpu.roll(k, HEAD_DIM, 1)
    return jnp.concatenate([jnp.where(first, k, kr), jnp.where(first, kr, k)], axis=1)


def _qkv_prep(z, cos, sin_signed, gq, gk, seg, *, tr, name):
    T = z.shape[0]

    def body(qa_ref, qb_ref, kv_ref, c_ref, s_ref, gq_ref, gk_ref, seg_ref, q_ref, k_ref, v_ref):
        c, s, seg_m = c_ref[...], s_ref[...], seg_ref[...]
        scale = HEAD_DIM ** -0.5
        qa = _norm_rope(qa_ref[...], gq_ref[...], c, s, seg_m) * scale
        qb = _norm_rope(qb_ref[...], gq_ref[...], c, s, seg_m) * scale
        q_ref[...] = jnp.concatenate([qa, qb], axis=1).astype(q_ref.dtype)
        kv = kv_ref[...]
        k = _norm_rope(kv[:, :KV_WIDTH], gk_ref[...], c, s, seg_m[:128, :128])
        k_ref[...] = _dup_heads(k).astype(k_ref.dtype)
        v_ref[...] = _dup_heads(kv[:, KV_WIDTH:]).astype(v_ref.dtype)

    col = lambda j: pl.BlockSpec((tr, 256), lambda i: (i, j))
    tab = pl.BlockSpec((tr, 128), lambda i: (i, 0))
    return pl.pallas_call(
        body, grid=(T // tr,),
        in_specs=[col(1), col(2), col(3), tab, tab, _full((1, 256)), _full((1, 128)), _full((256, 256))],
        out_specs=[pl.BlockSpec((tr, 512), lambda i: (i, 0)), col(0), col(0)],
        out_shape=[SDS((T, 512), MXU_DTYPE), SDS((T, 256), MXU_DTYPE), SDS((T, 256), MXU_DTYPE)],
        name=name, compiler_params=_cp("parallel"))(z, z, z, cos, sin_signed, gq, gk, seg)


GROUP_HEADS = 4
GROUP_ROWS = GROUP_HEADS * ATTN_BLOCK
ALL_ROWS = N_Q_HEADS * ATTN_BLOCK


def _attn_mask(has_prev):
    qi = _row((ALL_ROWS, 2 * ATTN_BLOCK)) % ATTN_BLOCK
    kj = _lane((ALL_ROWS, 2 * ATTN_BLOCK))
    return (kj > qi) & (kj <= qi + ATTN_BLOCK) & ((kj >= ATTN_BLOCK) | has_prev)


FWD_STEP_BLOCKS = 8
BWD_STEP_BLOCKS = 2


def _band(prev, cur, blk):
    lo = cur[(blk - 1) * ATTN_BLOCK:blk * ATTN_BLOCK] if blk else prev
    return jnp.concatenate([lo, cur[blk * ATTN_BLOCK:(blk + 1) * ATTN_BLOCK]], axis=0)


def _stack_heads(x, g):
    first = _lane((ATTN_BLOCK, 128)) < HEAD_DIM
    parts = []
    for pair in (2 * g, 2 * g + 1):
        x128 = x[:, 128 * pair:128 * (pair + 1)]
        zero = jnp.zeros_like(x128)
        parts += [jnp.where(first, x128, zero), jnp.where(first, zero, x128)]
    return jnp.concatenate(parts, axis=0)


def _unstack_heads(y):
    first = _lane((ATTN_BLOCK, 128)) < HEAD_DIM
    b = ATTN_BLOCK
    return jnp.concatenate([jnp.where(first, y[0:b], y[b:2 * b]), jnp.where(first, y[2 * b:3 * b], y[3 * b:4 * b])],
                           axis=1)


def _sink_col(sk_ref):
    return jnp.concatenate([jnp.broadcast_to(sk_ref[h:h + 1, 0:1], (ATTN_BLOCK, 1)) for h in range(N_Q_HEADS)],
                           axis=0)


def _by_group(a8, b2, dims=((1,), (0,))):
    return jnp.concatenate([_dot(a8[:GROUP_ROWS], b2[:, :128], dims), _dot(a8[GROUP_ROWS:], b2[:, 128:], dims)],
                           axis=0)


def _softmax_exp(q8, k2, mask, sink):
    s = jnp.where(mask, _by_group(q8, k2, NT), NEG_BIG)
    m = jnp.maximum(jnp.max(s, axis=1, keepdims=True), sink)
    p = jnp.exp(s - m)
    ps = jnp.exp(sink - m)
    return p, ps, 1.0 / (jnp.sum(p, axis=1, keepdims=True) + ps)


def _attn_fwd(q, k, v, sinks_b, *, name):
    T = q.shape[0]
    nb = T // ATTN_BLOCK
    STEP_BLOCKS = min(FWD_STEP_BLOCKS, nb)
    STEP_ROWS = STEP_BLOCKS * ATTN_BLOCK

    def body(q_ref, kc_ref, kp_ref, vc_ref, vp_ref, sk_ref, o_ref):
        n = pl.program_id(0)
        kc, kp, vc, vp = kc_ref[...], kp_ref[...], vc_ref[...], vp_ref[...]
        sink = _sink_col(sk_ref)
        for blk in range(STEP_BLOCKS):
            rows = slice(blk * ATTN_BLOCK, (blk + 1) * ATTN_BLOCK)
            mask = _attn_mask((n > 0) if blk == 0 else True)
            k2, v2 = _band(kp, kc, blk), _band(vp, vc, blk)
            qv = q_ref[rows, :]
            q8 = jnp.concatenate([_stack_heads(qv, 0), _stack_heads(qv, 1)], axis=0)
            p, _, inv = _softmax_exp(q8, k2, mask, sink)
            o8 = _by_group(p, v2) * inv
            o_ref[rows, :] = jnp.concatenate([_unstack_heads(o8[:GROUP_ROWS]), _unstack_heads(o8[GROUP_ROWS:])],
                                             axis=1).astype(o_ref.dtype)

    cur = lambda w: pl.BlockSpec((STEP_ROWS, w), lambda n: (n, 0))
    prev = lambda w: pl.BlockSpec((ATTN_BLOCK, w), lambda n: (jnp.maximum(STEP_BLOCKS * n - 1, 0), 0))
    return pl.pallas_call(
        body, grid=(nb // STEP_BLOCKS,),
        in_specs=[cur(512), cur(256), prev(256), cur(256), prev(256), _full((8, 128))],
        out_specs=cur(512), out_shape=SDS((T, 512), MXU_DTYPE), name=name,
        compiler_params=_cp("parallel"))(q, k, k, v, v, sinks_b)


def _attn_bwd(q, k, v, sinks_b, do, *, name, deps=()):
    T = q.shape[0]
    nb = T // ATTN_BLOCK
    STEP_BLOCKS = min(BWD_STEP_BLOCKS, nb)
    STEP_ROWS = STEP_BLOCKS * ATTN_BLOCK

    def body(q_ref, kc_ref, kp_ref, vc_ref, vp_ref, sk_ref, do_ref,
             dq_ref, dkc_ref, dkp_ref, dvc_ref, dvp_ref, dsk_ref):
        n = pl.program_id(0)
        kc, kp, vc, vp = kc_ref[...], kp_ref[...], vc_ref[...], vp_ref[...]
        sink = _sink_col(sk_ref)

        @pl.when(n == 0)
        def _():
            dsk_ref[...] = jnp.zeros_like(dsk_ref)

        for blk in range(STEP_BLOCKS):
            rows = slice(blk * ATTN_BLOCK, (blk + 1) * ATTN_BLOCK)
            mask = _attn_mask((n > 0) if blk == 0 else True)
            k2, v2 = _band(kp, kc, blk), _band(vp, vc, blk)
            qv, dov = q_ref[rows, :], do_ref[rows, :]
            q8 = jnp.concatenate([_stack_heads(qv, 0), _stack_heads(qv, 1)], axis=0)
            do8 = jnp.concatenate([_stack_heads(dov, 0), _stack_heads(dov, 1)], axis=0)
            p, ps, inv = _softmax_exp(q8, k2, mask, sink)
            pn = p * inv
            delta = jnp.sum(do8 * _by_group(pn, v2), axis=1, keepdims=True)
            ds = pn * (_by_group(do8, v2, NT) - delta)
            dq8 = _by_group(ds, k2)
            dq_ref[rows, :] = jnp.concatenate([_unstack_heads(dq8[:GROUP_ROWS]), _unstack_heads(dq8[GROUP_ROWS:])],
                                              axis=1)
            dk = jnp.concatenate([_dot(ds[:GROUP_ROWS], q8[:GROUP_ROWS], TN),
                                  _dot(ds[GROUP_ROWS:], q8[GROUP_ROWS:], TN)], axis=1)
            dv = jnp.concatenate([_dot(pn[:GROUP_ROWS], do8[:GROUP_ROWS], TN),
                                  _dot(pn[GROUP_ROWS:], do8[GROUP_ROWS:], TN)], axis=1)
            wsink = (ps * inv) * delta
            for h in range(N_Q_HEADS):
                dsink = -jnp.sum(wsink[ATTN_BLOCK * h:ATTN_BLOCK * (h + 1)], axis=0, keepdims=True)
                dsk_ref[h:h + 1, :] += jnp.broadcast_to(dsink, (1, 128))
            dkp_ref[rows, :] = dk[:ATTN_BLOCK]
            dkc_ref[rows, :] = dk[ATTN_BLOCK:]
            dvp_ref[rows, :] = dv[:ATTN_BLOCK]
            dvc_ref[rows, :] = dv[ATTN_BLOCK:]

    cur = lambda w: pl.BlockSpec((STEP_ROWS, w), lambda n: (n, 0))
    prev = lambda w: pl.BlockSpec((ATTN_BLOCK, w), lambda n: (jnp.maximum(STEP_BLOCKS * n - 1, 0), 0))
    f = lambda w: SDS((T, w), F32)
    return pl.pallas_call(
        _after(body, 7, deps), grid=(nb // STEP_BLOCKS,),
        in_specs=[cur(512), cur(256), prev(256), cur(256), prev(256), _full((8, 128)), cur(512)] + [ANY] * len(deps),
        out_specs=[cur(512), cur(256), cur(256), cur(256), cur(256), _full((8, 128))],
        out_shape=[f(512), f(256), f(256), f(256), f(256), SDS((8, 128), F32)],
        name=name, compiler_params=_cp("arbitrary"))(q, k, k, v, v, sinks_b, do, *deps)


def _mixer_ab_bwd(z, cos, sin_signed, gq, gk, seg, dq, dkc, dkp, dvc, dvp, dpa, wbd, scale, dz, *, tr, name, deps=()):
    T = z.shape[0]
    n = T // tr
    hb = tr // 16
    ab = tr // ATTN_BLOCK

    def unfold(cur, nxt_tile, nxt_halo, i):
        nxt = jnp.concatenate([nxt_tile[ATTN_BLOCK:], jnp.where(i == n - 1, 0.0, nxt_halo)], axis=0)
        tot = cur + nxt
        first = _lane((tr, 128)) < HEAD_DIM
        a = tot[:, :128]
        b = tot[:, 128:]
        a = a + pltpu.roll(a, HEAD_DIM, 1)
        b = b + pltpu.roll(b, HEAD_DIM, 1)
        return jnp.where(first, a, b)

    def body(xp_ref, xpp_ref, qa_ref, qb_ref, kv_ref, c_ref, s_ref, gq_ref, gk_ref, seg_ref,
             dq_ref, dkc_ref, dkp_ref, dkh_ref, dvc_ref, dvp_ref, dvh_ref, dpa_ref, dpan_ref, w_ref, sc_ref, _dz_in,
             dz_ref, dgq_ref, dgk_ref, dw_ref, dsc_ref):
        i = pl.program_id(0)
        c, s, seg_m = c_ref[...], s_ref[...], seg_ref[...]
        scale_q = HEAD_DIM ** -0.5
        dqv = dq_ref[...] * scale_q
        dxa, dga = _norm_rope_bwd(qa_ref[...], gq_ref[...], c, s, seg_m, dqv[:, :256])
        dxb, dgb = _norm_rope_bwd(qb_ref[...], gq_ref[...], c, s, seg_m, dqv[:, 256:])
        dk = unfold(dkc_ref[...], dkp_ref[...], dkh_ref[...], i)
        dv = unfold(dvc_ref[...], dvp_ref[...], dvh_ref[...], i)
        kv = kv_ref[...]
        dxk, dgk = _norm_rope_bwd(kv[:, :KV_WIDTH], gk_ref[...], c, s, seg_m[:128, :128], dk)
        dxp, dwbd, dscale = _pool_bwd_tile(i, n, tr, xp_ref[...], xpp_ref[...], dpa_ref[...], dpan_ref[...],
                                           w_ref[...], sc_ref[...])
        dz_ref[...] = jnp.concatenate([dxp, dxa, dxb, dxk, dv], axis=1).astype(dz_ref.dtype)

        @pl.when(i == 0)
        def _():
            dgq_ref[...] = jnp.zeros_like(dgq_ref)
            dgk_ref[...] = jnp.zeros_like(dgk_ref)
            dw_ref[...] = jnp.zeros_like(dw_ref)
            dsc_ref[...] = jnp.zeros_like(dsc_ref)
        dgq_ref[...] += _fold_lanes(dga + dgb, HEAD_DIM)
        dgk_ref[...] += _fold_lanes(dgk, HEAD_DIM)
        dw_ref[...] += dwbd
        dsc_ref[...] += dscale

    col = lambda j: pl.BlockSpec((tr, 256), lambda i: (i, j))
    rows = lambda w: pl.BlockSpec((tr, w), lambda i: (i, 0))
    nxt_blk = pl.BlockSpec((ATTN_BLOCK, 256), lambda i: (jnp.minimum((i + 1) * ab, T // ATTN_BLOCK - 1), 0))
    prev16 = pl.BlockSpec((16, 256), lambda i: (jnp.maximum(i * hb - 1, 0), 0))
    next16 = pl.BlockSpec((16, 256), lambda i: (jnp.minimum((i + 1) * hb, T // 16 - 1), 0))
    return pl.pallas_call(
        _after(body, 22, deps), grid=(n,),
        in_specs=[col(0), prev16, col(1), col(2), col(3), rows(128), rows(128),
                  _full((1, 256)), _full((1, 128)), _full((256, 256)),
                  rows(512), rows(256), rows(256), nxt_blk, rows(256), rows(256), nxt_blk,
                  rows(256), next16, _full((256, 256)), _full((1, 256)), ANY] + [ANY] * len(deps),
        out_specs=[rows(1024), _full((1, 256)), _full((1, 128)), _full((256, 256)), _full((1, 256))],
        out_shape=[SDS((T, IN_COLS), MXU_DTYPE), SDS((1, 256), F32), SDS((1, 128), F32),
                   SDS((256, 256), F32), SDS((1, 256), F32)],
        input_output_aliases={21: 0}, name=name, compiler_params=_cp("arbitrary"))(
            z, z, z, z, z, cos, sin_signed, gq, gk, seg, dq, dkc, dkp, dkp, dvc, dvp, dvp, dpa, dpa, wbd, scale, dz,
            *deps)


def _sgu_common(zu, zv, vn, seg):
    u, du = _gelu_and_grad(zu)
    gv, dgv = _gelu_and_grad(zv)
    ms = _split_dot(gv * gv, seg) * (1.0 / HEAD_DIM)
    r = lax.rsqrt(ms + EPS)
    xh = gv * r
    return u, du, dgv, r, xh, xh * vn


def _sgu_fwd(z, wtril, bexp, vn, seg, *, tr, name):
    T = z.shape[0]
    nch = tr // CHUNK

    def body(u_ref, v_ref, w_ref, b_ref, vn_ref, seg_ref, o_ref):
        u, _, _, _, _, vg = _sgu_common(u_ref[...], v_ref[...], vn_ref[...], seg_ref[...])
        grp = _lane((CHUNK, SGU_WIDTH)) // HEAD_DIM
        outs = []
        for ch in range(nch):
            vc = vg[ch * CHUNK:(ch + 1) * CHUNK]
            s = b_ref[...]
            for g in range(4):
                s = s + jnp.where(grp == g, _dot(w_ref[g], vc), 0.0)
            outs.append(u[ch * CHUNK:(ch + 1) * CHUNK] * s)
        o_ref[...] = jnp.concatenate(outs, axis=0).astype(o_ref.dtype)

    col = lambda j: pl.BlockSpec((tr, 256), lambda i: (i, j))
    return pl.pallas_call(
        body, grid=(T // tr,),
        in_specs=[col(4), col(5), _full((4, CHUNK, CHUNK)), _full((CHUNK, 256)), _full((1, 256)), _full((256, 256))],
        out_specs=col(0), out_shape=SDS((T, SGU_WIDTH), MXU_DTYPE), name=name,
        compiler_params=_cp("parallel"))(z, z, wtril, bexp, vn, seg)


def _sgu_bwd(z, wtril, bexp, vn, seg, dsg, dz, *, tr, name):
    T = z.shape[0]
    nch = tr // CHUNK

    def body(u_ref, v_ref, w_ref, b_ref, vn_ref, seg_ref, d_ref, _dz_in, dz_ref, dw_ref, db_ref, dvn_ref):
        i = pl.program_id(0)
        seg_m = seg_ref[...]
        vn_v = vn_ref[...]
        u, du, dgv, r, xh, vg = _sgu_common(u_ref[...], v_ref[...], vn_v, seg_m)
        d = d_ref[...]
        grp = _lane((CHUNK, SGU_WIDTH)) // HEAD_DIM
        tril = _row((CHUNK, CHUNK)) >= _lane((CHUNK, CHUNK))

        @pl.when(i == 0)
        def _():
            dw_ref[...] = jnp.zeros_like(dw_ref)
            db_ref[...] = jnp.zeros_like(db_ref)
            dvn_ref[...] = jnp.zeros_like(dvn_ref)

        dus, dvgs = [], []
        for ch in range(nch):
            sl = slice(ch * CHUNK, (ch + 1) * CHUNK)
            vc = vg[sl]
            s = b_ref[...]
            for g in range(4):
                s = s + jnp.where(grp == g, _dot(w_ref[g], vc), 0.0)
            dus.append(d[sl] * s)
            ds = d[sl] * u[sl]
            db_ref[...] += _split_dot(ds, seg_m)
            dvg = jnp.zeros((CHUNK, SGU_WIDTH), F32)
            for g in range(4):
                dsm = jnp.where(grp == g, ds, 0.0)
                dvg = dvg + jnp.where(grp == g, _dot(w_ref[g], ds, TN), 0.0)
                dw_ref[g] += jnp.where(tril, _dot(dsm, vc, NT), 0.0)
            dvgs.append(dvg)
        dup = jnp.concatenate(dus, axis=0)
        dvg = jnp.concatenate(dvgs, axis=0)
        dvn_ref[...] += _fold_lanes(jnp.sum(dvg * xh, axis=0, keepdims=True), HEAD_DIM)
        gy = dvg * vn_v
        dgvv = r * (gy - xh * (_split_dot(xh * gy, seg_m) * (1.0 / HEAD_DIM)))
        dz_ref[...] = jnp.concatenate([dup * du, dgvv * dgv], axis=1).astype(dz_ref.dtype)

    col = lambda j: pl.BlockSpec((tr, 256), lambda i: (i, j))
    return pl.pallas_call(
        body, grid=(T // tr,),
        in_specs=[col(4), col(5), _full((4, CHUNK, CHUNK)), _full((CHUNK, 256)), _full((1, 256)), _full((256, 256)),
                  col(0), ANY],
        out_specs=[pl.BlockSpec((tr, 512), lambda i: (i, 2)), _full((4, CHUNK, CHUNK)), _full((CHUNK, 256)),
                   _full((1, 256))],
        out_shape=[SDS((T, IN_COLS), MXU_DTYPE), SDS((4, CHUNK, CHUNK), F32), SDS((CHUNK, 256), F32),
                   SDS((1, 256), F32)],
        input_output_aliases={7: 0}, name=name, compiler_params=_cp("arbitrary"))(
            z, z, wtril, bexp, vn, seg, dsg, dz)


def _merge_fwd(pa, at, sg, wa, wb, wc, z, x, w_out, *, tm, tn, name):
    T = pa.shape[0]
    gb = GATE_COL0 // tn
    nb = D_MODEL // tn

    def body(pa_ref, at_ref, sg_ref, wa_ref, wb_ref, wc_ref, g0_ref, g1_ref, g2_ref, x_ref, wo_ref,
             m_ref, y_ref, x1_ref):
        j = pl.program_id(1)
        acc = None
        for idx, (op_ref, w_ref, g_ref) in enumerate(((pa_ref, wa_ref, g0_ref), (at_ref, wb_ref, g1_ref),
                                                      (sg_ref, wc_ref, g2_ref))):
            y = _dot(op_ref[...], w_ref[...])
            y_ref[idx] = y.astype(y_ref.dtype)
            t = _sigmoid(g_ref[...]) * y
            acc = t if acc is None else acc + t
        merged = acc.astype(m_ref.dtype)
        m_ref[...] = merged
        p = _dot(merged, wo_ref[...])

        @pl.when(j == 0)
        def _():
            x1_ref[...] = x_ref[...] + p

        @pl.when(j > 0)
        def _():
            x1_ref[...] += p

    op = lambda w: pl.BlockSpec((tm, w), lambda i, j: (i, 0))
    wt = lambda k: pl.BlockSpec((k, tn), lambda i, j: (0, j))
    gate = lambda b: pl.BlockSpec((tm, tn), lambda i, j: (i, gb + b * nb + j))
    return pl.pallas_call(
        body, grid=(T // tm, nb),
        in_specs=[op(256), op(512), op(256), wt(256), wt(512), wt(256), gate(0), gate(1), gate(2),
                  op(D_MODEL), pl.BlockSpec((tn, D_MODEL), lambda i, j: (j, 0))],
        out_specs=[pl.BlockSpec((tm, tn), lambda i, j: (i, j)), pl.BlockSpec((3, tm, tn), lambda i, j: (0, i, j)),
                   op(D_MODEL)],
        out_shape=[SDS((T, D_MODEL), MXU_DTYPE), SDS((3, T, D_MODEL), MXU_DTYPE), SDS((T, D_MODEL), F32)],
        name=name, compiler_params=_cp("parallel", "arbitrary"))(pa, at, sg, wa, wb, wc, z, z, z, x, w_out)


def _out_dx_merge_bwd(dxb, w_out, y, z, ws, xs, *, tm, tn, name):
    T = dxb.shape[0]
    gb = GATE_COL0 // tn
    nb = D_MODEL // tn
    nr = T // tm
    widths = [w.shape[0] for w in ws]

    def body(dx_ref, w_ref, y_ref, g_ref, *refs):
        w_refs, x_refs = refs[0:3], refs[3:6]
        dz_ref, dx_refs, dw_refs = refs[6], refs[7:10], refs[10:13]
        dm_ref, acc_refs = refs[13], refs[14:17]
        i, b, j = pl.program_id(0), pl.program_id(1), pl.program_id(2)

        @pl.when((b == 0) & (j == 0))
        def _():
            dm = _dot(dx_ref[...], w_ref[...], NT)
            for jj in range(nb):
                dm_ref[jj] = dm[:, jj * tn:(jj + 1) * tn]

        d = dm_ref[j]
        g = _sigmoid(g_ref[...])
        dy = (d * g).astype(MXU_DTYPE)
        dz_ref[...] = (d * y_ref[...].astype(F32) * g * (1.0 - g)).astype(dz_ref.dtype)
        for branch in range(3):
            @pl.when(b == branch)
            def _():
                p = _dot(dy, w_refs[branch][...], NT)
                q = _dot(x_refs[branch][...], dy, TN)

                @pl.when(j == 0)
                def _():
                    dx_refs[branch][...] = p

                @pl.when(j > 0)
                def _():
                    dx_refs[branch][...] += p

                @pl.when(i == 0)
                def _():
                    acc_refs[branch][j] = q

                @pl.when(i > 0)
                def _():
                    acc_refs[branch][j] += q

        @pl.when((i == nr - 1) & (b == 2) & (j == nb - 1))
        def _():
            for branch in range(3):
                for jj in range(nb):
                    dw_refs[branch][:, jj * tn:(jj + 1) * tn] = acc_refs[branch][jj]

    wspec = lambda k: pl.BlockSpec((k, tn), lambda i, b, j: (0, j))
    rows = lambda k: pl.BlockSpec((tm, k), lambda i, b, j: (i, 0))
    return pl.pallas_call(
        body, grid=(nr, 3, nb),
        in_specs=[rows(D_MODEL),
                  pl.BlockSpec((D_MODEL, D_MODEL), lambda i, b, j: (0, 0), pipeline_mode=pl.Buffered(1)),
                  pl.BlockSpec((None, tm, tn), lambda i, b, j: (b, i, j)),
                  pl.BlockSpec((tm, tn), lambda i, b, j: (i, gb + b * nb + j))]
        + [wspec(k) for k in widths] + [rows(k) for k in widths],
        out_specs=[pl.BlockSpec((tm, tn), lambda i, b, j: (i, gb + b * nb + j))]
        + [rows(k) for k in widths] + [_full((k, D_MODEL)) for k in widths],
        out_shape=[SDS((T, IN_COLS), MXU_DTYPE)] + [SDS((T, k), F32) for k in widths]
        + [SDS((k, D_MODEL), F32) for k in widths],
        scratch_shapes=[pltpu.VMEM((nb, tm, tn), F32)] + [pltpu.VMEM((nb, k, tn), F32) for k in widths],
        name=name, compiler_params=_cp("arbitrary", "arbitrary", "arbitrary"))(dxb, w_out, y, z, *ws, *xs)


def _conv3(xe, w, b):
    return (w[0:1] * pltpu.roll(xe, 2, 0) + w[1:2] * pltpu.roll(xe, 1, 0) + w[2:3] * xe)[8:] + b


def _conv_act_fwd(up, cw, cb, *, tr, tc, name):
    T = up.shape[0]
    nc = D_FF // tc
    hb = tr // 8

    def body(ug_ref, ugp_ref, uv_ref, uvp_ref, wg_ref, wv_ref, bg_ref, bv_ref, o_ref):
        i = pl.program_id(1)
        first = i == 0
        cg = _conv3(jnp.concatenate([jnp.where(first, 0.0, ugp_ref[...]), ug_ref[...]], axis=0), wg_ref[...], bg_ref[...])
        cv = _conv3(jnp.concatenate([jnp.where(first, 0.0, uvp_ref[...]), uv_ref[...]], axis=0), wv_ref[...], bv_ref[...])
        o_ref[...] = (cg * _sigmoid(cg) * cv).astype(o_ref.dtype)

    tile = lambda off: pl.BlockSpec((tr, tc), lambda j, i: (i, off + j))
    prev = lambda off: pl.BlockSpec((8, tc), lambda j, i: (jnp.maximum(i * hb - 1, 0), off + j))
    par = lambda rows, off: pl.BlockSpec((rows, tc), lambda j, i: (0, off + j))
    return pl.pallas_call(
        body, grid=(nc, T // tr),
        in_specs=[tile(0), prev(0), tile(nc), prev(nc), par(3, 0), par(3, nc), par(1, 0), par(1, nc)],
        out_specs=pl.BlockSpec((tr, tc), lambda j, i: (i, j)),
        out_shape=SDS((T, D_FF), MXU_DTYPE), name=name,
        compiler_params=_cp("parallel", "parallel"))(up, up, up, up, cw, cw, cb, cb)


def _conv_act_bwd(up, cw, cb, dact, *, tr, tc, name, deps=()):
    T = up.shape[0]
    nc = D_FF // tc
    hb = tr // 8
    nr = T // tr

    def body(ug_ref, ugp_ref, ugn_ref, uv_ref, uvp_ref, uvn_ref, da_ref, dan_ref, wg_ref, wv_ref, bg_ref, bv_ref,
             du_ref, dwg_ref, dwv_ref, dbg_ref, dbv_ref):
        i = pl.program_id(1)
        first, last = i == 0, i == nr - 1
        da = jnp.concatenate([da_ref[...], jnp.where(last, 0.0, dan_ref[...])], axis=0)
        uge = jnp.concatenate([jnp.where(first, 0.0, ugp_ref[...]), ug_ref[...], ugn_ref[...]], axis=0)
        uve = jnp.concatenate([jnp.where(first, 0.0, uvp_ref[...]), uv_ref[...], uvn_ref[...]], axis=0)
        wg, wv = wg_ref[...], wv_ref[...]
        ug1, ug2 = pltpu.roll(uge, 1, 0)[8:], pltpu.roll(uge, 2, 0)[8:]
        uv1, uv2 = pltpu.roll(uve, 1, 0)[8:], pltpu.roll(uve, 2, 0)[8:]
        cg = wg[0:1] * ug2 + wg[1:2] * ug1 + wg[2:3] * uge[8:] + bg_ref[...]
        cv = wv[0:1] * uv2 + wv[1:2] * uv1 + wv[2:3] * uve[8:] + bv_ref[...]
        sg = _sigmoid(cg)
        dcg = da * cv * (sg * (1.0 + cg * (1.0 - sg)))
        dcv = da * (cg * sg)
        nrow = tr + 8

        def back(dc, w):
            return (w[2:3] * dc + w[1:2] * pltpu.roll(dc, nrow - 1, 0) + w[0:1] * pltpu.roll(dc, nrow - 2, 0))[:tr]

        du_ref[0] = back(dcg, wg).astype(du_ref.dtype)
        du_ref[1] = back(dcv, wv).astype(du_ref.dtype)

        def wgrad(dc, u0, u1, u2):
            d = dc[:tr]
            rows = [jnp.sum(d * u2[:tr], axis=0, keepdims=True), jnp.sum(d * u1[:tr], axis=0, keepdims=True),
                    jnp.sum(d * u0[8:8 + tr], axis=0, keepdims=True)]
            return jnp.concatenate(rows, axis=0), jnp.sum(d, axis=0, keepdims=True)

        dwg, dbg = wgrad(dcg, uge, ug1, ug2)
        dwv, dbv = wgrad(dcv, uve, uv1, uv2)

        @pl.when(first)
        def _():
            dwg_ref[...] = jnp.zeros_like(dwg_ref)
            dwv_ref[...] = jnp.zeros_like(dwv_ref)
            dbg_ref[...] = jnp.zeros_like(dbg_ref)
            dbv_ref[...] = jnp.zeros_like(dbv_ref)
        dwg_ref[...] += dwg
        dwv_ref[...] += dwv
        dbg_ref[...] += dbg
        dbv_ref[...] += dbv

    tile = lambda off: pl.BlockSpec((tr, tc), lambda j, i: (i, off + j))
    prev = lambda off: pl.BlockSpec((8, tc), lambda j, i: (jnp.maximum(i * hb - 1, 0), off + j))
    nxt = lambda off: pl.BlockSpec((8, tc), lambda j, i: (jnp.minimum((i + 1) * hb, T // 8 - 1), off + j))
    par = lambda rows, off: pl.BlockSpec((rows, tc), lambda j, i: (0, off + j))
    acc = lambda rows: pl.BlockSpec((rows, tc), lambda j, i: (0, j))
    return pl.pallas_call(
        _after(body, 12, deps), grid=(nc, nr),
        in_specs=[tile(0), prev(0), nxt(0), tile(nc), prev(nc), nxt(nc), tile(0), nxt(0),
                  par(3, 0), par(3, nc), par(1, 0), par(1, nc)] + [ANY] * len(deps),
        out_specs=[pl.BlockSpec((2, tr, tc), lambda j, i: (0, i, j)), acc(3), acc(3), acc(1), acc(1)],
        out_shape=[SDS((2, T, D_FF), MXU_DTYPE), SDS((3, D_FF), F32), SDS((3, D_FF), F32),
                   SDS((1, D_FF), F32), SDS((1, D_FF), F32)],
        name=name, compiler_params=_cp("parallel", "arbitrary"))(
            up, up, up, up, up, up, dact, dact, cw, cw, cb, cb, *deps)


def _row_tile(rows, cap):
    t = min(cap, rows)
    t -= t % 8
    while rows % t:
        t -= 8
    return t


def _adamw(w, g, m, v, *, tr, name, copy_g=False):
    R, C = w.shape
    assert R % tr == 0, (R, tr)

    def body(w_ref, g_ref, m_ref, v_ref, d_ref, nm_ref, nv_ref, *rest):
        gv = g_ref[...]
        mn = ADAM_B1 * m_ref[...] + (1.0 - ADAM_B1) * gv
        vn = ADAM_B2 * v_ref[...] + (1.0 - ADAM_B2) * (gv * gv)
        m_hat = mn / (1.0 - ADAM_B1 ** ADAM_STEP)
        v_hat = vn / (1.0 - ADAM_B2 ** ADAM_STEP)
        d_ref[...] = -ADAM_LR * (m_hat / (jnp.sqrt(v_hat) + ADAM_EPS) + ADAM_WD * w_ref[...])
        nm_ref[...] = mn
        nv_ref[...] = vn
        if copy_g:
            rest[0][...] = gv

    rows = pl.BlockSpec((tr, C), lambda i: (i, 0))
    n_out = 4 if copy_g else 3
    return pl.pallas_call(
        body, grid=(R // tr,), in_specs=[rows] * 4, out_specs=[rows] * n_out,
        out_shape=[SDS((R, C), F32)] * n_out, name=name, compiler_params=_cp("parallel"))(w, g, m, v)


def _sum_slots(r, *, tr, name):
    S, R, C = r.shape
    assert R % tr == 0, (R, tr)

    def body(r_ref, o_ref):
        acc = r_ref[0]
        for s in range(1, S):
            acc = acc + r_ref[s]
        o_ref[...] = acc

    return pl.pallas_call(
        body, grid=(R // tr,), in_specs=[pl.BlockSpec((S, tr, C), lambda i: (0, i, 0))],
        out_specs=pl.BlockSpec((tr, C), lambda i: (i, 0)), out_shape=SDS((R, C), F32),
        name=name, compiler_params=_cp("parallel"))(r)


def _pair_add(g4, h, pos, *, name):
    A, _, r, C = g4.shape
    cs = C if A == N_CHIPS else C // N_CHIPS
    tr = _row_tile(r, 256)
    if A == N_CHIPS:
        g_map, h_map = (lambda t, i, pos: (t, pos[1], i, 0)), (lambda t, i, pos: (t, i, 0))
    else:
        g_map, h_map = (lambda t, i, pos: (0, pos[1], i, t)), (lambda t, i, pos: (0, i, t))

    def body(pos_ref, g_ref, h_ref, o_ref):
        o_ref[...] = (g_ref[...] + h_ref[...]).astype(o_ref.dtype)

    grid_spec = pltpu.PrefetchScalarGridSpec(
        num_scalar_prefetch=1, grid=(N_CHIPS, r // tr),
        in_specs=[pl.BlockSpec((None, None, tr, cs), g_map), pl.BlockSpec((None, tr, cs), h_map)],
        out_specs=pl.BlockSpec((None, tr, cs), lambda t, i, pos: (t, i, 0)))
    return pl.pallas_call(body, grid_spec=grid_spec, out_shape=SDS((N_CHIPS, r, cs), COMM_DTYPE), name=name,
                          compiler_params=_cp("parallel", "parallel"))(pos, g4, h)


def _chip_sum(p, r2, f_into, pos, layer, *, name):
    _, r, cs = p.shape
    tr = _row_tile(r, 256)

    def body(pos_ref, own_ref, r_ref, *rest):
        o_ref = rest[-1]
        o_ref[...] = ((own_ref[...].astype(F32) + r_ref[0].astype(F32)) + r_ref[1].astype(F32)) + r_ref[2].astype(F32)

    in_specs = [pl.BlockSpec((None, tr, cs), lambda i, pos: (pos[0], i, 0)),
                pl.BlockSpec((3, tr, cs), lambda i, pos: (0, i, 0))]
    operands = [pos, p, r2]
    aliases = {}
    if f_into is not None:
        in_specs.append(ANY)
        operands.append(f_into)
        aliases = {3: 0}
    grid_spec = pltpu.PrefetchScalarGridSpec(
        num_scalar_prefetch=1, grid=(r // tr,), in_specs=in_specs,
        out_specs=pl.BlockSpec((None, None, tr, cs), lambda i, pos: (layer, pos[1], i, 0)))
    return pl.pallas_call(body, grid_spec=grid_spec, out_shape=SDS((DEPTH, 2, r, cs), F32), name=name,
                          input_output_aliases=aliases, compiler_params=_cp("parallel"))(*operands)


def _mesh_pos():
    return lax.axis_index("x"), lax.axis_index("y"), lax.axis_index("c")


HBM = pl.BlockSpec(memory_space=pltpu.HBM)
SEM = pl.BlockSpec(memory_space=pltpu.SEMAPHORE)
DATAFLOW = pltpu.SideEffectType.DATAFLOW_SIDE_EFFECTING
CHIP_FLIPS = (2, 1, 3)


def _chip_peers():
    x, y, c = _mesh_pos()
    return 2 * x + y, [(1 - x, y, c), (x, 1 - y, c), (1 - x, 1 - y, c)], (x, y, 1 - c), c


def _split_start(arrays, n_copies, issue, *, name, deps=()):
    k = len(arrays)
    nd = len(deps)

    def body(*refs):
        issue(refs[:k], refs[k + nd], refs[k + nd + 1])
        refs[2 * k + nd + 2][...] = jnp.zeros((8, 128), F32)

    out = pl.pallas_call(
        body, name=name,
        out_shape=(pltpu.SemaphoreType.DMA((n_copies,)), pltpu.SemaphoreType.DMA((n_copies,)),
                   *[pltpu.HBM(a.shape, a.dtype) for a in arrays], SDS((8, 128), F32)),
        in_specs=[HBM] * k + [ANY] * nd, out_specs=(SEM, SEM, *[HBM] * k, pl.BlockSpec(memory_space=pltpu.VMEM)),
        input_output_aliases={i: 2 + i for i in range(k)},
        compiler_params=pltpu.CompilerParams(has_side_effects=DATAFLOW))(
            *[pltpu.with_memory_space_constraint(a, pltpu.HBM) for a in arrays], *deps)
    return (out[0], out[1]), list(out[2:2 + k]), out[2 + k]


def _split_wait(sems, arrays, after, waits, *, name):
    k = len(arrays)
    afters = tuple(after) if isinstance(after, (tuple, list)) else (after,)

    def body(*refs):
        waits(refs[:k], refs[k], refs[k + 1])

    out = pl.pallas_call(
        body, name=name, out_shape=tuple(pltpu.HBM(a.shape, a.dtype) for a in arrays),
        in_specs=[HBM] * k + [SEM, SEM] + [ANY] * len(afters), out_specs=tuple([HBM] * k),
        input_output_aliases={i: i for i in range(k)},
        compiler_params=pltpu.CompilerParams(has_side_effects=DATAFLOW))(*arrays, sems[0], sems[1], *afters)
    return list(out)


def _wait_both(cp):
    cp.wait_send()
    cp.wait_recv()


def _cast_place(shard, pos, dtype, *, name, layer=None, slots=N_CHIPS, which=0):
    R, C = shard.shape[-2:]
    tr = R if R % 8 else _row_tile(R, 256)
    if layer is None:
        in_spec = pl.BlockSpec((tr, C), lambda i, pos: (i, 0))
    else:
        in_spec = pl.BlockSpec((None, tr, C), lambda i, pos: (layer, i, 0))

    def body(pos_ref, x_ref, o_ref):
        o_ref[...] = x_ref[...].astype(o_ref.dtype)

    grid_spec = pltpu.PrefetchScalarGridSpec(
        num_scalar_prefetch=1, grid=(R // tr,), in_specs=[in_spec],
        out_specs=pl.BlockSpec((None, tr, C), lambda i, pos: (pos[which], i, 0)))
    return pl.pallas_call(body, grid_spec=grid_spec, out_shape=SDS((slots, R, C), dtype), name=name,
                          compiler_params=_cp("parallel"))(pos, shard)


def _device_peers():
    x, y, c = _mesh_pos()
    peers = [(x ^ ((f >> 2) & 1), y ^ ((f >> 1) & 1), c ^ (f & 1)) for f in range(1, N_DEV)]
    return 4 * x + 2 * y + c, peers


class _Gather:
    def __init__(self, lands, name, deps=(), all_devices=False):
        n = len(lands)
        self.name = name
        npeer = N_DEV - 1 if all_devices else N_CHIPS - 1

        def copies(refs, ss, rs):
            me, peers = _device_peers() if all_devices else _chip_peers()[:2]
            return [pltpu.make_async_remote_copy(
                src_ref=refs[w].at[me], dst_ref=refs[w].at[me], send_sem=ss.at[npeer * w + p],
                recv_sem=rs.at[npeer * w + p], device_id=peers[p], device_id_type=MESH)
                for w in range(n) for p in range(npeer)]

        def issue(refs, ss, rs):
            for cp in copies(refs, ss, rs):
                cp.start()

        def waits(refs, ss, rs):
            for cp in copies(refs, ss, rs):
                _wait_both(cp)

        self._waits = waits
        self.sems, self.arrays, self.token = _split_start(list(lands), npeer * n, issue, name=name + "_start",
                                                          deps=deps)

    def wait(self, after):
        return _split_wait(self.sems, self.arrays, after, self._waits, name=self.name + "_wait")


def _swap_halves_start(g4s, *, name):
    n = len(g4s)
    lands = [lax.empty((g.shape[0],) + g.shape[2:], g.dtype) for g in g4s]

    def copies(refs, ss, rs):
        _, _, sibling, c = _chip_peers()
        return [pltpu.make_async_remote_copy(
            src_ref=refs[w].at[:, 1 - c], dst_ref=refs[n + w], send_sem=ss.at[w], recv_sem=rs.at[w],
            device_id=sibling, device_id_type=MESH) for w in range(n)]

    def issue(refs, ss, rs):
        for cp in copies(refs, ss, rs):
            cp.start()

    def waits(refs, ss, rs):
        for cp in copies(refs, ss, rs):
            _wait_both(cp)

    sems, arrays, token = _split_start(list(g4s) + lands, n, issue, name=name + "_start")
    return sems, arrays, token, waits


def _scatter_start(parts, *, name, deps=()):
    n = len(parts)
    lands = [lax.empty((3,) + p.shape[1:], p.dtype) for p in parts]

    def copies(refs, ss, rs):
        me, peers, _, _ = _chip_peers()
        return [pltpu.make_async_remote_copy(
            src_ref=refs[w].at[me ^ CHIP_FLIPS[p]], dst_ref=refs[n + w].at[p],
            send_sem=ss.at[3 * w + p], recv_sem=rs.at[3 * w + p], device_id=peers[p], device_id_type=MESH)
            for w in range(n) for p in range(3)]

    def issue(refs, ss, rs):
        for cp in copies(refs, ss, rs):
            cp.start()

    def waits(refs, ss, rs):
        for cp in copies(refs, ss, rs):
            _wait_both(cp)

    sems, arrays, token = _split_start(list(parts) + lands, 3 * n, issue, name=name + "_start", deps=deps)
    return sems, arrays, token, waits


def _pair_share_start(fs, layer, *, name):
    n = len(fs)

    def copies(refs, ss, rs):
        _, _, sibling, c = _chip_peers()
        return [pltpu.make_async_remote_copy(
            src_ref=refs[w].at[layer, c], dst_ref=refs[w].at[layer, c], send_sem=ss.at[w], recv_sem=rs.at[w],
            device_id=sibling, device_id_type=MESH) for w in range(n)]

    def issue(refs, ss, rs):
        for cp in copies(refs, ss, rs):
            cp.start()

    def waits(refs, ss, rs):
        for cp in copies(refs, ss, rs):
            _wait_both(cp)

    sems, arrays, token = _split_start(list(fs), n, issue, name=name + "_start")
    return sems, arrays, token, waits


BIG = ('w_in', 'w_proj_a', 'w_proj_b', 'w_proj_c', 'w_out', 'w_up', 'w_down')
BIG_SHARD_AXIS = {'w_in': 2, 'w_proj_a': 2, 'w_proj_b': 2, 'w_proj_c': 2, 'w_out': 1, 'w_up': 2, 'w_down': 1}
SMALL = ('norm1', 'q_norm', 'k_norm', 'sinks', 'w_pool', 'pool_scale', 'sgu_v_norm', 'w_s', 'b_s', 'norm2',
         'conv_b', 'conv_w')
WEIGHTS = ('norm1', 'w_in', 'q_norm', 'k_norm', 'sinks', 'w_pool', 'pool_scale', 'sgu_v_norm', 'w_s', 'b_s',
           'w_proj_a', 'w_proj_b', 'w_proj_c', 'w_out', 'norm2', 'w_up', 'conv_w', 'conv_b', 'w_down')


def _rope_tables(positions):
    inv_freq = ROPE_THETA ** (-jnp.arange(0, HEAD_DIM, 2, dtype=F32) / HEAD_DIM)
    ang = positions.astype(F32)[:, None] * inv_freq
    cos, sin = jnp.cos(ang), jnp.sin(ang)
    c = jnp.concatenate([cos, cos], axis=1)
    s = jnp.concatenate([-sin, sin], axis=1)
    return jnp.concatenate([c, c], axis=1), jnp.concatenate([s, s], axis=1)


def _block_diag4(w):
    out = jnp.zeros((POOL_WIDTH, POOL_WIDTH), w.dtype)
    for g in range(4):
        out = lax.dynamic_update_slice(out, w[g], (g * HEAD_DIM, g * HEAD_DIM))
    return out


def _local_step(x, target, cos, sin, sp, sched):
    T = x.shape[0]
    tm1 = min(1024, T)
    tm = min(512, T)
    tr = min(1024, T)
    trc = min(512, T)
    tkt = min(2048, T)
    seg = _seg_matrix(256, HEAD_DIM)
    saved = []
    xl = x
    for l in range(DEPTH):
        p = f"l{l}_"
        c = dict(
            g1=sp['norm1'][l][None], g2=sp['norm2'][l][None],
            wbd=_block_diag4(sp['w_pool'][l]).astype(MXU_DTYPE), scale=sp['pool_scale'][l][None],
            gq=jnp.tile(sp['q_norm'][l], 4)[None], gk=jnp.tile(sp['k_norm'][l], 2)[None],
            sinks=jnp.broadcast_to(sp['sinks'][l][:, None], (N_Q_HEADS, 128)),
            wtril=jnp.tril(sp['w_s'][l]).astype(MXU_DTYPE),
            bexp=jnp.repeat(sp['b_s'][l].T, HEAD_DIM, axis=1), vn=jnp.tile(sp['sgu_v_norm'][l], 4)[None],
            cb=sp['conv_b'][l][None])
        c['w_in'] = sched.weight('w_in', l, xl)
        z, h1 = _norm_mm(xl, c['g1'], c['w_in'], tm=tm1, tn=1152, name=p + "in_proj",
                         deps=sched.start_tokens() if l == 0 else ())
        pa = _pool_fwd(z, c['wbd'], c['scale'], tr=tr, name=p + "pool")
        q, k, v = _qkv_prep(z, cos, sin, c['gq'], c['gk'], seg, tr=tr, name=p + "qkv_prep")
        at = _attn_fwd(q, k, v, c['sinks'], name=p + "attn")
        sg = _sgu_fwd(z, c['wtril'], c['bexp'], c['vn'], seg, tr=tr, name=p + "sgu")
        for n in ('w_proj_a', 'w_proj_b', 'w_proj_c', 'w_out'):
            c[n] = sched.weight(n, l, (pa, at, sg))
        merged, y3, x1 = _merge_fwd(pa, at, sg, c['w_proj_a'], c['w_proj_b'], c['w_proj_c'], z, xl, c['w_out'],
                                    tm=tm1, tn=512, name=p + "merge_out_proj")
        for n in ('w_up', 'conv_w', 'w_down'):
            c[n] = sched.weight(n, l, x1)
        up, h2 = _norm_mm(x1, c['g2'], c['w_up'], tm=tm1, tn=1408, name=p + "up_proj")
        act = _conv_act_fwd(up, c['conv_w'], c['cb'], tr=trc, tc=1408, name=p + "conv_act")
        saved.append(dict(c, x=xl, h1=h1, z=z, pa=pa, q=q, k=k, v=v, at=at, sg=sg, merged=merged, y3=y3,
                          x1=x1, h2=h2, up=up, act=act))
        if l < DEPTH - 1:
            xl = _mm(act, c['w_down'], mode='nn', add=x1, tm=tm, tn=D_MODEL, tk=D_FF, name=p + "down_proj")
        else:
            loss_row, dx, dxb = _down_proj_loss(act, c['w_down'], x1, target, tm=tm, name=p + "down_proj_loss")

    gs = {n: [None] * DEPTH for n in SMALL}
    for l in reversed(range(DEPTH)):
        p = f"l{l}_b_"
        s = saved[l]
        gb = {}
        dact = _mm(dxb, s['w_down'], mode='nt', tm=tm1, tn=1408, tk=D_MODEL, name=p + "down_dx")
        gb['w_down'] = _mm(s['act'], dxb, mode='tn', tm=1408, tn=D_MODEL, tk=tkt, name=p + "down_dw")
        toks = sched.slot(l, 'down', gb['w_down'])
        dup, dwg, dwv, dbg, dbv = _conv_act_bwd(s['up'], s['conv_w'], s['cb'], dact, tr=min(1024, T), tc=256,
                                                name=p + "conv_act", deps=toks)
        gs['conv_w'][l] = jnp.concatenate([dwg, dwv], axis=1)
        gs['conv_b'][l] = jnp.concatenate([dbg, dbv], axis=1)[0]
        toks = sched.slot(l, 'conv', dup)
        for half in range(2):
            gb['w_up'] = _mm(s['h2'], dup, mode='tn', b_lead=half, tm=D_MODEL, tn=1408, tk=tkt,
                             out_into=gb.get('w_up'), out_joff=2 * half, out_n=2 * D_FF, name=p + f"up_dw{half}",
                             deps=toks if half == 0 else ())
        toks = sched.slot(l, 'ffn', gb['w_up'], gb)
        dx1, dx1b, dg2 = _mm_nt_sharded_rms(dup, s['w_up'], s['x1'], s['g2'], dx, tm=tm,
                                            name=p + "up_dx_rms2", deps=toks)
        gs['norm2'][l] = dg2[0]
        gb['w_out'] = _mm(s['merged'], dx1b, mode='tn', tm=D_MODEL, tn=D_MODEL, tk=tkt, name=p + "out_dw")
        (dz, dpa, dat, dsg, gb['w_proj_a'], gb['w_proj_b'], gb['w_proj_c']) = _out_dx_merge_bwd(
            dx1b, s['w_out'], s['y3'], s['z'], [s['w_proj_a'], s['w_proj_b'], s['w_proj_c']],
            [s['pa'], s['at'], s['sg']], tm=tm1, tn=512, name=p + "out_dx_merge")
        toks = sched.slot(l, 'mid', dz)
        dq, dkc, dkp, dvc, dvp, dsk = _attn_bwd(s['q'], s['k'], s['v'], s['sinks'], dat, name=p + "attn", deps=toks)
        gs['sinks'][l] = dsk[:, 0]
        toks = sched.slot(l, 'attn', dq)
        dz, dgq, dgk, dwbd, dsc = _mixer_ab_bwd(s['z'], cos, sin, s['gq'], s['gk'], seg, dq, dkc, dkp, dvc, dvp,
                                                dpa, s['wbd'], s['scale'], dz, tr=tr, name=p + "qkv_pool", deps=toks)
        gs['q_norm'][l] = dgq[0, :HEAD_DIM]
        gs['k_norm'][l] = dgk[0, :HEAD_DIM]
        gs['w_pool'][l] = jnp.stack([dwbd[g * HEAD_DIM:(g + 1) * HEAD_DIM, g * HEAD_DIM:(g + 1) * HEAD_DIM]
                                     for g in range(4)])
        gs['pool_scale'][l] = dsc[0]
        dz, dws, dbrows, dvn = _sgu_bwd(s['z'], s['wtril'], s['bexp'], s['vn'], seg, dsg, dz, tr=tr, name=p + "sgu")
        gs['w_s'][l] = dws
        gs['b_s'][l] = dbrows[:, ::HEAD_DIM].T
        gs['sgu_v_norm'][l] = dvn[0, :HEAD_DIM]
        gb['w_in'] = _mm(s['h1'], dz, mode='tn', tm=D_MODEL, tn=1152, tk=tkt, name=p + "in_dw")
        toks = sched.slot(l, 'mix', gb['w_in'], gb)
        dx, dxb, dg1 = _mm_nt_sharded_rms(dz, s['w_in'], s['x'], s['g1'], dx1, tm=tm,
                                          name=p + "in_dx_rms1", deps=toks)
        gs['norm1'][l] = dg1[0]
    gs = {n: jnp.stack(v) for n, v in gs.items()}
    return loss_row, dx, gs


GROUP_F = ('w_down', 'w_up')
GROUP_M = ('w_out', 'w_proj_a', 'w_proj_b', 'w_proj_c', 'w_in')
ROW_SHARDED = ('w_out', 'w_down')

REDUCE_PLAN = {
    (1, 'ffn'): (('S1', 'F', 1),),
    (1, 'mid'): (('W1', 'F', 1),),
    (1, 'mix'): (('S1', 'M', 1),),
    (0, 'down'): (('W1', 'M', 1),),
    (0, 'conv'): (('W2', 'F', 1),),
    (0, 'ffn'): (('S1', 'F', 0), ('W3', 'F', 1)),
    (0, 'mid'): (('W1', 'F', 0),),
    (0, 'attn'): (('W2', 'M', 1),),
    (0, 'mix'): (('S1', 'M', 0), ('W3', 'M', 1)),
}
REDUCE_TAIL_A = (('W1', 'M', 0), ('W2', 'F', 0))
REDUCE_TAIL_B = (('W3', 'F', 0),)
REDUCE_TAIL_C = (('W2', 'M', 0), ('W3', 'M', 0))


class _Comm:
    def __init__(self, w, pos):
        self.pos = pos
        groups = {'a': [('w_in', 0)],
                  'b': [(n, 0) for n in ('w_proj_a', 'w_proj_b', 'w_proj_c', 'w_out')],
                  'c': [(n, 0) for n in ('w_up', 'conv_w', 'w_down')],
                  'd': [(n, 1) for n in BIG] + [('conv_w', 1)]}
        self.gathers, self.group_of, self.weights = {}, {}, {}
        self.tokens = []
        for g, ks in groups.items():
            lands = [_cast_place(w[n], pos, F32 if n == 'conv_w' else MXU_DTYPE, layer=l, name=f"gw_place_{n}{l}")
                     for n, l in ks]
            self.gathers[g] = (_Gather(lands, "gw_" + g, deps=self.tokens[-1:]), ks)
            self.tokens.append(self.gathers[g][0].token)
            self.group_of.update({k: g for k in ks})
        self.red = {}
        self.final = {}

    def start_tokens(self):
        return self.tokens[-1:]

    def weight(self, name, layer, after):
        if (name, layer) not in self.weights:
            gather, ks = self.gathers[self.group_of[(name, layer)]]
            for (n, l), full in zip(ks, gather.wait(after)):
                if n == 'conv_w' or n.startswith('w_proj'):
                    full = full.transpose(1, 0, 2).reshape(full.shape[1], -1)
                elif n in ROW_SHARDED:
                    full = full.reshape(-1, full.shape[2])
                self.weights[(n, l)] = full
        return self.weights[(name, layer)]

    def slot(self, layer, slot, after, grads=None):
        tokens = []
        for step, grp, lyr in REDUCE_PLAN.get((layer, slot), ()):
            tok = self._step(step, grp, lyr, after, grads)
            if tok is not None:
                tokens.append(tok)
        return tokens

    def tail(self, steps, after, deps=()):
        toks = (self._step(step, grp, lyr, after, None, deps) for step, grp, lyr in steps)
        return [t for t in toks if t is not None]

    def shards(self):
        return {n: f.reshape(DEPTH, 2 * f.shape[2], f.shape[3]) for n, f in self.final.items()}

    def _step(self, step, grp, layer, after, grads, deps=()):
        names = GROUP_F if grp == 'F' else GROUP_M
        tag = f"{grp.lower()}{layer}"
        st = self.red.setdefault((grp, layer), {})
        n = len(names)
        if step == 'S1':
            g4s = []
            for nm in names:
                g = grads[nm]
                R, C = g.shape
                g4s.append(g.reshape(N_CHIPS, 2, R // (2 * N_CHIPS), C) if nm in ROW_SHARDED
                           else g.reshape(1, 2, R // 2, C))
            st['s1'] = _swap_halves_start(g4s, name="rs1_" + tag)
            return st['s1'][2]
        if step == 'W1':
            sems, arrays, _, waits = st.pop('s1')
            arrays = _split_wait(sems, arrays, after, waits, name=f"rs1_{tag}_wait")
            parts = [_pair_add(arrays[i], arrays[n + i], self.pos, name=f"pair_add_{tag}_{names[i]}")
                     for i in range(n)]
            st['s2'] = _scatter_start(parts, name="rs2_" + tag, deps=deps)
            return st['s2'][2]
        if step == 'W2':
            sems, arrays, _, waits = st.pop('s2')
            arrays = _split_wait(sems, arrays, after, waits, name=f"rs2_{tag}_wait")
            fs = [_chip_sum(arrays[i], arrays[n + i], self.final.get(names[i]), self.pos, layer,
                            name=f"chip_sum_{tag}_{names[i]}") for i in range(n)]
            st['s3'] = _pair_share_start(fs, layer, name="rs3_" + tag)
            return st['s3'][2]
        sems, arrays, _, waits = st.pop('s3')
        self.final.update(zip(names, _split_wait(sems, arrays, after, waits, name=f"rs3_{tag}_wait")))
        return None


def _pack(arrays):
    rows = []
    for a in arrays:
        nel = int(np.prod(a.shape))
        if nel % 1024 == 0:
            rows.append(a.astype(F32).reshape(nel // 128, 128))
        else:
            f = a.reshape(-1).astype(F32)
            rows.append(jnp.pad(f, (0, (-nel) % 1024)).reshape(-1, 128))
    return jnp.concatenate(rows, axis=0)


def _unpack(pack, shapes):
    out, row = [], 0
    for shp in shapes:
        nel = int(np.prod(shp))
        nrow = 8 * -(-nel // 1024)
        part = pack[row:row + nrow]
        out.append(part.reshape(shp) if nel % 1024 == 0 else part.reshape(-1)[:nel].reshape(shp))
        row += nrow
    return out


def kernel(x, positions, norm1, w_in, q_norm, k_norm, sinks, w_pool, pool_scale, sgu_v_norm, w_s, b_s, w_proj_a, w_proj_b, w_proj_c, w_out, norm2, w_up, conv_w, conv_b, w_down, loss_target, m_norm1, m_w_in, m_q_norm, m_k_norm, m_sinks, m_w_pool, m_pool_scale, m_sgu_v_norm, m_w_s, m_b_s, m_w_proj_a, m_w_proj_b, m_w_proj_c, m_w_out, m_norm2, m_w_up, m_conv_w, m_conv_b, m_w_down, v_norm1, v_w_in, v_q_norm, v_k_norm, v_sinks, v_w_pool, v_pool_scale, v_sgu_v_norm, v_w_s, v_b_s, v_w_proj_a, v_w_proj_b, v_w_proj_c, v_w_out, v_norm2, v_w_up, v_conv_w, v_conv_b, v_w_down):
    w = dict(norm1=norm1, w_in=w_in, q_norm=q_norm, k_norm=k_norm, sinks=sinks, w_pool=w_pool, pool_scale=pool_scale,
             sgu_v_norm=sgu_v_norm, w_s=w_s, b_s=b_s, w_proj_a=w_proj_a, w_proj_b=w_proj_b, w_proj_c=w_proj_c,
             w_out=w_out, norm2=norm2, w_up=w_up, conv_w=conv_w, conv_b=conv_b, w_down=w_down)
    m = dict(norm1=m_norm1, w_in=m_w_in, q_norm=m_q_norm, k_norm=m_k_norm, sinks=m_sinks, w_pool=m_w_pool,
             pool_scale=m_pool_scale, sgu_v_norm=m_sgu_v_norm, w_s=m_w_s, b_s=m_b_s, w_proj_a=m_w_proj_a,
             w_proj_b=m_w_proj_b, w_proj_c=m_w_proj_c, w_out=m_w_out, norm2=m_norm2, w_up=m_w_up, conv_w=m_conv_w,
             conv_b=m_conv_b, w_down=m_w_down)
    v = dict(norm1=v_norm1, w_in=v_w_in, q_norm=v_q_norm, k_norm=v_k_norm, sinks=v_sinks, w_pool=v_w_pool,
             pool_scale=v_pool_scale, sgu_v_norm=v_sgu_v_norm, w_s=v_w_s, b_s=v_b_s, w_proj_a=v_w_proj_a,
             w_proj_b=v_w_proj_b, w_proj_c=v_w_proj_c, w_out=v_w_out, norm2=v_norm2, w_up=v_w_up, conv_w=v_conv_w,
             conv_b=v_conv_b, w_down=v_w_down)
    chip = 2 * lax.axis_index("x") + lax.axis_index("y")
    core = lax.axis_index("c")

    pos = jnp.stack([chip, core, 2 * chip + core]).astype(jnp.int32)
    comm = _Comm(w, pos)

    cos, sin = _rope_tables(positions[0])
    sp = {n: w[n] for n in SMALL if n != 'conv_w'}
    loss_row, dx, gs = _local_step(x[0], loss_target[0], cos, sin, sp, comm)

    delta, new_m, new_v, grad_out = {}, {}, {}, {}

    def adamw_big(names, grads):
        for n in names:
            shp = w[n].shape
            two_d = lambda a: a.reshape(shp[0] * shp[1], shp[2])
            d, nm, nv, g = _adamw(two_d(w[n]), two_d(grads[n]), two_d(m[n]), two_d(v[n]),
                                  tr=_row_tile(shp[0] * shp[1], 256), name=f"adamw_{n}", copy_g=True)
            delta[n], new_m[n], new_v[n], grad_out[n] = d.reshape(shp), nm.reshape(shp), nv.reshape(shp), g.reshape(shp)

    small_shapes = [gs[n].shape for n in SMALL] + [(1,)]
    small_pack = _pack([gs[n] for n in SMALL] + [loss_row[0, :1]])
    small = _Gather([_cast_place(small_pack, pos, F32, slots=N_DEV, which=2, name="small_place")], "small_gather",
                    all_devices=True)
    toks = comm.tail(REDUCE_TAIL_A[:1], (dx, small.token))
    comm.tail(REDUCE_TAIL_A[1:], (dx, *toks))
    comm.tail(REDUCE_TAIL_B, dx)
    adamw_big(GROUP_F, comm.shards())
    red = _sum_slots(small.wait(new_v[GROUP_F[-1]])[0], tr=small_pack.shape[0], name="small_sum")
    *small_grads, loss = _unpack(red, small_shapes)
    g_small = dict(zip(SMALL, small_grads))
    comm.tail(REDUCE_TAIL_C, red)
    grads = comm.shards()
    grads.update(g_small)
    shard_cols = conv_w.shape[2]
    grads['conv_w'] = lax.dynamic_slice_in_dim(g_small['conv_w'], chip * shard_cols, shard_cols, axis=2)

    adamw_big(GROUP_M, grads)
    shapes = [w[n].shape for n in SMALL]
    packs = [_pack([src[n] for n in SMALL]) for src in (w, grads, m, v)]
    d, nm, nv = _adamw(*packs, tr=packs[0].shape[0], name="adamw_small")
    for dst, src in ((delta, d), (new_m, nm), (new_v, nv)):
        dst.update(zip(SMALL, _unpack(src, shapes)))

    grads.update(grad_out)
    return (loss[0], dx[None], *[grads[n] for n in WEIGHTS], *[delta[n] for n in WEIGHTS],
            *[new_m[n] for n in WEIGHTS], *[new_v[n] for n in WEIGHTS])
```

```python
import functools
import math

import numpy as np
import jax
import jax.numpy as jnp
from jax import lax
from jax.experimental import pallas as pl
from jax.experimental.pallas import tpu as pltpu

F32 = jnp.float32
MXU_DTYPE = jnp.bfloat16
COMM_DTYPE = jnp.bfloat16
ACT_DTYPE = jnp.bfloat16
HALO = 16

D_MODEL = 1024
DEPTH = 2
HEAD_DIM = 64
POOL_WINDOWS = (2, 4, 8, 16)
POOL_WIDTH = 256
N_Q_HEADS = 8
ATTN_BLOCK = 128
ATTN_WIDTH = 512
KV_WIDTH = 128
CHUNK = 128
SGU_WIDTH = 256
IN_COLS = 4608
GATE_COL0 = 1536
D_FF = 2816
ROPE_THETA = 10000.0
EPS = 1e-6
ADAM_LR, ADAM_B1, ADAM_B2, ADAM_EPS, ADAM_WD, ADAM_STEP = 0.001, 0.9, 0.999, 1e-08, 0.01, 10

N_CHIPS = 4
N_DEV = 8
VMEM_LIMIT_BYTES = 56 * 1024 * 1024
NEG_BIG = -1e30
MESH = pl.DeviceIdType.MESH
ANY = pl.BlockSpec(memory_space=pl.ANY)

SDS = jax.ShapeDtypeStruct


def _cp(*sem):
    return pltpu.CompilerParams(dimension_semantics=sem, vmem_limit_bytes=VMEM_LIMIT_BYTES)


def _dot(a, b, dims=((1,), (0,))):
    return lax.dot_general(a.astype(MXU_DTYPE), b.astype(MXU_DTYPE), (dims, ((), ())),
                           preferred_element_type=F32)


NT = ((1,), (1,))
TN = ((0,), (0,))


def _split_dot(x, m):
    hi = x.astype(MXU_DTYPE)
    lo = (x - hi.astype(F32)).astype(MXU_DTYPE)
    return _dot(hi, m) + _dot(lo, m)


def _seg_matrix(width, seg):
    idx = np.arange(width) // seg
    return jnp.asarray((idx[:, None] == idx[None, :]).astype(np.float32), dtype=MXU_DTYPE)


def _lane(shape):
    return lax.broadcasted_iota(jnp.int32, shape, len(shape) - 1)


def _row(shape):
    return lax.broadcasted_iota(jnp.int32, shape, 0)


def _full(shape):
    nd = len(shape)
    return pl.BlockSpec(shape, lambda *_: (0,) * nd)


def _gelu(x):
    k = math.sqrt(2.0 / math.pi)
    th = jnp.tanh(k * (x + 0.044715 * (x * x * x)))
    return 0.5 * x * (1.0 + th)


def _gelu_and_grad(x):
    k = math.sqrt(2.0 / math.pi)
    x2 = x * x
    th = jnp.tanh(k * (x + 0.044715 * (x2 * x)))
    g = 0.5 * x * (1.0 + th)
    dg = 0.5 * (1.0 + th) + 0.5 * x * (1.0 - th * th) * (k * (1.0 + 3.0 * 0.044715 * x2))
    return g, dg


def _sigmoid(x):
    return 0.5 * jnp.tanh(0.5 * x) + 0.5


def _swap_halves(x):
    w = x.shape[-1]
    first = (_lane(x.shape) % HEAD_DIM) < (HEAD_DIM // 2)
    return jnp.where(first, pltpu.roll(x, w - HEAD_DIM // 2, 1), pltpu.roll(x, HEAD_DIM // 2, 1))


def _tile_lanes(x, reps):
    return x if reps == 1 else jnp.concatenate([x] * reps, axis=1)


def _fold_lanes(x, period):
    w = x.shape[-1]
    while w > period:
        w //= 2
        x = x + pltpu.roll(x, w, 1)
    return x


def _mm(a, b, *, mode, tm, tn, tk, out_dtype=F32, add=None, name,
        a_lead=None, b_lead=None, b_sharded=False, out_into=None,
        b_koff=0, out_joff=0, out_n=None, deps=()):
    ash = a.shape[1:] if a_lead is not None else a.shape
    bsh = b.shape[1:] if b_lead is not None else b.shape
    if b_sharded:
        bsh = (b.shape[1], N_CHIPS * b.shape[2])
    if mode == 'nn':
        (M, K), (K2, N) = ash, bsh
    elif mode == 'nt':
        (M, K), (N, K2) = ash, bsh
    else:
        (K, M), (K2, N) = ash, bsh
    assert K == K2 or (mode == 'nt' and K2 > K), (ash, bsh, mode)
    assert M % tm == 0 and N % tn == 0 and K % tk == 0, (M, N, K, tm, tn, tk)
    nk = K // tk
    dims = {'nn': ((1,), (0,)), 'nt': NT, 'tn': TN}[mode]

    def lead(spec_shape, imap, lead_idx):
        if lead_idx is None:
            return pl.BlockSpec(spec_shape, imap)
        return pl.BlockSpec((None,) + spec_shape, lambda i, j, k: (lead_idx,) + imap(i, j, k))

    if mode == 'tn':
        a_spec = lead((tk, tm), lambda i, j, k: (k, i), a_lead)
    else:
        a_spec = lead((tm, tk), lambda i, j, k: (i, k), a_lead)
    if b_sharded:
        per = b.shape[2] // (tk if mode == 'nt' else tn)
        assert per * (tk if mode == 'nt' else tn) == b.shape[2] and mode != 'tn'
        if mode == 'nt':
            b_spec = pl.BlockSpec((None, tn, tk), lambda i, j, k: ((k + b_koff) // per, j, (k + b_koff) % per))
        else:
            b_spec = pl.BlockSpec((None, tk, tn), lambda i, j, k: (j // per, k, j % per))
    elif mode == 'nt':
        b_spec = lead((tn, tk), lambda i, j, k: (j, k + b_koff), b_lead)
    else:
        b_spec = lead((tk, tn), lambda i, j, k: (k, j), b_lead)
    o_spec = pl.BlockSpec((tm, tn), lambda i, j, k: (i, j + out_joff))
    n_out = N if out_n is None else out_n
    in_specs = [a_spec, b_spec]
    operands = [a, b]
    if add is not None:
        in_specs.append(pl.BlockSpec((tm, tn), lambda i, j, k: (i, j)))
        operands.append(add)
    aliases = {}
    if out_into is not None:
        in_specs.append(ANY)
        operands.append(out_into)
        aliases = {len(operands) - 1: 0}
    in_specs += [ANY] * len(deps)
    operands += list(deps)
    has_add = add is not None
    acc_in_out = nk > 1 and out_dtype == F32

    def body(*refs):
        a_ref, b_ref = refs[0], refs[1]
        pos = 2
        add_ref = None
        if has_add:
            add_ref = refs[pos]
            pos += 1
        if out_into is not None:
            pos += 1
        pos += len(deps)
        o_ref = refs[pos]
        acc_ref = refs[pos + 1] if (nk > 1 and not acc_in_out) else None
        p = _dot(a_ref[...], b_ref[...], dims)
        if nk == 1:
            if has_add:
                p = p + add_ref[...]
            o_ref[...] = p.astype(o_ref.dtype)
            return
        k = pl.program_id(2)
        tgt = o_ref if acc_in_out else acc_ref

        @pl.when(k == 0)
        def _():
            tgt[...] = p + add_ref[...] if has_add else p

        @pl.when(k > 0)
        def _():
            tgt[...] += p

        if not acc_in_out:
            @pl.when(k == nk - 1)
            def _():
                o_ref[...] = acc_ref[...].astype(o_ref.dtype)

    out_shape = SDS((M, n_out), out_dtype)
    scratch = [pltpu.VMEM((tm, tn), F32)] if (nk > 1 and not acc_in_out) else []
    return pl.pallas_call(
        body, grid=(M // tm, N // tn, nk), in_specs=in_specs, out_specs=o_spec, out_shape=out_shape,
        scratch_shapes=scratch, input_output_aliases=aliases, name=name,
        compiler_params=_cp("parallel", "parallel", "arbitrary"))(*operands)


def _rms_bwd_rows(xv, g, dh, dres):
    r = lax.rsqrt(jnp.mean(xv * xv, axis=-1, keepdims=True) + EPS)
    xh = xv * r
    gy = dh * g
    dx = r * (gy - xh * jnp.mean(xh * gy, axis=-1, keepdims=True)) + dres
    return dx, jnp.sum(dh * xh, axis=0, keepdims=True)


def _mm_nt_sharded_rms(a, b, x, g, dres, *, tm, name, deps=()):
    a3 = a if a.ndim == 3 else a[None]
    A, M, ka = a3.shape
    S, N, ns = b.shape
    per = S // A
    assert ka == per * ns and M % tm == 0 and N == x.shape[1], (a3.shape, b.shape, x.shape)

    def body(a_ref, b_ref, x_ref, g_ref, dres_ref, dx_ref, dxb_ref, dg_ref):
        acc = None
        for s in range(S):
            lo = (s % per) * ns
            p = _dot(a_ref[s // per, :, lo:lo + ns], b_ref[s], NT)
            acc = p if acc is None else acc + p
        dx, dg = _rms_bwd_rows(x_ref[...], g_ref[...], acc, dres_ref[...])
        dx_ref[...] = dx
        dxb_ref[...] = dx.astype(dxb_ref.dtype)

        @pl.when(pl.program_id(0) == 0)
        def _():
            dg_ref[...] = jnp.zeros_like(dg_ref)
        dg_ref[...] += dg

    rows = pl.BlockSpec((tm, N), lambda i: (i, 0))
    return pl.pallas_call(
        _after(body, 5, deps), grid=(M // tm,),
        in_specs=[pl.BlockSpec((A, tm, ka), lambda i: (0, i, 0)),
                  pl.BlockSpec((S, N, ns), lambda i: (0, 0, 0), pipeline_mode=pl.Buffered(1)),
                  rows, _full((1, N)), rows] + [ANY] * len(deps),
        out_specs=[rows, rows, _full((1, N))],
        out_shape=[SDS((M, N), F32), SDS((M, N), MXU_DTYPE), SDS((1, N), F32)], name=name,
        compiler_params=_cp("arbitrary"))(a3, b, x, g, dres, *deps)


def _norm_mm(x, g, b, *, tm, tn, name, deps=()):
    M, K = x.shape
    S, K2, ns = b.shape
    per = ns // tn
    assert K == K2 and per * tn == ns and M % tm == 0, (x.shape, b.shape)

    def body(x_ref, g_ref, b_ref, o_ref, h_ref):
        @pl.when(pl.program_id(1) == 0)
        def _():
            xv = x_ref[...]
            r = lax.rsqrt(jnp.mean(xv * xv, axis=-1, keepdims=True) + EPS)
            h_ref[...] = (xv * r * g_ref[...]).astype(h_ref.dtype)
        o_ref[...] = _dot(h_ref[...], b_ref[...]).astype(o_ref.dtype)

    return pl.pallas_call(
        _after(body, 3, deps), grid=(M // tm, S * per),
        in_specs=[pl.BlockSpec((tm, K), lambda i, j: (i, 0)), _full((1, K)),
                  pl.BlockSpec((None, K, tn), lambda i, j: (j // per, 0, j % per))] + [ANY] * len(deps),
        out_specs=[pl.BlockSpec((tm, tn), lambda i, j: (i, j)), pl.BlockSpec((tm, K), lambda i, j: (i, 0))],
        out_shape=[SDS((M, S * ns), ACT_DTYPE), SDS((M, K), MXU_DTYPE)], name=name,
        compiler_params=_cp("parallel", "arbitrary"))(x, g, b, *deps)


def _after(body, n_in, deps):
    nd = len(deps)
    if nd == 0:
        return body
    return lambda *refs: body(*refs[:n_in], *refs[n_in + nd:])


def _down_proj_loss(act, w, x1, target, *, tm, name):
    T, K = act.shape
    D = w.shape[1]

    def body(a_ref, w_ref, x_ref, t_ref, loss_ref, dy_ref, dyb_ref):
        i = pl.program_id(0)
        d = (x_ref[...] + _dot(a_ref[...], w_ref[...])) - t_ref[...]
        dy = d * (1.0 / D)
        dy_ref[...] = dy
        dyb_ref[...] = dy.astype(dyb_ref.dtype)
        part = jnp.sum(jnp.sum(d * d, axis=1, keepdims=True), axis=0, keepdims=True) * (0.5 / D)

        @pl.when(i == 0)
        def _():
            loss_ref[...] = jnp.zeros_like(loss_ref)
        loss_ref[...] += jnp.broadcast_to(part, loss_ref.shape)

    rows = pl.BlockSpec((tm, D), lambda i: (i, 0))
    return pl.pallas_call(
        body, grid=(T // tm,), in_specs=[pl.BlockSpec((tm, K), lambda i: (i, 0)), _full((K, D)), rows, rows],
        out_specs=[_full((1, 128)), rows, rows],
        out_shape=[SDS((1, 128), F32), SDS((T, D), F32), SDS((T, D), MXU_DTYPE)],
        name=name, compiler_params=_cp("arbitrary"))(act, w, x1, target)


def _pool_lane_consts(shape):
    lane = _lane(shape)
    grp = lane // (POOL_WIDTH // 4)
    win = jnp.where(grp == 0, 2, jnp.where(grp == 1, 4, jnp.where(grp == 2, 8, 16)))
    return grp, win


def _pool_select(grp, s2, s4, s8, s16):
    return jnp.where(grp == 0, s2, jnp.where(grp == 1, s4, jnp.where(grp == 2, s8, s16)))


def _pool_diff(xe, row0, tr):
    s2 = xe + pltpu.roll(xe, 1, 0)
    s4 = s2 + pltpu.roll(s2, 2, 0)
    s8 = s4 + pltpu.roll(s4, 4, 0)
    s16 = s8 + pltpu.roll(s8, 8, 0)
    shape = (tr, POOL_WIDTH)
    grp, win = _pool_lane_consts(shape)
    sums = _pool_select(grp, s2[16:], s4[16:], s8[16:], s16[16:])
    t = row0 + _row(shape)
    cnt = jnp.minimum(t + 1, win).astype(F32)
    return sums / cnt - xe[16:]


def _pool_fwd(z, wbd, scale, *, tr, name):
    T = z.shape[0]
    hb = tr // 16

    def body(x_ref, xp_ref, w_ref, s_ref, o_ref):
        i = pl.program_id(0)
        halo = jnp.where(i == 0, 0.0, xp_ref[...].astype(F32))
        diff = _pool_diff(jnp.concatenate([halo, x_ref[...].astype(F32)], axis=0), i * tr, tr)
        o_ref[...] = (_dot(diff, w_ref[...]) * s_ref[...]).astype(o_ref.dtype)

    return pl.pallas_call(
        body, grid=(T // tr,),
        in_specs=[pl.BlockSpec((tr, POOL_WIDTH), lambda i: (i, 0)),
                  pl.BlockSpec((16, POOL_WIDTH), lambda i: (jnp.maximum(i * hb - 1, 0), 0)),
                  _full((POOL_WIDTH, POOL_WIDTH)), _full((1, POOL_WIDTH))],
        out_specs=pl.BlockSpec((tr, POOL_WIDTH), lambda i: (i, 0)),
        out_shape=SDS((T, POOL_WIDTH), MXU_DTYPE), name=name, compiler_params=_cp("parallel"))(z, z, wbd, scale)


def _pool_bwd_tile(i, n, tr, x, xprev, dpa, dpa_next, wbd, scale):
    halo = jnp.where(i == 0, 0.0, xprev)
    diff = _pool_diff(jnp.concatenate([halo, x], axis=0), i * tr, tr)
    mixed = _dot(diff, wbd)
    dscale = jnp.sum(dpa * mixed, axis=0, keepdims=True)
    dnext = jnp.where(i == n - 1, 0.0, dpa_next)
    dmix_e = jnp.concatenate([dpa, dnext], axis=0) * scale
    ddiff_e = _dot(dmix_e, wbd, NT)
    dwbd = _dot(diff, dmix_e[:tr], TN)
    shape = (tr + 16, POOL_WIDTH)
    grp, win = _pool_lane_consts(shape)
    t = i * tr + _row(shape)
    e = ddiff_e / jnp.minimum(t + 1, win).astype(F32)
    nrow = tr + 16
    a2 = e + pltpu.roll(e, nrow - 1, 0)
    a4 = a2 + pltpu.roll(a2, nrow - 2, 0)
    a8 = a4 + pltpu.roll(a4, nrow - 4, 0)
    a16 = a8 + pltpu.roll(a8, nrow - 8, 0)
    dx = _pool_select(grp, a2, a4, a8, a16)[:tr] - ddiff_e[:tr]
    return dx, dwbd, dscale


def _norm_rope(x, g, cos, sin_signed, seg):
    reps = x.shape[1] // 128
    ms = _split_dot(x * x, seg) * (1.0 / HEAD_DIM)
    r = lax.rsqrt(ms + EPS)
    xn = x * r * g
    c, s = _tile_lanes(cos, reps), _tile_lanes(sin_signed, reps)
    return xn * c + _swap_halves(xn) * s


def _norm_rope_bwd(x, g, cos, sin_signed, seg, dout):
    reps = x.shape[1] // 128
    c, s = _tile_lanes(cos, reps), _tile_lanes(sin_signed, reps)
    dxn = dout * c + _swap_halves(dout * s)
    ms = _split_dot(x * x, seg) * (1.0 / HEAD_DIM)
    r = lax.rsqrt(ms + EPS)
    xh = x * r
    gy = dxn * g
    dx = r * (gy - xh * (_split_dot(xh * gy, seg) * (1.0 / HEAD_DIM)))
    dg = jnp.sum(dxn * xh, axis=0, keepdims=True)
    return dx, dg


def _dup_heads(k):
    first = _lane(k.shape) < HEAD_DIM
    kr = pltpu.roll(k, HEAD_DIM, 1)
    return jnp.concatenate([jnp.where(first, k, kr), jnp.where(first, kr, k)], axis=1)


def _qkv_prep(z, cos, sin_signed, gq, gk, seg, *, tr, name):
    T = z.shape[0]

    def body(qa_ref, qb_ref, kv_ref, c_ref, s_ref, gq_ref, gk_ref, seg_ref, q_ref, k_ref, v_ref):
        c, s, seg_m = c_ref[...], s_ref[...], seg_ref[...]
        scale = HEAD_DIM ** -0.5
        qa = _norm_rope(qa_ref[...].astype(F32), gq_ref[...], c, s, seg_m) * scale
        qb = _norm_rope(qb_ref[...].astype(F32), gq_ref[...], c, s, seg_m) * scale
        q_ref[...] = jnp.concatenate([qa, qb], axis=1).astype(q_ref.dtype)
        kv = kv_ref[...].astype(F32)
        k = _norm_rope(kv[:, :KV_WIDTH], gk_ref[...], c, s, seg_m[:128, :128])
        k_ref[...] = _dup_heads(k).astype(k_ref.dtype)
        v_ref[...] = _dup_heads(kv[:, KV_WIDTH:]).astype(v_ref.dtype)

    col = lambda j: pl.BlockSpec((tr, 256), lambda i: (i, j))
    tab = pl.BlockSpec((tr, 128), lambda i: (i, 0))
    return pl.pallas_call(
        body, grid=(T // tr,),
        in_specs=[col(1), col(2), col(3), tab, tab, _full((1, 256)), _full((1, 128)), _full((256, 256))],
        out_specs=[pl.BlockSpec((tr, 512), lambda i: (i, 0)), col(0), col(0)],
        out_shape=[SDS((T, 512), MXU_DTYPE), SDS((T, 256), MXU_DTYPE), SDS((T, 256), MXU_DTYPE)],
        name=name, compiler_params=_cp("parallel"))(z, z, z, cos, sin_signed, gq, gk, seg)


GROUP_HEADS = 4
GROUP_ROWS = GROUP_HEADS * ATTN_BLOCK
ALL_ROWS = N_Q_HEADS * ATTN_BLOCK


def _attn_mask(has_prev):
    qi = _row((ALL_ROWS, 2 * ATTN_BLOCK)) % ATTN_BLOCK
    kj = _lane((ALL_ROWS, 2 * ATTN_BLOCK))
    return (kj > qi) & (kj <= qi + ATTN_BLOCK) & ((kj >= ATTN_BLOCK) | has_prev)


FWD_STEP_BLOCKS = 8
BWD_STEP_BLOCKS = 2


def _band(prev, cur, blk):
    lo = cur[(blk - 1) * ATTN_BLOCK:blk * ATTN_BLOCK] if blk else prev
    return jnp.concatenate([lo, cur[blk * ATTN_BLOCK:(blk + 1) * ATTN_BLOCK]], axis=0)


def _stack_heads(x, g):
    first = _lane((ATTN_BLOCK, 128)) < HEAD_DIM
    parts = []
    for pair in (2 * g, 2 * g + 1):
        x128 = x[:, 128 * pair:128 * (pair + 1)]
        zero = jnp.zeros_like(x128)
        parts += [jnp.where(first, x128, zero), jnp.where(first, zero, x128)]
    return jnp.concatenate(parts, axis=0)


def _unstack_heads(y):
    first = _lane((ATTN_BLOCK, 128)) < HEAD_DIM
    b = ATTN_BLOCK
    return jnp.concatenate([jnp.where(first, y[0:b], y[b:2 * b]), jnp.where(first, y[2 * b:3 * b], y[3 * b:4 * b])],
                           axis=1)


def _sink_col(sk_ref):
    return jnp.concatenate([jnp.broadcast_to(sk_ref[h:h + 1, 0:1], (ATTN_BLOCK, 1)) for h in range(N_Q_HEADS)],
                           axis=0)


def _by_group(a8, b2, dims=((1,), (0,))):
    return jnp.concatenate([_dot(a8[:GROUP_ROWS], b2[:, :128], dims), _dot(a8[GROUP_ROWS:], b2[:, 128:], dims)],
                           axis=0)


def _softmax_exp(q8, k2, mask, sink):
    s = jnp.where(mask, _by_group(q8, k2, NT), NEG_BIG)
    m = jnp.maximum(jnp.max(s, axis=1, keepdims=True), sink)
    p = jnp.exp(s - m)
    ps = jnp.exp(sink - m)
    return p, ps, 1.0 / (jnp.sum(p, axis=1, keepdims=True) + ps)


def _attn_fwd(q, k, v, sinks_b, *, name):
    T = q.shape[0]
    nb = T // ATTN_BLOCK
    STEP_BLOCKS = min(FWD_STEP_BLOCKS, nb)
    STEP_ROWS = STEP_BLOCKS * ATTN_BLOCK

    def body(q_ref, kc_ref, kp_ref, vc_ref, vp_ref, sk_ref, o_ref):
        n = pl.program_id(0)
        kc, kp, vc, vp = kc_ref[...], kp_ref[...], vc_ref[...], vp_ref[...]
        sink = _sink_col(sk_ref)
        for blk in range(STEP_BLOCKS):
            rows = slice(blk * ATTN_BLOCK, (blk + 1) * ATTN_BLOCK)
            mask = _attn_mask((n > 0) if blk == 0 else True)
            k2, v2 = _band(kp, kc, blk), _band(vp, vc, blk)
            qv = q_ref[rows, :]
            q8 = jnp.concatenate([_stack_heads(qv, 0), _stack_heads(qv, 1)], axis=0)
            p, _, inv = _softmax_exp(q8, k2, mask, sink)
            o8 = _by_group(p, v2) * inv
            o_ref[rows, :] = jnp.concatenate([_unstack_heads(o8[:GROUP_ROWS]), _unstack_heads(o8[GROUP_ROWS:])],
                                             axis=1).astype(o_ref.dtype)

    cur = lambda w: pl.BlockSpec((STEP_ROWS, w), lambda n: (n, 0))
    prev = lambda w: pl.BlockSpec((ATTN_BLOCK, w), lambda n: (jnp.maximum(STEP_BLOCKS * n - 1, 0), 0))
    return pl.pallas_call(
        body, grid=(nb // STEP_BLOCKS,),
        in_specs=[cur(512), cur(256), prev(256), cur(256), prev(256), _full((8, 128))],
        out_specs=cur(512), out_shape=SDS((T, 512), MXU_DTYPE), name=name,
        compiler_params=_cp("parallel"))(q, k, k, v, v, sinks_b)


def _attn_bwd(q, k, v, sinks_b, do, *, name, deps=()):
    T = q.shape[0]
    nb = T // ATTN_BLOCK
    STEP_BLOCKS = min(BWD_STEP_BLOCKS, nb)
    STEP_ROWS = STEP_BLOCKS * ATTN_BLOCK

    def body(q_ref, kc_ref, kp_ref, vc_ref, vp_ref, sk_ref, do_ref,
             dq_ref, dkc_ref, dkp_ref, dvc_ref, dvp_ref, dsk_ref):
        n = pl.program_id(0)
        kc, kp, vc, vp = kc_ref[...], kp_ref[...], vc_ref[...], vp_ref[...]
        sink = _sink_col(sk_ref)

        @pl.when(n == 0)
        def _():
            dsk_ref[...] = jnp.zeros_like(dsk_ref)

        for blk in range(STEP_BLOCKS):
            rows = slice(blk * ATTN_BLOCK, (blk + 1) * ATTN_BLOCK)
            mask = _attn_mask((n > 0) if blk == 0 else True)
            k2, v2 = _band(kp, kc, blk), _band(vp, vc, blk)
            qv, dov = q_ref[rows, :], do_ref[rows, :]
            q8 = jnp.concatenate([_stack_heads(qv, 0), _stack_heads(qv, 1)], axis=0)
            do8 = jnp.concatenate([_stack_heads(dov, 0), _stack_heads(dov, 1)], axis=0)
            p, ps, inv = _softmax_exp(q8, k2, mask, sink)
            pn = p * inv
            delta = jnp.sum(do8 * _by_group(pn, v2), axis=1, keepdims=True)
            ds = pn * (_by_group(do8, v2, NT) - delta)
            dq8 = _by_group(ds, k2)
            dq_ref[rows, :] = jnp.concatenate([_unstack_heads(dq8[:GROUP_ROWS]), _unstack_heads(dq8[GROUP_ROWS:])],
                                              axis=1)
            dk = jnp.concatenate([_dot(ds[:GROUP_ROWS], q8[:GROUP_ROWS], TN),
                                  _dot(ds[GROUP_ROWS:], q8[GROUP_ROWS:], TN)], axis=1)
            dv = jnp.concatenate([_dot(pn[:GROUP_ROWS], do8[:GROUP_ROWS], TN),
                                  _dot(pn[GROUP_ROWS:], do8[GROUP_ROWS:], TN)], axis=1)
            wsink = (ps * inv) * delta
            for h in range(N_Q_HEADS):
                dsink = -jnp.sum(wsink[ATTN_BLOCK * h:ATTN_BLOCK * (h + 1)], axis=0, keepdims=True)
                dsk_ref[h:h + 1, :] += jnp.broadcast_to(dsink, (1, 128))
            dkp_ref[rows, :] = dk[:ATTN_BLOCK]
            dkc_ref[rows, :] = dk[ATTN_BLOCK:]
            dvp_ref[rows, :] = dv[:ATTN_BLOCK]
            dvc_ref[rows, :] = dv[ATTN_BLOCK:]

    cur = lambda w: pl.BlockSpec((STEP_ROWS, w), lambda n: (n, 0))
    prev = lambda w: pl.BlockSpec((ATTN_BLOCK, w), lambda n: (jnp.maximum(STEP_BLOCKS * n - 1, 0), 0))
    f = lambda w: SDS((T, w), F32)
    return pl.pallas_call(
        _after(body, 7, deps), grid=(nb // STEP_BLOCKS,),
        in_specs=[cur(512), cur(256), prev(256), cur(256), prev(256), _full((8, 128)), cur(512)] + [ANY] * len(deps),
        out_specs=[cur(512), cur(256), cur(256), cur(256), cur(256), _full((8, 128))],
        out_shape=[f(512), f(256), f(256), f(256), f(256), SDS((8, 128), F32)],
        name=name, compiler_params=_cp("arbitrary"))(q, k, k, v, v, sinks_b, do, *deps)


def _mixer_ab_bwd(z, cos, sin_signed, gq, gk, seg, dq, dkc, dkp, dvc, dvp, dpa, wbd, scale, dz, *, tr, name, deps=()):
    T = z.shape[0]
    n = T // tr
    hb = tr // 16
    ab = tr // ATTN_BLOCK

    def unfold(cur, nxt_tile, nxt_halo, i):
        nxt = jnp.concatenate([nxt_tile[ATTN_BLOCK:], jnp.where(i == n - 1, 0.0, nxt_halo)], axis=0)
        tot = cur + nxt
        first = _lane((tr, 128)) < HEAD_DIM
        a = tot[:, :128]
        b = tot[:, 128:]
        a = a + pltpu.roll(a, HEAD_DIM, 1)
        b = b + pltpu.roll(b, HEAD_DIM, 1)
        return jnp.where(first, a, b)

    def body(xp_ref, xpp_ref, qa_ref, qb_ref, kv_ref, c_ref, s_ref, gq_ref, gk_ref, seg_ref,
             dq_ref, dkc_ref, dkp_ref, dkh_ref, dvc_ref, dvp_ref, dvh_ref, dpa_ref, dpan_ref, w_ref, sc_ref, _dz_in,
             dz_ref, dgq_ref, dgk_ref, dw_ref, dsc_ref):
        i = pl.program_id(0)
        c, s, seg_m = c_ref[...], s_ref[...], seg_ref[...]
        scale_q = HEAD_DIM ** -0.5
        dqv = dq_ref[...] * scale_q
        dxa, dga = _norm_rope_bwd(qa_ref[...].astype(F32), gq_ref[...], c, s, seg_m, dqv[:, :256])
        dxb, dgb = _norm_rope_bwd(qb_ref[...].astype(F32), gq_ref[...], c, s, seg_m, dqv[:, 256:])
        dk = unfold(dkc_ref[...], dkp_ref[...], dkh_ref[...], i)
        dv = unfold(dvc_ref[...], dvp_ref[...], dvh_ref[...], i)
        kv = kv_ref[...].astype(F32)
        dxk, dgk = _norm_rope_bwd(kv[:, :KV_WIDTH], gk_ref[...], c, s, seg_m[:128, :128], dk)
        dxp, dwbd, dscale = _pool_bwd_tile(i, n, tr, xp_ref[...].astype(F32), xpp_ref[...].astype(F32),
                                           dpa_ref[...], dpan_ref[...],
                                           w_ref[...], sc_ref[...])
        dz_ref[...] = jnp.concatenate([dxp, dxa, dxb, dxk, dv], axis=1).astype(dz_ref.dtype)

        @pl.when(i == 0)
        def _():
            dgq_ref[...] = jnp.zeros_like(dgq_ref)
            dgk_ref[...] = jnp.zeros_like(dgk_ref)
            dw_ref[...] = jnp.zeros_like(dw_ref)
            dsc_ref[...] = jnp.zeros_like(dsc_ref)
        dgq_ref[...] += _fold_lanes(dga + dgb, HEAD_DIM)
        dgk_ref[...] += _fold_lanes(dgk, HEAD_DIM)
        dw_ref[...] += dwbd
        dsc_ref[...] += dscale

    col = lambda j: pl.BlockSpec((tr, 256), lambda i: (i, j))
    rows = lambda w: pl.BlockSpec((tr, w), lambda i: (i, 0))
    nxt_blk = pl.BlockSpec((ATTN_BLOCK, 256), lambda i: (jnp.minimum((i + 1) * ab, T // ATTN_BLOCK - 1), 0))
    prev16 = pl.BlockSpec((16, 256), lambda i: (jnp.maximum(i * hb - 1, 0), 0))
    next16 = pl.BlockSpec((16, 256), lambda i: (jnp.minimum((i + 1) * hb, T // 16 - 1), 0))
    return pl.pallas_call(
        _after(body, 22, deps), grid=(n,),
        in_specs=[col(0), prev16, col(1), col(2), col(3), rows(128), rows(128),
                  _full((1, 256)), _full((1, 128)), _full((256, 256)),
                  rows(512), rows(256), rows(256), nxt_blk, rows(256), rows(256), nxt_blk,
                  rows(256), next16, _full((256, 256)), _full((1, 256)), ANY] + [ANY] * len(deps),
        out_specs=[rows(1024), _full((1, 256)), _full((1, 128)), _full((256, 256)), _full((1, 256))],
        out_shape=[SDS((T, IN_COLS), MXU_DTYPE), SDS((1, 256), F32), SDS((1, 128), F32),
                   SDS((256, 256), F32), SDS((1, 256), F32)],
        input_output_aliases={21: 0}, name=name, compiler_params=_cp("arbitrary"))(
            z, z, z, z, z, cos, sin_signed, gq, gk, seg, dq, dkc, dkp, dkp, dvc, dvp, dvp, dpa, dpa, wbd, scale, dz,
            *deps)


def _sgu_common(zu, zv, vn, seg):
    u, du = _gelu_and_grad(zu)
    gv, dgv = _gelu_and_grad(zv)
    ms = _split_dot(gv * gv, seg) * (1.0 / HEAD_DIM)
    r = lax.rsqrt(ms + EPS)
    xh = gv * r
    return u, du, dgv, r, xh, xh * vn


def _sgu_fwd(z, wtril, bexp, vn, seg, *, tr, name):
    T = z.shape[0]
    nch = tr // CHUNK

    def body(u_ref, v_ref, w_ref, b_ref, vn_ref, seg_ref, o_ref):
        u, _, _, _, _, vg = _sgu_common(u_ref[...].astype(F32), v_ref[...].astype(F32), vn_ref[...], seg_ref[...])
        grp = _lane((CHUNK, SGU_WIDTH)) // HEAD_DIM
        outs = []
        for ch in range(nch):
            vc = vg[ch * CHUNK:(ch + 1) * CHUNK]
            s = b_ref[...]
            for g in range(4):
                s = s + jnp.where(grp == g, _dot(w_ref[g], vc), 0.0)
            outs.append(u[ch * CHUNK:(ch + 1) * CHUNK] * s)
        o_ref[...] = jnp.concatenate(outs, axis=0).astype(o_ref.dtype)

    col = lambda j: pl.BlockSpec((tr, 256), lambda i: (i, j))
    return pl.pallas_call(
        body, grid=(T // tr,),
        in_specs=[col(4), col(5), _full((4, CHUNK, CHUNK)), _full((CHUNK, 256)), _full((1, 256)), _full((256, 256))],
        out_specs=col(0), out_shape=SDS((T, SGU_WIDTH), MXU_DTYPE), name=name,
        compiler_params=_cp("parallel"))(z, z, wtril, bexp, vn, seg)


def _sgu_bwd(z, wtril, bexp, vn, seg, dsg, dz, *, tr, name):
    T = z.shape[0]
    nch = tr // CHUNK

    def body(u_ref, v_ref, w_ref, b_ref, vn_ref, seg_ref, d_ref, _dz_in, dz_ref, dw_ref, db_ref, dvn_ref):
        i = pl.program_id(0)
        seg_m = seg_ref[...]
        vn_v = vn_ref[...]
        u, du, dgv, r, xh, vg = _sgu_common(u_ref[...].astype(F32), v_ref[...].astype(F32), vn_v, seg_m)
        d = d_ref[...]
        grp = _lane((CHUNK, SGU_WIDTH)) // HEAD_DIM
        tril = _row((CHUNK, CHUNK)) >= _lane((CHUNK, CHUNK))

        @pl.when(i == 0)
        def _():
            dw_ref[...] = jnp.zeros_like(dw_ref)
            db_ref[...] = jnp.zeros_like(db_ref)
            dvn_ref[...] = jnp.zeros_like(dvn_ref)

        dus, dvgs = [], []
        for ch in range(nch):
            sl = slice(ch * CHUNK, (ch + 1) * CHUNK)
            vc = vg[sl]
            s = b_ref[...]
            for g in range(4):
                s = s + jnp.where(grp == g, _dot(w_ref[g], vc), 0.0)
            dus.append(d[sl] * s)
            ds = d[sl] * u[sl]
            db_ref[...] += _split_dot(ds, seg_m)
            dvg = jnp.zeros((CHUNK, SGU_WIDTH), F32)
            for g in range(4):
                dsm = jnp.where(grp == g, ds, 0.0)
                dvg = dvg + jnp.where(grp == g, _dot(w_ref[g], ds, TN), 0.0)
                dw_ref[g] += jnp.where(tril, _dot(dsm, vc, NT), 0.0)
            dvgs.append(dvg)
        dup = jnp.concatenate(dus, axis=0)
        dvg = jnp.concatenate(dvgs, axis=0)
        dvn_ref[...] += _fold_lanes(jnp.sum(dvg * xh, axis=0, keepdims=True), HEAD_DIM)
        gy = dvg * vn_v
        dgvv = r * (gy - xh * (_split_dot(xh * gy, seg_m) * (1.0 / HEAD_DIM)))
        dz_ref[...] = jnp.concatenate([dup * du, dgvv * dgv], axis=1).astype(dz_ref.dtype)

    col = lambda j: pl.BlockSpec((tr, 256), lambda i: (i, j))
    return pl.pallas_call(
        body, grid=(T // tr,),
        in_specs=[col(4), col(5), _full((4, CHUNK, CHUNK)), _full((CHUNK, 256)), _full((1, 256)), _full((256, 256)),
                  col(0), ANY],
        out_specs=[pl.BlockSpec((tr, 512), lambda i: (i, 2)), _full((4, CHUNK, CHUNK)), _full((CHUNK, 256)),
                   _full((1, 256))],
        out_shape=[SDS((T, IN_COLS), MXU_DTYPE), SDS((4, CHUNK, CHUNK), F32), SDS((CHUNK, 256), F32),
                   SDS((1, 256), F32)],
        input_output_aliases={7: 0}, name=name, compiler_params=_cp("arbitrary"))(
            z, z, wtril, bexp, vn, seg, dsg, dz)


def _merge_fwd(pa, at, sg, wa, wb, wc, z, x, w_out, *, tm, tn, name):
    T = pa.shape[0]
    gb = GATE_COL0 // tn
    nb = D_MODEL // tn

    def body(pa_ref, at_ref, sg_ref, wa_ref, wb_ref, wc_ref, g0_ref, g1_ref, g2_ref, x_ref, wo_ref,
             m_ref, y_ref, x1_ref):
        j = pl.program_id(1)
        acc = None
        for idx, (op_ref, w_ref, g_ref) in enumerate(((pa_ref, wa_ref, g0_ref), (at_ref, wb_ref, g1_ref),
                                                      (sg_ref, wc_ref, g2_ref))):
            y = _dot(op_ref[...], w_ref[...])
            y_ref[idx] = y.astype(y_ref.dtype)
            t = _sigmoid(g_ref[...].astype(F32)) * y
            acc = t if acc is None else acc + t
        merged = acc.astype(m_ref.dtype)
        m_ref[...] = merged
        p = _dot(merged, wo_ref[...])

        @pl.when(j == 0)
        def _():
            x1_ref[...] = x_ref[...] + p

        @pl.when(j > 0)
        def _():
            x1_ref[...] += p

    op = lambda w: pl.BlockSpec((tm, w), lambda i, j: (i, 0))
    wt = lambda k: pl.BlockSpec((k, tn), lambda i, j: (0, j))
    gate = lambda b: pl.BlockSpec((tm, tn), lambda i, j: (i, gb + b * nb + j))
    return pl.pallas_call(
        body, grid=(T // tm, nb),
        in_specs=[op(256), op(512), op(256), wt(256), wt(512), wt(256), gate(0), gate(1), gate(2),
                  op(D_MODEL), pl.BlockSpec((tn, D_MODEL), lambda i, j: (j, 0))],
        out_specs=[pl.BlockSpec((tm, tn), lambda i, j: (i, j)), pl.BlockSpec((3, tm, tn), lambda i, j: (0, i, j)),
                   op(D_MODEL)],
        out_shape=[SDS((T, D_MODEL), MXU_DTYPE), SDS((3, T, D_MODEL), MXU_DTYPE), SDS((T, D_MODEL), F32)],
        name=name, compiler_params=_cp("parallel", "arbitrary"))(pa, at, sg, wa, wb, wc, z, z, z, x, w_out)


def _out_dx_merge_bwd(dxb, w_out, y, z, ws, xs, *, tm, tn, name):
    T = dxb.shape[0]
    gb = GATE_COL0 // tn
    nb = D_MODEL // tn
    nr = T // tm
    widths = [w.shape[0] for w in ws]

    def body(dx_ref, w_ref, y_ref, g_ref, *refs):
        w_refs, x_refs = refs[0:3], refs[3:6]
        dz_ref, dx_refs, dw_refs = refs[6], refs[7:10], refs[10:13]
        dm_ref, acc_refs = refs[13], refs[14:17]
        i, b, j = pl.program_id(0), pl.program_id(1), pl.program_id(2)

        @pl.when((b == 0) & (j == 0))
        def _():
            dm = _dot(dx_ref[...], w_ref[...], NT)
            for jj in range(nb):
                dm_ref[jj] = dm[:, jj * tn:(jj + 1) * tn]

        d = dm_ref[j]
        g = _sigmoid(g_ref[...].astype(F32))
        dy = (d * g).astype(MXU_DTYPE)
        dz_ref[...] = (d * y_ref[...].astype(F32) * g * (1.0 - g)).astype(dz_ref.dtype)
        for branch in range(3):
            @pl.when(b == branch)
            def _():
                p = _dot(dy, w_refs[branch][...], NT)
                q = _dot(x_refs[branch][...], dy, TN)

                @pl.when(j == 0)
                def _():
                    dx_refs[branch][...] = p

                @pl.when(j > 0)
                def _():
                    dx_refs[branch][...] += p

                @pl.when(i == 0)
                def _():
                    acc_refs[branch][j] = q

                @pl.when(i > 0)
                def _():
                    acc_refs[branch][j] += q

        @pl.when((i == nr - 1) & (b == 2) & (j == nb - 1))
        def _():
            for branch in range(3):
                for jj in range(nb):
                    dw_refs[branch][:, jj * tn:(jj + 1) * tn] = acc_refs[branch][jj]

    wspec = lambda k: pl.BlockSpec((k, tn), lambda i, b, j: (0, j))
    rows = lambda k: pl.BlockSpec((tm, k), lambda i, b, j: (i, 0))
    return pl.pallas_call(
        body, grid=(nr, 3, nb),
        in_specs=[rows(D_MODEL),
                  pl.BlockSpec((D_MODEL, D_MODEL), lambda i, b, j: (0, 0), pipeline_mode=pl.Buffered(1)),
                  pl.BlockSpec((None, tm, tn), lambda i, b, j: (b, i, j)),
                  pl.BlockSpec((tm, tn), lambda i, b, j: (i, gb + b * nb + j))]
        + [wspec(k) for k in widths] + [rows(k) for k in widths],
        out_specs=[pl.BlockSpec((tm, tn), lambda i, b, j: (i, gb + b * nb + j))]
        + [rows(k) for k in widths] + [_full((k, D_MODEL)) for k in widths],
        out_shape=[SDS((T, IN_COLS), MXU_DTYPE)] + [SDS((T, k), F32) for k in widths]
        + [SDS((k, D_MODEL), F32) for k in widths],
        scratch_shapes=[pltpu.VMEM((nb, tm, tn), F32)] + [pltpu.VMEM((nb, k, tn), F32) for k in widths],
        name=name, compiler_params=_cp("arbitrary", "arbitrary", "arbitrary"))(dxb, w_out, y, z, *ws, *xs)


def _conv3(xe, w, b):
    return (w[0:1] * pltpu.roll(xe, 2, 0) + w[1:2] * pltpu.roll(xe, 1, 0) + w[2:3] * xe)[8:] + b


def _conv_act_fwd(up, cw, cb, *, tr, tc, name):
    T = up.shape[0]
    nc = D_FF // tc
    hb = tr // HALO

    def body(ug_ref, ugp_ref, uv_ref, uvp_ref, wg_ref, wv_ref, bg_ref, bv_ref, o_ref):
        i = pl.program_id(1)
        first = i == 0

        def halo_tile(prev_ref, cur_ref):
            prev8 = prev_ref[...].astype(F32)[HALO - 8:]
            return jnp.concatenate([jnp.where(first, 0.0, prev8), cur_ref[...].astype(F32)], axis=0)

        cg = _conv3(halo_tile(ugp_ref, ug_ref), wg_ref[...], bg_ref[...])
        cv = _conv3(halo_tile(uvp_ref, uv_ref), wv_ref[...], bv_ref[...])
        o_ref[...] = (cg * _sigmoid(cg) * cv).astype(o_ref.dtype)

    tile = lambda off: pl.BlockSpec((tr, tc), lambda j, i: (i, off + j))
    prev = lambda off: pl.BlockSpec((HALO, tc), lambda j, i: (jnp.maximum(i * hb - 1, 0), off + j))
    par = lambda rows, off: pl.BlockSpec((rows, tc), lambda j, i: (0, off + j))
    return pl.pallas_call(
        body, grid=(nc, T // tr),
        in_specs=[tile(0), prev(0), tile(nc), prev(nc), par(3, 0), par(3, nc), par(1, 0), par(1, nc)],
        out_specs=pl.BlockSpec((tr, tc), lambda j, i: (i, j)),
        out_shape=SDS((T, D_FF), MXU_DTYPE), name=name,
        compiler_params=_cp("parallel", "parallel"))(up, up, up, up, cw, cw, cb, cb)


def _conv_act_bwd(up, cw, cb, dact, *, tr, tc, name, deps=()):
    T = up.shape[0]
    nc = D_FF // tc
    hb = tr // 8
    hbu = tr // HALO
    nr = T // tr

    def body(ug_ref, ugp_ref, ugn_ref, uv_ref, uvp_ref, uvn_ref, da_ref, dan_ref, wg_ref, wv_ref, bg_ref, bv_ref,
             du_ref, dwg_ref, dwv_ref, dbg_ref, dbv_ref):
        i = pl.program_id(1)
        first, last = i == 0, i == nr - 1
        da = jnp.concatenate([da_ref[...], jnp.where(last, 0.0, dan_ref[...])], axis=0)

        def with_halos(prev_ref, cur_ref, next_ref):
            prev8 = prev_ref[...].astype(F32)[HALO - 8:]
            next8 = next_ref[...].astype(F32)[:8]
            return jnp.concatenate([jnp.where(first, 0.0, prev8), cur_ref[...].astype(F32), next8], axis=0)

        uge = with_halos(ugp_ref, ug_ref, ugn_ref)
        uve = with_halos(uvp_ref, uv_ref, uvn_ref)
        wg, wv = wg_ref[...], wv_ref[...]
        ug1, ug2 = pltpu.roll(uge, 1, 0)[8:], pltpu.roll(uge, 2, 0)[8:]
        uv1, uv2 = pltpu.roll(uve, 1, 0)[8:], pltpu.roll(uve, 2, 0)[8:]
        cg = wg[0:1] * ug2 + wg[1:2] * ug1 + wg[2:3] * uge[8:] + bg_ref[...]
        cv = wv[0:1] * uv2 + wv[1:2] * uv1 + wv[2:3] * uve[8:] + bv_ref[...]
        sg = _sigmoid(cg)
        dcg = da * cv * (sg * (1.0 + cg * (1.0 - sg)))
        dcv = da * (cg * sg)
        nrow = tr + 8

        def back(dc, w):
            return (w[2:3] * dc + w[1:2] * pltpu.roll(dc, nrow - 1, 0) + w[0:1] * pltpu.roll(dc, nrow - 2, 0))[:tr]

        du_ref[0] = back(dcg, wg).astype(du_ref.dtype)
        du_ref[1] = back(dcv, wv).astype(du_ref.dtype)

        def wgrad(dc, u0, u1, u2):
            d = dc[:tr]
            rows = [jnp.sum(d * u2[:tr], axis=0, keepdims=True), jnp.sum(d * u1[:tr], axis=0, keepdims=True),
                    jnp.sum(d * u0[8:8 + tr], axis=0, keepdims=True)]
            return jnp.concatenate(rows, axis=0), jnp.sum(d, axis=0, keepdims=True)

        dwg, dbg = wgrad(dcg, uge, ug1, ug2)
        dwv, dbv = wgrad(dcv, uve, uv1, uv2)

        @pl.when(first)
        def _():
            dwg_ref[...] = jnp.zeros_like(dwg_ref)
            dwv_ref[...] = jnp.zeros_like(dwv_ref)
            dbg_ref[...] = jnp.zeros_like(dbg_ref)
            dbv_ref[...] = jnp.zeros_like(dbv_ref)
        dwg_ref[...] += dwg
        dwv_ref[...] += dwv
        dbg_ref[...] += dbg
        dbv_ref[...] += dbv

    tile = lambda off: pl.BlockSpec((tr, tc), lambda j, i: (i, off + j))
    prev = lambda off: pl.BlockSpec((HALO, tc), lambda j, i: (jnp.maximum(i * hbu - 1, 0), off + j))
    nxt = lambda off: pl.BlockSpec((HALO, tc), lambda j, i: (jnp.minimum((i + 1) * hbu, T // HALO - 1), off + j))
    dnext = pl.BlockSpec((8, tc), lambda j, i: (jnp.minimum((i + 1) * hb, T // 8 - 1), j))
    par = lambda rows, off: pl.BlockSpec((rows, tc), lambda j, i: (0, off + j))
    acc = lambda rows: pl.BlockSpec((rows, tc), lambda j, i: (0, j))
    return pl.pallas_call(
        _after(body, 12, deps), grid=(nc, nr),
        in_specs=[tile(0), prev(0), nxt(0), tile(nc), prev(nc), nxt(nc), tile(0), dnext,
                  par(3, 0), par(3, nc), par(1, 0), par(1, nc)] + [ANY] * len(deps),
        out_specs=[pl.BlockSpec((2, tr, tc), lambda j, i: (0, i, j)), acc(3), acc(3), acc(1), acc(1)],
        out_shape=[SDS((2, T, D_FF), MXU_DTYPE), SDS((3, D_FF), F32), SDS((3, D_FF), F32),
                   SDS((1, D_FF), F32), SDS((1, D_FF), F32)],
        name=name, compiler_params=_cp("parallel", "arbitrary"))(
            up, up, up, up, up, up, dact, dact, cw, cw, cb, cb, *deps)


def _row_tile(rows, cap):
    t = min(cap, rows)
    t -= t % 8
    while rows % t:
        t -= 8
    return t


def _adamw(w, g, m, v, *, tr, name, copy_g=False):
    R, C = w.shape
    assert R % tr == 0, (R, tr)

    def body(w_ref, g_ref, m_ref, v_ref, d_ref, nm_ref, nv_ref, *rest):
        gv = g_ref[...]
        mn = ADAM_B1 * m_ref[...] + (1.0 - ADAM_B1) * gv
        vn = ADAM_B2 * v_ref[...] + (1.0 - ADAM_B2) * (gv * gv)
        m_hat = mn / (1.0 - ADAM_B1 ** ADAM_STEP)
        v_hat = vn / (1.0 - ADAM_B2 ** ADAM_STEP)
        d_ref[...] = -ADAM_LR * (m_hat / (jnp.sqrt(v_hat) + ADAM_EPS) + ADAM_WD * w_ref[...])
        nm_ref[...] = mn
        nv_ref[...] = vn
        if copy_g:
            rest[0][...] = gv

    rows = pl.BlockSpec((tr, C), lambda i: (i, 0))
    n_out = 4 if copy_g else 3
    return pl.pallas_call(
        body, grid=(R // tr,), in_specs=[rows] * 4, out_specs=[rows] * n_out,
        out_shape=[SDS((R, C), F32)] * n_out, name=name, compiler_params=_cp("parallel"))(w, g, m, v)


def _sum_slots(r, *, tr, name):
    S, R, C = r.shape
    assert R % tr == 0, (R, tr)

    def body(r_ref, o_ref):
        acc = r_ref[0]
        for s in range(1, S):
            acc = acc + r_ref[s]
        o_ref[...] = acc

    return pl.pallas_call(
        body, grid=(R // tr,), in_specs=[pl.BlockSpec((S, tr, C), lambda i: (0, i, 0))],
        out_specs=pl.BlockSpec((tr, C), lambda i: (i, 0)), out_shape=SDS((R, C), F32),
        name=name, compiler_params=_cp("parallel"))(r)


def _pair_add(g4, h, pos, *, name):
    A, _, r, C = g4.shape
    cs = C if A == N_CHIPS else C // N_CHIPS
    tr = _row_tile(r, 256)
    if A == N_CHIPS:
        g_map, h_map = (lambda t, i, pos: (t, pos[1], i, 0)), (lambda t, i, pos: (t, i, 0))
    else:
        g_map, h_map = (lambda t, i, pos: (0, pos[1], i, t)), (lambda t, i, pos: (0, i, t))

    def body(pos_ref, g_ref, h_ref, o_ref):
        o_ref[...] = (g_ref[...] + h_ref[...]).astype(o_ref.dtype)

    grid_spec = pltpu.PrefetchScalarGridSpec(
        num_scalar_prefetch=1, grid=(N_CHIPS, r // tr),
        in_specs=[pl.BlockSpec((None, None, tr, cs), g_map), pl.BlockSpec((None, tr, cs), h_map)],
        out_specs=pl.BlockSpec((None, tr, cs), lambda t, i, pos: (t, i, 0)))
    return pl.pallas_call(body, grid_spec=grid_spec, out_shape=SDS((N_CHIPS, r, cs), COMM_DTYPE), name=name,
                          compiler_params=_cp("parallel", "parallel"))(pos, g4, h)


def _chip_sum(p, r2, f_into, pos, layer, *, name):
    _, r, cs = p.shape
    tr = _row_tile(r, 256)

    def body(pos_ref, own_ref, r_ref, *rest):
        o_ref = rest[-1]
        o_ref[...] = ((own_ref[...].astype(F32) + r_ref[0].astype(F32)) + r_ref[1].astype(F32)) + r_ref[2].astype(F32)

    in_specs = [pl.BlockSpec((None, tr, cs), lambda i, pos: (pos[0], i, 0)),
                pl.BlockSpec((3, tr, cs), lambda i, pos: (0, i, 0))]
    operands = [pos, p, r2]
    aliases = {}
    if f_into is not None:
        in_specs.append(ANY)
        operands.append(f_into)
        aliases = {3: 0}
    grid_spec = pltpu.PrefetchScalarGridSpec(
        num_scalar_prefetch=1, grid=(r // tr,), in_specs=in_specs,
        out_specs=pl.BlockSpec((None, None, tr, cs), lambda i, pos: (layer, pos[1], i, 0)))
    return pl.pallas_call(body, grid_spec=grid_spec, out_shape=SDS((DEPTH, 2, r, cs), F32), name=name,
                          input_output_aliases=aliases, compiler_params=_cp("parallel"))(*operands)


def _mesh_pos():
    return lax.axis_index("x"), lax.axis_index("y"), lax.axis_index("c")


HBM = pl.BlockSpec(memory_space=pltpu.HBM)
SEM = pl.BlockSpec(memory_space=pltpu.SEMAPHORE)
DATAFLOW = pltpu.SideEffectType.DATAFLOW_SIDE_EFFECTING
CHIP_FLIPS = (2, 1, 3)


def _chip_peers():
    x, y, c = _mesh_pos()
    return 2 * x + y, [(1 - x, y, c), (x, 1 - y, c), (1 - x, 1 - y, c)], (x, y, 1 - c), c


def _split_start(arrays, n_copies, issue, *, name, deps=()):
    k = len(arrays)
    nd = len(deps)

    def body(*refs):
        issue(refs[:k], refs[k + nd], refs[k + nd + 1])
        refs[2 * k + nd + 2][...] = jnp.zeros((8, 128), F32)

    out = pl.pallas_call(
        body, name=name,
        out_shape=(pltpu.SemaphoreType.DMA((n_copies,)), pltpu.SemaphoreType.DMA((n_copies,)),
                   *[pltpu.HBM(a.shape, a.dtype) for a in arrays], SDS((8, 128), F32)),
        in_specs=[HBM] * k + [ANY] * nd, out_specs=(SEM, SEM, *[HBM] * k, pl.BlockSpec(memory_space=pltpu.VMEM)),
        input_output_aliases={i: 2 + i for i in range(k)},
        compiler_params=pltpu.CompilerParams(has_side_effects=DATAFLOW))(
            *[pltpu.with_memory_space_constraint(a, pltpu.HBM) for a in arrays], *deps)
    return (out[0], out[1]), list(out[2:2 + k]), out[2 + k]


def _split_wait(sems, arrays, after, waits, *, name):
    k = len(arrays)
    afters = tuple(after) if isinstance(after, (tuple, list)) else (after,)

    def body(*refs):
        waits(refs[:k], refs[k], refs[k + 1])

    out = pl.pallas_call(
        body, name=name, out_shape=tuple(pltpu.HBM(a.shape, a.dtype) for a in arrays),
        in_specs=[HBM] * k + [SEM, SEM] + [ANY] * len(afters), out_specs=tuple([HBM] * k),
        input_output_aliases={i: i for i in range(k)},
        compiler_params=pltpu.CompilerParams(has_side_effects=DATAFLOW))(*arrays, sems[0], sems[1], *afters)
    return list(out)


def _wait_both(cp):
    cp.wait_send()
    cp.wait_recv()


def _cast_place(shard, pos, dtype, *, name, layer=None, slots=N_CHIPS, which=0):
    R, C = shard.shape[-2:]
    tr = R if R % 8 else _row_tile(R, 256)
    if layer is None:
        in_spec = pl.BlockSpec((tr, C), lambda i, pos: (i, 0))
    else:
        in_spec = pl.BlockSpec((None, tr, C), lambda i, pos: (layer, i, 0))

    def body(pos_ref, x_ref, o_ref):
        o_ref[...] = x_ref[...].astype(o_ref.dtype)

    grid_spec = pltpu.PrefetchScalarGridSpec(
        num_scalar_prefetch=1, grid=(R // tr,), in_specs=[in_spec],
        out_specs=pl.BlockSpec((None, tr, C), lambda i, pos: (pos[which], i, 0)))
    return pl.pallas_call(body, grid_spec=grid_spec, out_shape=SDS((slots, R, C), dtype), name=name,
                          compiler_params=_cp("parallel"))(pos, shard)


def _device_peers():
    x, y, c = _mesh_pos()
    peers = [(x ^ ((f >> 2) & 1), y ^ ((f >> 1) & 1), c ^ (f & 1)) for f in range(1, N_DEV)]
    return 4 * x + 2 * y + c, peers


class _Gather:
    def __init__(self, lands, name, deps=(), all_devices=False):
        n = len(lands)
        self.name = name
        npeer = N_DEV - 1 if all_devices else N_CHIPS - 1

        def copies(refs, ss, rs):
            me, peers = _device_peers() if all_devices else _chip_peers()[:2]
            return [pltpu.make_async_remote_copy(
                src_ref=refs[w].at[me], dst_ref=refs[w].at[me], send_sem=ss.at[npeer * w + p],
                recv_sem=rs.at[npeer * w + p], device_id=peers[p], device_id_type=MESH)
                for w in range(n) for p in range(npeer)]

        def issue(refs, ss, rs):
            for cp in copies(refs, ss, rs):
                cp.start()

        def waits(refs, ss, rs):
            for cp in copies(refs, ss, rs):
                _wait_both(cp)

        self._waits = waits
        self.sems, self.arrays, self.token = _split_start(list(lands), npeer * n, issue, name=name + "_start",
                                                          deps=deps)

    def wait(self, after):
        return _split_wait(self.sems, self.arrays, after, self._waits, name=self.name + "_wait")


def _swap_halves_start(g4s, *, name):
    n = len(g4s)
    lands = [lax.empty((g.shape[0],) + g.shape[2:], g.dtype) for g in g4s]

    def copies(refs, ss, rs):
        _, _, sibling, c = _chip_peers()
        return [pltpu.make_async_remote_copy(
            src_ref=refs[w].at[:, 1 - c], dst_ref=refs[n + w], send_sem=ss.at[w], recv_sem=rs.at[w],
            device_id=sibling, device_id_type=MESH) for w in range(n)]

    def issue(refs, ss, rs):
        for cp in copies(refs, ss, rs):
            cp.start()

    def waits(refs, ss, rs):
        for cp in copies(refs, ss, rs):
            _wait_both(cp)

    sems, arrays, token = _split_start(list(g4s) + lands, n, issue, name=name + "_start")
    return sems, arrays, token, waits


def _scatter_start(parts, *, name, deps=()):
    n = len(parts)
    lands = [lax.empty((3,) + p.shape[1:], p.dtype) for p in parts]

    def copies(refs, ss, rs):
        me, peers, _, _ = _chip_peers()
        return [pltpu.make_async_remote_copy(
            src_ref=refs[w].at[me ^ CHIP_FLIPS[p]], dst_ref=refs[n + w].at[p],
            send_sem=ss.at[3 * w + p], recv_sem=rs.at[3 * w + p], device_id=peers[p], device_id_type=MESH)
            for w in range(n) for p in range(3)]

    def issue(refs, ss, rs):
        for cp in copies(refs, ss, rs):
            cp.start()

    def waits(refs, ss, rs):
        for cp in copies(refs, ss, rs):
            _wait_both(cp)

    sems, arrays, token = _split_start(list(parts) + lands, 3 * n, issue, name=name + "_start", deps=deps)
    return sems, arrays, token, waits


def _pair_share_start(fs, layer, *, name):
    n = len(fs)

    def copies(refs, ss, rs):
        _, _, sibling, c = _chip_peers()
        return [pltpu.make_async_remote_copy(
            src_ref=refs[w].at[layer, c], dst_ref=refs[w].at[layer, c], send_sem=ss.at[w], recv_sem=rs.at[w],
            device_id=sibling, device_id_type=MESH) for w in range(n)]

    def issue(refs, ss, rs):
        for cp in copies(refs, ss, rs):
            cp.start()

    def waits(refs, ss, rs):
        for cp in copies(refs, ss, rs):
            _wait_both(cp)

    sems, arrays, token = _split_start(list(fs), n, issue, name=name + "_start")
    return sems, arrays, token, waits


BIG = ('w_in', 'w_proj_a', 'w_proj_b', 'w_proj_c', 'w_out', 'w_up', 'w_down')
BIG_SHARD_AXIS = {'w_in': 2, 'w_proj_a': 2, 'w_proj_b': 2, 'w_proj_c': 2, 'w_out': 1, 'w_up': 2, 'w_down': 1}
SMALL = ('norm1', 'q_norm', 'k_norm', 'sinks', 'w_pool', 'pool_scale', 'sgu_v_norm', 'w_s', 'b_s', 'norm2',
         'conv_b', 'conv_w')
WEIGHTS = ('norm1', 'w_in', 'q_norm', 'k_norm', 'sinks', 'w_pool', 'pool_scale', 'sgu_v_norm', 'w_s', 'b_s',
           'w_proj_a', 'w_proj_b', 'w_proj_c', 'w_out', 'norm2', 'w_up', 'conv_w', 'conv_b', 'w_down')


def _rope_tables(positions):
    inv_freq = ROPE_THETA ** (-jnp.arange(0, HEAD_DIM, 2, dtype=F32) / HEAD_DIM)
    ang = positions.astype(F32)[:, None] * inv_freq
    cos, sin = jnp.cos(ang), jnp.sin(ang)
    c = jnp.concatenate([cos, cos], axis=1)
    s = jnp.concatenate([-sin, sin], axis=1)
    return jnp.concatenate([c, c], axis=1), jnp.concatenate([s, s], axis=1)


def _block_diag4(w):
    out = jnp.zeros((POOL_WIDTH, POOL_WIDTH), w.dtype)
    for g in range(4):
        out = lax.dynamic_update_slice(out, w[g], (g * HEAD_DIM, g * HEAD_DIM))
    return out


def _local_step(x, target, cos, sin, sp, sched):
    T = x.shape[0]
    tm1 = min(1024, T)
    tm = min(512, T)
    tr = min(1024, T)
    trc = min(512, T)
    tkt = min(2048, T)
    seg = _seg_matrix(256, HEAD_DIM)
    saved = []
    xl = x
    for l in range(DEPTH):
        p = f"l{l}_"
        c = dict(
            g1=sp['norm1'][l][None], g2=sp['norm2'][l][None],
            wbd=_block_diag4(sp['w_pool'][l]).astype(MXU_DTYPE), scale=sp['pool_scale'][l][None],
            gq=jnp.tile(sp['q_norm'][l], 4)[None], gk=jnp.tile(sp['k_norm'][l], 2)[None],
            sinks=jnp.broadcast_to(sp['sinks'][l][:, None], (N_Q_HEADS, 128)),
            wtril=jnp.tril(sp['w_s'][l]).astype(MXU_DTYPE),
            bexp=jnp.repeat(sp['b_s'][l].T, HEAD_DIM, axis=1), vn=jnp.tile(sp['sgu_v_norm'][l], 4)[None],
            cb=sp['conv_b'][l][None])
        c['w_in'] = sched.weight('w_in', l, xl)
        z, h1 = _norm_mm(xl, c['g1'], c['w_in'], tm=tm1, tn=1152, name=p + "in_proj",
                         deps=sched.start_tokens() if l == 0 else ())
        pa = _pool_fwd(z, c['wbd'], c['scale'], tr=tr, name=p + "pool")
        q, k, v = _qkv_prep(z, cos, sin, c['gq'], c['gk'], seg, tr=tr, name=p + "qkv_prep")
        at = _attn_fwd(q, k, v, c['sinks'], name=p + "attn")
        sg = _sgu_fwd(z, c['wtril'], c['bexp'], c['vn'], seg, tr=tr, name=p + "sgu")
        for n in ('w_proj_a', 'w_proj_b', 'w_proj_c', 'w_out'):
            c[n] = sched.weight(n, l, (pa, at, sg))
        merged, y3, x1 = _merge_fwd(pa, at, sg, c['w_proj_a'], c['w_proj_b'], c['w_proj_c'], z, xl, c['w_out'],
                                    tm=tm1, tn=512, name=p + "merge_out_proj")
        for n in ('w_up', 'conv_w', 'w_down'):
            c[n] = sched.weight(n, l, x1)
        up, h2 = _norm_mm(x1, c['g2'], c['w_up'], tm=tm1, tn=1408, name=p + "up_proj")
        act = _conv_act_fwd(up, c['conv_w'], c['cb'], tr=trc, tc=1408, name=p + "conv_act")
        saved.append(dict(c, x=xl, h1=h1, z=z, pa=pa, q=q, k=k, v=v, at=at, sg=sg, merged=merged, y3=y3,
                          x1=x1, h2=h2, up=up, act=act))
        if l < DEPTH - 1:
            xl = _mm(act, c['w_down'], mode='nn', add=x1, tm=tm, tn=D_MODEL, tk=D_FF, name=p + "down_proj")
        else:
            loss_row, dx, dxb = _down_proj_loss(act, c['w_down'], x1, target, tm=tm, name=p + "down_proj_loss")

    gs = {n: [None] * DEPTH for n in SMALL}
    for l in reversed(range(DEPTH)):
        p = f"l{l}_b_"
        s = saved[l]
        gb = {}
        dact = _mm(dxb, s['w_down'], mode='nt', tm=tm1, tn=1408, tk=D_MODEL, name=p + "down_dx")
        gb['w_down'] = _mm(s['act'], dxb, mode='tn', tm=1408, tn=D_MODEL, tk=tkt, name=p + "down_dw")
        toks = sched.slot(l, 'down', gb['w_down'])
        dup, dwg, dwv, dbg, dbv = _conv_act_bwd(s['up'], s['conv_w'], s['cb'], dact, tr=min(1024, T), tc=256,
                                                name=p + "conv_act", deps=toks)
        gs['conv_w'][l] = jnp.concatenate([dwg, dwv], axis=1)
        gs['conv_b'][l] = jnp.concatenate([dbg, dbv], axis=1)[0]
        toks = sched.slot(l, 'conv', dup)
        for half in range(2):
            gb['w_up'] = _mm(s['h2'], dup, mode='tn', b_lead=half, tm=D_MODEL, tn=1408, tk=tkt,
                             out_into=gb.get('w_up'), out_joff=2 * half, out_n=2 * D_FF, name=p + f"up_dw{half}",
                             deps=toks if half == 0 else ())
        toks = sched.slot(l, 'ffn', gb['w_up'], gb)
        dx1, dx1b, dg2 = _mm_nt_sharded_rms(dup, s['w_up'], s['x1'], s['g2'], dx, tm=tm,
                                            name=p + "up_dx_rms2", deps=toks)
        gs['norm2'][l] = dg2[0]
        gb['w_out'] = _mm(s['merged'], dx1b, mode='tn', tm=D_MODEL, tn=D_MODEL, tk=tkt, name=p + "out_dw")
        (dz, dpa, dat, dsg, gb['w_proj_a'], gb['w_proj_b'], gb['w_proj_c']) = _out_dx_merge_bwd(
            dx1b, s['w_out'], s['y3'], s['z'], [s['w_proj_a'], s['w_proj_b'], s['w_proj_c']],
            [s['pa'], s['at'], s['sg']], tm=tm1, tn=512, name=p + "out_dx_merge")
        toks = sched.slot(l, 'mid', dz)
        dq, dkc, dkp, dvc, dvp, dsk = _attn_bwd(s['q'], s['k'], s['v'], s['sinks'], dat, name=p + "attn", deps=toks)
        gs['sinks'][l] = dsk[:, 0]
        toks = sched.slot(l, 'attn', dq)
        dz, dgq, dgk, dwbd, dsc = _mixer_ab_bwd(s['z'], cos, sin, s['gq'], s['gk'], seg, dq, dkc, dkp, dvc, dvp,
                                                dpa, s['wbd'], s['scale'], dz, tr=tr, name=p + "qkv_pool", deps=toks)
        gs['q_norm'][l] = dgq[0, :HEAD_DIM]
        gs['k_norm'][l] = dgk[0, :HEAD_DIM]
        gs['w_pool'][l] = jnp.stack([dwbd[g * HEAD_DIM:(g + 1) * HEAD_DIM, g * HEAD_DIM:(g + 1) * HEAD_DIM]
                                     for g in range(4)])
        gs['pool_scale'][l] = dsc[0]
        dz, dws, dbrows, dvn = _sgu_bwd(s['z'], s['wtril'], s['bexp'], s['vn'], seg, dsg, dz, tr=tr, name=p + "sgu")
        gs['w_s'][l] = dws
        gs['b_s'][l] = dbrows[:, ::HEAD_DIM].T
        gs['sgu_v_norm'][l] = dvn[0, :HEAD_DIM]
        gb['w_in'] = _mm(s['h1'], dz, mode='tn', tm=D_MODEL, tn=1152, tk=tkt, name=p + "in_dw")
        toks = sched.slot(l, 'mix', gb['w_in'], gb)
        dx, dxb, dg1 = _mm_nt_sharded_rms(dz, s['w_in'], s['x'], s['g1'], dx1, tm=tm,
                                          name=p + "in_dx_rms1", deps=toks)
        gs['norm1'][l] = dg1[0]
    gs = {n: jnp.stack(v) for n, v in gs.items()}
    return loss_row, dx, gs


GROUP_F = ('w_down', 'w_up')
GROUP_M = ('w_out', 'w_proj_a', 'w_proj_b', 'w_proj_c', 'w_in')
ROW_SHARDED = ('w_out', 'w_down')

REDUCE_PLAN = {
    (1, 'ffn'): (('S1', 'F', 1),),
    (1, 'mid'): (('W1', 'F', 1),),
    (1, 'mix'): (('S1', 'M', 1),),
    (0, 'down'): (('W1', 'M', 1),),
    (0, 'conv'): (('W2', 'F', 1),),
    (0, 'ffn'): (('S1', 'F', 0), ('W3', 'F', 1)),
    (0, 'mid'): (('W1', 'F', 0),),
    (0, 'attn'): (('W2', 'M', 1),),
    (0, 'mix'): (('S1', 'M', 0), ('W3', 'M', 1)),
}
REDUCE_TAIL_A = (('W1', 'M', 0), ('W2', 'F', 0))
REDUCE_TAIL_B = (('W3', 'F', 0),)
REDUCE_TAIL_C = (('W2', 'M', 0), ('W3', 'M', 0))


class _Comm:
    def __init__(self, w, pos):
        self.pos = pos
        groups = {'a': [('w_in', 0)],
                  'b': [(n, 0) for n in ('w_proj_a', 'w_proj_b', 'w_proj_c', 'w_out')],
                  'c': [(n, 0) for n in ('w_up', 'conv_w', 'w_down')],
                  'd': [(n, 1) for n in BIG] + [('conv_w', 1)]}
        self.gathers, self.group_of, self.weights = {}, {}, {}
        self.tokens = []
        for g, ks in groups.items():
            lands = [_cast_place(w[n], pos, F32 if n == 'conv_w' else MXU_DTYPE, layer=l, name=f"gw_place_{n}{l}")
                     for n, l in ks]
            self.gathers[g] = (_Gather(lands, "gw_" + g, deps=self.tokens[-1:]), ks)
            self.tokens.append(self.gathers[g][0].token)
            self.group_of.update({k: g for k in ks})
        self.red = {}
        self.final = {}

    def start_tokens(self):
        return self.tokens[-1:]

    def weight(self, name, layer, after):
        if (name, layer) not in self.weights:
            gather, ks = self.gathers[self.group_of[(name, layer)]]
            for (n, l), full in zip(ks, gather.wait(after)):
                if n == 'conv_w' or n.startswith('w_proj'):
                    full = full.transpose(1, 0, 2).reshape(full.shape[1], -1)
                elif n in ROW_SHARDED:
                    full = full.reshape(-1, full.shape[2])
                self.weights[(n, l)] = full
        return self.weights[(name, layer)]

    def slot(self, layer, slot, after, grads=None):
        tokens = []
        for step, grp, lyr in REDUCE_PLAN.get((layer, slot), ()):
            tok = self._step(step, grp, lyr, after, grads)
            if tok is not None:
                tokens.append(tok)
        return tokens

    def tail(self, steps, after, deps=()):
        toks = (self._step(step, grp, lyr, after, None, deps) for step, grp, lyr in steps)
        return [t for t in toks if t is not None]

    def shards(self):
        return {n: f.reshape(DEPTH, 2 * f.shape[2], f.shape[3]) for n, f in self.final.items()}

    def _step(self, step, grp, layer, after, grads, deps=()):
        names = GROUP_F if grp == 'F' else GROUP_M
        tag = f"{grp.lower()}{layer}"
        st = self.red.setdefault((grp, layer), {})
        n = len(names)
        if step == 'S1':
            g4s = []
            for nm in names:
                g = grads[nm]
                R, C = g.shape
                g4s.append(g.reshape(N_CHIPS, 2, R // (2 * N_CHIPS), C) if nm in ROW_SHARDED
                           else g.reshape(1, 2, R // 2, C))
            st['s1'] = _swap_halves_start(g4s, name="rs1_" + tag)
            return st['s1'][2]
        if step == 'W1':
            sems, arrays, _, waits = st.pop('s1')
            arrays = _split_wait(sems, arrays, after, waits, name=f"rs1_{tag}_wait")
            parts = [_pair_add(arrays[i], arrays[n + i], self.pos, name=f"pair_add_{tag}_{names[i]}")
                     for i in range(n)]
            st['s2'] = _scatter_start(parts, name="rs2_" + tag, deps=deps)
            return st['s2'][2]
        if step == 'W2':
            sems, arrays, _, waits = st.pop('s2')
            arrays = _split_wait(sems, arrays, after, waits, name=f"rs2_{tag}_wait")
            fs = [_chip_sum(arrays[i], arrays[n + i], self.final.get(names[i]), self.pos, layer,
                            name=f"chip_sum_{tag}_{names[i]}") for i in range(n)]
            st['s3'] = _pair_share_start(fs, layer, name="rs3_" + tag)
            return st['s3'][2]
        sems, arrays, _, waits = st.pop('s3')
        self.final.update(zip(names, _split_wait(sems, arrays, after, waits, name=f"rs3_{tag}_wait")))
        return None


def _pack(arrays):
    rows = []
    for a in arrays:
        nel = int(np.prod(a.shape))
        if nel % 1024 == 0:
            rows.append(a.astype(F32).reshape(nel // 128, 128))
        else:
            f = a.reshape(-1).astype(F32)
            rows.append(jnp.pad(f, (0, (-nel) % 1024)).reshape(-1, 128))
    return jnp.concatenate(rows, axis=0)


def _unpack(pack, shapes):
    out, row = [], 0
    for shp in shapes:
        nel = int(np.prod(shp))
        nrow = 8 * -(-nel // 1024)
        part = pack[row:row + nrow]
        out.append(part.reshape(shp) if nel % 1024 == 0 else part.reshape(-1)[:nel].reshape(shp))
        row += nrow
    return out


def kernel(x, positions, norm1, w_in, q_norm, k_norm, sinks, w_pool, pool_scale, sgu_v_norm, w_s, b_s, w_proj_a, w_proj_b, w_proj_c, w_out, norm2, w_up, conv_w, conv_b, w_down, loss_target, m_norm1, m_w_in, m_q_norm, m_k_norm, m_sinks, m_w_pool, m_pool_scale, m_sgu_v_norm, m_w_s, m_b_s, m_w_proj_a, m_w_proj_b, m_w_proj_c, m_w_out, m_norm2, m_w_up, m_conv_w, m_conv_b, m_w_down, v_norm1, v_w_in, v_q_norm, v_k_norm, v_sinks, v_w_pool, v_pool_scale, v_sgu_v_norm, v_w_s, v_b_s, v_w_proj_a, v_w_proj_b, v_w_proj_c, v_w_out, v_norm2, v_w_up, v_conv_w, v_conv_b, v_w_down):
    w = dict(norm1=norm1, w_in=w_in, q_norm=q_norm, k_norm=k_norm, sinks=sinks, w_pool=w_pool, pool_scale=pool_scale,
             sgu_v_norm=sgu_v_norm, w_s=w_s, b_s=b_s, w_proj_a=w_proj_a, w_proj_b=w_proj_b, w_proj_c=w_proj_c,
             w_out=w_out, norm2=norm2, w_up=w_up, conv_w=conv_w, conv_b=conv_b, w_down=w_down)
    m = dict(norm1=m_norm1, w_in=m_w_in, q_norm=m_q_norm, k_norm=m_k_norm, sinks=m_sinks, w_pool=m_w_pool,
             pool_scale=m_pool_scale, sgu_v_norm=m_sgu_v_norm, w_s=m_w_s, b_s=m_b_s, w_proj_a=m_w_proj_a,
             w_proj_b=m_w_proj_b, w_proj_c=m_w_proj_c, w_out=m_w_out, norm2=m_norm2, w_up=m_w_up, conv_w=m_conv_w,
             conv_b=m_conv_b, w_down=m_w_down)
    v = dict(norm1=v_norm1, w_in=v_w_in, q_norm=v_q_norm, k_norm=v_k_norm, sinks=v_sinks, w_pool=v_w_pool,
             pool_scale=v_pool_scale, sgu_v_norm=v_sgu_v_norm, w_s=v_w_s, b_s=v_b_s, w_proj_a=v_w_proj_a,
             w_proj_b=v_w_proj_b, w_proj_c=v_w_proj_c, w_out=v_w_out, norm2=v_norm2, w_up=v_w_up, conv_w=v_conv_w,
             conv_b=v_conv_b, w_down=v_w_down)
    chip = 2 * lax.axis_index("x") + lax.axis_index("y")
    core = lax.axis_index("c")

    pos = jnp.stack([chip, core, 2 * chip + core]).astype(jnp.int32)
    comm = _Comm(w, pos)

    cos, sin = _rope_tables(positions[0])
    sp = {n: w[n] for n in SMALL if n != 'conv_w'}
    loss_row, dx, gs = _local_step(x[0], loss_target[0], cos, sin, sp, comm)

    delta, new_m, new_v, grad_out = {}, {}, {}, {}

    def adamw_big(names, grads):
        for n in names:
            shp = w[n].shape
            two_d = lambda a: a.reshape(shp[0] * shp[1], shp[2])
            d, nm, nv, g = _adamw(two_d(w[n]), two_d(grads[n]), two_d(m[n]), two_d(v[n]),
                                  tr=_row_tile(shp[0] * shp[1], 256), name=f"adamw_{n}", copy_g=True)
            delta[n], new_m[n], new_v[n], grad_out[n] = d.reshape(shp), nm.reshape(shp), nv.reshape(shp), g.reshape(shp)

    small_shapes = [gs[n].shape for n in SMALL] + [(1,)]
    small_pack = _pack([gs[n] for n in SMALL] + [loss_row[0, :1]])
    small = _Gather([_cast_place(small_pack, pos, F32, slots=N_DEV, which=2, name="small_place")], "small_gather",
                    all_devices=True)
    toks = comm.tail(REDUCE_TAIL_A[:1], (dx, small.token))
    comm.tail(REDUCE_TAIL_A[1:], (dx, *toks))
    comm.tail(REDUCE_TAIL_B, dx)
    adamw_big(GROUP_F, comm.shards())
    red = _sum_slots(small.wait(new_v[GROUP_F[-1]])[0], tr=small_pack.shape[0], name="small_sum")
    *small_grads, loss = _unpack(red, small_shapes)
    g_small = dict(zip(SMALL, small_grads))
    comm.tail(REDUCE_TAIL_C, red)
    grads = comm.shards()
    grads.update(g_small)
    shard_cols = conv_w.shape[2]
    grads['conv_w'] = lax.dynamic_slice_in_dim(g_small['conv_w'], chip * shard_cols, shard_cols, axis=2)

    adamw_big(GROUP_M, grads)
    shapes = [w[n].shape for n in SMALL]
    packs = [_pack([src[n] for n in SMALL]) for src in (w, grads, m, v)]
    d, nm, nv = _adamw(*packs, tr=packs[0].shape[0], name="adamw_small")
    for dst, src in ((delta, d), (new_m, nm), (new_v, nv)):
        dst.update(zip(SMALL, _unpack(src, shapes)))

    grads.update(grad_out)
    return (loss[0], dx[None], *[grads[n] for n in WEIGHTS], *[delta[n] for n in WEIGHTS],
            *[new_m[n] for n in WEIGHTS], *[new_v[n] for n in WEIGHTS])
```

```python
import functools
import math

import numpy as np
import jax
import jax.numpy as jnp
from jax import lax
from jax.experimental import pallas as pl
from jax.experimental.pallas import tpu as pltpu

F32 = jnp.float32
MXU_DTYPE = jnp.bfloat16
COMM_DTYPE = jnp.bfloat16
ACT_DTYPE = jnp.bfloat16
HALO = 16

D_MODEL = 1024
DEPTH = 2
HEAD_DIM = 64
POOL_WINDOWS = (2, 4, 8, 16)
POOL_WIDTH = 256
N_Q_HEADS = 8
ATTN_BLOCK = 128
ATTN_WIDTH = 512
KV_WIDTH = 128
CHUNK = 128
SGU_WIDTH = 256
IN_COLS = 4608
GATE_COL0 = 1536
D_FF = 2816
ROPE_THETA = 10000.0
EPS = 1e-6
ADAM_LR, ADAM_B1, ADAM_B2, ADAM_EPS, ADAM_WD, ADAM_STEP = 0.001, 0.9, 0.999, 1e-08, 0.01, 10

N_CHIPS = 4
N_DEV = 8
VMEM_LIMIT_BYTES = 56 * 1024 * 1024
NEG_BIG = -1e30
MESH = pl.DeviceIdType.MESH
ANY = pl.BlockSpec(memory_space=pl.ANY)

SDS = jax.ShapeDtypeStruct


def _cp(*sem):
    return pltpu.CompilerParams(dimension_semantics=sem, vmem_limit_bytes=VMEM_LIMIT_BYTES)


def _dot(a, b, dims=((1,), (0,))):
    return lax.dot_general(a.astype(MXU_DTYPE), b.astype(MXU_DTYPE), (dims, ((), ())),
                           preferred_element_type=F32)


NT = ((1,), (1,))
TN = ((0,), (0,))


def _split_dot(x, m):
    hi = x.astype(MXU_DTYPE)
    lo = (x - hi.astype(F32)).astype(MXU_DTYPE)
    return _dot(hi, m) + _dot(lo, m)


def _seg_matrix(width, seg):
    idx = np.arange(width) // seg
    return jnp.asarray((idx[:, None] == idx[None, :]).astype(np.float32), dtype=MXU_DTYPE)


def _lane(shape):
    return lax.broadcasted_iota(jnp.int32, shape, len(shape) - 1)


def _row(shape):
    return lax.broadcasted_iota(jnp.int32, shape, 0)


def _full(shape):
    nd = len(shape)
    return pl.BlockSpec(shape, lambda *_: (0,) * nd)


def _gelu(x):
    k = math.sqrt(2.0 / math.pi)
    th = jnp.tanh(k * (x + 0.044715 * (x * x * x)))
    return 0.5 * x * (1.0 + th)


def _gelu_and_grad(x):
    k = math.sqrt(2.0 / math.pi)
    x2 = x * x
    th = jnp.tanh(k * (x + 0.044715 * (x2 * x)))
    g = 0.5 * x * (1.0 + th)
    dg = 0.5 * (1.0 + th) + 0.5 * x * (1.0 - th * th) * (k * (1.0 + 3.0 * 0.044715 * x2))
    return g, dg


def _sigmoid(x):
    return 0.5 * jnp.tanh(0.5 * x) + 0.5


def _swap_halves(x):
    w = x.shape[-1]
    first = (_lane(x.shape) % HEAD_DIM) < (HEAD_DIM // 2)
    return jnp.where(first, pltpu.roll(x, w - HEAD_DIM // 2, 1), pltpu.roll(x, HEAD_DIM // 2, 1))


def _tile_lanes(x, reps):
    return x if reps == 1 else jnp.concatenate([x] * reps, axis=1)


def _fold_lanes(x, period):
    w = x.shape[-1]
    while w > period:
        w //= 2
        x = x + pltpu.roll(x, w, 1)
    return x


def _mm(a, b, *, mode, tm, tn, tk, out_dtype=F32, add=None, name,
        a_lead=None, b_lead=None, b_sharded=False, out_into=None,
        b_koff=0, out_joff=0, out_n=None, deps=()):
    ash = a.shape[1:] if a_lead is not None else a.shape
    bsh = b.shape[1:] if b_lead is not None else b.shape
    if b_sharded:
        bsh = (b.shape[1], N_CHIPS * b.shape[2])
    if mode == 'nn':
        (M, K), (K2, N) = ash, bsh
    elif mode == 'nt':
        (M, K), (N, K2) = ash, bsh
    else:
        (K, M), (K2, N) = ash, bsh
    assert K == K2 or (mode == 'nt' and K2 > K), (ash, bsh, mode)
    assert M % tm == 0 and N % tn == 0 and K % tk == 0, (M, N, K, tm, tn, tk)
    nk = K // tk
    dims = {'nn': ((1,), (0,)), 'nt': NT, 'tn': TN}[mode]

    def lead(spec_shape, imap, lead_idx):
        if lead_idx is None:
            return pl.BlockSpec(spec_shape, imap)
        return pl.BlockSpec((None,) + spec_shape, lambda i, j, k: (lead_idx,) + imap(i, j, k))

    if mode == 'tn':
        a_spec = lead((tk, tm), lambda i, j, k: (k, i), a_lead)
    else:
        a_spec = lead((tm, tk), lambda i, j, k: (i, k), a_lead)
    if b_sharded:
        per = b.shape[2] // (tk if mode == 'nt' else tn)
        assert per * (tk if mode == 'nt' else tn) == b.shape[2] and mode != 'tn'
        if mode == 'nt':
            b_spec = pl.BlockSpec((None, tn, tk), lambda i, j, k: ((k + b_koff) // per, j, (k + b_koff) % per))
        else:
            b_spec = pl.BlockSpec((None, tk, tn), lambda i, j, k: (j // per, k, j % per))
    elif mode == 'nt':
        b_spec = lead((tn, tk), lambda i, j, k: (j, k + b_koff), b_lead)
    else:
        b_spec = lead((tk, tn), lambda i, j, k: (k, j), b_lead)
    o_spec = pl.BlockSpec((tm, tn), lambda i, j, k: (i, j + out_joff))
    n_out = N if out_n is None else out_n
    in_specs = [a_spec, b_spec]
    operands = [a, b]
    if add is not None:
        in_specs.append(pl.BlockSpec((tm, tn), lambda i, j, k: (i, j)))
        operands.append(add)
    aliases = {}
    if out_into is not None:
        in_specs.append(ANY)
        operands.append(out_into)
        aliases = {len(operands) - 1: 0}
    in_specs += [ANY] * len(deps)
    operands += list(deps)
    has_add = add is not None
    acc_in_out = nk > 1 and out_dtype == F32

    def body(*refs):
        a_ref, b_ref = refs[0], refs[1]
        pos = 2
        add_ref = None
        if has_add:
            add_ref = refs[pos]
            pos += 1
        if out_into is not None:
            pos += 1
        pos += len(deps)
        o_ref = refs[pos]
        acc_ref = refs[pos + 1] if (nk > 1 and not acc_in_out) else None
        p = _dot(a_ref[...], b_ref[...], dims)
        if nk == 1:
            if has_add:
                p = p + add_ref[...]
            o_ref[...] = p.astype(o_ref.dtype)
            return
        k = pl.program_id(2)
        tgt = o_ref if acc_in_out else acc_ref

        @pl.when(k == 0)
        def _():
            tgt[...] = p + add_ref[...] if has_add else p

        @pl.when(k > 0)
        def _():
            tgt[...] += p

        if not acc_in_out:
            @pl.when(k == nk - 1)
            def _():
                o_ref[...] = acc_ref[...].astype(o_ref.dtype)

    out_shape = SDS((M, n_out), out_dtype)
    scratch = [pltpu.VMEM((tm, tn), F32)] if (nk > 1 and not acc_in_out) else []
    return pl.pallas_call(
        body, grid=(M // tm, N // tn, nk), in_specs=in_specs, out_specs=o_spec, out_shape=out_shape,
        scratch_shapes=scratch, input_output_aliases=aliases, name=name,
        compiler_params=_cp("parallel", "parallel", "arbitrary"))(*operands)


def _rms_bwd_rows(xv, g, dh, dres):
    r = lax.rsqrt(jnp.mean(xv * xv, axis=-1, keepdims=True) + EPS)
    xh = xv * r
    gy = dh * g
    dx = r * (gy - xh * jnp.mean(xh * gy, axis=-1, keepdims=True)) + dres
    return dx, jnp.sum(dh * xh, axis=0, keepdims=True)


def _mm_nt_sharded_rms(a, b, x, g, dres, *, tm, name, deps=()):
    a3 = a if a.ndim == 3 else a[None]
    A, M, ka = a3.shape
    S, N, ns = b.shape
    per = S // A
    assert ka == per * ns and M % tm == 0 and N == x.shape[1], (a3.shape, b.shape, x.shape)

    def body(a_ref, b_ref, x_ref, g_ref, dres_ref, dx_ref, dxb_ref, dg_ref):
        acc = None
        for s in range(S):
            lo = (s % per) * ns
            p = _dot(a_ref[s // per, :, lo:lo + ns], b_ref[s], NT)
            acc = p if acc is None else acc + p
        dx, dg = _rms_bwd_rows(x_ref[...], g_ref[...], acc, dres_ref[...])
        dx_ref[...] = dx
        dxb_ref[...] = dx.astype(dxb_ref.dtype)

        @pl.when(pl.program_id(0) == 0)
        def _():
            dg_ref[...] = jnp.zeros_like(dg_ref)
        dg_ref[...] += dg

    rows = pl.BlockSpec((tm, N), lambda i: (i, 0))
    return pl.pallas_call(
        _after(body, 5, deps), grid=(M // tm,),
        in_specs=[pl.BlockSpec((A, tm, ka), lambda i: (0, i, 0)),
                  pl.BlockSpec((S, N, ns), lambda i: (0, 0, 0), pipeline_mode=pl.Buffered(1)),
                  rows, _full((1, N)), rows] + [ANY] * len(deps),
        out_specs=[rows, rows, _full((1, N))],
        out_shape=[SDS((M, N), F32), SDS((M, N), MXU_DTYPE), SDS((1, N), F32)], name=name,
        compiler_params=_cp("arbitrary"))(a3, b, x, g, dres, *deps)


def _norm_mm(x, g, b, *, tm, tn, name, deps=()):
    M, K = x.shape
    S, K2, ns = b.shape
    per = ns // tn
    assert K == K2 and per * tn == ns and M % tm == 0, (x.shape, b.shape)

    def body(x_ref, g_ref, b_ref, o_ref, h_ref):
        @pl.when(pl.program_id(1) == 0)
        def _():
            xv = x_ref[...]
            r = lax.rsqrt(jnp.mean(xv * xv, axis=-1, keepdims=True) + EPS)
            h_ref[...] = (xv * r * g_ref[...]).astype(h_ref.dtype)
        o_ref[...] = _dot(h_ref[...], b_ref[...]).astype(o_ref.dtype)

    return pl.pallas_call(
        _after(body, 3, deps), grid=(M // tm, S * per),
        in_specs=[pl.BlockSpec((tm, K), lambda i, j: (i, 0)), _full((1, K)),
                  pl.BlockSpec((None, K, tn), lambda i, j: (j // per, 0, j % per))] + [ANY] * len(deps),
        out_specs=[pl.BlockSpec((tm, tn), lambda i, j: (i, j)), pl.BlockSpec((tm, K), lambda i, j: (i, 0))],
        out_shape=[SDS((M, S * ns), ACT_DTYPE), SDS((M, K), MXU_DTYPE)], name=name,
        compiler_params=_cp("parallel", "arbitrary"))(x, g, b, *deps)


def _after(body, n_in, deps):
    nd = len(deps)
    if nd == 0:
        return body
    return lambda *refs: body(*refs[:n_in], *refs[n_in + nd:])


def _down_proj_loss(act, w, x1, target, *, tm, name):
    T, K = act.shape
    D = w.shape[1]

    def body(a_ref, w_ref, x_ref, t_ref, loss_ref, dy_ref, dyb_ref):
        i = pl.program_id(0)
        d = (x_ref[...] + _dot(a_ref[...], w_ref[...])) - t_ref[...]
        dy = d * (1.0 / D)
        dy_ref[...] = dy
        dyb_ref[...] = dy.astype(dyb_ref.dtype)
        part = jnp.sum(jnp.sum(d * d, axis=1, keepdims=True), axis=0, keepdims=True) * (0.5 / D)

        @pl.when(i == 0)
        def _():
            loss_ref[...] = jnp.zeros_like(loss_ref)
        loss_ref[...] += jnp.broadcast_to(part, loss_ref.shape)

    rows = pl.BlockSpec((tm, D), lambda i: (i, 0))
    return pl.pallas_call(
        body, grid=(T // tm,), in_specs=[pl.BlockSpec((tm, K), lambda i: (i, 0)), _full((K, D)), rows, rows],
        out_specs=[_full((1, 128)), rows, rows],
        out_shape=[SDS((1, 128), F32), SDS((T, D), F32), SDS((T, D), MXU_DTYPE)],
        name=name, compiler_params=_cp("arbitrary"))(act, w, x1, target)


def _pool_lane_consts(shape):
    lane = _lane(shape)
    grp = lane // (POOL_WIDTH // 4)
    win = jnp.where(grp == 0, 2, jnp.where(grp == 1, 4, jnp.where(grp == 2, 8, 16)))
    return grp, win


def _pool_select(grp, s2, s4, s8, s16):
    return jnp.where(grp == 0, s2, jnp.where(grp == 1, s4, jnp.where(grp == 2, s8, s16)))


def _pool_diff(xe, row0, tr):
    s2 = xe + pltpu.roll(xe, 1, 0)
    s4 = s2 + pltpu.roll(s2, 2, 0)
    s8 = s4 + pltpu.roll(s4, 4, 0)
    s16 = s8 + pltpu.roll(s8, 8, 0)
    shape = (tr, POOL_WIDTH)
    grp, win = _pool_lane_consts(shape)
    sums = _pool_select(grp, s2[16:], s4[16:], s8[16:], s16[16:])
    t = row0 + _row(shape)
    cnt = jnp.minimum(t + 1, win).astype(F32)
    return sums / cnt - xe[16:]


def _pool_fwd(z, wbd, scale, *, tr, name):
    T = z.shape[0]
    hb = tr // 16

    def body(x_ref, xp_ref, w_ref, s_ref, o_ref):
        i = pl.program_id(0)
        halo = jnp.where(i == 0, 0.0, xp_ref[...].astype(F32))
        diff = _pool_diff(jnp.concatenate([halo, x_ref[...].astype(F32)], axis=0), i * tr, tr)
        o_ref[...] = (_dot(diff, w_ref[...]) * s_ref[...]).astype(o_ref.dtype)

    return pl.pallas_call(
        body, grid=(T // tr,),
        in_specs=[pl.BlockSpec((tr, POOL_WIDTH), lambda i: (i, 0)),
                  pl.BlockSpec((16, POOL_WIDTH), lambda i: (jnp.maximum(i * hb - 1, 0), 0)),
                  _full((POOL_WIDTH, POOL_WIDTH)), _full((1, POOL_WIDTH))],
        out_specs=pl.BlockSpec((tr, POOL_WIDTH), lambda i: (i, 0)),
        out_shape=SDS((T, POOL_WIDTH), MXU_DTYPE), name=name, compiler_params=_cp("parallel"))(z, z, wbd, scale)


def _pool_bwd_tile(i, n, tr, x, xprev, dpa, dpa_next, wbd, scale):
    halo = jnp.where(i == 0, 0.0, xprev)
    diff = _pool_diff(jnp.concatenate([halo, x], axis=0), i * tr, tr)
    mixed = _dot(diff, wbd)
    dscale = jnp.sum(dpa * mixed, axis=0, keepdims=True)
    dnext = jnp.where(i == n - 1, 0.0, dpa_next)
    dmix_e = jnp.concatenate([dpa, dnext], axis=0) * scale
    ddiff_e = _dot(dmix_e, wbd, NT)
    dwbd = _dot(diff, dmix_e[:tr], TN)
    shape = (tr + 16, POOL_WIDTH)
    grp, win = _pool_lane_consts(shape)
    t = i * tr + _row(shape)
    e = ddiff_e / jnp.minimum(t + 1, win).astype(F32)
    nrow = tr + 16
    a2 = e + pltpu.roll(e, nrow - 1, 0)
    a4 = a2 + pltpu.roll(a2, nrow - 2, 0)
    a8 = a4 + pltpu.roll(a4, nrow - 4, 0)
    a16 = a8 + pltpu.roll(a8, nrow - 8, 0)
    dx = _pool_select(grp, a2, a4, a8, a16)[:tr] - ddiff_e[:tr]
    return dx, dwbd, dscale


def _norm_rope(x, g, cos, sin_signed, seg):
    reps = x.shape[1] // 128
    ms = _split_dot(x * x, seg) * (1.0 / HEAD_DIM)
    r = lax.rsqrt(ms + EPS)
    xn = x * r * g
    c, s = _tile_lanes(cos, reps), _tile_lanes(sin_signed, reps)
    return xn * c + _swap_halves(xn) * s


def _norm_rope_bwd(x, g, cos, sin_signed, seg, dout):
    reps = x.shape[1] // 128
    c, s = _tile_lanes(cos, reps), _tile_lanes(sin_signed, reps)
    dxn = dout * c + _swap_halves(dout * s)
    ms = _split_dot(x * x, seg) * (1.0 / HEAD_DIM)
    r = lax.rsqrt(ms + EPS)
    xh = x * r
    gy = dxn * g
    dx = r * (gy - xh * (_split_dot(xh * gy, seg) * (1.0 / HEAD_DIM)))
    dg = jnp.sum(dxn * xh, axis=0, keepdims=True)
    return dx, dg


def _dup_heads(k):
    first = _lane(k.shape) < HEAD_DIM
    kr = pltpu.roll(k, HEAD_DIM, 1)
    return jnp.concatenate([jnp.where(first, k, kr), jnp.where(first, kr, k)], axis=1)


def _qkv_prep(z, cos, sin_signed, gq, gk, seg, *, tr, name):
    T = z.shape[0]

    def body(qa_ref, qb_ref, kv_ref, c_ref, s_ref, gq_ref, gk_ref, seg_ref, q_ref, k_ref, v_ref):
        c, s, seg_m = c_ref[...], s_ref[...], seg_ref[...]
        scale = HEAD_DIM ** -0.5
        qa = _norm_rope(qa_ref[...].astype(F32), gq_ref[...], c, s, seg_m) * scale
        qb = _norm_rope(qb_ref[...].astype(F32), gq_ref[...], c, s, seg_m) * scale
        q_ref[...] = jnp.concatenate([qa, qb], axis=1).astype(q_ref.dtype)
        kv = kv_ref[...].astype(F32)
        k = _norm_rope(kv[:, :KV_WIDTH], gk_ref[...], c, s, seg_m[:128, :128])
        k_ref[...] = _dup_heads(k).astype(k_ref.dtype)
        v_ref[...] = _dup_heads(kv[:, KV_WIDTH:]).astype(v_ref.dtype)

    col = lambda j: pl.BlockSpec((tr, 256), lambda i: (i, j))
    tab = pl.BlockSpec((tr, 128), lambda i: (i, 0))
    return pl.pallas_call(
        body, grid=(T // tr,),
        in_specs=[col(1), col(2), col(3), tab, tab, _full((1, 256)), _full((1, 128)), _full((256, 256))],
        out_specs=[pl.BlockSpec((tr, 512), lambda i: (i, 0)), col(0), col(0)],
        out_shape=[SDS((T, 512), MXU_DTYPE), SDS((T, 256), MXU_DTYPE), SDS((T, 256), MXU_DTYPE)],
        name=name, compiler_params=_cp("parallel"))(z, z, z, cos, sin_signed, gq, gk, seg)


GROUP_HEADS = 4
GROUP_ROWS = GROUP_HEADS * ATTN_BLOCK
ALL_ROWS = N_Q_HEADS * ATTN_BLOCK


def _attn_mask(has_prev):
    qi = _row((ALL_ROWS, 2 * ATTN_BLOCK)) % ATTN_BLOCK
    kj = _lane((ALL_ROWS, 2 * ATTN_BLOCK))
    return (kj > qi) & (kj <= qi + ATTN_BLOCK) & ((kj >= ATTN_BLOCK) | has_prev)


FWD_STEP_BLOCKS = 8
BWD_STEP_BLOCKS = 2


def _band(prev, cur, blk):
    lo = cur[(blk - 1) * ATTN_BLOCK:blk * ATTN_BLOCK] if blk else prev
    return jnp.concatenate([lo, cur[blk * ATTN_BLOCK:(blk + 1) * ATTN_BLOCK]], axis=0)


def _stack_heads(x, g):
    first = _lane((ATTN_BLOCK, 128)) < HEAD_DIM
    parts = []
    for pair in (2 * g, 2 * g + 1):
        x128 = x[:, 128 * pair:128 * (pair + 1)]
        zero = jnp.zeros_like(x128)
        parts += [jnp.where(first, x128, zero), jnp.where(first, zero, x128)]
    return jnp.concatenate(parts, axis=0)


def _unstack_heads(y):
    first = _lane((ATTN_BLOCK, 128)) < HEAD_DIM
    b = ATTN_BLOCK
    return jnp.concatenate([jnp.where(first, y[0:b], y[b:2 * b]), jnp.where(first, y[2 * b:3 * b], y[3 * b:4 * b])],
                           axis=1)


def _sink_col(sk_ref):
    return jnp.concatenate([jnp.broadcast_to(sk_ref[h:h + 1, 0:1], (ATTN_BLOCK, 1)) for h in range(N_Q_HEADS)],
                           axis=0)


def _by_group(a8, b2, dims=((1,), (0,))):
    return jnp.concatenate([_dot(a8[:GROUP_ROWS], b2[:, :128], dims), _dot(a8[GROUP_ROWS:], b2[:, 128:], dims)],
                           axis=0)


def _softmax_exp(q8, k2, mask, sink):
    s = jnp.where(mask, _by_group(q8, k2, NT), NEG_BIG)
    m = jnp.maximum(jnp.max(s, axis=1, keepdims=True), sink)
    p = jnp.exp(s - m)
    ps = jnp.exp(sink - m)
    return p, ps, 1.0 / (jnp.sum(p, axis=1, keepdims=True) + ps)


def _attn_fwd(q, k, v, sinks_b, *, name):
    T = q.shape[0]
    nb = T // ATTN_BLOCK
    STEP_BLOCKS = min(FWD_STEP_BLOCKS, nb)
    STEP_ROWS = STEP_BLOCKS * ATTN_BLOCK

    def body(q_ref, kc_ref, kp_ref, vc_ref, vp_ref, sk_ref, o_ref):
        n = pl.program_id(0)
        kc, kp, vc, vp = kc_ref[...], kp_ref[...], vc_ref[...], vp_ref[...]
        sink = _sink_col(sk_ref)
        for blk in range(STEP_BLOCKS):
            rows = slice(blk * ATTN_BLOCK, (blk + 1) * ATTN_BLOCK)
            mask = _attn_mask((n > 0) if blk == 0 else True)
            k2, v2 = _band(kp, kc, blk), _band(vp, vc, blk)
            qv = q_ref[rows, :]
            q8 = jnp.concatenate([_stack_heads(qv, 0), _stack_heads(qv, 1)], axis=0)
            p, _, inv = _softmax_exp(q8, k2, mask, sink)
            o8 = _by_group(p, v2) * inv
            o_ref[rows, :] = jnp.concatenate([_unstack_heads(o8[:GROUP_ROWS]), _unstack_heads(o8[GROUP_ROWS:])],
                                             axis=1).astype(o_ref.dtype)

    cur = lambda w: pl.BlockSpec((STEP_ROWS, w), lambda n: (n, 0))
    prev = lambda w: pl.BlockSpec((ATTN_BLOCK, w), lambda n: (jnp.maximum(STEP_BLOCKS * n - 1, 0), 0))
    return pl.pallas_call(
        body, grid=(nb // STEP_BLOCKS,),
        in_specs=[cur(512), cur(256), prev(256), cur(256), prev(256), _full((8, 128))],
        out_specs=cur(512), out_shape=SDS((T, 512), MXU_DTYPE), name=name,
        compiler_params=_cp("parallel"))(q, k, k, v, v, sinks_b)


def _attn_bwd(q, k, v, sinks_b, do, *, name, deps=()):
    T = q.shape[0]
    nb = T // ATTN_BLOCK
    STEP_BLOCKS = min(BWD_STEP_BLOCKS, nb)
    STEP_ROWS = STEP_BLOCKS * ATTN_BLOCK

    def body(q_ref, kc_ref, kp_ref, vc_ref, vp_ref, sk_ref, do_ref,
             dq_ref, dkc_ref, dkp_ref, dvc_ref, dvp_ref, dsk_ref):
        n = pl.program_id(0)
        kc, kp, vc, vp = kc_ref[...], kp_ref[...], vc_ref[...], vp_ref[...]
        sink = _sink_col(sk_ref)

        @pl.when(n == 0)
        def _():
            dsk_ref[...] = jnp.zeros_like(dsk_ref)

        for blk in range(STEP_BLOCKS):
            rows = slice(blk * ATTN_BLOCK, (blk + 1) * ATTN_BLOCK)
            mask = _attn_mask((n > 0) if blk == 0 else True)
            k2, v2 = _band(kp, kc, blk), _band(vp, vc, blk)
            qv, dov = q_ref[rows, :], do_ref[rows, :]
            q8 = jnp.concatenate([_stack_heads(qv, 0), _stack_heads(qv, 1)], axis=0)
            do8 = jnp.concatenate([_stack_heads(dov, 0), _stack_heads(dov, 1)], axis=0)
            p, ps, inv = _softmax_exp(q8, k2, mask, sink)
            pn = p * inv
            delta = jnp.sum(do8 * _by_group(pn, v2), axis=1, keepdims=True)
            ds = pn * (_by_group(do8, v2, NT) - delta)
            dq8 = _by_group(ds, k2)
            dq_ref[rows, :] = jnp.concatenate([_unstack_heads(dq8[:GROUP_ROWS]), _unstack_heads(dq8[GROUP_ROWS:])],
                                              axis=1)
            dk = jnp.concatenate([_dot(ds[:GROUP_ROWS], q8[:GROUP_ROWS], TN),
                                  _dot(ds[GROUP_ROWS:], q8[GROUP_ROWS:], TN)], axis=1)
            dv = jnp.concatenate([_dot(pn[:GROUP_ROWS], do8[:GROUP_ROWS], TN),
                                  _dot(pn[GROUP_ROWS:], do8[GROUP_ROWS:], TN)], axis=1)
            wsink = (ps * inv) * delta
            for h in range(N_Q_HEADS):
                dsink = -jnp.sum(wsink[ATTN_BLOCK * h:ATTN_BLOCK * (h + 1)], axis=0, keepdims=True)
                dsk_ref[h:h + 1, :] += jnp.broadcast_to(dsink, (1, 128))
            dkp_ref[rows, :] = dk[:ATTN_BLOCK]
            dkc_ref[rows, :] = dk[ATTN_BLOCK:]
            dvp_ref[rows, :] = dv[:ATTN_BLOCK]
            dvc_ref[rows, :] = dv[ATTN_BLOCK:]

    cur = lambda w: pl.BlockSpec((STEP_ROWS, w), lambda n: (n, 0))
    prev = lambda w: pl.BlockSpec((ATTN_BLOCK, w), lambda n: (jnp.maximum(STEP_BLOCKS * n - 1, 0), 0))
    f = lambda w: SDS((T, w), F32)
    return pl.pallas_call(
        _after(body, 7, deps), grid=(nb // STEP_BLOCKS,),
        in_specs=[cur(512), cur(256), prev(256), cur(256), prev(256), _full((8, 128)), cur(512)] + [ANY] * len(deps),
        out_specs=[cur(512), cur(256), cur(256), cur(256), cur(256), _full((8, 128))],
        out_shape=[f(512), f(256), f(256), f(256), f(256), SDS((8, 128), F32)],
        name=name, compiler_params=_cp("arbitrary"))(q, k, k, v, v, sinks_b, do, *deps)


def _mixer_ab_bwd(z, cos, sin_signed, gq, gk, seg, dq, dkc, dkp, dvc, dvp, dpa, wbd, scale, dz, *, tr, name, deps=()):
    T = z.shape[0]
    n = T // tr
    hb = tr // 16
    ab = tr // ATTN_BLOCK

    def unfold(cur, nxt_tile, nxt_halo, i):
        nxt = jnp.concatenate([nxt_tile[ATTN_BLOCK:], jnp.where(i == n - 1, 0.0, nxt_halo)], axis=0)
        tot = cur + nxt
        first = _lane((tr, 128)) < HEAD_DIM
        a = tot[:, :128]
        b = tot[:, 128:]
        a = a + pltpu.roll(a, HEAD_DIM, 1)
        b = b + pltpu.roll(b, HEAD_DIM, 1)
        return jnp.where(first, a, b)

    def body(xp_ref, xpp_ref, qa_ref, qb_ref, kv_ref, c_ref, s_ref, gq_ref, gk_ref, seg_ref,
             dq_ref, dkc_ref, dkp_ref, dkh_ref, dvc_ref, dvp_ref, dvh_ref, dpa_ref, dpan_ref, w_ref, sc_ref, _dz_in,
             dz_ref, dgq_ref, dgk_ref, dw_ref, dsc_ref):
        i = pl.program_id(0)
        c, s, seg_m = c_ref[...], s_ref[...], seg_ref[...]
        scale_q = HEAD_DIM ** -0.5
        dqv = dq_ref[...] * scale_q
        dxa, dga = _norm_rope_bwd(qa_ref[...].astype(F32), gq_ref[...], c, s, seg_m, dqv[:, :256])
        dxb, dgb = _norm_rope_bwd(qb_ref[...].astype(F32), gq_ref[...], c, s, seg_m, dqv[:, 256:])
        dk = unfold(dkc_ref[...], dkp_ref[...], dkh_ref[...], i)
        dv = unfold(dvc_ref[...], dvp_ref[...], dvh_ref[...], i)
        kv = kv_ref[...].astype(F32)
        dxk, dgk = _norm_rope_bwd(kv[:, :KV_WIDTH], gk_ref[...], c, s, seg_m[:128, :128], dk)
        dxp, dwbd, dscale = _pool_bwd_tile(i, n, tr, xp_ref[...].astype(F32), xpp_ref[...].astype(F32),
                                           dpa_ref[...], dpan_ref[...],
                                           w_ref[...], sc_ref[...])
        dz_ref[...] = jnp.concatenate([dxp, dxa, dxb, dxk, dv], axis=1).astype(dz_ref.dtype)

        @pl.when(i == 0)
        def _():
            dgq_ref[...] = jnp.zeros_like(dgq_ref)
            dgk_ref[...] = jnp.zeros_like(dgk_ref)
            dw_ref[...] = jnp.zeros_like(dw_ref)
            dsc_ref[...] = jnp.zeros_like(dsc_ref)
        dgq_ref[...] += _fold_lanes(dga + dgb, HEAD_DIM)
        dgk_ref[...] += _fold_lanes(dgk, HEAD_DIM)
        dw_ref[...] += dwbd
        dsc_ref[...] += dscale

    col = lambda j: pl.BlockSpec((tr, 256), lambda i: (i, j))
    rows = lambda w: pl.BlockSpec((tr, w), lambda i: (i, 0))
    nxt_blk = pl.BlockSpec((ATTN_BLOCK, 256), lambda i: (jnp.minimum((i + 1) * ab, T // ATTN_BLOCK - 1), 0))
    prev16 = pl.BlockSpec((16, 256), lambda i: (jnp.maximum(i * hb - 1, 0), 0))
    next16 = pl.BlockSpec((16, 256), lambda i: (jnp.minimum((i + 1) * hb, T // 16 - 1), 0))
    return pl.pallas_call(
        _after(body, 22, deps), grid=(n,),
        in_specs=[col(0), prev16, col(1), col(2), col(3), rows(128), rows(128),
                  _full((1, 256)), _full((1, 128)), _full((256, 256)),
                  rows(512), rows(256), rows(256), nxt_blk, rows(256), rows(256), nxt_blk,
                  rows(256), next16, _full((256, 256)), _full((1, 256)), ANY] + [ANY] * len(deps),
        out_specs=[rows(1024), _full((1, 256)), _full((1, 128)), _full((256, 256)), _full((1, 256))],
        out_shape=[SDS((T, IN_COLS), MXU_DTYPE), SDS((1, 256), F32), SDS((1, 128), F32),
                   SDS((256, 256), F32), SDS((1, 256), F32)],
        input_output_aliases={21: 0}, name=name, compiler_params=_cp("arbitrary"))(
            z, z, z, z, z, cos, sin_signed, gq, gk, seg, dq, dkc, dkp, dkp, dvc, dvp, dvp, dpa, dpa, wbd, scale, dz,
            *deps)


def _sgu_common(zu, zv, vn, seg):
    u, du = _gelu_and_grad(zu)
    gv, dgv = _gelu_and_grad(zv)
    ms = _split_dot(gv * gv, seg) * (1.0 / HEAD_DIM)
    r = lax.rsqrt(ms + EPS)
    xh = gv * r
    return u, du, dgv, r, xh, xh * vn


def _sgu_fwd(z, wtril, bexp, vn, seg, *, tr, name):
    T = z.shape[0]
    nch = tr // CHUNK

    def body(u_ref, v_ref, w_ref, b_ref, vn_ref, seg_ref, o_ref):
        u, _, _, _, _, vg = _sgu_common(u_ref[...].astype(F32), v_ref[...].astype(F32), vn_ref[...], seg_ref[...])
        grp = _lane((CHUNK, SGU_WIDTH)) // HEAD_DIM
        outs = []
        for ch in range(nch):
            vc = vg[ch * CHUNK:(ch + 1) * CHUNK]
            s = b_ref[...]
            for g in range(4):
                s = s + jnp.where(grp == g, _dot(w_ref[g], vc), 0.0)
            outs.append(u[ch * CHUNK:(ch + 1) * CHUNK] * s)
        o_ref[...] = jnp.concatenate(outs, axis=0).astype(o_ref.dtype)

    col = lambda j: pl.BlockSpec((tr, 256), lambda i: (i, j))
    return pl.pallas_call(
        body, grid=(T // tr,),
        in_specs=[col(4), col(5), _full((4, CHUNK, CHUNK)), _full((CHUNK, 256)), _full((1, 256)), _full((256, 256))],
        out_specs=col(0), out_shape=SDS((T, SGU_WIDTH), MXU_DTYPE), name=name,
        compiler_params=_cp("parallel"))(z, z, wtril, bexp, vn, seg)


def _sgu_bwd(z, wtril, bexp, vn, seg, dsg, dz, *, tr, name):
    T = z.shape[0]
    nch = tr // CHUNK

    def body(u_ref, v_ref, w_ref, b_ref, vn_ref, seg_ref, d_ref, _dz_in, dz_ref, dw_ref, db_ref, dvn_ref):
        i = pl.program_id(0)
        seg_m = seg_ref[...]
        vn_v = vn_ref[...]
        u, du, dgv, r, xh, vg = _sgu_common(u_ref[...].astype(F32), v_ref[...].astype(F32), vn_v, seg_m)
        d = d_ref[...]
        grp = _lane((CHUNK, SGU_WIDTH)) // HEAD_DIM
        tril = _row((CHUNK, CHUNK)) >= _lane((CHUNK, CHUNK))

        @pl.when(i == 0)
        def _():
            dw_ref[...] = jnp.zeros_like(dw_ref)
            db_ref[...] = jnp.zeros_like(db_ref)
            dvn_ref[...] = jnp.zeros_like(dvn_ref)

        dus, dvgs = [], []
        for ch in range(nch):
            sl = slice(ch * CHUNK, (ch + 1) * CHUNK)
            vc = vg[sl]
            s = b_ref[...]
            for g in range(4):
                s = s + jnp.where(grp == g, _dot(w_ref[g], vc), 0.0)
            dus.append(d[sl] * s)
            ds = d[sl] * u[sl]
            db_ref[...] += _split_dot(ds, seg_m)
            dvg = jnp.zeros((CHUNK, SGU_WIDTH), F32)
            for g in range(4):
                dsm = jnp.where(grp == g, ds, 0.0)
                dvg = dvg + jnp.where(grp == g, _dot(w_ref[g], ds, TN), 0.0)
                dw_ref[g] += jnp.where(tril, _dot(dsm, vc, NT), 0.0)
            dvgs.append(dvg)
        dup = jnp.concatenate(dus, axis=0)
        dvg = jnp.concatenate(dvgs, axis=0)
        dvn_ref[...] += _fold_lanes(jnp.sum(dvg * xh, axis=0, keepdims=True), HEAD_DIM)
        gy = dvg * vn_v
        dgvv = r * (gy - xh * (_split_dot(xh * gy, seg_m) * (1.0 / HEAD_DIM)))
        dz_ref[...] = jnp.concatenate([dup * du, dgvv * dgv], axis=1).astype(dz_ref.dtype)

    col = lambda j: pl.BlockSpec((tr, 256), lambda i: (i, j))
    return pl.pallas_call(
        body, grid=(T // tr,),
        in_specs=[col(4), col(5), _full((4, CHUNK, CHUNK)), _full((CHUNK, 256)), _full((1, 256)), _full((256, 256)),
                  col(0), ANY],
        out_specs=[pl.BlockSpec((tr, 512), lambda i: (i, 2)), _full((4, CHUNK, CHUNK)), _full((CHUNK, 256)),
                   _full((1, 256))],
        out_shape=[SDS((T, IN_COLS), MXU_DTYPE), SDS((4, CHUNK, CHUNK), F32), SDS((CHUNK, 256), F32),
                   SDS((1, 256), F32)],
        input_output_aliases={7: 0}, name=name, compiler_params=_cp("arbitrary"))(
            z, z, wtril, bexp, vn, seg, dsg, dz)


def _merge_fwd(pa, at, sg, wa, wb, wc, z, x, w_out, *, tm, tn, name):
    T = pa.shape[0]
    gb = GATE_COL0 // tn
    nb = D_MODEL // tn

    def body(pa_ref, at_ref, sg_ref, wa_ref, wb_ref, wc_ref, g0_ref, g1_ref, g2_ref, x_ref, wo_ref,
             m_ref, y_ref, x1_ref):
        j = pl.program_id(1)
        acc = None
        for idx, (op_ref, w_ref, g_ref) in enumerate(((pa_ref, wa_ref, g0_ref), (at_ref, wb_ref, g1_ref),
                                                      (sg_ref, wc_ref, g2_ref))):
            y = _dot(op_ref[...], w_ref[...])
            y_ref[idx] = y.astype(y_ref.dtype)
            t = _sigmoid(g_ref[...].astype(F32)) * y
            acc = t if acc is None else acc + t
        merged = acc.astype(m_ref.dtype)
        m_ref[...] = merged
        p = _dot(merged, wo_ref[...])

        @pl.when(j == 0)
        def _():
            x1_ref[...] = x_ref[...] + p

        @pl.when(j > 0)
        def _():
            x1_ref[...] += p

    op = lambda w: pl.BlockSpec((tm, w), lambda i, j: (i, 0))
    wt = lambda k: pl.BlockSpec((k, tn), lambda i, j: (0, j))
    gate = lambda b: pl.BlockSpec((tm, tn), lambda i, j: (i, gb + b * nb + j))
    return pl.pallas_call(
        body, grid=(T // tm, nb),
        in_specs=[op(256), op(512), op(256), wt(256), wt(512), wt(256), gate(0), gate(1), gate(2),
                  op(D_MODEL), pl.BlockSpec((tn, D_MODEL), lambda i, j: (j, 0))],
        out_specs=[pl.BlockSpec((tm, tn), lambda i, j: (i, j)), pl.BlockSpec((3, tm, tn), lambda i, j: (0, i, j)),
                   op(D_MODEL)],
        out_shape=[SDS((T, D_MODEL), MXU_DTYPE), SDS((3, T, D_MODEL), MXU_DTYPE), SDS((T, D_MODEL), F32)],
        name=name, compiler_params=_cp("parallel", "arbitrary"))(pa, at, sg, wa, wb, wc, z, z, z, x, w_out)


def _out_dx_merge_bwd(dxb, w_out, y, z, ws, xs, *, tm, tn, name):
    T = dxb.shape[0]
    gb = GATE_COL0 // tn
    nb = D_MODEL // tn
    nr = T // tm
    widths = [w.shape[0] for w in ws]

    def body(dx_ref, w_ref, y_ref, g_ref, *refs):
        w_refs, x_refs = refs[0:3], refs[3:6]
        dz_ref, dx_refs, dw_refs = refs[6], refs[7:10], refs[10:13]
        dm_ref, acc_refs = refs[13], refs[14:17]
        i, b, j = pl.program_id(0), pl.program_id(1), pl.program_id(2)

        @pl.when((b == 0) & (j == 0))
        def _():
            dm = _dot(dx_ref[...], w_ref[...], NT)
            for jj in range(nb):
                dm_ref[jj] = dm[:, jj * tn:(jj + 1) * tn]

        d = dm_ref[j]
        g = _sigmoid(g_ref[...].astype(F32))
        dy = (d * g).astype(MXU_DTYPE)
        dz_ref[...] = (d * y_ref[...].astype(F32) * g * (1.0 - g)).astype(dz_ref.dtype)
        for branch in range(3):
            @pl.when(b == branch)
            def _():
                p = _dot(dy, w_refs[branch][...], NT)
                q = _dot(x_refs[branch][...], dy, TN)

                @pl.when(j == 0)
                def _():
                    dx_refs[branch][...] = p

                @pl.when(j > 0)
                def _():
                    dx_refs[branch][...] += p

                @pl.when(i == 0)
                def _():
                    acc_refs[branch][j] = q

                @pl.when(i > 0)
                def _():
                    acc_refs[branch][j] += q

        @pl.when((i == nr - 1) & (b == 2) & (j == nb - 1))
        def _():
            for branch in range(3):
                for jj in range(nb):
                    dw_refs[branch][:, jj * tn:(jj + 1) * tn] = acc_refs[branch][jj]

    wspec = lambda k: pl.BlockSpec((k, tn), lambda i, b, j: (0, j))
    rows = lambda k: pl.BlockSpec((tm, k), lambda i, b, j: (i, 0))
    return pl.pallas_call(
        body, grid=(nr, 3, nb),
        in_specs=[rows(D_MODEL),
                  pl.BlockSpec((D_MODEL, D_MODEL), lambda i, b, j: (0, 0), pipeline_mode=pl.Buffered(1)),
                  pl.BlockSpec((None, tm, tn), lambda i, b, j: (b, i, j)),
                  pl.BlockSpec((tm, tn), lambda i, b, j: (i, gb + b * nb + j))]
        + [wspec(k) for k in widths] + [rows(k) for k in widths],
        out_specs=[pl.BlockSpec((tm, tn), lambda i, b, j: (i, gb + b * nb + j))]
        + [rows(k) for k in widths] + [_full((k, D_MODEL)) for k in widths],
        out_shape=[SDS((T, IN_COLS), MXU_DTYPE)] + [SDS((T, k), F32) for k in widths]
        + [SDS((k, D_MODEL), F32) for k in widths],
        scratch_shapes=[pltpu.VMEM((nb, tm, tn), F32)] + [pltpu.VMEM((nb, k, tn), F32) for k in widths],
        name=name, compiler_params=_cp("arbitrary", "arbitrary", "arbitrary"))(dxb, w_out, y, z, *ws, *xs)


def _conv3(xe, w, b):
    return (w[0:1] * pltpu.roll(xe, 2, 0) + w[1:2] * pltpu.roll(xe, 1, 0) + w[2:3] * xe)[8:] + b


def _conv_act_fwd(up, cw, cb, *, tr, tc, name):
    T = up.shape[0]
    nc = D_FF // tc
    hb = tr // HALO

    def body(ug_ref, ugp_ref, uv_ref, uvp_ref, wg_ref, wv_ref, bg_ref, bv_ref, o_ref):
        i = pl.program_id(1)
        first = i == 0

        def halo_tile(prev_ref, cur_ref):
            prev8 = prev_ref[...].astype(F32)[HALO - 8:]
            return jnp.concatenate([jnp.where(first, 0.0, prev8), cur_ref[...].astype(F32)], axis=0)

        cg = _conv3(halo_tile(ugp_ref, ug_ref), wg_ref[...], bg_ref[...])
        cv = _conv3(halo_tile(uvp_ref, uv_ref), wv_ref[...], bv_ref[...])
        o_ref[...] = (cg * _sigmoid(cg) * cv).astype(o_ref.dtype)

    tile = lambda off: pl.BlockSpec((tr, tc), lambda j, i: (i, off + j))
    prev = lambda off: pl.BlockSpec((HALO, tc), lambda j, i: (jnp.maximum(i * hb - 1, 0), off + j))
    par = lambda rows, off: pl.BlockSpec((rows, tc), lambda j, i: (0, off + j))
    return pl.pallas_call(
        body, grid=(nc, T // tr),
        in_specs=[tile(0), prev(0), tile(nc), prev(nc), par(3, 0), par(3, nc), par(1, 0), par(1, nc)],
        out_specs=pl.BlockSpec((tr, tc), lambda j, i: (i, j)),
        out_shape=SDS((T, D_FF), MXU_DTYPE), name=name,
        compiler_params=_cp("parallel", "parallel"))(up, up, up, up, cw, cw, cb, cb)


def _conv_act_bwd(up, cw, cb, dact, *, tr, tc, name, deps=()):
    T = up.shape[0]
    nc = D_FF // tc
    hb = tr // 8
    hbu = tr // HALO
    nr = T // tr

    def body(ug_ref, ugp_ref, ugn_ref, uv_ref, uvp_ref, uvn_ref, da_ref, dan_ref, wg_ref, wv_ref, bg_ref, bv_ref,
             du_ref, dwg_ref, dwv_ref, dbg_ref, dbv_ref):
        i = pl.program_id(1)
        first, last = i == 0, i == nr - 1
        da = jnp.concatenate([da_ref[...], jnp.where(last, 0.0, dan_ref[...])], axis=0)

        def with_halos(prev_ref, cur_ref, next_ref):
            prev8 = prev_ref[...].astype(F32)[HALO - 8:]
            next8 = next_ref[...].astype(F32)[:8]
            return jnp.concatenate([jnp.where(first, 0.0, prev8), cur_ref[...].astype(F32), next8], axis=0)

        uge = with_halos(ugp_ref, ug_ref, ugn_ref)
        uve = with_halos(uvp_ref, uv_ref, uvn_ref)
        wg, wv = wg_ref[...], wv_ref[...]
        ug1, ug2 = pltpu.roll(uge, 1, 0)[8:], pltpu.roll(uge, 2, 0)[8:]
        uv1, uv2 = pltpu.roll(uve, 1, 0)[8:], pltpu.roll(uve, 2, 0)[8:]
        cg = wg[0:1] * ug2 + wg[1:2] * ug1 + wg[2:3] * uge[8:] + bg_ref[...]
        cv = wv[0:1] * uv2 + wv[1:2] * uv1 + wv[2:3] * uve[8:] + bv_ref[...]
        sg = _sigmoid(cg)
        dcg = da * cv * (sg * (1.0 + cg * (1.0 - sg)))
        dcv = da * (cg * sg)
        nrow = tr + 8

        def back(dc, w):
            return (w[2:3] * dc + w[1:2] * pltpu.roll(dc, nrow - 1, 0) + w[0:1] * pltpu.roll(dc, nrow - 2, 0))[:tr]

        du_ref[0] = back(dcg, wg).astype(du_ref.dtype)
        du_ref[1] = back(dcv, wv).astype(du_ref.dtype)

        def wgrad(dc, u0, u1, u2):
            d = dc[:tr]
            rows = [jnp.sum(d * u2[:tr], axis=0, keepdims=True), jnp.sum(d * u1[:tr], axis=0, keepdims=True),
                    jnp.sum(d * u0[8:8 + tr], axis=0, keepdims=True)]
            return jnp.concatenate(rows, axis=0), jnp.sum(d, axis=0, keepdims=True)

        dwg, dbg = wgrad(dcg, uge, ug1, ug2)
        dwv, dbv = wgrad(dcv, uve, uv1, uv2)

        @pl.when(first)
        def _():
            dwg_ref[...] = jnp.zeros_like(dwg_ref)
            dwv_ref[...] = jnp.zeros_like(dwv_ref)
            dbg_ref[...] = jnp.zeros_like(dbg_ref)
            dbv_ref[...] = jnp.zeros_like(dbv_ref)
        dwg_ref[...] += dwg
        dwv_ref[...] += dwv
        dbg_ref[...] += dbg
        dbv_ref[...] += dbv

    tile = lambda off: pl.BlockSpec((tr, tc), lambda j, i: (i, off + j))
    prev = lambda off: pl.BlockSpec((HALO, tc), lambda j, i: (jnp.maximum(i * hbu - 1, 0), off + j))
    nxt = lambda off: pl.BlockSpec((HALO, tc), lambda j, i: (jnp.minimum((i + 1) * hbu, T // HALO - 1), off + j))
    dnext = pl.BlockSpec((8, tc), lambda j, i: (jnp.minimum((i + 1) * hb, T // 8 - 1), j))
    par = lambda rows, off: pl.BlockSpec((rows, tc), lambda j, i: (0, off + j))
    acc = lambda rows: pl.BlockSpec((rows, tc), lambda j, i: (0, j))
    return pl.pallas_call(
        _after(body, 12, deps), grid=(nc, nr),
        in_specs=[tile(0), prev(0), nxt(0), tile(nc), prev(nc), nxt(nc), tile(0), dnext,
                  par(3, 0), par(3, nc), par(1, 0), par(1, nc)] + [ANY] * len(deps),
        out_specs=[pl.BlockSpec((2, tr, tc), lambda j, i: (0, i, j)), acc(3), acc(3), acc(1), acc(1)],
        out_shape=[SDS((2, T, D_FF), MXU_DTYPE), SDS((3, D_FF), F32), SDS((3, D_FF), F32),
                   SDS((1, D_FF), F32), SDS((1, D_FF), F32)],
        name=name, compiler_params=_cp("parallel", "arbitrary"))(
            up, up, up, up, up, up, dact, dact, cw, cw, cb, cb, *deps)


def _row_tile(rows, cap):
    t = min(cap, rows)
    t -= t % 8
    while rows % t:
        t -= 8
    return t


def _adamw(w, g, m, v, *, tr, name, copy_g=False):
    R, C = w.shape
    assert R % tr == 0, (R, tr)

    def body(w_ref, g_ref, m_ref, v_ref, d_ref, nm_ref, nv_ref, *rest):
        gv = g_ref[...]
        mn = ADAM_B1 * m_ref[...] + (1.0 - ADAM_B1) * gv
        vn = ADAM_B2 * v_ref[...] + (1.0 - ADAM_B2) * (gv * gv)
        m_hat = mn / (1.0 - ADAM_B1 ** ADAM_STEP)
        v_hat = vn / (1.0 - ADAM_B2 ** ADAM_STEP)
        d_ref[...] = -ADAM_LR * (m_hat / (jnp.sqrt(v_hat) + ADAM_EPS) + ADAM_WD * w_ref[...])
        nm_ref[...] = mn
        nv_ref[...] = vn
        if copy_g:
            rest[0][...] = gv

    rows = pl.BlockSpec((tr, C), lambda i: (i, 0))
    n_out = 4 if copy_g else 3
    return pl.pallas_call(
        body, grid=(R // tr,), in_specs=[rows] * 4, out_specs=[rows] * n_out,
        out_shape=[SDS((R, C), F32)] * n_out, name=name, compiler_params=_cp("parallel"))(w, g, m, v)


def _sum_slots(r, *, tr, name):
    S, R, C = r.shape
    assert R % tr == 0, (R, tr)

    def body(r_ref, o_ref):
        acc = r_ref[0]
        for s in range(1, S):
            acc = acc + r_ref[s]
        o_ref[...] = acc

    return pl.pallas_call(
        body, grid=(R // tr,), in_specs=[pl.BlockSpec((S, tr, C), lambda i: (0, i, 0))],
        out_specs=pl.BlockSpec((tr, C), lambda i: (i, 0)), out_shape=SDS((R, C), F32),
        name=name, compiler_params=_cp("parallel"))(r)


def _pair_add(g4, h, pos, *, name):
    A, _, r, C = g4.shape
    cs = C if A == N_CHIPS else C // N_CHIPS
    tr = _row_tile(r, 256)
    if A == N_CHIPS:
        g_map, h_map = (lambda t, i, pos: (t, pos[1], i, 0)), (lambda t, i, pos: (t, i, 0))
    else:
        g_map, h_map = (lambda t, i, pos: (0, pos[1], i, t)), (lambda t, i, pos: (0, i, t))

    def body(pos_ref, g_ref, h_ref, o_ref):
        o_ref[...] = (g_ref[...] + h_ref[...]).astype(o_ref.dtype)

    grid_spec = pltpu.PrefetchScalarGridSpec(
        num_scalar_prefetch=1, grid=(N_CHIPS, r // tr),
        in_specs=[pl.BlockSpec((None, None, tr, cs), g_map), pl.BlockSpec((None, tr, cs), h_map)],
        out_specs=pl.BlockSpec((None, tr, cs), lambda t, i, pos: (t, i, 0)))
    return pl.pallas_call(body, grid_spec=grid_spec, out_shape=SDS((N_CHIPS, r, cs), COMM_DTYPE), name=name,
                          compiler_params=_cp("parallel", "parallel"))(pos, g4, h)


def _chip_sum(p, r2, f_into, pos, layer, *, name):
    _, r, cs = p.shape
    tr = _row_tile(r, 256)

    def body(pos_ref, own_ref, r_ref, *rest):
        o_ref = rest[-1]
        o_ref[...] = ((own_ref[...].astype(F32) + r_ref[0].astype(F32)) + r_ref[1].astype(F32)) + r_ref[2].astype(F32)

    in_specs = [pl.BlockSpec((None, tr, cs), lambda i, pos: (pos[0], i, 0)),
                pl.BlockSpec((3, tr, cs), lambda i, pos: (0, i, 0))]
    operands = [pos, p, r2]
    aliases = {}
    if f_into is not None:
        in_specs.append(ANY)
        operands.append(f_into)
        aliases = {3: 0}
    grid_spec = pltpu.PrefetchScalarGridSpec(
        num_scalar_prefetch=1, grid=(r // tr,), in_specs=in_specs,
        out_specs=pl.BlockSpec((None, None, tr, cs), lambda i, pos: (layer, pos[1], i, 0)))
    return pl.pallas_call(body, grid_spec=grid_spec, out_shape=SDS((DEPTH, 2, r, cs), F32), name=name,
                          input_output_aliases=aliases, compiler_params=_cp("parallel"))(*operands)


def _mesh_pos():
    return lax.axis_index("x"), lax.axis_index("y"), lax.axis_index("c")


HBM = pl.BlockSpec(memory_space=pltpu.HBM)
SEM = pl.BlockSpec(memory_space=pltpu.SEMAPHORE)
DATAFLOW = pltpu.SideEffectType.DATAFLOW_SIDE_EFFECTING
CHIP_FLIPS = (2, 1, 3)


def _chip_peers():
    x, y, c = _mesh_pos()
    return 2 * x + y, [(1 - x, y, c), (x, 1 - y, c), (1 - x, 1 - y, c)], (x, y, 1 - c), c


def _split_start(arrays, n_copies, issue, *, name, deps=()):
    k = len(arrays)
    nd = len(deps)

    def body(*refs):
        issue(refs[:k], refs[k + nd], refs[k + nd + 1])
        refs[2 * k + nd + 2][...] = jnp.zeros((8, 128), F32)

    out = pl.pallas_call(
        body, name=name,
        out_shape=(pltpu.SemaphoreType.DMA((n_copies,)), pltpu.SemaphoreType.DMA((n_copies,)),
                   *[pltpu.HBM(a.shape, a.dtype) for a in arrays], SDS((8, 128), F32)),
        in_specs=[HBM] * k + [ANY] * nd, out_specs=(SEM, SEM, *[HBM] * k, pl.BlockSpec(memory_space=pltpu.VMEM)),
        input_output_aliases={i: 2 + i for i in range(k)},
        compiler_params=pltpu.CompilerParams(has_side_effects=DATAFLOW))(
            *[pltpu.with_memory_space_constraint(a, pltpu.HBM) for a in arrays], *deps)
    return (out[0], out[1]), list(out[2:2 + k]), out[2 + k]


def _split_wait(sems, arrays, after, waits, *, name):
    k = len(arrays)
    afters = tuple(after) if isinstance(after, (tuple, list)) else (after,)

    def body(*refs):
        waits(refs[:k], refs[k], refs[k + 1])

    out = pl.pallas_call(
        body, name=name, out_shape=tuple(pltpu.HBM(a.shape, a.dtype) for a in arrays),
        in_specs=[HBM] * k + [SEM, SEM] + [ANY] * len(afters), out_specs=tuple([HBM] * k),
        input_output_aliases={i: i for i in range(k)},
        compiler_params=pltpu.CompilerParams(has_side_effects=DATAFLOW))(*arrays, sems[0], sems[1], *afters)
    return list(out)


def _wait_both(cp):
    cp.wait_send()
    cp.wait_recv()


def _cast_place(shard, pos, dtype, *, name, layer=None, slots=N_CHIPS, which=0):
    R, C = shard.shape[-2:]
    tr = R if R % 8 else _row_tile(R, 256)
    if layer is None:
        in_spec = pl.BlockSpec((tr, C), lambda i, pos: (i, 0))
    else:
        in_spec = pl.BlockSpec((None, tr, C), lambda i, pos: (layer, i, 0))

    def body(pos_ref, x_ref, o_ref):
        o_ref[...] = x_ref[...].astype(o_ref.dtype)

    grid_spec = pltpu.PrefetchScalarGridSpec(
        num_scalar_prefetch=1, grid=(R // tr,), in_specs=[in_spec],
        out_specs=pl.BlockSpec((None, tr, C), lambda i, pos: (pos[which], i, 0)))
    return pl.pallas_call(body, grid_spec=grid_spec, out_shape=SDS((slots, R, C), dtype), name=name,
                          compiler_params=_cp("parallel"))(pos, shard)


def _device_peers():
    x, y, c = _mesh_pos()
    peers = [(x ^ ((f >> 2) & 1), y ^ ((f >> 1) & 1), c ^ (f & 1)) for f in range(1, N_DEV)]
    return 4 * x + 2 * y + c, peers


class _Gather:
    def __init__(self, lands, name, deps=(), all_devices=False, halves=False):
        n = len(lands)
        self.name, self.halves = name, halves
        npeer = N_DEV - 1 if all_devices else N_CHIPS - 1
        if halves:
            lands = [a.reshape(a.shape[0], 2, a.shape[1] // 2, a.shape[2]) for a in lands]

        def copies(refs, ss, rs):
            if halves:
                me, peers, _, c = _chip_peers()
                own = lambda r: r.at[me, c]
            else:
                me, peers = _device_peers() if all_devices else _chip_peers()[:2]
                own = lambda r: r.at[me]
            return [pltpu.make_async_remote_copy(
                src_ref=own(refs[w]), dst_ref=own(refs[w]), send_sem=ss.at[npeer * w + p],
                recv_sem=rs.at[npeer * w + p], device_id=peers[p], device_id_type=MESH)
                for w in range(n) for p in range(npeer)]

        def issue(refs, ss, rs):
            for cp in copies(refs, ss, rs):
                cp.start()

        def waits(refs, ss, rs):
            for cp in copies(refs, ss, rs):
                _wait_both(cp)

        self._waits = waits
        self.sems, self.arrays, self.token = _split_start(list(lands), npeer * n, issue, name=name + "_start",
                                                          deps=deps)

    def wait(self, after):
        arrays = _split_wait(self.sems, self.arrays, after, self._waits, name=self.name + "_wait")
        if not self.halves:
            return arrays
        n = len(arrays)

        def copies(refs, ss, rs):
            me, _, sibling, c = _chip_peers()
            return [pltpu.make_async_remote_copy(
                src_ref=refs[w].at[me ^ CHIP_FLIPS[p], c], dst_ref=refs[w].at[me ^ CHIP_FLIPS[p], c],
                send_sem=ss.at[3 * w + p], recv_sem=rs.at[3 * w + p], device_id=sibling, device_id_type=MESH)
                for w in range(n) for p in range(3)]

        def issue(refs, ss, rs):
            for cp in copies(refs, ss, rs):
                cp.start()

        def waits(refs, ss, rs):
            for cp in copies(refs, ss, rs):
                _wait_both(cp)

        sems, arrays, _ = _split_start(arrays, 3 * n, issue, name=self.name + "_share_start")
        arrays = _split_wait(sems, arrays, after, waits, name=self.name + "_share_wait")
        return [a.reshape(a.shape[0], 2 * a.shape[2], a.shape[3]) for a in arrays]


def _swap_halves_start(g4s, *, name):
    n = len(g4s)
    lands = [lax.empty((g.shape[0],) + g.shape[2:], g.dtype) for g in g4s]

    def copies(refs, ss, rs):
        _, _, sibling, c = _chip_peers()
        return [pltpu.make_async_remote_copy(
            src_ref=refs[w].at[:, 1 - c], dst_ref=refs[n + w], send_sem=ss.at[w], recv_sem=rs.at[w],
            device_id=sibling, device_id_type=MESH) for w in range(n)]

    def issue(refs, ss, rs):
        for cp in copies(refs, ss, rs):
            cp.start()

    def waits(refs, ss, rs):
        for cp in copies(refs, ss, rs):
            _wait_both(cp)

    sems, arrays, token = _split_start(list(g4s) + lands, n, issue, name=name + "_start")
    return sems, arrays, token, waits


def _scatter_start(parts, *, name, deps=()):
    n = len(parts)
    lands = [lax.empty((3,) + p.shape[1:], p.dtype) for p in parts]

    def copies(refs, ss, rs):
        me, peers, _, _ = _chip_peers()
        return [pltpu.make_async_remote_copy(
            src_ref=refs[w].at[me ^ CHIP_FLIPS[p]], dst_ref=refs[n + w].at[p],
            send_sem=ss.at[3 * w + p], recv_sem=rs.at[3 * w + p], device_id=peers[p], device_id_type=MESH)
            for w in range(n) for p in range(3)]

    def issue(refs, ss, rs):
        for cp in copies(refs, ss, rs):
            cp.start()

    def waits(refs, ss, rs):
        for cp in copies(refs, ss, rs):
            _wait_both(cp)

    sems, arrays, token = _split_start(list(parts) + lands, 3 * n, issue, name=name + "_start", deps=deps)
    return sems, arrays, token, waits


def _pair_share_start(fs, layer, *, name):
    n = len(fs)

    def copies(refs, ss, rs):
        _, _, sibling, c = _chip_peers()
        return [pltpu.make_async_remote_copy(
            src_ref=refs[w].at[layer, c], dst_ref=refs[w].at[layer, c], send_sem=ss.at[w], recv_sem=rs.at[w],
            device_id=sibling, device_id_type=MESH) for w in range(n)]

    def issue(refs, ss, rs):
        for cp in copies(refs, ss, rs):
            cp.start()

    def waits(refs, ss, rs):
        for cp in copies(refs, ss, rs):
            _wait_both(cp)

    sems, arrays, token = _split_start(list(fs), n, issue, name=name + "_start")
    return sems, arrays, token, waits


BIG = ('w_in', 'w_proj_a', 'w_proj_b', 'w_proj_c', 'w_out', 'w_up', 'w_down')
BIG_SHARD_AXIS = {'w_in': 2, 'w_proj_a': 2, 'w_proj_b': 2, 'w_proj_c': 2, 'w_out': 1, 'w_up': 2, 'w_down': 1}
SMALL = ('norm1', 'q_norm', 'k_norm', 'sinks', 'w_pool', 'pool_scale', 'sgu_v_norm', 'w_s', 'b_s', 'norm2',
         'conv_b', 'conv_w')
WEIGHTS = ('norm1', 'w_in', 'q_norm', 'k_norm', 'sinks', 'w_pool', 'pool_scale', 'sgu_v_norm', 'w_s', 'b_s',
           'w_proj_a', 'w_proj_b', 'w_proj_c', 'w_out', 'norm2', 'w_up', 'conv_w', 'conv_b', 'w_down')


def _rope_tables(positions):
    inv_freq = ROPE_THETA ** (-jnp.arange(0, HEAD_DIM, 2, dtype=F32) / HEAD_DIM)
    ang = positions.astype(F32)[:, None] * inv_freq
    cos, sin = jnp.cos(ang), jnp.sin(ang)
    c = jnp.concatenate([cos, cos], axis=1)
    s = jnp.concatenate([-sin, sin], axis=1)
    return jnp.concatenate([c, c], axis=1), jnp.concatenate([s, s], axis=1)


def _block_diag4(w):
    out = jnp.zeros((POOL_WIDTH, POOL_WIDTH), w.dtype)
    for g in range(4):
        out = lax.dynamic_update_slice(out, w[g], (g * HEAD_DIM, g * HEAD_DIM))
    return out


def _local_step(x, target, cos, sin, sp, sched):
    T = x.shape[0]
    tm1 = min(1024, T)
    tm = min(512, T)
    tr = min(1024, T)
    trc = min(512, T)
    tkt = min(2048, T)
    seg = _seg_matrix(256, HEAD_DIM)
    saved = []
    xl = x
    for l in range(DEPTH):
        p = f"l{l}_"
        c = dict(
            g1=sp['norm1'][l][None], g2=sp['norm2'][l][None],
            wbd=_block_diag4(sp['w_pool'][l]).astype(MXU_DTYPE), scale=sp['pool_scale'][l][None],
            gq=jnp.tile(sp['q_norm'][l], 4)[None], gk=jnp.tile(sp['k_norm'][l], 2)[None],
            sinks=jnp.broadcast_to(sp['sinks'][l][:, None], (N_Q_HEADS, 128)),
            wtril=jnp.tril(sp['w_s'][l]).astype(MXU_DTYPE),
            bexp=jnp.repeat(sp['b_s'][l].T, HEAD_DIM, axis=1), vn=jnp.tile(sp['sgu_v_norm'][l], 4)[None],
            cb=sp['conv_b'][l][None])
        c['w_in'] = sched.weight('w_in', l, xl)
        z, h1 = _norm_mm(xl, c['g1'], c['w_in'], tm=tm1, tn=1152, name=p + "in_proj",
                         deps=sched.start_tokens() if l == 0 else ())
        pa = _pool_fwd(z, c['wbd'], c['scale'], tr=tr, name=p + "pool")
        q, k, v = _qkv_prep(z, cos, sin, c['gq'], c['gk'], seg, tr=tr, name=p + "qkv_prep")
        at = _attn_fwd(q, k, v, c['sinks'], name=p + "attn")
        sg = _sgu_fwd(z, c['wtril'], c['bexp'], c['vn'], seg, tr=tr, name=p + "sgu")
        for n in ('w_proj_a', 'w_proj_b', 'w_proj_c', 'w_out'):
            c[n] = sched.weight(n, l, (pa, at, sg))
        merged, y3, x1 = _merge_fwd(pa, at, sg, c['w_proj_a'], c['w_proj_b'], c['w_proj_c'], z, xl, c['w_out'],
                                    tm=tm1, tn=512, name=p + "merge_out_proj")
        for n in ('w_up', 'conv_w', 'w_down'):
            c[n] = sched.weight(n, l, x1)
        up, h2 = _norm_mm(x1, c['g2'], c['w_up'], tm=tm1, tn=1408, name=p + "up_proj")
        act = _conv_act_fwd(up, c['conv_w'], c['cb'], tr=trc, tc=1408, name=p + "conv_act")
        saved.append(dict(c, x=xl, h1=h1, z=z, pa=pa, q=q, k=k, v=v, at=at, sg=sg, merged=merged, y3=y3,
                          x1=x1, h2=h2, up=up, act=act))
        if l < DEPTH - 1:
            xl = _mm(act, c['w_down'], mode='nn', add=x1, tm=tm, tn=D_MODEL, tk=D_FF, name=p + "down_proj")
        else:
            loss_row, dx, dxb = _down_proj_loss(act, c['w_down'], x1, target, tm=tm, name=p + "down_proj_loss")

    gs = {n: [None] * DEPTH for n in SMALL}
    for l in reversed(range(DEPTH)):
        p = f"l{l}_b_"
        s = saved[l]
        gb = {}
        dact = _mm(dxb, s['w_down'], mode='nt', tm=tm1, tn=1408, tk=D_MODEL, name=p + "down_dx")
        gb['w_down'] = _mm(s['act'], dxb, mode='tn', tm=1408, tn=D_MODEL, tk=tkt, name=p + "down_dw")
        toks = sched.slot(l, 'down', gb['w_down'])
        dup, dwg, dwv, dbg, dbv = _conv_act_bwd(s['up'], s['conv_w'], s['cb'], dact, tr=min(1024, T), tc=256,
                                                name=p + "conv_act", deps=toks)
        gs['conv_w'][l] = jnp.concatenate([dwg, dwv], axis=1)
        gs['conv_b'][l] = jnp.concatenate([dbg, dbv], axis=1)[0]
        toks = sched.slot(l, 'conv', dup)
        for half in range(2):
            gb['w_up'] = _mm(s['h2'], dup, mode='tn', b_lead=half, tm=D_MODEL, tn=1408, tk=tkt,
                             out_into=gb.get('w_up'), out_joff=2 * half, out_n=2 * D_FF, name=p + f"up_dw{half}",
                             deps=toks if half == 0 else ())
        toks = sched.slot(l, 'ffn', gb['w_up'], gb)
        dx1, dx1b, dg2 = _mm_nt_sharded_rms(dup, s['w_up'], s['x1'], s['g2'], dx, tm=tm,
                                            name=p + "up_dx_rms2", deps=toks)
        gs['norm2'][l] = dg2[0]
        gb['w_out'] = _mm(s['merged'], dx1b, mode='tn', tm=D_MODEL, tn=D_MODEL, tk=tkt, name=p + "out_dw")
        (dz, dpa, dat, dsg, gb['w_proj_a'], gb['w_proj_b'], gb['w_proj_c']) = _out_dx_merge_bwd(
            dx1b, s['w_out'], s['y3'], s['z'], [s['w_proj_a'], s['w_proj_b'], s['w_proj_c']],
            [s['pa'], s['at'], s['sg']], tm=tm1, tn=512, name=p + "out_dx_merge")
        toks = sched.slot(l, 'mid', dz)
        dq, dkc, dkp, dvc, dvp, dsk = _attn_bwd(s['q'], s['k'], s['v'], s['sinks'], dat, name=p + "attn", deps=toks)
        gs['sinks'][l] = dsk[:, 0]
        toks = sched.slot(l, 'attn', dq)
        dz, dgq, dgk, dwbd, dsc = _mixer_ab_bwd(s['z'], cos, sin, s['gq'], s['gk'], seg, dq, dkc, dkp, dvc, dvp,
                                                dpa, s['wbd'], s['scale'], dz, tr=tr, name=p + "qkv_pool", deps=toks)
        gs['q_norm'][l] = dgq[0, :HEAD_DIM]
        gs['k_norm'][l] = dgk[0, :HEAD_DIM]
        gs['w_pool'][l] = jnp.stack([dwbd[g * HEAD_DIM:(g + 1) * HEAD_DIM, g * HEAD_DIM:(g + 1) * HEAD_DIM]
                                     for g in range(4)])
        gs['pool_scale'][l] = dsc[0]
        dz, dws, dbrows, dvn = _sgu_bwd(s['z'], s['wtril'], s['bexp'], s['vn'], seg, dsg, dz, tr=tr, name=p + "sgu")
        gs['w_s'][l] = dws
        gs['b_s'][l] = dbrows[:, ::HEAD_DIM].T
        gs['sgu_v_norm'][l] = dvn[0, :HEAD_DIM]
        gb['w_in'] = _mm(s['h1'], dz, mode='tn', tm=D_MODEL, tn=1152, tk=tkt, name=p + "in_dw")
        toks = sched.slot(l, 'mix', gb['w_in'], gb)
        dx, dxb, dg1 = _mm_nt_sharded_rms(dz, s['w_in'], s['x'], s['g1'], dx1, tm=tm,
                                          name=p + "in_dx_rms1", deps=toks)
        gs['norm1'][l] = dg1[0]
    gs = {n: jnp.stack(v) for n, v in gs.items()}
    return loss_row, dx, gs


GROUP_F = ('w_down', 'w_up')
GROUP_M = ('w_out', 'w_proj_a', 'w_proj_b', 'w_proj_c', 'w_in')
ROW_SHARDED = ('w_out', 'w_down')

REDUCE_PLAN = {
    (1, 'ffn'): (('S1', 'F', 1),),
    (1, 'mid'): (('W1', 'F', 1),),
    (1, 'mix'): (('S1', 'M', 1),),
    (0, 'down'): (('W1', 'M', 1),),
    (0, 'conv'): (('W2', 'F', 1),),
    (0, 'ffn'): (('S1', 'F', 0), ('W3', 'F', 1)),
    (0, 'mid'): (('W1', 'F', 0),),
    (0, 'attn'): (('W2', 'M', 1),),
    (0, 'mix'): (('S1', 'M', 0), ('W3', 'M', 1)),
}
REDUCE_TAIL_A = (('W1', 'M', 0), ('W2', 'F', 0))
REDUCE_TAIL_B = (('W3', 'F', 0),)
REDUCE_TAIL_C = (('W2', 'M', 0), ('W3', 'M', 0))


class _Comm:
    def __init__(self, w, pos):
        self.pos = pos
        groups = {'a': [('w_in', 0)],
                  'b': [(n, 0) for n in ('w_proj_a', 'w_proj_b', 'w_proj_c', 'w_out')],
                  'c': [(n, 0) for n in ('w_up', 'conv_w', 'w_down')],
                  'd': [(n, 1) for n in BIG] + [('conv_w', 1)]}
        self.gathers, self.group_of, self.weights = {}, {}, {}
        self.tokens = []
        for g, ks in groups.items():
            lands = [_cast_place(w[n], pos, F32 if n == 'conv_w' else MXU_DTYPE, layer=l, name=f"gw_place_{n}{l}")
                     for n, l in ks]
            self.gathers[g] = (_Gather(lands, "gw_" + g, deps=self.tokens[-1:], halves=(g == 'a')), ks)
            self.tokens.append(self.gathers[g][0].token)
            self.group_of.update({k: g for k in ks})
        self.red = {}
        self.final = {}

    def start_tokens(self):
        return self.tokens[-1:]

    def weight(self, name, layer, after):
        if (name, layer) not in self.weights:
            gather, ks = self.gathers[self.group_of[(name, layer)]]
            for (n, l), full in zip(ks, gather.wait(after)):
                if n == 'conv_w' or n.startswith('w_proj'):
                    full = full.transpose(1, 0, 2).reshape(full.shape[1], -1)
                elif n in ROW_SHARDED:
                    full = full.reshape(-1, full.shape[2])
                self.weights[(n, l)] = full
        return self.weights[(name, layer)]

    def slot(self, layer, slot, after, grads=None):
        tokens = []
        for step, grp, lyr in REDUCE_PLAN.get((layer, slot), ()):
            tok = self._step(step, grp, lyr, after, grads)
            if tok is not None:
                tokens.append(tok)
        return tokens

    def tail(self, steps, after, deps=()):
        toks = (self._step(step, grp, lyr, after, None, deps) for step, grp, lyr in steps)
        return [t for t in toks if t is not None]

    def shards(self):
        return {n: f.reshape(DEPTH, 2 * f.shape[2], f.shape[3]) for n, f in self.final.items()}

    def _step(self, step, grp, layer, after, grads, deps=()):
        names = GROUP_F if grp == 'F' else GROUP_M
        tag = f"{grp.lower()}{layer}"
        st = self.red.setdefault((grp, layer), {})
        n = len(names)
        if step == 'S1':
            g4s = []
            for nm in names:
                g = grads[nm]
                R, C = g.shape
                g4s.append(g.reshape(N_CHIPS, 2, R // (2 * N_CHIPS), C) if nm in ROW_SHARDED
                           else g.reshape(1, 2, R // 2, C))
            st['s1'] = _swap_halves_start(g4s, name="rs1_" + tag)
            return st['s1'][2]
        if step == 'W1':
            sems, arrays, _, waits = st.pop('s1')
            arrays = _split_wait(sems, arrays, after, waits, name=f"rs1_{tag}_wait")
            parts = [_pair_add(arrays[i], arrays[n + i], self.pos, name=f"pair_add_{tag}_{names[i]}")
                     for i in range(n)]
            st['s2'] = _scatter_start(parts, name="rs2_" + tag, deps=deps)
            return st['s2'][2]
        if step == 'W2':
            sems, arrays, _, waits = st.pop('s2')
            arrays = _split_wait(sems, arrays, after, waits, name=f"rs2_{tag}_wait")
            fs = [_chip_sum(arrays[i], arrays[n + i], self.final.get(names[i]), self.pos, layer,
                            name=f"chip_sum_{tag}_{names[i]}") for i in range(n)]
            st['s3'] = _pair_share_start(fs, layer, name="rs3_" + tag)
            return st['s3'][2]
        sems, arrays, _, waits = st.pop('s3')
        self.final.update(zip(names, _split_wait(sems, arrays, after, waits, name=f"rs3_{tag}_wait")))
        return None


def _pack(arrays):
    rows = []
    for a in arrays:
        nel = int(np.prod(a.shape))
        if nel % 1024 == 0:
            rows.append(a.astype(F32).reshape(nel // 128, 128))
        else:
            f = a.reshape(-1).astype(F32)
            rows.append(jnp.pad(f, (0, (-nel) % 1024)).reshape(-1, 128))
    return jnp.concatenate(rows, axis=0)


def _unpack(pack, shapes):
    out, row = [], 0
    for shp in shapes:
        nel = int(np.prod(shp))
        nrow = 8 * -(-nel // 1024)
        part = pack[row:row + nrow]
        out.append(part.reshape(shp) if nel % 1024 == 0 else part.reshape(-1)[:nel].reshape(shp))
        row += nrow
    return out


def kernel(x, positions, norm1, w_in, q_norm, k_norm, sinks, w_pool, pool_scale, sgu_v_norm, w_s, b_s, w_proj_a, w_proj_b, w_proj_c, w_out, norm2, w_up, conv_w, conv_b, w_down, loss_target, m_norm1, m_w_in, m_q_norm, m_k_norm, m_sinks, m_w_pool, m_pool_scale, m_sgu_v_norm, m_w_s, m_b_s, m_w_proj_a, m_w_proj_b, m_w_proj_c, m_w_out, m_norm2, m_w_up, m_conv_w, m_conv_b, m_w_down, v_norm1, v_w_in, v_q_norm, v_k_norm, v_sinks, v_w_pool, v_pool_scale, v_sgu_v_norm, v_w_s, v_b_s, v_w_proj_a, v_w_proj_b, v_w_proj_c, v_w_out, v_norm2, v_w_up, v_conv_w, v_conv_b, v_w_down):
    w = dict(norm1=norm1, w_in=w_in, q_norm=q_norm, k_norm=k_norm, sinks=sinks, w_pool=w_pool, pool_scale=pool_scale,
             sgu_v_norm=sgu_v_norm, w_s=w_s, b_s=b_s, w_proj_a=w_proj_a, w_proj_b=w_proj_b, w_proj_c=w_proj_c,
             w_out=w_out, norm2=norm2, w_up=w_up, conv_w=conv_w, conv_b=conv_b, w_down=w_down)
    m = dict(norm1=m_norm1, w_in=m_w_in, q_norm=m_q_norm, k_norm=m_k_norm, sinks=m_sinks, w_pool=m_w_pool,
             pool_scale=m_pool_scale, sgu_v_norm=m_sgu_v_norm, w_s=m_w_s, b_s=m_b_s, w_proj_a=m_w_proj_a,
             w_proj_b=m_w_proj_b, w_proj_c=m_w_proj_c, w_out=m_w_out, norm2=m_norm2, w_up=m_w_up, conv_w=m_conv_w,
             conv_b=m_conv_b, w_down=m_w_down)
    v = dict(norm1=v_norm1, w_in=v_w_in, q_norm=v_q_norm, k_norm=v_k_norm, sinks=v_sinks, w_pool=v_w_pool,
             pool_scale=v_pool_scale, sgu_v_norm=v_sgu_v_norm, w_s=v_w_s, b_s=v_b_s, w_proj_a=v_w_proj_a,
             w_proj_b=v_w_proj_b, w_proj_c=v_w_proj_c, w_out=v_w_out, norm2=v_norm2, w_up=v_w_up, conv_w=v_conv_w,
             conv_b=v_conv_b, w_down=v_w_down)
    chip = 2 * lax.axis_index("x") + lax.axis_index("y")
    core = lax.axis_index("c")

    pos = jnp.stack([chip, core, 2 * chip + core]).astype(jnp.int32)
    comm = _Comm(w, pos)

    cos, sin = _rope_tables(positions[0])
    sp = {n: w[n] for n in SMALL if n != 'conv_w'}
    loss_row, dx, gs = _local_step(x[0], loss_target[0], cos, sin, sp, comm)

    delta, new_m, new_v, grad_out = {}, {}, {}, {}

    def adamw_big(names, grads):
        for n in names:
            shp = w[n].shape
            two_d = lambda a: a.reshape(shp[0] * shp[1], shp[2])
            d, nm, nv, g = _adamw(two_d(w[n]), two_d(grads[n]), two_d(m[n]), two_d(v[n]),
                                  tr=_row_tile(shp[0] * shp[1], 256), name=f"adamw_{n}", copy_g=True)
            delta[n], new_m[n], new_v[n], grad_out[n] = d.reshape(shp), nm.reshape(shp), nv.reshape(shp), g.reshape(shp)

    small_shapes = [gs[n].shape for n in SMALL] + [(1,)]
    small_pack = _pack([gs[n] for n in SMALL] + [loss_row[0, :1]])
    small = _Gather([_cast_place(small_pack, pos, F32, slots=N_DEV, which=2, name="small_place")], "small_gather",
                    all_devices=True)
    toks = comm.tail(REDUCE_TAIL_A[:1], (dx, small.token))
    comm.tail(REDUCE_TAIL_A[1:], (dx, *toks))
    comm.tail(REDUCE_TAIL_B, dx)
    adamw_big(GROUP_F, comm.shards())
    red = _sum_slots(small.wait(new_v[GROUP_F[-1]])[0], tr=small_pack.shape[0], name="small_sum")
    *small_grads, loss = _unpack(red, small_shapes)
    g_small = dict(zip(SMALL, small_grads))
    comm.tail(REDUCE_TAIL_C, red)
    grads = comm.shards()
    grads.update(g_small)
    shard_cols = conv_w.shape[2]
    grads['conv_w'] = lax.dynamic_slice_in_dim(g_small['conv_w'], chip * shard_cols, shard_cols, axis=2)

    adamw_big(GROUP_M, grads)
    shapes = [w[n].shape for n in SMALL]
    packs = [_pack([src[n] for n in SMALL]) for src in (w, grads, m, v)]
    d, nm, nv = _adamw(*packs, tr=packs[0].shape[0], name="adamw_small")
    for dst, src in ((delta, d), (new_m, nm), (new_v, nv)):
        dst.update(zip(SMALL, _unpack(src, shapes)))

    grads.update(grad_out)
    return (loss[0], dx[None], *[grads[n] for n in WEIGHTS], *[delta[n] for n in WEIGHTS],
            *[new_m[n] for n in WEIGHTS], *[new_v[n] for n in WEIGHTS])
```

```python
import functools
import math

import numpy as np
import jax
import jax.numpy as jnp
from jax import lax
from jax.experimental import pallas as pl
from jax.experimental.pallas import tpu as pltpu

F32 = jnp.float32
MXU_DTYPE = jnp.bfloat16
COMM_DTYPE = jnp.bfloat16
ACT_DTYPE = jnp.bfloat16
HALO = 16

D_MODEL = 1024
DEPTH = 2
HEAD_DIM = 64
POOL_WINDOWS = (2, 4, 8, 16)
POOL_WIDTH = 256
N_Q_HEADS = 8
ATTN_BLOCK = 128
ATTN_WIDTH = 512
KV_WIDTH = 128
CHUNK = 128
SGU_WIDTH = 256
IN_COLS = 4608
GATE_COL0 = 1536
D_FF = 2816
ROPE_THETA = 10000.0
EPS = 1e-6
ADAM_LR, ADAM_B1, ADAM_B2, ADAM_EPS, ADAM_WD, ADAM_STEP = 0.001, 0.9, 0.999, 1e-08, 0.01, 10

N_CHIPS = 4
N_DEV = 8
VMEM_LIMIT_BYTES = 56 * 1024 * 1024
NEG_BIG = -1e30
MESH = pl.DeviceIdType.MESH
ANY = pl.BlockSpec(memory_space=pl.ANY)

SDS = jax.ShapeDtypeStruct


def _cp(*sem):
    return pltpu.CompilerParams(dimension_semantics=sem, vmem_limit_bytes=VMEM_LIMIT_BYTES)


def _dot(a, b, dims=((1,), (0,))):
    return lax.dot_general(a.astype(MXU_DTYPE), b.astype(MXU_DTYPE), (dims, ((), ())),
                           preferred_element_type=F32)


NT = ((1,), (1,))
TN = ((0,), (0,))


def _split_dot(x, m):
    hi = x.astype(MXU_DTYPE)
    lo = (x - hi.astype(F32)).astype(MXU_DTYPE)
    return _dot(hi, m) + _dot(lo, m)


def _seg_matrix(width, seg):
    idx = np.arange(width) // seg
    return jnp.asarray((idx[:, None] == idx[None, :]).astype(np.float32), dtype=MXU_DTYPE)


def _lane(shape):
    return lax.broadcasted_iota(jnp.int32, shape, len(shape) - 1)


def _row(shape):
    return lax.broadcasted_iota(jnp.int32, shape, 0)


def _full(shape):
    nd = len(shape)
    return pl.BlockSpec(shape, lambda *_: (0,) * nd)


def _gelu(x):
    k = math.sqrt(2.0 / math.pi)
    th = jnp.tanh(k * (x + 0.044715 * (x * x * x)))
    return 0.5 * x * (1.0 + th)


def _gelu_and_grad(x):
    k = math.sqrt(2.0 / math.pi)
    x2 = x * x
    th = jnp.tanh(k * (x + 0.044715 * (x2 * x)))
    g = 0.5 * x * (1.0 + th)
    dg = 0.5 * (1.0 + th) + 0.5 * x * (1.0 - th * th) * (k * (1.0 + 3.0 * 0.044715 * x2))
    return g, dg


def _sigmoid(x):
    return 0.5 * jnp.tanh(0.5 * x) + 0.5


def _swap_halves(x):
    w = x.shape[-1]
    first = (_lane(x.shape) % HEAD_DIM) < (HEAD_DIM // 2)
    return jnp.where(first, pltpu.roll(x, w - HEAD_DIM // 2, 1), pltpu.roll(x, HEAD_DIM // 2, 1))


def _tile_lanes(x, reps):
    return x if reps == 1 else jnp.concatenate([x] * reps, axis=1)


def _fold_lanes(x, period):
    w = x.shape[-1]
    while w > period:
        w //= 2
        x = x + pltpu.roll(x, w, 1)
    return x


def _mm(a, b, *, mode, tm, tn, tk, out_dtype=F32, add=None, name,
        a_lead=None, b_lead=None, b_sharded=False, out_into=None,
        b_koff=0, out_joff=0, out_n=None, deps=()):
    ash = a.shape[1:] if a_lead is not None else a.shape
    bsh = b.shape[1:] if b_lead is not None else b.shape
    if b_sharded:
        bsh = (b.shape[1], N_CHIPS * b.shape[2])
    if mode == 'nn':
        (M, K), (K2, N) = ash, bsh
    elif mode == 'nt':
        (M, K), (N, K2) = ash, bsh
    else:
        (K, M), (K2, N) = ash, bsh
    assert K == K2 or (mode == 'nt' and K2 > K), (ash, bsh, mode)
    assert M % tm == 0 and N % tn == 0 and K % tk == 0, (M, N, K, tm, tn, tk)
    nk = K // tk
    dims = {'nn': ((1,), (0,)), 'nt': NT, 'tn': TN}[mode]

    def lead(spec_shape, imap, lead_idx):
        if lead_idx is None:
            return pl.BlockSpec(spec_shape, imap)
        return pl.BlockSpec((None,) + spec_shape, lambda i, j, k: (lead_idx,) + imap(i, j, k))

    if mode == 'tn':
        a_spec = lead((tk, tm), lambda i, j, k: (k, i), a_lead)
    else:
        a_spec = lead((tm, tk), lambda i, j, k: (i, k), a_lead)
    if b_sharded:
        per = b.shape[2] // (tk if mode == 'nt' else tn)
        assert per * (tk if mode == 'nt' else tn) == b.shape[2] and mode != 'tn'
        if mode == 'nt':
            b_spec = pl.BlockSpec((None, tn, tk), lambda i, j, k: ((k + b_koff) // per, j, (k + b_koff) % per))
        else:
            b_spec = pl.BlockSpec((None, tk, tn), lambda i, j, k: (j // per, k, j % per))
    elif mode == 'nt':
        b_spec = lead((tn, tk), lambda i, j, k: (j, k + b_koff), b_lead)
    else:
        b_spec = lead((tk, tn), lambda i, j, k: (k, j), b_lead)
    o_spec = pl.BlockSpec((tm, tn), lambda i, j, k: (i, j + out_joff))
    n_out = N if out_n is None else out_n
    in_specs = [a_spec, b_spec]
    operands = [a, b]
    if add is not None:
        in_specs.append(pl.BlockSpec((tm, tn), lambda i, j, k: (i, j)))
        operands.append(add)
    aliases = {}
    if out_into is not None:
        in_specs.append(ANY)
        operands.append(out_into)
        aliases = {len(operands) - 1: 0}
    in_specs += [ANY] * len(deps)
    operands += list(deps)
    has_add = add is not None
    acc_in_out = nk > 1 and out_dtype == F32

    def body(*refs):
        a_ref, b_ref = refs[0], refs[1]
        pos = 2
        add_ref = None
        if has_add:
            add_ref = refs[pos]
            pos += 1
        if out_into is not None:
            pos += 1
        pos += len(deps)
        o_ref = refs[pos]
        acc_ref = refs[pos + 1] if (nk > 1 and not acc_in_out) else None
        p = _dot(a_ref[...], b_ref[...], dims)
        if nk == 1:
            if has_add:
                p = p + add_ref[...]
            o_ref[...] = p.astype(o_ref.dtype)
            return
        k = pl.program_id(2)
        tgt = o_ref if acc_in_out else acc_ref

        @pl.when(k == 0)
        def _():
            tgt[...] = p + add_ref[...] if has_add else p

        @pl.when(k > 0)
        def _():
            tgt[...] += p

        if not acc_in_out:
            @pl.when(k == nk - 1)
            def _():
                o_ref[...] = acc_ref[...].astype(o_ref.dtype)

    out_shape = SDS((M, n_out), out_dtype)
    scratch = [pltpu.VMEM((tm, tn), F32)] if (nk > 1 and not acc_in_out) else []
    return pl.pallas_call(
        body, grid=(M // tm, N // tn, nk), in_specs=in_specs, out_specs=o_spec, out_shape=out_shape,
        scratch_shapes=scratch, input_output_aliases=aliases, name=name,
        compiler_params=_cp("parallel", "parallel", "arbitrary"))(*operands)


def _rms_bwd_rows(xv, g, dh, dres):
    r = lax.rsqrt(jnp.mean(xv * xv, axis=-1, keepdims=True) + EPS)
    xh = xv * r
    gy = dh * g
    dx = r * (gy - xh * jnp.mean(xh * gy, axis=-1, keepdims=True)) + dres
    return dx, jnp.sum(dh * xh, axis=0, keepdims=True)


def _mm_nt_sharded_rms(a, b, x, g, dres, *, tm, name, deps=()):
    a3 = a if a.ndim == 3 else a[None]
    A, M, ka = a3.shape
    S, N, ns = b.shape
    per = S // A
    assert ka == per * ns and M % tm == 0 and N == x.shape[1], (a3.shape, b.shape, x.shape)

    def body(a_ref, b_ref, x_ref, g_ref, dres_ref, dx_ref, dxb_ref, dg_ref):
        acc = None
        for s in range(S):
            lo = (s % per) * ns
            p = _dot(a_ref[s // per, :, lo:lo + ns], b_ref[s], NT)
            acc = p if acc is None else acc + p
        dx, dg = _rms_bwd_rows(x_ref[...], g_ref[...], acc, dres_ref[...])
        dx_ref[...] = dx
        dxb_ref[...] = dx.astype(dxb_ref.dtype)

        @pl.when(pl.program_id(0) == 0)
        def _():
            dg_ref[...] = jnp.zeros_like(dg_ref)
        dg_ref[...] += dg

    rows = pl.BlockSpec((tm, N), lambda i: (i, 0))
    return pl.pallas_call(
        _after(body, 5, deps), grid=(M // tm,),
        in_specs=[pl.BlockSpec((A, tm, ka), lambda i: (0, i, 0)),
                  pl.BlockSpec((S, N, ns), lambda i: (0, 0, 0), pipeline_mode=pl.Buffered(1)),
                  rows, _full((1, N)), rows] + [ANY] * len(deps),
        out_specs=[rows, rows, _full((1, N))],
        out_shape=[SDS((M, N), F32), SDS((M, N), MXU_DTYPE), SDS((1, N), F32)], name=name,
        compiler_params=_cp("arbitrary"))(a3, b, x, g, dres, *deps)


def _norm_mm(x, g, b, *, tm, tn, name, out_dtype, deps=()):
    M, K = x.shape
    S, K2, ns = b.shape
    per = ns // tn
    assert K == K2 and per * tn == ns and M % tm == 0, (x.shape, b.shape)

    def body(x_ref, g_ref, b_ref, o_ref, h_ref):
        @pl.when(pl.program_id(1) == 0)
        def _():
            xv = x_ref[...]
            r = lax.rsqrt(jnp.mean(xv * xv, axis=-1, keepdims=True) + EPS)
            h_ref[...] = (xv * r * g_ref[...]).astype(h_ref.dtype)
        o_ref[...] = _dot(h_ref[...], b_ref[...]).astype(o_ref.dtype)

    return pl.pallas_call(
        _after(body, 3, deps), grid=(M // tm, S * per),
        in_specs=[pl.BlockSpec((tm, K), lambda i, j: (i, 0)), _full((1, K)),
                  pl.BlockSpec((None, K, tn), lambda i, j: (j // per, 0, j % per))] + [ANY] * len(deps),
        out_specs=[pl.BlockSpec((tm, tn), lambda i, j: (i, j)), pl.BlockSpec((tm, K), lambda i, j: (i, 0))],
        out_shape=[SDS((M, S * ns), out_dtype), SDS((M, K), MXU_DTYPE)], name=name,
        compiler_params=_cp("parallel", "arbitrary"))(x, g, b, *deps)


def _after(body, n_in, deps):
    nd = len(deps)
    if nd == 0:
        return body
    return lambda *refs: body(*refs[:n_in], *refs[n_in + nd:])


def _down_proj_loss(act, w, x1, target, *, tm, name):
    T, K = act.shape
    D = w.shape[1]

    def body(a_ref, w_ref, x_ref, t_ref, loss_ref, dy_ref, dyb_ref):
        i = pl.program_id(0)
        d = (x_ref[...] + _dot(a_ref[...], w_ref[...])) - t_ref[...]
        dy = d * (1.0 / D)
        dy_ref[...] = dy
        dyb_ref[...] = dy.astype(dyb_ref.dtype)
        part = jnp.sum(jnp.sum(d * d, axis=1, keepdims=True), axis=0, keepdims=True) * (0.5 / D)

        @pl.when(i == 0)
        def _():
            loss_ref[...] = jnp.zeros_like(loss_ref)
        loss_ref[...] += jnp.broadcast_to(part, loss_ref.shape)

    rows = pl.BlockSpec((tm, D), lambda i: (i, 0))
    return pl.pallas_call(
        body, grid=(T // tm,), in_specs=[pl.BlockSpec((tm, K), lambda i: (i, 0)), _full((K, D)), rows, rows],
        out_specs=[_full((1, 128)), rows, rows],
        out_shape=[SDS((1, 128), F32), SDS((T, D), F32), SDS((T, D), MXU_DTYPE)],
        name=name, compiler_params=_cp("arbitrary"))(act, w, x1, target)


def _pool_lane_consts(shape):
    lane = _lane(shape)
    grp = lane // (POOL_WIDTH // 4)
    win = jnp.where(grp == 0, 2, jnp.where(grp == 1, 4, jnp.where(grp == 2, 8, 16)))
    return grp, win


def _pool_select(grp, s2, s4, s8, s16):
    return jnp.where(grp == 0, s2, jnp.where(grp == 1, s4, jnp.where(grp == 2, s8, s16)))


def _pool_diff(xe, row0, tr):
    s2 = xe + pltpu.roll(xe, 1, 0)
    s4 = s2 + pltpu.roll(s2, 2, 0)
    s8 = s4 + pltpu.roll(s4, 4, 0)
    s16 = s8 + pltpu.roll(s8, 8, 0)
    shape = (tr, POOL_WIDTH)
    grp, win = _pool_lane_consts(shape)
    sums = _pool_select(grp, s2[16:], s4[16:], s8[16:], s16[16:])
    t = row0 + _row(shape)
    cnt = jnp.minimum(t + 1, win).astype(F32)
    return sums / cnt - xe[16:]


def _pool_fwd(z, wbd, scale, *, tr, name):
    T = z.shape[0]
    hb = tr // 16

    def body(x_ref, xp_ref, w_ref, s_ref, o_ref):
        i = pl.program_id(0)
        halo = jnp.where(i == 0, 0.0, xp_ref[...].astype(F32))
        diff = _pool_diff(jnp.concatenate([halo, x_ref[...].astype(F32)], axis=0), i * tr, tr)
        o_ref[...] = (_dot(diff, w_ref[...]) * s_ref[...]).astype(o_ref.dtype)

    return pl.pallas_call(
        body, grid=(T // tr,),
        in_specs=[pl.BlockSpec((tr, POOL_WIDTH), lambda i: (i, 0)),
                  pl.BlockSpec((16, POOL_WIDTH), lambda i: (jnp.maximum(i * hb - 1, 0), 0)),
                  _full((POOL_WIDTH, POOL_WIDTH)), _full((1, POOL_WIDTH))],
        out_specs=pl.BlockSpec((tr, POOL_WIDTH), lambda i: (i, 0)),
        out_shape=SDS((T, POOL_WIDTH), MXU_DTYPE), name=name, compiler_params=_cp("parallel"))(z, z, wbd, scale)


def _pool_bwd_tile(i, n, tr, x, xprev, dpa, dpa_next, wbd, scale):
    halo = jnp.where(i == 0, 0.0, xprev)
    diff = _pool_diff(jnp.concatenate([halo, x], axis=0), i * tr, tr)
    mixed = _dot(diff, wbd)
    dscale = jnp.sum(dpa * mixed, axis=0, keepdims=True)
    dnext = jnp.where(i == n - 1, 0.0, dpa_next)
    dmix_e = jnp.concatenate([dpa, dnext], axis=0) * scale
    ddiff_e = _dot(dmix_e, wbd, NT)
    dwbd = _dot(diff, dmix_e[:tr], TN)
    shape = (tr + 16, POOL_WIDTH)
    grp, win = _pool_lane_consts(shape)
    t = i * tr + _row(shape)
    e = ddiff_e / jnp.minimum(t + 1, win).astype(F32)
    nrow = tr + 16
    a2 = e + pltpu.roll(e, nrow - 1, 0)
    a4 = a2 + pltpu.roll(a2, nrow - 2, 0)
    a8 = a4 + pltpu.roll(a4, nrow - 4, 0)
    a16 = a8 + pltpu.roll(a8, nrow - 8, 0)
    dx = _pool_select(grp, a2, a4, a8, a16)[:tr] - ddiff_e[:tr]
    return dx, dwbd, dscale


def _norm_rope(x, g, cos, sin_signed, seg):
    reps = x.shape[1] // 128
    ms = _split_dot(x * x, seg) * (1.0 / HEAD_DIM)
    r = lax.rsqrt(ms + EPS)
    xn = x * r * g
    c, s = _tile_lanes(cos, reps), _tile_lanes(sin_signed, reps)
    return xn * c + _swap_halves(xn) * s


def _norm_rope_bwd(x, g, cos, sin_signed, seg, dout):
    reps = x.shape[1] // 128
    c, s = _tile_lanes(cos, reps), _tile_lanes(sin_signed, reps)
    dxn = dout * c + _swap_halves(dout * s)
    ms = _split_dot(x * x, seg) * (1.0 / HEAD_DIM)
    r = lax.rsqrt(ms + EPS)
    xh = x * r
    gy = dxn * g
    dx = r * (gy - xh * (_split_dot(xh * gy, seg) * (1.0 / HEAD_DIM)))
    dg = jnp.sum(dxn * xh, axis=0, keepdims=True)
    return dx, dg


def _dup_heads(k):
    first = _lane(k.shape) < HEAD_DIM
    kr = pltpu.roll(k, HEAD_DIM, 1)
    return jnp.concatenate([jnp.where(first, k, kr), jnp.where(first, kr, k)], axis=1)


def _qkv_prep(z, cos, sin_signed, gq, gk, seg, *, tr, name, deps=()):
    T = z.shape[0]

    def body(qa_ref, qb_ref, kv_ref, c_ref, s_ref, gq_ref, gk_ref, seg_ref, q_ref, k_ref, v_ref):
        c, s, seg_m = c_ref[...], s_ref[...], seg_ref[...]
        scale = HEAD_DIM ** -0.5
        qa = _norm_rope(qa_ref[...].astype(F32), gq_ref[...], c, s, seg_m) * scale
        qb = _norm_rope(qb_ref[...].astype(F32), gq_ref[...], c, s, seg_m) * scale
        q_ref[...] = jnp.concatenate([qa, qb], axis=1).astype(q_ref.dtype)
        kv = kv_ref[...].astype(F32)
        k = _norm_rope(kv[:, :KV_WIDTH], gk_ref[...], c, s, seg_m[:128, :128])
        k_ref[...] = _dup_heads(k).astype(k_ref.dtype)
        v_ref[...] = _dup_heads(kv[:, KV_WIDTH:]).astype(v_ref.dtype)

    col = lambda j: pl.BlockSpec((tr, 256), lambda i: (i, j))
    tab = pl.BlockSpec((tr, 128), lambda i: (i, 0))
    return pl.pallas_call(
        _after(body, 8, deps), grid=(T // tr,),
        in_specs=[col(1), col(2), col(3), tab, tab, _full((1, 256)), _full((1, 128)), _full((256, 256))]
        + [ANY] * len(deps),
        out_specs=[pl.BlockSpec((tr, 512), lambda i: (i, 0)), col(0), col(0)],
        out_shape=[SDS((T, 512), MXU_DTYPE), SDS((T, 256), MXU_DTYPE), SDS((T, 256), MXU_DTYPE)],
        name=name, compiler_params=_cp("parallel"))(z, z, z, cos, sin_signed, gq, gk, seg, *deps)


GROUP_HEADS = 4
GROUP_ROWS = GROUP_HEADS * ATTN_BLOCK
ALL_ROWS = N_Q_HEADS * ATTN_BLOCK


def _attn_mask(has_prev):
    qi = _row((ALL_ROWS, 2 * ATTN_BLOCK)) % ATTN_BLOCK
    kj = _lane((ALL_ROWS, 2 * ATTN_BLOCK))
    return (kj > qi) & (kj <= qi + ATTN_BLOCK) & ((kj >= ATTN_BLOCK) | has_prev)


FWD_STEP_BLOCKS = 8
BWD_STEP_BLOCKS = 2


def _band(prev, cur, blk):
    lo = cur[(blk - 1) * ATTN_BLOCK:blk * ATTN_BLOCK] if blk else prev
    return jnp.concatenate([lo, cur[blk * ATTN_BLOCK:(blk + 1) * ATTN_BLOCK]], axis=0)


def _stack_heads(x, g):
    first = _lane((ATTN_BLOCK, 128)) < HEAD_DIM
    parts = []
    for pair in (2 * g, 2 * g + 1):
        x128 = x[:, 128 * pair:128 * (pair + 1)]
        zero = jnp.zeros_like(x128)
        parts += [jnp.where(first, x128, zero), jnp.where(first, zero, x128)]
    return jnp.concatenate(parts, axis=0)


def _unstack_heads(y):
    first = _lane((ATTN_BLOCK, 128)) < HEAD_DIM
    b = ATTN_BLOCK
    return jnp.concatenate([jnp.where(first, y[0:b], y[b:2 * b]), jnp.where(first, y[2 * b:3 * b], y[3 * b:4 * b])],
                           axis=1)


def _sink_col(sk_ref):
    return jnp.concatenate([jnp.broadcast_to(sk_ref[h:h + 1, 0:1], (ATTN_BLOCK, 1)) for h in range(N_Q_HEADS)],
                           axis=0)


def _by_group(a8, b2, dims=((1,), (0,))):
    return jnp.concatenate([_dot(a8[:GROUP_ROWS], b2[:, :128], dims), _dot(a8[GROUP_ROWS:], b2[:, 128:], dims)],
                           axis=0)


def _softmax_exp(q8, k2, mask, sink):
    s = jnp.where(mask, _by_group(q8, k2, NT), NEG_BIG)
    m = jnp.maximum(jnp.max(s, axis=1, keepdims=True), sink)
    p = jnp.exp(s - m)
    ps = jnp.exp(sink - m)
    return p, ps, 1.0 / (jnp.sum(p, axis=1, keepdims=True) + ps)


def _attn_fwd(q, k, v, sinks_b, *, name):
    T = q.shape[0]
    nb = T // ATTN_BLOCK
    STEP_BLOCKS = min(FWD_STEP_BLOCKS, nb)
    STEP_ROWS = STEP_BLOCKS * ATTN_BLOCK

    def body(q_ref, kc_ref, kp_ref, vc_ref, vp_ref, sk_ref, o_ref):
        n = pl.program_id(0)
        kc, kp, vc, vp = kc_ref[...], kp_ref[...], vc_ref[...], vp_ref[...]
        sink = _sink_col(sk_ref)
        for blk in range(STEP_BLOCKS):
            rows = slice(blk * ATTN_BLOCK, (blk + 1) * ATTN_BLOCK)
            mask = _attn_mask((n > 0) if blk == 0 else True)
            k2, v2 = _band(kp, kc, blk), _band(vp, vc, blk)
            qv = q_ref[rows, :]
            q8 = jnp.concatenate([_stack_heads(qv, 0), _stack_heads(qv, 1)], axis=0)
            p, _, inv = _softmax_exp(q8, k2, mask, sink)
            o8 = _by_group(p, v2) * inv
            o_ref[rows, :] = jnp.concatenate([_unstack_heads(o8[:GROUP_ROWS]), _unstack_heads(o8[GROUP_ROWS:])],
                                             axis=1).astype(o_ref.dtype)

    cur = lambda w: pl.BlockSpec((STEP_ROWS, w), lambda n: (n, 0))
    prev = lambda w: pl.BlockSpec((ATTN_BLOCK, w), lambda n: (jnp.maximum(STEP_BLOCKS * n - 1, 0), 0))
    return pl.pallas_call(
        body, grid=(nb // STEP_BLOCKS,),
        in_specs=[cur(512), cur(256), prev(256), cur(256), prev(256), _full((8, 128))],
        out_specs=cur(512), out_shape=SDS((T, 512), MXU_DTYPE), name=name,
        compiler_params=_cp("parallel"))(q, k, k, v, v, sinks_b)


def _attn_bwd(q, k, v, sinks_b, do, *, name, deps=()):
    T = q.shape[0]
    nb = T // ATTN_BLOCK
    STEP_BLOCKS = min(BWD_STEP_BLOCKS, nb)
    STEP_ROWS = STEP_BLOCKS * ATTN_BLOCK

    def body(q_ref, kc_ref, kp_ref, vc_ref, vp_ref, sk_ref, do_ref,
             dq_ref, dkc_ref, dkp_ref, dvc_ref, dvp_ref, dsk_ref):
        n = pl.program_id(0)
        kc, kp, vc, vp = kc_ref[...], kp_ref[...], vc_ref[...], vp_ref[...]
        sink = _sink_col(sk_ref)

        @pl.when(n == 0)
        def _():
            dsk_ref[...] = jnp.zeros_like(dsk_ref)

        for blk in range(STEP_BLOCKS):
            rows = slice(blk * ATTN_BLOCK, (blk + 1) * ATTN_BLOCK)
            mask = _attn_mask((n > 0) if blk == 0 else True)
            k2, v2 = _band(kp, kc, blk), _band(vp, vc, blk)
            qv, dov = q_ref[rows, :], do_ref[rows, :]
            q8 = jnp.concatenate([_stack_heads(qv, 0), _stack_heads(qv, 1)], axis=0)
            do8 = jnp.concatenate([_stack_heads(dov, 0), _stack_heads(dov, 1)], axis=0)
            p, ps, inv = _softmax_exp(q8, k2, mask, sink)
            pn = p * inv
            delta = jnp.sum(do8 * _by_group(pn, v2), axis=1, keepdims=True)
            ds = pn * (_by_group(do8, v2, NT) - delta)
            dq8 = _by_group(ds, k2)
            dq_ref[rows, :] = jnp.concatenate([_unstack_heads(dq8[:GROUP_ROWS]), _unstack_heads(dq8[GROUP_ROWS:])],
                                              axis=1)
            dk = jnp.concatenate([_dot(ds[:GROUP_ROWS], q8[:GROUP_ROWS], TN),
                                  _dot(ds[GROUP_ROWS:], q8[GROUP_ROWS:], TN)], axis=1)
            dv = jnp.concatenate([_dot(pn[:GROUP_ROWS], do8[:GROUP_ROWS], TN),
                                  _dot(pn[GROUP_ROWS:], do8[GROUP_ROWS:], TN)], axis=1)
            wsink = (ps * inv) * delta
            for h in range(N_Q_HEADS):
                dsink = -jnp.sum(wsink[ATTN_BLOCK * h:ATTN_BLOCK * (h + 1)], axis=0, keepdims=True)
                dsk_ref[h:h + 1, :] += jnp.broadcast_to(dsink, (1, 128))
            dkp_ref[rows, :] = dk[:ATTN_BLOCK]
            dkc_ref[rows, :] = dk[ATTN_BLOCK:]
            dvp_ref[rows, :] = dv[:ATTN_BLOCK]
            dvc_ref[rows, :] = dv[ATTN_BLOCK:]

    cur = lambda w: pl.BlockSpec((STEP_ROWS, w), lambda n: (n, 0))
    prev = lambda w: pl.BlockSpec((ATTN_BLOCK, w), lambda n: (jnp.maximum(STEP_BLOCKS * n - 1, 0), 0))
    f = lambda w: SDS((T, w), F32)
    return pl.pallas_call(
        _after(body, 7, deps), grid=(nb // STEP_BLOCKS,),
        in_specs=[cur(512), cur(256), prev(256), cur(256), prev(256), _full((8, 128)), cur(512)] + [ANY] * len(deps),
        out_specs=[cur(512), cur(256), cur(256), cur(256), cur(256), _full((8, 128))],
        out_shape=[f(512), f(256), f(256), f(256), f(256), SDS((8, 128), F32)],
        name=name, compiler_params=_cp("arbitrary"))(q, k, k, v, v, sinks_b, do, *deps)


def _mixer_ab_bwd(z, cos, sin_signed, gq, gk, seg, dq, dkc, dkp, dvc, dvp, dpa, wbd, scale, dz, *, tr, name, deps=()):
    T = z.shape[0]
    n = T // tr
    hb = tr // 16
    ab = tr // ATTN_BLOCK

    def unfold(cur, nxt_tile, nxt_halo, i):
        nxt = jnp.concatenate([nxt_tile[ATTN_BLOCK:], jnp.where(i == n - 1, 0.0, nxt_halo)], axis=0)
        tot = cur + nxt
        first = _lane((tr, 128)) < HEAD_DIM
        a = tot[:, :128]
        b = tot[:, 128:]
        a = a + pltpu.roll(a, HEAD_DIM, 1)
        b = b + pltpu.roll(b, HEAD_DIM, 1)
        return jnp.where(first, a, b)

    def body(xp_ref, xpp_ref, qa_ref, qb_ref, kv_ref, c_ref, s_ref, gq_ref, gk_ref, seg_ref,
             dq_ref, dkc_ref, dkp_ref, dkh_ref, dvc_ref, dvp_ref, dvh_ref, dpa_ref, dpan_ref, w_ref, sc_ref, _dz_in,
             dz_ref, dgq_ref, dgk_ref, dw_ref, dsc_ref):
        i = pl.program_id(0)
        c, s, seg_m = c_ref[...], s_ref[...], seg_ref[...]
        scale_q = HEAD_DIM ** -0.5
        dqv = dq_ref[...] * scale_q
        dxa, dga = _norm_rope_bwd(qa_ref[...].astype(F32), gq_ref[...], c, s, seg_m, dqv[:, :256])
        dxb, dgb = _norm_rope_bwd(qb_ref[...].astype(F32), gq_ref[...], c, s, seg_m, dqv[:, 256:])
        dk = unfold(dkc_ref[...], dkp_ref[...], dkh_ref[...], i)
        dv = unfold(dvc_ref[...], dvp_ref[...], dvh_ref[...], i)
        kv = kv_ref[...].astype(F32)
        dxk, dgk = _norm_rope_bwd(kv[:, :KV_WIDTH], gk_ref[...], c, s, seg_m[:128, :128], dk)
        dxp, dwbd, dscale = _pool_bwd_tile(i, n, tr, xp_ref[...].astype(F32), xpp_ref[...].astype(F32),
                                           dpa_ref[...], dpan_ref[...],
                                           w_ref[...], sc_ref[...])
        dz_ref[...] = jnp.concatenate([dxp, dxa, dxb, dxk, dv], axis=1).astype(dz_ref.dtype)

        @pl.when(i == 0)
        def _():
            dgq_ref[...] = jnp.zeros_like(dgq_ref)
            dgk_ref[...] = jnp.zeros_like(dgk_ref)
            dw_ref[...] = jnp.zeros_like(dw_ref)
            dsc_ref[...] = jnp.zeros_like(dsc_ref)
        dgq_ref[...] += _fold_lanes(dga + dgb, HEAD_DIM)
        dgk_ref[...] += _fold_lanes(dgk, HEAD_DIM)
        dw_ref[...] += dwbd
        dsc_ref[...] += dscale

    col = lambda j: pl.BlockSpec((tr, 256), lambda i: (i, j))
    rows = lambda w: pl.BlockSpec((tr, w), lambda i: (i, 0))
    nxt_blk = pl.BlockSpec((ATTN_BLOCK, 256), lambda i: (jnp.minimum((i + 1) * ab, T // ATTN_BLOCK - 1), 0))
    prev16 = pl.BlockSpec((16, 256), lambda i: (jnp.maximum(i * hb - 1, 0), 0))
    next16 = pl.BlockSpec((16, 256), lambda i: (jnp.minimum((i + 1) * hb, T // 16 - 1), 0))
    return pl.pallas_call(
        _after(body, 22, deps), grid=(n,),
        in_specs=[col(0), prev16, col(1), col(2), col(3), rows(128), rows(128),
                  _full((1, 256)), _full((1, 128)), _full((256, 256)),
                  rows(512), rows(256), rows(256), nxt_blk, rows(256), rows(256), nxt_blk,
                  rows(256), next16, _full((256, 256)), _full((1, 256)), ANY] + [ANY] * len(deps),
        out_specs=[rows(1024), _full((1, 256)), _full((1, 128)), _full((256, 256)), _full((1, 256))],
        out_shape=[SDS((T, IN_COLS), MXU_DTYPE), SDS((1, 256), F32), SDS((1, 128), F32),
                   SDS((256, 256), F32), SDS((1, 256), F32)],
        input_output_aliases={21: 0}, name=name, compiler_params=_cp("arbitrary"))(
            z, z, z, z, z, cos, sin_signed, gq, gk, seg, dq, dkc, dkp, dkp, dvc, dvp, dvp, dpa, dpa, wbd, scale, dz,
            *deps)


def _sgu_common(zu, zv, vn, seg):
    u, du = _gelu_and_grad(zu)
    gv, dgv = _gelu_and_grad(zv)
    ms = _split_dot(gv * gv, seg) * (1.0 / HEAD_DIM)
    r = lax.rsqrt(ms + EPS)
    xh = gv * r
    return u, du, dgv, r, xh, xh * vn


def _sgu_fwd(z, wtril, bexp, vn, seg, *, tr, name):
    T = z.shape[0]
    nch = tr // CHUNK

    def body(u_ref, v_ref, w_ref, b_ref, vn_ref, seg_ref, o_ref):
        u, _, _, _, _, vg = _sgu_common(u_ref[...].astype(F32), v_ref[...].astype(F32), vn_ref[...], seg_ref[...])
        grp = _lane((CHUNK, SGU_WIDTH)) // HEAD_DIM
        outs = []
        for ch in range(nch):
            vc = vg[ch * CHUNK:(ch + 1) * CHUNK]
            s = b_ref[...]
            for g in range(4):
                s = s + jnp.where(grp == g, _dot(w_ref[g], vc), 0.0)
            outs.append(u[ch * CHUNK:(ch + 1) * CHUNK] * s)
        o_ref[...] = jnp.concatenate(outs, axis=0).astype(o_ref.dtype)

    col = lambda j: pl.BlockSpec((tr, 256), lambda i: (i, j))
    return pl.pallas_call(
        body, grid=(T // tr,),
        in_specs=[col(4), col(5), _full((4, CHUNK, CHUNK)), _full((CHUNK, 256)), _full((1, 256)), _full((256, 256))],
        out_specs=col(0), out_shape=SDS((T, SGU_WIDTH), MXU_DTYPE), name=name,
        compiler_params=_cp("parallel"))(z, z, wtril, bexp, vn, seg)


def _sgu_bwd(z, wtril, bexp, vn, seg, dsg, dz, *, tr, name):
    T = z.shape[0]
    nch = tr // CHUNK

    def body(u_ref, v_ref, w_ref, b_ref, vn_ref, seg_ref, d_ref, _dz_in, dz_ref, dw_ref, db_ref, dvn_ref):
        i = pl.program_id(0)
        seg_m = seg_ref[...]
        vn_v = vn_ref[...]
        u, du, dgv, r, xh, vg = _sgu_common(u_ref[...].astype(F32), v_ref[...].astype(F32), vn_v, seg_m)
        d = d_ref[...]
        grp = _lane((CHUNK, SGU_WIDTH)) // HEAD_DIM
        tril = _row((CHUNK, CHUNK)) >= _lane((CHUNK, CHUNK))

        @pl.when(i == 0)
        def _():
            dw_ref[...] = jnp.zeros_like(dw_ref)
            db_ref[...] = jnp.zeros_like(db_ref)
            dvn_ref[...] = jnp.zeros_like(dvn_ref)

        dus, dvgs = [], []
        for ch in range(nch):
            sl = slice(ch * CHUNK, (ch + 1) * CHUNK)
            vc = vg[sl]
            s = b_ref[...]
            for g in range(4):
                s = s + jnp.where(grp == g, _dot(w_ref[g], vc), 0.0)
            dus.append(d[sl] * s)
            ds = d[sl] * u[sl]
            db_ref[...] += _split_dot(ds, seg_m)
            dvg = jnp.zeros((CHUNK, SGU_WIDTH), F32)
            for g in range(4):
                dsm = jnp.where(grp == g, ds, 0.0)
                dvg = dvg + jnp.where(grp == g, _dot(w_ref[g], ds, TN), 0.0)
                dw_ref[g] += jnp.where(tril, _dot(dsm, vc, NT), 0.0)
            dvgs.append(dvg)
        dup = jnp.concatenate(dus, axis=0)
        dvg = jnp.concatenate(dvgs, axis=0)
        dvn_ref[...] += _fold_lanes(jnp.sum(dvg * xh, axis=0, keepdims=True), HEAD_DIM)
        gy = dvg * vn_v
        dgvv = r * (gy - xh * (_split_dot(xh * gy, seg_m) * (1.0 / HEAD_DIM)))
        dz_ref[...] = jnp.concatenate([dup * du, dgvv * dgv], axis=1).astype(dz_ref.dtype)

    col = lambda j: pl.BlockSpec((tr, 256), lambda i: (i, j))
    return pl.pallas_call(
        body, grid=(T // tr,),
        in_specs=[col(4), col(5), _full((4, CHUNK, CHUNK)), _full((CHUNK, 256)), _full((1, 256)), _full((256, 256)),
                  col(0), ANY],
        out_specs=[pl.BlockSpec((tr, 512), lambda i: (i, 2)), _full((4, CHUNK, CHUNK)), _full((CHUNK, 256)),
                   _full((1, 256))],
        out_shape=[SDS((T, IN_COLS), MXU_DTYPE), SDS((4, CHUNK, CHUNK), F32), SDS((CHUNK, 256), F32),
                   SDS((1, 256), F32)],
        input_output_aliases={7: 0}, name=name, compiler_params=_cp("arbitrary"))(
            z, z, wtril, bexp, vn, seg, dsg, dz)


def _merge_fwd(pa, at, sg, wa, wb, wc, z, x, w_out, *, tm, tn, name):
    T = pa.shape[0]
    gb = GATE_COL0 // tn
    nb = D_MODEL // tn

    def body(pa_ref, at_ref, sg_ref, wa_ref, wb_ref, wc_ref, g0_ref, g1_ref, g2_ref, x_ref, wo_ref,
             m_ref, y_ref, x1_ref):
        j = pl.program_id(1)
        acc = None
        for idx, (op_ref, w_ref, g_ref) in enumerate(((pa_ref, wa_ref, g0_ref), (at_ref, wb_ref, g1_ref),
                                                      (sg_ref, wc_ref, g2_ref))):
            y = _dot(op_ref[...], w_ref[...])
            y_ref[idx] = y.astype(y_ref.dtype)
            t = _sigmoid(g_ref[...].astype(F32)) * y
            acc = t if acc is None else acc + t
        merged = acc.astype(m_ref.dtype)
        m_ref[...] = merged
        p = _dot(merged, wo_ref[...])

        @pl.when(j == 0)
        def _():
            x1_ref[...] = x_ref[...] + p

        @pl.when(j > 0)
        def _():
            x1_ref[...] += p

    op = lambda w: pl.BlockSpec((tm, w), lambda i, j: (i, 0))
    wt = lambda k: pl.BlockSpec((k, tn), lambda i, j: (0, j))
    gate = lambda b: pl.BlockSpec((tm, tn), lambda i, j: (i, gb + b * nb + j))
    return pl.pallas_call(
        body, grid=(T // tm, nb),
        in_specs=[op(256), op(512), op(256), wt(256), wt(512), wt(256), gate(0), gate(1), gate(2),
                  op(D_MODEL), pl.BlockSpec((tn, D_MODEL), lambda i, j: (j, 0))],
        out_specs=[pl.BlockSpec((tm, tn), lambda i, j: (i, j)), pl.BlockSpec((3, tm, tn), lambda i, j: (0, i, j)),
                   op(D_MODEL)],
        out_shape=[SDS((T, D_MODEL), MXU_DTYPE), SDS((3, T, D_MODEL), MXU_DTYPE), SDS((T, D_MODEL), F32)],
        name=name, compiler_params=_cp("parallel", "arbitrary"))(pa, at, sg, wa, wb, wc, z, z, z, x, w_out)


def _out_dx_merge_bwd(dxb, w_out, y, z, ws, xs, *, tm, tn, name):
    T = dxb.shape[0]
    gb = GATE_COL0 // tn
    nb = D_MODEL // tn
    nr = T // tm
    widths = [w.shape[0] for w in ws]

    def body(dx_ref, w_ref, y_ref, g_ref, *refs):
        w_refs, x_refs = refs[0:3], refs[3:6]
        dz_ref, dx_refs, dw_refs = refs[6], refs[7:10], refs[10:13]
        dm_ref, acc_refs = refs[13], refs[14:17]
        i, b, j = pl.program_id(0), pl.program_id(1), pl.program_id(2)

        @pl.when((b == 0) & (j == 0))
        def _():
            dm = _dot(dx_ref[...], w_ref[...], NT)
            for jj in range(nb):
                dm_ref[jj] = dm[:, jj * tn:(jj + 1) * tn]

        d = dm_ref[j]
        g = _sigmoid(g_ref[...].astype(F32))
        dy = (d * g).astype(MXU_DTYPE)
        dz_ref[...] = (d * y_ref[...].astype(F32) * g * (1.0 - g)).astype(dz_ref.dtype)
        for branch in range(3):
            @pl.when(b == branch)
            def _():
                p = _dot(dy, w_refs[branch][...], NT)
                q = _dot(x_refs[branch][...], dy, TN)

                @pl.when(j == 0)
                def _():
                    dx_refs[branch][...] = p

                @pl.when(j > 0)
                def _():
                    dx_refs[branch][...] += p

                @pl.when(i == 0)
                def _():
                    acc_refs[branch][j] = q

                @pl.when(i > 0)
                def _():
                    acc_refs[branch][j] += q

        @pl.when((i == nr - 1) & (b == 2) & (j == nb - 1))
        def _():
            for branch in range(3):
                for jj in range(nb):
                    dw_refs[branch][:, jj * tn:(jj + 1) * tn] = acc_refs[branch][jj]

    wspec = lambda k: pl.BlockSpec((k, tn), lambda i, b, j: (0, j))
    rows = lambda k: pl.BlockSpec((tm, k), lambda i, b, j: (i, 0))
    return pl.pallas_call(
        body, grid=(nr, 3, nb),
        in_specs=[rows(D_MODEL),
                  pl.BlockSpec((D_MODEL, D_MODEL), lambda i, b, j: (0, 0), pipeline_mode=pl.Buffered(1)),
                  pl.BlockSpec((None, tm, tn), lambda i, b, j: (b, i, j)),
                  pl.BlockSpec((tm, tn), lambda i, b, j: (i, gb + b * nb + j))]
        + [wspec(k) for k in widths] + [rows(k) for k in widths],
        out_specs=[pl.BlockSpec((tm, tn), lambda i, b, j: (i, gb + b * nb + j))]
        + [rows(k) for k in widths] + [_full((k, D_MODEL)) for k in widths],
        out_shape=[SDS((T, IN_COLS), MXU_DTYPE)] + [SDS((T, k), F32) for k in widths]
        + [SDS((k, D_MODEL), F32) for k in widths],
        scratch_shapes=[pltpu.VMEM((nb, tm, tn), F32)] + [pltpu.VMEM((nb, k, tn), F32) for k in widths],
        name=name, compiler_params=_cp("arbitrary", "arbitrary", "arbitrary"))(dxb, w_out, y, z, *ws, *xs)


def _conv3(xe, w, b):
    return (w[0:1] * pltpu.roll(xe, 2, 0) + w[1:2] * pltpu.roll(xe, 1, 0) + w[2:3] * xe)[8:] + b


def _conv_act_fwd(up, cw, cb, *, tr, tc, name):
    T = up.shape[0]
    nc = D_FF // tc
    hb = tr // HALO

    def body(ug_ref, ugp_ref, uv_ref, uvp_ref, wg_ref, wv_ref, bg_ref, bv_ref, o_ref):
        i = pl.program_id(1)
        first = i == 0

        def halo_tile(prev_ref, cur_ref):
            prev8 = prev_ref[...].astype(F32)[HALO - 8:]
            return jnp.concatenate([jnp.where(first, 0.0, prev8), cur_ref[...].astype(F32)], axis=0)

        cg = _conv3(halo_tile(ugp_ref, ug_ref), wg_ref[...], bg_ref[...])
        cv = _conv3(halo_tile(uvp_ref, uv_ref), wv_ref[...], bv_ref[...])
        o_ref[...] = (cg * _sigmoid(cg) * cv).astype(o_ref.dtype)

    tile = lambda off: pl.BlockSpec((tr, tc), lambda j, i: (i, off + j))
    prev = lambda off: pl.BlockSpec((HALO, tc), lambda j, i: (jnp.maximum(i * hb - 1, 0), off + j))
    par = lambda rows, off: pl.BlockSpec((rows, tc), lambda j, i: (0, off + j))
    return pl.pallas_call(
        body, grid=(nc, T // tr),
        in_specs=[tile(0), prev(0), tile(nc), prev(nc), par(3, 0), par(3, nc), par(1, 0), par(1, nc)],
        out_specs=pl.BlockSpec((tr, tc), lambda j, i: (i, j)),
        out_shape=SDS((T, D_FF), MXU_DTYPE), name=name,
        compiler_params=_cp("parallel", "parallel"))(up, up, up, up, cw, cw, cb, cb)


def _conv_act_bwd(up, cw, cb, dact, *, tr, tc, name, deps=()):
    T = up.shape[0]
    nc = D_FF // tc
    hb = tr // 8
    hbu = tr // HALO
    nr = T // tr

    def body(ug_ref, ugp_ref, ugn_ref, uv_ref, uvp_ref, uvn_ref, da_ref, dan_ref, wg_ref, wv_ref, bg_ref, bv_ref,
             du_ref, dwg_ref, dwv_ref, dbg_ref, dbv_ref):
        i = pl.program_id(1)
        first, last = i == 0, i == nr - 1
        da = jnp.concatenate([da_ref[...], jnp.where(last, 0.0, dan_ref[...])], axis=0)

        def with_halos(prev_ref, cur_ref, next_ref):
            prev8 = prev_ref[...].astype(F32)[HALO - 8:]
            next8 = next_ref[...].astype(F32)[:8]
            return jnp.concatenate([jnp.where(first, 0.0, prev8), cur_ref[...].astype(F32), next8], axis=0)

        uge = with_halos(ugp_ref, ug_ref, ugn_ref)
        uve = with_halos(uvp_ref, uv_ref, uvn_ref)
        wg, wv = wg_ref[...], wv_ref[...]
        ug1, ug2 = pltpu.roll(uge, 1, 0)[8:], pltpu.roll(uge, 2, 0)[8:]
        uv1, uv2 = pltpu.roll(uve, 1, 0)[8:], pltpu.roll(uve, 2, 0)[8:]
        cg = wg[0:1] * ug2 + wg[1:2] * ug1 + wg[2:3] * uge[8:] + bg_ref[...]
        cv = wv[0:1] * uv2 + wv[1:2] * uv1 + wv[2:3] * uve[8:] + bv_ref[...]
        sg = _sigmoid(cg)
        dcg = da * cv * (sg * (1.0 + cg * (1.0 - sg)))
        dcv = da * (cg * sg)
        nrow = tr + 8

        def back(dc, w):
            return (w[2:3] * dc + w[1:2] * pltpu.roll(dc, nrow - 1, 0) + w[0:1] * pltpu.roll(dc, nrow - 2, 0))[:tr]

        du_ref[0] = back(dcg, wg).astype(du_ref.dtype)
        du_ref[1] = back(dcv, wv).astype(du_ref.dtype)

        def wgrad(dc, u0, u1, u2):
            d = dc[:tr]
            rows = [jnp.sum(d * u2[:tr], axis=0, keepdims=True), jnp.sum(d * u1[:tr], axis=0, keepdims=True),
                    jnp.sum(d * u0[8:8 + tr], axis=0, keepdims=True)]
            return jnp.concatenate(rows, axis=0), jnp.sum(d, axis=0, keepdims=True)

        dwg, dbg = wgrad(dcg, uge, ug1, ug2)
        dwv, dbv = wgrad(dcv, uve, uv1, uv2)

        @pl.when(first)
        def _():
            dwg_ref[...] = jnp.zeros_like(dwg_ref)
            dwv_ref[...] = jnp.zeros_like(dwv_ref)
            dbg_ref[...] = jnp.zeros_like(dbg_ref)
            dbv_ref[...] = jnp.zeros_like(dbv_ref)
        dwg_ref[...] += dwg
        dwv_ref[...] += dwv
        dbg_ref[...] += dbg
        dbv_ref[...] += dbv

    tile = lambda off: pl.BlockSpec((tr, tc), lambda j, i: (i, off + j))
    prev = lambda off: pl.BlockSpec((HALO, tc), lambda j, i: (jnp.maximum(i * hbu - 1, 0), off + j))
    nxt = lambda off: pl.BlockSpec((HALO, tc), lambda j, i: (jnp.minimum((i + 1) * hbu, T // HALO - 1), off + j))
    dnext = pl.BlockSpec((8, tc), lambda j, i: (jnp.minimum((i + 1) * hb, T // 8 - 1), j))
    par = lambda rows, off: pl.BlockSpec((rows, tc), lambda j, i: (0, off + j))
    acc = lambda rows: pl.BlockSpec((rows, tc), lambda j, i: (0, j))
    return pl.pallas_call(
        _after(body, 12, deps), grid=(nc, nr),
        in_specs=[tile(0), prev(0), nxt(0), tile(nc), prev(nc), nxt(nc), tile(0), dnext,
                  par(3, 0), par(3, nc), par(1, 0), par(1, nc)] + [ANY] * len(deps),
        out_specs=[pl.BlockSpec((2, tr, tc), lambda j, i: (0, i, j)), acc(3), acc(3), acc(1), acc(1)],
        out_shape=[SDS((2, T, D_FF), MXU_DTYPE), SDS((3, D_FF), F32), SDS((3, D_FF), F32),
                   SDS((1, D_FF), F32), SDS((1, D_FF), F32)],
        name=name, compiler_params=_cp("parallel", "arbitrary"))(
            up, up, up, up, up, up, dact, dact, cw, cw, cb, cb, *deps)


def _row_tile(rows, cap):
    t = min(cap, rows)
    t -= t % 8
    while rows % t:
        t -= 8
    return t


def _adamw(w, g, m, v, *, tr, name, copy_g=False):
    R, C = w.shape
    assert R % tr == 0, (R, tr)

    def body(w_ref, g_ref, m_ref, v_ref, d_ref, nm_ref, nv_ref, *rest):
        gv = g_ref[...]
        mn = ADAM_B1 * m_ref[...] + (1.0 - ADAM_B1) * gv
        vn = ADAM_B2 * v_ref[...] + (1.0 - ADAM_B2) * (gv * gv)
        m_hat = mn / (1.0 - ADAM_B1 ** ADAM_STEP)
        v_hat = vn / (1.0 - ADAM_B2 ** ADAM_STEP)
        d_ref[...] = -ADAM_LR * (m_hat / (jnp.sqrt(v_hat) + ADAM_EPS) + ADAM_WD * w_ref[...])
        nm_ref[...] = mn
        nv_ref[...] = vn
        if copy_g:
            rest[0][...] = gv

    rows = pl.BlockSpec((tr, C), lambda i: (i, 0))
    n_out = 4 if copy_g else 3
    return pl.pallas_call(
        body, grid=(R // tr,), in_specs=[rows] * 4, out_specs=[rows] * n_out,
        out_shape=[SDS((R, C), F32)] * n_out, name=name, compiler_params=_cp("parallel"))(w, g, m, v)


def _sum_slots(r, *, tr, name):
    S, R, C = r.shape
    assert R % tr == 0, (R, tr)

    def body(r_ref, o_ref):
        acc = r_ref[0]
        for s in range(1, S):
            acc = acc + r_ref[s]
        o_ref[...] = acc

    return pl.pallas_call(
        body, grid=(R // tr,), in_specs=[pl.BlockSpec((S, tr, C), lambda i: (0, i, 0))],
        out_specs=pl.BlockSpec((tr, C), lambda i: (i, 0)), out_shape=SDS((R, C), F32),
        name=name, compiler_params=_cp("parallel"))(r)


def _pair_add(g4, h, pos, *, name):
    A, _, r, C = g4.shape
    cs = C if A == N_CHIPS else C // N_CHIPS
    tr = _row_tile(r, 256)
    if A == N_CHIPS:
        g_map, h_map = (lambda t, i, pos: (t, pos[1], i, 0)), (lambda t, i, pos: (t, i, 0))
    else:
        g_map, h_map = (lambda t, i, pos: (0, pos[1], i, t)), (lambda t, i, pos: (0, i, t))

    def body(pos_ref, g_ref, h_ref, o_ref):
        o_ref[...] = (g_ref[...] + h_ref[...]).astype(o_ref.dtype)

    grid_spec = pltpu.PrefetchScalarGridSpec(
        num_scalar_prefetch=1, grid=(N_CHIPS, r // tr),
        in_specs=[pl.BlockSpec((None, None, tr, cs), g_map), pl.BlockSpec((None, tr, cs), h_map)],
        out_specs=pl.BlockSpec((None, tr, cs), lambda t, i, pos: (t, i, 0)))
    return pl.pallas_call(body, grid_spec=grid_spec, out_shape=SDS((N_CHIPS, r, cs), COMM_DTYPE), name=name,
                          compiler_params=_cp("parallel", "parallel"))(pos, g4, h)


def _chip_sum(p, r2, f_into, pos, layer, *, name):
    _, r, cs = p.shape
    tr = _row_tile(r, 256)

    def body(pos_ref, own_ref, r_ref, *rest):
        o_ref = rest[-1]
        o_ref[...] = ((own_ref[...].astype(F32) + r_ref[0].astype(F32)) + r_ref[1].astype(F32)) + r_ref[2].astype(F32)

    in_specs = [pl.BlockSpec((None, tr, cs), lambda i, pos: (pos[0], i, 0)),
                pl.BlockSpec((3, tr, cs), lambda i, pos: (0, i, 0))]
    operands = [pos, p, r2]
    aliases = {}
    if f_into is not None:
        in_specs.append(ANY)
        operands.append(f_into)
        aliases = {3: 0}
    grid_spec = pltpu.PrefetchScalarGridSpec(
        num_scalar_prefetch=1, grid=(r // tr,), in_specs=in_specs,
        out_specs=pl.BlockSpec((None, None, tr, cs), lambda i, pos: (layer, pos[1], i, 0)))
    return pl.pallas_call(body, grid_spec=grid_spec, out_shape=SDS((DEPTH, 2, r, cs), F32), name=name,
                          input_output_aliases=aliases, compiler_params=_cp("parallel"))(*operands)


def _mesh_pos():
    return lax.axis_index("x"), lax.axis_index("y"), lax.axis_index("c")


HBM = pl.BlockSpec(memory_space=pltpu.HBM)
SEM = pl.BlockSpec(memory_space=pltpu.SEMAPHORE)
DATAFLOW = pltpu.SideEffectType.DATAFLOW_SIDE_EFFECTING
CHIP_FLIPS = (2, 1, 3)


def _chip_peers():
    x, y, c = _mesh_pos()
    return 2 * x + y, [(1 - x, y, c), (x, 1 - y, c), (1 - x, 1 - y, c)], (x, y, 1 - c), c


def _split_start(arrays, n_copies, issue, *, name, deps=()):
    k = len(arrays)
    nd = len(deps)

    def body(*refs):
        issue(refs[:k], refs[k + nd], refs[k + nd + 1])
        refs[2 * k + nd + 2][...] = jnp.zeros((8, 128), F32)

    out = pl.pallas_call(
        body, name=name,
        out_shape=(pltpu.SemaphoreType.DMA((n_copies,)), pltpu.SemaphoreType.DMA((n_copies,)),
                   *[pltpu.HBM(a.shape, a.dtype) for a in arrays], SDS((8, 128), F32)),
        in_specs=[HBM] * k + [ANY] * nd, out_specs=(SEM, SEM, *[HBM] * k, pl.BlockSpec(memory_space=pltpu.VMEM)),
        input_output_aliases={i: 2 + i for i in range(k)},
        compiler_params=pltpu.CompilerParams(has_side_effects=DATAFLOW))(
            *[pltpu.with_memory_space_constraint(a, pltpu.HBM) for a in arrays], *deps)
    return (out[0], out[1]), list(out[2:2 + k]), out[2 + k]


def _split_wait(sems, arrays, after, waits, *, name):
    k = len(arrays)
    afters = tuple(after) if isinstance(after, (tuple, list)) else (after,)

    def body(*refs):
        waits(refs[:k], refs[k], refs[k + 1])

    out = pl.pallas_call(
        body, name=name, out_shape=tuple(pltpu.HBM(a.shape, a.dtype) for a in arrays),
        in_specs=[HBM] * k + [SEM, SEM] + [ANY] * len(afters), out_specs=tuple([HBM] * k),
        input_output_aliases={i: i for i in range(k)},
        compiler_params=pltpu.CompilerParams(has_side_effects=DATAFLOW))(*arrays, sems[0], sems[1], *afters)
    return list(out)


def _wait_both(cp):
    cp.wait_send()
    cp.wait_recv()


def _cast_place(shard, pos, dtype, *, name, layer=None, slots=N_CHIPS, which=0, deps=()):
    R, C = shard.shape[-2:]
    tr = R if R % 8 else _row_tile(R, 256)
    if layer is None:
        in_spec = pl.BlockSpec((tr, C), lambda i, pos: (i, 0))
    else:
        in_spec = pl.BlockSpec((None, tr, C), lambda i, pos: (layer, i, 0))

    def body(pos_ref, x_ref, o_ref):
        o_ref[...] = x_ref[...].astype(o_ref.dtype)

    grid_spec = pltpu.PrefetchScalarGridSpec(
        num_scalar_prefetch=1, grid=(R // tr,), in_specs=[in_spec] + [ANY] * len(deps),
        out_specs=pl.BlockSpec((None, tr, C), lambda i, pos: (pos[which], i, 0)))
    return pl.pallas_call(_after(body, 2, deps), grid_spec=grid_spec, out_shape=SDS((slots, R, C), dtype), name=name,
                          compiler_params=_cp("parallel"))(pos, shard, *deps)


def _device_peers():
    x, y, c = _mesh_pos()
    peers = [(x ^ ((f >> 2) & 1), y ^ ((f >> 1) & 1), c ^ (f & 1)) for f in range(1, N_DEV)]
    return 4 * x + 2 * y + c, peers


class _Gather:
    def __init__(self, lands, name, deps=(), all_devices=False, halves=False):
        n = len(lands)
        self.name, self.halves = name, halves
        npeer = N_DEV - 1 if all_devices else N_CHIPS - 1
        if halves:
            lands = [a.reshape(a.shape[0], 2, a.shape[1] // 2, a.shape[2]) for a in lands]

        def copies(refs, ss, rs):
            if halves:
                me, peers, _, c = _chip_peers()
                own = lambda r: r.at[me, c]
            else:
                me, peers = _device_peers() if all_devices else _chip_peers()[:2]
                own = lambda r: r.at[me]
            return [pltpu.make_async_remote_copy(
                src_ref=own(refs[w]), dst_ref=own(refs[w]), send_sem=ss.at[npeer * w + p],
                recv_sem=rs.at[npeer * w + p], device_id=peers[p], device_id_type=MESH)
                for w in range(n) for p in range(npeer)]

        def issue(refs, ss, rs):
            for cp in copies(refs, ss, rs):
                cp.start()

        def waits(refs, ss, rs):
            for cp in copies(refs, ss, rs):
                _wait_both(cp)

        self._waits = waits
        self.sems, self.arrays, self.token = _split_start(list(lands), npeer * n, issue, name=name + "_start",
                                                          deps=deps)

    def wait(self, after):
        arrays = _split_wait(self.sems, self.arrays, after, self._waits, name=self.name + "_wait")
        if not self.halves:
            return arrays
        n = len(arrays)

        def copies(refs, ss, rs):
            me, _, sibling, c = _chip_peers()
            return [pltpu.make_async_remote_copy(
                src_ref=refs[w].at[me ^ CHIP_FLIPS[p], c], dst_ref=refs[w].at[me ^ CHIP_FLIPS[p], c],
                send_sem=ss.at[3 * w + p], recv_sem=rs.at[3 * w + p], device_id=sibling, device_id_type=MESH)
                for w in range(n) for p in range(3)]

        def issue(refs, ss, rs):
            for cp in copies(refs, ss, rs):
                cp.start()

        def waits(refs, ss, rs):
            for cp in copies(refs, ss, rs):
                _wait_both(cp)

        sems, arrays, _ = _split_start(arrays, 3 * n, issue, name=self.name + "_share_start")
        arrays = _split_wait(sems, arrays, after, waits, name=self.name + "_share_wait")
        return [a.reshape(a.shape[0], 2 * a.shape[2], a.shape[3]) for a in arrays]


def _swap_halves_start(g4s, *, name):
    n = len(g4s)
    lands = [lax.empty((g.shape[0],) + g.shape[2:], g.dtype) for g in g4s]

    def copies(refs, ss, rs):
        _, _, sibling, c = _chip_peers()
        return [pltpu.make_async_remote_copy(
            src_ref=refs[w].at[:, 1 - c], dst_ref=refs[n + w], send_sem=ss.at[w], recv_sem=rs.at[w],
            device_id=sibling, device_id_type=MESH) for w in range(n)]

    def issue(refs, ss, rs):
        for cp in copies(refs, ss, rs):
            cp.start()

    def waits(refs, ss, rs):
        for cp in copies(refs, ss, rs):
            _wait_both(cp)

    sems, arrays, token = _split_start(list(g4s) + lands, n, issue, name=name + "_start")
    return sems, arrays, token, waits


def _scatter_start(parts, *, name, deps=()):
    n = len(parts)
    lands = [lax.empty((3,) + p.shape[1:], p.dtype) for p in parts]

    def copies(refs, ss, rs):
        me, peers, _, _ = _chip_peers()
        return [pltpu.make_async_remote_copy(
            src_ref=refs[w].at[me ^ CHIP_FLIPS[p]], dst_ref=refs[n + w].at[p],
            send_sem=ss.at[3 * w + p], recv_sem=rs.at[3 * w + p], device_id=peers[p], device_id_type=MESH)
            for w in range(n) for p in range(3)]

    def issue(refs, ss, rs):
        for cp in copies(refs, ss, rs):
            cp.start()

    def waits(refs, ss, rs):
        for cp in copies(refs, ss, rs):
            _wait_both(cp)

    sems, arrays, token = _split_start(list(parts) + lands, 3 * n, issue, name=name + "_start", deps=deps)
    return sems, arrays, token, waits


def _pair_share_start(fs, layer, *, name):
    n = len(fs)

    def copies(refs, ss, rs):
        _, _, sibling, c = _chip_peers()
        return [pltpu.make_async_remote_copy(
            src_ref=refs[w].at[layer, c], dst_ref=refs[w].at[layer, c], send_sem=ss.at[w], recv_sem=rs.at[w],
            device_id=sibling, device_id_type=MESH) for w in range(n)]

    def issue(refs, ss, rs):
        for cp in copies(refs, ss, rs):
            cp.start()

    def waits(refs, ss, rs):
        for cp in copies(refs, ss, rs):
            _wait_both(cp)

    sems, arrays, token = _split_start(list(fs), n, issue, name=name + "_start")
    return sems, arrays, token, waits


BIG = ('w_in', 'w_proj_a', 'w_proj_b', 'w_proj_c', 'w_out', 'w_up', 'w_down')
BIG_SHARD_AXIS = {'w_in': 2, 'w_proj_a': 2, 'w_proj_b': 2, 'w_proj_c': 2, 'w_out': 1, 'w_up': 2, 'w_down': 1}
SMALL = ('norm1', 'q_norm', 'k_norm', 'sinks', 'w_pool', 'pool_scale', 'sgu_v_norm', 'w_s', 'b_s', 'norm2',
         'conv_b', 'conv_w')
WEIGHTS = ('norm1', 'w_in', 'q_norm', 'k_norm', 'sinks', 'w_pool', 'pool_scale', 'sgu_v_norm', 'w_s', 'b_s',
           'w_proj_a', 'w_proj_b', 'w_proj_c', 'w_out', 'norm2', 'w_up', 'conv_w', 'conv_b', 'w_down')


def _rope_tables(positions):
    inv_freq = ROPE_THETA ** (-jnp.arange(0, HEAD_DIM, 2, dtype=F32) / HEAD_DIM)
    ang = positions.astype(F32)[:, None] * inv_freq
    cos, sin = jnp.cos(ang), jnp.sin(ang)
    c = jnp.concatenate([cos, cos], axis=1)
    s = jnp.concatenate([-sin, sin], axis=1)
    return jnp.concatenate([c, c], axis=1), jnp.concatenate([s, s], axis=1)


def _block_diag4(w):
    out = jnp.zeros((POOL_WIDTH, POOL_WIDTH), w.dtype)
    for g in range(4):
        out = lax.dynamic_update_slice(out, w[g], (g * HEAD_DIM, g * HEAD_DIM))
    return out


def _local_step(x, target, cos, sin, sp, sched):
    T = x.shape[0]
    tm1 = min(1024, T)
    tm = min(512, T)
    tr = min(1024, T)
    trc = min(512, T)
    tkt = min(2048, T)
    seg = _seg_matrix(256, HEAD_DIM)
    saved = []
    xl = x
    for l in range(DEPTH):
        p = f"l{l}_"
        c = dict(
            g1=sp['norm1'][l][None], g2=sp['norm2'][l][None],
            wbd=_block_diag4(sp['w_pool'][l]).astype(MXU_DTYPE), scale=sp['pool_scale'][l][None],
            gq=jnp.tile(sp['q_norm'][l], 4)[None], gk=jnp.tile(sp['k_norm'][l], 2)[None],
            sinks=jnp.broadcast_to(sp['sinks'][l][:, None], (N_Q_HEADS, 128)),
            wtril=jnp.tril(sp['w_s'][l]).astype(MXU_DTYPE),
            bexp=jnp.repeat(sp['b_s'][l].T, HEAD_DIM, axis=1), vn=jnp.tile(sp['sgu_v_norm'][l], 4)[None],
            cb=sp['conv_b'][l][None])
        c['w_in'] = sched.weight('w_in', l, xl)
        z, h1 = _norm_mm(xl, c['g1'], c['w_in'], tm=tm1, tn=1152, out_dtype=ACT_DTYPE, name=p + "in_proj",
                         deps=sched.start_tokens() if l == 0 else ())
        pa = _pool_fwd(z, c['wbd'], c['scale'], tr=tr, name=p + "pool")
        q, k, v = _qkv_prep(z, cos, sin, c['gq'], c['gk'], seg, tr=tr, name=p + "qkv_prep",
                            deps=sched.start_next_layer(z) if l == 0 else ())
        at = _attn_fwd(q, k, v, c['sinks'], name=p + "attn")
        sg = _sgu_fwd(z, c['wtril'], c['bexp'], c['vn'], seg, tr=tr, name=p + "sgu")
        for n in ('w_proj_a', 'w_proj_b', 'w_proj_c', 'w_out'):
            c[n] = sched.weight(n, l, (pa, at, sg))
        merged, y3, x1 = _merge_fwd(pa, at, sg, c['w_proj_a'], c['w_proj_b'], c['w_proj_c'], z, xl, c['w_out'],
                                    tm=tm1, tn=512, name=p + "merge_out_proj")
        for n in ('w_up', 'conv_w', 'w_down'):
            c[n] = sched.weight(n, l, x1)
        up, h2 = _norm_mm(x1, c['g2'], c['w_up'], tm=tm1, tn=1408, out_dtype=F32, name=p + "up_proj")
        act = _conv_act_fwd(up, c['conv_w'], c['cb'], tr=trc, tc=1408, name=p + "conv_act")
        saved.append(dict(c, x=xl, h1=h1, z=z, pa=pa, q=q, k=k, v=v, at=at, sg=sg, merged=merged, y3=y3,
                          x1=x1, h2=h2, up=up, act=act))
        if l < DEPTH - 1:
            xl = _mm(act, c['w_down'], mode='nn', add=x1, tm=tm, tn=D_MODEL, tk=D_FF, name=p + "down_proj")
        else:
            loss_row, dx, dxb = _down_proj_loss(act, c['w_down'], x1, target, tm=tm, name=p + "down_proj_loss")

    gs = {n: [None] * DEPTH for n in SMALL}
    for l in reversed(range(DEPTH)):
        p = f"l{l}_b_"
        s = saved[l]
        gb = {}
        dact = _mm(dxb, s['w_down'], mode='nt', tm=tm1, tn=1408, tk=D_MODEL, name=p + "down_dx")
        gb['w_down'] = _mm(s['act'], dxb, mode='tn', tm=1408, tn=D_MODEL, tk=tkt, name=p + "down_dw")
        toks = sched.slot(l, 'down', gb['w_down'])
        dup, dwg, dwv, dbg, dbv = _conv_act_bwd(s['up'], s['conv_w'], s['cb'], dact, tr=min(1024, T), tc=256,
                                                name=p + "conv_act", deps=toks)
        gs['conv_w'][l] = jnp.concatenate([dwg, dwv], axis=1)
        gs['conv_b'][l] = jnp.concatenate([dbg, dbv], axis=1)[0]
        toks = sched.slot(l, 'conv', dup)
        for half in range(2):
            gb['w_up'] = _mm(s['h2'], dup, mode='tn', b_lead=half, tm=D_MODEL, tn=1408, tk=tkt,
                             out_into=gb.get('w_up'), out_joff=2 * half, out_n=2 * D_FF, name=p + f"up_dw{half}",
                             deps=toks if half == 0 else ())
        toks = sched.slot(l, 'ffn', gb['w_up'], gb)
        dx1, dx1b, dg2 = _mm_nt_sharded_rms(dup, s['w_up'], s['x1'], s['g2'], dx, tm=tm,
                                            name=p + "up_dx_rms2", deps=toks)
        gs['norm2'][l] = dg2[0]
        gb['w_out'] = _mm(s['merged'], dx1b, mode='tn', tm=D_MODEL, tn=D_MODEL, tk=tkt, name=p + "out_dw")
        (dz, dpa, dat, dsg, gb['w_proj_a'], gb['w_proj_b'], gb['w_proj_c']) = _out_dx_merge_bwd(
            dx1b, s['w_out'], s['y3'], s['z'], [s['w_proj_a'], s['w_proj_b'], s['w_proj_c']],
            [s['pa'], s['at'], s['sg']], tm=tm1, tn=512, name=p + "out_dx_merge")
        toks = sched.slot(l, 'mid', dz)
        dq, dkc, dkp, dvc, dvp, dsk = _attn_bwd(s['q'], s['k'], s['v'], s['sinks'], dat, name=p + "attn", deps=toks)
        gs['sinks'][l] = dsk[:, 0]
        toks = sched.slot(l, 'attn', dq)
        dz, dgq, dgk, dwbd, dsc = _mixer_ab_bwd(s['z'], cos, sin, s['gq'], s['gk'], seg, dq, dkc, dkp, dvc, dvp,
                                                dpa, s['wbd'], s['scale'], dz, tr=tr, name=p + "qkv_pool", deps=toks)
        gs['q_norm'][l] = dgq[0, :HEAD_DIM]
        gs['k_norm'][l] = dgk[0, :HEAD_DIM]
        gs['w_pool'][l] = jnp.stack([dwbd[g * HEAD_DIM:(g + 1) * HEAD_DIM, g * HEAD_DIM:(g + 1) * HEAD_DIM]
                                     for g in range(4)])
        gs['pool_scale'][l] = dsc[0]
        dz, dws, dbrows, dvn = _sgu_bwd(s['z'], s['wtril'], s['bexp'], s['vn'], seg, dsg, dz, tr=tr, name=p + "sgu")
        gs['w_s'][l] = dws
        gs['b_s'][l] = dbrows[:, ::HEAD_DIM].T
        gs['sgu_v_norm'][l] = dvn[0, :HEAD_DIM]
        gb['w_in'] = _mm(s['h1'], dz, mode='tn', tm=D_MODEL, tn=1152, tk=tkt, name=p + "in_dw")
        toks = sched.slot(l, 'mix', gb['w_in'], gb)
        dx, dxb, dg1 = _mm_nt_sharded_rms(dz, s['w_in'], s['x'], s['g1'], dx1, tm=tm,
                                          name=p + "in_dx_rms1", deps=toks)
        gs['norm1'][l] = dg1[0]
    gs = {n: jnp.stack(v) for n, v in gs.items()}
    return loss_row, dx, gs


GROUP_F = ('w_down', 'w_up')
GROUP_M = ('w_out', 'w_proj_a', 'w_proj_b', 'w_proj_c', 'w_in')
ROW_SHARDED = ('w_out', 'w_down')

REDUCE_PLAN = {
    (1, 'ffn'): (('S1', 'F', 1),),
    (1, 'mid'): (('W1', 'F', 1),),
    (1, 'mix'): (('S1', 'M', 1),),
    (0, 'down'): (('W1', 'M', 1),),
    (0, 'conv'): (('W2', 'F', 1),),
    (0, 'ffn'): (('S1', 'F', 0), ('W3', 'F', 1)),
    (0, 'mid'): (('W1', 'F', 0),),
    (0, 'attn'): (('W2', 'M', 1),),
    (0, 'mix'): (('S1', 'M', 0), ('W3', 'M', 1)),
}
REDUCE_TAIL_A = (('W1', 'M', 0), ('W2', 'F', 0))
REDUCE_TAIL_B = (('W3', 'F', 0),)
REDUCE_TAIL_C = (('W2', 'M', 0), ('W3', 'M', 0))


class _Comm:
    def __init__(self, w, pos):
        self.pos, self.w = pos, w
        self.gathers, self.group_of, self.weights = {}, {}, {}
        self.tokens = []
        self._start_group('a', [('w_in', 0)])
        self._start_group('b', [(n, 0) for n in ('w_proj_a', 'w_proj_b', 'w_proj_c', 'w_out')])
        self._start_group('c', [(n, 0) for n in ('w_up', 'conv_w', 'w_down')])
        self.red = {}
        self.final = {}

    def _start_group(self, g, ks, deps=()):
        lands = [_cast_place(self.w[n], self.pos, F32 if n == 'conv_w' else MXU_DTYPE, layer=l,
                             name=f"gw_place_{n}{l}", deps=deps) for n, l in ks]
        self.gathers[g] = (_Gather(lands, "gw_" + g, deps=self.tokens[-1:], halves=(g == 'a')), ks)
        self.tokens.append(self.gathers[g][0].token)
        self.group_of.update({k: g for k in ks})

    def start_tokens(self):
        return self.tokens[-1:]

    def start_next_layer(self, dep):
        self._start_group('d', [(n, 1) for n in BIG] + [('conv_w', 1)], deps=[dep])
        return self.tokens[-1:]

    def weight(self, name, layer, after):
        if (name, layer) not in self.weights:
            gather, ks = self.gathers[self.group_of[(name, layer)]]
            for (n, l), full in zip(ks, gather.wait(after)):
                if n == 'conv_w' or n.startswith('w_proj'):
                    full = full.transpose(1, 0, 2).reshape(full.shape[1], -1)
                elif n in ROW_SHARDED:
                    full = full.reshape(-1, full.shape[2])
                self.weights[(n, l)] = full
        return self.weights[(name, layer)]

    def slot(self, layer, slot, after, grads=None):
        tokens = []
        for step, grp, lyr in REDUCE_PLAN.get((layer, slot), ()):
            tok = self._step(step, grp, lyr, after, grads)
            if tok is not None:
                tokens.append(tok)
        return tokens

    def tail(self, steps, after, deps=()):
        toks = (self._step(step, grp, lyr, after, None, deps) for step, grp, lyr in steps)
        return [t for t in toks if t is not None]

    def shards(self):
        return {n: f.reshape(DEPTH, 2 * f.shape[2], f.shape[3]) for n, f in self.final.items()}

    def _step(self, step, grp, layer, after, grads, deps=()):
        names = GROUP_F if grp == 'F' else GROUP_M
        tag = f"{grp.lower()}{layer}"
        st = self.red.setdefault((grp, layer), {})
        n = len(names)
        if step == 'S1':
            g4s = []
            for nm in names:
                g = grads[nm]
                R, C = g.shape
                g4s.append(g.reshape(N_CHIPS, 2, R // (2 * N_CHIPS), C) if nm in ROW_SHARDED
                           else g.reshape(1, 2, R // 2, C))
            st['s1'] = _swap_halves_start(g4s, name="rs1_" + tag)
            return st['s1'][2]
        if step == 'W1':
            sems, arrays, _, waits = st.pop('s1')
            arrays = _split_wait(sems, arrays, after, waits, name=f"rs1_{tag}_wait")
            parts = [_pair_add(arrays[i], arrays[n + i], self.pos, name=f"pair_add_{tag}_{names[i]}")
                     for i in range(n)]
            st['s2'] = _scatter_start(parts, name="rs2_" + tag, deps=deps)
            return st['s2'][2]
        if step == 'W2':
            sems, arrays, _, waits = st.pop('s2')
            arrays = _split_wait(sems, arrays, after, waits, name=f"rs2_{tag}_wait")
            fs = [_chip_sum(arrays[i], arrays[n + i], self.final.get(names[i]), self.pos, layer,
                            name=f"chip_sum_{tag}_{names[i]}") for i in range(n)]
            st['s3'] = _pair_share_start(fs, layer, name="rs3_" + tag)
            return st['s3'][2]
        sems, arrays, _, waits = st.pop('s3')
        self.final.update(zip(names, _split_wait(sems, arrays, after, waits, name=f"rs3_{tag}_wait")))
        return None


def _pack(arrays):
    rows = []
    for a in arrays:
        nel = int(np.prod(a.shape))
        if nel % 1024 == 0:
            rows.append(a.astype(F32).reshape(nel // 128, 128))
        else:
            f = a.reshape(-1).astype(F32)
            rows.append(jnp.pad(f, (0, (-nel) % 1024)).reshape(-1, 128))
    return jnp.concatenate(rows, axis=0)


def _unpack(pack, shapes):
    out, row = [], 0
    for shp in shapes:
        nel = int(np.prod(shp))
        nrow = 8 * -(-nel // 1024)
        part = pack[row:row + nrow]
        out.append(part.reshape(shp) if nel % 1024 == 0 else part.reshape(-1)[:nel].reshape(shp))
        row += nrow
    return out


def kernel(x, positions, norm1, w_in, q_norm, k_norm, sinks, w_pool, pool_scale, sgu_v_norm, w_s, b_s, w_proj_a, w_proj_b, w_proj_c, w_out, norm2, w_up, conv_w, conv_b, w_down, loss_target, m_norm1, m_w_in, m_q_norm, m_k_norm, m_sinks, m_w_pool, m_pool_scale, m_sgu_v_norm, m_w_s, m_b_s, m_w_proj_a, m_w_proj_b, m_w_proj_c, m_w_out, m_norm2, m_w_up, m_conv_w, m_conv_b, m_w_down, v_norm1, v_w_in, v_q_norm, v_k_norm, v_sinks, v_w_pool, v_pool_scale, v_sgu_v_norm, v_w_s, v_b_s, v_w_proj_a, v_w_proj_b, v_w_proj_c, v_w_out, v_norm2, v_w_up, v_conv_w, v_conv_b, v_w_down):
    w = dict(norm1=norm1, w_in=w_in, q_norm=q_norm, k_norm=k_norm, sinks=sinks, w_pool=w_pool, pool_scale=pool_scale,
             sgu_v_norm=sgu_v_norm, w_s=w_s, b_s=b_s, w_proj_a=w_proj_a, w_proj_b=w_proj_b, w_proj_c=w_proj_c,
             w_out=w_out, norm2=norm2, w_up=w_up, conv_w=conv_w, conv_b=conv_b, w_down=w_down)
    m = dict(norm1=m_norm1, w_in=m_w_in, q_norm=m_q_norm, k_norm=m_k_norm, sinks=m_sinks, w_pool=m_w_pool,
             pool_scale=m_pool_scale, sgu_v_norm=m_sgu_v_norm, w_s=m_w_s, b_s=m_b_s, w_proj_a=m_w_proj_a,
             w_proj_b=m_w_proj_b, w_proj_c=m_w_proj_c, w_out=m_w_out, norm2=m_norm2, w_up=m_w_up, conv_w=m_conv_w,
             conv_b=m_conv_b, w_down=m_w_down)
    v = dict(norm1=v_norm1, w_in=v_w_in, q_norm=v_q_norm, k_norm=v_k_norm, sinks=v_sinks, w_pool=v_w_pool,
             pool_scale=v_pool_scale, sgu_v_norm=v_sgu_v_norm, w_s=v_w_s, b_s=v_b_s, w_proj_a=v_w_proj_a,
             w_proj_b=v_w_proj_b, w_proj_c=v_w_proj_c, w_out=v_w_out, norm2=v_norm2, w_up=v_w_up, conv_w=v_conv_w,
             conv_b=v_conv_b, w_down=v_w_down)
    chip = 2 * lax.axis_index("x") + lax.axis_index("y")
    core = lax.axis_index("c")

    pos = jnp.stack([chip, core, 2 * chip + core]).astype(jnp.int32)
    comm = _Comm(w, pos)

    cos, sin = _rope_tables(positions[0])
    sp = {n: w[n] for n in SMALL if n != 'conv_w'}
    loss_row, dx, gs = _local_step(x[0], loss_target[0], cos, sin, sp, comm)

    delta, new_m, new_v, grad_out = {}, {}, {}, {}

    def adamw_big(names, grads):
        for n in names:
            shp = w[n].shape
            two_d = lambda a: a.reshape(shp[0] * shp[1], shp[2])
            d, nm, nv, g = _adamw(two_d(w[n]), two_d(grads[n]), two_d(m[n]), two_d(v[n]),
                                  tr=_row_tile(shp[0] * shp[1], 256), name=f"adamw_{n}", copy_g=True)
            delta[n], new_m[n], new_v[n], grad_out[n] = d.reshape(shp), nm.reshape(shp), nv.reshape(shp), g.reshape(shp)

    small_shapes = [gs[n].shape for n in SMALL] + [(1,)]
    small_pack = _pack([gs[n] for n in SMALL] + [loss_row[0, :1]])
    small = _Gather([_cast_place(small_pack, pos, F32, slots=N_DEV, which=2, name="small_place")], "small_gather",
                    all_devices=True)
    toks = comm.tail(REDUCE_TAIL_A[:1], (dx, small.token))
    comm.tail(REDUCE_TAIL_A[1:], (dx, *toks))
    comm.tail(REDUCE_TAIL_B, dx)
    adamw_big(GROUP_F, comm.shards())
    red = _sum_slots(small.wait(new_v[GROUP_F[-1]])[0], tr=small_pack.shape[0], name="small_sum")
    *small_grads, loss = _unpack(red, small_shapes)
    g_small = dict(zip(SMALL, small_grads))
    comm.tail(REDUCE_TAIL_C, red)
    grads = comm.shards()
    grads.update(g_small)
    shard_cols = conv_w.shape[2]
    grads['conv_w'] = lax.dynamic_slice_in_dim(g_small['conv_w'], chip * shard_cols, shard_cols, axis=2)

    adamw_big(GROUP_M, grads)
    shapes = [w[n].shape for n in SMALL]
    packs = [_pack([src[n] for n in SMALL]) for src in (w, grads, m, v)]
    d, nm, nv = _adamw(*packs, tr=packs[0].shape[0], name="adamw_small")
    for dst, src in ((delta, d), (new_m, nm), (new_v, nv)):
        dst.update(zip(SMALL, _unpack(src, shapes)))

    grads.update(grad_out)
    return (loss[0], dx[None], *[grads[n] for n in WEIGHTS], *[delta[n] for n in WEIGHTS],
            *[new_m[n] for n in WEIGHTS], *[new_v[n] for n in WEIGHTS])
```

```python
import functools
import math

import numpy as np
import jax
import jax.numpy as jnp
from jax import lax
from jax.experimental import pallas as pl
from jax.experimental.pallas import tpu as pltpu

F32 = jnp.float32
MXU_DTYPE = jnp.bfloat16
COMM_DTYPE = jnp.bfloat16
ACT_DTYPE = jnp.bfloat16
HALO = 16

D_MODEL = 1024
DEPTH = 2
HEAD_DIM = 64
POOL_WINDOWS = (2, 4, 8, 16)
POOL_WIDTH = 256
N_Q_HEADS = 8
ATTN_BLOCK = 128
ATTN_WIDTH = 512
KV_WIDTH = 128
CHUNK = 128
SGU_WIDTH = 256
IN_COLS = 4608
GATE_COL0 = 1536
D_FF = 2816
ROPE_THETA = 10000.0
EPS = 1e-6
ADAM_LR, ADAM_B1, ADAM_B2, ADAM_EPS, ADAM_WD, ADAM_STEP = 0.001, 0.9, 0.999, 1e-08, 0.01, 10

N_CHIPS = 4
N_DEV = 8
VMEM_LIMIT_BYTES = 56 * 1024 * 1024
NEG_BIG = -1e30
MESH = pl.DeviceIdType.MESH
ANY = pl.BlockSpec(memory_space=pl.ANY)

SDS = jax.ShapeDtypeStruct


def _cp(*sem):
    return pltpu.CompilerParams(dimension_semantics=sem, vmem_limit_bytes=VMEM_LIMIT_BYTES)


def _dot(a, b, dims=((1,), (0,))):
    return lax.dot_general(a.astype(MXU_DTYPE), b.astype(MXU_DTYPE), (dims, ((), ())),
                           preferred_element_type=F32)


NT = ((1,), (1,))
TN = ((0,), (0,))


def _split_dot(x, m):
    hi = x.astype(MXU_DTYPE)
    lo = (x - hi.astype(F32)).astype(MXU_DTYPE)
    return _dot(hi, m) + _dot(lo, m)


def _seg_matrix(width, seg):
    idx = np.arange(width) // seg
    return jnp.asarray((idx[:, None] == idx[None, :]).astype(np.float32), dtype=MXU_DTYPE)


def _lane(shape):
    return lax.broadcasted_iota(jnp.int32, shape, len(shape) - 1)


def _row(shape):
    return lax.broadcasted_iota(jnp.int32, shape, 0)


def _full(shape):
    nd = len(shape)
    return pl.BlockSpec(shape, lambda *_: (0,) * nd)


def _gelu(x):
    k = math.sqrt(2.0 / math.pi)
    th = jnp.tanh(k * (x + 0.044715 * (x * x * x)))
    return 0.5 * x * (1.0 + th)


def _gelu_and_grad(x):
    k = math.sqrt(2.0 / math.pi)
    x2 = x * x
    th = jnp.tanh(k * (x + 0.044715 * (x2 * x)))
    g = 0.5 * x * (1.0 + th)
    dg = 0.5 * (1.0 + th) + 0.5 * x * (1.0 - th * th) * (k * (1.0 + 3.0 * 0.044715 * x2))
    return g, dg


def _sigmoid(x):
    return 0.5 * jnp.tanh(0.5 * x) + 0.5


def _swap_halves(x):
    w = x.shape[-1]
    first = (_lane(x.shape) % HEAD_DIM) < (HEAD_DIM // 2)
    return jnp.where(first, pltpu.roll(x, w - HEAD_DIM // 2, 1), pltpu.roll(x, HEAD_DIM // 2, 1))


def _tile_lanes(x, reps):
    return x if reps == 1 else jnp.concatenate([x] * reps, axis=1)


def _fold_lanes(x, period):
    w = x.shape[-1]
    while w > period:
        w //= 2
        x = x + pltpu.roll(x, w, 1)
    return x


def _mm(a, b, *, mode, tm, tn, tk, out_dtype=F32, add=None, name,
        a_lead=None, b_lead=None, b_sharded=False, out_into=None,
        b_koff=0, out_joff=0, out_n=None, deps=()):
    ash = a.shape[1:] if a_lead is not None else a.shape
    bsh = b.shape[1:] if b_lead is not None else b.shape
    if b_sharded:
        bsh = (b.shape[1], N_CHIPS * b.shape[2])
    if mode == 'nn':
        (M, K), (K2, N) = ash, bsh
    elif mode == 'nt':
        (M, K), (N, K2) = ash, bsh
    else:
        (K, M), (K2, N) = ash, bsh
    assert K == K2 or (mode == 'nt' and K2 > K), (ash, bsh, mode)
    assert M % tm == 0 and N % tn == 0 and K % tk == 0, (M, N, K, tm, tn, tk)
    nk = K // tk
    dims = {'nn': ((1,), (0,)), 'nt': NT, 'tn': TN}[mode]

    def lead(spec_shape, imap, lead_idx):
        if lead_idx is None:
            return pl.BlockSpec(spec_shape, imap)
        return pl.BlockSpec((None,) + spec_shape, lambda i, j, k: (lead_idx,) + imap(i, j, k))

    if mode == 'tn':
        a_spec = lead((tk, tm), lambda i, j, k: (k, i), a_lead)
    else:
        a_spec = lead((tm, tk), lambda i, j, k: (i, k), a_lead)
    if b_sharded:
        per = b.shape[2] // (tk if mode == 'nt' else tn)
        assert per * (tk if mode == 'nt' else tn) == b.shape[2] and mode != 'tn'
        if mode == 'nt':
            b_spec = pl.BlockSpec((None, tn, tk), lambda i, j, k: ((k + b_koff) // per, j, (k + b_koff) % per))
        else:
            b_spec = pl.BlockSpec((None, tk, tn), lambda i, j, k: (j // per, k, j % per))
    elif mode == 'nt':
        b_spec = lead((tn, tk), lambda i, j, k: (j, k + b_koff), b_lead)
    else:
        b_spec = lead((tk, tn), lambda i, j, k: (k, j), b_lead)
    o_spec = pl.BlockSpec((tm, tn), lambda i, j, k: (i, j + out_joff))
    n_out = N if out_n is None else out_n
    in_specs = [a_spec, b_spec]
    operands = [a, b]
    if add is not None:
        in_specs.append(pl.BlockSpec((tm, tn), lambda i, j, k: (i, j)))
        operands.append(add)
    aliases = {}
    if out_into is not None:
        in_specs.append(ANY)
        operands.append(out_into)
        aliases = {len(operands) - 1: 0}
    in_specs += [ANY] * len(deps)
    operands += list(deps)
    has_add = add is not None
    acc_in_out = nk > 1 and out_dtype == F32

    def body(*refs):
        a_ref, b_ref = refs[0], refs[1]
        pos = 2
        add_ref = None
        if has_add:
            add_ref = refs[pos]
            pos += 1
        if out_into is not None:
            pos += 1
        pos += len(deps)
        o_ref = refs[pos]
        acc_ref = refs[pos + 1] if (nk > 1 and not acc_in_out) else None
        p = _dot(a_ref[...], b_ref[...], dims)
        if nk == 1:
            if has_add:
                p = p + add_ref[...]
            o_ref[...] = p.astype(o_ref.dtype)
            return
        k = pl.program_id(2)
        tgt = o_ref if acc_in_out else acc_ref

        @pl.when(k == 0)
        def _():
            tgt[...] = p + add_ref[...] if has_add else p

        @pl.when(k > 0)
        def _():
            tgt[...] += p

        if not acc_in_out:
            @pl.when(k == nk - 1)
            def _():
                o_ref[...] = acc_ref[...].astype(o_ref.dtype)

    out_shape = SDS((M, n_out), out_dtype)
    scratch = [pltpu.VMEM((tm, tn), F32)] if (nk > 1 and not acc_in_out) else []
    return pl.pallas_call(
        body, grid=(M // tm, N // tn, nk), in_specs=in_specs, out_specs=o_spec, out_shape=out_shape,
        scratch_shapes=scratch, input_output_aliases=aliases, name=name,
        compiler_params=_cp("parallel", "parallel", "arbitrary"))(*operands)


def _rms_bwd_rows(xv, g, dh, dres):
    r = lax.rsqrt(jnp.mean(xv * xv, axis=-1, keepdims=True) + EPS)
    xh = xv * r
    gy = dh * g
    dx = r * (gy - xh * jnp.mean(xh * gy, axis=-1, keepdims=True)) + dres
    return dx, jnp.sum(dh * xh, axis=0, keepdims=True)


def _mm_nt_sharded_rms(a, b, x, g, dres, *, tm, name, deps=()):
    a3 = a if a.ndim == 3 else a[None]
    A, M, ka = a3.shape
    S, N, ns = b.shape
    per = S // A
    assert ka == per * ns and M % tm == 0 and N == x.shape[1], (a3.shape, b.shape, x.shape)

    def body(a_ref, b_ref, x_ref, g_ref, dres_ref, dx_ref, dxb_ref, dg_ref):
        acc = None
        for s in range(S):
            lo = (s % per) * ns
            p = _dot(a_ref[s // per, :, lo:lo + ns], b_ref[s], NT)
            acc = p if acc is None else acc + p
        dx, dg = _rms_bwd_rows(x_ref[...], g_ref[...], acc, dres_ref[...])
        dx_ref[...] = dx
        dxb_ref[...] = dx.astype(dxb_ref.dtype)

        @pl.when(pl.program_id(0) == 0)
        def _():
            dg_ref[...] = jnp.zeros_like(dg_ref)
        dg_ref[...] += dg

    rows = pl.BlockSpec((tm, N), lambda i: (i, 0))
    return pl.pallas_call(
        _after(body, 5, deps), grid=(M // tm,),
        in_specs=[pl.BlockSpec((A, tm, ka), lambda i: (0, i, 0)),
                  pl.BlockSpec((S, N, ns), lambda i: (0, 0, 0), pipeline_mode=pl.Buffered(1)),
                  rows, _full((1, N)), rows] + [ANY] * len(deps),
        out_specs=[rows, rows, _full((1, N))],
        out_shape=[SDS((M, N), F32), SDS((M, N), MXU_DTYPE), SDS((1, N), F32)], name=name,
        compiler_params=_cp("arbitrary"))(a3, b, x, g, dres, *deps)


def _norm_mm(x, g, b, *, tm, tn, name, out_dtype, deps=()):
    M, K = x.shape
    S, K2, ns = b.shape
    per = ns // tn
    assert K == K2 and per * tn == ns and M % tm == 0, (x.shape, b.shape)

    def body(x_ref, g_ref, b_ref, o_ref, h_ref):
        @pl.when(pl.program_id(1) == 0)
        def _():
            xv = x_ref[...]
            r = lax.rsqrt(jnp.mean(xv * xv, axis=-1, keepdims=True) + EPS)
            h_ref[...] = (xv * r * g_ref[...]).astype(h_ref.dtype)
        o_ref[...] = _dot(h_ref[...], b_ref[...]).astype(o_ref.dtype)

    return pl.pallas_call(
        _after(body, 3, deps), grid=(M // tm, S * per),
        in_specs=[pl.BlockSpec((tm, K), lambda i, j: (i, 0)), _full((1, K)),
                  pl.BlockSpec((None, K, tn), lambda i, j: (j // per, 0, j % per))] + [ANY] * len(deps),
        out_specs=[pl.BlockSpec((tm, tn), lambda i, j: (i, j)), pl.BlockSpec((tm, K), lambda i, j: (i, 0))],
        out_shape=[SDS((M, S * ns), out_dtype), SDS((M, K), MXU_DTYPE)], name=name,
        compiler_params=_cp("parallel", "arbitrary"))(x, g, b, *deps)


def _after(body, n_in, deps):
    nd = len(deps)
    if nd == 0:
        return body
    return lambda *refs: body(*refs[:n_in], *refs[n_in + nd:])


def _down_proj_loss(act, w, x1, target, *, tm, name):
    T, K = act.shape
    D = w.shape[1]

    def body(a_ref, w_ref, x_ref, t_ref, loss_ref, dy_ref, dyb_ref):
        i = pl.program_id(0)
        d = (x_ref[...] + _dot(a_ref[...], w_ref[...])) - t_ref[...]
        dy = d * (1.0 / D)
        dy_ref[...] = dy
        dyb_ref[...] = dy.astype(dyb_ref.dtype)
        part = jnp.sum(jnp.sum(d * d, axis=1, keepdims=True), axis=0, keepdims=True) * (0.5 / D)

        @pl.when(i == 0)
        def _():
            loss_ref[...] = jnp.zeros_like(loss_ref)
        loss_ref[...] += jnp.broadcast_to(part, loss_ref.shape)

    rows = pl.BlockSpec((tm, D), lambda i: (i, 0))
    return pl.pallas_call(
        body, grid=(T // tm,), in_specs=[pl.BlockSpec((tm, K), lambda i: (i, 0)), _full((K, D)), rows, rows],
        out_specs=[_full((1, 128)), rows, rows],
        out_shape=[SDS((1, 128), F32), SDS((T, D), F32), SDS((T, D), MXU_DTYPE)],
        name=name, compiler_params=_cp("arbitrary"))(act, w, x1, target)


def _pool_lane_consts(shape):
    lane = _lane(shape)
    grp = lane // (POOL_WIDTH // 4)
    win = jnp.where(grp == 0, 2, jnp.where(grp == 1, 4, jnp.where(grp == 2, 8, 16)))
    return grp, win


def _pool_select(grp, s2, s4, s8, s16):
    return jnp.where(grp == 0, s2, jnp.where(grp == 1, s4, jnp.where(grp == 2, s8, s16)))


def _pool_diff(xe, row0, tr):
    s2 = xe + pltpu.roll(xe, 1, 0)
    s4 = s2 + pltpu.roll(s2, 2, 0)
    s8 = s4 + pltpu.roll(s4, 4, 0)
    s16 = s8 + pltpu.roll(s8, 8, 0)
    shape = (tr, POOL_WIDTH)
    grp, win = _pool_lane_consts(shape)
    sums = _pool_select(grp, s2[16:], s4[16:], s8[16:], s16[16:])
    t = row0 + _row(shape)
    cnt = jnp.minimum(t + 1, win).astype(F32)
    return sums / cnt - xe[16:]


def _pool_fwd(z, wbd, scale, *, tr, name):
    T = z.shape[0]
    hb = tr // 16

    def body(x_ref, xp_ref, w_ref, s_ref, o_ref):
        i = pl.program_id(0)
        halo = jnp.where(i == 0, 0.0, xp_ref[...].astype(F32))
        diff = _pool_diff(jnp.concatenate([halo, x_ref[...].astype(F32)], axis=0), i * tr, tr)
        o_ref[...] = (_dot(diff, w_ref[...]) * s_ref[...]).astype(o_ref.dtype)

    return pl.pallas_call(
        body, grid=(T // tr,),
        in_specs=[pl.BlockSpec((tr, POOL_WIDTH), lambda i: (i, 0)),
                  pl.BlockSpec((16, POOL_WIDTH), lambda i: (jnp.maximum(i * hb - 1, 0), 0)),
                  _full((POOL_WIDTH, POOL_WIDTH)), _full((1, POOL_WIDTH))],
        out_specs=pl.BlockSpec((tr, POOL_WIDTH), lambda i: (i, 0)),
        out_shape=SDS((T, POOL_WIDTH), MXU_DTYPE), name=name, compiler_params=_cp("parallel"))(z, z, wbd, scale)


def _pool_bwd_tile(i, n, tr, x, xprev, dpa, dpa_next, wbd, scale):
    halo = jnp.where(i == 0, 0.0, xprev)
    diff = _pool_diff(jnp.concatenate([halo, x], axis=0), i * tr, tr)
    mixed = _dot(diff, wbd)
    dscale = jnp.sum(dpa * mixed, axis=0, keepdims=True)
    dnext = jnp.where(i == n - 1, 0.0, dpa_next)
    dmix_e = jnp.concatenate([dpa, dnext], axis=0) * scale
    ddiff_e = _dot(dmix_e, wbd, NT)
    dwbd = _dot(diff, dmix_e[:tr], TN)
    shape = (tr + 16, POOL_WIDTH)
    grp, win = _pool_lane_consts(shape)
    t = i * tr + _row(shape)
    e = ddiff_e / jnp.minimum(t + 1, win).astype(F32)
    nrow = tr + 16
    a2 = e + pltpu.roll(e, nrow - 1, 0)
    a4 = a2 + pltpu.roll(a2, nrow - 2, 0)
    a8 = a4 + pltpu.roll(a4, nrow - 4, 0)
    a16 = a8 + pltpu.roll(a8, nrow - 8, 0)
    dx = _pool_select(grp, a2, a4, a8, a16)[:tr] - ddiff_e[:tr]
    return dx, dwbd, dscale


def _norm_rope(x, g, cos, sin_signed, seg):
    reps = x.shape[1] // 128
    ms = _split_dot(x * x, seg) * (1.0 / HEAD_DIM)
    r = lax.rsqrt(ms + EPS)
    xn = x * r * g
    c, s = _tile_lanes(cos, reps), _tile_lanes(sin_signed, reps)
    return xn * c + _swap_halves(xn) * s


def _norm_rope_bwd(x, g, cos, sin_signed, seg, dout):
    reps = x.shape[1] // 128
    c, s = _tile_lanes(cos, reps), _tile_lanes(sin_signed, reps)
    dxn = dout * c + _swap_halves(dout * s)
    ms = _split_dot(x * x, seg) * (1.0 / HEAD_DIM)
    r = lax.rsqrt(ms + EPS)
    xh = x * r
    gy = dxn * g
    dx = r * (gy - xh * (_split_dot(xh * gy, seg) * (1.0 / HEAD_DIM)))
    dg = jnp.sum(dxn * xh, axis=0, keepdims=True)
    return dx, dg


def _dup_heads(k):
    first = _lane(k.shape) < HEAD_DIM
    kr = pltpu.roll(k, HEAD_DIM, 1)
    return jnp.concatenate([jnp.where(first, k, kr), jnp.where(first, kr, k)], axis=1)


def _qkv_prep(z, cos, sin_signed, gq, gk, seg, *, tr, name, deps=()):
    T = z.shape[0]

    def body(qa_ref, qb_ref, kv_ref, c_ref, s_ref, gq_ref, gk_ref, seg_ref, q_ref, k_ref, v_ref):
        c, s, seg_m = c_ref[...], s_ref[...], seg_ref[...]
        scale = HEAD_DIM ** -0.5
        qa = _norm_rope(qa_ref[...].astype(F32), gq_ref[...], c, s, seg_m) * scale
        qb = _norm_rope(qb_ref[...].astype(F32), gq_ref[...], c, s, seg_m) * scale
        q_ref[...] = jnp.concatenate([qa, qb], axis=1).astype(q_ref.dtype)
        kv = kv_ref[...].astype(F32)
        k = _norm_rope(kv[:, :KV_WIDTH], gk_ref[...], c, s, seg_m[:128, :128])
        k_ref[...] = _dup_heads(k).astype(k_ref.dtype)
        v_ref[...] = _dup_heads(kv[:, KV_WIDTH:]).astype(v_ref.dtype)

    col = lambda j: pl.BlockSpec((tr, 256), lambda i: (i, j))
    tab = pl.BlockSpec((tr, 128), lambda i: (i, 0))
    return pl.pallas_call(
        _after(body, 8, deps), grid=(T // tr,),
        in_specs=[col(1), col(2), col(3), tab, tab, _full((1, 256)), _full((1, 128)), _full((256, 256))]
        + [ANY] * len(deps),
        out_specs=[pl.BlockSpec((tr, 512), lambda i: (i, 0)), col(0), col(0)],
        out_shape=[SDS((T, 512), MXU_DTYPE), SDS((T, 256), MXU_DTYPE), SDS((T, 256), MXU_DTYPE)],
        name=name, compiler_params=_cp("parallel"))(z, z, z, cos, sin_signed, gq, gk, seg, *deps)


GROUP_HEADS = 4
GROUP_ROWS = GROUP_HEADS * ATTN_BLOCK
ALL_ROWS = N_Q_HEADS * ATTN_BLOCK


def _attn_mask(has_prev):
    qi = _row((ALL_ROWS, 2 * ATTN_BLOCK)) % ATTN_BLOCK
    kj = _lane((ALL_ROWS, 2 * ATTN_BLOCK))
    return (kj > qi) & (kj <= qi + ATTN_BLOCK) & ((kj >= ATTN_BLOCK) | has_prev)


FWD_STEP_BLOCKS = 8
BWD_STEP_BLOCKS = 2


def _band(prev, cur, blk):
    lo = cur[(blk - 1) * ATTN_BLOCK:blk * ATTN_BLOCK] if blk else prev
    return jnp.concatenate([lo, cur[blk * ATTN_BLOCK:(blk + 1) * ATTN_BLOCK]], axis=0)


def _stack_heads(x, g):
    first = _lane((ATTN_BLOCK, 128)) < HEAD_DIM
    parts = []
    for pair in (2 * g, 2 * g + 1):
        x128 = x[:, 128 * pair:128 * (pair + 1)]
        zero = jnp.zeros_like(x128)
        parts += [jnp.where(first, x128, zero), jnp.where(first, zero, x128)]
    return jnp.concatenate(parts, axis=0)


def _unstack_heads(y):
    first = _lane((ATTN_BLOCK, 128)) < HEAD_DIM
    b = ATTN_BLOCK
    return jnp.concatenate([jnp.where(first, y[0:b], y[b:2 * b]), jnp.where(first, y[2 * b:3 * b], y[3 * b:4 * b])],
                           axis=1)


def _sink_col(sk_ref):
    return jnp.concatenate([jnp.broadcast_to(sk_ref[h:h + 1, 0:1], (ATTN_BLOCK, 1)) for h in range(N_Q_HEADS)],
                           axis=0)


def _by_group(a8, b2, dims=((1,), (0,))):
    return jnp.concatenate([_dot(a8[:GROUP_ROWS], b2[:, :128], dims), _dot(a8[GROUP_ROWS:], b2[:, 128:], dims)],
                           axis=0)


def _softmax_exp(q8, k2, mask, sink):
    s = jnp.where(mask, _by_group(q8, k2, NT), NEG_BIG)
    m = jnp.maximum(jnp.max(s, axis=1, keepdims=True), sink)
    p = jnp.exp(s - m)
    ps = jnp.exp(sink - m)
    return p, ps, 1.0 / (jnp.sum(p, axis=1, keepdims=True) + ps)


def _attn_fwd(q, k, v, sinks_b, *, name):
    T = q.shape[0]
    nb = T // ATTN_BLOCK
    STEP_BLOCKS = min(FWD_STEP_BLOCKS, nb)
    STEP_ROWS = STEP_BLOCKS * ATTN_BLOCK

    def body(q_ref, kc_ref, kp_ref, vc_ref, vp_ref, sk_ref, o_ref):
        n = pl.program_id(0)
        kc, kp, vc, vp = kc_ref[...], kp_ref[...], vc_ref[...], vp_ref[...]
        sink = _sink_col(sk_ref)
        for blk in range(STEP_BLOCKS):
            rows = slice(blk * ATTN_BLOCK, (blk + 1) * ATTN_BLOCK)
            mask = _attn_mask((n > 0) if blk == 0 else True)
            k2, v2 = _band(kp, kc, blk), _band(vp, vc, blk)
            qv = q_ref[rows, :]
            q8 = jnp.concatenate([_stack_heads(qv, 0), _stack_heads(qv, 1)], axis=0)
            p, _, inv = _softmax_exp(q8, k2, mask, sink)
            o8 = _by_group(p, v2) * inv
            o_ref[rows, :] = jnp.concatenate([_unstack_heads(o8[:GROUP_ROWS]), _unstack_heads(o8[GROUP_ROWS:])],
                                             axis=1).astype(o_ref.dtype)

    cur = lambda w: pl.BlockSpec((STEP_ROWS, w), lambda n: (n, 0))
    prev = lambda w: pl.BlockSpec((ATTN_BLOCK, w), lambda n: (jnp.maximum(STEP_BLOCKS * n - 1, 0), 0))
    return pl.pallas_call(
        body, grid=(nb // STEP_BLOCKS,),
        in_specs=[cur(512), cur(256), prev(256), cur(256), prev(256), _full((8, 128))],
        out_specs=cur(512), out_shape=SDS((T, 512), MXU_DTYPE), name=name,
        compiler_params=_cp("parallel"))(q, k, k, v, v, sinks_b)


def _attn_bwd(q, k, v, sinks_b, do, *, name, deps=()):
    T = q.shape[0]
    nb = T // ATTN_BLOCK
    STEP_BLOCKS = min(BWD_STEP_BLOCKS, nb)
    STEP_ROWS = STEP_BLOCKS * ATTN_BLOCK

    def body(q_ref, kc_ref, kp_ref, vc_ref, vp_ref, sk_ref, do_ref,
             dq_ref, dkc_ref, dkp_ref, dvc_ref, dvp_ref, dsk_ref):
        n = pl.program_id(0)
        kc, kp, vc, vp = kc_ref[...], kp_ref[...], vc_ref[...], vp_ref[...]
        sink = _sink_col(sk_ref)

        @pl.when(n == 0)
        def _():
            dsk_ref[...] = jnp.zeros_like(dsk_ref)

        for blk in range(STEP_BLOCKS):
            rows = slice(blk * ATTN_BLOCK, (blk + 1) * ATTN_BLOCK)
            mask = _attn_mask((n > 0) if blk == 0 else True)
            k2, v2 = _band(kp, kc, blk), _band(vp, vc, blk)
            qv, dov = q_ref[rows, :], do_ref[rows, :]
            q8 = jnp.concatenate([_stack_heads(qv, 0), _stack_heads(qv, 1)], axis=0)
            do8 = jnp.concatenate([_stack_heads(dov, 0), _stack_heads(dov, 1)], axis=0)
            p, ps, inv = _softmax_exp(q8, k2, mask, sink)
            pn = p * inv
            delta = jnp.sum(do8 * _by_group(pn, v2), axis=1, keepdims=True)
            ds = pn * (_by_group(do8, v2, NT) - delta)
            dq8 = _by_group(ds, k2)
            dq_ref[rows, :] = jnp.concatenate([_unstack_heads(dq8[:GROUP_ROWS]), _unstack_heads(dq8[GROUP_ROWS:])],
                                              axis=1)
            dk = jnp.concatenate([_dot(ds[:GROUP_ROWS], q8[:GROUP_ROWS], TN),
                                  _dot(ds[GROUP_ROWS:], q8[GROUP_ROWS:], TN)], axis=1)
            dv = jnp.concatenate([_dot(pn[:GROUP_ROWS], do8[:GROUP_ROWS], TN),
                                  _dot(pn[GROUP_ROWS:], do8[GROUP_ROWS:], TN)], axis=1)
            wsink = (ps * inv) * delta
            for h in range(N_Q_HEADS):
                dsink = -jnp.sum(wsink[ATTN_BLOCK * h:ATTN_BLOCK * (h + 1)], axis=0, keepdims=True)
                dsk_ref[h:h + 1, :] += jnp.broadcast_to(dsink, (1, 128))
            dkp_ref[rows, :] = dk[:ATTN_BLOCK]
            dkc_ref[rows, :] = dk[ATTN_BLOCK:]
            dvp_ref[rows, :] = dv[:ATTN_BLOCK]
            dvc_ref[rows, :] = dv[ATTN_BLOCK:]

    cur = lambda w: pl.BlockSpec((STEP_ROWS, w), lambda n: (n, 0))
    prev = lambda w: pl.BlockSpec((ATTN_BLOCK, w), lambda n: (jnp.maximum(STEP_BLOCKS * n - 1, 0), 0))
    f = lambda w: SDS((T, w), F32)
    return pl.pallas_call(
        _after(body, 7, deps), grid=(nb // STEP_BLOCKS,),
        in_specs=[cur(512), cur(256), prev(256), cur(256), prev(256), _full((8, 128)), cur(512)] + [ANY] * len(deps),
        out_specs=[cur(512), cur(256), cur(256), cur(256), cur(256), _full((8, 128))],
        out_shape=[f(512), f(256), f(256), f(256), f(256), SDS((8, 128), F32)],
        name=name, compiler_params=_cp("arbitrary"))(q, k, k, v, v, sinks_b, do, *deps)


def _mixer_ab_bwd(z, cos, sin_signed, gq, gk, seg, dq, dkc, dkp, dvc, dvp, dpa, wbd, scale, dz, *, tr, name, deps=()):
    T = z.shape[0]
    n = T // tr
    hb = tr // 16
    ab = tr // ATTN_BLOCK

    def unfold(cur, nxt_tile, nxt_halo, i):
        nxt = jnp.concatenate([nxt_tile[ATTN_BLOCK:], jnp.where(i == n - 1, 0.0, nxt_halo)], axis=0)
        tot = cur + nxt
        first = _lane((tr, 128)) < HEAD_DIM
        a = tot[:, :128]
        b = tot[:, 128:]
        a = a + pltpu.roll(a, HEAD_DIM, 1)
        b = b + pltpu.roll(b, HEAD_DIM, 1)
        return jnp.where(first, a, b)

    def body(xp_ref, xpp_ref, qa_ref, qb_ref, kv_ref, c_ref, s_ref, gq_ref, gk_ref, seg_ref,
             dq_ref, dkc_ref, dkp_ref, dkh_ref, dvc_ref, dvp_ref, dvh_ref, dpa_ref, dpan_ref, w_ref, sc_ref, _dz_in,
             dz_ref, dgq_ref, dgk_ref, dw_ref, dsc_ref):
        i = pl.program_id(0)
        c, s, seg_m = c_ref[...], s_ref[...], seg_ref[...]
        scale_q = HEAD_DIM ** -0.5
        dqv = dq_ref[...] * scale_q
        dxa, dga = _norm_rope_bwd(qa_ref[...].astype(F32), gq_ref[...], c, s, seg_m, dqv[:, :256])
        dxb, dgb = _norm_rope_bwd(qb_ref[...].astype(F32), gq_ref[...], c, s, seg_m, dqv[:, 256:])
        dk = unfold(dkc_ref[...], dkp_ref[...], dkh_ref[...], i)
        dv = unfold(dvc_ref[...], dvp_ref[...], dvh_ref[...], i)
        kv = kv_ref[...].astype(F32)
        dxk, dgk = _norm_rope_bwd(kv[:, :KV_WIDTH], gk_ref[...], c, s, seg_m[:128, :128], dk)
        dxp, dwbd, dscale = _pool_bwd_tile(i, n, tr, xp_ref[...].astype(F32), xpp_ref[...].astype(F32),
                                           dpa_ref[...], dpan_ref[...],
                                           w_ref[...], sc_ref[...])
        dz_ref[...] = jnp.concatenate([dxp, dxa, dxb, dxk, dv], axis=1).astype(dz_ref.dtype)

        @pl.when(i == 0)
        def _():
            dgq_ref[...] = jnp.zeros_like(dgq_ref)
            dgk_ref[...] = jnp.zeros_like(dgk_ref)
            dw_ref[...] = jnp.zeros_like(dw_ref)
            dsc_ref[...] = jnp.zeros_like(dsc_ref)
        dgq_ref[...] += _fold_lanes(dga + dgb, HEAD_DIM)
        dgk_ref[...] += _fold_lanes(dgk, HEAD_DIM)
        dw_ref[...] += dwbd
        dsc_ref[...] += dscale

    col = lambda j: pl.BlockSpec((tr, 256), lambda i: (i, j))
    rows = lambda w: pl.BlockSpec((tr, w), lambda i: (i, 0))
    nxt_blk = pl.BlockSpec((ATTN_BLOCK, 256), lambda i: (jnp.minimum((i + 1) * ab, T // ATTN_BLOCK - 1), 0))
    prev16 = pl.BlockSpec((16, 256), lambda i: (jnp.maximum(i * hb - 1, 0), 0))
    next16 = pl.BlockSpec((16, 256), lambda i: (jnp.minimum((i + 1) * hb, T // 16 - 1), 0))
    return pl.pallas_call(
        _after(body, 22, deps), grid=(n,),
        in_specs=[col(0), prev16, col(1), col(2), col(3), rows(128), rows(128),
                  _full((1, 256)), _full((1, 128)), _full((256, 256)),
                  rows(512), rows(256), rows(256), nxt_blk, rows(256), rows(256), nxt_blk,
                  rows(256), next16, _full((256, 256)), _full((1, 256)), ANY] + [ANY] * len(deps),
        out_specs=[rows(1024), _full((1, 256)), _full((1, 128)), _full((256, 256)), _full((1, 256))],
        out_shape=[SDS((T, IN_COLS), MXU_DTYPE), SDS((1, 256), F32), SDS((1, 128), F32),
                   SDS((256, 256), F32), SDS((1, 256), F32)],
        input_output_aliases={21: 0}, name=name, compiler_params=_cp("arbitrary"))(
            z, z, z, z, z, cos, sin_signed, gq, gk, seg, dq, dkc, dkp, dkp, dvc, dvp, dvp, dpa, dpa, wbd, scale, dz,
            *deps)


def _sgu_common(zu, zv, vn, seg):
    u, du = _gelu_and_grad(zu)
    gv, dgv = _gelu_and_grad(zv)
    ms = _split_dot(gv * gv, seg) * (1.0 / HEAD_DIM)
    r = lax.rsqrt(ms + EPS)
    xh = gv * r
    return u, du, dgv, r, xh, xh * vn


def _sgu_fwd(z, wtril, bexp, vn, seg, *, tr, name):
    T = z.shape[0]
    nch = tr // CHUNK

    def body(u_ref, v_ref, w_ref, b_ref, vn_ref, seg_ref, o_ref):
        u, _, _, _, _, vg = _sgu_common(u_ref[...].astype(F32), v_ref[...].astype(F32), vn_ref[...], seg_ref[...])
        grp = _lane((CHUNK, SGU_WIDTH)) // HEAD_DIM
        outs = []
        for ch in range(nch):
            vc = vg[ch * CHUNK:(ch + 1) * CHUNK]
            s = b_ref[...]
            for g in range(4):
                s = s + jnp.where(grp == g, _dot(w_ref[g], vc), 0.0)
            outs.append(u[ch * CHUNK:(ch + 1) * CHUNK] * s)
        o_ref[...] = jnp.concatenate(outs, axis=0).astype(o_ref.dtype)

    col = lambda j: pl.BlockSpec((tr, 256), lambda i: (i, j))
    return pl.pallas_call(
        body, grid=(T // tr,),
        in_specs=[col(4), col(5), _full((4, CHUNK, CHUNK)), _full((CHUNK, 256)), _full((1, 256)), _full((256, 256))],
        out_specs=col(0), out_shape=SDS((T, SGU_WIDTH), MXU_DTYPE), name=name,
        compiler_params=_cp("parallel"))(z, z, wtril, bexp, vn, seg)


def _sgu_bwd(z, wtril, bexp, vn, seg, dsg, dz, *, tr, name):
    T = z.shape[0]
    nch = tr // CHUNK

    def body(u_ref, v_ref, w_ref, b_ref, vn_ref, seg_ref, d_ref, _dz_in, dz_ref, dw_ref, db_ref, dvn_ref):
        i = pl.program_id(0)
        seg_m = seg_ref[...]
        vn_v = vn_ref[...]
        u, du, dgv, r, xh, vg = _sgu_common(u_ref[...].astype(F32), v_ref[...].astype(F32), vn_v, seg_m)
        d = d_ref[...]
        grp = _lane((CHUNK, SGU_WIDTH)) // HEAD_DIM
        tril = _row((CHUNK, CHUNK)) >= _lane((CHUNK, CHUNK))

        @pl.when(i == 0)
        def _():
            dw_ref[...] = jnp.zeros_like(dw_ref)
            db_ref[...] = jnp.zeros_like(db_ref)
            dvn_ref[...] = jnp.zeros_like(dvn_ref)

        dus, dvgs = [], []
        for ch in range(nch):
            sl = slice(ch * CHUNK, (ch + 1) * CHUNK)
            vc = vg[sl]
            s = b_ref[...]
            for g in range(4):
                s = s + jnp.where(grp == g, _dot(w_ref[g], vc), 0.0)
            dus.append(d[sl] * s)
            ds = d[sl] * u[sl]
            db_ref[...] += _split_dot(ds, seg_m)
            dvg = jnp.zeros((CHUNK, SGU_WIDTH), F32)
            for g in range(4):
                dsm = jnp.where(grp == g, ds, 0.0)
                dvg = dvg + jnp.where(grp == g, _dot(w_ref[g], ds, TN), 0.0)
                dw_ref[g] += jnp.where(tril, _dot(dsm, vc, NT), 0.0)
            dvgs.append(dvg)
        dup = jnp.concatenate(dus, axis=0)
        dvg = jnp.concatenate(dvgs, axis=0)
        dvn_ref[...] += _fold_lanes(jnp.sum(dvg * xh, axis=0, keepdims=True), HEAD_DIM)
        gy = dvg * vn_v
        dgvv = r * (gy - xh * (_split_dot(xh * gy, seg_m) * (1.0 / HEAD_DIM)))
        dz_ref[...] = jnp.concatenate([dup * du, dgvv * dgv], axis=1).astype(dz_ref.dtype)

    col = lambda j: pl.BlockSpec((tr, 256), lambda i: (i, j))
    return pl.pallas_call(
        body, grid=(T // tr,),
        in_specs=[col(4), col(5), _full((4, CHUNK, CHUNK)), _full((CHUNK, 256)), _full((1, 256)), _full((256, 256)),
                  col(0), ANY],
        out_specs=[pl.BlockSpec((tr, 512), lambda i: (i, 2)), _full((4, CHUNK, CHUNK)), _full((CHUNK, 256)),
                   _full((1, 256))],
        out_shape=[SDS((T, IN_COLS), MXU_DTYPE), SDS((4, CHUNK, CHUNK), F32), SDS((CHUNK, 256), F32),
                   SDS((1, 256), F32)],
        input_output_aliases={7: 0}, name=name, compiler_params=_cp("arbitrary"))(
            z, z, wtril, bexp, vn, seg, dsg, dz)


def _merge_fwd(pa, at, sg, wa, wb, wc, z, x, w_out, *, tm, tn, name):
    T = pa.shape[0]
    gb = GATE_COL0 // tn
    nb = D_MODEL // tn

    def body(pa_ref, at_ref, sg_ref, wa_ref, wb_ref, wc_ref, g0_ref, g1_ref, g2_ref, x_ref, wo_ref,
             m_ref, y_ref, x1_ref):
        j = pl.program_id(1)
        acc = None
        for idx, (op_ref, w_ref, g_ref) in enumerate(((pa_ref, wa_ref, g0_ref), (at_ref, wb_ref, g1_ref),
                                                      (sg_ref, wc_ref, g2_ref))):
            y = _dot(op_ref[...], w_ref[...])
            y_ref[idx] = y.astype(y_ref.dtype)
            t = _sigmoid(g_ref[...].astype(F32)) * y
            acc = t if acc is None else acc + t
        merged = acc.astype(m_ref.dtype)
        m_ref[...] = merged
        p = _dot(merged, wo_ref[...])

        @pl.when(j == 0)
        def _():
            x1_ref[...] = x_ref[...] + p

        @pl.when(j > 0)
        def _():
            x1_ref[...] += p

    op = lambda w: pl.BlockSpec((tm, w), lambda i, j: (i, 0))
    wt = lambda k: pl.BlockSpec((k, tn), lambda i, j: (0, j))
    gate = lambda b: pl.BlockSpec((tm, tn), lambda i, j: (i, gb + b * nb + j))
    return pl.pallas_call(
        body, grid=(T // tm, nb),
        in_specs=[op(256), op(512), op(256), wt(256), wt(512), wt(256), gate(0), gate(1), gate(2),
                  op(D_MODEL), pl.BlockSpec((tn, D_MODEL), lambda i, j: (j, 0))],
        out_specs=[pl.BlockSpec((tm, tn), lambda i, j: (i, j)), pl.BlockSpec((3, tm, tn), lambda i, j: (0, i, j)),
                   op(D_MODEL)],
        out_shape=[SDS((T, D_MODEL), MXU_DTYPE), SDS((3, T, D_MODEL), MXU_DTYPE), SDS((T, D_MODEL), F32)],
        name=name, compiler_params=_cp("parallel", "arbitrary"))(pa, at, sg, wa, wb, wc, z, z, z, x, w_out)


def _out_dx_merge_bwd(dxb, w_out, y, z, ws, xs, *, tm, tn, name):
    T = dxb.shape[0]
    gb = GATE_COL0 // tn
    nb = D_MODEL // tn
    nr = T // tm
    widths = [w.shape[0] for w in ws]

    def body(dx_ref, w_ref, y_ref, g_ref, *refs):
        w_refs, x_refs = refs[0:3], refs[3:6]
        dz_ref, dx_refs, dw_refs = refs[6], refs[7:10], refs[10:13]
        dm_ref, acc_refs = refs[13], refs[14:17]
        i, b, j = pl.program_id(0), pl.program_id(1), pl.program_id(2)

        @pl.when((b == 0) & (j == 0))
        def _():
            dm = _dot(dx_ref[...], w_ref[...], NT)
            for jj in range(nb):
                dm_ref[jj] = dm[:, jj * tn:(jj + 1) * tn]

        d = dm_ref[j]
        g = _sigmoid(g_ref[...].astype(F32))
        dy = (d * g).astype(MXU_DTYPE)
        dz_ref[...] = (d * y_ref[...].astype(F32) * g * (1.0 - g)).astype(dz_ref.dtype)
        for branch in range(3):
            @pl.when(b == branch)
            def _():
                p = _dot(dy, w_refs[branch][...], NT)
                q = _dot(x_refs[branch][...], dy, TN)

                @pl.when(j == 0)
                def _():
                    dx_refs[branch][...] = p

                @pl.when(j > 0)
                def _():
                    dx_refs[branch][...] += p

                @pl.when(i == 0)
                def _():
                    acc_refs[branch][j] = q

                @pl.when(i > 0)
                def _():
                    acc_refs[branch][j] += q

        @pl.when((i == nr - 1) & (b == 2) & (j == nb - 1))
        def _():
            for branch in range(3):
                for jj in range(nb):
                    dw_refs[branch][:, jj * tn:(jj + 1) * tn] = acc_refs[branch][jj]

    wspec = lambda k: pl.BlockSpec((k, tn), lambda i, b, j: (0, j))
    rows = lambda k: pl.BlockSpec((tm, k), lambda i, b, j: (i, 0))
    return pl.pallas_call(
        body, grid=(nr, 3, nb),
        in_specs=[rows(D_MODEL),
                  pl.BlockSpec((D_MODEL, D_MODEL), lambda i, b, j: (0, 0), pipeline_mode=pl.Buffered(1)),
                  pl.BlockSpec((None, tm, tn), lambda i, b, j: (b, i, j)),
                  pl.BlockSpec((tm, tn), lambda i, b, j: (i, gb + b * nb + j))]
        + [wspec(k) for k in widths] + [rows(k) for k in widths],
        out_specs=[pl.BlockSpec((tm, tn), lambda i, b, j: (i, gb + b * nb + j))]
        + [rows(k) for k in widths] + [_full((k, D_MODEL)) for k in widths],
        out_shape=[SDS((T, IN_COLS), MXU_DTYPE)] + [SDS((T, k), F32) for k in widths]
        + [SDS((k, D_MODEL), F32) for k in widths],
        scratch_shapes=[pltpu.VMEM((nb, tm, tn), F32)] + [pltpu.VMEM((nb, k, tn), F32) for k in widths],
        name=name, compiler_params=_cp("arbitrary", "arbitrary", "arbitrary"))(dxb, w_out, y, z, *ws, *xs)


def _conv3(xe, w, b):
    return (w[0:1] * pltpu.roll(xe, 2, 0) + w[1:2] * pltpu.roll(xe, 1, 0) + w[2:3] * xe)[8:] + b


def _conv_act_fwd(up, cw, cb, *, tr, tc, name):
    T = up.shape[0]
    nc = D_FF // tc
    hb = tr // HALO

    def body(ug_ref, ugp_ref, uv_ref, uvp_ref, wg_ref, wv_ref, bg_ref, bv_ref, o_ref):
        i = pl.program_id(1)
        first = i == 0

        def halo_tile(prev_ref, cur_ref):
            prev8 = prev_ref[...].astype(F32)[HALO - 8:]
            return jnp.concatenate([jnp.where(first, 0.0, prev8), cur_ref[...].astype(F32)], axis=0)

        cg = _conv3(halo_tile(ugp_ref, ug_ref), wg_ref[...], bg_ref[...])
        cv = _conv3(halo_tile(uvp_ref, uv_ref), wv_ref[...], bv_ref[...])
        o_ref[...] = (cg * _sigmoid(cg) * cv).astype(o_ref.dtype)

    tile = lambda off: pl.BlockSpec((tr, tc), lambda j, i: (i, off + j))
    prev = lambda off: pl.BlockSpec((HALO, tc), lambda j, i: (jnp.maximum(i * hb - 1, 0), off + j))
    par = lambda rows, off: pl.BlockSpec((rows, tc), lambda j, i: (0, off + j))
    return pl.pallas_call(
        body, grid=(nc, T // tr),
        in_specs=[tile(0), prev(0), tile(nc), prev(nc), par(3, 0), par(3, nc), par(1, 0), par(1, nc)],
        out_specs=pl.BlockSpec((tr, tc), lambda j, i: (i, j)),
        out_shape=SDS((T, D_FF), MXU_DTYPE), name=name,
        compiler_params=_cp("parallel", "parallel"))(up, up, up, up, cw, cw, cb, cb)


def _conv_act_bwd(up, cw, cb, dact, *, tr, tc, name, deps=()):
    T = up.shape[0]
    nc = D_FF // tc
    hb = tr // 8
    hbu = tr // HALO
    nr = T // tr

    def body(ug_ref, ugp_ref, ugn_ref, uv_ref, uvp_ref, uvn_ref, da_ref, dan_ref, wg_ref, wv_ref, bg_ref, bv_ref,
             du_ref, dwg_ref, dwv_ref, dbg_ref, dbv_ref):
        i = pl.program_id(1)
        first, last = i == 0, i == nr - 1
        da = jnp.concatenate([da_ref[...], jnp.where(last, 0.0, dan_ref[...])], axis=0)

        def with_halos(prev_ref, cur_ref, next_ref):
            prev8 = prev_ref[...].astype(F32)[HALO - 8:]
            next8 = next_ref[...].astype(F32)[:8]
            return jnp.concatenate([jnp.where(first, 0.0, prev8), cur_ref[...].astype(F32), next8], axis=0)

        uge = with_halos(ugp_ref, ug_ref, ugn_ref)
        uve = with_halos(uvp_ref, uv_ref, uvn_ref)
        wg, wv = wg_ref[...], wv_ref[...]
        ug1, ug2 = pltpu.roll(uge, 1, 0)[8:], pltpu.roll(uge, 2, 0)[8:]
        uv1, uv2 = pltpu.roll(uve, 1, 0)[8:], pltpu.roll(uve, 2, 0)[8:]
        cg = wg[0:1] * ug2 + wg[1:2] * ug1 + wg[2:3] * uge[8:] + bg_ref[...]
        cv = wv[0:1] * uv2 + wv[1:2] * uv1 + wv[2:3] * uve[8:] + bv_ref[...]
        sg = _sigmoid(cg)
        dcg = da * cv * (sg * (1.0 + cg * (1.0 - sg)))
        dcv = da * (cg * sg)
        nrow = tr + 8

        def back(dc, w):
            return (w[2:3] * dc + w[1:2] * pltpu.roll(dc, nrow - 1, 0) + w[0:1] * pltpu.roll(dc, nrow - 2, 0))[:tr]

        du_ref[0] = back(dcg, wg).astype(du_ref.dtype)
        du_ref[1] = back(dcv, wv).astype(du_ref.dtype)

        def wgrad(dc, u0, u1, u2):
            d = dc[:tr]
            rows = [jnp.sum(d * u2[:tr], axis=0, keepdims=True), jnp.sum(d * u1[:tr], axis=0, keepdims=True),
                    jnp.sum(d * u0[8:8 + tr], axis=0, keepdims=True)]
            return jnp.concatenate(rows, axis=0), jnp.sum(d, axis=0, keepdims=True)

        dwg, dbg = wgrad(dcg, uge, ug1, ug2)
        dwv, dbv = wgrad(dcv, uve, uv1, uv2)

        @pl.when(first)
        def _():
            dwg_ref[...] = jnp.zeros_like(dwg_ref)
            dwv_ref[...] = jnp.zeros_like(dwv_ref)
            dbg_ref[...] = jnp.zeros_like(dbg_ref)
            dbv_ref[...] = jnp.zeros_like(dbv_ref)
        dwg_ref[...] += dwg
        dwv_ref[...] += dwv
        dbg_ref[...] += dbg
        dbv_ref[...] += dbv

    tile = lambda off: pl.BlockSpec((tr, tc), lambda j, i: (i, off + j))
    prev = lambda off: pl.BlockSpec((HALO, tc), lambda j, i: (jnp.maximum(i * hbu - 1, 0), off + j))
    nxt = lambda off: pl.BlockSpec((HALO, tc), lambda j, i: (jnp.minimum((i + 1) * hbu, T // HALO - 1), off + j))
    dnext = pl.BlockSpec((8, tc), lambda j, i: (jnp.minimum((i + 1) * hb, T // 8 - 1), j))
    par = lambda rows, off: pl.BlockSpec((rows, tc), lambda j, i: (0, off + j))
    acc = lambda rows: pl.BlockSpec((rows, tc), lambda j, i: (0, j))
    return pl.pallas_call(
        _after(body, 12, deps), grid=(nc, nr),
        in_specs=[tile(0), prev(0), nxt(0), tile(nc), prev(nc), nxt(nc), tile(0), dnext,
                  par(3, 0), par(3, nc), par(1, 0), par(1, nc)] + [ANY] * len(deps),
        out_specs=[pl.BlockSpec((2, tr, tc), lambda j, i: (0, i, j)), acc(3), acc(3), acc(1), acc(1)],
        out_shape=[SDS((2, T, D_FF), MXU_DTYPE), SDS((3, D_FF), F32), SDS((3, D_FF), F32),
                   SDS((1, D_FF), F32), SDS((1, D_FF), F32)],
        name=name, compiler_params=_cp("parallel", "arbitrary"))(
            up, up, up, up, up, up, dact, dact, cw, cw, cb, cb, *deps)


def _row_tile(rows, cap):
    t = min(cap, rows)
    t -= t % 8
    while rows % t:
        t -= 8
    return t


def _adamw(w, g, m, v, *, tr, name, copy_g=False):
    R, C = w.shape
    assert R % tr == 0, (R, tr)

    def body(w_ref, g_ref, m_ref, v_ref, d_ref, nm_ref, nv_ref, *rest):
        gv = g_ref[...]
        mn = ADAM_B1 * m_ref[...] + (1.0 - ADAM_B1) * gv
        vn = ADAM_B2 * v_ref[...] + (1.0 - ADAM_B2) * (gv * gv)
        m_hat = mn / (1.0 - ADAM_B1 ** ADAM_STEP)
        v_hat = vn / (1.0 - ADAM_B2 ** ADAM_STEP)
        d_ref[...] = -ADAM_LR * (m_hat / (jnp.sqrt(v_hat) + ADAM_EPS) + ADAM_WD * w_ref[...])
        nm_ref[...] = mn
        nv_ref[...] = vn
        if copy_g:
            rest[0][...] = gv

    rows = pl.BlockSpec((tr, C), lambda i: (i, 0))
    n_out = 4 if copy_g else 3
    return pl.pallas_call(
        body, grid=(R // tr,), in_specs=[rows] * 4, out_specs=[rows] * n_out,
        out_shape=[SDS((R, C), F32)] * n_out, name=name, compiler_params=_cp("parallel"))(w, g, m, v)


def _sum_slots(r, *, tr, name):
    S, R, C = r.shape
    assert R % tr == 0, (R, tr)

    def body(r_ref, o_ref):
        acc = r_ref[0]
        for s in range(1, S):
            acc = acc + r_ref[s]
        o_ref[...] = acc

    return pl.pallas_call(
        body, grid=(R // tr,), in_specs=[pl.BlockSpec((S, tr, C), lambda i: (0, i, 0))],
        out_specs=pl.BlockSpec((tr, C), lambda i: (i, 0)), out_shape=SDS((R, C), F32),
        name=name, compiler_params=_cp("parallel"))(r)


def _pair_add(g4, h, pos, *, name):
    A, _, r, C = g4.shape
    cs = C if A == N_CHIPS else C // N_CHIPS
    tr = _row_tile(r, 256)
    if A == N_CHIPS:
        g_map, h_map = (lambda t, i, pos: (t, pos[1], i, 0)), (lambda t, i, pos: (t, i, 0))
    else:
        g_map, h_map = (lambda t, i, pos: (0, pos[1], i, t)), (lambda t, i, pos: (0, i, t))

    def body(pos_ref, g_ref, h_ref, o_ref):
        o_ref[...] = (g_ref[...] + h_ref[...]).astype(o_ref.dtype)

    grid_spec = pltpu.PrefetchScalarGridSpec(
        num_scalar_prefetch=1, grid=(N_CHIPS, r // tr),
        in_specs=[pl.BlockSpec((None, None, tr, cs), g_map), pl.BlockSpec((None, tr, cs), h_map)],
        out_specs=pl.BlockSpec((None, tr, cs), lambda t, i, pos: (t, i, 0)))
    return pl.pallas_call(body, grid_spec=grid_spec, out_shape=SDS((N_CHIPS, r, cs), COMM_DTYPE), name=name,
                          compiler_params=_cp("parallel", "parallel"))(pos, g4, h)


def _chip_sum(p, r2, f_into, pos, layer, *, name):
    _, r, cs = p.shape
    tr = _row_tile(r, 256)

    def body(pos_ref, own_ref, r_ref, *rest):
        o_ref = rest[-1]
        o_ref[...] = ((own_ref[...].astype(F32) + r_ref[0].astype(F32)) + r_ref[1].astype(F32)) + r_ref[2].astype(F32)

    in_specs = [pl.BlockSpec((None, tr, cs), lambda i, pos: (pos[0], i, 0)),
                pl.BlockSpec((3, tr, cs), lambda i, pos: (0, i, 0))]
    operands = [pos, p, r2]
    aliases = {}
    if f_into is not None:
        in_specs.append(ANY)
        operands.append(f_into)
        aliases = {3: 0}
    grid_spec = pltpu.PrefetchScalarGridSpec(
        num_scalar_prefetch=1, grid=(r // tr,), in_specs=in_specs,
        out_specs=pl.BlockSpec((None, None, tr, cs), lambda i, pos: (layer, pos[1], i, 0)))
    return pl.pallas_call(body, grid_spec=grid_spec, out_shape=SDS((DEPTH, 2, r, cs), F32), name=name,
                          input_output_aliases=aliases, compiler_params=_cp("parallel"))(*operands)


def _mesh_pos():
    return lax.axis_index("x"), lax.axis_index("y"), lax.axis_index("c")


HBM = pl.BlockSpec(memory_space=pltpu.HBM)
SEM = pl.BlockSpec(memory_space=pltpu.SEMAPHORE)
DATAFLOW = pltpu.SideEffectType.DATAFLOW_SIDE_EFFECTING
CHIP_FLIPS = (2, 1, 3)


def _chip_peers():
    x, y, c = _mesh_pos()
    return 2 * x + y, [(1 - x, y, c), (x, 1 - y, c), (1 - x, 1 - y, c)], (x, y, 1 - c), c


def _split_start(arrays, n_copies, issue, *, name, deps=()):
    k = len(arrays)
    nd = len(deps)

    def body(*refs):
        issue(refs[:k], refs[k + nd], refs[k + nd + 1])
        refs[2 * k + nd + 2][...] = jnp.zeros((8, 128), F32)

    out = pl.pallas_call(
        body, name=name,
        out_shape=(pltpu.SemaphoreType.DMA((n_copies,)), pltpu.SemaphoreType.DMA((n_copies,)),
                   *[pltpu.HBM(a.shape, a.dtype) for a in arrays], SDS((8, 128), F32)),
        in_specs=[HBM] * k + [ANY] * nd, out_specs=(SEM, SEM, *[HBM] * k, pl.BlockSpec(memory_space=pltpu.VMEM)),
        input_output_aliases={i: 2 + i for i in range(k)},
        compiler_params=pltpu.CompilerParams(has_side_effects=DATAFLOW))(
            *[pltpu.with_memory_space_constraint(a, pltpu.HBM) for a in arrays], *deps)
    return (out[0], out[1]), list(out[2:2 + k]), out[2 + k]


def _split_wait(sems, arrays, after, waits, *, name):
    k = len(arrays)
    afters = tuple(after) if isinstance(after, (tuple, list)) else (after,)

    def body(*refs):
        waits(refs[:k], refs[k], refs[k + 1])

    out = pl.pallas_call(
        body, name=name, out_shape=tuple(pltpu.HBM(a.shape, a.dtype) for a in arrays),
        in_specs=[HBM] * k + [SEM, SEM] + [ANY] * len(afters), out_specs=tuple([HBM] * k),
        input_output_aliases={i: i for i in range(k)},
        compiler_params=pltpu.CompilerParams(has_side_effects=DATAFLOW))(*arrays, sems[0], sems[1], *afters)
    return list(out)


def _wait_both(cp):
    cp.wait_send()
    cp.wait_recv()


def _cast_place(shard, pos, dtype, *, name, layer=None, slots=N_CHIPS, which=0, deps=()):
    R, C = shard.shape[-2:]
    tr = R if R % 8 else _row_tile(R, 256)
    if layer is None:
        in_spec = pl.BlockSpec((tr, C), lambda i, pos: (i, 0))
    else:
        in_spec = pl.BlockSpec((None, tr, C), lambda i, pos: (layer, i, 0))

    def body(pos_ref, x_ref, o_ref):
        o_ref[...] = x_ref[...].astype(o_ref.dtype)

    grid_spec = pltpu.PrefetchScalarGridSpec(
        num_scalar_prefetch=1, grid=(R // tr,), in_specs=[in_spec] + [ANY] * len(deps),
        out_specs=pl.BlockSpec((None, tr, C), lambda i, pos: (pos[which], i, 0)))
    return pl.pallas_call(_after(body, 2, deps), grid_spec=grid_spec, out_shape=SDS((slots, R, C), dtype), name=name,
                          compiler_params=_cp("parallel"))(pos, shard, *deps)


def _device_peers():
    x, y, c = _mesh_pos()
    peers = [(x ^ ((f >> 2) & 1), y ^ ((f >> 1) & 1), c ^ (f & 1)) for f in range(1, N_DEV)]
    return 4 * x + 2 * y + c, peers


class _Gather:
    def __init__(self, lands, name, deps=(), all_devices=False, halves=False):
        n = len(lands)
        self.name, self.halves = name, halves
        npeer = N_DEV - 1 if all_devices else N_CHIPS - 1
        if halves:
            lands = [a.reshape(a.shape[0], 2, a.shape[1] // 2, a.shape[2]) for a in lands]

        def copies(refs, ss, rs):
            if halves:
                me, peers, _, c = _chip_peers()
                own = lambda r: r.at[me, c]
            else:
                me, peers = _device_peers() if all_devices else _chip_peers()[:2]
                own = lambda r: r.at[me]
            return [pltpu.make_async_remote_copy(
                src_ref=own(refs[w]), dst_ref=own(refs[w]), send_sem=ss.at[npeer * w + p],
                recv_sem=rs.at[npeer * w + p], device_id=peers[p], device_id_type=MESH)
                for w in range(n) for p in range(npeer)]

        def issue(refs, ss, rs):
            for cp in copies(refs, ss, rs):
                cp.start()

        def waits(refs, ss, rs):
            for cp in copies(refs, ss, rs):
                _wait_both(cp)

        self._waits = waits
        self.sems, self.arrays, self.token = _split_start(list(lands), npeer * n, issue, name=name + "_start",
                                                          deps=deps)

    def wait(self, after):
        arrays = _split_wait(self.sems, self.arrays, after, self._waits, name=self.name + "_wait")
        if not self.halves:
            return arrays
        n = len(arrays)

        def copies(refs, ss, rs):
            me, _, sibling, c = _chip_peers()
            return [pltpu.make_async_remote_copy(
                src_ref=refs[w].at[me ^ CHIP_FLIPS[p], c], dst_ref=refs[w].at[me ^ CHIP_FLIPS[p], c],
                send_sem=ss.at[3 * w + p], recv_sem=rs.at[3 * w + p], device_id=sibling, device_id_type=MESH)
                for w in range(n) for p in range(3)]

        def issue(refs, ss, rs):
            for cp in copies(refs, ss, rs):
                cp.start()

        def waits(refs, ss, rs):
            for cp in copies(refs, ss, rs):
                _wait_both(cp)

        sems, arrays, _ = _split_start(arrays, 3 * n, issue, name=self.name + "_share_start")
        arrays = _split_wait(sems, arrays, after, waits, name=self.name + "_share_wait")
        return [a.reshape(a.shape[0], 2 * a.shape[2], a.shape[3]) for a in arrays]


def _swap_halves_start(g4s, *, name):
    n = len(g4s)
    lands = [lax.empty((g.shape[0],) + g.shape[2:], g.dtype) for g in g4s]

    def copies(refs, ss, rs):
        _, _, sibling, c = _chip_peers()
        return [pltpu.make_async_remote_copy(
            src_ref=refs[w].at[:, 1 - c], dst_ref=refs[n + w], send_sem=ss.at[w], recv_sem=rs.at[w],
            device_id=sibling, device_id_type=MESH) for w in range(n)]

    def issue(refs, ss, rs):
        for cp in copies(refs, ss, rs):
            cp.start()

    def waits(refs, ss, rs):
        for cp in copies(refs, ss, rs):
            _wait_both(cp)

    sems, arrays, token = _split_start(list(g4s) + lands, n, issue, name=name + "_start")
    return sems, arrays, token, waits


def _scatter_start(parts, *, name, deps=()):
    n = len(parts)
    lands = [lax.empty((3,) + p.shape[1:], p.dtype) for p in parts]

    def copies(refs, ss, rs):
        me, peers, _, _ = _chip_peers()
        return [pltpu.make_async_remote_copy(
            src_ref=refs[w].at[me ^ CHIP_FLIPS[p]], dst_ref=refs[n + w].at[p],
            send_sem=ss.at[3 * w + p], recv_sem=rs.at[3 * w + p], device_id=peers[p], device_id_type=MESH)
            for w in range(n) for p in range(3)]

    def issue(refs, ss, rs):
        for cp in copies(refs, ss, rs):
            cp.start()

    def waits(refs, ss, rs):
        for cp in copies(refs, ss, rs):
            _wait_both(cp)

    sems, arrays, token = _split_start(list(parts) + lands, 3 * n, issue, name=name + "_start", deps=deps)
    return sems, arrays, token, waits


def _pair_share_start(fs, layer, *, name):
    n = len(fs)

    def copies(refs, ss, rs):
        _, _, sibling, c = _chip_peers()
        return [pltpu.make_async_remote_copy(
            src_ref=refs[w].at[layer, c], dst_ref=refs[w].at[layer, c], send_sem=ss.at[w], recv_sem=rs.at[w],
            device_id=sibling, device_id_type=MESH) for w in range(n)]

    def issue(refs, ss, rs):
        for cp in copies(refs, ss, rs):
            cp.start()

    def waits(refs, ss, rs):
        for cp in copies(refs, ss, rs):
            _wait_both(cp)

    sems, arrays, token = _split_start(list(fs), n, issue, name=name + "_start")
    return sems, arrays, token, waits


BIG = ('w_in', 'w_proj_a', 'w_proj_b', 'w_proj_c', 'w_out', 'w_up', 'w_down')
BIG_SHARD_AXIS = {'w_in': 2, 'w_proj_a': 2, 'w_proj_b': 2, 'w_proj_c': 2, 'w_out': 1, 'w_up': 2, 'w_down': 1}
SMALL = ('norm1', 'q_norm', 'k_norm', 'sinks', 'w_pool', 'pool_scale', 'sgu_v_norm', 'w_s', 'b_s', 'norm2',
         'conv_b', 'conv_w')
WEIGHTS = ('norm1', 'w_in', 'q_norm', 'k_norm', 'sinks', 'w_pool', 'pool_scale', 'sgu_v_norm', 'w_s', 'b_s',
           'w_proj_a', 'w_proj_b', 'w_proj_c', 'w_out', 'norm2', 'w_up', 'conv_w', 'conv_b', 'w_down')


def _rope_tables(positions):
    inv_freq = ROPE_THETA ** (-jnp.arange(0, HEAD_DIM, 2, dtype=F32) / HEAD_DIM)
    ang = positions.astype(F32)[:, None] * inv_freq
    cos, sin = jnp.cos(ang), jnp.sin(ang)
    c = jnp.concatenate([cos, cos], axis=1)
    s = jnp.concatenate([-sin, sin], axis=1)
    return jnp.concatenate([c, c], axis=1), jnp.concatenate([s, s], axis=1)


def _block_diag4(w):
    out = jnp.zeros((POOL_WIDTH, POOL_WIDTH), w.dtype)
    for g in range(4):
        out = lax.dynamic_update_slice(out, w[g], (g * HEAD_DIM, g * HEAD_DIM))
    return out


def _local_step(x, target, cos, sin, sp, sched):
    T = x.shape[0]
    tm1 = min(1024, T)
    tm = min(512, T)
    tr = min(1024, T)
    trc = min(512, T)
    tkt = min(2048, T)
    seg = _seg_matrix(256, HEAD_DIM)
    saved = []
    xl = x
    for l in range(DEPTH):
        p = f"l{l}_"
        c = dict(
            g1=sp['norm1'][l][None], g2=sp['norm2'][l][None],
            wbd=_block_diag4(sp['w_pool'][l]).astype(MXU_DTYPE), scale=sp['pool_scale'][l][None],
            gq=jnp.tile(sp['q_norm'][l], 4)[None], gk=jnp.tile(sp['k_norm'][l], 2)[None],
            sinks=jnp.broadcast_to(sp['sinks'][l][:, None], (N_Q_HEADS, 128)),
            wtril=jnp.tril(sp['w_s'][l]).astype(MXU_DTYPE),
            bexp=jnp.repeat(sp['b_s'][l].T, HEAD_DIM, axis=1), vn=jnp.tile(sp['sgu_v_norm'][l], 4)[None],
            cb=sp['conv_b'][l][None])
        c['w_in'] = sched.weight('w_in', l, (xl, *sched.start_tokens()) if l == 0 else xl)
        z, h1 = _norm_mm(xl, c['g1'], c['w_in'], tm=tm1, tn=1152, out_dtype=ACT_DTYPE, name=p + "in_proj",
                         deps=sched.start_tokens() if l == 0 else ())
        pa = _pool_fwd(z, c['wbd'], c['scale'], tr=tr, name=p + "pool")
        q, k, v = _qkv_prep(z, cos, sin, c['gq'], c['gk'], seg, tr=tr, name=p + "qkv_prep",
                            deps=sched.start_next_layer(z) if l == 0 else ())
        at = _attn_fwd(q, k, v, c['sinks'], name=p + "attn")
        sg = _sgu_fwd(z, c['wtril'], c['bexp'], c['vn'], seg, tr=tr, name=p + "sgu")
        for n in ('w_proj_a', 'w_proj_b', 'w_proj_c', 'w_out'):
            c[n] = sched.weight(n, l, (pa, at, sg))
        merged, y3, x1 = _merge_fwd(pa, at, sg, c['w_proj_a'], c['w_proj_b'], c['w_proj_c'], z, xl, c['w_out'],
                                    tm=tm1, tn=512, name=p + "merge_out_proj")
        for n in ('w_up', 'conv_w', 'w_down'):
            c[n] = sched.weight(n, l, x1)
        up, h2 = _norm_mm(x1, c['g2'], c['w_up'], tm=tm1, tn=1408, out_dtype=F32, name=p + "up_proj")
        act = _conv_act_fwd(up, c['conv_w'], c['cb'], tr=trc, tc=1408, name=p + "conv_act")
        saved.append(dict(c, x=xl, h1=h1, z=z, pa=pa, q=q, k=k, v=v, at=at, sg=sg, merged=merged, y3=y3,
                          x1=x1, h2=h2, up=up, act=act))
        if l < DEPTH - 1:
            xl = _mm(act, c['w_down'], mode='nn', add=x1, tm=tm, tn=D_MODEL, tk=D_FF, name=p + "down_proj")
        else:
            loss_row, dx, dxb = _down_proj_loss(act, c['w_down'], x1, target, tm=tm, name=p + "down_proj_loss")

    gs = {n: [None] * DEPTH for n in SMALL}
    for l in reversed(range(DEPTH)):
        p = f"l{l}_b_"
        s = saved[l]
        gb = {}
        dact = _mm(dxb, s['w_down'], mode='nt', tm=tm1, tn=1408, tk=D_MODEL, name=p + "down_dx")
        gb['w_down'] = _mm(s['act'], dxb, mode='tn', tm=1408, tn=D_MODEL, tk=tkt, name=p + "down_dw")
        toks = sched.slot(l, 'down', gb['w_down'])
        dup, dwg, dwv, dbg, dbv = _conv_act_bwd(s['up'], s['conv_w'], s['cb'], dact, tr=min(1024, T), tc=256,
                                                name=p + "conv_act", deps=toks)
        gs['conv_w'][l] = jnp.concatenate([dwg, dwv], axis=1)
        gs['conv_b'][l] = jnp.concatenate([dbg, dbv], axis=1)[0]
        toks = sched.slot(l, 'conv', dup)
        for half in range(2):
            gb['w_up'] = _mm(s['h2'], dup, mode='tn', b_lead=half, tm=D_MODEL, tn=1408, tk=tkt,
                             out_into=gb.get('w_up'), out_joff=2 * half, out_n=2 * D_FF, name=p + f"up_dw{half}",
                             deps=toks if half == 0 else ())
        toks = sched.slot(l, 'ffn', gb['w_up'], gb)
        dx1, dx1b, dg2 = _mm_nt_sharded_rms(dup, s['w_up'], s['x1'], s['g2'], dx, tm=tm,
                                            name=p + "up_dx_rms2", deps=toks)
        gs['norm2'][l] = dg2[0]
        gb['w_out'] = _mm(s['merged'], dx1b, mode='tn', tm=D_MODEL, tn=D_MODEL, tk=tkt, name=p + "out_dw")
        (dz, dpa, dat, dsg, gb['w_proj_a'], gb['w_proj_b'], gb['w_proj_c']) = _out_dx_merge_bwd(
            dx1b, s['w_out'], s['y3'], s['z'], [s['w_proj_a'], s['w_proj_b'], s['w_proj_c']],
            [s['pa'], s['at'], s['sg']], tm=tm1, tn=512, name=p + "out_dx_merge")
        toks = sched.slot(l, 'mid', dz)
        dq, dkc, dkp, dvc, dvp, dsk = _attn_bwd(s['q'], s['k'], s['v'], s['sinks'], dat, name=p + "attn", deps=toks)
        gs['sinks'][l] = dsk[:, 0]
        toks = sched.slot(l, 'attn', dq)
        dz, dgq, dgk, dwbd, dsc = _mixer_ab_bwd(s['z'], cos, sin, s['gq'], s['gk'], seg, dq, dkc, dkp, dvc, dvp,
                                                dpa, s['wbd'], s['scale'], dz, tr=tr, name=p + "qkv_pool", deps=toks)
        gs['q_norm'][l] = dgq[0, :HEAD_DIM]
        gs['k_norm'][l] = dgk[0, :HEAD_DIM]
        gs['w_pool'][l] = jnp.stack([dwbd[g * HEAD_DIM:(g + 1) * HEAD_DIM, g * HEAD_DIM:(g + 1) * HEAD_DIM]
                                     for g in range(4)])
        gs['pool_scale'][l] = dsc[0]
        dz, dws, dbrows, dvn = _sgu_bwd(s['z'], s['wtril'], s['bexp'], s['vn'], seg, dsg, dz, tr=tr, name=p + "sgu")
        gs['w_s'][l] = dws
        gs['b_s'][l] = dbrows[:, ::HEAD_DIM].T
        gs['sgu_v_norm'][l] = dvn[0, :HEAD_DIM]
        gb['w_in'] = _mm(s['h1'], dz, mode='tn', tm=D_MODEL, tn=1152, tk=tkt, name=p + "in_dw")
        toks = sched.slot(l, 'mix', gb['w_in'], gb)
        dx, dxb, dg1 = _mm_nt_sharded_rms(dz, s['w_in'], s['x'], s['g1'], dx1, tm=tm,
                                          name=p + "in_dx_rms1", deps=toks)
        gs['norm1'][l] = dg1[0]
    gs = {n: jnp.stack(v) for n, v in gs.items()}
    return loss_row, dx, gs


GROUP_F = ('w_down', 'w_up')
GROUP_M = ('w_out', 'w_proj_a', 'w_proj_b', 'w_proj_c', 'w_in')
ROW_SHARDED = ('w_out', 'w_down')

REDUCE_PLAN = {
    (1, 'ffn'): (('S1', 'F', 1),),
    (1, 'mid'): (('W1', 'F', 1),),
    (1, 'mix'): (('S1', 'M', 1),),
    (0, 'down'): (('W1', 'M', 1),),
    (0, 'conv'): (('W2', 'F', 1),),
    (0, 'ffn'): (('S1', 'F', 0), ('W3', 'F', 1)),
    (0, 'mid'): (('W1', 'F', 0),),
    (0, 'attn'): (('W2', 'M', 1),),
    (0, 'mix'): (('S1', 'M', 0), ('W3', 'M', 1)),
}
REDUCE_TAIL_A = (('W1', 'M', 0), ('W2', 'F', 0))
REDUCE_TAIL_B = (('W3', 'F', 0),)
REDUCE_TAIL_C = (('W2', 'M', 0), ('W3', 'M', 0))


class _Comm:
    def __init__(self, w, pos):
        self.pos, self.w = pos, w
        self.gathers, self.group_of, self.weights = {}, {}, {}
        self.tokens = []
        self._start_group('a', [('w_in', 0)])
        self._start_group('b', [(n, 0) for n in ('w_proj_a', 'w_proj_b', 'w_proj_c', 'w_out')])
        self._start_group('c', [(n, 0) for n in ('w_up', 'conv_w', 'w_down')])
        self.red = {}
        self.final = {}

    def _start_group(self, g, ks, deps=()):
        lands = [_cast_place(self.w[n], self.pos, F32 if n == 'conv_w' else MXU_DTYPE, layer=l,
                             name=f"gw_place_{n}{l}", deps=deps) for n, l in ks]
        self.gathers[g] = (_Gather(lands, "gw_" + g, deps=self.tokens[-1:], halves=(g == 'a')), ks)
        self.tokens.append(self.gathers[g][0].token)
        self.group_of.update({k: g for k in ks})

    def start_tokens(self):
        return self.tokens[-1:]

    def start_next_layer(self, dep):
        self._start_group('d', [(n, 1) for n in BIG] + [('conv_w', 1)], deps=[dep])
        return self.tokens[-1:]

    def weight(self, name, layer, after):
        if (name, layer) not in self.weights:
            gather, ks = self.gathers[self.group_of[(name, layer)]]
            for (n, l), full in zip(ks, gather.wait(after)):
                if n == 'conv_w' or n.startswith('w_proj'):
                    full = full.transpose(1, 0, 2).reshape(full.shape[1], -1)
                elif n in ROW_SHARDED:
                    full = full.reshape(-1, full.shape[2])
                self.weights[(n, l)] = full
        return self.weights[(name, layer)]

    def slot(self, layer, slot, after, grads=None):
        tokens = []
        for step, grp, lyr in REDUCE_PLAN.get((layer, slot), ()):
            tok = self._step(step, grp, lyr, after, grads)
            if tok is not None:
                tokens.append(tok)
        return tokens

    def tail(self, steps, after, deps=()):
        toks = (self._step(step, grp, lyr, after, None, deps) for step, grp, lyr in steps)
        return [t for t in toks if t is not None]

    def shards(self):
        return {n: f.reshape(DEPTH, 2 * f.shape[2], f.shape[3]) for n, f in self.final.items()}

    def _step(self, step, grp, layer, after, grads, deps=()):
        names = GROUP_F if grp == 'F' else GROUP_M
        tag = f"{grp.lower()}{layer}"
        st = self.red.setdefault((grp, layer), {})
        n = len(names)
        if step == 'S1':
            g4s = []
            for nm in names:
                g = grads[nm]
                R, C = g.shape
                g4s.append(g.reshape(N_CHIPS, 2, R // (2 * N_CHIPS), C) if nm in ROW_SHARDED
                           else g.reshape(1, 2, R // 2, C))
            st['s1'] = _swap_halves_start(g4s, name="rs1_" + tag)
            return st['s1'][2]
        if step == 'W1':
            sems, arrays, _, waits = st.pop('s1')
            arrays = _split_wait(sems, arrays, after, waits, name=f"rs1_{tag}_wait")
            parts = [_pair_add(arrays[i], arrays[n + i], self.pos, name=f"pair_add_{tag}_{names[i]}")
                     for i in range(n)]
            st['s2'] = _scatter_start(parts, name="rs2_" + tag, deps=deps)
            return st['s2'][2]
        if step == 'W2':
            sems, arrays, _, waits = st.pop('s2')
            arrays = _split_wait(sems, arrays, after, waits, name=f"rs2_{tag}_wait")
            fs = [_chip_sum(arrays[i], arrays[n + i], self.final.get(names[i]), self.pos, layer,
                            name=f"chip_sum_{tag}_{names[i]}") for i in range(n)]
            st['s3'] = _pair_share_start(fs, layer, name="rs3_" + tag)
            return st['s3'][2]
        sems, arrays, _, waits = st.pop('s3')
        self.final.update(zip(names, _split_wait(sems, arrays, after, waits, name=f"rs3_{tag}_wait")))
        return None


def _pack(arrays):
    rows = []
    for a in arrays:
        nel = int(np.prod(a.shape))
        if nel % 1024 == 0:
            rows.append(a.astype(F32).reshape(nel // 128, 128))
        else:
            f = a.reshape(-1).astype(F32)
            rows.append(jnp.pad(f, (0, (-nel) % 1024)).reshape(-1, 128))
    return jnp.concatenate(rows, axis=0)


def _unpack(pack, shapes):
    out, row = [], 0
    for shp in shapes:
        nel = int(np.prod(shp))
        nrow = 8 * -(-nel // 1024)
        part = pack[row:row + nrow]
        out.append(part.reshape(shp) if nel % 1024 == 0 else part.reshape(-1)[:nel].reshape(shp))
        row += nrow
    return out


def kernel(x, positions, norm1, w_in, q_norm, k_norm, sinks, w_pool, pool_scale, sgu_v_norm, w_s, b_s, w_proj_a, w_proj_b, w_proj_c, w_out, norm2, w_up, conv_w, conv_b, w_down, loss_target, m_norm1, m_w_in, m_q_norm, m_k_norm, m_sinks, m_w_pool, m_pool_scale, m_sgu_v_norm, m_w_s, m_b_s, m_w_proj_a, m_w_proj_b, m_w_proj_c, m_w_out, m_norm2, m_w_up, m_conv_w, m_conv_b, m_w_down, v_norm1, v_w_in, v_q_norm, v_k_norm, v_sinks, v_w_pool, v_pool_scale, v_sgu_v_norm, v_w_s, v_b_s, v_w_proj_a, v_w_proj_b, v_w_proj_c, v_w_out, v_norm2, v_w_up, v_conv_w, v_conv_b, v_w_down):
    w = dict(norm1=norm1, w_in=w_in, q_norm=q_norm, k_norm=k_norm, sinks=sinks, w_pool=w_pool, pool_scale=pool_scale,
             sgu_v_norm=sgu_v_norm, w_s=w_s, b_s=b_s, w_proj_a=w_proj_a, w_proj_b=w_proj_b, w_proj_c=w_proj_c,
             w_out=w_out, norm2=norm2, w_up=w_up, conv_w=conv_w, conv_b=conv_b, w_down=w_down)
    m = dict(norm1=m_norm1, w_in=m_w_in, q_norm=m_q_norm, k_norm=m_k_norm, sinks=m_sinks, w_pool=m_w_pool,
             pool_scale=m_pool_scale, sgu_v_norm=m_sgu_v_norm, w_s=m_w_s, b_s=m_b_s, w_proj_a=m_w_proj_a,
             w_proj_b=m_w_proj_b, w_proj_c=m_w_proj_c, w_out=m_w_out, norm2=m_norm2, w_up=m_w_up, conv_w=m_conv_w,
             conv_b=m_conv_b, w_down=m_w_down)
    v = dict(norm1=v_norm1, w_in=v_w_in, q_norm=v_q_norm, k_norm=v_k_norm, sinks=v_sinks, w_pool=v_w_pool,
             pool_scale=v_pool_scale, sgu_v_norm=v_sgu_v_norm, w_s=v_w_s, b_s=v_b_s, w_proj_a=v_w_proj_a,
             w_proj_b=v_w_proj_b, w_proj_c=v_w_proj_c, w_out=v_w_out, norm2=v_norm2, w_up=v_w_up, conv_w=v_conv_w,
             conv_b=v_conv_b, w_down=v_w_down)
    chip = 2 * lax.axis_index("x") + lax.axis_index("y")
    core = lax.axis_index("c")

    pos = jnp.stack([chip, core, 2 * chip + core]).astype(jnp.int32)
    comm = _Comm(w, pos)

    cos, sin = _rope_tables(positions[0])
    sp = {n: w[n] for n in SMALL if n != 'conv_w'}
    loss_row, dx, gs = _local_step(x[0], loss_target[0], cos, sin, sp, comm)

    delta, new_m, new_v, grad_out = {}, {}, {}, {}

    def adamw_big(names, grads):
        for n in names:
            shp = w[n].shape
            two_d = lambda a: a.reshape(shp[0] * shp[1], shp[2])
            d, nm, nv, g = _adamw(two_d(w[n]), two_d(grads[n]), two_d(m[n]), two_d(v[n]),
                                  tr=_row_tile(shp[0] * shp[1], 256), name=f"adamw_{n}", copy_g=True)
            delta[n], new_m[n], new_v[n], grad_out[n] = d.reshape(shp), nm.reshape(shp), nv.reshape(shp), g.reshape(shp)

    small_shapes = [gs[n].shape for n in SMALL] + [(1,)]
    small_pack = _pack([gs[n] for n in SMALL] + [loss_row[0, :1]])
    small = _Gather([_cast_place(small_pack, pos, F32, slots=N_DEV, which=2, name="small_place")], "small_gather",
                    all_devices=True)
    toks = comm.tail(REDUCE_TAIL_A[:1], (dx, small.token))
    comm.tail(REDUCE_TAIL_A[1:], (dx, *toks))
    comm.tail(REDUCE_TAIL_B, dx)
    adamw_big(GROUP_F, comm.shards())
    red = _sum_slots(small.wait(new_v[GROUP_F[-1]])[0], tr=small_pack.shape[0], name="small_sum")
    *small_grads, loss = _unpack(red, small_shapes)
    g_small = dict(zip(SMALL, small_grads))
    comm.tail(REDUCE_TAIL_C, red)
    grads = comm.shards()
    grads.update(g_small)
    shard_cols = conv_w.shape[2]
    grads['conv_w'] = lax.dynamic_slice_in_dim(g_small['conv_w'], chip * shard_cols, shard_cols, axis=2)

    adamw_big(GROUP_M, grads)
    shapes = [w[n].shape for n in SMALL]
    packs = [_pack([src[n] for n in SMALL]) for src in (w, grads, m, v)]
    d, nm, nv = _adamw(*packs, tr=packs[0].shape[0], name="adamw_small")
    for dst, src in ((delta, d), (new_m, nm), (new_v, nv)):
        dst.update(zip(SMALL, _unpack(src, shapes)))

    grads.update(grad_out)
    return (loss[0], dx[None], *[grads[n] for n in WEIGHTS], *[delta[n] for n in WEIGHTS],
            *[new_m[n] for n in WEIGHTS], *[new_v[n] for n in WEIGHTS])
```

```python
import functools
import math

import numpy as np
import jax
import jax.numpy as jnp
from jax import lax
from jax.experimental import pallas as pl
from jax.experimental.pallas import tpu as pltpu

F32 = jnp.float32
MXU_DTYPE = jnp.bfloat16
COMM_DTYPE = jnp.bfloat16
ACT_DTYPE = jnp.bfloat16
HALO = 16

D_MODEL = 1024
DEPTH = 2
HEAD_DIM = 64
POOL_WINDOWS = (2, 4, 8, 16)
POOL_WIDTH = 256
N_Q_HEADS = 8
ATTN_BLOCK = 128
ATTN_WIDTH = 512
KV_WIDTH = 128
CHUNK = 128
SGU_WIDTH = 256
IN_COLS = 4608
GATE_COL0 = 1536
D_FF = 2816
ROPE_THETA = 10000.0
EPS = 1e-6
ADAM_LR, ADAM_B1, ADAM_B2, ADAM_EPS, ADAM_WD, ADAM_STEP = 0.001, 0.9, 0.999, 1e-08, 0.01, 10

N_CHIPS = 4
N_DEV = 8
VMEM_LIMIT_BYTES = 56 * 1024 * 1024
NEG_BIG = -1e30
MESH = pl.DeviceIdType.MESH
ANY = pl.BlockSpec(memory_space=pl.ANY)

SDS = jax.ShapeDtypeStruct


def _cp(*sem):
    return pltpu.CompilerParams(dimension_semantics=sem, vmem_limit_bytes=VMEM_LIMIT_BYTES)


def _dot(a, b, dims=((1,), (0,))):
    return lax.dot_general(a.astype(MXU_DTYPE), b.astype(MXU_DTYPE), (dims, ((), ())),
                           preferred_element_type=F32)


NT = ((1,), (1,))
TN = ((0,), (0,))


def _split_dot(x, m):
    hi = x.astype(MXU_DTYPE)
    lo = (x - hi.astype(F32)).astype(MXU_DTYPE)
    return _dot(hi, m) + _dot(lo, m)


def _seg_matrix(width, seg):
    idx = np.arange(width) // seg
    return jnp.asarray((idx[:, None] == idx[None, :]).astype(np.float32), dtype=MXU_DTYPE)


def _lane(shape):
    return lax.broadcasted_iota(jnp.int32, shape, len(shape) - 1)


def _row(shape):
    return lax.broadcasted_iota(jnp.int32, shape, 0)


def _full(shape):
    nd = len(shape)
    return pl.BlockSpec(shape, lambda *_: (0,) * nd)


def _gelu(x):
    k = math.sqrt(2.0 / math.pi)
    th = jnp.tanh(k * (x + 0.044715 * (x * x * x)))
    return 0.5 * x * (1.0 + th)


def _gelu_and_grad(x):
    k = math.sqrt(2.0 / math.pi)
    x2 = x * x
    th = jnp.tanh(k * (x + 0.044715 * (x2 * x)))
    g = 0.5 * x * (1.0 + th)
    dg = 0.5 * (1.0 + th) + 0.5 * x * (1.0 - th * th) * (k * (1.0 + 3.0 * 0.044715 * x2))
    return g, dg


def _sigmoid(x):
    return 0.5 * jnp.tanh(0.5 * x) + 0.5


def _swap_halves(x):
    w = x.shape[-1]
    first = (_lane(x.shape) % HEAD_DIM) < (HEAD_DIM // 2)
    return jnp.where(first, pltpu.roll(x, w - HEAD_DIM // 2, 1), pltpu.roll(x, HEAD_DIM // 2, 1))


def _tile_lanes(x, reps):
    return x if reps == 1 else jnp.concatenate([x] * reps, axis=1)


def _fold_lanes(x, period):
    w = x.shape[-1]
    while w > period:
        w //= 2
        x = x + pltpu.roll(x, w, 1)
    return x


def _mm(a, b, *, mode, tm, tn, tk, out_dtype=F32, add=None, name,
        a_lead=None, b_lead=None, b_sharded=False, out_into=None,
        b_koff=0, out_joff=0, out_n=None, deps=()):
    ash = a.shape[1:] if a_lead is not None else a.shape
    bsh = b.shape[1:] if b_lead is not None else b.shape
    if b_sharded:
        bsh = (b.shape[1], N_CHIPS * b.shape[2])
    if mode == 'nn':
        (M, K), (K2, N) = ash, bsh
    elif mode == 'nt':
        (M, K), (N, K2) = ash, bsh
    else:
        (K, M), (K2, N) = ash, bsh
    assert K == K2 or (mode == 'nt' and K2 > K), (ash, bsh, mode)
    assert M % tm == 0 and N % tn == 0 and K % tk == 0, (M, N, K, tm, tn, tk)
    nk = K // tk
    dims = {'nn': ((1,), (0,)), 'nt': NT, 'tn': TN}[mode]

    def lead(spec_shape, imap, lead_idx):
        if lead_idx is None:
            return pl.BlockSpec(spec_shape, imap)
        return pl.BlockSpec((None,) + spec_shape, lambda i, j, k: (lead_idx,) + imap(i, j, k))

    if mode == 'tn':
        a_spec = lead((tk, tm), lambda i, j, k: (k, i), a_lead)
    else:
        a_spec = lead((tm, tk), lambda i, j, k: (i, k), a_lead)
    if b_sharded:
        per = b.shape[2] // (tk if mode == 'nt' else tn)
        assert per * (tk if mode == 'nt' else tn) == b.shape[2] and mode != 'tn'
        if mode == 'nt':
            b_spec = pl.BlockSpec((None, tn, tk), lambda i, j, k: ((k + b_koff) // per, j, (k + b_koff) % per))
        else:
            b_spec = pl.BlockSpec((None, tk, tn), lambda i, j, k: (j // per, k, j % per))
    elif mode == 'nt':
        b_spec = lead((tn, tk), lambda i, j, k: (j, k + b_koff), b_lead)
    else:
        b_spec = lead((tk, tn), lambda i, j, k: (k, j), b_lead)
    o_spec = pl.BlockSpec((tm, tn), lambda i, j, k: (i, j + out_joff))
    n_out = N if out_n is None else out_n
    in_specs = [a_spec, b_spec]
    operands = [a, b]
    if add is not None:
        in_specs.append(pl.BlockSpec((tm, tn), lambda i, j, k: (i, j)))
        operands.append(add)
    aliases = {}
    if out_into is not None:
        in_specs.append(ANY)
        operands.append(out_into)
        aliases = {len(operands) - 1: 0}
    in_specs += [ANY] * len(deps)
    operands += list(deps)
    has_add = add is not None
    acc_in_out = nk > 1 and out_dtype == F32

    def body(*refs):
        a_ref, b_ref = refs[0], refs[1]
        pos = 2
        add_ref = None
        if has_add:
            add_ref = refs[pos]
            pos += 1
        if out_into is not None:
            pos += 1
        pos += len(deps)
        o_ref = refs[pos]
        acc_ref = refs[pos + 1] if (nk > 1 and not acc_in_out) else None
        p = _dot(a_ref[...], b_ref[...], dims)
        if nk == 1:
            if has_add:
                p = p + add_ref[...]
            o_ref[...] = p.astype(o_ref.dtype)
            return
        k = pl.program_id(2)
        tgt = o_ref if acc_in_out else acc_ref

        @pl.when(k == 0)
        def _():
            tgt[...] = p + add_ref[...] if has_add else p

        @pl.when(k > 0)
        def _():
            tgt[...] += p

        if not acc_in_out:
            @pl.when(k == nk - 1)
            def _():
                o_ref[...] = acc_ref[...].astype(o_ref.dtype)

    out_shape = SDS((M, n_out), out_dtype)
    scratch = [pltpu.VMEM((tm, tn), F32)] if (nk > 1 and not acc_in_out) else []
    return pl.pallas_call(
        body, grid=(M // tm, N // tn, nk), in_specs=in_specs, out_specs=o_spec, out_shape=out_shape,
        scratch_shapes=scratch, input_output_aliases=aliases, name=name,
        compiler_params=_cp("parallel", "parallel", "arbitrary"))(*operands)


def _rms_bwd_rows(xv, g, dh, dres):
    r = lax.rsqrt(jnp.mean(xv * xv, axis=-1, keepdims=True) + EPS)
    xh = xv * r
    gy = dh * g
    dx = r * (gy - xh * jnp.mean(xh * gy, axis=-1, keepdims=True)) + dres
    return dx, jnp.sum(dh * xh, axis=0, keepdims=True)


def _mm_nt_sharded_rms(a, b, x, g, dres, *, tm, name, deps=()):
    a3 = a if a.ndim == 3 else a[None]
    A, M, ka = a3.shape
    S, N, ns = b.shape
    per = S // A
    assert ka == per * ns and M % tm == 0 and N == x.shape[1], (a3.shape, b.shape, x.shape)

    def body(a_ref, b_ref, x_ref, g_ref, dres_ref, dx_ref, dxb_ref, dg_ref):
        acc = None
        for s in range(S):
            lo = (s % per) * ns
            p = _dot(a_ref[s // per, :, lo:lo + ns], b_ref[s], NT)
            acc = p if acc is None else acc + p
        dx, dg = _rms_bwd_rows(x_ref[...], g_ref[...], acc, dres_ref[...])
        dx_ref[...] = dx
        dxb_ref[...] = dx.astype(dxb_ref.dtype)

        @pl.when(pl.program_id(0) == 0)
        def _():
            dg_ref[...] = jnp.zeros_like(dg_ref)
        dg_ref[...] += dg

    rows = pl.BlockSpec((tm, N), lambda i: (i, 0))
    return pl.pallas_call(
        _after(body, 5, deps), grid=(M // tm,),
        in_specs=[pl.BlockSpec((A, tm, ka), lambda i: (0, i, 0)),
                  pl.BlockSpec((S, N, ns), lambda i: (0, 0, 0), pipeline_mode=pl.Buffered(1)),
                  rows, _full((1, N)), rows] + [ANY] * len(deps),
        out_specs=[rows, rows, _full((1, N))],
        out_shape=[SDS((M, N), F32), SDS((M, N), MXU_DTYPE), SDS((1, N), F32)], name=name,
        compiler_params=_cp("arbitrary"))(a3, b, x, g, dres, *deps)


def _norm_mm(x, g, b, *, tm, tn, name, deps=()):
    M, K = x.shape
    S, K2, ns = b.shape
    per = ns // tn
    assert K == K2 and per * tn == ns and M % tm == 0, (x.shape, b.shape)

    def body(x_ref, g_ref, b_ref, o_ref, h_ref):
        @pl.when(pl.program_id(1) == 0)
        def _():
            xv = x_ref[...]
            r = lax.rsqrt(jnp.mean(xv * xv, axis=-1, keepdims=True) + EPS)
            h_ref[...] = (xv * r * g_ref[...]).astype(h_ref.dtype)
        o_ref[...] = _dot(h_ref[...], b_ref[...]).astype(o_ref.dtype)

    return pl.pallas_call(
        _after(body, 3, deps), grid=(M // tm, S * per),
        in_specs=[pl.BlockSpec((tm, K), lambda i, j: (i, 0)), _full((1, K)),
                  pl.BlockSpec((None, K, tn), lambda i, j: (j // per, 0, j % per))] + [ANY] * len(deps),
        out_specs=[pl.BlockSpec((tm, tn), lambda i, j: (i, j)), pl.BlockSpec((tm, K), lambda i, j: (i, 0))],
        out_shape=[SDS((M, S * ns), ACT_DTYPE), SDS((M, K), MXU_DTYPE)], name=name,
        compiler_params=_cp("parallel", "arbitrary"))(x, g, b, *deps)


def _after(body, n_in, deps):
    nd = len(deps)
    if nd == 0:
        return body
    return lambda *refs: body(*refs[:n_in], *refs[n_in + nd:])


def _down_proj_loss(act, w, x1, target, *, tm, name):
    T, K = act.shape
    D = w.shape[1]

    def body(a_ref, w_ref, x_ref, t_ref, loss_ref, dy_ref, dyb_ref):
        i = pl.program_id(0)
        d = (x_ref[...] + _dot(a_ref[...], w_ref[...])) - t_ref[...]
        dy = d * (1.0 / D)
        dy_ref[...] = dy
        dyb_ref[...] = dy.astype(dyb_ref.dtype)
        part = jnp.sum(jnp.sum(d * d, axis=1, keepdims=True), axis=0, keepdims=True) * (0.5 / D)

        @pl.when(i == 0)
        def _():
            loss_ref[...] = jnp.zeros_like(loss_ref)
        loss_ref[...] += jnp.broadcast_to(part, loss_ref.shape)

    rows = pl.BlockSpec((tm, D), lambda i: (i, 0))
    return pl.pallas_call(
        body, grid=(T // tm,), in_specs=[pl.BlockSpec((tm, K), lambda i: (i, 0)), _full((K, D)), rows, rows],
        out_specs=[_full((1, 128)), rows, rows],
        out_shape=[SDS((1, 128), F32), SDS((T, D), F32), SDS((T, D), MXU_DTYPE)],
        name=name, compiler_params=_cp("arbitrary"))(act, w, x1, target)


def _pool_lane_consts(shape):
    lane = _lane(shape)
    grp = lane // (POOL_WIDTH // 4)
    win = jnp.where(grp == 0, 2, jnp.where(grp == 1, 4, jnp.where(grp == 2, 8, 16)))
    return grp, win


def _pool_select(grp, s2, s4, s8, s16):
    return jnp.where(grp == 0, s2, jnp.where(grp == 1, s4, jnp.where(grp == 2, s8, s16)))


def _pool_diff(xe, row0, tr):
    s2 = xe + pltpu.roll(xe, 1, 0)
    s4 = s2 + pltpu.roll(s2, 2, 0)
    s8 = s4 + pltpu.roll(s4, 4, 0)
    s16 = s8 + pltpu.roll(s8, 8, 0)
    shape = (tr, POOL_WIDTH)
    grp, win = _pool_lane_consts(shape)
    sums = _pool_select(grp, s2[16:], s4[16:], s8[16:], s16[16:])
    t = row0 + _row(shape)
    cnt = jnp.minimum(t + 1, win).astype(F32)
    return sums / cnt - xe[16:]


def _pool_fwd(z, wbd, scale, *, tr, name):
    T = z.shape[0]
    hb = tr // 16

    def body(x_ref, xp_ref, w_ref, s_ref, o_ref):
        i = pl.program_id(0)
        halo = jnp.where(i == 0, 0.0, xp_ref[...].astype(F32))
        diff = _pool_diff(jnp.concatenate([halo, x_ref[...].astype(F32)], axis=0), i * tr, tr)
        o_ref[...] = (_dot(diff, w_ref[...]) * s_ref[...]).astype(o_ref.dtype)

    return pl.pallas_call(
        body, grid=(T // tr,),
        in_specs=[pl.BlockSpec((tr, POOL_WIDTH), lambda i: (i, 0)),
                  pl.BlockSpec((16, POOL_WIDTH), lambda i: (jnp.maximum(i * hb - 1, 0), 0)),
                  _full((POOL_WIDTH, POOL_WIDTH)), _full((1, POOL_WIDTH))],
        out_specs=pl.BlockSpec((tr, POOL_WIDTH), lambda i: (i, 0)),
        out_shape=SDS((T, POOL_WIDTH), MXU_DTYPE), name=name, compiler_params=_cp("parallel"))(z, z, wbd, scale)


def _pool_bwd_tile(i, n, tr, x, xprev, dpa, dpa_next, wbd, scale):
    halo = jnp.where(i == 0, 0.0, xprev)
    diff = _pool_diff(jnp.concatenate([halo, x], axis=0), i * tr, tr)
    mixed = _dot(diff, wbd)
    dscale = jnp.sum(dpa * mixed, axis=0, keepdims=True)
    dnext = jnp.where(i == n - 1, 0.0, dpa_next)
    dmix_e = jnp.concatenate([dpa, dnext], axis=0) * scale
    ddiff_e = _dot(dmix_e, wbd, NT)
    dwbd = _dot(diff, dmix_e[:tr], TN)
    shape = (tr + 16, POOL_WIDTH)
    grp, win = _pool_lane_consts(shape)
    t = i * tr + _row(shape)
    e = ddiff_e / jnp.minimum(t + 1, win).astype(F32)
    nrow = tr + 16
    a2 = e + pltpu.roll(e, nrow - 1, 0)
    a4 = a2 + pltpu.roll(a2, nrow - 2, 0)
    a8 = a4 + pltpu.roll(a4, nrow - 4, 0)
    a16 = a8 + pltpu.roll(a8, nrow - 8, 0)
    dx = _pool_select(grp, a2, a4, a8, a16)[:tr] - ddiff_e[:tr]
    return dx, dwbd, dscale


def _norm_rope(x, g, cos, sin_signed, seg):
    reps = x.shape[1] // 128
    ms = _split_dot(x * x, seg) * (1.0 / HEAD_DIM)
    r = lax.rsqrt(ms + EPS)
    xn = x * r * g
    c, s = _tile_lanes(cos, reps), _tile_lanes(sin_signed, reps)
    return xn * c + _swap_halves(xn) * s


def _norm_rope_bwd(x, g, cos, sin_signed, seg, dout):
    reps = x.shape[1] // 128
    c, s = _tile_lanes(cos, reps), _tile_lanes(sin_signed, reps)
    dxn = dout * c + _swap_halves(dout * s)
    ms = _split_dot(x * x, seg) * (1.0 / HEAD_DIM)
    r = lax.rsqrt(ms + EPS)
    xh = x * r
    gy = dxn * g
    dx = r * (gy - xh * (_split_dot(xh * gy, seg) * (1.0 / HEAD_DIM)))
    dg = jnp.sum(dxn * xh, axis=0, keepdims=True)
    return dx, dg


def _dup_heads(k):
    first = _lane(k.shape) < HEAD_DIM
    kr = pltpu.roll(k, HEAD_DIM, 1)
    return jnp.concatenate([jnp.where(first, k, kr), jnp.where(first, kr, k)], axis=1)


def _qkv_prep(z, cos, sin_signed, gq, gk, seg, *, tr, name):
    T = z.shape[0]

    def body(qa_ref, qb_ref, kv_ref, c_ref, s_ref, gq_ref, gk_ref, seg_ref, q_ref, k_ref, v_ref):
        c, s, seg_m = c_ref[...], s_ref[...], seg_ref[...]
        scale = HEAD_DIM ** -0.5
        qa = _norm_rope(qa_ref[...].astype(F32), gq_ref[...], c, s, seg_m) * scale
        qb = _norm_rope(qb_ref[...].astype(F32), gq_ref[...], c, s, seg_m) * scale
        q_ref[...] = jnp.concatenate([qa, qb], axis=1).astype(q_ref.dtype)
        kv = kv_ref[...].astype(F32)
        k = _norm_rope(kv[:, :KV_WIDTH], gk_ref[...], c, s, seg_m[:128, :128])
        k_ref[...] = _dup_heads(k).astype(k_ref.dtype)
        v_ref[...] = _dup_heads(kv[:, KV_WIDTH:]).astype(v_ref.dtype)

    col = lambda j: pl.BlockSpec((tr, 256), lambda i: (i, j))
    tab = pl.BlockSpec((tr, 128), lambda i: (i, 0))
    return pl.pallas_call(
        body, grid=(T // tr,),
        in_specs=[col(1), col(2), col(3), tab, tab, _full((1, 256)), _full((1, 128)), _full((256, 256))],
        out_specs=[pl.BlockSpec((tr, 512), lambda i: (i, 0)), col(0), col(0)],
        out_shape=[SDS((T, 512), MXU_DTYPE), SDS((T, 256), MXU_DTYPE), SDS((T, 256), MXU_DTYPE)],
        name=name, compiler_params=_cp("parallel"))(z, z, z, cos, sin_signed, gq, gk, seg)


GROUP_HEADS = 4
GROUP_ROWS = GROUP_HEADS * ATTN_BLOCK
ALL_ROWS = N_Q_HEADS * ATTN_BLOCK


def _attn_mask(has_prev):
    qi = _row((ALL_ROWS, 2 * ATTN_BLOCK)) % ATTN_BLOCK
    kj = _lane((ALL_ROWS, 2 * ATTN_BLOCK))
    return (kj > qi) & (kj <= qi + ATTN_BLOCK) & ((kj >= ATTN_BLOCK) | has_prev)


FWD_STEP_BLOCKS = 8
BWD_STEP_BLOCKS = 2


def _band(prev, cur, blk):
    lo = cur[(blk - 1) * ATTN_BLOCK:blk * ATTN_BLOCK] if blk else prev
    return jnp.concatenate([lo, cur[blk * ATTN_BLOCK:(blk + 1) * ATTN_BLOCK]], axis=0)


def _stack_heads(x, g):
    first = _lane((ATTN_BLOCK, 128)) < HEAD_DIM
    parts = []
    for pair in (2 * g, 2 * g + 1):
        x128 = x[:, 128 * pair:128 * (pair + 1)]
        zero = jnp.zeros_like(x128)
        parts += [jnp.where(first, x128, zero), jnp.where(first, zero, x128)]
    return jnp.concatenate(parts, axis=0)


def _unstack_heads(y):
    first = _lane((ATTN_BLOCK, 128)) < HEAD_DIM
    b = ATTN_BLOCK
    return jnp.concatenate([jnp.where(first, y[0:b], y[b:2 * b]), jnp.where(first, y[2 * b:3 * b], y[3 * b:4 * b])],
                           axis=1)


def _sink_col(sk_ref):
    return jnp.concatenate([jnp.broadcast_to(sk_ref[h:h + 1, 0:1], (ATTN_BLOCK, 1)) for h in range(N_Q_HEADS)],
                           axis=0)


def _by_group(a8, b2, dims=((1,), (0,))):
    return jnp.concatenate([_dot(a8[:GROUP_ROWS], b2[:, :128], dims), _dot(a8[GROUP_ROWS:], b2[:, 128:], dims)],
                           axis=0)


def _softmax_exp(q8, k2, mask, sink):
    s = jnp.where(mask, _by_group(q8, k2, NT), NEG_BIG)
    m = jnp.maximum(jnp.max(s, axis=1, keepdims=True), sink)
    p = jnp.exp(s - m)
    ps = jnp.exp(sink - m)
    return p, ps, 1.0 / (jnp.sum(p, axis=1, keepdims=True) + ps)


def _attn_fwd(q, k, v, sinks_b, *, name):
    T = q.shape[0]
    nb = T // ATTN_BLOCK
    STEP_BLOCKS = min(FWD_STEP_BLOCKS, nb)
    STEP_ROWS = STEP_BLOCKS * ATTN_BLOCK

    def body(q_ref, kc_ref, kp_ref, vc_ref, vp_ref, sk_ref, o_ref):
        n = pl.program_id(0)
        kc, kp, vc, vp = kc_ref[...], kp_ref[...], vc_ref[...], vp_ref[...]
        sink = _sink_col(sk_ref)
        for blk in range(STEP_BLOCKS):
            rows = slice(blk * ATTN_BLOCK, (blk + 1) * ATTN_BLOCK)
            mask = _attn_mask((n > 0) if blk == 0 else True)
            k2, v2 = _band(kp, kc, blk), _band(vp, vc, blk)
            qv = q_ref[rows, :]
            q8 = jnp.concatenate([_stack_heads(qv, 0), _stack_heads(qv, 1)], axis=0)
            p, _, inv = _softmax_exp(q8, k2, mask, sink)
            o8 = _by_group(p, v2) * inv
            o_ref[rows, :] = jnp.concatenate([_unstack_heads(o8[:GROUP_ROWS]), _unstack_heads(o8[GROUP_ROWS:])],
                                             axis=1).astype(o_ref.dtype)

    cur = lambda w: pl.BlockSpec((STEP_ROWS, w), lambda n: (n, 0))
    prev = lambda w: pl.BlockSpec((ATTN_BLOCK, w), lambda n: (jnp.maximum(STEP_BLOCKS * n - 1, 0), 0))
    return pl.pallas_call(
        body, grid=(nb // STEP_BLOCKS,),
        in_specs=[cur(512), cur(256), prev(256), cur(256), prev(256), _full((8, 128))],
        out_specs=cur(512), out_shape=SDS((T, 512), MXU_DTYPE), name=name,
        compiler_params=_cp("parallel"))(q, k, k, v, v, sinks_b)


def _attn_bwd(q, k, v, sinks_b, do, *, name, deps=()):
    T = q.shape[0]
    nb = T // ATTN_BLOCK
    STEP_BLOCKS = min(BWD_STEP_BLOCKS, nb)
    STEP_ROWS = STEP_BLOCKS * ATTN_BLOCK

    def body(q_ref, kc_ref, kp_ref, vc_ref, vp_ref, sk_ref, do_ref,
             dq_ref, dkc_ref, dkp_ref, dvc_ref, dvp_ref, dsk_ref):
        n = pl.program_id(0)
        kc, kp, vc, vp = kc_ref[...], kp_ref[...], vc_ref[...], vp_ref[...]
        sink = _sink_col(sk_ref)

        @pl.when(n == 0)
        def _():
            dsk_ref[...] = jnp.zeros_like(dsk_ref)

        for blk in range(STEP_BLOCKS):
            rows = slice(blk * ATTN_BLOCK, (blk + 1) * ATTN_BLOCK)
            mask = _attn_mask((n > 0) if blk == 0 else True)
            k2, v2 = _band(kp, kc, blk), _band(vp, vc, blk)
            qv, dov = q_ref[rows, :], do_ref[rows, :]
            q8 = jnp.concatenate([_stack_heads(qv, 0), _stack_heads(qv, 1)], axis=0)
            do8 = jnp.concatenate([_stack_heads(dov, 0), _stack_heads(dov, 1)], axis=0)
            p, ps, inv = _softmax_exp(q8, k2, mask, sink)
            pn = p * inv
            delta = jnp.sum(do8 * _by_group(pn, v2), axis=1, keepdims=True)
            ds = pn * (_by_group(do8, v2, NT) - delta)
            dq8 = _by_group(ds, k2)
            dq_ref[rows, :] = jnp.concatenate([_unstack_heads(dq8[:GROUP_ROWS]), _unstack_heads(dq8[GROUP_ROWS:])],
                                              axis=1)
            dk = jnp.concatenate([_dot(ds[:GROUP_ROWS], q8[:GROUP_ROWS], TN),
                                  _dot(ds[GROUP_ROWS:], q8[GROUP_ROWS:], TN)], axis=1)
            dv = jnp.concatenate([_dot(pn[:GROUP_ROWS], do8[:GROUP_ROWS], TN),
                                  _dot(pn[GROUP_ROWS:], do8[GROUP_ROWS:], TN)], axis=1)
            wsink = (ps * inv) * delta
            for h in range(N_Q_HEADS):
                dsink = -jnp.sum(wsink[ATTN_BLOCK * h:ATTN_BLOCK * (h + 1)], axis=0, keepdims=True)
                dsk_ref[h:h + 1, :] += jnp.broadcast_to(dsink, (1, 128))
            dkp_ref[rows, :] = dk[:ATTN_BLOCK]
            dkc_ref[rows, :] = dk[ATTN_BLOCK:]
            dvp_ref[rows, :] = dv[:ATTN_BLOCK]
            dvc_ref[rows, :] = dv[ATTN_BLOCK:]

    cur = lambda w: pl.BlockSpec((STEP_ROWS, w), lambda n: (n, 0))
    prev = lambda w: pl.BlockSpec((ATTN_BLOCK, w), lambda n: (jnp.maximum(STEP_BLOCKS * n - 1, 0), 0))
    f = lambda w: SDS((T, w), F32)
    return pl.pallas_call(
        _after(body, 7, deps), grid=(nb // STEP_BLOCKS,),
        in_specs=[cur(512), cur(256), prev(256), cur(256), prev(256), _full((8, 128)), cur(512)] + [ANY] * len(deps),
        out_specs=[cur(512), cur(256), cur(256), cur(256), cur(256), _full((8, 128))],
        out_shape=[f(512), f(256), f(256), f(256), f(256), SDS((8, 128), F32)],
        name=name, compiler_params=_cp("arbitrary"))(q, k, k, v, v, sinks_b, do, *deps)


def _mixer_ab_bwd(z, cos, sin_signed, gq, gk, seg, dq, dkc, dkp, dvc, dvp, dpa, wbd, scale, dz, *, tr, name, deps=()):
    T = z.shape[0]
    n = T // tr
    hb = tr // 16
    ab = tr // ATTN_BLOCK

    def unfold(cur, nxt_tile, nxt_halo, i):
        nxt = jnp.concatenate([nxt_tile[ATTN_BLOCK:], jnp.where(i == n - 1, 0.0, nxt_halo)], axis=0)
        tot = cur + nxt
        first = _lane((tr, 128)) < HEAD_DIM
        a = tot[:, :128]
        b = tot[:, 128:]
        a = a + pltpu.roll(a, HEAD_DIM, 1)
        b = b + pltpu.roll(b, HEAD_DIM, 1)
        return jnp.where(first, a, b)

    def body(xp_ref, xpp_ref, qa_ref, qb_ref, kv_ref, c_ref, s_ref, gq_ref, gk_ref, seg_ref,
             dq_ref, dkc_ref, dkp_ref, dkh_ref, dvc_ref, dvp_ref, dvh_ref, dpa_ref, dpan_ref, w_ref, sc_ref, _dz_in,
             dz_ref, dgq_ref, dgk_ref, dw_ref, dsc_ref):
        i = pl.program_id(0)
        c, s, seg_m = c_ref[...], s_ref[...], seg_ref[...]
        scale_q = HEAD_DIM ** -0.5
        dqv = dq_ref[...] * scale_q
        dxa, dga = _norm_rope_bwd(qa_ref[...].astype(F32), gq_ref[...], c, s, seg_m, dqv[:, :256])
        dxb, dgb = _norm_rope_bwd(qb_ref[...].astype(F32), gq_ref[...], c, s, seg_m, dqv[:, 256:])
        dk = unfold(dkc_ref[...], dkp_ref[...], dkh_ref[...], i)
        dv = unfold(dvc_ref[...], dvp_ref[...], dvh_ref[...], i)
        kv = kv_ref[...].astype(F32)
        dxk, dgk = _norm_rope_bwd(kv[:, :KV_WIDTH], gk_ref[...], c, s, seg_m[:128, :128], dk)
        dxp, dwbd, dscale = _pool_bwd_tile(i, n, tr, xp_ref[...].astype(F32), xpp_ref[...].astype(F32),
                                           dpa_ref[...], dpan_ref[...],
                                           w_ref[...], sc_ref[...])
        dz_ref[...] = jnp.concatenate([dxp, dxa, dxb, dxk, dv], axis=1).astype(dz_ref.dtype)

        @pl.when(i == 0)
        def _():
            dgq_ref[...] = jnp.zeros_like(dgq_ref)
            dgk_ref[...] = jnp.zeros_like(dgk_ref)
            dw_ref[...] = jnp.zeros_like(dw_ref)
            dsc_ref[...] = jnp.zeros_like(dsc_ref)
        dgq_ref[...] += _fold_lanes(dga + dgb, HEAD_DIM)
        dgk_ref[...] += _fold_lanes(dgk, HEAD_DIM)
        dw_ref[...] += dwbd
        dsc_ref[...] += dscale

    col = lambda j: pl.BlockSpec((tr, 256), lambda i: (i, j))
    rows = lambda w: pl.BlockSpec((tr, w), lambda i: (i, 0))
    nxt_blk = pl.BlockSpec((ATTN_BLOCK, 256), lambda i: (jnp.minimum((i + 1) * ab, T // ATTN_BLOCK - 1), 0))
    prev16 = pl.BlockSpec((16, 256), lambda i: (jnp.maximum(i * hb - 1, 0), 0))
    next16 = pl.BlockSpec((16, 256), lambda i: (jnp.minimum((i + 1) * hb, T // 16 - 1), 0))
    return pl.pallas_call(
        _after(body, 22, deps), grid=(n,),
        in_specs=[col(0), prev16, col(1), col(2), col(3), rows(128), rows(128),
                  _full((1, 256)), _full((1, 128)), _full((256, 256)),
                  rows(512), rows(256), rows(256), nxt_blk, rows(256), rows(256), nxt_blk,
                  rows(256), next16, _full((256, 256)), _full((1, 256)), ANY] + [ANY] * len(deps),
        out_specs=[rows(1024), _full((1, 256)), _full((1, 128)), _full((256, 256)), _full((1, 256))],
        out_shape=[SDS((T, IN_COLS), MXU_DTYPE), SDS((1, 256), F32), SDS((1, 128), F32),
                   SDS((256, 256), F32), SDS((1, 256), F32)],
        input_output_aliases={21: 0}, name=name, compiler_params=_cp("arbitrary"))(
            z, z, z, z, z, cos, sin_signed, gq, gk, seg, dq, dkc, dkp, dkp, dvc, dvp, dvp, dpa, dpa, wbd, scale, dz,
            *deps)


def _sgu_common(zu, zv, vn, seg):
    u, du = _gelu_and_grad(zu)
    gv, dgv = _gelu_and_grad(zv)
    ms = _split_dot(gv * gv, seg) * (1.0 / HEAD_DIM)
    r = lax.rsqrt(ms + EPS)
    xh = gv * r
    return u, du, dgv, r, xh, xh * vn


def _sgu_fwd(z, wtril, bexp, vn, seg, *, tr, name):
    T = z.shape[0]
    nch = tr // CHUNK

    def body(u_ref, v_ref, w_ref, b_ref, vn_ref, seg_ref, o_ref):
        u, _, _, _, _, vg = _sgu_common(u_ref[...].astype(F32), v_ref[...].astype(F32), vn_ref[...], seg_ref[...])
        grp = _lane((CHUNK, SGU_WIDTH)) // HEAD_DIM
        outs = []
        for ch in range(nch):
            vc = vg[ch * CHUNK:(ch + 1) * CHUNK]
            s = b_ref[...]
            for g in range(4):
                s = s + jnp.where(grp == g, _dot(w_ref[g], vc), 0.0)
            outs.append(u[ch * CHUNK:(ch + 1) * CHUNK] * s)
        o_ref[...] = jnp.concatenate(outs, axis=0).astype(o_ref.dtype)

    col = lambda j: pl.BlockSpec((tr, 256), lambda i: (i, j))
    return pl.pallas_call(
        body, grid=(T // tr,),
        in_specs=[col(4), col(5), _full((4, CHUNK, CHUNK)), _full((CHUNK, 256)), _full((1, 256)), _full((256, 256))],
        out_specs=col(0), out_shape=SDS((T, SGU_WIDTH), MXU_DTYPE), name=name,
        compiler_params=_cp("parallel"))(z, z, wtril, bexp, vn, seg)


def _sgu_bwd(z, wtril, bexp, vn, seg, dsg, dz, *, tr, name):
    T = z.shape[0]
    nch = tr // CHUNK

    def body(u_ref, v_ref, w_ref, b_ref, vn_ref, seg_ref, d_ref, _dz_in, dz_ref, dw_ref, db_ref, dvn_ref):
        i = pl.program_id(0)
        seg_m = seg_ref[...]
        vn_v = vn_ref[...]
        u, du, dgv, r, xh, vg = _sgu_common(u_ref[...].astype(F32), v_ref[...].astype(F32), vn_v, seg_m)
        d = d_ref[...]
        grp = _lane((CHUNK, SGU_WIDTH)) // HEAD_DIM
        tril = _row((CHUNK, CHUNK)) >= _lane((CHUNK, CHUNK))

        @pl.when(i == 0)
        def _():
            dw_ref[...] = jnp.zeros_like(dw_ref)
            db_ref[...] = jnp.zeros_like(db_ref)
            dvn_ref[...] = jnp.zeros_like(dvn_ref)

        dus, dvgs = [], []
        for ch in range(nch):
            sl = slice(ch * CHUNK, (ch + 1) * CHUNK)
            vc = vg[sl]
            s = b_ref[...]
            for g in range(4):
                s = s + jnp.where(grp == g, _dot(w_ref[g], vc), 0.0)
            dus.append(d[sl] * s)
            ds = d[sl] * u[sl]
            db_ref[...] += _split_dot(ds, seg_m)
            dvg = jnp.zeros((CHUNK, SGU_WIDTH), F32)
            for g in range(4):
                dsm = jnp.where(grp == g, ds, 0.0)
                dvg = dvg + jnp.where(grp == g, _dot(w_ref[g], ds, TN), 0.0)
                dw_ref[g] += jnp.where(tril, _dot(dsm, vc, NT), 0.0)
            dvgs.append(dvg)
        dup = jnp.concatenate(dus, axis=0)
        dvg = jnp.concatenate(dvgs, axis=0)
        dvn_ref[...] += _fold_lanes(jnp.sum(dvg * xh, axis=0, keepdims=True), HEAD_DIM)
        gy = dvg * vn_v
        dgvv = r * (gy - xh * (_split_dot(xh * gy, seg_m) * (1.0 / HEAD_DIM)))
        dz_ref[...] = jnp.concatenate([dup * du, dgvv * dgv], axis=1).astype(dz_ref.dtype)

    col = lambda j: pl.BlockSpec((tr, 256), lambda i: (i, j))
    return pl.pallas_call(
        body, grid=(T // tr,),
        in_specs=[col(4), col(5), _full((4, CHUNK, CHUNK)), _full((CHUNK, 256)), _full((1, 256)), _full((256, 256)),
                  col(0), ANY],
        out_specs=[pl.BlockSpec((tr, 512), lambda i: (i, 2)), _full((4, CHUNK, CHUNK)), _full((CHUNK, 256)),
                   _full((1, 256))],
        out_shape=[SDS((T, IN_COLS), MXU_DTYPE), SDS((4, CHUNK, CHUNK), F32), SDS((CHUNK, 256), F32),
                   SDS((1, 256), F32)],
        input_output_aliases={7: 0}, name=name, compiler_params=_cp("arbitrary"))(
            z, z, wtril, bexp, vn, seg, dsg, dz)


def _merge_fwd(pa, at, sg, wa, wb, wc, z, x, w_out, *, tm, tn, name):
    T = pa.shape[0]
    gb = GATE_COL0 // tn
    nb = D_MODEL // tn

    def body(pa_ref, at_ref, sg_ref, wa_ref, wb_ref, wc_ref, g0_ref, g1_ref, g2_ref, x_ref, wo_ref,
             m_ref, y_ref, x1_ref):
        j = pl.program_id(1)
        acc = None
        for idx, (op_ref, w_ref, g_ref) in enumerate(((pa_ref, wa_ref, g0_ref), (at_ref, wb_ref, g1_ref),
                                                      (sg_ref, wc_ref, g2_ref))):
            y = _dot(op_ref[...], w_ref[...])
            y_ref[idx] = y.astype(y_ref.dtype)
            t = _sigmoid(g_ref[...].astype(F32)) * y
            acc = t if acc is None else acc + t
        merged = acc.astype(m_ref.dtype)
        m_ref[...] = merged
        p = _dot(merged, wo_ref[...])

        @pl.when(j == 0)
        def _():
            x1_ref[...] = x_ref[...] + p

        @pl.when(j > 0)
        def _():
            x1_ref[...] += p

    op = lambda w: pl.BlockSpec((tm, w), lambda i, j: (i, 0))
    wt = lambda k: pl.BlockSpec((k, tn), lambda i, j: (0, j))
    gate = lambda b: pl.BlockSpec((tm, tn), lambda i, j: (i, gb + b * nb + j))
    return pl.pallas_call(
        body, grid=(T // tm, nb),
        in_specs=[op(256), op(512), op(256), wt(256), wt(512), wt(256), gate(0), gate(1), gate(2),
                  op(D_MODEL), pl.BlockSpec((tn, D_MODEL), lambda i, j: (j, 0))],
        out_specs=[pl.BlockSpec((tm, tn), lambda i, j: (i, j)), pl.BlockSpec((3, tm, tn), lambda i, j: (0, i, j)),
                   op(D_MODEL)],
        out_shape=[SDS((T, D_MODEL), MXU_DTYPE), SDS((3, T, D_MODEL), MXU_DTYPE), SDS((T, D_MODEL), F32)],
        name=name, compiler_params=_cp("parallel", "arbitrary"))(pa, at, sg, wa, wb, wc, z, z, z, x, w_out)


def _out_dx_merge_bwd(dxb, w_out, y, z, ws, xs, *, tm, tn, name):
    T = dxb.shape[0]
    gb = GATE_COL0 // tn
    nb = D_MODEL // tn
    nr = T // tm
    widths = [w.shape[0] for w in ws]

    def body(dx_ref, w_ref, y_ref, g_ref, *refs):
        w_refs, x_refs = refs[0:3], refs[3:6]
        dz_ref, dx_refs, dw_refs = refs[6], refs[7:10], refs[10:13]
        dm_ref, acc_refs = refs[13], refs[14:17]
        i, b, j = pl.program_id(0), pl.program_id(1), pl.program_id(2)

        @pl.when((b == 0) & (j == 0))
        def _():
            dm = _dot(dx_ref[...], w_ref[...], NT)
            for jj in range(nb):
                dm_ref[jj] = dm[:, jj * tn:(jj + 1) * tn]

        d = dm_ref[j]
        g = _sigmoid(g_ref[...].astype(F32))
        dy = (d * g).astype(MXU_DTYPE)
        dz_ref[...] = (d * y_ref[...].astype(F32) * g * (1.0 - g)).astype(dz_ref.dtype)
        for branch in range(3):
            @pl.when(b == branch)
            def _():
                p = _dot(dy, w_refs[branch][...], NT)
                q = _dot(x_refs[branch][...], dy, TN)

                @pl.when(j == 0)
                def _():
                    dx_refs[branch][...] = p

                @pl.when(j > 0)
                def _():
                    dx_refs[branch][...] += p

                @pl.when(i == 0)
                def _():
                    acc_refs[branch][j] = q

                @pl.when(i > 0)
                def _():
                    acc_refs[branch][j] += q

        @pl.when((i == nr - 1) & (b == 2) & (j == nb - 1))
        def _():
            for branch in range(3):
                for jj in range(nb):
                    dw_refs[branch][:, jj * tn:(jj + 1) * tn] = acc_refs[branch][jj]

    wspec = lambda k: pl.BlockSpec((k, tn), lambda i, b, j: (0, j))
    rows = lambda k: pl.BlockSpec((tm, k), lambda i, b, j: (i, 0))
    return pl.pallas_call(
        body, grid=(nr, 3, nb),
        in_specs=[rows(D_MODEL),
                  pl.BlockSpec((D_MODEL, D_MODEL), lambda i, b, j: (0, 0), pipeline_mode=pl.Buffered(1)),
                  pl.BlockSpec((None, tm, tn), lambda i, b, j: (b, i, j)),
                  pl.BlockSpec((tm, tn), lambda i, b, j: (i, gb + b * nb + j))]
        + [wspec(k) for k in widths] + [rows(k) for k in widths],
        out_specs=[pl.BlockSpec((tm, tn), lambda i, b, j: (i, gb + b * nb + j))]
        + [rows(k) for k in widths] + [_full((k, D_MODEL)) for k in widths],
        out_shape=[SDS((T, IN_COLS), MXU_DTYPE)] + [SDS((T, k), F32) for k in widths]
        + [SDS((k, D_MODEL), F32) for k in widths],
        scratch_shapes=[pltpu.VMEM((nb, tm, tn), F32)] + [pltpu.VMEM((nb, k, tn), F32) for k in widths],
        name=name, compiler_params=_cp("arbitrary", "arbitrary", "arbitrary"))(dxb, w_out, y, z, *ws, *xs)


def _conv3(xe, w, b):
    return (w[0:1] * pltpu.roll(xe, 2, 0) + w[1:2] * pltpu.roll(xe, 1, 0) + w[2:3] * xe)[8:] + b


def _conv_act_fwd(up, cw, cb, *, tr, tc, name):
    T = up.shape[0]
    nc = D_FF // tc
    hb = tr // HALO

    def body(ug_ref, ugp_ref, uv_ref, uvp_ref, wg_ref, wv_ref, bg_ref, bv_ref, o_ref):
        i = pl.program_id(1)
        first = i == 0

        def halo_tile(prev_ref, cur_ref):
            prev8 = prev_ref[...].astype(F32)[HALO - 8:]
            return jnp.concatenate([jnp.where(first, 0.0, prev8), cur_ref[...].astype(F32)], axis=0)

        cg = _conv3(halo_tile(ugp_ref, ug_ref), wg_ref[...], bg_ref[...])
        cv = _conv3(halo_tile(uvp_ref, uv_ref), wv_ref[...], bv_ref[...])
        o_ref[...] = (cg * _sigmoid(cg) * cv).astype(o_ref.dtype)

    tile = lambda off: pl.BlockSpec((tr, tc), lambda j, i: (i, off + j))
    prev = lambda off: pl.BlockSpec((HALO, tc), lambda j, i: (jnp.maximum(i * hb - 1, 0), off + j))
    par = lambda rows, off: pl.BlockSpec((rows, tc), lambda j, i: (0, off + j))
    return pl.pallas_call(
        body, grid=(nc, T // tr),
        in_specs=[tile(0), prev(0), tile(nc), prev(nc), par(3, 0), par(3, nc), par(1, 0), par(1, nc)],
        out_specs=pl.BlockSpec((tr, tc), lambda j, i: (i, j)),
        out_shape=SDS((T, D_FF), MXU_DTYPE), name=name,
        compiler_params=_cp("parallel", "parallel"))(up, up, up, up, cw, cw, cb, cb)


def _conv_act_bwd(up, cw, cb, dact, *, tr, tc, name, deps=()):
    T = up.shape[0]
    nc = D_FF // tc
    hbu = tr // HALO
    nr = T // tr

    def body(ug_ref, ugp_ref, ugn_ref, uv_ref, uvp_ref, uvn_ref, da_ref, dan_ref, wg_ref, wv_ref, bg_ref, bv_ref,
             du_ref, dwg_ref, dwv_ref, dbg_ref, dbv_ref):
        i = pl.program_id(1)
        first, last = i == 0, i == nr - 1
        da = jnp.concatenate([da_ref[...].astype(F32), jnp.where(last, 0.0, dan_ref[...].astype(F32)[:8])],
                             axis=0)

        def with_halos(prev_ref, cur_ref, next_ref):
            prev8 = prev_ref[...].astype(F32)[HALO - 8:]
            next8 = next_ref[...].astype(F32)[:8]
            return jnp.concatenate([jnp.where(first, 0.0, prev8), cur_ref[...].astype(F32), next8], axis=0)

        uge = with_halos(ugp_ref, ug_ref, ugn_ref)
        uve = with_halos(uvp_ref, uv_ref, uvn_ref)
        wg, wv = wg_ref[...], wv_ref[...]
        ug1, ug2 = pltpu.roll(uge, 1, 0)[8:], pltpu.roll(uge, 2, 0)[8:]
        uv1, uv2 = pltpu.roll(uve, 1, 0)[8:], pltpu.roll(uve, 2, 0)[8:]
        cg = wg[0:1] * ug2 + wg[1:2] * ug1 + wg[2:3] * uge[8:] + bg_ref[...]
        cv = wv[0:1] * uv2 + wv[1:2] * uv1 + wv[2:3] * uve[8:] + bv_ref[...]
        sg = _sigmoid(cg)
        dcg = da * cv * (sg * (1.0 + cg * (1.0 - sg)))
        dcv = da * (cg * sg)
        nrow = tr + 8

        def back(dc, w):
            return (w[2:3] * dc + w[1:2] * pltpu.roll(dc, nrow - 1, 0) + w[0:1] * pltpu.roll(dc, nrow - 2, 0))[:tr]

        du_ref[0] = back(dcg, wg).astype(du_ref.dtype)
        du_ref[1] = back(dcv, wv).astype(du_ref.dtype)

        def wgrad(dc, u0, u1, u2):
            d = dc[:tr]
            rows = [jnp.sum(d * u2[:tr], axis=0, keepdims=True), jnp.sum(d * u1[:tr], axis=0, keepdims=True),
                    jnp.sum(d * u0[8:8 + tr], axis=0, keepdims=True)]
            return jnp.concatenate(rows, axis=0), jnp.sum(d, axis=0, keepdims=True)

        dwg, dbg = wgrad(dcg, uge, ug1, ug2)
        dwv, dbv = wgrad(dcv, uve, uv1, uv2)

        @pl.when(first)
        def _():
            dwg_ref[...] = jnp.zeros_like(dwg_ref)
            dwv_ref[...] = jnp.zeros_like(dwv_ref)
            dbg_ref[...] = jnp.zeros_like(dbg_ref)
            dbv_ref[...] = jnp.zeros_like(dbv_ref)
        dwg_ref[...] += dwg
        dwv_ref[...] += dwv
        dbg_ref[...] += dbg
        dbv_ref[...] += dbv

    tile = lambda off: pl.BlockSpec((tr, tc), lambda j, i: (i, off + j))
    prev = lambda off: pl.BlockSpec((HALO, tc), lambda j, i: (jnp.maximum(i * hbu - 1, 0), off + j))
    nxt = lambda off: pl.BlockSpec((HALO, tc), lambda j, i: (jnp.minimum((i + 1) * hbu, T // HALO - 1), off + j))
    dnext = pl.BlockSpec((HALO, tc), lambda j, i: (jnp.minimum((i + 1) * hbu, T // HALO - 1), j))
    par = lambda rows, off: pl.BlockSpec((rows, tc), lambda j, i: (0, off + j))
    acc = lambda rows: pl.BlockSpec((rows, tc), lambda j, i: (0, j))
    return pl.pallas_call(
        _after(body, 12, deps), grid=(nc, nr),
        in_specs=[tile(0), prev(0), nxt(0), tile(nc), prev(nc), nxt(nc), tile(0), dnext,
                  par(3, 0), par(3, nc), par(1, 0), par(1, nc)] + [ANY] * len(deps),
        out_specs=[pl.BlockSpec((2, tr, tc), lambda j, i: (0, i, j)), acc(3), acc(3), acc(1), acc(1)],
        out_shape=[SDS((2, T, D_FF), MXU_DTYPE), SDS((3, D_FF), F32), SDS((3, D_FF), F32),
                   SDS((1, D_FF), F32), SDS((1, D_FF), F32)],
        name=name, compiler_params=_cp("parallel", "arbitrary"))(
            up, up, up, up, up, up, dact, dact, cw, cw, cb, cb, *deps)


def _row_tile(rows, cap):
    t = min(cap, rows)
    t -= t % 8
    while rows % t:
        t -= 8
    return t


def _adamw(w, g, m, v, *, tr, name, copy_g=False):
    R, C = w.shape
    assert R % tr == 0, (R, tr)

    def body(w_ref, g_ref, m_ref, v_ref, d_ref, nm_ref, nv_ref, *rest):
        gv = g_ref[...]
        mn = ADAM_B1 * m_ref[...] + (1.0 - ADAM_B1) * gv
        vn = ADAM_B2 * v_ref[...] + (1.0 - ADAM_B2) * (gv * gv)
        m_hat = mn / (1.0 - ADAM_B1 ** ADAM_STEP)
        v_hat = vn / (1.0 - ADAM_B2 ** ADAM_STEP)
        d_ref[...] = -ADAM_LR * (m_hat / (jnp.sqrt(v_hat) + ADAM_EPS) + ADAM_WD * w_ref[...])
        nm_ref[...] = mn
        nv_ref[...] = vn
        if copy_g:
            rest[0][...] = gv

    rows = pl.BlockSpec((tr, C), lambda i: (i, 0))
    n_out = 4 if copy_g else 3
    return pl.pallas_call(
        body, grid=(R // tr,), in_specs=[rows] * 4, out_specs=[rows] * n_out,
        out_shape=[SDS((R, C), F32)] * n_out, name=name, compiler_params=_cp("parallel"))(w, g, m, v)


def _sum_slots(r, *, tr, name):
    S, R, C = r.shape
    assert R % tr == 0, (R, tr)

    def body(r_ref, o_ref):
        acc = r_ref[0]
        for s in range(1, S):
            acc = acc + r_ref[s]
        o_ref[...] = acc

    return pl.pallas_call(
        body, grid=(R // tr,), in_specs=[pl.BlockSpec((S, tr, C), lambda i: (0, i, 0))],
        out_specs=pl.BlockSpec((tr, C), lambda i: (i, 0)), out_shape=SDS((R, C), F32),
        name=name, compiler_params=_cp("parallel"))(r)


def _pair_add(g4, h, pos, *, name):
    A, _, r, C = g4.shape
    cs = C if A == N_CHIPS else C // N_CHIPS
    tr = _row_tile(r, 256)
    if A == N_CHIPS:
        g_map, h_map = (lambda t, i, pos: (t, pos[1], i, 0)), (lambda t, i, pos: (t, i, 0))
    else:
        g_map, h_map = (lambda t, i, pos: (0, pos[1], i, t)), (lambda t, i, pos: (0, i, t))

    def body(pos_ref, g_ref, h_ref, o_ref):
        o_ref[...] = (g_ref[...] + h_ref[...]).astype(o_ref.dtype)

    grid_spec = pltpu.PrefetchScalarGridSpec(
        num_scalar_prefetch=1, grid=(N_CHIPS, r // tr),
        in_specs=[pl.BlockSpec((None, None, tr, cs), g_map), pl.BlockSpec((None, tr, cs), h_map)],
        out_specs=pl.BlockSpec((None, tr, cs), lambda t, i, pos: (t, i, 0)))
    return pl.pallas_call(body, grid_spec=grid_spec, out_shape=SDS((N_CHIPS, r, cs), COMM_DTYPE), name=name,
                          compiler_params=_cp("parallel", "parallel"))(pos, g4, h)


def _chip_sum(p, r2, f_into, pos, layer, *, name):
    _, r, cs = p.shape
    tr = _row_tile(r, 256)

    def body(pos_ref, own_ref, r_ref, *rest):
        o_ref = rest[-1]
        o_ref[...] = ((own_ref[...].astype(F32) + r_ref[0].astype(F32)) + r_ref[1].astype(F32)) + r_ref[2].astype(F32)

    in_specs = [pl.BlockSpec((None, tr, cs), lambda i, pos: (pos[0], i, 0)),
                pl.BlockSpec((3, tr, cs), lambda i, pos: (0, i, 0))]
    operands = [pos, p, r2]
    aliases = {}
    if f_into is not None:
        in_specs.append(ANY)
        operands.append(f_into)
        aliases = {3: 0}
    grid_spec = pltpu.PrefetchScalarGridSpec(
        num_scalar_prefetch=1, grid=(r // tr,), in_specs=in_specs,
        out_specs=pl.BlockSpec((None, None, tr, cs), lambda i, pos: (layer, pos[1], i, 0)))
    return pl.pallas_call(body, grid_spec=grid_spec, out_shape=SDS((DEPTH, 2, r, cs), F32), name=name,
                          input_output_aliases=aliases, compiler_params=_cp("parallel"))(*operands)


def _mesh_pos():
    return lax.axis_index("x"), lax.axis_index("y"), lax.axis_index("c")


HBM = pl.BlockSpec(memory_space=pltpu.HBM)
SEM = pl.BlockSpec(memory_space=pltpu.SEMAPHORE)
DATAFLOW = pltpu.SideEffectType.DATAFLOW_SIDE_EFFECTING
CHIP_FLIPS = (2, 1, 3)


def _chip_peers():
    x, y, c = _mesh_pos()
    return 2 * x + y, [(1 - x, y, c), (x, 1 - y, c), (1 - x, 1 - y, c)], (x, y, 1 - c), c


def _split_start(arrays, n_copies, issue, *, name, deps=()):
    k = len(arrays)
    nd = len(deps)

    def body(*refs):
        issue(refs[:k], refs[k + nd], refs[k + nd + 1])
        refs[2 * k + nd + 2][...] = jnp.zeros((8, 128), F32)

    out = pl.pallas_call(
        body, name=name,
        out_shape=(pltpu.SemaphoreType.DMA((n_copies,)), pltpu.SemaphoreType.DMA((n_copies,)),
                   *[pltpu.HBM(a.shape, a.dtype) for a in arrays], SDS((8, 128), F32)),
        in_specs=[HBM] * k + [ANY] * nd, out_specs=(SEM, SEM, *[HBM] * k, pl.BlockSpec(memory_space=pltpu.VMEM)),
        input_output_aliases={i: 2 + i for i in range(k)},
        compiler_params=pltpu.CompilerParams(has_side_effects=DATAFLOW))(
            *[pltpu.with_memory_space_constraint(a, pltpu.HBM) for a in arrays], *deps)
    return (out[0], out[1]), list(out[2:2 + k]), out[2 + k]


def _split_wait(sems, arrays, after, waits, *, name):
    k = len(arrays)
    afters = tuple(after) if isinstance(after, (tuple, list)) else (after,)

    def body(*refs):
        waits(refs[:k], refs[k], refs[k + 1])

    out = pl.pallas_call(
        body, name=name, out_shape=tuple(pltpu.HBM(a.shape, a.dtype) for a in arrays),
        in_specs=[HBM] * k + [SEM, SEM] + [ANY] * len(afters), out_specs=tuple([HBM] * k),
        input_output_aliases={i: i for i in range(k)},
        compiler_params=pltpu.CompilerParams(has_side_effects=DATAFLOW))(*arrays, sems[0], sems[1], *afters)
    return list(out)


def _wait_both(cp):
    cp.wait_send()
    cp.wait_recv()


def _cast_place(shard, pos, dtype, *, name, layer=None, slots=N_CHIPS, which=0):
    R, C = shard.shape[-2:]
    tr = R if R % 8 else _row_tile(R, 256)
    if layer is None:
        in_spec = pl.BlockSpec((tr, C), lambda i, pos: (i, 0))
    else:
        in_spec = pl.BlockSpec((None, tr, C), lambda i, pos: (layer, i, 0))

    def body(pos_ref, x_ref, o_ref):
        o_ref[...] = x_ref[...].astype(o_ref.dtype)

    grid_spec = pltpu.PrefetchScalarGridSpec(
        num_scalar_prefetch=1, grid=(R // tr,), in_specs=[in_spec],
        out_specs=pl.BlockSpec((None, tr, C), lambda i, pos: (pos[which], i, 0)))
    return pl.pallas_call(body, grid_spec=grid_spec, out_shape=SDS((slots, R, C), dtype), name=name,
                          compiler_params=_cp("parallel"))(pos, shard)


def _device_peers():
    x, y, c = _mesh_pos()
    peers = [(x ^ ((f >> 2) & 1), y ^ ((f >> 1) & 1), c ^ (f & 1)) for f in range(1, N_DEV)]
    return 4 * x + 2 * y + c, peers


class _Gather:
    def __init__(self, lands, name, deps=(), all_devices=False, halves=False):
        n = len(lands)
        self.name, self.halves = name, halves
        npeer = N_DEV - 1 if all_devices else N_CHIPS - 1
        if halves:
            lands = [a.reshape(a.shape[0], 2, a.shape[1] // 2, a.shape[2]) for a in lands]

        def copies(refs, ss, rs):
            if halves:
                me, peers, _, c = _chip_peers()
                own = lambda r: r.at[me, c]
            else:
                me, peers = _device_peers() if all_devices else _chip_peers()[:2]
                own = lambda r: r.at[me]
            return [pltpu.make_async_remote_copy(
                src_ref=own(refs[w]), dst_ref=own(refs[w]), send_sem=ss.at[npeer * w + p],
                recv_sem=rs.at[npeer * w + p], device_id=peers[p], device_id_type=MESH)
                for w in range(n) for p in range(npeer)]

        def issue(refs, ss, rs):
            for cp in copies(refs, ss, rs):
                cp.start()

        def waits(refs, ss, rs):
            for cp in copies(refs, ss, rs):
                _wait_both(cp)

        self._waits = waits
        self.sems, self.arrays, self.token = _split_start(list(lands), npeer * n, issue, name=name + "_start",
                                                          deps=deps)

    def wait(self, after):
        arrays = _split_wait(self.sems, self.arrays, after, self._waits, name=self.name + "_wait")
        if not self.halves:
            return arrays
        n = len(arrays)

        def copies(refs, ss, rs):
            me, _, sibling, c = _chip_peers()
            return [pltpu.make_async_remote_copy(
                src_ref=refs[w].at[me ^ CHIP_FLIPS[p], c], dst_ref=refs[w].at[me ^ CHIP_FLIPS[p], c],
                send_sem=ss.at[3 * w + p], recv_sem=rs.at[3 * w + p], device_id=sibling, device_id_type=MESH)
                for w in range(n) for p in range(3)]

        def issue(refs, ss, rs):
            for cp in copies(refs, ss, rs):
                cp.start()

        def waits(refs, ss, rs):
            for cp in copies(refs, ss, rs):
                _wait_both(cp)

        sems, arrays, _ = _split_start(arrays, 3 * n, issue, name=self.name + "_share_start")
        arrays = _split_wait(sems, arrays, after, waits, name=self.name + "_share_wait")
        return [a.reshape(a.shape[0], 2 * a.shape[2], a.shape[3]) for a in arrays]


def _swap_halves_start(g4s, *, name):
    n = len(g4s)
    lands = [lax.empty((g.shape[0],) + g.shape[2:], g.dtype) for g in g4s]

    def copies(refs, ss, rs):
        _, _, sibling, c = _chip_peers()
        return [pltpu.make_async_remote_copy(
            src_ref=refs[w].at[:, 1 - c], dst_ref=refs[n + w], send_sem=ss.at[w], recv_sem=rs.at[w],
            device_id=sibling, device_id_type=MESH) for w in range(n)]

    def issue(refs, ss, rs):
        for cp in copies(refs, ss, rs):
            cp.start()

    def waits(refs, ss, rs):
        for cp in copies(refs, ss, rs):
            _wait_both(cp)

    sems, arrays, token = _split_start(list(g4s) + lands, n, issue, name=name + "_start")
    return sems, arrays, token, waits


def _scatter_start(parts, *, name, deps=()):
    n = len(parts)
    lands = [lax.empty((3,) + p.shape[1:], p.dtype) for p in parts]

    def copies(refs, ss, rs):
        me, peers, _, _ = _chip_peers()
        return [pltpu.make_async_remote_copy(
            src_ref=refs[w].at[me ^ CHIP_FLIPS[p]], dst_ref=refs[n + w].at[p],
            send_sem=ss.at[3 * w + p], recv_sem=rs.at[3 * w + p], device_id=peers[p], device_id_type=MESH)
            for w in range(n) for p in range(3)]

    def issue(refs, ss, rs):
        for cp in copies(refs, ss, rs):
            cp.start()

    def waits(refs, ss, rs):
        for cp in copies(refs, ss, rs):
            _wait_both(cp)

    sems, arrays, token = _split_start(list(parts) + lands, 3 * n, issue, name=name + "_start", deps=deps)
    return sems, arrays, token, waits


def _pair_share_start(fs, layer, *, name):
    n = len(fs)

    def copies(refs, ss, rs):
        _, _, sibling, c = _chip_peers()
        return [pltpu.make_async_remote_copy(
            src_ref=refs[w].at[layer, c], dst_ref=refs[w].at[layer, c], send_sem=ss.at[w], recv_sem=rs.at[w],
            device_id=sibling, device_id_type=MESH) for w in range(n)]

    def issue(refs, ss, rs):
        for cp in copies(refs, ss, rs):
            cp.start()

    def waits(refs, ss, rs):
        for cp in copies(refs, ss, rs):
            _wait_both(cp)

    sems, arrays, token = _split_start(list(fs), n, issue, name=name + "_start")
    return sems, arrays, token, waits


BIG = ('w_in', 'w_proj_a', 'w_proj_b', 'w_proj_c', 'w_out', 'w_up', 'w_down')
BIG_SHARD_AXIS = {'w_in': 2, 'w_proj_a': 2, 'w_proj_b': 2, 'w_proj_c': 2, 'w_out': 1, 'w_up': 2, 'w_down': 1}
SMALL = ('norm1', 'q_norm', 'k_norm', 'sinks', 'w_pool', 'pool_scale', 'sgu_v_norm', 'w_s', 'b_s', 'norm2',
         'conv_b', 'conv_w')
WEIGHTS = ('norm1', 'w_in', 'q_norm', 'k_norm', 'sinks', 'w_pool', 'pool_scale', 'sgu_v_norm', 'w_s', 'b_s',
           'w_proj_a', 'w_proj_b', 'w_proj_c', 'w_out', 'norm2', 'w_up', 'conv_w', 'conv_b', 'w_down')


def _rope_tables(positions):
    inv_freq = ROPE_THETA ** (-jnp.arange(0, HEAD_DIM, 2, dtype=F32) / HEAD_DIM)
    ang = positions.astype(F32)[:, None] * inv_freq
    cos, sin = jnp.cos(ang), jnp.sin(ang)
    c = jnp.concatenate([cos, cos], axis=1)
    s = jnp.concatenate([-sin, sin], axis=1)
    return jnp.concatenate([c, c], axis=1), jnp.concatenate([s, s], axis=1)


def _block_diag4(w):
    out = jnp.zeros((POOL_WIDTH, POOL_WIDTH), w.dtype)
    for g in range(4):
        out = lax.dynamic_update_slice(out, w[g], (g * HEAD_DIM, g * HEAD_DIM))
    return out


def _local_step(x, target, cos, sin, sp, sched):
    T = x.shape[0]
    tm1 = min(1024, T)
    tm = min(512, T)
    tr = min(1024, T)
    trc = min(512, T)
    tkt = min(2048, T)
    seg = _seg_matrix(256, HEAD_DIM)
    saved = []
    xl = x
    for l in range(DEPTH):
        p = f"l{l}_"
        c = dict(
            g1=sp['norm1'][l][None], g2=sp['norm2'][l][None],
            wbd=_block_diag4(sp['w_pool'][l]).astype(MXU_DTYPE), scale=sp['pool_scale'][l][None],
            gq=jnp.tile(sp['q_norm'][l], 4)[None], gk=jnp.tile(sp['k_norm'][l], 2)[None],
            sinks=jnp.broadcast_to(sp['sinks'][l][:, None], (N_Q_HEADS, 128)),
            wtril=jnp.tril(sp['w_s'][l]).astype(MXU_DTYPE),
            bexp=jnp.repeat(sp['b_s'][l].T, HEAD_DIM, axis=1), vn=jnp.tile(sp['sgu_v_norm'][l], 4)[None],
            cb=sp['conv_b'][l][None])
        c['w_in'] = sched.weight('w_in', l, xl)
        z, h1 = _norm_mm(xl, c['g1'], c['w_in'], tm=tm1, tn=1152, name=p + "in_proj",
                         deps=sched.start_tokens() if l == 0 else ())
        pa = _pool_fwd(z, c['wbd'], c['scale'], tr=tr, name=p + "pool")
        q, k, v = _qkv_prep(z, cos, sin, c['gq'], c['gk'], seg, tr=tr, name=p + "qkv_prep")
        at = _attn_fwd(q, k, v, c['sinks'], name=p + "attn")
        sg = _sgu_fwd(z, c['wtril'], c['bexp'], c['vn'], seg, tr=tr, name=p + "sgu")
        for n in ('w_proj_a', 'w_proj_b', 'w_proj_c', 'w_out'):
            c[n] = sched.weight(n, l, (pa, at, sg))
        merged, y3, x1 = _merge_fwd(pa, at, sg, c['w_proj_a'], c['w_proj_b'], c['w_proj_c'], z, xl, c['w_out'],
                                    tm=tm1, tn=512, name=p + "merge_out_proj")
        for n in ('w_up', 'conv_w', 'w_down'):
            c[n] = sched.weight(n, l, x1)
        up, h2 = _norm_mm(x1, c['g2'], c['w_up'], tm=tm1, tn=1408, name=p + "up_proj")
        act = _conv_act_fwd(up, c['conv_w'], c['cb'], tr=trc, tc=1408, name=p + "conv_act")
        saved.append(dict(c, x=xl, h1=h1, z=z, pa=pa, q=q, k=k, v=v, at=at, sg=sg, merged=merged, y3=y3,
                          x1=x1, h2=h2, up=up, act=act))
        if l < DEPTH - 1:
            xl = _mm(act, c['w_down'], mode='nn', add=x1, tm=tm, tn=D_MODEL, tk=D_FF, name=p + "down_proj")
        else:
            loss_row, dx, dxb = _down_proj_loss(act, c['w_down'], x1, target, tm=tm, name=p + "down_proj_loss")

    gs = {n: [None] * DEPTH for n in SMALL}
    for l in reversed(range(DEPTH)):
        p = f"l{l}_b_"
        s = saved[l]
        gb = {}
        dact = _mm(dxb, s['w_down'], mode='nt', tm=tm1, tn=1408, tk=D_MODEL, out_dtype=ACT_DTYPE,
                   name=p + "down_dx")
        gb['w_down'] = _mm(s['act'], dxb, mode='tn', tm=1408, tn=D_MODEL, tk=tkt, name=p + "down_dw")
        toks = sched.slot(l, 'down', gb['w_down'])
        dup, dwg, dwv, dbg, dbv = _conv_act_bwd(s['up'], s['conv_w'], s['cb'], dact, tr=min(1024, T), tc=256,
                                                name=p + "conv_act", deps=toks)
        gs['conv_w'][l] = jnp.concatenate([dwg, dwv], axis=1)
        gs['conv_b'][l] = jnp.concatenate([dbg, dbv], axis=1)[0]
        toks = sched.slot(l, 'conv', dup)
        for half in range(2):
            gb['w_up'] = _mm(s['h2'], dup, mode='tn', b_lead=half, tm=D_MODEL, tn=1408, tk=tkt,
                             out_into=gb.get('w_up'), out_joff=2 * half, out_n=2 * D_FF, name=p + f"up_dw{half}",
                             deps=toks if half == 0 else ())
        toks = sched.slot(l, 'ffn', gb['w_up'], gb)
        dx1, dx1b, dg2 = _mm_nt_sharded_rms(dup, s['w_up'], s['x1'], s['g2'], dx, tm=tm,
                                            name=p + "up_dx_rms2", deps=toks)
        gs['norm2'][l] = dg2[0]
        gb['w_out'] = _mm(s['merged'], dx1b, mode='tn', tm=D_MODEL, tn=D_MODEL, tk=tkt, name=p + "out_dw")
        (dz, dpa, dat, dsg, gb['w_proj_a'], gb['w_proj_b'], gb['w_proj_c']) = _out_dx_merge_bwd(
            dx1b, s['w_out'], s['y3'], s['z'], [s['w_proj_a'], s['w_proj_b'], s['w_proj_c']],
            [s['pa'], s['at'], s['sg']], tm=tm1, tn=512, name=p + "out_dx_merge")
        toks = sched.slot(l, 'mid', dz)
        dq, dkc, dkp, dvc, dvp, dsk = _attn_bwd(s['q'], s['k'], s['v'], s['sinks'], dat, name=p + "attn", deps=toks)
        gs['sinks'][l] = dsk[:, 0]
        toks = sched.slot(l, 'attn', dq)
        dz, dgq, dgk, dwbd, dsc = _mixer_ab_bwd(s['z'], cos, sin, s['gq'], s['gk'], seg, dq, dkc, dkp, dvc, dvp,
                                                dpa, s['wbd'], s['scale'], dz, tr=tr, name=p + "qkv_pool", deps=toks)
        gs['q_norm'][l] = dgq[0, :HEAD_DIM]
        gs['k_norm'][l] = dgk[0, :HEAD_DIM]
        gs['w_pool'][l] = jnp.stack([dwbd[g * HEAD_DIM:(g + 1) * HEAD_DIM, g * HEAD_DIM:(g + 1) * HEAD_DIM]
                                     for g in range(4)])
        gs['pool_scale'][l] = dsc[0]
        dz, dws, dbrows, dvn = _sgu_bwd(s['z'], s['wtril'], s['bexp'], s['vn'], seg, dsg, dz, tr=tr, name=p + "sgu")
        gs['w_s'][l] = dws
        gs['b_s'][l] = dbrows[:, ::HEAD_DIM].T
        gs['sgu_v_norm'][l] = dvn[0, :HEAD_DIM]
        gb['w_in'] = _mm(s['h1'], dz, mode='tn', tm=D_MODEL, tn=1152, tk=tkt, name=p + "in_dw")
        toks = sched.slot(l, 'mix', gb['w_in'], gb)
        dx, dxb, dg1 = _mm_nt_sharded_rms(dz, s['w_in'], s['x'], s['g1'], dx1, tm=tm,
                                          name=p + "in_dx_rms1", deps=toks)
        gs['norm1'][l] = dg1[0]
    gs = {n: jnp.stack(v) for n, v in gs.items()}
    return loss_row, dx, gs


GROUP_F = ('w_down', 'w_up')
GROUP_M = ('w_out', 'w_proj_a', 'w_proj_b', 'w_proj_c', 'w_in')
ROW_SHARDED = ('w_out', 'w_down')

REDUCE_PLAN = {
    (1, 'ffn'): (('S1', 'F', 1),),
    (1, 'mid'): (('W1', 'F', 1),),
    (1, 'mix'): (('S1', 'M', 1),),
    (0, 'down'): (('W1', 'M', 1),),
    (0, 'conv'): (('W2', 'F', 1),),
    (0, 'ffn'): (('S1', 'F', 0), ('W3', 'F', 1)),
    (0, 'mid'): (('W1', 'F', 0),),
    (0, 'attn'): (('W2', 'M', 1),),
    (0, 'mix'): (('S1', 'M', 0), ('W3', 'M', 1)),
}
REDUCE_TAIL_A = (('W1', 'M', 0), ('W2', 'F', 0))
REDUCE_TAIL_B = (('W3', 'F', 0),)
REDUCE_TAIL_C = (('W2', 'M', 0), ('W3', 'M', 0))


class _Comm:
    def __init__(self, w, pos):
        self.pos = pos
        groups = {'a': [('w_in', 0)],
                  'b': [(n, 0) for n in ('w_proj_a', 'w_proj_b', 'w_proj_c', 'w_out')],
                  'c': [(n, 0) for n in ('w_up', 'conv_w', 'w_down')],
                  'd': [(n, 1) for n in BIG] + [('conv_w', 1)]}
        self.gathers, self.group_of, self.weights = {}, {}, {}
        self.tokens = []
        for g, ks in groups.items():
            lands = [_cast_place(w[n], pos, F32 if n == 'conv_w' else MXU_DTYPE, layer=l, name=f"gw_place_{n}{l}")
                     for n, l in ks]
            self.gathers[g] = (_Gather(lands, "gw_" + g, deps=self.tokens[-1:], halves=(g == 'a')), ks)
            self.tokens.append(self.gathers[g][0].token)
            self.group_of.update({k: g for k in ks})
        self.red = {}
        self.final = {}

    def start_tokens(self):
        return self.tokens[-1:]

    def weight(self, name, layer, after):
        if (name, layer) not in self.weights:
            gather, ks = self.gathers[self.group_of[(name, layer)]]
            for (n, l), full in zip(ks, gather.wait(after)):
                if n == 'conv_w' or n.startswith('w_proj'):
                    full = full.transpose(1, 0, 2).reshape(full.shape[1], -1)
                elif n in ROW_SHARDED:
                    full = full.reshape(-1, full.shape[2])
                self.weights[(n, l)] = full
        return self.weights[(name, layer)]

    def slot(self, layer, slot, after, grads=None):
        tokens = []
        for step, grp, lyr in REDUCE_PLAN.get((layer, slot), ()):
            tok = self._step(step, grp, lyr, after, grads)
            if tok is not None:
                tokens.append(tok)
        return tokens

    def tail(self, steps, after, deps=()):
        toks = (self._step(step, grp, lyr, after, None, deps) for step, grp, lyr in steps)
        return [t for t in toks if t is not None]

    def shards(self):
        return {n: f.reshape(DEPTH, 2 * f.shape[2], f.shape[3]) for n, f in self.final.items()}

    def _step(self, step, grp, layer, after, grads, deps=()):
        names = GROUP_F if grp == 'F' else GROUP_M
        tag = f"{grp.lower()}{layer}"
        st = self.red.setdefault((grp, layer), {})
        n = len(names)
        if step == 'S1':
            g4s = []
            for nm in names:
                g = grads[nm]
                R, C = g.shape
                g4s.append(g.reshape(N_CHIPS, 2, R // (2 * N_CHIPS), C) if nm in ROW_SHARDED
                           else g.reshape(1, 2, R // 2, C))
            st['s1'] = _swap_halves_start(g4s, name="rs1_" + tag)
            return st['s1'][2]
        if step == 'W1':
            sems, arrays, _, waits = st.pop('s1')
            arrays = _split_wait(sems, arrays, after, waits, name=f"rs1_{tag}_wait")
            parts = [_pair_add(arrays[i], arrays[n + i], self.pos, name=f"pair_add_{tag}_{names[i]}")
                     for i in range(n)]
            st['s2'] = _scatter_start(parts, name="rs2_" + tag, deps=deps)
            return st['s2'][2]
        if step == 'W2':
            sems, arrays, _, waits = st.pop('s2')
            arrays = _split_wait(sems, arrays, after, waits, name=f"rs2_{tag}_wait")
            fs = [_chip_sum(arrays[i], arrays[n + i], self.final.get(names[i]), self.pos, layer,
                            name=f"chip_sum_{tag}_{names[i]}") for i in range(n)]
            st['s3'] = _pair_share_start(fs, layer, name="rs3_" + tag)
            return st['s3'][2]
        sems, arrays, _, waits = st.pop('s3')
        self.final.update(zip(names, _split_wait(sems, arrays, after, waits, name=f"rs3_{tag}_wait")))
        return None


def _pack(arrays):
    rows = []
    for a in arrays:
        nel = int(np.prod(a.shape))
        if nel % 1024 == 0:
            rows.append(a.astype(F32).reshape(nel // 128, 128))
        else:
            f = a.reshape(-1).astype(F32)
            rows.append(jnp.pad(f, (0, (-nel) % 1024)).reshape(-1, 128))
    return jnp.concatenate(rows, axis=0)


def _unpack(pack, shapes):
    out, row = [], 0
    for shp in shapes:
        nel = int(np.prod(shp))
        nrow = 8 * -(-nel // 1024)
        part = pack[row:row + nrow]
        out.append(part.reshape(shp) if nel % 1024 == 0 else part.reshape(-1)[:nel].reshape(shp))
        row += nrow
    return out


def kernel(x, positions, norm1, w_in, q_norm, k_norm, sinks, w_pool, pool_scale, sgu_v_norm, w_s, b_s, w_proj_a, w_proj_b, w_proj_c, w_out, norm2, w_up, conv_w, conv_b, w_down, loss_target, m_norm1, m_w_in, m_q_norm, m_k_norm, m_sinks, m_w_pool, m_pool_scale, m_sgu_v_norm, m_w_s, m_b_s, m_w_proj_a, m_w_proj_b, m_w_proj_c, m_w_out, m_norm2, m_w_up, m_conv_w, m_conv_b, m_w_down, v_norm1, v_w_in, v_q_norm, v_k_norm, v_sinks, v_w_pool, v_pool_scale, v_sgu_v_norm, v_w_s, v_b_s, v_w_proj_a, v_w_proj_b, v_w_proj_c, v_w_out, v_norm2, v_w_up, v_conv_w, v_conv_b, v_w_down):
    w = dict(norm1=norm1, w_in=w_in, q_norm=q_norm, k_norm=k_norm, sinks=sinks, w_pool=w_pool, pool_scale=pool_scale,
             sgu_v_norm=sgu_v_norm, w_s=w_s, b_s=b_s, w_proj_a=w_proj_a, w_proj_b=w_proj_b, w_proj_c=w_proj_c,
             w_out=w_out, norm2=norm2, w_up=w_up, conv_w=conv_w, conv_b=conv_b, w_down=w_down)
    m = dict(norm1=m_norm1, w_in=m_w_in, q_norm=m_q_norm, k_norm=m_k_norm, sinks=m_sinks, w_pool=m_w_pool,
             pool_scale=m_pool_scale, sgu_v_norm=m_sgu_v_norm, w_s=m_w_s, b_s=m_b_s, w_proj_a=m_w_proj_a,
             w_proj_b=m_w_proj_b, w_proj_c=m_w_proj_c, w_out=m_w_out, norm2=m_norm2, w_up=m_w_up, conv_w=m_conv_w,
             conv_b=m_conv_b, w_down=m_w_down)
    v = dict(norm1=v_norm1, w_in=v_w_in, q_norm=v_q_norm, k_norm=v_k_norm, sinks=v_sinks, w_pool=v_w_pool,
             pool_scale=v_pool_scale, sgu_v_norm=v_sgu_v_norm, w_s=v_w_s, b_s=v_b_s, w_proj_a=v_w_proj_a,
             w_proj_b=v_w_proj_b, w_proj_c=v_w_proj_c, w_out=v_w_out, norm2=v_norm2, w_up=v_w_up, conv_w=v_conv_w,
             conv_b=v_conv_b, w_down=v_w_down)
    chip = 2 * lax.axis_index("x") + lax.axis_index("y")
    core = lax.axis_index("c")

    pos = jnp.stack([chip, core, 2 * chip + core]).astype(jnp.int32)
    comm = _Comm(w, pos)

    cos, sin = _rope_tables(positions[0])
    sp = {n: w[n] for n in SMALL if n != 'conv_w'}
    loss_row, dx, gs = _local_step(x[0], loss_target[0], cos, sin, sp, comm)

    delta, new_m, new_v, grad_out = {}, {}, {}, {}

    def adamw_big(names, grads):
        for n in names:
            shp = w[n].shape
            two_d = lambda a: a.reshape(shp[0] * shp[1], shp[2])
            d, nm, nv, g = _adamw(two_d(w[n]), two_d(grads[n]), two_d(m[n]), two_d(v[n]),
                                  tr=_row_tile(shp[0] * shp[1], 256), name=f"adamw_{n}", copy_g=True)
            delta[n], new_m[n], new_v[n], grad_out[n] = d.reshape(shp), nm.reshape(shp), nv.reshape(shp), g.reshape(shp)

    small_shapes = [gs[n].shape for n in SMALL] + [(1,)]
    small_pack = _pack([gs[n] for n in SMALL] + [loss_row[0, :1]])
    small = _Gather([_cast_place(small_pack, pos, F32, slots=N_DEV, which=2, name="small_place")], "small_gather",
                    all_devices=True)
    toks = comm.tail(REDUCE_TAIL_A[:1], (dx, small.token))
    comm.tail(REDUCE_TAIL_A[1:], (dx, *toks))
    comm.tail(REDUCE_TAIL_B, dx)
    adamw_big(GROUP_F, comm.shards())
    red = _sum_slots(small.wait(new_v[GROUP_F[-1]])[0], tr=small_pack.shape[0], name="small_sum")
    *small_grads, loss = _unpack(red, small_shapes)
    g_small = dict(zip(SMALL, small_grads))
    comm.tail(REDUCE_TAIL_C, red)
    grads = comm.shards()
    grads.update(g_small)
    shard_cols = conv_w.shape[2]
    grads['conv_w'] = lax.dynamic_slice_in_dim(g_small['conv_w'], chip * shard_cols, shard_cols, axis=2)

    adamw_big(GROUP_M, grads)
    shapes = [w[n].shape for n in SMALL]
    packs = [_pack([src[n] for n in SMALL]) for src in (w, grads, m, v)]
    d, nm, nv = _adamw(*packs, tr=packs[0].shape[0], name="adamw_small")
    for dst, src in ((delta, d), (new_m, nm), (new_v, nv)):
        dst.update(zip(SMALL, _unpack(src, shapes)))

    grads.update(grad_out)
    return (loss[0], dx[None], *[grads[n] for n in WEIGHTS], *[delta[n] for n in WEIGHTS],
            *[new_m[n] for n in WEIGHTS], *[new_v[n] for n in WEIGHTS])
```

```python
import functools
import math

import numpy as np
import jax
import jax.numpy as jnp
from jax import lax
from jax.experimental import pallas as pl
from jax.experimental.pallas import tpu as pltpu

F32 = jnp.float32
MXU_DTYPE = jnp.bfloat16
COMM_DTYPE = jnp.bfloat16
ACT_DTYPE = jnp.bfloat16
HALO = 16

D_MODEL = 1024
DEPTH = 2
HEAD_DIM = 64
POOL_WINDOWS = (2, 4, 8, 16)
POOL_WIDTH = 256
N_Q_HEADS = 8
ATTN_BLOCK = 128
ATTN_WIDTH = 512
KV_WIDTH = 128
CHUNK = 128
SGU_WIDTH = 256
IN_COLS = 4608
GATE_COL0 = 1536
D_FF = 2816
ROPE_THETA = 10000.0
EPS = 1e-6
ADAM_LR, ADAM_B1, ADAM_B2, ADAM_EPS, ADAM_WD, ADAM_STEP = 0.001, 0.9, 0.999, 1e-08, 0.01, 10

N_CHIPS = 4
N_DEV = 8
VMEM_LIMIT_BYTES = 56 * 1024 * 1024
NEG_BIG = -1e30
MESH = pl.DeviceIdType.MESH
ANY = pl.BlockSpec(memory_space=pl.ANY)

SDS = jax.ShapeDtypeStruct


def _cp(*sem):
    return pltpu.CompilerParams(dimension_semantics=sem, vmem_limit_bytes=VMEM_LIMIT_BYTES)


def _dot(a, b, dims=((1,), (0,))):
    return lax.dot_general(a.astype(MXU_DTYPE), b.astype(MXU_DTYPE), (dims, ((), ())),
                           preferred_element_type=F32)


NT = ((1,), (1,))
TN = ((0,), (0,))


def _split_dot(x, m):
    hi = x.astype(MXU_DTYPE)
    lo = (x - hi.astype(F32)).astype(MXU_DTYPE)
    return _dot(hi, m) + _dot(lo, m)


def _seg_matrix(width, seg):
    idx = np.arange(width) // seg
    return jnp.asarray((idx[:, None] == idx[None, :]).astype(np.float32), dtype=MXU_DTYPE)


def _lane(shape):
    return lax.broadcasted_iota(jnp.int32, shape, len(shape) - 1)


def _row(shape):
    return lax.broadcasted_iota(jnp.int32, shape, 0)


def _full(shape):
    nd = len(shape)
    return pl.BlockSpec(shape, lambda *_: (0,) * nd)


def _gelu(x):
    k = math.sqrt(2.0 / math.pi)
    th = jnp.tanh(k * (x + 0.044715 * (x * x * x)))
    return 0.5 * x * (1.0 + th)


def _gelu_and_grad(x):
    k = math.sqrt(2.0 / math.pi)
    x2 = x * x
    th = jnp.tanh(k * (x + 0.044715 * (x2 * x)))
    g = 0.5 * x * (1.0 + th)
    dg = 0.5 * (1.0 + th) + 0.5 * x * (1.0 - th * th) * (k * (1.0 + 3.0 * 0.044715 * x2))
    return g, dg


def _sigmoid(x):
    return 0.5 * jnp.tanh(0.5 * x) + 0.5


def _swap_halves(x):
    w = x.shape[-1]
    first = (_lane(x.shape) % HEAD_DIM) < (HEAD_DIM // 2)
    return jnp.where(first, pltpu.roll(x, w - HEAD_DIM // 2, 1), pltpu.roll(x, HEAD_DIM // 2, 1))


def _tile_lanes(x, reps):
    return x if reps == 1 else jnp.concatenate([x] * reps, axis=1)


def _fold_lanes(x, period):
    w = x.shape[-1]
    while w > period:
        w //= 2
        x = x + pltpu.roll(x, w, 1)
    return x


def _mm(a, b, *, mode, tm, tn, tk, out_dtype=F32, add=None, name,
        a_lead=None, b_lead=None, b_sharded=False, out_into=None,
        b_koff=0, out_joff=0, out_n=None, deps=()):
    ash = a.shape[1:] if a_lead is not None else a.shape
    bsh = b.shape[1:] if b_lead is not None else b.shape
    if b_sharded:
        bsh = (b.shape[1], N_CHIPS * b.shape[2])
    if mode == 'nn':
        (M, K), (K2, N) = ash, bsh
    elif mode == 'nt':
        (M, K), (N, K2) = ash, bsh
    else:
        (K, M), (K2, N) = ash, bsh
    assert K == K2 or (mode == 'nt' and K2 > K), (ash, bsh, mode)
    assert M % tm == 0 and N % tn == 0 and K % tk == 0, (M, N, K, tm, tn, tk)
    nk = K // tk
    dims = {'nn': ((1,), (0,)), 'nt': NT, 'tn': TN}[mode]

    def lead(spec_shape, imap, lead_idx):
        if lead_idx is None:
            return pl.BlockSpec(spec_shape, imap)
        return pl.BlockSpec((None,) + spec_shape, lambda i, j, k: (lead_idx,) + imap(i, j, k))

    if mode == 'tn':
        a_spec = lead((tk, tm), lambda i, j, k: (k, i), a_lead)
    else:
        a_spec = lead((tm, tk), lambda i, j, k: (i, k), a_lead)
    if b_sharded:
        per = b.shape[2] // (tk if mode == 'nt' else tn)
        assert per * (tk if mode == 'nt' else tn) == b.shape[2] and mode != 'tn'
        if mode == 'nt':
            b_spec = pl.BlockSpec((None, tn, tk), lambda i, j, k: ((k + b_koff) // per, j, (k + b_koff) % per))
        else:
            b_spec = pl.BlockSpec((None, tk, tn), lambda i, j, k: (j // per, k, j % per))
    elif mode == 'nt':
        b_spec = lead((tn, tk), lambda i, j, k: (j, k + b_koff), b_lead)
    else:
        b_spec = lead((tk, tn), lambda i, j, k: (k, j), b_lead)
    o_spec = pl.BlockSpec((tm, tn), lambda i, j, k: (i, j + out_joff))
    n_out = N if out_n is None else out_n
    in_specs = [a_spec, b_spec]
    operands = [a, b]
    if add is not None:
        in_specs.append(pl.BlockSpec((tm, tn), lambda i, j, k: (i, j)))
        operands.append(add)
    aliases = {}
    if out_into is not None:
        in_specs.append(ANY)
        operands.append(out_into)
        aliases = {len(operands) - 1: 0}
    in_specs += [ANY] * len(deps)
    operands += list(deps)
    has_add = add is not None
    acc_in_out = nk > 1 and out_dtype == F32

    def body(*refs):
        a_ref, b_ref = refs[0], refs[1]
        pos = 2
        add_ref = None
        if has_add:
            add_ref = refs[pos]
            pos += 1
        if out_into is not None:
            pos += 1
        pos += len(deps)
        o_ref = refs[pos]
        acc_ref = refs[pos + 1] if (nk > 1 and not acc_in_out) else None
        p = _dot(a_ref[...], b_ref[...], dims)
        if nk == 1:
            if has_add:
                p = p + add_ref[...]
            o_ref[...] = p.astype(o_ref.dtype)
            return
        k = pl.program_id(2)
        tgt = o_ref if acc_in_out else acc_ref

        @pl.when(k == 0)
        def _():
            tgt[...] = p + add_ref[...] if has_add else p

        @pl.when(k > 0)
        def _():
            tgt[...] += p

        if not acc_in_out:
            @pl.when(k == nk - 1)
            def _():
                o_ref[...] = acc_ref[...].astype(o_ref.dtype)

    out_shape = SDS((M, n_out), out_dtype)
    scratch = [pltpu.VMEM((tm, tn), F32)] if (nk > 1 and not acc_in_out) else []
    return pl.pallas_call(
        body, grid=(M // tm, N // tn, nk), in_specs=in_specs, out_specs=o_spec, out_shape=out_shape,
        scratch_shapes=scratch, input_output_aliases=aliases, name=name,
        compiler_params=_cp("parallel", "parallel", "arbitrary"))(*operands)


def _rms_bwd_rows(xv, g, dh, dres):
    r = lax.rsqrt(jnp.mean(xv * xv, axis=-1, keepdims=True) + EPS)
    xh = xv * r
    gy = dh * g
    dx = r * (gy - xh * jnp.mean(xh * gy, axis=-1, keepdims=True)) + dres
    return dx, jnp.sum(dh * xh, axis=0, keepdims=True)


def _mm_nt_sharded_rms(a, b, x, g, dres, *, tm, name, deps=()):
    a3 = a if a.ndim == 3 else a[None]
    A, M, ka = a3.shape
    S, N, ns = b.shape
    per = S // A
    assert ka == per * ns and M % tm == 0 and N == x.shape[1], (a3.shape, b.shape, x.shape)

    def body(a_ref, b_ref, x_ref, g_ref, dres_ref, dx_ref, dxb_ref, dg_ref):
        acc = None
        for s in range(S):
            lo = (s % per) * ns
            p = _dot(a_ref[s // per, :, lo:lo + ns], b_ref[s], NT)
            acc = p if acc is None else acc + p
        dx, dg = _rms_bwd_rows(x_ref[...], g_ref[...], acc, dres_ref[...])
        dx_ref[...] = dx
        dxb_ref[...] = dx.astype(dxb_ref.dtype)

        @pl.when(pl.program_id(0) == 0)
        def _():
            dg_ref[...] = jnp.zeros_like(dg_ref)
        dg_ref[...] += dg

    rows = pl.BlockSpec((tm, N), lambda i: (i, 0))
    return pl.pallas_call(
        _after(body, 5, deps), grid=(M // tm,),
        in_specs=[pl.BlockSpec((A, tm, ka), lambda i: (0, i, 0)),
                  pl.BlockSpec((S, N, ns), lambda i: (0, 0, 0), pipeline_mode=pl.Buffered(1)),
                  rows, _full((1, N)), rows] + [ANY] * len(deps),
        out_specs=[rows, rows, _full((1, N))],
        out_shape=[SDS((M, N), F32), SDS((M, N), MXU_DTYPE), SDS((1, N), F32)], name=name,
        compiler_params=_cp("arbitrary"))(a3, b, x, g, dres, *deps)


def _norm_mm(x, g, b, *, tm, tn, name, deps=()):
    M, K = x.shape
    S, K2, ns = b.shape
    per = ns // tn
    assert K == K2 and per * tn == ns and M % tm == 0, (x.shape, b.shape)

    def body(x_ref, g_ref, b_ref, o_ref, h_ref):
        @pl.when(pl.program_id(1) == 0)
        def _():
            xv = x_ref[...]
            r = lax.rsqrt(jnp.mean(xv * xv, axis=-1, keepdims=True) + EPS)
            h_ref[...] = (xv * r * g_ref[...]).astype(h_ref.dtype)
        o_ref[...] = _dot(h_ref[...], b_ref[...]).astype(o_ref.dtype)

    return pl.pallas_call(
        _after(body, 3, deps), grid=(M // tm, S * per),
        in_specs=[pl.BlockSpec((tm, K), lambda i, j: (i, 0)), _full((1, K)),
                  pl.BlockSpec((None, K, tn), lambda i, j: (j // per, 0, j % per))] + [ANY] * len(deps),
        out_specs=[pl.BlockSpec((tm, tn), lambda i, j: (i, j)), pl.BlockSpec((tm, K), lambda i, j: (i, 0))],
        out_shape=[SDS((M, S * ns), ACT_DTYPE), SDS((M, K), MXU_DTYPE)], name=name,
        compiler_params=_cp("parallel", "arbitrary"))(x, g, b, *deps)


def _after(body, n_in, deps):
    nd = len(deps)
    if nd == 0:
        return body
    return lambda *refs: body(*refs[:n_in], *refs[n_in + nd:])


def _down_proj_loss(act, w, x1, target, *, tm, name):
    T, K = act.shape
    D = w.shape[1]

    def body(a_ref, w_ref, x_ref, t_ref, loss_ref, dy_ref, dyb_ref):
        i = pl.program_id(0)
        d = (x_ref[...] + _dot(a_ref[...], w_ref[...])) - t_ref[...]
        dy = d * (1.0 / D)
        dy_ref[...] = dy
        dyb_ref[...] = dy.astype(dyb_ref.dtype)
        part = jnp.sum(jnp.sum(d * d, axis=1, keepdims=True), axis=0, keepdims=True) * (0.5 / D)

        @pl.when(i == 0)
        def _():
            loss_ref[...] = jnp.zeros_like(loss_ref)
        loss_ref[...] += jnp.broadcast_to(part, loss_ref.shape)

    rows = pl.BlockSpec((tm, D), lambda i: (i, 0))
    return pl.pallas_call(
        body, grid=(T // tm,), in_specs=[pl.BlockSpec((tm, K), lambda i: (i, 0)), _full((K, D)), rows, rows],
        out_specs=[_full((1, 128)), rows, rows],
        out_shape=[SDS((1, 128), F32), SDS((T, D), F32), SDS((T, D), MXU_DTYPE)],
        name=name, compiler_params=_cp("arbitrary"))(act, w, x1, target)


def _pool_lane_consts(shape):
    lane = _lane(shape)
    grp = lane // (POOL_WIDTH // 4)
    win = jnp.where(grp == 0, 2, jnp.where(grp == 1, 4, jnp.where(grp == 2, 8, 16)))
    return grp, win


def _pool_select(grp, s2, s4, s8, s16):
    return jnp.where(grp == 0, s2, jnp.where(grp == 1, s4, jnp.where(grp == 2, s8, s16)))


def _pool_diff(xe, row0, tr):
    s2 = xe + pltpu.roll(xe, 1, 0)
    s4 = s2 + pltpu.roll(s2, 2, 0)
    s8 = s4 + pltpu.roll(s4, 4, 0)
    s16 = s8 + pltpu.roll(s8, 8, 0)
    shape = (tr, POOL_WIDTH)
    grp, win = _pool_lane_consts(shape)
    sums = _pool_select(grp, s2[16:], s4[16:], s8[16:], s16[16:])
    t = row0 + _row(shape)
    cnt = jnp.minimum(t + 1, win).astype(F32)
    return sums / cnt - xe[16:]


def _pool_fwd(z, wbd, scale, *, tr, name):
    T = z.shape[0]
    hb = tr // 16

    def body(x_ref, xp_ref, w_ref, s_ref, o_ref):
        i = pl.program_id(0)
        halo = jnp.where(i == 0, 0.0, xp_ref[...].astype(F32))
        diff = _pool_diff(jnp.concatenate([halo, x_ref[...].astype(F32)], axis=0), i * tr, tr)
        o_ref[...] = (_dot(diff, w_ref[...]) * s_ref[...]).astype(o_ref.dtype)

    return pl.pallas_call(
        body, grid=(T // tr,),
        in_specs=[pl.BlockSpec((tr, POOL_WIDTH), lambda i: (i, 0)),
                  pl.BlockSpec((16, POOL_WIDTH), lambda i: (jnp.maximum(i * hb - 1, 0), 0)),
                  _full((POOL_WIDTH, POOL_WIDTH)), _full((1, POOL_WIDTH))],
        out_specs=pl.BlockSpec((tr, POOL_WIDTH), lambda i: (i, 0)),
        out_shape=SDS((T, POOL_WIDTH), MXU_DTYPE), name=name, compiler_params=_cp("parallel"))(z, z, wbd, scale)


def _pool_bwd_tile(i, n, tr, x, xprev, dpa, dpa_next, wbd, scale):
    halo = jnp.where(i == 0, 0.0, xprev)
    diff = _pool_diff(jnp.concatenate([halo, x], axis=0), i * tr, tr)
    mixed = _dot(diff, wbd)
    dscale = jnp.sum(dpa * mixed, axis=0, keepdims=True)
    dnext = jnp.where(i == n - 1, 0.0, dpa_next)
    dmix_e = jnp.concatenate([dpa, dnext], axis=0) * scale
    ddiff_e = _dot(dmix_e, wbd, NT)
    dwbd = _dot(diff, dmix_e[:tr], TN)
    shape = (tr + 16, POOL_WIDTH)
    grp, win = _pool_lane_consts(shape)
    t = i * tr + _row(shape)
    e = ddiff_e / jnp.minimum(t + 1, win).astype(F32)
    nrow = tr + 16
    a2 = e + pltpu.roll(e, nrow - 1, 0)
    a4 = a2 + pltpu.roll(a2, nrow - 2, 0)
    a8 = a4 + pltpu.roll(a4, nrow - 4, 0)
    a16 = a8 + pltpu.roll(a8, nrow - 8, 0)
    dx = _pool_select(grp, a2, a4, a8, a16)[:tr] - ddiff_e[:tr]
    return dx, dwbd, dscale


def _norm_rope(x, g, cos, sin_signed, seg):
    reps = x.shape[1] // 128
    ms = _split_dot(x * x, seg) * (1.0 / HEAD_DIM)
    r = lax.rsqrt(ms + EPS)
    xn = x * r * g
    c, s = _tile_lanes(cos, reps), _tile_lanes(sin_signed, reps)
    return xn * c + _swap_halves(xn) * s


def _norm_rope_bwd(x, g, cos, sin_signed, seg, dout):
    reps = x.shape[1] // 128
    c, s = _tile_lanes(cos, reps), _tile_lanes(sin_signed, reps)
    dxn = dout * c + _swap_halves(dout * s)
    ms = _split_dot(x * x, seg) * (1.0 / HEAD_DIM)
    r = lax.rsqrt(ms + EPS)
    xh = x * r
    gy = dxn * g
    dx = r * (gy - xh * (_split_dot(xh * gy, seg) * (1.0 / HEAD_DIM)))
    dg = jnp.sum(dxn * xh, axis=0, keepdims=True)
    return dx, dg


def _dup_heads(k):
    first = _lane(k.shape) < HEAD_DIM
    kr = pltpu.roll(k, HEAD_DIM, 1)
    return jnp.concatenate([jnp.where(first, k, kr), jnp.where(first, kr, k)], axis=1)


def _qkv_prep(z, cos, sin_signed, gq, gk, seg, *, tr, name):
    T = z.shape[0]

    def body(qa_ref, qb_ref, kv_ref, c_ref, s_ref, gq_ref, gk_ref, seg_ref, q_ref, k_ref, v_ref):
        c, s, seg_m = c_ref[...], s_ref[...], seg_ref[...]
        scale = HEAD_DIM ** -0.5
        qa = _norm_rope(qa_ref[...].astype(F32), gq_ref[...], c, s, seg_m) * scale
        qb = _norm_rope(qb_ref[...].astype(F32), gq_ref[...], c, s, seg_m) * scale
        q_ref[...] = jnp.concatenate([qa, qb], axis=1).astype(q_ref.dtype)
        kv = kv_ref[...].astype(F32)
        k = _norm_rope(kv[:, :KV_WIDTH], gk_ref[...], c, s, seg_m[:128, :128])
        k_ref[...] = _dup_heads(k).astype(k_ref.dtype)
        v_ref[...] = _dup_heads(kv[:, KV_WIDTH:]).astype(v_ref.dtype)

    col = lambda j: pl.BlockSpec((tr, 256), lambda i: (i, j))
    tab = pl.BlockSpec((tr, 128), lambda i: (i, 0))
    return pl.pallas_call(
        body, grid=(T // tr,),
        in_specs=[col(1), col(2), col(3), tab, tab, _full((1, 256)), _full((1, 128)), _full((256, 256))],
        out_specs=[pl.BlockSpec((tr, 512), lambda i: (i, 0)), col(0), col(0)],
        out_shape=[SDS((T, 512), MXU_DTYPE), SDS((T, 256), MXU_DTYPE), SDS((T, 256), MXU_DTYPE)],
        name=name, compiler_params=_cp("parallel"))(z, z, z, cos, sin_signed, gq, gk, seg)


GROUP_HEADS = 4
GROUP_ROWS = GROUP_HEADS * ATTN_BLOCK
ALL_ROWS = N_Q_HEADS * ATTN_BLOCK


def _attn_mask(has_prev):
    qi = _row((ALL_ROWS, 2 * ATTN_BLOCK)) % ATTN_BLOCK
    kj = _lane((ALL_ROWS, 2 * ATTN_BLOCK))
    return (kj > qi) & (kj <= qi + ATTN_BLOCK) & ((kj >= ATTN_BLOCK) | has_prev)


FWD_STEP_BLOCKS = 8
BWD_STEP_BLOCKS = 2


def _band(prev, cur, blk):
    lo = cur[(blk - 1) * ATTN_BLOCK:blk * ATTN_BLOCK] if blk else prev
    return jnp.concatenate([lo, cur[blk * ATTN_BLOCK:(blk + 1) * ATTN_BLOCK]], axis=0)


def _stack_heads(x, g):
    first = _lane((ATTN_BLOCK, 128)) < HEAD_DIM
    parts = []
    for pair in (2 * g, 2 * g + 1):
        x128 = x[:, 128 * pair:128 * (pair + 1)]
        zero = jnp.zeros_like(x128)
        parts += [jnp.where(first, x128, zero), jnp.where(first, zero, x128)]
    return jnp.concatenate(parts, axis=0)


def _unstack_heads(y):
    first = _lane((ATTN_BLOCK, 128)) < HEAD_DIM
    b = ATTN_BLOCK
    return jnp.concatenate([jnp.where(first, y[0:b], y[b:2 * b]), jnp.where(first, y[2 * b:3 * b], y[3 * b:4 * b])],
                           axis=1)


def _sink_col(sk_ref):
    return jnp.concatenate([jnp.broadcast_to(sk_ref[h:h + 1, 0:1], (ATTN_BLOCK, 1)) for h in range(N_Q_HEADS)],
                           axis=0)


def _by_group(a8, b2, dims=((1,), (0,))):
    return jnp.concatenate([_dot(a8[:GROUP_ROWS], b2[:, :128], dims), _dot(a8[GROUP_ROWS:], b2[:, 128:], dims)],
                           axis=0)


def _softmax_exp(q8, k2, mask, sink):
    s = jnp.where(mask, _by_group(q8, k2, NT), NEG_BIG)
    m = jnp.maximum(jnp.max(s, axis=1, keepdims=True), sink)
    p = jnp.exp(s - m)
    ps = jnp.exp(sink - m)
    return p, ps, 1.0 / (jnp.sum(p, axis=1, keepdims=True) + ps)


def _attn_fwd(q, k, v, sinks_b, *, name):
    T = q.shape[0]
    nb = T // ATTN_BLOCK
    STEP_BLOCKS = min(FWD_STEP_BLOCKS, nb)
    STEP_ROWS = STEP_BLOCKS * ATTN_BLOCK

    def body(q_ref, kc_ref, kp_ref, vc_ref, vp_ref, sk_ref, o_ref):
        n = pl.program_id(0)
        kc, kp, vc, vp = kc_ref[...], kp_ref[...], vc_ref[...], vp_ref[...]
        sink = _sink_col(sk_ref)
        for blk in range(STEP_BLOCKS):
            rows = slice(blk * ATTN_BLOCK, (blk + 1) * ATTN_BLOCK)
            mask = _attn_mask((n > 0) if blk == 0 else True)
            k2, v2 = _band(kp, kc, blk), _band(vp, vc, blk)
            qv = q_ref[rows, :]
            q8 = jnp.concatenate([_stack_heads(qv, 0), _stack_heads(qv, 1)], axis=0)
            p, _, inv = _softmax_exp(q8, k2, mask, sink)
            o8 = _by_group(p, v2) * inv
            o_ref[rows, :] = jnp.concatenate([_unstack_heads(o8[:GROUP_ROWS]), _unstack_heads(o8[GROUP_ROWS:])],
                                             axis=1).astype(o_ref.dtype)

    cur = lambda w: pl.BlockSpec((STEP_ROWS, w), lambda n: (n, 0))
    prev = lambda w: pl.BlockSpec((ATTN_BLOCK, w), lambda n: (jnp.maximum(STEP_BLOCKS * n - 1, 0), 0))
    return pl.pallas_call(
        body, grid=(nb // STEP_BLOCKS,),
        in_specs=[cur(512), cur(256), prev(256), cur(256), prev(256), _full((8, 128))],
        out_specs=cur(512), out_shape=SDS((T, 512), MXU_DTYPE), name=name,
        compiler_params=_cp("parallel"))(q, k, k, v, v, sinks_b)


def _attn_bwd(q, k, v, sinks_b, do, *, name, deps=()):
    T = q.shape[0]
    nb = T // ATTN_BLOCK
    STEP_BLOCKS = min(BWD_STEP_BLOCKS, nb)
    STEP_ROWS = STEP_BLOCKS * ATTN_BLOCK

    def body(q_ref, kc_ref, kp_ref, vc_ref, vp_ref, sk_ref, do_ref,
             dq_ref, dkc_ref, dkp_ref, dvc_ref, dvp_ref, dsk_ref):
        n = pl.program_id(0)
        kc, kp, vc, vp = kc_ref[...], kp_ref[...], vc_ref[...], vp_ref[...]
        sink = _sink_col(sk_ref)

        @pl.when(n == 0)
        def _():
            dsk_ref[...] = jnp.zeros_like(dsk_ref)

        for blk in range(STEP_BLOCKS):
            rows = slice(blk * ATTN_BLOCK, (blk + 1) * ATTN_BLOCK)
            mask = _attn_mask((n > 0) if blk == 0 else True)
            k2, v2 = _band(kp, kc, blk), _band(vp, vc, blk)
            qv, dov = q_ref[rows, :], do_ref[rows, :]
            q8 = jnp.concatenate([_stack_heads(qv, 0), _stack_heads(qv, 1)], axis=0)
            do8 = jnp.concatenate([_stack_heads(dov, 0), _stack_heads(dov, 1)], axis=0)
            p, ps, inv = _softmax_exp(q8, k2, mask, sink)
            pn = p * inv
            delta = jnp.sum(do8 * _by_group(pn, v2), axis=1, keepdims=True)
            ds = pn * (_by_group(do8, v2, NT) - delta)
            dq8 = _by_group(ds, k2)
            dq_ref[rows, :] = jnp.concatenate([_unstack_heads(dq8[:GROUP_ROWS]), _unstack_heads(dq8[GROUP_ROWS:])],
                                              axis=1)
            dk = jnp.concatenate([_dot(ds[:GROUP_ROWS], q8[:GROUP_ROWS], TN),
                                  _dot(ds[GROUP_ROWS:], q8[GROUP_ROWS:], TN)], axis=1)
            dv = jnp.concatenate([_dot(pn[:GROUP_ROWS], do8[:GROUP_ROWS], TN),
                                  _dot(pn[GROUP_ROWS:], do8[GROUP_ROWS:], TN)], axis=1)
            wsink = (ps * inv) * delta
            for h in range(N_Q_HEADS):
                dsink = -jnp.sum(wsink[ATTN_BLOCK * h:ATTN_BLOCK * (h + 1)], axis=0, keepdims=True)
                dsk_ref[h:h + 1, :] += jnp.broadcast_to(dsink, (1, 128))
            dkp_ref[rows, :] = dk[:ATTN_BLOCK]
            dkc_ref[rows, :] = dk[ATTN_BLOCK:]
            dvp_ref[rows, :] = dv[:ATTN_BLOCK]
            dvc_ref[rows, :] = dv[ATTN_BLOCK:]

    cur = lambda w: pl.BlockSpec((STEP_ROWS, w), lambda n: (n, 0))
    prev = lambda w: pl.BlockSpec((ATTN_BLOCK, w), lambda n: (jnp.maximum(STEP_BLOCKS * n - 1, 0), 0))
    f = lambda w: SDS((T, w), F32)
    return pl.pallas_call(
        _after(body, 7, deps), grid=(nb // STEP_BLOCKS,),
        in_specs=[cur(512), cur(256), prev(256), cur(256), prev(256), _full((8, 128)), cur(512)] + [ANY] * len(deps),
        out_specs=[cur(512), cur(256), cur(256), cur(256), cur(256), _full((8, 128))],
        out_shape=[f(512), f(256), f(256), f(256), f(256), SDS((8, 128), F32)],
        name=name, compiler_params=_cp("arbitrary"))(q, k, k, v, v, sinks_b, do, *deps)


def _mixer_ab_bwd(z, cos, sin_signed, gq, gk, seg, dq, dkc, dkp, dvc, dvp, dpa, wbd, scale, dz, *, tr, name, deps=()):
    T = z.shape[0]
    n = T // tr
    hb = tr // 16
    ab = tr // ATTN_BLOCK

    def unfold(cur, nxt_tile, nxt_halo, i):
        nxt = jnp.concatenate([nxt_tile[ATTN_BLOCK:], jnp.where(i == n - 1, 0.0, nxt_halo)], axis=0)
        tot = cur + nxt
        first = _lane((tr, 128)) < HEAD_DIM
        a = tot[:, :128]
        b = tot[:, 128:]
        a = a + pltpu.roll(a, HEAD_DIM, 1)
        b = b + pltpu.roll(b, HEAD_DIM, 1)
        return jnp.where(first, a, b)

    def body(xp_ref, xpp_ref, qa_ref, qb_ref, kv_ref, c_ref, s_ref, gq_ref, gk_ref, seg_ref,
             dq_ref, dkc_ref, dkp_ref, dkh_ref, dvc_ref, dvp_ref, dvh_ref, dpa_ref, dpan_ref, w_ref, sc_ref, _dz_in,
             dz_ref, dgq_ref, dgk_ref, dw_ref, dsc_ref):
        i = pl.program_id(0)
        c, s, seg_m = c_ref[...], s_ref[...], seg_ref[...]
        scale_q = HEAD_DIM ** -0.5
        dqv = dq_ref[...] * scale_q
        dxa, dga = _norm_rope_bwd(qa_ref[...].astype(F32), gq_ref[...], c, s, seg_m, dqv[:, :256])
        dxb, dgb = _norm_rope_bwd(qb_ref[...].astype(F32), gq_ref[...], c, s, seg_m, dqv[:, 256:])
        dk = unfold(dkc_ref[...], dkp_ref[...], dkh_ref[...], i)
        dv = unfold(dvc_ref[...], dvp_ref[...], dvh_ref[...], i)
        kv = kv_ref[...].astype(F32)
        dxk, dgk = _norm_rope_bwd(kv[:, :KV_WIDTH], gk_ref[...], c, s, seg_m[:128, :128], dk)
        dxp, dwbd, dscale = _pool_bwd_tile(i, n, tr, xp_ref[...].astype(F32), xpp_ref[...].astype(F32),
                                           dpa_ref[...], dpan_ref[...],
                                           w_ref[...], sc_ref[...])
        dz_ref[...] = jnp.concatenate([dxp, dxa, dxb, dxk, dv], axis=1).astype(dz_ref.dtype)

        @pl.when(i == 0)
        def _():
            dgq_ref[...] = jnp.zeros_like(dgq_ref)
            dgk_ref[...] = jnp.zeros_like(dgk_ref)
            dw_ref[...] = jnp.zeros_like(dw_ref)
            dsc_ref[...] = jnp.zeros_like(dsc_ref)
        dgq_ref[...] += _fold_lanes(dga + dgb, HEAD_DIM)
        dgk_ref[...] += _fold_lanes(dgk, HEAD_DIM)
        dw_ref[...] += dwbd
        dsc_ref[...] += dscale

    col = lambda j: pl.BlockSpec((tr, 256), lambda i: (i, j))
    rows = lambda w: pl.BlockSpec((tr, w), lambda i: (i, 0))
    nxt_blk = pl.BlockSpec((ATTN_BLOCK, 256), lambda i: (jnp.minimum((i + 1) * ab, T // ATTN_BLOCK - 1), 0))
    prev16 = pl.BlockSpec((16, 256), lambda i: (jnp.maximum(i * hb - 1, 0), 0))
    next16 = pl.BlockSpec((16, 256), lambda i: (jnp.minimum((i + 1) * hb, T // 16 - 1), 0))
    return pl.pallas_call(
        _after(body, 22, deps), grid=(n,),
        in_specs=[col(0), prev16, col(1), col(2), col(3), rows(128), rows(128),
                  _full((1, 256)), _full((1, 128)), _full((256, 256)),
                  rows(512), rows(256), rows(256), nxt_blk, rows(256), rows(256), nxt_blk,
                  rows(256), next16, _full((256, 256)), _full((1, 256)), ANY] + [ANY] * len(deps),
        out_specs=[rows(1024), _full((1, 256)), _full((1, 128)), _full((256, 256)), _full((1, 256))],
        out_shape=[SDS((T, IN_COLS), MXU_DTYPE), SDS((1, 256), F32), SDS((1, 128), F32),
                   SDS((256, 256), F32), SDS((1, 256), F32)],
        input_output_aliases={21: 0}, name=name, compiler_params=_cp("arbitrary"))(
            z, z, z, z, z, cos, sin_signed, gq, gk, seg, dq, dkc, dkp, dkp, dvc, dvp, dvp, dpa, dpa, wbd, scale, dz,
            *deps)


def _sgu_common(zu, zv, vn, seg):
    u, du = _gelu_and_grad(zu)
    gv, dgv = _gelu_and_grad(zv)
    ms = _split_dot(gv * gv, seg) * (1.0 / HEAD_DIM)
    r = lax.rsqrt(ms + EPS)
    xh = gv * r
    return u, du, dgv, r, xh, xh * vn


def _sgu_fwd(z, wtril, bexp, vn, seg, *, tr, name):
    T = z.shape[0]
    nch = tr // CHUNK

    def body(u_ref, v_ref, w_ref, b_ref, vn_ref, seg_ref, o_ref):
        u, _, _, _, _, vg = _sgu_common(u_ref[...].astype(F32), v_ref[...].astype(F32), vn_ref[...], seg_ref[...])
        grp = _lane((CHUNK, SGU_WIDTH)) // HEAD_DIM
        outs = []
        for ch in range(nch):
            vc = vg[ch * CHUNK:(ch + 1) * CHUNK]
            s = b_ref[...]
            for g in range(4):
                s = s + jnp.where(grp == g, _dot(w_ref[g], vc), 0.0)
            outs.append(u[ch * CHUNK:(ch + 1) * CHUNK] * s)
        o_ref[...] = jnp.concatenate(outs, axis=0).astype(o_ref.dtype)

    col = lambda j: pl.BlockSpec((tr, 256), lambda i: (i, j))
    return pl.pallas_call(
        body, grid=(T // tr,),
        in_specs=[col(4), col(5), _full((4, CHUNK, CHUNK)), _full((CHUNK, 256)), _full((1, 256)), _full((256, 256))],
        out_specs=col(0), out_shape=SDS((T, SGU_WIDTH), MXU_DTYPE), name=name,
        compiler_params=_cp("parallel"))(z, z, wtril, bexp, vn, seg)


def _mixers_fwd(z, cos, sin_signed, gq, gk, seg, wbd, scale, wtril, bexp, vn, *, tr, name):
    T = z.shape[0]
    hb = tr // 16
    nch = tr // CHUNK
    width = POOL_WIDTH + ATTN_WIDTH + 2 * KV_WIDTH + 2 * SGU_WIDTH

    def body(z_ref, xp_ref, c_ref, s_ref, gq_ref, gk_ref, seg_ref, w_ref, sc_ref, wt_ref, b_ref, vn_ref,
             pa_ref, q_ref, k_ref, v_ref, sg_ref):
        i = pl.program_id(0)
        zt = z_ref[...].astype(F32)
        seg_m = seg_ref[...]
        halo = jnp.where(i == 0, 0.0, xp_ref[...].astype(F32))
        diff = _pool_diff(jnp.concatenate([halo, zt[:, :POOL_WIDTH]], axis=0), i * tr, tr)
        pa_ref[...] = (_dot(diff, w_ref[...]) * sc_ref[...]).astype(pa_ref.dtype)
        c, s = c_ref[...], s_ref[...]
        scale_q = HEAD_DIM ** -0.5
        qa = _norm_rope(zt[:, 256:512], gq_ref[...], c, s, seg_m) * scale_q
        qb = _norm_rope(zt[:, 512:768], gq_ref[...], c, s, seg_m) * scale_q
        q_ref[...] = jnp.concatenate([qa, qb], axis=1).astype(q_ref.dtype)
        kk = _norm_rope(zt[:, 768:896], gk_ref[...], c, s, seg_m[:128, :128])
        k_ref[...] = _dup_heads(kk).astype(k_ref.dtype)
        v_ref[...] = _dup_heads(zt[:, 896:1024]).astype(v_ref.dtype)
        u, _, _, _, _, vg = _sgu_common(zt[:, 1024:1280], zt[:, 1280:1536], vn_ref[...], seg_m)
        grp = _lane((CHUNK, SGU_WIDTH)) // HEAD_DIM
        outs = []
        for ch in range(nch):
            vc = vg[ch * CHUNK:(ch + 1) * CHUNK]
            acc = b_ref[...]
            for g in range(4):
                acc = acc + jnp.where(grp == g, _dot(wt_ref[g], vc), 0.0)
            outs.append(u[ch * CHUNK:(ch + 1) * CHUNK] * acc)
        sg_ref[...] = jnp.concatenate(outs, axis=0).astype(sg_ref.dtype)

    rows = lambda w: pl.BlockSpec((tr, w), lambda i: (i, 0))
    return pl.pallas_call(
        body, grid=(T // tr,),
        in_specs=[rows(width), pl.BlockSpec((16, POOL_WIDTH), lambda i: (jnp.maximum(i * hb - 1, 0), 0)),
                  rows(128), rows(128), _full((1, 256)), _full((1, 128)), _full((256, 256)),
                  _full((POOL_WIDTH, POOL_WIDTH)), _full((1, POOL_WIDTH)),
                  _full((4, CHUNK, CHUNK)), _full((CHUNK, 256)), _full((1, 256))],
        out_specs=[rows(256), rows(512), rows(256), rows(256), rows(256)],
        out_shape=[SDS((T, 256), MXU_DTYPE), SDS((T, 512), MXU_DTYPE), SDS((T, 256), MXU_DTYPE),
                   SDS((T, 256), MXU_DTYPE), SDS((T, 256), MXU_DTYPE)],
        name=name, compiler_params=_cp("parallel"))(z, z, cos, sin_signed, gq, gk, seg, wbd, scale, wtril, bexp, vn)


def _sgu_bwd(z, wtril, bexp, vn, seg, dsg, dz, *, tr, name):
    T = z.shape[0]
    nch = tr // CHUNK

    def body(u_ref, v_ref, w_ref, b_ref, vn_ref, seg_ref, d_ref, _dz_in, dz_ref, dw_ref, db_ref, dvn_ref):
        i = pl.program_id(0)
        seg_m = seg_ref[...]
        vn_v = vn_ref[...]
        u, du, dgv, r, xh, vg = _sgu_common(u_ref[...].astype(F32), v_ref[...].astype(F32), vn_v, seg_m)
        d = d_ref[...]
        grp = _lane((CHUNK, SGU_WIDTH)) // HEAD_DIM
        tril = _row((CHUNK, CHUNK)) >= _lane((CHUNK, CHUNK))

        @pl.when(i == 0)
        def _():
            dw_ref[...] = jnp.zeros_like(dw_ref)
            db_ref[...] = jnp.zeros_like(db_ref)
            dvn_ref[...] = jnp.zeros_like(dvn_ref)

        dus, dvgs = [], []
        for ch in range(nch):
            sl = slice(ch * CHUNK, (ch + 1) * CHUNK)
            vc = vg[sl]
            s = b_ref[...]
            for g in range(4):
                s = s + jnp.where(grp == g, _dot(w_ref[g], vc), 0.0)
            dus.append(d[sl] * s)
            ds = d[sl] * u[sl]
            db_ref[...] += _split_dot(ds, seg_m)
            dvg = jnp.zeros((CHUNK, SGU_WIDTH), F32)
            for g in range(4):
                dsm = jnp.where(grp == g, ds, 0.0)
                dvg = dvg + jnp.where(grp == g, _dot(w_ref[g], ds, TN), 0.0)
                dw_ref[g] += jnp.where(tril, _dot(dsm, vc, NT), 0.0)
            dvgs.append(dvg)
        dup = jnp.concatenate(dus, axis=0)
        dvg = jnp.concatenate(dvgs, axis=0)
        dvn_ref[...] += _fold_lanes(jnp.sum(dvg * xh, axis=0, keepdims=True), HEAD_DIM)
        gy = dvg * vn_v
        dgvv = r * (gy - xh * (_split_dot(xh * gy, seg_m) * (1.0 / HEAD_DIM)))
        dz_ref[...] = jnp.concatenate([dup * du, dgvv * dgv], axis=1).astype(dz_ref.dtype)

    col = lambda j: pl.BlockSpec((tr, 256), lambda i: (i, j))
    return pl.pallas_call(
        body, grid=(T // tr,),
        in_specs=[col(4), col(5), _full((4, CHUNK, CHUNK)), _full((CHUNK, 256)), _full((1, 256)), _full((256, 256)),
                  col(0), ANY],
        out_specs=[pl.BlockSpec((tr, 512), lambda i: (i, 2)), _full((4, CHUNK, CHUNK)), _full((CHUNK, 256)),
                   _full((1, 256))],
        out_shape=[SDS((T, IN_COLS), MXU_DTYPE), SDS((4, CHUNK, CHUNK), F32), SDS((CHUNK, 256), F32),
                   SDS((1, 256), F32)],
        input_output_aliases={7: 0}, name=name, compiler_params=_cp("arbitrary"))(
            z, z, wtril, bexp, vn, seg, dsg, dz)


def _merge_fwd(pa, at, sg, wa, wb, wc, z, x, w_out, *, tm, tn, name):
    T = pa.shape[0]
    gb = GATE_COL0 // tn
    nb = D_MODEL // tn

    def body(pa_ref, at_ref, sg_ref, wa_ref, wb_ref, wc_ref, g0_ref, g1_ref, g2_ref, x_ref, wo_ref,
             m_ref, y_ref, x1_ref):
        j = pl.program_id(1)
        acc = None
        for idx, (op_ref, w_ref, g_ref) in enumerate(((pa_ref, wa_ref, g0_ref), (at_ref, wb_ref, g1_ref),
                                                      (sg_ref, wc_ref, g2_ref))):
            y = _dot(op_ref[...], w_ref[...])
            y_ref[idx] = y.astype(y_ref.dtype)
            t = _sigmoid(g_ref[...].astype(F32)) * y
            acc = t if acc is None else acc + t
        merged = acc.astype(m_ref.dtype)
        m_ref[...] = merged
        p = _dot(merged, wo_ref[...])

        @pl.when(j == 0)
        def _():
            x1_ref[...] = x_ref[...] + p

        @pl.when(j > 0)
        def _():
            x1_ref[...] += p

    op = lambda w: pl.BlockSpec((tm, w), lambda i, j: (i, 0))
    wt = lambda k: pl.BlockSpec((k, tn), lambda i, j: (0, j))
    gate = lambda b: pl.BlockSpec((tm, tn), lambda i, j: (i, gb + b * nb + j))
    return pl.pallas_call(
        body, grid=(T // tm, nb),
        in_specs=[op(256), op(512), op(256), wt(256), wt(512), wt(256), gate(0), gate(1), gate(2),
                  op(D_MODEL), pl.BlockSpec((tn, D_MODEL), lambda i, j: (j, 0))],
        out_specs=[pl.BlockSpec((tm, tn), lambda i, j: (i, j)), pl.BlockSpec((3, tm, tn), lambda i, j: (0, i, j)),
                   op(D_MODEL)],
        out_shape=[SDS((T, D_MODEL), MXU_DTYPE), SDS((3, T, D_MODEL), MXU_DTYPE), SDS((T, D_MODEL), F32)],
        name=name, compiler_params=_cp("parallel", "arbitrary"))(pa, at, sg, wa, wb, wc, z, z, z, x, w_out)


def _out_dx_merge_bwd(dxb, w_out, y, z, ws, xs, *, tm, tn, name):
    T = dxb.shape[0]
    gb = GATE_COL0 // tn
    nb = D_MODEL // tn
    nr = T // tm
    widths = [w.shape[0] for w in ws]

    def body(dx_ref, w_ref, y_ref, g_ref, *refs):
        w_refs, x_refs = refs[0:3], refs[3:6]
        dz_ref, dx_refs, dw_refs = refs[6], refs[7:10], refs[10:13]
        dm_ref, acc_refs = refs[13], refs[14:17]
        i, b, j = pl.program_id(0), pl.program_id(1), pl.program_id(2)

        @pl.when((b == 0) & (j == 0))
        def _():
            dm = _dot(dx_ref[...], w_ref[...], NT)
            for jj in range(nb):
                dm_ref[jj] = dm[:, jj * tn:(jj + 1) * tn]

        d = dm_ref[j]
        g = _sigmoid(g_ref[...].astype(F32))
        dy = (d * g).astype(MXU_DTYPE)
        dz_ref[...] = (d * y_ref[...].astype(F32) * g * (1.0 - g)).astype(dz_ref.dtype)
        for branch in range(3):
            @pl.when(b == branch)
            def _():
                p = _dot(dy, w_refs[branch][...], NT)
                q = _dot(x_refs[branch][...], dy, TN)

                @pl.when(j == 0)
                def _():
                    dx_refs[branch][...] = p

                @pl.when(j > 0)
                def _():
                    dx_refs[branch][...] += p

                @pl.when(i == 0)
                def _():
                    acc_refs[branch][j] = q

                @pl.when(i > 0)
                def _():
                    acc_refs[branch][j] += q

        @pl.when((i == nr - 1) & (b == 2) & (j == nb - 1))
        def _():
            for branch in range(3):
                for jj in range(nb):
                    dw_refs[branch][:, jj * tn:(jj + 1) * tn] = acc_refs[branch][jj]

    wspec = lambda k: pl.BlockSpec((k, tn), lambda i, b, j: (0, j))
    rows = lambda k: pl.BlockSpec((tm, k), lambda i, b, j: (i, 0))
    return pl.pallas_call(
        body, grid=(nr, 3, nb),
        in_specs=[rows(D_MODEL),
                  pl.BlockSpec((D_MODEL, D_MODEL), lambda i, b, j: (0, 0), pipeline_mode=pl.Buffered(1)),
                  pl.BlockSpec((None, tm, tn), lambda i, b, j: (b, i, j)),
                  pl.BlockSpec((tm, tn), lambda i, b, j: (i, gb + b * nb + j))]
        + [wspec(k) for k in widths] + [rows(k) for k in widths],
        out_specs=[pl.BlockSpec((tm, tn), lambda i, b, j: (i, gb + b * nb + j))]
        + [rows(k) for k in widths] + [_full((k, D_MODEL)) for k in widths],
        out_shape=[SDS((T, IN_COLS), MXU_DTYPE)] + [SDS((T, k), F32) for k in widths]
        + [SDS((k, D_MODEL), F32) for k in widths],
        scratch_shapes=[pltpu.VMEM((nb, tm, tn), F32)] + [pltpu.VMEM((nb, k, tn), F32) for k in widths],
        name=name, compiler_params=_cp("arbitrary", "arbitrary", "arbitrary"))(dxb, w_out, y, z, *ws, *xs)


def _conv3(xe, w, b):
    return (w[0:1] * pltpu.roll(xe, 2, 0) + w[1:2] * pltpu.roll(xe, 1, 0) + w[2:3] * xe)[8:] + b


def _conv_act_fwd(up, cw, cb, *, tr, tc, name):
    T = up.shape[0]
    nc = D_FF // tc
    hb = tr // HALO

    def body(ug_ref, ugp_ref, uv_ref, uvp_ref, wg_ref, wv_ref, bg_ref, bv_ref, o_ref):
        i = pl.program_id(1)
        first = i == 0

        def halo_tile(prev_ref, cur_ref):
            prev8 = prev_ref[...].astype(F32)[HALO - 8:]
            return jnp.concatenate([jnp.where(first, 0.0, prev8), cur_ref[...].astype(F32)], axis=0)

        cg = _conv3(halo_tile(ugp_ref, ug_ref), wg_ref[...], bg_ref[...])
        cv = _conv3(halo_tile(uvp_ref, uv_ref), wv_ref[...], bv_ref[...])
        o_ref[...] = (cg * _sigmoid(cg) * cv).astype(o_ref.dtype)

    tile = lambda off: pl.BlockSpec((tr, tc), lambda j, i: (i, off + j))
    prev = lambda off: pl.BlockSpec((HALO, tc), lambda j, i: (jnp.maximum(i * hb - 1, 0), off + j))
    par = lambda rows, off: pl.BlockSpec((rows, tc), lambda j, i: (0, off + j))
    return pl.pallas_call(
        body, grid=(nc, T // tr),
        in_specs=[tile(0), prev(0), tile(nc), prev(nc), par(3, 0), par(3, nc), par(1, 0), par(1, nc)],
        out_specs=pl.BlockSpec((tr, tc), lambda j, i: (i, j)),
        out_shape=SDS((T, D_FF), MXU_DTYPE), name=name,
        compiler_params=_cp("parallel", "parallel"))(up, up, up, up, cw, cw, cb, cb)


def _conv_act_bwd(up, cw, cb, dact, *, tr, tc, name, deps=()):
    T = up.shape[0]
    nc = D_FF // tc
    hb = tr // 8
    hbu = tr // HALO
    nr = T // tr

    def body(ug_ref, ugp_ref, ugn_ref, uv_ref, uvp_ref, uvn_ref, da_ref, dan_ref, wg_ref, wv_ref, bg_ref, bv_ref,
             du_ref, dwg_ref, dwv_ref, dbg_ref, dbv_ref):
        i = pl.program_id(1)
        first, last = i == 0, i == nr - 1
        da = jnp.concatenate([da_ref[...], jnp.where(last, 0.0, dan_ref[...])], axis=0)

        def with_halos(prev_ref, cur_ref, next_ref):
            prev8 = prev_ref[...].astype(F32)[HALO - 8:]
            next8 = next_ref[...].astype(F32)[:8]
            return jnp.concatenate([jnp.where(first, 0.0, prev8), cur_ref[...].astype(F32), next8], axis=0)

        uge = with_halos(ugp_ref, ug_ref, ugn_ref)
        uve = with_halos(uvp_ref, uv_ref, uvn_ref)
        wg, wv = wg_ref[...], wv_ref[...]
        ug1, ug2 = pltpu.roll(uge, 1, 0)[8:], pltpu.roll(uge, 2, 0)[8:]
        uv1, uv2 = pltpu.roll(uve, 1, 0)[8:], pltpu.roll(uve, 2, 0)[8:]
        cg = wg[0:1] * ug2 + wg[1:2] * ug1 + wg[2:3] * uge[8:] + bg_ref[...]
        cv = wv[0:1] * uv2 + wv[1:2] * uv1 + wv[2:3] * uve[8:] + bv_ref[...]
        sg = _sigmoid(cg)
        dcg = da * cv * (sg * (1.0 + cg * (1.0 - sg)))
        dcv = da * (cg * sg)
        nrow = tr + 8

        def back(dc, w):
            return (w[2:3] * dc + w[1:2] * pltpu.roll(dc, nrow - 1, 0) + w[0:1] * pltpu.roll(dc, nrow - 2, 0))[:tr]

        du_ref[0] = back(dcg, wg).astype(du_ref.dtype)
        du_ref[1] = back(dcv, wv).astype(du_ref.dtype)

        def wgrad(dc, u0, u1, u2):
            d = dc[:tr]
            rows = [jnp.sum(d * u2[:tr], axis=0, keepdims=True), jnp.sum(d * u1[:tr], axis=0, keepdims=True),
                    jnp.sum(d * u0[8:8 + tr], axis=0, keepdims=True)]
            return jnp.concatenate(rows, axis=0), jnp.sum(d, axis=0, keepdims=True)

        dwg, dbg = wgrad(dcg, uge, ug1, ug2)
        dwv, dbv = wgrad(dcv, uve, uv1, uv2)

        @pl.when(first)
        def _():
            dwg_ref[...] = jnp.zeros_like(dwg_ref)
            dwv_ref[...] = jnp.zeros_like(dwv_ref)
            dbg_ref[...] = jnp.zeros_like(dbg_ref)
            dbv_ref[...] = jnp.zeros_like(dbv_ref)
        dwg_ref[...] += dwg
        dwv_ref[...] += dwv
        dbg_ref[...] += dbg
        dbv_ref[...] += dbv

    tile = lambda off: pl.BlockSpec((tr, tc), lambda j, i: (i, off + j))
    prev = lambda off: pl.BlockSpec((HALO, tc), lambda j, i: (jnp.maximum(i * hbu - 1, 0), off + j))
    nxt = lambda off: pl.BlockSpec((HALO, tc), lambda j, i: (jnp.minimum((i + 1) * hbu, T // HALO - 1), off + j))
    dnext = pl.BlockSpec((8, tc), lambda j, i: (jnp.minimum((i + 1) * hb, T // 8 - 1), j))
    par = lambda rows, off: pl.BlockSpec((rows, tc), lambda j, i: (0, off + j))
    acc = lambda rows: pl.BlockSpec((rows, tc), lambda j, i: (0, j))
    return pl.pallas_call(
        _after(body, 12, deps), grid=(nc, nr),
        in_specs=[tile(0), prev(0), nxt(0), tile(nc), prev(nc), nxt(nc), tile(0), dnext,
                  par(3, 0), par(3, nc), par(1, 0), par(1, nc)] + [ANY] * len(deps),
        out_specs=[pl.BlockSpec((2, tr, tc), lambda j, i: (0, i, j)), acc(3), acc(3), acc(1), acc(1)],
        out_shape=[SDS((2, T, D_FF), MXU_DTYPE), SDS((3, D_FF), F32), SDS((3, D_FF), F32),
                   SDS((1, D_FF), F32), SDS((1, D_FF), F32)],
        name=name, compiler_params=_cp("parallel", "arbitrary"))(
            up, up, up, up, up, up, dact, dact, cw, cw, cb, cb, *deps)


def _row_tile(rows, cap):
    t = min(cap, rows)
    t -= t % 8
    while rows % t:
        t -= 8
    return t


def _adamw(w, g, m, v, *, tr, name, copy_g=False):
    R, C = w.shape
    assert R % tr == 0, (R, tr)

    def body(w_ref, g_ref, m_ref, v_ref, d_ref, nm_ref, nv_ref, *rest):
        gv = g_ref[...]
        mn = ADAM_B1 * m_ref[...] + (1.0 - ADAM_B1) * gv
        vn = ADAM_B2 * v_ref[...] + (1.0 - ADAM_B2) * (gv * gv)
        m_hat = mn / (1.0 - ADAM_B1 ** ADAM_STEP)
        v_hat = vn / (1.0 - ADAM_B2 ** ADAM_STEP)
        d_ref[...] = -ADAM_LR * (m_hat / (jnp.sqrt(v_hat) + ADAM_EPS) + ADAM_WD * w_ref[...])
        nm_ref[...] = mn
        nv_ref[...] = vn
        if copy_g:
            rest[0][...] = gv

    rows = pl.BlockSpec((tr, C), lambda i: (i, 0))
    n_out = 4 if copy_g else 3
    return pl.pallas_call(
        body, grid=(R // tr,), in_specs=[rows] * 4, out_specs=[rows] * n_out,
        out_shape=[SDS((R, C), F32)] * n_out, name=name, compiler_params=_cp("parallel"))(w, g, m, v)


def _sum_slots(r, *, tr, name):
    S, R, C = r.shape
    assert R % tr == 0, (R, tr)

    def body(r_ref, o_ref):
        acc = r_ref[0]
        for s in range(1, S):
            acc = acc + r_ref[s]
        o_ref[...] = acc

    return pl.pallas_call(
        body, grid=(R // tr,), in_specs=[pl.BlockSpec((S, tr, C), lambda i: (0, i, 0))],
        out_specs=pl.BlockSpec((tr, C), lambda i: (i, 0)), out_shape=SDS((R, C), F32),
        name=name, compiler_params=_cp("parallel"))(r)


def _pair_add(g4, h, pos, *, name):
    A, _, r, C = g4.shape
    cs = C if A == N_CHIPS else C // N_CHIPS
    tr = _row_tile(r, 256)
    if A == N_CHIPS:
        g_map, h_map = (lambda t, i, pos: (t, pos[1], i, 0)), (lambda t, i, pos: (t, i, 0))
    else:
        g_map, h_map = (lambda t, i, pos: (0, pos[1], i, t)), (lambda t, i, pos: (0, i, t))

    def body(pos_ref, g_ref, h_ref, o_ref):
        o_ref[...] = (g_ref[...] + h_ref[...]).astype(o_ref.dtype)

    grid_spec = pltpu.PrefetchScalarGridSpec(
        num_scalar_prefetch=1, grid=(N_CHIPS, r // tr),
        in_specs=[pl.BlockSpec((None, None, tr, cs), g_map), pl.BlockSpec((None, tr, cs), h_map)],
        out_specs=pl.BlockSpec((None, tr, cs), lambda t, i, pos: (t, i, 0)))
    return pl.pallas_call(body, grid_spec=grid_spec, out_shape=SDS((N_CHIPS, r, cs), COMM_DTYPE), name=name,
                          compiler_params=_cp("parallel", "parallel"))(pos, g4, h)


def _chip_sum(p, r2, f_into, pos, layer, *, name):
    _, r, cs = p.shape
    tr = _row_tile(r, 256)

    def body(pos_ref, own_ref, r_ref, *rest):
        o_ref = rest[-1]
        o_ref[...] = ((own_ref[...].astype(F32) + r_ref[0].astype(F32)) + r_ref[1].astype(F32)) + r_ref[2].astype(F32)

    in_specs = [pl.BlockSpec((None, tr, cs), lambda i, pos: (pos[0], i, 0)),
                pl.BlockSpec((3, tr, cs), lambda i, pos: (0, i, 0))]
    operands = [pos, p, r2]
    aliases = {}
    if f_into is not None:
        in_specs.append(ANY)
        operands.append(f_into)
        aliases = {3: 0}
    grid_spec = pltpu.PrefetchScalarGridSpec(
        num_scalar_prefetch=1, grid=(r // tr,), in_specs=in_specs,
        out_specs=pl.BlockSpec((None, None, tr, cs), lambda i, pos: (layer, pos[1], i, 0)))
    return pl.pallas_call(body, grid_spec=grid_spec, out_shape=SDS((DEPTH, 2, r, cs), F32), name=name,
                          input_output_aliases=aliases, compiler_params=_cp("parallel"))(*operands)


def _mesh_pos():
    return lax.axis_index("x"), lax.axis_index("y"), lax.axis_index("c")


HBM = pl.BlockSpec(memory_space=pltpu.HBM)
SEM = pl.BlockSpec(memory_space=pltpu.SEMAPHORE)
DATAFLOW = pltpu.SideEffectType.DATAFLOW_SIDE_EFFECTING
CHIP_FLIPS = (2, 1, 3)


def _chip_peers():
    x, y, c = _mesh_pos()
    return 2 * x + y, [(1 - x, y, c), (x, 1 - y, c), (1 - x, 1 - y, c)], (x, y, 1 - c), c


def _split_start(arrays, n_copies, issue, *, name, deps=()):
    k = len(arrays)
    nd = len(deps)

    def body(*refs):
        issue(refs[:k], refs[k + nd], refs[k + nd + 1])
        refs[2 * k + nd + 2][...] = jnp.zeros((8, 128), F32)

    out = pl.pallas_call(
        body, name=name,
        out_shape=(pltpu.SemaphoreType.DMA((n_copies,)), pltpu.SemaphoreType.DMA((n_copies,)),
                   *[pltpu.HBM(a.shape, a.dtype) for a in arrays], SDS((8, 128), F32)),
        in_specs=[HBM] * k + [ANY] * nd, out_specs=(SEM, SEM, *[HBM] * k, pl.BlockSpec(memory_space=pltpu.VMEM)),
        input_output_aliases={i: 2 + i for i in range(k)},
        compiler_params=pltpu.CompilerParams(has_side_effects=DATAFLOW))(
            *[pltpu.with_memory_space_constraint(a, pltpu.HBM) for a in arrays], *deps)
    return (out[0], out[1]), list(out[2:2 + k]), out[2 + k]


def _split_wait(sems, arrays, after, waits, *, name):
    k = len(arrays)
    afters = tuple(after) if isinstance(after, (tuple, list)) else (after,)

    def body(*refs):
        waits(refs[:k], refs[k], refs[k + 1])

    out = pl.pallas_call(
        body, name=name, out_shape=tuple(pltpu.HBM(a.shape, a.dtype) for a in arrays),
        in_specs=[HBM] * k + [SEM, SEM] + [ANY] * len(afters), out_specs=tuple([HBM] * k),
        input_output_aliases={i: i for i in range(k)},
        compiler_params=pltpu.CompilerParams(has_side_effects=DATAFLOW))(*arrays, sems[0], sems[1], *afters)
    return list(out)


def _wait_both(cp):
    cp.wait_send()
    cp.wait_recv()


def _cast_place(shard, pos, dtype, *, name, layer=None, slots=N_CHIPS, which=0):
    R, C = shard.shape[-2:]
    tr = R if R % 8 else _row_tile(R, 256)
    if layer is None:
        in_spec = pl.BlockSpec((tr, C), lambda i, pos: (i, 0))
    else:
        in_spec = pl.BlockSpec((None, tr, C), lambda i, pos: (layer, i, 0))

    def body(pos_ref, x_ref, o_ref):
        o_ref[...] = x_ref[...].astype(o_ref.dtype)

    grid_spec = pltpu.PrefetchScalarGridSpec(
        num_scalar_prefetch=1, grid=(R // tr,), in_specs=[in_spec],
        out_specs=pl.BlockSpec((None, tr, C), lambda i, pos: (pos[which], i, 0)))
    return pl.pallas_call(body, grid_spec=grid_spec, out_shape=SDS((slots, R, C), dtype), name=name,
                          compiler_params=_cp("parallel"))(pos, shard)


def _device_peers():
    x, y, c = _mesh_pos()
    peers = [(x ^ ((f >> 2) & 1), y ^ ((f >> 1) & 1), c ^ (f & 1)) for f in range(1, N_DEV)]
    return 4 * x + 2 * y + c, peers


class _Gather:
    def __init__(self, lands, name, deps=(), all_devices=False, halves=False):
        n = len(lands)
        self.name, self.halves = name, halves
        npeer = N_DEV - 1 if all_devices else N_CHIPS - 1
        if halves:
            lands = [a.reshape(a.shape[0], 2, a.shape[1] // 2, a.shape[2]) for a in lands]

        def copies(refs, ss, rs):
            if halves:
                me, peers, _, c = _chip_peers()
                own = lambda r: r.at[me, c]
            else:
                me, peers = _device_peers() if all_devices else _chip_peers()[:2]
                own = lambda r: r.at[me]
            return [pltpu.make_async_remote_copy(
                src_ref=own(refs[w]), dst_ref=own(refs[w]), send_sem=ss.at[npeer * w + p],
                recv_sem=rs.at[npeer * w + p], device_id=peers[p], device_id_type=MESH)
                for w in range(n) for p in range(npeer)]

        def issue(refs, ss, rs):
            for cp in copies(refs, ss, rs):
                cp.start()

        def waits(refs, ss, rs):
            for cp in copies(refs, ss, rs):
                _wait_both(cp)

        self._waits = waits
        self.sems, self.arrays, self.token = _split_start(list(lands), npeer * n, issue, name=name + "_start",
                                                          deps=deps)

    def wait(self, after):
        arrays = _split_wait(self.sems, self.arrays, after, self._waits, name=self.name + "_wait")
        if not self.halves:
            return arrays
        n = len(arrays)

        def copies(refs, ss, rs):
            me, _, sibling, c = _chip_peers()
            return [pltpu.make_async_remote_copy(
                src_ref=refs[w].at[me ^ CHIP_FLIPS[p], c], dst_ref=refs[w].at[me ^ CHIP_FLIPS[p], c],
                send_sem=ss.at[3 * w + p], recv_sem=rs.at[3 * w + p], device_id=sibling, device_id_type=MESH)
                for w in range(n) for p in range(3)]

        def issue(refs, ss, rs):
            for cp in copies(refs, ss, rs):
                cp.start()

        def waits(refs, ss, rs):
            for cp in copies(refs, ss, rs):
                _wait_both(cp)

        sems, arrays, _ = _split_start(arrays, 3 * n, issue, name=self.name + "_share_start")
        arrays = _split_wait(sems, arrays, after, waits, name=self.name + "_share_wait")
        return [a.reshape(a.shape[0], 2 * a.shape[2], a.shape[3]) for a in arrays]


def _swap_halves_start(g4s, *, name):
    n = len(g4s)
    lands = [lax.empty((g.shape[0],) + g.shape[2:], g.dtype) for g in g4s]

    def copies(refs, ss, rs):
        _, _, sibling, c = _chip_peers()
        return [pltpu.make_async_remote_copy(
            src_ref=refs[w].at[:, 1 - c], dst_ref=refs[n + w], send_sem=ss.at[w], recv_sem=rs.at[w],
            device_id=sibling, device_id_type=MESH) for w in range(n)]

    def issue(refs, ss, rs):
        for cp in copies(refs, ss, rs):
            cp.start()

    def waits(refs, ss, rs):
        for cp in copies(refs, ss, rs):
            _wait_both(cp)

    sems, arrays, token = _split_start(list(g4s) + lands, n, issue, name=name + "_start")
    return sems, arrays, token, waits


def _scatter_start(parts, *, name, deps=()):
    n = len(parts)
    lands = [lax.empty((3,) + p.shape[1:], p.dtype) for p in parts]

    def copies(refs, ss, rs):
        me, peers, _, _ = _chip_peers()
        return [pltpu.make_async_remote_copy(
            src_ref=refs[w].at[me ^ CHIP_FLIPS[p]], dst_ref=refs[n + w].at[p],
            send_sem=ss.at[3 * w + p], recv_sem=rs.at[3 * w + p], device_id=peers[p], device_id_type=MESH)
            for w in range(n) for p in range(3)]

    def issue(refs, ss, rs):
        for cp in copies(refs, ss, rs):
            cp.start()

    def waits(refs, ss, rs):
        for cp in copies(refs, ss, rs):
            _wait_both(cp)

    sems, arrays, token = _split_start(list(parts) + lands, 3 * n, issue, name=name + "_start", deps=deps)
    return sems, arrays, token, waits


def _pair_share_start(fs, layer, *, name):
    n = len(fs)

    def copies(refs, ss, rs):
        _, _, sibling, c = _chip_peers()
        return [pltpu.make_async_remote_copy(
            src_ref=refs[w].at[layer, c], dst_ref=refs[w].at[layer, c], send_sem=ss.at[w], recv_sem=rs.at[w],
            device_id=sibling, device_id_type=MESH) for w in range(n)]

    def issue(refs, ss, rs):
        for cp in copies(refs, ss, rs):
            cp.start()

    def waits(refs, ss, rs):
        for cp in copies(refs, ss, rs):
            _wait_both(cp)

    sems, arrays, token = _split_start(list(fs), n, issue, name=name + "_start")
    return sems, arrays, token, waits


BIG = ('w_in', 'w_proj_a', 'w_proj_b', 'w_proj_c', 'w_out', 'w_up', 'w_down')
BIG_SHARD_AXIS = {'w_in': 2, 'w_proj_a': 2, 'w_proj_b': 2, 'w_proj_c': 2, 'w_out': 1, 'w_up': 2, 'w_down': 1}
SMALL = ('norm1', 'q_norm', 'k_norm', 'sinks', 'w_pool', 'pool_scale', 'sgu_v_norm', 'w_s', 'b_s', 'norm2',
         'conv_b', 'conv_w')
WEIGHTS = ('norm1', 'w_in', 'q_norm', 'k_norm', 'sinks', 'w_pool', 'pool_scale', 'sgu_v_norm', 'w_s', 'b_s',
           'w_proj_a', 'w_proj_b', 'w_proj_c', 'w_out', 'norm2', 'w_up', 'conv_w', 'conv_b', 'w_down')


def _rope_tables(positions):
    inv_freq = ROPE_THETA ** (-jnp.arange(0, HEAD_DIM, 2, dtype=F32) / HEAD_DIM)
    ang = positions.astype(F32)[:, None] * inv_freq
    cos, sin = jnp.cos(ang), jnp.sin(ang)
    c = jnp.concatenate([cos, cos], axis=1)
    s = jnp.concatenate([-sin, sin], axis=1)
    return jnp.concatenate([c, c], axis=1), jnp.concatenate([s, s], axis=1)


def _block_diag4(w):
    out = jnp.zeros((POOL_WIDTH, POOL_WIDTH), w.dtype)
    for g in range(4):
        out = lax.dynamic_update_slice(out, w[g], (g * HEAD_DIM, g * HEAD_DIM))
    return out


def _local_step(x, target, cos, sin, sp, sched):
    T = x.shape[0]
    tm1 = min(1024, T)
    tm = min(512, T)
    tr = min(1024, T)
    trc = min(512, T)
    tkt = min(2048, T)
    seg = _seg_matrix(256, HEAD_DIM)
    saved = []
    xl = x
    for l in range(DEPTH):
        p = f"l{l}_"
        c = dict(
            g1=sp['norm1'][l][None], g2=sp['norm2'][l][None],
            wbd=_block_diag4(sp['w_pool'][l]).astype(MXU_DTYPE), scale=sp['pool_scale'][l][None],
            gq=jnp.tile(sp['q_norm'][l], 4)[None], gk=jnp.tile(sp['k_norm'][l], 2)[None],
            sinks=jnp.broadcast_to(sp['sinks'][l][:, None], (N_Q_HEADS, 128)),
            wtril=jnp.tril(sp['w_s'][l]).astype(MXU_DTYPE),
            bexp=jnp.repeat(sp['b_s'][l].T, HEAD_DIM, axis=1), vn=jnp.tile(sp['sgu_v_norm'][l], 4)[None],
            cb=sp['conv_b'][l][None])
        c['w_in'] = sched.weight('w_in', l, xl)
        z, h1 = _norm_mm(xl, c['g1'], c['w_in'], tm=tm1, tn=1152, name=p + "in_proj",
                         deps=sched.start_tokens() if l == 0 else ())
        pa, q, k, v, sg = _mixers_fwd(z, cos, sin, c['gq'], c['gk'], seg, c['wbd'], c['scale'], c['wtril'],
                                      c['bexp'], c['vn'], tr=tr, name=p + "mixers")
        at = _attn_fwd(q, k, v, c['sinks'], name=p + "attn")
        for n in ('w_proj_a', 'w_proj_b', 'w_proj_c', 'w_out'):
            c[n] = sched.weight(n, l, (pa, at, sg))
        merged, y3, x1 = _merge_fwd(pa, at, sg, c['w_proj_a'], c['w_proj_b'], c['w_proj_c'], z, xl, c['w_out'],
                                    tm=tm1, tn=512, name=p + "merge_out_proj")
        for n in ('w_up', 'conv_w', 'w_down'):
            c[n] = sched.weight(n, l, x1)
        up, h2 = _norm_mm(x1, c['g2'], c['w_up'], tm=tm1, tn=1408, name=p + "up_proj")
        act = _conv_act_fwd(up, c['conv_w'], c['cb'], tr=trc, tc=1408, name=p + "conv_act")
        saved.append(dict(c, x=xl, h1=h1, z=z, pa=pa, q=q, k=k, v=v, at=at, sg=sg, merged=merged, y3=y3,
                          x1=x1, h2=h2, up=up, act=act))
        if l < DEPTH - 1:
            xl = _mm(act, c['w_down'], mode='nn', add=x1, tm=tm, tn=D_MODEL, tk=D_FF, name=p + "down_proj")
        else:
            loss_row, dx, dxb = _down_proj_loss(act, c['w_down'], x1, target, tm=tm, name=p + "down_proj_loss")

    gs = {n: [None] * DEPTH for n in SMALL}
    for l in reversed(range(DEPTH)):
        p = f"l{l}_b_"
        s = saved[l]
        gb = {}
        dact = _mm(dxb, s['w_down'], mode='nt', tm=tm1, tn=1408, tk=D_MODEL, name=p + "down_dx")
        gb['w_down'] = _mm(s['act'], dxb, mode='tn', tm=1408, tn=D_MODEL, tk=tkt, name=p + "down_dw")
        toks = sched.slot(l, 'down', gb['w_down'])
        dup, dwg, dwv, dbg, dbv = _conv_act_bwd(s['up'], s['conv_w'], s['cb'], dact, tr=min(1024, T), tc=256,
                                                name=p + "conv_act", deps=toks)
        gs['conv_w'][l] = jnp.concatenate([dwg, dwv], axis=1)
        gs['conv_b'][l] = jnp.concatenate([dbg, dbv], axis=1)[0]
        toks = sched.slot(l, 'conv', dup)
        for half in range(2):
            gb['w_up'] = _mm(s['h2'], dup, mode='tn', b_lead=half, tm=D_MODEL, tn=1408, tk=tkt,
                             out_into=gb.get('w_up'), out_joff=2 * half, out_n=2 * D_FF, name=p + f"up_dw{half}",
                             deps=toks if half == 0 else ())
        toks = sched.slot(l, 'ffn', gb['w_up'], gb)
        dx1, dx1b, dg2 = _mm_nt_sharded_rms(dup, s['w_up'], s['x1'], s['g2'], dx, tm=tm,
                                            name=p + "up_dx_rms2", deps=toks)
        gs['norm2'][l] = dg2[0]
        gb['w_out'] = _mm(s['merged'], dx1b, mode='tn', tm=D_MODEL, tn=D_MODEL, tk=tkt, name=p + "out_dw")
        (dz, dpa, dat, dsg, gb['w_proj_a'], gb['w_proj_b'], gb['w_proj_c']) = _out_dx_merge_bwd(
            dx1b, s['w_out'], s['y3'], s['z'], [s['w_proj_a'], s['w_proj_b'], s['w_proj_c']],
            [s['pa'], s['at'], s['sg']], tm=tm1, tn=512, name=p + "out_dx_merge")
        toks = sched.slot(l, 'mid', dz)
        dq, dkc, dkp, dvc, dvp, dsk = _attn_bwd(s['q'], s['k'], s['v'], s['sinks'], dat, name=p + "attn", deps=toks)
        gs['sinks'][l] = dsk[:, 0]
        toks = sched.slot(l, 'attn', dq)
        dz, dgq, dgk, dwbd, dsc = _mixer_ab_bwd(s['z'], cos, sin, s['gq'], s['gk'], seg, dq, dkc, dkp, dvc, dvp,
                                                dpa, s['wbd'], s['scale'], dz, tr=tr, name=p + "qkv_pool", deps=toks)
        gs['q_norm'][l] = dgq[0, :HEAD_DIM]
        gs['k_norm'][l] = dgk[0, :HEAD_DIM]
        gs['w_pool'][l] = jnp.stack([dwbd[g * HEAD_DIM:(g + 1) * HEAD_DIM, g * HEAD_DIM:(g + 1) * HEAD_DIM]
                                     for g in range(4)])
        gs['pool_scale'][l] = dsc[0]
        dz, dws, dbrows, dvn = _sgu_bwd(s['z'], s['wtril'], s['bexp'], s['vn'], seg, dsg, dz, tr=tr, name=p + "sgu")
        gs['w_s'][l] = dws
        gs['b_s'][l] = dbrows[:, ::HEAD_DIM].T
        gs['sgu_v_norm'][l] = dvn[0, :HEAD_DIM]
        gb['w_in'] = _mm(s['h1'], dz, mode='tn', tm=D_MODEL, tn=1152, tk=tkt, name=p + "in_dw")
        toks = sched.slot(l, 'mix', gb['w_in'], gb)
        dx, dxb, dg1 = _mm_nt_sharded_rms(dz, s['w_in'], s['x'], s['g1'], dx1, tm=tm,
                                          name=p + "in_dx_rms1", deps=toks)
        gs['norm1'][l] = dg1[0]
    gs = {n: jnp.stack(v) for n, v in gs.items()}
    return loss_row, dx, gs


GROUP_F = ('w_down', 'w_up')
GROUP_M = ('w_out', 'w_proj_a', 'w_proj_b', 'w_proj_c', 'w_in')
ROW_SHARDED = ('w_out', 'w_down')

REDUCE_PLAN = {
    (1, 'ffn'): (('S1', 'F', 1),),
    (1, 'mid'): (('W1', 'F', 1),),
    (1, 'mix'): (('S1', 'M', 1),),
    (0, 'down'): (('W1', 'M', 1),),
    (0, 'conv'): (('W2', 'F', 1),),
    (0, 'ffn'): (('S1', 'F', 0), ('W3', 'F', 1)),
    (0, 'mid'): (('W1', 'F', 0),),
    (0, 'attn'): (('W2', 'M', 1),),
    (0, 'mix'): (('S1', 'M', 0), ('W3', 'M', 1)),
}
REDUCE_TAIL_A = (('W1', 'M', 0), ('W2', 'F', 0))
REDUCE_TAIL_B = (('W3', 'F', 0),)
REDUCE_TAIL_C = (('W2', 'M', 0), ('W3', 'M', 0))


class _Comm:
    def __init__(self, w, pos):
        self.pos = pos
        groups = {'a': [('w_in', 0)],
                  'b': [(n, 0) for n in ('w_proj_a', 'w_proj_b', 'w_proj_c', 'w_out')],
                  'c': [(n, 0) for n in ('w_up', 'conv_w', 'w_down')],
                  'd': [(n, 1) for n in BIG] + [('conv_w', 1)]}
        self.gathers, self.group_of, self.weights = {}, {}, {}
        self.tokens = []
        for g, ks in groups.items():
            lands = [_cast_place(w[n], pos, F32 if n == 'conv_w' else MXU_DTYPE, layer=l, name=f"gw_place_{n}{l}")
                     for n, l in ks]
            self.gathers[g] = (_Gather(lands, "gw_" + g, deps=self.tokens[-1:], halves=(g == 'a')), ks)
            self.tokens.append(self.gathers[g][0].token)
            self.group_of.update({k: g for k in ks})
        self.red = {}
        self.final = {}

    def start_tokens(self):
        return self.tokens[-1:]

    def weight(self, name, layer, after):
        if (name, layer) not in self.weights:
            gather, ks = self.gathers[self.group_of[(name, layer)]]
            for (n, l), full in zip(ks, gather.wait(after)):
                if n == 'conv_w' or n.startswith('w_proj'):
                    full = full.transpose(1, 0, 2).reshape(full.shape[1], -1)
                elif n in ROW_SHARDED:
                    full = full.reshape(-1, full.shape[2])
                self.weights[(n, l)] = full
        return self.weights[(name, layer)]

    def slot(self, layer, slot, after, grads=None):
        tokens = []
        for step, grp, lyr in REDUCE_PLAN.get((layer, slot), ()):
            tok = self._step(step, grp, lyr, after, grads)
            if tok is not None:
                tokens.append(tok)
        return tokens

    def tail(self, steps, after, deps=()):
        toks = (self._step(step, grp, lyr, after, None, deps) for step, grp, lyr in steps)
        return [t for t in toks if t is not None]

    def shards(self):
        return {n: f.reshape(DEPTH, 2 * f.shape[2], f.shape[3]) for n, f in self.final.items()}

    def _step(self, step, grp, layer, after, grads, deps=()):
        names = GROUP_F if grp == 'F' else GROUP_M
        tag = f"{grp.lower()}{layer}"
        st = self.red.setdefault((grp, layer), {})
        n = len(names)
        if step == 'S1':
            g4s = []
            for nm in names:
                g = grads[nm]
                R, C = g.shape
                g4s.append(g.reshape(N_CHIPS, 2, R // (2 * N_CHIPS), C) if nm in ROW_SHARDED
                           else g.reshape(1, 2, R // 2, C))
            st['s1'] = _swap_halves_start(g4s, name="rs1_" + tag)
            return st['s1'][2]
        if step == 'W1':
            sems, arrays, _, waits = st.pop('s1')
            arrays = _split_wait(sems, arrays, after, waits, name=f"rs1_{tag}_wait")
            parts = [_pair_add(arrays[i], arrays[n + i], self.pos, name=f"pair_add_{tag}_{names[i]}")
                     for i in range(n)]
            st['s2'] = _scatter_start(parts, name="rs2_" + tag, deps=deps)
            return st['s2'][2]
        if step == 'W2':
            sems, arrays, _, waits = st.pop('s2')
            arrays = _split_wait(sems, arrays, after, waits, name=f"rs2_{tag}_wait")
            fs = [_chip_sum(arrays[i], arrays[n + i], self.final.get(names[i]), self.pos, layer,
                            name=f"chip_sum_{tag}_{names[i]}") for i in range(n)]
            st['s3'] = _pair_share_start(fs, layer, name="rs3_" + tag)
            return st['s3'][2]
        sems, arrays, _, waits = st.pop('s3')
        self.final.update(zip(names, _split_wait(sems, arrays, after, waits, name=f"rs3_{tag}_wait")))
        return None


def _pack(arrays):
    rows = []
    for a in arrays:
        nel = int(np.prod(a.shape))
        if nel % 1024 == 0:
            rows.append(a.astype(F32).reshape(nel // 128, 128))
        else:
            f = a.reshape(-1).astype(F32)
            rows.append(jnp.pad(f, (0, (-nel) % 1024)).reshape(-1, 128))
    return jnp.concatenate(rows, axis=0)


def _unpack(pack, shapes):
    out, row = [], 0
    for shp in shapes:
        nel = int(np.prod(shp))
        nrow = 8 * -(-nel // 1024)
        part = pack[row:row + nrow]
        out.append(part.reshape(shp) if nel % 1024 == 0 else part.reshape(-1)[:nel].reshape(shp))
        row += nrow
    return out


def kernel(x, positions, norm1, w_in, q_norm, k_norm, sinks, w_pool, pool_scale, sgu_v_norm, w_s, b_s, w_proj_a, w_proj_b, w_proj_c, w_out, norm2, w_up, conv_w, conv_b, w_down, loss_target, m_norm1, m_w_in, m_q_norm, m_k_norm, m_sinks, m_w_pool, m_pool_scale, m_sgu_v_norm, m_w_s, m_b_s, m_w_proj_a, m_w_proj_b, m_w_proj_c, m_w_out, m_norm2, m_w_up, m_conv_w, m_conv_b, m_w_down, v_norm1, v_w_in, v_q_norm, v_k_norm, v_sinks, v_w_pool, v_pool_scale, v_sgu_v_norm, v_w_s, v_b_s, v_w_proj_a, v_w_proj_b, v_w_proj_c, v_w_out, v_norm2, v_w_up, v_conv_w, v_conv_b, v_w_down):
    w = dict(norm1=norm1, w_in=w_in, q_norm=q_norm, k_norm=k_norm, sinks=sinks, w_pool=w_pool, pool_scale=pool_scale,
             sgu_v_norm=sgu_v_norm, w_s=w_s, b_s=b_s, w_proj_a=w_proj_a, w_proj_b=w_proj_b, w_proj_c=w_proj_c,
             w_out=w_out, norm2=norm2, w_up=w_up, conv_w=conv_w, conv_b=conv_b, w_down=w_down)
    m = dict(norm1=m_norm1, w_in=m_w_in, q_norm=m_q_norm, k_norm=m_k_norm, sinks=m_sinks, w_pool=m_w_pool,
             pool_scale=m_pool_scale, sgu_v_norm=m_sgu_v_norm, w_s=m_w_s, b_s=m_b_s, w_proj_a=m_w_proj_a,
             w_proj_b=m_w_proj_b, w_proj_c=m_w_proj_c, w_out=m_w_out, norm2=m_norm2, w_up=m_w_up, conv_w=m_conv_w,
             conv_b=m_conv_b, w_down=m_w_down)
    v = dict(norm1=v_norm1, w_in=v_w_in, q_norm=v_q_norm, k_norm=v_k_norm, sinks=v_sinks, w_pool=v_w_pool,
             pool_scale=v_pool_scale, sgu_v_norm=v_sgu_v_norm, w_s=v_w_s, b_s=v_b_s, w_proj_a=v_w_proj_a,
             w_proj_b=v_w_proj_b, w_proj_c=v_w_proj_c, w_out=v_w_out, norm2=v_norm2, w_up=v_w_up, conv_w=v_conv_w,
             conv_b=v_conv_b, w_down=v_w_down)
    chip = 2 * lax.axis_index("x") + lax.axis_index("y")
    core = lax.axis_index("c")

    pos = jnp.stack([chip, core, 2 * chip + core]).astype(jnp.int32)
    comm = _Comm(w, pos)

    cos, sin = _rope_tables(positions[0])
    sp = {n: w[n] for n in SMALL if n != 'conv_w'}
    loss_row, dx, gs = _local_step(x[0], loss_target[0], cos, sin, sp, comm)

    delta, new_m, new_v, grad_out = {}, {}, {}, {}

    def adamw_big(names, grads):
        for n in names:
            shp = w[n].shape
            two_d = lambda a: a.reshape(shp[0] * shp[1], shp[2])
            d, nm, nv, g = _adamw(two_d(w[n]), two_d(grads[n]), two_d(m[n]), two_d(v[n]),
                                  tr=_row_tile(shp[0] * shp[1], 256), name=f"adamw_{n}", copy_g=True)
            delta[n], new_m[n], new_v[n], grad_out[n] = d.reshape(shp), nm.reshape(shp), nv.reshape(shp), g.reshape(shp)

    small_shapes = [gs[n].shape for n in SMALL] + [(1,)]
    small_pack = _pack([gs[n] for n in SMALL] + [loss_row[0, :1]])
    small = _Gather([_cast_place(small_pack, pos, F32, slots=N_DEV, which=2, name="small_place")], "small_gather",
                    all_devices=True)
    toks = comm.tail(REDUCE_TAIL_A[:1], (dx, small.token))
    comm.tail(REDUCE_TAIL_A[1:], (dx, *toks))
    comm.tail(REDUCE_TAIL_B, dx)
    adamw_big(GROUP_F, comm.shards())
    red = _sum_slots(small.wait(new_v[GROUP_F[-1]])[0], tr=small_pack.shape[0], name="small_sum")
    *small_grads, loss = _unpack(red, small_shapes)
    g_small = dict(zip(SMALL, small_grads))
    comm.tail(REDUCE_TAIL_C, red)
    grads = comm.shards()
    grads.update(g_small)
    shard_cols = conv_w.shape[2]
    grads['conv_w'] = lax.dynamic_slice_in_dim(g_small['conv_w'], chip * shard_cols, shard_cols, axis=2)

    adamw_big(GROUP_M, grads)
    shapes = [w[n].shape for n in SMALL]
    packs = [_pack([src[n] for n in SMALL]) for src in (w, grads, m, v)]
    d, nm, nv = _adamw(*packs, tr=packs[0].shape[0], name="adamw_small")
    for dst, src in ((delta, d), (new_m, nm), (new_v, nv)):
        dst.update(zip(SMALL, _unpack(src, shapes)))

    grads.update(grad_out)
    return (loss[0], dx[None], *[grads[n] for n in WEIGHTS], *[delta[n] for n in WEIGHTS],
            *[new_m[n] for n in WEIGHTS], *[new_v[n] for n in WEIGHTS])
```

```python
import functools
import math

import numpy as np
import jax
import jax.numpy as jnp
from jax import lax
from jax.experimental import pallas as pl
from jax.experimental.pallas import tpu as pltpu

F32 = jnp.float32
MXU_DTYPE = jnp.bfloat16
COMM_DTYPE = jnp.bfloat16
ACT_DTYPE = jnp.bfloat16
HALO = 16

D_MODEL = 1024
DEPTH = 2
HEAD_DIM = 64
POOL_WINDOWS = (2, 4, 8, 16)
POOL_WIDTH = 256
N_Q_HEADS = 8
ATTN_BLOCK = 128
ATTN_WIDTH = 512
KV_WIDTH = 128
CHUNK = 128
SGU_WIDTH = 256
IN_COLS = 4608
GATE_COL0 = 1536
D_FF = 2816
ROPE_THETA = 10000.0
EPS = 1e-6
ADAM_LR, ADAM_B1, ADAM_B2, ADAM_EPS, ADAM_WD, ADAM_STEP = 0.001, 0.9, 0.999, 1e-08, 0.01, 10

N_CHIPS = 4
N_DEV = 8
VMEM_LIMIT_BYTES = 56 * 1024 * 1024
NEG_BIG = -1e30
MESH = pl.DeviceIdType.MESH
ANY = pl.BlockSpec(memory_space=pl.ANY)

SDS = jax.ShapeDtypeStruct


def _cp(*sem):
    return pltpu.CompilerParams(dimension_semantics=sem, vmem_limit_bytes=VMEM_LIMIT_BYTES)


def _dot(a, b, dims=((1,), (0,))):
    return lax.dot_general(a.astype(MXU_DTYPE), b.astype(MXU_DTYPE), (dims, ((), ())),
                           preferred_element_type=F32)


NT = ((1,), (1,))
TN = ((0,), (0,))


def _split_dot(x, m):
    hi = x.astype(MXU_DTYPE)
    lo = (x - hi.astype(F32)).astype(MXU_DTYPE)
    return _dot(hi, m) + _dot(lo, m)


def _seg_matrix(width, seg):
    idx = np.arange(width) // seg
    return jnp.asarray((idx[:, None] == idx[None, :]).astype(np.float32), dtype=MXU_DTYPE)


def _lane(shape):
    return lax.broadcasted_iota(jnp.int32, shape, len(shape) - 1)


def _row(shape):
    return lax.broadcasted_iota(jnp.int32, shape, 0)


def _full(shape):
    nd = len(shape)
    return pl.BlockSpec(shape, lambda *_: (0,) * nd)


def _gelu(x):
    k = math.sqrt(2.0 / math.pi)
    th = jnp.tanh(k * (x + 0.044715 * (x * x * x)))
    return 0.5 * x * (1.0 + th)


def _gelu_and_grad(x):
    k = math.sqrt(2.0 / math.pi)
    x2 = x * x
    th = jnp.tanh(k * (x + 0.044715 * (x2 * x)))
    g = 0.5 * x * (1.0 + th)
    dg = 0.5 * (1.0 + th) + 0.5 * x * (1.0 - th * th) * (k * (1.0 + 3.0 * 0.044715 * x2))
    return g, dg


def _sigmoid(x):
    return 0.5 * jnp.tanh(0.5 * x) + 0.5


def _swap_halves(x):
    w = x.shape[-1]
    first = (_lane(x.shape) % HEAD_DIM) < (HEAD_DIM // 2)
    return jnp.where(first, pltpu.roll(x, w - HEAD_DIM // 2, 1), pltpu.roll(x, HEAD_DIM // 2, 1))


def _tile_lanes(x, reps):
    return x if reps == 1 else jnp.concatenate([x] * reps, axis=1)


def _fold_lanes(x, period):
    w = x.shape[-1]
    while w > period:
        w //= 2
        x = x + pltpu.roll(x, w, 1)
    return x


def _mm(a, b, *, mode, tm, tn, tk, out_dtype=F32, add=None, name,
        a_lead=None, b_lead=None, b_sharded=False, out_into=None,
        b_koff=0, out_joff=0, out_n=None, deps=()):
    ash = a.shape[1:] if a_lead is not None else a.shape
    bsh = b.shape[1:] if b_lead is not None else b.shape
    if b_sharded:
        bsh = (b.shape[1], N_CHIPS * b.shape[2])
    if mode == 'nn':
        (M, K), (K2, N) = ash, bsh
    elif mode == 'nt':
        (M, K), (N, K2) = ash, bsh
    else:
        (K, M), (K2, N) = ash, bsh
    assert K == K2 or (mode == 'nt' and K2 > K), (ash, bsh, mode)
    assert M % tm == 0 and N % tn == 0 and K % tk == 0, (M, N, K, tm, tn, tk)
    nk = K // tk
    dims = {'nn': ((1,), (0,)), 'nt': NT, 'tn': TN}[mode]

    def lead(spec_shape, imap, lead_idx):
        if lead_idx is None:
            return pl.BlockSpec(spec_shape, imap)
        return pl.BlockSpec((None,) + spec_shape, lambda i, j, k: (lead_idx,) + imap(i, j, k))

    if mode == 'tn':
        a_spec = lead((tk, tm), lambda i, j, k: (k, i), a_lead)
    else:
        a_spec = lead((tm, tk), lambda i, j, k: (i, k), a_lead)
    if b_sharded:
        per = b.shape[2] // (tk if mode == 'nt' else tn)
        assert per * (tk if mode == 'nt' else tn) == b.shape[2] and mode != 'tn'
        if mode == 'nt':
            b_spec = pl.BlockSpec((None, tn, tk), lambda i, j, k: ((k + b_koff) // per, j, (k + b_koff) % per))
        else:
            b_spec = pl.BlockSpec((None, tk, tn), lambda i, j, k: (j // per, k, j % per))
    elif mode == 'nt':
        b_spec = lead((tn, tk), lambda i, j, k: (j, k + b_koff), b_lead)
    else:
        b_spec = lead((tk, tn), lambda i, j, k: (k, j), b_lead)
    o_spec = pl.BlockSpec((tm, tn), lambda i, j, k: (i, j + out_joff))
    n_out = N if out_n is None else out_n
    in_specs = [a_spec, b_spec]
    operands = [a, b]
    if add is not None:
        in_specs.append(pl.BlockSpec((tm, tn), lambda i, j, k: (i, j)))
        operands.append(add)
    aliases = {}
    if out_into is not None:
        in_specs.append(ANY)
        operands.append(out_into)
        aliases = {len(operands) - 1: 0}
    in_specs += [ANY] * len(deps)
    operands += list(deps)
    has_add = add is not None
    acc_in_out = nk > 1 and out_dtype == F32

    def body(*refs):
        a_ref, b_ref = refs[0], refs[1]
        pos = 2
        add_ref = None
        if has_add:
            add_ref = refs[pos]
            pos += 1
        if out_into is not None:
            pos += 1
        pos += len(deps)
        o_ref = refs[pos]
        acc_ref = refs[pos + 1] if (nk > 1 and not acc_in_out) else None
        p = _dot(a_ref[...], b_ref[...], dims)
        if nk == 1:
            if has_add:
                p = p + add_ref[...]
            o_ref[...] = p.astype(o_ref.dtype)
            return
        k = pl.program_id(2)
        tgt = o_ref if acc_in_out else acc_ref

        @pl.when(k == 0)
        def _():
            tgt[...] = p + add_ref[...] if has_add else p

        @pl.when(k > 0)
        def _():
            tgt[...] += p

        if not acc_in_out:
            @pl.when(k == nk - 1)
            def _():
                o_ref[...] = acc_ref[...].astype(o_ref.dtype)

    out_shape = SDS((M, n_out), out_dtype)
    scratch = [pltpu.VMEM((tm, tn), F32)] if (nk > 1 and not acc_in_out) else []
    return pl.pallas_call(
        body, grid=(M // tm, N // tn, nk), in_specs=in_specs, out_specs=o_spec, out_shape=out_shape,
        scratch_shapes=scratch, input_output_aliases=aliases, name=name,
        compiler_params=_cp("parallel", "parallel", "arbitrary"))(*operands)


def _rms_bwd_rows(xv, g, dh, dres):
    r = lax.rsqrt(jnp.mean(xv * xv, axis=-1, keepdims=True) + EPS)
    xh = xv * r
    gy = dh * g
    dx = r * (gy - xh * jnp.mean(xh * gy, axis=-1, keepdims=True)) + dres
    return dx, jnp.sum(dh * xh, axis=0, keepdims=True)


def _mm_nt_sharded_rms(a, b, x, g, dres, *, tm, name, deps=()):
    a3 = a if a.ndim == 3 else a[None]
    A, M, ka = a3.shape
    S, N, ns = b.shape
    per = S // A
    assert ka == per * ns and M % tm == 0 and N == x.shape[1], (a3.shape, b.shape, x.shape)

    def body(a_ref, b_ref, x_ref, g_ref, dres_ref, dx_ref, dxb_ref, dg_ref):
        acc = None
        for s in range(S):
            lo = (s % per) * ns
            p = _dot(a_ref[s // per, :, lo:lo + ns], b_ref[s], NT)
            acc = p if acc is None else acc + p
        dx, dg = _rms_bwd_rows(x_ref[...], g_ref[...], acc, dres_ref[...])
        dx_ref[...] = dx
        dxb_ref[...] = dx.astype(dxb_ref.dtype)

        @pl.when(pl.program_id(0) == 0)
        def _():
            dg_ref[...] = jnp.zeros_like(dg_ref)
        dg_ref[...] += dg

    rows = pl.BlockSpec((tm, N), lambda i: (i, 0))
    return pl.pallas_call(
        _after(body, 5, deps), grid=(M // tm,),
        in_specs=[pl.BlockSpec((A, tm, ka), lambda i: (0, i, 0)),
                  pl.BlockSpec((S, N, ns), lambda i: (0, 0, 0), pipeline_mode=pl.Buffered(1)),
                  rows, _full((1, N)), rows] + [ANY] * len(deps),
        out_specs=[rows, rows, _full((1, N))],
        out_shape=[SDS((M, N), F32), SDS((M, N), MXU_DTYPE), SDS((1, N), F32)], name=name,
        compiler_params=_cp("arbitrary"))(a3, b, x, g, dres, *deps)


def _norm_mm(x, g, b, *, tm, tn, name, deps=()):
    M, K = x.shape
    S, K2, ns = b.shape
    per = ns // tn
    assert K == K2 and per * tn == ns and M % tm == 0, (x.shape, b.shape)

    def body(x_ref, g_ref, b_ref, o_ref, h_ref):
        @pl.when(pl.program_id(1) == 0)
        def _():
            xv = x_ref[...]
            r = lax.rsqrt(jnp.mean(xv * xv, axis=-1, keepdims=True) + EPS)
            h_ref[...] = (xv * r * g_ref[...]).astype(h_ref.dtype)
        o_ref[...] = _dot(h_ref[...], b_ref[...]).astype(o_ref.dtype)

    return pl.pallas_call(
        _after(body, 3, deps), grid=(M // tm, S * per),
        in_specs=[pl.BlockSpec((tm, K), lambda i, j: (i, 0)), _full((1, K)),
                  pl.BlockSpec((None, K, tn), lambda i, j: (j // per, 0, j % per))] + [ANY] * len(deps),
        out_specs=[pl.BlockSpec((tm, tn), lambda i, j: (i, j)), pl.BlockSpec((tm, K), lambda i, j: (i, 0))],
        out_shape=[SDS((M, S * ns), ACT_DTYPE), SDS((M, K), MXU_DTYPE)], name=name,
        compiler_params=_cp("parallel", "arbitrary"))(x, g, b, *deps)


def _after(body, n_in, deps):
    nd = len(deps)
    if nd == 0:
        return body
    return lambda *refs: body(*refs[:n_in], *refs[n_in + nd:])


def _down_proj_loss(act, w, x1, target, *, tm, name):
    T, K = act.shape
    D = w.shape[1]

    def body(a_ref, w_ref, x_ref, t_ref, loss_ref, dy_ref, dyb_ref):
        i = pl.program_id(0)
        d = (x_ref[...] + _dot(a_ref[...], w_ref[...])) - t_ref[...]
        dy = d * (1.0 / D)
        dy_ref[...] = dy
        dyb_ref[...] = dy.astype(dyb_ref.dtype)
        part = jnp.sum(jnp.sum(d * d, axis=1, keepdims=True), axis=0, keepdims=True) * (0.5 / D)

        @pl.when(i == 0)
        def _():
            loss_ref[...] = jnp.zeros_like(loss_ref)
        loss_ref[...] += jnp.broadcast_to(part, loss_ref.shape)

    rows = pl.BlockSpec((tm, D), lambda i: (i, 0))
    return pl.pallas_call(
        body, grid=(T // tm,), in_specs=[pl.BlockSpec((tm, K), lambda i: (i, 0)), _full((K, D)), rows, rows],
        out_specs=[_full((1, 128)), rows, rows],
        out_shape=[SDS((1, 128), F32), SDS((T, D), F32), SDS((T, D), MXU_DTYPE)],
        name=name, compiler_params=_cp("arbitrary"))(act, w, x1, target)


def _pool_lane_consts(shape):
    lane = _lane(shape)
    grp = lane // (POOL_WIDTH // 4)
    win = jnp.where(grp == 0, 2, jnp.where(grp == 1, 4, jnp.where(grp == 2, 8, 16)))
    return grp, win


def _pool_select(grp, s2, s4, s8, s16):
    return jnp.where(grp == 0, s2, jnp.where(grp == 1, s4, jnp.where(grp == 2, s8, s16)))


def _pool_diff(xe, row0, tr):
    s2 = xe + pltpu.roll(xe, 1, 0)
    s4 = s2 + pltpu.roll(s2, 2, 0)
    s8 = s4 + pltpu.roll(s4, 4, 0)
    s16 = s8 + pltpu.roll(s8, 8, 0)
    shape = (tr, POOL_WIDTH)
    grp, win = _pool_lane_consts(shape)
    sums = _pool_select(grp, s2[16:], s4[16:], s8[16:], s16[16:])
    t = row0 + _row(shape)
    cnt = jnp.minimum(t + 1, win).astype(F32)
    return sums / cnt - xe[16:]


def _pool_fwd(z, wbd, scale, *, tr, name):
    T = z.shape[0]
    hb = tr // 16

    def body(x_ref, xp_ref, w_ref, s_ref, o_ref):
        i = pl.program_id(0)
        halo = jnp.where(i == 0, 0.0, xp_ref[...].astype(F32))
        diff = _pool_diff(jnp.concatenate([halo, x_ref[...].astype(F32)], axis=0), i * tr, tr)
        o_ref[...] = (_dot(diff, w_ref[...]) * s_ref[...]).astype(o_ref.dtype)

    return pl.pallas_call(
        body, grid=(T // tr,),
        in_specs=[pl.BlockSpec((tr, POOL_WIDTH), lambda i: (i, 0)),
                  pl.BlockSpec((16, POOL_WIDTH), lambda i: (jnp.maximum(i * hb - 1, 0), 0)),
                  _full((POOL_WIDTH, POOL_WIDTH)), _full((1, POOL_WIDTH))],
        out_specs=pl.BlockSpec((tr, POOL_WIDTH), lambda i: (i, 0)),
        out_shape=SDS((T, POOL_WIDTH), MXU_DTYPE), name=name, compiler_params=_cp("parallel"))(z, z, wbd, scale)


def _pool_bwd_tile(i, n, tr, x, xprev, dpa, dpa_next, wbd, scale):
    halo = jnp.where(i == 0, 0.0, xprev)
    diff = _pool_diff(jnp.concatenate([halo, x], axis=0), i * tr, tr)
    mixed = _dot(diff, wbd)
    dscale = jnp.sum(dpa * mixed, axis=0, keepdims=True)
    dnext = jnp.where(i == n - 1, 0.0, dpa_next)
    dmix_e = jnp.concatenate([dpa, dnext], axis=0) * scale
    ddiff_e = _dot(dmix_e, wbd, NT)
    dwbd = _dot(diff, dmix_e[:tr], TN)
    shape = (tr + 16, POOL_WIDTH)
    grp, win = _pool_lane_consts(shape)
    t = i * tr + _row(shape)
    e = ddiff_e / jnp.minimum(t + 1, win).astype(F32)
    nrow = tr + 16
    a2 = e + pltpu.roll(e, nrow - 1, 0)
    a4 = a2 + pltpu.roll(a2, nrow - 2, 0)
    a8 = a4 + pltpu.roll(a4, nrow - 4, 0)
    a16 = a8 + pltpu.roll(a8, nrow - 8, 0)
    dx = _pool_select(grp, a2, a4, a8, a16)[:tr] - ddiff_e[:tr]
    return dx, dwbd, dscale


def _norm_rope(x, g, cos, sin_signed, seg):
    reps = x.shape[1] // 128
    ms = _split_dot(x * x, seg) * (1.0 / HEAD_DIM)
    r = lax.rsqrt(ms + EPS)
    xn = x * r * g
    c, s = _tile_lanes(cos, reps), _tile_lanes(sin_signed, reps)
    return xn * c + _swap_halves(xn) * s


def _norm_rope_bwd(x, g, cos, sin_signed, seg, dout):
    reps = x.shape[1] // 128
    c, s = _tile_lanes(cos, reps), _tile_lanes(sin_signed, reps)
    dxn = dout * c + _swap_halves(dout * s)
    ms = _split_dot(x * x, seg) * (1.0 / HEAD_DIM)
    r = lax.rsqrt(ms + EPS)
    xh = x * r
    gy = dxn * g
    dx = r * (gy - xh * (_split_dot(xh * gy, seg) * (1.0 / HEAD_DIM)))
    dg = jnp.sum(dxn * xh, axis=0, keepdims=True)
    return dx, dg


def _dup_heads(k):
    first = _lane(k.shape) < HEAD_DIM
    kr = pltpu.roll(k, HEAD_DIM, 1)
    return jnp.concatenate([jnp.where(first, k, kr), jnp.where(first, kr, k)], axis=1)


def _qkv_prep(z, cos, sin_signed, gq, gk, seg, *, tr, name):
    T = z.shape[0]

    def body(qa_ref, qb_ref, kv_ref, c_ref, s_ref, gq_ref, gk_ref, seg_ref, q_ref, k_ref, v_ref):
        c, s, seg_m = c_ref[...], s_ref[...], seg_ref[...]
        scale = HEAD_DIM ** -0.5
        qa = _norm_rope(qa_ref[...].astype(F32), gq_ref[...], c, s, seg_m) * scale
        qb = _norm_rope(qb_ref[...].astype(F32), gq_ref[...], c, s, seg_m) * scale
        q_ref[...] = jnp.concatenate([qa, qb], axis=1).astype(q_ref.dtype)
        kv = kv_ref[...].astype(F32)
        k = _norm_rope(kv[:, :KV_WIDTH], gk_ref[...], c, s, seg_m[:128, :128])
        k_ref[...] = _dup_heads(k).astype(k_ref.dtype)
        v_ref[...] = _dup_heads(kv[:, KV_WIDTH:]).astype(v_ref.dtype)

    col = lambda j: pl.BlockSpec((tr, 256), lambda i: (i, j))
    tab = pl.BlockSpec((tr, 128), lambda i: (i, 0))
    return pl.pallas_call(
        body, grid=(T // tr,),
        in_specs=[col(1), col(2), col(3), tab, tab, _full((1, 256)), _full((1, 128)), _full((256, 256))],
        out_specs=[pl.BlockSpec((tr, 512), lambda i: (i, 0)), col(0), col(0)],
        out_shape=[SDS((T, 512), MXU_DTYPE), SDS((T, 256), MXU_DTYPE), SDS((T, 256), MXU_DTYPE)],
        name=name, compiler_params=_cp("parallel"))(z, z, z, cos, sin_signed, gq, gk, seg)


GROUP_HEADS = 4
GROUP_ROWS = GROUP_HEADS * ATTN_BLOCK
ALL_ROWS = N_Q_HEADS * ATTN_BLOCK


def _attn_mask(has_prev):
    qi = _row((ALL_ROWS, 2 * ATTN_BLOCK)) % ATTN_BLOCK
    kj = _lane((ALL_ROWS, 2 * ATTN_BLOCK))
    return (kj > qi) & (kj <= qi + ATTN_BLOCK) & ((kj >= ATTN_BLOCK) | has_prev)


FWD_STEP_BLOCKS = 8
BWD_STEP_BLOCKS = 2


def _band(prev, cur, blk):
    lo = cur[(blk - 1) * ATTN_BLOCK:blk * ATTN_BLOCK] if blk else prev
    return jnp.concatenate([lo, cur[blk * ATTN_BLOCK:(blk + 1) * ATTN_BLOCK]], axis=0)


def _stack_heads(x, g):
    first = _lane((ATTN_BLOCK, 128)) < HEAD_DIM
    parts = []
    for pair in (2 * g, 2 * g + 1):
        x128 = x[:, 128 * pair:128 * (pair + 1)]
        zero = jnp.zeros_like(x128)
        parts += [jnp.where(first, x128, zero), jnp.where(first, zero, x128)]
    return jnp.concatenate(parts, axis=0)


def _unstack_heads(y):
    first = _lane((ATTN_BLOCK, 128)) < HEAD_DIM
    b = ATTN_BLOCK
    return jnp.concatenate([jnp.where(first, y[0:b], y[b:2 * b]), jnp.where(first, y[2 * b:3 * b], y[3 * b:4 * b])],
                           axis=1)


def _sink_col(sk_ref):
    return jnp.concatenate([jnp.broadcast_to(sk_ref[h:h + 1, 0:1], (ATTN_BLOCK, 1)) for h in range(N_Q_HEADS)],
                           axis=0)


def _by_group(a8, b2, dims=((1,), (0,))):
    return jnp.concatenate([_dot(a8[:GROUP_ROWS], b2[:, :128], dims), _dot(a8[GROUP_ROWS:], b2[:, 128:], dims)],
                           axis=0)


def _softmax_exp(q8, k2, mask, sink):
    s = jnp.where(mask, _by_group(q8, k2, NT), NEG_BIG)
    m = jnp.maximum(jnp.max(s, axis=1, keepdims=True), sink)
    p = jnp.exp(s - m)
    ps = jnp.exp(sink - m)
    return p, ps, 1.0 / (jnp.sum(p, axis=1, keepdims=True) + ps)


def _attn_fwd(q, k, v, sinks_b, *, name):
    T = q.shape[0]
    nb = T // ATTN_BLOCK
    STEP_BLOCKS = min(FWD_STEP_BLOCKS, nb)
    STEP_ROWS = STEP_BLOCKS * ATTN_BLOCK

    def body(q_ref, kc_ref, kp_ref, vc_ref, vp_ref, sk_ref, o_ref):
        n = pl.program_id(0)
        kc, kp, vc, vp = kc_ref[...], kp_ref[...], vc_ref[...], vp_ref[...]
        sink = _sink_col(sk_ref)
        for blk in range(STEP_BLOCKS):
            rows = slice(blk * ATTN_BLOCK, (blk + 1) * ATTN_BLOCK)
            mask = _attn_mask((n > 0) if blk == 0 else True)
            k2, v2 = _band(kp, kc, blk), _band(vp, vc, blk)
            qv = q_ref[rows, :]
            q8 = jnp.concatenate([_stack_heads(qv, 0), _stack_heads(qv, 1)], axis=0)
            p, _, inv = _softmax_exp(q8, k2, mask, sink)
            o8 = _by_group(p, v2) * inv
            o_ref[rows, :] = jnp.concatenate([_unstack_heads(o8[:GROUP_ROWS]), _unstack_heads(o8[GROUP_ROWS:])],
                                             axis=1).astype(o_ref.dtype)

    cur = lambda w: pl.BlockSpec((STEP_ROWS, w), lambda n: (n, 0))
    prev = lambda w: pl.BlockSpec((ATTN_BLOCK, w), lambda n: (jnp.maximum(STEP_BLOCKS * n - 1, 0), 0))
    return pl.pallas_call(
        body, grid=(nb // STEP_BLOCKS,),
        in_specs=[cur(512), cur(256), prev(256), cur(256), prev(256), _full((8, 128))],
        out_specs=cur(512), out_shape=SDS((T, 512), MXU_DTYPE), name=name,
        compiler_params=_cp("parallel"))(q, k, k, v, v, sinks_b)


def _attn_bwd(q, k, v, sinks_b, do, *, name, deps=()):
    T = q.shape[0]
    nb = T // ATTN_BLOCK
    STEP_BLOCKS = min(BWD_STEP_BLOCKS, nb)
    STEP_ROWS = STEP_BLOCKS * ATTN_BLOCK

    def body(q_ref, kc_ref, kp_ref, vc_ref, vp_ref, sk_ref, do_ref,
             dq_ref, dkc_ref, dkp_ref, dvc_ref, dvp_ref, dsk_ref):
        n = pl.program_id(0)
        kc, kp, vc, vp = kc_ref[...], kp_ref[...], vc_ref[...], vp_ref[...]
        sink = _sink_col(sk_ref)

        @pl.when(n == 0)
        def _():
            dsk_ref[...] = jnp.zeros_like(dsk_ref)

        for blk in range(STEP_BLOCKS):
            rows = slice(blk * ATTN_BLOCK, (blk + 1) * ATTN_BLOCK)
            mask = _attn_mask((n > 0) if blk == 0 else True)
            k2, v2 = _band(kp, kc, blk), _band(vp, vc, blk)
            qv, dov = q_ref[rows, :], do_ref[rows, :]
            q8 = jnp.concatenate([_stack_heads(qv, 0), _stack_heads(qv, 1)], axis=0)
            do8 = jnp.concatenate([_stack_heads(dov, 0), _stack_heads(dov, 1)], axis=0)
            p, ps, inv = _softmax_exp(q8, k2, mask, sink)
            pn = p * inv
            delta = jnp.sum(do8 * _by_group(pn, v2), axis=1, keepdims=True)
            ds = pn * (_by_group(do8, v2, NT) - delta)
            dq8 = _by_group(ds, k2)
            dq_ref[rows, :] = jnp.concatenate([_unstack_heads(dq8[:GROUP_ROWS]), _unstack_heads(dq8[GROUP_ROWS:])],
                                              axis=1)
            dk = jnp.concatenate([_dot(ds[:GROUP_ROWS], q8[:GROUP_ROWS], TN),
                                  _dot(ds[GROUP_ROWS:], q8[GROUP_ROWS:], TN)], axis=1)
            dv = jnp.concatenate([_dot(pn[:GROUP_ROWS], do8[:GROUP_ROWS], TN),
                                  _dot(pn[GROUP_ROWS:], do8[GROUP_ROWS:], TN)], axis=1)
            wsink = (ps * inv) * delta
            for h in range(N_Q_HEADS):
                dsink = -jnp.sum(wsink[ATTN_BLOCK * h:ATTN_BLOCK * (h + 1)], axis=0, keepdims=True)
                dsk_ref[h:h + 1, :] += jnp.broadcast_to(dsink, (1, 128))
            dkp_ref[rows, :] = dk[:ATTN_BLOCK]
            dkc_ref[rows, :] = dk[ATTN_BLOCK:]
            dvp_ref[rows, :] = dv[:ATTN_BLOCK]
            dvc_ref[rows, :] = dv[ATTN_BLOCK:]

    cur = lambda w: pl.BlockSpec((STEP_ROWS, w), lambda n: (n, 0))
    prev = lambda w: pl.BlockSpec((ATTN_BLOCK, w), lambda n: (jnp.maximum(STEP_BLOCKS * n - 1, 0), 0))
    f = lambda w: SDS((T, w), F32)
    return pl.pallas_call(
        _after(body, 7, deps), grid=(nb // STEP_BLOCKS,),
        in_specs=[cur(512), cur(256), prev(256), cur(256), prev(256), _full((8, 128)), cur(512)] + [ANY] * len(deps),
        out_specs=[cur(512), cur(256), cur(256), cur(256), cur(256), _full((8, 128))],
        out_shape=[f(512), f(256), f(256), f(256), f(256), SDS((8, 128), F32)],
        name=name, compiler_params=_cp("arbitrary"))(q, k, k, v, v, sinks_b, do, *deps)


def _mixer_ab_bwd(z, cos, sin_signed, gq, gk, seg, dq, dkc, dkp, dvc, dvp, dpa, wbd, scale, dz, *, tr, name, deps=()):
    T = z.shape[0]
    n = T // tr
    hb = tr // 16
    ab = tr // ATTN_BLOCK

    def unfold(cur, nxt_tile, nxt_halo, i):
        nxt = jnp.concatenate([nxt_tile[ATTN_BLOCK:], jnp.where(i == n - 1, 0.0, nxt_halo)], axis=0)
        tot = cur + nxt
        first = _lane((tr, 128)) < HEAD_DIM
        a = tot[:, :128]
        b = tot[:, 128:]
        a = a + pltpu.roll(a, HEAD_DIM, 1)
        b = b + pltpu.roll(b, HEAD_DIM, 1)
        return jnp.where(first, a, b)

    def body(xp_ref, xpp_ref, qa_ref, qb_ref, kv_ref, c_ref, s_ref, gq_ref, gk_ref, seg_ref,
             dq_ref, dkc_ref, dkp_ref, dkh_ref, dvc_ref, dvp_ref, dvh_ref, dpa_ref, dpan_ref, w_ref, sc_ref, _dz_in,
             dz_ref, dgq_ref, dgk_ref, dw_ref, dsc_ref):
        i = pl.program_id(0)
        c, s, seg_m = c_ref[...], s_ref[...], seg_ref[...]
        scale_q = HEAD_DIM ** -0.5
        dqv = dq_ref[...] * scale_q
        dxa, dga = _norm_rope_bwd(qa_ref[...].astype(F32), gq_ref[...], c, s, seg_m, dqv[:, :256])
        dxb, dgb = _norm_rope_bwd(qb_ref[...].astype(F32), gq_ref[...], c, s, seg_m, dqv[:, 256:])
        dk = unfold(dkc_ref[...], dkp_ref[...], dkh_ref[...], i)
        dv = unfold(dvc_ref[...], dvp_ref[...], dvh_ref[...], i)
        kv = kv_ref[...].astype(F32)
        dxk, dgk = _norm_rope_bwd(kv[:, :KV_WIDTH], gk_ref[...], c, s, seg_m[:128, :128], dk)
        dxp, dwbd, dscale = _pool_bwd_tile(i, n, tr, xp_ref[...].astype(F32), xpp_ref[...].astype(F32),
                                           dpa_ref[...], dpan_ref[...],
                                           w_ref[...], sc_ref[...])
        dz_ref[...] = jnp.concatenate([dxp, dxa, dxb, dxk, dv], axis=1).astype(dz_ref.dtype)

        @pl.when(i == 0)
        def _():
            dgq_ref[...] = jnp.zeros_like(dgq_ref)
            dgk_ref[...] = jnp.zeros_like(dgk_ref)
            dw_ref[...] = jnp.zeros_like(dw_ref)
            dsc_ref[...] = jnp.zeros_like(dsc_ref)
        dgq_ref[...] += _fold_lanes(dga + dgb, HEAD_DIM)
        dgk_ref[...] += _fold_lanes(dgk, HEAD_DIM)
        dw_ref[...] += dwbd
        dsc_ref[...] += dscale

    col = lambda j: pl.BlockSpec((tr, 256), lambda i: (i, j))
    rows = lambda w: pl.BlockSpec((tr, w), lambda i: (i, 0))
    nxt_blk = pl.BlockSpec((ATTN_BLOCK, 256), lambda i: (jnp.minimum((i + 1) * ab, T // ATTN_BLOCK - 1), 0))
    prev16 = pl.BlockSpec((16, 256), lambda i: (jnp.maximum(i * hb - 1, 0), 0))
    next16 = pl.BlockSpec((16, 256), lambda i: (jnp.minimum((i + 1) * hb, T // 16 - 1), 0))
    return pl.pallas_call(
        _after(body, 22, deps), grid=(n,),
        in_specs=[col(0), prev16, col(1), col(2), col(3), rows(128), rows(128),
                  _full((1, 256)), _full((1, 128)), _full((256, 256)),
                  rows(512), rows(256), rows(256), nxt_blk, rows(256), rows(256), nxt_blk,
                  rows(256), next16, _full((256, 256)), _full((1, 256)), ANY] + [ANY] * len(deps),
        out_specs=[rows(1024), _full((1, 256)), _full((1, 128)), _full((256, 256)), _full((1, 256))],
        out_shape=[SDS((T, IN_COLS), MXU_DTYPE), SDS((1, 256), F32), SDS((1, 128), F32),
                   SDS((256, 256), F32), SDS((1, 256), F32)],
        input_output_aliases={21: 0}, name=name, compiler_params=_cp("arbitrary"))(
            z, z, z, z, z, cos, sin_signed, gq, gk, seg, dq, dkc, dkp, dkp, dvc, dvp, dvp, dpa, dpa, wbd, scale, dz,
            *deps)


def _sgu_common(zu, zv, vn, seg):
    u, du = _gelu_and_grad(zu)
    gv, dgv = _gelu_and_grad(zv)
    ms = _split_dot(gv * gv, seg) * (1.0 / HEAD_DIM)
    r = lax.rsqrt(ms + EPS)
    xh = gv * r
    return u, du, dgv, r, xh, xh * vn


def _sgu_fwd(z, wtril, bexp, vn, seg, *, tr, name):
    T = z.shape[0]
    nch = tr // CHUNK

    def body(u_ref, v_ref, w_ref, b_ref, vn_ref, seg_ref, o_ref):
        u, _, _, _, _, vg = _sgu_common(u_ref[...].astype(F32), v_ref[...].astype(F32), vn_ref[...], seg_ref[...])
        grp = _lane((CHUNK, SGU_WIDTH)) // HEAD_DIM
        outs = []
        for ch in range(nch):
            vc = vg[ch * CHUNK:(ch + 1) * CHUNK]
            s = b_ref[...]
            for g in range(4):
                s = s + jnp.where(grp == g, _dot(w_ref[g], vc), 0.0)
            outs.append(u[ch * CHUNK:(ch + 1) * CHUNK] * s)
        o_ref[...] = jnp.concatenate(outs, axis=0).astype(o_ref.dtype)

    col = lambda j: pl.BlockSpec((tr, 256), lambda i: (i, j))
    return pl.pallas_call(
        body, grid=(T // tr,),
        in_specs=[col(4), col(5), _full((4, CHUNK, CHUNK)), _full((CHUNK, 256)), _full((1, 256)), _full((256, 256))],
        out_specs=col(0), out_shape=SDS((T, SGU_WIDTH), MXU_DTYPE), name=name,
        compiler_params=_cp("parallel"))(z, z, wtril, bexp, vn, seg)


def _mixers_fwd(z, cos, sin_signed, gq, gk, seg, wbd, scale, wtril, bexp, vn, *, tr, name):
    T = z.shape[0]
    hb = tr // 16
    nch = tr // CHUNK
    width = POOL_WIDTH + ATTN_WIDTH + 2 * KV_WIDTH + 2 * SGU_WIDTH

    def body(z_ref, xp_ref, c_ref, s_ref, gq_ref, gk_ref, seg_ref, w_ref, sc_ref, wt_ref, b_ref, vn_ref,
             pa_ref, q_ref, k_ref, v_ref, sg_ref):
        i = pl.program_id(0)
        zt = z_ref[...].astype(F32)
        seg_m = seg_ref[...]
        halo = jnp.where(i == 0, 0.0, xp_ref[...].astype(F32))
        diff = _pool_diff(jnp.concatenate([halo, zt[:, :POOL_WIDTH]], axis=0), i * tr, tr)
        pa_ref[...] = (_dot(diff, w_ref[...]) * sc_ref[...]).astype(pa_ref.dtype)
        c, s = c_ref[...], s_ref[...]
        scale_q = HEAD_DIM ** -0.5
        qa = _norm_rope(zt[:, 256:512], gq_ref[...], c, s, seg_m) * scale_q
        qb = _norm_rope(zt[:, 512:768], gq_ref[...], c, s, seg_m) * scale_q
        q_ref[...] = jnp.concatenate([qa, qb], axis=1).astype(q_ref.dtype)
        kk = _norm_rope(zt[:, 768:896], gk_ref[...], c, s, seg_m[:128, :128])
        k_ref[...] = _dup_heads(kk).astype(k_ref.dtype)
        v_ref[...] = _dup_heads(zt[:, 896:1024]).astype(v_ref.dtype)
        u, _, _, _, _, vg = _sgu_common(zt[:, 1024:1280], zt[:, 1280:1536], vn_ref[...], seg_m)
        grp = _lane((CHUNK, SGU_WIDTH)) // HEAD_DIM
        outs = []
        for ch in range(nch):
            vc = vg[ch * CHUNK:(ch + 1) * CHUNK]
            acc = b_ref[...]
            for g in range(4):
                acc = acc + jnp.where(grp == g, _dot(wt_ref[g], vc), 0.0)
            outs.append(u[ch * CHUNK:(ch + 1) * CHUNK] * acc)
        sg_ref[...] = jnp.concatenate(outs, axis=0).astype(sg_ref.dtype)

    rows = lambda w: pl.BlockSpec((tr, w), lambda i: (i, 0))
    return pl.pallas_call(
        body, grid=(T // tr,),
        in_specs=[rows(width), pl.BlockSpec((16, POOL_WIDTH), lambda i: (jnp.maximum(i * hb - 1, 0), 0)),
                  rows(128), rows(128), _full((1, 256)), _full((1, 128)), _full((256, 256)),
                  _full((POOL_WIDTH, POOL_WIDTH)), _full((1, POOL_WIDTH)),
                  _full((4, CHUNK, CHUNK)), _full((CHUNK, 256)), _full((1, 256))],
        out_specs=[rows(256), rows(512), rows(256), rows(256), rows(256)],
        out_shape=[SDS((T, 256), MXU_DTYPE), SDS((T, 512), MXU_DTYPE), SDS((T, 256), MXU_DTYPE),
                   SDS((T, 256), MXU_DTYPE), SDS((T, 256), MXU_DTYPE)],
        name=name, compiler_params=_cp("parallel"))(z, z, cos, sin_signed, gq, gk, seg, wbd, scale, wtril, bexp, vn)


def _sgu_bwd(z, wtril, bexp, vn, seg, dsg, dz, *, tr, name):
    T = z.shape[0]
    nch = tr // CHUNK

    def body(u_ref, v_ref, w_ref, b_ref, vn_ref, seg_ref, d_ref, _dz_in, dz_ref, dw_ref, db_ref, dvn_ref):
        i = pl.program_id(0)
        seg_m = seg_ref[...]
        vn_v = vn_ref[...]
        u, du, dgv, r, xh, vg = _sgu_common(u_ref[...].astype(F32), v_ref[...].astype(F32), vn_v, seg_m)
        d = d_ref[...]
        grp = _lane((CHUNK, SGU_WIDTH)) // HEAD_DIM
        tril = _row((CHUNK, CHUNK)) >= _lane((CHUNK, CHUNK))

        @pl.when(i == 0)
        def _():
            dw_ref[...] = jnp.zeros_like(dw_ref)
            db_ref[...] = jnp.zeros_like(db_ref)
            dvn_ref[...] = jnp.zeros_like(dvn_ref)

        dus, dvgs = [], []
        for ch in range(nch):
            sl = slice(ch * CHUNK, (ch + 1) * CHUNK)
            vc = vg[sl]
            s = b_ref[...]
            for g in range(4):
                s = s + jnp.where(grp == g, _dot(w_ref[g], vc), 0.0)
            dus.append(d[sl] * s)
            ds = d[sl] * u[sl]
            db_ref[...] += _split_dot(ds, seg_m)
            dvg = jnp.zeros((CHUNK, SGU_WIDTH), F32)
            for g in range(4):
                dsm = jnp.where(grp == g, ds, 0.0)
                dvg = dvg + jnp.where(grp == g, _dot(w_ref[g], ds, TN), 0.0)
                dw_ref[g] += jnp.where(tril, _dot(dsm, vc, NT), 0.0)
            dvgs.append(dvg)
        dup = jnp.concatenate(dus, axis=0)
        dvg = jnp.concatenate(dvgs, axis=0)
        dvn_ref[...] += _fold_lanes(jnp.sum(dvg * xh, axis=0, keepdims=True), HEAD_DIM)
        gy = dvg * vn_v
        dgvv = r * (gy - xh * (_split_dot(xh * gy, seg_m) * (1.0 / HEAD_DIM)))
        dz_ref[...] = jnp.concatenate([dup * du, dgvv * dgv], axis=1).astype(dz_ref.dtype)

    col = lambda j: pl.BlockSpec((tr, 256), lambda i: (i, j))
    return pl.pallas_call(
        body, grid=(T // tr,),
        in_specs=[col(4), col(5), _full((4, CHUNK, CHUNK)), _full((CHUNK, 256)), _full((1, 256)), _full((256, 256)),
                  col(0), ANY],
        out_specs=[pl.BlockSpec((tr, 512), lambda i: (i, 2)), _full((4, CHUNK, CHUNK)), _full((CHUNK, 256)),
                   _full((1, 256))],
        out_shape=[SDS((T, IN_COLS), MXU_DTYPE), SDS((4, CHUNK, CHUNK), F32), SDS((CHUNK, 256), F32),
                   SDS((1, 256), F32)],
        input_output_aliases={7: 0}, name=name, compiler_params=_cp("arbitrary"))(
            z, z, wtril, bexp, vn, seg, dsg, dz)


def _mixers_bwd(z, cos, sin_signed, gq, gk, seg, dq, dkc, dkp, dvc, dvp, dpa, wbd, scale, wtril, bexp, vn, dsg, dz,
                *, tr, name, deps=()):
    T = z.shape[0]
    n = T // tr
    hb = tr // 16
    ab = tr // ATTN_BLOCK
    nch = tr // CHUNK
    width = POOL_WIDTH + ATTN_WIDTH + 2 * KV_WIDTH + 2 * SGU_WIDTH

    def unfold(cur, nxt_tile, nxt_halo, i):
        nxt = jnp.concatenate([nxt_tile[ATTN_BLOCK:], jnp.where(i == n - 1, 0.0, nxt_halo)], axis=0)
        tot = cur + nxt
        first = _lane((tr, 128)) < HEAD_DIM
        a = tot[:, :128]
        b = tot[:, 128:]
        a = a + pltpu.roll(a, HEAD_DIM, 1)
        b = b + pltpu.roll(b, HEAD_DIM, 1)
        return jnp.where(first, a, b)

    def body(z_ref, xpp_ref, c_ref, s_ref, gq_ref, gk_ref, seg_ref,
             dq_ref, dkc_ref, dkp_ref, dkh_ref, dvc_ref, dvp_ref, dvh_ref, dpa_ref, dpan_ref, w_ref, sc_ref,
             wt_ref, b_ref, vn_ref, d_ref, _dz_in,
             dz_ref, dgq_ref, dgk_ref, dw_ref, dsc_ref, dws_ref, db_ref, dvn_ref):
        i = pl.program_id(0)
        c, s, seg_m = c_ref[...], s_ref[...], seg_ref[...]

        @pl.when(i == 0)
        def _():
            dgq_ref[...] = jnp.zeros_like(dgq_ref)
            dgk_ref[...] = jnp.zeros_like(dgk_ref)
            dw_ref[...] = jnp.zeros_like(dw_ref)
            dsc_ref[...] = jnp.zeros_like(dsc_ref)
            dws_ref[...] = jnp.zeros_like(dws_ref)
            db_ref[...] = jnp.zeros_like(db_ref)
            dvn_ref[...] = jnp.zeros_like(dvn_ref)

        scale_q = HEAD_DIM ** -0.5
        dqv = dq_ref[...] * scale_q
        dxa, dga = _norm_rope_bwd(z_ref[:, 256:512].astype(F32), gq_ref[...], c, s, seg_m, dqv[:, :256])
        dxb, dgb = _norm_rope_bwd(z_ref[:, 512:768].astype(F32), gq_ref[...], c, s, seg_m, dqv[:, 256:])
        dk = unfold(dkc_ref[...], dkp_ref[...], dkh_ref[...], i)
        dv = unfold(dvc_ref[...], dvp_ref[...], dvh_ref[...], i)
        dxk, dgk = _norm_rope_bwd(z_ref[:, 768:896].astype(F32), gk_ref[...], c, s, seg_m[:128, :128], dk)
        dxp, dwbd, dscale = _pool_bwd_tile(i, n, tr, z_ref[:, :POOL_WIDTH].astype(F32), xpp_ref[...].astype(F32),
                                           dpa_ref[...], dpan_ref[...], w_ref[...], sc_ref[...])
        dz_ref[:, :1024] = jnp.concatenate([dxp, dxa, dxb, dxk, dv], axis=1).astype(dz_ref.dtype)
        dgq_ref[...] += _fold_lanes(dga + dgb, HEAD_DIM)
        dgk_ref[...] += _fold_lanes(dgk, HEAD_DIM)
        dw_ref[...] += dwbd
        dsc_ref[...] += dscale

        vn_v = vn_ref[...]
        u, du, dgv, r, xh, vg = _sgu_common(z_ref[:, 1024:1280].astype(F32), z_ref[:, 1280:1536].astype(F32),
                                            vn_v, seg_m)
        d = d_ref[...]
        grp = _lane((CHUNK, SGU_WIDTH)) // HEAD_DIM
        tril = _row((CHUNK, CHUNK)) >= _lane((CHUNK, CHUNK))
        dus, dvgs = [], []
        for ch in range(nch):
            sl = slice(ch * CHUNK, (ch + 1) * CHUNK)
            vc = vg[sl]
            acc = b_ref[...]
            for g in range(4):
                acc = acc + jnp.where(grp == g, _dot(wt_ref[g], vc), 0.0)
            dus.append(d[sl] * acc)
            ds = d[sl] * u[sl]
            db_ref[...] += _split_dot(ds, seg_m)
            dvg = jnp.zeros((CHUNK, SGU_WIDTH), F32)
            for g in range(4):
                dsm = jnp.where(grp == g, ds, 0.0)
                dvg = dvg + jnp.where(grp == g, _dot(wt_ref[g], ds, TN), 0.0)
                dws_ref[g] += jnp.where(tril, _dot(dsm, vc, NT), 0.0)
            dvgs.append(dvg)
        dup = jnp.concatenate(dus, axis=0)
        dvg = jnp.concatenate(dvgs, axis=0)
        dvn_ref[...] += _fold_lanes(jnp.sum(dvg * xh, axis=0, keepdims=True), HEAD_DIM)
        gy = dvg * vn_v
        dgvv = r * (gy - xh * (_split_dot(xh * gy, seg_m) * (1.0 / HEAD_DIM)))
        dz_ref[:, 1024:1536] = jnp.concatenate([dup * du, dgvv * dgv], axis=1).astype(dz_ref.dtype)

    rows = lambda w: pl.BlockSpec((tr, w), lambda i: (i, 0))
    nxt_blk = pl.BlockSpec((ATTN_BLOCK, 256), lambda i: (jnp.minimum((i + 1) * ab, T // ATTN_BLOCK - 1), 0))
    prev16 = pl.BlockSpec((16, 256), lambda i: (jnp.maximum(i * hb - 1, 0), 0))
    next16 = pl.BlockSpec((16, 256), lambda i: (jnp.minimum((i + 1) * hb, T // 16 - 1), 0))
    return pl.pallas_call(
        _after(body, 23, deps), grid=(n,),
        in_specs=[rows(width), prev16, rows(128), rows(128),
                  _full((1, 256)), _full((1, 128)), _full((256, 256)),
                  rows(512), rows(256), rows(256), nxt_blk, rows(256), rows(256), nxt_blk,
                  rows(256), next16, _full((256, 256)), _full((1, 256)),
                  _full((4, CHUNK, CHUNK)), _full((CHUNK, 256)), _full((1, 256)), rows(256), ANY]
        + [ANY] * len(deps),
        out_specs=[rows(width), _full((1, 256)), _full((1, 128)), _full((256, 256)), _full((1, 256)),
                   _full((4, CHUNK, CHUNK)), _full((CHUNK, 256)), _full((1, 256))],
        out_shape=[SDS((T, IN_COLS), MXU_DTYPE), SDS((1, 256), F32), SDS((1, 128), F32),
                   SDS((256, 256), F32), SDS((1, 256), F32),
                   SDS((4, CHUNK, CHUNK), F32), SDS((CHUNK, 256), F32), SDS((1, 256), F32)],
        input_output_aliases={22: 0}, name=name, compiler_params=_cp("arbitrary"))(
            z, z, cos, sin_signed, gq, gk, seg, dq, dkc, dkp, dkp, dvc, dvp, dvp, dpa, dpa, wbd, scale,
            wtril, bexp, vn, dsg, dz, *deps)


def _merge_fwd(pa, at, sg, wa, wb, wc, z, x, w_out, *, tm, tn, name):
    T = pa.shape[0]
    gb = GATE_COL0 // tn
    nb = D_MODEL // tn

    def body(pa_ref, at_ref, sg_ref, wa_ref, wb_ref, wc_ref, g0_ref, g1_ref, g2_ref, x_ref, wo_ref,
             m_ref, y_ref, x1_ref):
        j = pl.program_id(1)
        acc = None
        for idx, (op_ref, w_ref, g_ref) in enumerate(((pa_ref, wa_ref, g0_ref), (at_ref, wb_ref, g1_ref),
                                                      (sg_ref, wc_ref, g2_ref))):
            y = _dot(op_ref[...], w_ref[...])
            y_ref[idx] = y.astype(y_ref.dtype)
            t = _sigmoid(g_ref[...].astype(F32)) * y
            acc = t if acc is None else acc + t
        merged = acc.astype(m_ref.dtype)
        m_ref[...] = merged
        p = _dot(merged, wo_ref[...])

        @pl.when(j == 0)
        def _():
            x1_ref[...] = x_ref[...] + p

        @pl.when(j > 0)
        def _():
            x1_ref[...] += p

    op = lambda w: pl.BlockSpec((tm, w), lambda i, j: (i, 0))
    wt = lambda k: pl.BlockSpec((k, tn), lambda i, j: (0, j))
    gate = lambda b: pl.BlockSpec((tm, tn), lambda i, j: (i, gb + b * nb + j))
    return pl.pallas_call(
        body, grid=(T // tm, nb),
        in_specs=[op(256), op(512), op(256), wt(256), wt(512), wt(256), gate(0), gate(1), gate(2),
                  op(D_MODEL), pl.BlockSpec((tn, D_MODEL), lambda i, j: (j, 0))],
        out_specs=[pl.BlockSpec((tm, tn), lambda i, j: (i, j)), pl.BlockSpec((3, tm, tn), lambda i, j: (0, i, j)),
                   op(D_MODEL)],
        out_shape=[SDS((T, D_MODEL), MXU_DTYPE), SDS((3, T, D_MODEL), MXU_DTYPE), SDS((T, D_MODEL), F32)],
        name=name, compiler_params=_cp("parallel", "arbitrary"))(pa, at, sg, wa, wb, wc, z, z, z, x, w_out)


def _out_dx_merge_bwd(dxb, w_out, y, z, ws, xs, *, tm, tn, name):
    T = dxb.shape[0]
    gb = GATE_COL0 // tn
    nb = D_MODEL // tn
    nr = T // tm
    widths = [w.shape[0] for w in ws]

    def body(dx_ref, w_ref, y_ref, g_ref, *refs):
        w_refs, x_refs = refs[0:3], refs[3:6]
        dz_ref, dx_refs, dw_refs = refs[6], refs[7:10], refs[10:13]
        dm_ref, acc_refs = refs[13], refs[14:17]
        i, b, j = pl.program_id(0), pl.program_id(1), pl.program_id(2)

        @pl.when((b == 0) & (j == 0))
        def _():
            dm = _dot(dx_ref[...], w_ref[...], NT)
            for jj in range(nb):
                dm_ref[jj] = dm[:, jj * tn:(jj + 1) * tn]

        d = dm_ref[j]
        g = _sigmoid(g_ref[...].astype(F32))
        dy = (d * g).astype(MXU_DTYPE)
        dz_ref[...] = (d * y_ref[...].astype(F32) * g * (1.0 - g)).astype(dz_ref.dtype)
        for branch in range(3):
            @pl.when(b == branch)
            def _():
                p = _dot(dy, w_refs[branch][...], NT)
                q = _dot(x_refs[branch][...], dy, TN)

                @pl.when(j == 0)
                def _():
                    dx_refs[branch][...] = p

                @pl.when(j > 0)
                def _():
                    dx_refs[branch][...] += p

                @pl.when(i == 0)
                def _():
                    acc_refs[branch][j] = q

                @pl.when(i > 0)
                def _():
                    acc_refs[branch][j] += q

        @pl.when((i == nr - 1) & (b == 2) & (j == nb - 1))
        def _():
            for branch in range(3):
                for jj in range(nb):
                    dw_refs[branch][:, jj * tn:(jj + 1) * tn] = acc_refs[branch][jj]

    wspec = lambda k: pl.BlockSpec((k, tn), lambda i, b, j: (0, j))
    rows = lambda k: pl.BlockSpec((tm, k), lambda i, b, j: (i, 0))
    return pl.pallas_call(
        body, grid=(nr, 3, nb),
        in_specs=[rows(D_MODEL),
                  pl.BlockSpec((D_MODEL, D_MODEL), lambda i, b, j: (0, 0), pipeline_mode=pl.Buffered(1)),
                  pl.BlockSpec((None, tm, tn), lambda i, b, j: (b, i, j)),
                  pl.BlockSpec((tm, tn), lambda i, b, j: (i, gb + b * nb + j))]
        + [wspec(k) for k in widths] + [rows(k) for k in widths],
        out_specs=[pl.BlockSpec((tm, tn), lambda i, b, j: (i, gb + b * nb + j))]
        + [rows(k) for k in widths] + [_full((k, D_MODEL)) for k in widths],
        out_shape=[SDS((T, IN_COLS), MXU_DTYPE)] + [SDS((T, k), F32) for k in widths]
        + [SDS((k, D_MODEL), F32) for k in widths],
        scratch_shapes=[pltpu.VMEM((nb, tm, tn), F32)] + [pltpu.VMEM((nb, k, tn), F32) for k in widths],
        name=name, compiler_params=_cp("arbitrary", "arbitrary", "arbitrary"))(dxb, w_out, y, z, *ws, *xs)


def _conv3(xe, w, b):
    return (w[0:1] * pltpu.roll(xe, 2, 0) + w[1:2] * pltpu.roll(xe, 1, 0) + w[2:3] * xe)[8:] + b


def _conv_act_fwd(up, cw, cb, *, tr, tc, name):
    T = up.shape[0]
    nc = D_FF // tc
    hb = tr // HALO

    def body(ug_ref, ugp_ref, uv_ref, uvp_ref, wg_ref, wv_ref, bg_ref, bv_ref, o_ref):
        i = pl.program_id(1)
        first = i == 0

        def halo_tile(prev_ref, cur_ref):
            prev8 = prev_ref[...].astype(F32)[HALO - 8:]
            return jnp.concatenate([jnp.where(first, 0.0, prev8), cur_ref[...].astype(F32)], axis=0)

        cg = _conv3(halo_tile(ugp_ref, ug_ref), wg_ref[...], bg_ref[...])
        cv = _conv3(halo_tile(uvp_ref, uv_ref), wv_ref[...], bv_ref[...])
        o_ref[...] = (cg * _sigmoid(cg) * cv).astype(o_ref.dtype)

    tile = lambda off: pl.BlockSpec((tr, tc), lambda j, i: (i, off + j))
    prev = lambda off: pl.BlockSpec((HALO, tc), lambda j, i: (jnp.maximum(i * hb - 1, 0), off + j))
    par = lambda rows, off: pl.BlockSpec((rows, tc), lambda j, i: (0, off + j))
    return pl.pallas_call(
        body, grid=(nc, T // tr),
        in_specs=[tile(0), prev(0), tile(nc), prev(nc), par(3, 0), par(3, nc), par(1, 0), par(1, nc)],
        out_specs=pl.BlockSpec((tr, tc), lambda j, i: (i, j)),
        out_shape=SDS((T, D_FF), MXU_DTYPE), name=name,
        compiler_params=_cp("parallel", "parallel"))(up, up, up, up, cw, cw, cb, cb)


def _conv_act_bwd(up, cw, cb, dact, *, tr, tc, name, deps=()):
    T = up.shape[0]
    nc = D_FF // tc
    hb = tr // 8
    hbu = tr // HALO
    nr = T // tr

    def body(ug_ref, ugp_ref, ugn_ref, uv_ref, uvp_ref, uvn_ref, da_ref, dan_ref, wg_ref, wv_ref, bg_ref, bv_ref,
             du_ref, dwg_ref, dwv_ref, dbg_ref, dbv_ref):
        i = pl.program_id(1)
        first, last = i == 0, i == nr - 1
        da = jnp.concatenate([da_ref[...], jnp.where(last, 0.0, dan_ref[...])], axis=0)

        def with_halos(prev_ref, cur_ref, next_ref):
            prev8 = prev_ref[...].astype(F32)[HALO - 8:]
            next8 = next_ref[...].astype(F32)[:8]
            return jnp.concatenate([jnp.where(first, 0.0, prev8), cur_ref[...].astype(F32), next8], axis=0)

        uge = with_halos(ugp_ref, ug_ref, ugn_ref)
        uve = with_halos(uvp_ref, uv_ref, uvn_ref)
        wg, wv = wg_ref[...], wv_ref[...]
        ug1, ug2 = pltpu.roll(uge, 1, 0)[8:], pltpu.roll(uge, 2, 0)[8:]
        uv1, uv2 = pltpu.roll(uve, 1, 0)[8:], pltpu.roll(uve, 2, 0)[8:]
        cg = wg[0:1] * ug2 + wg[1:2] * ug1 + wg[2:3] * uge[8:] + bg_ref[...]
        cv = wv[0:1] * uv2 + wv[1:2] * uv1 + wv[2:3] * uve[8:] + bv_ref[...]
        sg = _sigmoid(cg)
        dcg = da * cv * (sg * (1.0 + cg * (1.0 - sg)))
        dcv = da * (cg * sg)
        nrow = tr + 8

        def back(dc, w):
            return (w[2:3] * dc + w[1:2] * pltpu.roll(dc, nrow - 1, 0) + w[0:1] * pltpu.roll(dc, nrow - 2, 0))[:tr]

        du_ref[0] = back(dcg, wg).astype(du_ref.dtype)
        du_ref[1] = back(dcv, wv).astype(du_ref.dtype)

        def wgrad(dc, u0, u1, u2):
            d = dc[:tr]
            rows = [jnp.sum(d * u2[:tr], axis=0, keepdims=True), jnp.sum(d * u1[:tr], axis=0, keepdims=True),
                    jnp.sum(d * u0[8:8 + tr], axis=0, keepdims=True)]
            return jnp.concatenate(rows, axis=0), jnp.sum(d, axis=0, keepdims=True)

        dwg, dbg = wgrad(dcg, uge, ug1, ug2)
        dwv, dbv = wgrad(dcv, uve, uv1, uv2)

        @pl.when(first)
        def _():
            dwg_ref[...] = jnp.zeros_like(dwg_ref)
            dwv_ref[...] = jnp.zeros_like(dwv_ref)
            dbg_ref[...] = jnp.zeros_like(dbg_ref)
            dbv_ref[...] = jnp.zeros_like(dbv_ref)
        dwg_ref[...] += dwg
        dwv_ref[...] += dwv
        dbg_ref[...] += dbg
        dbv_ref[...] += dbv

    tile = lambda off: pl.BlockSpec((tr, tc), lambda j, i: (i, off + j))
    prev = lambda off: pl.BlockSpec((HALO, tc), lambda j, i: (jnp.maximum(i * hbu - 1, 0), off + j))
    nxt = lambda off: pl.BlockSpec((HALO, tc), lambda j, i: (jnp.minimum((i + 1) * hbu, T // HALO - 1), off + j))
    dnext = pl.BlockSpec((8, tc), lambda j, i: (jnp.minimum((i + 1) * hb, T // 8 - 1), j))
    par = lambda rows, off: pl.BlockSpec((rows, tc), lambda j, i: (0, off + j))
    acc = lambda rows: pl.BlockSpec((rows, tc), lambda j, i: (0, j))
    return pl.pallas_call(
        _after(body, 12, deps), grid=(nc, nr),
        in_specs=[tile(0), prev(0), nxt(0), tile(nc), prev(nc), nxt(nc), tile(0), dnext,
                  par(3, 0), par(3, nc), par(1, 0), par(1, nc)] + [ANY] * len(deps),
        out_specs=[pl.BlockSpec((2, tr, tc), lambda j, i: (0, i, j)), acc(3), acc(3), acc(1), acc(1)],
        out_shape=[SDS((2, T, D_FF), MXU_DTYPE), SDS((3, D_FF), F32), SDS((3, D_FF), F32),
                   SDS((1, D_FF), F32), SDS((1, D_FF), F32)],
        name=name, compiler_params=_cp("parallel", "arbitrary"))(
            up, up, up, up, up, up, dact, dact, cw, cw, cb, cb, *deps)


def _row_tile(rows, cap):
    t = min(cap, rows)
    t -= t % 8
    while rows % t:
        t -= 8
    return t


def _adamw(w, g, m, v, *, tr, name, copy_g=False):
    R, C = w.shape
    assert R % tr == 0, (R, tr)

    def body(w_ref, g_ref, m_ref, v_ref, d_ref, nm_ref, nv_ref, *rest):
        gv = g_ref[...]
        mn = ADAM_B1 * m_ref[...] + (1.0 - ADAM_B1) * gv
        vn = ADAM_B2 * v_ref[...] + (1.0 - ADAM_B2) * (gv * gv)
        m_hat = mn / (1.0 - ADAM_B1 ** ADAM_STEP)
        v_hat = vn / (1.0 - ADAM_B2 ** ADAM_STEP)
        d_ref[...] = -ADAM_LR * (m_hat / (jnp.sqrt(v_hat) + ADAM_EPS) + ADAM_WD * w_ref[...])
        nm_ref[...] = mn
        nv_ref[...] = vn
        if copy_g:
            rest[0][...] = gv

    rows = pl.BlockSpec((tr, C), lambda i: (i, 0))
    n_out = 4 if copy_g else 3
    return pl.pallas_call(
        body, grid=(R // tr,), in_specs=[rows] * 4, out_specs=[rows] * n_out,
        out_shape=[SDS((R, C), F32)] * n_out, name=name, compiler_params=_cp("parallel"))(w, g, m, v)


def _sum_slots(r, *, tr, name):
    S, R, C = r.shape
    assert R % tr == 0, (R, tr)

    def body(r_ref, o_ref):
        acc = r_ref[0]
        for s in range(1, S):
            acc = acc + r_ref[s]
        o_ref[...] = acc

    return pl.pallas_call(
        body, grid=(R // tr,), in_specs=[pl.BlockSpec((S, tr, C), lambda i: (0, i, 0))],
        out_specs=pl.BlockSpec((tr, C), lambda i: (i, 0)), out_shape=SDS((R, C), F32),
        name=name, compiler_params=_cp("parallel"))(r)


def _pair_add(g4, h, pos, *, name):
    A, _, r, C = g4.shape
    cs = C if A == N_CHIPS else C // N_CHIPS
    tr = _row_tile(r, 256)
    if A == N_CHIPS:
        g_map, h_map = (lambda t, i, pos: (t, pos[1], i, 0)), (lambda t, i, pos: (t, i, 0))
    else:
        g_map, h_map = (lambda t, i, pos: (0, pos[1], i, t)), (lambda t, i, pos: (0, i, t))

    def body(pos_ref, g_ref, h_ref, o_ref):
        o_ref[...] = (g_ref[...] + h_ref[...]).astype(o_ref.dtype)

    grid_spec = pltpu.PrefetchScalarGridSpec(
        num_scalar_prefetch=1, grid=(N_CHIPS, r // tr),
        in_specs=[pl.BlockSpec((None, None, tr, cs), g_map), pl.BlockSpec((None, tr, cs), h_map)],
        out_specs=pl.BlockSpec((None, tr, cs), lambda t, i, pos: (t, i, 0)))
    return pl.pallas_call(body, grid_spec=grid_spec, out_shape=SDS((N_CHIPS, r, cs), COMM_DTYPE), name=name,
                          compiler_params=_cp("parallel", "parallel"))(pos, g4, h)


def _chip_sum(p, r2, f_into, pos, layer, *, name):
    _, r, cs = p.shape
    tr = _row_tile(r, 256)

    def body(pos_ref, own_ref, r_ref, *rest):
        o_ref = rest[-1]
        o_ref[...] = ((own_ref[...].astype(F32) + r_ref[0].astype(F32)) + r_ref[1].astype(F32)) + r_ref[2].astype(F32)

    in_specs = [pl.BlockSpec((None, tr, cs), lambda i, pos: (pos[0], i, 0)),
                pl.BlockSpec((3, tr, cs), lambda i, pos: (0, i, 0))]
    operands = [pos, p, r2]
    aliases = {}
    if f_into is not None:
        in_specs.append(ANY)
        operands.append(f_into)
        aliases = {3: 0}
    grid_spec = pltpu.PrefetchScalarGridSpec(
        num_scalar_prefetch=1, grid=(r // tr,), in_specs=in_specs,
        out_specs=pl.BlockSpec((None, None, tr, cs), lambda i, pos: (layer, pos[1], i, 0)))
    return pl.pallas_call(body, grid_spec=grid_spec, out_shape=SDS((DEPTH, 2, r, cs), F32), name=name,
                          input_output_aliases=aliases, compiler_params=_cp("parallel"))(*operands)


def _mesh_pos():
    return lax.axis_index("x"), lax.axis_index("y"), lax.axis_index("c")


HBM = pl.BlockSpec(memory_space=pltpu.HBM)
SEM = pl.BlockSpec(memory_space=pltpu.SEMAPHORE)
DATAFLOW = pltpu.SideEffectType.DATAFLOW_SIDE_EFFECTING
CHIP_FLIPS = (2, 1, 3)


def _chip_peers():
    x, y, c = _mesh_pos()
    return 2 * x + y, [(1 - x, y, c), (x, 1 - y, c), (1 - x, 1 - y, c)], (x, y, 1 - c), c


def _split_start(arrays, n_copies, issue, *, name, deps=()):
    k = len(arrays)
    nd = len(deps)

    def body(*refs):
        issue(refs[:k], refs[k + nd], refs[k + nd + 1])
        refs[2 * k + nd + 2][...] = jnp.zeros((8, 128), F32)

    out = pl.pallas_call(
        body, name=name,
        out_shape=(pltpu.SemaphoreType.DMA((n_copies,)), pltpu.SemaphoreType.DMA((n_copies,)),
                   *[pltpu.HBM(a.shape, a.dtype) for a in arrays], SDS((8, 128), F32)),
        in_specs=[HBM] * k + [ANY] * nd, out_specs=(SEM, SEM, *[HBM] * k, pl.BlockSpec(memory_space=pltpu.VMEM)),
        input_output_aliases={i: 2 + i for i in range(k)},
        compiler_params=pltpu.CompilerParams(has_side_effects=DATAFLOW))(
            *[pltpu.with_memory_space_constraint(a, pltpu.HBM) for a in arrays], *deps)
    return (out[0], out[1]), list(out[2:2 + k]), out[2 + k]


def _split_wait(sems, arrays, after, waits, *, name):
    k = len(arrays)
    afters = tuple(after) if isinstance(after, (tuple, list)) else (after,)

    def body(*refs):
        waits(refs[:k], refs[k], refs[k + 1])

    out = pl.pallas_call(
        body, name=name, out_shape=tuple(pltpu.HBM(a.shape, a.dtype) for a in arrays),
        in_specs=[HBM] * k + [SEM, SEM] + [ANY] * len(afters), out_specs=tuple([HBM] * k),
        input_output_aliases={i: i for i in range(k)},
        compiler_params=pltpu.CompilerParams(has_side_effects=DATAFLOW))(*arrays, sems[0], sems[1], *afters)
    return list(out)


def _wait_both(cp):
    cp.wait_send()
    cp.wait_recv()


def _cast_place(shard, pos, dtype, *, name, layer=None, slots=N_CHIPS, which=0):
    R, C = shard.shape[-2:]
    tr = R if R % 8 else _row_tile(R, 256)
    if layer is None:
        in_spec = pl.BlockSpec((tr, C), lambda i, pos: (i, 0))
    else:
        in_spec = pl.BlockSpec((None, tr, C), lambda i, pos: (layer, i, 0))

    def body(pos_ref, x_ref, o_ref):
        o_ref[...] = x_ref[...].astype(o_ref.dtype)

    grid_spec = pltpu.PrefetchScalarGridSpec(
        num_scalar_prefetch=1, grid=(R // tr,), in_specs=[in_spec],
        out_specs=pl.BlockSpec((None, tr, C), lambda i, pos: (pos[which], i, 0)))
    return pl.pallas_call(body, grid_spec=grid_spec, out_shape=SDS((slots, R, C), dtype), name=name,
                          compiler_params=_cp("parallel"))(pos, shard)


def _device_peers():
    x, y, c = _mesh_pos()
    peers = [(x ^ ((f >> 2) & 1), y ^ ((f >> 1) & 1), c ^ (f & 1)) for f in range(1, N_DEV)]
    return 4 * x + 2 * y + c, peers


class _Gather:
    def __init__(self, lands, name, deps=(), all_devices=False, halves=False):
        n = len(lands)
        self.name, self.halves = name, halves
        npeer = N_DEV - 1 if all_devices else N_CHIPS - 1
        if halves:
            lands = [a.reshape(a.shape[0], 2, a.shape[1] // 2, a.shape[2]) for a in lands]

        def copies(refs, ss, rs):
            if halves:
                me, peers, _, c = _chip_peers()
                own = lambda r: r.at[me, c]
            else:
                me, peers = _device_peers() if all_devices else _chip_peers()[:2]
                own = lambda r: r.at[me]
            return [pltpu.make_async_remote_copy(
                src_ref=own(refs[w]), dst_ref=own(refs[w]), send_sem=ss.at[npeer * w + p],
                recv_sem=rs.at[npeer * w + p], device_id=peers[p], device_id_type=MESH)
                for w in range(n) for p in range(npeer)]

        def issue(refs, ss, rs):
            for cp in copies(refs, ss, rs):
                cp.start()

        def waits(refs, ss, rs):
            for cp in copies(refs, ss, rs):
                _wait_both(cp)

        self._waits = waits
        self.sems, self.arrays, self.token = _split_start(list(lands), npeer * n, issue, name=name + "_start",
                                                          deps=deps)

    def wait(self, after):
        arrays = _split_wait(self.sems, self.arrays, after, self._waits, name=self.name + "_wait")
        if not self.halves:
            return arrays
        n = len(arrays)

        def copies(refs, ss, rs):
            me, _, sibling, c = _chip_peers()
            return [pltpu.make_async_remote_copy(
                src_ref=refs[w].at[me ^ CHIP_FLIPS[p], c], dst_ref=refs[w].at[me ^ CHIP_FLIPS[p], c],
                send_sem=ss.at[3 * w + p], recv_sem=rs.at[3 * w + p], device_id=sibling, device_id_type=MESH)
                for w in range(n) for p in range(3)]

        def issue(refs, ss, rs):
            for cp in copies(refs, ss, rs):
                cp.start()

        def waits(refs, ss, rs):
            for cp in copies(refs, ss, rs):
                _wait_both(cp)

        sems, arrays, _ = _split_start(arrays, 3 * n, issue, name=self.name + "_share_start")
        arrays = _split_wait(sems, arrays, after, waits, name=self.name + "_share_wait")
        return [a.reshape(a.shape[0], 2 * a.shape[2], a.shape[3]) for a in arrays]


def _swap_halves_start(g4s, *, name):
    n = len(g4s)
    lands = [lax.empty((g.shape[0],) + g.shape[2:], g.dtype) for g in g4s]

    def copies(refs, ss, rs):
        _, _, sibling, c = _chip_peers()
        return [pltpu.make_async_remote_copy(
            src_ref=refs[w].at[:, 1 - c], dst_ref=refs[n + w], send_sem=ss.at[w], recv_sem=rs.at[w],
            device_id=sibling, device_id_type=MESH) for w in range(n)]

    def issue(refs, ss, rs):
        for cp in copies(refs, ss, rs):
            cp.start()

    def waits(refs, ss, rs):
        for cp in copies(refs, ss, rs):
            _wait_both(cp)

    sems, arrays, token = _split_start(list(g4s) + lands, n, issue, name=name + "_start")
    return sems, arrays, token, waits


def _scatter_start(parts, *, name, deps=()):
    n = len(parts)
    lands = [lax.empty((3,) + p.shape[1:], p.dtype) for p in parts]

    def copies(refs, ss, rs):
        me, peers, _, _ = _chip_peers()
        return [pltpu.make_async_remote_copy(
            src_ref=refs[w].at[me ^ CHIP_FLIPS[p]], dst_ref=refs[n + w].at[p],
            send_sem=ss.at[3 * w + p], recv_sem=rs.at[3 * w + p], device_id=peers[p], device_id_type=MESH)
            for w in range(n) for p in range(3)]

    def issue(refs, ss, rs):
        for cp in copies(refs, ss, rs):
            cp.start()

    def waits(refs, ss, rs):
        for cp in copies(refs, ss, rs):
            _wait_both(cp)

    sems, arrays, token = _split_start(list(parts) + lands, 3 * n, issue, name=name + "_start", deps=deps)
    return sems, arrays, token, waits


def _pair_share_start(fs, layer, *, name):
    n = len(fs)

    def copies(refs, ss, rs):
        _, _, sibling, c = _chip_peers()
        return [pltpu.make_async_remote_copy(
            src_ref=refs[w].at[layer, c], dst_ref=refs[w].at[layer, c], send_sem=ss.at[w], recv_sem=rs.at[w],
            device_id=sibling, device_id_type=MESH) for w in range(n)]

    def issue(refs, ss, rs):
        for cp in copies(refs, ss, rs):
            cp.start()

    def waits(refs, ss, rs):
        for cp in copies(refs, ss, rs):
            _wait_both(cp)

    sems, arrays, token = _split_start(list(fs), n, issue, name=name + "_start")
    return sems, arrays, token, waits


BIG = ('w_in', 'w_proj_a', 'w_proj_b', 'w_proj_c', 'w_out', 'w_up', 'w_down')
BIG_SHARD_AXIS = {'w_in': 2, 'w_proj_a': 2, 'w_proj_b': 2, 'w_proj_c': 2, 'w_out': 1, 'w_up': 2, 'w_down': 1}
SMALL = ('norm1', 'q_norm', 'k_norm', 'sinks', 'w_pool', 'pool_scale', 'sgu_v_norm', 'w_s', 'b_s', 'norm2',
         'conv_b', 'conv_w')
WEIGHTS = ('norm1', 'w_in', 'q_norm', 'k_norm', 'sinks', 'w_pool', 'pool_scale', 'sgu_v_norm', 'w_s', 'b_s',
           'w_proj_a', 'w_proj_b', 'w_proj_c', 'w_out', 'norm2', 'w_up', 'conv_w', 'conv_b', 'w_down')


def _rope_tables(positions):
    inv_freq = ROPE_THETA ** (-jnp.arange(0, HEAD_DIM, 2, dtype=F32) / HEAD_DIM)
    ang = positions.astype(F32)[:, None] * inv_freq
    cos, sin = jnp.cos(ang), jnp.sin(ang)
    c = jnp.concatenate([cos, cos], axis=1)
    s = jnp.concatenate([-sin, sin], axis=1)
    return jnp.concatenate([c, c], axis=1), jnp.concatenate([s, s], axis=1)


def _block_diag4(w):
    out = jnp.zeros((POOL_WIDTH, POOL_WIDTH), w.dtype)
    for g in range(4):
        out = lax.dynamic_update_slice(out, w[g], (g * HEAD_DIM, g * HEAD_DIM))
    return out


def _local_step(x, target, cos, sin, sp, sched):
    T = x.shape[0]
    tm1 = min(1024, T)
    tm = min(512, T)
    tr = min(1024, T)
    trc = min(512, T)
    tkt = min(2048, T)
    seg = _seg_matrix(256, HEAD_DIM)
    saved = []
    xl = x
    for l in range(DEPTH):
        p = f"l{l}_"
        c = dict(
            g1=sp['norm1'][l][None], g2=sp['norm2'][l][None],
            wbd=_block_diag4(sp['w_pool'][l]).astype(MXU_DTYPE), scale=sp['pool_scale'][l][None],
            gq=jnp.tile(sp['q_norm'][l], 4)[None], gk=jnp.tile(sp['k_norm'][l], 2)[None],
            sinks=jnp.broadcast_to(sp['sinks'][l][:, None], (N_Q_HEADS, 128)),
            wtril=jnp.tril(sp['w_s'][l]).astype(MXU_DTYPE),
            bexp=jnp.repeat(sp['b_s'][l].T, HEAD_DIM, axis=1), vn=jnp.tile(sp['sgu_v_norm'][l], 4)[None],
            cb=sp['conv_b'][l][None])
        c['w_in'] = sched.weight('w_in', l, xl)
        z, h1 = _norm_mm(xl, c['g1'], c['w_in'], tm=tm1, tn=1152, name=p + "in_proj",
                         deps=sched.start_tokens() if l == 0 else ())
        pa, q, k, v, sg = _mixers_fwd(z, cos, sin, c['gq'], c['gk'], seg, c['wbd'], c['scale'], c['wtril'],
                                      c['bexp'], c['vn'], tr=tr, name=p + "mixers")
        at = _attn_fwd(q, k, v, c['sinks'], name=p + "attn")
        for n in ('w_proj_a', 'w_proj_b', 'w_proj_c', 'w_out'):
            c[n] = sched.weight(n, l, (pa, at, sg))
        merged, y3, x1 = _merge_fwd(pa, at, sg, c['w_proj_a'], c['w_proj_b'], c['w_proj_c'], z, xl, c['w_out'],
                                    tm=tm1, tn=512, name=p + "merge_out_proj")
        for n in ('w_up', 'conv_w', 'w_down'):
            c[n] = sched.weight(n, l, x1)
        up, h2 = _norm_mm(x1, c['g2'], c['w_up'], tm=tm1, tn=1408, name=p + "up_proj")
        act = _conv_act_fwd(up, c['conv_w'], c['cb'], tr=trc, tc=1408, name=p + "conv_act")
        saved.append(dict(c, x=xl, h1=h1, z=z, pa=pa, q=q, k=k, v=v, at=at, sg=sg, merged=merged, y3=y3,
                          x1=x1, h2=h2, up=up, act=act))
        if l < DEPTH - 1:
            xl = _mm(act, c['w_down'], mode='nn', add=x1, tm=tm, tn=D_MODEL, tk=D_FF, name=p + "down_proj")
        else:
            loss_row, dx, dxb = _down_proj_loss(act, c['w_down'], x1, target, tm=tm, name=p + "down_proj_loss")

    gs = {n: [None] * DEPTH for n in SMALL}
    for l in reversed(range(DEPTH)):
        p = f"l{l}_b_"
        s = saved[l]
        gb = {}
        dact = _mm(dxb, s['w_down'], mode='nt', tm=tm1, tn=1408, tk=D_MODEL, name=p + "down_dx")
        gb['w_down'] = _mm(s['act'], dxb, mode='tn', tm=1408, tn=D_MODEL, tk=tkt, name=p + "down_dw")
        toks = sched.slot(l, 'down', gb['w_down'])
        dup, dwg, dwv, dbg, dbv = _conv_act_bwd(s['up'], s['conv_w'], s['cb'], dact, tr=min(1024, T), tc=256,
                                                name=p + "conv_act", deps=toks)
        gs['conv_w'][l] = jnp.concatenate([dwg, dwv], axis=1)
        gs['conv_b'][l] = jnp.concatenate([dbg, dbv], axis=1)[0]
        toks = sched.slot(l, 'conv', dup)
        for half in range(2):
            gb['w_up'] = _mm(s['h2'], dup, mode='tn', b_lead=half, tm=D_MODEL, tn=1408, tk=tkt,
                             out_into=gb.get('w_up'), out_joff=2 * half, out_n=2 * D_FF, name=p + f"up_dw{half}",
                             deps=toks if half == 0 else ())
        toks = sched.slot(l, 'ffn', gb['w_up'], gb)
        dx1, dx1b, dg2 = _mm_nt_sharded_rms(dup, s['w_up'], s['x1'], s['g2'], dx, tm=tm,
                                            name=p + "up_dx_rms2", deps=toks)
        gs['norm2'][l] = dg2[0]
        gb['w_out'] = _mm(s['merged'], dx1b, mode='tn', tm=D_MODEL, tn=D_MODEL, tk=tkt, name=p + "out_dw")
        (dz, dpa, dat, dsg, gb['w_proj_a'], gb['w_proj_b'], gb['w_proj_c']) = _out_dx_merge_bwd(
            dx1b, s['w_out'], s['y3'], s['z'], [s['w_proj_a'], s['w_proj_b'], s['w_proj_c']],
            [s['pa'], s['at'], s['sg']], tm=tm1, tn=512, name=p + "out_dx_merge")
        toks = sched.slot(l, 'mid', dz)
        dq, dkc, dkp, dvc, dvp, dsk = _attn_bwd(s['q'], s['k'], s['v'], s['sinks'], dat, name=p + "attn", deps=toks)
        gs['sinks'][l] = dsk[:, 0]
        toks = sched.slot(l, 'attn', dq)
        dz, dgq, dgk, dwbd, dsc, dws, dbrows, dvn = _mixers_bwd(
            s['z'], cos, sin, s['gq'], s['gk'], seg, dq, dkc, dkp, dvc, dvp, dpa, s['wbd'], s['scale'],
            s['wtril'], s['bexp'], s['vn'], dsg, dz, tr=trc, name=p + "mixers", deps=toks)
        gs['q_norm'][l] = dgq[0, :HEAD_DIM]
        gs['k_norm'][l] = dgk[0, :HEAD_DIM]
        gs['w_pool'][l] = jnp.stack([dwbd[g * HEAD_DIM:(g + 1) * HEAD_DIM, g * HEAD_DIM:(g + 1) * HEAD_DIM]
                                     for g in range(4)])
        gs['pool_scale'][l] = dsc[0]
        gs['w_s'][l] = dws
        gs['b_s'][l] = dbrows[:, ::HEAD_DIM].T
        gs['sgu_v_norm'][l] = dvn[0, :HEAD_DIM]
        gb['w_in'] = _mm(s['h1'], dz, mode='tn', tm=D_MODEL, tn=1152, tk=tkt, name=p + "in_dw")
        toks = sched.slot(l, 'mix', gb['w_in'], gb)
        dx, dxb, dg1 = _mm_nt_sharded_rms(dz, s['w_in'], s['x'], s['g1'], dx1, tm=tm,
                                          name=p + "in_dx_rms1", deps=toks)
        gs['norm1'][l] = dg1[0]
    gs = {n: jnp.stack(v) for n, v in gs.items()}
    return loss_row, dx, gs


GROUP_F = ('w_down', 'w_up')
GROUP_M = ('w_out', 'w_proj_a', 'w_proj_b', 'w_proj_c', 'w_in')
ROW_SHARDED = ('w_out', 'w_down')

REDUCE_PLAN = {
    (1, 'ffn'): (('S1', 'F', 1),),
    (1, 'mid'): (('W1', 'F', 1),),
    (1, 'mix'): (('S1', 'M', 1),),
    (0, 'down'): (('W1', 'M', 1),),
    (0, 'conv'): (('W2', 'F', 1),),
    (0, 'ffn'): (('S1', 'F', 0), ('W3', 'F', 1)),
    (0, 'mid'): (('W1', 'F', 0),),
    (0, 'attn'): (('W2', 'M', 1),),
    (0, 'mix'): (('S1', 'M', 0), ('W3', 'M', 1)),
}
REDUCE_TAIL_A = (('W1', 'M', 0), ('W2', 'F', 0))
REDUCE_TAIL_B = (('W3', 'F', 0),)
REDUCE_TAIL_C = (('W2', 'M', 0), ('W3', 'M', 0))


class _Comm:
    def __init__(self, w, pos):
        self.pos = pos
        groups = {'a': [('w_in', 0)],
                  'b': [(n, 0) for n in ('w_proj_a', 'w_proj_b', 'w_proj_c', 'w_out')],
                  'c': [(n, 0) for n in ('w_up', 'conv_w', 'w_down')],
                  'd': [(n, 1) for n in BIG] + [('conv_w', 1)]}
        self.gathers, self.group_of, self.weights = {}, {}, {}
        self.tokens = []
        for g, ks in groups.items():
            lands = [_cast_place(w[n], pos, F32 if n == 'conv_w' else MXU_DTYPE, layer=l, name=f"gw_place_{n}{l}")
                     for n, l in ks]
            self.gathers[g] = (_Gather(lands, "gw_" + g, deps=self.tokens[-1:], halves=(g == 'a')), ks)
            self.tokens.append(self.gathers[g][0].token)
            self.group_of.update({k: g for k in ks})
        self.red = {}
        self.final = {}

    def start_tokens(self):
        return self.tokens[-1:]

    def weight(self, name, layer, after):
        if (name, layer) not in self.weights:
            gather, ks = self.gathers[self.group_of[(name, layer)]]
            for (n, l), full in zip(ks, gather.wait(after)):
                if n == 'conv_w' or n.startswith('w_proj'):
                    full = full.transpose(1, 0, 2).reshape(full.shape[1], -1)
                elif n in ROW_SHARDED:
                    full = full.reshape(-1, full.shape[2])
                self.weights[(n, l)] = full
        return self.weights[(name, layer)]

    def slot(self, layer, slot, after, grads=None):
        tokens = []
        for step, grp, lyr in REDUCE_PLAN.get((layer, slot), ()):
            tok = self._step(step, grp, lyr, after, grads)
            if tok is not None:
                tokens.append(tok)
        return tokens

    def tail(self, steps, after, deps=()):
        toks = (self._step(step, grp, lyr, after, None, deps) for step, grp, lyr in steps)
        return [t for t in toks if t is not None]

    def shards(self):
        return {n: f.reshape(DEPTH, 2 * f.shape[2], f.shape[3]) for n, f in self.final.items()}

    def _step(self, step, grp, layer, after, grads, deps=()):
        names = GROUP_F if grp == 'F' else GROUP_M
        tag = f"{grp.lower()}{layer}"
        st = self.red.setdefault((grp, layer), {})
        n = len(names)
        if step == 'S1':
            g4s = []
            for nm in names:
                g = grads[nm]
                R, C = g.shape
                g4s.append(g.reshape(N_CHIPS, 2, R // (2 * N_CHIPS), C) if nm in ROW_SHARDED
                           else g.reshape(1, 2, R // 2, C))
            st['s1'] = _swap_halves_start(g4s, name="rs1_" + tag)
            return st['s1'][2]
        if step == 'W1':
            sems, arrays, _, waits = st.pop('s1')
            arrays = _split_wait(sems, arrays, after, waits, name=f"rs1_{tag}_wait")
            parts = [_pair_add(arrays[i], arrays[n + i], self.pos, name=f"pair_add_{tag}_{names[i]}")
                     for i in range(n)]
            st['s2'] = _scatter_start(parts, name="rs2_" + tag, deps=deps)
            return st['s2'][2]
        if step == 'W2':
            sems, arrays, _, waits = st.pop('s2')
            arrays = _split_wait(sems, arrays, after, waits, name=f"rs2_{tag}_wait")
            fs = [_chip_sum(arrays[i], arrays[n + i], self.final.get(names[i]), self.pos, layer,
                            name=f"chip_sum_{tag}_{names[i]}") for i in range(n)]
            st['s3'] = _pair_share_start(fs, layer, name="rs3_" + tag)
            return st['s3'][2]
        sems, arrays, _, waits = st.pop('s3')
        self.final.update(zip(names, _split_wait(sems, arrays, after, waits, name=f"rs3_{tag}_wait")))
        return None


def _pack(arrays):
    rows = []
    for a in arrays:
        nel = int(np.prod(a.shape))
        if nel % 1024 == 0:
            rows.append(a.astype(F32).reshape(nel // 128, 128))
        else:
            f = a.reshape(-1).astype(F32)
            rows.append(jnp.pad(f, (0, (-nel) % 1024)).reshape(-1, 128))
    return jnp.concatenate(rows, axis=0)


def _unpack(pack, shapes):
    out, row = [], 0
    for shp in shapes:
        nel = int(np.prod(shp))
        nrow = 8 * -(-nel // 1024)
        part = pack[row:row + nrow]
        out.append(part.reshape(shp) if nel % 1024 == 0 else part.reshape(-1)[:nel].reshape(shp))
        row += nrow
    return out


def kernel(x, positions, norm1, w_in, q_norm, k_norm, sinks, w_pool, pool_scale, sgu_v_norm, w_s, b_s, w_proj_a, w_proj_b, w_proj_c, w_out, norm2, w_up, conv_w, conv_b, w_down, loss_target, m_norm1, m_w_in, m_q_norm, m_k_norm, m_sinks, m_w_pool, m_pool_scale, m_sgu_v_norm, m_w_s, m_b_s, m_w_proj_a, m_w_proj_b, m_w_proj_c, m_w_out, m_norm2, m_w_up, m_conv_w, m_conv_b, m_w_down, v_norm1, v_w_in, v_q_norm, v_k_norm, v_sinks, v_w_pool, v_pool_scale, v_sgu_v_norm, v_w_s, v_b_s, v_w_proj_a, v_w_proj_b, v_w_proj_c, v_w_out, v_norm2, v_w_up, v_conv_w, v_conv_b, v_w_down):
    w = dict(norm1=norm1, w_in=w_in, q_norm=q_norm, k_norm=k_norm, sinks=sinks, w_pool=w_pool, pool_scale=pool_scale,
             sgu_v_norm=sgu_v_norm, w_s=w_s, b_s=b_s, w_proj_a=w_proj_a, w_proj_b=w_proj_b, w_proj_c=w_proj_c,
             w_out=w_out, norm2=norm2, w_up=w_up, conv_w=conv_w, conv_b=conv_b, w_down=w_down)
    m = dict(norm1=m_norm1, w_in=m_w_in, q_norm=m_q_norm, k_norm=m_k_norm, sinks=m_sinks, w_pool=m_w_pool,
             pool_scale=m_pool_scale, sgu_v_norm=m_sgu_v_norm, w_s=m_w_s, b_s=m_b_s, w_proj_a=m_w_proj_a,
             w_proj_b=m_w_proj_b, w_proj_c=m_w_proj_c, w_out=m_w_out, norm2=m_norm2, w_up=m_w_up, conv_w=m_conv_w,
             conv_b=m_conv_b, w_down=m_w_down)
    v = dict(norm1=v_norm1, w_in=v_w_in, q_norm=v_q_norm, k_norm=v_k_norm, sinks=v_sinks, w_pool=v_w_pool,
             pool_scale=v_pool_scale, sgu_v_norm=v_sgu_v_norm, w_s=v_w_s, b_s=v_b_s, w_proj_a=v_w_proj_a,
             w_proj_b=v_w_proj_b, w_proj_c=v_w_proj_c, w_out=v_w_out, norm2=v_norm2, w_up=v_w_up, conv_w=v_conv_w,
             conv_b=v_conv_b, w_down=v_w_down)
    chip = 2 * lax.axis_index("x") + lax.axis_index("y")
    core = lax.axis_index("c")

    pos = jnp.stack([chip, core, 2 * chip + core]).astype(jnp.int32)
    comm = _Comm(w, pos)

    cos, sin = _rope_tables(positions[0])
    sp = {n: w[n] for n in SMALL if n != 'conv_w'}
    loss_row, dx, gs = _local_step(x[0], loss_target[0], cos, sin, sp, comm)

    delta, new_m, new_v, grad_out = {}, {}, {}, {}

    def adamw_big(names, grads):
        for n in names:
            shp = w[n].shape
            two_d = lambda a: a.reshape(shp[0] * shp[1], shp[2])
            d, nm, nv, g = _adamw(two_d(w[n]), two_d(grads[n]), two_d(m[n]), two_d(v[n]),
                                  tr=_row_tile(shp[0] * shp[1], 256), name=f"adamw_{n}", copy_g=True)
            delta[n], new_m[n], new_v[n], grad_out[n] = d.reshape(shp), nm.reshape(shp), nv.reshape(shp), g.reshape(shp)

    small_shapes = [gs[n].shape for n in SMALL] + [(1,)]
    small_pack = _pack([gs[n] for n in SMALL] + [loss_row[0, :1]])
    small = _Gather([_cast_place(small_pack, pos, F32, slots=N_DEV, which=2, name="small_place")], "small_gather",
                    all_devices=True)
    toks = comm.tail(REDUCE_TAIL_A[:1], (dx, small.token))
    comm.tail(REDUCE_TAIL_A[1:], (dx, *toks))
    comm.tail(REDUCE_TAIL_B, dx)
    adamw_big(GROUP_F, comm.shards())
    red = _sum_slots(small.wait(new_v[GROUP_F[-1]])[0], tr=small_pack.shape[0], name="small_sum")
    *small_grads, loss = _unpack(red, small_shapes)
    g_small = dict(zip(SMALL, small_grads))
    comm.tail(REDUCE_TAIL_C, red)
    grads = comm.shards()
    grads.update(g_small)
    shard_cols = conv_w.shape[2]
    grads['conv_w'] = lax.dynamic_slice_in_dim(g_small['conv_w'], chip * shard_cols, shard_cols, axis=2)

    adamw_big(GROUP_M, grads)
    shapes = [w[n].shape for n in SMALL]
    packs = [_pack([src[n] for n in SMALL]) for src in (w, grads, m, v)]
    d, nm, nv = _adamw(*packs, tr=packs[0].shape[0], name="adamw_small")
    for dst, src in ((delta, d), (new_m, nm), (new_v, nv)):
        dst.update(zip(SMALL, _unpack(src, shapes)))

    grads.update(grad_out)
    return (loss[0], dx[None], *[grads[n] for n in WEIGHTS], *[delta[n] for n in WEIGHTS],
            *[new_m[n] for n in WEIGHTS], *[new_v[n] for n in WEIGHTS])
```

```python
import functools
import math

import numpy as np
import jax
import jax.numpy as jnp
from jax import lax
from jax.experimental import pallas as pl
from jax.experimental.pallas import tpu as pltpu

F32 = jnp.float32
MXU_DTYPE = jnp.bfloat16
COMM_DTYPE = jnp.bfloat16
ACT_DTYPE = jnp.bfloat16
HALO = 16

D_MODEL = 1024
DEPTH = 2
HEAD_DIM = 64
POOL_WINDOWS = (2, 4, 8, 16)
POOL_WIDTH = 256
N_Q_HEADS = 8
ATTN_BLOCK = 128
ATTN_WIDTH = 512
KV_WIDTH = 128
CHUNK = 128
SGU_WIDTH = 256
IN_COLS = 4608
GATE_COL0 = 1536
D_FF = 2816
ROPE_THETA = 10000.0
EPS = 1e-6
ADAM_LR, ADAM_B1, ADAM_B2, ADAM_EPS, ADAM_WD, ADAM_STEP = 0.001, 0.9, 0.999, 1e-08, 0.01, 10

N_CHIPS = 4
N_DEV = 8
VMEM_LIMIT_BYTES = 56 * 1024 * 1024
NEG_BIG = -1e30
MESH = pl.DeviceIdType.MESH
ANY = pl.BlockSpec(memory_space=pl.ANY)

SDS = jax.ShapeDtypeStruct


def _cp(*sem):
    return pltpu.CompilerParams(dimension_semantics=sem, vmem_limit_bytes=VMEM_LIMIT_BYTES)


def _dot(a, b, dims=((1,), (0,))):
    return lax.dot_general(a.astype(MXU_DTYPE), b.astype(MXU_DTYPE), (dims, ((), ())),
                           preferred_element_type=F32)


NT = ((1,), (1,))
TN = ((0,), (0,))


def _split_dot(x, m):
    hi = x.astype(MXU_DTYPE)
    lo = (x - hi.astype(F32)).astype(MXU_DTYPE)
    return _dot(hi, m) + _dot(lo, m)


def _seg_matrix(width, seg):
    idx = np.arange(width) // seg
    return jnp.asarray((idx[:, None] == idx[None, :]).astype(np.float32), dtype=MXU_DTYPE)


def _lane(shape):
    return lax.broadcasted_iota(jnp.int32, shape, len(shape) - 1)


def _row(shape):
    return lax.broadcasted_iota(jnp.int32, shape, 0)


def _full(shape):
    nd = len(shape)
    return pl.BlockSpec(shape, lambda *_: (0,) * nd)


def _gelu(x):
    k = math.sqrt(2.0 / math.pi)
    th = jnp.tanh(k * (x + 0.044715 * (x * x * x)))
    return 0.5 * x * (1.0 + th)


def _gelu_and_grad(x):
    k = math.sqrt(2.0 / math.pi)
    x2 = x * x
    th = jnp.tanh(k * (x + 0.044715 * (x2 * x)))
    g = 0.5 * x * (1.0 + th)
    dg = 0.5 * (1.0 + th) + 0.5 * x * (1.0 - th * th) * (k * (1.0 + 3.0 * 0.044715 * x2))
    return g, dg


def _sigmoid(x):
    return 0.5 * jnp.tanh(0.5 * x) + 0.5


def _swap_halves(x):
    w = x.shape[-1]
    first = (_lane(x.shape) % HEAD_DIM) < (HEAD_DIM // 2)
    return jnp.where(first, pltpu.roll(x, w - HEAD_DIM // 2, 1), pltpu.roll(x, HEAD_DIM // 2, 1))


def _tile_lanes(x, reps):
    return x if reps == 1 else jnp.concatenate([x] * reps, axis=1)


def _fold_lanes(x, period):
    w = x.shape[-1]
    while w > period:
        w //= 2
        x = x + pltpu.roll(x, w, 1)
    return x


def _mm(a, b, *, mode, tm, tn, tk, out_dtype=F32, add=None, name,
        a_lead=None, b_lead=None, b_sharded=False, out_into=None,
        b_koff=0, out_joff=0, out_n=None, deps=()):
    ash = a.shape[1:] if a_lead is not None else a.shape
    bsh = b.shape[1:] if b_lead is not None else b.shape
    if b_sharded:
        bsh = (b.shape[1], N_CHIPS * b.shape[2])
    if mode == 'nn':
        (M, K), (K2, N) = ash, bsh
    elif mode == 'nt':
        (M, K), (N, K2) = ash, bsh
    else:
        (K, M), (K2, N) = ash, bsh
    assert K == K2 or (mode == 'nt' and K2 > K), (ash, bsh, mode)
    assert M % tm == 0 and N % tn == 0 and K % tk == 0, (M, N, K, tm, tn, tk)
    nk = K // tk
    dims = {'nn': ((1,), (0,)), 'nt': NT, 'tn': TN}[mode]

    def lead(spec_shape, imap, lead_idx):
        if lead_idx is None:
            return pl.BlockSpec(spec_shape, imap)
        return pl.BlockSpec((None,) + spec_shape, lambda i, j, k: (lead_idx,) + imap(i, j, k))

    if mode == 'tn':
        a_spec = lead((tk, tm), lambda i, j, k: (k, i), a_lead)
    else:
        a_spec = lead((tm, tk), lambda i, j, k: (i, k), a_lead)
    if b_sharded:
        per = b.shape[2] // (tk if mode == 'nt' else tn)
        assert per * (tk if mode == 'nt' else tn) == b.shape[2] and mode != 'tn'
        if mode == 'nt':
            b_spec = pl.BlockSpec((None, tn, tk), lambda i, j, k: ((k + b_koff) // per, j, (k + b_koff) % per))
        else:
            b_spec = pl.BlockSpec((None, tk, tn), lambda i, j, k: (j // per, k, j % per))
    elif mode == 'nt':
        b_spec = lead((tn, tk), lambda i, j, k: (j, k + b_koff), b_lead)
    else:
        b_spec = lead((tk, tn), lambda i, j, k: (k, j), b_lead)
    o_spec = pl.BlockSpec((tm, tn), lambda i, j, k: (i, j + out_joff))
    n_out = N if out_n is None else out_n
    in_specs = [a_spec, b_spec]
    operands = [a, b]
    if add is not None:
        in_specs.append(pl.BlockSpec((tm, tn), lambda i, j, k: (i, j)))
        operands.append(add)
    aliases = {}
    if out_into is not None:
        in_specs.append(ANY)
        operands.append(out_into)
        aliases = {len(operands) - 1: 0}
    in_specs += [ANY] * len(deps)
    operands += list(deps)
    has_add = add is not None
    acc_in_out = nk > 1 and out_dtype == F32

    def body(*refs):
        a_ref, b_ref = refs[0], refs[1]
        pos = 2
        add_ref = None
        if has_add:
            add_ref = refs[pos]
            pos += 1
        if out_into is not None:
            pos += 1
        pos += len(deps)
        o_ref = refs[pos]
        acc_ref = refs[pos + 1] if (nk > 1 and not acc_in_out) else None
        p = _dot(a_ref[...], b_ref[...], dims)
        if nk == 1:
            if has_add:
                p = p + add_ref[...]
            o_ref[...] = p.astype(o_ref.dtype)
            return
        k = pl.program_id(2)
        tgt = o_ref if acc_in_out else acc_ref

        @pl.when(k == 0)
        def _():
            tgt[...] = p + add_ref[...] if has_add else p

        @pl.when(k > 0)
        def _():
            tgt[...] += p

        if not acc_in_out:
            @pl.when(k == nk - 1)
            def _():
                o_ref[...] = acc_ref[...].astype(o_ref.dtype)

    out_shape = SDS((M, n_out), out_dtype)
    scratch = [pltpu.VMEM((tm, tn), F32)] if (nk > 1 and not acc_in_out) else []
    return pl.pallas_call(
        body, grid=(M // tm, N // tn, nk), in_specs=in_specs, out_specs=o_spec, out_shape=out_shape,
        scratch_shapes=scratch, input_output_aliases=aliases, name=name,
        compiler_params=_cp("parallel", "parallel", "arbitrary"))(*operands)


def _rms_bwd_rows(xv, g, dh, dres):
    r = lax.rsqrt(jnp.mean(xv * xv, axis=-1, keepdims=True) + EPS)
    xh = xv * r
    gy = dh * g
    dx = r * (gy - xh * jnp.mean(xh * gy, axis=-1, keepdims=True)) + dres
    return dx, jnp.sum(dh * xh, axis=0, keepdims=True)


def _mm_nt_sharded_rms(a, b, x, g, dres, *, tm, name, deps=()):
    a3 = a if a.ndim == 3 else a[None]
    A, M, ka = a3.shape
    S, N, ns = b.shape
    per = S // A
    assert ka == per * ns and M % tm == 0 and N == x.shape[1], (a3.shape, b.shape, x.shape)

    def body(a_ref, b_ref, x_ref, g_ref, dres_ref, dx_ref, dxb_ref, dg_ref):
        acc = None
        for s in range(S):
            lo = (s % per) * ns
            p = _dot(a_ref[s // per, :, lo:lo + ns], b_ref[s], NT)
            acc = p if acc is None else acc + p
        dx, dg = _rms_bwd_rows(x_ref[...], g_ref[...], acc, dres_ref[...])
        dx_ref[...] = dx
        dxb_ref[...] = dx.astype(dxb_ref.dtype)

        @pl.when(pl.program_id(0) == 0)
        def _():
            dg_ref[...] = jnp.zeros_like(dg_ref)
        dg_ref[...] += dg

    rows = pl.BlockSpec((tm, N), lambda i: (i, 0))
    return pl.pallas_call(
        _after(body, 5, deps), grid=(M // tm,),
        in_specs=[pl.BlockSpec((A, tm, ka), lambda i: (0, i, 0)),
                  pl.BlockSpec((S, N, ns), lambda i: (0, 0, 0), pipeline_mode=pl.Buffered(1)),
                  rows, _full((1, N)), rows] + [ANY] * len(deps),
        out_specs=[rows, rows, _full((1, N))],
        out_shape=[SDS((M, N), F32), SDS((M, N), MXU_DTYPE), SDS((1, N), F32)], name=name,
        compiler_params=_cp("arbitrary"))(a3, b, x, g, dres, *deps)


def _norm_mm(x, g, b, *, tm, tn, name, deps=()):
    M, K = x.shape
    S, K2, ns = b.shape
    per = ns // tn
    assert K == K2 and per * tn == ns and M % tm == 0, (x.shape, b.shape)

    def body(x_ref, g_ref, b_ref, o_ref, h_ref):
        @pl.when(pl.program_id(1) == 0)
        def _():
            xv = x_ref[...]
            r = lax.rsqrt(jnp.mean(xv * xv, axis=-1, keepdims=True) + EPS)
            h_ref[...] = (xv * r * g_ref[...]).astype(h_ref.dtype)
        o_ref[...] = _dot(h_ref[...], b_ref[...]).astype(o_ref.dtype)

    return pl.pallas_call(
        _after(body, 3, deps), grid=(M // tm, S * per),
        in_specs=[pl.BlockSpec((tm, K), lambda i, j: (i, 0)), _full((1, K)),
                  pl.BlockSpec((None, K, tn), lambda i, j: (j // per, 0, j % per))] + [ANY] * len(deps),
        out_specs=[pl.BlockSpec((tm, tn), lambda i, j: (i, j)), pl.BlockSpec((tm, K), lambda i, j: (i, 0))],
        out_shape=[SDS((M, S * ns), ACT_DTYPE), SDS((M, K), MXU_DTYPE)], name=name,
        compiler_params=_cp("parallel", "arbitrary"))(x, g, b, *deps)


def _after(body, n_in, deps):
    nd = len(deps)
    if nd == 0:
        return body
    return lambda *refs: body(*refs[:n_in], *refs[n_in + nd:])


def _down_proj_loss(act, w, x1, target, *, tm, name):
    T, K = act.shape
    D = w.shape[1]

    def body(a_ref, w_ref, x_ref, t_ref, loss_ref, dy_ref, dyb_ref):
        i = pl.program_id(0)
        d = (x_ref[...] + _dot(a_ref[...], w_ref[...])) - t_ref[...]
        dy = d * (1.0 / D)
        dy_ref[...] = dy
        dyb_ref[...] = dy.astype(dyb_ref.dtype)
        part = jnp.sum(jnp.sum(d * d, axis=1, keepdims=True), axis=0, keepdims=True) * (0.5 / D)

        @pl.when(i == 0)
        def _():
            loss_ref[...] = jnp.zeros_like(loss_ref)
        loss_ref[...] += jnp.broadcast_to(part, loss_ref.shape)

    rows = pl.BlockSpec((tm, D), lambda i: (i, 0))
    return pl.pallas_call(
        body, grid=(T // tm,), in_specs=[pl.BlockSpec((tm, K), lambda i: (i, 0)), _full((K, D)), rows, rows],
        out_specs=[_full((1, 128)), rows, rows],
        out_shape=[SDS((1, 128), F32), SDS((T, D), F32), SDS((T, D), MXU_DTYPE)],
        name=name, compiler_params=_cp("arbitrary"))(act, w, x1, target)


def _pool_lane_consts(shape):
    lane = _lane(shape)
    grp = lane // (POOL_WIDTH // 4)
    win = jnp.where(grp == 0, 2, jnp.where(grp == 1, 4, jnp.where(grp == 2, 8, 16)))
    return grp, win


def _pool_select(grp, s2, s4, s8, s16):
    return jnp.where(grp == 0, s2, jnp.where(grp == 1, s4, jnp.where(grp == 2, s8, s16)))


def _pool_diff(xe, row0, tr):
    s2 = xe + pltpu.roll(xe, 1, 0)
    s4 = s2 + pltpu.roll(s2, 2, 0)
    s8 = s4 + pltpu.roll(s4, 4, 0)
    s16 = s8 + pltpu.roll(s8, 8, 0)
    shape = (tr, POOL_WIDTH)
    grp, win = _pool_lane_consts(shape)
    sums = _pool_select(grp, s2[16:], s4[16:], s8[16:], s16[16:])
    t = row0 + _row(shape)
    cnt = jnp.minimum(t + 1, win).astype(F32)
    return sums / cnt - xe[16:]


def _pool_fwd(z, wbd, scale, *, tr, name):
    T = z.shape[0]
    hb = tr // 16

    def body(x_ref, xp_ref, w_ref, s_ref, o_ref):
        i = pl.program_id(0)
        halo = jnp.where(i == 0, 0.0, xp_ref[...].astype(F32))
        diff = _pool_diff(jnp.concatenate([halo, x_ref[...].astype(F32)], axis=0), i * tr, tr)
        o_ref[...] = (_dot(diff, w_ref[...]) * s_ref[...]).astype(o_ref.dtype)

    return pl.pallas_call(
        body, grid=(T // tr,),
        in_specs=[pl.BlockSpec((tr, POOL_WIDTH), lambda i: (i, 0)),
                  pl.BlockSpec((16, POOL_WIDTH), lambda i: (jnp.maximum(i * hb - 1, 0), 0)),
                  _full((POOL_WIDTH, POOL_WIDTH)), _full((1, POOL_WIDTH))],
        out_specs=pl.BlockSpec((tr, POOL_WIDTH), lambda i: (i, 0)),
        out_shape=SDS((T, POOL_WIDTH), MXU_DTYPE), name=name, compiler_params=_cp("parallel"))(z, z, wbd, scale)


def _pool_bwd_tile(i, n, tr, x, xprev, dpa, dpa_next, wbd, scale):
    halo = jnp.where(i == 0, 0.0, xprev)
    diff = _pool_diff(jnp.concatenate([halo, x], axis=0), i * tr, tr)
    mixed = _dot(diff, wbd)
    dscale = jnp.sum(dpa * mixed, axis=0, keepdims=True)
    dnext = jnp.where(i == n - 1, 0.0, dpa_next)
    dmix_e = jnp.concatenate([dpa, dnext], axis=0) * scale
    ddiff_e = _dot(dmix_e, wbd, NT)
    dwbd = _dot(diff, dmix_e[:tr], TN)
    shape = (tr + 16, POOL_WIDTH)
    grp, win = _pool_lane_consts(shape)
    t = i * tr + _row(shape)
    e = ddiff_e / jnp.minimum(t + 1, win).astype(F32)
    nrow = tr + 16
    a2 = e + pltpu.roll(e, nrow - 1, 0)
    a4 = a2 + pltpu.roll(a2, nrow - 2, 0)
    a8 = a4 + pltpu.roll(a4, nrow - 4, 0)
    a16 = a8 + pltpu.roll(a8, nrow - 8, 0)
    dx = _pool_select(grp, a2, a4, a8, a16)[:tr] - ddiff_e[:tr]
    return dx, dwbd, dscale


def _norm_rope(x, g, cos, sin_signed, seg):
    reps = x.shape[1] // 128
    ms = _split_dot(x * x, seg) * (1.0 / HEAD_DIM)
    r = lax.rsqrt(ms + EPS)
    xn = x * r * g
    c, s = _tile_lanes(cos, reps), _tile_lanes(sin_signed, reps)
    return xn * c + _swap_halves(xn) * s


def _norm_rope_bwd(x, g, cos, sin_signed, seg, dout):
    reps = x.shape[1] // 128
    c, s = _tile_lanes(cos, reps), _tile_lanes(sin_signed, reps)
    dxn = dout * c + _swap_halves(dout * s)
    ms = _split_dot(x * x, seg) * (1.0 / HEAD_DIM)
    r = lax.rsqrt(ms + EPS)
    xh = x * r
    gy = dxn * g
    dx = r * (gy - xh * (_split_dot(xh * gy, seg) * (1.0 / HEAD_DIM)))
    dg = jnp.sum(dxn * xh, axis=0, keepdims=True)
    return dx, dg


def _dup_heads(k):
    first = _lane(k.shape) < HEAD_DIM
    kr = pltpu.roll(k, HEAD_DIM, 1)
    return jnp.concatenate([jnp.where(first, k, kr), jnp.where(first, kr, k)], axis=1)


def _qkv_prep(z, cos, sin_signed, gq, gk, seg, *, tr, name):
    T = z.shape[0]

    def body(qa_ref, qb_ref, kv_ref, c_ref, s_ref, gq_ref, gk_ref, seg_ref, q_ref, k_ref, v_ref):
        c, s, seg_m = c_ref[...], s_ref[...], seg_ref[...]
        scale = HEAD_DIM ** -0.5
        qa = _norm_rope(qa_ref[...].astype(F32), gq_ref[...], c, s, seg_m) * scale
        qb = _norm_rope(qb_ref[...].astype(F32), gq_ref[...], c, s, seg_m) * scale
        q_ref[...] = jnp.concatenate([qa, qb], axis=1).astype(q_ref.dtype)
        kv = kv_ref[...].astype(F32)
        k = _norm_rope(kv[:, :KV_WIDTH], gk_ref[...], c, s, seg_m[:128, :128])
        k_ref[...] = _dup_heads(k).astype(k_ref.dtype)
        v_ref[...] = _dup_heads(kv[:, KV_WIDTH:]).astype(v_ref.dtype)

    col = lambda j: pl.BlockSpec((tr, 256), lambda i: (i, j))
    tab = pl.BlockSpec((tr, 128), lambda i: (i, 0))
    return pl.pallas_call(
        body, grid=(T // tr,),
        in_specs=[col(1), col(2), col(3), tab, tab, _full((1, 256)), _full((1, 128)), _full((256, 256))],
        out_specs=[pl.BlockSpec((tr, 512), lambda i: (i, 0)), col(0), col(0)],
        out_shape=[SDS((T, 512), MXU_DTYPE), SDS((T, 256), MXU_DTYPE), SDS((T, 256), MXU_DTYPE)],
        name=name, compiler_params=_cp("parallel"))(z, z, z, cos, sin_signed, gq, gk, seg)


GROUP_HEADS = 4
GROUP_ROWS = GROUP_HEADS * ATTN_BLOCK
ALL_ROWS = N_Q_HEADS * ATTN_BLOCK


def _attn_mask(has_prev):
    qi = _row((ALL_ROWS, 2 * ATTN_BLOCK)) % ATTN_BLOCK
    kj = _lane((ALL_ROWS, 2 * ATTN_BLOCK))
    return (kj > qi) & (kj <= qi + ATTN_BLOCK) & ((kj >= ATTN_BLOCK) | has_prev)


FWD_STEP_BLOCKS = 8
BWD_STEP_BLOCKS = 2


def _band(prev, cur, blk):
    lo = cur[(blk - 1) * ATTN_BLOCK:blk * ATTN_BLOCK] if blk else prev
    return jnp.concatenate([lo, cur[blk * ATTN_BLOCK:(blk + 1) * ATTN_BLOCK]], axis=0)


def _stack_heads(x, g):
    first = _lane((ATTN_BLOCK, 128)) < HEAD_DIM
    parts = []
    for pair in (2 * g, 2 * g + 1):
        x128 = x[:, 128 * pair:128 * (pair + 1)]
        zero = jnp.zeros_like(x128)
        parts += [jnp.where(first, x128, zero), jnp.where(first, zero, x128)]
    return jnp.concatenate(parts, axis=0)


def _unstack_heads(y):
    first = _lane((ATTN_BLOCK, 128)) < HEAD_DIM
    b = ATTN_BLOCK
    return jnp.concatenate([jnp.where(first, y[0:b], y[b:2 * b]), jnp.where(first, y[2 * b:3 * b], y[3 * b:4 * b])],
                           axis=1)


def _sink_col(sk_ref):
    return jnp.concatenate([jnp.broadcast_to(sk_ref[h:h + 1, 0:1], (ATTN_BLOCK, 1)) for h in range(N_Q_HEADS)],
                           axis=0)


def _by_group(a8, b2, dims=((1,), (0,))):
    return jnp.concatenate([_dot(a8[:GROUP_ROWS], b2[:, :128], dims), _dot(a8[GROUP_ROWS:], b2[:, 128:], dims)],
                           axis=0)


def _softmax_exp(q8, k2, mask, sink):
    s = jnp.where(mask, _by_group(q8, k2, NT), NEG_BIG)
    m = jnp.maximum(jnp.max(s, axis=1, keepdims=True), sink)
    p = jnp.exp(s - m)
    ps = jnp.exp(sink - m)
    return p, ps, 1.0 / (jnp.sum(p, axis=1, keepdims=True) + ps)


def _attn_fwd(q, k, v, sinks_b, *, name):
    T = q.shape[0]
    nb = T // ATTN_BLOCK
    STEP_BLOCKS = min(FWD_STEP_BLOCKS, nb)
    STEP_ROWS = STEP_BLOCKS * ATTN_BLOCK

    def body(q_ref, kc_ref, kp_ref, vc_ref, vp_ref, sk_ref, o_ref):
        n = pl.program_id(0)
        kc, kp, vc, vp = kc_ref[...], kp_ref[...], vc_ref[...], vp_ref[...]
        sink = _sink_col(sk_ref)
        for blk in range(STEP_BLOCKS):
            rows = slice(blk * ATTN_BLOCK, (blk + 1) * ATTN_BLOCK)
            mask = _attn_mask((n > 0) if blk == 0 else True)
            k2, v2 = _band(kp, kc, blk), _band(vp, vc, blk)
            qv = q_ref[rows, :]
            q8 = jnp.concatenate([_stack_heads(qv, 0), _stack_heads(qv, 1)], axis=0)
            p, _, inv = _softmax_exp(q8, k2, mask, sink)
            o8 = _by_group(p, v2) * inv
            o_ref[rows, :] = jnp.concatenate([_unstack_heads(o8[:GROUP_ROWS]), _unstack_heads(o8[GROUP_ROWS:])],
                                             axis=1).astype(o_ref.dtype)

    cur = lambda w: pl.BlockSpec((STEP_ROWS, w), lambda n: (n, 0))
    prev = lambda w: pl.BlockSpec((ATTN_BLOCK, w), lambda n: (jnp.maximum(STEP_BLOCKS * n - 1, 0), 0))
    return pl.pallas_call(
        body, grid=(nb // STEP_BLOCKS,),
        in_specs=[cur(512), cur(256), prev(256), cur(256), prev(256), _full((8, 128))],
        out_specs=cur(512), out_shape=SDS((T, 512), MXU_DTYPE), name=name,
        compiler_params=_cp("parallel"))(q, k, k, v, v, sinks_b)


def _attn_bwd(q, k, v, sinks_b, do, *, name, deps=()):
    T = q.shape[0]
    nb = T // ATTN_BLOCK
    STEP_BLOCKS = min(BWD_STEP_BLOCKS, nb)
    STEP_ROWS = STEP_BLOCKS * ATTN_BLOCK

    def body(q_ref, kc_ref, kp_ref, vc_ref, vp_ref, sk_ref, do_ref,
             dq_ref, dkc_ref, dkp_ref, dvc_ref, dvp_ref, dsk_ref):
        n = pl.program_id(0)
        kc, kp, vc, vp = kc_ref[...], kp_ref[...], vc_ref[...], vp_ref[...]
        sink = _sink_col(sk_ref)

        @pl.when(n == 0)
        def _():
            dsk_ref[...] = jnp.zeros_like(dsk_ref)

        for blk in range(STEP_BLOCKS):
            rows = slice(blk * ATTN_BLOCK, (blk + 1) * ATTN_BLOCK)
            mask = _attn_mask((n > 0) if blk == 0 else True)
            k2, v2 = _band(kp, kc, blk), _band(vp, vc, blk)
            qv, dov = q_ref[rows, :], do_ref[rows, :]
            q8 = jnp.concatenate([_stack_heads(qv, 0), _stack_heads(qv, 1)], axis=0)
            do8 = jnp.concatenate([_stack_heads(dov, 0), _stack_heads(dov, 1)], axis=0)
            p, ps, inv = _softmax_exp(q8, k2, mask, sink)
            pn = p * inv
            delta = jnp.sum(do8 * _by_group(pn, v2), axis=1, keepdims=True)
            ds = pn * (_by_group(do8, v2, NT) - delta)
            dq8 = _by_group(ds, k2)
            dq_ref[rows, :] = jnp.concatenate([_unstack_heads(dq8[:GROUP_ROWS]), _unstack_heads(dq8[GROUP_ROWS:])],
                                              axis=1)
            dk = jnp.concatenate([_dot(ds[:GROUP_ROWS], q8[:GROUP_ROWS], TN),
                                  _dot(ds[GROUP_ROWS:], q8[GROUP_ROWS:], TN)], axis=1)
            dv = jnp.concatenate([_dot(pn[:GROUP_ROWS], do8[:GROUP_ROWS], TN),
                                  _dot(pn[GROUP_ROWS:], do8[GROUP_ROWS:], TN)], axis=1)
            wsink = (ps * inv) * delta
            for h in range(N_Q_HEADS):
                dsink = -jnp.sum(wsink[ATTN_BLOCK * h:ATTN_BLOCK * (h + 1)], axis=0, keepdims=True)
                dsk_ref[h:h + 1, :] += jnp.broadcast_to(dsink, (1, 128))
            dkp_ref[rows, :] = dk[:ATTN_BLOCK]
            dkc_ref[rows, :] = dk[ATTN_BLOCK:]
            dvp_ref[rows, :] = dv[:ATTN_BLOCK]
            dvc_ref[rows, :] = dv[ATTN_BLOCK:]

    cur = lambda w: pl.BlockSpec((STEP_ROWS, w), lambda n: (n, 0))
    prev = lambda w: pl.BlockSpec((ATTN_BLOCK, w), lambda n: (jnp.maximum(STEP_BLOCKS * n - 1, 0), 0))
    f = lambda w: SDS((T, w), F32)
    return pl.pallas_call(
        _after(body, 7, deps), grid=(nb // STEP_BLOCKS,),
        in_specs=[cur(512), cur(256), prev(256), cur(256), prev(256), _full((8, 128)), cur(512)] + [ANY] * len(deps),
        out_specs=[cur(512), cur(256), cur(256), cur(256), cur(256), _full((8, 128))],
        out_shape=[f(512), f(256), f(256), f(256), f(256), SDS((8, 128), F32)],
        name=name, compiler_params=_cp("arbitrary"))(q, k, k, v, v, sinks_b, do, *deps)


def _mixer_ab_bwd(z, cos, sin_signed, gq, gk, seg, dq, dkc, dkp, dvc, dvp, dpa, wbd, scale, dz, *, tr, name, deps=()):
    T = z.shape[0]
    n = T // tr
    hb = tr // 16
    ab = tr // ATTN_BLOCK

    def unfold(cur, nxt_tile, nxt_halo, i):
        nxt = jnp.concatenate([nxt_tile[ATTN_BLOCK:], jnp.where(i == n - 1, 0.0, nxt_halo)], axis=0)
        tot = cur + nxt
        first = _lane((tr, 128)) < HEAD_DIM
        a = tot[:, :128]
        b = tot[:, 128:]
        a = a + pltpu.roll(a, HEAD_DIM, 1)
        b = b + pltpu.roll(b, HEAD_DIM, 1)
        return jnp.where(first, a, b)

    def body(xp_ref, xpp_ref, qa_ref, qb_ref, kv_ref, c_ref, s_ref, gq_ref, gk_ref, seg_ref,
             dq_ref, dkc_ref, dkp_ref, dkh_ref, dvc_ref, dvp_ref, dvh_ref, dpa_ref, dpan_ref, w_ref, sc_ref, _dz_in,
             dz_ref, dgq_ref, dgk_ref, dw_ref, dsc_ref):
        i = pl.program_id(0)
        c, s, seg_m = c_ref[...], s_ref[...], seg_ref[...]
        scale_q = HEAD_DIM ** -0.5
        dqv = dq_ref[...] * scale_q
        dxa, dga = _norm_rope_bwd(qa_ref[...].astype(F32), gq_ref[...], c, s, seg_m, dqv[:, :256])
        dxb, dgb = _norm_rope_bwd(qb_ref[...].astype(F32), gq_ref[...], c, s, seg_m, dqv[:, 256:])
        dk = unfold(dkc_ref[...], dkp_ref[...], dkh_ref[...], i)
        dv = unfold(dvc_ref[...], dvp_ref[...], dvh_ref[...], i)
        kv = kv_ref[...].astype(F32)
        dxk, dgk = _norm_rope_bwd(kv[:, :KV_WIDTH], gk_ref[...], c, s, seg_m[:128, :128], dk)
        dxp, dwbd, dscale = _pool_bwd_tile(i, n, tr, xp_ref[...].astype(F32), xpp_ref[...].astype(F32),
                                           dpa_ref[...], dpan_ref[...],
                                           w_ref[...], sc_ref[...])
        dz_ref[...] = jnp.concatenate([dxp, dxa, dxb, dxk, dv], axis=1).astype(dz_ref.dtype)

        @pl.when(i == 0)
        def _():
            dgq_ref[...] = jnp.zeros_like(dgq_ref)
            dgk_ref[...] = jnp.zeros_like(dgk_ref)
            dw_ref[...] = jnp.zeros_like(dw_ref)
            dsc_ref[...] = jnp.zeros_like(dsc_ref)
        dgq_ref[...] += _fold_lanes(dga + dgb, HEAD_DIM)
        dgk_ref[...] += _fold_lanes(dgk, HEAD_DIM)
        dw_ref[...] += dwbd
        dsc_ref[...] += dscale

    col = lambda j: pl.BlockSpec((tr, 256), lambda i: (i, j))
    rows = lambda w: pl.BlockSpec((tr, w), lambda i: (i, 0))
    nxt_blk = pl.BlockSpec((ATTN_BLOCK, 256), lambda i: (jnp.minimum((i + 1) * ab, T // ATTN_BLOCK - 1), 0))
    prev16 = pl.BlockSpec((16, 256), lambda i: (jnp.maximum(i * hb - 1, 0), 0))
    next16 = pl.BlockSpec((16, 256), lambda i: (jnp.minimum((i + 1) * hb, T // 16 - 1), 0))
    return pl.pallas_call(
        _after(body, 22, deps), grid=(n,),
        in_specs=[col(0), prev16, col(1), col(2), col(3), rows(128), rows(128),
                  _full((1, 256)), _full((1, 128)), _full((256, 256)),
                  rows(512), rows(256), rows(256), nxt_blk, rows(256), rows(256), nxt_blk,
                  rows(256), next16, _full((256, 256)), _full((1, 256)), ANY] + [ANY] * len(deps),
        out_specs=[rows(1024), _full((1, 256)), _full((1, 128)), _full((256, 256)), _full((1, 256))],
        out_shape=[SDS((T, IN_COLS), MXU_DTYPE), SDS((1, 256), F32), SDS((1, 128), F32),
                   SDS((256, 256), F32), SDS((1, 256), F32)],
        input_output_aliases={21: 0}, name=name, compiler_params=_cp("arbitrary"))(
            z, z, z, z, z, cos, sin_signed, gq, gk, seg, dq, dkc, dkp, dkp, dvc, dvp, dvp, dpa, dpa, wbd, scale, dz,
            *deps)


def _sgu_common(zu, zv, vn, seg):
    u, du = _gelu_and_grad(zu)
    gv, dgv = _gelu_and_grad(zv)
    ms = _split_dot(gv * gv, seg) * (1.0 / HEAD_DIM)
    r = lax.rsqrt(ms + EPS)
    xh = gv * r
    return u, du, dgv, r, xh, xh * vn


def _sgu_fwd(z, wtril, bexp, vn, seg, *, tr, name):
    T = z.shape[0]
    nch = tr // CHUNK

    def body(u_ref, v_ref, w_ref, b_ref, vn_ref, seg_ref, o_ref):
        u, _, _, _, _, vg = _sgu_common(u_ref[...].astype(F32), v_ref[...].astype(F32), vn_ref[...], seg_ref[...])
        grp = _lane((CHUNK, SGU_WIDTH)) // HEAD_DIM
        outs = []
        for ch in range(nch):
            vc = vg[ch * CHUNK:(ch + 1) * CHUNK]
            s = b_ref[...]
            for g in range(4):
                s = s + jnp.where(grp == g, _dot(w_ref[g], vc), 0.0)
            outs.append(u[ch * CHUNK:(ch + 1) * CHUNK] * s)
        o_ref[...] = jnp.concatenate(outs, axis=0).astype(o_ref.dtype)

    col = lambda j: pl.BlockSpec((tr, 256), lambda i: (i, j))
    return pl.pallas_call(
        body, grid=(T // tr,),
        in_specs=[col(4), col(5), _full((4, CHUNK, CHUNK)), _full((CHUNK, 256)), _full((1, 256)), _full((256, 256))],
        out_specs=col(0), out_shape=SDS((T, SGU_WIDTH), MXU_DTYPE), name=name,
        compiler_params=_cp("parallel"))(z, z, wtril, bexp, vn, seg)


def _mixers_fwd(z, cos, sin_signed, gq, gk, seg, wbd, scale, wtril, bexp, vn, *, tr, name):
    T = z.shape[0]
    hb = tr // 16
    nch = tr // CHUNK
    width = POOL_WIDTH + ATTN_WIDTH + 2 * KV_WIDTH + 2 * SGU_WIDTH

    def body(z_ref, xp_ref, c_ref, s_ref, gq_ref, gk_ref, seg_ref, w_ref, sc_ref, wt_ref, b_ref, vn_ref,
             pa_ref, q_ref, k_ref, v_ref, sg_ref):
        i = pl.program_id(0)
        zt = z_ref[...].astype(F32)
        seg_m = seg_ref[...]
        halo = jnp.where(i == 0, 0.0, xp_ref[...].astype(F32))
        diff = _pool_diff(jnp.concatenate([halo, zt[:, :POOL_WIDTH]], axis=0), i * tr, tr)
        pa_ref[...] = (_dot(diff, w_ref[...]) * sc_ref[...]).astype(pa_ref.dtype)
        c, s = c_ref[...], s_ref[...]
        scale_q = HEAD_DIM ** -0.5
        qa = _norm_rope(zt[:, 256:512], gq_ref[...], c, s, seg_m) * scale_q
        qb = _norm_rope(zt[:, 512:768], gq_ref[...], c, s, seg_m) * scale_q
        q_ref[...] = jnp.concatenate([qa, qb], axis=1).astype(q_ref.dtype)
        kk = _norm_rope(zt[:, 768:896], gk_ref[...], c, s, seg_m[:128, :128])
        k_ref[...] = _dup_heads(kk).astype(k_ref.dtype)
        v_ref[...] = _dup_heads(zt[:, 896:1024]).astype(v_ref.dtype)
        u, _, _, _, _, vg = _sgu_common(zt[:, 1024:1280], zt[:, 1280:1536], vn_ref[...], seg_m)
        grp = _lane((CHUNK, SGU_WIDTH)) // HEAD_DIM
        outs = []
        for ch in range(nch):
            vc = vg[ch * CHUNK:(ch + 1) * CHUNK]
            acc = b_ref[...]
            for g in range(4):
                acc = acc + jnp.where(grp == g, _dot(wt_ref[g], vc), 0.0)
            outs.append(u[ch * CHUNK:(ch + 1) * CHUNK] * acc)
        sg_ref[...] = jnp.concatenate(outs, axis=0).astype(sg_ref.dtype)

    rows = lambda w: pl.BlockSpec((tr, w), lambda i: (i, 0))
    return pl.pallas_call(
        body, grid=(T // tr,),
        in_specs=[rows(width), pl.BlockSpec((16, POOL_WIDTH), lambda i: (jnp.maximum(i * hb - 1, 0), 0)),
                  rows(128), rows(128), _full((1, 256)), _full((1, 128)), _full((256, 256)),
                  _full((POOL_WIDTH, POOL_WIDTH)), _full((1, POOL_WIDTH)),
                  _full((4, CHUNK, CHUNK)), _full((CHUNK, 256)), _full((1, 256))],
        out_specs=[rows(256), rows(512), rows(256), rows(256), rows(256)],
        out_shape=[SDS((T, 256), MXU_DTYPE), SDS((T, 512), MXU_DTYPE), SDS((T, 256), MXU_DTYPE),
                   SDS((T, 256), MXU_DTYPE), SDS((T, 256), MXU_DTYPE)],
        name=name, compiler_params=_cp("parallel"))(z, z, cos, sin_signed, gq, gk, seg, wbd, scale, wtril, bexp, vn)


def _sgu_bwd(z, wtril, bexp, vn, seg, dsg, dz, *, tr, name):
    T = z.shape[0]
    nch = tr // CHUNK

    def body(u_ref, v_ref, w_ref, b_ref, vn_ref, seg_ref, d_ref, _dz_in, dz_ref, dw_ref, db_ref, dvn_ref):
        i = pl.program_id(0)
        seg_m = seg_ref[...]
        vn_v = vn_ref[...]
        u, du, dgv, r, xh, vg = _sgu_common(u_ref[...].astype(F32), v_ref[...].astype(F32), vn_v, seg_m)
        d = d_ref[...]
        grp = _lane((CHUNK, SGU_WIDTH)) // HEAD_DIM
        tril = _row((CHUNK, CHUNK)) >= _lane((CHUNK, CHUNK))

        @pl.when(i == 0)
        def _():
            dw_ref[...] = jnp.zeros_like(dw_ref)
            db_ref[...] = jnp.zeros_like(db_ref)
            dvn_ref[...] = jnp.zeros_like(dvn_ref)

        dus, dvgs = [], []
        for ch in range(nch):
            sl = slice(ch * CHUNK, (ch + 1) * CHUNK)
            vc = vg[sl]
            s = b_ref[...]
            for g in range(4):
                s = s + jnp.where(grp == g, _dot(w_ref[g], vc), 0.0)
            dus.append(d[sl] * s)
            ds = d[sl] * u[sl]
            db_ref[...] += _split_dot(ds, seg_m)
            dvg = jnp.zeros((CHUNK, SGU_WIDTH), F32)
            for g in range(4):
                dsm = jnp.where(grp == g, ds, 0.0)
                dvg = dvg + jnp.where(grp == g, _dot(w_ref[g], ds, TN), 0.0)
                dw_ref[g] += jnp.where(tril, _dot(dsm, vc, NT), 0.0)
            dvgs.append(dvg)
        dup = jnp.concatenate(dus, axis=0)
        dvg = jnp.concatenate(dvgs, axis=0)
        dvn_ref[...] += _fold_lanes(jnp.sum(dvg * xh, axis=0, keepdims=True), HEAD_DIM)
        gy = dvg * vn_v
        dgvv = r * (gy - xh * (_split_dot(xh * gy, seg_m) * (1.0 / HEAD_DIM)))
        dz_ref[...] = jnp.concatenate([dup * du, dgvv * dgv], axis=1).astype(dz_ref.dtype)

    col = lambda j: pl.BlockSpec((tr, 256), lambda i: (i, j))
    return pl.pallas_call(
        body, grid=(T // tr,),
        in_specs=[col(4), col(5), _full((4, CHUNK, CHUNK)), _full((CHUNK, 256)), _full((1, 256)), _full((256, 256)),
                  col(0), ANY],
        out_specs=[pl.BlockSpec((tr, 512), lambda i: (i, 2)), _full((4, CHUNK, CHUNK)), _full((CHUNK, 256)),
                   _full((1, 256))],
        out_shape=[SDS((T, IN_COLS), MXU_DTYPE), SDS((4, CHUNK, CHUNK), F32), SDS((CHUNK, 256), F32),
                   SDS((1, 256), F32)],
        input_output_aliases={7: 0}, name=name, compiler_params=_cp("arbitrary"))(
            z, z, wtril, bexp, vn, seg, dsg, dz)


def _mixers_bwd(z, cos, sin_signed, gq, gk, seg, dq, dkc, dkp, dvc, dvp, dpa, wbd, scale, wtril, bexp, vn, dsg, dz,
                *, tr, name, deps=()):
    T = z.shape[0]
    n = T // tr
    hb = tr // 16
    ab = tr // ATTN_BLOCK
    nch = tr // CHUNK
    width = POOL_WIDTH + ATTN_WIDTH + 2 * KV_WIDTH + 2 * SGU_WIDTH

    def unfold(cur, nxt_tile, nxt_halo, i):
        nxt = jnp.concatenate([nxt_tile[ATTN_BLOCK:], jnp.where(i == n - 1, 0.0, nxt_halo)], axis=0)
        tot = cur + nxt
        first = _lane((tr, 128)) < HEAD_DIM
        a = tot[:, :128]
        b = tot[:, 128:]
        a = a + pltpu.roll(a, HEAD_DIM, 1)
        b = b + pltpu.roll(b, HEAD_DIM, 1)
        return jnp.where(first, a, b)

    def body(z_ref, xpp_ref, c_ref, s_ref, gq_ref, gk_ref, seg_ref,
             dq_ref, dkc_ref, dkp_ref, dkh_ref, dvc_ref, dvp_ref, dvh_ref, dpa_ref, dpan_ref, w_ref, sc_ref,
             wt_ref, b_ref, vn_ref, d_ref, _dz_in,
             dz_ref, dgq_ref, dgk_ref, dw_ref, dsc_ref, dws_ref, db_ref, dvn_ref):
        i = pl.program_id(0)
        c, s, seg_m = c_ref[...], s_ref[...], seg_ref[...]

        @pl.when(i == 0)
        def _():
            dgq_ref[...] = jnp.zeros_like(dgq_ref)
            dgk_ref[...] = jnp.zeros_like(dgk_ref)
            dw_ref[...] = jnp.zeros_like(dw_ref)
            dsc_ref[...] = jnp.zeros_like(dsc_ref)
            dws_ref[...] = jnp.zeros_like(dws_ref)
            db_ref[...] = jnp.zeros_like(db_ref)
            dvn_ref[...] = jnp.zeros_like(dvn_ref)

        scale_q = HEAD_DIM ** -0.5
        dqv = dq_ref[...] * scale_q
        dxa, dga = _norm_rope_bwd(z_ref[:, 256:512].astype(F32), gq_ref[...], c, s, seg_m, dqv[:, :256])
        dxb, dgb = _norm_rope_bwd(z_ref[:, 512:768].astype(F32), gq_ref[...], c, s, seg_m, dqv[:, 256:])
        dk = unfold(dkc_ref[...], dkp_ref[...], dkh_ref[...], i)
        dv = unfold(dvc_ref[...], dvp_ref[...], dvh_ref[...], i)
        dxk, dgk = _norm_rope_bwd(z_ref[:, 768:896].astype(F32), gk_ref[...], c, s, seg_m[:128, :128], dk)
        dxp, dwbd, dscale = _pool_bwd_tile(i, n, tr, z_ref[:, :POOL_WIDTH].astype(F32), xpp_ref[...].astype(F32),
                                           dpa_ref[...], dpan_ref[...], w_ref[...], sc_ref[...])
        dz_ref[:, :1024] = jnp.concatenate([dxp, dxa, dxb, dxk, dv], axis=1).astype(dz_ref.dtype)
        dgq_ref[...] += _fold_lanes(dga + dgb, HEAD_DIM)
        dgk_ref[...] += _fold_lanes(dgk, HEAD_DIM)
        dw_ref[...] += dwbd
        dsc_ref[...] += dscale

        vn_v = vn_ref[...]
        u, du, dgv, r, xh, vg = _sgu_common(z_ref[:, 1024:1280].astype(F32), z_ref[:, 1280:1536].astype(F32),
                                            vn_v, seg_m)
        d = d_ref[...]
        grp = _lane((CHUNK, SGU_WIDTH)) // HEAD_DIM
        tril = _row((CHUNK, CHUNK)) >= _lane((CHUNK, CHUNK))
        dus, dvgs = [], []
        for ch in range(nch):
            sl = slice(ch * CHUNK, (ch + 1) * CHUNK)
            vc = vg[sl]
            acc = b_ref[...]
            for g in range(4):
                acc = acc + jnp.where(grp == g, _dot(wt_ref[g], vc), 0.0)
            dus.append(d[sl] * acc)
            ds = d[sl] * u[sl]
            db_ref[...] += _split_dot(ds, seg_m)
            dvg = jnp.zeros((CHUNK, SGU_WIDTH), F32)
            for g in range(4):
                dsm = jnp.where(grp == g, ds, 0.0)
                dvg = dvg + jnp.where(grp == g, _dot(wt_ref[g], ds, TN), 0.0)
                dws_ref[g] += jnp.where(tril, _dot(dsm, vc, NT), 0.0)
            dvgs.append(dvg)
        dup = jnp.concatenate(dus, axis=0)
        dvg = jnp.concatenate(dvgs, axis=0)
        dvn_ref[...] += _fold_lanes(jnp.sum(dvg * xh, axis=0, keepdims=True), HEAD_DIM)
        gy = dvg * vn_v
        dgvv = r * (gy - xh * (_split_dot(xh * gy, seg_m) * (1.0 / HEAD_DIM)))
        dz_ref[:, 1024:1536] = jnp.concatenate([dup * du, dgvv * dgv], axis=1).astype(dz_ref.dtype)

    rows = lambda w: pl.BlockSpec((tr, w), lambda i: (i, 0))
    nxt_blk = pl.BlockSpec((ATTN_BLOCK, 256), lambda i: (jnp.minimum((i + 1) * ab, T // ATTN_BLOCK - 1), 0))
    prev16 = pl.BlockSpec((16, 256), lambda i: (jnp.maximum(i * hb - 1, 0), 0))
    next16 = pl.BlockSpec((16, 256), lambda i: (jnp.minimum((i + 1) * hb, T // 16 - 1), 0))
    return pl.pallas_call(
        _after(body, 23, deps), grid=(n,),
        in_specs=[rows(width), prev16, rows(128), rows(128),
                  _full((1, 256)), _full((1, 128)), _full((256, 256)),
                  rows(512), rows(256), rows(256), nxt_blk, rows(256), rows(256), nxt_blk,
                  rows(256), next16, _full((256, 256)), _full((1, 256)),
                  _full((4, CHUNK, CHUNK)), _full((CHUNK, 256)), _full((1, 256)), rows(256), ANY]
        + [ANY] * len(deps),
        out_specs=[rows(width), _full((1, 256)), _full((1, 128)), _full((256, 256)), _full((1, 256)),
                   _full((4, CHUNK, CHUNK)), _full((CHUNK, 256)), _full((1, 256))],
        out_shape=[SDS((T, IN_COLS), MXU_DTYPE), SDS((1, 256), F32), SDS((1, 128), F32),
                   SDS((256, 256), F32), SDS((1, 256), F32),
                   SDS((4, CHUNK, CHUNK), F32), SDS((CHUNK, 256), F32), SDS((1, 256), F32)],
        input_output_aliases={22: 0}, name=name, compiler_params=_cp("arbitrary"))(
            z, z, cos, sin_signed, gq, gk, seg, dq, dkc, dkp, dkp, dvc, dvp, dvp, dpa, dpa, wbd, scale,
            wtril, bexp, vn, dsg, dz, *deps)


def _merge_fwd(pa, at, sg, wa, wb, wc, z, x, w_out, *, tm, tn, name):
    T = pa.shape[0]
    gb = GATE_COL0 // tn
    nb = D_MODEL // tn

    def body(pa_ref, at_ref, sg_ref, wa_ref, wb_ref, wc_ref, g0_ref, g1_ref, g2_ref, x_ref, wo_ref,
             m_ref, y_ref, x1_ref):
        j = pl.program_id(1)
        acc = None
        for idx, (op_ref, w_ref, g_ref) in enumerate(((pa_ref, wa_ref, g0_ref), (at_ref, wb_ref, g1_ref),
                                                      (sg_ref, wc_ref, g2_ref))):
            y = _dot(op_ref[...], w_ref[...])
            y_ref[idx] = y.astype(y_ref.dtype)
            t = _sigmoid(g_ref[...].astype(F32)) * y
            acc = t if acc is None else acc + t
        merged = acc.astype(m_ref.dtype)
        m_ref[...] = merged
        p = _dot(merged, wo_ref[...])

        @pl.when(j == 0)
        def _():
            x1_ref[...] = x_ref[...] + p

        @pl.when(j > 0)
        def _():
            x1_ref[...] += p

    op = lambda w: pl.BlockSpec((tm, w), lambda i, j: (i, 0))
    wt = lambda k: pl.BlockSpec((k, tn), lambda i, j: (0, j))
    gate = lambda b: pl.BlockSpec((tm, tn), lambda i, j: (i, gb + b * nb + j))
    return pl.pallas_call(
        body, grid=(T // tm, nb),
        in_specs=[op(256), op(512), op(256), wt(256), wt(512), wt(256), gate(0), gate(1), gate(2),
                  op(D_MODEL), pl.BlockSpec((tn, D_MODEL), lambda i, j: (j, 0))],
        out_specs=[pl.BlockSpec((tm, tn), lambda i, j: (i, j)), pl.BlockSpec((3, tm, tn), lambda i, j: (0, i, j)),
                   op(D_MODEL)],
        out_shape=[SDS((T, D_MODEL), MXU_DTYPE), SDS((3, T, D_MODEL), MXU_DTYPE), SDS((T, D_MODEL), F32)],
        name=name, compiler_params=_cp("parallel", "arbitrary"))(pa, at, sg, wa, wb, wc, z, z, z, x, w_out)


def _out_dx_merge_bwd(dxb, w_out, y, z, ws, xs, *, tm, tn, name):
    T = dxb.shape[0]
    gb = GATE_COL0 // tn
    nb = D_MODEL // tn
    nr = T // tm
    widths = [w.shape[0] for w in ws]

    def body(dx_ref, w_ref, y_ref, g_ref, *refs):
        w_refs, x_refs = refs[0:3], refs[3:6]
        dz_ref, dx_refs, dw_refs = refs[6], refs[7:10], refs[10:13]
        dm_ref, acc_refs = refs[13], refs[14:17]
        i, b, j = pl.program_id(0), pl.program_id(1), pl.program_id(2)

        @pl.when((b == 0) & (j == 0))
        def _():
            dm = _dot(dx_ref[...], w_ref[...], NT)
            for jj in range(nb):
                dm_ref[jj] = dm[:, jj * tn:(jj + 1) * tn]

        d = dm_ref[j]
        g = _sigmoid(g_ref[...].astype(F32))
        dy = (d * g).astype(MXU_DTYPE)
        dz_ref[...] = (d * y_ref[...].astype(F32) * g * (1.0 - g)).astype(dz_ref.dtype)
        for branch in range(3):
            @pl.when(b == branch)
            def _():
                p = _dot(dy, w_refs[branch][...], NT)
                q = _dot(x_refs[branch][...], dy, TN)

                @pl.when(j == 0)
                def _():
                    dx_refs[branch][...] = p

                @pl.when(j > 0)
                def _():
                    dx_refs[branch][...] += p

                @pl.when(i == 0)
                def _():
                    acc_refs[branch][j] = q

                @pl.when(i > 0)
                def _():
                    acc_refs[branch][j] += q

        @pl.when((i == nr - 1) & (b == 2) & (j == nb - 1))
        def _():
            for branch in range(3):
                for jj in range(nb):
                    dw_refs[branch][:, jj * tn:(jj + 1) * tn] = acc_refs[branch][jj]

    wspec = lambda k: pl.BlockSpec((k, tn), lambda i, b, j: (0, j))
    rows = lambda k: pl.BlockSpec((tm, k), lambda i, b, j: (i, 0))
    return pl.pallas_call(
        body, grid=(nr, 3, nb),
        in_specs=[rows(D_MODEL),
                  pl.BlockSpec((D_MODEL, D_MODEL), lambda i, b, j: (0, 0), pipeline_mode=pl.Buffered(1)),
                  pl.BlockSpec((None, tm, tn), lambda i, b, j: (b, i, j)),
                  pl.BlockSpec((tm, tn), lambda i, b, j: (i, gb + b * nb + j))]
        + [wspec(k) for k in widths] + [rows(k) for k in widths],
        out_specs=[pl.BlockSpec((tm, tn), lambda i, b, j: (i, gb + b * nb + j))]
        + [rows(k) for k in widths] + [_full((k, D_MODEL)) for k in widths],
        out_shape=[SDS((T, IN_COLS), MXU_DTYPE)] + [SDS((T, k), F32) for k in widths]
        + [SDS((k, D_MODEL), F32) for k in widths],
        scratch_shapes=[pltpu.VMEM((nb, tm, tn), F32)] + [pltpu.VMEM((nb, k, tn), F32) for k in widths],
        name=name, compiler_params=_cp("arbitrary", "arbitrary", "arbitrary"))(dxb, w_out, y, z, *ws, *xs)


def _conv3(xe, w, b):
    return (w[0:1] * pltpu.roll(xe, 2, 0) + w[1:2] * pltpu.roll(xe, 1, 0) + w[2:3] * xe)[8:] + b


def _conv_act_fwd(up, cw, cb, *, tr, tc, name):
    T = up.shape[0]
    nc = D_FF // tc
    hb = tr // HALO

    def body(ug_ref, ugp_ref, uv_ref, uvp_ref, wg_ref, wv_ref, bg_ref, bv_ref, o_ref):
        i = pl.program_id(1)
        first = i == 0

        def halo_tile(prev_ref, cur_ref):
            prev8 = prev_ref[...].astype(F32)[HALO - 8:]
            return jnp.concatenate([jnp.where(first, 0.0, prev8), cur_ref[...].astype(F32)], axis=0)

        cg = _conv3(halo_tile(ugp_ref, ug_ref), wg_ref[...], bg_ref[...])
        cv = _conv3(halo_tile(uvp_ref, uv_ref), wv_ref[...], bv_ref[...])
        o_ref[...] = (cg * _sigmoid(cg) * cv).astype(o_ref.dtype)

    tile = lambda off: pl.BlockSpec((tr, tc), lambda j, i: (i, off + j))
    prev = lambda off: pl.BlockSpec((HALO, tc), lambda j, i: (jnp.maximum(i * hb - 1, 0), off + j))
    par = lambda rows, off: pl.BlockSpec((rows, tc), lambda j, i: (0, off + j))
    return pl.pallas_call(
        body, grid=(nc, T // tr),
        in_specs=[tile(0), prev(0), tile(nc), prev(nc), par(3, 0), par(3, nc), par(1, 0), par(1, nc)],
        out_specs=pl.BlockSpec((tr, tc), lambda j, i: (i, j)),
        out_shape=SDS((T, D_FF), MXU_DTYPE), name=name,
        compiler_params=_cp("parallel", "parallel"))(up, up, up, up, cw, cw, cb, cb)


def _conv_act_bwd(up, cw, cb, dact, *, tr, tc, name, deps=()):
    T = up.shape[0]
    nc = D_FF // tc
    hb = tr // 8
    hbu = tr // HALO
    nr = T // tr

    def body(ug_ref, ugp_ref, ugn_ref, uv_ref, uvp_ref, uvn_ref, da_ref, dan_ref, wg_ref, wv_ref, bg_ref, bv_ref,
             du_ref, dwg_ref, dwv_ref, dbg_ref, dbv_ref):
        i = pl.program_id(1)
        first, last = i == 0, i == nr - 1
        da = jnp.concatenate([da_ref[...], jnp.where(last, 0.0, dan_ref[...])], axis=0)

        def with_halos(prev_ref, cur_ref, next_ref):
            prev8 = prev_ref[...].astype(F32)[HALO - 8:]
            next8 = next_ref[...].astype(F32)[:8]
            return jnp.concatenate([jnp.where(first, 0.0, prev8), cur_ref[...].astype(F32), next8], axis=0)

        uge = with_halos(ugp_ref, ug_ref, ugn_ref)
        uve = with_halos(uvp_ref, uv_ref, uvn_ref)
        wg, wv = wg_ref[...], wv_ref[...]
        ug1, ug2 = pltpu.roll(uge, 1, 0)[8:], pltpu.roll(uge, 2, 0)[8:]
        uv1, uv2 = pltpu.roll(uve, 1, 0)[8:], pltpu.roll(uve, 2, 0)[8:]
        cg = wg[0:1] * ug2 + wg[1:2] * ug1 + wg[2:3] * uge[8:] + bg_ref[...]
        cv = wv[0:1] * uv2 + wv[1:2] * uv1 + wv[2:3] * uve[8:] + bv_ref[...]
        sg = _sigmoid(cg)
        dcg = da * cv * (sg * (1.0 + cg * (1.0 - sg)))
        dcv = da * (cg * sg)
        nrow = tr + 8

        def back(dc, w):
            return (w[2:3] * dc + w[1:2] * pltpu.roll(dc, nrow - 1, 0) + w[0:1] * pltpu.roll(dc, nrow - 2, 0))[:tr]

        du_ref[0] = back(dcg, wg).astype(du_ref.dtype)
        du_ref[1] = back(dcv, wv).astype(du_ref.dtype)

        def wgrad(dc, u0, u1, u2):
            d = dc[:tr]
            rows = [jnp.sum(d * u2[:tr], axis=0, keepdims=True), jnp.sum(d * u1[:tr], axis=0, keepdims=True),
                    jnp.sum(d * u0[8:8 + tr], axis=0, keepdims=True)]
            return jnp.concatenate(rows, axis=0), jnp.sum(d, axis=0, keepdims=True)

        dwg, dbg = wgrad(dcg, uge, ug1, ug2)
        dwv, dbv = wgrad(dcv, uve, uv1, uv2)

        @pl.when(first)
        def _():
            dwg_ref[...] = jnp.zeros_like(dwg_ref)
            dwv_ref[...] = jnp.zeros_like(dwv_ref)
            dbg_ref[...] = jnp.zeros_like(dbg_ref)
            dbv_ref[...] = jnp.zeros_like(dbv_ref)
        dwg_ref[...] += dwg
        dwv_ref[...] += dwv
        dbg_ref[...] += dbg
        dbv_ref[...] += dbv

    tile = lambda off: pl.BlockSpec((tr, tc), lambda j, i: (i, off + j))
    prev = lambda off: pl.BlockSpec((HALO, tc), lambda j, i: (jnp.maximum(i * hbu - 1, 0), off + j))
    nxt = lambda off: pl.BlockSpec((HALO, tc), lambda j, i: (jnp.minimum((i + 1) * hbu, T // HALO - 1), off + j))
    dnext = pl.BlockSpec((8, tc), lambda j, i: (jnp.minimum((i + 1) * hb, T // 8 - 1), j))
    par = lambda rows, off: pl.BlockSpec((rows, tc), lambda j, i: (0, off + j))
    acc = lambda rows: pl.BlockSpec((rows, tc), lambda j, i: (0, j))
    return pl.pallas_call(
        _after(body, 12, deps), grid=(nc, nr),
        in_specs=[tile(0), prev(0), nxt(0), tile(nc), prev(nc), nxt(nc), tile(0), dnext,
                  par(3, 0), par(3, nc), par(1, 0), par(1, nc)] + [ANY] * len(deps),
        out_specs=[pl.BlockSpec((2, tr, tc), lambda j, i: (0, i, j)), acc(3), acc(3), acc(1), acc(1)],
        out_shape=[SDS((2, T, D_FF), MXU_DTYPE), SDS((3, D_FF), F32), SDS((3, D_FF), F32),
                   SDS((1, D_FF), F32), SDS((1, D_FF), F32)],
        name=name, compiler_params=_cp("parallel", "arbitrary"))(
            up, up, up, up, up, up, dact, dact, cw, cw, cb, cb, *deps)


def _row_tile(rows, cap):
    t = min(cap, rows)
    t -= t % 8
    while rows % t:
        t -= 8
    return t


def _adamw(w, g, m, v, *, tr, name, copy_g=False):
    R, C = w.shape
    assert R % tr == 0, (R, tr)

    def body(w_ref, g_ref, m_ref, v_ref, d_ref, nm_ref, nv_ref, *rest):
        gv = g_ref[...]
        mn = ADAM_B1 * m_ref[...] + (1.0 - ADAM_B1) * gv
        vn = ADAM_B2 * v_ref[...] + (1.0 - ADAM_B2) * (gv * gv)
        m_hat = mn / (1.0 - ADAM_B1 ** ADAM_STEP)
        v_hat = vn / (1.0 - ADAM_B2 ** ADAM_STEP)
        d_ref[...] = -ADAM_LR * (m_hat / (jnp.sqrt(v_hat) + ADAM_EPS) + ADAM_WD * w_ref[...])
        nm_ref[...] = mn
        nv_ref[...] = vn
        if copy_g:
            rest[0][...] = gv

    rows = pl.BlockSpec((tr, C), lambda i: (i, 0))
    n_out = 4 if copy_g else 3
    return pl.pallas_call(
        body, grid=(R // tr,), in_specs=[rows] * 4, out_specs=[rows] * n_out,
        out_shape=[SDS((R, C), F32)] * n_out, name=name, compiler_params=_cp("parallel"))(w, g, m, v)


def _sum_slots(r, *, tr, name):
    S, R, C = r.shape
    assert R % tr == 0, (R, tr)

    def body(r_ref, o_ref):
        acc = r_ref[0]
        for s in range(1, S):
            acc = acc + r_ref[s]
        o_ref[...] = acc

    return pl.pallas_call(
        body, grid=(R // tr,), in_specs=[pl.BlockSpec((S, tr, C), lambda i: (0, i, 0))],
        out_specs=pl.BlockSpec((tr, C), lambda i: (i, 0)), out_shape=SDS((R, C), F32),
        name=name, compiler_params=_cp("parallel"))(r)


def _pair_add(g4, h, pos, *, name):
    A, _, r, C = g4.shape
    cs = C if A == N_CHIPS else C // N_CHIPS
    tr = _row_tile(r, 256)
    if A == N_CHIPS:
        g_map, h_map = (lambda t, i, pos: (t, pos[1], i, 0)), (lambda t, i, pos: (t, i, 0))
    else:
        g_map, h_map = (lambda t, i, pos: (0, pos[1], i, t)), (lambda t, i, pos: (0, i, t))

    def body(pos_ref, g_ref, h_ref, o_ref):
        o_ref[...] = (g_ref[...] + h_ref[...]).astype(o_ref.dtype)

    grid_spec = pltpu.PrefetchScalarGridSpec(
        num_scalar_prefetch=1, grid=(N_CHIPS, r // tr),
        in_specs=[pl.BlockSpec((None, None, tr, cs), g_map), pl.BlockSpec((None, tr, cs), h_map)],
        out_specs=pl.BlockSpec((None, tr, cs), lambda t, i, pos: (t, i, 0)))
    return pl.pallas_call(body, grid_spec=grid_spec, out_shape=SDS((N_CHIPS, r, cs), COMM_DTYPE), name=name,
                          compiler_params=_cp("parallel", "parallel"))(pos, g4, h)


def _chip_sum(p, r2, f_into, pos, layer, *, name):
    _, r, cs = p.shape
    tr = _row_tile(r, 256)

    def body(pos_ref, own_ref, r_ref, *rest):
        o_ref = rest[-1]
        o_ref[...] = ((own_ref[...].astype(F32) + r_ref[0].astype(F32)) + r_ref[1].astype(F32)) + r_ref[2].astype(F32)

    in_specs = [pl.BlockSpec((None, tr, cs), lambda i, pos: (pos[0], i, 0)),
                pl.BlockSpec((3, tr, cs), lambda i, pos: (0, i, 0))]
    operands = [pos, p, r2]
    aliases = {}
    if f_into is not None:
        in_specs.append(ANY)
        operands.append(f_into)
        aliases = {3: 0}
    grid_spec = pltpu.PrefetchScalarGridSpec(
        num_scalar_prefetch=1, grid=(r // tr,), in_specs=in_specs,
        out_specs=pl.BlockSpec((None, None, tr, cs), lambda i, pos: (layer, pos[1], i, 0)))
    return pl.pallas_call(body, grid_spec=grid_spec, out_shape=SDS((DEPTH, 2, r, cs), F32), name=name,
                          input_output_aliases=aliases, compiler_params=_cp("parallel"))(*operands)


def _mesh_pos():
    return lax.axis_index("x"), lax.axis_index("y"), lax.axis_index("c")


HBM = pl.BlockSpec(memory_space=pltpu.HBM)
SEM = pl.BlockSpec(memory_space=pltpu.SEMAPHORE)
DATAFLOW = pltpu.SideEffectType.DATAFLOW_SIDE_EFFECTING
CHIP_FLIPS = (2, 1, 3)


def _chip_peers():
    x, y, c = _mesh_pos()
    return 2 * x + y, [(1 - x, y, c), (x, 1 - y, c), (1 - x, 1 - y, c)], (x, y, 1 - c), c


def _split_start(arrays, n_copies, issue, *, name, deps=()):
    k = len(arrays)
    nd = len(deps)

    def body(*refs):
        issue(refs[:k], refs[k + nd], refs[k + nd + 1])
        refs[2 * k + nd + 2][...] = jnp.zeros((8, 128), F32)

    out = pl.pallas_call(
        body, name=name,
        out_shape=(pltpu.SemaphoreType.DMA((n_copies,)), pltpu.SemaphoreType.DMA((n_copies,)),
                   *[pltpu.HBM(a.shape, a.dtype) for a in arrays], SDS((8, 128), F32)),
        in_specs=[HBM] * k + [ANY] * nd, out_specs=(SEM, SEM, *[HBM] * k, pl.BlockSpec(memory_space=pltpu.VMEM)),
        input_output_aliases={i: 2 + i for i in range(k)},
        compiler_params=pltpu.CompilerParams(has_side_effects=DATAFLOW))(
            *[pltpu.with_memory_space_constraint(a, pltpu.HBM) for a in arrays], *deps)
    return (out[0], out[1]), list(out[2:2 + k]), out[2 + k]


def _split_wait(sems, arrays, after, waits, *, name):
    k = len(arrays)
    afters = tuple(after) if isinstance(after, (tuple, list)) else (after,)

    def body(*refs):
        waits(refs[:k], refs[k], refs[k + 1])

    out = pl.pallas_call(
        body, name=name, out_shape=tuple(pltpu.HBM(a.shape, a.dtype) for a in arrays),
        in_specs=[HBM] * k + [SEM, SEM] + [ANY] * len(afters), out_specs=tuple([HBM] * k),
        input_output_aliases={i: i for i in range(k)},
        compiler_params=pltpu.CompilerParams(has_side_effects=DATAFLOW))(*arrays, sems[0], sems[1], *afters)
    return list(out)


def _wait_both(cp):
    cp.wait_send()
    cp.wait_recv()


def _cast_place(shard, pos, dtype, *, name, layer=None, slots=N_CHIPS, which=0):
    R, C = shard.shape[-2:]
    tr = R if R % 8 else _row_tile(R, 256)
    if layer is None:
        in_spec = pl.BlockSpec((tr, C), lambda i, pos: (i, 0))
    else:
        in_spec = pl.BlockSpec((None, tr, C), lambda i, pos: (layer, i, 0))

    def body(pos_ref, x_ref, o_ref):
        o_ref[...] = x_ref[...].astype(o_ref.dtype)

    grid_spec = pltpu.PrefetchScalarGridSpec(
        num_scalar_prefetch=1, grid=(R // tr,), in_specs=[in_spec],
        out_specs=pl.BlockSpec((None, tr, C), lambda i, pos: (pos[which], i, 0)))
    return pl.pallas_call(body, grid_spec=grid_spec, out_shape=SDS((slots, R, C), dtype), name=name,
                          compiler_params=_cp("parallel"))(pos, shard)


def _device_peers():
    x, y, c = _mesh_pos()
    peers = [(x ^ ((f >> 2) & 1), y ^ ((f >> 1) & 1), c ^ (f & 1)) for f in range(1, N_DEV)]
    return 4 * x + 2 * y + c, peers


class _Gather:
    def __init__(self, lands, name, deps=(), all_devices=False, halves=False):
        n = len(lands)
        self.name, self.halves = name, halves
        npeer = N_DEV - 1 if all_devices else N_CHIPS - 1
        if halves:
            lands = [a.reshape(a.shape[0], 2, a.shape[1] // 2, a.shape[2]) for a in lands]

        def copies(refs, ss, rs):
            if halves:
                me, peers, _, c = _chip_peers()
                own = lambda r: r.at[me, c]
            else:
                me, peers = _device_peers() if all_devices else _chip_peers()[:2]
                own = lambda r: r.at[me]
            return [pltpu.make_async_remote_copy(
                src_ref=own(refs[w]), dst_ref=own(refs[w]), send_sem=ss.at[npeer * w + p],
                recv_sem=rs.at[npeer * w + p], device_id=peers[p], device_id_type=MESH)
                for w in range(n) for p in range(npeer)]

        def issue(refs, ss, rs):
            for cp in copies(refs, ss, rs):
                cp.start()

        def waits(refs, ss, rs):
            for cp in copies(refs, ss, rs):
                _wait_both(cp)

        self._waits = waits
        self.sems, self.arrays, self.token = _split_start(list(lands), npeer * n, issue, name=name + "_start",
                                                          deps=deps)

    def wait(self, after):
        arrays = _split_wait(self.sems, self.arrays, after, self._waits, name=self.name + "_wait")
        if not self.halves:
            return arrays
        n = len(arrays)

        def copies(refs, ss, rs):
            me, _, sibling, c = _chip_peers()
            return [pltpu.make_async_remote_copy(
                src_ref=refs[w].at[me ^ CHIP_FLIPS[p], c], dst_ref=refs[w].at[me ^ CHIP_FLIPS[p], c],
                send_sem=ss.at[3 * w + p], recv_sem=rs.at[3 * w + p], device_id=sibling, device_id_type=MESH)
                for w in range(n) for p in range(3)]

        def issue(refs, ss, rs):
            for cp in copies(refs, ss, rs):
                cp.start()

        def waits(refs, ss, rs):
            for cp in copies(refs, ss, rs):
                _wait_both(cp)

        sems, arrays, _ = _split_start(arrays, 3 * n, issue, name=self.name + "_share_start")
        arrays = _split_wait(sems, arrays, after, waits, name=self.name + "_share_wait")
        return [a.reshape(a.shape[0], 2 * a.shape[2], a.shape[3]) for a in arrays]


def _swap_halves_start(g4s, *, name):
    n = len(g4s)
    lands = [lax.empty((g.shape[0],) + g.shape[2:], g.dtype) for g in g4s]

    def copies(refs, ss, rs):
        _, _, sibling, c = _chip_peers()
        return [pltpu.make_async_remote_copy(
            src_ref=refs[w].at[:, 1 - c], dst_ref=refs[n + w], send_sem=ss.at[w], recv_sem=rs.at[w],
            device_id=sibling, device_id_type=MESH) for w in range(n)]

    def issue(refs, ss, rs):
        for cp in copies(refs, ss, rs):
            cp.start()

    def waits(refs, ss, rs):
        for cp in copies(refs, ss, rs):
            _wait_both(cp)

    sems, arrays, token = _split_start(list(g4s) + lands, n, issue, name=name + "_start")
    return sems, arrays, token, waits


def _scatter_start(parts, *, name, deps=()):
    n = len(parts)
    lands = [lax.empty((3,) + p.shape[1:], p.dtype) for p in parts]

    def copies(refs, ss, rs):
        me, peers, _, _ = _chip_peers()
        return [pltpu.make_async_remote_copy(
            src_ref=refs[w].at[me ^ CHIP_FLIPS[p]], dst_ref=refs[n + w].at[p],
            send_sem=ss.at[3 * w + p], recv_sem=rs.at[3 * w + p], device_id=peers[p], device_id_type=MESH)
            for w in range(n) for p in range(3)]

    def issue(refs, ss, rs):
        for cp in copies(refs, ss, rs):
            cp.start()

    def waits(refs, ss, rs):
        for cp in copies(refs, ss, rs):
            _wait_both(cp)

    sems, arrays, token = _split_start(list(parts) + lands, 3 * n, issue, name=name + "_start", deps=deps)
    return sems, arrays, token, waits


def _pair_share_start(fs, layer, *, name):
    n = len(fs)

    def copies(refs, ss, rs):
        _, _, sibling, c = _chip_peers()
        return [pltpu.make_async_remote_copy(
            src_ref=refs[w].at[layer, c], dst_ref=refs[w].at[layer, c], send_sem=ss.at[w], recv_sem=rs.at[w],
            device_id=sibling, device_id_type=MESH) for w in range(n)]

    def issue(refs, ss, rs):
        for cp in copies(refs, ss, rs):
            cp.start()

    def waits(refs, ss, rs):
        for cp in copies(refs, ss, rs):
            _wait_both(cp)

    sems, arrays, token = _split_start(list(fs), n, issue, name=name + "_start")
    return sems, arrays, token, waits


BIG = ('w_in', 'w_proj_a', 'w_proj_b', 'w_proj_c', 'w_out', 'w_up', 'w_down')
BIG_SHARD_AXIS = {'w_in': 2, 'w_proj_a': 2, 'w_proj_b': 2, 'w_proj_c': 2, 'w_out': 1, 'w_up': 2, 'w_down': 1}
SMALL = ('norm1', 'q_norm', 'k_norm', 'sinks', 'w_pool', 'pool_scale', 'sgu_v_norm', 'w_s', 'b_s', 'norm2',
         'conv_b', 'conv_w')
WEIGHTS = ('norm1', 'w_in', 'q_norm', 'k_norm', 'sinks', 'w_pool', 'pool_scale', 'sgu_v_norm', 'w_s', 'b_s',
           'w_proj_a', 'w_proj_b', 'w_proj_c', 'w_out', 'norm2', 'w_up', 'conv_w', 'conv_b', 'w_down')


def _rope_tables(positions):
    inv_freq = ROPE_THETA ** (-jnp.arange(0, HEAD_DIM, 2, dtype=F32) / HEAD_DIM)
    ang = positions.astype(F32)[:, None] * inv_freq
    cos, sin = jnp.cos(ang), jnp.sin(ang)
    c = jnp.concatenate([cos, cos], axis=1)
    s = jnp.concatenate([-sin, sin], axis=1)
    return jnp.concatenate([c, c], axis=1), jnp.concatenate([s, s], axis=1)


def _block_diag4(w):
    out = jnp.zeros((POOL_WIDTH, POOL_WIDTH), w.dtype)
    for g in range(4):
        out = lax.dynamic_update_slice(out, w[g], (g * HEAD_DIM, g * HEAD_DIM))
    return out


def _local_step(x, target, cos, sin, sp, sched):
    T = x.shape[0]
    tm1 = min(1024, T)
    tm = min(512, T)
    tr = min(1024, T)
    trc = min(512, T)
    tkt = min(2048, T)
    seg = _seg_matrix(256, HEAD_DIM)
    saved = []
    xl = x
    for l in range(DEPTH):
        p = f"l{l}_"
        c = dict(
            g1=sp['norm1'][l][None], g2=sp['norm2'][l][None],
            wbd=_block_diag4(sp['w_pool'][l]).astype(MXU_DTYPE), scale=sp['pool_scale'][l][None],
            gq=jnp.tile(sp['q_norm'][l], 4)[None], gk=jnp.tile(sp['k_norm'][l], 2)[None],
            sinks=jnp.broadcast_to(sp['sinks'][l][:, None], (N_Q_HEADS, 128)),
            wtril=jnp.tril(sp['w_s'][l]).astype(MXU_DTYPE),
            bexp=jnp.repeat(sp['b_s'][l].T, HEAD_DIM, axis=1), vn=jnp.tile(sp['sgu_v_norm'][l], 4)[None],
            cb=sp['conv_b'][l][None])
        c['w_in'] = sched.weight('w_in', l, xl)
        z, h1 = _norm_mm(xl, c['g1'], c['w_in'], tm=tm1, tn=1152, name=p + "in_proj",
                         deps=sched.start_tokens() if l == 0 else ())
        pa, q, k, v, sg = _mixers_fwd(z, cos, sin, c['gq'], c['gk'], seg, c['wbd'], c['scale'], c['wtril'],
                                      c['bexp'], c['vn'], tr=tr, name=p + "mixers")
        at = _attn_fwd(q, k, v, c['sinks'], name=p + "attn")
        for n in ('w_proj_a', 'w_proj_b', 'w_proj_c', 'w_out'):
            c[n] = sched.weight(n, l, (pa, at, sg))
        merged, y3, x1 = _merge_fwd(pa, at, sg, c['w_proj_a'], c['w_proj_b'], c['w_proj_c'], z, xl, c['w_out'],
                                    tm=tm1, tn=512, name=p + "merge_out_proj")
        for n in ('w_up', 'conv_w', 'w_down'):
            c[n] = sched.weight(n, l, x1)
        up, h2 = _norm_mm(x1, c['g2'], c['w_up'], tm=tm1, tn=1408, name=p + "up_proj")
        act = _conv_act_fwd(up, c['conv_w'], c['cb'], tr=trc, tc=1408, name=p + "conv_act")
        saved.append(dict(c, x=xl, h1=h1, z=z, pa=pa, q=q, k=k, v=v, at=at, sg=sg, merged=merged, y3=y3,
                          x1=x1, h2=h2, up=up, act=act))
        if l < DEPTH - 1:
            xl = _mm(act, c['w_down'], mode='nn', add=x1, tm=tm, tn=D_MODEL, tk=D_FF, name=p + "down_proj")
        else:
            loss_row, dx, dxb = _down_proj_loss(act, c['w_down'], x1, target, tm=tm, name=p + "down_proj_loss")

    gs = {n: [None] * DEPTH for n in SMALL}
    for l in reversed(range(DEPTH)):
        p = f"l{l}_b_"
        s = saved[l]
        gb = {}
        dact = _mm(dxb, s['w_down'], mode='nt', tm=tm1, tn=1408, tk=D_MODEL, name=p + "down_dx")
        gb['w_down'] = _mm(s['act'], dxb, mode='tn', tm=1408, tn=D_MODEL, tk=tkt, name=p + "down_dw")
        toks = sched.slot(l, 'down', gb['w_down'])
        dup, dwg, dwv, dbg, dbv = _conv_act_bwd(s['up'], s['conv_w'], s['cb'], dact, tr=min(1024, T), tc=256,
                                                name=p + "conv_act", deps=toks)
        gs['conv_w'][l] = jnp.concatenate([dwg, dwv], axis=1)
        gs['conv_b'][l] = jnp.concatenate([dbg, dbv], axis=1)[0]
        toks = sched.slot(l, 'conv', dup)
        for half in range(2):
            gb['w_up'] = _mm(s['h2'], dup, mode='tn', b_lead=half, tm=D_MODEL, tn=1408, tk=tkt,
                             out_into=gb.get('w_up'), out_joff=2 * half, out_n=2 * D_FF, name=p + f"up_dw{half}",
                             deps=toks if half == 0 else ())
        toks = sched.slot(l, 'ffn', gb['w_up'], gb)
        dx1, dx1b, dg2 = _mm_nt_sharded_rms(dup, s['w_up'], s['x1'], s['g2'], dx, tm=tm,
                                            name=p + "up_dx_rms2", deps=toks)
        gs['norm2'][l] = dg2[0]
        gb['w_out'] = _mm(s['merged'], dx1b, mode='tn', tm=D_MODEL, tn=D_MODEL, tk=tkt, name=p + "out_dw")
        (dz, dpa, dat, dsg, gb['w_proj_a'], gb['w_proj_b'], gb['w_proj_c']) = _out_dx_merge_bwd(
            dx1b, s['w_out'], s['y3'], s['z'], [s['w_proj_a'], s['w_proj_b'], s['w_proj_c']],
            [s['pa'], s['at'], s['sg']], tm=tm1, tn=512, name=p + "out_dx_merge")
        toks = sched.slot(l, 'mid', dz)
        dq, dkc, dkp, dvc, dvp, dsk = _attn_bwd(s['q'], s['k'], s['v'], s['sinks'], dat, name=p + "attn", deps=toks)
        gs['sinks'][l] = dsk[:, 0]
        toks = sched.slot(l, 'attn', dq)
        dz, dgq, dgk, dwbd, dsc, dws, dbrows, dvn = _mixers_bwd(
            s['z'], cos, sin, s['gq'], s['gk'], seg, dq, dkc, dkp, dvc, dvp, dpa, s['wbd'], s['scale'],
            s['wtril'], s['bexp'], s['vn'], dsg, dz, tr=tr, name=p + "mixers", deps=toks)
        gs['q_norm'][l] = dgq[0, :HEAD_DIM]
        gs['k_norm'][l] = dgk[0, :HEAD_DIM]
        gs['w_pool'][l] = jnp.stack([dwbd[g * HEAD_DIM:(g + 1) * HEAD_DIM, g * HEAD_DIM:(g + 1) * HEAD_DIM]
                                     for g in range(4)])
        gs['pool_scale'][l] = dsc[0]
        gs['w_s'][l] = dws
        gs['b_s'][l] = dbrows[:, ::HEAD_DIM].T
        gs['sgu_v_norm'][l] = dvn[0, :HEAD_DIM]
        gb['w_in'] = _mm(s['h1'], dz, mode='tn', tm=D_MODEL, tn=1152, tk=tkt, name=p + "in_dw")
        toks = sched.slot(l, 'mix', gb['w_in'], gb)
        dx, dxb, dg1 = _mm_nt_sharded_rms(dz, s['w_in'], s['x'], s['g1'], dx1, tm=tm,
                                          name=p + "in_dx_rms1", deps=toks)
        gs['norm1'][l] = dg1[0]
    gs = {n: jnp.stack(v) for n, v in gs.items()}
    return loss_row, dx, gs


GROUP_F = ('w_down', 'w_up')
GROUP_M = ('w_out', 'w_proj_a', 'w_proj_b', 'w_proj_c', 'w_in')
ROW_SHARDED = ('w_out', 'w_down')

REDUCE_PLAN = {
    (1, 'ffn'): (('S1', 'F', 1),),
    (1, 'mid'): (('W1', 'F', 1),),
    (1, 'mix'): (('S1', 'M', 1),),
    (0, 'down'): (('W1', 'M', 1),),
    (0, 'conv'): (('W2', 'F', 1),),
    (0, 'ffn'): (('S1', 'F', 0), ('W3', 'F', 1)),
    (0, 'mid'): (('W1', 'F', 0),),
    (0, 'attn'): (('W2', 'M', 1),),
    (0, 'mix'): (('S1', 'M', 0), ('W3', 'M', 1)),
}
REDUCE_TAIL_A = (('W1', 'M', 0), ('W2', 'F', 0))
REDUCE_TAIL_B = (('W3', 'F', 0),)
REDUCE_TAIL_C = (('W2', 'M', 0), ('W3', 'M', 0))


class _Comm:
    def __init__(self, w, pos):
        self.pos = pos
        groups = {'a': [('w_in', 0)],
                  'b': [(n, 0) for n in ('w_proj_a', 'w_proj_b', 'w_proj_c', 'w_out')],
                  'c': [(n, 0) for n in ('w_up', 'conv_w', 'w_down')],
                  'd': [(n, 1) for n in BIG] + [('conv_w', 1)]}
        self.gathers, self.group_of, self.weights = {}, {}, {}
        self.tokens = []
        for g, ks in groups.items():
            lands = [_cast_place(w[n], pos, F32 if n == 'conv_w' else MXU_DTYPE, layer=l, name=f"gw_place_{n}{l}")
                     for n, l in ks]
            self.gathers[g] = (_Gather(lands, "gw_" + g, deps=self.tokens[-1:], halves=(g == 'a')), ks)
            self.tokens.append(self.gathers[g][0].token)
            self.group_of.update({k: g for k in ks})
        self.red = {}
        self.final = {}

    def start_tokens(self):
        return self.tokens[-1:]

    def weight(self, name, layer, after):
        if (name, layer) not in self.weights:
            gather, ks = self.gathers[self.group_of[(name, layer)]]
            for (n, l), full in zip(ks, gather.wait(after)):
                if n == 'conv_w' or n.startswith('w_proj'):
                    full = full.transpose(1, 0, 2).reshape(full.shape[1], -1)
                elif n in ROW_SHARDED:
                    full = full.reshape(-1, full.shape[2])
                self.weights[(n, l)] = full
        return self.weights[(name, layer)]

    def slot(self, layer, slot, after, grads=None):
        tokens = []
        for step, grp, lyr in REDUCE_PLAN.get((layer, slot), ()):
            tok = self._step(step, grp, lyr, after, grads)
            if tok is not None:
                tokens.append(tok)
        return tokens

    def tail(self, steps, after, deps=()):
        toks = (self._step(step, grp, lyr, after, None, deps) for step, grp, lyr in steps)
        return [t for t in toks if t is not None]

    def shards(self):
        return {n: f.reshape(DEPTH, 2 * f.shape[2], f.shape[3]) for n, f in self.final.items()}

    def _step(self, step, grp, layer, after, grads, deps=()):
        names = GROUP_F if grp == 'F' else GROUP_M
        tag = f"{grp.lower()}{layer}"
        st = self.red.setdefault((grp, layer), {})
        n = len(names)
        if step == 'S1':
            g4s = []
            for nm in names:
                g = grads[nm]
                R, C = g.shape
                g4s.append(g.reshape(N_CHIPS, 2, R // (2 * N_CHIPS), C) if nm in ROW_SHARDED
                           else g.reshape(1, 2, R // 2, C))
            st['s1'] = _swap_halves_start(g4s, name="rs1_" + tag)
            return st['s1'][2]
        if step == 'W1':
            sems, arrays, _, waits = st.pop('s1')
            arrays = _split_wait(sems, arrays, after, waits, name=f"rs1_{tag}_wait")
            parts = [_pair_add(arrays[i], arrays[n + i], self.pos, name=f"pair_add_{tag}_{names[i]}")
                     for i in range(n)]
            st['s2'] = _scatter_start(parts, name="rs2_" + tag, deps=deps)
            return st['s2'][2]
        if step == 'W2':
            sems, arrays, _, waits = st.pop('s2')
            arrays = _split_wait(sems, arrays, after, waits, name=f"rs2_{tag}_wait")
            fs = [_chip_sum(arrays[i], arrays[n + i], self.final.get(names[i]), self.pos, layer,
                            name=f"chip_sum_{tag}_{names[i]}") for i in range(n)]
            st['s3'] = _pair_share_start(fs, layer, name="rs3_" + tag)
            return st['s3'][2]
        sems, arrays, _, waits = st.pop('s3')
        self.final.update(zip(names, _split_wait(sems, arrays, after, waits, name=f"rs3_{tag}_wait")))
        return None


def _pack(arrays):
    rows = []
    for a in arrays:
        nel = int(np.prod(a.shape))
        if nel % 1024 == 0:
            rows.append(a.astype(F32).reshape(nel // 128, 128))
        else:
            f = a.reshape(-1).astype(F32)
            rows.append(jnp.pad(f, (0, (-nel) % 1024)).reshape(-1, 128))
    return jnp.concatenate(rows, axis=0)


def _unpack(pack, shapes):
    out, row = [], 0
    for shp in shapes:
        nel = int(np.prod(shp))
        nrow = 8 * -(-nel // 1024)
        part = pack[row:row + nrow]
        out.append(part.reshape(shp) if nel % 1024 == 0 else part.reshape(-1)[:nel].reshape(shp))
        row += nrow
    return out


def kernel(x, positions, norm1, w_in, q_norm, k_norm, sinks, w_pool, pool_scale, sgu_v_norm, w_s, b_s, w_proj_a, w_proj_b, w_proj_c, w_out, norm2, w_up, conv_w, conv_b, w_down, loss_target, m_norm1, m_w_in, m_q_norm, m_k_norm, m_sinks, m_w_pool, m_pool_scale, m_sgu_v_norm, m_w_s, m_b_s, m_w_proj_a, m_w_proj_b, m_w_proj_c, m_w_out, m_norm2, m_w_up, m_conv_w, m_conv_b, m_w_down, v_norm1, v_w_in, v_q_norm, v_k_norm, v_sinks, v_w_pool, v_pool_scale, v_sgu_v_norm, v_w_s, v_b_s, v_w_proj_a, v_w_proj_b, v_w_proj_c, v_w_out, v_norm2, v_w_up, v_conv_w, v_conv_b, v_w_down):
    w = dict(norm1=norm1, w_in=w_in, q_norm=q_norm, k_norm=k_norm, sinks=sinks, w_pool=w_pool, pool_scale=pool_scale,
             sgu_v_norm=sgu_v_norm, w_s=w_s, b_s=b_s, w_proj_a=w_proj_a, w_proj_b=w_proj_b, w_proj_c=w_proj_c,
             w_out=w_out, norm2=norm2, w_up=w_up, conv_w=conv_w, conv_b=conv_b, w_down=w_down)
    m = dict(norm1=m_norm1, w_in=m_w_in, q_norm=m_q_norm, k_norm=m_k_norm, sinks=m_sinks, w_pool=m_w_pool,
             pool_scale=m_pool_scale, sgu_v_norm=m_sgu_v_norm, w_s=m_w_s, b_s=m_b_s, w_proj_a=m_w_proj_a,
             w_proj_b=m_w_proj_b, w_proj_c=m_w_proj_c, w_out=m_w_out, norm2=m_norm2, w_up=m_w_up, conv_w=m_conv_w,
             conv_b=m_conv_b, w_down=m_w_down)
    v = dict(norm1=v_norm1, w_in=v_w_in, q_norm=v_q_norm, k_norm=v_k_norm, sinks=v_sinks, w_pool=v_w_pool,
             pool_scale=v_pool_scale, sgu_v_norm=v_sgu_v_norm, w_s=v_w_s, b_s=v_b_s, w_proj_a=v_w_proj_a,
             w_proj_b=v_w_proj_b, w_proj_c=v_w_proj_c, w_out=v_w_out, norm2=v_norm2, w_up=v_w_up, conv_w=v_conv_w,
             conv_b=v_conv_b, w_down=v_w_down)
    chip = 2 * lax.axis_index("x") + lax.axis_index("y")
    core = lax.axis_index("c")

    pos = jnp.stack([chip, core, 2 * chip + core]).astype(jnp.int32)
    comm = _Comm(w, pos)

    cos, sin = _rope_tables(positions[0])
    sp = {n: w[n] for n in SMALL if n != 'conv_w'}
    loss_row, dx, gs = _local_step(x[0], loss_target[0], cos, sin, sp, comm)

    delta, new_m, new_v, grad_out = {}, {}, {}, {}

    def adamw_big(names, grads):
        for n in names:
            shp = w[n].shape
            two_d = lambda a: a.reshape(shp[0] * shp[1], shp[2])
            d, nm, nv, g = _adamw(two_d(w[n]), two_d(grads[n]), two_d(m[n]), two_d(v[n]),
                                  tr=_row_tile(shp[0] * shp[1], 256), name=f"adamw_{n}", copy_g=True)
            delta[n], new_m[n], new_v[n], grad_out[n] = d.reshape(shp), nm.reshape(shp), nv.reshape(shp), g.reshape(shp)

    small_shapes = [gs[n].shape for n in SMALL] + [(1,)]
    small_pack = _pack([gs[n] for n in SMALL] + [loss_row[0, :1]])
    small = _Gather([_cast_place(small_pack, pos, F32, slots=N_DEV, which=2, name="small_place")], "small_gather",
                    all_devices=True)
    toks = comm.tail(REDUCE_TAIL_A[:1], (dx, small.token))
    comm.tail(REDUCE_TAIL_A[1:], (dx, *toks))
    comm.tail(REDUCE_TAIL_B, dx)
    adamw_big(GROUP_F, comm.shards())
    red = _sum_slots(small.wait(new_v[GROUP_F[-1]])[0], tr=small_pack.shape[0], name="small_sum")
    *small_grads, loss = _unpack(red, small_shapes)
    g_small = dict(zip(SMALL, small_grads))
    comm.tail(REDUCE_TAIL_C, red)
    grads = comm.shards()
    grads.update(g_small)
    shard_cols = conv_w.shape[2]
    grads['conv_w'] = lax.dynamic_slice_in_dim(g_small['conv_w'], chip * shard_cols, shard_cols, axis=2)

    adamw_big(GROUP_M, grads)
    shapes = [w[n].shape for n in SMALL]
    packs = [_pack([src[n] for n in SMALL]) for src in (w, grads, m, v)]
    d, nm, nv = _adamw(*packs, tr=packs[0].shape[0], name="adamw_small")
    for dst, src in ((delta, d), (new_m, nm), (new_v, nv)):
        dst.update(zip(SMALL, _unpack(src, shapes)))

    grads.update(grad_out)
    return (loss[0], dx[None], *[grads[n] for n in WEIGHTS], *[delta[n] for n in WEIGHTS],
            *[new_m[n] for n in WEIGHTS], *[new_v[n] for n in WEIGHTS])
```
